```python
import jax, jax.numpy as jnp
from jax import lax
import numpy as np

D_MODEL = 1024
BATCH = 8
SEQ = 4096
DEPTH = 2

N_EVEN = (DEPTH + 1) // 2
N_ODD = DEPTH // 2

RET_HEADS = 4
RET_DK = D_MODEL // (2 * RET_HEADS)
RET_DV = D_MODEL // (2 * RET_HEADS)
RET_CHUNK = 128
ROPE_THETA = 10000.0

LRU_WIDTH = D_MODEL // 2
LRU_BLOCKS = 8
LRU_BLOCK = LRU_WIDTH // LRU_BLOCKS
LRU_C = 8.0
CONV_K = 4

GDN_HEADS = 8
GDN_DK = D_MODEL // GDN_HEADS
GDN_DV = D_MODEL // GDN_HEADS
GDN_CHUNK = 64

D_FF = 4 * D_MODEL
EPS = 1e-6

RET_QK = RET_HEADS * RET_DK
RET_V = RET_HEADS * RET_DV
EVEN_SPLITS = (RET_QK, RET_QK, RET_V, RET_V, LRU_WIDTH, LRU_WIDTH)
EVEN_IN = sum(EVEN_SPLITS)
EVEN_MIX = RET_V + LRU_WIDTH
GDN_K = GDN_HEADS * GDN_DK
GDN_V = GDN_HEADS * GDN_DV
GDN_CONV_DIM = 2 * GDN_K + GDN_V
ODD_SPLITS = (GDN_K, GDN_K, GDN_V, GDN_V, GDN_HEADS, GDN_HEADS)
ODD_IN = sum(ODD_SPLITS)

kernel_name = "hybrid_retention_rglru_gdn_trunk"


def _split(p, sizes):
    offs = np.cumsum(sizes)[:-1].tolist()
    return jnp.split(p, offs, axis=-1)


def rms_norm(x, w):
    xf = x.astype(jnp.float32)
    return xf * lax.rsqrt(jnp.mean(xf * xf, axis=-1, keepdims=True) + EPS) * w.astype(jnp.float32)


def head_rms(x):
    return x * lax.rsqrt(jnp.mean(x * x, axis=-1, keepdims=True) + EPS)


def causal_depthwise_conv(x, w, b=None):
    K, C = w.shape
    y = lax.conv_general_dilated(x, w[:, None, :].astype(x.dtype), window_strides=(1,),
                                 padding=[(K - 1, 0)], dimension_numbers=('NWC', 'WIO', 'NWC'),
                                 feature_group_count=C)
    if b is not None:
        y = y + b.astype(x.dtype)
    return y


def rope(x, pos):
    half = x.shape[-1] // 2
    inv = ROPE_THETA ** (-jnp.arange(half, dtype=jnp.float32) / half)
    ang = pos.astype(jnp.float32)[:, None] * inv[None, :]
    cos = jnp.cos(ang)[None, :, None, :]
    sin = jnp.sin(ang)[None, :, None, :]
    x1, x2 = x[..., :half], x[..., half:]
    return jnp.concatenate([x1 * cos - x2 * sin, x1 * sin + x2 * cos], axis=-1)


def retention(q, k, v):
    B, T, H, dk = q.shape
    dv = v.shape[-1]
    C = RET_CHUNK
    N = T // C
    log_gamma = jnp.log1p(-jnp.exp2(-5.0 - jnp.arange(H, dtype=jnp.float32)))
    k = k * (dk ** -0.5)
    q = q.reshape(B, N, C, H, dk)
    k = k.reshape(B, N, C, H, dk)
    v = v.reshape(B, N, C, H, dv)
    idx = jnp.arange(C, dtype=jnp.float32)
    diff = idx[:, None] - idx[None, :]
    causal = diff >= 0
    decay = jnp.where(causal, jnp.exp(log_gamma[:, None, None] * jnp.where(causal, diff, 0.0)), 0.0)
    scores = jnp.einsum('bnihd,bnjhd->bnhij', q, k) * decay
    o_intra = jnp.einsum('bnhij,bnjhe->bnihe', scores, v)
    q_decay = jnp.exp(log_gamma[None, :] * (idx[:, None] + 1.0))
    k_decay = jnp.exp(log_gamma[None, :] * (C - 1.0 - idx[:, None]))
    chunk_gamma = jnp.exp(log_gamma * C)
    kv = jnp.einsum('bnjhd,bnjhe->bnhde', k * k_decay[:, :, None], v)

    def step(S, kv_n):
        return chunk_gamma[None, :, None, None] * S + kv_n, S

    S0 = jnp.zeros((B, H, dk, dv), jnp.float32)
    _, S_prev = lax.scan(step, S0, jnp.moveaxis(kv, 1, 0))
    S_prev = jnp.moveaxis(S_prev, 0, 1)
    o_inter = jnp.einsum('bnihd,bnhde->bnihe', q * q_decay[:, :, None], S_prev)
    return (o_intra + o_inter).reshape(B, T, H, dv)


def rg_lru(x, w_r, b_r, w_i, b_i, lam):
    B, T, W = x.shape
    xb = x.reshape(B, T, LRU_BLOCKS, LRU_BLOCK)
    r = jax.nn.sigmoid(jnp.einsum('btnd,nde->btne', xb, w_r.astype(jnp.float32)).reshape(B, T, W)
                       + b_r.astype(jnp.float32))
    i = jax.nn.sigmoid(jnp.einsum('btnd,nde->btne', xb, w_i.astype(jnp.float32)).reshape(B, T, W)
                       + b_i.astype(jnp.float32))
    log_a = -LRU_C * r * jax.nn.softplus(-lam.astype(jnp.float32))
    a = jnp.exp(log_a)
    mult = jnp.sqrt(-jnp.expm1(2.0 * log_a))
    mult = jnp.where(jnp.arange(T)[None, :, None] == 0, 1.0, mult)
    u = x * i * mult

    def combine(left, right):
        a1, b1 = left
        a2, b2 = right
        return a1 * a2, a2 * b1 + b2

    _, h = lax.associative_scan(combine, (a, u), axis=1)
    return h


def gated_delta_rule(q, k, v, g, beta):
    B, T, H, dk = q.shape
    dv = v.shape[-1]
    C = GDN_CHUNK
    N = T // C
    q = q * (dk ** -0.5)
    to_chunks = lambda t: jnp.moveaxis(t.reshape((B, N, C, H) + t.shape[3:]), 3, 1)
    q, k, v, g, beta = map(to_chunks, (q, k, v, g, beta))
    G = jnp.cumsum(g, axis=-1)
    tri_incl = jnp.tril(jnp.ones((C, C), bool))
    tri_strict = jnp.tril(jnp.ones((C, C), bool), k=-1)
    gdiff = G[..., :, None] - G[..., None, :]
    decay_incl = jnp.where(tri_incl, jnp.exp(jnp.where(tri_incl, gdiff, 0.0)), 0.0)
    decay_strict = jnp.where(tri_strict, decay_incl, 0.0)
    k_beta = k * beta[..., None]
    v_beta = v * beta[..., None]
    L = jnp.einsum('bhnid,bhnjd->bhnij', k_beta, k) * decay_strict
    A = jnp.eye(C, dtype=jnp.float32) + L
    u = lax.linalg.triangular_solve(A, v_beta, left_side=True, lower=True)
    w = lax.linalg.triangular_solve(A, k_beta * jnp.exp(G)[..., None], left_side=True, lower=True)
    qk = jnp.einsum('bhnid,bhnjd->bhnij', q, k) * decay_incl
    q_g = q * jnp.exp(G)[..., None]
    k_g = k * jnp.exp(G[..., -1:] - G)[..., None]
    chunk_decay = jnp.exp(G[..., -1])

    def step(S, xs):
        u_n, w_n, qk_n, qg_n, kg_n, cd_n = xs
        v_new = u_n - jnp.einsum('bhcd,bhde->bhce', w_n, S)
        o = jnp.einsum('bhcd,bhde->bhce', qg_n, S) + jnp.einsum('bhij,bhje->bhie', qk_n, v_new)
        S = S * cd_n[..., None, None] + jnp.einsum('bhcd,bhce->bhde', kg_n, v_new)
        return S, o

    xs = tuple(jnp.moveaxis(t, 2, 0) for t in (u, w, qk, q_g, k_g, chunk_decay))
    S0 = jnp.zeros((B, H, dk, dv), jnp.float32)
    _, o = lax.scan(step, S0, xs)
    return jnp.transpose(o, (1, 0, 3, 2, 4)).reshape(B, T, H, dv)


def retention_rglru_mixer(h, pos, w_in, lru_conv_w, lru_conv_b, lru_w_r, lru_b_r,
                          lru_w_i, lru_b_i, lru_lambda, w_out):
    B, T, _ = h.shape
    p = h @ w_in.astype(jnp.float32)
    q, k, v, g_ret, x_lru, y_lru = _split(p, EVEN_SPLITS)
    q = rope(q.reshape(B, T, RET_HEADS, RET_DK), pos)
    k = rope(k.reshape(B, T, RET_HEADS, RET_DK), pos)
    v = v.reshape(B, T, RET_HEADS, RET_DV)
    o_ret = head_rms(retention(q, k, v)).reshape(B, T, RET_V) * jax.nn.silu(g_ret)
    x_lru = causal_depthwise_conv(x_lru, lru_conv_w, lru_conv_b)
    o_lru = rg_lru(x_lru, lru_w_r, lru_b_r, lru_w_i, lru_b_i, lru_lambda) * jax.nn.gelu(y_lru)
    return jnp.concatenate([o_ret, o_lru], axis=-1) @ w_out.astype(jnp.float32)


def gated_deltanet_mixer(h, w_in, conv_w, a_log, dt_bias, norm_w, w_out):
    B, T, _ = h.shape
    p = h @ w_in.astype(jnp.float32)
    qkv, z, b, a = _split(p, (GDN_CONV_DIM, GDN_V, GDN_HEADS, GDN_HEADS))
    qkv = jax.nn.silu(causal_depthwise_conv(qkv, conv_w))
    q, k, v = _split(qkv, (GDN_K, GDN_K, GDN_V))
    q = q.reshape(B, T, GDN_HEADS, GDN_DK)
    k = k.reshape(B, T, GDN_HEADS, GDN_DK)
    v = v.reshape(B, T, GDN_HEADS, GDN_DV)
    q = q * lax.rsqrt(jnp.sum(q * q, axis=-1, keepdims=True) + EPS)
    k = k * lax.rsqrt(jnp.sum(k * k, axis=-1, keepdims=True) + EPS)
    beta = jax.nn.sigmoid(b)
    g = -jnp.exp(a_log.astype(jnp.float32)) * jax.nn.softplus(a + dt_bias.astype(jnp.float32))
    o = gated_delta_rule(q, k, v, g, beta)
    o = head_rms(o) * norm_w.astype(jnp.float32) * jax.nn.silu(z.reshape(B, T, GDN_HEADS, GDN_DV))
    return o.reshape(B, T, GDN_V) @ w_out.astype(jnp.float32)


def squared_relu_mlp(h, w_up, w_down):
    return jnp.square(jax.nn.relu(h @ w_up.astype(jnp.float32))) @ w_down.astype(jnp.float32)


def _fwd_setup_inputs(seed: int = 0) -> dict:
    key = jax.random.key(seed)
    ks = jax.random.split(key, 24)
    f32 = jnp.float32
    nrm = lambda k, shape, scale: jax.random.normal(k, shape, f32) * scale
    x = nrm(ks[0], (BATCH, SEQ, D_MODEL), 1.0)
    mixer_norm_w = 1.0 + nrm(ks[1], (DEPTH, D_MODEL), 0.02)
    mlp_norm_w = 1.0 + nrm(ks[2], (DEPTH, D_MODEL), 0.02)
    final_norm_w = 1.0 + nrm(ks[3], (D_MODEL,), 0.02)
    w_in_even = nrm(ks[4], (N_EVEN, D_MODEL, EVEN_IN), D_MODEL ** -0.5)
    lru_conv_w = nrm(ks[5], (N_EVEN, CONV_K, LRU_WIDTH), CONV_K ** -0.5)
    lru_conv_b = nrm(ks[6], (N_EVEN, LRU_WIDTH), 0.01)
    lru_w_r = nrm(ks[7], (N_EVEN, LRU_BLOCKS, LRU_BLOCK, LRU_BLOCK), LRU_BLOCK ** -0.5)
    lru_b_r = nrm(ks[8], (N_EVEN, LRU_WIDTH), 0.01)
    lru_w_i = nrm(ks[9], (N_EVEN, LRU_BLOCKS, LRU_BLOCK, LRU_BLOCK), LRU_BLOCK ** -0.5)
    lru_b_i = nrm(ks[10], (N_EVEN, LRU_WIDTH), 0.01)
    a_c = jax.random.uniform(ks[11], (N_EVEN, LRU_WIDTH), f32, 0.9, 0.999)
    s = a_c ** (1.0 / LRU_C)
    lru_lambda = jnp.log(s) - jnp.log1p(-s)
    w_out_even = nrm(ks[12], (N_EVEN, EVEN_MIX, D_MODEL), EVEN_MIX ** -0.5)
    w_in_odd = nrm(ks[13], (N_ODD, D_MODEL, ODD_IN), D_MODEL ** -0.5)
    gdn_conv_w = nrm(ks[14], (N_ODD, CONV_K, GDN_CONV_DIM), CONV_K ** -0.5)
    gdn_a_log = jnp.log(jax.random.uniform(ks[15], (N_ODD, GDN_HEADS), f32, 1.0, 16.0))
    dt = jnp.exp(jax.random.uniform(ks[16], (N_ODD, GDN_HEADS), f32,
                                    float(np.log(1e-3)), float(np.log(1e-1))))
    gdn_dt_bias = dt + jnp.log(-jnp.expm1(-dt))
    gdn_norm_w = 1.0 + nrm(ks[17], (N_ODD, GDN_DV), 0.02)
    w_out_odd = nrm(ks[18], (N_ODD, GDN_V, D_MODEL), GDN_V ** -0.5)
    w_up = nrm(ks[19], (DEPTH, D_MODEL, D_FF), D_MODEL ** -0.5)
    w_down = nrm(ks[20], (DEPTH, D_FF, D_MODEL), D_FF ** -0.5)
    return {"x": x, "mixer_norm_w": mixer_norm_w, "mlp_norm_w": mlp_norm_w,
            "final_norm_w": final_norm_w, "w_in_even": w_in_even, "lru_conv_w": lru_conv_w,
            "lru_conv_b": lru_conv_b, "lru_w_r": lru_w_r, "lru_b_r": lru_b_r,
            "lru_w_i": lru_w_i, "lru_b_i": lru_b_i, "lru_lambda": lru_lambda,
            "w_out_even": w_out_even, "w_in_odd": w_in_odd, "gdn_conv_w": gdn_conv_w,
            "gdn_a_log": gdn_a_log, "gdn_dt_bias": gdn_dt_bias, "gdn_norm_w": gdn_norm_w,
            "w_out_odd": w_out_odd, "w_up": w_up, "w_down": w_down}


def _fwd_reference(x, mixer_norm_w, mlp_norm_w, final_norm_w, w_in_even, lru_conv_w, lru_conv_b,
              lru_w_r, lru_b_r, lru_w_i, lru_b_i, lru_lambda, w_out_even, w_in_odd,
              gdn_conv_w, gdn_a_log, gdn_dt_bias, gdn_norm_w, w_out_odd, w_up, w_down):
    pos = jnp.arange(x.shape[1], dtype=jnp.int32)
    for layer in range(DEPTH):
        j = layer // 2
        h = rms_norm(x, mixer_norm_w[layer])
        if layer % 2 == 0:
            mix = retention_rglru_mixer(h, pos, w_in_even[j], lru_conv_w[j], lru_conv_b[j],
                                        lru_w_r[j], lru_b_r[j], lru_w_i[j], lru_b_i[j],
                                        lru_lambda[j], w_out_even[j])
        else:
            mix = gated_deltanet_mixer(h, w_in_odd[j], gdn_conv_w[j], gdn_a_log[j],
                                       gdn_dt_bias[j], gdn_norm_w[j], w_out_odd[j])
        x = x + mix.astype(x.dtype)
        x = x + squared_relu_mlp(rms_norm(x, mlp_norm_w[layer]), w_up[layer], w_down[layer]).astype(x.dtype)
    return rms_norm(x, final_norm_w).astype(x.dtype)


import jax as _jax
import jax.numpy as _jnp

TWIN_FORMAT = 'train_step'
FWD_PARAMS = ['x', 'mixer_norm_w', 'mlp_norm_w', 'final_norm_w', 'w_in_even', 'lru_conv_w', 'lru_conv_b', 'lru_w_r', 'lru_b_r', 'lru_w_i', 'lru_b_i', 'lru_lambda', 'w_out_even', 'w_in_odd', 'gdn_conv_w', 'gdn_a_log', 'gdn_dt_bias', 'gdn_norm_w', 'w_out_odd', 'w_up', 'w_down']
TWIN_WEIGHTS = ['mixer_norm_w', 'mlp_norm_w', 'final_norm_w', 'w_in_even', 'lru_conv_w', 'lru_conv_b', 'lru_w_r', 'lru_b_r', 'lru_w_i', 'lru_b_i', 'lru_lambda', 'w_out_even', 'w_in_odd', 'gdn_conv_w', 'gdn_a_log', 'gdn_dt_bias', 'gdn_norm_w', 'w_out_odd', 'w_up', 'w_down']
TWIN_DIFF_INPUT = 'x'
TWIN_INPUTS = ['x', 'mixer_norm_w', 'mlp_norm_w', 'final_norm_w', 'w_in_even', 'lru_conv_w', 'lru_conv_b', 'lru_w_r', 'lru_b_r', 'lru_w_i', 'lru_b_i', 'lru_lambda', 'w_out_even', 'w_in_odd', 'gdn_conv_w', 'gdn_a_log', 'gdn_dt_bias', 'gdn_norm_w', 'w_out_odd', 'w_up', 'w_down', 'loss_target', 'm_mixer_norm_w', 'm_mlp_norm_w', 'm_final_norm_w', 'm_w_in_even', 'm_lru_conv_w', 'm_lru_conv_b', 'm_lru_w_r', 'm_lru_b_r', 'm_lru_w_i', 'm_lru_b_i', 'm_lru_lambda', 'm_w_out_even', 'm_w_in_odd', 'm_gdn_conv_w', 'm_gdn_a_log', 'm_gdn_dt_bias', 'm_gdn_norm_w', 'm_w_out_odd', 'm_w_up', 'm_w_down', 'v_mixer_norm_w', 'v_mlp_norm_w', 'v_final_norm_w', 'v_w_in_even', 'v_lru_conv_w', 'v_lru_conv_b', 'v_lru_w_r', 'v_lru_b_r', 'v_lru_w_i', 'v_lru_b_i', 'v_lru_lambda', 'v_w_out_even', 'v_w_in_odd', 'v_gdn_conv_w', 'v_gdn_a_log', 'v_gdn_dt_bias', 'v_gdn_norm_w', 'v_w_out_odd', 'v_w_up', 'v_w_down']
TWIN_OUTPUTS = ['loss', 'grad_x', 'grad_mixer_norm_w', 'grad_mlp_norm_w', 'grad_final_norm_w', 'grad_w_in_even', 'grad_lru_conv_w', 'grad_lru_conv_b', 'grad_lru_w_r', 'grad_lru_b_r', 'grad_lru_w_i', 'grad_lru_b_i', 'grad_lru_lambda', 'grad_w_out_even', 'grad_w_in_odd', 'grad_gdn_conv_w', 'grad_gdn_a_log', 'grad_gdn_dt_bias', 'grad_gdn_norm_w', 'grad_w_out_odd', 'grad_w_up', 'grad_w_down', 'delta_mixer_norm_w', 'delta_mlp_norm_w', 'delta_final_norm_w', 'delta_w_in_even', 'delta_lru_conv_w', 'delta_lru_conv_b', 'delta_lru_w_r', 'delta_lru_b_r', 'delta_lru_w_i', 'delta_lru_b_i', 'delta_lru_lambda', 'delta_w_out_even', 'delta_w_in_odd', 'delta_gdn_conv_w', 'delta_gdn_a_log', 'delta_gdn_dt_bias', 'delta_gdn_norm_w', 'delta_w_out_odd', 'delta_w_up', 'delta_w_down', 'new_m_mixer_norm_w', 'new_m_mlp_norm_w', 'new_m_final_norm_w', 'new_m_w_in_even', 'new_m_lru_conv_w', 'new_m_lru_conv_b', 'new_m_lru_w_r', 'new_m_lru_b_r', 'new_m_lru_w_i', 'new_m_lru_b_i', 'new_m_lru_lambda', 'new_m_w_out_even', 'new_m_w_in_odd', 'new_m_gdn_conv_w', 'new_m_gdn_a_log', 'new_m_gdn_dt_bias', 'new_m_gdn_norm_w', 'new_m_w_out_odd', 'new_m_w_up', 'new_m_w_down', 'new_v_mixer_norm_w', 'new_v_mlp_norm_w', 'new_v_final_norm_w', 'new_v_w_in_even', 'new_v_lru_conv_w', 'new_v_lru_conv_b', 'new_v_lru_w_r', 'new_v_lru_b_r', 'new_v_lru_w_i', 'new_v_lru_b_i', 'new_v_lru_lambda', 'new_v_w_out_even', 'new_v_w_in_odd', 'new_v_gdn_conv_w', 'new_v_gdn_a_log', 'new_v_gdn_dt_bias', 'new_v_gdn_norm_w', 'new_v_w_out_odd', 'new_v_w_up', 'new_v_w_down']
TWIN_LEAF_KINDS = {'loss': 'loss', 'grad_x': 'grad_x', 'grad_mixer_norm_w': 'grad_w', 'grad_mlp_norm_w': 'grad_w', 'grad_final_norm_w': 'grad_w', 'grad_w_in_even': 'grad_w', 'grad_lru_conv_w': 'grad_w', 'grad_lru_conv_b': 'grad_w', 'grad_lru_w_r': 'grad_w', 'grad_lru_b_r': 'grad_w', 'grad_lru_w_i': 'grad_w', 'grad_lru_b_i': 'grad_w', 'grad_lru_lambda': 'grad_w', 'grad_w_out_even': 'grad_w', 'grad_w_in_odd': 'grad_w', 'grad_gdn_conv_w': 'grad_w', 'grad_gdn_a_log': 'grad_w', 'grad_gdn_dt_bias': 'grad_w', 'grad_gdn_norm_w': 'grad_w', 'grad_w_out_odd': 'grad_w', 'grad_w_up': 'grad_w', 'grad_w_down': 'grad_w', 'delta_mixer_norm_w': 'delta_w', 'delta_mlp_norm_w': 'delta_w', 'delta_final_norm_w': 'delta_w', 'delta_w_in_even': 'delta_w', 'delta_lru_conv_w': 'delta_w', 'delta_lru_conv_b': 'delta_w', 'delta_lru_w_r': 'delta_w', 'delta_lru_b_r': 'delta_w', 'delta_lru_w_i': 'delta_w', 'delta_lru_b_i': 'delta_w', 'delta_lru_lambda': 'delta_w', 'delta_w_out_even': 'delta_w', 'delta_w_in_odd': 'delta_w', 'delta_gdn_conv_w': 'delta_w', 'delta_gdn_a_log': 'delta_w', 'delta_gdn_dt_bias': 'delta_w', 'delta_gdn_norm_w': 'delta_w', 'delta_w_out_odd': 'delta_w', 'delta_w_up': 'delta_w', 'delta_w_down': 'delta_w', 'new_m_mixer_norm_w': 'new_m', 'new_m_mlp_norm_w': 'new_m', 'new_m_final_norm_w': 'new_m', 'new_m_w_in_even': 'new_m', 'new_m_lru_conv_w': 'new_m', 'new_m_lru_conv_b': 'new_m', 'new_m_lru_w_r': 'new_m', 'new_m_lru_b_r': 'new_m', 'new_m_lru_w_i': 'new_m', 'new_m_lru_b_i': 'new_m', 'new_m_lru_lambda': 'new_m', 'new_m_w_out_even': 'new_m', 'new_m_w_in_odd': 'new_m', 'new_m_gdn_conv_w': 'new_m', 'new_m_gdn_a_log': 'new_m', 'new_m_gdn_dt_bias': 'new_m', 'new_m_gdn_norm_w': 'new_m', 'new_m_w_out_odd': 'new_m', 'new_m_w_up': 'new_m', 'new_m_w_down': 'new_m', 'new_v_mixer_norm_w': 'new_v', 'new_v_mlp_norm_w': 'new_v', 'new_v_final_norm_w': 'new_v', 'new_v_w_in_even': 'new_v', 'new_v_lru_conv_w': 'new_v', 'new_v_lru_conv_b': 'new_v', 'new_v_lru_w_r': 'new_v', 'new_v_lru_b_r': 'new_v', 'new_v_lru_w_i': 'new_v', 'new_v_lru_b_i': 'new_v', 'new_v_lru_lambda': 'new_v', 'new_v_w_out_even': 'new_v', 'new_v_w_in_odd': 'new_v', 'new_v_gdn_conv_w': 'new_v', 'new_v_gdn_a_log': 'new_v', 'new_v_gdn_dt_bias': 'new_v', 'new_v_gdn_norm_w': 'new_v', 'new_v_w_out_odd': 'new_v', 'new_v_w_up': 'new_v', 'new_v_w_down': 'new_v'}


def _forward(args):
    return _fwd_reference(*[args[k] for k in FWD_PARAMS])


def _output_shape():
    out = _jax.eval_shape(lambda: _forward(_fwd_setup_inputs(0)))
    return out.shape, out.dtype

N_MICROBATCH = 1
ADAM_LR = 0.001
ADAM_B1 = 0.9
ADAM_B2 = 0.999
ADAM_EPS = 1e-08
ADAM_WD = 0.01
ADAM_STEP = 10
PER_EXAMPLE_BATCH_AXIS = {'x': 0, 'loss_target': 0}
SHARED_INPUTS = []
_WEIGHT_DTYPES = {'mixer_norm_w': _jnp.float32, 'mlp_norm_w': _jnp.float32, 'final_norm_w': _jnp.float32, 'w_in_even': _jnp.float32, 'lru_conv_w': _jnp.float32, 'lru_conv_b': _jnp.float32, 'lru_w_r': _jnp.float32, 'lru_b_r': _jnp.float32, 'lru_w_i': _jnp.float32, 'lru_b_i': _jnp.float32, 'lru_lambda': _jnp.float32, 'w_out_even': _jnp.float32, 'w_in_odd': _jnp.float32, 'gdn_conv_w': _jnp.float32, 'gdn_a_log': _jnp.float32, 'gdn_dt_bias': _jnp.float32, 'gdn_norm_w': _jnp.float32, 'w_out_odd': _jnp.float32, 'w_up': _jnp.float32, 'w_down': _jnp.float32}
MOMENT_SCALE = {'mixer_norm_w': 1.531473e-01, 'mlp_norm_w': 1.514426e-01, 'final_norm_w': 3.251703e+01, 'w_in_even': 1.095066e-01, 'lru_conv_w': 8.263590e-02, 'lru_conv_b': 1.025635e+00, 'lru_w_r': 3.705293e-02, 'lru_b_r': 2.996536e-02, 'lru_w_i': 6.657304e-02, 'lru_b_i': 2.739514e-02, 'lru_lambda': 5.914082e-02, 'w_out_even': 1.094261e-01, 'w_in_odd': 5.244903e-02, 'gdn_conv_w': 4.835900e-02, 'gdn_a_log': 6.394648e-01, 'gdn_dt_bias': 5.862628e-01, 'gdn_norm_w': 2.007842e-01, 'w_out_odd': 6.389280e-02, 'w_up': 7.467871e-02, 'w_down': 1.421804e-01}


def _to_microbatches(a, axis):
    t = _jnp.moveaxis(a, axis, 0)
    t = t.reshape((N_MICROBATCH, t.shape[0] // N_MICROBATCH) + t.shape[1:])
    return _jnp.moveaxis(t, 1, axis + 1)


def setup_inputs(seed: int = 0) -> dict:
    inp = _fwd_setup_inputs(seed)
    key = _jax.random.fold_in(_jax.random.key(seed), 7919)
    shape, _ = _output_shape()
    out = dict(inp)
    out["loss_target"] = _jax.random.normal(_jax.random.fold_in(key, 0), shape, _jnp.float32)
    for i, name in enumerate(TWIN_WEIGHTS):
        w = inp[name].astype(_jnp.float32)
        if MOMENT_SCALE is None:
            s = _jnp.sqrt(_jnp.mean(_jnp.square(w)) + 1e-30)
        else:
            s = MOMENT_SCALE[name]
        km, kv = _jax.random.split(_jax.random.fold_in(key, i + 1))
        out[name] = w
        out["m_" + name] = s * _jax.random.normal(km, w.shape, _jnp.float32)
        out["v_" + name] = (s * s) * _jax.random.uniform(kv, w.shape, _jnp.float32, 0.5, 1.5)
    if N_MICROBATCH > 1:
        for name, axis in PER_EXAMPLE_BATCH_AXIS.items():
            out[name] = _to_microbatches(out[name], axis)
    return {'x': out['x'], 'mixer_norm_w': out['mixer_norm_w'], 'mlp_norm_w': out['mlp_norm_w'], 'final_norm_w': out['final_norm_w'], 'w_in_even': out['w_in_even'], 'lru_conv_w': out['lru_conv_w'], 'lru_conv_b': out['lru_conv_b'], 'lru_w_r': out['lru_w_r'], 'lru_b_r': out['lru_b_r'], 'lru_w_i': out['lru_w_i'], 'lru_b_i': out['lru_b_i'], 'lru_lambda': out['lru_lambda'], 'w_out_even': out['w_out_even'], 'w_in_odd': out['w_in_odd'], 'gdn_conv_w': out['gdn_conv_w'], 'gdn_a_log': out['gdn_a_log'], 'gdn_dt_bias': out['gdn_dt_bias'], 'gdn_norm_w': out['gdn_norm_w'], 'w_out_odd': out['w_out_odd'], 'w_up': out['w_up'], 'w_down': out['w_down'], 'loss_target': out['loss_target'], 'm_mixer_norm_w': out['m_mixer_norm_w'], 'm_mlp_norm_w': out['m_mlp_norm_w'], 'm_final_norm_w': out['m_final_norm_w'], 'm_w_in_even': out['m_w_in_even'], 'm_lru_conv_w': out['m_lru_conv_w'], 'm_lru_conv_b': out['m_lru_conv_b'], 'm_lru_w_r': out['m_lru_w_r'], 'm_lru_b_r': out['m_lru_b_r'], 'm_lru_w_i': out['m_lru_w_i'], 'm_lru_b_i': out['m_lru_b_i'], 'm_lru_lambda': out['m_lru_lambda'], 'm_w_out_even': out['m_w_out_even'], 'm_w_in_odd': out['m_w_in_odd'], 'm_gdn_conv_w': out['m_gdn_conv_w'], 'm_gdn_a_log': out['m_gdn_a_log'], 'm_gdn_dt_bias': out['m_gdn_dt_bias'], 'm_gdn_norm_w': out['m_gdn_norm_w'], 'm_w_out_odd': out['m_w_out_odd'], 'm_w_up': out['m_w_up'], 'm_w_down': out['m_w_down'], 'v_mixer_norm_w': out['v_mixer_norm_w'], 'v_mlp_norm_w': out['v_mlp_norm_w'], 'v_final_norm_w': out['v_final_norm_w'], 'v_w_in_even': out['v_w_in_even'], 'v_lru_conv_w': out['v_lru_conv_w'], 'v_lru_conv_b': out['v_lru_conv_b'], 'v_lru_w_r': out['v_lru_w_r'], 'v_lru_b_r': out['v_lru_b_r'], 'v_lru_w_i': out['v_lru_w_i'], 'v_lru_b_i': out['v_lru_b_i'], 'v_lru_lambda': out['v_lru_lambda'], 'v_w_out_even': out['v_w_out_even'], 'v_w_in_odd': out['v_w_in_odd'], 'v_gdn_conv_w': out['v_gdn_conv_w'], 'v_gdn_a_log': out['v_gdn_a_log'], 'v_gdn_dt_bias': out['v_gdn_dt_bias'], 'v_gdn_norm_w': out['v_gdn_norm_w'], 'v_w_out_odd': out['v_w_out_odd'], 'v_w_up': out['v_w_up'], 'v_w_down': out['v_w_down']}


def _loss(weights, diff, rest, loss_target):
    with _jax.named_scope("forward"):
        args = {**rest, TWIN_DIFF_INPUT: diff, **{k: w.astype(_WEIGHT_DTYPES[k]) for k, w in weights.items()}}
        y = _forward(args)
    with _jax.named_scope("loss_head"):
        err = _jnp.square(y.astype(_jnp.float32) - loss_target)
        return 0.5 * _jnp.sum(_jnp.mean(err, axis=-1)) if err.ndim else 0.5 * err


def _adamw(w, g, m, v):
    m = ADAM_B1 * m + (1.0 - ADAM_B1) * g
    v = ADAM_B2 * v + (1.0 - ADAM_B2) * _jnp.square(g)
    m_hat = m / (1.0 - ADAM_B1 ** ADAM_STEP)
    v_hat = v / (1.0 - ADAM_B2 ** ADAM_STEP)
    delta = -ADAM_LR * (m_hat / (_jnp.sqrt(v_hat) + ADAM_EPS) + ADAM_WD * w)
    return delta, m, v


def reference(x, mixer_norm_w, mlp_norm_w, final_norm_w, w_in_even, lru_conv_w, lru_conv_b, lru_w_r, lru_b_r, lru_w_i, lru_b_i, lru_lambda, w_out_even, w_in_odd, gdn_conv_w, gdn_a_log, gdn_dt_bias, gdn_norm_w, w_out_odd, w_up, w_down, loss_target, m_mixer_norm_w, m_mlp_norm_w, m_final_norm_w, m_w_in_even, m_lru_conv_w, m_lru_conv_b, m_lru_w_r, m_lru_b_r, m_lru_w_i, m_lru_b_i, m_lru_lambda, m_w_out_even, m_w_in_odd, m_gdn_conv_w, m_gdn_a_log, m_gdn_dt_bias, m_gdn_norm_w, m_w_out_odd, m_w_up, m_w_down, v_mixer_norm_w, v_mlp_norm_w, v_final_norm_w, v_w_in_even, v_lru_conv_w, v_lru_conv_b, v_lru_w_r, v_lru_b_r, v_lru_w_i, v_lru_b_i, v_lru_lambda, v_w_out_even, v_w_in_odd, v_gdn_conv_w, v_gdn_a_log, v_gdn_dt_bias, v_gdn_norm_w, v_w_out_odd, v_w_up, v_w_down):
    given = dict(x=x, mixer_norm_w=mixer_norm_w, mlp_norm_w=mlp_norm_w, final_norm_w=final_norm_w, w_in_even=w_in_even, lru_conv_w=lru_conv_w, lru_conv_b=lru_conv_b, lru_w_r=lru_w_r, lru_b_r=lru_b_r, lru_w_i=lru_w_i, lru_b_i=lru_b_i, lru_lambda=lru_lambda, w_out_even=w_out_even, w_in_odd=w_in_odd, gdn_conv_w=gdn_conv_w, gdn_a_log=gdn_a_log, gdn_dt_bias=gdn_dt_bias, gdn_norm_w=gdn_norm_w, w_out_odd=w_out_odd, w_up=w_up, w_down=w_down, loss_target=loss_target, m_mixer_norm_w=m_mixer_norm_w, m_mlp_norm_w=m_mlp_norm_w, m_final_norm_w=m_final_norm_w, m_w_in_even=m_w_in_even, m_lru_conv_w=m_lru_conv_w, m_lru_conv_b=m_lru_conv_b, m_lru_w_r=m_lru_w_r, m_lru_b_r=m_lru_b_r, m_lru_w_i=m_lru_w_i, m_lru_b_i=m_lru_b_i, m_lru_lambda=m_lru_lambda, m_w_out_even=m_w_out_even, m_w_in_odd=m_w_in_odd, m_gdn_conv_w=m_gdn_conv_w, m_gdn_a_log=m_gdn_a_log, m_gdn_dt_bias=m_gdn_dt_bias, m_gdn_norm_w=m_gdn_norm_w, m_w_out_odd=m_w_out_odd, m_w_up=m_w_up, m_w_down=m_w_down, v_mixer_norm_w=v_mixer_norm_w, v_mlp_norm_w=v_mlp_norm_w, v_final_norm_w=v_final_norm_w, v_w_in_even=v_w_in_even, v_lru_conv_w=v_lru_conv_w, v_lru_conv_b=v_lru_conv_b, v_lru_w_r=v_lru_w_r, v_lru_b_r=v_lru_b_r, v_lru_w_i=v_lru_w_i, v_lru_b_i=v_lru_b_i, v_lru_lambda=v_lru_lambda, v_w_out_even=v_w_out_even, v_w_in_odd=v_w_in_odd, v_gdn_conv_w=v_gdn_conv_w, v_gdn_a_log=v_gdn_a_log, v_gdn_dt_bias=v_gdn_dt_bias, v_gdn_norm_w=v_gdn_norm_w, v_w_out_odd=v_w_out_odd, v_w_up=v_w_up, v_w_down=v_w_down)
    weights = {n: given[n] for n in TWIN_WEIGHTS}
    shared = {n: given[n] for n in SHARED_INPUTS}
    per_example = {n: given[n] for n in ['x']}
    grad_fn = _jax.value_and_grad(_loss, argnums=(0, 1))

    def one_microbatch(ex, loss_target):
        ex = dict(ex)
        diff = ex.pop(TWIN_DIFF_INPUT)
        return grad_fn(weights, diff, {**shared, **ex}, loss_target)

    if N_MICROBATCH == 1:
        loss, (grad_w, grad_x) = one_microbatch(per_example, given["loss_target"])
    else:
        def body(carry, xs):
            loss_sum, grad_sum = carry
            l_k, (gw_k, gx_k) = one_microbatch(xs[0], xs[1])
            with _jax.named_scope("update"):
                return (loss_sum + l_k, _jax.tree.map(_jnp.add, grad_sum, gw_k)), gx_k

        init = (_jnp.zeros((), _jnp.float32), _jax.tree.map(_jnp.zeros_like, weights))
        (loss, grad_w), grad_x = _jax.lax.scan(body, init, (per_example, given["loss_target"]))
    with _jax.named_scope("update"):
        delta_w, new_m, new_v = {}, {}, {}
        for n in TWIN_WEIGHTS:
            delta_w[n], new_m[n], new_v[n] = _adamw(weights[n], grad_w[n], given["m_" + n], given["v_" + n])
    return (loss, grad_x, *[grad_w[n] for n in TWIN_WEIGHTS], *[delta_w[n] for n in TWIN_WEIGHTS],
            *[new_m[n] for n in TWIN_WEIGHTS], *[new_v[n] for n in TWIN_WEIGHTS])
```

```python
import math

import numpy as np
import jax
import jax.numpy as jnp
from jax import lax
from jax.experimental import pallas as pl
from jax.experimental.pallas import tpu as pltpu

f32 = jnp.float32
bf16 = jnp.bfloat16

N_DEV = 8
D_MODEL = 1024
D_FF = 4096
EPS = 1e-6
RET_HEADS = 4
RET_CHUNK = 128
ROPE_THETA = 10000.0
LRU_WIDTH = 512
LRU_C = 8.0
GDN_HEADS = 8
GDN_CHUNK = 64
HEAD_DIM = 128
ODD_IN = 4112
ODD_PAD = 4224
ADAM_LR, ADAM_B1, ADAM_B2, ADAM_EPS, ADAM_WD, ADAM_STEP = 0.001, 0.9, 0.999, 1e-08, 0.01, 10
VMEM_LIMIT = 56 * 1024 * 1024

_NN = (((1,), (0,)), ((), ()))
_NT = (((1,), (1,)), ((), ()))
_TN = (((0,), (0,)), ((), ()))
MESH = pl.DeviceIdType.MESH


def _cparams(sem):
    return pltpu.CompilerParams(dimension_semantics=sem, vmem_limit_bytes=VMEM_LIMIT)


def _dot(a, b, dn):
    return lax.dot_general(a.astype(bf16), b.astype(bf16), dn, preferred_element_type=f32)


def _split2(a):
    hi = a.astype(bf16)
    lo = (a - hi.astype(f32)).astype(bf16)
    return hi, lo


def _dot3(a, b, dn):
    ah, al = _split2(a)
    bh, bl = _split2(b)
    d = lambda p, q: lax.dot_general(p, q, dn, preferred_element_type=f32)
    return d(ah, bh) + (d(ah, bl) + d(al, bh))


def _dot01(a01, b, dn):
    a = a01.astype(bf16)
    b0 = b.astype(bf16)
    r1 = b - b0.astype(f32)
    b1 = r1.astype(bf16)
    b2 = (r1 - b1.astype(f32)).astype(bf16)
    d = lambda q: lax.dot_general(a, q, dn, preferred_element_type=f32)
    return d(b0) + (d(b1) + d(b2))


def _sigmoid(x):
    return jax.nn.sigmoid(x)


def _silu(x):
    return x * _sigmoid(x)


def _dsilu(x):
    s = _sigmoid(x)
    return s * (1.0 + x * (1.0 - s))


def _softplus(x):
    return jnp.maximum(x, 0.0) + jnp.log1p(jnp.exp(-jnp.abs(x)))


_GELU_C = math.sqrt(2.0 / math.pi)


def _gelu(y):
    return 0.5 * y * (1.0 + jnp.tanh(_GELU_C * (y + 0.044715 * y * y * y)))


def _dgelu(y):
    t = jnp.tanh(_GELU_C * (y + 0.044715 * y * y * y))
    return 0.5 * (1.0 + t) + 0.5 * y * (1.0 - t * t) * _GELU_C * (1.0 + 3.0 * 0.044715 * y * y)


def _matmul(a, b, form, *, outs, name, epilogue=None, extras=(), tm=512, tn=512, tk=512):
    if form == "tn":
        K, M = a.shape
    else:
        M, K = a.shape
    N = b.shape[0] if form == "nt" else b.shape[1]
    tm, tn, tk = min(tm, M), min(tn, N), min(tk, K)
    assert M % tm == 0 and N % tn == 0 and K % tk == 0, (name, M, N, K, tm, tn, tk)
    nk = K // tk
    dn = {"nn": _NN, "nt": _NT, "tn": _TN}[form]
    if form == "tn":
        a_spec = pl.BlockSpec((tk, tm), lambda i, j, k: (k, i))
    else:
        a_spec = pl.BlockSpec((tm, tk), lambda i, j, k: (i, k))
    if form == "nt":
        b_spec = pl.BlockSpec((tn, tk), lambda i, j, k: (j, k))
    else:
        b_spec = pl.BlockSpec((tk, tn), lambda i, j, k: (k, j))
    o_spec = pl.BlockSpec((tm, tn), lambda i, j, k: (i, j))
    n_ex = len(extras)

    def body(*refs):
        a_ref, b_ref = refs[:2]
        ex = refs[2:2 + n_ex]
        o_refs = refs[2 + n_ex:-1]
        acc = refs[-1]
        k = pl.program_id(2)

        @pl.when(k == 0)
        def _():
            acc[...] = jnp.zeros_like(acc)

        acc[...] += _dot(a_ref[...], b_ref[...], dn)

        @pl.when(k == nk - 1)
        def _():
            if epilogue is None:
                vals = (acc[...],)
            else:
                vals = epilogue(acc[...], *[e[...] for e in ex])
            for r, v in zip(o_refs, vals):
                r[...] = v.astype(r.dtype)

    return pl.pallas_call(
        body, grid=(M // tm, N // tn, nk),
        in_specs=[a_spec, b_spec] + [o_spec] * n_ex,
        out_specs=[o_spec] * len(outs),
        out_shape=[jax.ShapeDtypeStruct((M, N), d) for d in outs],
        scratch_shapes=[pltpu.VMEM((tm, tn), f32)],
        compiler_params=_cparams(("parallel", "parallel", "arbitrary")), name=name,
    )(a, b, *extras)


def _rms_fwd(x, w, name):
    T, D = x.shape
    tt = min(512, T)

    def body(x_ref, w_ref, h_ref):
        xv = x_ref[...]
        r = lax.rsqrt(jnp.mean(xv * xv, axis=1, keepdims=True) + EPS)
        h_ref[...] = (xv * r * w_ref[...]).astype(bf16)

    return pl.pallas_call(
        body, grid=(T // tt,),
        in_specs=[pl.BlockSpec((tt, D), lambda i: (i, 0)), pl.BlockSpec((1, D), lambda i: (0, 0))],
        out_specs=pl.BlockSpec((tt, D), lambda i: (i, 0)),
        out_shape=jax.ShapeDtypeStruct((T, D), bf16),
        compiler_params=_cparams(("parallel",)), name=name,
    )(x, w)


def _rms_bwd(x, w, dh, dres, name):
    T, D = x.shape
    tt = min(512, T)

    def body(x_ref, w_ref, dh_ref, dres_ref, dx_ref, dw_ref):
        @pl.when(pl.program_id(0) == 0)
        def _():
            dw_ref[...] = jnp.zeros_like(dw_ref)

        xv = x_ref[...]
        r = lax.rsqrt(jnp.mean(xv * xv, axis=1, keepdims=True) + EPS)
        xn = xv * r
        dhv = dh_ref[...]
        dhw = dhv * w_ref[...]
        dx_ref[...] = dres_ref[...] + r * (dhw - xn * jnp.mean(dhw * xn, axis=1, keepdims=True))
        dw_ref[...] += jnp.sum(dhv * xn, axis=0, keepdims=True)

    row = pl.BlockSpec((tt, D), lambda i: (i, 0))
    vec = pl.BlockSpec((1, D), lambda i: (0, 0))
    return pl.pallas_call(
        body, grid=(T // tt,), in_specs=[row, vec, row, row], out_specs=[row, vec],
        out_shape=[jax.ShapeDtypeStruct((T, D), f32), jax.ShapeDtypeStruct((1, D), f32)],
        compiler_params=_cparams(("arbitrary",)), name=name,
    )(x, w, dh, dres)


def _loss_head(x, w, tgt, name):
    T, D = x.shape
    tt = min(512, T)

    def body(x_ref, w_ref, t_ref, l_ref, dx_ref, dw_ref):
        @pl.when(pl.program_id(0) == 0)
        def _():
            dw_ref[...] = jnp.zeros_like(dw_ref)
            l_ref[...] = jnp.zeros_like(l_ref)

        xv = x_ref[...]
        wv = w_ref[...]
        r = lax.rsqrt(jnp.mean(xv * xv, axis=1, keepdims=True) + EPS)
        xn = xv * r
        e = xn * wv - t_ref[...]
        l_ref[...] += 0.5 * jnp.sum(jnp.mean(e * e, axis=1, keepdims=True), axis=0, keepdims=True)
        dy = e * (1.0 / D)
        dyw = dy * wv
        dx_ref[...] = r * (dyw - xn * jnp.mean(dyw * xn, axis=1, keepdims=True))
        dw_ref[...] += jnp.sum(dy * xn, axis=0, keepdims=True)

    row = pl.BlockSpec((tt, D), lambda i: (i, 0))
    vec = pl.BlockSpec((1, D), lambda i: (0, 0))
    return pl.pallas_call(
        body, grid=(T // tt,), in_specs=[row, vec, row],
        out_specs=[pl.BlockSpec((1, 128), lambda i: (0, 0)), row, vec],
        out_shape=[jax.ShapeDtypeStruct((1, 128), f32), jax.ShapeDtypeStruct((T, D), f32),
                   jax.ShapeDtypeStruct((1, D), f32)],
        compiler_params=_cparams(("arbitrary",)), name=name,
    )(x, w, tgt)


def _ret_tables():
    H, C = RET_HEADS, RET_CHUNK
    lg = np.log1p(-np.exp2(-5.0 - np.arange(H, dtype=np.float32))).astype(np.float32)
    idx = np.arange(C, dtype=np.float32)
    diff = idx[:, None] - idx[None, :]
    causal = diff >= 0
    dm = np.where(causal[None], np.exp(lg[:, None, None] * np.where(causal, diff, 0.0)[None]), 0.0)
    qd = np.exp(lg[:, None] * (idx[None, :] + 1.0))
    kd = np.exp(lg[:, None] * (C - 1.0 - idx[None, :]))
    cg = np.exp(lg * C)
    tab = np.zeros((H, 4, C, HEAD_DIM), np.float32)
    tab[:, 0] = dm
    tab[:, 1] = qd[:, :, None]
    tab[:, 2] = kd[:, :, None]
    tab[:, 3] = cg[:, None, None]
    return jnp.asarray(tab)


def _rope_tables(T):
    half = HEAD_DIM // 2
    inv = ROPE_THETA ** (-jnp.arange(half, dtype=f32) / half)
    ang = jnp.arange(T, dtype=jnp.int32).astype(f32)[:, None] * inv[None, :]
    c, s = jnp.cos(ang), jnp.sin(ang)
    return jnp.concatenate([c, c], axis=1), jnp.concatenate([-s, s], axis=1)


def _rope(x, cos, sin):
    return x * cos + pltpu.roll(x, HEAD_DIM // 2, 1) * sin


def _unrope(y, cos, sin):
    return y * cos + pltpu.roll(y * sin, HEAD_DIM // 2, 1)


def _ret_fwd(p, cos, sin, tab, name):
    T = p.shape[0]
    C, H = RET_CHUNK, RET_HEADS
    N = T // C
    scale = HEAD_DIM ** -0.5

    def body(q_ref, k_ref, v_ref, g_ref, c_ref, s_ref, t_ref, y_ref, o_ref, sp_ref, st):
        n, h = pl.program_id(0), pl.program_id(1)

        @pl.when(n == 0)
        def _():
            st[h] = jnp.zeros((HEAD_DIM, HEAD_DIM), f32)

        cos_, sin_ = c_ref[...], s_ref[...]
        q = _rope(q_ref[...], cos_, sin_)
        k = _rope(k_ref[...], cos_, sin_) * scale
        v = v_ref[...]
        dm, qd, kd, cg = t_ref[0, 0], t_ref[0, 1], t_ref[0, 2], t_ref[0, 3]
        S = st[h]
        P = _dot(q, k, _NT) * dm
        o = _dot(P, v, _NN) + _dot(q * qd, S, _NN)
        sp_ref[0, 0] = S
        st[h] = cg * S + _dot(k * kd, v, _TN)
        o_ref[...] = o
        r = lax.rsqrt(jnp.mean(o * o, axis=1, keepdims=True) + EPS)
        y_ref[...] = (o * r * _silu(g_ref[...])).astype(bf16)

    blk = lambda off: pl.BlockSpec((C, HEAD_DIM), lambda n, h: (n, off + h))
    tbl = pl.BlockSpec((C, HEAD_DIM), lambda n, h: (n, 0))
    return pl.pallas_call(
        body, grid=(N, H),
        in_specs=[blk(0), blk(H), blk(2 * H), blk(3 * H), tbl, tbl,
                  pl.BlockSpec((1, 4, C, HEAD_DIM), lambda n, h: (h, 0, 0, 0))],
        out_specs=[blk(0), blk(0), pl.BlockSpec((1, 1, HEAD_DIM, HEAD_DIM), lambda n, h: (n, h, 0, 0))],
        out_shape=[jax.ShapeDtypeStruct((T, H * HEAD_DIM), bf16), jax.ShapeDtypeStruct((T, H * HEAD_DIM), f32),
                   jax.ShapeDtypeStruct((N, H, HEAD_DIM, HEAD_DIM), f32)],
        scratch_shapes=[pltpu.VMEM((H, HEAD_DIM, HEAD_DIM), f32)],
        compiler_params=_cparams(("arbitrary", "arbitrary")), name=name,
    )(p, p, p, p, cos, sin, tab)


def _ret_bwd(p, cos, sin, tab, o_raw, sprev, dmix, name):
    T = p.shape[0]
    C, H = RET_CHUNK, RET_HEADS
    N = T // C
    scale = HEAD_DIM ** -0.5

    def body(q_ref, k_ref, v_ref, g_ref, c_ref, s_ref, t_ref, o_ref, sp_ref, dy_ref,
             dq_ref, dk_ref, dv_ref, dg_ref, dst):
        n, h = pl.program_id(0), pl.program_id(1)

        @pl.when(n == 0)
        def _():
            dst[h] = jnp.zeros((HEAD_DIM, HEAD_DIM), f32)

        cos_, sin_ = c_ref[...], s_ref[...]
        q = _rope(q_ref[...], cos_, sin_)
        k = _rope(k_ref[...], cos_, sin_) * scale
        v = v_ref[...]
        g = g_ref[...]
        dm, qd, kd, cg = t_ref[0, 0], t_ref[0, 1], t_ref[0, 2], t_ref[0, 3]
        S = sp_ref[0, 0]
        o = o_ref[...]
        dy = dy_ref[...]
        r = lax.rsqrt(jnp.mean(o * o, axis=1, keepdims=True) + EPS)
        nrm = o * r
        dn = dy * _silu(g)
        dg_ref[...] = dy * nrm * _dsilu(g)
        do = r * (dn - nrm * jnp.mean(dn * nrm, axis=1, keepdims=True))
        dS1 = dst[h]
        P = _dot(q, k, _NT) * dm
        dP = _dot(do, v, _NT) * dm
        dq = _dot(dP, k, _NN) + _dot(do, S, _NT) * qd
        dk = _dot(dP, q, _TN) + _dot(v, dS1, _NT) * kd
        dv_ref[...] = _dot(P, do, _TN) + _dot(k * kd, dS1, _NN)
        dst[h] = cg * dS1 + _dot(q * qd, do, _TN)
        dq_ref[...] = _unrope(dq, cos_, sin_)
        dk_ref[...] = _unrope(dk * scale, cos_, sin_)

    rev = lambda off: pl.BlockSpec((C, HEAD_DIM), lambda n, h: (N - 1 - n, off + h))
    tbl = pl.BlockSpec((C, HEAD_DIM), lambda n, h: (N - 1 - n, 0))
    out = jax.ShapeDtypeStruct((T, H * HEAD_DIM), f32)
    return pl.pallas_call(
        body, grid=(N, H),
        in_specs=[rev(0), rev(H), rev(2 * H), rev(3 * H), tbl, tbl,
                  pl.BlockSpec((1, 4, C, HEAD_DIM), lambda n, h: (h, 0, 0, 0)), rev(0),
                  pl.BlockSpec((1, 1, HEAD_DIM, HEAD_DIM), lambda n, h: (N - 1 - n, h, 0, 0)), rev(0)],
        out_specs=[rev(0)] * 4, out_shape=[out] * 4,
        scratch_shapes=[pltpu.VMEM((H, HEAD_DIM, HEAD_DIM), f32)],
        compiler_params=_cparams(("arbitrary", "arbitrary")), name=name,
    )(p, p, p, p, cos, sin, tab, o_raw, sprev, dmix)


CONV_K = 4
PAD = 8


def _conv_fwd(x, col_off, w, b, act, name):
    T = x.shape[0]
    C = w.shape[1]
    G = C // 128
    tt = min(512, T)
    NT = T // tt
    has_b = b is not None

    def body(*refs):
        if has_b:
            x_ref, w_ref, b_ref, y_ref, pad = refs
        else:
            x_ref, w_ref, y_ref, pad = refs
        t = pl.program_id(1)

        @pl.when(t == 0)
        def _():
            pad[pl.ds(0, PAD), :] = jnp.zeros((PAD, 128), f32)

        pad[pl.ds(PAD, tt), :] = x_ref[...]
        wv = w_ref[...]
        y = wv[0:1, :] * pad[pl.ds(PAD - 3, tt), :]
        for kk in range(1, CONV_K):
            y = y + wv[kk:kk + 1, :] * pad[pl.ds(PAD - 3 + kk, tt), :]
        if has_b:
            y = y + b_ref[...]
        tail = pad[pl.ds(tt, PAD), :]
        pad[pl.ds(0, PAD), :] = tail
        y_ref[...] = _silu(y) if act else y

    in_specs = [pl.BlockSpec((tt, 128), lambda g, t: (t, col_off + g)),
                pl.BlockSpec((CONV_K, 128), lambda g, t: (0, g))]
    args = [x, w]
    if has_b:
        in_specs.append(pl.BlockSpec((1, 128), lambda g, t: (0, g)))
        args.append(b)
    return pl.pallas_call(
        body, grid=(G, NT), in_specs=in_specs,
        out_specs=pl.BlockSpec((tt, 128), lambda g, t: (t, g)),
        out_shape=jax.ShapeDtypeStruct((T, C), f32),
        scratch_shapes=[pltpu.VMEM((tt + PAD, 128), f32)],
        compiler_params=_cparams(("parallel", "arbitrary")), name=name,
    )(*args)


def _conv_bwd(x, col_off, w, b, act, dout, name):
    T = x.shape[0]
    C = w.shape[1]
    G = C // 128
    tt = min(512, T)
    NT = T // tt
    has_b = b is not None

    def body(*refs):
        if has_b:
            x_ref, xp_ref, w_ref, b_ref, d_ref, dx_ref, dw_ref, db_ref, pad, dpad = refs
        else:
            x_ref, xp_ref, w_ref, d_ref, dx_ref, dw_ref, db_ref, pad, dpad = refs
        t = pl.program_id(1)
        first_tile = t == NT - 1

        @pl.when(t == 0)
        def _():
            dpad[pl.ds(tt, PAD), :] = jnp.zeros((PAD, 128), f32)
            dw_ref[...] = jnp.zeros_like(dw_ref)
            db_ref[...] = jnp.zeros_like(db_ref)

        pad[pl.ds(0, PAD), :] = jnp.where(first_tile, 0.0, xp_ref[...])
        pad[pl.ds(PAD, tt), :] = x_ref[...]
        wv = w_ref[...]
        dy = d_ref[...]
        if act:
            y = wv[0:1, :] * pad[pl.ds(PAD - 3, tt), :]
            for kk in range(1, CONV_K):
                y = y + wv[kk:kk + 1, :] * pad[pl.ds(PAD - 3 + kk, tt), :]
            if has_b:
                y = y + b_ref[...]
            dy = dy * _dsilu(y)
        dpad[pl.ds(0, tt), :] = dy
        dx = wv[3:4, :] * dy
        for j in range(1, CONV_K):
            dx = dx + wv[3 - j:4 - j, :] * dpad[pl.ds(j, tt), :]
        dx_ref[...] = dx
        head = dpad[pl.ds(0, PAD), :]
        dpad[pl.ds(tt, PAD), :] = head
        for kk in range(CONV_K):
            dw_ref[kk:kk + 1, :] += jnp.sum(dy * pad[pl.ds(PAD - 3 + kk, tt), :], axis=0, keepdims=True)
        db_ref[...] += jnp.sum(dy, axis=0, keepdims=True)

    rows8 = tt // PAD
    in_specs = [pl.BlockSpec((tt, 128), lambda g, t: (NT - 1 - t, col_off + g)),
                pl.BlockSpec((PAD, 128), lambda g, t: (jnp.maximum((NT - 1 - t) * rows8 - 1, 0), col_off + g)),
                pl.BlockSpec((CONV_K, 128), lambda g, t: (0, g))]
    args = [x, x, w]
    if has_b:
        in_specs.append(pl.BlockSpec((1, 128), lambda g, t: (0, g)))
        args.append(b)
    in_specs.append(pl.BlockSpec((tt, 128), lambda g, t: (NT - 1 - t, g)))
    args.append(dout)
    return pl.pallas_call(
        body, grid=(G, NT), in_specs=in_specs,
        out_specs=[pl.BlockSpec((tt, 128), lambda g, t: (NT - 1 - t, g)),
                   pl.BlockSpec((CONV_K, 128), lambda g, t: (0, g)),
                   pl.BlockSpec((1, 128), lambda g, t: (0, g))],
        out_shape=[jax.ShapeDtypeStruct((T, C), f32), jax.ShapeDtypeStruct((CONV_K, C), f32),
                   jax.ShapeDtypeStruct((1, C), f32)],
        scratch_shapes=[pltpu.VMEM((tt + PAD, 128), f32), pltpu.VMEM((tt + PAD, 128), f32)],
        compiler_params=_cparams(("parallel", "arbitrary")), name=name,
    )(*args)


def _lru_gates(xc, wr, wi, br, bi, lam):
    r = _sigmoid(_dot(xc, wr, _NN) + br)
    i = _sigmoid(_dot(xc, wi, _NN) + bi)
    sp = _softplus(-lam)
    a = jnp.exp(-LRU_C * r * sp)
    mult = jnp.sqrt(1.0 - a * a)
    return r, i, sp, a, mult


def _lru_fwd(xc, p, y_off, wr, wi, br, bi, lam, name):
    T = xc.shape[0]
    G = LRU_WIDTH // 128
    tt = min(512, T)
    NT = T // tt

    def body(x_ref, y_ref, wr_ref, wi_ref, br_ref, bi_ref, l_ref, o_ref, h_ref, hc):
        t = pl.program_id(1)

        @pl.when(t == 0)
        def _():
            hc[...] = jnp.zeros_like(hc)

        x = x_ref[...]
        r, i, sp, a, mult = _lru_gates(x, wr_ref[...], wi_ref[...], br_ref[...], bi_ref[...], l_ref[...])
        row = lax.broadcasted_iota(jnp.int32, (tt, 128), 0)
        mult = jnp.where((row == 0) & (t == 0), 1.0, mult)
        U = x * i * mult
        A = a
        d = 1
        while d < tt:
            keep = row >= d
            Ush = jnp.where(keep, pltpu.roll(U, d, 0), 0.0)
            Ash = jnp.where(keep, pltpu.roll(A, d, 0), 1.0)
            U = A * Ush + U
            A = A * Ash
            d *= 2
        h = U + A * hc[0:1, :]
        h_ref[...] = h
        hc[...] = jnp.broadcast_to(h[tt - 1:tt, :], hc.shape)
        o_ref[...] = (h * _gelu(y_ref[...])).astype(bf16)

    tile = pl.BlockSpec((tt, 128), lambda g, t: (t, g))
    vec = pl.BlockSpec((1, 128), lambda g, t: (0, g))
    wsp = pl.BlockSpec((128, 128), lambda g, t: (g, g))
    return pl.pallas_call(
        body, grid=(G, NT),
        in_specs=[tile, pl.BlockSpec((tt, 128), lambda g, t: (t, y_off + g)), wsp, wsp, vec, vec, vec],
        out_specs=[tile, tile],
        out_shape=[jax.ShapeDtypeStruct((T, LRU_WIDTH), bf16), jax.ShapeDtypeStruct((T, LRU_WIDTH), f32)],
        scratch_shapes=[pltpu.VMEM((8, 128), f32)],
        compiler_params=_cparams(("parallel", "arbitrary")), name=name,
    )(xc, p, wr, wi, br, bi, lam)


def _lru_bwd(xc, p, y_off, wr, wi, br, bi, lam, hs, dmix, d_off, name):
    T = xc.shape[0]
    G = LRU_WIDTH // 128
    tt = min(512, T)
    NT = T // tt

    def body(x_ref, y_ref, wr_ref, wi_ref, br_ref, bi_ref, l_ref, h_ref, hp_ref, do_ref,
             dx_ref, dy_ref, dwr_ref, dwi_ref, dbr_ref, dbi_ref, dl_ref, lc, an):
        t = pl.program_id(1)
        first_tile = t == NT - 1

        @pl.when(t == 0)
        def _():
            lc[...] = jnp.zeros_like(lc)
            an[...] = jnp.zeros_like(an)
            dwr_ref[...] = jnp.zeros_like(dwr_ref)
            dwi_ref[...] = jnp.zeros_like(dwi_ref)
            dbr_ref[...] = jnp.zeros_like(dbr_ref)
            dbi_ref[...] = jnp.zeros_like(dbi_ref)
            dl_ref[...] = jnp.zeros_like(dl_ref)

        x = x_ref[...]
        y = y_ref[...]
        wr, wi, lam_ = wr_ref[...], wi_ref[...], l_ref[...]
        r, i, sp, a, mult_raw = _lru_gates(x, wr, wi, br_ref[...], bi_ref[...], lam_)
        row = lax.broadcasted_iota(jnp.int32, (tt, 128), 0)
        t0 = (row == 0) & first_tile
        mult = jnp.where(t0, 1.0, mult_raw)
        h = h_ref[...]
        do = do_ref[...]
        dh = do * _gelu(y)
        dy_ref[...] = do * h * _dgelu(y)
        B = jnp.where(row == tt - 1, an[0:1, :], pltpu.roll(a, tt - 1, 0))
        L = dh
        d = 1
        while d < tt:
            keep = row < tt - d
            Lsh = jnp.where(keep, pltpu.roll(L, tt - d, 0), 0.0)
            Bsh = jnp.where(keep, pltpu.roll(B, tt - d, 0), 1.0)
            L = L + B * Lsh
            B = B * Bsh
            d *= 2
        L = L + B * lc[0:1, :]
        lc[...] = jnp.broadcast_to(L[0:1, :], lc.shape)
        an[...] = jnp.broadcast_to(a[0:1, :], an.shape)
        hprev = jnp.where(first_tile, 0.0, hp_ref[...])[PAD - 1:PAD, :]
        hm1 = jnp.where(row == 0, hprev, pltpu.roll(h, 1, 0))
        da = L * hm1
        dxc = L * i * mult
        di = L * x * mult
        dmult = jnp.where(t0, 0.0, L * x * i)
        da = da - jnp.where(t0, 0.0, dmult * a / mult_raw)
        dlog_a = da * a
        dr = dlog_a * (-LRU_C) * sp
        dsp = jnp.sum(dlog_a * (-LRU_C) * r, axis=0, keepdims=True)
        dpr = dr * r * (1.0 - r)
        dpi = di * i * (1.0 - i)
        dx_ref[...] = dxc + _dot(dpr, wr, _NT) + _dot(dpi, wi, _NT)
        dwr_ref[0] += _dot(x, dpr, _TN)
        dwi_ref[0] += _dot(x, dpi, _TN)
        dbr_ref[...] += jnp.sum(dpr, axis=0, keepdims=True)
        dbi_ref[...] += jnp.sum(dpi, axis=0, keepdims=True)
        dl_ref[...] += dsp * (-_sigmoid(-lam_))

    rows8 = tt // PAD
    tile = pl.BlockSpec((tt, 128), lambda g, t: (NT - 1 - t, g))
    vec = pl.BlockSpec((1, 128), lambda g, t: (0, g))
    wsp = pl.BlockSpec((128, 128), lambda g, t: (g, g))
    wout = pl.BlockSpec((1, 128, 128), lambda g, t: (g, 0, 0))
    return pl.pallas_call(
        body, grid=(G, NT),
        in_specs=[tile, pl.BlockSpec((tt, 128), lambda g, t: (NT - 1 - t, y_off + g)), wsp, wsp, vec, vec, vec, tile,
                  pl.BlockSpec((PAD, 128), lambda g, t: (jnp.maximum((NT - 1 - t) * rows8 - 1, 0), g)),
                  pl.BlockSpec((tt, 128), lambda g, t: (NT - 1 - t, d_off + g))],
        out_specs=[tile, tile, wout, wout, vec, vec, vec],
        out_shape=[jax.ShapeDtypeStruct((T, LRU_WIDTH), f32), jax.ShapeDtypeStruct((T, LRU_WIDTH), f32),
                   jax.ShapeDtypeStruct((G, 128, 128), f32), jax.ShapeDtypeStruct((G, 128, 128), f32),
                   jax.ShapeDtypeStruct((1, LRU_WIDTH), f32), jax.ShapeDtypeStruct((1, LRU_WIDTH), f32),
                   jax.ShapeDtypeStruct((1, LRU_WIDTH), f32)],
        scratch_shapes=[pltpu.VMEM((8, 128), f32), pltpu.VMEM((8, 128), f32)],
        compiler_params=_cparams(("parallel", "arbitrary")), name=name,
    )(xc, p, wr, wi, br, bi, lam, hs, hs, dmix)


def _gdn_chunk(qr, kr, v, ba, alog, dtb, h, S):
    C = GDN_CHUNK
    lane = lax.broadcasted_iota(jnp.int32, (C, 128), 1)
    ri = lax.broadcasted_iota(jnp.int32, (C, C), 0)
    ci = lax.broadcasted_iota(jnp.int32, (C, C), 1)
    col = lambda m, j: jnp.sum(jnp.where(lane == j, m, 0.0), axis=1, keepdims=True)
    beta = col(_sigmoid(ba), h)
    ea = jnp.exp(alog)
    g = col(-ea * _softplus(ba + dtb), GDN_HEADS + h)
    rq = lax.rsqrt(jnp.sum(qr * qr, axis=1, keepdims=True) + EPS)
    rk = lax.rsqrt(jnp.sum(kr * kr, axis=1, keepdims=True) + EPS)
    qh, kn = qr * rq, kr * rk
    qn = qh * (HEAD_DIM ** -0.5)
    tri = (ri >= ci).astype(f32)
    Gb = _dot01(tri, jnp.broadcast_to(g, (C, 128)), _NN)
    Gc = col(Gb, 0)
    Grow = _dot01(jnp.ones((C, 128), f32), jnp.where(lane == 0, Gb, 0.0), _NT)
    incl = ri >= ci
    Di = jnp.where(incl, jnp.exp(jnp.where(incl, Gc - Grow, 0.0)), 0.0)
    Ds = jnp.where(ri > ci, Di, 0.0)
    rowc = lax.broadcasted_iota(jnp.int32, (C, 1), 0)
    Gl = jnp.sum(jnp.where(rowc == C - 1, Gc, 0.0), axis=0, keepdims=True)
    eG = jnp.exp(Gc)
    eGl = jnp.exp(Gl - Gc)
    cd = jnp.exp(Gl)
    kb = kn * beta
    vb = v * beta
    Lm = _dot(kb, kn, _NT) * Ds
    eye = (ri == ci).astype(f32)
    X = -Lm
    Tinv = eye + X
    Pw = X
    for _ in range(5):
        Pw = _dot3(Pw, Pw, _NN)
        Tinv = Tinv + _dot3(Tinv, Pw, _NN)
    kbg = kb * eG
    u = _dot3(Tinv, vb, _NN)
    w = _dot3(Tinv, kbg, _NN)
    QKr = _dot(qn, kn, _NT)
    QK = QKr * Di
    qg = qn * eG
    kg = kn * eGl
    vn = u - _dot(w, S, _NN)
    o = _dot(qg, S, _NN) + _dot(QK, vn, _NN)
    S1 = S * cd + _dot(kg, vn, _TN)
    return dict(beta=beta, g=g, rq=rq, rk=rk, qh=qh, kn=kn, qn=qn, Gc=Gc, Di=Di, Ds=Ds, eG=eG, eGl=eGl, cd=cd,
                kb=kb, vb=vb, Lm=Lm, Tinv=Tinv, kbg=kbg, u=u, w=w, QK=QK, qg=qg, kg=kg, vn=vn, o=o, S1=S1,
                lane=lane, ri=ri, ci=ci, rowc=rowc, ea=ea)


def _gdn_specs(N, rev):
    C = GDN_CHUNK
    H = GDN_HEADS
    nn = (lambda n: N - 1 - n) if rev else (lambda n: n)
    blk = lambda off: pl.BlockSpec((C, HEAD_DIM), lambda n, h: (nn(n), off + h))
    one = lambda off: pl.BlockSpec((C, HEAD_DIM), lambda n, h: (nn(n), off))
    vec = pl.BlockSpec((1, 128), lambda n, h: (0, 0))
    st = pl.BlockSpec((1, 1, HEAD_DIM, HEAD_DIM), lambda n, h: (nn(n), h, 0, 0))
    return blk, one, vec, st


def _gdn_fwd(qkv, p, alog, dtb, nw, name):
    T = qkv.shape[0]
    C, H = GDN_CHUNK, GDN_HEADS
    N = T // C
    blk, one, vec, stspec = _gdn_specs(N, False)

    def body(q_ref, k_ref, v_ref, z_ref, ba_ref, al_ref, dt_ref, nw_ref, y_ref, sp_ref, st):
        n, h = pl.program_id(0), pl.program_id(1)

        @pl.when(n == 0)
        def _():
            st[h] = jnp.zeros((HEAD_DIM, HEAD_DIM), f32)

        S = st[h]
        f = _gdn_chunk(q_ref[...], k_ref[...], v_ref[...], ba_ref[...], al_ref[...], dt_ref[...], h, S)
        sp_ref[0, 0] = S
        st[h] = f["S1"]
        o = f["o"]
        r = lax.rsqrt(jnp.mean(o * o, axis=1, keepdims=True) + EPS)
        y_ref[...] = (o * r * nw_ref[...] * _silu(z_ref[...])).astype(bf16)

    return pl.pallas_call(
        body, grid=(N, H),
        in_specs=[blk(0), blk(H), blk(2 * H), blk(3 * H), one(4 * H), vec, vec, vec],
        out_specs=[blk(0), stspec],
        out_shape=[jax.ShapeDtypeStruct((T, H * HEAD_DIM), bf16), jax.ShapeDtypeStruct((N, H, HEAD_DIM, HEAD_DIM), f32)],
        scratch_shapes=[pltpu.VMEM((H, HEAD_DIM, HEAD_DIM), f32)],
        compiler_params=_cparams(("arbitrary", "arbitrary")), name=name,
    )(qkv, qkv, qkv, p, p, alog, dtb, nw)


def _gdn_bwd(qkv, p, alog, dtb, nw, sprev, dy_all, name):
    T = qkv.shape[0]
    C, H = GDN_CHUNK, GDN_HEADS
    N = T // C
    blk, one, vec, stspec = _gdn_specs(N, True)
    rs = lambda m: jnp.sum(m, axis=1, keepdims=True)

    def body(q_ref, k_ref, v_ref, z_ref, ba_ref, al_ref, dt_ref, nw_ref, sp_ref, dy_ref,
             dq_ref, dk_ref, dv_ref, dz_ref, dba_ref, dal_ref, ddt_ref, dnw_ref, dst):
        n, h = pl.program_id(0), pl.program_id(1)

        @pl.when(n == 0)
        def _():
            dst[h] = jnp.zeros((HEAD_DIM, HEAD_DIM), f32)

        @pl.when((n == 0) & (h == 0))
        def _():
            dal_ref[...] = jnp.zeros_like(dal_ref)
            ddt_ref[...] = jnp.zeros_like(ddt_ref)
            dnw_ref[...] = jnp.zeros_like(dnw_ref)

        @pl.when(h == 0)
        def _():
            dba_ref[...] = jnp.zeros_like(dba_ref)

        qr, kr, v, ba = q_ref[...], k_ref[...], v_ref[...], ba_ref[...]
        alog, dtb_ = al_ref[...], dt_ref[...]
        S = sp_ref[0, 0]
        f = _gdn_chunk(qr, kr, v, ba, alog, dtb_, h, S)
        beta, kn, qn, kb, vb, Tinv, kbg = f["beta"], f["kn"], f["qn"], f["kb"], f["vb"], f["Tinv"], f["kbg"]
        eG, eGl, cd, Di, Ds, QK, vn, w_, qg, kg = (f["eG"], f["eGl"], f["cd"], f["Di"], f["Ds"], f["QK"], f["vn"],
                                                    f["w"], f["qg"], f["kg"])
        lane, ri, ci, rowc = f["lane"], f["ri"], f["ci"], f["rowc"]
        o = f["o"]
        z = z_ref[...]
        nwv = nw_ref[...]
        dy = dy_ref[...]
        r = lax.rsqrt(jnp.mean(o * o, axis=1, keepdims=True) + EPS)
        nrm = o * r
        sz = _silu(z)
        dn = dy * nwv * sz
        dz_ref[...] = dy * nrm * nwv * _dsilu(z)
        dnw_ref[...] += jnp.sum(dy * nrm * sz, axis=0, keepdims=True)
        do = r * (dn - nrm * jnp.mean(dn * nrm, axis=1, keepdims=True))
        dS1 = dst[h]
        dcd = jnp.sum(jnp.sum(S * dS1, axis=1, keepdims=True), axis=0, keepdims=True)
        dkg = _dot(vn, dS1, _NT)
        dvn = _dot(kg, dS1, _NN) + _dot(QK, do, _TN)
        dqg = _dot(do, S, _NT)
        dQK = _dot(do, vn, _NT)
        dw = -_dot(dvn, S, _NT)
        dst[h] = cd * dS1 + _dot(qg, do, _TN) - _dot(w_, dvn, _TN)
        dqn = dqg * eG
        deG = rs(dqg * qn)
        dkn = dkg * eGl
        deGl = rs(dkg * kn)
        dQKr = dQK * Di
        E = dQK * QK
        dqn = dqn + _dot(dQKr, kn, _NN)
        dkn = dkn + _dot(dQKr, qn, _TN)
        dT = _dot3(dvn, vb, _NT) + _dot3(dw, kbg, _NT)
        dvb = _dot3(Tinv, dvn, _TN)
        dkbg = _dot3(Tinv, dw, _TN)
        dkb = dkbg * eG
        deG = deG + rs(dkbg * kb)
        dL = -_dot3(_dot3(Tinv, dT, _TN), Tinv, _NT)
        dKK = dL * Ds
        E = E + dL * f["Lm"]
        dkb = dkb + _dot(dKK, kn, _NN)
        dkn = dkn + _dot(dKK, kb, _TN) + dkb * beta
        dbeta = rs(dkb * kn) + rs(dvb * v)
        dv_ref[...] = dvb * beta
        dG = rs(E) - rs(E.T) + deG * eG - deGl * eGl
        dGl = jnp.sum(deGl * eGl, axis=0, keepdims=True) + dcd * cd
        dG = dG + jnp.where(rowc == C - 1, dGl, 0.0)
        triu = (ri <= ci).astype(f32)
        dgb = _dot01(triu, jnp.broadcast_to(dG, (C, 128)), _NN)
        qh = f["qh"]
        dq_ref[...] = (HEAD_DIM ** -0.5) * f["rq"] * (dqn - qh * rs(dqn * qh))
        dk_ref[...] = f["rk"] * (dkn - kn * rs(dkn * kn))
        db = dbeta * beta * (1.0 - beta)
        da = dgb * (-f["ea"]) * _sigmoid(ba + dtb_)
        hot_b = lane == h
        hot_a = lane == GDN_HEADS + h
        dba_ref[...] += jnp.where(hot_b, db, 0.0) + jnp.where(hot_a, da, 0.0)
        ddt_ref[...] += jnp.sum(jnp.where(hot_a, da, 0.0), axis=0, keepdims=True)
        dal_ref[...] += jnp.sum(jnp.where(hot_a, dgb * f["g"], 0.0), axis=0, keepdims=True)

    big = jax.ShapeDtypeStruct((T, H * HEAD_DIM), f32)
    small = jax.ShapeDtypeStruct((1, 128), f32)
    return pl.pallas_call(
        body, grid=(N, H),
        in_specs=[blk(0), blk(H), blk(2 * H), blk(3 * H), one(4 * H), vec, vec, vec, stspec, blk(0)],
        out_specs=[blk(0), blk(0), blk(0), blk(0), one(0), vec, vec, vec],
        out_shape=[big, big, big, big, jax.ShapeDtypeStruct((T, 128), f32), small, small, small],
        scratch_shapes=[pltpu.VMEM((H, HEAD_DIM, HEAD_DIM), f32)],
        compiler_params=_cparams(("arbitrary", "arbitrary")), name=name,
    )(qkv, qkv, qkv, p, p, alog, dtb, nw, sprev, dy_all)


def _adamw(w, gs, m, v, name):
    R, Cc = w.shape
    S = gs.shape[0]
    tr = R
    for cand in (256, 128, 64, 32, 16, 8):
        if R % cand == 0 and R > cand:
            tr = cand
            break
    c1 = 1.0 - ADAM_B1 ** ADAM_STEP
    c2 = 1.0 - ADAM_B2 ** ADAM_STEP

    def body(w_ref, g_ref, m_ref, v_ref, go_ref, d_ref, mo_ref, vo_ref):
        g = g_ref[0]
        for s in range(1, S):
            g = g + g_ref[s]
        mn = ADAM_B1 * m_ref[...] + (1.0 - ADAM_B1) * g
        vn = ADAM_B2 * v_ref[...] + (1.0 - ADAM_B2) * (g * g)
        go_ref[...] = g
        mo_ref[...] = mn
        vo_ref[...] = vn
        d_ref[...] = -ADAM_LR * ((mn / c1) / (jnp.sqrt(vn / c2) + ADAM_EPS) + ADAM_WD * w_ref[...])

    blk = pl.BlockSpec((tr, Cc), lambda i: (i, 0))
    out = jax.ShapeDtypeStruct((R, Cc), f32)
    return pl.pallas_call(
        body, grid=(R // tr,),
        in_specs=[blk, pl.BlockSpec((S, tr, Cc), lambda i: (0, i, 0)), blk, blk],
        out_specs=[blk] * 4, out_shape=[out] * 4,
        compiler_params=_cparams(("parallel",)), name=name,
    )(w, gs, m, v)


def _me():
    x, y, c = lax.axis_index("x"), lax.axis_index("y"), lax.axis_index("c")
    return x, y, c, 4 * x + 2 * y + c


def _peer(k):
    x, y, c, _ = _me()
    px = 1 - x if k & 4 else x
    py = 1 - y if k & 2 else y
    pc = 1 - c if k & 1 else c
    return (px, py, pc), 4 * px + 2 * py + pc


def _all_gather(arrs, name):
    n = len(arrs)

    def body(*refs):
        ins, outs = refs[:n], refs[n:2 * n]
        ssem, rsem, lsem = refs[2 * n:]
        me = _me()[3]
        for i in range(n):
            pltpu.make_async_copy(ins[i], outs[i].at[me], lsem.at[i]).start()
            for k in range(1, N_DEV):
                pid, _ = _peer(k)
                pltpu.make_async_remote_copy(src_ref=ins[i], dst_ref=outs[i].at[me], send_sem=ssem.at[i, k - 1],
                                             recv_sem=rsem.at[i, k - 1], device_id=pid, device_id_type=MESH).start()
        for i in range(n):
            for k in range(1, N_DEV):
                pid, pidx = _peer(k)
                cp = pltpu.make_async_remote_copy(src_ref=ins[i], dst_ref=outs[i].at[pidx], send_sem=ssem.at[i, k - 1],
                                                  recv_sem=rsem.at[i, k - 1], device_id=pid, device_id_type=MESH)
                cp.wait_send()
                cp.wait_recv()
            pltpu.make_async_copy(ins[i], outs[i].at[me], lsem.at[i]).wait()

    anyspec = pl.BlockSpec(memory_space=pl.ANY)
    return pl.pallas_call(
        body, in_specs=[anyspec] * n, out_specs=[anyspec] * n,
        out_shape=[jax.ShapeDtypeStruct((N_DEV,) + a.shape, a.dtype) for a in arrs],
        scratch_shapes=[pltpu.SemaphoreType.DMA((n, N_DEV - 1)), pltpu.SemaphoreType.DMA((n, N_DEV - 1)),
                        pltpu.SemaphoreType.DMA((n,))],
        name=name,
    )(*arrs)


def _exchange(sends, name):
    n = len(sends)

    def body(*refs):
        ins, outs = refs[:n], refs[n:2 * n]
        ssem, rsem, lsem = refs[2 * n:]
        me = _me()[3]
        for i in range(n):
            pltpu.make_async_copy(ins[i].at[me], outs[i].at[me], lsem.at[i]).start()
            for k in range(1, N_DEV):
                pid, pidx = _peer(k)
                pltpu.make_async_remote_copy(src_ref=ins[i].at[pidx], dst_ref=outs[i].at[me], send_sem=ssem.at[i, k - 1],
                                             recv_sem=rsem.at[i, k - 1], device_id=pid, device_id_type=MESH).start()
        for i in range(n):
            for k in range(1, N_DEV):
                pid, pidx = _peer(k)
                cp = pltpu.make_async_remote_copy(src_ref=ins[i].at[pidx], dst_ref=outs[i].at[pidx],
                                                  send_sem=ssem.at[i, k - 1], recv_sem=rsem.at[i, k - 1],
                                                  device_id=pid, device_id_type=MESH)
                cp.wait_send()
                cp.wait_recv()
            pltpu.make_async_copy(ins[i].at[me], outs[i].at[me], lsem.at[i]).wait()

    anyspec = pl.BlockSpec(memory_space=pl.ANY)
    return pl.pallas_call(
        body, in_specs=[anyspec] * n, out_specs=[anyspec] * n,
        out_shape=[jax.ShapeDtypeStruct(a.shape, a.dtype) for a in sends],
        scratch_shapes=[pltpu.SemaphoreType.DMA((n, N_DEV - 1)), pltpu.SemaphoreType.DMA((n, N_DEV - 1)),
                        pltpu.SemaphoreType.DMA((n,))],
        name=name,
    )(*sends)


def _block_diag(w):
    nb, bs = w.shape[0], w.shape[1]
    eye = jnp.eye(nb, dtype=w.dtype)
    return (eye[:, None, :, None] * w[:, :, None, :]).reshape(nb * bs, nb * bs)


def _diag_blocks(d):
    return jnp.stack([d[g, s * 64:(s + 1) * 64, s * 64:(s + 1) * 64] for g in range(4) for s in range(2)])


def _mlp_fwd(x, nw, wu, wd, tag):
    hm = _rms_fwd(x, nw, f"rms_mlp_{tag}")
    relu_ep = lambda acc: (jnp.maximum(acc, 0.0), jnp.square(jnp.maximum(acc, 0.0)))
    r, act = _matmul(hm, wu, "nn", outs=[bf16, bf16], epilogue=relu_ep, name=f"mlp_up_{tag}")
    (xo,) = _matmul(act, wd, "nn", outs=[f32], extras=(x,), epilogue=lambda acc, res: (res + acc,), name=f"mlp_down_{tag}")
    return xo, (hm, r, act)


def _mlp_bwd(x, nw, wu, wd, saved, dxo, tag):
    hm, r, act = saved
    (du,) = _matmul(dxo, wd, "nt", outs=[bf16], extras=(r,), epilogue=lambda acc, rr: (acc * (2.0 * rr.astype(f32)),),
                    name=f"mlp_dact_{tag}")
    (dwd,) = _matmul(act, dxo, "tn", outs=[f32], name=f"mlp_dwd_{tag}")
    (dwu,) = _matmul(hm, du, "tn", outs=[f32], name=f"mlp_dwu_{tag}")
    (dhm,) = _matmul(du, wu, "nt", outs=[f32], name=f"mlp_dh_{tag}")
    dx, dnw = _rms_bwd(x, nw, dhm, dxo, f"rms_mlp_bwd_{tag}")
    return dx, dnw, dwu, dwd


def _local_step(x, tgt, W):
    T = x.shape[0]
    cos, sin = _rope_tables(T)
    rtab = _ret_tables()
    row = lambda a: a.reshape(1, -1)
    mix_nw, mlp_nw = W["mixer_norm_w"], W["mlp_norm_w"]
    wr_bd, wi_bd = _block_diag(W["lru_w_r"]), _block_diag(W["lru_w_i"])
    lru_b, lru_br, lru_bi, lru_lam = row(W["lru_conv_b"]), row(W["lru_b_r"]), row(W["lru_b_i"]), row(W["lru_lambda"])
    pad16 = lambda a: jnp.pad(a.reshape(1, GDN_HEADS), ((0, 0), (GDN_HEADS, 128 - 2 * GDN_HEADS)))
    alog, dtb = pad16(W["gdn_a_log"]), pad16(W["gdn_dt_bias"])
    gnw = row(W["gdn_norm_w"])

    x0 = x
    h0 = _rms_fwd(x0, mix_nw[0:1], "rms_mix_0")
    (pe,) = _matmul(h0, W["w_in_even"], "nn", outs=[f32], name="in_even")
    y_ret, o_ret, s_ret = _ret_fwd(pe, cos, sin, rtab, "ret_fwd")
    xc = _conv_fwd(pe, 16, W["lru_conv_w"], lru_b, False, "lru_conv_fwd")
    y_lru, h_lru = _lru_fwd(xc, pe, 20, wr_bd, wi_bd, lru_br, lru_bi, lru_lam, "lru_fwd")
    mix0 = jnp.concatenate([y_ret, y_lru], axis=1)
    (x1,) = _matmul(mix0, W["w_out_even"], "nn", outs=[f32], extras=(x0,), epilogue=lambda acc, res: (res + acc,), name="out_even")
    x2, mlp0 = _mlp_fwd(x1, mlp_nw[0:1], W["w_up0"], W["w_down0"], "0")
    h1 = _rms_fwd(x2, mix_nw[1:2], "rms_mix_1")
    (po,) = _matmul(h1, W["w_in_odd"], "nn", outs=[f32], tn=384, name="in_odd")
    qkv = _conv_fwd(po, 0, W["gdn_conv_w"], None, True, "gdn_conv_fwd")
    y_gdn, s_gdn = _gdn_fwd(qkv, po, alog, dtb, gnw, "gdn_fwd")
    (x3,) = _matmul(y_gdn, W["w_out_odd"], "nn", outs=[f32], extras=(x2,), epilogue=lambda acc, res: (res + acc,), name="out_odd")
    x4, mlp1 = _mlp_fwd(x3, mlp_nw[1:2], W["w_up1"], W["w_down1"], "1")
    loss, dx4, d_final = _loss_head(x4, row(W["final_norm_w"]), tgt, "loss_head")
    dx3, d_mlp_nw1, d_wu1, d_wd1 = _mlp_bwd(x3, mlp_nw[1:2], W["w_up1"], W["w_down1"], mlp1, dx4, "1")
    (dy_gdn,) = _matmul(dx3, W["w_out_odd"], "nt", outs=[f32], name="out_odd_dx")
    (d_woo,) = _matmul(y_gdn, dx3, "tn", outs=[f32], name="out_odd_dw")
    dq, dk, dv, dz, dba, d_alog, d_dtb, d_gnw = _gdn_bwd(qkv, po, alog, dtb, gnw, s_gdn, dy_gdn, "gdn_bwd")
    dqkv = jnp.concatenate([dq, dk, dv], axis=1)
    dqkv_pre, d_gconv, _ = _conv_bwd(po, 0, W["gdn_conv_w"], None, True, dqkv, "gdn_conv_bwd")
    dpo = jnp.concatenate([dqkv_pre, dz, dba], axis=1).astype(bf16)
    (d_wio,) = _matmul(h1, dpo, "tn", outs=[f32], tn=384, name="in_odd_dw")
    (dh1,) = _matmul(dpo, W["w_in_odd"], "nt", outs=[f32], tk=384, name="in_odd_dx")
    dx2, d_mix_nw1 = _rms_bwd(x2, mix_nw[1:2], dh1, dx3, "rms_mix_bwd_1")
    dx1, d_mlp_nw0, d_wu0, d_wd0 = _mlp_bwd(x1, mlp_nw[0:1], W["w_up0"], W["w_down0"], mlp0, dx2, "0")
    (dmix0,) = _matmul(dx1, W["w_out_even"], "nt", outs=[f32], name="out_even_dx")
    (d_woe,) = _matmul(mix0, dx1, "tn", outs=[f32], name="out_even_dw")
    dq_r, dk_r, dv_r, dg_r = _ret_bwd(pe, cos, sin, rtab, o_ret, s_ret, dmix0, "ret_bwd")
    dxc, dy_l, d_wr, d_wi, d_br, d_bi, d_lam = _lru_bwd(xc, pe, 20, wr_bd, wi_bd, lru_br, lru_bi, lru_lam, h_lru, dmix0, 4, "lru_bwd")
    dx_lru, d_lconv, d_lconv_b = _conv_bwd(pe, 16, W["lru_conv_w"], lru_b, False, dxc, "lru_conv_bwd")
    dpe = jnp.concatenate([dq_r, dk_r, dv_r, dg_r, dx_lru, dy_l], axis=1).astype(bf16)
    (d_wie,) = _matmul(h0, dpe, "tn", outs=[f32], name="in_even_dw")
    (dh0,) = _matmul(dpe, W["w_in_even"], "nt", outs=[f32], name="in_even_dx")
    dx0, d_mix_nw0 = _rms_bwd(x0, mix_nw[0:1], dh0, dx1, "rms_mix_bwd_0")

    G = dict(
        mixer_norm_w=jnp.concatenate([d_mix_nw0, d_mix_nw1], axis=0),
        mlp_norm_w=jnp.concatenate([d_mlp_nw0, d_mlp_nw1], axis=0),
        final_norm_w=d_final.reshape(-1),
        w_in_even=d_wie, lru_conv_w=d_lconv, lru_conv_b=d_lconv_b.reshape(-1),
        lru_w_r=_diag_blocks(d_wr), lru_b_r=d_br.reshape(-1), lru_w_i=_diag_blocks(d_wi), lru_b_i=d_bi.reshape(-1),
        lru_lambda=d_lam.reshape(-1), w_out_even=d_woe, w_in_odd=d_wio[:, :ODD_IN], gdn_conv_w=d_gconv,
        gdn_a_log=d_alog[0, GDN_HEADS:2 * GDN_HEADS], gdn_dt_bias=d_dtb[0, GDN_HEADS:2 * GDN_HEADS],
        gdn_norm_w=d_gnw.reshape(-1), w_out_odd=d_woo, w_up0=d_wu0, w_up1=d_wu1, w_down0=d_wd0, w_down1=d_wd1,
    )
    return loss, dx0, G


_SMALL = ["mixer_norm_w", "mlp_norm_w", "final_norm_w", "lru_conv_b", "lru_w_r", "lru_b_r", "lru_w_i", "lru_b_i",
          "lru_lambda", "gdn_a_log", "gdn_dt_bias", "gdn_norm_w"]
_PACK_ROWS = 688


def _pack(parts):
    flat = jnp.concatenate([p.reshape(-1) for p in parts])
    return jnp.pad(flat, (0, _PACK_ROWS * 128 - flat.shape[0])).reshape(_PACK_ROWS, 128)


def _unpack(packed, shapes):
    flat = packed.reshape(-1)
    out, off = [], 0
    for s in shapes:
        n = int(np.prod(s))
        out.append(flat[off:off + n].reshape(s))
        off += n
    return out


def kernel(x, mixer_norm_w, mlp_norm_w, final_norm_w, w_in_even, lru_conv_w, lru_conv_b, lru_w_r, lru_b_r, lru_w_i, lru_b_i, lru_lambda, w_out_even, w_in_odd, gdn_conv_w, gdn_a_log, gdn_dt_bias, gdn_norm_w, w_out_odd, w_up, w_down, loss_target, m_mixer_norm_w, m_mlp_norm_w, m_final_norm_w, m_w_in_even, m_lru_conv_w, m_lru_conv_b, m_lru_w_r, m_lru_b_r, m_lru_w_i, m_lru_b_i, m_lru_lambda, m_w_out_even, m_w_in_odd, m_gdn_conv_w, m_gdn_a_log, m_gdn_dt_bias, m_gdn_norm_w, m_w_out_odd, m_w_up, m_w_down, v_mixer_norm_w, v_mlp_norm_w, v_final_norm_w, v_w_in_even, v_lru_conv_w, v_lru_conv_b, v_lru_w_r, v_lru_b_r, v_lru_w_i, v_lru_b_i, v_lru_lambda, v_w_out_even, v_w_in_odd, v_gdn_conv_w, v_gdn_a_log, v_gdn_dt_bias, v_gdn_norm_w, v_w_out_odd, v_w_up, v_w_down):
    Pw = dict(mixer_norm_w=mixer_norm_w, mlp_norm_w=mlp_norm_w, final_norm_w=final_norm_w, w_in_even=w_in_even,
              lru_conv_w=lru_conv_w, lru_conv_b=lru_conv_b, lru_w_r=lru_w_r, lru_b_r=lru_b_r, lru_w_i=lru_w_i,
              lru_b_i=lru_b_i, lru_lambda=lru_lambda, w_out_even=w_out_even, w_in_odd=w_in_odd, gdn_conv_w=gdn_conv_w,
              gdn_a_log=gdn_a_log, gdn_dt_bias=gdn_dt_bias, gdn_norm_w=gdn_norm_w, w_out_odd=w_out_odd, w_up=w_up,
              w_down=w_down)
    Pm = dict(mixer_norm_w=m_mixer_norm_w, mlp_norm_w=m_mlp_norm_w, final_norm_w=m_final_norm_w, w_in_even=m_w_in_even,
              lru_conv_w=m_lru_conv_w, lru_conv_b=m_lru_conv_b, lru_w_r=m_lru_w_r, lru_b_r=m_lru_b_r, lru_w_i=m_lru_w_i,
              lru_b_i=m_lru_b_i, lru_lambda=m_lru_lambda, w_out_even=m_w_out_even, w_in_odd=m_w_in_odd,
              gdn_conv_w=m_gdn_conv_w, gdn_a_log=m_gdn_a_log, gdn_dt_bias=m_gdn_dt_bias, gdn_norm_w=m_gdn_norm_w,
              w_out_odd=m_w_out_odd, w_up=m_w_up, w_down=m_w_down)
    Pv = dict(mixer_norm_w=v_mixer_norm_w, mlp_norm_w=v_mlp_norm_w, final_norm_w=v_final_norm_w, w_in_even=v_w_in_even,
              lru_conv_w=v_lru_conv_w, lru_conv_b=v_lru_conv_b, lru_w_r=v_lru_w_r, lru_b_r=v_lru_b_r, lru_w_i=v_lru_w_i,
              lru_b_i=v_lru_b_i, lru_lambda=v_lru_lambda, w_out_even=v_w_out_even, w_in_odd=v_w_in_odd,
              gdn_conv_w=v_gdn_conv_w, gdn_a_log=v_gdn_a_log, gdn_dt_bias=v_gdn_dt_bias, gdn_norm_w=v_gdn_norm_w,
              w_out_odd=v_w_out_odd, w_up=v_w_up, w_down=v_w_down)
    me = _me()[3]
    T = x.shape[1]

    shards = [w_in_even[0].astype(bf16), w_out_even[0].astype(bf16), w_in_odd[0].astype(bf16), w_out_odd[0].astype(bf16),
              w_up.reshape(2 * D_MODEL, D_FF // N_DEV).astype(bf16), w_down.reshape(2 * D_FF // N_DEV, D_MODEL).astype(bf16),
              lru_conv_w[0], gdn_conv_w[0]]
    g_wie, g_woe, g_wio, g_woo, g_wu, g_wd, g_lc, g_gc = _all_gather(shards, "gather_weights")
    cols = lambda g: jnp.transpose(g, (1, 0, 2)).reshape(g.shape[1], -1)
    wu = jnp.transpose(g_wu.reshape(N_DEV, 2, D_MODEL, D_FF // N_DEV), (1, 2, 0, 3)).reshape(2, D_MODEL, D_FF)
    wd = jnp.transpose(g_wd.reshape(N_DEV, 2, D_FF // N_DEV, D_MODEL), (1, 0, 2, 3)).reshape(2, D_FF, D_MODEL)
    W = {k: Pw[k] for k in _SMALL}
    W["lru_w_r"], W["lru_w_i"] = lru_w_r[0], lru_w_i[0]
    for k in ("lru_conv_b", "lru_b_r", "lru_b_i", "lru_lambda", "gdn_a_log", "gdn_dt_bias", "gdn_norm_w"):
        W[k] = Pw[k][0]
    W.update(w_in_even=cols(g_wie), w_out_even=g_woe.reshape(D_MODEL, D_MODEL),
             w_in_odd=jnp.pad(cols(g_wio), ((0, 0), (0, ODD_PAD - ODD_IN))), w_out_odd=g_woo.reshape(D_MODEL, D_MODEL),
             w_up0=wu[0], w_up1=wu[1], w_down0=wd[0], w_down1=wd[1], lru_conv_w=cols(g_lc), gdn_conv_w=cols(g_gc))

    loss, dx, G = _local_step(x[0], loss_target[0], W)

    by_cols = lambda g, n: jnp.transpose(g.reshape(g.shape[0], N_DEV, n), (1, 0, 2))
    nff = D_FF // N_DEV
    s_wu = jnp.stack([by_cols(G["w_up0"], nff), by_cols(G["w_up1"], nff)], axis=1).reshape(N_DEV, 2 * D_MODEL, nff)
    s_wd = jnp.stack([G["w_down0"].reshape(N_DEV, nff, D_MODEL), G["w_down1"].reshape(N_DEV, nff, D_MODEL)],
                     axis=1).reshape(N_DEV, 2 * nff, D_MODEL)
    small_g = [G[k].reshape(Pw[k].shape) for k in _SMALL] + [G["lru_conv_w"], G["gdn_conv_w"]]
    packed = _pack(small_g)
    sends = [by_cols(G["w_in_even"], 384), G["w_out_even"].reshape(N_DEV, 128, D_MODEL), by_cols(G["w_in_odd"], 514),
             G["w_out_odd"].reshape(N_DEV, 128, D_MODEL), s_wu, s_wd, jnp.broadcast_to(packed[None], (N_DEV,) + packed.shape)]
    r_wie, r_woe, r_wio, r_woo, r_wu, r_wd, r_small = _exchange(sends, "exchange_grads")

    out = {}

    def upd(name, gs, shape2d):
        g, d, mn, vn = _adamw(Pw[name].reshape(shape2d), gs, Pm[name].reshape(shape2d), Pv[name].reshape(shape2d), f"adamw_{name}")
        sh = Pw[name].shape
        out[name] = (g.reshape(sh), d.reshape(sh), mn.reshape(sh), vn.reshape(sh))

    upd("w_in_even", r_wie, (D_MODEL, 384))
    upd("w_out_even", r_woe, (128, D_MODEL))
    upd("w_in_odd", r_wio, (D_MODEL, 514))
    upd("w_out_odd", r_woo, (128, D_MODEL))
    upd("w_up", r_wu, (2 * D_MODEL, nff))
    upd("w_down", r_wd, (2 * nff, D_MODEL))
    small_shapes = [Pw[k].shape for k in _SMALL]
    pw, pm, pv = (_pack([P[k] for k in _SMALL]) for P in (Pw, Pm, Pv))
    sg, sd, sm, sv = _adamw(pw, r_small, pm, pv, "adamw_small")
    for arrs_i, packed_out in enumerate((sg, sd, sm, sv)):
        for k, a in zip(_SMALL, _unpack(packed_out, small_shapes)):
            out.setdefault(k, [None] * 4)[arrs_i] = a
    n_small = sum(int(np.prod(s)) for s in small_shapes)
    gflat = sg.reshape(-1)
    g_lconv = gflat[n_small:n_small + CONV_K * LRU_WIDTH].reshape(CONV_K, LRU_WIDTH)
    g_gconv = gflat[n_small + CONV_K * LRU_WIDTH:n_small + CONV_K * (LRU_WIDTH + 3072)].reshape(CONV_K, 3072)
    upd("lru_conv_w", lax.dynamic_slice_in_dim(g_lconv, me * 64, 64, axis=1)[None], (CONV_K, 64))
    upd("gdn_conv_w", lax.dynamic_slice_in_dim(g_gconv, me * 384, 384, axis=1)[None], (CONV_K, 384))

    names = ["mixer_norm_w", "mlp_norm_w", "final_norm_w", "w_in_even", "lru_conv_w", "lru_conv_b", "lru_w_r", "lru_b_r",
             "lru_w_i", "lru_b_i", "lru_lambda", "w_out_even", "w_in_odd", "gdn_conv_w", "gdn_a_log", "gdn_dt_bias",
             "gdn_norm_w", "w_out_odd", "w_up", "w_down"]
    total = lax.psum(loss[0, 0], ("x", "y", "c"))
    res = [total, dx[None]]
    for j in range(4):
        res += [out[k][j] for k in names]
    return tuple(res)
```

```python
import math

import numpy as np
import jax
import jax.numpy as jnp
from jax import lax
from jax.experimental import pallas as pl
from jax.experimental.pallas import tpu as pltpu

f32 = jnp.float32
bf16 = jnp.bfloat16

N_DEV = 8
D_MODEL = 1024
D_FF = 4096
EPS = 1e-6
RET_HEADS = 4
RET_CHUNK = 128
ROPE_THETA = 10000.0
LRU_WIDTH = 512
LRU_C = 8.0
GDN_HEADS = 8
GDN_CHUNK = 64
HEAD_DIM = 128
ODD_IN = 4112
ODD_PAD = 4224
ADAM_LR, ADAM_B1, ADAM_B2, ADAM_EPS, ADAM_WD, ADAM_STEP = 0.001, 0.9, 0.999, 1e-08, 0.01, 10
VMEM_LIMIT = 56 * 1024 * 1024

_NN = (((1,), (0,)), ((), ()))
_NT = (((1,), (1,)), ((), ()))
_TN = (((0,), (0,)), ((), ()))
MESH = pl.DeviceIdType.MESH


def _cparams(sem):
    return pltpu.CompilerParams(dimension_semantics=sem, vmem_limit_bytes=VMEM_LIMIT)


def _dot(a, b, dn):
    return lax.dot_general(a.astype(bf16), b.astype(bf16), dn, preferred_element_type=f32)


def _split2(a):
    hi = a.astype(bf16)
    lo = (a - hi.astype(f32)).astype(bf16)
    return hi, lo


def _dot3(a, b, dn):
    ah, al = _split2(a)
    bh, bl = _split2(b)
    d = lambda p, q: lax.dot_general(p, q, dn, preferred_element_type=f32)
    return d(ah, bh) + (d(ah, bl) + d(al, bh))


def _dot01(a01, b, dn):
    a = a01.astype(bf16)
    b0 = b.astype(bf16)
    r1 = b - b0.astype(f32)
    b1 = r1.astype(bf16)
    b2 = (r1 - b1.astype(f32)).astype(bf16)
    d = lambda q: lax.dot_general(a, q, dn, preferred_element_type=f32)
    return d(b0) + (d(b1) + d(b2))


def _sigmoid(x):
    return jax.nn.sigmoid(x)


def _silu(x):
    return x * _sigmoid(x)


def _dsilu(x):
    s = _sigmoid(x)
    return s * (1.0 + x * (1.0 - s))


def _softplus(x):
    return jnp.maximum(x, 0.0) + jnp.log1p(jnp.exp(-jnp.abs(x)))


_GELU_C = math.sqrt(2.0 / math.pi)


def _gelu(y):
    return 0.5 * y * (1.0 + jnp.tanh(_GELU_C * (y + 0.044715 * y * y * y)))


def _dgelu(y):
    t = jnp.tanh(_GELU_C * (y + 0.044715 * y * y * y))
    return 0.5 * (1.0 + t) + 0.5 * y * (1.0 - t * t) * _GELU_C * (1.0 + 3.0 * 0.044715 * y * y)


def _matmul(a, b, form, *, outs, name, epilogue=None, extras=(), tm=1024, tn=512, tk=1024, shard_cols=False):
    if form == "tn":
        K, M = a.shape
    else:
        M, K = a.shape
    N = b.shape[0] if form == "nt" else b.shape[1]
    if shard_cols:
        tn = N // N_DEV
    tm, tn, tk = min(tm, M), min(tn, N), min(tk, K)
    assert M % tm == 0 and N % tn == 0 and K % tk == 0, (name, M, N, K, tm, tn, tk)
    nk = K // tk
    dn = {"nn": _NN, "nt": _NT, "tn": _TN}[form]
    if form == "tn":
        a_spec = pl.BlockSpec((tk, tm), lambda i, j, k: (k, i))
    else:
        a_spec = pl.BlockSpec((tm, tk), lambda i, j, k: (i, k))
    if form == "nt":
        b_spec = pl.BlockSpec((tn, tk), lambda i, j, k: (j, k))
    else:
        b_spec = pl.BlockSpec((tk, tn), lambda i, j, k: (k, j))
    e_spec = pl.BlockSpec((tm, tn), lambda i, j, k: (i, j))
    if shard_cols:
        o_spec = pl.BlockSpec((None, tm, tn), lambda i, j, k: (j, i, 0))
        o_shape = (N_DEV, M, tn)
    else:
        o_spec = e_spec
        o_shape = (M, N)
    n_ex = len(extras)

    def finish(acc, ex, o_refs):
        vals = (acc,) if epilogue is None else epilogue(acc, *[e[...] for e in ex])
        for r, v in zip(o_refs, vals):
            r[...] = v.astype(r.dtype)

    def body_one(*refs):
        a_ref, b_ref = refs[:2]
        finish(_dot(a_ref[...], b_ref[...], dn), refs[2:2 + n_ex], refs[2 + n_ex:])

    def body_acc(*refs):
        a_ref, b_ref = refs[:2]
        acc = refs[-1]
        k = pl.program_id(2)

        @pl.when(k == 0)
        def _():
            acc[...] = _dot(a_ref[...], b_ref[...], dn)

        @pl.when((k > 0) & (k < nk - 1))
        def _():
            acc[...] += _dot(a_ref[...], b_ref[...], dn)

        @pl.when(k == nk - 1)
        def _():
            finish(acc[...] + _dot(a_ref[...], b_ref[...], dn), refs[2:2 + n_ex], refs[2 + n_ex:-1])

    return pl.pallas_call(
        body_one if nk == 1 else body_acc, grid=(M // tm, N // tn, nk),
        in_specs=[a_spec, b_spec] + [e_spec] * n_ex,
        out_specs=[o_spec] * len(outs),
        out_shape=[jax.ShapeDtypeStruct(o_shape, d) for d in outs],
        scratch_shapes=[] if nk == 1 else [pltpu.VMEM((tm, tn), f32)],
        compiler_params=_cparams(("parallel", "parallel", "arbitrary")), name=name,
    )(a, b, *extras)


def _rms_fwd(x, w, name):
    T, D = x.shape
    tt = min(512, T)

    def body(x_ref, w_ref, h_ref):
        xv = x_ref[...]
        r = lax.rsqrt(jnp.mean(xv * xv, axis=1, keepdims=True) + EPS)
        h_ref[...] = (xv * r * w_ref[...]).astype(bf16)

    return pl.pallas_call(
        body, grid=(T // tt,),
        in_specs=[pl.BlockSpec((tt, D), lambda i: (i, 0)), pl.BlockSpec((1, D), lambda i: (0, 0))],
        out_specs=pl.BlockSpec((tt, D), lambda i: (i, 0)),
        out_shape=jax.ShapeDtypeStruct((T, D), bf16),
        compiler_params=_cparams(("parallel",)), name=name,
    )(x, w)


def _rms_bwd(x, w, dh, dres, name):
    T, D = x.shape
    tt = min(512, T)

    def body(x_ref, w_ref, dh_ref, dres_ref, dx_ref, dxb_ref, dw_ref):
        @pl.when(pl.program_id(0) == 0)
        def _():
            dw_ref[...] = jnp.zeros_like(dw_ref)

        xv = x_ref[...]
        r = lax.rsqrt(jnp.mean(xv * xv, axis=1, keepdims=True) + EPS)
        xn = xv * r
        dhv = dh_ref[...]
        dhw = dhv * w_ref[...]
        dx = dres_ref[...] + r * (dhw - xn * jnp.mean(dhw * xn, axis=1, keepdims=True))
        dx_ref[...] = dx
        dxb_ref[...] = dx.astype(bf16)
        dw_ref[...] += jnp.sum(dhv * xn, axis=0, keepdims=True)

    row = pl.BlockSpec((tt, D), lambda i: (i, 0))
    vec = pl.BlockSpec((1, D), lambda i: (0, 0))
    return pl.pallas_call(
        body, grid=(T // tt,), in_specs=[row, vec, row, row], out_specs=[row, row, vec],
        out_shape=[jax.ShapeDtypeStruct((T, D), f32), jax.ShapeDtypeStruct((T, D), bf16), jax.ShapeDtypeStruct((1, D), f32)],
        compiler_params=_cparams(("arbitrary",)), name=name,
    )(x, w, dh, dres)


def _loss_head(x, w, tgt, name):
    T, D = x.shape
    tt = min(512, T)

    def body(x_ref, w_ref, t_ref, l_ref, dx_ref, dxb_ref, dw_ref):
        @pl.when(pl.program_id(0) == 0)
        def _():
            dw_ref[...] = jnp.zeros_like(dw_ref)
            l_ref[...] = jnp.zeros_like(l_ref)

        xv = x_ref[...]
        wv = w_ref[...]
        r = lax.rsqrt(jnp.mean(xv * xv, axis=1, keepdims=True) + EPS)
        xn = xv * r
        e = xn * wv - t_ref[...]
        l_ref[...] += 0.5 * jnp.sum(jnp.mean(e * e, axis=1, keepdims=True), axis=0, keepdims=True)
        dy = e * (1.0 / D)
        dyw = dy * wv
        dx = r * (dyw - xn * jnp.mean(dyw * xn, axis=1, keepdims=True))
        dx_ref[...] = dx
        dxb_ref[...] = dx.astype(bf16)
        dw_ref[...] += jnp.sum(dy * xn, axis=0, keepdims=True)

    row = pl.BlockSpec((tt, D), lambda i: (i, 0))
    vec = pl.BlockSpec((1, D), lambda i: (0, 0))
    return pl.pallas_call(
        body, grid=(T // tt,), in_specs=[row, vec, row],
        out_specs=[pl.BlockSpec((1, 128), lambda i: (0, 0)), row, row, vec],
        out_shape=[jax.ShapeDtypeStruct((1, 128), f32), jax.ShapeDtypeStruct((T, D), f32),
                   jax.ShapeDtypeStruct((T, D), bf16), jax.ShapeDtypeStruct((1, D), f32)],
        compiler_params=_cparams(("arbitrary",)), name=name,
    )(x, w, tgt)


def _ret_tables():
    H, C = RET_HEADS, RET_CHUNK
    lg = np.log1p(-np.exp2(-5.0 - np.arange(H, dtype=np.float32))).astype(np.float32)
    idx = np.arange(C, dtype=np.float32)
    diff = idx[:, None] - idx[None, :]
    causal = diff >= 0
    dm = np.where(causal[None], np.exp(lg[:, None, None] * np.where(causal, diff, 0.0)[None]), 0.0)
    qd = np.exp(lg[:, None] * (idx[None, :] + 1.0))
    kd = np.exp(lg[:, None] * (C - 1.0 - idx[None, :]))
    cg = np.exp(lg * C)
    tab = np.zeros((H, 4, C, HEAD_DIM), np.float32)
    tab[:, 0] = dm
    tab[:, 1] = qd[:, :, None]
    tab[:, 2] = kd[:, :, None]
    tab[:, 3] = cg[:, None, None]
    return jnp.asarray(tab)


def _rope_tables(T):
    half = HEAD_DIM // 2
    inv = ROPE_THETA ** (-jnp.arange(half, dtype=f32) / half)
    ang = jnp.arange(T, dtype=jnp.int32).astype(f32)[:, None] * inv[None, :]
    c, s = jnp.cos(ang), jnp.sin(ang)
    return jnp.concatenate([c, c], axis=1), jnp.concatenate([-s, s], axis=1)


def _rope(x, cos, sin):
    return x * cos + pltpu.roll(x, HEAD_DIM // 2, 1) * sin


def _unrope(y, cos, sin):
    return y * cos + pltpu.roll(y * sin, HEAD_DIM // 2, 1)


def _ret_fwd(p, cos, sin, tab, name):
    T = p.shape[0]
    C, H = RET_CHUNK, RET_HEADS
    N = T // C
    scale = HEAD_DIM ** -0.5

    def body(q_ref, k_ref, v_ref, g_ref, c_ref, s_ref, t_ref, y_ref, o_ref, sp_ref, st):
        n, h = pl.program_id(0), pl.program_id(1)

        @pl.when(n == 0)
        def _():
            st[h] = jnp.zeros((HEAD_DIM, HEAD_DIM), f32)

        cos_, sin_ = c_ref[...], s_ref[...]
        q = _rope(q_ref[...], cos_, sin_)
        k = _rope(k_ref[...], cos_, sin_) * scale
        v = v_ref[...]
        dm, qd, kd, cg = t_ref[0, 0], t_ref[0, 1], t_ref[0, 2], t_ref[0, 3]
        S = st[h]
        P = _dot(q, k, _NT) * dm
        o = _dot(P, v, _NN) + _dot(q * qd, S, _NN)
        sp_ref[0, 0] = S
        st[h] = cg * S + _dot(k * kd, v, _TN)
        o_ref[...] = o
        r = lax.rsqrt(jnp.mean(o * o, axis=1, keepdims=True) + EPS)
        y_ref[...] = (o * r * _silu(g_ref[...])).astype(bf16)

    blk = lambda off: pl.BlockSpec((C, HEAD_DIM), lambda n, h: (n, off + h))
    tbl = pl.BlockSpec((C, HEAD_DIM), lambda n, h: (n, 0))
    return pl.pallas_call(
        body, grid=(N, H),
        in_specs=[blk(0), blk(H), blk(2 * H), blk(3 * H), tbl, tbl,
                  pl.BlockSpec((1, 4, C, HEAD_DIM), lambda n, h: (h, 0, 0, 0))],
        out_specs=[blk(0), blk(0), pl.BlockSpec((1, 1, HEAD_DIM, HEAD_DIM), lambda n, h: (n, h, 0, 0))],
        out_shape=[jax.ShapeDtypeStruct((T, H * HEAD_DIM), bf16), jax.ShapeDtypeStruct((T, H * HEAD_DIM), f32),
                   jax.ShapeDtypeStruct((N, H, HEAD_DIM, HEAD_DIM), f32)],
        scratch_shapes=[pltpu.VMEM((H, HEAD_DIM, HEAD_DIM), f32)],
        compiler_params=_cparams(("arbitrary", "arbitrary")), name=name,
    )(p, p, p, p, cos, sin, tab)


def _ret_bwd(p, cos, sin, tab, o_raw, sprev, dmix, name):
    T = p.shape[0]
    C, H = RET_CHUNK, RET_HEADS
    N = T // C
    scale = HEAD_DIM ** -0.5

    def body(q_ref, k_ref, v_ref, g_ref, c_ref, s_ref, t_ref, o_ref, sp_ref, dy_ref,
             dq_ref, dk_ref, dv_ref, dg_ref, dst):
        n, h = pl.program_id(0), pl.program_id(1)

        @pl.when(n == 0)
        def _():
            dst[h] = jnp.zeros((HEAD_DIM, HEAD_DIM), f32)

        cos_, sin_ = c_ref[...], s_ref[...]
        q = _rope(q_ref[...], cos_, sin_)
        k = _rope(k_ref[...], cos_, sin_) * scale
        v = v_ref[...]
        g = g_ref[...]
        dm, qd, kd, cg = t_ref[0, 0], t_ref[0, 1], t_ref[0, 2], t_ref[0, 3]
        S = sp_ref[0, 0]
        o = o_ref[...]
        dy = dy_ref[...]
        r = lax.rsqrt(jnp.mean(o * o, axis=1, keepdims=True) + EPS)
        nrm = o * r
        dn = dy * _silu(g)
        dg_ref[...] = dy * nrm * _dsilu(g)
        do = r * (dn - nrm * jnp.mean(dn * nrm, axis=1, keepdims=True))
        dS1 = dst[h]
        P = _dot(q, k, _NT) * dm
        dP = _dot(do, v, _NT) * dm
        dq = _dot(dP, k, _NN) + _dot(do, S, _NT) * qd
        dk = _dot(dP, q, _TN) + _dot(v, dS1, _NT) * kd
        dv_ref[...] = _dot(P, do, _TN) + _dot(k * kd, dS1, _NN)
        dst[h] = cg * dS1 + _dot(q * qd, do, _TN)
        dq_ref[...] = _unrope(dq, cos_, sin_)
        dk_ref[...] = _unrope(dk * scale, cos_, sin_)

    rev = lambda off: pl.BlockSpec((C, HEAD_DIM), lambda n, h: (N - 1 - n, off + h))
    tbl = pl.BlockSpec((C, HEAD_DIM), lambda n, h: (N - 1 - n, 0))
    out = jax.ShapeDtypeStruct((T, H * HEAD_DIM), f32)
    return pl.pallas_call(
        body, grid=(N, H),
        in_specs=[rev(0), rev(H), rev(2 * H), rev(3 * H), tbl, tbl,
                  pl.BlockSpec((1, 4, C, HEAD_DIM), lambda n, h: (h, 0, 0, 0)), rev(0),
                  pl.BlockSpec((1, 1, HEAD_DIM, HEAD_DIM), lambda n, h: (N - 1 - n, h, 0, 0)), rev(0)],
        out_specs=[rev(0)] * 4, out_shape=[out] * 4,
        scratch_shapes=[pltpu.VMEM((H, HEAD_DIM, HEAD_DIM), f32)],
        compiler_params=_cparams(("arbitrary", "arbitrary")), name=name,
    )(p, p, p, p, cos, sin, tab, o_raw, sprev, dmix)


CONV_K = 4
PAD = 8


def _conv_fwd(x, col_off, w, b, act, name):
    T = x.shape[0]
    C = w.shape[1]
    G = C // 128
    tt = min(512, T)
    NT = T // tt
    has_b = b is not None

    def body(*refs):
        if has_b:
            x_ref, w_ref, b_ref, y_ref, pad = refs
        else:
            x_ref, w_ref, y_ref, pad = refs
        t = pl.program_id(1)

        @pl.when(t == 0)
        def _():
            pad[pl.ds(0, PAD), :] = jnp.zeros((PAD, 128), f32)

        pad[pl.ds(PAD, tt), :] = x_ref[...]
        wv = w_ref[...]
        y = wv[0:1, :] * pad[pl.ds(PAD - 3, tt), :]
        for kk in range(1, CONV_K):
            y = y + wv[kk:kk + 1, :] * pad[pl.ds(PAD - 3 + kk, tt), :]
        if has_b:
            y = y + b_ref[...]
        tail = pad[pl.ds(tt, PAD), :]
        pad[pl.ds(0, PAD), :] = tail
        y_ref[...] = _silu(y) if act else y

    in_specs = [pl.BlockSpec((tt, 128), lambda g, t: (t, col_off + g)),
                pl.BlockSpec((CONV_K, 128), lambda g, t: (0, g))]
    args = [x, w]
    if has_b:
        in_specs.append(pl.BlockSpec((1, 128), lambda g, t: (0, g)))
        args.append(b)
    return pl.pallas_call(
        body, grid=(G, NT), in_specs=in_specs,
        out_specs=pl.BlockSpec((tt, 128), lambda g, t: (t, g)),
        out_shape=jax.ShapeDtypeStruct((T, C), f32),
        scratch_shapes=[pltpu.VMEM((tt + PAD, 128), f32)],
        compiler_params=_cparams(("parallel", "arbitrary")), name=name,
    )(*args)


def _conv_bwd(x, col_off, w, b, act, dout, name):
    T = x.shape[0]
    C = w.shape[1]
    G = C // 128
    tt = min(512, T)
    NT = T // tt
    has_b = b is not None

    def body(*refs):
        if has_b:
            x_ref, xp_ref, w_ref, b_ref, d_ref, dx_ref, dw_ref, db_ref, pad, dpad = refs
        else:
            x_ref, xp_ref, w_ref, d_ref, dx_ref, dw_ref, db_ref, pad, dpad = refs
        t = pl.program_id(1)
        first_tile = t == NT - 1

        @pl.when(t == 0)
        def _():
            dpad[pl.ds(tt, PAD), :] = jnp.zeros((PAD, 128), f32)
            dw_ref[...] = jnp.zeros_like(dw_ref)
            db_ref[...] = jnp.zeros_like(db_ref)

        pad[pl.ds(0, PAD), :] = jnp.where(first_tile, 0.0, xp_ref[...])
        pad[pl.ds(PAD, tt), :] = x_ref[...]
        wv = w_ref[...]
        dy = d_ref[...]
        if act:
            y = wv[0:1, :] * pad[pl.ds(PAD - 3, tt), :]
            for kk in range(1, CONV_K):
                y = y + wv[kk:kk + 1, :] * pad[pl.ds(PAD - 3 + kk, tt), :]
            if has_b:
                y = y + b_ref[...]
            dy = dy * _dsilu(y)
        dpad[pl.ds(0, tt), :] = dy
        dx = wv[3:4, :] * dy
        for j in range(1, CONV_K):
            dx = dx + wv[3 - j:4 - j, :] * dpad[pl.ds(j, tt), :]
        dx_ref[...] = dx
        head = dpad[pl.ds(0, PAD), :]
        dpad[pl.ds(tt, PAD), :] = head
        for kk in range(CONV_K):
            dw_ref[kk:kk + 1, :] += jnp.sum(dy * pad[pl.ds(PAD - 3 + kk, tt), :], axis=0, keepdims=True)
        db_ref[...] += jnp.sum(dy, axis=0, keepdims=True)

    rows8 = tt // PAD
    in_specs = [pl.BlockSpec((tt, 128), lambda g, t: (NT - 1 - t, col_off + g)),
                pl.BlockSpec((PAD, 128), lambda g, t: (jnp.maximum((NT - 1 - t) * rows8 - 1, 0), col_off + g)),
                pl.BlockSpec((CONV_K, 128), lambda g, t: (0, g))]
    args = [x, x, w]
    if has_b:
        in_specs.append(pl.BlockSpec((1, 128), lambda g, t: (0, g)))
        args.append(b)
    in_specs.append(pl.BlockSpec((tt, 128), lambda g, t: (NT - 1 - t, g)))
    args.append(dout)
    return pl.pallas_call(
        body, grid=(G, NT), in_specs=in_specs,
        out_specs=[pl.BlockSpec((tt, 128), lambda g, t: (NT - 1 - t, g)),
                   pl.BlockSpec((CONV_K, 128), lambda g, t: (0, g)),
                   pl.BlockSpec((1, 128), lambda g, t: (0, g))],
        out_shape=[jax.ShapeDtypeStruct((T, C), f32), jax.ShapeDtypeStruct((CONV_K, C), f32),
                   jax.ShapeDtypeStruct((1, C), f32)],
        scratch_shapes=[pltpu.VMEM((tt + PAD, 128), f32), pltpu.VMEM((tt + PAD, 128), f32)],
        compiler_params=_cparams(("parallel", "arbitrary")), name=name,
    )(*args)


def _lru_gates(xc, wr, wi, br, bi, lam):
    r = _sigmoid(_dot(xc, wr, _NN) + br)
    i = _sigmoid(_dot(xc, wi, _NN) + bi)
    sp = _softplus(-lam)
    a = jnp.exp(-LRU_C * r * sp)
    mult = jnp.sqrt(1.0 - a * a)
    return r, i, sp, a, mult


def _lru_fwd(xc, p, y_off, wr, wi, br, bi, lam, name):
    T = xc.shape[0]
    G = LRU_WIDTH // 128
    tt = min(512, T)
    NT = T // tt

    def body(x_ref, y_ref, wr_ref, wi_ref, br_ref, bi_ref, l_ref, o_ref, h_ref, hc):
        t = pl.program_id(1)

        @pl.when(t == 0)
        def _():
            hc[...] = jnp.zeros_like(hc)

        x = x_ref[...]
        r, i, sp, a, mult = _lru_gates(x, wr_ref[...], wi_ref[...], br_ref[...], bi_ref[...], l_ref[...])
        row = lax.broadcasted_iota(jnp.int32, (tt, 128), 0)
        mult = jnp.where((row == 0) & (t == 0), 1.0, mult)
        U = x * i * mult
        A = a
        d = 1
        while d < tt:
            keep = row >= d
            Ush = jnp.where(keep, pltpu.roll(U, d, 0), 0.0)
            Ash = jnp.where(keep, pltpu.roll(A, d, 0), 1.0)
            U = A * Ush + U
            A = A * Ash
            d *= 2
        h = U + A * hc[0:1, :]
        h_ref[...] = h
        hc[...] = jnp.broadcast_to(h[tt - 1:tt, :], hc.shape)
        o_ref[...] = (h * _gelu(y_ref[...])).astype(bf16)

    tile = pl.BlockSpec((tt, 128), lambda g, t: (t, g))
    vec = pl.BlockSpec((1, 128), lambda g, t: (0, g))
    wsp = pl.BlockSpec((128, 128), lambda g, t: (g, g))
    return pl.pallas_call(
        body, grid=(G, NT),
        in_specs=[tile, pl.BlockSpec((tt, 128), lambda g, t: (t, y_off + g)), wsp, wsp, vec, vec, vec],
        out_specs=[tile, tile],
        out_shape=[jax.ShapeDtypeStruct((T, LRU_WIDTH), bf16), jax.ShapeDtypeStruct((T, LRU_WIDTH), f32)],
        scratch_shapes=[pltpu.VMEM((8, 128), f32)],
        compiler_params=_cparams(("parallel", "arbitrary")), name=name,
    )(xc, p, wr, wi, br, bi, lam)


def _lru_bwd(xc, p, y_off, wr, wi, br, bi, lam, hs, dmix, d_off, name):
    T = xc.shape[0]
    G = LRU_WIDTH // 128
    tt = min(512, T)
    NT = T // tt

    def body(x_ref, y_ref, wr_ref, wi_ref, br_ref, bi_ref, l_ref, h_ref, hp_ref, do_ref,
             dx_ref, dy_ref, dwr_ref, dwi_ref, dbr_ref, dbi_ref, dl_ref, lc, an):
        t = pl.program_id(1)
        first_tile = t == NT - 1

        @pl.when(t == 0)
        def _():
            lc[...] = jnp.zeros_like(lc)
            an[...] = jnp.zeros_like(an)
            dwr_ref[...] = jnp.zeros_like(dwr_ref)
            dwi_ref[...] = jnp.zeros_like(dwi_ref)
            dbr_ref[...] = jnp.zeros_like(dbr_ref)
            dbi_ref[...] = jnp.zeros_like(dbi_ref)
            dl_ref[...] = jnp.zeros_like(dl_ref)

        x = x_ref[...]
        y = y_ref[...]
        wr, wi, lam_ = wr_ref[...], wi_ref[...], l_ref[...]
        r, i, sp, a, mult_raw = _lru_gates(x, wr, wi, br_ref[...], bi_ref[...], lam_)
        row = lax.broadcasted_iota(jnp.int32, (tt, 128), 0)
        t0 = (row == 0) & first_tile
        mult = jnp.where(t0, 1.0, mult_raw)
        h = h_ref[...]
        do = do_ref[...]
        dh = do * _gelu(y)
        dy_ref[...] = do * h * _dgelu(y)
        B = jnp.where(row == tt - 1, an[0:1, :], pltpu.roll(a, tt - 1, 0))
        L = dh
        d = 1
        while d < tt:
            keep = row < tt - d
            Lsh = jnp.where(keep, pltpu.roll(L, tt - d, 0), 0.0)
            Bsh = jnp.where(keep, pltpu.roll(B, tt - d, 0), 1.0)
            L = L + B * Lsh
            B = B * Bsh
            d *= 2
        L = L + B * lc[0:1, :]
        lc[...] = jnp.broadcast_to(L[0:1, :], lc.shape)
        an[...] = jnp.broadcast_to(a[0:1, :], an.shape)
        hprev = jnp.where(first_tile, 0.0, hp_ref[...])[PAD - 1:PAD, :]
        hm1 = jnp.where(row == 0, hprev, pltpu.roll(h, 1, 0))
        da = L * hm1
        dxc = L * i * mult
        di = L * x * mult
        dmult = jnp.where(t0, 0.0, L * x * i)
        da = da - jnp.where(t0, 0.0, dmult * a / mult_raw)
        dlog_a = da * a
        dr = dlog_a * (-LRU_C) * sp
        dsp = jnp.sum(dlog_a * (-LRU_C) * r, axis=0, keepdims=True)
        dpr = dr * r * (1.0 - r)
        dpi = di * i * (1.0 - i)
        dx_ref[...] = dxc + _dot(dpr, wr, _NT) + _dot(dpi, wi, _NT)
        dwr_ref[0] += _dot(x, dpr, _TN)
        dwi_ref[0] += _dot(x, dpi, _TN)
        dbr_ref[...] += jnp.sum(dpr, axis=0, keepdims=True)
        dbi_ref[...] += jnp.sum(dpi, axis=0, keepdims=True)
        dl_ref[...] += dsp * (-_sigmoid(-lam_))

    rows8 = tt // PAD
    tile = pl.BlockSpec((tt, 128), lambda g, t: (NT - 1 - t, g))
    vec = pl.BlockSpec((1, 128), lambda g, t: (0, g))
    wsp = pl.BlockSpec((128, 128), lambda g, t: (g, g))
    wout = pl.BlockSpec((1, 128, 128), lambda g, t: (g, 0, 0))
    return pl.pallas_call(
        body, grid=(G, NT),
        in_specs=[tile, pl.BlockSpec((tt, 128), lambda g, t: (NT - 1 - t, y_off + g)), wsp, wsp, vec, vec, vec, tile,
                  pl.BlockSpec((PAD, 128), lambda g, t: (jnp.maximum((NT - 1 - t) * rows8 - 1, 0), g)),
                  pl.BlockSpec((tt, 128), lambda g, t: (NT - 1 - t, d_off + g))],
        out_specs=[tile, tile, wout, wout, vec, vec, vec],
        out_shape=[jax.ShapeDtypeStruct((T, LRU_WIDTH), f32), jax.ShapeDtypeStruct((T, LRU_WIDTH), f32),
                   jax.ShapeDtypeStruct((G, 128, 128), f32), jax.ShapeDtypeStruct((G, 128, 128), f32),
                   jax.ShapeDtypeStruct((1, LRU_WIDTH), f32), jax.ShapeDtypeStruct((1, LRU_WIDTH), f32),
                   jax.ShapeDtypeStruct((1, LRU_WIDTH), f32)],
        scratch_shapes=[pltpu.VMEM((8, 128), f32), pltpu.VMEM((8, 128), f32)],
        compiler_params=_cparams(("parallel", "arbitrary")), name=name,
    )(xc, p, wr, wi, br, bi, lam, hs, hs, dmix)


_NN3 = (((2,), (1,)), ((0,), (0,)))
_NT3 = (((2,), (2,)), ((0,), (0,)))
_TN3 = (((1,), (1,)), ((0,), (0,)))


def _heads(ref):
    return jnp.stack([ref[:, h * HEAD_DIM:(h + 1) * HEAD_DIM] for h in range(GDN_HEADS)])


def _gdn_chunk(qr, kr, v, ba, alog, dtb, S):
    C, H = GDN_CHUNK, GDN_HEADS
    lane = lax.broadcasted_iota(jnp.int32, (C, 128), 1)
    lane3 = lax.broadcasted_iota(jnp.int32, (H, C, 128), 2)
    ri = lax.broadcasted_iota(jnp.int32, (C, C), 0)
    ci = lax.broadcasted_iota(jnp.int32, (C, C), 1)
    rowc = lax.broadcasted_iota(jnp.int32, (C, 1), 0)
    col = lambda m, j: jnp.sum(jnp.where(lane == j, m, 0.0), axis=1, keepdims=True)
    cols = lambda m, off: jnp.stack([col(m, off + h) for h in range(H)])
    ea = jnp.exp(alog)
    g_all = -ea * _softplus(ba + dtb)
    tri = (ri >= ci).astype(f32)
    G_all = _dot01(tri, g_all, _NN)
    beta = cols(_sigmoid(ba), 0)
    Gc = cols(G_all, H)
    rq = lax.rsqrt(jnp.sum(qr * qr, axis=2, keepdims=True) + EPS)
    rk = lax.rsqrt(jnp.sum(kr * kr, axis=2, keepdims=True) + EPS)
    qh, kn = qr * rq, kr * rk
    qn = qh * (HEAD_DIM ** -0.5)
    Grow = _dot01(jnp.ones((H, C, 128), f32), jnp.where(lane3 == 0, Gc, 0.0), _NT3)
    incl = ri >= ci
    Di = jnp.where(incl, jnp.exp(jnp.where(incl, Gc - Grow, 0.0)), 0.0)
    Ds = jnp.where(ri > ci, Di, 0.0)
    Gl = jnp.sum(jnp.where(rowc == C - 1, Gc, 0.0), axis=1, keepdims=True)
    eG = jnp.exp(Gc)
    eGl = jnp.exp(Gl - Gc)
    cd = jnp.exp(Gl)
    kb = kn * beta
    vb = v * beta
    Lm = _dot(kb, kn, _NT3) * Ds
    eye = (ri == ci).astype(f32)
    X = -Lm
    Tinv = eye + X
    Pw = X
    for _ in range(5):
        Pw = _dot3(Pw, Pw, _NN3)
        Tinv = Tinv + _dot3(Tinv, Pw, _NN3)
    kbg = kb * eG
    u = _dot3(Tinv, vb, _NN3)
    w = _dot3(Tinv, kbg, _NN3)
    QK = _dot(qn, kn, _NT3) * Di
    qg = qn * eG
    kg = kn * eGl
    vn = u - _dot(w, S, _NN3)
    o = _dot(qg, S, _NN3) + _dot(QK, vn, _NN3)
    S1 = S * cd + _dot(kg, vn, _TN3)
    return dict(beta=beta, g_all=g_all, rq=rq, rk=rk, qh=qh, kn=kn, qn=qn, Di=Di, Ds=Ds, eG=eG, eGl=eGl, cd=cd,
                kb=kb, vb=vb, Lm=Lm, Tinv=Tinv, kbg=kbg, w=w, QK=QK, qg=qg, kg=kg, vn=vn, o=o, S1=S1,
                lane=lane, ri=ri, ci=ci, rowc=rowc, ea=ea)


def _gdn_specs(N, rev):
    C = GDN_CHUNK
    H = GDN_HEADS
    nn = (lambda n: N - 1 - n) if rev else (lambda n: n)
    wide = lambda blk: pl.BlockSpec((C, H * HEAD_DIM), lambda n: (nn(n), blk))
    one = lambda off: pl.BlockSpec((C, HEAD_DIM), lambda n: (nn(n), off))
    vec = pl.BlockSpec((1, 128), lambda n: (0, 0))
    st = pl.BlockSpec((1, H, HEAD_DIM, HEAD_DIM), lambda n: (nn(n), 0, 0, 0))
    return wide, one, vec, st


def _gdn_fwd(qkv, p, alog, dtb, nw, name):
    T = qkv.shape[0]
    C, H = GDN_CHUNK, GDN_HEADS
    N = T // C
    wide, one, vec, stspec = _gdn_specs(N, False)

    def body(q_ref, k_ref, v_ref, z_ref, ba_ref, al_ref, dt_ref, nw_ref, y_ref, sp_ref, st):
        @pl.when(pl.program_id(0) == 0)
        def _():
            st[...] = jnp.zeros_like(st)

        S = st[...]
        f = _gdn_chunk(_heads(q_ref), _heads(k_ref), _heads(v_ref), ba_ref[...], al_ref[...], dt_ref[...], S)
        sp_ref[0] = S
        st[...] = f["S1"]
        o = f["o"]
        r = lax.rsqrt(jnp.mean(o * o, axis=2, keepdims=True) + EPS)
        y = o * r * nw_ref[...] * _silu(_heads(z_ref))
        for h in range(H):
            y_ref[:, h * HEAD_DIM:(h + 1) * HEAD_DIM] = y[h].astype(bf16)

    return pl.pallas_call(
        body, grid=(N,),
        in_specs=[wide(0), wide(1), wide(2), wide(3), one(4 * H), vec, vec, vec],
        out_specs=[wide(0), stspec],
        out_shape=[jax.ShapeDtypeStruct((T, H * HEAD_DIM), bf16), jax.ShapeDtypeStruct((N, H, HEAD_DIM, HEAD_DIM), f32)],
        scratch_shapes=[pltpu.VMEM((H, HEAD_DIM, HEAD_DIM), f32)],
        compiler_params=_cparams(("arbitrary",)), name=name,
    )(qkv, qkv, qkv, p, p, alog, dtb, nw)


def _gdn_bwd(qkv, p, alog, dtb, nw, sprev, dy_all, name):
    T = qkv.shape[0]
    C, H = GDN_CHUNK, GDN_HEADS
    N = T // C
    wide, one, vec, stspec = _gdn_specs(N, True)
    rs = lambda m: jnp.sum(m, axis=2, keepdims=True)

    def put(ref, val):
        for h in range(H):
            ref[:, h * HEAD_DIM:(h + 1) * HEAD_DIM] = val[h]

    def body(q_ref, k_ref, v_ref, z_ref, ba_ref, al_ref, dt_ref, nw_ref, sp_ref, dy_ref,
             dq_ref, dk_ref, dv_ref, dz_ref, dba_ref, dal_ref, ddt_ref, dnw_ref, dst):
        @pl.when(pl.program_id(0) == 0)
        def _():
            dst[...] = jnp.zeros_like(dst)
            dal_ref[...] = jnp.zeros_like(dal_ref)
            ddt_ref[...] = jnp.zeros_like(ddt_ref)
            dnw_ref[...] = jnp.zeros_like(dnw_ref)

        ba, alog, dtb_, nwv = ba_ref[...], al_ref[...], dt_ref[...], nw_ref[...]
        v = _heads(v_ref)
        S = sp_ref[0]
        f = _gdn_chunk(_heads(q_ref), _heads(k_ref), v, ba, alog, dtb_, S)
        beta, kn, qn, kb, vb, Tinv, kbg = f["beta"], f["kn"], f["qn"], f["kb"], f["vb"], f["Tinv"], f["kbg"]
        eG, eGl, cd, Di, Ds, QK, vn, w_, qg, kg = (f["eG"], f["eGl"], f["cd"], f["Di"], f["Ds"], f["QK"], f["vn"],
                                                    f["w"], f["qg"], f["kg"])
        lane, ri, ci, rowc = f["lane"], f["ri"], f["ci"], f["rowc"]
        o = f["o"]
        z = _heads(z_ref)
        dy = _heads(dy_ref)
        r = lax.rsqrt(jnp.mean(o * o, axis=2, keepdims=True) + EPS)
        nrm = o * r
        sz = _silu(z)
        dn = dy * nwv * sz
        put(dz_ref, dy * nrm * nwv * _dsilu(z))
        dnw_ref[...] += jnp.sum(jnp.sum(dy * nrm * sz, axis=0), axis=0, keepdims=True)
        do = r * (dn - nrm * jnp.mean(dn * nrm, axis=2, keepdims=True))
        dS1 = dst[...]
        dcd = jnp.sum(jnp.sum(S * dS1, axis=2, keepdims=True), axis=1, keepdims=True)
        dkg = _dot(vn, dS1, _NT3)
        dvn = _dot(kg, dS1, _NN3) + _dot(QK, do, _TN3)
        dqg = _dot(do, S, _NT3)
        dQK = _dot(do, vn, _NT3)
        dw = -_dot(dvn, S, _NT3)
        dst[...] = cd * dS1 + _dot(qg, do, _TN3) - _dot(w_, dvn, _TN3)
        dqn = dqg * eG
        deG = rs(dqg * qn)
        dkn = dkg * eGl
        deGl = rs(dkg * kn)
        dQKr = dQK * Di
        E = dQK * QK
        dqn = dqn + _dot(dQKr, kn, _NN3)
        dkn = dkn + _dot(dQKr, qn, _TN3)
        dT = _dot3(dvn, vb, _NT3) + _dot3(dw, kbg, _NT3)
        dvb = _dot3(Tinv, dvn, _TN3)
        dkbg = _dot3(Tinv, dw, _TN3)
        dkb = dkbg * eG
        deG = deG + rs(dkbg * kb)
        dL = -_dot3(_dot3(Tinv, dT, _TN3), Tinv, _NT3)
        dKK = dL * Ds
        E = E + dL * f["Lm"]
        dkb = dkb + _dot(dKK, kn, _NN3)
        dkn = dkn + _dot(dKK, kb, _TN3) + dkb * beta
        dbeta = rs(dkb * kn) + rs(dvb * v)
        put(dv_ref, dvb * beta)
        dG = rs(E) - rs(jnp.swapaxes(E, 1, 2)) + deG * eG - deGl * eGl
        dGl = jnp.sum(deGl * eGl, axis=1, keepdims=True) + dcd * cd
        dG = dG + jnp.where(rowc == C - 1, dGl, 0.0)
        qh = f["qh"]
        put(dq_ref, (HEAD_DIM ** -0.5) * f["rq"] * (dqn - qh * rs(dqn * qh)))
        put(dk_ref, f["rk"] * (dkn - kn * rs(dkn * kn)))
        db = dbeta * beta * (1.0 - beta)
        db_all = jnp.where(lane == 0, db[0], 0.0)
        dG_all = jnp.where(lane == H, dG[0], 0.0)
        for h in range(1, H):
            db_all = db_all + jnp.where(lane == h, db[h], 0.0)
            dG_all = dG_all + jnp.where(lane == H + h, dG[h], 0.0)
        triu = (ri <= ci).astype(f32)
        dg_all = _dot01(triu, dG_all, _NN)
        da_all = dg_all * (-f["ea"]) * _sigmoid(ba + dtb_)
        dba_ref[...] = db_all + da_all
        ddt_ref[...] += jnp.sum(da_all, axis=0, keepdims=True)
        dal_ref[...] += jnp.sum(dg_all * f["g_all"], axis=0, keepdims=True)

    big = jax.ShapeDtypeStruct((T, H * HEAD_DIM), f32)
    small = jax.ShapeDtypeStruct((1, 128), f32)
    return pl.pallas_call(
        body, grid=(N,),
        in_specs=[wide(0), wide(1), wide(2), wide(3), one(4 * H), vec, vec, vec, stspec, wide(0)],
        out_specs=[wide(0), wide(0), wide(0), wide(0), one(0), vec, vec, vec],
        out_shape=[big, big, big, big, jax.ShapeDtypeStruct((T, 128), f32), small, small, small],
        scratch_shapes=[pltpu.VMEM((H, HEAD_DIM, HEAD_DIM), f32)],
        compiler_params=_cparams(("arbitrary",)), name=name,
    )(qkv, qkv, qkv, p, p, alog, dtb, nw, sprev, dy_all)


def _adamw(w, gs, m, v, name):
    R, Cc = w.shape
    S = gs.shape[0]
    tr = R
    for cand in (256, 128, 64, 32, 16, 8):
        if R % cand == 0 and R > cand:
            tr = cand
            break
    c1 = 1.0 - ADAM_B1 ** ADAM_STEP
    c2 = 1.0 - ADAM_B2 ** ADAM_STEP

    def body(w_ref, g_ref, m_ref, v_ref, go_ref, d_ref, mo_ref, vo_ref):
        g = g_ref[0].astype(f32)
        for s in range(1, S):
            g = g + g_ref[s].astype(f32)
        mn = ADAM_B1 * m_ref[...] + (1.0 - ADAM_B1) * g
        vn = ADAM_B2 * v_ref[...] + (1.0 - ADAM_B2) * (g * g)
        go_ref[...] = g
        mo_ref[...] = mn
        vo_ref[...] = vn
        d_ref[...] = -ADAM_LR * ((mn / c1) / (jnp.sqrt(vn / c2) + ADAM_EPS) + ADAM_WD * w_ref[...])

    blk = pl.BlockSpec((tr, Cc), lambda i: (i, 0))
    out = jax.ShapeDtypeStruct((R, Cc), f32)
    return pl.pallas_call(
        body, grid=(R // tr,),
        in_specs=[blk, pl.BlockSpec((S, tr, Cc), lambda i: (0, i, 0)), blk, blk],
        out_specs=[blk] * 4, out_shape=[out] * 4,
        compiler_params=_cparams(("parallel",)), name=name,
    )(w, gs, m, v)


def _me():
    x, y, c = lax.axis_index("x"), lax.axis_index("y"), lax.axis_index("c")
    return x, y, c, 4 * x + 2 * y + c


def _peer(k):
    x, y, c, _ = _me()
    px = 1 - x if k & 4 else x
    py = 1 - y if k & 2 else y
    pc = 1 - c if k & 1 else c
    return (px, py, pc), 4 * px + 2 * py + pc


def _all_gather(arrs, name):
    n = len(arrs)

    def body(*refs):
        ins, outs = refs[:n], refs[n:2 * n]
        ssem, rsem, lsem = refs[2 * n:]
        me = _me()[3]
        for i in range(n):
            pltpu.make_async_copy(ins[i], outs[i].at[me], lsem.at[i]).start()
            for k in range(1, N_DEV):
                pid, _ = _peer(k)
                pltpu.make_async_remote_copy(src_ref=ins[i], dst_ref=outs[i].at[me], send_sem=ssem.at[i, k - 1],
                                             recv_sem=rsem.at[i, k - 1], device_id=pid, device_id_type=MESH).start()
        for i in range(n):
            for k in range(1, N_DEV):
                pid, pidx = _peer(k)
                cp = pltpu.make_async_remote_copy(src_ref=ins[i], dst_ref=outs[i].at[pidx], send_sem=ssem.at[i, k - 1],
                                                  recv_sem=rsem.at[i, k - 1], device_id=pid, device_id_type=MESH)
                cp.wait_send()
                cp.wait_recv()
            pltpu.make_async_copy(ins[i], outs[i].at[me], lsem.at[i]).wait()

    anyspec = pl.BlockSpec(memory_space=pl.ANY)
    return pl.pallas_call(
        body, in_specs=[anyspec] * n, out_specs=[anyspec] * n,
        out_shape=[jax.ShapeDtypeStruct((N_DEV,) + a.shape, a.dtype) for a in arrs],
        scratch_shapes=[pltpu.SemaphoreType.DMA((n, N_DEV - 1)), pltpu.SemaphoreType.DMA((n, N_DEV - 1)),
                        pltpu.SemaphoreType.DMA((n,))],
        name=name,
    )(*arrs)


def _exchange(sends, name):
    n = len(sends)

    def body(*refs):
        ins, outs = refs[:n], refs[n:2 * n]
        ssem, rsem, lsem = refs[2 * n:]
        me = _me()[3]
        for i in range(n):
            pltpu.make_async_copy(ins[i].at[me], outs[i].at[me], lsem.at[i]).start()
            for k in range(1, N_DEV):
                pid, pidx = _peer(k)
                pltpu.make_async_remote_copy(src_ref=ins[i].at[pidx], dst_ref=outs[i].at[me], send_sem=ssem.at[i, k - 1],
                                             recv_sem=rsem.at[i, k - 1], device_id=pid, device_id_type=MESH).start()
        for i in range(n):
            for k in range(1, N_DEV):
                pid, pidx = _peer(k)
                cp = pltpu.make_async_remote_copy(src_ref=ins[i].at[pidx], dst_ref=outs[i].at[pidx],
                                                  send_sem=ssem.at[i, k - 1], recv_sem=rsem.at[i, k - 1],
                                                  device_id=pid, device_id_type=MESH)
                cp.wait_send()
                cp.wait_recv()
            pltpu.make_async_copy(ins[i].at[me], outs[i].at[me], lsem.at[i]).wait()

    anyspec = pl.BlockSpec(memory_space=pl.ANY)
    return pl.pallas_call(
        body, in_specs=[anyspec] * n, out_specs=[anyspec] * n,
        out_shape=[jax.ShapeDtypeStruct(a.shape, a.dtype) for a in sends],
        scratch_shapes=[pltpu.SemaphoreType.DMA((n, N_DEV - 1)), pltpu.SemaphoreType.DMA((n, N_DEV - 1)),
                        pltpu.SemaphoreType.DMA((n,))],
        name=name,
    )(*sends)


def _block_diag(w):
    nb, bs = w.shape[0], w.shape[1]
    eye = jnp.eye(nb, dtype=w.dtype)
    return (eye[:, None, :, None] * w[:, :, None, :]).reshape(nb * bs, nb * bs)


def _diag_blocks(d):
    return jnp.stack([d[g, s * 64:(s + 1) * 64, s * 64:(s + 1) * 64] for g in range(4) for s in range(2)])


def _mlp_fwd(x, nw, wu, wd, tag):
    hm = _rms_fwd(x, nw, f"rms_mlp_{tag}")
    relu_ep = lambda acc: (jnp.maximum(acc, 0.0), jnp.square(jnp.maximum(acc, 0.0)))
    r, act = _matmul(hm, wu, "nn", outs=[bf16, bf16], epilogue=relu_ep, name=f"mlp_up_{tag}")
    (xo,) = _matmul(act, wd, "nn", outs=[f32], extras=(x,), epilogue=lambda acc, res: (res + acc,), name=f"mlp_down_{tag}")
    return xo, (hm, r, act)


def _mlp_bwd(x, nw, wu, wd, saved, dxo, dxo_b, tag):
    hm, r, act = saved
    (du,) = _matmul(dxo_b, wd, "nt", outs=[bf16], extras=(r,), epilogue=lambda acc, rr: (acc * (2.0 * rr.astype(f32)),),
                    name=f"mlp_dact_{tag}")
    (dwd,) = _matmul(act, dxo_b, "tn", outs=[bf16], name=f"mlp_dwd_{tag}")
    (dwu,) = _matmul(hm, du, "tn", outs=[bf16], shard_cols=True, name=f"mlp_dwu_{tag}")
    (dhm,) = _matmul(du, wu, "nt", outs=[f32], name=f"mlp_dh_{tag}")
    dx, dx_b, dnw = _rms_bwd(x, nw, dhm, dxo, f"rms_mlp_bwd_{tag}")
    return dx, dx_b, dnw, dwu, dwd.reshape(N_DEV, D_FF // N_DEV, D_MODEL)


def _local_step(x, tgt, W):
    T = x.shape[0]
    cos, sin = _rope_tables(T)
    rtab = _ret_tables()
    row = lambda a: a.reshape(1, -1)
    mix_nw, mlp_nw = W["mixer_norm_w"], W["mlp_norm_w"]
    wr_bd, wi_bd = _block_diag(W["lru_w_r"]), _block_diag(W["lru_w_i"])
    lru_b, lru_br, lru_bi, lru_lam = row(W["lru_conv_b"]), row(W["lru_b_r"]), row(W["lru_b_i"]), row(W["lru_lambda"])
    pad16 = lambda a: jnp.pad(a.reshape(1, GDN_HEADS), ((0, 0), (GDN_HEADS, 128 - 2 * GDN_HEADS)))
    alog, dtb = pad16(W["gdn_a_log"]), pad16(W["gdn_dt_bias"])
    gnw = row(W["gdn_norm_w"])

    x0 = x
    h0 = _rms_fwd(x0, mix_nw[0:1], "rms_mix_0")
    (pe,) = _matmul(h0, W["w_in_even"], "nn", outs=[f32], name="in_even")
    y_ret, o_ret, s_ret = _ret_fwd(pe, cos, sin, rtab, "ret_fwd")
    xc = _conv_fwd(pe, 16, W["lru_conv_w"], lru_b, False, "lru_conv_fwd")
    y_lru, h_lru = _lru_fwd(xc, pe, 20, wr_bd, wi_bd, lru_br, lru_bi, lru_lam, "lru_fwd")
    mix0 = jnp.concatenate([y_ret, y_lru], axis=1)
    (x1,) = _matmul(mix0, W["w_out_even"], "nn", outs=[f32], extras=(x0,), epilogue=lambda acc, res: (res + acc,), name="out_even")
    x2, mlp0 = _mlp_fwd(x1, mlp_nw[0:1], W["w_up0"], W["w_down0"], "0")
    h1 = _rms_fwd(x2, mix_nw[1:2], "rms_mix_1")
    (po,) = _matmul(h1, W["w_in_odd"], "nn", outs=[f32], tn=384, name="in_odd")
    qkv = _conv_fwd(po, 0, W["gdn_conv_w"], None, True, "gdn_conv_fwd")
    y_gdn, s_gdn = _gdn_fwd(qkv, po, alog, dtb, gnw, "gdn_fwd")
    (x3,) = _matmul(y_gdn, W["w_out_odd"], "nn", outs=[f32], extras=(x2,), epilogue=lambda acc, res: (res + acc,), name="out_odd")
    x4, mlp1 = _mlp_fwd(x3, mlp_nw[1:2], W["w_up1"], W["w_down1"], "1")
    loss, dx4, dx4_b, d_final = _loss_head(x4, row(W["final_norm_w"]), tgt, "loss_head")
    dx3, dx3_b, d_mlp_nw1, d_wu1, d_wd1 = _mlp_bwd(x3, mlp_nw[1:2], W["w_up1"], W["w_down1"], mlp1, dx4, dx4_b, "1")
    (dy_gdn,) = _matmul(dx3_b, W["w_out_odd"], "nt", outs=[f32], name="out_odd_dx")
    (d_woo,) = _matmul(y_gdn, dx3_b, "tn", outs=[bf16], name="out_odd_dw")
    dq, dk, dv, dz, dba, d_alog, d_dtb, d_gnw = _gdn_bwd(qkv, po, alog, dtb, gnw, s_gdn, dy_gdn, "gdn_bwd")
    dqkv = jnp.concatenate([dq, dk, dv], axis=1)
    dqkv_pre, d_gconv, _ = _conv_bwd(po, 0, W["gdn_conv_w"], None, True, dqkv, "gdn_conv_bwd")
    dpo = jnp.concatenate([dqkv_pre, dz, dba], axis=1).astype(bf16)
    (d_wio,) = _matmul(h1, dpo, "tn", outs=[bf16], tn=384, name="in_odd_dw")
    (dh1,) = _matmul(dpo, W["w_in_odd"], "nt", outs=[f32], tm=512, tk=ODD_PAD, name="in_odd_dx")
    dx2, dx2_b, d_mix_nw1 = _rms_bwd(x2, mix_nw[1:2], dh1, dx3, "rms_mix_bwd_1")
    dx1, dx1_b, d_mlp_nw0, d_wu0, d_wd0 = _mlp_bwd(x1, mlp_nw[0:1], W["w_up0"], W["w_down0"], mlp0, dx2, dx2_b, "0")
    (dmix0,) = _matmul(dx1_b, W["w_out_even"], "nt", outs=[f32], name="out_even_dx")
    (d_woe,) = _matmul(mix0, dx1_b, "tn", outs=[bf16], name="out_even_dw")
    dq_r, dk_r, dv_r, dg_r = _ret_bwd(pe, cos, sin, rtab, o_ret, s_ret, dmix0, "ret_bwd")
    dxc, dy_l, d_wr, d_wi, d_br, d_bi, d_lam = _lru_bwd(xc, pe, 20, wr_bd, wi_bd, lru_br, lru_bi, lru_lam, h_lru, dmix0, 4, "lru_bwd")
    dx_lru, d_lconv, d_lconv_b = _conv_bwd(pe, 16, W["lru_conv_w"], lru_b, False, dxc, "lru_conv_bwd")
    dpe = jnp.concatenate([dq_r, dk_r, dv_r, dg_r, dx_lru, dy_l], axis=1).astype(bf16)
    (d_wie,) = _matmul(h0, dpe, "tn", outs=[bf16], shard_cols=True, name="in_even_dw")
    (dh0,) = _matmul(dpe, W["w_in_even"], "nt", outs=[f32], name="in_even_dx")
    dx0, _, d_mix_nw0 = _rms_bwd(x0, mix_nw[0:1], dh0, dx1, "rms_mix_bwd_0")

    n_odd = ODD_IN // N_DEV
    G = dict(
        mixer_norm_w=jnp.concatenate([d_mix_nw0, d_mix_nw1], axis=0),
        mlp_norm_w=jnp.concatenate([d_mlp_nw0, d_mlp_nw1], axis=0),
        final_norm_w=d_final.reshape(-1),
        w_in_even=d_wie, lru_conv_w=d_lconv, lru_conv_b=d_lconv_b.reshape(-1),
        lru_w_r=_diag_blocks(d_wr), lru_b_r=d_br.reshape(-1), lru_w_i=_diag_blocks(d_wi), lru_b_i=d_bi.reshape(-1),
        lru_lambda=d_lam.reshape(-1), w_out_even=d_woe.reshape(N_DEV, D_MODEL // N_DEV, D_MODEL),
        w_in_odd=jnp.transpose(d_wio[:, :ODD_IN].reshape(D_MODEL, N_DEV, n_odd), (1, 0, 2)), gdn_conv_w=d_gconv,
        gdn_a_log=d_alog[0, GDN_HEADS:2 * GDN_HEADS], gdn_dt_bias=d_dtb[0, GDN_HEADS:2 * GDN_HEADS],
        gdn_norm_w=d_gnw.reshape(-1), w_out_odd=d_woo.reshape(N_DEV, D_MODEL // N_DEV, D_MODEL),
        w_up0=d_wu0, w_up1=d_wu1, w_down0=d_wd0, w_down1=d_wd1,
    )
    return loss, dx0, G


_SMALL = ["mixer_norm_w", "mlp_norm_w", "final_norm_w", "lru_conv_b", "lru_w_r", "lru_b_r", "lru_w_i", "lru_b_i",
          "lru_lambda", "gdn_a_log", "gdn_dt_bias", "gdn_norm_w"]
_PACK_ROWS = 688


def _pack(parts):
    flat = jnp.concatenate([p.reshape(-1) for p in parts])
    return jnp.pad(flat, (0, _PACK_ROWS * 128 - flat.shape[0])).reshape(_PACK_ROWS, 128)


def _unpack(packed, shapes):
    flat = packed.reshape(-1)
    out, off = [], 0
    for s in shapes:
        n = int(np.prod(s))
        out.append(flat[off:off + n].reshape(s))
        off += n
    return out


def kernel(x, mixer_norm_w, mlp_norm_w, final_norm_w, w_in_even, lru_conv_w, lru_conv_b, lru_w_r, lru_b_r, lru_w_i, lru_b_i, lru_lambda, w_out_even, w_in_odd, gdn_conv_w, gdn_a_log, gdn_dt_bias, gdn_norm_w, w_out_odd, w_up, w_down, loss_target, m_mixer_norm_w, m_mlp_norm_w, m_final_norm_w, m_w_in_even, m_lru_conv_w, m_lru_conv_b, m_lru_w_r, m_lru_b_r, m_lru_w_i, m_lru_b_i, m_lru_lambda, m_w_out_even, m_w_in_odd, m_gdn_conv_w, m_gdn_a_log, m_gdn_dt_bias, m_gdn_norm_w, m_w_out_odd, m_w_up, m_w_down, v_mixer_norm_w, v_mlp_norm_w, v_final_norm_w, v_w_in_even, v_lru_conv_w, v_lru_conv_b, v_lru_w_r, v_lru_b_r, v_lru_w_i, v_lru_b_i, v_lru_lambda, v_w_out_even, v_w_in_odd, v_gdn_conv_w, v_gdn_a_log, v_gdn_dt_bias, v_gdn_norm_w, v_w_out_odd, v_w_up, v_w_down):
    Pw = dict(mixer_norm_w=mixer_norm_w, mlp_norm_w=mlp_norm_w, final_norm_w=final_norm_w, w_in_even=w_in_even,
              lru_conv_w=lru_conv_w, lru_conv_b=lru_conv_b, lru_w_r=lru_w_r, lru_b_r=lru_b_r, lru_w_i=lru_w_i,
              lru_b_i=lru_b_i, lru_lambda=lru_lambda, w_out_even=w_out_even, w_in_odd=w_in_odd, gdn_conv_w=gdn_conv_w,
              gdn_a_log=gdn_a_log, gdn_dt_bias=gdn_dt_bias, gdn_norm_w=gdn_norm_w, w_out_odd=w_out_odd, w_up=w_up,
              w_down=w_down)
    Pm = dict(mixer_norm_w=m_mixer_norm_w, mlp_norm_w=m_mlp_norm_w, final_norm_w=m_final_norm_w, w_in_even=m_w_in_even,
              lru_conv_w=m_lru_conv_w, lru_conv_b=m_lru_conv_b, lru_w_r=m_lru_w_r, lru_b_r=m_lru_b_r, lru_w_i=m_lru_w_i,
              lru_b_i=m_lru_b_i, lru_lambda=m_lru_lambda, w_out_even=m_w_out_even, w_in_odd=m_w_in_odd,
              gdn_conv_w=m_gdn_conv_w, gdn_a_log=m_gdn_a_log, gdn_dt_bias=m_gdn_dt_bias, gdn_norm_w=m_gdn_norm_w,
              w_out_odd=m_w_out_odd, w_up=m_w_up, w_down=m_w_down)
    Pv = dict(mixer_norm_w=v_mixer_norm_w, mlp_norm_w=v_mlp_norm_w, final_norm_w=v_final_norm_w, w_in_even=v_w_in_even,
              lru_conv_w=v_lru_conv_w, lru_conv_b=v_lru_conv_b, lru_w_r=v_lru_w_r, lru_b_r=v_lru_b_r, lru_w_i=v_lru_w_i,
              lru_b_i=v_lru_b_i, lru_lambda=v_lru_lambda, w_out_even=v_w_out_even, w_in_odd=v_w_in_odd,
              gdn_conv_w=v_gdn_conv_w, gdn_a_log=v_gdn_a_log, gdn_dt_bias=v_gdn_dt_bias, gdn_norm_w=v_gdn_norm_w,
              w_out_odd=v_w_out_odd, w_up=v_w_up, w_down=v_w_down)
    me = _me()[3]
    T = x.shape[1]

    shards = [w_in_even[0].astype(bf16), w_out_even[0].astype(bf16), w_in_odd[0].astype(bf16), w_out_odd[0].astype(bf16),
              w_up.reshape(2 * D_MODEL, D_FF // N_DEV).astype(bf16), w_down.reshape(2 * D_FF // N_DEV, D_MODEL).astype(bf16),
              lru_conv_w[0], gdn_conv_w[0]]
    g_wie, g_woe, g_wio, g_woo, g_wu, g_wd, g_lc, g_gc = _all_gather(shards, "gather_weights")
    cols = lambda g: jnp.transpose(g, (1, 0, 2)).reshape(g.shape[1], -1)
    wu = jnp.transpose(g_wu.reshape(N_DEV, 2, D_MODEL, D_FF // N_DEV), (1, 2, 0, 3)).reshape(2, D_MODEL, D_FF)
    wd = jnp.transpose(g_wd.reshape(N_DEV, 2, D_FF // N_DEV, D_MODEL), (1, 0, 2, 3)).reshape(2, D_FF, D_MODEL)
    W = {k: Pw[k] for k in _SMALL}
    W["lru_w_r"], W["lru_w_i"] = lru_w_r[0], lru_w_i[0]
    for k in ("lru_conv_b", "lru_b_r", "lru_b_i", "lru_lambda", "gdn_a_log", "gdn_dt_bias", "gdn_norm_w"):
        W[k] = Pw[k][0]
    W.update(w_in_even=cols(g_wie), w_out_even=g_woe.reshape(D_MODEL, D_MODEL),
             w_in_odd=jnp.pad(cols(g_wio), ((0, 0), (0, ODD_PAD - ODD_IN))), w_out_odd=g_woo.reshape(D_MODEL, D_MODEL),
             w_up0=wu[0], w_up1=wu[1], w_down0=wd[0], w_down1=wd[1], lru_conv_w=cols(g_lc), gdn_conv_w=cols(g_gc))

    loss, dx, G = _local_step(x[0], loss_target[0], W)

    nff = D_FF // N_DEV
    s_wu = jnp.stack([G["w_up0"], G["w_up1"]], axis=1).reshape(N_DEV, 2 * D_MODEL, nff)
    s_wd = jnp.stack([G["w_down0"], G["w_down1"]], axis=1).reshape(N_DEV, 2 * nff, D_MODEL)
    small_g = [G[k].reshape(Pw[k].shape) for k in _SMALL] + [G["lru_conv_w"], G["gdn_conv_w"]]
    packed = _pack(small_g)
    sends = [G["w_in_even"], G["w_out_even"], G["w_in_odd"], G["w_out_odd"], s_wu, s_wd,
             jnp.broadcast_to(packed[None], (N_DEV,) + packed.shape)]
    r_wie, r_woe, r_wio, r_woo, r_wu, r_wd, r_small = _exchange(sends, "exchange_grads")

    out = {}

    def upd(name, gs, shape2d):
        g, d, mn, vn = _adamw(Pw[name].reshape(shape2d), gs, Pm[name].reshape(shape2d), Pv[name].reshape(shape2d), f"adamw_{name}")
        sh = Pw[name].shape
        out[name] = (g.reshape(sh), d.reshape(sh), mn.reshape(sh), vn.reshape(sh))

    upd("w_in_even", r_wie, (D_MODEL, 384))
    upd("w_out_even", r_woe, (128, D_MODEL))
    upd("w_in_odd", r_wio, (D_MODEL, 514))
    upd("w_out_odd", r_woo, (128, D_MODEL))
    upd("w_up", r_wu, (2 * D_MODEL, nff))
    upd("w_down", r_wd, (2 * nff, D_MODEL))
    small_shapes = [Pw[k].shape for k in _SMALL]
    pw, pm, pv = (_pack([P[k] for k in _SMALL]) for P in (Pw, Pm, Pv))
    sg, sd, sm, sv = _adamw(pw, r_small, pm, pv, "adamw_small")
    for arrs_i, packed_out in enumerate((sg, sd, sm, sv)):
        for k, a in zip(_SMALL, _unpack(packed_out, small_shapes)):
            out.setdefault(k, [None] * 4)[arrs_i] = a
    n_small = sum(int(np.prod(s)) for s in small_shapes)
    gflat = sg.reshape(-1)
    g_lconv = gflat[n_small:n_small + CONV_K * LRU_WIDTH].reshape(CONV_K, LRU_WIDTH)
    g_gconv = gflat[n_small + CONV_K * LRU_WIDTH:n_small + CONV_K * (LRU_WIDTH + 3072)].reshape(CONV_K, 3072)
    upd("lru_conv_w", lax.dynamic_slice_in_dim(g_lconv, me * 64, 64, axis=1)[None], (CONV_K, 64))
    upd("gdn_conv_w", lax.dynamic_slice_in_dim(g_gconv, me * 384, 384, axis=1)[None], (CONV_K, 384))

    names = ["mixer_norm_w", "mlp_norm_w", "final_norm_w", "w_in_even", "lru_conv_w", "lru_conv_b", "lru_w_r", "lru_b_r",
             "lru_w_i", "lru_b_i", "lru_lambda", "w_out_even", "w_in_odd", "gdn_conv_w", "gdn_a_log", "gdn_dt_bias",
             "gdn_norm_w", "w_out_odd", "w_up", "w_down"]
    total = lax.psum(loss[0, 0], ("x", "y", "c"))
    res = [total, dx[None]]
    for j in range(4):
        res += [out[k][j] for k in names]
    return tuple(res)
```

```python
import math

import numpy as np
import jax
import jax.numpy as jnp
from jax import lax
from jax.experimental import pallas as pl
from jax.experimental.pallas import tpu as pltpu

f32 = jnp.float32
bf16 = jnp.bfloat16

N_DEV = 8
D_MODEL = 1024
D_FF = 4096
EPS = 1e-6
RET_HEADS = 4
RET_CHUNK = 128
ROPE_THETA = 10000.0
LRU_WIDTH = 512
LRU_C = 8.0
GDN_HEADS = 8
GDN_CHUNK = 64
HEAD_DIM = 128
ODD_IN = 4112
ODD_PAD = 4224
ADAM_LR, ADAM_B1, ADAM_B2, ADAM_EPS, ADAM_WD, ADAM_STEP = 0.001, 0.9, 0.999, 1e-08, 0.01, 10
VMEM_LIMIT = 56 * 1024 * 1024

_NN = (((1,), (0,)), ((), ()))
_NT = (((1,), (1,)), ((), ()))
_TN = (((0,), (0,)), ((), ()))
MESH = pl.DeviceIdType.MESH


def _cparams(sem):
    return pltpu.CompilerParams(dimension_semantics=sem, vmem_limit_bytes=VMEM_LIMIT)


def _dot(a, b, dn):
    return lax.dot_general(a.astype(bf16), b.astype(bf16), dn, preferred_element_type=f32)


def _split2(a):
    hi = a.astype(bf16)
    lo = (a - hi.astype(f32)).astype(bf16)
    return hi, lo


def _dot3(a, b, dn):
    ah, al = _split2(a)
    bh, bl = _split2(b)
    d = lambda p, q: lax.dot_general(p, q, dn, preferred_element_type=f32)
    return d(ah, bh) + (d(ah, bl) + d(al, bh))


def _dot01(a01, b, dn):
    a = a01.astype(bf16)
    b0 = b.astype(bf16)
    r1 = b - b0.astype(f32)
    b1 = r1.astype(bf16)
    b2 = (r1 - b1.astype(f32)).astype(bf16)
    d = lambda q: lax.dot_general(a, q, dn, preferred_element_type=f32)
    return d(b0) + (d(b1) + d(b2))


def _sigmoid(x):
    return jax.nn.sigmoid(x)


def _silu(x):
    return x * _sigmoid(x)


def _dsilu(x):
    s = _sigmoid(x)
    return s * (1.0 + x * (1.0 - s))


def _softplus(x):
    return jnp.maximum(x, 0.0) + jnp.log1p(jnp.exp(-jnp.abs(x)))


_GELU_C = math.sqrt(2.0 / math.pi)


def _gelu(y):
    return 0.5 * y * (1.0 + jnp.tanh(_GELU_C * (y + 0.044715 * y * y * y)))


def _dgelu(y):
    t = jnp.tanh(_GELU_C * (y + 0.044715 * y * y * y))
    return 0.5 * (1.0 + t) + 0.5 * y * (1.0 - t * t) * _GELU_C * (1.0 + 3.0 * 0.044715 * y * y)


def _matmul(a, b, form, *, outs, name, epilogue=None, extras=(), tm=1024, tn=512, tk=1024, shard_cols=False):
    if form == "tn":
        K, M = a.shape
    else:
        M, K = a.shape
    N = b.shape[0] if form == "nt" else b.shape[1]
    if shard_cols:
        tn = N // N_DEV
    tm, tn, tk = min(tm, M), min(tn, N), min(tk, K)
    assert M % tm == 0 and N % tn == 0 and K % tk == 0, (name, M, N, K, tm, tn, tk)
    nk = K // tk
    dn = {"nn": _NN, "nt": _NT, "tn": _TN}[form]
    if form == "tn":
        a_spec = pl.BlockSpec((tk, tm), lambda i, j, k: (k, i))
    else:
        a_spec = pl.BlockSpec((tm, tk), lambda i, j, k: (i, k))
    if form == "nt":
        b_spec = pl.BlockSpec((tn, tk), lambda i, j, k: (j, k))
    else:
        b_spec = pl.BlockSpec((tk, tn), lambda i, j, k: (k, j))
    e_spec = pl.BlockSpec((tm, tn), lambda i, j, k: (i, j))
    if shard_cols:
        o_spec = pl.BlockSpec((None, tm, tn), lambda i, j, k: (j, i, 0))
        o_shape = (N_DEV, M, tn)
    else:
        o_spec = e_spec
        o_shape = (M, N)
    n_ex = len(extras)

    def finish(acc, ex, o_refs):
        vals = (acc,) if epilogue is None else epilogue(acc, *[e[...] for e in ex])
        for r, v in zip(o_refs, vals):
            r[...] = v.astype(r.dtype)

    def body_one(*refs):
        a_ref, b_ref = refs[:2]
        finish(_dot(a_ref[...], b_ref[...], dn), refs[2:2 + n_ex], refs[2 + n_ex:])

    def body_acc(*refs):
        a_ref, b_ref = refs[:2]
        acc = refs[-1]
        k = pl.program_id(2)

        @pl.when(k == 0)
        def _():
            acc[...] = _dot(a_ref[...], b_ref[...], dn)

        @pl.when((k > 0) & (k < nk - 1))
        def _():
            acc[...] += _dot(a_ref[...], b_ref[...], dn)

        @pl.when(k == nk - 1)
        def _():
            finish(acc[...] + _dot(a_ref[...], b_ref[...], dn), refs[2:2 + n_ex], refs[2 + n_ex:-1])

    return pl.pallas_call(
        body_one if nk == 1 else body_acc, grid=(M // tm, N // tn, nk),
        in_specs=[a_spec, b_spec] + [e_spec] * n_ex,
        out_specs=[o_spec] * len(outs),
        out_shape=[jax.ShapeDtypeStruct(o_shape, d) for d in outs],
        scratch_shapes=[] if nk == 1 else [pltpu.VMEM((tm, tn), f32)],
        compiler_params=_cparams(("parallel", "parallel", "arbitrary")), name=name,
    )(a, b, *extras)


def _rms_fwd(x, w, name):
    T, D = x.shape
    tt = min(512, T)

    def body(x_ref, w_ref, h_ref):
        xv = x_ref[...]
        r = lax.rsqrt(jnp.mean(xv * xv, axis=1, keepdims=True) + EPS)
        h_ref[...] = (xv * r * w_ref[...]).astype(bf16)

    return pl.pallas_call(
        body, grid=(T // tt,),
        in_specs=[pl.BlockSpec((tt, D), lambda i: (i, 0)), pl.BlockSpec((1, D), lambda i: (0, 0))],
        out_specs=pl.BlockSpec((tt, D), lambda i: (i, 0)),
        out_shape=jax.ShapeDtypeStruct((T, D), bf16),
        compiler_params=_cparams(("parallel",)), name=name,
    )(x, w)


def _rms_bwd(x, w, dh, dres, name):
    T, D = x.shape
    tt = min(512, T)

    def body(x_ref, w_ref, dh_ref, dres_ref, dx_ref, dxb_ref, dw_ref):
        @pl.when(pl.program_id(0) == 0)
        def _():
            dw_ref[...] = jnp.zeros_like(dw_ref)

        xv = x_ref[...]
        r = lax.rsqrt(jnp.mean(xv * xv, axis=1, keepdims=True) + EPS)
        xn = xv * r
        dhv = dh_ref[...]
        dhw = dhv * w_ref[...]
        dx = dres_ref[...] + r * (dhw - xn * jnp.mean(dhw * xn, axis=1, keepdims=True))
        dx_ref[...] = dx
        dxb_ref[...] = dx.astype(bf16)
        dw_ref[...] += jnp.sum(dhv * xn, axis=0, keepdims=True)

    row = pl.BlockSpec((tt, D), lambda i: (i, 0))
    vec = pl.BlockSpec((1, D), lambda i: (0, 0))
    return pl.pallas_call(
        body, grid=(T // tt,), in_specs=[row, vec, row, row], out_specs=[row, row, vec],
        out_shape=[jax.ShapeDtypeStruct((T, D), f32), jax.ShapeDtypeStruct((T, D), bf16), jax.ShapeDtypeStruct((1, D), f32)],
        compiler_params=_cparams(("arbitrary",)), name=name,
    )(x, w, dh, dres)


def _loss_head(x, w, tgt, name):
    T, D = x.shape
    tt = min(512, T)

    def body(x_ref, w_ref, t_ref, l_ref, dx_ref, dxb_ref, dw_ref):
        @pl.when(pl.program_id(0) == 0)
        def _():
            dw_ref[...] = jnp.zeros_like(dw_ref)
            l_ref[...] = jnp.zeros_like(l_ref)

        xv = x_ref[...]
        wv = w_ref[...]
        r = lax.rsqrt(jnp.mean(xv * xv, axis=1, keepdims=True) + EPS)
        xn = xv * r
        e = xn * wv - t_ref[...]
        l_ref[...] += 0.5 * jnp.sum(jnp.mean(e * e, axis=1, keepdims=True), axis=0, keepdims=True)
        dy = e * (1.0 / D)
        dyw = dy * wv
        dx = r * (dyw - xn * jnp.mean(dyw * xn, axis=1, keepdims=True))
        dx_ref[...] = dx
        dxb_ref[...] = dx.astype(bf16)
        dw_ref[...] += jnp.sum(dy * xn, axis=0, keepdims=True)

    row = pl.BlockSpec((tt, D), lambda i: (i, 0))
    vec = pl.BlockSpec((1, D), lambda i: (0, 0))
    return pl.pallas_call(
        body, grid=(T // tt,), in_specs=[row, vec, row],
        out_specs=[pl.BlockSpec((1, 128), lambda i: (0, 0)), row, row, vec],
        out_shape=[jax.ShapeDtypeStruct((1, 128), f32), jax.ShapeDtypeStruct((T, D), f32),
                   jax.ShapeDtypeStruct((T, D), bf16), jax.ShapeDtypeStruct((1, D), f32)],
        compiler_params=_cparams(("arbitrary",)), name=name,
    )(x, w, tgt)


def _ret_tables():
    H, C = RET_HEADS, RET_CHUNK
    lg = np.log1p(-np.exp2(-5.0 - np.arange(H, dtype=np.float32))).astype(np.float32)
    idx = np.arange(C, dtype=np.float32)
    diff = idx[:, None] - idx[None, :]
    causal = diff >= 0
    dm = np.where(causal[None], np.exp(lg[:, None, None] * np.where(causal, diff, 0.0)[None]), 0.0)
    qd = np.exp(lg[:, None] * (idx[None, :] + 1.0))
    kd = np.exp(lg[:, None] * (C - 1.0 - idx[None, :]))
    cg = np.exp(lg * C)
    tab = np.zeros((H, 4, C, HEAD_DIM), np.float32)
    tab[:, 0] = dm
    tab[:, 1] = qd[:, :, None]
    tab[:, 2] = kd[:, :, None]
    tab[:, 3] = cg[:, None, None]
    return jnp.asarray(tab)


def _rope_tables(T):
    half = HEAD_DIM // 2
    inv = ROPE_THETA ** (-jnp.arange(half, dtype=f32) / half)
    ang = jnp.arange(T, dtype=jnp.int32).astype(f32)[:, None] * inv[None, :]
    c, s = jnp.cos(ang), jnp.sin(ang)
    return jnp.concatenate([c, c], axis=1), jnp.concatenate([-s, s], axis=1)


def _rope(x, cos, sin):
    return x * cos + pltpu.roll(x, HEAD_DIM // 2, 1) * sin


def _unrope(y, cos, sin):
    return y * cos + pltpu.roll(y * sin, HEAD_DIM // 2, 1)


def _ret_fwd(p, cos, sin, tab, name):
    T = p.shape[0]
    C, H = RET_CHUNK, RET_HEADS
    N = T // C
    scale = HEAD_DIM ** -0.5

    def body(q_ref, k_ref, v_ref, g_ref, c_ref, s_ref, t_ref, y_ref, o_ref, sp_ref, st):
        n, h = pl.program_id(0), pl.program_id(1)

        @pl.when(n == 0)
        def _():
            st[h] = jnp.zeros((HEAD_DIM, HEAD_DIM), f32)

        cos_, sin_ = c_ref[...], s_ref[...]
        q = _rope(q_ref[...], cos_, sin_)
        k = _rope(k_ref[...], cos_, sin_) * scale
        v = v_ref[...]
        dm, qd, kd, cg = t_ref[0, 0], t_ref[0, 1], t_ref[0, 2], t_ref[0, 3]
        S = st[h]
        P = _dot(q, k, _NT) * dm
        o = _dot(P, v, _NN) + _dot(q * qd, S, _NN)
        sp_ref[0, 0] = S
        st[h] = cg * S + _dot(k * kd, v, _TN)
        o_ref[...] = o
        r = lax.rsqrt(jnp.mean(o * o, axis=1, keepdims=True) + EPS)
        y_ref[...] = (o * r * _silu(g_ref[...])).astype(bf16)

    blk = lambda off: pl.BlockSpec((C, HEAD_DIM), lambda n, h: (n, off + h))
    tbl = pl.BlockSpec((C, HEAD_DIM), lambda n, h: (n, 0))
    return pl.pallas_call(
        body, grid=(N, H),
        in_specs=[blk(0), blk(H), blk(2 * H), blk(3 * H), tbl, tbl,
                  pl.BlockSpec((1, 4, C, HEAD_DIM), lambda n, h: (h, 0, 0, 0))],
        out_specs=[blk(0), blk(0), pl.BlockSpec((1, 1, HEAD_DIM, HEAD_DIM), lambda n, h: (n, h, 0, 0))],
        out_shape=[jax.ShapeDtypeStruct((T, H * HEAD_DIM), bf16), jax.ShapeDtypeStruct((T, H * HEAD_DIM), f32),
                   jax.ShapeDtypeStruct((N, H, HEAD_DIM, HEAD_DIM), f32)],
        scratch_shapes=[pltpu.VMEM((H, HEAD_DIM, HEAD_DIM), f32)],
        compiler_params=_cparams(("arbitrary", "arbitrary")), name=name,
    )(p, p, p, p, cos, sin, tab)


def _ret_bwd(p, cos, sin, tab, o_raw, sprev, dmix, name):
    T = p.shape[0]
    C, H = RET_CHUNK, RET_HEADS
    N = T // C
    scale = HEAD_DIM ** -0.5

    def body(q_ref, k_ref, v_ref, g_ref, c_ref, s_ref, t_ref, o_ref, sp_ref, dy_ref,
             dq_ref, dk_ref, dv_ref, dg_ref, dst):
        n, h = pl.program_id(0), pl.program_id(1)

        @pl.when(n == 0)
        def _():
            dst[h] = jnp.zeros((HEAD_DIM, HEAD_DIM), f32)

        cos_, sin_ = c_ref[...], s_ref[...]
        q = _rope(q_ref[...], cos_, sin_)
        k = _rope(k_ref[...], cos_, sin_) * scale
        v = v_ref[...]
        g = g_ref[...]
        dm, qd, kd, cg = t_ref[0, 0], t_ref[0, 1], t_ref[0, 2], t_ref[0, 3]
        S = sp_ref[0, 0]
        o = o_ref[...]
        dy = dy_ref[...]
        r = lax.rsqrt(jnp.mean(o * o, axis=1, keepdims=True) + EPS)
        nrm = o * r
        dn = dy * _silu(g)
        dg_ref[...] = dy * nrm * _dsilu(g)
        do = r * (dn - nrm * jnp.mean(dn * nrm, axis=1, keepdims=True))
        dS1 = dst[h]
        P = _dot(q, k, _NT) * dm
        dP = _dot(do, v, _NT) * dm
        dq = _dot(dP, k, _NN) + _dot(do, S, _NT) * qd
        dk = _dot(dP, q, _TN) + _dot(v, dS1, _NT) * kd
        dv_ref[...] = _dot(P, do, _TN) + _dot(k * kd, dS1, _NN)
        dst[h] = cg * dS1 + _dot(q * qd, do, _TN)
        dq_ref[...] = _unrope(dq, cos_, sin_)
        dk_ref[...] = _unrope(dk * scale, cos_, sin_)

    rev = lambda off: pl.BlockSpec((C, HEAD_DIM), lambda n, h: (N - 1 - n, off + h))
    tbl = pl.BlockSpec((C, HEAD_DIM), lambda n, h: (N - 1 - n, 0))
    out = jax.ShapeDtypeStruct((T, H * HEAD_DIM), f32)
    return pl.pallas_call(
        body, grid=(N, H),
        in_specs=[rev(0), rev(H), rev(2 * H), rev(3 * H), tbl, tbl,
                  pl.BlockSpec((1, 4, C, HEAD_DIM), lambda n, h: (h, 0, 0, 0)), rev(0),
                  pl.BlockSpec((1, 1, HEAD_DIM, HEAD_DIM), lambda n, h: (N - 1 - n, h, 0, 0)), rev(0)],
        out_specs=[rev(0)] * 4, out_shape=[out] * 4,
        scratch_shapes=[pltpu.VMEM((H, HEAD_DIM, HEAD_DIM), f32)],
        compiler_params=_cparams(("arbitrary", "arbitrary")), name=name,
    )(p, p, p, p, cos, sin, tab, o_raw, sprev, dmix)


CONV_K = 4
PAD = 8


def _conv_fwd(x, col_off, w, b, act, name):
    T = x.shape[0]
    C = w.shape[1]
    G = C // 128
    tt = min(512, T)
    NT = T // tt
    has_b = b is not None

    def body(*refs):
        if has_b:
            x_ref, w_ref, b_ref, y_ref, pad = refs
        else:
            x_ref, w_ref, y_ref, pad = refs
        t = pl.program_id(1)

        @pl.when(t == 0)
        def _():
            pad[pl.ds(0, PAD), :] = jnp.zeros((PAD, 128), f32)

        pad[pl.ds(PAD, tt), :] = x_ref[...]
        wv = w_ref[...]
        y = wv[0:1, :] * pad[pl.ds(PAD - 3, tt), :]
        for kk in range(1, CONV_K):
            y = y + wv[kk:kk + 1, :] * pad[pl.ds(PAD - 3 + kk, tt), :]
        if has_b:
            y = y + b_ref[...]
        tail = pad[pl.ds(tt, PAD), :]
        pad[pl.ds(0, PAD), :] = tail
        y_ref[...] = _silu(y) if act else y

    in_specs = [pl.BlockSpec((tt, 128), lambda g, t: (t, col_off + g)),
                pl.BlockSpec((CONV_K, 128), lambda g, t: (0, g))]
    args = [x, w]
    if has_b:
        in_specs.append(pl.BlockSpec((1, 128), lambda g, t: (0, g)))
        args.append(b)
    return pl.pallas_call(
        body, grid=(G, NT), in_specs=in_specs,
        out_specs=pl.BlockSpec((tt, 128), lambda g, t: (t, g)),
        out_shape=jax.ShapeDtypeStruct((T, C), f32),
        scratch_shapes=[pltpu.VMEM((tt + PAD, 128), f32)],
        compiler_params=_cparams(("parallel", "arbitrary")), name=name,
    )(*args)


def _conv_bwd(x, col_off, w, b, act, dout, name):
    T = x.shape[0]
    C = w.shape[1]
    G = C // 128
    tt = min(512, T)
    NT = T // tt
    has_b = b is not None

    def body(*refs):
        if has_b:
            x_ref, xp_ref, w_ref, b_ref, d_ref, dx_ref, dw_ref, db_ref, pad, dpad = refs
        else:
            x_ref, xp_ref, w_ref, d_ref, dx_ref, dw_ref, db_ref, pad, dpad = refs
        t = pl.program_id(1)
        first_tile = t == NT - 1

        @pl.when(t == 0)
        def _():
            dpad[pl.ds(tt, PAD), :] = jnp.zeros((PAD, 128), f32)
            dw_ref[...] = jnp.zeros_like(dw_ref)
            db_ref[...] = jnp.zeros_like(db_ref)

        pad[pl.ds(0, PAD), :] = jnp.where(first_tile, 0.0, xp_ref[...])
        pad[pl.ds(PAD, tt), :] = x_ref[...]
        wv = w_ref[...]
        dy = d_ref[...]
        if act:
            y = wv[0:1, :] * pad[pl.ds(PAD - 3, tt), :]
            for kk in range(1, CONV_K):
                y = y + wv[kk:kk + 1, :] * pad[pl.ds(PAD - 3 + kk, tt), :]
            if has_b:
                y = y + b_ref[...]
            dy = dy * _dsilu(y)
        dpad[pl.ds(0, tt), :] = dy
        dx = wv[3:4, :] * dy
        for j in range(1, CONV_K):
            dx = dx + wv[3 - j:4 - j, :] * dpad[pl.ds(j, tt), :]
        dx_ref[...] = dx
        head = dpad[pl.ds(0, PAD), :]
        dpad[pl.ds(tt, PAD), :] = head
        for kk in range(CONV_K):
            dw_ref[kk:kk + 1, :] += jnp.sum(dy * pad[pl.ds(PAD - 3 + kk, tt), :], axis=0, keepdims=True)
        db_ref[...] += jnp.sum(dy, axis=0, keepdims=True)

    rows8 = tt // PAD
    in_specs = [pl.BlockSpec((tt, 128), lambda g, t: (NT - 1 - t, col_off + g)),
                pl.BlockSpec((PAD, 128), lambda g, t: (jnp.maximum((NT - 1 - t) * rows8 - 1, 0), col_off + g)),
                pl.BlockSpec((CONV_K, 128), lambda g, t: (0, g))]
    args = [x, x, w]
    if has_b:
        in_specs.append(pl.BlockSpec((1, 128), lambda g, t: (0, g)))
        args.append(b)
    in_specs.append(pl.BlockSpec((tt, 128), lambda g, t: (NT - 1 - t, g)))
    args.append(dout)
    return pl.pallas_call(
        body, grid=(G, NT), in_specs=in_specs,
        out_specs=[pl.BlockSpec((tt, 128), lambda g, t: (NT - 1 - t, g)),
                   pl.BlockSpec((CONV_K, 128), lambda g, t: (0, g)),
                   pl.BlockSpec((1, 128), lambda g, t: (0, g))],
        out_shape=[jax.ShapeDtypeStruct((T, C), f32), jax.ShapeDtypeStruct((CONV_K, C), f32),
                   jax.ShapeDtypeStruct((1, C), f32)],
        scratch_shapes=[pltpu.VMEM((tt + PAD, 128), f32), pltpu.VMEM((tt + PAD, 128), f32)],
        compiler_params=_cparams(("parallel", "arbitrary")), name=name,
    )(*args)


def _lru_gates(xc, wr, wi, br, bi, lam):
    r = _sigmoid(_dot(xc, wr, _NN) + br)
    i = _sigmoid(_dot(xc, wi, _NN) + bi)
    sp = _softplus(-lam)
    a = jnp.exp(-LRU_C * r * sp)
    mult = jnp.sqrt(1.0 - a * a)
    return r, i, sp, a, mult


def _lru_fwd(xc, p, y_off, wr, wi, br, bi, lam, name):
    T = xc.shape[0]
    G = LRU_WIDTH // 128
    tt = min(512, T)
    NT = T // tt

    def body(x_ref, y_ref, wr_ref, wi_ref, br_ref, bi_ref, l_ref, o_ref, h_ref, hc):
        t = pl.program_id(1)

        @pl.when(t == 0)
        def _():
            hc[...] = jnp.zeros_like(hc)

        x = x_ref[...]
        r, i, sp, a, mult = _lru_gates(x, wr_ref[...], wi_ref[...], br_ref[...], bi_ref[...], l_ref[...])
        row = lax.broadcasted_iota(jnp.int32, (tt, 128), 0)
        mult = jnp.where((row == 0) & (t == 0), 1.0, mult)
        U = x * i * mult
        A = a
        d = 1
        while d < tt:
            keep = row >= d
            Ush = jnp.where(keep, pltpu.roll(U, d, 0), 0.0)
            Ash = jnp.where(keep, pltpu.roll(A, d, 0), 1.0)
            U = A * Ush + U
            A = A * Ash
            d *= 2
        h = U + A * hc[0:1, :]
        h_ref[...] = h
        hc[...] = jnp.broadcast_to(h[tt - 1:tt, :], hc.shape)
        o_ref[...] = (h * _gelu(y_ref[...])).astype(bf16)

    tile = pl.BlockSpec((tt, 128), lambda g, t: (t, g))
    vec = pl.BlockSpec((1, 128), lambda g, t: (0, g))
    wsp = pl.BlockSpec((128, 128), lambda g, t: (g, g))
    return pl.pallas_call(
        body, grid=(G, NT),
        in_specs=[tile, pl.BlockSpec((tt, 128), lambda g, t: (t, y_off + g)), wsp, wsp, vec, vec, vec],
        out_specs=[tile, tile],
        out_shape=[jax.ShapeDtypeStruct((T, LRU_WIDTH), bf16), jax.ShapeDtypeStruct((T, LRU_WIDTH), f32)],
        scratch_shapes=[pltpu.VMEM((8, 128), f32)],
        compiler_params=_cparams(("parallel", "arbitrary")), name=name,
    )(xc, p, wr, wi, br, bi, lam)


def _lru_bwd(xc, p, y_off, wr, wi, br, bi, lam, hs, dmix, d_off, name):
    T = xc.shape[0]
    G = LRU_WIDTH // 128
    tt = min(512, T)
    NT = T // tt

    def body(x_ref, y_ref, wr_ref, wi_ref, br_ref, bi_ref, l_ref, h_ref, hp_ref, do_ref,
             dx_ref, dy_ref, dwr_ref, dwi_ref, dbr_ref, dbi_ref, dl_ref, lc, an):
        t = pl.program_id(1)
        first_tile = t == NT - 1

        @pl.when(t == 0)
        def _():
            lc[...] = jnp.zeros_like(lc)
            an[...] = jnp.zeros_like(an)
            dwr_ref[...] = jnp.zeros_like(dwr_ref)
            dwi_ref[...] = jnp.zeros_like(dwi_ref)
            dbr_ref[...] = jnp.zeros_like(dbr_ref)
            dbi_ref[...] = jnp.zeros_like(dbi_ref)
            dl_ref[...] = jnp.zeros_like(dl_ref)

        x = x_ref[...]
        y = y_ref[...]
        wr, wi, lam_ = wr_ref[...], wi_ref[...], l_ref[...]
        r, i, sp, a, mult_raw = _lru_gates(x, wr, wi, br_ref[...], bi_ref[...], lam_)
        row = lax.broadcasted_iota(jnp.int32, (tt, 128), 0)
        t0 = (row == 0) & first_tile
        mult = jnp.where(t0, 1.0, mult_raw)
        h = h_ref[...]
        do = do_ref[...]
        dh = do * _gelu(y)
        dy_ref[...] = do * h * _dgelu(y)
        B = jnp.where(row == tt - 1, an[0:1, :], pltpu.roll(a, tt - 1, 0))
        L = dh
        d = 1
        while d < tt:
            keep = row < tt - d
            Lsh = jnp.where(keep, pltpu.roll(L, tt - d, 0), 0.0)
            Bsh = jnp.where(keep, pltpu.roll(B, tt - d, 0), 1.0)
            L = L + B * Lsh
            B = B * Bsh
            d *= 2
        L = L + B * lc[0:1, :]
        lc[...] = jnp.broadcast_to(L[0:1, :], lc.shape)
        an[...] = jnp.broadcast_to(a[0:1, :], an.shape)
        hprev = jnp.where(first_tile, 0.0, hp_ref[...])[PAD - 1:PAD, :]
        hm1 = jnp.where(row == 0, hprev, pltpu.roll(h, 1, 0))
        da = L * hm1
        dxc = L * i * mult
        di = L * x * mult
        dmult = jnp.where(t0, 0.0, L * x * i)
        da = da - jnp.where(t0, 0.0, dmult * a / mult_raw)
        dlog_a = da * a
        dr = dlog_a * (-LRU_C) * sp
        dsp = jnp.sum(dlog_a * (-LRU_C) * r, axis=0, keepdims=True)
        dpr = dr * r * (1.0 - r)
        dpi = di * i * (1.0 - i)
        dx_ref[...] = dxc + _dot(dpr, wr, _NT) + _dot(dpi, wi, _NT)
        dwr_ref[0] += _dot(x, dpr, _TN)
        dwi_ref[0] += _dot(x, dpi, _TN)
        dbr_ref[...] += jnp.sum(dpr, axis=0, keepdims=True)
        dbi_ref[...] += jnp.sum(dpi, axis=0, keepdims=True)
        dl_ref[...] += dsp * (-_sigmoid(-lam_))

    rows8 = tt // PAD
    tile = pl.BlockSpec((tt, 128), lambda g, t: (NT - 1 - t, g))
    vec = pl.BlockSpec((1, 128), lambda g, t: (0, g))
    wsp = pl.BlockSpec((128, 128), lambda g, t: (g, g))
    wout = pl.BlockSpec((1, 128, 128), lambda g, t: (g, 0, 0))
    return pl.pallas_call(
        body, grid=(G, NT),
        in_specs=[tile, pl.BlockSpec((tt, 128), lambda g, t: (NT - 1 - t, y_off + g)), wsp, wsp, vec, vec, vec, tile,
                  pl.BlockSpec((PAD, 128), lambda g, t: (jnp.maximum((NT - 1 - t) * rows8 - 1, 0), g)),
                  pl.BlockSpec((tt, 128), lambda g, t: (NT - 1 - t, d_off + g))],
        out_specs=[tile, tile, wout, wout, vec, vec, vec],
        out_shape=[jax.ShapeDtypeStruct((T, LRU_WIDTH), f32), jax.ShapeDtypeStruct((T, LRU_WIDTH), f32),
                   jax.ShapeDtypeStruct((G, 128, 128), f32), jax.ShapeDtypeStruct((G, 128, 128), f32),
                   jax.ShapeDtypeStruct((1, LRU_WIDTH), f32), jax.ShapeDtypeStruct((1, LRU_WIDTH), f32),
                   jax.ShapeDtypeStruct((1, LRU_WIDTH), f32)],
        scratch_shapes=[pltpu.VMEM((8, 128), f32), pltpu.VMEM((8, 128), f32)],
        compiler_params=_cparams(("parallel", "arbitrary")), name=name,
    )(xc, p, wr, wi, br, bi, lam, hs, hs, dmix)


_NN3 = (((2,), (1,)), ((0,), (0,)))
_NT3 = (((2,), (2,)), ((0,), (0,)))
_TN3 = (((1,), (1,)), ((0,), (0,)))


def _heads(ref):
    return jnp.stack([ref[:, h * HEAD_DIM:(h + 1) * HEAD_DIM] for h in range(GDN_HEADS)])


def _gdn_chunk(qr, kr, v, ba, alog, dtb, S):
    C, H = GDN_CHUNK, GDN_HEADS
    lane = lax.broadcasted_iota(jnp.int32, (C, 128), 1)
    lane3 = lax.broadcasted_iota(jnp.int32, (H, C, 128), 2)
    ri = lax.broadcasted_iota(jnp.int32, (C, C), 0)
    ci = lax.broadcasted_iota(jnp.int32, (C, C), 1)
    rowc = lax.broadcasted_iota(jnp.int32, (C, 1), 0)
    col = lambda m, j: jnp.sum(jnp.where(lane == j, m, 0.0), axis=1, keepdims=True)
    cols = lambda m, off: jnp.stack([col(m, off + h) for h in range(H)])
    ea = jnp.exp(alog)
    g_all = -ea * _softplus(ba + dtb)
    tri = (ri >= ci).astype(f32)
    G_all = _dot01(tri, g_all, _NN)
    beta = cols(_sigmoid(ba), 0)
    Gc = cols(G_all, H)
    rq = lax.rsqrt(jnp.sum(qr * qr, axis=2, keepdims=True) + EPS)
    rk = lax.rsqrt(jnp.sum(kr * kr, axis=2, keepdims=True) + EPS)
    qh, kn = qr * rq, kr * rk
    qn = qh * (HEAD_DIM ** -0.5)
    Grow = _dot01(jnp.ones((H, C, 128), f32), jnp.where(lane3 == 0, Gc, 0.0), _NT3)
    incl = ri >= ci
    Di = jnp.where(incl, jnp.exp(jnp.where(incl, Gc - Grow, 0.0)), 0.0)
    Ds = jnp.where(ri > ci, Di, 0.0)
    Gl = jnp.sum(jnp.where(rowc == C - 1, Gc, 0.0), axis=1, keepdims=True)
    eG = jnp.exp(Gc)
    eGl = jnp.exp(Gl - Gc)
    cd = jnp.exp(Gl)
    kb = kn * beta
    vb = v * beta
    Lm = _dot(kb, kn, _NT3) * Ds
    eye = (ri == ci).astype(f32)
    X = -Lm
    Tinv = eye + X
    Pw = X
    for _ in range(5):
        Pw = _dot3(Pw, Pw, _NN3)
        Tinv = Tinv + _dot3(Tinv, Pw, _NN3)
    kbg = kb * eG
    u = _dot3(Tinv, vb, _NN3)
    w = _dot3(Tinv, kbg, _NN3)
    QK = _dot(qn, kn, _NT3) * Di
    qg = qn * eG
    kg = kn * eGl
    vn = u - _dot(w, S, _NN3)
    o = _dot(qg, S, _NN3) + _dot(QK, vn, _NN3)
    S1 = S * cd + _dot(kg, vn, _TN3)
    return dict(beta=beta, g_all=g_all, rq=rq, rk=rk, qh=qh, kn=kn, qn=qn, Di=Di, Ds=Ds, eG=eG, eGl=eGl, cd=cd,
                kb=kb, vb=vb, Lm=Lm, Tinv=Tinv, kbg=kbg, w=w, QK=QK, qg=qg, kg=kg, vn=vn, o=o, S1=S1,
                lane=lane, ri=ri, ci=ci, rowc=rowc, ea=ea)


def _gdn_specs(N, rev):
    C = GDN_CHUNK
    H = GDN_HEADS
    nn = (lambda n: N - 1 - n) if rev else (lambda n: n)
    wide = lambda blk: pl.BlockSpec((C, H * HEAD_DIM), lambda n: (nn(n), blk))
    one = lambda off: pl.BlockSpec((C, HEAD_DIM), lambda n: (nn(n), off))
    vec = pl.BlockSpec((1, 128), lambda n: (0, 0))
    st = pl.BlockSpec((1, H, HEAD_DIM, HEAD_DIM), lambda n: (nn(n), 0, 0, 0))
    return wide, one, vec, st


def _gdn_fwd(qkv, p, alog, dtb, nw, name):
    T = qkv.shape[0]
    C, H = GDN_CHUNK, GDN_HEADS
    N = T // C
    wide, one, vec, stspec = _gdn_specs(N, False)

    def body(q_ref, k_ref, v_ref, z_ref, ba_ref, al_ref, dt_ref, nw_ref, y_ref, sp_ref, st):
        @pl.when(pl.program_id(0) == 0)
        def _():
            st[...] = jnp.zeros_like(st)

        S = st[...]
        f = _gdn_chunk(_heads(q_ref), _heads(k_ref), _heads(v_ref), ba_ref[...], al_ref[...], dt_ref[...], S)
        sp_ref[0] = S
        st[...] = f["S1"]
        o = f["o"]
        r = lax.rsqrt(jnp.mean(o * o, axis=2, keepdims=True) + EPS)
        y = o * r * nw_ref[...] * _silu(_heads(z_ref))
        for h in range(H):
            y_ref[:, h * HEAD_DIM:(h + 1) * HEAD_DIM] = y[h].astype(bf16)

    return pl.pallas_call(
        body, grid=(N,),
        in_specs=[wide(0), wide(1), wide(2), wide(3), one(4 * H), vec, vec, vec],
        out_specs=[wide(0), stspec],
        out_shape=[jax.ShapeDtypeStruct((T, H * HEAD_DIM), bf16), jax.ShapeDtypeStruct((N, H, HEAD_DIM, HEAD_DIM), f32)],
        scratch_shapes=[pltpu.VMEM((H, HEAD_DIM, HEAD_DIM), f32)],
        compiler_params=_cparams(("arbitrary",)), name=name,
    )(qkv, qkv, qkv, p, p, alog, dtb, nw)


def _gdn_bwd(qkv, p, alog, dtb, nw, sprev, dy_all, name):
    T = qkv.shape[0]
    C, H = GDN_CHUNK, GDN_HEADS
    N = T // C
    wide, one, vec, stspec = _gdn_specs(N, True)
    rs = lambda m: jnp.sum(m, axis=2, keepdims=True)

    def put(ref, val):
        for h in range(H):
            ref[:, h * HEAD_DIM:(h + 1) * HEAD_DIM] = val[h]

    def body(q_ref, k_ref, v_ref, z_ref, ba_ref, al_ref, dt_ref, nw_ref, sp_ref, dy_ref,
             dq_ref, dk_ref, dv_ref, dz_ref, dba_ref, dal_ref, ddt_ref, dnw_ref, dst):
        @pl.when(pl.program_id(0) == 0)
        def _():
            dst[...] = jnp.zeros_like(dst)
            dal_ref[...] = jnp.zeros_like(dal_ref)
            ddt_ref[...] = jnp.zeros_like(ddt_ref)
            dnw_ref[...] = jnp.zeros_like(dnw_ref)

        ba, alog, dtb_, nwv = ba_ref[...], al_ref[...], dt_ref[...], nw_ref[...]
        v = _heads(v_ref)
        S = sp_ref[0]
        f = _gdn_chunk(_heads(q_ref), _heads(k_ref), v, ba, alog, dtb_, S)
        beta, kn, qn, kb, vb, Tinv, kbg = f["beta"], f["kn"], f["qn"], f["kb"], f["vb"], f["Tinv"], f["kbg"]
        eG, eGl, cd, Di, Ds, QK, vn, w_, qg, kg = (f["eG"], f["eGl"], f["cd"], f["Di"], f["Ds"], f["QK"], f["vn"],
                                                    f["w"], f["qg"], f["kg"])
        lane, ri, ci, rowc = f["lane"], f["ri"], f["ci"], f["rowc"]
        o = f["o"]
        z = _heads(z_ref)
        dy = _heads(dy_ref)
        r = lax.rsqrt(jnp.mean(o * o, axis=2, keepdims=True) + EPS)
        nrm = o * r
        sz = _silu(z)
        dn = dy * nwv * sz
        put(dz_ref, dy * nrm * nwv * _dsilu(z))
        dnw_ref[...] += jnp.sum(jnp.sum(dy * nrm * sz, axis=0), axis=0, keepdims=True)
        do = r * (dn - nrm * jnp.mean(dn * nrm, axis=2, keepdims=True))
        dS1 = dst[...]
        dcd = jnp.sum(jnp.sum(S * dS1, axis=2, keepdims=True), axis=1, keepdims=True)
        dkg = _dot(vn, dS1, _NT3)
        dvn = _dot(kg, dS1, _NN3) + _dot(QK, do, _TN3)
        dqg = _dot(do, S, _NT3)
        dQK = _dot(do, vn, _NT3)
        dw = -_dot(dvn, S, _NT3)
        dst[...] = cd * dS1 + _dot(qg, do, _TN3) - _dot(w_, dvn, _TN3)
        dqn = dqg * eG
        deG = rs(dqg * qn)
        dkn = dkg * eGl
        deGl = rs(dkg * kn)
        dQKr = dQK * Di
        E = dQK * QK
        dqn = dqn + _dot(dQKr, kn, _NN3)
        dkn = dkn + _dot(dQKr, qn, _TN3)
        dT = _dot3(dvn, vb, _NT3) + _dot3(dw, kbg, _NT3)
        dvb = _dot3(Tinv, dvn, _TN3)
        dkbg = _dot3(Tinv, dw, _TN3)
        dkb = dkbg * eG
        deG = deG + rs(dkbg * kb)
        dL = -_dot3(_dot3(Tinv, dT, _TN3), Tinv, _NT3)
        dKK = dL * Ds
        E = E + dL * f["Lm"]
        dkb = dkb + _dot(dKK, kn, _NN3)
        dkn = dkn + _dot(dKK, kb, _TN3) + dkb * beta
        dbeta = rs(dkb * kn) + rs(dvb * v)
        put(dv_ref, dvb * beta)
        dG = rs(E) - rs(jnp.swapaxes(E, 1, 2)) + deG * eG - deGl * eGl
        dGl = jnp.sum(deGl * eGl, axis=1, keepdims=True) + dcd * cd
        dG = dG + jnp.where(rowc == C - 1, dGl, 0.0)
        qh = f["qh"]
        put(dq_ref, (HEAD_DIM ** -0.5) * f["rq"] * (dqn - qh * rs(dqn * qh)))
        put(dk_ref, f["rk"] * (dkn - kn * rs(dkn * kn)))
        db = dbeta * beta * (1.0 - beta)
        db_all = jnp.where(lane == 0, db[0], 0.0)
        dG_all = jnp.where(lane == H, dG[0], 0.0)
        for h in range(1, H):
            db_all = db_all + jnp.where(lane == h, db[h], 0.0)
            dG_all = dG_all + jnp.where(lane == H + h, dG[h], 0.0)
        triu = (ri <= ci).astype(f32)
        dg_all = _dot01(triu, dG_all, _NN)
        da_all = dg_all * (-f["ea"]) * _sigmoid(ba + dtb_)
        dba_ref[...] = db_all + da_all
        ddt_ref[...] += jnp.sum(da_all, axis=0, keepdims=True)
        dal_ref[...] += jnp.sum(dg_all * f["g_all"], axis=0, keepdims=True)

    big = jax.ShapeDtypeStruct((T, H * HEAD_DIM), f32)
    small = jax.ShapeDtypeStruct((1, 128), f32)
    return pl.pallas_call(
        body, grid=(N,),
        in_specs=[wide(0), wide(1), wide(2), wide(3), one(4 * H), vec, vec, vec, stspec, wide(0)],
        out_specs=[wide(0), wide(0), wide(0), wide(0), one(0), vec, vec, vec],
        out_shape=[big, big, big, big, jax.ShapeDtypeStruct((T, 128), f32), small, small, small],
        scratch_shapes=[pltpu.VMEM((H, HEAD_DIM, HEAD_DIM), f32)],
        compiler_params=_cparams(("arbitrary",)), name=name,
    )(qkv, qkv, qkv, p, p, alog, dtb, nw, sprev, dy_all)


def _adamw(w, gs, m, v, name):
    R, Cc = w.shape
    S = gs.shape[0]
    tr = R
    for cand in (256, 128, 64, 32, 16, 8):
        if R % cand == 0 and R > cand:
            tr = cand
            break
    c1 = 1.0 - ADAM_B1 ** ADAM_STEP
    c2 = 1.0 - ADAM_B2 ** ADAM_STEP

    def body(w_ref, g_ref, m_ref, v_ref, go_ref, d_ref, mo_ref, vo_ref):
        g = g_ref[0].astype(f32)
        for s in range(1, S):
            g = g + g_ref[s].astype(f32)
        mn = ADAM_B1 * m_ref[...] + (1.0 - ADAM_B1) * g
        vn = ADAM_B2 * v_ref[...] + (1.0 - ADAM_B2) * (g * g)
        go_ref[...] = g
        mo_ref[...] = mn
        vo_ref[...] = vn
        d_ref[...] = -ADAM_LR * ((mn / c1) / (jnp.sqrt(vn / c2) + ADAM_EPS) + ADAM_WD * w_ref[...])

    blk = pl.BlockSpec((tr, Cc), lambda i: (i, 0))
    out = jax.ShapeDtypeStruct((R, Cc), f32)
    return pl.pallas_call(
        body, grid=(R // tr,),
        in_specs=[blk, pl.BlockSpec((S, tr, Cc), lambda i: (0, i, 0)), blk, blk],
        out_specs=[blk] * 4, out_shape=[out] * 4,
        compiler_params=_cparams(("parallel",)), name=name,
    )(w, gs, m, v)


def _me():
    x, y, c = lax.axis_index("x"), lax.axis_index("y"), lax.axis_index("c")
    return x, y, c, 4 * x + 2 * y + c


def _peer(k):
    x, y, c, _ = _me()
    px = 1 - x if k & 4 else x
    py = 1 - y if k & 2 else y
    pc = 1 - c if k & 1 else c
    return (px, py, pc), 4 * px + 2 * py + pc


_HBM = pl.BlockSpec(memory_space=pltpu.HBM)
_SEM = pl.BlockSpec(memory_space=pltpu.SEMAPHORE)
_EFFECT = pltpu.SideEffectType.DATAFLOW_SIDE_EFFECTING


def _copy(src, land, ssem, rsem, k, blocked, landing_slot_of_peer):
    pid, pidx = _peer(k)
    slot = pidx if landing_slot_of_peer else _me()[3]
    return pltpu.make_async_remote_copy(src_ref=src.at[pidx] if blocked else src, dst_ref=land.at[slot],
                                        send_sem=ssem.at[k - 1], recv_sem=rsem.at[k - 1], device_id=pid, device_id_type=MESH)


def _send_start(srcs, blocked, name):
    n = len(srcs)
    lands = [lax.empty(a.shape if blocked else (N_DEV,) + a.shape, a.dtype) for a in srcs]

    def body(*refs):
        src, land, sems, token = refs[:n], refs[n:2 * n], refs[2 * n:4 * n], refs[-1]
        for i in range(n):
            for k in range(1, N_DEV):
                _copy(src[i], land[i], sems[2 * i], sems[2 * i + 1], k, blocked, False).start()
        token[...] = jnp.zeros_like(token)

    sem = pltpu.SemaphoreType.DMA((N_DEV - 1,))
    hbm = lambda a: pltpu.with_memory_space_constraint(a, pltpu.HBM)
    res = pl.pallas_call(
        body, name=name,
        out_shape=tuple([sem] * (2 * n)) + tuple(pltpu.HBM(a.shape, a.dtype) for a in srcs + lands)
        + (jax.ShapeDtypeStruct((8, 128), f32),),
        in_specs=(_HBM,) * (2 * n),
        out_specs=(_SEM,) * (2 * n) + (_HBM,) * (2 * n) + (pl.BlockSpec(memory_space=pltpu.VMEM),),
        input_output_aliases={j: 2 * n + j for j in range(2 * n)},
        compiler_params=pltpu.CompilerParams(has_side_effects=_EFFECT),
    )(*[hbm(a) for a in srcs], *[hbm(a) for a in lands])
    handles = [(res[2 * i], res[2 * i + 1], res[2 * n + i], res[3 * n + i]) for i in range(n)]
    return handles, res[-1]


def _send_wait(handle, blocked, after, name):
    ssem, rsem, src, land = handle

    def body(src_ref, land_ref, ssem_ref, rsem_ref, after_ref, src_out, land_out):
        for k in range(1, N_DEV):
            cp = _copy(src_ref, land_ref, ssem_ref, rsem_ref, k, blocked, True)
            cp.wait_send()
            cp.wait_recv()

    return pl.pallas_call(
        body, name=name, out_shape=(pltpu.HBM(src.shape, src.dtype), pltpu.HBM(land.shape, land.dtype)),
        in_specs=(_HBM, _HBM, _SEM, _SEM, pl.BlockSpec(memory_space=pl.ANY)), out_specs=(_HBM, _HBM),
        input_output_aliases={0: 0, 1: 1}, compiler_params=pltpu.CompilerParams(has_side_effects=_EFFECT),
    )(src, land, ssem, rsem, after)[1]


def _block_diag(w):
    nb, bs = w.shape[0], w.shape[1]
    eye = jnp.eye(nb, dtype=w.dtype)
    return (eye[:, None, :, None] * w[:, :, None, :]).reshape(nb * bs, nb * bs)


def _diag_blocks(d):
    return jnp.stack([d[g, s * 64:(s + 1) * 64, s * 64:(s + 1) * 64] for g in range(4) for s in range(2)])


def _mlp_fwd(x, nw, wu, wd, tag):
    hm = _rms_fwd(x, nw, f"rms_mlp_{tag}")
    relu_ep = lambda acc: (jnp.maximum(acc, 0.0), jnp.square(jnp.maximum(acc, 0.0)))
    r, act = _matmul(hm, wu, "nn", outs=[bf16, bf16], epilogue=relu_ep, name=f"mlp_up_{tag}")
    (xo,) = _matmul(act, wd, "nn", outs=[f32], extras=(x,), epilogue=lambda acc, res: (res + acc,), name=f"mlp_down_{tag}")
    return xo, (hm, r, act)


def _mlp_bwd(x, nw, wu, wd, saved, dxo, dxo_b, tag):
    hm, r, act = saved
    (du,) = _matmul(dxo_b, wd, "nt", outs=[bf16], extras=(r,), epilogue=lambda acc, rr: (acc * (2.0 * rr.astype(f32)),),
                    name=f"mlp_dact_{tag}")
    (dwd,) = _matmul(act, dxo_b, "tn", outs=[bf16], name=f"mlp_dwd_{tag}")
    (dwu,) = _matmul(hm, du, "tn", outs=[bf16], shard_cols=True, name=f"mlp_dwu_{tag}")
    (dhm,) = _matmul(du, wu, "nt", outs=[f32], name=f"mlp_dh_{tag}")
    dx, dx_b, dnw = _rms_bwd(x, nw, dhm, dxo, f"rms_mlp_bwd_{tag}")
    return dx, dx_b, dnw, dwu, dwd.reshape(N_DEV, D_FF // N_DEV, D_MODEL)


def _local_step(x, tgt, P, weight, sink):
    T = x.shape[0]
    cos, sin = _rope_tables(T)
    rtab = _ret_tables()
    row = lambda a: a.reshape(1, -1)
    mix_nw, mlp_nw = P["mixer_norm_w"], P["mlp_norm_w"]
    wr_bd, wi_bd = _block_diag(P["lru_w_r"]), _block_diag(P["lru_w_i"])
    lru_b, lru_br, lru_bi, lru_lam = row(P["lru_conv_b"]), row(P["lru_b_r"]), row(P["lru_b_i"]), row(P["lru_lambda"])
    pad16 = lambda a: jnp.pad(a.reshape(1, GDN_HEADS), ((0, 0), (GDN_HEADS, 128 - 2 * GDN_HEADS)))
    alog, dtb = pad16(P["gdn_a_log"]), pad16(P["gdn_dt_bias"])
    gnw = row(P["gdn_norm_w"])
    residual = lambda acc, res: (res + acc,)

    x0 = x
    h0 = _rms_fwd(x0, mix_nw[0:1], "rms_mix_0")
    w_ie = weight("w_in_even", h0)
    (pe,) = _matmul(h0, w_ie, "nn", outs=[f32], name="in_even")
    y_ret, o_ret, s_ret = _ret_fwd(pe, cos, sin, rtab, "ret_fwd")
    w_lc = weight("lru_conv_w", pe)
    xc = _conv_fwd(pe, 16, w_lc, lru_b, False, "lru_conv_fwd")
    y_lru, h_lru = _lru_fwd(xc, pe, 20, wr_bd, wi_bd, lru_br, lru_bi, lru_lam, "lru_fwd")
    mix0 = jnp.concatenate([y_ret, y_lru], axis=1)
    w_oe = weight("w_out_even", mix0)
    (x1,) = _matmul(mix0, w_oe, "nn", outs=[f32], extras=(x0,), epilogue=residual, name="out_even")
    w_u0, w_d0 = weight("w_up0", x1), weight("w_down0", x1)
    x2, mlp0 = _mlp_fwd(x1, mlp_nw[0:1], w_u0, w_d0, "0")
    h1 = _rms_fwd(x2, mix_nw[1:2], "rms_mix_1")
    w_io = weight("w_in_odd", h1)
    (po,) = _matmul(h1, w_io, "nn", outs=[f32], tn=384, name="in_odd")
    w_gc = weight("gdn_conv_w", po)
    qkv = _conv_fwd(po, 0, w_gc, None, True, "gdn_conv_fwd")
    y_gdn, s_gdn = _gdn_fwd(qkv, po, alog, dtb, gnw, "gdn_fwd")
    w_oo = weight("w_out_odd", y_gdn)
    (x3,) = _matmul(y_gdn, w_oo, "nn", outs=[f32], extras=(x2,), epilogue=residual, name="out_odd")
    w_u1, w_d1 = weight("w_up1", x3), weight("w_down1", x3)
    x4, mlp1 = _mlp_fwd(x3, mlp_nw[1:2], w_u1, w_d1, "1")
    loss, dx4, dx4_b, d_final = _loss_head(x4, row(P["final_norm_w"]), tgt, "loss_head")
    dx3, dx3_b, d_mlp_nw1, d_wu1, d_wd1 = _mlp_bwd(x3, mlp_nw[1:2], w_u1, w_d1, mlp1, dx4, dx4_b, "1")
    tok = sink(dict(w_up1=d_wu1, w_down1=d_wd1))
    (dy_gdn,) = _matmul(dx3_b, w_oo, "nt", outs=[f32], name="out_odd_dx")
    (d_woo,) = _matmul(y_gdn, dx3_b, "tn", outs=[bf16], name="out_odd_dw")
    dq, dk, dv, dz, dba, d_alog, d_dtb, d_gnw = _gdn_bwd(qkv, po, alog, dtb, gnw + tok[0:1, :], s_gdn, dy_gdn, "gdn_bwd")
    dqkv = jnp.concatenate([dq, dk, dv], axis=1)
    dqkv_pre, d_gconv, _ = _conv_bwd(po, 0, w_gc, None, True, dqkv, "gdn_conv_bwd")
    dpo = jnp.concatenate([dqkv_pre, dz, dba], axis=1).astype(bf16)
    (d_wio,) = _matmul(h1, dpo, "tn", outs=[bf16], tn=384, name="in_odd_dw")
    n_odd = ODD_IN // N_DEV
    tok = sink(dict(w_out_odd=d_woo.reshape(N_DEV, D_MODEL // N_DEV, D_MODEL),
                    w_in_odd=jnp.transpose(d_wio[:, :ODD_IN].reshape(D_MODEL, N_DEV, n_odd), (1, 0, 2))))
    (dh1,) = _matmul(dpo, w_io, "nt", outs=[f32], tm=512, tk=ODD_PAD, name="in_odd_dx")
    dx2, dx2_b, d_mix_nw1 = _rms_bwd(x2, mix_nw[1:2] + tok[0:1, 0:1], dh1, dx3, "rms_mix_bwd_1")
    dx1, dx1_b, d_mlp_nw0, d_wu0, d_wd0 = _mlp_bwd(x1, mlp_nw[0:1], w_u0, w_d0, mlp0, dx2, dx2_b, "0")
    tok = sink(dict(w_up0=d_wu0, w_down0=d_wd0))
    (dmix0,) = _matmul(dx1_b, w_oe, "nt", outs=[f32], name="out_even_dx")
    (d_woe,) = _matmul(mix0, dx1_b, "tn", outs=[bf16], name="out_even_dw")
    dq_r, dk_r, dv_r, dg_r = _ret_bwd(pe, cos, sin, rtab, o_ret, s_ret, dmix0, "ret_bwd")
    dxc, dy_l, d_wr, d_wi, d_br, d_bi, d_lam = _lru_bwd(xc, pe, 20, wr_bd, wi_bd, lru_br, lru_bi, lru_lam + tok[0:1, 0:1],
                                                        h_lru, dmix0, 4, "lru_bwd")
    dx_lru, d_lconv, d_lconv_b = _conv_bwd(pe, 16, w_lc, lru_b, False, dxc, "lru_conv_bwd")
    dpe = jnp.concatenate([dq_r, dk_r, dv_r, dg_r, dx_lru, dy_l], axis=1).astype(bf16)
    (d_wie,) = _matmul(h0, dpe, "tn", outs=[bf16], shard_cols=True, name="in_even_dw")
    (dh0,) = _matmul(dpe, w_ie, "nt", outs=[f32], name="in_even_dx")
    dx0, _, d_mix_nw0 = _rms_bwd(x0, mix_nw[0:1], dh0, dx1, "rms_mix_bwd_0")

    G = dict(
        mixer_norm_w=jnp.concatenate([d_mix_nw0, d_mix_nw1], axis=0),
        mlp_norm_w=jnp.concatenate([d_mlp_nw0, d_mlp_nw1], axis=0),
        final_norm_w=d_final.reshape(-1),
        w_in_even=d_wie, lru_conv_w=d_lconv, lru_conv_b=d_lconv_b.reshape(-1),
        lru_w_r=_diag_blocks(d_wr), lru_b_r=d_br.reshape(-1), lru_w_i=_diag_blocks(d_wi), lru_b_i=d_bi.reshape(-1),
        lru_lambda=d_lam.reshape(-1), w_out_even=d_woe.reshape(N_DEV, D_MODEL // N_DEV, D_MODEL), gdn_conv_w=d_gconv,
        gdn_a_log=d_alog[0, GDN_HEADS:2 * GDN_HEADS], gdn_dt_bias=d_dtb[0, GDN_HEADS:2 * GDN_HEADS],
        gdn_norm_w=d_gnw.reshape(-1),
    )
    return loss, dx0, G


_SMALL = ["mixer_norm_w", "mlp_norm_w", "final_norm_w", "lru_conv_b", "lru_w_r", "lru_b_r", "lru_w_i", "lru_b_i",
          "lru_lambda", "gdn_a_log", "gdn_dt_bias", "gdn_norm_w"]
_PACK_ROWS = 688


def _pack(parts):
    flat = jnp.concatenate([p.reshape(-1) for p in parts])
    return jnp.pad(flat, (0, _PACK_ROWS * 128 - flat.shape[0])).reshape(_PACK_ROWS, 128)


def _unpack(packed, shapes):
    flat = packed.reshape(-1)
    out, off = [], 0
    for s in shapes:
        n = int(np.prod(s))
        out.append(flat[off:off + n].reshape(s))
        off += n
    return out


def kernel(x, mixer_norm_w, mlp_norm_w, final_norm_w, w_in_even, lru_conv_w, lru_conv_b, lru_w_r, lru_b_r, lru_w_i, lru_b_i, lru_lambda, w_out_even, w_in_odd, gdn_conv_w, gdn_a_log, gdn_dt_bias, gdn_norm_w, w_out_odd, w_up, w_down, loss_target, m_mixer_norm_w, m_mlp_norm_w, m_final_norm_w, m_w_in_even, m_lru_conv_w, m_lru_conv_b, m_lru_w_r, m_lru_b_r, m_lru_w_i, m_lru_b_i, m_lru_lambda, m_w_out_even, m_w_in_odd, m_gdn_conv_w, m_gdn_a_log, m_gdn_dt_bias, m_gdn_norm_w, m_w_out_odd, m_w_up, m_w_down, v_mixer_norm_w, v_mlp_norm_w, v_final_norm_w, v_w_in_even, v_lru_conv_w, v_lru_conv_b, v_lru_w_r, v_lru_b_r, v_lru_w_i, v_lru_b_i, v_lru_lambda, v_w_out_even, v_w_in_odd, v_gdn_conv_w, v_gdn_a_log, v_gdn_dt_bias, v_gdn_norm_w, v_w_out_odd, v_w_up, v_w_down):
    Pw = dict(mixer_norm_w=mixer_norm_w, mlp_norm_w=mlp_norm_w, final_norm_w=final_norm_w, w_in_even=w_in_even,
              lru_conv_w=lru_conv_w, lru_conv_b=lru_conv_b, lru_w_r=lru_w_r, lru_b_r=lru_b_r, lru_w_i=lru_w_i,
              lru_b_i=lru_b_i, lru_lambda=lru_lambda, w_out_even=w_out_even, w_in_odd=w_in_odd, gdn_conv_w=gdn_conv_w,
              gdn_a_log=gdn_a_log, gdn_dt_bias=gdn_dt_bias, gdn_norm_w=gdn_norm_w, w_out_odd=w_out_odd, w_up=w_up,
              w_down=w_down)
    Pm = dict(mixer_norm_w=m_mixer_norm_w, mlp_norm_w=m_mlp_norm_w, final_norm_w=m_final_norm_w, w_in_even=m_w_in_even,
              lru_conv_w=m_lru_conv_w, lru_conv_b=m_lru_conv_b, lru_w_r=m_lru_w_r, lru_b_r=m_lru_b_r, lru_w_i=m_lru_w_i,
              lru_b_i=m_lru_b_i, lru_lambda=m_lru_lambda, w_out_even=m_w_out_even, w_in_odd=m_w_in_odd,
              gdn_conv_w=m_gdn_conv_w, gdn_a_log=m_gdn_a_log, gdn_dt_bias=m_gdn_dt_bias, gdn_norm_w=m_gdn_norm_w,
              w_out_odd=m_w_out_odd, w_up=m_w_up, w_down=m_w_down)
    Pv = dict(mixer_norm_w=v_mixer_norm_w, mlp_norm_w=v_mlp_norm_w, final_norm_w=v_final_norm_w, w_in_even=v_w_in_even,
              lru_conv_w=v_lru_conv_w, lru_conv_b=v_lru_conv_b, lru_w_r=v_lru_w_r, lru_b_r=v_lru_b_r, lru_w_i=v_lru_w_i,
              lru_b_i=v_lru_b_i, lru_lambda=v_lru_lambda, w_out_even=v_w_out_even, w_in_odd=v_w_in_odd,
              gdn_conv_w=v_gdn_conv_w, gdn_a_log=v_gdn_a_log, gdn_dt_bias=v_gdn_dt_bias, gdn_norm_w=v_gdn_norm_w,
              w_out_odd=v_w_out_odd, w_up=v_w_up, w_down=v_w_down)
    me = _me()[3]
    T = x.shape[1]

    cols = lambda g: jnp.transpose(g, (1, 0, 2)).reshape(g.shape[1], -1)
    rows = lambda g: g.reshape(-1, g.shape[2])
    wide = lambda g: jnp.pad(cols(g), ((0, 0), (0, ODD_PAD - ODD_IN)))
    gather = dict(
        w_in_even=(w_in_even[0].astype(bf16), cols), lru_conv_w=(lru_conv_w[0], cols),
        w_out_even=(w_out_even[0].astype(bf16), rows), w_up0=(w_up[0].astype(bf16), cols), w_down0=(w_down[0].astype(bf16), rows),
        w_in_odd=(w_in_odd[0].astype(bf16), wide), gdn_conv_w=(gdn_conv_w[0], cols),
        w_out_odd=(w_out_odd[0].astype(bf16), rows), w_up1=(w_up[1].astype(bf16), cols), w_down1=(w_down[1].astype(bf16), rows))
    handles, tok = _send_start([s for s, _ in gather.values()], False, "gather_start")
    handles = dict(zip(gather, handles))
    full = {}

    def weight(name, after):
        if name not in full:
            landed = _send_wait(handles[name], False, after, f"gather_wait_{name}")
            shard, finish = gather[name]
            full[name] = finish(lax.dynamic_update_slice_in_dim(landed, shard[None], me, 0))
        return full[name]

    P = {k: Pw[k] for k in ("mlp_norm_w", "final_norm_w")}
    P["mixer_norm_w"] = mixer_norm_w + tok[0:1, 0:1]
    for k in ("lru_w_r", "lru_w_i", "lru_conv_b", "lru_b_r", "lru_b_i", "lru_lambda", "gdn_a_log", "gdn_dt_bias", "gdn_norm_w"):
        P[k] = Pw[k][0]

    sent = {}

    def sink(grads):
        hs, token = _send_start(list(grads.values()), True, "grads_start_" + "_".join(grads))
        for (name, g), h in zip(grads.items(), hs):
            sent[name] = (h, g)
        return token

    loss, dx, G = _local_step(x[0], loss_target[0], P, weight, sink)
    small_g = [G[k].reshape(Pw[k].shape) for k in _SMALL] + [G["lru_conv_w"], G["gdn_conv_w"]]
    packed = _pack(small_g)
    sink(dict(w_out_even=G["w_out_even"], w_in_even=G["w_in_even"],
              small=jnp.broadcast_to(packed[None], (N_DEV,) + packed.shape)))

    def received(name):
        h, g = sent[name]
        landed = _send_wait(h, True, dx, f"grads_wait_{name}")
        return lax.dynamic_update_slice_in_dim(landed, lax.dynamic_slice_in_dim(g, me, 1, 0), me, 0)

    out = {}
    nff = D_FF // N_DEV

    def upd(name, gs, shape2d, layer=None):
        pick = (lambda a: a[layer]) if layer is not None else (lambda a: a)
        return _adamw(pick(Pw[name]).reshape(shape2d), gs, pick(Pm[name]).reshape(shape2d), pick(Pv[name]).reshape(shape2d),
                      f"adamw_{name}" + ("" if layer is None else str(layer)))

    def whole(name, gs, shape2d):
        out[name] = tuple(a.reshape(Pw[name].shape) for a in upd(name, gs, shape2d))

    def layers(name, shape2d):
        per = [upd(name, received(f"{name}{l}"), shape2d, l) for l in range(2)]
        out[name] = tuple(jnp.stack([per[0][j], per[1][j]]) for j in range(4))

    layers("w_up", (D_MODEL, nff))
    layers("w_down", (nff, D_MODEL))
    whole("w_out_odd", received("w_out_odd"), (128, D_MODEL))
    whole("w_in_odd", received("w_in_odd"), (D_MODEL, 514))
    whole("w_out_even", received("w_out_even"), (128, D_MODEL))
    whole("w_in_even", received("w_in_even"), (D_MODEL, 384))
    small_shapes = [Pw[k].shape for k in _SMALL]
    pw, pm, pv = (_pack([Q[k] for k in _SMALL]) for Q in (Pw, Pm, Pv))
    sg, sd, sm, sv = _adamw(pw, received("small"), pm, pv, "adamw_small")
    for arrs_i, packed_out in enumerate((sg, sd, sm, sv)):
        for k, a in zip(_SMALL, _unpack(packed_out, small_shapes)):
            out.setdefault(k, [None] * 4)[arrs_i] = a
    n_small = sum(int(np.prod(s)) for s in small_shapes)
    gflat = sg.reshape(-1)
    g_lconv = gflat[n_small:n_small + CONV_K * LRU_WIDTH].reshape(CONV_K, LRU_WIDTH)
    g_gconv = gflat[n_small + CONV_K * LRU_WIDTH:n_small + CONV_K * (LRU_WIDTH + 3072)].reshape(CONV_K, 3072)
    whole("lru_conv_w", lax.dynamic_slice_in_dim(g_lconv, me * 64, 64, axis=1)[None], (CONV_K, 64))
    whole("gdn_conv_w", lax.dynamic_slice_in_dim(g_gconv, me * 384, 384, axis=1)[None], (CONV_K, 384))

    names = ["mixer_norm_w", "mlp_norm_w", "final_norm_w", "w_in_even", "lru_conv_w", "lru_conv_b", "lru_w_r", "lru_b_r",
             "lru_w_i", "lru_b_i", "lru_lambda", "w_out_even", "w_in_odd", "gdn_conv_w", "gdn_a_log", "gdn_dt_bias",
             "gdn_norm_w", "w_out_odd", "w_up", "w_down"]
    total = lax.psum(loss[0, 0], ("x", "y", "c"))
    res = [total, dx[None]]
    for j in range(4):
        res += [out[k][j] for k in names]
    return tuple(res)
```

```python
import math

import numpy as np
import jax
import jax.numpy as jnp
from jax import lax
from jax.experimental import pallas as pl
from jax.experimental.pallas import tpu as pltpu

f32 = jnp.float32
bf16 = jnp.bfloat16

N_DEV = 8
D_MODEL = 1024
D_FF = 4096
EPS = 1e-6
RET_HEADS = 4
RET_CHUNK = 128
ROPE_THETA = 10000.0
LRU_WIDTH = 512
LRU_C = 8.0
GDN_HEADS = 8
GDN_CHUNK = 64
HEAD_DIM = 128
ODD_IN = 4112
ODD_PAD = 4224
ADAM_LR, ADAM_B1, ADAM_B2, ADAM_EPS, ADAM_WD, ADAM_STEP = 0.001, 0.9, 0.999, 1e-08, 0.01, 10
VMEM_LIMIT = 56 * 1024 * 1024

_NN = (((1,), (0,)), ((), ()))
_NT = (((1,), (1,)), ((), ()))
_TN = (((0,), (0,)), ((), ()))
MESH = pl.DeviceIdType.MESH


def _cparams(sem):
    return pltpu.CompilerParams(dimension_semantics=sem, vmem_limit_bytes=VMEM_LIMIT)


def _dot(a, b, dn):
    return lax.dot_general(a.astype(bf16), b.astype(bf16), dn, preferred_element_type=f32)


def _split2(a):
    hi = a.astype(bf16)
    lo = (a - hi.astype(f32)).astype(bf16)
    return hi, lo


def _dot3(a, b, dn):
    ah, al = _split2(a)
    bh, bl = _split2(b)
    d = lambda p, q: lax.dot_general(p, q, dn, preferred_element_type=f32)
    return d(ah, bh) + (d(ah, bl) + d(al, bh))


def _dot01(a01, b, dn):
    a = a01.astype(bf16)
    b0 = b.astype(bf16)
    r1 = b - b0.astype(f32)
    b1 = r1.astype(bf16)
    b2 = (r1 - b1.astype(f32)).astype(bf16)
    d = lambda q: lax.dot_general(a, q, dn, preferred_element_type=f32)
    return d(b0) + (d(b1) + d(b2))


def _sigmoid(x):
    return jax.nn.sigmoid(x)


def _silu(x):
    return x * _sigmoid(x)


def _dsilu(x):
    s = _sigmoid(x)
    return s * (1.0 + x * (1.0 - s))


def _softplus(x):
    return jnp.maximum(x, 0.0) + jnp.log1p(jnp.exp(-jnp.abs(x)))


_GELU_C = math.sqrt(2.0 / math.pi)


def _gelu(y):
    return 0.5 * y * (1.0 + jnp.tanh(_GELU_C * (y + 0.044715 * y * y * y)))


def _dgelu(y):
    t = jnp.tanh(_GELU_C * (y + 0.044715 * y * y * y))
    return 0.5 * (1.0 + t) + 0.5 * y * (1.0 - t * t) * _GELU_C * (1.0 + 3.0 * 0.044715 * y * y)


def _matmul(a, b, form, *, outs, name, epilogue=None, extras=(), tm=2048, tn=512, tk=1024, shard_cols=False):
    if form == "tn":
        K, M = a.shape
    else:
        M, K = a.shape
    N = b.shape[0] if form == "nt" else b.shape[1]
    if shard_cols:
        tn = N // N_DEV
    tm, tn, tk = min(tm, M), min(tn, N), min(tk, K)
    assert M % tm == 0 and N % tn == 0 and K % tk == 0, (name, M, N, K, tm, tn, tk)
    nk = K // tk
    dn = {"nn": _NN, "nt": _NT, "tn": _TN}[form]
    if form == "tn":
        a_spec = pl.BlockSpec((tk, tm), lambda i, j, k: (k, i))
    else:
        a_spec = pl.BlockSpec((tm, tk), lambda i, j, k: (i, k))
    if form == "nt":
        b_spec = pl.BlockSpec((tn, tk), lambda i, j, k: (j, k))
    else:
        b_spec = pl.BlockSpec((tk, tn), lambda i, j, k: (k, j))
    e_spec = pl.BlockSpec((tm, tn), lambda i, j, k: (i, j))
    if shard_cols:
        o_spec = pl.BlockSpec((None, tm, tn), lambda i, j, k: (j, i, 0))
        o_shape = (N_DEV, M, tn)
    else:
        o_spec = e_spec
        o_shape = (M, N)
    n_ex = len(extras)

    def finish(acc, ex, o_refs):
        vals = (acc,) if epilogue is None else epilogue(acc, *[e[...] for e in ex])
        for r, v in zip(o_refs, vals):
            r[...] = v.astype(r.dtype)

    def body_one(*refs):
        a_ref, b_ref = refs[:2]
        finish(_dot(a_ref[...], b_ref[...], dn), refs[2:2 + n_ex], refs[2 + n_ex:])

    def body_acc(*refs):
        a_ref, b_ref = refs[:2]
        acc = refs[-1]
        k = pl.program_id(2)

        @pl.when(k == 0)
        def _():
            acc[...] = _dot(a_ref[...], b_ref[...], dn)

        @pl.when((k > 0) & (k < nk - 1))
        def _():
            acc[...] += _dot(a_ref[...], b_ref[...], dn)

        @pl.when(k == nk - 1)
        def _():
            finish(acc[...] + _dot(a_ref[...], b_ref[...], dn), refs[2:2 + n_ex], refs[2 + n_ex:-1])

    return pl.pallas_call(
        body_one if nk == 1 else body_acc, grid=(M // tm, N // tn, nk),
        in_specs=[a_spec, b_spec] + [e_spec] * n_ex,
        out_specs=[o_spec] * len(outs),
        out_shape=[jax.ShapeDtypeStruct(o_shape, d) for d in outs],
        scratch_shapes=[] if nk == 1 else [pltpu.VMEM((tm, tn), f32)],
        compiler_params=_cparams(("parallel", "parallel", "arbitrary")), name=name,
    )(a, b, *extras)


def _rms_fwd(x, w, name):
    T, D = x.shape
    tt = min(512, T)

    def body(x_ref, w_ref, h_ref):
        xv = x_ref[...]
        r = lax.rsqrt(jnp.mean(xv * xv, axis=1, keepdims=True) + EPS)
        h_ref[...] = (xv * r * w_ref[...]).astype(bf16)

    return pl.pallas_call(
        body, grid=(T // tt,),
        in_specs=[pl.BlockSpec((tt, D), lambda i: (i, 0)), pl.BlockSpec((1, D), lambda i: (0, 0))],
        out_specs=pl.BlockSpec((tt, D), lambda i: (i, 0)),
        out_shape=jax.ShapeDtypeStruct((T, D), bf16),
        compiler_params=_cparams(("parallel",)), name=name,
    )(x, w)


def _rms_bwd(x, w, dh, dres, name):
    T, D = x.shape
    tt = min(512, T)

    def body(x_ref, w_ref, dh_ref, dres_ref, dx_ref, dxb_ref, dw_ref):
        @pl.when(pl.program_id(0) == 0)
        def _():
            dw_ref[...] = jnp.zeros_like(dw_ref)

        xv = x_ref[...]
        r = lax.rsqrt(jnp.mean(xv * xv, axis=1, keepdims=True) + EPS)
        xn = xv * r
        dhv = dh_ref[...]
        dhw = dhv * w_ref[...]
        dx = dres_ref[...] + r * (dhw - xn * jnp.mean(dhw * xn, axis=1, keepdims=True))
        dx_ref[...] = dx
        dxb_ref[...] = dx.astype(bf16)
        dw_ref[...] += jnp.sum(dhv * xn, axis=0, keepdims=True)

    row = pl.BlockSpec((tt, D), lambda i: (i, 0))
    vec = pl.BlockSpec((1, D), lambda i: (0, 0))
    return pl.pallas_call(
        body, grid=(T // tt,), in_specs=[row, vec, row, row], out_specs=[row, row, vec],
        out_shape=[jax.ShapeDtypeStruct((T, D), f32), jax.ShapeDtypeStruct((T, D), bf16), jax.ShapeDtypeStruct((1, D), f32)],
        compiler_params=_cparams(("arbitrary",)), name=name,
    )(x, w, dh, dres)


def _loss_head(x, w, tgt, name):
    T, D = x.shape
    tt = min(512, T)

    def body(x_ref, w_ref, t_ref, l_ref, dx_ref, dxb_ref, dw_ref):
        @pl.when(pl.program_id(0) == 0)
        def _():
            dw_ref[...] = jnp.zeros_like(dw_ref)
            l_ref[...] = jnp.zeros_like(l_ref)

        xv = x_ref[...]
        wv = w_ref[...]
        r = lax.rsqrt(jnp.mean(xv * xv, axis=1, keepdims=True) + EPS)
        xn = xv * r
        e = xn * wv - t_ref[...]
        l_ref[...] += 0.5 * jnp.sum(jnp.mean(e * e, axis=1, keepdims=True), axis=0, keepdims=True)
        dy = e * (1.0 / D)
        dyw = dy * wv
        dx = r * (dyw - xn * jnp.mean(dyw * xn, axis=1, keepdims=True))
        dx_ref[...] = dx
        dxb_ref[...] = dx.astype(bf16)
        dw_ref[...] += jnp.sum(dy * xn, axis=0, keepdims=True)

    row = pl.BlockSpec((tt, D), lambda i: (i, 0))
    vec = pl.BlockSpec((1, D), lambda i: (0, 0))
    return pl.pallas_call(
        body, grid=(T // tt,), in_specs=[row, vec, row],
        out_specs=[pl.BlockSpec((1, 128), lambda i: (0, 0)), row, row, vec],
        out_shape=[jax.ShapeDtypeStruct((1, 128), f32), jax.ShapeDtypeStruct((T, D), f32),
                   jax.ShapeDtypeStruct((T, D), bf16), jax.ShapeDtypeStruct((1, D), f32)],
        compiler_params=_cparams(("arbitrary",)), name=name,
    )(x, w, tgt)


def _ret_tables():
    H, C = RET_HEADS, RET_CHUNK
    lg = np.log1p(-np.exp2(-5.0 - np.arange(H, dtype=np.float32))).astype(np.float32)
    idx = np.arange(C, dtype=np.float32)
    diff = idx[:, None] - idx[None, :]
    causal = diff >= 0
    dm = np.where(causal[None], np.exp(lg[:, None, None] * np.where(causal, diff, 0.0)[None]), 0.0)
    qd = np.exp(lg[:, None] * (idx[None, :] + 1.0))
    kd = np.exp(lg[:, None] * (C - 1.0 - idx[None, :]))
    cg = np.exp(lg * C)
    tab = np.zeros((H, 4, C, HEAD_DIM), np.float32)
    tab[:, 0] = dm
    tab[:, 1] = qd[:, :, None]
    tab[:, 2] = kd[:, :, None]
    tab[:, 3] = cg[:, None, None]
    return jnp.asarray(tab)


def _rope_tables(T):
    half = HEAD_DIM // 2
    inv = ROPE_THETA ** (-jnp.arange(half, dtype=f32) / half)
    ang = jnp.arange(T, dtype=jnp.int32).astype(f32)[:, None] * inv[None, :]
    c, s = jnp.cos(ang), jnp.sin(ang)
    return jnp.concatenate([c, c], axis=1), jnp.concatenate([-s, s], axis=1)


def _rope(x, cos, sin):
    return x * cos + pltpu.roll(x, HEAD_DIM // 2, 1) * sin


def _unrope(y, cos, sin):
    return y * cos + pltpu.roll(y * sin, HEAD_DIM // 2, 1)


def _stack_heads(ref, H, f=None):
    parts = [ref[:, h * HEAD_DIM:(h + 1) * HEAD_DIM] for h in range(H)]
    return jnp.stack(parts if f is None else [f(a) for a in parts])


def _ret_fwd(p, cos, sin, tab, name):
    T = p.shape[0]
    C, H = RET_CHUNK, RET_HEADS
    N = T // C
    scale = HEAD_DIM ** -0.5

    def body(q_ref, k_ref, v_ref, g_ref, c_ref, s_ref, t_ref, y_ref, o_ref, sp_ref, st):
        @pl.when(pl.program_id(0) == 0)
        def _():
            st[...] = jnp.zeros_like(st)

        cos_, sin_ = c_ref[...], s_ref[...]
        rot = lambda a: _rope(a, cos_, sin_)
        q = _stack_heads(q_ref, H, rot)
        k = _stack_heads(k_ref, H, rot) * scale
        v = _stack_heads(v_ref, H)
        dm, qd, kd, cg = t_ref[:, 0], t_ref[:, 1], t_ref[:, 2], t_ref[:, 3]
        S = st[...]
        P = _dot(q, k, _NT3) * dm
        o = _dot(P, v, _NN3) + _dot(q * qd, S, _NN3)
        sp_ref[0] = S
        st[...] = cg * S + _dot(k * kd, v, _TN3)
        r = lax.rsqrt(jnp.mean(o * o, axis=2, keepdims=True) + EPS)
        y = o * r * _silu(_stack_heads(g_ref, H))
        for h in range(H):
            o_ref[:, h * HEAD_DIM:(h + 1) * HEAD_DIM] = o[h]
            y_ref[:, h * HEAD_DIM:(h + 1) * HEAD_DIM] = y[h].astype(bf16)

    wide = lambda blk: pl.BlockSpec((C, H * HEAD_DIM), lambda n: (n, blk))
    tbl = pl.BlockSpec((C, HEAD_DIM), lambda n: (n, 0))
    return pl.pallas_call(
        body, grid=(N,),
        in_specs=[wide(0), wide(1), wide(2), wide(3), tbl, tbl,
                  pl.BlockSpec((H, 4, C, HEAD_DIM), lambda n: (0, 0, 0, 0))],
        out_specs=[wide(0), wide(0), pl.BlockSpec((1, H, HEAD_DIM, HEAD_DIM), lambda n: (n, 0, 0, 0))],
        out_shape=[jax.ShapeDtypeStruct((T, H * HEAD_DIM), bf16), jax.ShapeDtypeStruct((T, H * HEAD_DIM), f32),
                   jax.ShapeDtypeStruct((N, H, HEAD_DIM, HEAD_DIM), f32)],
        scratch_shapes=[pltpu.VMEM((H, HEAD_DIM, HEAD_DIM), f32)],
        compiler_params=_cparams(("arbitrary",)), name=name,
    )(p, p, p, p, cos, sin, tab)


def _ret_bwd(p, cos, sin, tab, o_raw, sprev, dmix, name):
    T = p.shape[0]
    C, H = RET_CHUNK, RET_HEADS
    N = T // C
    scale = HEAD_DIM ** -0.5

    W = H * HEAD_DIM

    def body(q_ref, k_ref, v_ref, g_ref, c_ref, s_ref, t_ref, o_ref, sp_ref, dy_ref, d_ref, dst):
        @pl.when(pl.program_id(0) == 0)
        def _():
            dst[...] = jnp.zeros_like(dst)

        cos_, sin_ = c_ref[...], s_ref[...]
        rot = lambda a: _rope(a, cos_, sin_)
        q = _stack_heads(q_ref, H, rot)
        k = _stack_heads(k_ref, H, rot) * scale
        v = _stack_heads(v_ref, H)
        g = _stack_heads(g_ref, H)
        dm, qd, kd, cg = t_ref[:, 0], t_ref[:, 1], t_ref[:, 2], t_ref[:, 3]
        S = sp_ref[0]
        o = _stack_heads(o_ref, H)
        dy = _stack_heads(dy_ref, H)
        r = lax.rsqrt(jnp.mean(o * o, axis=2, keepdims=True) + EPS)
        nrm = o * r
        dn = dy * _silu(g)
        dg = dy * nrm * _dsilu(g)
        do = r * (dn - nrm * jnp.mean(dn * nrm, axis=2, keepdims=True))
        dS1 = dst[...]
        P = _dot(q, k, _NT3) * dm
        dP = _dot(do, v, _NT3) * dm
        dq = _dot(dP, k, _NN3) + _dot(do, S, _NT3) * qd
        dk = (_dot(dP, q, _TN3) + _dot(v, dS1, _NT3) * kd) * scale
        dv = _dot(P, do, _TN3) + _dot(k * kd, dS1, _NN3)
        dst[...] = cg * dS1 + _dot(q * qd, do, _TN3)
        for h in range(H):
            d_ref[:, h * HEAD_DIM:(h + 1) * HEAD_DIM] = _unrope(dq[h], cos_, sin_).astype(bf16)
            d_ref[:, W + h * HEAD_DIM:W + (h + 1) * HEAD_DIM] = _unrope(dk[h], cos_, sin_).astype(bf16)
            d_ref[:, 2 * W + h * HEAD_DIM:2 * W + (h + 1) * HEAD_DIM] = dv[h].astype(bf16)
            d_ref[:, 3 * W + h * HEAD_DIM:3 * W + (h + 1) * HEAD_DIM] = dg[h].astype(bf16)

    rev = lambda blk: pl.BlockSpec((C, W), lambda n: (N - 1 - n, blk))
    tbl = pl.BlockSpec((C, HEAD_DIM), lambda n: (N - 1 - n, 0))
    return pl.pallas_call(
        body, grid=(N,),
        in_specs=[rev(0), rev(1), rev(2), rev(3), tbl, tbl,
                  pl.BlockSpec((H, 4, C, HEAD_DIM), lambda n: (0, 0, 0, 0)), rev(0),
                  pl.BlockSpec((1, H, HEAD_DIM, HEAD_DIM), lambda n: (N - 1 - n, 0, 0, 0)), rev(0)],
        out_specs=pl.BlockSpec((C, 4 * W), lambda n: (N - 1 - n, 0)),
        out_shape=jax.ShapeDtypeStruct((T, 4 * W), bf16),
        scratch_shapes=[pltpu.VMEM((H, HEAD_DIM, HEAD_DIM), f32)],
        compiler_params=_cparams(("arbitrary",)), name=name,
    )(p, p, p, p, cos, sin, tab, o_raw, sprev, dmix)


CONV_K = 4
CONV_W = 512
PAD = 8


def _conv_fwd(x, col_off, w, b, act, name):
    T = x.shape[0]
    C = w.shape[1]
    G = C // CONV_W
    tt = min(512, T)
    NT = T // tt
    has_b = b is not None

    def body(*refs):
        if has_b:
            x_ref, w_ref, b_ref, y_ref, pad = refs
        else:
            x_ref, w_ref, y_ref, pad = refs
        t = pl.program_id(1)

        @pl.when(t == 0)
        def _():
            pad[pl.ds(0, PAD), :] = jnp.zeros((PAD, CONV_W), f32)

        pad[pl.ds(PAD, tt), :] = x_ref[...]
        wv = w_ref[...]
        y = wv[0:1, :] * pad[pl.ds(PAD - 3, tt), :]
        for kk in range(1, CONV_K):
            y = y + wv[kk:kk + 1, :] * pad[pl.ds(PAD - 3 + kk, tt), :]
        if has_b:
            y = y + b_ref[...]
        tail = pad[pl.ds(tt, PAD), :]
        pad[pl.ds(0, PAD), :] = tail
        y_ref[...] = _silu(y) if act else y

    in_specs = [pl.BlockSpec((tt, CONV_W), lambda g, t: (t, col_off + g)),
                pl.BlockSpec((CONV_K, CONV_W), lambda g, t: (0, g))]
    args = [x, w]
    if has_b:
        in_specs.append(pl.BlockSpec((1, CONV_W), lambda g, t: (0, g)))
        args.append(b)
    return pl.pallas_call(
        body, grid=(G, NT), in_specs=in_specs,
        out_specs=pl.BlockSpec((tt, CONV_W), lambda g, t: (t, g)),
        out_shape=jax.ShapeDtypeStruct((T, C), f32),
        scratch_shapes=[pltpu.VMEM((tt + PAD, CONV_W), f32)],
        compiler_params=_cparams(("parallel", "arbitrary")), name=name,
    )(*args)


def _conv_bwd(x, col_off, w, b, act, dout, name):
    T = x.shape[0]
    C = w.shape[1]
    G = C // CONV_W
    tt = min(512, T)
    NT = T // tt
    has_b = b is not None

    def body(*refs):
        if has_b:
            x_ref, xp_ref, w_ref, b_ref, d_ref, dx_ref, dw_ref, db_ref, pad, dpad = refs
        else:
            x_ref, xp_ref, w_ref, d_ref, dx_ref, dw_ref, db_ref, pad, dpad = refs
        t = pl.program_id(1)
        first_tile = t == NT - 1

        @pl.when(t == 0)
        def _():
            dpad[pl.ds(tt, PAD), :] = jnp.zeros((PAD, CONV_W), f32)
            dw_ref[...] = jnp.zeros_like(dw_ref)
            db_ref[...] = jnp.zeros_like(db_ref)

        pad[pl.ds(0, PAD), :] = jnp.where(first_tile, 0.0, xp_ref[...])
        pad[pl.ds(PAD, tt), :] = x_ref[...]
        wv = w_ref[...]
        dy = d_ref[...]
        if act:
            y = wv[0:1, :] * pad[pl.ds(PAD - 3, tt), :]
            for kk in range(1, CONV_K):
                y = y + wv[kk:kk + 1, :] * pad[pl.ds(PAD - 3 + kk, tt), :]
            if has_b:
                y = y + b_ref[...]
            dy = dy * _dsilu(y)
        dpad[pl.ds(0, tt), :] = dy
        dx = wv[3:4, :] * dy
        for j in range(1, CONV_K):
            dx = dx + wv[3 - j:4 - j, :] * dpad[pl.ds(j, tt), :]
        dx_ref[...] = dx.astype(bf16)
        head = dpad[pl.ds(0, PAD), :]
        dpad[pl.ds(tt, PAD), :] = head
        for kk in range(CONV_K):
            dw_ref[kk:kk + 1, :] += jnp.sum(dy * pad[pl.ds(PAD - 3 + kk, tt), :], axis=0, keepdims=True)
        db_ref[...] += jnp.sum(dy, axis=0, keepdims=True)

    rows8 = tt // PAD
    in_specs = [pl.BlockSpec((tt, CONV_W), lambda g, t: (NT - 1 - t, col_off + g)),
                pl.BlockSpec((PAD, CONV_W), lambda g, t: (jnp.maximum((NT - 1 - t) * rows8 - 1, 0), col_off + g)),
                pl.BlockSpec((CONV_K, CONV_W), lambda g, t: (0, g))]
    args = [x, x, w]
    if has_b:
        in_specs.append(pl.BlockSpec((1, CONV_W), lambda g, t: (0, g)))
        args.append(b)
    in_specs.append(pl.BlockSpec((tt, CONV_W), lambda g, t: (NT - 1 - t, g)))
    args.append(dout)
    return pl.pallas_call(
        body, grid=(G, NT), in_specs=in_specs,
        out_specs=[pl.BlockSpec((tt, CONV_W), lambda g, t: (NT - 1 - t, g)),
                   pl.BlockSpec((CONV_K, CONV_W), lambda g, t: (0, g)),
                   pl.BlockSpec((1, CONV_W), lambda g, t: (0, g))],
        out_shape=[jax.ShapeDtypeStruct((T, C), bf16), jax.ShapeDtypeStruct((CONV_K, C), f32),
                   jax.ShapeDtypeStruct((1, C), f32)],
        scratch_shapes=[pltpu.VMEM((tt + PAD, CONV_W), f32), pltpu.VMEM((tt + PAD, CONV_W), f32)],
        compiler_params=_cparams(("parallel", "arbitrary")), name=name,
    )(*args)


def _lru_gates(xc, wr, wi, br, bi, lam):
    r = _sigmoid(_dot(xc, wr, _NN) + br)
    i = _sigmoid(_dot(xc, wi, _NN) + bi)
    sp = _softplus(-lam)
    a = jnp.exp(-LRU_C * r * sp)
    mult = jnp.sqrt(1.0 - a * a)
    return r, i, sp, a, mult


def _lru_fwd(xc, p, y_off, wr, wi, br, bi, lam, name):
    T = xc.shape[0]
    G = LRU_WIDTH // 128
    tt = min(512, T)
    NT = T // tt

    def body(x_ref, y_ref, wr_ref, wi_ref, br_ref, bi_ref, l_ref, o_ref, h_ref, hc):
        t = pl.program_id(1)

        @pl.when(t == 0)
        def _():
            hc[...] = jnp.zeros_like(hc)

        x = x_ref[...]
        r, i, sp, a, mult = _lru_gates(x, wr_ref[...], wi_ref[...], br_ref[...], bi_ref[...], l_ref[...])
        row = lax.broadcasted_iota(jnp.int32, (tt, 128), 0)
        mult = jnp.where((row == 0) & (t == 0), 1.0, mult)
        U = x * i * mult
        A = a
        d = 1
        while d < tt:
            keep = row >= d
            Ush = jnp.where(keep, pltpu.roll(U, d, 0), 0.0)
            Ash = jnp.where(keep, pltpu.roll(A, d, 0), 1.0)
            U = A * Ush + U
            A = A * Ash
            d *= 2
        h = U + A * hc[0:1, :]
        h_ref[...] = h
        hc[...] = jnp.broadcast_to(h[tt - 1:tt, :], hc.shape)
        o_ref[...] = (h * _gelu(y_ref[...])).astype(bf16)

    tile = pl.BlockSpec((tt, 128), lambda g, t: (t, g))
    vec = pl.BlockSpec((1, 128), lambda g, t: (0, g))
    wsp = pl.BlockSpec((128, 128), lambda g, t: (g, g))
    return pl.pallas_call(
        body, grid=(G, NT),
        in_specs=[tile, pl.BlockSpec((tt, 128), lambda g, t: (t, y_off + g)), wsp, wsp, vec, vec, vec],
        out_specs=[tile, tile],
        out_shape=[jax.ShapeDtypeStruct((T, LRU_WIDTH), bf16), jax.ShapeDtypeStruct((T, LRU_WIDTH), f32)],
        scratch_shapes=[pltpu.VMEM((8, 128), f32)],
        compiler_params=_cparams(("parallel", "arbitrary")), name=name,
    )(xc, p, wr, wi, br, bi, lam)


def _lru_bwd(xc, p, y_off, wr, wi, br, bi, lam, hs, dmix, d_off, name):
    T = xc.shape[0]
    G = LRU_WIDTH // 128
    tt = min(512, T)
    NT = T // tt

    def body(x_ref, y_ref, wr_ref, wi_ref, br_ref, bi_ref, l_ref, h_ref, hp_ref, do_ref,
             dx_ref, dy_ref, dwr_ref, dwi_ref, dbr_ref, dbi_ref, dl_ref, lc, an):
        t = pl.program_id(1)
        first_tile = t == NT - 1

        @pl.when(t == 0)
        def _():
            lc[...] = jnp.zeros_like(lc)
            an[...] = jnp.zeros_like(an)
            dwr_ref[...] = jnp.zeros_like(dwr_ref)
            dwi_ref[...] = jnp.zeros_like(dwi_ref)
            dbr_ref[...] = jnp.zeros_like(dbr_ref)
            dbi_ref[...] = jnp.zeros_like(dbi_ref)
            dl_ref[...] = jnp.zeros_like(dl_ref)

        x = x_ref[...]
        y = y_ref[...]
        wr, wi, lam_ = wr_ref[...], wi_ref[...], l_ref[...]
        r, i, sp, a, mult_raw = _lru_gates(x, wr, wi, br_ref[...], bi_ref[...], lam_)
        row = lax.broadcasted_iota(jnp.int32, (tt, 128), 0)
        t0 = (row == 0) & first_tile
        mult = jnp.where(t0, 1.0, mult_raw)
        h = h_ref[...]
        do = do_ref[...]
        dh = do * _gelu(y)
        dy_ref[...] = (do * h * _dgelu(y)).astype(bf16)
        B = jnp.where(row == tt - 1, an[0:1, :], pltpu.roll(a, tt - 1, 0))
        L = dh
        d = 1
        while d < tt:
            keep = row < tt - d
            Lsh = jnp.where(keep, pltpu.roll(L, tt - d, 0), 0.0)
            Bsh = jnp.where(keep, pltpu.roll(B, tt - d, 0), 1.0)
            L = L + B * Lsh
            B = B * Bsh
            d *= 2
        L = L + B * lc[0:1, :]
        lc[...] = jnp.broadcast_to(L[0:1, :], lc.shape)
        an[...] = jnp.broadcast_to(a[0:1, :], an.shape)
        hprev = jnp.where(first_tile, 0.0, hp_ref[...])[PAD - 1:PAD, :]
        hm1 = jnp.where(row == 0, hprev, pltpu.roll(h, 1, 0))
        da = L * hm1
        dxc = L * i * mult
        di = L * x * mult
        dmult = jnp.where(t0, 0.0, L * x * i)
        da = da - jnp.where(t0, 0.0, dmult * a / mult_raw)
        dlog_a = da * a
        dr = dlog_a * (-LRU_C) * sp
        dsp = jnp.sum(dlog_a * (-LRU_C) * r, axis=0, keepdims=True)
        dpr = dr * r * (1.0 - r)
        dpi = di * i * (1.0 - i)
        dx_ref[...] = dxc + _dot(dpr, wr, _NT) + _dot(dpi, wi, _NT)
        dwr_ref[0] += _dot(x, dpr, _TN)
        dwi_ref[0] += _dot(x, dpi, _TN)
        dbr_ref[...] += jnp.sum(dpr, axis=0, keepdims=True)
        dbi_ref[...] += jnp.sum(dpi, axis=0, keepdims=True)
        dl_ref[...] += dsp * (-_sigmoid(-lam_))

    rows8 = tt // PAD
    tile = pl.BlockSpec((tt, 128), lambda g, t: (NT - 1 - t, g))
    vec = pl.BlockSpec((1, 128), lambda g, t: (0, g))
    wsp = pl.BlockSpec((128, 128), lambda g, t: (g, g))
    wout = pl.BlockSpec((1, 128, 128), lambda g, t: (g, 0, 0))
    return pl.pallas_call(
        body, grid=(G, NT),
        in_specs=[tile, pl.BlockSpec((tt, 128), lambda g, t: (NT - 1 - t, y_off + g)), wsp, wsp, vec, vec, vec, tile,
                  pl.BlockSpec((PAD, 128), lambda g, t: (jnp.maximum((NT - 1 - t) * rows8 - 1, 0), g)),
                  pl.BlockSpec((tt, 128), lambda g, t: (NT - 1 - t, d_off + g))],
        out_specs=[tile, tile, wout, wout, vec, vec, vec],
        out_shape=[jax.ShapeDtypeStruct((T, LRU_WIDTH), f32), jax.ShapeDtypeStruct((T, LRU_WIDTH), bf16),
                   jax.ShapeDtypeStruct((G, 128, 128), f32), jax.ShapeDtypeStruct((G, 128, 128), f32),
                   jax.ShapeDtypeStruct((1, LRU_WIDTH), f32), jax.ShapeDtypeStruct((1, LRU_WIDTH), f32),
                   jax.ShapeDtypeStruct((1, LRU_WIDTH), f32)],
        scratch_shapes=[pltpu.VMEM((8, 128), f32), pltpu.VMEM((8, 128), f32)],
        compiler_params=_cparams(("parallel", "arbitrary")), name=name,
    )(xc, p, wr, wi, br, bi, lam, hs, hs, dmix)


_NN3 = (((2,), (1,)), ((0,), (0,)))
_NT3 = (((2,), (2,)), ((0,), (0,)))
_TN3 = (((1,), (1,)), ((0,), (0,)))


def _heads(ref):
    return _stack_heads(ref, GDN_HEADS)


def _gdn_chunk(qr, kr, v, ba, alog, dtb, S):
    C, H = GDN_CHUNK, GDN_HEADS
    lane = lax.broadcasted_iota(jnp.int32, (C, 128), 1)
    lane3 = lax.broadcasted_iota(jnp.int32, (H, C, 128), 2)
    ri = lax.broadcasted_iota(jnp.int32, (C, C), 0)
    ci = lax.broadcasted_iota(jnp.int32, (C, C), 1)
    rowc = lax.broadcasted_iota(jnp.int32, (C, 1), 0)
    col = lambda m, j: jnp.sum(jnp.where(lane == j, m, 0.0), axis=1, keepdims=True)
    cols = lambda m, off: jnp.stack([col(m, off + h) for h in range(H)])
    ea = jnp.exp(alog)
    g_all = -ea * _softplus(ba + dtb)
    tri = (ri >= ci).astype(f32)
    G_all = _dot01(tri, g_all, _NN)
    beta = cols(_sigmoid(ba), 0)
    Gc = cols(G_all, H)
    rq = lax.rsqrt(jnp.sum(qr * qr, axis=2, keepdims=True) + EPS)
    rk = lax.rsqrt(jnp.sum(kr * kr, axis=2, keepdims=True) + EPS)
    qh, kn = qr * rq, kr * rk
    qn = qh * (HEAD_DIM ** -0.5)
    Grow = _dot01(jnp.ones((H, C, 128), f32), jnp.where(lane3 == 0, Gc, 0.0), _NT3)
    incl = ri >= ci
    Di = jnp.where(incl, jnp.exp(jnp.where(incl, Gc - Grow, 0.0)), 0.0)
    Ds = jnp.where(ri > ci, Di, 0.0)
    Gl = jnp.sum(jnp.where(rowc == C - 1, Gc, 0.0), axis=1, keepdims=True)
    eG = jnp.exp(Gc)
    eGl = jnp.exp(Gl - Gc)
    cd = jnp.exp(Gl)
    kb = kn * beta
    vb = v * beta
    Lm = _dot(kb, kn, _NT3) * Ds
    eye = (ri == ci).astype(f32)
    X = -Lm
    Tinv = eye + X
    Pw = X
    for _ in range(5):
        Pw = _dot3(Pw, Pw, _NN3)
        Tinv = Tinv + _dot3(Tinv, Pw, _NN3)
    kbg = kb * eG
    u = _dot3(Tinv, vb, _NN3)
    w = _dot3(Tinv, kbg, _NN3)
    QK = _dot(qn, kn, _NT3) * Di
    qg = qn * eG
    kg = kn * eGl
    vn = u - _dot(w, S, _NN3)
    o = _dot(qg, S, _NN3) + _dot(QK, vn, _NN3)
    S1 = S * cd + _dot(kg, vn, _TN3)
    return dict(beta=beta, g_all=g_all, rq=rq, rk=rk, qh=qh, kn=kn, qn=qn, Di=Di, Ds=Ds, eG=eG, eGl=eGl, cd=cd,
                kb=kb, vb=vb, Lm=Lm, Tinv=Tinv, kbg=kbg, w=w, QK=QK, qg=qg, kg=kg, vn=vn, o=o, S1=S1,
                lane=lane, ri=ri, ci=ci, rowc=rowc, ea=ea)


def _gdn_specs(N, rev):
    C = GDN_CHUNK
    H = GDN_HEADS
    nn = (lambda n: N - 1 - n) if rev else (lambda n: n)
    wide = lambda blk: pl.BlockSpec((C, H * HEAD_DIM), lambda n: (nn(n), blk))
    one = lambda off: pl.BlockSpec((C, HEAD_DIM), lambda n: (nn(n), off))
    vec = pl.BlockSpec((1, 128), lambda n: (0, 0))
    st = pl.BlockSpec((1, H, HEAD_DIM, HEAD_DIM), lambda n: (nn(n), 0, 0, 0))
    return wide, one, vec, st


def _gdn_fwd(qkv, p, alog, dtb, nw, name):
    T = qkv.shape[0]
    C, H = GDN_CHUNK, GDN_HEADS
    N = T // C
    wide, one, vec, stspec = _gdn_specs(N, False)

    def body(q_ref, k_ref, v_ref, z_ref, ba_ref, al_ref, dt_ref, nw_ref, y_ref, sp_ref, st):
        @pl.when(pl.program_id(0) == 0)
        def _():
            st[...] = jnp.zeros_like(st)

        S = st[...]
        f = _gdn_chunk(_heads(q_ref), _heads(k_ref), _heads(v_ref), ba_ref[...], al_ref[...], dt_ref[...], S)
        sp_ref[0] = S
        st[...] = f["S1"]
        o = f["o"]
        r = lax.rsqrt(jnp.mean(o * o, axis=2, keepdims=True) + EPS)
        y = o * r * nw_ref[...] * _silu(_heads(z_ref))
        for h in range(H):
            y_ref[:, h * HEAD_DIM:(h + 1) * HEAD_DIM] = y[h].astype(bf16)

    return pl.pallas_call(
        body, grid=(N,),
        in_specs=[wide(0), wide(1), wide(2), wide(3), one(4 * H), vec, vec, vec],
        out_specs=[wide(0), stspec],
        out_shape=[jax.ShapeDtypeStruct((T, H * HEAD_DIM), bf16), jax.ShapeDtypeStruct((N, H, HEAD_DIM, HEAD_DIM), f32)],
        scratch_shapes=[pltpu.VMEM((H, HEAD_DIM, HEAD_DIM), f32)],
        compiler_params=_cparams(("arbitrary",)), name=name,
    )(qkv, qkv, qkv, p, p, alog, dtb, nw)


def _gdn_bwd(qkv, p, alog, dtb, nw, sprev, dy_all, name):
    T = qkv.shape[0]
    C, H = GDN_CHUNK, GDN_HEADS
    N = T // C
    wide, one, vec, stspec = _gdn_specs(N, True)
    rs = lambda m: jnp.sum(m, axis=2, keepdims=True)

    def put(ref, val, col=0):
        for h in range(H):
            ref[:, col + h * HEAD_DIM:col + (h + 1) * HEAD_DIM] = val[h].astype(ref.dtype)

    def body(q_ref, k_ref, v_ref, z_ref, ba_ref, al_ref, dt_ref, nw_ref, sp_ref, dy_ref,
             dqkv_ref, dz_ref, dba_ref, dal_ref, ddt_ref, dnw_ref, dst):
        @pl.when(pl.program_id(0) == 0)
        def _():
            dst[...] = jnp.zeros_like(dst)
            dal_ref[...] = jnp.zeros_like(dal_ref)
            ddt_ref[...] = jnp.zeros_like(ddt_ref)
            dnw_ref[...] = jnp.zeros_like(dnw_ref)

        ba, alog, dtb_, nwv = ba_ref[...], al_ref[...], dt_ref[...], nw_ref[...]
        v = _heads(v_ref)
        S = sp_ref[0]
        f = _gdn_chunk(_heads(q_ref), _heads(k_ref), v, ba, alog, dtb_, S)
        beta, kn, qn, kb, vb, Tinv, kbg = f["beta"], f["kn"], f["qn"], f["kb"], f["vb"], f["Tinv"], f["kbg"]
        eG, eGl, cd, Di, Ds, QK, vn, w_, qg, kg = (f["eG"], f["eGl"], f["cd"], f["Di"], f["Ds"], f["QK"], f["vn"],
                                                    f["w"], f["qg"], f["kg"])
        lane, ri, ci, rowc = f["lane"], f["ri"], f["ci"], f["rowc"]
        o = f["o"]
        z = _heads(z_ref)
        dy = _heads(dy_ref)
        r = lax.rsqrt(jnp.mean(o * o, axis=2, keepdims=True) + EPS)
        nrm = o * r
        sz = _silu(z)
        dn = dy * nwv * sz
        put(dz_ref, dy * nrm * nwv * _dsilu(z))
        dnw_ref[...] += jnp.sum(jnp.sum(dy * nrm * sz, axis=0), axis=0, keepdims=True)
        do = r * (dn - nrm * jnp.mean(dn * nrm, axis=2, keepdims=True))
        dS1 = dst[...]
        dcd = jnp.sum(jnp.sum(S * dS1, axis=2, keepdims=True), axis=1, keepdims=True)
        dkg = _dot(vn, dS1, _NT3)
        dvn = _dot(kg, dS1, _NN3) + _dot(QK, do, _TN3)
        dqg = _dot(do, S, _NT3)
        dQK = _dot(do, vn, _NT3)
        dw = -_dot(dvn, S, _NT3)
        dst[...] = cd * dS1 + _dot(qg, do, _TN3) - _dot(w_, dvn, _TN3)
        dqn = dqg * eG
        deG = rs(dqg * qn)
        dkn = dkg * eGl
        deGl = rs(dkg * kn)
        dQKr = dQK * Di
        E = dQK * QK
        dqn = dqn + _dot(dQKr, kn, _NN3)
        dkn = dkn + _dot(dQKr, qn, _TN3)
        dT = _dot3(dvn, vb, _NT3) + _dot3(dw, kbg, _NT3)
        dvb = _dot3(Tinv, dvn, _TN3)
        dkbg = _dot3(Tinv, dw, _TN3)
        dkb = dkbg * eG
        deG = deG + rs(dkbg * kb)
        dL = -_dot3(_dot3(Tinv, dT, _TN3), Tinv, _NT3)
        dKK = dL * Ds
        E = E + dL * f["Lm"]
        dkb = dkb + _dot(dKK, kn, _NN3)
        dkn = dkn + _dot(dKK, kb, _TN3) + dkb * beta
        dbeta = rs(dkb * kn) + rs(dvb * v)
        put(dqkv_ref, dvb * beta, 2 * H * HEAD_DIM)
        dG = rs(E) - rs(jnp.swapaxes(E, 1, 2)) + deG * eG - deGl * eGl
        dGl = jnp.sum(deGl * eGl, axis=1, keepdims=True) + dcd * cd
        dG = dG + jnp.where(rowc == C - 1, dGl, 0.0)
        qh = f["qh"]
        put(dqkv_ref, (HEAD_DIM ** -0.5) * f["rq"] * (dqn - qh * rs(dqn * qh)))
        put(dqkv_ref, f["rk"] * (dkn - kn * rs(dkn * kn)), H * HEAD_DIM)
        db = dbeta * beta * (1.0 - beta)
        db_all = jnp.where(lane == 0, db[0], 0.0)
        dG_all = jnp.where(lane == H, dG[0], 0.0)
        for h in range(1, H):
            db_all = db_all + jnp.where(lane == h, db[h], 0.0)
            dG_all = dG_all + jnp.where(lane == H + h, dG[h], 0.0)
        triu = (ri <= ci).astype(f32)
        dg_all = _dot01(triu, dG_all, _NN)
        da_all = dg_all * (-f["ea"]) * _sigmoid(ba + dtb_)
        dba_ref[...] = (db_all + da_all).astype(bf16)
        ddt_ref[...] += jnp.sum(da_all, axis=0, keepdims=True)
        dal_ref[...] += jnp.sum(dg_all * f["g_all"], axis=0, keepdims=True)

    small = jax.ShapeDtypeStruct((1, 128), f32)
    return pl.pallas_call(
        body, grid=(N,),
        in_specs=[wide(0), wide(1), wide(2), wide(3), one(4 * H), vec, vec, vec, stspec, wide(0)],
        out_specs=[pl.BlockSpec((C, 3 * H * HEAD_DIM), lambda n: (N - 1 - n, 0)), wide(0), one(0), vec, vec, vec],
        out_shape=[jax.ShapeDtypeStruct((T, 3 * H * HEAD_DIM), f32), jax.ShapeDtypeStruct((T, H * HEAD_DIM), bf16),
                   jax.ShapeDtypeStruct((T, 128), bf16), small, small, small],
        scratch_shapes=[pltpu.VMEM((H, HEAD_DIM, HEAD_DIM), f32)],
        compiler_params=_cparams(("arbitrary",)), name=name,
    )(qkv, qkv, qkv, p, p, alog, dtb, nw, sprev, dy_all)


def _adamw(w, gs, m, v, name):
    R, Cc = w.shape
    S = gs.shape[0]
    tr = R
    for cand in (256, 128, 64, 32, 16, 8):
        if R % cand == 0 and R > cand:
            tr = cand
            break
    c1 = 1.0 - ADAM_B1 ** ADAM_STEP
    c2 = 1.0 - ADAM_B2 ** ADAM_STEP

    def body(w_ref, g_ref, m_ref, v_ref, go_ref, d_ref, mo_ref, vo_ref):
        g = g_ref[0].astype(f32)
        for s in range(1, S):
            g = g + g_ref[s].astype(f32)
        mn = ADAM_B1 * m_ref[...] + (1.0 - ADAM_B1) * g
        vn = ADAM_B2 * v_ref[...] + (1.0 - ADAM_B2) * (g * g)
        go_ref[...] = g
        mo_ref[...] = mn
        vo_ref[...] = vn
        d_ref[...] = -ADAM_LR * ((mn / c1) / (jnp.sqrt(vn / c2) + ADAM_EPS) + ADAM_WD * w_ref[...])

    blk = pl.BlockSpec((tr, Cc), lambda i: (i, 0))
    out = jax.ShapeDtypeStruct((R, Cc), f32)
    return pl.pallas_call(
        body, grid=(R // tr,),
        in_specs=[blk, pl.BlockSpec((S, tr, Cc), lambda i: (0, i, 0)), blk, blk],
        out_specs=[blk] * 4, out_shape=[out] * 4,
        compiler_params=_cparams(("parallel",)), name=name,
    )(w, gs, m, v)


def _me():
    x, y, c = lax.axis_index("x"), lax.axis_index("y"), lax.axis_index("c")
    return x, y, c, 4 * x + 2 * y + c


def _peer(k):
    x, y, c, _ = _me()
    px = 1 - x if k & 4 else x
    py = 1 - y if k & 2 else y
    pc = 1 - c if k & 1 else c
    return (px, py, pc), 4 * px + 2 * py + pc


_HBM = pl.BlockSpec(memory_space=pltpu.HBM)
_SEM = pl.BlockSpec(memory_space=pltpu.SEMAPHORE)
_EFFECT = pltpu.SideEffectType.DATAFLOW_SIDE_EFFECTING


def _copy(src, land, ssem, rsem, k, blocked, landing_slot_of_peer):
    pid, pidx = _peer(k)
    slot = pidx if landing_slot_of_peer else _me()[3]
    return pltpu.make_async_remote_copy(src_ref=src.at[pidx] if blocked else src, dst_ref=land.at[slot],
                                        send_sem=ssem.at[k - 1], recv_sem=rsem.at[k - 1], device_id=pid, device_id_type=MESH)


def _send_start(srcs, blocked, name):
    n = len(srcs)
    lands = [lax.empty(a.shape if blocked else (N_DEV,) + a.shape, a.dtype) for a in srcs]

    def body(*refs):
        src, land, sems, token = refs[:n], refs[n:2 * n], refs[2 * n:4 * n], refs[-1]
        for i in range(n):
            for k in range(1, N_DEV):
                _copy(src[i], land[i], sems[2 * i], sems[2 * i + 1], k, blocked, False).start()
        token[...] = jnp.zeros_like(token)

    sem = pltpu.SemaphoreType.DMA((N_DEV - 1,))
    hbm = lambda a: pltpu.with_memory_space_constraint(a, pltpu.HBM)
    res = pl.pallas_call(
        body, name=name,
        out_shape=tuple([sem] * (2 * n)) + tuple(pltpu.HBM(a.shape, a.dtype) for a in srcs + lands)
        + (jax.ShapeDtypeStruct((8, 128), f32),),
        in_specs=(_HBM,) * (2 * n),
        out_specs=(_SEM,) * (2 * n) + (_HBM,) * (2 * n) + (pl.BlockSpec(memory_space=pltpu.VMEM),),
        input_output_aliases={j: 2 * n + j for j in range(2 * n)},
        compiler_params=pltpu.CompilerParams(has_side_effects=_EFFECT),
    )(*[hbm(a) for a in srcs], *[hbm(a) for a in lands])
    handles = [(res[2 * i], res[2 * i + 1], res[2 * n + i], res[3 * n + i]) for i in range(n)]
    return handles, res[-1]


def _send_wait(handle, blocked, after, name):
    ssem, rsem, src, land = handle

    def body(src_ref, land_ref, ssem_ref, rsem_ref, after_ref, src_out, land_out):
        for k in range(1, N_DEV):
            cp = _copy(src_ref, land_ref, ssem_ref, rsem_ref, k, blocked, True)
            cp.wait_send()
            cp.wait_recv()

    return pl.pallas_call(
        body, name=name, out_shape=(pltpu.HBM(src.shape, src.dtype), pltpu.HBM(land.shape, land.dtype)),
        in_specs=(_HBM, _HBM, _SEM, _SEM, pl.BlockSpec(memory_space=pl.ANY)), out_specs=(_HBM, _HBM),
        input_output_aliases={0: 0, 1: 1}, compiler_params=pltpu.CompilerParams(has_side_effects=_EFFECT),
    )(src, land, ssem, rsem, after)[1]


def _block_diag(w):
    nb, bs = w.shape[0], w.shape[1]
    eye = jnp.eye(nb, dtype=w.dtype)
    return (eye[:, None, :, None] * w[:, :, None, :]).reshape(nb * bs, nb * bs)


def _diag_blocks(d):
    return jnp.stack([d[g, s * 64:(s + 1) * 64, s * 64:(s + 1) * 64] for g in range(4) for s in range(2)])


def _mlp_fwd(x, nw, wu, wd, tag):
    hm = _rms_fwd(x, nw, f"rms_mlp_{tag}")
    relu_ep = lambda acc: (jnp.maximum(acc, 0.0), jnp.square(jnp.maximum(acc, 0.0)))
    r, act = _matmul(hm, wu, "nn", outs=[bf16, bf16], epilogue=relu_ep, name=f"mlp_up_{tag}")
    (xo,) = _matmul(act, wd, "nn", outs=[f32], extras=(x,), epilogue=lambda acc, res: (res + acc,), name=f"mlp_down_{tag}")
    return xo, (hm, r, act)


def _mlp_bwd(x, nw, wu, wd, saved, dxo, dxo_b, tag):
    hm, r, act = saved
    (du,) = _matmul(dxo_b, wd, "nt", outs=[bf16], extras=(r,), epilogue=lambda acc, rr: (acc * (2.0 * rr.astype(f32)),),
                    name=f"mlp_dact_{tag}")
    (dwd,) = _matmul(act, dxo_b, "tn", outs=[bf16], name=f"mlp_dwd_{tag}")
    (dwu,) = _matmul(hm, du, "tn", outs=[bf16], shard_cols=True, name=f"mlp_dwu_{tag}")
    (dhm,) = _matmul(du, wu, "nt", outs=[f32], name=f"mlp_dh_{tag}")
    dx, dx_b, dnw = _rms_bwd(x, nw, dhm, dxo, f"rms_mlp_bwd_{tag}")
    return dx, dx_b, dnw, dwu, dwd.reshape(N_DEV, D_FF // N_DEV, D_MODEL)


def _local_step(x, tgt, P, weight, sink):
    T = x.shape[0]
    cos, sin = _rope_tables(T)
    rtab = _ret_tables()
    row = lambda a: a.reshape(1, -1)
    mix_nw, mlp_nw = P["mixer_norm_w"], P["mlp_norm_w"]
    wr_bd, wi_bd = _block_diag(P["lru_w_r"]), _block_diag(P["lru_w_i"])
    lru_b, lru_br, lru_bi, lru_lam = row(P["lru_conv_b"]), row(P["lru_b_r"]), row(P["lru_b_i"]), row(P["lru_lambda"])
    pad16 = lambda a: jnp.pad(a.reshape(1, GDN_HEADS), ((0, 0), (GDN_HEADS, 128 - 2 * GDN_HEADS)))
    alog, dtb = pad16(P["gdn_a_log"]), pad16(P["gdn_dt_bias"])
    gnw = row(P["gdn_norm_w"])
    residual = lambda acc, res: (res + acc,)

    x0 = x
    h0 = _rms_fwd(x0, mix_nw[0:1], "rms_mix_0")
    w_ie = weight("w_in_even", h0)
    (pe,) = _matmul(h0, w_ie, "nn", outs=[f32], name="in_even")
    y_ret, o_ret, s_ret = _ret_fwd(pe, cos, sin, rtab, "ret_fwd")
    w_lc = weight("lru_conv_w", pe)
    xc = _conv_fwd(pe, 4, w_lc, lru_b, False, "lru_conv_fwd")
    y_lru, h_lru = _lru_fwd(xc, pe, 20, wr_bd, wi_bd, lru_br, lru_bi, lru_lam, "lru_fwd")
    mix0 = jnp.concatenate([y_ret, y_lru], axis=1)
    w_oe = weight("w_out_even", mix0)
    (x1,) = _matmul(mix0, w_oe, "nn", outs=[f32], extras=(x0,), epilogue=residual, name="out_even")
    w_u0, w_d0 = weight("w_up0", x1), weight("w_down0", x1)
    x2, mlp0 = _mlp_fwd(x1, mlp_nw[0:1], w_u0, w_d0, "0")
    h1 = _rms_fwd(x2, mix_nw[1:2], "rms_mix_1")
    w_io = weight("w_in_odd", h1)
    (po,) = _matmul(h1, w_io, "nn", outs=[f32], tn=384, name="in_odd")
    w_gc = weight("gdn_conv_w", po)
    qkv = _conv_fwd(po, 0, w_gc, None, True, "gdn_conv_fwd")
    y_gdn, s_gdn = _gdn_fwd(qkv, po, alog, dtb, gnw, "gdn_fwd")
    w_oo = weight("w_out_odd", y_gdn)
    (x3,) = _matmul(y_gdn, w_oo, "nn", outs=[f32], extras=(x2,), epilogue=residual, name="out_odd")
    w_u1, w_d1 = weight("w_up1", x3), weight("w_down1", x3)
    x4, mlp1 = _mlp_fwd(x3, mlp_nw[1:2], w_u1, w_d1, "1")
    loss, dx4, dx4_b, d_final = _loss_head(x4, row(P["final_norm_w"]), tgt, "loss_head")
    dx3, dx3_b, d_mlp_nw1, d_wu1, d_wd1 = _mlp_bwd(x3, mlp_nw[1:2], w_u1, w_d1, mlp1, dx4, dx4_b, "1")
    tok = sink(dict(w_up1=d_wu1, w_down1=d_wd1))
    (dy_gdn,) = _matmul(dx3_b, w_oo, "nt", outs=[f32], name="out_odd_dx")
    (d_woo,) = _matmul(y_gdn, dx3_b, "tn", outs=[bf16], name="out_odd_dw")
    dqkv, dz, dba, d_alog, d_dtb, d_gnw = _gdn_bwd(qkv, po, alog, dtb, gnw + tok[0:1, :], s_gdn, dy_gdn, "gdn_bwd")
    dqkv_pre, d_gconv, _ = _conv_bwd(po, 0, w_gc, None, True, dqkv, "gdn_conv_bwd")
    dpo = jnp.concatenate([dqkv_pre, dz, dba], axis=1)
    (d_wio,) = _matmul(h1, dpo, "tn", outs=[bf16], tn=384, name="in_odd_dw")
    n_odd = ODD_IN // N_DEV
    tok = sink(dict(w_out_odd=d_woo.reshape(N_DEV, D_MODEL // N_DEV, D_MODEL),
                    w_in_odd=jnp.transpose(d_wio[:, :ODD_IN].reshape(D_MODEL, N_DEV, n_odd), (1, 0, 2))))
    (dh1,) = _matmul(dpo, w_io, "nt", outs=[f32], tm=512, tk=ODD_PAD, name="in_odd_dx")
    dx2, dx2_b, d_mix_nw1 = _rms_bwd(x2, mix_nw[1:2] + tok[0:1, 0:1], dh1, dx3, "rms_mix_bwd_1")
    dx1, dx1_b, d_mlp_nw0, d_wu0, d_wd0 = _mlp_bwd(x1, mlp_nw[0:1], w_u0, w_d0, mlp0, dx2, dx2_b, "0")
    tok = sink(dict(w_up0=d_wu0, w_down0=d_wd0))
    (dmix0,) = _matmul(dx1_b, w_oe, "nt", outs=[f32], name="out_even_dx")
    (d_woe,) = _matmul(mix0, dx1_b, "tn", outs=[bf16], name="out_even_dw")
    d_ret = _ret_bwd(pe, cos, sin, rtab, o_ret, s_ret, dmix0, "ret_bwd")
    dxc, dy_l, d_wr, d_wi, d_br, d_bi, d_lam = _lru_bwd(xc, pe, 20, wr_bd, wi_bd, lru_br, lru_bi, lru_lam + tok[0:1, 0:1],
                                                        h_lru, dmix0, 4, "lru_bwd")
    dx_lru, d_lconv, d_lconv_b = _conv_bwd(pe, 4, w_lc, lru_b, False, dxc, "lru_conv_bwd")
    dpe = jnp.concatenate([d_ret, dx_lru, dy_l], axis=1)
    (d_wie,) = _matmul(h0, dpe, "tn", outs=[bf16], shard_cols=True, name="in_even_dw")
    (dh0,) = _matmul(dpe, w_ie, "nt", outs=[f32], name="in_even_dx")
    dx0, _, d_mix_nw0 = _rms_bwd(x0, mix_nw[0:1], dh0, dx1, "rms_mix_bwd_0")

    G = dict(
        mixer_norm_w=jnp.concatenate([d_mix_nw0, d_mix_nw1], axis=0),
        mlp_norm_w=jnp.concatenate([d_mlp_nw0, d_mlp_nw1], axis=0),
        final_norm_w=d_final.reshape(-1),
        w_in_even=d_wie, lru_conv_w=d_lconv, lru_conv_b=d_lconv_b.reshape(-1),
        lru_w_r=_diag_blocks(d_wr), lru_b_r=d_br.reshape(-1), lru_w_i=_diag_blocks(d_wi), lru_b_i=d_bi.reshape(-1),
        lru_lambda=d_lam.reshape(-1), w_out_even=d_woe.reshape(N_DEV, D_MODEL // N_DEV, D_MODEL), gdn_conv_w=d_gconv,
        gdn_a_log=d_alog[0, GDN_HEADS:2 * GDN_HEADS], gdn_dt_bias=d_dtb[0, GDN_HEADS:2 * GDN_HEADS],
        gdn_norm_w=d_gnw.reshape(-1),
    )
    return loss, dx0, G


_SMALL = ["mixer_norm_w", "mlp_norm_w", "final_norm_w", "lru_conv_b", "lru_w_r", "lru_b_r", "lru_w_i", "lru_b_i",
          "lru_lambda", "gdn_a_log", "gdn_dt_bias", "gdn_norm_w"]
_PACK_ROWS = 688


def _pack(parts):
    flat = jnp.concatenate([p.reshape(-1) for p in parts])
    return jnp.pad(flat, (0, _PACK_ROWS * 128 - flat.shape[0])).reshape(_PACK_ROWS, 128)


def _unpack(packed, shapes):
    flat = packed.reshape(-1)
    out, off = [], 0
    for s in shapes:
        n = int(np.prod(s))
        out.append(flat[off:off + n].reshape(s))
        off += n
    return out


def kernel(x, mixer_norm_w, mlp_norm_w, final_norm_w, w_in_even, lru_conv_w, lru_conv_b, lru_w_r, lru_b_r, lru_w_i, lru_b_i, lru_lambda, w_out_even, w_in_odd, gdn_conv_w, gdn_a_log, gdn_dt_bias, gdn_norm_w, w_out_odd, w_up, w_down, loss_target, m_mixer_norm_w, m_mlp_norm_w, m_final_norm_w, m_w_in_even, m_lru_conv_w, m_lru_conv_b, m_lru_w_r, m_lru_b_r, m_lru_w_i, m_lru_b_i, m_lru_lambda, m_w_out_even, m_w_in_odd, m_gdn_conv_w, m_gdn_a_log, m_gdn_dt_bias, m_gdn_norm_w, m_w_out_odd, m_w_up, m_w_down, v_mixer_norm_w, v_mlp_norm_w, v_final_norm_w, v_w_in_even, v_lru_conv_w, v_lru_conv_b, v_lru_w_r, v_lru_b_r, v_lru_w_i, v_lru_b_i, v_lru_lambda, v_w_out_even, v_w_in_odd, v_gdn_conv_w, v_gdn_a_log, v_gdn_dt_bias, v_gdn_norm_w, v_w_out_odd, v_w_up, v_w_down):
    Pw = dict(mixer_norm_w=mixer_norm_w, mlp_norm_w=mlp_norm_w, final_norm_w=final_norm_w, w_in_even=w_in_even,
              lru_conv_w=lru_conv_w, lru_conv_b=lru_conv_b, lru_w_r=lru_w_r, lru_b_r=lru_b_r, lru_w_i=lru_w_i,
              lru_b_i=lru_b_i, lru_lambda=lru_lambda, w_out_even=w_out_even, w_in_odd=w_in_odd, gdn_conv_w=gdn_conv_w,
              gdn_a_log=gdn_a_log, gdn_dt_bias=gdn_dt_bias, gdn_norm_w=gdn_norm_w, w_out_odd=w_out_odd, w_up=w_up,
              w_down=w_down)
    Pm = dict(mixer_norm_w=m_mixer_norm_w, mlp_norm_w=m_mlp_norm_w, final_norm_w=m_final_norm_w, w_in_even=m_w_in_even,
              lru_conv_w=m_lru_conv_w, lru_conv_b=m_lru_conv_b, lru_w_r=m_lru_w_r, lru_b_r=m_lru_b_r, lru_w_i=m_lru_w_i,
              lru_b_i=m_lru_b_i, lru_lambda=m_lru_lambda, w_out_even=m_w_out_even, w_in_odd=m_w_in_odd,
              gdn_conv_w=m_gdn_conv_w, gdn_a_log=m_gdn_a_log, gdn_dt_bias=m_gdn_dt_bias, gdn_norm_w=m_gdn_norm_w,
              w_out_odd=m_w_out_odd, w_up=m_w_up, w_down=m_w_down)
    Pv = dict(mixer_norm_w=v_mixer_norm_w, mlp_norm_w=v_mlp_norm_w, final_norm_w=v_final_norm_w, w_in_even=v_w_in_even,
              lru_conv_w=v_lru_conv_w, lru_conv_b=v_lru_conv_b, lru_w_r=v_lru_w_r, lru_b_r=v_lru_b_r, lru_w_i=v_lru_w_i,
              lru_b_i=v_lru_b_i, lru_lambda=v_lru_lambda, w_out_even=v_w_out_even, w_in_odd=v_w_in_odd,
              gdn_conv_w=v_gdn_conv_w, gdn_a_log=v_gdn_a_log, gdn_dt_bias=v_gdn_dt_bias, gdn_norm_w=v_gdn_norm_w,
              w_out_odd=v_w_out_odd, w_up=v_w_up, w_down=v_w_down)
    me = _me()[3]
    T = x.shape[1]

    cols = lambda g: jnp.transpose(g, (1, 0, 2)).reshape(g.shape[1], -1)
    rows = lambda g: g.reshape(-1, g.shape[2])
    wide = lambda g: jnp.pad(cols(g), ((0, 0), (0, ODD_PAD - ODD_IN)))
    gather = dict(
        w_in_even=(w_in_even[0].astype(bf16), cols), lru_conv_w=(lru_conv_w[0], cols),
        w_out_even=(w_out_even[0].astype(bf16), rows), w_up0=(w_up[0].astype(bf16), cols), w_down0=(w_down[0].astype(bf16), rows),
        w_in_odd=(w_in_odd[0].astype(bf16), wide), gdn_conv_w=(gdn_conv_w[0], cols),
        w_out_odd=(w_out_odd[0].astype(bf16), rows), w_up1=(w_up[1].astype(bf16), cols), w_down1=(w_down[1].astype(bf16), rows))
    handles, tok = _send_start([s for s, _ in gather.values()], False, "gather_start")
    handles = dict(zip(gather, handles))
    full = {}

    def weight(name, after):
        if name not in full:
            landed = _send_wait(handles[name], False, after, f"gather_wait_{name}")
            shard, finish = gather[name]
            full[name] = finish(lax.dynamic_update_slice_in_dim(landed, shard[None], me, 0))
        return full[name]

    P = {k: Pw[k] for k in ("mlp_norm_w", "final_norm_w")}
    P["mixer_norm_w"] = mixer_norm_w + tok[0:1, 0:1]
    for k in ("lru_w_r", "lru_w_i", "lru_conv_b", "lru_b_r", "lru_b_i", "lru_lambda", "gdn_a_log", "gdn_dt_bias", "gdn_norm_w"):
        P[k] = Pw[k][0]

    sent = {}

    def sink(grads):
        hs, token = _send_start(list(grads.values()), True, "grads_start_" + "_".join(grads))
        for (name, g), h in zip(grads.items(), hs):
            sent[name] = (h, g)
        return token

    loss, dx, G = _local_step(x[0], loss_target[0], P, weight, sink)
    small_g = [G[k].reshape(Pw[k].shape) for k in _SMALL] + [G["lru_conv_w"], G["gdn_conv_w"]]
    packed = _pack(small_g)
    sink(dict(w_out_even=G["w_out_even"], w_in_even=G["w_in_even"],
              small=jnp.broadcast_to(packed[None], (N_DEV,) + packed.shape)))

    def received(name):
        h, g = sent[name]
        landed = _send_wait(h, True, dx, f"grads_wait_{name}")
        return lax.dynamic_update_slice_in_dim(landed, lax.dynamic_slice_in_dim(g, me, 1, 0), me, 0)

    out = {}
    nff = D_FF // N_DEV

    def upd(name, gs, shape2d, layer=None):
        pick = (lambda a: a[layer]) if layer is not None else (lambda a: a)
        return _adamw(pick(Pw[name]).reshape(shape2d), gs, pick(Pm[name]).reshape(shape2d), pick(Pv[name]).reshape(shape2d),
                      f"adamw_{name}" + ("" if layer is None else str(layer)))

    def whole(name, gs, shape2d):
        out[name] = tuple(a.reshape(Pw[name].shape) for a in upd(name, gs, shape2d))

    def layers(name, shape2d):
        per = [upd(name, received(f"{name}{l}"), shape2d, l) for l in range(2)]
        out[name] = tuple(jnp.stack([per[0][j], per[1][j]]) for j in range(4))

    layers("w_up", (D_MODEL, nff))
    layers("w_down", (nff, D_MODEL))
    whole("w_out_odd", received("w_out_odd"), (128, D_MODEL))
    whole("w_in_odd", received("w_in_odd"), (D_MODEL, 514))
    whole("w_out_even", received("w_out_even"), (128, D_MODEL))
    whole("w_in_even", received("w_in_even"), (D_MODEL, 384))
    small_shapes = [Pw[k].shape for k in _SMALL]
    pw, pm, pv = (_pack([Q[k] for k in _SMALL]) for Q in (Pw, Pm, Pv))
    sg, sd, sm, sv = _adamw(pw, received("small"), pm, pv, "adamw_small")
    for arrs_i, packed_out in enumerate((sg, sd, sm, sv)):
        for k, a in zip(_SMALL, _unpack(packed_out, small_shapes)):
            out.setdefault(k, [None] * 4)[arrs_i] = a
    n_small = sum(int(np.prod(s)) for s in small_shapes)
    gflat = sg.reshape(-1)
    g_lconv = gflat[n_small:n_small + CONV_K * LRU_WIDTH].reshape(CONV_K, LRU_WIDTH)
    g_gconv = gflat[n_small + CONV_K * LRU_WIDTH:n_small + CONV_K * (LRU_WIDTH + 3072)].reshape(CONV_K, 3072)
    whole("lru_conv_w", lax.dynamic_slice_in_dim(g_lconv, me * 64, 64, axis=1)[None], (CONV_K, 64))
    whole("gdn_conv_w", lax.dynamic_slice_in_dim(g_gconv, me * 384, 384, axis=1)[None], (CONV_K, 384))

    names = ["mixer_norm_w", "mlp_norm_w", "final_norm_w", "w_in_even", "lru_conv_w", "lru_conv_b", "lru_w_r", "lru_b_r",
             "lru_w_i", "lru_b_i", "lru_lambda", "w_out_even", "w_in_odd", "gdn_conv_w", "gdn_a_log", "gdn_dt_bias",
             "gdn_norm_w", "w_out_odd", "w_up", "w_down"]
    total = lax.psum(loss[0, 0], ("x", "y", "c"))
    res = [total, dx[None]]
    for j in range(4):
        res += [out[k][j] for k in names]
    return tuple(res)
```

```python
import math

import numpy as np
import jax
import jax.numpy as jnp
from jax import lax
from jax.experimental import pallas as pl
from jax.experimental.pallas import tpu as pltpu

f32 = jnp.float32
bf16 = jnp.bfloat16

N_DEV = 8
D_MODEL = 1024
D_FF = 4096
EPS = 1e-6
RET_HEADS = 4
RET_CHUNK = 128
ROPE_THETA = 10000.0
LRU_WIDTH = 512
LRU_C = 8.0
GDN_HEADS = 8
GDN_CHUNK = 64
HEAD_DIM = 128
ODD_IN = 4112
ODD_PAD = 4224
ADAM_LR, ADAM_B1, ADAM_B2, ADAM_EPS, ADAM_WD, ADAM_STEP = 0.001, 0.9, 0.999, 1e-08, 0.01, 10
VMEM_LIMIT = 56 * 1024 * 1024

_NN = (((1,), (0,)), ((), ()))
_NT = (((1,), (1,)), ((), ()))
_TN = (((0,), (0,)), ((), ()))
MESH = pl.DeviceIdType.MESH


def _cparams(sem):
    return pltpu.CompilerParams(dimension_semantics=sem, vmem_limit_bytes=VMEM_LIMIT)


def _dot(a, b, dn):
    return lax.dot_general(a.astype(bf16), b.astype(bf16), dn, preferred_element_type=f32)


def _dot01(a01, b, dn):
    a = a01.astype(bf16)
    b0 = b.astype(bf16)
    r1 = b - b0.astype(f32)
    b1 = r1.astype(bf16)
    b2 = (r1 - b1.astype(f32)).astype(bf16)
    d = lambda q: lax.dot_general(a, q, dn, preferred_element_type=f32)
    return d(b0) + (d(b1) + d(b2))


def _sigmoid(x):
    return jax.nn.sigmoid(x)


def _silu(x):
    return x * _sigmoid(x)


def _dsilu(x):
    s = _sigmoid(x)
    return s * (1.0 + x * (1.0 - s))


def _softplus(x):
    return jnp.maximum(x, 0.0) + jnp.log1p(jnp.exp(-jnp.abs(x)))


_GELU_C = math.sqrt(2.0 / math.pi)


def _gelu(y):
    return 0.5 * y * (1.0 + jnp.tanh(_GELU_C * (y + 0.044715 * y * y * y)))


def _dgelu(y):
    t = jnp.tanh(_GELU_C * (y + 0.044715 * y * y * y))
    return 0.5 * (1.0 + t) + 0.5 * y * (1.0 - t * t) * _GELU_C * (1.0 + 3.0 * 0.044715 * y * y)


def _matmul(a, b, form, *, outs, name, epilogue=None, extras=(), tm=2048, tn=512, tk=1024, shard_cols=0):
    if form == "tn":
        K, M = a.shape
    else:
        M, K = a.shape
    N = b.shape[0] if form == "nt" else b.shape[1]
    ns = N // N_DEV
    if shard_cols:
        tn = ns * shard_cols
    tm, tn, tk = min(tm, M), min(tn, N), min(tk, K)
    assert M % tm == 0 and N % tn == 0 and K % tk == 0, (name, M, N, K, tm, tn, tk)
    nk = K // tk
    dn = {"nn": _NN, "nt": _NT, "tn": _TN}[form]
    if form == "tn":
        a_spec = pl.BlockSpec((tk, tm), lambda i, j, k: (k, i))
    else:
        a_spec = pl.BlockSpec((tm, tk), lambda i, j, k: (i, k))
    if form == "nt":
        b_spec = pl.BlockSpec((tn, tk), lambda i, j, k: (j, k))
    else:
        b_spec = pl.BlockSpec((tk, tn), lambda i, j, k: (k, j))
    e_spec = pl.BlockSpec((tm, tn), lambda i, j, k: (i, j))
    if shard_cols:
        o_spec = pl.BlockSpec((shard_cols, tm, ns), lambda i, j, k: (j, i, 0))
        o_shape = (N_DEV, M, ns)
    else:
        o_spec = e_spec
        o_shape = (M, N)
    n_ex = len(extras)

    def finish(acc, ex, o_refs):
        vals = (acc,) if epilogue is None else epilogue(acc, *[e[...] for e in ex])
        for r, v in zip(o_refs, vals):
            if shard_cols:
                for s in range(shard_cols):
                    r[s] = v[:, s * ns:(s + 1) * ns].astype(r.dtype)
            else:
                r[...] = v.astype(r.dtype)

    def body_one(*refs):
        a_ref, b_ref = refs[:2]
        finish(_dot(a_ref[...], b_ref[...], dn), refs[2:2 + n_ex], refs[2 + n_ex:])

    def body_acc(*refs):
        a_ref, b_ref = refs[:2]
        acc = refs[-1]
        k = pl.program_id(2)

        @pl.when(k == 0)
        def _():
            acc[...] = _dot(a_ref[...], b_ref[...], dn)

        @pl.when((k > 0) & (k < nk - 1))
        def _():
            acc[...] += _dot(a_ref[...], b_ref[...], dn)

        @pl.when(k == nk - 1)
        def _():
            finish(acc[...] + _dot(a_ref[...], b_ref[...], dn), refs[2:2 + n_ex], refs[2 + n_ex:-1])

    return pl.pallas_call(
        body_one if nk == 1 else body_acc, grid=(M // tm, N // tn, nk),
        in_specs=[a_spec, b_spec] + [e_spec] * n_ex,
        out_specs=[o_spec] * len(outs),
        out_shape=[jax.ShapeDtypeStruct(o_shape, d) for d in outs],
        scratch_shapes=[] if nk == 1 else [pltpu.VMEM((tm, tn), f32)],
        compiler_params=_cparams(("parallel", "parallel", "arbitrary")), name=name,
    )(a, b, *extras)


def _rms_fwd(x, w, name):
    T, D = x.shape
    tt = min(512, T)

    def body(x_ref, w_ref, h_ref):
        xv = x_ref[...]
        r = lax.rsqrt(jnp.mean(xv * xv, axis=1, keepdims=True) + EPS)
        h_ref[...] = (xv * r * w_ref[...]).astype(bf16)

    return pl.pallas_call(
        body, grid=(T // tt,),
        in_specs=[pl.BlockSpec((tt, D), lambda i: (i, 0)), pl.BlockSpec((1, D), lambda i: (0, 0))],
        out_specs=pl.BlockSpec((tt, D), lambda i: (i, 0)),
        out_shape=jax.ShapeDtypeStruct((T, D), bf16),
        compiler_params=_cparams(("parallel",)), name=name,
    )(x, w)


def _rms_bwd(x, w, dh, dres, name):
    T, D = x.shape
    tt = min(512, T)

    def body(x_ref, w_ref, dh_ref, dres_ref, dx_ref, dxb_ref, dw_ref):
        @pl.when(pl.program_id(0) == 0)
        def _():
            dw_ref[...] = jnp.zeros_like(dw_ref)

        xv = x_ref[...]
        r = lax.rsqrt(jnp.mean(xv * xv, axis=1, keepdims=True) + EPS)
        xn = xv * r
        dhv = dh_ref[...]
        dhw = dhv * w_ref[...]
        dx = dres_ref[...] + r * (dhw - xn * jnp.mean(dhw * xn, axis=1, keepdims=True))
        dx_ref[...] = dx
        dxb_ref[...] = dx.astype(bf16)
        dw_ref[...] += jnp.sum(dhv * xn, axis=0, keepdims=True)

    row = pl.BlockSpec((tt, D), lambda i: (i, 0))
    vec = pl.BlockSpec((1, D), lambda i: (0, 0))
    return pl.pallas_call(
        body, grid=(T // tt,), in_specs=[row, vec, row, row], out_specs=[row, row, vec],
        out_shape=[jax.ShapeDtypeStruct((T, D), f32), jax.ShapeDtypeStruct((T, D), bf16), jax.ShapeDtypeStruct((1, D), f32)],
        compiler_params=_cparams(("arbitrary",)), name=name,
    )(x, w, dh, dres)


def _loss_head(x, w, tgt, name):
    T, D = x.shape
    tt = min(512, T)

    def body(x_ref, w_ref, t_ref, l_ref, dx_ref, dxb_ref, dw_ref):
        @pl.when(pl.program_id(0) == 0)
        def _():
            dw_ref[...] = jnp.zeros_like(dw_ref)
            l_ref[...] = jnp.zeros_like(l_ref)

        xv = x_ref[...]
        wv = w_ref[...]
        r = lax.rsqrt(jnp.mean(xv * xv, axis=1, keepdims=True) + EPS)
        xn = xv * r
        e = xn * wv - t_ref[...]
        l_ref[...] += 0.5 * jnp.sum(jnp.mean(e * e, axis=1, keepdims=True), axis=0, keepdims=True)
        dy = e * (1.0 / D)
        dyw = dy * wv
        dx = r * (dyw - xn * jnp.mean(dyw * xn, axis=1, keepdims=True))
        dx_ref[...] = dx
        dxb_ref[...] = dx.astype(bf16)
        dw_ref[...] += jnp.sum(dy * xn, axis=0, keepdims=True)

    row = pl.BlockSpec((tt, D), lambda i: (i, 0))
    vec = pl.BlockSpec((1, D), lambda i: (0, 0))
    return pl.pallas_call(
        body, grid=(T // tt,), in_specs=[row, vec, row],
        out_specs=[pl.BlockSpec((1, 128), lambda i: (0, 0)), row, row, vec],
        out_shape=[jax.ShapeDtypeStruct((1, 128), f32), jax.ShapeDtypeStruct((T, D), f32),
                   jax.ShapeDtypeStruct((T, D), bf16), jax.ShapeDtypeStruct((1, D), f32)],
        compiler_params=_cparams(("arbitrary",)), name=name,
    )(x, w, tgt)


def _ret_tables():
    H, C = RET_HEADS, RET_CHUNK
    lg = np.log1p(-np.exp2(-5.0 - np.arange(H, dtype=np.float32))).astype(np.float32)
    idx = np.arange(C, dtype=np.float32)
    diff = idx[:, None] - idx[None, :]
    causal = diff >= 0
    dm = np.where(causal[None], np.exp(lg[:, None, None] * np.where(causal, diff, 0.0)[None]), 0.0)
    qd = np.exp(lg[:, None] * (idx[None, :] + 1.0))
    kd = np.exp(lg[:, None] * (C - 1.0 - idx[None, :]))
    cg = np.exp(lg * C)
    tab = np.zeros((H, 4, C, HEAD_DIM), np.float32)
    tab[:, 0] = dm
    tab[:, 1] = qd[:, :, None]
    tab[:, 2] = kd[:, :, None]
    tab[:, 3] = cg[:, None, None]
    return jnp.asarray(tab)


def _rope_tables(T):
    half = HEAD_DIM // 2
    inv = ROPE_THETA ** (-jnp.arange(half, dtype=f32) / half)
    ang = jnp.arange(T, dtype=jnp.int32).astype(f32)[:, None] * inv[None, :]
    c, s = jnp.cos(ang), jnp.sin(ang)
    return jnp.concatenate([c, c], axis=1), jnp.concatenate([-s, s], axis=1)


def _rope(x, cos, sin):
    return x * cos + pltpu.roll(x, HEAD_DIM // 2, 1) * sin


def _unrope(y, cos, sin):
    return y * cos + pltpu.roll(y * sin, HEAD_DIM // 2, 1)


def _stack_heads(ref, H, f=None):
    parts = [ref[:, h * HEAD_DIM:(h + 1) * HEAD_DIM] for h in range(H)]
    return jnp.stack(parts if f is None else [f(a) for a in parts])


def _ret_fwd(p, cos, sin, tab, name):
    T = p.shape[0]
    C, H = RET_CHUNK, RET_HEADS
    N = T // C
    scale = HEAD_DIM ** -0.5

    def body(q_ref, k_ref, v_ref, g_ref, c_ref, s_ref, t_ref, y_ref, o_ref, sp_ref, st):
        @pl.when(pl.program_id(0) == 0)
        def _():
            st[...] = jnp.zeros_like(st)

        cos_, sin_ = c_ref[...], s_ref[...]
        rot = lambda a: _rope(a, cos_, sin_)
        q = _stack_heads(q_ref, H, rot)
        k = _stack_heads(k_ref, H, rot) * scale
        v = _stack_heads(v_ref, H)
        dm, qd, kd, cg = t_ref[:, 0], t_ref[:, 1], t_ref[:, 2], t_ref[:, 3]
        S = st[...]
        P = _dot(q, k, _NT3) * dm
        o = _dot(P, v, _NN3) + _dot(q * qd, S, _NN3)
        sp_ref[0] = S
        st[...] = cg * S + _dot(k * kd, v, _TN3)
        r = lax.rsqrt(jnp.mean(o * o, axis=2, keepdims=True) + EPS)
        y = o * r * _silu(_stack_heads(g_ref, H))
        for h in range(H):
            o_ref[:, h * HEAD_DIM:(h + 1) * HEAD_DIM] = o[h]
            y_ref[:, h * HEAD_DIM:(h + 1) * HEAD_DIM] = y[h].astype(bf16)

    wide = lambda blk: pl.BlockSpec((C, H * HEAD_DIM), lambda n: (n, blk))
    tbl = pl.BlockSpec((C, HEAD_DIM), lambda n: (n, 0))
    return pl.pallas_call(
        body, grid=(N,),
        in_specs=[wide(0), wide(1), wide(2), wide(3), tbl, tbl,
                  pl.BlockSpec((H, 4, C, HEAD_DIM), lambda n: (0, 0, 0, 0))],
        out_specs=[wide(0), wide(0), pl.BlockSpec((1, H, HEAD_DIM, HEAD_DIM), lambda n: (n, 0, 0, 0))],
        out_shape=[jax.ShapeDtypeStruct((T, H * HEAD_DIM), bf16), jax.ShapeDtypeStruct((T, H * HEAD_DIM), f32),
                   jax.ShapeDtypeStruct((N, H, HEAD_DIM, HEAD_DIM), f32)],
        scratch_shapes=[pltpu.VMEM((H, HEAD_DIM, HEAD_DIM), f32)],
        compiler_params=_cparams(("arbitrary",)), name=name,
    )(p, p, p, p, cos, sin, tab)


def _ret_bwd(p, cos, sin, tab, o_raw, sprev, dmix, name):
    T = p.shape[0]
    C, H = RET_CHUNK, RET_HEADS
    N = T // C
    scale = HEAD_DIM ** -0.5

    W = H * HEAD_DIM

    def body(q_ref, k_ref, v_ref, g_ref, c_ref, s_ref, t_ref, o_ref, sp_ref, dy_ref, d_ref, dst):
        @pl.when(pl.program_id(0) == 0)
        def _():
            dst[...] = jnp.zeros_like(dst)

        cos_, sin_ = c_ref[...], s_ref[...]
        rot = lambda a: _rope(a, cos_, sin_)
        q = _stack_heads(q_ref, H, rot)
        k = _stack_heads(k_ref, H, rot) * scale
        v = _stack_heads(v_ref, H)
        g = _stack_heads(g_ref, H)
        dm, qd, kd, cg = t_ref[:, 0], t_ref[:, 1], t_ref[:, 2], t_ref[:, 3]
        S = sp_ref[0]
        o = _stack_heads(o_ref, H)
        dy = _stack_heads(dy_ref, H)
        r = lax.rsqrt(jnp.mean(o * o, axis=2, keepdims=True) + EPS)
        nrm = o * r
        dn = dy * _silu(g)
        dg = dy * nrm * _dsilu(g)
        do = r * (dn - nrm * jnp.mean(dn * nrm, axis=2, keepdims=True))
        dS1 = dst[...]
        P = _dot(q, k, _NT3) * dm
        dP = _dot(do, v, _NT3) * dm
        dq = _dot(dP, k, _NN3) + _dot(do, S, _NT3) * qd
        dk = (_dot(dP, q, _TN3) + _dot(v, dS1, _NT3) * kd) * scale
        dv = _dot(P, do, _TN3) + _dot(k * kd, dS1, _NN3)
        dst[...] = cg * dS1 + _dot(q * qd, do, _TN3)
        for h in range(H):
            d_ref[:, h * HEAD_DIM:(h + 1) * HEAD_DIM] = _unrope(dq[h], cos_, sin_).astype(bf16)
            d_ref[:, W + h * HEAD_DIM:W + (h + 1) * HEAD_DIM] = _unrope(dk[h], cos_, sin_).astype(bf16)
            d_ref[:, 2 * W + h * HEAD_DIM:2 * W + (h + 1) * HEAD_DIM] = dv[h].astype(bf16)
            d_ref[:, 3 * W + h * HEAD_DIM:3 * W + (h + 1) * HEAD_DIM] = dg[h].astype(bf16)

    rev = lambda blk: pl.BlockSpec((C, W), lambda n: (N - 1 - n, blk))
    tbl = pl.BlockSpec((C, HEAD_DIM), lambda n: (N - 1 - n, 0))
    return pl.pallas_call(
        body, grid=(N,),
        in_specs=[rev(0), rev(1), rev(2), rev(3), tbl, tbl,
                  pl.BlockSpec((H, 4, C, HEAD_DIM), lambda n: (0, 0, 0, 0)), rev(0),
                  pl.BlockSpec((1, H, HEAD_DIM, HEAD_DIM), lambda n: (N - 1 - n, 0, 0, 0)), rev(0)],
        out_specs=pl.BlockSpec((C, 4 * W), lambda n: (N - 1 - n, 0)),
        out_shape=jax.ShapeDtypeStruct((T, 4 * W), bf16),
        scratch_shapes=[pltpu.VMEM((H, HEAD_DIM, HEAD_DIM), f32)],
        compiler_params=_cparams(("arbitrary",)), name=name,
    )(p, p, p, p, cos, sin, tab, o_raw, sprev, dmix)


CONV_K = 4
CONV_W = 512
PAD = 8


def _conv_fwd(x, col_off, w, b, act, name):
    T = x.shape[0]
    C = w.shape[1]
    G = C // CONV_W
    tt = min(512, T)
    NT = T // tt
    has_b = b is not None

    def body(*refs):
        if has_b:
            x_ref, w_ref, b_ref, y_ref, pad = refs
        else:
            x_ref, w_ref, y_ref, pad = refs
        t = pl.program_id(1)

        @pl.when(t == 0)
        def _():
            pad[pl.ds(0, PAD), :] = jnp.zeros((PAD, CONV_W), f32)

        pad[pl.ds(PAD, tt), :] = x_ref[...]
        wv = w_ref[...]
        y = wv[0:1, :] * pad[pl.ds(PAD - 3, tt), :]
        for kk in range(1, CONV_K):
            y = y + wv[kk:kk + 1, :] * pad[pl.ds(PAD - 3 + kk, tt), :]
        if has_b:
            y = y + b_ref[...]
        tail = pad[pl.ds(tt, PAD), :]
        pad[pl.ds(0, PAD), :] = tail
        y_ref[...] = _silu(y) if act else y

    in_specs = [pl.BlockSpec((tt, CONV_W), lambda g, t: (t, col_off + g)),
                pl.BlockSpec((CONV_K, CONV_W), lambda g, t: (0, g))]
    args = [x, w]
    if has_b:
        in_specs.append(pl.BlockSpec((1, CONV_W), lambda g, t: (0, g)))
        args.append(b)
    return pl.pallas_call(
        body, grid=(G, NT), in_specs=in_specs,
        out_specs=pl.BlockSpec((tt, CONV_W), lambda g, t: (t, g)),
        out_shape=jax.ShapeDtypeStruct((T, C), f32),
        scratch_shapes=[pltpu.VMEM((tt + PAD, CONV_W), f32)],
        compiler_params=_cparams(("parallel", "arbitrary")), name=name,
    )(*args)


def _conv_bwd(x, col_off, w, b, act, dout, name):
    T = x.shape[0]
    C = w.shape[1]
    G = C // CONV_W
    tt = min(512, T)
    NT = T // tt
    has_b = b is not None

    def body(*refs):
        if has_b:
            x_ref, xp_ref, w_ref, b_ref, d_ref, dx_ref, dw_ref, db_ref, pad, dpad = refs
        else:
            x_ref, xp_ref, w_ref, d_ref, dx_ref, dw_ref, db_ref, pad, dpad = refs
        t = pl.program_id(1)
        first_tile = t == NT - 1

        @pl.when(t == 0)
        def _():
            dpad[pl.ds(tt, PAD), :] = jnp.zeros((PAD, CONV_W), f32)
            dw_ref[...] = jnp.zeros_like(dw_ref)
            db_ref[...] = jnp.zeros_like(db_ref)

        pad[pl.ds(0, PAD), :] = jnp.where(first_tile, 0.0, xp_ref[...])
        pad[pl.ds(PAD, tt), :] = x_ref[...]
        wv = w_ref[...]
        dy = d_ref[...]
        if act:
            y = wv[0:1, :] * pad[pl.ds(PAD - 3, tt), :]
            for kk in range(1, CONV_K):
                y = y + wv[kk:kk + 1, :] * pad[pl.ds(PAD - 3 + kk, tt), :]
            if has_b:
                y = y + b_ref[...]
            dy = dy * _dsilu(y)
        dpad[pl.ds(0, tt), :] = dy
        dx = wv[3:4, :] * dy
        for j in range(1, CONV_K):
            dx = dx + wv[3 - j:4 - j, :] * dpad[pl.ds(j, tt), :]
        dx_ref[...] = dx.astype(bf16)
        head = dpad[pl.ds(0, PAD), :]
        dpad[pl.ds(tt, PAD), :] = head
        for kk in range(CONV_K):
            dw_ref[kk:kk + 1, :] += jnp.sum(dy * pad[pl.ds(PAD - 3 + kk, tt), :], axis=0, keepdims=True)
        db_ref[...] += jnp.sum(dy, axis=0, keepdims=True)

    rows8 = tt // PAD
    in_specs = [pl.BlockSpec((tt, CONV_W), lambda g, t: (NT - 1 - t, col_off + g)),
                pl.BlockSpec((PAD, CONV_W), lambda g, t: (jnp.maximum((NT - 1 - t) * rows8 - 1, 0), col_off + g)),
                pl.BlockSpec((CONV_K, CONV_W), lambda g, t: (0, g))]
    args = [x, x, w]
    if has_b:
        in_specs.append(pl.BlockSpec((1, CONV_W), lambda g, t: (0, g)))
        args.append(b)
    in_specs.append(pl.BlockSpec((tt, CONV_W), lambda g, t: (NT - 1 - t, g)))
    args.append(dout)
    return pl.pallas_call(
        body, grid=(G, NT), in_specs=in_specs,
        out_specs=[pl.BlockSpec((tt, CONV_W), lambda g, t: (NT - 1 - t, g)),
                   pl.BlockSpec((CONV_K, CONV_W), lambda g, t: (0, g)),
                   pl.BlockSpec((1, CONV_W), lambda g, t: (0, g))],
        out_shape=[jax.ShapeDtypeStruct((T, C), bf16), jax.ShapeDtypeStruct((CONV_K, C), f32),
                   jax.ShapeDtypeStruct((1, C), f32)],
        scratch_shapes=[pltpu.VMEM((tt + PAD, CONV_W), f32), pltpu.VMEM((tt + PAD, CONV_W), f32)],
        compiler_params=_cparams(("parallel", "arbitrary")), name=name,
    )(*args)


def _lru_gates(xc, wr, wi, br, bi, lam):
    r = _sigmoid(_dot(xc, wr, _NN) + br)
    i = _sigmoid(_dot(xc, wi, _NN) + bi)
    sp = _softplus(-lam)
    a = jnp.exp(-LRU_C * r * sp)
    mult = jnp.sqrt(1.0 - a * a)
    return r, i, sp, a, mult


def _lru_fwd(xc, p, y_off, wr, wi, br, bi, lam, name):
    T = xc.shape[0]
    G = LRU_WIDTH // 128
    tt = min(512, T)
    NT = T // tt

    def body(x_ref, y_ref, wr_ref, wi_ref, br_ref, bi_ref, l_ref, o_ref, h_ref, hc):
        t = pl.program_id(1)

        @pl.when(t == 0)
        def _():
            hc[...] = jnp.zeros_like(hc)

        x = x_ref[...]
        r, i, sp, a, mult = _lru_gates(x, wr_ref[...], wi_ref[...], br_ref[...], bi_ref[...], l_ref[...])
        row = lax.broadcasted_iota(jnp.int32, (tt, 128), 0)
        mult = jnp.where((row == 0) & (t == 0), 1.0, mult)
        U = x * i * mult
        A = a
        d = 1
        while d < tt:
            keep = row >= d
            Ush = jnp.where(keep, pltpu.roll(U, d, 0), 0.0)
            Ash = jnp.where(keep, pltpu.roll(A, d, 0), 1.0)
            U = A * Ush + U
            A = A * Ash
            d *= 2
        h = U + A * hc[0:1, :]
        h_ref[...] = h
        hc[...] = jnp.broadcast_to(h[tt - 1:tt, :], hc.shape)
        o_ref[...] = (h * _gelu(y_ref[...])).astype(bf16)

    tile = pl.BlockSpec((tt, 128), lambda g, t: (t, g))
    vec = pl.BlockSpec((1, 128), lambda g, t: (0, g))
    wsp = pl.BlockSpec((128, 128), lambda g, t: (g, g))
    return pl.pallas_call(
        body, grid=(G, NT),
        in_specs=[tile, pl.BlockSpec((tt, 128), lambda g, t: (t, y_off + g)), wsp, wsp, vec, vec, vec],
        out_specs=[tile, tile],
        out_shape=[jax.ShapeDtypeStruct((T, LRU_WIDTH), bf16), jax.ShapeDtypeStruct((T, LRU_WIDTH), f32)],
        scratch_shapes=[pltpu.VMEM((8, 128), f32)],
        compiler_params=_cparams(("parallel", "arbitrary")), name=name,
    )(xc, p, wr, wi, br, bi, lam)


def _lru_bwd(xc, p, y_off, wr, wi, br, bi, lam, hs, dmix, d_off, name):
    T = xc.shape[0]
    G = LRU_WIDTH // 128
    tt = min(512, T)
    NT = T // tt

    def body(x_ref, y_ref, wr_ref, wi_ref, br_ref, bi_ref, l_ref, h_ref, hp_ref, do_ref,
             dx_ref, dy_ref, dwr_ref, dwi_ref, dbr_ref, dbi_ref, dl_ref, lc, an):
        t = pl.program_id(1)
        first_tile = t == NT - 1

        @pl.when(t == 0)
        def _():
            lc[...] = jnp.zeros_like(lc)
            an[...] = jnp.zeros_like(an)
            dwr_ref[...] = jnp.zeros_like(dwr_ref)
            dwi_ref[...] = jnp.zeros_like(dwi_ref)
            dbr_ref[...] = jnp.zeros_like(dbr_ref)
            dbi_ref[...] = jnp.zeros_like(dbi_ref)
            dl_ref[...] = jnp.zeros_like(dl_ref)

        x = x_ref[...]
        y = y_ref[...]
        wr, wi, lam_ = wr_ref[...], wi_ref[...], l_ref[...]
        r, i, sp, a, mult_raw = _lru_gates(x, wr, wi, br_ref[...], bi_ref[...], lam_)
        row = lax.broadcasted_iota(jnp.int32, (tt, 128), 0)
        t0 = (row == 0) & first_tile
        mult = jnp.where(t0, 1.0, mult_raw)
        h = h_ref[...]
        do = do_ref[...]
        dh = do * _gelu(y)
        dy_ref[...] = (do * h * _dgelu(y)).astype(bf16)
        B = jnp.where(row == tt - 1, an[0:1, :], pltpu.roll(a, tt - 1, 0))
        L = dh
        d = 1
        while d < tt:
            keep = row < tt - d
            Lsh = jnp.where(keep, pltpu.roll(L, tt - d, 0), 0.0)
            Bsh = jnp.where(keep, pltpu.roll(B, tt - d, 0), 1.0)
            L = L + B * Lsh
            B = B * Bsh
            d *= 2
        L = L + B * lc[0:1, :]
        lc[...] = jnp.broadcast_to(L[0:1, :], lc.shape)
        an[...] = jnp.broadcast_to(a[0:1, :], an.shape)
        hprev = jnp.where(first_tile, 0.0, hp_ref[...])[PAD - 1:PAD, :]
        hm1 = jnp.where(row == 0, hprev, pltpu.roll(h, 1, 0))
        da = L * hm1
        dxc = L * i * mult
        di = L * x * mult
        dmult = jnp.where(t0, 0.0, L * x * i)
        da = da - jnp.where(t0, 0.0, dmult * a / mult_raw)
        dlog_a = da * a
        dr = dlog_a * (-LRU_C) * sp
        dsp = jnp.sum(dlog_a * (-LRU_C) * r, axis=0, keepdims=True)
        dpr = dr * r * (1.0 - r)
        dpi = di * i * (1.0 - i)
        dx_ref[...] = dxc + _dot(dpr, wr, _NT) + _dot(dpi, wi, _NT)
        dwr_ref[0] += _dot(x, dpr, _TN)
        dwi_ref[0] += _dot(x, dpi, _TN)
        dbr_ref[...] += jnp.sum(dpr, axis=0, keepdims=True)
        dbi_ref[...] += jnp.sum(dpi, axis=0, keepdims=True)
        dl_ref[...] += dsp * (-_sigmoid(-lam_))

    rows8 = tt // PAD
    tile = pl.BlockSpec((tt, 128), lambda g, t: (NT - 1 - t, g))
    vec = pl.BlockSpec((1, 128), lambda g, t: (0, g))
    wsp = pl.BlockSpec((128, 128), lambda g, t: (g, g))
    wout = pl.BlockSpec((1, 128, 128), lambda g, t: (g, 0, 0))
    return pl.pallas_call(
        body, grid=(G, NT),
        in_specs=[tile, pl.BlockSpec((tt, 128), lambda g, t: (NT - 1 - t, y_off + g)), wsp, wsp, vec, vec, vec, tile,
                  pl.BlockSpec((PAD, 128), lambda g, t: (jnp.maximum((NT - 1 - t) * rows8 - 1, 0), g)),
                  pl.BlockSpec((tt, 128), lambda g, t: (NT - 1 - t, d_off + g))],
        out_specs=[tile, tile, wout, wout, vec, vec, vec],
        out_shape=[jax.ShapeDtypeStruct((T, LRU_WIDTH), f32), jax.ShapeDtypeStruct((T, LRU_WIDTH), bf16),
                   jax.ShapeDtypeStruct((G, 128, 128), f32), jax.ShapeDtypeStruct((G, 128, 128), f32),
                   jax.ShapeDtypeStruct((1, LRU_WIDTH), f32), jax.ShapeDtypeStruct((1, LRU_WIDTH), f32),
                   jax.ShapeDtypeStruct((1, LRU_WIDTH), f32)],
        scratch_shapes=[pltpu.VMEM((8, 128), f32), pltpu.VMEM((8, 128), f32)],
        compiler_params=_cparams(("parallel", "arbitrary")), name=name,
    )(xc, p, wr, wi, br, bi, lam, hs, hs, dmix)


_NN3 = (((2,), (1,)), ((0,), (0,)))
_NT3 = (((2,), (2,)), ((0,), (0,)))
_TN3 = (((1,), (1,)), ((0,), (0,)))


def _heads(ref):
    return _stack_heads(ref, GDN_HEADS)


def _rowsum(x):
    H, C, L = x.shape
    return _dot(x.reshape(H * C, L), jnp.ones((L, HEAD_DIM), f32), _NN).reshape(H, C, HEAD_DIM)


def _gdn_chunk(qr, kr, v, ba, alog, dtb, S, saved=None):
    C, H = GDN_CHUNK, GDN_HEADS
    lane = lax.broadcasted_iota(jnp.int32, (C, 128), 1)
    lane3 = lax.broadcasted_iota(jnp.int32, (H, C, 128), 2)
    ri = lax.broadcasted_iota(jnp.int32, (C, C), 0)
    ci = lax.broadcasted_iota(jnp.int32, (C, C), 1)
    rowc = lax.broadcasted_iota(jnp.int32, (C, 1), 0)
    col = lambda m, j: jnp.sum(jnp.where(lane == j, m, 0.0), axis=1, keepdims=True)
    cols = lambda m, off: jnp.stack([col(m, off + h) for h in range(H)])
    ea = jnp.exp(alog)
    g_all = -ea * _softplus(ba + dtb)
    tri = (ri >= ci).astype(f32)
    G_all = _dot01(tri, g_all, _NN)
    beta = cols(_sigmoid(ba), 0)
    Gc = cols(G_all, H)
    rq = lax.rsqrt(_rowsum(qr * qr) + EPS)
    rk = lax.rsqrt(_rowsum(kr * kr) + EPS)
    qh, kn = qr * rq, kr * rk
    qn = qh * (HEAD_DIM ** -0.5)
    Grow = _dot01(jnp.ones((H, C, 128), f32), jnp.where(lane3 == 0, Gc, 0.0), _NT3)
    incl = ri >= ci
    Di = jnp.where(incl, jnp.exp(jnp.where(incl, Gc - Grow, 0.0)), 0.0)
    Ds = jnp.where(ri > ci, Di, 0.0)
    Gl = jnp.sum(jnp.where(rowc == C - 1, Gc, 0.0), axis=1, keepdims=True)
    eG = jnp.exp(Gc)
    eGl = jnp.exp(Gl - Gc)
    cd = jnp.exp(Gl)
    kb = kn * beta
    vb = v * beta
    Lm = _dot(kb, kn, _NT3) * Ds
    kbg = kb * eG
    QK = _dot(qn, kn, _NT3) * Di
    qg = qn * eG
    kg = kn * eGl
    if saved is None:
        same = lambda s: (ri // s) == (ci // s)
        Xd = jnp.where(same(8), -Lm, 0.0)
        Tinv = (ri == ci).astype(f32) + Xd
        Pw = Xd
        for _ in range(2):
            Pw = _dot(Pw, Pw, _NN3)
            Tinv = Tinv + _dot(Tinv, Pw, _NN3)
        for s in (8, 16, 32):
            off = jnp.where(same(2 * s) & jnp.logical_not(same(s)), Lm, 0.0)
            Tinv = Tinv - _dot(_dot(Tinv, off, _NN3), Tinv, _NN3)
        w = _dot(Tinv, kbg, _NN3)
        vn = _dot(Tinv, vb, _NN3) - _dot(w, S, _NN3)
        o = _dot(qg, S, _NN3) + _dot(QK, vn, _NN3)
        S1 = S * cd + _dot(kg, vn, _TN3)
    else:
        Tinv, vn, o = saved
        w = _dot(Tinv, kbg, _NN3)
        S1 = None
    return dict(beta=beta, g_all=g_all, rq=rq, rk=rk, qh=qh, kn=kn, qn=qn, Di=Di, Ds=Ds, eG=eG, eGl=eGl, cd=cd,
                kb=kb, vb=vb, Lm=Lm, Tinv=Tinv, kbg=kbg, w=w, QK=QK, qg=qg, kg=kg, vn=vn, o=o, S1=S1,
                lane=lane, ri=ri, ci=ci, rowc=rowc, ea=ea)


def _gdn_specs(N, rev):
    C = GDN_CHUNK
    H = GDN_HEADS
    nn = (lambda n: N - 1 - n) if rev else (lambda n: n)
    wide = lambda blk: pl.BlockSpec((C, H * HEAD_DIM), lambda n: (nn(n), blk))
    one = lambda off: pl.BlockSpec((C, HEAD_DIM), lambda n: (nn(n), off))
    vec = pl.BlockSpec((1, 128), lambda n: (0, 0))
    st = pl.BlockSpec((1, H, HEAD_DIM, HEAD_DIM), lambda n: (nn(n), 0, 0, 0))
    return wide, one, vec, st


def _gdn_fwd(qkv, p, alog, dtb, nw, name):
    T = qkv.shape[0]
    C, H = GDN_CHUNK, GDN_HEADS
    N = T // C
    wide, one, vec, stspec = _gdn_specs(N, False)

    def body(q_ref, k_ref, v_ref, z_ref, ba_ref, al_ref, dt_ref, nw_ref, y_ref, sp_ref, ti_ref, vn_ref, o_ref, st):
        @pl.when(pl.program_id(0) == 0)
        def _():
            st[...] = jnp.zeros_like(st)

        S = st[...]
        f = _gdn_chunk(_heads(q_ref), _heads(k_ref), _heads(v_ref), ba_ref[...], al_ref[...], dt_ref[...], S)
        sp_ref[0] = S
        st[...] = f["S1"]
        ti_ref[0] = f["Tinv"]
        o, vn = f["o"], f["vn"]
        r = lax.rsqrt(_rowsum(o * o) * (1.0 / HEAD_DIM) + EPS)
        y = o * r * nw_ref[...] * _silu(_heads(z_ref))
        for h in range(H):
            sl = slice(h * HEAD_DIM, (h + 1) * HEAD_DIM)
            y_ref[:, sl] = y[h].astype(bf16)
            vn_ref[:, sl] = vn[h]
            o_ref[:, sl] = o[h]

    wide_f32 = jax.ShapeDtypeStruct((T, H * HEAD_DIM), f32)
    return pl.pallas_call(
        body, grid=(N,),
        in_specs=[wide(0), wide(1), wide(2), wide(3), one(4 * H), vec, vec, vec],
        out_specs=[wide(0), stspec, pl.BlockSpec((1, H, C, C), lambda n: (n, 0, 0, 0)), wide(0), wide(0)],
        out_shape=[jax.ShapeDtypeStruct((T, H * HEAD_DIM), bf16), jax.ShapeDtypeStruct((N, H, HEAD_DIM, HEAD_DIM), f32),
                   jax.ShapeDtypeStruct((N, H, C, C), f32), wide_f32, wide_f32],
        scratch_shapes=[pltpu.VMEM((H, HEAD_DIM, HEAD_DIM), f32)],
        compiler_params=_cparams(("arbitrary",)), name=name,
    )(qkv, qkv, qkv, p, p, alog, dtb, nw)


def _gdn_bwd(qkv, p, alog, dtb, nw, sprev, tinv, vn_all, o_all, dy_all, name):
    T = qkv.shape[0]
    C, H = GDN_CHUNK, GDN_HEADS
    N = T // C
    wide, one, vec, stspec = _gdn_specs(N, True)
    rs = lambda m: jnp.sum(m, axis=2, keepdims=True)

    def put(ref, val, col=0):
        for h in range(H):
            ref[:, col + h * HEAD_DIM:col + (h + 1) * HEAD_DIM] = val[h].astype(ref.dtype)

    def body(q_ref, k_ref, v_ref, z_ref, ba_ref, al_ref, dt_ref, nw_ref, sp_ref, ti_ref, vn_ref, o_ref, dy_ref,
             dqkv_ref, dz_ref, dba_ref, dal_ref, ddt_ref, dnw_ref, dst):
        @pl.when(pl.program_id(0) == 0)
        def _():
            dst[...] = jnp.zeros_like(dst)
            dal_ref[...] = jnp.zeros_like(dal_ref)
            ddt_ref[...] = jnp.zeros_like(ddt_ref)
            dnw_ref[...] = jnp.zeros_like(dnw_ref)

        ba, alog, dtb_, nwv = ba_ref[...], al_ref[...], dt_ref[...], nw_ref[...]
        v = _heads(v_ref)
        S = sp_ref[0]
        f = _gdn_chunk(_heads(q_ref), _heads(k_ref), v, ba, alog, dtb_, S, saved=(ti_ref[0], _heads(vn_ref), _heads(o_ref)))
        beta, kn, qn, kb, vb, Tinv, kbg = f["beta"], f["kn"], f["qn"], f["kb"], f["vb"], f["Tinv"], f["kbg"]
        eG, eGl, cd, Di, Ds, QK, vn, w_, qg, kg = (f["eG"], f["eGl"], f["cd"], f["Di"], f["Ds"], f["QK"], f["vn"],
                                                    f["w"], f["qg"], f["kg"])
        lane, ri, ci, rowc = f["lane"], f["ri"], f["ci"], f["rowc"]
        o = f["o"]
        z = _heads(z_ref)
        dy = _heads(dy_ref)
        r = lax.rsqrt(_rowsum(o * o) * (1.0 / HEAD_DIM) + EPS)
        nrm = o * r
        sz = _silu(z)
        dn = dy * nwv * sz
        put(dz_ref, dy * nrm * nwv * _dsilu(z))
        dnw_ref[...] += jnp.sum(jnp.sum(dy * nrm * sz, axis=0), axis=0, keepdims=True)
        do = r * (dn - nrm * (_rowsum(dn * nrm) * (1.0 / HEAD_DIM)))
        dS1 = dst[...]
        dcd = jnp.sum(jnp.sum(S * dS1, axis=2, keepdims=True), axis=1, keepdims=True)
        dkg = _dot(vn, dS1, _NT3)
        dvn = _dot(kg, dS1, _NN3) + _dot(QK, do, _TN3)
        dqg = _dot(do, S, _NT3)
        dQK = _dot(do, vn, _NT3)
        dw = -_dot(dvn, S, _NT3)
        dst[...] = cd * dS1 + _dot(qg, do, _TN3) - _dot(w_, dvn, _TN3)
        dqn = dqg * eG
        dkn = dkg * eGl
        deGl = rs(dkg * kn)
        dQKr = dQK * Di
        E = dQK * QK
        dqn = dqn + _dot(dQKr, kn, _NN3)
        dkn = dkn + _dot(dQKr, qn, _TN3)
        dT = _dot(dvn, vb, _NT3) + _dot(dw, kbg, _NT3)
        dvb = _dot(Tinv, dvn, _TN3)
        dkbg = _dot(Tinv, dw, _TN3)
        dkb = dkbg * eG
        deG = rs(dqg * qn + dkbg * kb)
        dL = -_dot(_dot(Tinv, dT, _TN3), Tinv, _NT3)
        dKK = dL * Ds
        E = E + dL * f["Lm"]
        dkb = dkb + _dot(dKK, kn, _NN3)
        dkn = dkn + _dot(dKK, kb, _TN3) + dkb * beta
        dbeta = rs(dkb * kn + dvb * v)
        put(dqkv_ref, dvb * beta, 2 * H * HEAD_DIM)
        dG = rs(E) - rs(jnp.swapaxes(E, 1, 2)) + deG * eG - deGl * eGl
        dGl = jnp.sum(deGl * eGl, axis=1, keepdims=True) + dcd * cd
        dG = dG + jnp.where(rowc == C - 1, dGl, 0.0)
        qh = f["qh"]
        put(dqkv_ref, (HEAD_DIM ** -0.5) * f["rq"] * (dqn - qh * _rowsum(dqn * qh)))
        put(dqkv_ref, f["rk"] * (dkn - kn * _rowsum(dkn * kn)), H * HEAD_DIM)
        db = dbeta * beta * (1.0 - beta)
        db_all = jnp.where(lane == 0, db[0], 0.0)
        dG_all = jnp.where(lane == H, dG[0], 0.0)
        for h in range(1, H):
            db_all = db_all + jnp.where(lane == h, db[h], 0.0)
            dG_all = dG_all + jnp.where(lane == H + h, dG[h], 0.0)
        triu = (ri <= ci).astype(f32)
        dg_all = _dot01(triu, dG_all, _NN)
        da_all = dg_all * (-f["ea"]) * _sigmoid(ba + dtb_)
        dba_ref[...] = (db_all + da_all).astype(bf16)
        ddt_ref[...] += jnp.sum(da_all, axis=0, keepdims=True)
        dal_ref[...] += jnp.sum(dg_all * f["g_all"], axis=0, keepdims=True)

    small = jax.ShapeDtypeStruct((1, 128), f32)
    return pl.pallas_call(
        body, grid=(N,),
        in_specs=[wide(0), wide(1), wide(2), wide(3), one(4 * H), vec, vec, vec, stspec,
                  pl.BlockSpec((1, H, C, C), lambda n: (N - 1 - n, 0, 0, 0)), wide(0), wide(0), wide(0)],
        out_specs=[pl.BlockSpec((C, 3 * H * HEAD_DIM), lambda n: (N - 1 - n, 0)), wide(0), one(0), vec, vec, vec],
        out_shape=[jax.ShapeDtypeStruct((T, 3 * H * HEAD_DIM), f32), jax.ShapeDtypeStruct((T, H * HEAD_DIM), bf16),
                   jax.ShapeDtypeStruct((T, 128), bf16), small, small, small],
        scratch_shapes=[pltpu.VMEM((H, HEAD_DIM, HEAD_DIM), f32)],
        compiler_params=_cparams(("arbitrary",)), name=name,
    )(qkv, qkv, qkv, p, p, alog, dtb, nw, sprev, tinv, vn_all, o_all, dy_all)


def _adamw(w, gs, m, v, name):
    R, Cc = w.shape
    S = gs.shape[0]
    tr = R
    for cand in (256, 128, 64, 32, 16, 8):
        if R % cand == 0 and R > cand:
            tr = cand
            break
    c1 = 1.0 - ADAM_B1 ** ADAM_STEP
    c2 = 1.0 - ADAM_B2 ** ADAM_STEP

    def body(w_ref, g_ref, m_ref, v_ref, go_ref, d_ref, mo_ref, vo_ref):
        g = g_ref[0].astype(f32)
        for s in range(1, S):
            g = g + g_ref[s].astype(f32)
        mn = ADAM_B1 * m_ref[...] + (1.0 - ADAM_B1) * g
        vn = ADAM_B2 * v_ref[...] + (1.0 - ADAM_B2) * (g * g)
        go_ref[...] = g
        mo_ref[...] = mn
        vo_ref[...] = vn
        d_ref[...] = -ADAM_LR * ((mn / c1) / (jnp.sqrt(vn / c2) + ADAM_EPS) + ADAM_WD * w_ref[...])

    blk = pl.BlockSpec((tr, Cc), lambda i: (i, 0))
    out = jax.ShapeDtypeStruct((R, Cc), f32)
    return pl.pallas_call(
        body, grid=(R // tr,),
        in_specs=[blk, pl.BlockSpec((S, tr, Cc), lambda i: (0, i, 0)), blk, blk],
        out_specs=[blk] * 4, out_shape=[out] * 4,
        compiler_params=_cparams(("parallel",)), name=name,
    )(w, gs, m, v)


def _me():
    x, y, c = lax.axis_index("x"), lax.axis_index("y"), lax.axis_index("c")
    return x, y, c, 4 * x + 2 * y + c


def _peer(k):
    x, y, c, _ = _me()
    px = 1 - x if k & 4 else x
    py = 1 - y if k & 2 else y
    pc = 1 - c if k & 1 else c
    return (px, py, pc), 4 * px + 2 * py + pc


_HBM = pl.BlockSpec(memory_space=pltpu.HBM)
_SEM = pl.BlockSpec(memory_space=pltpu.SEMAPHORE)
_EFFECT = pltpu.SideEffectType.DATAFLOW_SIDE_EFFECTING


def _copy(src, land, ssem, rsem, k, blocked, landing_slot_of_peer):
    pid, pidx = _peer(k)
    slot = pidx if landing_slot_of_peer else _me()[3]
    return pltpu.make_async_remote_copy(src_ref=src.at[pidx] if blocked else src, dst_ref=land.at[slot],
                                        send_sem=ssem.at[k - 1], recv_sem=rsem.at[k - 1], device_id=pid, device_id_type=MESH)


def _send_start(srcs, blocked, name):
    n = len(srcs)
    lands = [lax.empty(a.shape if blocked else (N_DEV,) + a.shape, a.dtype) for a in srcs]

    def body(*refs):
        src, land, sems, token = refs[:n], refs[n:2 * n], refs[2 * n:4 * n], refs[-1]
        for i in range(n):
            for k in range(1, N_DEV):
                _copy(src[i], land[i], sems[2 * i], sems[2 * i + 1], k, blocked, False).start()
        token[...] = jnp.zeros_like(token)

    sem = pltpu.SemaphoreType.DMA((N_DEV - 1,))
    hbm = lambda a: pltpu.with_memory_space_constraint(a, pltpu.HBM)
    res = pl.pallas_call(
        body, name=name,
        out_shape=tuple([sem] * (2 * n)) + tuple(pltpu.HBM(a.shape, a.dtype) for a in srcs + lands)
        + (jax.ShapeDtypeStruct((8, 128), f32),),
        in_specs=(_HBM,) * (2 * n),
        out_specs=(_SEM,) * (2 * n) + (_HBM,) * (2 * n) + (pl.BlockSpec(memory_space=pltpu.VMEM),),
        input_output_aliases={j: 2 * n + j for j in range(2 * n)},
        compiler_params=pltpu.CompilerParams(has_side_effects=_EFFECT),
    )(*[hbm(a) for a in srcs], *[hbm(a) for a in lands])
    handles = [(res[2 * i], res[2 * i + 1], res[2 * n + i], res[3 * n + i]) for i in range(n)]
    return handles, res[-1]


def _send_wait(handle, blocked, after, name):
    ssem, rsem, src, land = handle

    def body(src_ref, land_ref, ssem_ref, rsem_ref, after_ref, src_out, land_out):
        for k in range(1, N_DEV):
            cp = _copy(src_ref, land_ref, ssem_ref, rsem_ref, k, blocked, True)
            cp.wait_send()
            cp.wait_recv()

    return pl.pallas_call(
        body, name=name, out_shape=(pltpu.HBM(src.shape, src.dtype), pltpu.HBM(land.shape, land.dtype)),
        in_specs=(_HBM, _HBM, _SEM, _SEM, pl.BlockSpec(memory_space=pl.ANY)), out_specs=(_HBM, _HBM),
        input_output_aliases={0: 0, 1: 1}, compiler_params=pltpu.CompilerParams(has_side_effects=_EFFECT),
    )(src, land, ssem, rsem, after)[1]


def _block_diag(w):
    nb, bs = w.shape[0], w.shape[1]
    eye = jnp.eye(nb, dtype=w.dtype)
    return (eye[:, None, :, None] * w[:, :, None, :]).reshape(nb * bs, nb * bs)


def _diag_blocks(d):
    return jnp.stack([d[g, s * 64:(s + 1) * 64, s * 64:(s + 1) * 64] for g in range(4) for s in range(2)])


def _mlp_fwd(x, nw, wu, wd, tag):
    hm = _rms_fwd(x, nw, f"rms_mlp_{tag}")
    relu_ep = lambda acc: (jnp.maximum(acc, 0.0), jnp.square(jnp.maximum(acc, 0.0)))
    r, act = _matmul(hm, wu, "nn", outs=[bf16, bf16], epilogue=relu_ep, name=f"mlp_up_{tag}")
    (xo,) = _matmul(act, wd, "nn", outs=[f32], extras=(x,), epilogue=lambda acc, res: (res + acc,), name=f"mlp_down_{tag}")
    return xo, (hm, r, act)


def _mlp_bwd(x, nw, wu, wd, saved, dxo, dxo_b, tag):
    hm, r, act = saved
    (du,) = _matmul(dxo_b, wd, "nt", outs=[bf16], extras=(r,), epilogue=lambda acc, rr: (acc * (2.0 * rr.astype(f32)),),
                    name=f"mlp_dact_{tag}")
    (dwd,) = _matmul(act, dxo_b, "tn", outs=[bf16], name=f"mlp_dwd_{tag}")
    (dwu,) = _matmul(hm, du, "tn", outs=[bf16], shard_cols=2, name=f"mlp_dwu_{tag}")
    (dhm,) = _matmul(du, wu, "nt", outs=[f32], name=f"mlp_dh_{tag}")
    dx, dx_b, dnw = _rms_bwd(x, nw, dhm, dxo, f"rms_mlp_bwd_{tag}")
    return dx, dx_b, dnw, dwu, dwd.reshape(N_DEV, D_FF // N_DEV, D_MODEL)


def _local_step(x, tgt, P, weight, sink):
    T = x.shape[0]
    cos, sin = _rope_tables(T)
    rtab = _ret_tables()
    row = lambda a: a.reshape(1, -1)
    mix_nw, mlp_nw = P["mixer_norm_w"], P["mlp_norm_w"]
    wr_bd, wi_bd = _block_diag(P["lru_w_r"]), _block_diag(P["lru_w_i"])
    lru_b, lru_br, lru_bi, lru_lam = row(P["lru_conv_b"]), row(P["lru_b_r"]), row(P["lru_b_i"]), row(P["lru_lambda"])
    pad16 = lambda a: jnp.pad(a.reshape(1, GDN_HEADS), ((0, 0), (GDN_HEADS, 128 - 2 * GDN_HEADS)))
    alog, dtb = pad16(P["gdn_a_log"]), pad16(P["gdn_dt_bias"])
    gnw = row(P["gdn_norm_w"])
    residual = lambda acc, res: (res + acc,)

    x0 = x
    h0 = _rms_fwd(x0, mix_nw[0:1], "rms_mix_0")
    w_ie = weight("w_in_even", h0)
    (pe,) = _matmul(h0, w_ie, "nn", outs=[f32], name="in_even")
    y_ret, o_ret, s_ret = _ret_fwd(pe, cos, sin, rtab, "ret_fwd")
    w_lc = weight("lru_conv_w", pe)
    xc = _conv_fwd(pe, 4, w_lc, lru_b, False, "lru_conv_fwd")
    y_lru, h_lru = _lru_fwd(xc, pe, 20, wr_bd, wi_bd, lru_br, lru_bi, lru_lam, "lru_fwd")
    mix0 = jnp.concatenate([y_ret, y_lru], axis=1)
    w_oe = weight("w_out_even", mix0)
    (x1,) = _matmul(mix0, w_oe, "nn", outs=[f32], extras=(x0,), epilogue=residual, name="out_even")
    w_u0, w_d0 = weight("w_up0", x1), weight("w_down0", x1)
    x2, mlp0 = _mlp_fwd(x1, mlp_nw[0:1], w_u0, w_d0, "0")
    h1 = _rms_fwd(x2, mix_nw[1:2], "rms_mix_1")
    w_io = weight("w_in_odd", h1)
    (po,) = _matmul(h1, w_io, "nn", outs=[f32], tm=1024, tn=ODD_PAD // 3, name="in_odd")
    w_gc = weight("gdn_conv_w", po)
    qkv = _conv_fwd(po, 0, w_gc, None, True, "gdn_conv_fwd")
    y_gdn, s_gdn, ti_gdn, vn_gdn, o_gdn = _gdn_fwd(qkv, po, alog, dtb, gnw, "gdn_fwd")
    w_oo = weight("w_out_odd", y_gdn)
    (x3,) = _matmul(y_gdn, w_oo, "nn", outs=[f32], extras=(x2,), epilogue=residual, name="out_odd")
    w_u1, w_d1 = weight("w_up1", x3), weight("w_down1", x3)
    x4, mlp1 = _mlp_fwd(x3, mlp_nw[1:2], w_u1, w_d1, "1")
    loss, dx4, dx4_b, d_final = _loss_head(x4, row(P["final_norm_w"]), tgt, "loss_head")
    dx3, dx3_b, d_mlp_nw1, d_wu1, d_wd1 = _mlp_bwd(x3, mlp_nw[1:2], w_u1, w_d1, mlp1, dx4, dx4_b, "1")
    tok = sink(dict(w_up1=d_wu1, w_down1=d_wd1))
    (dy_gdn,) = _matmul(dx3_b, w_oo, "nt", outs=[f32], name="out_odd_dx")
    (d_woo,) = _matmul(y_gdn, dx3_b, "tn", outs=[bf16], name="out_odd_dw")
    dqkv, dz, dba, d_alog, d_dtb, d_gnw = _gdn_bwd(qkv, po, alog, dtb, gnw + tok[0:1, :], s_gdn, ti_gdn, vn_gdn, o_gdn, dy_gdn,
                                                  "gdn_bwd")
    dqkv_pre, d_gconv, _ = _conv_bwd(po, 0, w_gc, None, True, dqkv, "gdn_conv_bwd")
    dpo = jnp.concatenate([dqkv_pre, dz, dba], axis=1)
    (d_wio,) = _matmul(h1, dpo, "tn", outs=[bf16], tn=ODD_PAD // 3, name="in_odd_dw")
    n_odd = ODD_IN // N_DEV
    tok = sink(dict(w_out_odd=d_woo.reshape(N_DEV, D_MODEL // N_DEV, D_MODEL),
                    w_in_odd=jnp.transpose(d_wio[:, :ODD_IN].reshape(D_MODEL, N_DEV, n_odd), (1, 0, 2))))
    (dh1,) = _matmul(dpo, w_io, "nt", outs=[f32], tm=1024, tk=ODD_PAD, name="in_odd_dx")
    dx2, dx2_b, d_mix_nw1 = _rms_bwd(x2, mix_nw[1:2] + tok[0:1, 0:1], dh1, dx3, "rms_mix_bwd_1")
    dx1, dx1_b, d_mlp_nw0, d_wu0, d_wd0 = _mlp_bwd(x1, mlp_nw[0:1], w_u0, w_d0, mlp0, dx2, dx2_b, "0")
    (d_woe,) = _matmul(mix0, dx1_b, "tn", outs=[bf16], name="out_even_dw")
    tok = sink(dict(w_up0=d_wu0, w_down0=d_wd0, w_out_even=d_woe.reshape(N_DEV, D_MODEL // N_DEV, D_MODEL)))
    (dmix0,) = _matmul(dx1_b, w_oe, "nt", outs=[f32], name="out_even_dx")
    d_ret = _ret_bwd(pe, cos, sin, rtab, o_ret, s_ret, dmix0, "ret_bwd")
    dxc, dy_l, d_wr, d_wi, d_br, d_bi, d_lam = _lru_bwd(xc, pe, 20, wr_bd, wi_bd, lru_br, lru_bi, lru_lam + tok[0:1, 0:1],
                                                        h_lru, dmix0, 4, "lru_bwd")
    dx_lru, d_lconv, d_lconv_b = _conv_bwd(pe, 4, w_lc, lru_b, False, dxc, "lru_conv_bwd")
    dpe = jnp.concatenate([d_ret, dx_lru, dy_l], axis=1)
    (d_wie,) = _matmul(h0, dpe, "tn", outs=[bf16], shard_cols=2, name="in_even_dw")
    (dh0,) = _matmul(dpe, w_ie, "nt", outs=[f32], name="in_even_dx")
    dx0, _, d_mix_nw0 = _rms_bwd(x0, mix_nw[0:1], dh0, dx1, "rms_mix_bwd_0")

    G = dict(
        mixer_norm_w=jnp.concatenate([d_mix_nw0, d_mix_nw1], axis=0),
        mlp_norm_w=jnp.concatenate([d_mlp_nw0, d_mlp_nw1], axis=0),
        final_norm_w=d_final.reshape(-1),
        w_in_even=d_wie, lru_conv_w=d_lconv, lru_conv_b=d_lconv_b.reshape(-1),
        lru_w_r=_diag_blocks(d_wr), lru_b_r=d_br.reshape(-1), lru_w_i=_diag_blocks(d_wi), lru_b_i=d_bi.reshape(-1),
        lru_lambda=d_lam.reshape(-1), gdn_conv_w=d_gconv,
        gdn_a_log=d_alog[0, GDN_HEADS:2 * GDN_HEADS], gdn_dt_bias=d_dtb[0, GDN_HEADS:2 * GDN_HEADS],
        gdn_norm_w=d_gnw.reshape(-1),
    )
    return loss, dx0, G


_SMALL = ["mixer_norm_w", "mlp_norm_w", "final_norm_w", "lru_conv_b", "lru_w_r", "lru_b_r", "lru_w_i", "lru_b_i",
          "lru_lambda", "gdn_a_log", "gdn_dt_bias", "gdn_norm_w"]
_PACK_ROWS = 688


def _pack(parts):
    flat = jnp.concatenate([p.reshape(-1) for p in parts])
    return jnp.pad(flat, (0, _PACK_ROWS * 128 - flat.shape[0])).reshape(_PACK_ROWS, 128)


def _unpack(packed, shapes):
    flat = packed.reshape(-1)
    out, off = [], 0
    for s in shapes:
        n = int(np.prod(s))
        out.append(flat[off:off + n].reshape(s))
        off += n
    return out


def kernel(x, mixer_norm_w, mlp_norm_w, final_norm_w, w_in_even, lru_conv_w, lru_conv_b, lru_w_r, lru_b_r, lru_w_i, lru_b_i, lru_lambda, w_out_even, w_in_odd, gdn_conv_w, gdn_a_log, gdn_dt_bias, gdn_norm_w, w_out_odd, w_up, w_down, loss_target, m_mixer_norm_w, m_mlp_norm_w, m_final_norm_w, m_w_in_even, m_lru_conv_w, m_lru_conv_b, m_lru_w_r, m_lru_b_r, m_lru_w_i, m_lru_b_i, m_lru_lambda, m_w_out_even, m_w_in_odd, m_gdn_conv_w, m_gdn_a_log, m_gdn_dt_bias, m_gdn_norm_w, m_w_out_odd, m_w_up, m_w_down, v_mixer_norm_w, v_mlp_norm_w, v_final_norm_w, v_w_in_even, v_lru_conv_w, v_lru_conv_b, v_lru_w_r, v_lru_b_r, v_lru_w_i, v_lru_b_i, v_lru_lambda, v_w_out_even, v_w_in_odd, v_gdn_conv_w, v_gdn_a_log, v_gdn_dt_bias, v_gdn_norm_w, v_w_out_odd, v_w_up, v_w_down):
    Pw = dict(mixer_norm_w=mixer_norm_w, mlp_norm_w=mlp_norm_w, final_norm_w=final_norm_w, w_in_even=w_in_even,
              lru_conv_w=lru_conv_w, lru_conv_b=lru_conv_b, lru_w_r=lru_w_r, lru_b_r=lru_b_r, lru_w_i=lru_w_i,
              lru_b_i=lru_b_i, lru_lambda=lru_lambda, w_out_even=w_out_even, w_in_odd=w_in_odd, gdn_conv_w=gdn_conv_w,
              gdn_a_log=gdn_a_log, gdn_dt_bias=gdn_dt_bias, gdn_norm_w=gdn_norm_w, w_out_odd=w_out_odd, w_up=w_up,
              w_down=w_down)
    Pm = dict(mixer_norm_w=m_mixer_norm_w, mlp_norm_w=m_mlp_norm_w, final_norm_w=m_final_norm_w, w_in_even=m_w_in_even,
              lru_conv_w=m_lru_conv_w, lru_conv_b=m_lru_conv_b, lru_w_r=m_lru_w_r, lru_b_r=m_lru_b_r, lru_w_i=m_lru_w_i,
              lru_b_i=m_lru_b_i, lru_lambda=m_lru_lambda, w_out_even=m_w_out_even, w_in_odd=m_w_in_odd,
              gdn_conv_w=m_gdn_conv_w, gdn_a_log=m_gdn_a_log, gdn_dt_bias=m_gdn_dt_bias, gdn_norm_w=m_gdn_norm_w,
              w_out_odd=m_w_out_odd, w_up=m_w_up, w_down=m_w_down)
    Pv = dict(mixer_norm_w=v_mixer_norm_w, mlp_norm_w=v_mlp_norm_w, final_norm_w=v_final_norm_w, w_in_even=v_w_in_even,
              lru_conv_w=v_lru_conv_w, lru_conv_b=v_lru_conv_b, lru_w_r=v_lru_w_r, lru_b_r=v_lru_b_r, lru_w_i=v_lru_w_i,
              lru_b_i=v_lru_b_i, lru_lambda=v_lru_lambda, w_out_even=v_w_out_even, w_in_odd=v_w_in_odd,
              gdn_conv_w=v_gdn_conv_w, gdn_a_log=v_gdn_a_log, gdn_dt_bias=v_gdn_dt_bias, gdn_norm_w=v_gdn_norm_w,
              w_out_odd=v_w_out_odd, w_up=v_w_up, w_down=v_w_down)
    me = _me()[3]
    T = x.shape[1]

    cols = lambda g: jnp.transpose(g, (1, 0, 2)).reshape(g.shape[1], -1)
    rows = lambda g: g.reshape(-1, g.shape[2])
    wide = lambda g: jnp.pad(cols(g), ((0, 0), (0, ODD_PAD - ODD_IN)))
    gather = dict(
        w_in_even=(w_in_even[0].astype(bf16), cols), lru_conv_w=(lru_conv_w[0], cols),
        w_out_even=(w_out_even[0].astype(bf16), rows), w_up0=(w_up[0].astype(bf16), cols), w_down0=(w_down[0].astype(bf16), rows),
        w_in_odd=(w_in_odd[0].astype(bf16), wide), gdn_conv_w=(gdn_conv_w[0], cols),
        w_out_odd=(w_out_odd[0].astype(bf16), rows), w_up1=(w_up[1].astype(bf16), cols), w_down1=(w_down[1].astype(bf16), rows))
    handles, tok = _send_start([s for s, _ in gather.values()], False, "gather_start")
    handles = dict(zip(gather, handles))
    full = {}

    def weight(name, after):
        if name not in full:
            landed = _send_wait(handles[name], False, after, f"gather_wait_{name}")
            shard, finish = gather[name]
            full[name] = finish(lax.dynamic_update_slice_in_dim(landed, shard[None], me, 0))
        return full[name]

    P = {k: Pw[k] for k in ("mlp_norm_w", "final_norm_w")}
    P["mixer_norm_w"] = mixer_norm_w + tok[0:1, 0:1]
    for k in ("lru_w_r", "lru_w_i", "lru_conv_b", "lru_b_r", "lru_b_i", "lru_lambda", "gdn_a_log", "gdn_dt_bias", "gdn_norm_w"):
        P[k] = Pw[k][0]

    sent = {}

    def sink(grads):
        hs, token = _send_start(list(grads.values()), True, "grads_start_" + "_".join(grads))
        for (name, g), h in zip(grads.items(), hs):
            sent[name] = (h, g)
        return token

    loss, dx, G = _local_step(x[0], loss_target[0], P, weight, sink)
    small_g = [G[k].reshape(Pw[k].shape) for k in _SMALL] + [G["lru_conv_w"], G["gdn_conv_w"]]
    packed = _pack(small_g)
    sink(dict(w_in_even=G["w_in_even"], small=jnp.broadcast_to(packed[None], (N_DEV,) + packed.shape)))

    def received(name, after=dx):
        h, g = sent[name]
        landed = _send_wait(h, True, after, f"grads_wait_{name}")
        return lax.dynamic_update_slice_in_dim(landed, lax.dynamic_slice_in_dim(g, me, 1, 0), me, 0)

    out = {}
    nff = D_FF // N_DEV

    def upd(name, gs, shape2d, layer=None):
        pick = (lambda a: a[layer]) if layer is not None else (lambda a: a)
        return _adamw(pick(Pw[name]).reshape(shape2d), gs, pick(Pm[name]).reshape(shape2d), pick(Pv[name]).reshape(shape2d),
                      f"adamw_{name}" + ("" if layer is None else str(layer)))

    def whole(name, gs, shape2d):
        out[name] = tuple(a.reshape(Pw[name].shape) for a in upd(name, gs, shape2d))

    def layers(name, shape2d):
        per = [upd(name, received(f"{name}{l}"), shape2d, l) for l in range(2)]
        out[name] = tuple(jnp.stack([per[0][j], per[1][j]]) for j in range(4))

    layers("w_up", (D_MODEL, nff))
    layers("w_down", (nff, D_MODEL))
    whole("w_out_odd", received("w_out_odd"), (128, D_MODEL))
    whole("w_in_odd", received("w_in_odd"), (D_MODEL, 514))
    whole("w_out_even", received("w_out_even"), (128, D_MODEL))
    whole("w_in_even", received("w_in_even", out["w_out_even"][1]), (D_MODEL, 384))
    small_shapes = [Pw[k].shape for k in _SMALL]
    pw, pm, pv = (_pack([Q[k] for k in _SMALL]) for Q in (Pw, Pm, Pv))
    sg, sd, sm, sv = _adamw(pw, received("small", out["w_in_even"][1]), pm, pv, "adamw_small")
    for arrs_i, packed_out in enumerate((sg, sd, sm, sv)):
        for k, a in zip(_SMALL, _unpack(packed_out, small_shapes)):
            out.setdefault(k, [None] * 4)[arrs_i] = a
    n_small = sum(int(np.prod(s)) for s in small_shapes)
    gflat = sg.reshape(-1)
    g_lconv = gflat[n_small:n_small + CONV_K * LRU_WIDTH].reshape(CONV_K, LRU_WIDTH)
    g_gconv = gflat[n_small + CONV_K * LRU_WIDTH:n_small + CONV_K * (LRU_WIDTH + 3072)].reshape(CONV_K, 3072)
    whole("lru_conv_w", lax.dynamic_slice_in_dim(g_lconv, me * 64, 64, axis=1)[None], (CONV_K, 64))
    whole("gdn_conv_w", lax.dynamic_slice_in_dim(g_gconv, me * 384, 384, axis=1)[None], (CONV_K, 384))

    names = ["mixer_norm_w", "mlp_norm_w", "final_norm_w", "w_in_even", "lru_conv_w", "lru_conv_b", "lru_w_r", "lru_b_r",
             "lru_w_i", "lru_b_i", "lru_lambda", "w_out_even", "w_in_odd", "gdn_conv_w", "gdn_a_log", "gdn_dt_bias",
             "gdn_norm_w", "w_out_odd", "w_up", "w_down"]
    total = lax.psum(loss[0, 0], ("x", "y", "c"))
    res = [total, dx[None]]
    for j in range(4):
        res += [out[k][j] for k in names]
    return tuple(res)
```

```python
import math

import numpy as np
import jax
import jax.numpy as jnp
from jax import lax
from jax.experimental import pallas as pl
from jax.experimental.pallas import tpu as pltpu

f32 = jnp.float32
bf16 = jnp.bfloat16

N_DEV = 8
D_MODEL = 1024
D_FF = 4096
EPS = 1e-6
RET_HEADS = 4
RET_CHUNK = 128
ROPE_THETA = 10000.0
LRU_WIDTH = 512
LRU_C = 8.0
GDN_HEADS = 8
GDN_CHUNK = 64
GDN_STEP = 2
HEAD_DIM = 128
ODD_IN = 4112
ODD_PAD = 4224
ADAM_LR, ADAM_B1, ADAM_B2, ADAM_EPS, ADAM_WD, ADAM_STEP = 0.001, 0.9, 0.999, 1e-08, 0.01, 10
VMEM_LIMIT = 56 * 1024 * 1024

_NN = (((1,), (0,)), ((), ()))
_NT = (((1,), (1,)), ((), ()))
_TN = (((0,), (0,)), ((), ()))
MESH = pl.DeviceIdType.MESH


def _cparams(sem):
    return pltpu.CompilerParams(dimension_semantics=sem, vmem_limit_bytes=VMEM_LIMIT)


def _dot(a, b, dn):
    return lax.dot_general(a.astype(bf16), b.astype(bf16), dn, preferred_element_type=f32)


def _dot01(a01, b, dn):
    a = a01.astype(bf16)
    b0 = b.astype(bf16)
    r1 = b - b0.astype(f32)
    b1 = r1.astype(bf16)
    b2 = (r1 - b1.astype(f32)).astype(bf16)
    d = lambda q: lax.dot_general(a, q, dn, preferred_element_type=f32)
    return d(b0) + (d(b1) + d(b2))


def _sigmoid(x):
    return jax.nn.sigmoid(x)


def _silu(x):
    return x * _sigmoid(x)


def _dsilu(x):
    s = _sigmoid(x)
    return s * (1.0 + x * (1.0 - s))


def _softplus(x):
    return jnp.maximum(x, 0.0) + jnp.log1p(jnp.exp(-jnp.abs(x)))


_GELU_C = math.sqrt(2.0 / math.pi)


def _gelu(y):
    return 0.5 * y * (1.0 + jnp.tanh(_GELU_C * (y + 0.044715 * y * y * y)))


def _dgelu(y):
    t = jnp.tanh(_GELU_C * (y + 0.044715 * y * y * y))
    return 0.5 * (1.0 + t) + 0.5 * y * (1.0 - t * t) * _GELU_C * (1.0 + 3.0 * 0.044715 * y * y)


def _matmul(a, b, form, *, outs, name, epilogue=None, extras=(), tm=2048, tn=512, tk=1024, shard_cols=0):
    if form == "tn":
        K, M = a.shape
    else:
        M, K = a.shape
    N = b.shape[0] if form == "nt" else b.shape[1]
    ns = N // N_DEV
    if shard_cols:
        tn = ns * shard_cols
    tm, tn, tk = min(tm, M), min(tn, N), min(tk, K)
    assert M % tm == 0 and N % tn == 0 and K % tk == 0, (name, M, N, K, tm, tn, tk)
    nk = K // tk
    dn = {"nn": _NN, "nt": _NT, "tn": _TN}[form]
    if form == "tn":
        a_spec = pl.BlockSpec((tk, tm), lambda i, j, k: (k, i))
    else:
        a_spec = pl.BlockSpec((tm, tk), lambda i, j, k: (i, k))
    if form == "nt":
        b_spec = pl.BlockSpec((tn, tk), lambda i, j, k: (j, k))
    else:
        b_spec = pl.BlockSpec((tk, tn), lambda i, j, k: (k, j))
    e_spec = pl.BlockSpec((tm, tn), lambda i, j, k: (i, j))
    if shard_cols:
        o_spec = pl.BlockSpec((shard_cols, tm, ns), lambda i, j, k: (j, i, 0))
        o_shape = (N_DEV, M, ns)
    else:
        o_spec = e_spec
        o_shape = (M, N)
    n_ex = len(extras)

    def finish(acc, ex, o_refs):
        vals = (acc,) if epilogue is None else epilogue(acc, *[e[...] for e in ex])
        for r, v in zip(o_refs, vals):
            if shard_cols:
                for s in range(shard_cols):
                    r[s] = v[:, s * ns:(s + 1) * ns].astype(r.dtype)
            else:
                r[...] = v.astype(r.dtype)

    def body_one(*refs):
        a_ref, b_ref = refs[:2]
        finish(_dot(a_ref[...], b_ref[...], dn), refs[2:2 + n_ex], refs[2 + n_ex:])

    def body_acc(*refs):
        a_ref, b_ref = refs[:2]
        acc = refs[-1]
        k = pl.program_id(2)

        @pl.when(k == 0)
        def _():
            acc[...] = _dot(a_ref[...], b_ref[...], dn)

        @pl.when((k > 0) & (k < nk - 1))
        def _():
            acc[...] += _dot(a_ref[...], b_ref[...], dn)

        @pl.when(k == nk - 1)
        def _():
            finish(acc[...] + _dot(a_ref[...], b_ref[...], dn), refs[2:2 + n_ex], refs[2 + n_ex:-1])

    return pl.pallas_call(
        body_one if nk == 1 else body_acc, grid=(M // tm, N // tn, nk),
        in_specs=[a_spec, b_spec] + [e_spec] * n_ex,
        out_specs=[o_spec] * len(outs),
        out_shape=[jax.ShapeDtypeStruct(o_shape, d) for d in outs],
        scratch_shapes=[] if nk == 1 else [pltpu.VMEM((tm, tn), f32)],
        compiler_params=_cparams(("parallel", "parallel", "arbitrary")), name=name,
    )(a, b, *extras)


def _rms_fwd(x, w, name):
    T, D = x.shape
    tt = min(512, T)

    def body(x_ref, w_ref, h_ref):
        xv = x_ref[...]
        r = lax.rsqrt(jnp.mean(xv * xv, axis=1, keepdims=True) + EPS)
        h_ref[...] = (xv * r * w_ref[...]).astype(bf16)

    return pl.pallas_call(
        body, grid=(T // tt,),
        in_specs=[pl.BlockSpec((tt, D), lambda i: (i, 0)), pl.BlockSpec((1, D), lambda i: (0, 0))],
        out_specs=pl.BlockSpec((tt, D), lambda i: (i, 0)),
        out_shape=jax.ShapeDtypeStruct((T, D), bf16),
        compiler_params=_cparams(("parallel",)), name=name,
    )(x, w)


def _rms_bwd(x, w, dh, dres, name):
    T, D = x.shape
    tt = min(512, T)

    def body(x_ref, w_ref, dh_ref, dres_ref, dx_ref, dxb_ref, dw_ref):
        @pl.when(pl.program_id(0) == 0)
        def _():
            dw_ref[...] = jnp.zeros_like(dw_ref)

        xv = x_ref[...]
        r = lax.rsqrt(jnp.mean(xv * xv, axis=1, keepdims=True) + EPS)
        xn = xv * r
        dhv = dh_ref[...]
        dhw = dhv * w_ref[...]
        dx = dres_ref[...] + r * (dhw - xn * jnp.mean(dhw * xn, axis=1, keepdims=True))
        dx_ref[...] = dx
        dxb_ref[...] = dx.astype(bf16)
        dw_ref[...] += jnp.sum(dhv * xn, axis=0, keepdims=True)

    row = pl.BlockSpec((tt, D), lambda i: (i, 0))
    vec = pl.BlockSpec((1, D), lambda i: (0, 0))
    return pl.pallas_call(
        body, grid=(T // tt,), in_specs=[row, vec, row, row], out_specs=[row, row, vec],
        out_shape=[jax.ShapeDtypeStruct((T, D), f32), jax.ShapeDtypeStruct((T, D), bf16), jax.ShapeDtypeStruct((1, D), f32)],
        compiler_params=_cparams(("arbitrary",)), name=name,
    )(x, w, dh, dres)


def _loss_head(x, w, tgt, name):
    T, D = x.shape
    tt = min(512, T)

    def body(x_ref, w_ref, t_ref, l_ref, dx_ref, dxb_ref, dw_ref):
        @pl.when(pl.program_id(0) == 0)
        def _():
            dw_ref[...] = jnp.zeros_like(dw_ref)
            l_ref[...] = jnp.zeros_like(l_ref)

        xv = x_ref[...]
        wv = w_ref[...]
        r = lax.rsqrt(jnp.mean(xv * xv, axis=1, keepdims=True) + EPS)
        xn = xv * r
        e = xn * wv - t_ref[...]
        l_ref[...] += 0.5 * jnp.sum(jnp.mean(e * e, axis=1, keepdims=True), axis=0, keepdims=True)
        dy = e * (1.0 / D)
        dyw = dy * wv
        dx = r * (dyw - xn * jnp.mean(dyw * xn, axis=1, keepdims=True))
        dx_ref[...] = dx
        dxb_ref[...] = dx.astype(bf16)
        dw_ref[...] += jnp.sum(dy * xn, axis=0, keepdims=True)

    row = pl.BlockSpec((tt, D), lambda i: (i, 0))
    vec = pl.BlockSpec((1, D), lambda i: (0, 0))
    return pl.pallas_call(
        body, grid=(T // tt,), in_specs=[row, vec, row],
        out_specs=[pl.BlockSpec((1, 128), lambda i: (0, 0)), row, row, vec],
        out_shape=[jax.ShapeDtypeStruct((1, 128), f32), jax.ShapeDtypeStruct((T, D), f32),
                   jax.ShapeDtypeStruct((T, D), bf16), jax.ShapeDtypeStruct((1, D), f32)],
        compiler_params=_cparams(("arbitrary",)), name=name,
    )(x, w, tgt)


def _ret_tables():
    H, C = RET_HEADS, RET_CHUNK
    lg = np.log1p(-np.exp2(-5.0 - np.arange(H, dtype=np.float32))).astype(np.float32)
    idx = np.arange(C, dtype=np.float32)
    diff = idx[:, None] - idx[None, :]
    causal = diff >= 0
    dm = np.where(causal[None], np.exp(lg[:, None, None] * np.where(causal, diff, 0.0)[None]), 0.0)
    qd = np.exp(lg[:, None] * (idx[None, :] + 1.0))
    kd = np.exp(lg[:, None] * (C - 1.0 - idx[None, :]))
    cg = np.exp(lg * C)
    tab = np.zeros((H, 4, C, HEAD_DIM), np.float32)
    tab[:, 0] = dm
    tab[:, 1] = qd[:, :, None]
    tab[:, 2] = kd[:, :, None]
    tab[:, 3] = cg[:, None, None]
    return jnp.asarray(tab)


def _rope_tables(T):
    half = HEAD_DIM // 2
    inv = ROPE_THETA ** (-jnp.arange(half, dtype=f32) / half)
    ang = jnp.arange(T, dtype=jnp.int32).astype(f32)[:, None] * inv[None, :]
    c, s = jnp.cos(ang), jnp.sin(ang)
    return jnp.concatenate([c, c], axis=1), jnp.concatenate([-s, s], axis=1)


def _rope(x, cos, sin):
    return x * cos + pltpu.roll(x, HEAD_DIM // 2, 1) * sin


def _unrope(y, cos, sin):
    return y * cos + pltpu.roll(y * sin, HEAD_DIM // 2, 1)


def _stack_heads(ref, H, f=None):
    parts = [ref[:, h * HEAD_DIM:(h + 1) * HEAD_DIM] for h in range(H)]
    return jnp.stack(parts if f is None else [f(a) for a in parts])


def _ret_fwd(p, cos, sin, tab, name):
    T = p.shape[0]
    C, H = RET_CHUNK, RET_HEADS
    N = T // C
    scale = HEAD_DIM ** -0.5

    def body(q_ref, k_ref, v_ref, g_ref, c_ref, s_ref, t_ref, y_ref, o_ref, sp_ref, st):
        @pl.when(pl.program_id(0) == 0)
        def _():
            st[...] = jnp.zeros_like(st)

        cos_, sin_ = c_ref[...], s_ref[...]
        rot = lambda a: _rope(a, cos_, sin_)
        q = _stack_heads(q_ref, H, rot)
        k = _stack_heads(k_ref, H, rot) * scale
        v = _stack_heads(v_ref, H)
        dm, qd, kd, cg = t_ref[:, 0], t_ref[:, 1], t_ref[:, 2], t_ref[:, 3]
        S = st[...]
        P = _dot(q, k, _NT3) * dm
        o = _dot(P, v, _NN3) + _dot(q * qd, S, _NN3)
        sp_ref[0] = S
        st[...] = cg * S + _dot(k * kd, v, _TN3)
        r = lax.rsqrt(jnp.mean(o * o, axis=2, keepdims=True) + EPS)
        y = o * r * _silu(_stack_heads(g_ref, H))
        for h in range(H):
            o_ref[:, h * HEAD_DIM:(h + 1) * HEAD_DIM] = o[h]
            y_ref[:, h * HEAD_DIM:(h + 1) * HEAD_DIM] = y[h].astype(bf16)

    wide = lambda blk: pl.BlockSpec((C, H * HEAD_DIM), lambda n: (n, blk))
    tbl = pl.BlockSpec((C, HEAD_DIM), lambda n: (n, 0))
    return pl.pallas_call(
        body, grid=(N,),
        in_specs=[wide(0), wide(1), wide(2), wide(3), tbl, tbl,
                  pl.BlockSpec((H, 4, C, HEAD_DIM), lambda n: (0, 0, 0, 0))],
        out_specs=[wide(0), wide(0), pl.BlockSpec((1, H, HEAD_DIM, HEAD_DIM), lambda n: (n, 0, 0, 0))],
        out_shape=[jax.ShapeDtypeStruct((T, D_MODEL), bf16), jax.ShapeDtypeStruct((T, H * HEAD_DIM), f32),
                   jax.ShapeDtypeStruct((N, H, HEAD_DIM, HEAD_DIM), f32)],
        scratch_shapes=[pltpu.VMEM((H, HEAD_DIM, HEAD_DIM), f32)],
        compiler_params=_cparams(("arbitrary",)), name=name,
    )(p, p, p, p, cos, sin, tab)


def _ret_bwd(p, cos, sin, tab, o_raw, sprev, dmix, name):
    T = p.shape[0]
    C, H = RET_CHUNK, RET_HEADS
    N = T // C
    scale = HEAD_DIM ** -0.5

    W = H * HEAD_DIM

    def body(q_ref, k_ref, v_ref, g_ref, c_ref, s_ref, t_ref, o_ref, sp_ref, dy_ref, d_ref, dst):
        @pl.when(pl.program_id(0) == 0)
        def _():
            dst[...] = jnp.zeros_like(dst)

        cos_, sin_ = c_ref[...], s_ref[...]
        rot = lambda a: _rope(a, cos_, sin_)
        q = _stack_heads(q_ref, H, rot)
        k = _stack_heads(k_ref, H, rot) * scale
        v = _stack_heads(v_ref, H)
        g = _stack_heads(g_ref, H)
        dm, qd, kd, cg = t_ref[:, 0], t_ref[:, 1], t_ref[:, 2], t_ref[:, 3]
        S = sp_ref[0]
        o = _stack_heads(o_ref, H)
        dy = _stack_heads(dy_ref, H)
        r = lax.rsqrt(jnp.mean(o * o, axis=2, keepdims=True) + EPS)
        nrm = o * r
        dn = dy * _silu(g)
        dg = dy * nrm * _dsilu(g)
        do = r * (dn - nrm * jnp.mean(dn * nrm, axis=2, keepdims=True))
        dS1 = dst[...]
        P = _dot(q, k, _NT3) * dm
        dP = _dot(do, v, _NT3) * dm
        dq = _dot(dP, k, _NN3) + _dot(do, S, _NT3) * qd
        dk = (_dot(dP, q, _TN3) + _dot(v, dS1, _NT3) * kd) * scale
        dv = _dot(P, do, _TN3) + _dot(k * kd, dS1, _NN3)
        dst[...] = cg * dS1 + _dot(q * qd, do, _TN3)
        for h in range(H):
            d_ref[:, h * HEAD_DIM:(h + 1) * HEAD_DIM] = _unrope(dq[h], cos_, sin_).astype(bf16)
            d_ref[:, W + h * HEAD_DIM:W + (h + 1) * HEAD_DIM] = _unrope(dk[h], cos_, sin_).astype(bf16)
            d_ref[:, 2 * W + h * HEAD_DIM:2 * W + (h + 1) * HEAD_DIM] = dv[h].astype(bf16)
            d_ref[:, 3 * W + h * HEAD_DIM:3 * W + (h + 1) * HEAD_DIM] = dg[h].astype(bf16)

    rev = lambda blk: pl.BlockSpec((C, W), lambda n: (N - 1 - n, blk))
    tbl = pl.BlockSpec((C, HEAD_DIM), lambda n: (N - 1 - n, 0))
    return pl.pallas_call(
        body, grid=(N,),
        in_specs=[rev(0), rev(1), rev(2), rev(3), tbl, tbl,
                  pl.BlockSpec((H, 4, C, HEAD_DIM), lambda n: (0, 0, 0, 0)), rev(0),
                  pl.BlockSpec((1, H, HEAD_DIM, HEAD_DIM), lambda n: (N - 1 - n, 0, 0, 0)), rev(0)],
        out_specs=pl.BlockSpec((C, 4 * W), lambda n: (N - 1 - n, 0)),
        out_shape=jax.ShapeDtypeStruct((T, 6 * W), bf16),
        scratch_shapes=[pltpu.VMEM((H, HEAD_DIM, HEAD_DIM), f32)],
        compiler_params=_cparams(("arbitrary",)), name=name,
    )(p, p, p, p, cos, sin, tab, o_raw, sprev, dmix)


CONV_K = 4
CONV_W = 512
PAD = 8
SUB_R = 64


def _conv_fwd(x, col_off, w, b, act, name):
    T = x.shape[0]
    C = w.shape[1]
    G = C // CONV_W
    tt = min(512, T)
    NT = T // tt
    has_b = b is not None

    def body(*refs):
        if has_b:
            x_ref, w_ref, b_ref, y_ref, pad = refs
        else:
            x_ref, w_ref, y_ref, pad = refs
        t = pl.program_id(1)

        @pl.when(t == 0)
        def _():
            pad[pl.ds(0, PAD), :] = jnp.zeros((PAD, CONV_W), f32)

        pad[pl.ds(PAD, tt), :] = x_ref[...]
        for g in range(CONV_W // 128):
            ls = slice(g * 128, (g + 1) * 128)
            wv = w_ref[:, ls]
            for c in range(tt // SUB_R):
                r0 = c * SUB_R
                y = wv[0:1, :] * pad[pl.ds(PAD - 3 + r0, SUB_R), ls]
                for kk in range(1, CONV_K):
                    y = y + wv[kk:kk + 1, :] * pad[pl.ds(PAD - 3 + kk + r0, SUB_R), ls]
                if has_b:
                    y = y + b_ref[:, ls]
                y_ref[pl.ds(r0, SUB_R), ls] = _silu(y) if act else y
        tail = pad[pl.ds(tt, PAD), :]
        pad[pl.ds(0, PAD), :] = tail

    in_specs = [pl.BlockSpec((tt, CONV_W), lambda g, t: (t, col_off + g)),
                pl.BlockSpec((CONV_K, CONV_W), lambda g, t: (0, g))]
    args = [x, w]
    if has_b:
        in_specs.append(pl.BlockSpec((1, CONV_W), lambda g, t: (0, g)))
        args.append(b)
    return pl.pallas_call(
        body, grid=(G, NT), in_specs=in_specs,
        out_specs=pl.BlockSpec((tt, CONV_W), lambda g, t: (t, g)),
        out_shape=jax.ShapeDtypeStruct((T, C), f32),
        scratch_shapes=[pltpu.VMEM((tt + PAD, CONV_W), f32)],
        compiler_params=_cparams(("parallel", "arbitrary")), name=name,
    )(*args)


def _conv_bwd(x, col_off, w, b, act, dout, dp, name):
    T = x.shape[0]
    C = w.shape[1]
    G = C // CONV_W
    tt = min(512, T)
    NT = T // tt
    has_b = b is not None

    def body(*refs):
        if has_b:
            x_ref, xp_ref, w_ref, b_ref, d_ref, dp_in, dx_ref, dw_ref, db_ref, pad, dpad = refs
        else:
            x_ref, xp_ref, w_ref, d_ref, dp_in, dx_ref, dw_ref, db_ref, pad, dpad = refs
        t = pl.program_id(1)
        first_tile = t == NT - 1

        @pl.when(t == 0)
        def _():
            dpad[pl.ds(tt, PAD), :] = jnp.zeros((PAD, CONV_W), f32)
            dw_ref[...] = jnp.zeros_like(dw_ref)
            db_ref[...] = jnp.zeros_like(db_ref)

        pad[pl.ds(0, PAD), :] = jnp.where(first_tile, 0.0, xp_ref[...])
        pad[pl.ds(PAD, tt), :] = x_ref[...]
        fold = lambda v: v.reshape(SUB_R // 8, 8, 128).sum(axis=0)
        for g in range(CONV_W // 128):
            ls = slice(g * 128, (g + 1) * 128)
            wv = w_ref[:, ls]
            acc = [jnp.zeros((8, 128), f32) for _ in range(CONV_K + 1)]
            for c in reversed(range(tt // SUB_R)):
                r0 = c * SUB_R
                xs = [pad[pl.ds(PAD - 3 + kk + r0, SUB_R), ls] for kk in range(CONV_K)]
                dy = d_ref[pl.ds(r0, SUB_R), ls]
                if act:
                    y = wv[0:1, :] * xs[0]
                    for kk in range(1, CONV_K):
                        y = y + wv[kk:kk + 1, :] * xs[kk]
                    if has_b:
                        y = y + b_ref[:, ls]
                    dy = dy * _dsilu(y)
                dpad[pl.ds(r0, SUB_R), ls] = dy
                dx = wv[3:4, :] * dy
                for j in range(1, CONV_K):
                    dx = dx + wv[3 - j:4 - j, :] * dpad[pl.ds(r0 + j, SUB_R), ls]
                dx_ref[pl.ds(r0, SUB_R), ls] = dx.astype(bf16)
                for kk in range(CONV_K):
                    acc[kk] = acc[kk] + fold(dy * xs[kk])
                acc[CONV_K] = acc[CONV_K] + fold(dy)
            for kk in range(CONV_K):
                dw_ref[kk:kk + 1, ls] += jnp.sum(acc[kk], axis=0, keepdims=True)
            db_ref[:, ls] += jnp.sum(acc[CONV_K], axis=0, keepdims=True)
        head = dpad[pl.ds(0, PAD), :]
        dpad[pl.ds(tt, PAD), :] = head

    rows8 = tt // PAD
    in_specs = [pl.BlockSpec((tt, CONV_W), lambda g, t: (NT - 1 - t, col_off + g)),
                pl.BlockSpec((PAD, CONV_W), lambda g, t: (jnp.maximum((NT - 1 - t) * rows8 - 1, 0), col_off + g)),
                pl.BlockSpec((CONV_K, CONV_W), lambda g, t: (0, g))]
    args = [x, x, w]
    if has_b:
        in_specs.append(pl.BlockSpec((1, CONV_W), lambda g, t: (0, g)))
        args.append(b)
    in_specs += [pl.BlockSpec((tt, CONV_W), lambda g, t: (NT - 1 - t, g)), pl.BlockSpec(memory_space=pl.ANY)]
    args += [dout, dp]
    return pl.pallas_call(
        body, grid=(G, NT), in_specs=in_specs,
        out_specs=[pl.BlockSpec((tt, CONV_W), lambda g, t: (NT - 1 - t, col_off + g)),
                   pl.BlockSpec((CONV_K, CONV_W), lambda g, t: (0, g)),
                   pl.BlockSpec((1, CONV_W), lambda g, t: (0, g))],
        out_shape=[jax.ShapeDtypeStruct(dp.shape, dp.dtype), jax.ShapeDtypeStruct((CONV_K, C), f32),
                   jax.ShapeDtypeStruct((1, C), f32)],
        input_output_aliases={len(args) - 1: 0},
        scratch_shapes=[pltpu.VMEM((tt + PAD, CONV_W), f32), pltpu.VMEM((tt + PAD, CONV_W), f32)],
        compiler_params=_cparams(("parallel", "arbitrary")), name=name,
    )(*args)


def _lru_gates(xc, wr, wi, br, bi, lam):
    r = _sigmoid(_dot(xc, wr, _NN) + br)
    i = _sigmoid(_dot(xc, wi, _NN) + bi)
    sp = _softplus(-lam)
    a = jnp.exp(-LRU_C * r * sp)
    mult = jnp.sqrt(1.0 - a * a)
    return r, i, sp, a, mult


def _lru_fwd(xc, p, y_off, wr, wi, br, bi, lam, mix, name):
    T = xc.shape[0]
    G = LRU_WIDTH // 128
    tt = min(512, T)
    NT = T // tt

    def body(x_ref, y_ref, wr_ref, wi_ref, br_ref, bi_ref, l_ref, mix_in, o_ref, h_ref, hc):
        t = pl.program_id(1)

        @pl.when(t == 0)
        def _():
            hc[...] = jnp.zeros_like(hc)

        x = x_ref[...]
        r, i, sp, a, mult = _lru_gates(x, wr_ref[...], wi_ref[...], br_ref[...], bi_ref[...], l_ref[...])
        row = lax.broadcasted_iota(jnp.int32, (tt, 128), 0)
        mult = jnp.where((row == 0) & (t == 0), 1.0, mult)
        U = x * i * mult
        A = a
        d = 1
        while d < tt:
            keep = row >= d
            Ush = jnp.where(keep, pltpu.roll(U, d, 0), 0.0)
            Ash = jnp.where(keep, pltpu.roll(A, d, 0), 1.0)
            U = A * Ush + U
            A = A * Ash
            d *= 2
        h = U + A * hc[0:1, :]
        h_ref[...] = h
        hc[...] = jnp.broadcast_to(h[tt - 1:tt, :], hc.shape)
        o_ref[...] = (h * _gelu(y_ref[...])).astype(bf16)

    tile = pl.BlockSpec((tt, 128), lambda g, t: (t, g))
    vec = pl.BlockSpec((1, 128), lambda g, t: (0, g))
    wsp = pl.BlockSpec((128, 128), lambda g, t: (g, g))
    return pl.pallas_call(
        body, grid=(G, NT),
        in_specs=[tile, pl.BlockSpec((tt, 128), lambda g, t: (t, y_off + g)), wsp, wsp, vec, vec, vec,
                  pl.BlockSpec(memory_space=pl.ANY)],
        out_specs=[pl.BlockSpec((tt, 128), lambda g, t: (t, G + g)), tile],
        out_shape=[jax.ShapeDtypeStruct(mix.shape, mix.dtype), jax.ShapeDtypeStruct((T, LRU_WIDTH), f32)],
        input_output_aliases={7: 0},
        scratch_shapes=[pltpu.VMEM((8, 128), f32)],
        compiler_params=_cparams(("parallel", "arbitrary")), name=name,
    )(xc, p, wr, wi, br, bi, lam, mix)


def _lru_bwd(xc, p, y_off, wr, wi, br, bi, lam, hs, dmix, d_off, dp, name):
    T = xc.shape[0]
    G = LRU_WIDTH // 128
    tt = min(512, T)
    NT = T // tt

    def body(x_ref, y_ref, wr_ref, wi_ref, br_ref, bi_ref, l_ref, h_ref, hp_ref, do_ref, dp_in,
             dx_ref, dy_ref, dwr_ref, dwi_ref, dbr_ref, dbi_ref, dl_ref, lc, an):
        t = pl.program_id(1)
        first_tile = t == NT - 1

        @pl.when(t == 0)
        def _():
            lc[...] = jnp.zeros_like(lc)
            an[...] = jnp.zeros_like(an)
            dwr_ref[...] = jnp.zeros_like(dwr_ref)
            dwi_ref[...] = jnp.zeros_like(dwi_ref)
            dbr_ref[...] = jnp.zeros_like(dbr_ref)
            dbi_ref[...] = jnp.zeros_like(dbi_ref)
            dl_ref[...] = jnp.zeros_like(dl_ref)

        x = x_ref[...]
        y = y_ref[...]
        wr, wi, lam_ = wr_ref[...], wi_ref[...], l_ref[...]
        r, i, sp, a, mult_raw = _lru_gates(x, wr, wi, br_ref[...], bi_ref[...], lam_)
        row = lax.broadcasted_iota(jnp.int32, (tt, 128), 0)
        t0 = (row == 0) & first_tile
        mult = jnp.where(t0, 1.0, mult_raw)
        h = h_ref[...]
        do = do_ref[...]
        dh = do * _gelu(y)
        dy_ref[...] = (do * h * _dgelu(y)).astype(bf16)
        B = jnp.where(row == tt - 1, an[0:1, :], pltpu.roll(a, tt - 1, 0))
        L = dh
        d = 1
        while d < tt:
            keep = row < tt - d
            Lsh = jnp.where(keep, pltpu.roll(L, tt - d, 0), 0.0)
            Bsh = jnp.where(keep, pltpu.roll(B, tt - d, 0), 1.0)
            L = L + B * Lsh
            B = B * Bsh
            d *= 2
        L = L + B * lc[0:1, :]
        lc[...] = jnp.broadcast_to(L[0:1, :], lc.shape)
        an[...] = jnp.broadcast_to(a[0:1, :], an.shape)
        hprev = jnp.where(first_tile, 0.0, hp_ref[...])[PAD - 1:PAD, :]
        hm1 = jnp.where(row == 0, hprev, pltpu.roll(h, 1, 0))
        da = L * hm1
        dxc = L * i * mult
        di = L * x * mult
        dmult = jnp.where(t0, 0.0, L * x * i)
        da = da - jnp.where(t0, 0.0, dmult * a / mult_raw)
        dlog_a = da * a
        dr = dlog_a * (-LRU_C) * sp
        dsp = jnp.sum(dlog_a * (-LRU_C) * r, axis=0, keepdims=True)
        dpr = dr * r * (1.0 - r)
        dpi = di * i * (1.0 - i)
        dx_ref[...] = dxc + _dot(dpr, wr, _NT) + _dot(dpi, wi, _NT)
        dwr_ref[0] += _dot(x, dpr, _TN)
        dwi_ref[0] += _dot(x, dpi, _TN)
        dbr_ref[...] += jnp.sum(dpr, axis=0, keepdims=True)
        dbi_ref[...] += jnp.sum(dpi, axis=0, keepdims=True)
        dl_ref[...] += dsp * (-_sigmoid(-lam_))

    rows8 = tt // PAD
    tile = pl.BlockSpec((tt, 128), lambda g, t: (NT - 1 - t, g))
    vec = pl.BlockSpec((1, 128), lambda g, t: (0, g))
    wsp = pl.BlockSpec((128, 128), lambda g, t: (g, g))
    wout = pl.BlockSpec((1, 128, 128), lambda g, t: (g, 0, 0))
    return pl.pallas_call(
        body, grid=(G, NT),
        in_specs=[tile, pl.BlockSpec((tt, 128), lambda g, t: (NT - 1 - t, y_off + g)), wsp, wsp, vec, vec, vec, tile,
                  pl.BlockSpec((PAD, 128), lambda g, t: (jnp.maximum((NT - 1 - t) * rows8 - 1, 0), g)),
                  pl.BlockSpec((tt, 128), lambda g, t: (NT - 1 - t, d_off + g)), pl.BlockSpec(memory_space=pl.ANY)],
        out_specs=[tile, pl.BlockSpec((tt, 128), lambda g, t: (NT - 1 - t, y_off + g)), wout, wout, vec, vec, vec],
        out_shape=[jax.ShapeDtypeStruct((T, LRU_WIDTH), f32), jax.ShapeDtypeStruct(dp.shape, dp.dtype),
                   jax.ShapeDtypeStruct((G, 128, 128), f32), jax.ShapeDtypeStruct((G, 128, 128), f32),
                   jax.ShapeDtypeStruct((1, LRU_WIDTH), f32), jax.ShapeDtypeStruct((1, LRU_WIDTH), f32),
                   jax.ShapeDtypeStruct((1, LRU_WIDTH), f32)],
        input_output_aliases={10: 1},
        scratch_shapes=[pltpu.VMEM((8, 128), f32), pltpu.VMEM((8, 128), f32)],
        compiler_params=_cparams(("parallel", "arbitrary")), name=name,
    )(xc, p, wr, wi, br, bi, lam, hs, hs, dmix, dp)


_NN3 = (((2,), (1,)), ((0,), (0,)))
_NT3 = (((2,), (2,)), ((0,), (0,)))
_TN3 = (((1,), (1,)), ((0,), (0,)))


def _heads(ref):
    return _stack_heads(ref, GDN_HEADS)


def _rowsum(x):
    H, C, L = x.shape
    return _dot(x.reshape(H * C, L), jnp.ones((L, HEAD_DIM), f32), _NN).reshape(H, C, HEAD_DIM)


def _gdn_chunk(qr, kr, v, ba, alog, dtb, S, saved=None):
    C, H = GDN_CHUNK, GDN_HEADS
    lane = lax.broadcasted_iota(jnp.int32, (C, 128), 1)
    lane3 = lax.broadcasted_iota(jnp.int32, (H, C, 128), 2)
    ri = lax.broadcasted_iota(jnp.int32, (C, C), 0)
    ci = lax.broadcasted_iota(jnp.int32, (C, C), 1)
    rowc = lax.broadcasted_iota(jnp.int32, (C, 1), 0)
    col = lambda m, j: jnp.sum(jnp.where(lane == j, m, 0.0), axis=1, keepdims=True)
    cols = lambda m, off: jnp.stack([col(m, off + h) for h in range(H)])
    ea = jnp.exp(alog)
    g_all = -ea * _softplus(ba + dtb)
    tri = (ri >= ci).astype(f32)
    G_all = _dot01(tri, g_all, _NN)
    beta = cols(_sigmoid(ba), 0)
    Gc = cols(G_all, H)
    rq = lax.rsqrt(_rowsum(qr * qr) + EPS)
    rk = lax.rsqrt(_rowsum(kr * kr) + EPS)
    qh, kn = qr * rq, kr * rk
    qn = qh * (HEAD_DIM ** -0.5)
    Grow = _dot01(jnp.ones((H, C, 128), f32), jnp.where(lane3 == 0, Gc, 0.0), _NT3)
    incl = ri >= ci
    Di = jnp.where(incl, jnp.exp(jnp.where(incl, Gc - Grow, 0.0)), 0.0)
    Ds = jnp.where(ri > ci, Di, 0.0)
    Gl = jnp.sum(jnp.where(rowc == C - 1, Gc, 0.0), axis=1, keepdims=True)
    eG = jnp.exp(Gc)
    eGl = jnp.exp(Gl - Gc)
    cd = jnp.exp(Gl)
    kb = kn * beta
    vb = v * beta
    Lm = _dot(kb, kn, _NT3) * Ds
    kbg = kb * eG
    QK = _dot(qn, kn, _NT3) * Di
    qg = qn * eG
    kg = kn * eGl
    if saved is None:
        same = lambda s: (ri // s) == (ci // s)
        Xd = jnp.where(same(8), -Lm, 0.0)
        Tinv = (ri == ci).astype(f32) + Xd
        Pw = Xd
        for _ in range(2):
            Pw = _dot(Pw, Pw, _NN3)
            Tinv = Tinv + _dot(Tinv, Pw, _NN3)
        for s in (8, 16, 32):
            off = jnp.where(same(2 * s) & jnp.logical_not(same(s)), Lm, 0.0)
            Tinv = Tinv - _dot(_dot(Tinv, off, _NN3), Tinv, _NN3)
        w = _dot(Tinv, kbg, _NN3)
        vn = _dot(Tinv, vb, _NN3) - _dot(w, S, _NN3)
        o = _dot(qg, S, _NN3) + _dot(QK, vn, _NN3)
        S1 = S * cd + _dot(kg, vn, _TN3)
    else:
        Tinv, vn, o = saved
        w = _dot(Tinv, kbg, _NN3)
        S1 = None
    return dict(beta=beta, g_all=g_all, rq=rq, rk=rk, qh=qh, kn=kn, qn=qn, Di=Di, Ds=Ds, eG=eG, eGl=eGl, cd=cd,
                kb=kb, vb=vb, Lm=Lm, Tinv=Tinv, kbg=kbg, w=w, QK=QK, qg=qg, kg=kg, vn=vn, o=o, S1=S1,
                lane=lane, ri=ri, ci=ci, rowc=rowc, ea=ea)


def _gdn_specs(T, rev):
    C = GDN_CHUNK
    H = GDN_HEADS
    K = min(GDN_STEP, T // C)
    NS = T // (C * K)
    nn = (lambda n: NS - 1 - n) if rev else (lambda n: n)
    wide = lambda blk: pl.BlockSpec((K * C, H * HEAD_DIM), lambda n: (nn(n), blk))
    one = lambda off: pl.BlockSpec((K * C, HEAD_DIM), lambda n: (nn(n), off))
    vec = pl.BlockSpec((1, 128), lambda n: (0, 0))
    st = lambda rows: pl.BlockSpec((K, H, rows, rows), lambda n: (nn(n), 0, 0, 0))
    return K, NS, wide, one, vec, st


def _gdn_fwd(qkv, p, alog, dtb, nw, name):
    T = qkv.shape[0]
    C, H = GDN_CHUNK, GDN_HEADS
    N = T // C
    K, NS, wide, one, vec, st_spec = _gdn_specs(T, False)

    def body(q_ref, k_ref, v_ref, z_ref, ba_ref, al_ref, dt_ref, nw_ref, y_ref, sp_ref, ti_ref, vn_ref, o_ref, st):
        @pl.when(pl.program_id(0) == 0)
        def _():
            st[...] = jnp.zeros_like(st)

        S = st[...]
        for c in range(K):
            rows = pl.ds(c * C, C)
            at = lambda ref: ref.at[rows, :]
            f = _gdn_chunk(_heads(at(q_ref)), _heads(at(k_ref)), _heads(at(v_ref)), ba_ref[rows, :], al_ref[...],
                           dt_ref[...], S)
            sp_ref[c] = S
            S = f["S1"]
            ti_ref[c] = f["Tinv"]
            o, vn = f["o"], f["vn"]
            r = lax.rsqrt(_rowsum(o * o) * (1.0 / HEAD_DIM) + EPS)
            y = o * r * nw_ref[...] * _silu(_heads(at(z_ref)))
            for h in range(H):
                sl = slice(h * HEAD_DIM, (h + 1) * HEAD_DIM)
                y_ref[rows, sl] = y[h].astype(bf16)
                vn_ref[rows, sl] = vn[h]
                o_ref[rows, sl] = o[h]
        st[...] = S

    wide_f32 = jax.ShapeDtypeStruct((T, H * HEAD_DIM), f32)
    return pl.pallas_call(
        body, grid=(NS,),
        in_specs=[wide(0), wide(1), wide(2), wide(3), one(4 * H), vec, vec, vec],
        out_specs=[wide(0), st_spec(HEAD_DIM), st_spec(C), wide(0), wide(0)],
        out_shape=[jax.ShapeDtypeStruct((T, H * HEAD_DIM), bf16), jax.ShapeDtypeStruct((N, H, HEAD_DIM, HEAD_DIM), f32),
                   jax.ShapeDtypeStruct((N, H, C, C), f32), wide_f32, wide_f32],
        scratch_shapes=[pltpu.VMEM((H, HEAD_DIM, HEAD_DIM), f32)],
        compiler_params=_cparams(("arbitrary",)), name=name,
    )(qkv, qkv, qkv, p, p, alog, dtb, nw)


def _gdn_bwd(qkv, p, alog, dtb, nw, sprev, tinv, vn_all, o_all, dy_all, name):
    T = qkv.shape[0]
    C, H = GDN_CHUNK, GDN_HEADS
    N = T // C
    K, NS, wide, one, vec, st_spec = _gdn_specs(T, True)
    rs = lambda m: jnp.sum(m, axis=2, keepdims=True)

    def put(ref, val, col=0):
        for h in range(H):
            ref[:, col + h * HEAD_DIM:col + (h + 1) * HEAD_DIM] = val[h].astype(ref.dtype)

    def body(q_ref, k_ref, v_ref, z_ref, ba_ref, al_ref, dt_ref, nw_ref, sp_ref, ti_ref, vn_ref, o_ref, dy_ref,
             dqkv_ref, dz_ref, dba_ref, dal_ref, ddt_ref, dnw_ref, dst):
        @pl.when(pl.program_id(0) == 0)
        def _():
            dst[...] = jnp.zeros_like(dst)
            dal_ref[...] = jnp.zeros_like(dal_ref)
            ddt_ref[...] = jnp.zeros_like(ddt_ref)
            dnw_ref[...] = jnp.zeros_like(dnw_ref)

        dS = dst[...]
        for c in reversed(range(K)):
            at = lambda ref, c=c: ref.at[pl.ds(c * C, C), :]
            dS = chunk(at(q_ref), at(k_ref), at(v_ref), at(z_ref), at(ba_ref), al_ref, dt_ref, nw_ref, sp_ref[c], ti_ref[c],
                       at(vn_ref), at(o_ref), at(dy_ref), at(dqkv_ref), at(dz_ref), at(dba_ref), dal_ref, ddt_ref, dnw_ref, dS)
        dst[...] = dS

    def chunk(q_ref, k_ref, v_ref, z_ref, ba_ref, al_ref, dt_ref, nw_ref, S, Tsaved, vn_ref, o_ref, dy_ref,
              dqkv_ref, dz_ref, dba_ref, dal_ref, ddt_ref, dnw_ref, dS1):
        ba, alog, dtb_, nwv = ba_ref[...], al_ref[...], dt_ref[...], nw_ref[...]
        v = _heads(v_ref)
        f = _gdn_chunk(_heads(q_ref), _heads(k_ref), v, ba, alog, dtb_, S, saved=(Tsaved, _heads(vn_ref), _heads(o_ref)))
        beta, kn, qn, kb, vb, Tinv, kbg = f["beta"], f["kn"], f["qn"], f["kb"], f["vb"], f["Tinv"], f["kbg"]
        eG, eGl, cd, Di, Ds, QK, vn, w_, qg, kg = (f["eG"], f["eGl"], f["cd"], f["Di"], f["Ds"], f["QK"], f["vn"],
                                                    f["w"], f["qg"], f["kg"])
        lane, ri, ci, rowc = f["lane"], f["ri"], f["ci"], f["rowc"]
        o = f["o"]
        z = _heads(z_ref)
        dy = _heads(dy_ref)
        r = lax.rsqrt(_rowsum(o * o) * (1.0 / HEAD_DIM) + EPS)
        nrm = o * r
        sz = _silu(z)
        dn = dy * nwv * sz
        put(dz_ref, dy * nrm * nwv * _dsilu(z))
        dnw_ref[...] += jnp.sum(jnp.sum(dy * nrm * sz, axis=0), axis=0, keepdims=True)
        do = r * (dn - nrm * (_rowsum(dn * nrm) * (1.0 / HEAD_DIM)))
        dcd = jnp.sum(jnp.sum(S * dS1, axis=2, keepdims=True), axis=1, keepdims=True)
        dkg = _dot(vn, dS1, _NT3)
        dvn = _dot(kg, dS1, _NN3) + _dot(QK, do, _TN3)
        dqg = _dot(do, S, _NT3)
        dQK = _dot(do, vn, _NT3)
        dw = -_dot(dvn, S, _NT3)
        dS0 = cd * dS1 + _dot(qg, do, _TN3) - _dot(w_, dvn, _TN3)
        dqn = dqg * eG
        dkn = dkg * eGl
        deGl = rs(dkg * kn)
        dQKr = dQK * Di
        E = dQK * QK
        dqn = dqn + _dot(dQKr, kn, _NN3)
        dkn = dkn + _dot(dQKr, qn, _TN3)
        dT = _dot(dvn, vb, _NT3) + _dot(dw, kbg, _NT3)
        dvb = _dot(Tinv, dvn, _TN3)
        dkbg = _dot(Tinv, dw, _TN3)
        dkb = dkbg * eG
        deG = rs(dqg * qn + dkbg * kb)
        dL = -_dot(_dot(Tinv, dT, _TN3), Tinv, _NT3)
        dKK = dL * Ds
        E = E + dL * f["Lm"]
        dkb = dkb + _dot(dKK, kn, _NN3)
        dkn = dkn + _dot(dKK, kb, _TN3) + dkb * beta
        dbeta = rs(dkb * kn + dvb * v)
        put(dqkv_ref, dvb * beta, 2 * H * HEAD_DIM)
        dG = rs(E) - rs(jnp.swapaxes(E, 1, 2)) + deG * eG - deGl * eGl
        dGl = jnp.sum(deGl * eGl, axis=1, keepdims=True) + dcd * cd
        dG = dG + jnp.where(rowc == C - 1, dGl, 0.0)
        qh = f["qh"]
        put(dqkv_ref, (HEAD_DIM ** -0.5) * f["rq"] * (dqn - qh * _rowsum(dqn * qh)))
        put(dqkv_ref, f["rk"] * (dkn - kn * _rowsum(dkn * kn)), H * HEAD_DIM)
        db = dbeta * beta * (1.0 - beta)
        db_all = jnp.where(lane == 0, db[0], 0.0)
        dG_all = jnp.where(lane == H, dG[0], 0.0)
        for h in range(1, H):
            db_all = db_all + jnp.where(lane == h, db[h], 0.0)
            dG_all = dG_all + jnp.where(lane == H + h, dG[h], 0.0)
        triu = (ri <= ci).astype(f32)
        dg_all = _dot01(triu, dG_all, _NN)
        da_all = dg_all * (-f["ea"]) * _sigmoid(ba + dtb_)
        dba_ref[...] = (db_all + da_all).astype(bf16)
        ddt_ref[...] += jnp.sum(da_all, axis=0, keepdims=True)
        dal_ref[...] += jnp.sum(dg_all * f["g_all"], axis=0, keepdims=True)
        return dS0

    small = jax.ShapeDtypeStruct((1, 128), f32)
    return pl.pallas_call(
        body, grid=(NS,),
        in_specs=[wide(0), wide(1), wide(2), wide(3), one(4 * H), vec, vec, vec, st_spec(HEAD_DIM), st_spec(C),
                  wide(0), wide(0), wide(0)],
        out_specs=[pl.BlockSpec((K * C, 3 * H * HEAD_DIM), lambda n: (NS - 1 - n, 0)), wide(3), one(0), vec, vec, vec],
        out_shape=[jax.ShapeDtypeStruct((T, 3 * H * HEAD_DIM), f32), jax.ShapeDtypeStruct((T, ODD_PAD), bf16),
                   jax.ShapeDtypeStruct((T, 128), bf16), small, small, small],
        scratch_shapes=[pltpu.VMEM((H, HEAD_DIM, HEAD_DIM), f32)],
        compiler_params=_cparams(("arbitrary",)), name=name,
    )(qkv, qkv, qkv, p, p, alog, dtb, nw, sprev, tinv, vn_all, o_all, dy_all)


def _adamw(w, gs, m, v, name):
    R, Cc = w.shape
    S = gs.shape[0]
    tr = R
    for cand in (256, 128, 64, 32, 16, 8):
        if R % cand == 0 and R > cand:
            tr = cand
            break
    c1 = 1.0 - ADAM_B1 ** ADAM_STEP
    c2 = 1.0 - ADAM_B2 ** ADAM_STEP

    def body(w_ref, g_ref, m_ref, v_ref, go_ref, d_ref, mo_ref, vo_ref):
        g = g_ref[0].astype(f32)
        for s in range(1, S):
            g = g + g_ref[s].astype(f32)
        mn = ADAM_B1 * m_ref[...] + (1.0 - ADAM_B1) * g
        vn = ADAM_B2 * v_ref[...] + (1.0 - ADAM_B2) * (g * g)
        go_ref[...] = g
        mo_ref[...] = mn
        vo_ref[...] = vn
        d_ref[...] = -ADAM_LR * ((mn / c1) / (jnp.sqrt(vn / c2) + ADAM_EPS) + ADAM_WD * w_ref[...])

    blk = pl.BlockSpec((tr, Cc), lambda i: (i, 0))
    out = jax.ShapeDtypeStruct((R, Cc), f32)
    return pl.pallas_call(
        body, grid=(R // tr,),
        in_specs=[blk, pl.BlockSpec((S, tr, Cc), lambda i: (0, i, 0)), blk, blk],
        out_specs=[blk] * 4, out_shape=[out] * 4,
        compiler_params=_cparams(("parallel",)), name=name,
    )(w, gs, m, v)


def _me():
    x, y, c = lax.axis_index("x"), lax.axis_index("y"), lax.axis_index("c")
    return x, y, c, 4 * x + 2 * y + c


def _peer(k):
    x, y, c, _ = _me()
    px = 1 - x if k & 4 else x
    py = 1 - y if k & 2 else y
    pc = 1 - c if k & 1 else c
    return (px, py, pc), 4 * px + 2 * py + pc


_HBM = pl.BlockSpec(memory_space=pltpu.HBM)
_SEM = pl.BlockSpec(memory_space=pltpu.SEMAPHORE)
_EFFECT = pltpu.SideEffectType.DATAFLOW_SIDE_EFFECTING


def _copy(src, land, ssem, rsem, k, blocked, landing_slot_of_peer):
    pid, pidx = _peer(k)
    slot = pidx if landing_slot_of_peer else _me()[3]
    return pltpu.make_async_remote_copy(src_ref=src.at[pidx] if blocked else src, dst_ref=land.at[slot],
                                        send_sem=ssem.at[k - 1], recv_sem=rsem.at[k - 1], device_id=pid, device_id_type=MESH)


def _send_start(srcs, blocked, name):
    n = len(srcs)
    lands = [lax.empty(a.shape if blocked else (N_DEV,) + a.shape, a.dtype) for a in srcs]

    def body(*refs):
        src, land, sems, token = refs[:n], refs[n:2 * n], refs[2 * n:4 * n], refs[-1]
        for i in range(n):
            for k in range(1, N_DEV):
                _copy(src[i], land[i], sems[2 * i], sems[2 * i + 1], k, blocked, False).start()
        token[...] = jnp.zeros_like(token)

    sem = pltpu.SemaphoreType.DMA((N_DEV - 1,))
    hbm = lambda a: pltpu.with_memory_space_constraint(a, pltpu.HBM)
    res = pl.pallas_call(
        body, name=name,
        out_shape=tuple([sem] * (2 * n)) + tuple(pltpu.HBM(a.shape, a.dtype) for a in srcs + lands)
        + (jax.ShapeDtypeStruct((8, 128), f32),),
        in_specs=(_HBM,) * (2 * n),
        out_specs=(_SEM,) * (2 * n) + (_HBM,) * (2 * n) + (pl.BlockSpec(memory_space=pltpu.VMEM),),
        input_output_aliases={j: 2 * n + j for j in range(2 * n)},
        compiler_params=pltpu.CompilerParams(has_side_effects=_EFFECT),
    )(*[hbm(a) for a in srcs], *[hbm(a) for a in lands])
    handles = [(res[2 * i], res[2 * i + 1], res[2 * n + i], res[3 * n + i]) for i in range(n)]
    return handles, res[-1]


def _send_wait(handle, blocked, after, name):
    ssem, rsem, src, land = handle

    def body(src_ref, land_ref, ssem_ref, rsem_ref, after_ref, src_out, land_out):
        for k in range(1, N_DEV):
            cp = _copy(src_ref, land_ref, ssem_ref, rsem_ref, k, blocked, True)
            cp.wait_send()
            cp.wait_recv()

    return pl.pallas_call(
        body, name=name, out_shape=(pltpu.HBM(src.shape, src.dtype), pltpu.HBM(land.shape, land.dtype)),
        in_specs=(_HBM, _HBM, _SEM, _SEM, pl.BlockSpec(memory_space=pl.ANY)), out_specs=(_HBM, _HBM),
        input_output_aliases={0: 0, 1: 1}, compiler_params=pltpu.CompilerParams(has_side_effects=_EFFECT),
    )(src, land, ssem, rsem, after)[1]


def _block_diag(w):
    nb, bs = w.shape[0], w.shape[1]
    eye = jnp.eye(nb, dtype=w.dtype)
    return (eye[:, None, :, None] * w[:, :, None, :]).reshape(nb * bs, nb * bs)


def _diag_blocks(d):
    return jnp.stack([d[g, s * 64:(s + 1) * 64, s * 64:(s + 1) * 64] for g in range(4) for s in range(2)])


def _mlp_fwd(x, nw, wu, wd, tag):
    hm = _rms_fwd(x, nw, f"rms_mlp_{tag}")
    relu_ep = lambda acc: (jnp.maximum(acc, 0.0), jnp.square(jnp.maximum(acc, 0.0)))
    r, act = _matmul(hm, wu, "nn", outs=[bf16, bf16], epilogue=relu_ep, name=f"mlp_up_{tag}")
    (xo,) = _matmul(act, wd, "nn", outs=[f32], extras=(x,), epilogue=lambda acc, res: (res + acc,), name=f"mlp_down_{tag}")
    return xo, (hm, r, act)


def _mlp_bwd(x, nw, wu, wd, saved, dxo, dxo_b, tag):
    hm, r, act = saved
    (du,) = _matmul(dxo_b, wd, "nt", outs=[bf16], extras=(r,), epilogue=lambda acc, rr: (acc * (2.0 * rr.astype(f32)),),
                    name=f"mlp_dact_{tag}")
    (dwd,) = _matmul(act, dxo_b, "tn", outs=[bf16], name=f"mlp_dwd_{tag}")
    (dwu,) = _matmul(hm, du, "tn", outs=[bf16], shard_cols=2, name=f"mlp_dwu_{tag}")
    (dhm,) = _matmul(du, wu, "nt", outs=[f32], name=f"mlp_dh_{tag}")
    dx, dx_b, dnw = _rms_bwd(x, nw, dhm, dxo, f"rms_mlp_bwd_{tag}")
    return dx, dx_b, dnw, dwu, dwd.reshape(N_DEV, D_FF // N_DEV, D_MODEL)


def _local_step(x, tgt, P, weight, sink):
    T = x.shape[0]
    cos, sin = _rope_tables(T)
    rtab = _ret_tables()
    row = lambda a: a.reshape(1, -1)
    mix_nw, mlp_nw = P["mixer_norm_w"], P["mlp_norm_w"]
    wr_bd, wi_bd = _block_diag(P["lru_w_r"]), _block_diag(P["lru_w_i"])
    lru_b, lru_br, lru_bi, lru_lam = row(P["lru_conv_b"]), row(P["lru_b_r"]), row(P["lru_b_i"]), row(P["lru_lambda"])
    pad16 = lambda a: jnp.pad(a.reshape(1, GDN_HEADS), ((0, 0), (GDN_HEADS, 128 - 2 * GDN_HEADS)))
    alog, dtb = pad16(P["gdn_a_log"]), pad16(P["gdn_dt_bias"])
    gnw = row(P["gdn_norm_w"])
    residual = lambda acc, res: (res + acc,)

    x0 = x
    h0 = _rms_fwd(x0, mix_nw[0:1], "rms_mix_0")
    w_ie = weight("w_in_even", h0)
    (pe,) = _matmul(h0, w_ie, "nn", outs=[f32], name="in_even")
    mix0, o_ret, s_ret = _ret_fwd(pe, cos, sin, rtab, "ret_fwd")
    w_lc = weight("lru_conv_w", pe)
    xc = _conv_fwd(pe, 4, w_lc, lru_b, False, "lru_conv_fwd")
    mix0, h_lru = _lru_fwd(xc, pe, 20, wr_bd, wi_bd, lru_br, lru_bi, lru_lam, mix0, "lru_fwd")
    w_oe = weight("w_out_even", mix0)
    (x1,) = _matmul(mix0, w_oe, "nn", outs=[f32], extras=(x0,), epilogue=residual, name="out_even")
    w_u0, w_d0 = weight("w_up0", x1), weight("w_down0", x1)
    x2, mlp0 = _mlp_fwd(x1, mlp_nw[0:1], w_u0, w_d0, "0")
    h1 = _rms_fwd(x2, mix_nw[1:2], "rms_mix_1")
    w_io = weight("w_in_odd", h1)
    (po,) = _matmul(h1, w_io, "nn", outs=[f32], tm=1024, tn=ODD_PAD // 3, name="in_odd")
    w_gc = weight("gdn_conv_w", po)
    qkv = _conv_fwd(po, 0, w_gc, None, True, "gdn_conv_fwd")
    y_gdn, s_gdn, ti_gdn, vn_gdn, o_gdn = _gdn_fwd(qkv, po, alog, dtb, gnw, "gdn_fwd")
    w_oo = weight("w_out_odd", y_gdn)
    (x3,) = _matmul(y_gdn, w_oo, "nn", outs=[f32], extras=(x2,), epilogue=residual, name="out_odd")
    w_u1, w_d1 = weight("w_up1", x3), weight("w_down1", x3)
    x4, mlp1 = _mlp_fwd(x3, mlp_nw[1:2], w_u1, w_d1, "1")
    loss, dx4, dx4_b, d_final = _loss_head(x4, row(P["final_norm_w"]), tgt, "loss_head")
    dx3, dx3_b, d_mlp_nw1, d_wu1, d_wd1 = _mlp_bwd(x3, mlp_nw[1:2], w_u1, w_d1, mlp1, dx4, dx4_b, "1")
    tok = sink(dict(w_up1=d_wu1, w_down1=d_wd1))
    (dy_gdn,) = _matmul(dx3_b, w_oo, "nt", outs=[f32], name="out_odd_dx")
    (d_woo,) = _matmul(y_gdn, dx3_b, "tn", outs=[bf16], name="out_odd_dw")
    dqkv, dpo, dba, d_alog, d_dtb, d_gnw = _gdn_bwd(qkv, po, alog, dtb, gnw + tok[0:1, :], s_gdn, ti_gdn, vn_gdn, o_gdn, dy_gdn,
                                                  "gdn_bwd")
    dpo, d_gconv, _ = _conv_bwd(po, 0, w_gc, None, True, dqkv, dpo, "gdn_conv_bwd")
    dpo = lax.dynamic_update_slice(dpo, dba, (0, 4 * D_MODEL))
    (d_wio,) = _matmul(h1, dpo, "tn", outs=[bf16], tn=ODD_PAD // 3, name="in_odd_dw")
    n_odd = ODD_IN // N_DEV
    tok = sink(dict(w_out_odd=d_woo.reshape(N_DEV, D_MODEL // N_DEV, D_MODEL),
                    w_in_odd=jnp.transpose(d_wio[:, :ODD_IN].reshape(D_MODEL, N_DEV, n_odd), (1, 0, 2))))
    (dh1,) = _matmul(dpo, w_io, "nt", outs=[f32], tm=1024, tk=ODD_PAD, name="in_odd_dx")
    dx2, dx2_b, d_mix_nw1 = _rms_bwd(x2, mix_nw[1:2] + tok[0:1, 0:1], dh1, dx3, "rms_mix_bwd_1")
    dx1, dx1_b, d_mlp_nw0, d_wu0, d_wd0 = _mlp_bwd(x1, mlp_nw[0:1], w_u0, w_d0, mlp0, dx2, dx2_b, "0")
    (d_woe,) = _matmul(mix0, dx1_b, "tn", outs=[bf16], name="out_even_dw")
    tok = sink(dict(w_up0=d_wu0, w_down0=d_wd0, w_out_even=d_woe.reshape(N_DEV, D_MODEL // N_DEV, D_MODEL)))
    (dmix0,) = _matmul(dx1_b, w_oe, "nt", outs=[f32], name="out_even_dx")
    dpe = _ret_bwd(pe, cos, sin, rtab, o_ret, s_ret, dmix0, "ret_bwd")
    dxc, dpe, d_wr, d_wi, d_br, d_bi, d_lam = _lru_bwd(xc, pe, 20, wr_bd, wi_bd, lru_br, lru_bi, lru_lam + tok[0:1, 0:1],
                                                       h_lru, dmix0, 4, dpe, "lru_bwd")
    dpe, d_lconv, d_lconv_b = _conv_bwd(pe, 4, w_lc, lru_b, False, dxc, dpe, "lru_conv_bwd")
    (d_wie,) = _matmul(h0, dpe, "tn", outs=[bf16], shard_cols=2, name="in_even_dw")
    (dh0,) = _matmul(dpe, w_ie, "nt", outs=[f32], name="in_even_dx")
    dx0, _, d_mix_nw0 = _rms_bwd(x0, mix_nw[0:1], dh0, dx1, "rms_mix_bwd_0")

    G = dict(
        mixer_norm_w=jnp.concatenate([d_mix_nw0, d_mix_nw1], axis=0),
        mlp_norm_w=jnp.concatenate([d_mlp_nw0, d_mlp_nw1], axis=0),
        final_norm_w=d_final.reshape(-1),
        w_in_even=d_wie, lru_conv_w=d_lconv, lru_conv_b=d_lconv_b.reshape(-1),
        lru_w_r=_diag_blocks(d_wr), lru_b_r=d_br.reshape(-1), lru_w_i=_diag_blocks(d_wi), lru_b_i=d_bi.reshape(-1),
        lru_lambda=d_lam.reshape(-1), gdn_conv_w=d_gconv,
        gdn_a_log=d_alog[0, GDN_HEADS:2 * GDN_HEADS], gdn_dt_bias=d_dtb[0, GDN_HEADS:2 * GDN_HEADS],
        gdn_norm_w=d_gnw.reshape(-1),
    )
    return loss, dx0, G


_SMALL = ["mixer_norm_w", "mlp_norm_w", "final_norm_w", "lru_conv_b", "lru_w_r", "lru_b_r", "lru_w_i", "lru_b_i",
          "lru_lambda", "gdn_a_log", "gdn_dt_bias", "gdn_norm_w"]
_PACK_ROWS = 688


def _pack(parts):
    flat = jnp.concatenate([p.reshape(-1) for p in parts])
    return jnp.pad(flat, (0, _PACK_ROWS * 128 - flat.shape[0])).reshape(_PACK_ROWS, 128)


def _unpack(packed, shapes):
    flat = packed.reshape(-1)
    out, off = [], 0
    for s in shapes:
        n = int(np.prod(s))
        out.append(flat[off:off + n].reshape(s))
        off += n
    return out


def kernel(x, mixer_norm_w, mlp_norm_w, final_norm_w, w_in_even, lru_conv_w, lru_conv_b, lru_w_r, lru_b_r, lru_w_i, lru_b_i, lru_lambda, w_out_even, w_in_odd, gdn_conv_w, gdn_a_log, gdn_dt_bias, gdn_norm_w, w_out_odd, w_up, w_down, loss_target, m_mixer_norm_w, m_mlp_norm_w, m_final_norm_w, m_w_in_even, m_lru_conv_w, m_lru_conv_b, m_lru_w_r, m_lru_b_r, m_lru_w_i, m_lru_b_i, m_lru_lambda, m_w_out_even, m_w_in_odd, m_gdn_conv_w, m_gdn_a_log, m_gdn_dt_bias, m_gdn_norm_w, m_w_out_odd, m_w_up, m_w_down, v_mixer_norm_w, v_mlp_norm_w, v_final_norm_w, v_w_in_even, v_lru_conv_w, v_lru_conv_b, v_lru_w_r, v_lru_b_r, v_lru_w_i, v_lru_b_i, v_lru_lambda, v_w_out_even, v_w_in_odd, v_gdn_conv_w, v_gdn_a_log, v_gdn_dt_bias, v_gdn_norm_w, v_w_out_odd, v_w_up, v_w_down):
    Pw = dict(mixer_norm_w=mixer_norm_w, mlp_norm_w=mlp_norm_w, final_norm_w=final_norm_w, w_in_even=w_in_even,
              lru_conv_w=lru_conv_w, lru_conv_b=lru_conv_b, lru_w_r=lru_w_r, lru_b_r=lru_b_r, lru_w_i=lru_w_i,
              lru_b_i=lru_b_i, lru_lambda=lru_lambda, w_out_even=w_out_even, w_in_odd=w_in_odd, gdn_conv_w=gdn_conv_w,
              gdn_a_log=gdn_a_log, gdn_dt_bias=gdn_dt_bias, gdn_norm_w=gdn_norm_w, w_out_odd=w_out_odd, w_up=w_up,
              w_down=w_down)
    Pm = dict(mixer_norm_w=m_mixer_norm_w, mlp_norm_w=m_mlp_norm_w, final_norm_w=m_final_norm_w, w_in_even=m_w_in_even,
              lru_conv_w=m_lru_conv_w, lru_conv_b=m_lru_conv_b, lru_w_r=m_lru_w_r, lru_b_r=m_lru_b_r, lru_w_i=m_lru_w_i,
              lru_b_i=m_lru_b_i, lru_lambda=m_lru_lambda, w_out_even=m_w_out_even, w_in_odd=m_w_in_odd,
              gdn_conv_w=m_gdn_conv_w, gdn_a_log=m_gdn_a_log, gdn_dt_bias=m_gdn_dt_bias, gdn_norm_w=m_gdn_norm_w,
              w_out_odd=m_w_out_odd, w_up=m_w_up, w_down=m_w_down)
    Pv = dict(mixer_norm_w=v_mixer_norm_w, mlp_norm_w=v_mlp_norm_w, final_norm_w=v_final_norm_w, w_in_even=v_w_in_even,
              lru_conv_w=v_lru_conv_w, lru_conv_b=v_lru_conv_b, lru_w_r=v_lru_w_r, lru_b_r=v_lru_b_r, lru_w_i=v_lru_w_i,
              lru_b_i=v_lru_b_i, lru_lambda=v_lru_lambda, w_out_even=v_w_out_even, w_in_odd=v_w_in_odd,
              gdn_conv_w=v_gdn_conv_w, gdn_a_log=v_gdn_a_log, gdn_dt_bias=v_gdn_dt_bias, gdn_norm_w=v_gdn_norm_w,
              w_out_odd=v_w_out_odd, w_up=v_w_up, w_down=v_w_down)
    me = _me()[3]
    T = x.shape[1]

    cols = lambda g: jnp.transpose(g, (1, 0, 2)).reshape(g.shape[1], -1)
    rows = lambda g: g.reshape(-1, g.shape[2])
    wide = lambda g: jnp.pad(cols(g), ((0, 0), (0, ODD_PAD - ODD_IN)))
    gather = dict(
        w_in_even=(w_in_even[0].astype(bf16), cols), lru_conv_w=(lru_conv_w[0], cols),
        w_out_even=(w_out_even[0].astype(bf16), rows), w_up0=(w_up[0].astype(bf16), cols), w_down0=(w_down[0].astype(bf16), rows),
        w_in_odd=(w_in_odd[0].astype(bf16), wide), gdn_conv_w=(gdn_conv_w[0], cols),
        w_out_odd=(w_out_odd[0].astype(bf16), rows), w_up1=(w_up[1].astype(bf16), cols), w_down1=(w_down[1].astype(bf16), rows))
    handles, tok = _send_start([s for s, _ in gather.values()], False, "gather_start")
    handles = dict(zip(gather, handles))
    full = {}

    def weight(name, after):
        if name not in full:
            landed = _send_wait(handles[name], False, after, f"gather_wait_{name}")
            shard, finish = gather[name]
            full[name] = finish(lax.dynamic_update_slice_in_dim(landed, shard[None], me, 0))
        return full[name]

    P = {k: Pw[k] for k in ("mlp_norm_w", "final_norm_w")}
    P["mixer_norm_w"] = mixer_norm_w + tok[0:1, 0:1]
    for k in ("lru_w_r", "lru_w_i", "lru_conv_b", "lru_b_r", "lru_b_i", "lru_lambda", "gdn_a_log", "gdn_dt_bias", "gdn_norm_w"):
        P[k] = Pw[k][0]

    sent = {}

    def sink(grads):
        hs, token = _send_start(list(grads.values()), True, "grads_start_" + "_".join(grads))
        for (name, g), h in zip(grads.items(), hs):
            sent[name] = (h, g)
        return token

    loss, dx, G = _local_step(x[0], loss_target[0], P, weight, sink)
    small_g = [G[k].reshape(Pw[k].shape) for k in _SMALL] + [G["lru_conv_w"], G["gdn_conv_w"]]
    packed = _pack(small_g)
    sink(dict(w_in_even=G["w_in_even"], small=jnp.broadcast_to(packed[None], (N_DEV,) + packed.shape)))

    def received(name, after=dx):
        h, g = sent[name]
        landed = _send_wait(h, True, after, f"grads_wait_{name}")
        return lax.dynamic_update_slice_in_dim(landed, lax.dynamic_slice_in_dim(g, me, 1, 0), me, 0)

    out = {}
    nff = D_FF // N_DEV

    def upd(name, gs, shape2d, layer=None):
        pick = (lambda a: a[layer]) if layer is not None else (lambda a: a)
        return _adamw(pick(Pw[name]).reshape(shape2d), gs, pick(Pm[name]).reshape(shape2d), pick(Pv[name]).reshape(shape2d),
                      f"adamw_{name}" + ("" if layer is None else str(layer)))

    def whole(name, gs, shape2d):
        out[name] = tuple(a.reshape(Pw[name].shape) for a in upd(name, gs, shape2d))

    def layers(name, shape2d):
        per = [upd(name, received(f"{name}{l}"), shape2d, l) for l in range(2)]
        out[name] = tuple(jnp.stack([per[0][j], per[1][j]]) for j in range(4))

    layers("w_up", (D_MODEL, nff))
    layers("w_down", (nff, D_MODEL))
    whole("w_out_odd", received("w_out_odd"), (128, D_MODEL))
    whole("w_in_odd", received("w_in_odd"), (D_MODEL, 514))
    whole("w_out_even", received("w_out_even"), (128, D_MODEL))
    whole("w_in_even", received("w_in_even", out["w_out_even"][1]), (D_MODEL, 384))
    small_shapes = [Pw[k].shape for k in _SMALL]
    pw, pm, pv = (_pack([Q[k] for k in _SMALL]) for Q in (Pw, Pm, Pv))
    sg, sd, sm, sv = _adamw(pw, received("small", out["w_in_even"][1]), pm, pv, "adamw_small")
    for arrs_i, packed_out in enumerate((sg, sd, sm, sv)):
        for k, a in zip(_SMALL, _unpack(packed_out, small_shapes)):
            out.setdefault(k, [None] * 4)[arrs_i] = a
    n_small = sum(int(np.prod(s)) for s in small_shapes)
    gflat = sg.reshape(-1)
    g_lconv = gflat[n_small:n_small + CONV_K * LRU_WIDTH].reshape(CONV_K, LRU_WIDTH)
    g_gconv = gflat[n_small + CONV_K * LRU_WIDTH:n_small + CONV_K * (LRU_WIDTH + 3072)].reshape(CONV_K, 3072)
    whole("lru_conv_w", lax.dynamic_slice_in_dim(g_lconv, me * 64, 64, axis=1)[None], (CONV_K, 64))
    whole("gdn_conv_w", lax.dynamic_slice_in_dim(g_gconv, me * 384, 384, axis=1)[None], (CONV_K, 384))

    names = ["mixer_norm_w", "mlp_norm_w", "final_norm_w", "w_in_even", "lru_conv_w", "lru_conv_b", "lru_w_r", "lru_b_r",
             "lru_w_i", "lru_b_i", "lru_lambda", "w_out_even", "w_in_odd", "gdn_conv_w", "gdn_a_log", "gdn_dt_bias",
             "gdn_norm_w", "w_out_odd", "w_up", "w_down"]
    total = lax.psum(loss[0, 0], ("x", "y", "c"))
    res = [total, dx[None]]
    for j in range(4):
        res += [out[k][j] for k in names]
    return tuple(res)
```

```python
import math

import numpy as np
import jax
import jax.numpy as jnp
from jax import lax
from jax.experimental import pallas as pl
from jax.experimental.pallas import tpu as pltpu

f32 = jnp.float32
bf16 = jnp.bfloat16

N_DEV = 8
D_MODEL = 1024
D_FF = 4096
EPS = 1e-6
RET_HEADS = 4
RET_CHUNK = 128
ROPE_THETA = 10000.0
LRU_WIDTH = 512
LRU_C = 8.0
GDN_HEADS = 8
GDN_CHUNK = 64
GDN_STEP = 2
HEAD_DIM = 128
ODD_IN = 4112
ODD_PAD = 4224
ADAM_LR, ADAM_B1, ADAM_B2, ADAM_EPS, ADAM_WD, ADAM_STEP = 0.001, 0.9, 0.999, 1e-08, 0.01, 10
VMEM_LIMIT = 56 * 1024 * 1024

_NN = (((1,), (0,)), ((), ()))
_NT = (((1,), (1,)), ((), ()))
_TN = (((0,), (0,)), ((), ()))
MESH = pl.DeviceIdType.MESH


def _cparams(sem):
    return pltpu.CompilerParams(dimension_semantics=sem, vmem_limit_bytes=VMEM_LIMIT)


def _dot(a, b, dn):
    return lax.dot_general(a.astype(bf16), b.astype(bf16), dn, preferred_element_type=f32)


def _dot01(a01, b, dn):
    a = a01.astype(bf16)
    b0 = b.astype(bf16)
    r1 = b - b0.astype(f32)
    b1 = r1.astype(bf16)
    b2 = (r1 - b1.astype(f32)).astype(bf16)
    d = lambda q: lax.dot_general(a, q, dn, preferred_element_type=f32)
    return d(b0) + (d(b1) + d(b2))


def _sigmoid(x):
    return jax.nn.sigmoid(x)


def _silu(x):
    return x * _sigmoid(x)


def _dsilu(x):
    s = _sigmoid(x)
    return s * (1.0 + x * (1.0 - s))


def _softplus(x):
    return jnp.maximum(x, 0.0) + jnp.log1p(jnp.exp(-jnp.abs(x)))


_GELU_C = math.sqrt(2.0 / math.pi)


def _gelu(y):
    return 0.5 * y * (1.0 + jnp.tanh(_GELU_C * (y + 0.044715 * y * y * y)))


def _dgelu(y):
    t = jnp.tanh(_GELU_C * (y + 0.044715 * y * y * y))
    return 0.5 * (1.0 + t) + 0.5 * y * (1.0 - t * t) * _GELU_C * (1.0 + 3.0 * 0.044715 * y * y)


def _matmul(a, b, form, *, outs, name, epilogue=None, extras=(), tm=2048, tn=512, tk=1024, shard_cols=0, a_map=None):
    if form == "tn":
        K, M = a.shape
    else:
        M, K = a.shape
    N = b.shape[0] if form == "nt" else b.shape[1]
    ns = N // N_DEV
    if shard_cols:
        tn = ns * shard_cols
    tm, tn, tk = min(tm, M), min(tn, N), min(tk, K)
    assert M % tm == 0 and N % tn == 0 and K % tk == 0, (name, M, N, K, tm, tn, tk)
    nk = K // tk
    dn = {"nn": _NN, "nt": _NT, "tn": _TN}[form]
    if form == "tn":
        a_spec = pl.BlockSpec((tk, tm), lambda i, j, k: (k, i))
    else:
        a_spec = pl.BlockSpec((tm, tk), lambda i, j, k: (i, k))
    if form == "nt":
        b_spec = pl.BlockSpec((tn, tk), lambda i, j, k: (j, k))
    else:
        b_spec = pl.BlockSpec((tk, tn), lambda i, j, k: (k, j))
    e_spec = pl.BlockSpec((tm, tn), lambda i, j, k: (i, j))
    if shard_cols:
        o_spec = pl.BlockSpec((shard_cols, tm, ns), lambda i, j, k: (j, i, 0))
        o_shape = (N_DEV, M, ns)
    else:
        o_spec = e_spec
        o_shape = (M, N)
    n_ex = len(extras)

    def finish(acc, ex, o_refs):
        vals = (acc,) if epilogue is None else epilogue(acc, *[e[...] for e in ex])
        for r, v in zip(o_refs, vals):
            if shard_cols:
                for s in range(shard_cols):
                    r[s] = v[:, s * ns:(s + 1) * ns].astype(r.dtype)
            else:
                r[...] = v.astype(r.dtype)

    def prod(a_ref, b_ref):
        av = a_ref[...]
        return _dot(av if a_map is None else a_map(av), b_ref[...], dn)

    def body_one(*refs):
        finish(prod(*refs[:2]), refs[2:2 + n_ex], refs[2 + n_ex:])

    def body_acc(*refs):
        a_ref, b_ref = refs[:2]
        acc = refs[-1]
        k = pl.program_id(2)

        @pl.when(k == 0)
        def _():
            acc[...] = prod(a_ref, b_ref)

        @pl.when((k > 0) & (k < nk - 1))
        def _():
            acc[...] += prod(a_ref, b_ref)

        @pl.when(k == nk - 1)
        def _():
            finish(acc[...] + prod(a_ref, b_ref), refs[2:2 + n_ex], refs[2 + n_ex:-1])

    return pl.pallas_call(
        body_one if nk == 1 else body_acc, grid=(M // tm, N // tn, nk),
        in_specs=[a_spec, b_spec] + [e_spec] * n_ex,
        out_specs=[o_spec] * len(outs),
        out_shape=[jax.ShapeDtypeStruct(o_shape, d) for d in outs],
        scratch_shapes=[] if nk == 1 else [pltpu.VMEM((tm, tn), f32)],
        compiler_params=_cparams(("parallel", "parallel", "arbitrary")), name=name,
    )(a, b, *extras)


def _rms_fwd(x, w, name):
    T, D = x.shape
    tt = min(512, T)

    def body(x_ref, w_ref, h_ref):
        xv = x_ref[...]
        r = lax.rsqrt(jnp.mean(xv * xv, axis=1, keepdims=True) + EPS)
        h_ref[...] = (xv * r * w_ref[...]).astype(bf16)

    return pl.pallas_call(
        body, grid=(T // tt,),
        in_specs=[pl.BlockSpec((tt, D), lambda i: (i, 0)), pl.BlockSpec((1, D), lambda i: (0, 0))],
        out_specs=pl.BlockSpec((tt, D), lambda i: (i, 0)),
        out_shape=jax.ShapeDtypeStruct((T, D), bf16),
        compiler_params=_cparams(("parallel",)), name=name,
    )(x, w)


def _rms_bwd(x, w, dh, dres, name):
    T, D = x.shape
    tt = min(512, T)

    def body(x_ref, w_ref, dh_ref, dres_ref, dx_ref, dxb_ref, dw_ref):
        @pl.when(pl.program_id(0) == 0)
        def _():
            dw_ref[...] = jnp.zeros_like(dw_ref)

        xv = x_ref[...]
        r = lax.rsqrt(jnp.mean(xv * xv, axis=1, keepdims=True) + EPS)
        xn = xv * r
        dhv = dh_ref[...]
        dhw = dhv * w_ref[...]
        dx = dres_ref[...] + r * (dhw - xn * jnp.mean(dhw * xn, axis=1, keepdims=True))
        dx_ref[...] = dx
        dxb_ref[...] = dx.astype(bf16)
        dw_ref[...] += jnp.sum(dhv * xn, axis=0, keepdims=True)

    row = pl.BlockSpec((tt, D), lambda i: (i, 0))
    vec = pl.BlockSpec((1, D), lambda i: (0, 0))
    return pl.pallas_call(
        body, grid=(T // tt,), in_specs=[row, vec, row, row], out_specs=[row, row, vec],
        out_shape=[jax.ShapeDtypeStruct((T, D), f32), jax.ShapeDtypeStruct((T, D), bf16), jax.ShapeDtypeStruct((1, D), f32)],
        compiler_params=_cparams(("arbitrary",)), name=name,
    )(x, w, dh, dres)


def _loss_head(x, w, tgt, name):
    T, D = x.shape
    tt = min(512, T)

    def body(x_ref, w_ref, t_ref, l_ref, dx_ref, dxb_ref, dw_ref):
        @pl.when(pl.program_id(0) == 0)
        def _():
            dw_ref[...] = jnp.zeros_like(dw_ref)
            l_ref[...] = jnp.zeros_like(l_ref)

        xv = x_ref[...]
        wv = w_ref[...]
        r = lax.rsqrt(jnp.mean(xv * xv, axis=1, keepdims=True) + EPS)
        xn = xv * r
        e = xn * wv - t_ref[...]
        l_ref[...] += 0.5 * jnp.sum(jnp.mean(e * e, axis=1, keepdims=True), axis=0, keepdims=True)
        dy = e * (1.0 / D)
        dyw = dy * wv
        dx = r * (dyw - xn * jnp.mean(dyw * xn, axis=1, keepdims=True))
        dx_ref[...] = dx
        dxb_ref[...] = dx.astype(bf16)
        dw_ref[...] += jnp.sum(dy * xn, axis=0, keepdims=True)

    row = pl.BlockSpec((tt, D), lambda i: (i, 0))
    vec = pl.BlockSpec((1, D), lambda i: (0, 0))
    return pl.pallas_call(
        body, grid=(T // tt,), in_specs=[row, vec, row],
        out_specs=[pl.BlockSpec((1, 128), lambda i: (0, 0)), row, row, vec],
        out_shape=[jax.ShapeDtypeStruct((1, 128), f32), jax.ShapeDtypeStruct((T, D), f32),
                   jax.ShapeDtypeStruct((T, D), bf16), jax.ShapeDtypeStruct((1, D), f32)],
        compiler_params=_cparams(("arbitrary",)), name=name,
    )(x, w, tgt)


def _ret_tables():
    H, C = RET_HEADS, RET_CHUNK
    lg = np.log1p(-np.exp2(-5.0 - np.arange(H, dtype=np.float32))).astype(np.float32)
    idx = np.arange(C, dtype=np.float32)
    diff = idx[:, None] - idx[None, :]
    causal = diff >= 0
    dm = np.where(causal[None], np.exp(lg[:, None, None] * np.where(causal, diff, 0.0)[None]), 0.0)
    qd = np.exp(lg[:, None] * (idx[None, :] + 1.0))
    kd = np.exp(lg[:, None] * (C - 1.0 - idx[None, :]))
    cg = np.exp(lg * C)
    tab = np.zeros((H, 4, C, HEAD_DIM), np.float32)
    tab[:, 0] = dm
    tab[:, 1] = qd[:, :, None]
    tab[:, 2] = kd[:, :, None]
    tab[:, 3] = cg[:, None, None]
    return jnp.asarray(tab)


def _rope_tables(T):
    half = HEAD_DIM // 2
    inv = ROPE_THETA ** (-jnp.arange(half, dtype=f32) / half)
    ang = jnp.arange(T, dtype=jnp.int32).astype(f32)[:, None] * inv[None, :]
    c, s = jnp.cos(ang), jnp.sin(ang)
    return jnp.concatenate([c, c], axis=1), jnp.concatenate([-s, s], axis=1)


def _rope(x, cos, sin):
    return x * cos + pltpu.roll(x, HEAD_DIM // 2, 1) * sin


def _unrope(y, cos, sin):
    return y * cos + pltpu.roll(y * sin, HEAD_DIM // 2, 1)


def _stack_heads(ref, H, f=None):
    parts = [ref[:, h * HEAD_DIM:(h + 1) * HEAD_DIM] for h in range(H)]
    return jnp.stack(parts if f is None else [f(a) for a in parts])


def _ret_fwd(p, cos, sin, tab, name):
    T = p.shape[0]
    C, H = RET_CHUNK, RET_HEADS
    N = T // C
    scale = HEAD_DIM ** -0.5

    def body(q_ref, k_ref, v_ref, g_ref, c_ref, s_ref, t_ref, y_ref, o_ref, sp_ref, st):
        @pl.when(pl.program_id(0) == 0)
        def _():
            st[...] = jnp.zeros_like(st)

        cos_, sin_ = c_ref[...], s_ref[...]
        rot = lambda a: _rope(a, cos_, sin_)
        q = _stack_heads(q_ref, H, rot)
        k = _stack_heads(k_ref, H, rot) * scale
        v = _stack_heads(v_ref, H)
        dm, qd, kd, cg = t_ref[:, 0], t_ref[:, 1], t_ref[:, 2], t_ref[:, 3]
        S = st[...]
        P = _dot(q, k, _NT3) * dm
        o = _dot(P, v, _NN3) + _dot(q * qd, S, _NN3)
        sp_ref[0] = S
        st[...] = cg * S + _dot(k * kd, v, _TN3)
        r = lax.rsqrt(jnp.mean(o * o, axis=2, keepdims=True) + EPS)
        y = o * r * _silu(_stack_heads(g_ref, H))
        for h in range(H):
            o_ref[:, h * HEAD_DIM:(h + 1) * HEAD_DIM] = o[h]
            y_ref[:, h * HEAD_DIM:(h + 1) * HEAD_DIM] = y[h].astype(bf16)

    wide = lambda blk: pl.BlockSpec((C, H * HEAD_DIM), lambda n: (n, blk))
    tbl = pl.BlockSpec((C, HEAD_DIM), lambda n: (n, 0))
    return pl.pallas_call(
        body, grid=(N,),
        in_specs=[wide(0), wide(1), wide(2), wide(3), tbl, tbl,
                  pl.BlockSpec((H, 4, C, HEAD_DIM), lambda n: (0, 0, 0, 0))],
        out_specs=[wide(0), wide(0), pl.BlockSpec((1, H, HEAD_DIM, HEAD_DIM), lambda n: (n, 0, 0, 0))],
        out_shape=[jax.ShapeDtypeStruct((T, D_MODEL), bf16), jax.ShapeDtypeStruct((T, H * HEAD_DIM), f32),
                   jax.ShapeDtypeStruct((N, H, HEAD_DIM, HEAD_DIM), f32)],
        scratch_shapes=[pltpu.VMEM((H, HEAD_DIM, HEAD_DIM), f32)],
        compiler_params=_cparams(("arbitrary",)), name=name,
    )(p, p, p, p, cos, sin, tab)


def _ret_bwd(p, cos, sin, tab, o_raw, sprev, dmix, name):
    T = p.shape[0]
    C, H = RET_CHUNK, RET_HEADS
    N = T // C
    scale = HEAD_DIM ** -0.5

    W = H * HEAD_DIM

    def body(q_ref, k_ref, v_ref, g_ref, c_ref, s_ref, t_ref, o_ref, sp_ref, dy_ref, d_ref, dst):
        @pl.when(pl.program_id(0) == 0)
        def _():
            dst[...] = jnp.zeros_like(dst)

        cos_, sin_ = c_ref[...], s_ref[...]
        rot = lambda a: _rope(a, cos_, sin_)
        q = _stack_heads(q_ref, H, rot)
        k = _stack_heads(k_ref, H, rot) * scale
        v = _stack_heads(v_ref, H)
        g = _stack_heads(g_ref, H)
        dm, qd, kd, cg = t_ref[:, 0], t_ref[:, 1], t_ref[:, 2], t_ref[:, 3]
        S = sp_ref[0]
        o = _stack_heads(o_ref, H)
        dy = _stack_heads(dy_ref, H)
        r = lax.rsqrt(jnp.mean(o * o, axis=2, keepdims=True) + EPS)
        nrm = o * r
        dn = dy * _silu(g)
        dg = dy * nrm * _dsilu(g)
        do = r * (dn - nrm * jnp.mean(dn * nrm, axis=2, keepdims=True))
        dS1 = dst[...]
        P = _dot(q, k, _NT3) * dm
        dP = _dot(do, v, _NT3) * dm
        dq = _dot(dP, k, _NN3) + _dot(do, S, _NT3) * qd
        dk = (_dot(dP, q, _TN3) + _dot(v, dS1, _NT3) * kd) * scale
        dv = _dot(P, do, _TN3) + _dot(k * kd, dS1, _NN3)
        dst[...] = cg * dS1 + _dot(q * qd, do, _TN3)
        for h in range(H):
            d_ref[:, h * HEAD_DIM:(h + 1) * HEAD_DIM] = _unrope(dq[h], cos_, sin_).astype(bf16)
            d_ref[:, W + h * HEAD_DIM:W + (h + 1) * HEAD_DIM] = _unrope(dk[h], cos_, sin_).astype(bf16)
            d_ref[:, 2 * W + h * HEAD_DIM:2 * W + (h + 1) * HEAD_DIM] = dv[h].astype(bf16)
            d_ref[:, 3 * W + h * HEAD_DIM:3 * W + (h + 1) * HEAD_DIM] = dg[h].astype(bf16)

    rev = lambda blk: pl.BlockSpec((C, W), lambda n: (N - 1 - n, blk))
    tbl = pl.BlockSpec((C, HEAD_DIM), lambda n: (N - 1 - n, 0))
    return pl.pallas_call(
        body, grid=(N,),
        in_specs=[rev(0), rev(1), rev(2), rev(3), tbl, tbl,
                  pl.BlockSpec((H, 4, C, HEAD_DIM), lambda n: (0, 0, 0, 0)), rev(0),
                  pl.BlockSpec((1, H, HEAD_DIM, HEAD_DIM), lambda n: (N - 1 - n, 0, 0, 0)), rev(0)],
        out_specs=pl.BlockSpec((C, 4 * W), lambda n: (N - 1 - n, 0)),
        out_shape=jax.ShapeDtypeStruct((T, 6 * W), bf16),
        scratch_shapes=[pltpu.VMEM((H, HEAD_DIM, HEAD_DIM), f32)],
        compiler_params=_cparams(("arbitrary",)), name=name,
    )(p, p, p, p, cos, sin, tab, o_raw, sprev, dmix)


CONV_K = 4
CONV_W = 512
PAD = 8
SUB_R = 64


def _conv_fwd(x, col_off, w, b, act, name):
    T = x.shape[0]
    C = w.shape[1]
    G = C // CONV_W
    tt = min(512, T)
    NT = T // tt
    has_b = b is not None

    def body(*refs):
        if has_b:
            x_ref, w_ref, b_ref, y_ref, pad = refs
        else:
            x_ref, w_ref, y_ref, pad = refs
        t = pl.program_id(1)

        @pl.when(t == 0)
        def _():
            pad[pl.ds(0, PAD), :] = jnp.zeros((PAD, CONV_W), f32)

        pad[pl.ds(PAD, tt), :] = x_ref[...]
        for g in range(CONV_W // 128):
            ls = slice(g * 128, (g + 1) * 128)
            wv = w_ref[:, ls]
            for c in range(tt // SUB_R):
                r0 = c * SUB_R
                y = wv[0:1, :] * pad[pl.ds(PAD - 3 + r0, SUB_R), ls]
                for kk in range(1, CONV_K):
                    y = y + wv[kk:kk + 1, :] * pad[pl.ds(PAD - 3 + kk + r0, SUB_R), ls]
                if has_b:
                    y = y + b_ref[:, ls]
                y_ref[pl.ds(r0, SUB_R), ls] = _silu(y) if act else y
        tail = pad[pl.ds(tt, PAD), :]
        pad[pl.ds(0, PAD), :] = tail

    in_specs = [pl.BlockSpec((tt, CONV_W), lambda g, t: (t, col_off + g)),
                pl.BlockSpec((CONV_K, CONV_W), lambda g, t: (0, g))]
    args = [x, w]
    if has_b:
        in_specs.append(pl.BlockSpec((1, CONV_W), lambda g, t: (0, g)))
        args.append(b)
    return pl.pallas_call(
        body, grid=(G, NT), in_specs=in_specs,
        out_specs=pl.BlockSpec((tt, CONV_W), lambda g, t: (t, g)),
        out_shape=jax.ShapeDtypeStruct((T, C), f32),
        scratch_shapes=[pltpu.VMEM((tt + PAD, CONV_W), f32)],
        compiler_params=_cparams(("parallel", "arbitrary")), name=name,
    )(*args)


def _conv_bwd(x, col_off, w, b, act, dout, dp, name):
    T = x.shape[0]
    C = w.shape[1]
    G = C // CONV_W
    tt = min(512, T)
    NT = T // tt
    has_b = b is not None

    def body(*refs):
        if has_b:
            x_ref, xp_ref, w_ref, b_ref, d_ref, dp_in, dx_ref, dw_ref, db_ref, pad, dpad = refs
        else:
            x_ref, xp_ref, w_ref, d_ref, dp_in, dx_ref, dw_ref, db_ref, pad, dpad = refs
        t = pl.program_id(1)
        first_tile = t == NT - 1

        @pl.when(t == 0)
        def _():
            dpad[pl.ds(tt, PAD), :] = jnp.zeros((PAD, CONV_W), f32)
            dw_ref[...] = jnp.zeros_like(dw_ref)
            db_ref[...] = jnp.zeros_like(db_ref)

        pad[pl.ds(0, PAD), :] = jnp.where(first_tile, 0.0, xp_ref[...])
        pad[pl.ds(PAD, tt), :] = x_ref[...]
        fold = lambda v: v.reshape(SUB_R // 8, 8, 128).sum(axis=0)
        for g in range(CONV_W // 128):
            ls = slice(g * 128, (g + 1) * 128)
            wv = w_ref[:, ls]
            acc = [jnp.zeros((8, 128), f32) for _ in range(CONV_K + 1)]
            for c in reversed(range(tt // SUB_R)):
                r0 = c * SUB_R
                xs = [pad[pl.ds(PAD - 3 + kk + r0, SUB_R), ls] for kk in range(CONV_K)]
                dy = d_ref[pl.ds(r0, SUB_R), ls]
                if act:
                    y = wv[0:1, :] * xs[0]
                    for kk in range(1, CONV_K):
                        y = y + wv[kk:kk + 1, :] * xs[kk]
                    if has_b:
                        y = y + b_ref[:, ls]
                    dy = dy * _dsilu(y)
                dpad[pl.ds(r0, SUB_R), ls] = dy
                dx = wv[3:4, :] * dy
                for j in range(1, CONV_K):
                    dx = dx + wv[3 - j:4 - j, :] * dpad[pl.ds(r0 + j, SUB_R), ls]
                dx_ref[pl.ds(r0, SUB_R), ls] = dx.astype(bf16)
                for kk in range(CONV_K):
                    acc[kk] = acc[kk] + fold(dy * xs[kk])
                acc[CONV_K] = acc[CONV_K] + fold(dy)
            for kk in range(CONV_K):
                dw_ref[kk:kk + 1, ls] += jnp.sum(acc[kk], axis=0, keepdims=True)
            db_ref[:, ls] += jnp.sum(acc[CONV_K], axis=0, keepdims=True)
        head = dpad[pl.ds(0, PAD), :]
        dpad[pl.ds(tt, PAD), :] = head

    rows8 = tt // PAD
    in_specs = [pl.BlockSpec((tt, CONV_W), lambda g, t: (NT - 1 - t, col_off + g)),
                pl.BlockSpec((PAD, CONV_W), lambda g, t: (jnp.maximum((NT - 1 - t) * rows8 - 1, 0), col_off + g)),
                pl.BlockSpec((CONV_K, CONV_W), lambda g, t: (0, g))]
    args = [x, x, w]
    if has_b:
        in_specs.append(pl.BlockSpec((1, CONV_W), lambda g, t: (0, g)))
        args.append(b)
    in_specs += [pl.BlockSpec((tt, CONV_W), lambda g, t: (NT - 1 - t, g)), pl.BlockSpec(memory_space=pl.ANY)]
    args += [dout, dp]
    return pl.pallas_call(
        body, grid=(G, NT), in_specs=in_specs,
        out_specs=[pl.BlockSpec((tt, CONV_W), lambda g, t: (NT - 1 - t, col_off + g)),
                   pl.BlockSpec((CONV_K, CONV_W), lambda g, t: (0, g)),
                   pl.BlockSpec((1, CONV_W), lambda g, t: (0, g))],
        out_shape=[jax.ShapeDtypeStruct(dp.shape, dp.dtype), jax.ShapeDtypeStruct((CONV_K, C), f32),
                   jax.ShapeDtypeStruct((1, C), f32)],
        input_output_aliases={len(args) - 1: 0},
        scratch_shapes=[pltpu.VMEM((tt + PAD, CONV_W), f32), pltpu.VMEM((tt + PAD, CONV_W), f32)],
        compiler_params=_cparams(("parallel", "arbitrary")), name=name,
    )(*args)


def _lru_gates(xc, wr, wi, br, bi, lam):
    r = _sigmoid(_dot(xc, wr, _NN) + br)
    i = _sigmoid(_dot(xc, wi, _NN) + bi)
    sp = _softplus(-lam)
    a = jnp.exp(-LRU_C * r * sp)
    mult = jnp.sqrt(1.0 - a * a)
    return r, i, sp, a, mult


def _lru_fwd(xc, p, y_off, wr, wi, br, bi, lam, mix, name):
    T = xc.shape[0]
    G = LRU_WIDTH // 128
    tt = min(512, T)
    NT = T // tt

    def body(x_ref, y_ref, wr_ref, wi_ref, br_ref, bi_ref, l_ref, mix_in, o_ref, h_ref, hc):
        t = pl.program_id(1)

        @pl.when(t == 0)
        def _():
            hc[...] = jnp.zeros_like(hc)

        x = x_ref[...]
        r, i, sp, a, mult = _lru_gates(x, wr_ref[...], wi_ref[...], br_ref[...], bi_ref[...], l_ref[...])
        row = lax.broadcasted_iota(jnp.int32, (tt, 128), 0)
        mult = jnp.where((row == 0) & (t == 0), 1.0, mult)
        U = x * i * mult
        A = a
        d = 1
        while d < tt:
            keep = row >= d
            Ush = jnp.where(keep, pltpu.roll(U, d, 0), 0.0)
            Ash = jnp.where(keep, pltpu.roll(A, d, 0), 1.0)
            U = A * Ush + U
            A = A * Ash
            d *= 2
        h = U + A * hc[0:1, :]
        h_ref[...] = h
        hc[...] = jnp.broadcast_to(h[tt - 1:tt, :], hc.shape)
        o_ref[...] = (h * _gelu(y_ref[...])).astype(bf16)

    tile = pl.BlockSpec((tt, 128), lambda g, t: (t, g))
    vec = pl.BlockSpec((1, 128), lambda g, t: (0, g))
    wsp = pl.BlockSpec((128, 128), lambda g, t: (g, g))
    return pl.pallas_call(
        body, grid=(G, NT),
        in_specs=[tile, pl.BlockSpec((tt, 128), lambda g, t: (t, y_off + g)), wsp, wsp, vec, vec, vec,
                  pl.BlockSpec(memory_space=pl.ANY)],
        out_specs=[pl.BlockSpec((tt, 128), lambda g, t: (t, G + g)), tile],
        out_shape=[jax.ShapeDtypeStruct(mix.shape, mix.dtype), jax.ShapeDtypeStruct((T, LRU_WIDTH), f32)],
        input_output_aliases={7: 0},
        scratch_shapes=[pltpu.VMEM((8, 128), f32)],
        compiler_params=_cparams(("parallel", "arbitrary")), name=name,
    )(xc, p, wr, wi, br, bi, lam, mix)


def _lru_bwd(xc, p, y_off, wr, wi, br, bi, lam, hs, dmix, d_off, dp, name):
    T = xc.shape[0]
    G = LRU_WIDTH // 128
    tt = min(512, T)
    NT = T // tt

    def body(x_ref, y_ref, wr_ref, wi_ref, br_ref, bi_ref, l_ref, h_ref, hp_ref, do_ref, dp_in,
             dx_ref, dy_ref, dwr_ref, dwi_ref, dbr_ref, dbi_ref, dl_ref, lc, an):
        t = pl.program_id(1)
        first_tile = t == NT - 1

        @pl.when(t == 0)
        def _():
            lc[...] = jnp.zeros_like(lc)
            an[...] = jnp.zeros_like(an)
            dwr_ref[...] = jnp.zeros_like(dwr_ref)
            dwi_ref[...] = jnp.zeros_like(dwi_ref)
            dbr_ref[...] = jnp.zeros_like(dbr_ref)
            dbi_ref[...] = jnp.zeros_like(dbi_ref)
            dl_ref[...] = jnp.zeros_like(dl_ref)

        x = x_ref[...]
        y = y_ref[...]
        wr, wi, lam_ = wr_ref[...], wi_ref[...], l_ref[...]
        r, i, sp, a, mult_raw = _lru_gates(x, wr, wi, br_ref[...], bi_ref[...], lam_)
        row = lax.broadcasted_iota(jnp.int32, (tt, 128), 0)
        t0 = (row == 0) & first_tile
        mult = jnp.where(t0, 1.0, mult_raw)
        h = h_ref[...]
        do = do_ref[...]
        dh = do * _gelu(y)
        dy_ref[...] = (do * h * _dgelu(y)).astype(bf16)
        B = jnp.where(row == tt - 1, an[0:1, :], pltpu.roll(a, tt - 1, 0))
        L = dh
        d = 1
        while d < tt:
            keep = row < tt - d
            Lsh = jnp.where(keep, pltpu.roll(L, tt - d, 0), 0.0)
            Bsh = jnp.where(keep, pltpu.roll(B, tt - d, 0), 1.0)
            L = L + B * Lsh
            B = B * Bsh
            d *= 2
        L = L + B * lc[0:1, :]
        lc[...] = jnp.broadcast_to(L[0:1, :], lc.shape)
        an[...] = jnp.broadcast_to(a[0:1, :], an.shape)
        hprev = jnp.where(first_tile, 0.0, hp_ref[...])[PAD - 1:PAD, :]
        hm1 = jnp.where(row == 0, hprev, pltpu.roll(h, 1, 0))
        da = L * hm1
        dxc = L * i * mult
        di = L * x * mult
        dmult = jnp.where(t0, 0.0, L * x * i)
        da = da - jnp.where(t0, 0.0, dmult * a / mult_raw)
        dlog_a = da * a
        dr = dlog_a * (-LRU_C) * sp
        dsp = jnp.sum(dlog_a * (-LRU_C) * r, axis=0, keepdims=True)
        dpr = dr * r * (1.0 - r)
        dpi = di * i * (1.0 - i)
        dx_ref[...] = dxc + _dot(dpr, wr, _NT) + _dot(dpi, wi, _NT)
        dwr_ref[0] += _dot(x, dpr, _TN)
        dwi_ref[0] += _dot(x, dpi, _TN)
        dbr_ref[...] += jnp.sum(dpr, axis=0, keepdims=True)
        dbi_ref[...] += jnp.sum(dpi, axis=0, keepdims=True)
        dl_ref[...] += dsp * (-_sigmoid(-lam_))

    rows8 = tt // PAD
    tile = pl.BlockSpec((tt, 128), lambda g, t: (NT - 1 - t, g))
    vec = pl.BlockSpec((1, 128), lambda g, t: (0, g))
    wsp = pl.BlockSpec((128, 128), lambda g, t: (g, g))
    wout = pl.BlockSpec((1, 128, 128), lambda g, t: (g, 0, 0))
    return pl.pallas_call(
        body, grid=(G, NT),
        in_specs=[tile, pl.BlockSpec((tt, 128), lambda g, t: (NT - 1 - t, y_off + g)), wsp, wsp, vec, vec, vec, tile,
                  pl.BlockSpec((PAD, 128), lambda g, t: (jnp.maximum((NT - 1 - t) * rows8 - 1, 0), g)),
                  pl.BlockSpec((tt, 128), lambda g, t: (NT - 1 - t, d_off + g)), pl.BlockSpec(memory_space=pl.ANY)],
        out_specs=[tile, pl.BlockSpec((tt, 128), lambda g, t: (NT - 1 - t, y_off + g)), wout, wout, vec, vec, vec],
        out_shape=[jax.ShapeDtypeStruct((T, LRU_WIDTH), f32), jax.ShapeDtypeStruct(dp.shape, dp.dtype),
                   jax.ShapeDtypeStruct((G, 128, 128), f32), jax.ShapeDtypeStruct((G, 128, 128), f32),
                   jax.ShapeDtypeStruct((1, LRU_WIDTH), f32), jax.ShapeDtypeStruct((1, LRU_WIDTH), f32),
                   jax.ShapeDtypeStruct((1, LRU_WIDTH), f32)],
        input_output_aliases={10: 1},
        scratch_shapes=[pltpu.VMEM((8, 128), f32), pltpu.VMEM((8, 128), f32)],
        compiler_params=_cparams(("parallel", "arbitrary")), name=name,
    )(xc, p, wr, wi, br, bi, lam, hs, hs, dmix, dp)


_NN3 = (((2,), (1,)), ((0,), (0,)))
_NT3 = (((2,), (2,)), ((0,), (0,)))
_TN3 = (((1,), (1,)), ((0,), (0,)))


def _heads(ref):
    return _stack_heads(ref, GDN_HEADS)


def _rowsum(x):
    H, C, L = x.shape
    return _dot(x.reshape(H * C, L), jnp.ones((L, HEAD_DIM), f32), _NN).reshape(H, C, HEAD_DIM)


def _gdn_chunk(qr, kr, v, ba, alog, dtb, S, saved=None):
    C, H = GDN_CHUNK, GDN_HEADS
    lane = lax.broadcasted_iota(jnp.int32, (C, 128), 1)
    lane3 = lax.broadcasted_iota(jnp.int32, (H, C, 128), 2)
    ri = lax.broadcasted_iota(jnp.int32, (C, C), 0)
    ci = lax.broadcasted_iota(jnp.int32, (C, C), 1)
    rowc = lax.broadcasted_iota(jnp.int32, (C, 1), 0)
    col = lambda m, j: jnp.sum(jnp.where(lane == j, m, 0.0), axis=1, keepdims=True)
    cols = lambda m, off: jnp.stack([col(m, off + h) for h in range(H)])
    ea = jnp.exp(alog)
    g_all = -ea * _softplus(ba + dtb)
    tri = (ri >= ci).astype(f32)
    G_all = _dot01(tri, g_all, _NN)
    beta = cols(_sigmoid(ba), 0)
    Gc = cols(G_all, H)
    rq = lax.rsqrt(_rowsum(qr * qr) + EPS)
    rk = lax.rsqrt(_rowsum(kr * kr) + EPS)
    qh, kn = qr * rq, kr * rk
    qn = qh * (HEAD_DIM ** -0.5)
    Grow = _dot01(jnp.ones((H, C, 128), f32), jnp.where(lane3 == 0, Gc, 0.0), _NT3)
    incl = ri >= ci
    Di = jnp.where(incl, jnp.exp(jnp.where(incl, Gc - Grow, 0.0)), 0.0)
    Ds = jnp.where(ri > ci, Di, 0.0)
    Gl = jnp.sum(jnp.where(rowc == C - 1, Gc, 0.0), axis=1, keepdims=True)
    eG = jnp.exp(Gc)
    eGl = jnp.exp(Gl - Gc)
    cd = jnp.exp(Gl)
    kb = kn * beta
    vb = v * beta
    Lm = _dot(kb, kn, _NT3) * Ds
    kbg = kb * eG
    QK = _dot(qn, kn, _NT3) * Di
    qg = qn * eG
    kg = kn * eGl
    if saved is None:
        same = lambda s: (ri // s) == (ci // s)
        Xd = jnp.where(same(8), -Lm, 0.0)
        Tinv = (ri == ci).astype(f32) + Xd
        Pw = Xd
        for _ in range(2):
            Pw = _dot(Pw, Pw, _NN3)
            Tinv = Tinv + _dot(Tinv, Pw, _NN3)
        for s in (8, 16, 32):
            off = jnp.where(same(2 * s) & jnp.logical_not(same(s)), Lm, 0.0)
            Tinv = Tinv - _dot(_dot(Tinv, off, _NN3), Tinv, _NN3)
        w = _dot(Tinv, kbg, _NN3)
        vn = _dot(Tinv, vb, _NN3) - _dot(w, S, _NN3)
        o = _dot(qg, S, _NN3) + _dot(QK, vn, _NN3)
        S1 = S * cd + _dot(kg, vn, _TN3)
    else:
        Tinv, vn, o = saved
        w = _dot(Tinv, kbg, _NN3)
        S1 = None
    return dict(beta=beta, g_all=g_all, rq=rq, rk=rk, qh=qh, kn=kn, qn=qn, Di=Di, Ds=Ds, eG=eG, eGl=eGl, cd=cd,
                kb=kb, vb=vb, Lm=Lm, Tinv=Tinv, kbg=kbg, w=w, QK=QK, qg=qg, kg=kg, vn=vn, o=o, S1=S1,
                lane=lane, ri=ri, ci=ci, rowc=rowc, ea=ea)


def _gdn_specs(T, rev):
    C = GDN_CHUNK
    H = GDN_HEADS
    K = min(GDN_STEP, T // C)
    NS = T // (C * K)
    nn = (lambda n: NS - 1 - n) if rev else (lambda n: n)
    wide = lambda blk: pl.BlockSpec((K * C, H * HEAD_DIM), lambda n: (nn(n), blk))
    one = lambda off: pl.BlockSpec((K * C, HEAD_DIM), lambda n: (nn(n), off))
    vec = pl.BlockSpec((1, 128), lambda n: (0, 0))
    st = lambda rows: pl.BlockSpec((K, H, rows, rows), lambda n: (nn(n), 0, 0, 0))
    return K, NS, wide, one, vec, st


def _gdn_fwd(qkv, p, alog, dtb, nw, name):
    T = qkv.shape[0]
    C, H = GDN_CHUNK, GDN_HEADS
    N = T // C
    K, NS, wide, one, vec, st_spec = _gdn_specs(T, False)

    def body(q_ref, k_ref, v_ref, z_ref, ba_ref, al_ref, dt_ref, nw_ref, y_ref, sp_ref, ti_ref, vn_ref, o_ref, st):
        @pl.when(pl.program_id(0) == 0)
        def _():
            st[...] = jnp.zeros_like(st)

        S = st[...]
        for c in range(K):
            rows = pl.ds(c * C, C)
            at = lambda ref: ref.at[rows, :]
            f = _gdn_chunk(_heads(at(q_ref)), _heads(at(k_ref)), _heads(at(v_ref)), ba_ref[rows, :], al_ref[...],
                           dt_ref[...], S)
            sp_ref[c] = S
            S = f["S1"]
            ti_ref[c] = f["Tinv"]
            o, vn = f["o"], f["vn"]
            r = lax.rsqrt(_rowsum(o * o) * (1.0 / HEAD_DIM) + EPS)
            y = o * r * nw_ref[...] * _silu(_heads(at(z_ref)))
            for h in range(H):
                sl = slice(h * HEAD_DIM, (h + 1) * HEAD_DIM)
                y_ref[rows, sl] = y[h].astype(bf16)
                vn_ref[rows, sl] = vn[h]
                o_ref[rows, sl] = o[h]
        st[...] = S

    wide_f32 = jax.ShapeDtypeStruct((T, H * HEAD_DIM), f32)
    return pl.pallas_call(
        body, grid=(NS,),
        in_specs=[wide(0), wide(1), wide(2), wide(3), one(4 * H), vec, vec, vec],
        out_specs=[wide(0), st_spec(HEAD_DIM), st_spec(C), wide(0), wide(0)],
        out_shape=[jax.ShapeDtypeStruct((T, H * HEAD_DIM), bf16), jax.ShapeDtypeStruct((N, H, HEAD_DIM, HEAD_DIM), f32),
                   jax.ShapeDtypeStruct((N, H, C, C), f32), wide_f32, wide_f32],
        scratch_shapes=[pltpu.VMEM((H, HEAD_DIM, HEAD_DIM), f32)],
        compiler_params=_cparams(("arbitrary",)), name=name,
    )(qkv, qkv, qkv, p, p, alog, dtb, nw)


def _gdn_bwd(qkv, p, alog, dtb, nw, sprev, tinv, vn_all, o_all, dy_all, name):
    T = qkv.shape[0]
    C, H = GDN_CHUNK, GDN_HEADS
    N = T // C
    K, NS, wide, one, vec, st_spec = _gdn_specs(T, True)
    rs = lambda m: jnp.sum(m, axis=2, keepdims=True)

    def put(ref, val, col=0):
        for h in range(H):
            ref[:, col + h * HEAD_DIM:col + (h + 1) * HEAD_DIM] = val[h].astype(ref.dtype)

    def body(q_ref, k_ref, v_ref, z_ref, ba_ref, al_ref, dt_ref, nw_ref, sp_ref, ti_ref, vn_ref, o_ref, dy_ref,
             dqkv_ref, dz_ref, dba_ref, dal_ref, ddt_ref, dnw_ref, dst):
        @pl.when(pl.program_id(0) == 0)
        def _():
            dst[...] = jnp.zeros_like(dst)
            dal_ref[...] = jnp.zeros_like(dal_ref)
            ddt_ref[...] = jnp.zeros_like(ddt_ref)
            dnw_ref[...] = jnp.zeros_like(dnw_ref)

        dS = dst[...]
        for c in reversed(range(K)):
            at = lambda ref, c=c: ref.at[pl.ds(c * C, C), :]
            dS = chunk(at(q_ref), at(k_ref), at(v_ref), at(z_ref), at(ba_ref), al_ref, dt_ref, nw_ref, sp_ref[c], ti_ref[c],
                       at(vn_ref), at(o_ref), at(dy_ref), at(dqkv_ref), at(dz_ref), at(dba_ref), dal_ref, ddt_ref, dnw_ref, dS)
        dst[...] = dS

    def chunk(q_ref, k_ref, v_ref, z_ref, ba_ref, al_ref, dt_ref, nw_ref, S, Tsaved, vn_ref, o_ref, dy_ref,
              dqkv_ref, dz_ref, dba_ref, dal_ref, ddt_ref, dnw_ref, dS1):
        ba, alog, dtb_, nwv = ba_ref[...], al_ref[...], dt_ref[...], nw_ref[...]
        v = _heads(v_ref)
        f = _gdn_chunk(_heads(q_ref), _heads(k_ref), v, ba, alog, dtb_, S, saved=(Tsaved, _heads(vn_ref), _heads(o_ref)))
        beta, kn, qn, kb, vb, Tinv, kbg = f["beta"], f["kn"], f["qn"], f["kb"], f["vb"], f["Tinv"], f["kbg"]
        eG, eGl, cd, Di, Ds, QK, vn, w_, qg, kg = (f["eG"], f["eGl"], f["cd"], f["Di"], f["Ds"], f["QK"], f["vn"],
                                                    f["w"], f["qg"], f["kg"])
        lane, ri, ci, rowc = f["lane"], f["ri"], f["ci"], f["rowc"]
        o = f["o"]
        z = _heads(z_ref)
        dy = _heads(dy_ref)
        r = lax.rsqrt(_rowsum(o * o) * (1.0 / HEAD_DIM) + EPS)
        nrm = o * r
        sz = _silu(z)
        dn = dy * nwv * sz
        put(dz_ref, dy * nrm * nwv * _dsilu(z))
        dnw_ref[...] += jnp.sum(jnp.sum(dy * nrm * sz, axis=0), axis=0, keepdims=True)
        do = r * (dn - nrm * (_rowsum(dn * nrm) * (1.0 / HEAD_DIM)))
        dcd = jnp.sum(jnp.sum(S * dS1, axis=2, keepdims=True), axis=1, keepdims=True)
        dkg = _dot(vn, dS1, _NT3)
        dvn = _dot(kg, dS1, _NN3) + _dot(QK, do, _TN3)
        dqg = _dot(do, S, _NT3)
        dQK = _dot(do, vn, _NT3)
        dw = -_dot(dvn, S, _NT3)
        dS0 = cd * dS1 + _dot(qg, do, _TN3) - _dot(w_, dvn, _TN3)
        dqn = dqg * eG
        dkn = dkg * eGl
        deGl = rs(dkg * kn)
        dQKr = dQK * Di
        E = dQK * QK
        dqn = dqn + _dot(dQKr, kn, _NN3)
        dkn = dkn + _dot(dQKr, qn, _TN3)
        dT = _dot(dvn, vb, _NT3) + _dot(dw, kbg, _NT3)
        dvb = _dot(Tinv, dvn, _TN3)
        dkbg = _dot(Tinv, dw, _TN3)
        dkb = dkbg * eG
        deG = rs(dqg * qn + dkbg * kb)
        dL = -_dot(_dot(Tinv, dT, _TN3), Tinv, _NT3)
        dKK = dL * Ds
        E = E + dL * f["Lm"]
        dkb = dkb + _dot(dKK, kn, _NN3)
        dkn = dkn + _dot(dKK, kb, _TN3) + dkb * beta
        dbeta = rs(dkb * kn + dvb * v)
        put(dqkv_ref, dvb * beta, 2 * H * HEAD_DIM)
        dG = rs(E) - rs(jnp.swapaxes(E, 1, 2)) + deG * eG - deGl * eGl
        dGl = jnp.sum(deGl * eGl, axis=1, keepdims=True) + dcd * cd
        dG = dG + jnp.where(rowc == C - 1, dGl, 0.0)
        qh = f["qh"]
        put(dqkv_ref, (HEAD_DIM ** -0.5) * f["rq"] * (dqn - qh * _rowsum(dqn * qh)))
        put(dqkv_ref, f["rk"] * (dkn - kn * _rowsum(dkn * kn)), H * HEAD_DIM)
        db = dbeta * beta * (1.0 - beta)
        db_all = jnp.where(lane == 0, db[0], 0.0)
        dG_all = jnp.where(lane == H, dG[0], 0.0)
        for h in range(1, H):
            db_all = db_all + jnp.where(lane == h, db[h], 0.0)
            dG_all = dG_all + jnp.where(lane == H + h, dG[h], 0.0)
        triu = (ri <= ci).astype(f32)
        dg_all = _dot01(triu, dG_all, _NN)
        da_all = dg_all * (-f["ea"]) * _sigmoid(ba + dtb_)
        dba_ref[...] = (db_all + da_all).astype(bf16)
        ddt_ref[...] += jnp.sum(da_all, axis=0, keepdims=True)
        dal_ref[...] += jnp.sum(dg_all * f["g_all"], axis=0, keepdims=True)
        return dS0

    small = jax.ShapeDtypeStruct((1, 128), f32)
    return pl.pallas_call(
        body, grid=(NS,),
        in_specs=[wide(0), wide(1), wide(2), wide(3), one(4 * H), vec, vec, vec, st_spec(HEAD_DIM), st_spec(C),
                  wide(0), wide(0), wide(0)],
        out_specs=[pl.BlockSpec((K * C, 3 * H * HEAD_DIM), lambda n: (NS - 1 - n, 0)), wide(3), one(0), vec, vec, vec],
        out_shape=[jax.ShapeDtypeStruct((T, 3 * H * HEAD_DIM), f32), jax.ShapeDtypeStruct((T, ODD_PAD), bf16),
                   jax.ShapeDtypeStruct((T, 128), bf16), small, small, small],
        scratch_shapes=[pltpu.VMEM((H, HEAD_DIM, HEAD_DIM), f32)],
        compiler_params=_cparams(("arbitrary",)), name=name,
    )(qkv, qkv, qkv, p, p, alog, dtb, nw, sprev, tinv, vn_all, o_all, dy_all)


def _adamw(w, gs, m, v, name, layer=None, prev=None):
    R, Cc = w.shape[-2:]
    S = gs.shape[0]
    tr = R
    for cand in (256, 128, 64, 32, 16, 8):
        if R % cand == 0 and R > cand:
            tr = cand
            break
    c1 = 1.0 - ADAM_B1 ** ADAM_STEP
    c2 = 1.0 - ADAM_B2 ** ADAM_STEP

    def body(w_ref, g_ref, m_ref, v_ref, *rest):
        go_ref, d_ref, mo_ref, vo_ref = rest[-4:]
        g = g_ref[0].astype(f32)
        for s in range(1, S):
            g = g + g_ref[s].astype(f32)
        mn = ADAM_B1 * m_ref[...] + (1.0 - ADAM_B1) * g
        vn = ADAM_B2 * v_ref[...] + (1.0 - ADAM_B2) * (g * g)
        go_ref[...] = g
        mo_ref[...] = mn
        vo_ref[...] = vn
        d_ref[...] = -ADAM_LR * ((mn / c1) / (jnp.sqrt(vn / c2) + ADAM_EPS) + ADAM_WD * w_ref[...])

    if layer is None:
        blk = pl.BlockSpec((tr, Cc), lambda i: (i, 0))
    else:
        blk = pl.BlockSpec((None, tr, Cc), lambda i: (layer, i, 0))
    out = jax.ShapeDtypeStruct(w.shape, f32)
    carried = [] if prev is None else list(prev)
    return pl.pallas_call(
        body, grid=(R // tr,),
        in_specs=[blk, pl.BlockSpec((S, tr, Cc), lambda i: (0, i, 0)), blk, blk]
        + [pl.BlockSpec(memory_space=pl.ANY)] * len(carried),
        out_specs=[blk] * 4, out_shape=[out] * 4,
        input_output_aliases={4 + j: j for j in range(len(carried))},
        compiler_params=_cparams(("parallel",)), name=name,
    )(w, gs, m, v, *carried)


def _me():
    x, y, c = lax.axis_index("x"), lax.axis_index("y"), lax.axis_index("c")
    return x, y, c, 4 * x + 2 * y + c


def _peer(k):
    x, y, c, _ = _me()
    px = 1 - x if k & 4 else x
    py = 1 - y if k & 2 else y
    pc = 1 - c if k & 1 else c
    return (px, py, pc), 4 * px + 2 * py + pc


_HBM = pl.BlockSpec(memory_space=pltpu.HBM)
_SEM = pl.BlockSpec(memory_space=pltpu.SEMAPHORE)
_EFFECT = pltpu.SideEffectType.DATAFLOW_SIDE_EFFECTING


def _copy(src, land, ssem, rsem, k, blocked, landing_slot_of_peer):
    pid, pidx = _peer(k)
    slot = pidx if landing_slot_of_peer else _me()[3]
    return pltpu.make_async_remote_copy(src_ref=src.at[pidx] if blocked else src, dst_ref=land.at[slot],
                                        send_sem=ssem.at[k - 1], recv_sem=rsem.at[k - 1], device_id=pid, device_id_type=MESH)


def _send_start(srcs, blocked, name):
    n = len(srcs)
    lands = [lax.empty(a.shape if blocked else (N_DEV,) + a.shape, a.dtype) for a in srcs]

    def body(*refs):
        src, land, sems, token = refs[:n], refs[n:2 * n], refs[2 * n:4 * n], refs[-1]
        for i in range(n):
            for k in range(1, N_DEV):
                _copy(src[i], land[i], sems[2 * i], sems[2 * i + 1], k, blocked, False).start()
        token[...] = jnp.zeros_like(token)

    sem = pltpu.SemaphoreType.DMA((N_DEV - 1,))
    hbm = lambda a: pltpu.with_memory_space_constraint(a, pltpu.HBM)
    res = pl.pallas_call(
        body, name=name,
        out_shape=tuple([sem] * (2 * n)) + tuple(pltpu.HBM(a.shape, a.dtype) for a in srcs + lands)
        + (jax.ShapeDtypeStruct((8, 128), f32),),
        in_specs=(_HBM,) * (2 * n),
        out_specs=(_SEM,) * (2 * n) + (_HBM,) * (2 * n) + (pl.BlockSpec(memory_space=pltpu.VMEM),),
        input_output_aliases={j: 2 * n + j for j in range(2 * n)},
        compiler_params=pltpu.CompilerParams(has_side_effects=_EFFECT),
    )(*[hbm(a) for a in srcs], *[hbm(a) for a in lands])
    handles = [(res[2 * i], res[2 * i + 1], res[2 * n + i], res[3 * n + i]) for i in range(n)]
    return handles, res[-1]


def _send_wait(handle, blocked, after, name):
    ssem, rsem, src, land = handle

    def body(src_ref, land_ref, ssem_ref, rsem_ref, after_ref, src_out, land_out):
        for k in range(1, N_DEV):
            cp = _copy(src_ref, land_ref, ssem_ref, rsem_ref, k, blocked, True)
            cp.wait_send()
            cp.wait_recv()

    return pl.pallas_call(
        body, name=name, out_shape=(pltpu.HBM(src.shape, src.dtype), pltpu.HBM(land.shape, land.dtype)),
        in_specs=(_HBM, _HBM, _SEM, _SEM, pl.BlockSpec(memory_space=pl.ANY)), out_specs=(_HBM, _HBM),
        input_output_aliases={0: 0, 1: 1}, compiler_params=pltpu.CompilerParams(has_side_effects=_EFFECT),
    )(src, land, ssem, rsem, after)[1]


def _block_diag(w):
    nb, bs = w.shape[0], w.shape[1]
    eye = jnp.eye(nb, dtype=w.dtype)
    return (eye[:, None, :, None] * w[:, :, None, :]).reshape(nb * bs, nb * bs)


def _diag_blocks(d):
    return jnp.stack([d[g, s * 64:(s + 1) * 64, s * 64:(s + 1) * 64] for g in range(4) for s in range(2)])


_SQUARE_TILES = dict(tm=1024, tn=1024, tk=1024)


def _mlp_fwd(x, nw, wu, wd, tag):
    hm = _rms_fwd(x, nw, f"rms_mlp_{tag}")
    (r,) = _matmul(hm, wu, "nn", outs=[bf16], epilogue=lambda acc: (jnp.maximum(acc, 0.0),), name=f"mlp_up_{tag}")
    (xo,) = _matmul(r, wd, "nn", outs=[f32], extras=(x,), epilogue=lambda acc, res: (res + acc,), a_map=jnp.square,
                    name=f"mlp_down_{tag}", **_SQUARE_TILES)
    return xo, (hm, r)


def _mlp_bwd(x, nw, wu, wd, saved, dxo, dxo_b, tag):
    hm, r = saved
    (du,) = _matmul(dxo_b, wd, "nt", outs=[bf16], extras=(r,), epilogue=lambda acc, rr: (acc * (2.0 * rr.astype(f32)),),
                    name=f"mlp_dact_{tag}")
    (dwd,) = _matmul(r, dxo_b, "tn", outs=[bf16], a_map=jnp.square, name=f"mlp_dwd_{tag}", **_SQUARE_TILES)
    (dwu,) = _matmul(hm, du, "tn", outs=[bf16], shard_cols=2, name=f"mlp_dwu_{tag}")
    (dhm,) = _matmul(du, wu, "nt", outs=[f32], name=f"mlp_dh_{tag}", **_SQUARE_TILES)
    dx, dx_b, dnw = _rms_bwd(x, nw, dhm, dxo, f"rms_mlp_bwd_{tag}")
    return dx, dx_b, dnw, dwu, dwd.reshape(N_DEV, D_FF // N_DEV, D_MODEL)


def _local_step(x, tgt, P, weight, sink):
    T = x.shape[0]
    cos, sin = _rope_tables(T)
    rtab = _ret_tables()
    row = lambda a: a.reshape(1, -1)
    mix_nw, mlp_nw = P["mixer_norm_w"], P["mlp_norm_w"]
    wr_bd, wi_bd = _block_diag(P["lru_w_r"]), _block_diag(P["lru_w_i"])
    lru_b, lru_br, lru_bi, lru_lam = row(P["lru_conv_b"]), row(P["lru_b_r"]), row(P["lru_b_i"]), row(P["lru_lambda"])
    pad16 = lambda a: jnp.pad(a.reshape(1, GDN_HEADS), ((0, 0), (GDN_HEADS, 128 - 2 * GDN_HEADS)))
    alog, dtb = pad16(P["gdn_a_log"]), pad16(P["gdn_dt_bias"])
    gnw = row(P["gdn_norm_w"])
    residual = lambda acc, res: (res + acc,)

    x0 = x
    h0 = _rms_fwd(x0, mix_nw[0:1], "rms_mix_0")
    w_ie = weight("w_in_even", h0)
    (pe,) = _matmul(h0, w_ie, "nn", outs=[f32], name="in_even")
    mix0, o_ret, s_ret = _ret_fwd(pe, cos, sin, rtab, "ret_fwd")
    w_lc = weight("lru_conv_w", pe)
    xc = _conv_fwd(pe, 4, w_lc, lru_b, False, "lru_conv_fwd")
    mix0, h_lru = _lru_fwd(xc, pe, 20, wr_bd, wi_bd, lru_br, lru_bi, lru_lam, mix0, "lru_fwd")
    w_oe = weight("w_out_even", mix0)
    (x1,) = _matmul(mix0, w_oe, "nn", outs=[f32], extras=(x0,), epilogue=residual, name="out_even")
    w_u0, w_d0 = weight("w_up0", x1), weight("w_down0", x1)
    x2, mlp0 = _mlp_fwd(x1, mlp_nw[0:1], w_u0, w_d0, "0")
    h1 = _rms_fwd(x2, mix_nw[1:2], "rms_mix_1")
    w_io = weight("w_in_odd", h1)
    (po,) = _matmul(h1, w_io, "nn", outs=[f32], tm=1024, tn=ODD_PAD // 3, name="in_odd")
    w_gc = weight("gdn_conv_w", po)
    qkv = _conv_fwd(po, 0, w_gc, None, True, "gdn_conv_fwd")
    y_gdn, s_gdn, ti_gdn, vn_gdn, o_gdn = _gdn_fwd(qkv, po, alog, dtb, gnw, "gdn_fwd")
    w_oo = weight("w_out_odd", y_gdn)
    (x3,) = _matmul(y_gdn, w_oo, "nn", outs=[f32], extras=(x2,), epilogue=residual, name="out_odd")
    w_u1, w_d1 = weight("w_up1", x3), weight("w_down1", x3)
    x4, mlp1 = _mlp_fwd(x3, mlp_nw[1:2], w_u1, w_d1, "1")
    loss, dx4, dx4_b, d_final = _loss_head(x4, row(P["final_norm_w"]), tgt, "loss_head")
    dx3, dx3_b, d_mlp_nw1, d_wu1, d_wd1 = _mlp_bwd(x3, mlp_nw[1:2], w_u1, w_d1, mlp1, dx4, dx4_b, "1")
    tok = sink(dict(w_up1=d_wu1, w_down1=d_wd1))
    (dy_gdn,) = _matmul(dx3_b, w_oo, "nt", outs=[f32], name="out_odd_dx")
    (d_woo,) = _matmul(y_gdn, dx3_b, "tn", outs=[bf16], name="out_odd_dw")
    dqkv, dpo, dba, d_alog, d_dtb, d_gnw = _gdn_bwd(qkv, po, alog, dtb, gnw + tok[0:1, :], s_gdn, ti_gdn, vn_gdn, o_gdn, dy_gdn,
                                                  "gdn_bwd")
    dpo, d_gconv, _ = _conv_bwd(po, 0, w_gc, None, True, dqkv, dpo, "gdn_conv_bwd")
    dpo = lax.dynamic_update_slice(dpo, dba, (0, 4 * D_MODEL))
    (d_wio,) = _matmul(h1, dpo, "tn", outs=[bf16], tn=ODD_PAD // 3, name="in_odd_dw")
    n_odd = ODD_IN // N_DEV
    tok = sink(dict(w_out_odd=d_woo.reshape(N_DEV, D_MODEL // N_DEV, D_MODEL),
                    w_in_odd=jnp.transpose(d_wio[:, :ODD_IN].reshape(D_MODEL, N_DEV, n_odd), (1, 0, 2))))
    (dh1,) = _matmul(dpo, w_io, "nt", outs=[f32], tm=1024, tn=1024, tk=ODD_PAD // 3, name="in_odd_dx")
    dx2, dx2_b, d_mix_nw1 = _rms_bwd(x2, mix_nw[1:2] + tok[0:1, 0:1], dh1, dx3, "rms_mix_bwd_1")
    dx1, dx1_b, d_mlp_nw0, d_wu0, d_wd0 = _mlp_bwd(x1, mlp_nw[0:1], w_u0, w_d0, mlp0, dx2, dx2_b, "0")
    (d_woe,) = _matmul(mix0, dx1_b, "tn", outs=[bf16], name="out_even_dw")
    tok = sink(dict(w_up0=d_wu0, w_down0=d_wd0, w_out_even=d_woe.reshape(N_DEV, D_MODEL // N_DEV, D_MODEL)))
    (dmix0,) = _matmul(dx1_b, w_oe, "nt", outs=[f32], name="out_even_dx")
    dpe = _ret_bwd(pe, cos, sin, rtab, o_ret, s_ret, dmix0, "ret_bwd")
    dxc, dpe, d_wr, d_wi, d_br, d_bi, d_lam = _lru_bwd(xc, pe, 20, wr_bd, wi_bd, lru_br, lru_bi, lru_lam + tok[0:1, 0:1],
                                                       h_lru, dmix0, 4, dpe, "lru_bwd")
    dpe, d_lconv, d_lconv_b = _conv_bwd(pe, 4, w_lc, lru_b, False, dxc, dpe, "lru_conv_bwd")
    (d_wie,) = _matmul(h0, dpe, "tn", outs=[bf16], shard_cols=2, name="in_even_dw")
    (dh0,) = _matmul(dpe, w_ie, "nt", outs=[f32], name="in_even_dx", **_SQUARE_TILES)
    dx0, _, d_mix_nw0 = _rms_bwd(x0, mix_nw[0:1], dh0, dx1, "rms_mix_bwd_0")

    G = dict(
        mixer_norm_w=jnp.concatenate([d_mix_nw0, d_mix_nw1], axis=0),
        mlp_norm_w=jnp.concatenate([d_mlp_nw0, d_mlp_nw1], axis=0),
        final_norm_w=d_final.reshape(-1),
        w_in_even=d_wie, lru_conv_w=d_lconv, lru_conv_b=d_lconv_b.reshape(-1),
        lru_w_r=_diag_blocks(d_wr), lru_b_r=d_br.reshape(-1), lru_w_i=_diag_blocks(d_wi), lru_b_i=d_bi.reshape(-1),
        lru_lambda=d_lam.reshape(-1), gdn_conv_w=d_gconv,
        gdn_a_log=d_alog[0, GDN_HEADS:2 * GDN_HEADS], gdn_dt_bias=d_dtb[0, GDN_HEADS:2 * GDN_HEADS],
        gdn_norm_w=d_gnw.reshape(-1),
    )
    return loss, dx0, G


_SMALL = ["mixer_norm_w", "mlp_norm_w", "final_norm_w", "lru_conv_b", "lru_w_r", "lru_b_r", "lru_w_i", "lru_b_i",
          "lru_lambda", "gdn_a_log", "gdn_dt_bias", "gdn_norm_w"]
_PACK_ROWS = 688


def _pack(parts):
    flat = jnp.concatenate([p.reshape(-1) for p in parts])
    return jnp.pad(flat, (0, _PACK_ROWS * 128 - flat.shape[0])).reshape(_PACK_ROWS, 128)


def _unpack(packed, shapes):
    flat = packed.reshape(-1)
    out, off = [], 0
    for s in shapes:
        n = int(np.prod(s))
        out.append(flat[off:off + n].reshape(s))
        off += n
    return out


def kernel(x, mixer_norm_w, mlp_norm_w, final_norm_w, w_in_even, lru_conv_w, lru_conv_b, lru_w_r, lru_b_r, lru_w_i, lru_b_i, lru_lambda, w_out_even, w_in_odd, gdn_conv_w, gdn_a_log, gdn_dt_bias, gdn_norm_w, w_out_odd, w_up, w_down, loss_target, m_mixer_norm_w, m_mlp_norm_w, m_final_norm_w, m_w_in_even, m_lru_conv_w, m_lru_conv_b, m_lru_w_r, m_lru_b_r, m_lru_w_i, m_lru_b_i, m_lru_lambda, m_w_out_even, m_w_in_odd, m_gdn_conv_w, m_gdn_a_log, m_gdn_dt_bias, m_gdn_norm_w, m_w_out_odd, m_w_up, m_w_down, v_mixer_norm_w, v_mlp_norm_w, v_final_norm_w, v_w_in_even, v_lru_conv_w, v_lru_conv_b, v_lru_w_r, v_lru_b_r, v_lru_w_i, v_lru_b_i, v_lru_lambda, v_w_out_even, v_w_in_odd, v_gdn_conv_w, v_gdn_a_log, v_gdn_dt_bias, v_gdn_norm_w, v_w_out_odd, v_w_up, v_w_down):
    Pw = dict(mixer_norm_w=mixer_norm_w, mlp_norm_w=mlp_norm_w, final_norm_w=final_norm_w, w_in_even=w_in_even,
              lru_conv_w=lru_conv_w, lru_conv_b=lru_conv_b, lru_w_r=lru_w_r, lru_b_r=lru_b_r, lru_w_i=lru_w_i,
              lru_b_i=lru_b_i, lru_lambda=lru_lambda, w_out_even=w_out_even, w_in_odd=w_in_odd, gdn_conv_w=gdn_conv_w,
              gdn_a_log=gdn_a_log, gdn_dt_bias=gdn_dt_bias, gdn_norm_w=gdn_norm_w, w_out_odd=w_out_odd, w_up=w_up,
              w_down=w_down)
    Pm = dict(mixer_norm_w=m_mixer_norm_w, mlp_norm_w=m_mlp_norm_w, final_norm_w=m_final_norm_w, w_in_even=m_w_in_even,
              lru_conv_w=m_lru_conv_w, lru_conv_b=m_lru_conv_b, lru_w_r=m_lru_w_r, lru_b_r=m_lru_b_r, lru_w_i=m_lru_w_i,
              lru_b_i=m_lru_b_i, lru_lambda=m_lru_lambda, w_out_even=m_w_out_even, w_in_odd=m_w_in_odd,
              gdn_conv_w=m_gdn_conv_w, gdn_a_log=m_gdn_a_log, gdn_dt_bias=m_gdn_dt_bias, gdn_norm_w=m_gdn_norm_w,
              w_out_odd=m_w_out_odd, w_up=m_w_up, w_down=m_w_down)
    Pv = dict(mixer_norm_w=v_mixer_norm_w, mlp_norm_w=v_mlp_norm_w, final_norm_w=v_final_norm_w, w_in_even=v_w_in_even,
              lru_conv_w=v_lru_conv_w, lru_conv_b=v_lru_conv_b, lru_w_r=v_lru_w_r, lru_b_r=v_lru_b_r, lru_w_i=v_lru_w_i,
              lru_b_i=v_lru_b_i, lru_lambda=v_lru_lambda, w_out_even=v_w_out_even, w_in_odd=v_w_in_odd,
              gdn_conv_w=v_gdn_conv_w, gdn_a_log=v_gdn_a_log, gdn_dt_bias=v_gdn_dt_bias, gdn_norm_w=v_gdn_norm_w,
              w_out_odd=v_w_out_odd, w_up=v_w_up, w_down=v_w_down)
    me = _me()[3]
    T = x.shape[1]

    cols = lambda g: jnp.transpose(g, (1, 0, 2)).reshape(g.shape[1], -1)
    rows = lambda g: g.reshape(-1, g.shape[2])
    wide = lambda g: jnp.pad(cols(g), ((0, 0), (0, ODD_PAD - ODD_IN)))
    gather = dict(
        w_in_even=(w_in_even[0].astype(bf16), cols), lru_conv_w=(lru_conv_w[0], cols),
        w_out_even=(w_out_even[0].astype(bf16), rows), w_up0=(w_up[0].astype(bf16), cols), w_down0=(w_down[0].astype(bf16), rows),
        w_in_odd=(w_in_odd[0].astype(bf16), wide), gdn_conv_w=(gdn_conv_w[0], cols),
        w_out_odd=(w_out_odd[0].astype(bf16), rows), w_up1=(w_up[1].astype(bf16), cols), w_down1=(w_down[1].astype(bf16), rows))
    handles, tok = _send_start([s for s, _ in gather.values()], False, "gather_start")
    handles = dict(zip(gather, handles))
    full = {}

    def weight(name, after):
        if name not in full:
            landed = _send_wait(handles[name], False, after, f"gather_wait_{name}")
            shard, finish = gather[name]
            full[name] = finish(lax.dynamic_update_slice_in_dim(landed, shard[None], me, 0))
        return full[name]

    P = {k: Pw[k] for k in ("mlp_norm_w", "final_norm_w")}
    P["mixer_norm_w"] = mixer_norm_w + tok[0:1, 0:1]
    for k in ("lru_w_r", "lru_w_i", "lru_conv_b", "lru_b_r", "lru_b_i", "lru_lambda", "gdn_a_log", "gdn_dt_bias", "gdn_norm_w"):
        P[k] = Pw[k][0]

    sent = {}

    def sink(grads):
        hs, token = _send_start(list(grads.values()), True, "grads_start_" + "_".join(grads))
        for (name, g), h in zip(grads.items(), hs):
            sent[name] = (h, g)
        return token

    loss, dx, G = _local_step(x[0], loss_target[0], P, weight, sink)
    small_g = [G[k].reshape(Pw[k].shape) for k in _SMALL] + [G["lru_conv_w"], G["gdn_conv_w"]]
    packed = _pack(small_g)
    sink(dict(w_in_even=G["w_in_even"], small=jnp.broadcast_to(packed[None], (N_DEV,) + packed.shape)))

    def received(name, after=dx):
        h, g = sent[name]
        landed = _send_wait(h, True, after, f"grads_wait_{name}")
        return lax.dynamic_update_slice_in_dim(landed, lax.dynamic_slice_in_dim(g, me, 1, 0), me, 0)

    out = {}
    nff = D_FF // N_DEV

    def whole(name, gs, shape2d):
        res = _adamw(Pw[name].reshape(shape2d), gs, Pm[name].reshape(shape2d), Pv[name].reshape(shape2d), f"adamw_{name}")
        out[name] = tuple(a.reshape(Pw[name].shape) for a in res)

    def layers(name):
        res = None
        for l in range(2):
            res = _adamw(Pw[name], received(f"{name}{l}"), Pm[name], Pv[name], f"adamw_{name}{l}", layer=l, prev=res)
        out[name] = tuple(res)

    layers("w_up")
    layers("w_down")
    whole("w_out_odd", received("w_out_odd"), (128, D_MODEL))
    whole("w_in_odd", received("w_in_odd"), (D_MODEL, 514))
    whole("w_out_even", received("w_out_even"), (128, D_MODEL))
    whole("w_in_even", received("w_in_even", out["w_out_even"][1]), (D_MODEL, 384))
    small_shapes = [Pw[k].shape for k in _SMALL]
    pw, pm, pv = (_pack([Q[k] for k in _SMALL]) for Q in (Pw, Pm, Pv))
    sg, sd, sm, sv = _adamw(pw, received("small", out["w_in_even"][1]), pm, pv, "adamw_small")
    for arrs_i, packed_out in enumerate((sg, sd, sm, sv)):
        for k, a in zip(_SMALL, _unpack(packed_out, small_shapes)):
            out.setdefault(k, [None] * 4)[arrs_i] = a
    n_small = sum(int(np.prod(s)) for s in small_shapes)
    gflat = sg.reshape(-1)
    g_lconv = gflat[n_small:n_small + CONV_K * LRU_WIDTH].reshape(CONV_K, LRU_WIDTH)
    g_gconv = gflat[n_small + CONV_K * LRU_WIDTH:n_small + CONV_K * (LRU_WIDTH + 3072)].reshape(CONV_K, 3072)
    whole("lru_conv_w", lax.dynamic_slice_in_dim(g_lconv, me * 64, 64, axis=1)[None], (CONV_K, 64))
    whole("gdn_conv_w", lax.dynamic_slice_in_dim(g_gconv, me * 384, 384, axis=1)[None], (CONV_K, 384))

    names = ["mixer_norm_w", "mlp_norm_w", "final_norm_w", "w_in_even", "lru_conv_w", "lru_conv_b", "lru_w_r", "lru_b_r",
             "lru_w_i", "lru_b_i", "lru_lambda", "w_out_even", "w_in_odd", "gdn_conv_w", "gdn_a_log", "gdn_dt_bias",
             "gdn_norm_w", "w_out_odd", "w_up", "w_down"]
    total = lax.psum(loss[0, 0], ("x", "y", "c"))
    res = [total, dx[None]]
    for j in range(4):
        res += [out[k][j] for k in names]
    return tuple(res)
```

```python
import math

import numpy as np
import jax
import jax.numpy as jnp
from jax import lax
from jax.experimental import pallas as pl
from jax.experimental.pallas import tpu as pltpu

f32 = jnp.float32
bf16 = jnp.bfloat16

N_DEV = 8
D_MODEL = 1024
D_FF = 4096
EPS = 1e-6
RET_HEADS = 4
RET_CHUNK = 128
ROPE_THETA = 10000.0
LRU_WIDTH = 512
LRU_C = 8.0
GDN_HEADS = 8
GDN_CHUNK = 64
GDN_STEP = 2
HEAD_DIM = 128
ODD_IN = 4112
ODD_PAD = 4224
ADAM_LR, ADAM_B1, ADAM_B2, ADAM_EPS, ADAM_WD, ADAM_STEP = 0.001, 0.9, 0.999, 1e-08, 0.01, 10
VMEM_LIMIT = 56 * 1024 * 1024

_NN = (((1,), (0,)), ((), ()))
_NT = (((1,), (1,)), ((), ()))
_TN = (((0,), (0,)), ((), ()))
MESH = pl.DeviceIdType.MESH


def _cparams(sem):
    return pltpu.CompilerParams(dimension_semantics=sem, vmem_limit_bytes=VMEM_LIMIT)


def _dot(a, b, dn):
    return lax.dot_general(a.astype(bf16), b.astype(bf16), dn, preferred_element_type=f32)


def _dot01(a01, b, dn):
    a = a01.astype(bf16)
    b0 = b.astype(bf16)
    r1 = b - b0.astype(f32)
    b1 = r1.astype(bf16)
    b2 = (r1 - b1.astype(f32)).astype(bf16)
    d = lambda q: lax.dot_general(a, q, dn, preferred_element_type=f32)
    return d(b0) + (d(b1) + d(b2))


def _sigmoid(x):
    return jax.nn.sigmoid(x)


def _silu(x):
    return x * _sigmoid(x)


def _dsilu(x):
    s = _sigmoid(x)
    return s * (1.0 + x * (1.0 - s))


def _softplus(x):
    return jnp.maximum(x, 0.0) + jnp.log1p(jnp.exp(-jnp.abs(x)))


_GELU_C = math.sqrt(2.0 / math.pi)


def _gelu(y):
    return 0.5 * y * (1.0 + jnp.tanh(_GELU_C * (y + 0.044715 * y * y * y)))


def _dgelu(y):
    t = jnp.tanh(_GELU_C * (y + 0.044715 * y * y * y))
    return 0.5 * (1.0 + t) + 0.5 * y * (1.0 - t * t) * _GELU_C * (1.0 + 3.0 * 0.044715 * y * y)


def _matmul(a, b, form, *, outs, name, epilogue=None, extras=(), tm=2048, tn=512, tk=1024, shard_cols=0, a_map=None):
    if form == "tn":
        K, M = a.shape
    else:
        M, K = a.shape
    N = b.shape[0] if form == "nt" else b.shape[1]
    ns = N // N_DEV
    if shard_cols:
        tn = ns * shard_cols
    tm, tn, tk = min(tm, M), min(tn, N), min(tk, K)
    assert M % tm == 0 and N % tn == 0 and K % tk == 0, (name, M, N, K, tm, tn, tk)
    nk = K // tk
    dn = {"nn": _NN, "nt": _NT, "tn": _TN}[form]
    if form == "tn":
        a_spec = pl.BlockSpec((tk, tm), lambda i, j, k: (k, i))
    else:
        a_spec = pl.BlockSpec((tm, tk), lambda i, j, k: (i, k))
    if form == "nt":
        b_spec = pl.BlockSpec((tn, tk), lambda i, j, k: (j, k))
    else:
        b_spec = pl.BlockSpec((tk, tn), lambda i, j, k: (k, j))
    e_spec = pl.BlockSpec((tm, tn), lambda i, j, k: (i, j))
    v_spec = pl.BlockSpec((1, tn), lambda i, j, k: (0, j))
    if shard_cols:
        o_spec = pl.BlockSpec((shard_cols, tm, ns), lambda i, j, k: (j, i, 0))
        o_shape = (N_DEV, M, ns)
    else:
        o_spec = e_spec
        o_shape = (M, N)
    n_ex = len(extras)
    sums = [isinstance(o, tuple) for o in outs]
    assert not any(sums) or tn == N, name

    def finish(acc, ex, o_refs, row_tile):
        vals = (acc,) if epilogue is None else epilogue(acc, *[e[...] for e in ex])
        for r, v, is_sum in zip(o_refs, vals, sums):
            if is_sum:
                @pl.when(row_tile == 0)
                def _(r=r, v=v):
                    r[...] = v.astype(r.dtype)

                @pl.when(row_tile > 0)
                def _(r=r, v=v):
                    r[...] += v.astype(r.dtype)
            elif shard_cols:
                for s in range(shard_cols):
                    r[s] = v[:, s * ns:(s + 1) * ns].astype(r.dtype)
            else:
                r[...] = v.astype(r.dtype)

    def prod(a_ref, b_ref):
        av = a_ref[...]
        return _dot(av if a_map is None else a_map(av), b_ref[...], dn)

    def body_one(*refs):
        finish(prod(*refs[:2]), refs[2:2 + n_ex], refs[2 + n_ex:], pl.program_id(0))

    def body_acc(*refs):
        a_ref, b_ref = refs[:2]
        acc = refs[-1]
        k = pl.program_id(2)
        row_tile = pl.program_id(0)

        @pl.when(k == 0)
        def _():
            acc[...] = prod(a_ref, b_ref)

        @pl.when((k > 0) & (k < nk - 1))
        def _():
            acc[...] += prod(a_ref, b_ref)

        @pl.when(k == nk - 1)
        def _():
            finish(acc[...] + prod(a_ref, b_ref), refs[2:2 + n_ex], refs[2 + n_ex:-1], row_tile)

    return pl.pallas_call(
        body_one if nk == 1 else body_acc, grid=(M // tm, N // tn, nk),
        in_specs=[a_spec, b_spec] + [v_spec if e.shape[0] == 1 else e_spec for e in extras],
        out_specs=[v_spec if s else o_spec for s in sums],
        out_shape=[jax.ShapeDtypeStruct((1, N), o[1]) if s else jax.ShapeDtypeStruct(o_shape, o) for o, s in zip(outs, sums)],
        scratch_shapes=[] if nk == 1 else [pltpu.VMEM((tm, tn), f32)],
        compiler_params=_cparams(("arbitrary" if any(sums) else "parallel", "parallel", "arbitrary")), name=name,
    )(a, b, *extras)


def _rms_fwd(x, w, name):
    T, D = x.shape
    tt = min(512, T)

    def body(x_ref, w_ref, h_ref):
        xv = x_ref[...]
        r = lax.rsqrt(jnp.mean(xv * xv, axis=1, keepdims=True) + EPS)
        h_ref[...] = (xv * r * w_ref[...]).astype(bf16)

    return pl.pallas_call(
        body, grid=(T // tt,),
        in_specs=[pl.BlockSpec((tt, D), lambda i: (i, 0)), pl.BlockSpec((1, D), lambda i: (0, 0))],
        out_specs=pl.BlockSpec((tt, D), lambda i: (i, 0)),
        out_shape=jax.ShapeDtypeStruct((T, D), bf16),
        compiler_params=_cparams(("parallel",)), name=name,
    )(x, w)


def _residual_rms_ep(acc, res, w):
    x = res + acc
    r = lax.rsqrt(jnp.mean(x * x, axis=1, keepdims=True) + EPS)
    return x, x * r * w


_RMS_BWD_OUTS = [f32, bf16, ("sum", f32)]


def _rms_bwd_ep(dh, x, dres, w):
    r = lax.rsqrt(jnp.mean(x * x, axis=1, keepdims=True) + EPS)
    xn = x * r
    dhw = dh * w
    dx = dres + r * (dhw - xn * jnp.mean(dhw * xn, axis=1, keepdims=True))
    return dx, dx, jnp.sum(dh * xn, axis=0, keepdims=True)


_LOSS_OUTS = [("sum", f32), f32, bf16, ("sum", f32)]


def _loss_ep(acc, res, w, tgt):
    x = res + acc
    D = x.shape[1]
    r = lax.rsqrt(jnp.mean(x * x, axis=1, keepdims=True) + EPS)
    xn = x * r
    e = xn * w - tgt
    loss = 0.5 * jnp.sum(jnp.mean(e * e, axis=1, keepdims=True), axis=0, keepdims=True)
    dy = e * (1.0 / D)
    dyw = dy * w
    dx = r * (dyw - xn * jnp.mean(dyw * xn, axis=1, keepdims=True))
    return jnp.broadcast_to(loss, (1, D)), dx, dx, jnp.sum(dy * xn, axis=0, keepdims=True)


def _ret_tables():
    H, C = RET_HEADS, RET_CHUNK
    lg = np.log1p(-np.exp2(-5.0 - np.arange(H, dtype=np.float32))).astype(np.float32)
    idx = np.arange(C, dtype=np.float32)
    diff = idx[:, None] - idx[None, :]
    causal = diff >= 0
    dm = np.where(causal[None], np.exp(lg[:, None, None] * np.where(causal, diff, 0.0)[None]), 0.0)
    qd = np.exp(lg[:, None] * (idx[None, :] + 1.0))
    kd = np.exp(lg[:, None] * (C - 1.0 - idx[None, :]))
    cg = np.exp(lg * C)
    tab = np.zeros((H, 4, C, HEAD_DIM), np.float32)
    tab[:, 0] = dm
    tab[:, 1] = qd[:, :, None]
    tab[:, 2] = kd[:, :, None]
    tab[:, 3] = cg[:, None, None]
    return jnp.asarray(tab)


def _rope_tables(T):
    half = HEAD_DIM // 2
    inv = ROPE_THETA ** (-jnp.arange(half, dtype=f32) / half)
    ang = jnp.arange(T, dtype=jnp.int32).astype(f32)[:, None] * inv[None, :]
    c, s = jnp.cos(ang), jnp.sin(ang)
    return jnp.concatenate([c, c], axis=1), jnp.concatenate([-s, s], axis=1)


def _rope(x, cos, sin):
    return x * cos + pltpu.roll(x, HEAD_DIM // 2, 1) * sin


def _unrope(y, cos, sin):
    return y * cos + pltpu.roll(y * sin, HEAD_DIM // 2, 1)


def _stack_heads(ref, H, f=None):
    parts = [ref[:, h * HEAD_DIM:(h + 1) * HEAD_DIM] for h in range(H)]
    return jnp.stack(parts if f is None else [f(a) for a in parts])


def _ret_fwd(p, cos, sin, tab, name):
    T = p.shape[0]
    C, H = RET_CHUNK, RET_HEADS
    N = T // C
    scale = HEAD_DIM ** -0.5

    def body(q_ref, k_ref, v_ref, g_ref, c_ref, s_ref, t_ref, y_ref, o_ref, sp_ref, st):
        @pl.when(pl.program_id(0) == 0)
        def _():
            st[...] = jnp.zeros_like(st)

        cos_, sin_ = c_ref[...], s_ref[...]
        rot = lambda a: _rope(a, cos_, sin_)
        q = _stack_heads(q_ref, H, rot)
        k = _stack_heads(k_ref, H, rot) * scale
        v = _stack_heads(v_ref, H)
        dm, qd, kd, cg = t_ref[:, 0], t_ref[:, 1], t_ref[:, 2], t_ref[:, 3]
        S = st[...]
        P = _dot(q, k, _NT3) * dm
        o = _dot(P, v, _NN3) + _dot(q * qd, S, _NN3)
        sp_ref[0] = S
        st[...] = cg * S + _dot(k * kd, v, _TN3)
        r = lax.rsqrt(jnp.mean(o * o, axis=2, keepdims=True) + EPS)
        y = o * r * _silu(_stack_heads(g_ref, H))
        for h in range(H):
            o_ref[:, h * HEAD_DIM:(h + 1) * HEAD_DIM] = o[h]
            y_ref[:, h * HEAD_DIM:(h + 1) * HEAD_DIM] = y[h].astype(bf16)

    wide = lambda blk: pl.BlockSpec((C, H * HEAD_DIM), lambda n: (n, blk))
    tbl = pl.BlockSpec((C, HEAD_DIM), lambda n: (n, 0))
    return pl.pallas_call(
        body, grid=(N,),
        in_specs=[wide(0), wide(1), wide(2), wide(3), tbl, tbl,
                  pl.BlockSpec((H, 4, C, HEAD_DIM), lambda n: (0, 0, 0, 0))],
        out_specs=[wide(0), wide(0), pl.BlockSpec((1, H, HEAD_DIM, HEAD_DIM), lambda n: (n, 0, 0, 0))],
        out_shape=[jax.ShapeDtypeStruct((T, D_MODEL), bf16), jax.ShapeDtypeStruct((T, H * HEAD_DIM), f32),
                   jax.ShapeDtypeStruct((N, H, HEAD_DIM, HEAD_DIM), f32)],
        scratch_shapes=[pltpu.VMEM((H, HEAD_DIM, HEAD_DIM), f32)],
        compiler_params=_cparams(("arbitrary",)), name=name,
    )(p, p, p, p, cos, sin, tab)


def _ret_bwd(p, cos, sin, tab, o_raw, sprev, dmix, name):
    T = p.shape[0]
    C, H = RET_CHUNK, RET_HEADS
    N = T // C
    scale = HEAD_DIM ** -0.5

    W = H * HEAD_DIM

    def body(q_ref, k_ref, v_ref, g_ref, c_ref, s_ref, t_ref, o_ref, sp_ref, dy_ref, d_ref, dst):
        @pl.when(pl.program_id(0) == 0)
        def _():
            dst[...] = jnp.zeros_like(dst)

        cos_, sin_ = c_ref[...], s_ref[...]
        rot = lambda a: _rope(a, cos_, sin_)
        q = _stack_heads(q_ref, H, rot)
        k = _stack_heads(k_ref, H, rot) * scale
        v = _stack_heads(v_ref, H)
        g = _stack_heads(g_ref, H)
        dm, qd, kd, cg = t_ref[:, 0], t_ref[:, 1], t_ref[:, 2], t_ref[:, 3]
        S = sp_ref[0]
        o = _stack_heads(o_ref, H)
        dy = _stack_heads(dy_ref, H)
        r = lax.rsqrt(jnp.mean(o * o, axis=2, keepdims=True) + EPS)
        nrm = o * r
        dn = dy * _silu(g)
        dg = dy * nrm * _dsilu(g)
        do = r * (dn - nrm * jnp.mean(dn * nrm, axis=2, keepdims=True))
        dS1 = dst[...]
        P = _dot(q, k, _NT3) * dm
        dP = _dot(do, v, _NT3) * dm
        dq = _dot(dP, k, _NN3) + _dot(do, S, _NT3) * qd
        dk = (_dot(dP, q, _TN3) + _dot(v, dS1, _NT3) * kd) * scale
        dv = _dot(P, do, _TN3) + _dot(k * kd, dS1, _NN3)
        dst[...] = cg * dS1 + _dot(q * qd, do, _TN3)
        for h in range(H):
            d_ref[:, h * HEAD_DIM:(h + 1) * HEAD_DIM] = _unrope(dq[h], cos_, sin_).astype(bf16)
            d_ref[:, W + h * HEAD_DIM:W + (h + 1) * HEAD_DIM] = _unrope(dk[h], cos_, sin_).astype(bf16)
            d_ref[:, 2 * W + h * HEAD_DIM:2 * W + (h + 1) * HEAD_DIM] = dv[h].astype(bf16)
            d_ref[:, 3 * W + h * HEAD_DIM:3 * W + (h + 1) * HEAD_DIM] = dg[h].astype(bf16)

    rev = lambda blk: pl.BlockSpec((C, W), lambda n: (N - 1 - n, blk))
    tbl = pl.BlockSpec((C, HEAD_DIM), lambda n: (N - 1 - n, 0))
    return pl.pallas_call(
        body, grid=(N,),
        in_specs=[rev(0), rev(1), rev(2), rev(3), tbl, tbl,
                  pl.BlockSpec((H, 4, C, HEAD_DIM), lambda n: (0, 0, 0, 0)), rev(0),
                  pl.BlockSpec((1, H, HEAD_DIM, HEAD_DIM), lambda n: (N - 1 - n, 0, 0, 0)), rev(0)],
        out_specs=pl.BlockSpec((C, 4 * W), lambda n: (N - 1 - n, 0)),
        out_shape=jax.ShapeDtypeStruct((T, 6 * W), bf16),
        scratch_shapes=[pltpu.VMEM((H, HEAD_DIM, HEAD_DIM), f32)],
        compiler_params=_cparams(("arbitrary",)), name=name,
    )(p, p, p, p, cos, sin, tab, o_raw, sprev, dmix)


CONV_K = 4
CONV_W = 512
PAD = 8
SUB_R = 64


def _conv_fwd(x, col_off, w, b, act, name):
    T = x.shape[0]
    C = w.shape[1]
    G = C // CONV_W
    tt = min(512, T)
    NT = T // tt
    has_b = b is not None

    def body(*refs):
        if has_b:
            x_ref, w_ref, b_ref, y_ref, pad = refs
        else:
            x_ref, w_ref, y_ref, pad = refs
        t = pl.program_id(1)

        @pl.when(t == 0)
        def _():
            pad[pl.ds(0, PAD), :] = jnp.zeros((PAD, CONV_W), f32)

        pad[pl.ds(PAD, tt), :] = x_ref[...]
        for g in range(CONV_W // 128):
            ls = slice(g * 128, (g + 1) * 128)
            wv = w_ref[:, ls]
            for c in range(tt // SUB_R):
                r0 = c * SUB_R
                y = wv[0:1, :] * pad[pl.ds(PAD - 3 + r0, SUB_R), ls]
                for kk in range(1, CONV_K):
                    y = y + wv[kk:kk + 1, :] * pad[pl.ds(PAD - 3 + kk + r0, SUB_R), ls]
                if has_b:
                    y = y + b_ref[:, ls]
                y_ref[pl.ds(r0, SUB_R), ls] = _silu(y) if act else y
        tail = pad[pl.ds(tt, PAD), :]
        pad[pl.ds(0, PAD), :] = tail

    in_specs = [pl.BlockSpec((tt, CONV_W), lambda g, t: (t, col_off + g)),
                pl.BlockSpec((CONV_K, CONV_W), lambda g, t: (0, g))]
    args = [x, w]
    if has_b:
        in_specs.append(pl.BlockSpec((1, CONV_W), lambda g, t: (0, g)))
        args.append(b)
    return pl.pallas_call(
        body, grid=(G, NT), in_specs=in_specs,
        out_specs=pl.BlockSpec((tt, CONV_W), lambda g, t: (t, g)),
        out_shape=jax.ShapeDtypeStruct((T, C), f32),
        scratch_shapes=[pltpu.VMEM((tt + PAD, CONV_W), f32)],
        compiler_params=_cparams(("parallel", "arbitrary")), name=name,
    )(*args)


def _conv_bwd(x, col_off, w, b, act, dout, dp, name):
    T = x.shape[0]
    C = w.shape[1]
    G = C // CONV_W
    tt = min(512, T)
    NT = T // tt
    has_b = b is not None

    def body(*refs):
        if has_b:
            x_ref, xp_ref, w_ref, b_ref, d_ref, dp_in, dx_ref, dw_ref, db_ref, pad, dpad = refs
        else:
            x_ref, xp_ref, w_ref, d_ref, dp_in, dx_ref, dw_ref, db_ref, pad, dpad = refs
        t = pl.program_id(1)
        first_tile = t == NT - 1

        @pl.when(t == 0)
        def _():
            dpad[pl.ds(tt, PAD), :] = jnp.zeros((PAD, CONV_W), f32)
            dw_ref[...] = jnp.zeros_like(dw_ref)
            db_ref[...] = jnp.zeros_like(db_ref)

        pad[pl.ds(0, PAD), :] = jnp.where(first_tile, 0.0, xp_ref[...])
        pad[pl.ds(PAD, tt), :] = x_ref[...]
        fold = lambda v: v.reshape(SUB_R // 8, 8, 128).sum(axis=0)
        for g in range(CONV_W // 128):
            ls = slice(g * 128, (g + 1) * 128)
            wv = w_ref[:, ls]
            acc = [jnp.zeros((8, 128), f32) for _ in range(CONV_K + 1)]
            for c in reversed(range(tt // SUB_R)):
                r0 = c * SUB_R
                xs = [pad[pl.ds(PAD - 3 + kk + r0, SUB_R), ls] for kk in range(CONV_K)]
                dy = d_ref[pl.ds(r0, SUB_R), ls]
                if act:
                    y = wv[0:1, :] * xs[0]
                    for kk in range(1, CONV_K):
                        y = y + wv[kk:kk + 1, :] * xs[kk]
                    if has_b:
                        y = y + b_ref[:, ls]
                    dy = dy * _dsilu(y)
                dpad[pl.ds(r0, SUB_R), ls] = dy
                dx = wv[3:4, :] * dy
                for j in range(1, CONV_K):
                    dx = dx + wv[3 - j:4 - j, :] * dpad[pl.ds(r0 + j, SUB_R), ls]
                dx_ref[pl.ds(r0, SUB_R), ls] = dx.astype(bf16)
                for kk in range(CONV_K):
                    acc[kk] = acc[kk] + fold(dy * xs[kk])
                acc[CONV_K] = acc[CONV_K] + fold(dy)
            for kk in range(CONV_K):
                dw_ref[kk:kk + 1, ls] += jnp.sum(acc[kk], axis=0, keepdims=True)
            db_ref[:, ls] += jnp.sum(acc[CONV_K], axis=0, keepdims=True)
        head = dpad[pl.ds(0, PAD), :]
        dpad[pl.ds(tt, PAD), :] = head

    rows8 = tt // PAD
    in_specs = [pl.BlockSpec((tt, CONV_W), lambda g, t: (NT - 1 - t, col_off + g)),
                pl.BlockSpec((PAD, CONV_W), lambda g, t: (jnp.maximum((NT - 1 - t) * rows8 - 1, 0), col_off + g)),
                pl.BlockSpec((CONV_K, CONV_W), lambda g, t: (0, g))]
    args = [x, x, w]
    if has_b:
        in_specs.append(pl.BlockSpec((1, CONV_W), lambda g, t: (0, g)))
        args.append(b)
    in_specs += [pl.BlockSpec((tt, CONV_W), lambda g, t: (NT - 1 - t, g)), pl.BlockSpec(memory_space=pl.ANY)]
    args += [dout, dp]
    return pl.pallas_call(
        body, grid=(G, NT), in_specs=in_specs,
        out_specs=[pl.BlockSpec((tt, CONV_W), lambda g, t: (NT - 1 - t, col_off + g)),
                   pl.BlockSpec((CONV_K, CONV_W), lambda g, t: (0, g)),
                   pl.BlockSpec((1, CONV_W), lambda g, t: (0, g))],
        out_shape=[jax.ShapeDtypeStruct(dp.shape, dp.dtype), jax.ShapeDtypeStruct((CONV_K, C), f32),
                   jax.ShapeDtypeStruct((1, C), f32)],
        input_output_aliases={len(args) - 1: 0},
        scratch_shapes=[pltpu.VMEM((tt + PAD, CONV_W), f32), pltpu.VMEM((tt + PAD, CONV_W), f32)],
        compiler_params=_cparams(("parallel", "arbitrary")), name=name,
    )(*args)


def _lru_gates(xc, wr, wi, br, bi, lam):
    r = _sigmoid(_dot(xc, wr, _NN) + br)
    i = _sigmoid(_dot(xc, wi, _NN) + bi)
    sp = _softplus(-lam)
    a = jnp.exp(-LRU_C * r * sp)
    mult = jnp.sqrt(1.0 - a * a)
    return r, i, sp, a, mult


def _lru_fwd(xc, p, y_off, wr, wi, br, bi, lam, mix, name):
    T = xc.shape[0]
    G = LRU_WIDTH // 128
    tt = min(512, T)
    NT = T // tt

    def body(x_ref, y_ref, wr_ref, wi_ref, br_ref, bi_ref, l_ref, mix_in, o_ref, h_ref, hc):
        t = pl.program_id(1)

        @pl.when(t == 0)
        def _():
            hc[...] = jnp.zeros_like(hc)

        x = x_ref[...]
        r, i, sp, a, mult = _lru_gates(x, wr_ref[...], wi_ref[...], br_ref[...], bi_ref[...], l_ref[...])
        row = lax.broadcasted_iota(jnp.int32, (tt, 128), 0)
        mult = jnp.where((row == 0) & (t == 0), 1.0, mult)
        U = x * i * mult
        A = a
        d = 1
        while d < tt:
            keep = row >= d
            Ush = jnp.where(keep, pltpu.roll(U, d, 0), 0.0)
            Ash = jnp.where(keep, pltpu.roll(A, d, 0), 1.0)
            U = A * Ush + U
            A = A * Ash
            d *= 2
        h = U + A * hc[0:1, :]
        h_ref[...] = h
        hc[...] = jnp.broadcast_to(h[tt - 1:tt, :], hc.shape)
        o_ref[...] = (h * _gelu(y_ref[...])).astype(bf16)

    tile = pl.BlockSpec((tt, 128), lambda g, t: (t, g))
    vec = pl.BlockSpec((1, 128), lambda g, t: (0, g))
    wsp = pl.BlockSpec((128, 128), lambda g, t: (g, g))
    return pl.pallas_call(
        body, grid=(G, NT),
        in_specs=[tile, pl.BlockSpec((tt, 128), lambda g, t: (t, y_off + g)), wsp, wsp, vec, vec, vec,
                  pl.BlockSpec(memory_space=pl.ANY)],
        out_specs=[pl.BlockSpec((tt, 128), lambda g, t: (t, G + g)), tile],
        out_shape=[jax.ShapeDtypeStruct(mix.shape, mix.dtype), jax.ShapeDtypeStruct((T, LRU_WIDTH), f32)],
        input_output_aliases={7: 0},
        scratch_shapes=[pltpu.VMEM((8, 128), f32)],
        compiler_params=_cparams(("parallel", "arbitrary")), name=name,
    )(xc, p, wr, wi, br, bi, lam, mix)


def _lru_bwd(xc, p, y_off, wr, wi, br, bi, lam, hs, dmix, d_off, dp, name):
    T = xc.shape[0]
    G = LRU_WIDTH // 128
    tt = min(512, T)
    NT = T // tt

    def body(x_ref, y_ref, wr_ref, wi_ref, br_ref, bi_ref, l_ref, h_ref, hp_ref, do_ref, dp_in,
             dx_ref, dy_ref, dwr_ref, dwi_ref, dbr_ref, dbi_ref, dl_ref, lc, an):
        t = pl.program_id(1)
        first_tile = t == NT - 1

        @pl.when(t == 0)
        def _():
            lc[...] = jnp.zeros_like(lc)
            an[...] = jnp.zeros_like(an)
            dwr_ref[...] = jnp.zeros_like(dwr_ref)
            dwi_ref[...] = jnp.zeros_like(dwi_ref)
            dbr_ref[...] = jnp.zeros_like(dbr_ref)
            dbi_ref[...] = jnp.zeros_like(dbi_ref)
            dl_ref[...] = jnp.zeros_like(dl_ref)

        x = x_ref[...]
        y = y_ref[...]
        wr, wi, lam_ = wr_ref[...], wi_ref[...], l_ref[...]
        r, i, sp, a, mult_raw = _lru_gates(x, wr, wi, br_ref[...], bi_ref[...], lam_)
        row = lax.broadcasted_iota(jnp.int32, (tt, 128), 0)
        t0 = (row == 0) & first_tile
        mult = jnp.where(t0, 1.0, mult_raw)
        h = h_ref[...]
        do = do_ref[...]
        dh = do * _gelu(y)
        dy_ref[...] = (do * h * _dgelu(y)).astype(bf16)
        B = jnp.where(row == tt - 1, an[0:1, :], pltpu.roll(a, tt - 1, 0))
        L = dh
        d = 1
        while d < tt:
            keep = row < tt - d
            Lsh = jnp.where(keep, pltpu.roll(L, tt - d, 0), 0.0)
            Bsh = jnp.where(keep, pltpu.roll(B, tt - d, 0), 1.0)
            L = L + B * Lsh
            B = B * Bsh
            d *= 2
        L = L + B * lc[0:1, :]
        lc[...] = jnp.broadcast_to(L[0:1, :], lc.shape)
        an[...] = jnp.broadcast_to(a[0:1, :], an.shape)
        hprev = jnp.where(first_tile, 0.0, hp_ref[...])[PAD - 1:PAD, :]
        hm1 = jnp.where(row == 0, hprev, pltpu.roll(h, 1, 0))
        da = L * hm1
        dxc = L * i * mult
        di = L * x * mult
        dmult = jnp.where(t0, 0.0, L * x * i)
        da = da - jnp.where(t0, 0.0, dmult * a / mult_raw)
        dlog_a = da * a
        dr = dlog_a * (-LRU_C) * sp
        dsp = jnp.sum(dlog_a * (-LRU_C) * r, axis=0, keepdims=True)
        dpr = dr * r * (1.0 - r)
        dpi = di * i * (1.0 - i)
        dx_ref[...] = dxc + _dot(dpr, wr, _NT) + _dot(dpi, wi, _NT)
        dwr_ref[0] += _dot(x, dpr, _TN)
        dwi_ref[0] += _dot(x, dpi, _TN)
        dbr_ref[...] += jnp.sum(dpr, axis=0, keepdims=True)
        dbi_ref[...] += jnp.sum(dpi, axis=0, keepdims=True)
        dl_ref[...] += dsp * (-_sigmoid(-lam_))

    rows8 = tt // PAD
    tile = pl.BlockSpec((tt, 128), lambda g, t: (NT - 1 - t, g))
    vec = pl.BlockSpec((1, 128), lambda g, t: (0, g))
    wsp = pl.BlockSpec((128, 128), lambda g, t: (g, g))
    wout = pl.BlockSpec((1, 128, 128), lambda g, t: (g, 0, 0))
    return pl.pallas_call(
        body, grid=(G, NT),
        in_specs=[tile, pl.BlockSpec((tt, 128), lambda g, t: (NT - 1 - t, y_off + g)), wsp, wsp, vec, vec, vec, tile,
                  pl.BlockSpec((PAD, 128), lambda g, t: (jnp.maximum((NT - 1 - t) * rows8 - 1, 0), g)),
                  pl.BlockSpec((tt, 128), lambda g, t: (NT - 1 - t, d_off + g)), pl.BlockSpec(memory_space=pl.ANY)],
        out_specs=[tile, pl.BlockSpec((tt, 128), lambda g, t: (NT - 1 - t, y_off + g)), wout, wout, vec, vec, vec],
        out_shape=[jax.ShapeDtypeStruct((T, LRU_WIDTH), f32), jax.ShapeDtypeStruct(dp.shape, dp.dtype),
                   jax.ShapeDtypeStruct((G, 128, 128), f32), jax.ShapeDtypeStruct((G, 128, 128), f32),
                   jax.ShapeDtypeStruct((1, LRU_WIDTH), f32), jax.ShapeDtypeStruct((1, LRU_WIDTH), f32),
                   jax.ShapeDtypeStruct((1, LRU_WIDTH), f32)],
        input_output_aliases={10: 1},
        scratch_shapes=[pltpu.VMEM((8, 128), f32), pltpu.VMEM((8, 128), f32)],
        compiler_params=_cparams(("parallel", "arbitrary")), name=name,
    )(xc, p, wr, wi, br, bi, lam, hs, hs, dmix, dp)


_NN3 = (((2,), (1,)), ((0,), (0,)))
_NT3 = (((2,), (2,)), ((0,), (0,)))
_TN3 = (((1,), (1,)), ((0,), (0,)))


def _heads(ref):
    return _stack_heads(ref, GDN_HEADS)


def _rowsum(x):
    H, C, L = x.shape
    return _dot(x.reshape(H * C, L), jnp.ones((L, HEAD_DIM), f32), _NN).reshape(H, C, HEAD_DIM)


def _gdn_chunk(qr, kr, v, ba, alog, dtb, S, saved=None):
    C, H = GDN_CHUNK, GDN_HEADS
    lane = lax.broadcasted_iota(jnp.int32, (C, 128), 1)
    lane3 = lax.broadcasted_iota(jnp.int32, (H, C, 128), 2)
    ri = lax.broadcasted_iota(jnp.int32, (C, C), 0)
    ci = lax.broadcasted_iota(jnp.int32, (C, C), 1)
    rowc = lax.broadcasted_iota(jnp.int32, (C, 1), 0)
    col = lambda m, j: jnp.sum(jnp.where(lane == j, m, 0.0), axis=1, keepdims=True)
    cols = lambda m, off: jnp.stack([col(m, off + h) for h in range(H)])
    ea = jnp.exp(alog)
    g_all = -ea * _softplus(ba + dtb)
    tri = (ri >= ci).astype(f32)
    G_all = _dot01(tri, g_all, _NN)
    beta = cols(_sigmoid(ba), 0)
    Gc = cols(G_all, H)
    rq = lax.rsqrt(_rowsum(qr * qr) + EPS)
    rk = lax.rsqrt(_rowsum(kr * kr) + EPS)
    qh, kn = qr * rq, kr * rk
    qn = qh * (HEAD_DIM ** -0.5)
    Grow = _dot01(jnp.ones((H, C, 128), f32), jnp.where(lane3 == 0, Gc, 0.0), _NT3)
    incl = ri >= ci
    Di = jnp.where(incl, jnp.exp(jnp.where(incl, Gc - Grow, 0.0)), 0.0)
    Ds = jnp.where(ri > ci, Di, 0.0)
    Gl = jnp.sum(jnp.where(rowc == C - 1, Gc, 0.0), axis=1, keepdims=True)
    eG = jnp.exp(Gc)
    eGl = jnp.exp(Gl - Gc)
    cd = jnp.exp(Gl)
    kb = kn * beta
    vb = v * beta
    Lm = _dot(kb, kn, _NT3) * Ds
    kbg = kb * eG
    QK = _dot(qn, kn, _NT3) * Di
    qg = qn * eG
    kg = kn * eGl
    if saved is None:
        same = lambda s: (ri // s) == (ci // s)
        Xd = jnp.where(same(8), -Lm, 0.0)
        Tinv = (ri == ci).astype(f32) + Xd
        Pw = Xd
        for _ in range(2):
            Pw = _dot(Pw, Pw, _NN3)
            Tinv = Tinv + _dot(Tinv, Pw, _NN3)
        for s in (8, 16, 32):
            off = jnp.where(same(2 * s) & jnp.logical_not(same(s)), Lm, 0.0)
            Tinv = Tinv - _dot(_dot(Tinv, off, _NN3), Tinv, _NN3)
        w = _dot(Tinv, kbg, _NN3)
        vn = _dot(Tinv, vb, _NN3) - _dot(w, S, _NN3)
        o = _dot(qg, S, _NN3) + _dot(QK, vn, _NN3)
        S1 = S * cd + _dot(kg, vn, _TN3)
    else:
        Tinv, vn, o = saved
        w = _dot(Tinv, kbg, _NN3)
        S1 = None
    return dict(beta=beta, g_all=g_all, rq=rq, rk=rk, qh=qh, kn=kn, qn=qn, Di=Di, Ds=Ds, eG=eG, eGl=eGl, cd=cd,
                kb=kb, vb=vb, Lm=Lm, Tinv=Tinv, kbg=kbg, w=w, QK=QK, qg=qg, kg=kg, vn=vn, o=o, S1=S1,
                lane=lane, ri=ri, ci=ci, rowc=rowc, ea=ea)


def _gdn_specs(T, rev):
    C = GDN_CHUNK
    H = GDN_HEADS
    K = min(GDN_STEP, T // C)
    NS = T // (C * K)
    nn = (lambda n: NS - 1 - n) if rev else (lambda n: n)
    wide = lambda blk: pl.BlockSpec((K * C, H * HEAD_DIM), lambda n: (nn(n), blk))
    one = lambda off: pl.BlockSpec((K * C, HEAD_DIM), lambda n: (nn(n), off))
    vec = pl.BlockSpec((1, 128), lambda n: (0, 0))
    st = lambda rows: pl.BlockSpec((K, H, rows, rows), lambda n: (nn(n), 0, 0, 0))
    return K, NS, wide, one, vec, st


def _gdn_fwd(qkv, p, alog, dtb, nw, name):
    T = qkv.shape[0]
    C, H = GDN_CHUNK, GDN_HEADS
    N = T // C
    K, NS, wide, one, vec, st_spec = _gdn_specs(T, False)

    def body(q_ref, k_ref, v_ref, z_ref, ba_ref, al_ref, dt_ref, nw_ref, y_ref, sp_ref, ti_ref, vn_ref, o_ref, st):
        @pl.when(pl.program_id(0) == 0)
        def _():
            st[...] = jnp.zeros_like(st)

        S = st[...]
        for c in range(K):
            rows = pl.ds(c * C, C)
            at = lambda ref: ref.at[rows, :]
            f = _gdn_chunk(_heads(at(q_ref)), _heads(at(k_ref)), _heads(at(v_ref)), ba_ref[rows, :], al_ref[...],
                           dt_ref[...], S)
            sp_ref[c] = S
            S = f["S1"]
            ti_ref[c] = f["Tinv"]
            o, vn = f["o"], f["vn"]
            r = lax.rsqrt(_rowsum(o * o) * (1.0 / HEAD_DIM) + EPS)
            y = o * r * nw_ref[...] * _silu(_heads(at(z_ref)))
            for h in range(H):
                sl = slice(h * HEAD_DIM, (h + 1) * HEAD_DIM)
                y_ref[rows, sl] = y[h].astype(bf16)
                vn_ref[rows, sl] = vn[h]
                o_ref[rows, sl] = o[h]
        st[...] = S

    wide_f32 = jax.ShapeDtypeStruct((T, H * HEAD_DIM), f32)
    return pl.pallas_call(
        body, grid=(NS,),
        in_specs=[wide(0), wide(1), wide(2), wide(3), one(4 * H), vec, vec, vec],
        out_specs=[wide(0), st_spec(HEAD_DIM), st_spec(C), wide(0), wide(0)],
        out_shape=[jax.ShapeDtypeStruct((T, H * HEAD_DIM), bf16), jax.ShapeDtypeStruct((N, H, HEAD_DIM, HEAD_DIM), f32),
                   jax.ShapeDtypeStruct((N, H, C, C), f32), wide_f32, wide_f32],
        scratch_shapes=[pltpu.VMEM((H, HEAD_DIM, HEAD_DIM), f32)],
        compiler_params=_cparams(("arbitrary",)), name=name,
    )(qkv, qkv, qkv, p, p, alog, dtb, nw)


def _gdn_bwd(qkv, p, alog, dtb, nw, sprev, tinv, vn_all, o_all, dy_all, name):
    T = qkv.shape[0]
    C, H = GDN_CHUNK, GDN_HEADS
    N = T // C
    K, NS, wide, one, vec, st_spec = _gdn_specs(T, True)
    rs = lambda m: jnp.sum(m, axis=2, keepdims=True)

    def put(ref, val, col=0):
        for h in range(H):
            ref[:, col + h * HEAD_DIM:col + (h + 1) * HEAD_DIM] = val[h].astype(ref.dtype)

    def body(q_ref, k_ref, v_ref, z_ref, ba_ref, al_ref, dt_ref, nw_ref, sp_ref, ti_ref, vn_ref, o_ref, dy_ref,
             dqkv_ref, dz_ref, dba_ref, dal_ref, ddt_ref, dnw_ref, dst):
        @pl.when(pl.program_id(0) == 0)
        def _():
            dst[...] = jnp.zeros_like(dst)
            dal_ref[...] = jnp.zeros_like(dal_ref)
            ddt_ref[...] = jnp.zeros_like(ddt_ref)
            dnw_ref[...] = jnp.zeros_like(dnw_ref)

        dS = dst[...]
        for c in reversed(range(K)):
            at = lambda ref, c=c: ref.at[pl.ds(c * C, C), :]
            dS = chunk(at(q_ref), at(k_ref), at(v_ref), at(z_ref), at(ba_ref), al_ref, dt_ref, nw_ref, sp_ref[c], ti_ref[c],
                       at(vn_ref), at(o_ref), at(dy_ref), at(dqkv_ref), at(dz_ref), at(dba_ref), dal_ref, ddt_ref, dnw_ref, dS)
        dst[...] = dS

    def chunk(q_ref, k_ref, v_ref, z_ref, ba_ref, al_ref, dt_ref, nw_ref, S, Tsaved, vn_ref, o_ref, dy_ref,
              dqkv_ref, dz_ref, dba_ref, dal_ref, ddt_ref, dnw_ref, dS1):
        ba, alog, dtb_, nwv = ba_ref[...], al_ref[...], dt_ref[...], nw_ref[...]
        v = _heads(v_ref)
        f = _gdn_chunk(_heads(q_ref), _heads(k_ref), v, ba, alog, dtb_, S, saved=(Tsaved, _heads(vn_ref), _heads(o_ref)))
        beta, kn, qn, kb, vb, Tinv, kbg = f["beta"], f["kn"], f["qn"], f["kb"], f["vb"], f["Tinv"], f["kbg"]
        eG, eGl, cd, Di, Ds, QK, vn, w_, qg, kg = (f["eG"], f["eGl"], f["cd"], f["Di"], f["Ds"], f["QK"], f["vn"],
                                                    f["w"], f["qg"], f["kg"])
        lane, ri, ci, rowc = f["lane"], f["ri"], f["ci"], f["rowc"]
        o = f["o"]
        z = _heads(z_ref)
        dy = _heads(dy_ref)
        r = lax.rsqrt(_rowsum(o * o) * (1.0 / HEAD_DIM) + EPS)
        nrm = o * r
        sz = _silu(z)
        dn = dy * nwv * sz
        put(dz_ref, dy * nrm * nwv * _dsilu(z))
        dnw_ref[...] += jnp.sum(jnp.sum(dy * nrm * sz, axis=0), axis=0, keepdims=True)
        do = r * (dn - nrm * (_rowsum(dn * nrm) * (1.0 / HEAD_DIM)))
        dcd = jnp.sum(jnp.sum(S * dS1, axis=2, keepdims=True), axis=1, keepdims=True)
        dkg = _dot(vn, dS1, _NT3)
        dvn = _dot(kg, dS1, _NN3) + _dot(QK, do, _TN3)
        dqg = _dot(do, S, _NT3)
        dQK = _dot(do, vn, _NT3)
        dw = -_dot(dvn, S, _NT3)
        dS0 = cd * dS1 + _dot(qg, do, _TN3) - _dot(w_, dvn, _TN3)
        dqn = dqg * eG
        dkn = dkg * eGl
        deGl = rs(dkg * kn)
        dQKr = dQK * Di
        E = dQK * QK
        dqn = dqn + _dot(dQKr, kn, _NN3)
        dkn = dkn + _dot(dQKr, qn, _TN3)
        dT = _dot(dvn, vb, _NT3) + _dot(dw, kbg, _NT3)
        dvb = _dot(Tinv, dvn, _TN3)
        dkbg = _dot(Tinv, dw, _TN3)
        dkb = dkbg * eG
        deG = rs(dqg * qn + dkbg * kb)
        dL = -_dot(_dot(Tinv, dT, _TN3), Tinv, _NT3)
        dKK = dL * Ds
        E = E + dL * f["Lm"]
        dkb = dkb + _dot(dKK, kn, _NN3)
        dkn = dkn + _dot(dKK, kb, _TN3) + dkb * beta
        dbeta = rs(dkb * kn + dvb * v)
        put(dqkv_ref, dvb * beta, 2 * H * HEAD_DIM)
        dG = rs(E) - rs(jnp.swapaxes(E, 1, 2)) + deG * eG - deGl * eGl
        dGl = jnp.sum(deGl * eGl, axis=1, keepdims=True) + dcd * cd
        dG = dG + jnp.where(rowc == C - 1, dGl, 0.0)
        qh = f["qh"]
        put(dqkv_ref, (HEAD_DIM ** -0.5) * f["rq"] * (dqn - qh * _rowsum(dqn * qh)))
        put(dqkv_ref, f["rk"] * (dkn - kn * _rowsum(dkn * kn)), H * HEAD_DIM)
        db = dbeta * beta * (1.0 - beta)
        db_all = jnp.where(lane == 0, db[0], 0.0)
        dG_all = jnp.where(lane == H, dG[0], 0.0)
        for h in range(1, H):
            db_all = db_all + jnp.where(lane == h, db[h], 0.0)
            dG_all = dG_all + jnp.where(lane == H + h, dG[h], 0.0)
        triu = (ri <= ci).astype(f32)
        dg_all = _dot01(triu, dG_all, _NN)
        da_all = dg_all * (-f["ea"]) * _sigmoid(ba + dtb_)
        dba_ref[...] = (db_all + da_all).astype(bf16)
        ddt_ref[...] += jnp.sum(da_all, axis=0, keepdims=True)
        dal_ref[...] += jnp.sum(dg_all * f["g_all"], axis=0, keepdims=True)
        return dS0

    small = jax.ShapeDtypeStruct((1, 128), f32)
    return pl.pallas_call(
        body, grid=(NS,),
        in_specs=[wide(0), wide(1), wide(2), wide(3), one(4 * H), vec, vec, vec, st_spec(HEAD_DIM), st_spec(C),
                  wide(0), wide(0), wide(0)],
        out_specs=[pl.BlockSpec((K * C, 3 * H * HEAD_DIM), lambda n: (NS - 1 - n, 0)), wide(3), one(0), vec, vec, vec],
        out_shape=[jax.ShapeDtypeStruct((T, 3 * H * HEAD_DIM), f32), jax.ShapeDtypeStruct((T, ODD_PAD), bf16),
                   jax.ShapeDtypeStruct((T, 128), bf16), small, small, small],
        scratch_shapes=[pltpu.VMEM((H, HEAD_DIM, HEAD_DIM), f32)],
        compiler_params=_cparams(("arbitrary",)), name=name,
    )(qkv, qkv, qkv, p, p, alog, dtb, nw, sprev, tinv, vn_all, o_all, dy_all)


def _adamw(w, gs, m, v, name, layer=None, prev=None):
    R, Cc = w.shape[-2:]
    S = gs.shape[0]
    tr = R
    for cand in (256, 128, 64, 32, 16, 8):
        if R % cand == 0 and R > cand:
            tr = cand
            break
    c1 = 1.0 - ADAM_B1 ** ADAM_STEP
    c2 = 1.0 - ADAM_B2 ** ADAM_STEP

    def body(w_ref, g_ref, m_ref, v_ref, *rest):
        go_ref, d_ref, mo_ref, vo_ref = rest[-4:]
        g = g_ref[0].astype(f32)
        for s in range(1, S):
            g = g + g_ref[s].astype(f32)
        mn = ADAM_B1 * m_ref[...] + (1.0 - ADAM_B1) * g
        vn = ADAM_B2 * v_ref[...] + (1.0 - ADAM_B2) * (g * g)
        go_ref[...] = g
        mo_ref[...] = mn
        vo_ref[...] = vn
        d_ref[...] = -ADAM_LR * ((mn / c1) / (jnp.sqrt(vn / c2) + ADAM_EPS) + ADAM_WD * w_ref[...])

    if layer is None:
        blk = pl.BlockSpec((tr, Cc), lambda i: (i, 0))
    else:
        blk = pl.BlockSpec((None, tr, Cc), lambda i: (layer, i, 0))
    out = jax.ShapeDtypeStruct(w.shape, f32)
    carried = [] if prev is None else list(prev)
    return pl.pallas_call(
        body, grid=(R // tr,),
        in_specs=[blk, pl.BlockSpec((S, tr, Cc), lambda i: (0, i, 0)), blk, blk]
        + [pl.BlockSpec(memory_space=pl.ANY)] * len(carried),
        out_specs=[blk] * 4, out_shape=[out] * 4,
        input_output_aliases={4 + j: j for j in range(len(carried))},
        compiler_params=_cparams(("parallel",)), name=name,
    )(w, gs, m, v, *carried)


def _me():
    x, y, c = lax.axis_index("x"), lax.axis_index("y"), lax.axis_index("c")
    return x, y, c, 4 * x + 2 * y + c


def _peer(k):
    x, y, c, _ = _me()
    px = 1 - x if k & 4 else x
    py = 1 - y if k & 2 else y
    pc = 1 - c if k & 1 else c
    return (px, py, pc), 4 * px + 2 * py + pc


_HBM = pl.BlockSpec(memory_space=pltpu.HBM)
_SEM = pl.BlockSpec(memory_space=pltpu.SEMAPHORE)
_EFFECT = pltpu.SideEffectType.DATAFLOW_SIDE_EFFECTING


def _copy(src, land, ssem, rsem, k, blocked, landing_slot_of_peer):
    pid, pidx = _peer(k)
    slot = pidx if landing_slot_of_peer else _me()[3]
    return pltpu.make_async_remote_copy(src_ref=src.at[pidx] if blocked else src, dst_ref=land.at[slot],
                                        send_sem=ssem.at[k - 1], recv_sem=rsem.at[k - 1], device_id=pid, device_id_type=MESH)


def _send_start(srcs, blocked, name):
    n = len(srcs)
    lands = [lax.empty(a.shape if blocked else (N_DEV,) + a.shape, a.dtype) for a in srcs]

    def body(*refs):
        src, land, sems, token = refs[:n], refs[n:2 * n], refs[2 * n:4 * n], refs[-1]
        for i in range(n):
            for k in range(1, N_DEV):
                _copy(src[i], land[i], sems[2 * i], sems[2 * i + 1], k, blocked, False).start()
        token[...] = jnp.zeros_like(token)

    sem = pltpu.SemaphoreType.DMA((N_DEV - 1,))
    hbm = lambda a: pltpu.with_memory_space_constraint(a, pltpu.HBM)
    res = pl.pallas_call(
        body, name=name,
        out_shape=tuple([sem] * (2 * n)) + tuple(pltpu.HBM(a.shape, a.dtype) for a in srcs + lands)
        + (jax.ShapeDtypeStruct((8, 128), f32),),
        in_specs=(_HBM,) * (2 * n),
        out_specs=(_SEM,) * (2 * n) + (_HBM,) * (2 * n) + (pl.BlockSpec(memory_space=pltpu.VMEM),),
        input_output_aliases={j: 2 * n + j for j in range(2 * n)},
        compiler_params=pltpu.CompilerParams(has_side_effects=_EFFECT),
    )(*[hbm(a) for a in srcs], *[hbm(a) for a in lands])
    handles = [(res[2 * i], res[2 * i + 1], res[2 * n + i], res[3 * n + i]) for i in range(n)]
    return handles, res[-1]


def _send_wait(handle, blocked, after, name):
    ssem, rsem, src, land = handle

    def body(src_ref, land_ref, ssem_ref, rsem_ref, after_ref, src_out, land_out):
        for k in range(1, N_DEV):
            cp = _copy(src_ref, land_ref, ssem_ref, rsem_ref, k, blocked, True)
            cp.wait_send()
            cp.wait_recv()

    return pl.pallas_call(
        body, name=name, out_shape=(pltpu.HBM(src.shape, src.dtype), pltpu.HBM(land.shape, land.dtype)),
        in_specs=(_HBM, _HBM, _SEM, _SEM, pl.BlockSpec(memory_space=pl.ANY)), out_specs=(_HBM, _HBM),
        input_output_aliases={0: 0, 1: 1}, compiler_params=pltpu.CompilerParams(has_side_effects=_EFFECT),
    )(src, land, ssem, rsem, after)[1]


def _block_diag(w):
    nb, bs = w.shape[0], w.shape[1]
    eye = jnp.eye(nb, dtype=w.dtype)
    return (eye[:, None, :, None] * w[:, :, None, :]).reshape(nb * bs, nb * bs)


def _diag_blocks(d):
    return jnp.stack([d[g, s * 64:(s + 1) * 64, s * 64:(s + 1) * 64] for g in range(4) for s in range(2)])


_SQUARE_TILES = dict(tm=1024, tn=1024, tk=1024)


def _mlp_fwd(x, hm, wu, wd, tag, epilogue, extras, outs):
    (r,) = _matmul(hm, wu, "nn", outs=[bf16], epilogue=lambda acc: (jnp.maximum(acc, 0.0),), name=f"mlp_up_{tag}")
    res = _matmul(r, wd, "nn", outs=outs, extras=(x,) + tuple(extras), epilogue=epilogue, a_map=jnp.square,
                  name=f"mlp_down_{tag}", **_SQUARE_TILES)
    return res, (hm, r)


def _mlp_bwd(x, nw, wu, wd, saved, dxo, dxo_b, tag):
    hm, r = saved
    (du,) = _matmul(dxo_b, wd, "nt", outs=[bf16], extras=(r,), epilogue=lambda acc, rr: (acc * (2.0 * rr.astype(f32)),),
                    name=f"mlp_dact_{tag}")
    (dwd,) = _matmul(r, dxo_b, "tn", outs=[bf16], a_map=jnp.square, name=f"mlp_dwd_{tag}", **_SQUARE_TILES)
    (dwu,) = _matmul(hm, du, "tn", outs=[bf16], shard_cols=2, name=f"mlp_dwu_{tag}")
    dx, dx_b, dnw = _matmul(du, wu, "nt", outs=_RMS_BWD_OUTS, extras=(x, dxo, nw), epilogue=_rms_bwd_ep,
                            name=f"mlp_dh_{tag}", **_SQUARE_TILES)
    return dx, dx_b, dnw, dwu, dwd.reshape(N_DEV, D_FF // N_DEV, D_MODEL)


def _local_step(x, tgt, P, weight, sink):
    T = x.shape[0]
    cos, sin = _rope_tables(T)
    rtab = _ret_tables()
    row = lambda a: a.reshape(1, -1)
    mix_nw, mlp_nw = P["mixer_norm_w"], P["mlp_norm_w"]
    wr_bd, wi_bd = _block_diag(P["lru_w_r"]), _block_diag(P["lru_w_i"])
    lru_b, lru_br, lru_bi, lru_lam = row(P["lru_conv_b"]), row(P["lru_b_r"]), row(P["lru_b_i"]), row(P["lru_lambda"])
    pad16 = lambda a: jnp.pad(a.reshape(1, GDN_HEADS), ((0, 0), (GDN_HEADS, 128 - 2 * GDN_HEADS)))
    alog, dtb = pad16(P["gdn_a_log"]), pad16(P["gdn_dt_bias"])
    gnw = row(P["gdn_norm_w"])

    x0 = x
    h0 = _rms_fwd(x0, mix_nw[0:1], "rms_mix_0")
    w_ie = weight("w_in_even", h0)
    (pe,) = _matmul(h0, w_ie, "nn", outs=[f32], name="in_even")
    mix0, o_ret, s_ret = _ret_fwd(pe, cos, sin, rtab, "ret_fwd")
    w_lc = weight("lru_conv_w", pe)
    xc = _conv_fwd(pe, 4, w_lc, lru_b, False, "lru_conv_fwd")
    mix0, h_lru = _lru_fwd(xc, pe, 20, wr_bd, wi_bd, lru_br, lru_bi, lru_lam, mix0, "lru_fwd")
    w_oe = weight("w_out_even", mix0)
    x1, hm0 = _matmul(mix0, w_oe, "nn", outs=[f32, bf16], extras=(x0, mlp_nw[0:1]), epilogue=_residual_rms_ep,
                      name="out_even", tm=1024, tn=D_MODEL)
    w_u0, w_d0 = weight("w_up0", x1), weight("w_down0", x1)
    (x2, h1), mlp0 = _mlp_fwd(x1, hm0, w_u0, w_d0, "0", _residual_rms_ep, (mix_nw[1:2],), [f32, bf16])
    w_io = weight("w_in_odd", h1)
    (po,) = _matmul(h1, w_io, "nn", outs=[f32], tm=1024, tn=ODD_PAD // 3, name="in_odd")
    w_gc = weight("gdn_conv_w", po)
    qkv = _conv_fwd(po, 0, w_gc, None, True, "gdn_conv_fwd")
    y_gdn, s_gdn, ti_gdn, vn_gdn, o_gdn = _gdn_fwd(qkv, po, alog, dtb, gnw, "gdn_fwd")
    w_oo = weight("w_out_odd", y_gdn)
    x3, hm1 = _matmul(y_gdn, w_oo, "nn", outs=[f32, bf16], extras=(x2, mlp_nw[1:2]), epilogue=_residual_rms_ep,
                      name="out_odd", tm=1024, tn=D_MODEL)
    w_u1, w_d1 = weight("w_up1", x3), weight("w_down1", x3)
    (loss, dx4, dx4_b, d_final), mlp1 = _mlp_fwd(x3, hm1, w_u1, w_d1, "1", _loss_ep, (row(P["final_norm_w"]), tgt),
                                                 _LOSS_OUTS)
    dx3, dx3_b, d_mlp_nw1, d_wu1, d_wd1 = _mlp_bwd(x3, mlp_nw[1:2], w_u1, w_d1, mlp1, dx4, dx4_b, "1")
    tok = sink(dict(w_up1=d_wu1, w_down1=d_wd1))
    (dy_gdn,) = _matmul(dx3_b, w_oo, "nt", outs=[f32], name="out_odd_dx")
    (d_woo,) = _matmul(y_gdn, dx3_b, "tn", outs=[bf16], name="out_odd_dw")
    dqkv, dpo, dba, d_alog, d_dtb, d_gnw = _gdn_bwd(qkv, po, alog, dtb, gnw + tok[0:1, :], s_gdn, ti_gdn, vn_gdn, o_gdn, dy_gdn,
                                                  "gdn_bwd")
    dpo, d_gconv, _ = _conv_bwd(po, 0, w_gc, None, True, dqkv, dpo, "gdn_conv_bwd")
    dpo = lax.dynamic_update_slice(dpo, dba, (0, 4 * D_MODEL))
    (d_wio,) = _matmul(h1, dpo, "tn", outs=[bf16], tn=ODD_PAD // 3, name="in_odd_dw")
    n_odd = ODD_IN // N_DEV
    tok = sink(dict(w_out_odd=d_woo.reshape(N_DEV, D_MODEL // N_DEV, D_MODEL),
                    w_in_odd=jnp.transpose(d_wio[:, :ODD_IN].reshape(D_MODEL, N_DEV, n_odd), (1, 0, 2))))
    dx2, dx2_b, d_mix_nw1 = _matmul(dpo, w_io, "nt", outs=_RMS_BWD_OUTS, extras=(x2, dx3, mix_nw[1:2] + tok[0:1, 0:1]),
                                    epilogue=_rms_bwd_ep, tm=1024, tn=1024, tk=ODD_PAD // 3, name="in_odd_dx")
    dx1, dx1_b, d_mlp_nw0, d_wu0, d_wd0 = _mlp_bwd(x1, mlp_nw[0:1], w_u0, w_d0, mlp0, dx2, dx2_b, "0")
    (d_woe,) = _matmul(mix0, dx1_b, "tn", outs=[bf16], name="out_even_dw")
    tok = sink(dict(w_up0=d_wu0, w_down0=d_wd0, w_out_even=d_woe.reshape(N_DEV, D_MODEL // N_DEV, D_MODEL)))
    (dmix0,) = _matmul(dx1_b, w_oe, "nt", outs=[f32], name="out_even_dx")
    dpe = _ret_bwd(pe, cos, sin, rtab, o_ret, s_ret, dmix0, "ret_bwd")
    dxc, dpe, d_wr, d_wi, d_br, d_bi, d_lam = _lru_bwd(xc, pe, 20, wr_bd, wi_bd, lru_br, lru_bi, lru_lam + tok[0:1, 0:1],
                                                       h_lru, dmix0, 4, dpe, "lru_bwd")
    dpe, d_lconv, d_lconv_b = _conv_bwd(pe, 4, w_lc, lru_b, False, dxc, dpe, "lru_conv_bwd")
    (d_wie,) = _matmul(h0, dpe, "tn", outs=[bf16], shard_cols=2, name="in_even_dw")
    dx0, _, d_mix_nw0 = _matmul(dpe, w_ie, "nt", outs=_RMS_BWD_OUTS, extras=(x0, dx1, mix_nw[0:1]), epilogue=_rms_bwd_ep,
                                name="in_even_dx", **_SQUARE_TILES)

    G = dict(
        mixer_norm_w=jnp.concatenate([d_mix_nw0, d_mix_nw1], axis=0),
        mlp_norm_w=jnp.concatenate([d_mlp_nw0, d_mlp_nw1], axis=0),
        final_norm_w=d_final.reshape(-1),
        w_in_even=d_wie, lru_conv_w=d_lconv, lru_conv_b=d_lconv_b.reshape(-1),
        lru_w_r=_diag_blocks(d_wr), lru_b_r=d_br.reshape(-1), lru_w_i=_diag_blocks(d_wi), lru_b_i=d_bi.reshape(-1),
        lru_lambda=d_lam.reshape(-1), gdn_conv_w=d_gconv,
        gdn_a_log=d_alog[0, GDN_HEADS:2 * GDN_HEADS], gdn_dt_bias=d_dtb[0, GDN_HEADS:2 * GDN_HEADS],
        gdn_norm_w=d_gnw.reshape(-1),
    )
    return loss, dx0, G


_SMALL = ["mixer_norm_w", "mlp_norm_w", "final_norm_w", "lru_conv_b", "lru_w_r", "lru_b_r", "lru_w_i", "lru_b_i",
          "lru_lambda", "gdn_a_log", "gdn_dt_bias", "gdn_norm_w"]
_PACK_ROWS = 688


def _pack(parts):
    flat = jnp.concatenate([p.reshape(-1) for p in parts])
    return jnp.pad(flat, (0, _PACK_ROWS * 128 - flat.shape[0])).reshape(_PACK_ROWS, 128)


def _unpack(packed, shapes):
    flat = packed.reshape(-1)
    out, off = [], 0
    for s in shapes:
        n = int(np.prod(s))
        out.append(flat[off:off + n].reshape(s))
        off += n
    return out


def kernel(x, mixer_norm_w, mlp_norm_w, final_norm_w, w_in_even, lru_conv_w, lru_conv_b, lru_w_r, lru_b_r, lru_w_i, lru_b_i, lru_lambda, w_out_even, w_in_odd, gdn_conv_w, gdn_a_log, gdn_dt_bias, gdn_norm_w, w_out_odd, w_up, w_down, loss_target, m_mixer_norm_w, m_mlp_norm_w, m_final_norm_w, m_w_in_even, m_lru_conv_w, m_lru_conv_b, m_lru_w_r, m_lru_b_r, m_lru_w_i, m_lru_b_i, m_lru_lambda, m_w_out_even, m_w_in_odd, m_gdn_conv_w, m_gdn_a_log, m_gdn_dt_bias, m_gdn_norm_w, m_w_out_odd, m_w_up, m_w_down, v_mixer_norm_w, v_mlp_norm_w, v_final_norm_w, v_w_in_even, v_lru_conv_w, v_lru_conv_b, v_lru_w_r, v_lru_b_r, v_lru_w_i, v_lru_b_i, v_lru_lambda, v_w_out_even, v_w_in_odd, v_gdn_conv_w, v_gdn_a_log, v_gdn_dt_bias, v_gdn_norm_w, v_w_out_odd, v_w_up, v_w_down):
    Pw = dict(mixer_norm_w=mixer_norm_w, mlp_norm_w=mlp_norm_w, final_norm_w=final_norm_w, w_in_even=w_in_even,
              lru_conv_w=lru_conv_w, lru_conv_b=lru_conv_b, lru_w_r=lru_w_r, lru_b_r=lru_b_r, lru_w_i=lru_w_i,
              lru_b_i=lru_b_i, lru_lambda=lru_lambda, w_out_even=w_out_even, w_in_odd=w_in_odd, gdn_conv_w=gdn_conv_w,
              gdn_a_log=gdn_a_log, gdn_dt_bias=gdn_dt_bias, gdn_norm_w=gdn_norm_w, w_out_odd=w_out_odd, w_up=w_up,
              w_down=w_down)
    Pm = dict(mixer_norm_w=m_mixer_norm_w, mlp_norm_w=m_mlp_norm_w, final_norm_w=m_final_norm_w, w_in_even=m_w_in_even,
              lru_conv_w=m_lru_conv_w, lru_conv_b=m_lru_conv_b, lru_w_r=m_lru_w_r, lru_b_r=m_lru_b_r, lru_w_i=m_lru_w_i,
              lru_b_i=m_lru_b_i, lru_lambda=m_lru_lambda, w_out_even=m_w_out_even, w_in_odd=m_w_in_odd,
              gdn_conv_w=m_gdn_conv_w, gdn_a_log=m_gdn_a_log, gdn_dt_bias=m_gdn_dt_bias, gdn_norm_w=m_gdn_norm_w,
              w_out_odd=m_w_out_odd, w_up=m_w_up, w_down=m_w_down)
    Pv = dict(mixer_norm_w=v_mixer_norm_w, mlp_norm_w=v_mlp_norm_w, final_norm_w=v_final_norm_w, w_in_even=v_w_in_even,
              lru_conv_w=v_lru_conv_w, lru_conv_b=v_lru_conv_b, lru_w_r=v_lru_w_r, lru_b_r=v_lru_b_r, lru_w_i=v_lru_w_i,
              lru_b_i=v_lru_b_i, lru_lambda=v_lru_lambda, w_out_even=v_w_out_even, w_in_odd=v_w_in_odd,
              gdn_conv_w=v_gdn_conv_w, gdn_a_log=v_gdn_a_log, gdn_dt_bias=v_gdn_dt_bias, gdn_norm_w=v_gdn_norm_w,
              w_out_odd=v_w_out_odd, w_up=v_w_up, w_down=v_w_down)
    me = _me()[3]
    T = x.shape[1]

    cols = lambda g: jnp.transpose(g, (1, 0, 2)).reshape(g.shape[1], -1)
    rows = lambda g: g.reshape(-1, g.shape[2])
    wide = lambda g: jnp.pad(cols(g), ((0, 0), (0, ODD_PAD - ODD_IN)))
    gather = dict(
        w_in_even=(w_in_even[0].astype(bf16), cols), lru_conv_w=(lru_conv_w[0], cols),
        w_out_even=(w_out_even[0].astype(bf16), rows), w_up0=(w_up[0].astype(bf16), cols), w_down0=(w_down[0].astype(bf16), rows),
        w_in_odd=(w_in_odd[0].astype(bf16), wide), gdn_conv_w=(gdn_conv_w[0], cols),
        w_out_odd=(w_out_odd[0].astype(bf16), rows), w_up1=(w_up[1].astype(bf16), cols), w_down1=(w_down[1].astype(bf16), rows))
    handles, tok = _send_start([s for s, _ in gather.values()], False, "gather_start")
    handles = dict(zip(gather, handles))
    full = {}

    def weight(name, after):
        if name not in full:
            landed = _send_wait(handles[name], False, after, f"gather_wait_{name}")
            shard, finish = gather[name]
            full[name] = finish(lax.dynamic_update_slice_in_dim(landed, shard[None], me, 0))
        return full[name]

    P = {k: Pw[k] for k in ("mlp_norm_w", "final_norm_w")}
    P["mixer_norm_w"] = mixer_norm_w + tok[0:1, 0:1]
    for k in ("lru_w_r", "lru_w_i", "lru_conv_b", "lru_b_r", "lru_b_i", "lru_lambda", "gdn_a_log", "gdn_dt_bias", "gdn_norm_w"):
        P[k] = Pw[k][0]

    sent = {}

    def sink(grads):
        hs, token = _send_start(list(grads.values()), True, "grads_start_" + "_".join(grads))
        for (name, g), h in zip(grads.items(), hs):
            sent[name] = (h, g)
        return token

    loss, dx, G = _local_step(x[0], loss_target[0], P, weight, sink)
    small_g = [G[k].reshape(Pw[k].shape) for k in _SMALL] + [G["lru_conv_w"], G["gdn_conv_w"]]
    packed = _pack(small_g)
    sink(dict(w_in_even=G["w_in_even"], small=jnp.broadcast_to(packed[None], (N_DEV,) + packed.shape)))

    def received(name, after=dx):
        h, g = sent[name]
        landed = _send_wait(h, True, after, f"grads_wait_{name}")
        return lax.dynamic_update_slice_in_dim(landed, lax.dynamic_slice_in_dim(g, me, 1, 0), me, 0)

    out = {}
    nff = D_FF // N_DEV

    def whole(name, gs, shape2d):
        res = _adamw(Pw[name].reshape(shape2d), gs, Pm[name].reshape(shape2d), Pv[name].reshape(shape2d), f"adamw_{name}")
        out[name] = tuple(a.reshape(Pw[name].shape) for a in res)

    def layers(name):
        res = None
        for l in range(2):
            res = _adamw(Pw[name], received(f"{name}{l}"), Pm[name], Pv[name], f"adamw_{name}{l}", layer=l, prev=res)
        out[name] = tuple(res)

    layers("w_up")
    layers("w_down")
    whole("w_out_odd", received("w_out_odd"), (128, D_MODEL))
    whole("w_in_odd", received("w_in_odd"), (D_MODEL, 514))
    whole("w_out_even", received("w_out_even"), (128, D_MODEL))
    whole("w_in_even", received("w_in_even", out["w_out_even"][1]), (D_MODEL, 384))
    small_shapes = [Pw[k].shape for k in _SMALL]
    pw, pm, pv = (_pack([Q[k] for k in _SMALL]) for Q in (Pw, Pm, Pv))
    sg, sd, sm, sv = _adamw(pw, received("small", out["w_in_even"][1]), pm, pv, "adamw_small")
    for arrs_i, packed_out in enumerate((sg, sd, sm, sv)):
        for k, a in zip(_SMALL, _unpack(packed_out, small_shapes)):
            out.setdefault(k, [None] * 4)[arrs_i] = a
    n_small = sum(int(np.prod(s)) for s in small_shapes)
    gflat = sg.reshape(-1)
    g_lconv = gflat[n_small:n_small + CONV_K * LRU_WIDTH].reshape(CONV_K, LRU_WIDTH)
    g_gconv = gflat[n_small + CONV_K * LRU_WIDTH:n_small + CONV_K * (LRU_WIDTH + 3072)].reshape(CONV_K, 3072)
    whole("lru_conv_w", lax.dynamic_slice_in_dim(g_lconv, me * 64, 64, axis=1)[None], (CONV_K, 64))
    whole("gdn_conv_w", lax.dynamic_slice_in_dim(g_gconv, me * 384, 384, axis=1)[None], (CONV_K, 384))

    names = ["mixer_norm_w", "mlp_norm_w", "final_norm_w", "w_in_even", "lru_conv_w", "lru_conv_b", "lru_w_r", "lru_b_r",
             "lru_w_i", "lru_b_i", "lru_lambda", "w_out_even", "w_in_odd", "gdn_conv_w", "gdn_a_log", "gdn_dt_bias",
             "gdn_norm_w", "w_out_odd", "w_up", "w_down"]
    total = lax.psum(loss[0, 0], ("x", "y", "c"))
    res = [total, dx[None]]
    for j in range(4):
        res += [out[k][j] for k in names]
    return tuple(res)
```

```python
import math

import numpy as np
import jax
import jax.numpy as jnp
from jax import lax
from jax.experimental import pallas as pl
from jax.experimental.pallas import tpu as pltpu

f32 = jnp.float32
bf16 = jnp.bfloat16

N_DEV = 8
D_MODEL = 1024
D_FF = 4096
EPS = 1e-6
RET_HEADS = 4
RET_CHUNK = 128
RET_STEP = 4
ROPE_THETA = 10000.0
LRU_WIDTH = 512
LRU_C = 8.0
GDN_HEADS = 8
GDN_CHUNK = 64
GDN_STEP = 2
HEAD_DIM = 128
ODD_IN = 4112
ODD_PAD = 4224
ADAM_LR, ADAM_B1, ADAM_B2, ADAM_EPS, ADAM_WD, ADAM_STEP = 0.001, 0.9, 0.999, 1e-08, 0.01, 10
VMEM_LIMIT = 56 * 1024 * 1024

_NN = (((1,), (0,)), ((), ()))
_NT = (((1,), (1,)), ((), ()))
_TN = (((0,), (0,)), ((), ()))
MESH = pl.DeviceIdType.MESH


def _cparams(sem):
    return pltpu.CompilerParams(dimension_semantics=sem, vmem_limit_bytes=VMEM_LIMIT)


def _dot(a, b, dn):
    return lax.dot_general(a.astype(bf16), b.astype(bf16), dn, preferred_element_type=f32)


def _dot01(a01, b, dn):
    a = a01.astype(bf16)
    b0 = b.astype(bf16)
    r1 = b - b0.astype(f32)
    b1 = r1.astype(bf16)
    b2 = (r1 - b1.astype(f32)).astype(bf16)
    d = lambda q: lax.dot_general(a, q, dn, preferred_element_type=f32)
    return d(b0) + (d(b1) + d(b2))


def _sigmoid(x):
    return jax.nn.sigmoid(x)


def _silu(x):
    return x * _sigmoid(x)


def _dsilu(x):
    s = _sigmoid(x)
    return s * (1.0 + x * (1.0 - s))


def _softplus(x):
    return jnp.maximum(x, 0.0) + jnp.log1p(jnp.exp(-jnp.abs(x)))


_GELU_C = math.sqrt(2.0 / math.pi)


def _gelu(y):
    return 0.5 * y * (1.0 + jnp.tanh(_GELU_C * (y + 0.044715 * y * y * y)))


def _dgelu(y):
    t = jnp.tanh(_GELU_C * (y + 0.044715 * y * y * y))
    return 0.5 * (1.0 + t) + 0.5 * y * (1.0 - t * t) * _GELU_C * (1.0 + 3.0 * 0.044715 * y * y)


def _matmul(a, b, form, *, outs, name, epilogue=None, extras=(), tm=2048, tn=512, tk=1024, shard_cols=0, a_map=None):
    if form == "tn":
        K, M = a.shape
    else:
        M, K = a.shape
    if b.ndim == 3:
        assert form in ("nn", "nt"), name
        N = b.shape[1] if form == "nt" else N_DEV * b.shape[2]
        if form == "nn":
            tn = b.shape[2]
        else:
            tk = b.shape[2]
    else:
        N = b.shape[0] if form == "nt" else b.shape[1]
    ns = N // N_DEV
    if shard_cols:
        tn = ns * shard_cols
    tm, tn, tk = min(tm, M), min(tn, N), min(tk, K)
    assert M % tm == 0 and N % tn == 0 and K % tk == 0, (name, M, N, K, tm, tn, tk)
    nk = K // tk
    dn = {"nn": _NN, "nt": _NT, "tn": _TN}[form]
    if form == "tn":
        a_spec = pl.BlockSpec((tk, tm), lambda i, j, k: (k, i))
    else:
        a_spec = pl.BlockSpec((tm, tk), lambda i, j, k: (i, k))
    if b.ndim == 3:
        b_spec = (pl.BlockSpec((None, tn, tk), lambda i, j, k: (k, j, 0)) if form == "nt"
                  else pl.BlockSpec((None, tk, tn), lambda i, j, k: (j, k, 0)))
    elif form == "nt":
        b_spec = pl.BlockSpec((tn, tk), lambda i, j, k: (j, k))
    else:
        b_spec = pl.BlockSpec((tk, tn), lambda i, j, k: (k, j))
    e_spec = pl.BlockSpec((tm, tn), lambda i, j, k: (i, j))
    v_spec = pl.BlockSpec((1, tn), lambda i, j, k: (0, j))
    if shard_cols:
        o_spec = pl.BlockSpec((shard_cols, tm, ns), lambda i, j, k: (j, i, 0))
        o_shape = (N_DEV, M, ns)
    else:
        o_spec = e_spec
        o_shape = (M, N)
    n_ex = len(extras)
    sums = [isinstance(o, tuple) for o in outs]
    assert not any(sums) or tn == N, name

    def finish(acc, ex, o_refs, row_tile):
        vals = (acc,) if epilogue is None else epilogue(acc, *[e[...] for e in ex])
        for r, v, is_sum in zip(o_refs, vals, sums):
            if is_sum:
                @pl.when(row_tile == 0)
                def _(r=r, v=v):
                    r[...] = v.astype(r.dtype)

                @pl.when(row_tile > 0)
                def _(r=r, v=v):
                    r[...] += v.astype(r.dtype)
            elif shard_cols:
                for s in range(shard_cols):
                    r[s] = v[:, s * ns:(s + 1) * ns].astype(r.dtype)
            else:
                r[...] = v.astype(r.dtype)

    def prod(a_ref, b_ref):
        av = a_ref[...]
        return _dot(av if a_map is None else a_map(av), b_ref[...], dn)

    def body_one(*refs):
        finish(prod(*refs[:2]), refs[2:2 + n_ex], refs[2 + n_ex:], pl.program_id(0))

    def body_acc(*refs):
        a_ref, b_ref = refs[:2]
        acc = refs[-1]
        k = pl.program_id(2)
        row_tile = pl.program_id(0)

        @pl.when(k == 0)
        def _():
            acc[...] = prod(a_ref, b_ref)

        @pl.when((k > 0) & (k < nk - 1))
        def _():
            acc[...] += prod(a_ref, b_ref)

        @pl.when(k == nk - 1)
        def _():
            finish(acc[...] + prod(a_ref, b_ref), refs[2:2 + n_ex], refs[2 + n_ex:-1], row_tile)

    return pl.pallas_call(
        body_one if nk == 1 else body_acc, grid=(M // tm, N // tn, nk),
        in_specs=[a_spec, b_spec] + [v_spec if e.shape[0] == 1 else e_spec for e in extras],
        out_specs=[v_spec if s else o_spec for s in sums],
        out_shape=[jax.ShapeDtypeStruct((1, N), o[1]) if s else jax.ShapeDtypeStruct(o_shape, o) for o, s in zip(outs, sums)],
        scratch_shapes=[] if nk == 1 else [pltpu.VMEM((tm, tn), f32)],
        compiler_params=_cparams(("arbitrary" if any(sums) else "parallel", "parallel", "arbitrary")), name=name,
    )(a, b, *extras)


def _rms_fwd(x, w, name):
    T, D = x.shape
    tt = min(512, T)

    def body(x_ref, w_ref, h_ref):
        xv = x_ref[...]
        r = lax.rsqrt(jnp.mean(xv * xv, axis=1, keepdims=True) + EPS)
        h_ref[...] = (xv * r * w_ref[...]).astype(bf16)

    return pl.pallas_call(
        body, grid=(T // tt,),
        in_specs=[pl.BlockSpec((tt, D), lambda i: (i, 0)), pl.BlockSpec((1, D), lambda i: (0, 0))],
        out_specs=pl.BlockSpec((tt, D), lambda i: (i, 0)),
        out_shape=jax.ShapeDtypeStruct((T, D), bf16),
        compiler_params=_cparams(("parallel",)), name=name,
    )(x, w)


def _residual_rms_ep(acc, res, w):
    x = res + acc
    r = lax.rsqrt(jnp.mean(x * x, axis=1, keepdims=True) + EPS)
    return x, x * r * w


_RMS_BWD_OUTS = [f32, bf16, ("sum", f32)]


def _rms_bwd_ep(dh, x, dres, w):
    r = lax.rsqrt(jnp.mean(x * x, axis=1, keepdims=True) + EPS)
    xn = x * r
    dhw = dh * w
    dx = dres + r * (dhw - xn * jnp.mean(dhw * xn, axis=1, keepdims=True))
    return dx, dx, jnp.sum(dh * xn, axis=0, keepdims=True)


_LOSS_OUTS = [("sum", f32), f32, bf16, ("sum", f32)]


def _loss_ep(acc, res, w, tgt):
    x = res + acc
    D = x.shape[1]
    r = lax.rsqrt(jnp.mean(x * x, axis=1, keepdims=True) + EPS)
    xn = x * r
    e = xn * w - tgt
    loss = 0.5 * jnp.sum(jnp.mean(e * e, axis=1, keepdims=True), axis=0, keepdims=True)
    dy = e * (1.0 / D)
    dyw = dy * w
    dx = r * (dyw - xn * jnp.mean(dyw * xn, axis=1, keepdims=True))
    return jnp.broadcast_to(loss, (1, D)), dx, dx, jnp.sum(dy * xn, axis=0, keepdims=True)


def _ret_tables():
    H, C = RET_HEADS, RET_CHUNK
    lg = np.log1p(-np.exp2(-5.0 - np.arange(H, dtype=np.float32))).astype(np.float32)
    idx = np.arange(C, dtype=np.float32)
    diff = idx[:, None] - idx[None, :]
    causal = diff >= 0
    dm = np.where(causal[None], np.exp(lg[:, None, None] * np.where(causal, diff, 0.0)[None]), 0.0)
    qd = np.exp(lg[:, None] * (idx[None, :] + 1.0))
    kd = np.exp(lg[:, None] * (C - 1.0 - idx[None, :]))
    cg = np.exp(lg * C)
    tab = np.zeros((H, 4, C, HEAD_DIM), np.float32)
    tab[:, 0] = dm
    tab[:, 1] = qd[:, :, None]
    tab[:, 2] = kd[:, :, None]
    tab[:, 3] = cg[:, None, None]
    return jnp.asarray(tab)


def _rope_tables(T):
    half = HEAD_DIM // 2
    inv = ROPE_THETA ** (-jnp.arange(half, dtype=f32) / half)
    ang = jnp.arange(T, dtype=jnp.int32).astype(f32)[:, None] * inv[None, :]
    c, s = jnp.cos(ang), jnp.sin(ang)
    return jnp.concatenate([c, c], axis=1), jnp.concatenate([-s, s], axis=1)


def _rope(x, cos, sin):
    return x * cos + pltpu.roll(x, HEAD_DIM // 2, 1) * sin


def _unrope(y, cos, sin):
    return y * cos + pltpu.roll(y * sin, HEAD_DIM // 2, 1)


def _stack_heads(ref, H, f=None):
    parts = [ref[:, h * HEAD_DIM:(h + 1) * HEAD_DIM] for h in range(H)]
    return jnp.stack(parts if f is None else [f(a) for a in parts])


def _ret_fwd(p, cos, sin, tab, name):
    T = p.shape[0]
    C, H = RET_CHUNK, RET_HEADS
    N = T // C
    K = min(RET_STEP, N)
    NS = N // K
    scale = HEAD_DIM ** -0.5

    def body(q_ref, k_ref, v_ref, g_ref, c_ref, s_ref, t_ref, y_ref, o_ref, sp_ref, st):
        @pl.when(pl.program_id(0) == 0)
        def _():
            st[...] = jnp.zeros_like(st)

        dm, qd, kd, cg = t_ref[:, 0], t_ref[:, 1], t_ref[:, 2], t_ref[:, 3]
        S = st[...]
        for c in range(K):
            rows = pl.ds(c * C, C)
            cos_, sin_ = c_ref[rows, :], s_ref[rows, :]
            rot = lambda a: _rope(a, cos_, sin_)
            q = _stack_heads(q_ref.at[rows, :], H, rot)
            k = _stack_heads(k_ref.at[rows, :], H, rot) * scale
            v = _stack_heads(v_ref.at[rows, :], H)
            P = _dot(q, k, _NT3) * dm
            o = _dot(P, v, _NN3) + _dot(q * qd, S, _NN3)
            sp_ref[c] = S
            S = cg * S + _dot(k * kd, v, _TN3)
            r = lax.rsqrt(jnp.mean(o * o, axis=2, keepdims=True) + EPS)
            y = o * r * _silu(_stack_heads(g_ref.at[rows, :], H))
            for h in range(H):
                o_ref[rows, h * HEAD_DIM:(h + 1) * HEAD_DIM] = o[h]
                y_ref[rows, h * HEAD_DIM:(h + 1) * HEAD_DIM] = y[h].astype(bf16)
        st[...] = S

    wide = lambda blk: pl.BlockSpec((K * C, H * HEAD_DIM), lambda n: (n, blk))
    tbl = pl.BlockSpec((K * C, HEAD_DIM), lambda n: (n, 0))
    return pl.pallas_call(
        body, grid=(NS,),
        in_specs=[wide(0), wide(1), wide(2), wide(3), tbl, tbl,
                  pl.BlockSpec((H, 4, C, HEAD_DIM), lambda n: (0, 0, 0, 0))],
        out_specs=[wide(0), wide(0), pl.BlockSpec((K, H, HEAD_DIM, HEAD_DIM), lambda n: (n, 0, 0, 0))],
        out_shape=[jax.ShapeDtypeStruct((T, D_MODEL), bf16), jax.ShapeDtypeStruct((T, H * HEAD_DIM), f32),
                   jax.ShapeDtypeStruct((N, H, HEAD_DIM, HEAD_DIM), f32)],
        scratch_shapes=[pltpu.VMEM((H, HEAD_DIM, HEAD_DIM), f32)],
        compiler_params=_cparams(("arbitrary",)), name=name,
    )(p, p, p, p, cos, sin, tab)


def _ret_bwd(p, cos, sin, tab, o_raw, sprev, dmix, name):
    T = p.shape[0]
    C, H = RET_CHUNK, RET_HEADS
    N = T // C
    K = min(RET_STEP, N)
    NS = N // K
    scale = HEAD_DIM ** -0.5
    W = H * HEAD_DIM

    def body(q_ref, k_ref, v_ref, g_ref, c_ref, s_ref, t_ref, o_ref, sp_ref, dy_ref, d_ref, dst):
        @pl.when(pl.program_id(0) == 0)
        def _():
            dst[...] = jnp.zeros_like(dst)

        dm, qd, kd, cg = t_ref[:, 0], t_ref[:, 1], t_ref[:, 2], t_ref[:, 3]
        dS1 = dst[...]
        for c in reversed(range(K)):
            rows = pl.ds(c * C, C)
            cos_, sin_ = c_ref[rows, :], s_ref[rows, :]
            rot = lambda a: _rope(a, cos_, sin_)
            q = _stack_heads(q_ref.at[rows, :], H, rot)
            k = _stack_heads(k_ref.at[rows, :], H, rot) * scale
            v = _stack_heads(v_ref.at[rows, :], H)
            g = _stack_heads(g_ref.at[rows, :], H)
            S = sp_ref[c]
            o = _stack_heads(o_ref.at[rows, :], H)
            dy = _stack_heads(dy_ref.at[rows, :], H)
            r = lax.rsqrt(jnp.mean(o * o, axis=2, keepdims=True) + EPS)
            nrm = o * r
            dn = dy * _silu(g)
            dg = dy * nrm * _dsilu(g)
            do = r * (dn - nrm * jnp.mean(dn * nrm, axis=2, keepdims=True))
            P = _dot(q, k, _NT3) * dm
            dP = _dot(do, v, _NT3) * dm
            dq = _dot(dP, k, _NN3) + _dot(do, S, _NT3) * qd
            dk = (_dot(dP, q, _TN3) + _dot(v, dS1, _NT3) * kd) * scale
            dv = _dot(P, do, _TN3) + _dot(k * kd, dS1, _NN3)
            dS1 = cg * dS1 + _dot(q * qd, do, _TN3)
            for h in range(H):
                d_ref[rows, h * HEAD_DIM:(h + 1) * HEAD_DIM] = _unrope(dq[h], cos_, sin_).astype(bf16)
                d_ref[rows, W + h * HEAD_DIM:W + (h + 1) * HEAD_DIM] = _unrope(dk[h], cos_, sin_).astype(bf16)
                d_ref[rows, 2 * W + h * HEAD_DIM:2 * W + (h + 1) * HEAD_DIM] = dv[h].astype(bf16)
                d_ref[rows, 3 * W + h * HEAD_DIM:3 * W + (h + 1) * HEAD_DIM] = dg[h].astype(bf16)
        dst[...] = dS1

    rev = lambda blk: pl.BlockSpec((K * C, W), lambda n: (NS - 1 - n, blk))
    tbl = pl.BlockSpec((K * C, HEAD_DIM), lambda n: (NS - 1 - n, 0))
    return pl.pallas_call(
        body, grid=(NS,),
        in_specs=[rev(0), rev(1), rev(2), rev(3), tbl, tbl,
                  pl.BlockSpec((H, 4, C, HEAD_DIM), lambda n: (0, 0, 0, 0)), rev(0),
                  pl.BlockSpec((K, H, HEAD_DIM, HEAD_DIM), lambda n: (NS - 1 - n, 0, 0, 0)), rev(0)],
        out_specs=pl.BlockSpec((K * C, 4 * W), lambda n: (NS - 1 - n, 0)),
        out_shape=jax.ShapeDtypeStruct((T, 6 * W), bf16),
        scratch_shapes=[pltpu.VMEM((H, HEAD_DIM, HEAD_DIM), f32)],
        compiler_params=_cparams(("arbitrary",)), name=name,
    )(p, p, p, p, cos, sin, tab, o_raw, sprev, dmix)


CONV_K = 4
CONV_W = 512
PAD = 8
SUB_R = 64


def _conv_fwd(x, col_off, w, b, act, name):
    T = x.shape[0]
    C = w.shape[1]
    G = C // CONV_W
    tt = min(512, T)
    NT = T // tt
    has_b = b is not None

    def body(*refs):
        if has_b:
            x_ref, w_ref, b_ref, y_ref, pad = refs
        else:
            x_ref, w_ref, y_ref, pad = refs
        t = pl.program_id(1)

        @pl.when(t == 0)
        def _():
            pad[pl.ds(0, PAD), :] = jnp.zeros((PAD, CONV_W), f32)

        pad[pl.ds(PAD, tt), :] = x_ref[...]
        for g in range(CONV_W // 128):
            ls = slice(g * 128, (g + 1) * 128)
            wv = w_ref[:, ls]
            for c in range(tt // SUB_R):
                r0 = c * SUB_R
                y = wv[0:1, :] * pad[pl.ds(PAD - 3 + r0, SUB_R), ls]
                for kk in range(1, CONV_K):
                    y = y + wv[kk:kk + 1, :] * pad[pl.ds(PAD - 3 + kk + r0, SUB_R), ls]
                if has_b:
                    y = y + b_ref[:, ls]
                y_ref[pl.ds(r0, SUB_R), ls] = _silu(y) if act else y
        tail = pad[pl.ds(tt, PAD), :]
        pad[pl.ds(0, PAD), :] = tail

    in_specs = [pl.BlockSpec((tt, CONV_W), lambda g, t: (t, col_off + g)),
                pl.BlockSpec((CONV_K, CONV_W), lambda g, t: (0, g))]
    args = [x, w]
    if has_b:
        in_specs.append(pl.BlockSpec((1, CONV_W), lambda g, t: (0, g)))
        args.append(b)
    return pl.pallas_call(
        body, grid=(G, NT), in_specs=in_specs,
        out_specs=pl.BlockSpec((tt, CONV_W), lambda g, t: (t, g)),
        out_shape=jax.ShapeDtypeStruct((T, C), f32),
        scratch_shapes=[pltpu.VMEM((tt + PAD, CONV_W), f32)],
        compiler_params=_cparams(("parallel", "arbitrary")), name=name,
    )(*args)


def _conv_bwd(x, col_off, w, b, act, dout, dp, name):
    T = x.shape[0]
    C = w.shape[1]
    G = C // CONV_W
    tt = min(512, T)
    NT = T // tt
    has_b = b is not None

    def body(*refs):
        if has_b:
            x_ref, xp_ref, w_ref, b_ref, d_ref, dp_in, dx_ref, dw_ref, db_ref, pad, dpad = refs
        else:
            x_ref, xp_ref, w_ref, d_ref, dp_in, dx_ref, dw_ref, db_ref, pad, dpad = refs
        t = pl.program_id(1)
        first_tile = t == NT - 1

        @pl.when(t == 0)
        def _():
            dpad[pl.ds(tt, PAD), :] = jnp.zeros((PAD, CONV_W), f32)
            dw_ref[...] = jnp.zeros_like(dw_ref)
            db_ref[...] = jnp.zeros_like(db_ref)

        pad[pl.ds(0, PAD), :] = jnp.where(first_tile, 0.0, xp_ref[...])
        pad[pl.ds(PAD, tt), :] = x_ref[...]
        fold = lambda v: v.reshape(SUB_R // 8, 8, 128).sum(axis=0)
        for g in range(CONV_W // 128):
            ls = slice(g * 128, (g + 1) * 128)
            wv = w_ref[:, ls]
            acc = [jnp.zeros((8, 128), f32) for _ in range(CONV_K + 1)]
            for c in reversed(range(tt // SUB_R)):
                r0 = c * SUB_R
                xs = [pad[pl.ds(PAD - 3 + kk + r0, SUB_R), ls] for kk in range(CONV_K)]
                dy = d_ref[pl.ds(r0, SUB_R), ls]
                if act:
                    y = wv[0:1, :] * xs[0]
                    for kk in range(1, CONV_K):
                        y = y + wv[kk:kk + 1, :] * xs[kk]
                    if has_b:
                        y = y + b_ref[:, ls]
                    dy = dy * _dsilu(y)
                dpad[pl.ds(r0, SUB_R), ls] = dy
                dx = wv[3:4, :] * dy
                for j in range(1, CONV_K):
                    dx = dx + wv[3 - j:4 - j, :] * dpad[pl.ds(r0 + j, SUB_R), ls]
                dx_ref[pl.ds(r0, SUB_R), ls] = dx.astype(bf16)
                for kk in range(CONV_K):
                    acc[kk] = acc[kk] + fold(dy * xs[kk])
                acc[CONV_K] = acc[CONV_K] + fold(dy)
            for kk in range(CONV_K):
                dw_ref[kk:kk + 1, ls] += jnp.sum(acc[kk], axis=0, keepdims=True)
            db_ref[:, ls] += jnp.sum(acc[CONV_K], axis=0, keepdims=True)
        head = dpad[pl.ds(0, PAD), :]
        dpad[pl.ds(tt, PAD), :] = head

    rows8 = tt // PAD
    in_specs = [pl.BlockSpec((tt, CONV_W), lambda g, t: (NT - 1 - t, col_off + g)),
                pl.BlockSpec((PAD, CONV_W), lambda g, t: (jnp.maximum((NT - 1 - t) * rows8 - 1, 0), col_off + g)),
                pl.BlockSpec((CONV_K, CONV_W), lambda g, t: (0, g))]
    args = [x, x, w]
    if has_b:
        in_specs.append(pl.BlockSpec((1, CONV_W), lambda g, t: (0, g)))
        args.append(b)
    in_specs += [pl.BlockSpec((tt, CONV_W), lambda g, t: (NT - 1 - t, g)), pl.BlockSpec(memory_space=pl.ANY)]
    args += [dout, dp]
    return pl.pallas_call(
        body, grid=(G, NT), in_specs=in_specs,
        out_specs=[pl.BlockSpec((tt, CONV_W), lambda g, t: (NT - 1 - t, col_off + g)),
                   pl.BlockSpec((CONV_K, CONV_W), lambda g, t: (0, g)),
                   pl.BlockSpec((1, CONV_W), lambda g, t: (0, g))],
        out_shape=[jax.ShapeDtypeStruct(dp.shape, dp.dtype), jax.ShapeDtypeStruct((CONV_K, C), f32),
                   jax.ShapeDtypeStruct((1, C), f32)],
        input_output_aliases={len(args) - 1: 0},
        scratch_shapes=[pltpu.VMEM((tt + PAD, CONV_W), f32), pltpu.VMEM((tt + PAD, CONV_W), f32)],
        compiler_params=_cparams(("parallel", "arbitrary")), name=name,
    )(*args)


def _lru_gates(xc, wr, wi, br, bi, lam):
    r = _sigmoid(_dot(xc, wr, _NN) + br)
    i = _sigmoid(_dot(xc, wi, _NN) + bi)
    sp = _softplus(-lam)
    a = jnp.exp(-LRU_C * r * sp)
    mult = jnp.sqrt(1.0 - a * a)
    return r, i, sp, a, mult


def _lru_fwd(xc, p, y_off, wr, wi, br, bi, lam, mix, name):
    T = xc.shape[0]
    G = LRU_WIDTH // 128
    tt = min(512, T)
    NT = T // tt

    def body(x_ref, y_ref, wr_ref, wi_ref, br_ref, bi_ref, l_ref, mix_in, o_ref, h_ref, hc):
        t = pl.program_id(1)

        @pl.when(t == 0)
        def _():
            hc[...] = jnp.zeros_like(hc)

        x = x_ref[...]
        r, i, sp, a, mult = _lru_gates(x, wr_ref[...], wi_ref[...], br_ref[...], bi_ref[...], l_ref[...])
        row = lax.broadcasted_iota(jnp.int32, (tt, 128), 0)
        mult = jnp.where((row == 0) & (t == 0), 1.0, mult)
        U = x * i * mult
        A = a
        d = 1
        while d < tt:
            keep = row >= d
            Ush = jnp.where(keep, pltpu.roll(U, d, 0), 0.0)
            Ash = jnp.where(keep, pltpu.roll(A, d, 0), 1.0)
            U = A * Ush + U
            A = A * Ash
            d *= 2
        h = U + A * hc[0:1, :]
        h_ref[...] = h
        hc[...] = jnp.broadcast_to(h[tt - 1:tt, :], hc.shape)
        o_ref[...] = (h * _gelu(y_ref[...])).astype(bf16)

    tile = pl.BlockSpec((tt, 128), lambda g, t: (t, g))
    vec = pl.BlockSpec((1, 128), lambda g, t: (0, g))
    wsp = pl.BlockSpec((128, 128), lambda g, t: (g, g))
    return pl.pallas_call(
        body, grid=(G, NT),
        in_specs=[tile, pl.BlockSpec((tt, 128), lambda g, t: (t, y_off + g)), wsp, wsp, vec, vec, vec,
                  pl.BlockSpec(memory_space=pl.ANY)],
        out_specs=[pl.BlockSpec((tt, 128), lambda g, t: (t, G + g)), tile],
        out_shape=[jax.ShapeDtypeStruct(mix.shape, mix.dtype), jax.ShapeDtypeStruct((T, LRU_WIDTH), f32)],
        input_output_aliases={7: 0},
        scratch_shapes=[pltpu.VMEM((8, 128), f32)],
        compiler_params=_cparams(("parallel", "arbitrary")), name=name,
    )(xc, p, wr, wi, br, bi, lam, mix)


def _lru_bwd(xc, p, y_off, wr, wi, br, bi, lam, hs, dmix, d_off, dp, name):
    T = xc.shape[0]
    G = LRU_WIDTH // 128
    tt = min(512, T)
    NT = T // tt

    def body(x_ref, y_ref, wr_ref, wi_ref, br_ref, bi_ref, l_ref, h_ref, hp_ref, do_ref, dp_in,
             dx_ref, dy_ref, dwr_ref, dwi_ref, dbr_ref, dbi_ref, dl_ref, lc, an):
        t = pl.program_id(1)
        first_tile = t == NT - 1

        @pl.when(t == 0)
        def _():
            lc[...] = jnp.zeros_like(lc)
            an[...] = jnp.zeros_like(an)
            dwr_ref[...] = jnp.zeros_like(dwr_ref)
            dwi_ref[...] = jnp.zeros_like(dwi_ref)
            dbr_ref[...] = jnp.zeros_like(dbr_ref)
            dbi_ref[...] = jnp.zeros_like(dbi_ref)
            dl_ref[...] = jnp.zeros_like(dl_ref)

        x = x_ref[...]
        y = y_ref[...]
        wr, wi, lam_ = wr_ref[...], wi_ref[...], l_ref[...]
        r, i, sp, a, mult_raw = _lru_gates(x, wr, wi, br_ref[...], bi_ref[...], lam_)
        row = lax.broadcasted_iota(jnp.int32, (tt, 128), 0)
        t0 = (row == 0) & first_tile
        mult = jnp.where(t0, 1.0, mult_raw)
        h = h_ref[...]
        do = do_ref[...]
        dh = do * _gelu(y)
        dy_ref[...] = (do * h * _dgelu(y)).astype(bf16)
        B = jnp.where(row == tt - 1, an[0:1, :], pltpu.roll(a, tt - 1, 0))
        L = dh
        d = 1
        while d < tt:
            keep = row < tt - d
            Lsh = jnp.where(keep, pltpu.roll(L, tt - d, 0), 0.0)
            Bsh = jnp.where(keep, pltpu.roll(B, tt - d, 0), 1.0)
            L = L + B * Lsh
            B = B * Bsh
            d *= 2
        L = L + B * lc[0:1, :]
        lc[...] = jnp.broadcast_to(L[0:1, :], lc.shape)
        an[...] = jnp.broadcast_to(a[0:1, :], an.shape)
        hprev = jnp.where(first_tile, 0.0, hp_ref[...])[PAD - 1:PAD, :]
        hm1 = jnp.where(row == 0, hprev, pltpu.roll(h, 1, 0))
        da = L * hm1
        dxc = L * i * mult
        di = L * x * mult
        dmult = jnp.where(t0, 0.0, L * x * i)
        da = da - jnp.where(t0, 0.0, dmult * a / mult_raw)
        dlog_a = da * a
        dr = dlog_a * (-LRU_C) * sp
        dsp = jnp.sum(dlog_a * (-LRU_C) * r, axis=0, keepdims=True)
        dpr = dr * r * (1.0 - r)
        dpi = di * i * (1.0 - i)
        dx_ref[...] = dxc + _dot(dpr, wr, _NT) + _dot(dpi, wi, _NT)
        dwr_ref[0] += _dot(x, dpr, _TN)
        dwi_ref[0] += _dot(x, dpi, _TN)
        dbr_ref[...] += jnp.sum(dpr, axis=0, keepdims=True)
        dbi_ref[...] += jnp.sum(dpi, axis=0, keepdims=True)
        dl_ref[...] += dsp * (-_sigmoid(-lam_))

    rows8 = tt // PAD
    tile = pl.BlockSpec((tt, 128), lambda g, t: (NT - 1 - t, g))
    vec = pl.BlockSpec((1, 128), lambda g, t: (0, g))
    wsp = pl.BlockSpec((128, 128), lambda g, t: (g, g))
    wout = pl.BlockSpec((1, 128, 128), lambda g, t: (g, 0, 0))
    return pl.pallas_call(
        body, grid=(G, NT),
        in_specs=[tile, pl.BlockSpec((tt, 128), lambda g, t: (NT - 1 - t, y_off + g)), wsp, wsp, vec, vec, vec, tile,
                  pl.BlockSpec((PAD, 128), lambda g, t: (jnp.maximum((NT - 1 - t) * rows8 - 1, 0), g)),
                  pl.BlockSpec((tt, 128), lambda g, t: (NT - 1 - t, d_off + g)), pl.BlockSpec(memory_space=pl.ANY)],
        out_specs=[tile, pl.BlockSpec((tt, 128), lambda g, t: (NT - 1 - t, y_off + g)), wout, wout, vec, vec, vec],
        out_shape=[jax.ShapeDtypeStruct((T, LRU_WIDTH), f32), jax.ShapeDtypeStruct(dp.shape, dp.dtype),
                   jax.ShapeDtypeStruct((G, 128, 128), f32), jax.ShapeDtypeStruct((G, 128, 128), f32),
                   jax.ShapeDtypeStruct((1, LRU_WIDTH), f32), jax.ShapeDtypeStruct((1, LRU_WIDTH), f32),
                   jax.ShapeDtypeStruct((1, LRU_WIDTH), f32)],
        input_output_aliases={10: 1},
        scratch_shapes=[pltpu.VMEM((8, 128), f32), pltpu.VMEM((8, 128), f32)],
        compiler_params=_cparams(("parallel", "arbitrary")), name=name,
    )(xc, p, wr, wi, br, bi, lam, hs, hs, dmix, dp)


_NN3 = (((2,), (1,)), ((0,), (0,)))
_NT3 = (((2,), (2,)), ((0,), (0,)))
_TN3 = (((1,), (1,)), ((0,), (0,)))


def _heads(ref):
    return _stack_heads(ref, GDN_HEADS)


def _rowsum(x):
    H, C, L = x.shape
    return _dot(x.reshape(H * C, L), jnp.ones((L, HEAD_DIM), f32), _NN).reshape(H, C, HEAD_DIM)


def _gdn_chunk(qr, kr, v, ba, alog, dtb, S, saved=None):
    C, H = GDN_CHUNK, GDN_HEADS
    lane = lax.broadcasted_iota(jnp.int32, (C, 128), 1)
    lane3 = lax.broadcasted_iota(jnp.int32, (H, C, 128), 2)
    ri = lax.broadcasted_iota(jnp.int32, (C, C), 0)
    ci = lax.broadcasted_iota(jnp.int32, (C, C), 1)
    rowc = lax.broadcasted_iota(jnp.int32, (C, 1), 0)
    col = lambda m, j: jnp.sum(jnp.where(lane == j, m, 0.0), axis=1, keepdims=True)
    cols = lambda m, off: jnp.stack([col(m, off + h) for h in range(H)])
    ea = jnp.exp(alog)
    g_all = -ea * _softplus(ba + dtb)
    tri = (ri >= ci).astype(f32)
    G_all = _dot01(tri, g_all, _NN)
    beta = cols(_sigmoid(ba), 0)
    Gc = cols(G_all, H)
    rq = lax.rsqrt(_rowsum(qr * qr) + EPS)
    rk = lax.rsqrt(_rowsum(kr * kr) + EPS)
    qh, kn = qr * rq, kr * rk
    qn = qh * (HEAD_DIM ** -0.5)
    Grow = _dot01(jnp.ones((H, C, 128), f32), jnp.where(lane3 == 0, Gc, 0.0), _NT3)
    incl = ri >= ci
    Di = jnp.where(incl, jnp.exp(jnp.where(incl, Gc - Grow, 0.0)), 0.0)
    Ds = jnp.where(ri > ci, Di, 0.0)
    Gl = jnp.sum(jnp.where(rowc == C - 1, Gc, 0.0), axis=1, keepdims=True)
    eG = jnp.exp(Gc)
    eGl = jnp.exp(Gl - Gc)
    cd = jnp.exp(Gl)
    kb = kn * beta
    vb = v * beta
    Lm = _dot(kb, kn, _NT3) * Ds
    kbg = kb * eG
    QK = _dot(qn, kn, _NT3) * Di
    qg = qn * eG
    kg = kn * eGl
    if saved is None:
        same = lambda s: (ri // s) == (ci // s)
        Xd = jnp.where(same(8), -Lm, 0.0)
        Tinv = (ri == ci).astype(f32) + Xd
        Pw = Xd
        for _ in range(2):
            Pw = _dot(Pw, Pw, _NN3)
            Tinv = Tinv + _dot(Tinv, Pw, _NN3)
        for s in (8, 16, 32):
            off = jnp.where(same(2 * s) & jnp.logical_not(same(s)), Lm, 0.0)
            Tinv = Tinv - _dot(_dot(Tinv, off, _NN3), Tinv, _NN3)
        w = _dot(Tinv, kbg, _NN3)
        vn = _dot(Tinv, vb, _NN3) - _dot(w, S, _NN3)
        o = _dot(qg, S, _NN3) + _dot(QK, vn, _NN3)
        S1 = S * cd + _dot(kg, vn, _TN3)
    else:
        Tinv, vn, o = saved
        w = _dot(Tinv, kbg, _NN3)
        S1 = None
    return dict(beta=beta, g_all=g_all, rq=rq, rk=rk, qh=qh, kn=kn, qn=qn, Di=Di, Ds=Ds, eG=eG, eGl=eGl, cd=cd,
                kb=kb, vb=vb, Lm=Lm, Tinv=Tinv, kbg=kbg, w=w, QK=QK, qg=qg, kg=kg, vn=vn, o=o, S1=S1,
                lane=lane, ri=ri, ci=ci, rowc=rowc, ea=ea)


def _gdn_specs(T, rev):
    C = GDN_CHUNK
    H = GDN_HEADS
    K = min(GDN_STEP, T // C)
    NS = T // (C * K)
    nn = (lambda n: NS - 1 - n) if rev else (lambda n: n)
    wide = lambda blk: pl.BlockSpec((K * C, H * HEAD_DIM), lambda n: (nn(n), blk))
    one = lambda off: pl.BlockSpec((K * C, HEAD_DIM), lambda n: (nn(n), off))
    vec = pl.BlockSpec((1, 128), lambda n: (0, 0))
    st = lambda rows: pl.BlockSpec((K, H, rows, rows), lambda n: (nn(n), 0, 0, 0))
    return K, NS, wide, one, vec, st


def _gdn_fwd(qkv, p, alog, dtb, nw, name):
    T = qkv.shape[0]
    C, H = GDN_CHUNK, GDN_HEADS
    N = T // C
    K, NS, wide, one, vec, st_spec = _gdn_specs(T, False)

    def body(q_ref, k_ref, v_ref, z_ref, ba_ref, al_ref, dt_ref, nw_ref, y_ref, sp_ref, ti_ref, vn_ref, o_ref, st):
        @pl.when(pl.program_id(0) == 0)
        def _():
            st[...] = jnp.zeros_like(st)

        S = st[...]
        for c in range(K):
            rows = pl.ds(c * C, C)
            at = lambda ref: ref.at[rows, :]
            f = _gdn_chunk(_heads(at(q_ref)), _heads(at(k_ref)), _heads(at(v_ref)), ba_ref[rows, :], al_ref[...],
                           dt_ref[...], S)
            sp_ref[c] = S
            S = f["S1"]
            ti_ref[c] = f["Tinv"]
            o, vn = f["o"], f["vn"]
            r = lax.rsqrt(_rowsum(o * o) * (1.0 / HEAD_DIM) + EPS)
            y = o * r * nw_ref[...] * _silu(_heads(at(z_ref)))
            for h in range(H):
                sl = slice(h * HEAD_DIM, (h + 1) * HEAD_DIM)
                y_ref[rows, sl] = y[h].astype(bf16)
                vn_ref[rows, sl] = vn[h]
                o_ref[rows, sl] = o[h]
        st[...] = S

    wide_f32 = jax.ShapeDtypeStruct((T, H * HEAD_DIM), f32)
    return pl.pallas_call(
        body, grid=(NS,),
        in_specs=[wide(0), wide(1), wide(2), wide(3), one(4 * H), vec, vec, vec],
        out_specs=[wide(0), st_spec(HEAD_DIM), st_spec(C), wide(0), wide(0)],
        out_shape=[jax.ShapeDtypeStruct((T, H * HEAD_DIM), bf16), jax.ShapeDtypeStruct((N, H, HEAD_DIM, HEAD_DIM), f32),
                   jax.ShapeDtypeStruct((N, H, C, C), f32), wide_f32, wide_f32],
        scratch_shapes=[pltpu.VMEM((H, HEAD_DIM, HEAD_DIM), f32)],
        compiler_params=_cparams(("arbitrary",)), name=name,
    )(qkv, qkv, qkv, p, p, alog, dtb, nw)


def _gdn_bwd(qkv, p, alog, dtb, nw, sprev, tinv, vn_all, o_all, dy_all, name):
    T = qkv.shape[0]
    C, H = GDN_CHUNK, GDN_HEADS
    N = T // C
    K, NS, wide, one, vec, st_spec = _gdn_specs(T, True)
    rs = lambda m: jnp.sum(m, axis=2, keepdims=True)

    def put(ref, val, col=0):
        for h in range(H):
            ref[:, col + h * HEAD_DIM:col + (h + 1) * HEAD_DIM] = val[h].astype(ref.dtype)

    def body(q_ref, k_ref, v_ref, z_ref, ba_ref, al_ref, dt_ref, nw_ref, sp_ref, ti_ref, vn_ref, o_ref, dy_ref,
             dqkv_ref, dz_ref, dba_ref, dal_ref, ddt_ref, dnw_ref, dst):
        @pl.when(pl.program_id(0) == 0)
        def _():
            dst[...] = jnp.zeros_like(dst)
            dal_ref[...] = jnp.zeros_like(dal_ref)
            ddt_ref[...] = jnp.zeros_like(ddt_ref)
            dnw_ref[...] = jnp.zeros_like(dnw_ref)

        dS = dst[...]
        for c in reversed(range(K)):
            at = lambda ref, c=c: ref.at[pl.ds(c * C, C), :]
            dS = chunk(at(q_ref), at(k_ref), at(v_ref), at(z_ref), at(ba_ref), al_ref, dt_ref, nw_ref, sp_ref[c], ti_ref[c],
                       at(vn_ref), at(o_ref), at(dy_ref), at(dqkv_ref), at(dz_ref), at(dba_ref), dal_ref, ddt_ref, dnw_ref, dS)
        dst[...] = dS

    def chunk(q_ref, k_ref, v_ref, z_ref, ba_ref, al_ref, dt_ref, nw_ref, S, Tsaved, vn_ref, o_ref, dy_ref,
              dqkv_ref, dz_ref, dba_ref, dal_ref, ddt_ref, dnw_ref, dS1):
        ba, alog, dtb_, nwv = ba_ref[...], al_ref[...], dt_ref[...], nw_ref[...]
        v = _heads(v_ref)
        f = _gdn_chunk(_heads(q_ref), _heads(k_ref), v, ba, alog, dtb_, S, saved=(Tsaved, _heads(vn_ref), _heads(o_ref)))
        beta, kn, qn, kb, vb, Tinv, kbg = f["beta"], f["kn"], f["qn"], f["kb"], f["vb"], f["Tinv"], f["kbg"]
        eG, eGl, cd, Di, Ds, QK, vn, w_, qg, kg = (f["eG"], f["eGl"], f["cd"], f["Di"], f["Ds"], f["QK"], f["vn"],
                                                    f["w"], f["qg"], f["kg"])
        lane, ri, ci, rowc = f["lane"], f["ri"], f["ci"], f["rowc"]
        o = f["o"]
        z = _heads(z_ref)
        dy = _heads(dy_ref)
        r = lax.rsqrt(_rowsum(o * o) * (1.0 / HEAD_DIM) + EPS)
        nrm = o * r
        sz = _silu(z)
        dn = dy * nwv * sz
        put(dz_ref, dy * nrm * nwv * _dsilu(z))
        dnw_ref[...] += jnp.sum(jnp.sum(dy * nrm * sz, axis=0), axis=0, keepdims=True)
        do = r * (dn - nrm * (_rowsum(dn * nrm) * (1.0 / HEAD_DIM)))
        dcd = jnp.sum(jnp.sum(S * dS1, axis=2, keepdims=True), axis=1, keepdims=True)
        dkg = _dot(vn, dS1, _NT3)
        dvn = _dot(kg, dS1, _NN3) + _dot(QK, do, _TN3)
        dqg = _dot(do, S, _NT3)
        dQK = _dot(do, vn, _NT3)
        dw = -_dot(dvn, S, _NT3)
        dS0 = cd * dS1 + _dot(qg, do, _TN3) - _dot(w_, dvn, _TN3)
        dqn = dqg * eG
        dkn = dkg * eGl
        deGl = rs(dkg * kn)
        dQKr = dQK * Di
        E = dQK * QK
        dqn = dqn + _dot(dQKr, kn, _NN3)
        dkn = dkn + _dot(dQKr, qn, _TN3)
        dT = _dot(dvn, vb, _NT3) + _dot(dw, kbg, _NT3)
        dvb = _dot(Tinv, dvn, _TN3)
        dkbg = _dot(Tinv, dw, _TN3)
        dkb = dkbg * eG
        deG = rs(dqg * qn + dkbg * kb)
        dL = -_dot(_dot(Tinv, dT, _TN3), Tinv, _NT3)
        dKK = dL * Ds
        E = E + dL * f["Lm"]
        dkb = dkb + _dot(dKK, kn, _NN3)
        dkn = dkn + _dot(dKK, kb, _TN3) + dkb * beta
        dbeta = rs(dkb * kn + dvb * v)
        put(dqkv_ref, dvb * beta, 2 * H * HEAD_DIM)
        dG = rs(E) - rs(jnp.swapaxes(E, 1, 2)) + deG * eG - deGl * eGl
        dGl = jnp.sum(deGl * eGl, axis=1, keepdims=True) + dcd * cd
        dG = dG + jnp.where(rowc == C - 1, dGl, 0.0)
        qh = f["qh"]
        put(dqkv_ref, (HEAD_DIM ** -0.5) * f["rq"] * (dqn - qh * _rowsum(dqn * qh)))
        put(dqkv_ref, f["rk"] * (dkn - kn * _rowsum(dkn * kn)), H * HEAD_DIM)
        db = dbeta * beta * (1.0 - beta)
        db_all = jnp.where(lane == 0, db[0], 0.0)
        dG_all = jnp.where(lane == H, dG[0], 0.0)
        for h in range(1, H):
            db_all = db_all + jnp.where(lane == h, db[h], 0.0)
            dG_all = dG_all + jnp.where(lane == H + h, dG[h], 0.0)
        triu = (ri <= ci).astype(f32)
        dg_all = _dot01(triu, dG_all, _NN)
        da_all = dg_all * (-f["ea"]) * _sigmoid(ba + dtb_)
        dba_ref[...] = (db_all + da_all).astype(bf16)
        ddt_ref[...] += jnp.sum(da_all, axis=0, keepdims=True)
        dal_ref[...] += jnp.sum(dg_all * f["g_all"], axis=0, keepdims=True)
        return dS0

    small = jax.ShapeDtypeStruct((1, 128), f32)
    return pl.pallas_call(
        body, grid=(NS,),
        in_specs=[wide(0), wide(1), wide(2), wide(3), one(4 * H), vec, vec, vec, st_spec(HEAD_DIM), st_spec(C),
                  wide(0), wide(0), wide(0)],
        out_specs=[pl.BlockSpec((K * C, 3 * H * HEAD_DIM), lambda n: (NS - 1 - n, 0)), wide(3), one(0), vec, vec, vec],
        out_shape=[jax.ShapeDtypeStruct((T, 3 * H * HEAD_DIM), f32), jax.ShapeDtypeStruct((T, ODD_PAD), bf16),
                   jax.ShapeDtypeStruct((T, 128), bf16), small, small, small],
        scratch_shapes=[pltpu.VMEM((H, HEAD_DIM, HEAD_DIM), f32)],
        compiler_params=_cparams(("arbitrary",)), name=name,
    )(qkv, qkv, qkv, p, p, alog, dtb, nw, sprev, tinv, vn_all, o_all, dy_all)


def _adamw(w, gs, m, v, name, layer=None, prev=None):
    R, Cc = w.shape[-2:]
    S = gs.shape[0]
    tr = R
    for cand in (256, 128, 64, 32, 16, 8):
        if R % cand == 0 and R > cand:
            tr = cand
            break
    c1 = 1.0 - ADAM_B1 ** ADAM_STEP
    c2 = 1.0 - ADAM_B2 ** ADAM_STEP

    def body(w_ref, g_ref, m_ref, v_ref, *rest):
        go_ref, d_ref, mo_ref, vo_ref = rest[-4:]
        g = g_ref[0].astype(f32)
        for s in range(1, S):
            g = g + g_ref[s].astype(f32)
        mn = ADAM_B1 * m_ref[...] + (1.0 - ADAM_B1) * g
        vn = ADAM_B2 * v_ref[...] + (1.0 - ADAM_B2) * (g * g)
        go_ref[...] = g
        mo_ref[...] = mn
        vo_ref[...] = vn
        d_ref[...] = -ADAM_LR * ((mn / c1) / (jnp.sqrt(vn / c2) + ADAM_EPS) + ADAM_WD * w_ref[...])

    if layer is None:
        blk = pl.BlockSpec((tr, Cc), lambda i: (i, 0))
    else:
        blk = pl.BlockSpec((None, tr, Cc), lambda i: (layer, i, 0))
    out = jax.ShapeDtypeStruct(w.shape, f32)
    carried = [] if prev is None else list(prev)
    return pl.pallas_call(
        body, grid=(R // tr,),
        in_specs=[blk, pl.BlockSpec((S, tr, Cc), lambda i: (0, i, 0)), blk, blk]
        + [pl.BlockSpec(memory_space=pl.ANY)] * len(carried),
        out_specs=[blk] * 4, out_shape=[out] * 4,
        input_output_aliases={4 + j: j for j in range(len(carried))},
        compiler_params=_cparams(("parallel",)), name=name,
    )(w, gs, m, v, *carried)


def _me():
    x, y, c = lax.axis_index("x"), lax.axis_index("y"), lax.axis_index("c")
    return x, y, c, 4 * x + 2 * y + c


def _peer(k):
    x, y, c, _ = _me()
    px = 1 - x if k & 4 else x
    py = 1 - y if k & 2 else y
    pc = 1 - c if k & 1 else c
    return (px, py, pc), 4 * px + 2 * py + pc


_HBM = pl.BlockSpec(memory_space=pltpu.HBM)
_SEM = pl.BlockSpec(memory_space=pltpu.SEMAPHORE)
_EFFECT = pltpu.SideEffectType.DATAFLOW_SIDE_EFFECTING


def _copy(src, land, ssem, rsem, k, blocked, landing_slot_of_peer):
    pid, pidx = _peer(k)
    slot = pidx if landing_slot_of_peer else _me()[3]
    return pltpu.make_async_remote_copy(src_ref=src.at[pidx] if blocked else src, dst_ref=land.at[slot],
                                        send_sem=ssem.at[k - 1], recv_sem=rsem.at[k - 1], device_id=pid, device_id_type=MESH)


def _send_start(srcs, blocked, name):
    n = len(srcs)
    lands = [lax.empty(a.shape if blocked else (N_DEV,) + a.shape, a.dtype) for a in srcs]

    def body(*refs):
        src, land, sems, token = refs[:n], refs[n:2 * n], refs[2 * n:4 * n], refs[-1]
        for i in range(n):
            for k in range(1, N_DEV):
                _copy(src[i], land[i], sems[2 * i], sems[2 * i + 1], k, blocked, False).start()
        token[...] = jnp.zeros_like(token)

    sem = pltpu.SemaphoreType.DMA((N_DEV - 1,))
    hbm = lambda a: pltpu.with_memory_space_constraint(a, pltpu.HBM)
    res = pl.pallas_call(
        body, name=name,
        out_shape=tuple([sem] * (2 * n)) + tuple(pltpu.HBM(a.shape, a.dtype) for a in srcs + lands)
        + (jax.ShapeDtypeStruct((8, 128), f32),),
        in_specs=(_HBM,) * (2 * n),
        out_specs=(_SEM,) * (2 * n) + (_HBM,) * (2 * n) + (pl.BlockSpec(memory_space=pltpu.VMEM),),
        input_output_aliases={j: 2 * n + j for j in range(2 * n)},
        compiler_params=pltpu.CompilerParams(has_side_effects=_EFFECT),
    )(*[hbm(a) for a in srcs], *[hbm(a) for a in lands])
    handles = [(res[2 * i], res[2 * i + 1], res[2 * n + i], res[3 * n + i]) for i in range(n)]
    return handles, res[-1]


def _send_wait(handle, blocked, after, name):
    ssem, rsem, src, land = handle

    def body(src_ref, land_ref, ssem_ref, rsem_ref, after_ref, src_out, land_out):
        for k in range(1, N_DEV):
            cp = _copy(src_ref, land_ref, ssem_ref, rsem_ref, k, blocked, True)
            cp.wait_send()
            cp.wait_recv()

    return pl.pallas_call(
        body, name=name, out_shape=(pltpu.HBM(src.shape, src.dtype), pltpu.HBM(land.shape, land.dtype)),
        in_specs=(_HBM, _HBM, _SEM, _SEM, pl.BlockSpec(memory_space=pl.ANY)), out_specs=(_HBM, _HBM),
        input_output_aliases={0: 0, 1: 1}, compiler_params=pltpu.CompilerParams(has_side_effects=_EFFECT),
    )(src, land, ssem, rsem, after)[1]


def _block_diag(w):
    nb, bs = w.shape[0], w.shape[1]
    eye = jnp.eye(nb, dtype=w.dtype)
    return (eye[:, None, :, None] * w[:, :, None, :]).reshape(nb * bs, nb * bs)


def _diag_blocks(d):
    return jnp.stack([d[g, s * 64:(s + 1) * 64, s * 64:(s + 1) * 64] for g in range(4) for s in range(2)])


_SQUARE_TILES = dict(tm=1024, tn=1024, tk=1024)


def _mlp_fwd(x, hm, wu, wd, tag, epilogue, extras, outs):
    (r,) = _matmul(hm, wu, "nn", outs=[bf16], epilogue=lambda acc: (jnp.maximum(acc, 0.0),), name=f"mlp_up_{tag}")
    res = _matmul(r, wd, "nn", outs=outs, extras=(x,) + tuple(extras), epilogue=epilogue, a_map=jnp.square,
                  name=f"mlp_down_{tag}", **_SQUARE_TILES)
    return res, (hm, r)


def _mlp_bwd(x, nw, wu, wd, saved, dxo, dxo_b, tag):
    hm, r = saved
    (du,) = _matmul(dxo_b, wd, "nt", outs=[bf16], extras=(r,), epilogue=lambda acc, rr: (acc * (2.0 * rr.astype(f32)),),
                    name=f"mlp_dact_{tag}")
    (dwd,) = _matmul(r, dxo_b, "tn", outs=[bf16], a_map=jnp.square, name=f"mlp_dwd_{tag}", **_SQUARE_TILES)
    (dwu,) = _matmul(hm, du, "tn", outs=[bf16], shard_cols=2, name=f"mlp_dwu_{tag}")
    dx, dx_b, dnw = _matmul(du, wu, "nt", outs=_RMS_BWD_OUTS, extras=(x, dxo, nw), epilogue=_rms_bwd_ep,
                            name=f"mlp_dh_{tag}", **_SQUARE_TILES)
    return dx, dx_b, dnw, dwu, dwd.reshape(N_DEV, D_FF // N_DEV, D_MODEL)


def _local_step(x, tgt, P, weight, sink):
    T = x.shape[0]
    cos, sin = _rope_tables(T)
    rtab = _ret_tables()
    row = lambda a: a.reshape(1, -1)
    mix_nw, mlp_nw = P["mixer_norm_w"], P["mlp_norm_w"]
    wr_bd, wi_bd = _block_diag(P["lru_w_r"]), _block_diag(P["lru_w_i"])
    lru_b, lru_br, lru_bi, lru_lam = row(P["lru_conv_b"]), row(P["lru_b_r"]), row(P["lru_b_i"]), row(P["lru_lambda"])
    pad16 = lambda a: jnp.pad(a.reshape(1, GDN_HEADS), ((0, 0), (GDN_HEADS, 128 - 2 * GDN_HEADS)))
    alog, dtb = pad16(P["gdn_a_log"]), pad16(P["gdn_dt_bias"])
    gnw = row(P["gdn_norm_w"])

    x0 = x
    h0 = _rms_fwd(x0, mix_nw[0:1], "rms_mix_0")
    w_ie = weight("w_in_even", h0)
    (pe,) = _matmul(h0, w_ie, "nn", outs=[f32], name="in_even")
    mix0, o_ret, s_ret = _ret_fwd(pe, cos, sin, rtab, "ret_fwd")
    w_lc = weight("lru_conv_w", pe)
    xc = _conv_fwd(pe, 4, w_lc, lru_b, False, "lru_conv_fwd")
    mix0, h_lru = _lru_fwd(xc, pe, 20, wr_bd, wi_bd, lru_br, lru_bi, lru_lam, mix0, "lru_fwd")
    w_oe = weight("w_out_even", mix0)
    x1, hm0 = _matmul(mix0, w_oe, "nn", outs=[f32, bf16], extras=(x0, mlp_nw[0:1]), epilogue=_residual_rms_ep,
                      name="out_even", tm=1024, tn=D_MODEL)
    w_u0, w_d0 = weight("w_up0", x1), weight("w_down0", x1)
    (x2, h1), mlp0 = _mlp_fwd(x1, hm0, w_u0, w_d0, "0", _residual_rms_ep, (mix_nw[1:2],), [f32, bf16])
    w_io = weight("w_in_odd", h1)
    (po,) = _matmul(h1, w_io, "nn", outs=[f32], tm=1024, tn=ODD_PAD // 3, name="in_odd")
    w_gc = weight("gdn_conv_w", po)
    qkv = _conv_fwd(po, 0, w_gc, None, True, "gdn_conv_fwd")
    y_gdn, s_gdn, ti_gdn, vn_gdn, o_gdn = _gdn_fwd(qkv, po, alog, dtb, gnw, "gdn_fwd")
    w_oo = weight("w_out_odd", y_gdn)
    x3, hm1 = _matmul(y_gdn, w_oo, "nn", outs=[f32, bf16], extras=(x2, mlp_nw[1:2]), epilogue=_residual_rms_ep,
                      name="out_odd", tm=1024, tn=D_MODEL)
    w_u1, w_d1 = weight("w_up1", x3), weight("w_down1", x3)
    (loss, dx4, dx4_b, d_final), mlp1 = _mlp_fwd(x3, hm1, w_u1, w_d1, "1", _loss_ep, (row(P["final_norm_w"]), tgt),
                                                 _LOSS_OUTS)
    dx3, dx3_b, d_mlp_nw1, d_wu1, d_wd1 = _mlp_bwd(x3, mlp_nw[1:2], w_u1, w_d1, mlp1, dx4, dx4_b, "1")
    tok = sink(dict(w_up1=d_wu1, w_down1=d_wd1))
    (dy_gdn,) = _matmul(dx3_b, w_oo, "nt", outs=[f32], name="out_odd_dx")
    (d_woo,) = _matmul(y_gdn, dx3_b, "tn", outs=[bf16], name="out_odd_dw")
    dqkv, dpo, dba, d_alog, d_dtb, d_gnw = _gdn_bwd(qkv, po, alog, dtb, gnw + tok[0:1, :], s_gdn, ti_gdn, vn_gdn, o_gdn, dy_gdn,
                                                  "gdn_bwd")
    dpo, d_gconv, _ = _conv_bwd(po, 0, w_gc, None, True, dqkv, dpo, "gdn_conv_bwd")
    dpo = lax.dynamic_update_slice(dpo, dba, (0, 4 * D_MODEL))
    (d_wio,) = _matmul(h1, dpo, "tn", outs=[bf16], tn=ODD_PAD // 3, name="in_odd_dw")
    n_odd = ODD_IN // N_DEV
    tok = sink(dict(w_out_odd=d_woo.reshape(N_DEV, D_MODEL // N_DEV, D_MODEL),
                    w_in_odd=jnp.transpose(d_wio[:, :ODD_IN].reshape(D_MODEL, N_DEV, n_odd), (1, 0, 2))))
    dx2, dx2_b, d_mix_nw1 = _matmul(dpo, w_io, "nt", outs=_RMS_BWD_OUTS, extras=(x2, dx3, mix_nw[1:2] + tok[0:1, 0:1]),
                                    epilogue=_rms_bwd_ep, tm=1024, tn=1024, tk=ODD_PAD // 3, name="in_odd_dx")
    dx1, dx1_b, d_mlp_nw0, d_wu0, d_wd0 = _mlp_bwd(x1, mlp_nw[0:1], w_u0, w_d0, mlp0, dx2, dx2_b, "0")
    (d_woe,) = _matmul(mix0, dx1_b, "tn", outs=[bf16], name="out_even_dw")
    tok = sink(dict(w_up0=d_wu0, w_down0=d_wd0, w_out_even=d_woe.reshape(N_DEV, D_MODEL // N_DEV, D_MODEL)))
    (dmix0,) = _matmul(dx1_b, w_oe, "nt", outs=[f32], name="out_even_dx")
    dpe = _ret_bwd(pe, cos, sin, rtab, o_ret, s_ret, dmix0, "ret_bwd")
    dxc, dpe, d_wr, d_wi, d_br, d_bi, d_lam = _lru_bwd(xc, pe, 20, wr_bd, wi_bd, lru_br, lru_bi, lru_lam + tok[0:1, 0:1],
                                                       h_lru, dmix0, 4, dpe, "lru_bwd")
    dpe, d_lconv, d_lconv_b = _conv_bwd(pe, 4, w_lc, lru_b, False, dxc, dpe, "lru_conv_bwd")
    (d_wie,) = _matmul(h0, dpe, "tn", outs=[bf16], shard_cols=2, name="in_even_dw")
    dx0, _, d_mix_nw0 = _matmul(dpe, w_ie, "nt", outs=_RMS_BWD_OUTS, extras=(x0, dx1, mix_nw[0:1]), epilogue=_rms_bwd_ep,
                                name="in_even_dx", **_SQUARE_TILES)

    G = dict(
        mixer_norm_w=jnp.concatenate([d_mix_nw0, d_mix_nw1], axis=0),
        mlp_norm_w=jnp.concatenate([d_mlp_nw0, d_mlp_nw1], axis=0),
        final_norm_w=d_final.reshape(-1),
        w_in_even=d_wie, lru_conv_w=d_lconv, lru_conv_b=d_lconv_b.reshape(-1),
        lru_w_r=_diag_blocks(d_wr), lru_b_r=d_br.reshape(-1), lru_w_i=_diag_blocks(d_wi), lru_b_i=d_bi.reshape(-1),
        lru_lambda=d_lam.reshape(-1), gdn_conv_w=d_gconv,
        gdn_a_log=d_alog[0, GDN_HEADS:2 * GDN_HEADS], gdn_dt_bias=d_dtb[0, GDN_HEADS:2 * GDN_HEADS],
        gdn_norm_w=d_gnw.reshape(-1),
    )
    return loss, dx0, G


_SMALL = ["mixer_norm_w", "mlp_norm_w", "final_norm_w", "lru_conv_b", "lru_w_r", "lru_b_r", "lru_w_i", "lru_b_i",
          "lru_lambda", "gdn_a_log", "gdn_dt_bias", "gdn_norm_w"]
_PACK_ROWS = 688


def _pack(parts):
    flat = jnp.concatenate([p.reshape(-1) for p in parts])
    return jnp.pad(flat, (0, _PACK_ROWS * 128 - flat.shape[0])).reshape(_PACK_ROWS, 128)


def _unpack(packed, shapes):
    flat = packed.reshape(-1)
    out, off = [], 0
    for s in shapes:
        n = int(np.prod(s))
        out.append(flat[off:off + n].reshape(s))
        off += n
    return out


def kernel(x, mixer_norm_w, mlp_norm_w, final_norm_w, w_in_even, lru_conv_w, lru_conv_b, lru_w_r, lru_b_r, lru_w_i, lru_b_i, lru_lambda, w_out_even, w_in_odd, gdn_conv_w, gdn_a_log, gdn_dt_bias, gdn_norm_w, w_out_odd, w_up, w_down, loss_target, m_mixer_norm_w, m_mlp_norm_w, m_final_norm_w, m_w_in_even, m_lru_conv_w, m_lru_conv_b, m_lru_w_r, m_lru_b_r, m_lru_w_i, m_lru_b_i, m_lru_lambda, m_w_out_even, m_w_in_odd, m_gdn_conv_w, m_gdn_a_log, m_gdn_dt_bias, m_gdn_norm_w, m_w_out_odd, m_w_up, m_w_down, v_mixer_norm_w, v_mlp_norm_w, v_final_norm_w, v_w_in_even, v_lru_conv_w, v_lru_conv_b, v_lru_w_r, v_lru_b_r, v_lru_w_i, v_lru_b_i, v_lru_lambda, v_w_out_even, v_w_in_odd, v_gdn_conv_w, v_gdn_a_log, v_gdn_dt_bias, v_gdn_norm_w, v_w_out_odd, v_w_up, v_w_down):
    Pw = dict(mixer_norm_w=mixer_norm_w, mlp_norm_w=mlp_norm_w, final_norm_w=final_norm_w, w_in_even=w_in_even,
              lru_conv_w=lru_conv_w, lru_conv_b=lru_conv_b, lru_w_r=lru_w_r, lru_b_r=lru_b_r, lru_w_i=lru_w_i,
              lru_b_i=lru_b_i, lru_lambda=lru_lambda, w_out_even=w_out_even, w_in_odd=w_in_odd, gdn_conv_w=gdn_conv_w,
              gdn_a_log=gdn_a_log, gdn_dt_bias=gdn_dt_bias, gdn_norm_w=gdn_norm_w, w_out_odd=w_out_odd, w_up=w_up,
              w_down=w_down)
    Pm = dict(mixer_norm_w=m_mixer_norm_w, mlp_norm_w=m_mlp_norm_w, final_norm_w=m_final_norm_w, w_in_even=m_w_in_even,
              lru_conv_w=m_lru_conv_w, lru_conv_b=m_lru_conv_b, lru_w_r=m_lru_w_r, lru_b_r=m_lru_b_r, lru_w_i=m_lru_w_i,
              lru_b_i=m_lru_b_i, lru_lambda=m_lru_lambda, w_out_even=m_w_out_even, w_in_odd=m_w_in_odd,
              gdn_conv_w=m_gdn_conv_w, gdn_a_log=m_gdn_a_log, gdn_dt_bias=m_gdn_dt_bias, gdn_norm_w=m_gdn_norm_w,
              w_out_odd=m_w_out_odd, w_up=m_w_up, w_down=m_w_down)
    Pv = dict(mixer_norm_w=v_mixer_norm_w, mlp_norm_w=v_mlp_norm_w, final_norm_w=v_final_norm_w, w_in_even=v_w_in_even,
              lru_conv_w=v_lru_conv_w, lru_conv_b=v_lru_conv_b, lru_w_r=v_lru_w_r, lru_b_r=v_lru_b_r, lru_w_i=v_lru_w_i,
              lru_b_i=v_lru_b_i, lru_lambda=v_lru_lambda, w_out_even=v_w_out_even, w_in_odd=v_w_in_odd,
              gdn_conv_w=v_gdn_conv_w, gdn_a_log=v_gdn_a_log, gdn_dt_bias=v_gdn_dt_bias, gdn_norm_w=v_gdn_norm_w,
              w_out_odd=v_w_out_odd, w_up=v_w_up, w_down=v_w_down)
    me = _me()[3]
    T = x.shape[1]

    cols = lambda g: jnp.transpose(g, (1, 0, 2)).reshape(g.shape[1], -1)
    rows = lambda g: g.reshape(-1, g.shape[2])
    wide = lambda g: jnp.pad(cols(g), ((0, 0), (0, ODD_PAD - ODD_IN)))
    as_is = lambda g: g
    gather = dict(
        w_in_even=(w_in_even[0].astype(bf16), cols), lru_conv_w=(lru_conv_w[0], cols),
        w_out_even=(w_out_even[0].astype(bf16), rows), w_up0=(w_up[0].astype(bf16), as_is), w_down0=(w_down[0].astype(bf16), rows),
        w_in_odd=(w_in_odd[0].astype(bf16), wide), gdn_conv_w=(gdn_conv_w[0], cols),
        w_out_odd=(w_out_odd[0].astype(bf16), rows), w_up1=(w_up[1].astype(bf16), as_is), w_down1=(w_down[1].astype(bf16), rows))
    handles, tok = _send_start([s for s, _ in gather.values()], False, "gather_start")
    handles = dict(zip(gather, handles))
    full = {}

    def weight(name, after):
        if name not in full:
            landed = _send_wait(handles[name], False, after, f"gather_wait_{name}")
            shard, finish = gather[name]
            full[name] = finish(lax.dynamic_update_slice_in_dim(landed, shard[None], me, 0))
        return full[name]

    P = {k: Pw[k] for k in ("mlp_norm_w", "final_norm_w")}
    P["mixer_norm_w"] = mixer_norm_w + tok[0:1, 0:1]
    for k in ("lru_w_r", "lru_w_i", "lru_conv_b", "lru_b_r", "lru_b_i", "lru_lambda", "gdn_a_log", "gdn_dt_bias", "gdn_norm_w"):
        P[k] = Pw[k][0]

    sent = {}

    def sink(grads):
        hs, token = _send_start(list(grads.values()), True, "grads_start_" + "_".join(grads))
        for (name, g), h in zip(grads.items(), hs):
            sent[name] = (h, g)
        return token

    loss, dx, G = _local_step(x[0], loss_target[0], P, weight, sink)
    small_g = [G[k].reshape(Pw[k].shape) for k in _SMALL] + [G["lru_conv_w"], G["gdn_conv_w"]]
    packed = _pack(small_g)
    sink(dict(w_in_even=G["w_in_even"], small=jnp.broadcast_to(packed[None], (N_DEV,) + packed.shape)))

    def received(name, after=dx):
        h, g = sent[name]
        landed = _send_wait(h, True, after, f"grads_wait_{name}")
        return lax.dynamic_update_slice_in_dim(landed, lax.dynamic_slice_in_dim(g, me, 1, 0), me, 0)

    out = {}
    nff = D_FF // N_DEV

    def whole(name, gs, shape2d):
        res = _adamw(Pw[name].reshape(shape2d), gs, Pm[name].reshape(shape2d), Pv[name].reshape(shape2d), f"adamw_{name}")
        out[name] = tuple(a.reshape(Pw[name].shape) for a in res)

    def layers(name):
        res = None
        for l in range(2):
            res = _adamw(Pw[name], received(f"{name}{l}"), Pm[name], Pv[name], f"adamw_{name}{l}", layer=l, prev=res)
        out[name] = tuple(res)

    layers("w_up")
    layers("w_down")
    whole("w_out_odd", received("w_out_odd"), (128, D_MODEL))
    whole("w_in_odd", received("w_in_odd"), (D_MODEL, 514))
    whole("w_out_even", received("w_out_even"), (128, D_MODEL))
    whole("w_in_even", received("w_in_even", out["w_out_even"][1]), (D_MODEL, 384))
    small_shapes = [Pw[k].shape for k in _SMALL]
    pw, pm, pv = (_pack([Q[k] for k in _SMALL]) for Q in (Pw, Pm, Pv))
    sg, sd, sm, sv = _adamw(pw, received("small", out["w_in_even"][1]), pm, pv, "adamw_small")
    for arrs_i, packed_out in enumerate((sg, sd, sm, sv)):
        for k, a in zip(_SMALL, _unpack(packed_out, small_shapes)):
            out.setdefault(k, [None] * 4)[arrs_i] = a
    n_small = sum(int(np.prod(s)) for s in small_shapes)
    gflat = sg.reshape(-1)
    g_lconv = gflat[n_small:n_small + CONV_K * LRU_WIDTH].reshape(CONV_K, LRU_WIDTH)
    g_gconv = gflat[n_small + CONV_K * LRU_WIDTH:n_small + CONV_K * (LRU_WIDTH + 3072)].reshape(CONV_K, 3072)
    whole("lru_conv_w", lax.dynamic_slice_in_dim(g_lconv, me * 64, 64, axis=1)[None], (CONV_K, 64))
    whole("gdn_conv_w", lax.dynamic_slice_in_dim(g_gconv, me * 384, 384, axis=1)[None], (CONV_K, 384))

    names = ["mixer_norm_w", "mlp_norm_w", "final_norm_w", "w_in_even", "lru_conv_w", "lru_conv_b", "lru_w_r", "lru_b_r",
             "lru_w_i", "lru_b_i", "lru_lambda", "w_out_even", "w_in_odd", "gdn_conv_w", "gdn_a_log", "gdn_dt_bias",
             "gdn_norm_w", "w_out_odd", "w_up", "w_down"]
    total = lax.psum(loss[0, 0], ("x", "y", "c"))
    res = [total, dx[None]]
    for j in range(4):
        res += [out[k][j] for k in names]
    return tuple(res)
```

```python
import math

import numpy as np
import jax
import jax.numpy as jnp
from jax import lax
from jax.experimental import pallas as pl
from jax.experimental.pallas import tpu as pltpu

f32 = jnp.float32
bf16 = jnp.bfloat16

N_DEV = 8
D_MODEL = 1024
D_FF = 4096
EPS = 1e-6
RET_HEADS = 4
RET_CHUNK = 128
RET_STEP = 4
ROPE_THETA = 10000.0
LRU_WIDTH = 512
LRU_C = 8.0
GDN_HEADS = 8
GDN_CHUNK = 64
GDN_STEP = 2
HEAD_DIM = 128
ODD_IN = 4112
ODD_PAD = 4224
ODD_SHARD = ODD_IN // N_DEV
ODD_SHARD_PAD = 640
ADAM_LR, ADAM_B1, ADAM_B2, ADAM_EPS, ADAM_WD, ADAM_STEP = 0.001, 0.9, 0.999, 1e-08, 0.01, 10
VMEM_LIMIT = 56 * 1024 * 1024

_NN = (((1,), (0,)), ((), ()))
_NT = (((1,), (1,)), ((), ()))
_TN = (((0,), (0,)), ((), ()))
MESH = pl.DeviceIdType.MESH


def _cparams(sem):
    return pltpu.CompilerParams(dimension_semantics=sem, vmem_limit_bytes=VMEM_LIMIT)


def _dot(a, b, dn):
    return lax.dot_general(a.astype(bf16), b.astype(bf16), dn, preferred_element_type=f32)


def _dot01(a01, b, dn):
    a = a01.astype(bf16)
    b0 = b.astype(bf16)
    r1 = b - b0.astype(f32)
    b1 = r1.astype(bf16)
    b2 = (r1 - b1.astype(f32)).astype(bf16)
    d = lambda q: lax.dot_general(a, q, dn, preferred_element_type=f32)
    return d(b0) + (d(b1) + d(b2))


def _sigmoid(x):
    return jax.nn.sigmoid(x)


def _silu(x):
    return x * _sigmoid(x)


def _dsilu(x):
    s = _sigmoid(x)
    return s * (1.0 + x * (1.0 - s))


def _softplus(x):
    return jnp.maximum(x, 0.0) + jnp.log1p(jnp.exp(-jnp.abs(x)))


_GELU_C = math.sqrt(2.0 / math.pi)


def _gelu(y):
    return 0.5 * y * (1.0 + jnp.tanh(_GELU_C * (y + 0.044715 * y * y * y)))


def _dgelu(y):
    t = jnp.tanh(_GELU_C * (y + 0.044715 * y * y * y))
    return 0.5 * (1.0 + t) + 0.5 * y * (1.0 - t * t) * _GELU_C * (1.0 + 3.0 * 0.044715 * y * y)


def _matmul(a, b, form, *, outs, name, epilogue=None, extras=(), tm=2048, tn=512, tk=1024, shard_cols=0, a_map=None):
    if form == "tn":
        K, M = a.shape
    else:
        M, K = a.shape
    if b.ndim == 3:
        assert form in ("nn", "nt"), name
        N = b.shape[1] if form == "nt" else N_DEV * b.shape[2]
        if form == "nn":
            tn = b.shape[2]
        else:
            tk = b.shape[2]
    else:
        N = b.shape[0] if form == "nt" else b.shape[1]
    ns = N // N_DEV
    if shard_cols:
        tn = ns * shard_cols
    tm, tn, tk = min(tm, M), min(tn, N), min(tk, K)
    assert M % tm == 0 and N % tn == 0 and K % tk == 0, (name, M, N, K, tm, tn, tk)
    nk = K // tk
    dn = {"nn": _NN, "nt": _NT, "tn": _TN}[form]
    if form == "tn":
        a_spec = pl.BlockSpec((tk, tm), lambda i, j, k: (k, i))
    else:
        a_spec = pl.BlockSpec((tm, tk), lambda i, j, k: (i, k))
    if b.ndim == 3:
        b_spec = (pl.BlockSpec((None, tn, tk), lambda i, j, k: (k, j, 0)) if form == "nt"
                  else pl.BlockSpec((None, tk, tn), lambda i, j, k: (j, k, 0)))
    elif form == "nt":
        b_spec = pl.BlockSpec((tn, tk), lambda i, j, k: (j, k))
    else:
        b_spec = pl.BlockSpec((tk, tn), lambda i, j, k: (k, j))
    e_spec = pl.BlockSpec((tm, tn), lambda i, j, k: (i, j))
    v_spec = pl.BlockSpec((1, tn), lambda i, j, k: (0, j))
    if shard_cols:
        o_spec = pl.BlockSpec((shard_cols, tm, ns), lambda i, j, k: (j, i, 0))
        o_shape = (N_DEV, M, ns)
    else:
        o_spec = e_spec
        o_shape = (M, N)
    n_ex = len(extras)
    sums = [isinstance(o, tuple) for o in outs]
    assert not any(sums) or tn == N, name

    def finish(acc, ex, o_refs, row_tile):
        vals = (acc,) if epilogue is None else epilogue(acc, *[e[...] for e in ex])
        for r, v, is_sum in zip(o_refs, vals, sums):
            if is_sum:
                @pl.when(row_tile == 0)
                def _(r=r, v=v):
                    r[...] = v.astype(r.dtype)

                @pl.when(row_tile > 0)
                def _(r=r, v=v):
                    r[...] += v.astype(r.dtype)
            elif shard_cols:
                for s in range(shard_cols):
                    r[s] = v[:, s * ns:(s + 1) * ns].astype(r.dtype)
            else:
                r[...] = v.astype(r.dtype)

    def prod(a_ref, b_ref):
        av = a_ref[...]
        return _dot(av if a_map is None else a_map(av), b_ref[...], dn)

    def body_one(*refs):
        finish(prod(*refs[:2]), refs[2:2 + n_ex], refs[2 + n_ex:], pl.program_id(0))

    def body_acc(*refs):
        a_ref, b_ref = refs[:2]
        acc = refs[-1]
        k = pl.program_id(2)
        row_tile = pl.program_id(0)

        @pl.when(k == 0)
        def _():
            acc[...] = prod(a_ref, b_ref)

        @pl.when((k > 0) & (k < nk - 1))
        def _():
            acc[...] += prod(a_ref, b_ref)

        @pl.when(k == nk - 1)
        def _():
            finish(acc[...] + prod(a_ref, b_ref), refs[2:2 + n_ex], refs[2 + n_ex:-1], row_tile)

    return pl.pallas_call(
        body_one if nk == 1 else body_acc, grid=(M // tm, N // tn, nk),
        in_specs=[a_spec, b_spec] + [v_spec if e.shape[0] == 1 else e_spec for e in extras],
        out_specs=[v_spec if s else o_spec for s in sums],
        out_shape=[jax.ShapeDtypeStruct((1, N), o[1]) if s else jax.ShapeDtypeStruct(o_shape, o) for o, s in zip(outs, sums)],
        scratch_shapes=[] if nk == 1 else [pltpu.VMEM((tm, tn), f32)],
        compiler_params=_cparams(("arbitrary" if any(sums) else "parallel", "parallel", "arbitrary")), name=name,
    )(a, b, *extras)


def _rms_fwd(x, w, name):
    T, D = x.shape
    tt = min(512, T)

    def body(x_ref, w_ref, h_ref):
        xv = x_ref[...]
        r = lax.rsqrt(jnp.mean(xv * xv, axis=1, keepdims=True) + EPS)
        h_ref[...] = (xv * r * w_ref[...]).astype(bf16)

    return pl.pallas_call(
        body, grid=(T // tt,),
        in_specs=[pl.BlockSpec((tt, D), lambda i: (i, 0)), pl.BlockSpec((1, D), lambda i: (0, 0))],
        out_specs=pl.BlockSpec((tt, D), lambda i: (i, 0)),
        out_shape=jax.ShapeDtypeStruct((T, D), bf16),
        compiler_params=_cparams(("parallel",)), name=name,
    )(x, w)


def _residual_rms_ep(acc, res, w):
    x = res + acc
    r = lax.rsqrt(jnp.mean(x * x, axis=1, keepdims=True) + EPS)
    return x, x * r * w


_RMS_BWD_OUTS = [f32, bf16, ("sum", f32)]


def _rms_bwd_ep(dh, x, dres, w):
    r = lax.rsqrt(jnp.mean(x * x, axis=1, keepdims=True) + EPS)
    xn = x * r
    dhw = dh * w
    dx = dres + r * (dhw - xn * jnp.mean(dhw * xn, axis=1, keepdims=True))
    return dx, dx, jnp.sum(dh * xn, axis=0, keepdims=True)


_LOSS_OUTS = [("sum", f32), f32, bf16, ("sum", f32)]


def _loss_ep(acc, res, w, tgt):
    x = res + acc
    D = x.shape[1]
    r = lax.rsqrt(jnp.mean(x * x, axis=1, keepdims=True) + EPS)
    xn = x * r
    e = xn * w - tgt
    loss = 0.5 * jnp.sum(jnp.mean(e * e, axis=1, keepdims=True), axis=0, keepdims=True)
    dy = e * (1.0 / D)
    dyw = dy * w
    dx = r * (dyw - xn * jnp.mean(dyw * xn, axis=1, keepdims=True))
    return jnp.broadcast_to(loss, (1, D)), dx, dx, jnp.sum(dy * xn, axis=0, keepdims=True)


def _ret_tables():
    H, C = RET_HEADS, RET_CHUNK
    lg = np.log1p(-np.exp2(-5.0 - np.arange(H, dtype=np.float32))).astype(np.float32)
    idx = np.arange(C, dtype=np.float32)
    diff = idx[:, None] - idx[None, :]
    causal = diff >= 0
    dm = np.where(causal[None], np.exp(lg[:, None, None] * np.where(causal, diff, 0.0)[None]), 0.0)
    qd = np.exp(lg[:, None] * (idx[None, :] + 1.0))
    kd = np.exp(lg[:, None] * (C - 1.0 - idx[None, :]))
    cg = np.exp(lg * C)
    tab = np.zeros((H, 4, C, HEAD_DIM), np.float32)
    tab[:, 0] = dm
    tab[:, 1] = qd[:, :, None]
    tab[:, 2] = kd[:, :, None]
    tab[:, 3] = cg[:, None, None]
    return jnp.asarray(tab)


def _rope_tables(T):
    half = HEAD_DIM // 2
    inv = ROPE_THETA ** (-jnp.arange(half, dtype=f32) / half)
    ang = jnp.arange(T, dtype=jnp.int32).astype(f32)[:, None] * inv[None, :]
    c, s = jnp.cos(ang), jnp.sin(ang)
    return jnp.concatenate([c, c], axis=1), jnp.concatenate([-s, s], axis=1)


def _rope(x, cos, sin):
    return x * cos + pltpu.roll(x, HEAD_DIM // 2, 1) * sin


def _unrope(y, cos, sin):
    return y * cos + pltpu.roll(y * sin, HEAD_DIM // 2, 1)


def _stack_heads(ref, H, f=None):
    parts = [ref[:, h * HEAD_DIM:(h + 1) * HEAD_DIM] for h in range(H)]
    return jnp.stack(parts if f is None else [f(a) for a in parts])


def _ret_fwd(p, cos, sin, tab, name):
    T = p.shape[0]
    C, H = RET_CHUNK, RET_HEADS
    N = T // C
    K = min(RET_STEP, N)
    NS = N // K
    scale = HEAD_DIM ** -0.5

    def body(q_ref, k_ref, v_ref, g_ref, c_ref, s_ref, t_ref, y_ref, o_ref, sp_ref, st):
        @pl.when(pl.program_id(0) == 0)
        def _():
            st[...] = jnp.zeros_like(st)

        dm, qd, kd, cg = t_ref[:, 0], t_ref[:, 1], t_ref[:, 2], t_ref[:, 3]
        S = st[...]
        for c in range(K):
            rows = pl.ds(c * C, C)
            cos_, sin_ = c_ref[rows, :], s_ref[rows, :]
            rot = lambda a: _rope(a, cos_, sin_)
            q = _stack_heads(q_ref.at[rows, :], H, rot)
            k = _stack_heads(k_ref.at[rows, :], H, rot) * scale
            v = _stack_heads(v_ref.at[rows, :], H)
            P = _dot(q, k, _NT3) * dm
            o = _dot(P, v, _NN3) + _dot(q * qd, S, _NN3)
            sp_ref[c] = S
            S = cg * S + _dot(k * kd, v, _TN3)
            r = lax.rsqrt(jnp.mean(o * o, axis=2, keepdims=True) + EPS)
            y = o * r * _silu(_stack_heads(g_ref.at[rows, :], H))
            for h in range(H):
                o_ref[rows, h * HEAD_DIM:(h + 1) * HEAD_DIM] = o[h]
                y_ref[rows, h * HEAD_DIM:(h + 1) * HEAD_DIM] = y[h].astype(bf16)
        st[...] = S

    wide = lambda blk: pl.BlockSpec((K * C, H * HEAD_DIM), lambda n: (n, blk))
    tbl = pl.BlockSpec((K * C, HEAD_DIM), lambda n: (n, 0))
    return pl.pallas_call(
        body, grid=(NS,),
        in_specs=[wide(0), wide(1), wide(2), wide(3), tbl, tbl,
                  pl.BlockSpec((H, 4, C, HEAD_DIM), lambda n: (0, 0, 0, 0))],
        out_specs=[wide(0), wide(0), pl.BlockSpec((K, H, HEAD_DIM, HEAD_DIM), lambda n: (n, 0, 0, 0))],
        out_shape=[jax.ShapeDtypeStruct((T, D_MODEL), bf16), jax.ShapeDtypeStruct((T, H * HEAD_DIM), f32),
                   jax.ShapeDtypeStruct((N, H, HEAD_DIM, HEAD_DIM), f32)],
        scratch_shapes=[pltpu.VMEM((H, HEAD_DIM, HEAD_DIM), f32)],
        compiler_params=_cparams(("arbitrary",)), name=name,
    )(p, p, p, p, cos, sin, tab)


def _ret_bwd(p, cos, sin, tab, o_raw, sprev, dmix, name):
    T = p.shape[0]
    C, H = RET_CHUNK, RET_HEADS
    N = T // C
    K = min(RET_STEP, N)
    NS = N // K
    scale = HEAD_DIM ** -0.5
    W = H * HEAD_DIM

    def body(q_ref, k_ref, v_ref, g_ref, c_ref, s_ref, t_ref, o_ref, sp_ref, dy_ref, d_ref, dst):
        @pl.when(pl.program_id(0) == 0)
        def _():
            dst[...] = jnp.zeros_like(dst)

        dm, qd, kd, cg = t_ref[:, 0], t_ref[:, 1], t_ref[:, 2], t_ref[:, 3]
        dS1 = dst[...]
        for c in reversed(range(K)):
            rows = pl.ds(c * C, C)
            cos_, sin_ = c_ref[rows, :], s_ref[rows, :]
            rot = lambda a: _rope(a, cos_, sin_)
            q = _stack_heads(q_ref.at[rows, :], H, rot)
            k = _stack_heads(k_ref.at[rows, :], H, rot) * scale
            v = _stack_heads(v_ref.at[rows, :], H)
            g = _stack_heads(g_ref.at[rows, :], H)
            S = sp_ref[c]
            o = _stack_heads(o_ref.at[rows, :], H)
            dy = _stack_heads(dy_ref.at[rows, :], H)
            r = lax.rsqrt(jnp.mean(o * o, axis=2, keepdims=True) + EPS)
            nrm = o * r
            dn = dy * _silu(g)
            dg = dy * nrm * _dsilu(g)
            do = r * (dn - nrm * jnp.mean(dn * nrm, axis=2, keepdims=True))
            P = _dot(q, k, _NT3) * dm
            dP = _dot(do, v, _NT3) * dm
            dq = _dot(dP, k, _NN3) + _dot(do, S, _NT3) * qd
            dk = (_dot(dP, q, _TN3) + _dot(v, dS1, _NT3) * kd) * scale
            dv = _dot(P, do, _TN3) + _dot(k * kd, dS1, _NN3)
            dS1 = cg * dS1 + _dot(q * qd, do, _TN3)
            for h in range(H):
                d_ref[rows, h * HEAD_DIM:(h + 1) * HEAD_DIM] = _unrope(dq[h], cos_, sin_).astype(bf16)
                d_ref[rows, W + h * HEAD_DIM:W + (h + 1) * HEAD_DIM] = _unrope(dk[h], cos_, sin_).astype(bf16)
                d_ref[rows, 2 * W + h * HEAD_DIM:2 * W + (h + 1) * HEAD_DIM] = dv[h].astype(bf16)
                d_ref[rows, 3 * W + h * HEAD_DIM:3 * W + (h + 1) * HEAD_DIM] = dg[h].astype(bf16)
        dst[...] = dS1

    rev = lambda blk: pl.BlockSpec((K * C, W), lambda n: (NS - 1 - n, blk))
    tbl = pl.BlockSpec((K * C, HEAD_DIM), lambda n: (NS - 1 - n, 0))
    return pl.pallas_call(
        body, grid=(NS,),
        in_specs=[rev(0), rev(1), rev(2), rev(3), tbl, tbl,
                  pl.BlockSpec((H, 4, C, HEAD_DIM), lambda n: (0, 0, 0, 0)), rev(0),
                  pl.BlockSpec((K, H, HEAD_DIM, HEAD_DIM), lambda n: (NS - 1 - n, 0, 0, 0)), rev(0)],
        out_specs=pl.BlockSpec((K * C, 4 * W), lambda n: (NS - 1 - n, 0)),
        out_shape=jax.ShapeDtypeStruct((T, 6 * W), bf16),
        scratch_shapes=[pltpu.VMEM((H, HEAD_DIM, HEAD_DIM), f32)],
        compiler_params=_cparams(("arbitrary",)), name=name,
    )(p, p, p, p, cos, sin, tab, o_raw, sprev, dmix)


CONV_K = 4
CONV_W = 512
PAD = 8
SUB_R = 64


def _conv_fwd(x, col_off, w, b, act, name):
    T = x.shape[0]
    C = w.shape[1]
    G = C // CONV_W
    tt = min(512, T)
    NT = T // tt
    has_b = b is not None

    def body(*refs):
        if has_b:
            x_ref, w_ref, b_ref, y_ref, pad = refs
        else:
            x_ref, w_ref, y_ref, pad = refs
        t = pl.program_id(1)

        @pl.when(t == 0)
        def _():
            pad[pl.ds(0, PAD), :] = jnp.zeros((PAD, CONV_W), f32)

        pad[pl.ds(PAD, tt), :] = x_ref[...]
        for g in range(CONV_W // 128):
            ls = slice(g * 128, (g + 1) * 128)
            wv = w_ref[:, ls]
            for c in range(tt // SUB_R):
                r0 = c * SUB_R
                y = wv[0:1, :] * pad[pl.ds(PAD - 3 + r0, SUB_R), ls]
                for kk in range(1, CONV_K):
                    y = y + wv[kk:kk + 1, :] * pad[pl.ds(PAD - 3 + kk + r0, SUB_R), ls]
                if has_b:
                    y = y + b_ref[:, ls]
                y_ref[pl.ds(r0, SUB_R), ls] = _silu(y) if act else y
        tail = pad[pl.ds(tt, PAD), :]
        pad[pl.ds(0, PAD), :] = tail

    in_specs = [pl.BlockSpec((tt, CONV_W), lambda g, t: (t, col_off + g)),
                pl.BlockSpec((CONV_K, CONV_W), lambda g, t: (0, g))]
    args = [x, w]
    if has_b:
        in_specs.append(pl.BlockSpec((1, CONV_W), lambda g, t: (0, g)))
        args.append(b)
    return pl.pallas_call(
        body, grid=(G, NT), in_specs=in_specs,
        out_specs=pl.BlockSpec((tt, CONV_W), lambda g, t: (t, g)),
        out_shape=jax.ShapeDtypeStruct((T, C), f32),
        scratch_shapes=[pltpu.VMEM((tt + PAD, CONV_W), f32)],
        compiler_params=_cparams(("parallel", "arbitrary")), name=name,
    )(*args)


def _conv_bwd(x, col_off, w, b, act, dout, dp, name):
    T = x.shape[0]
    C = w.shape[1]
    G = C // CONV_W
    tt = min(512, T)
    NT = T // tt
    has_b = b is not None

    def body(*refs):
        if has_b:
            x_ref, xp_ref, w_ref, b_ref, d_ref, dp_in, dx_ref, dw_ref, db_ref, pad, dpad = refs
        else:
            x_ref, xp_ref, w_ref, d_ref, dp_in, dx_ref, dw_ref, db_ref, pad, dpad = refs
        t = pl.program_id(1)
        first_tile = t == NT - 1

        @pl.when(t == 0)
        def _():
            dpad[pl.ds(tt, PAD), :] = jnp.zeros((PAD, CONV_W), f32)
            dw_ref[...] = jnp.zeros_like(dw_ref)
            db_ref[...] = jnp.zeros_like(db_ref)

        pad[pl.ds(0, PAD), :] = jnp.where(first_tile, 0.0, xp_ref[...])
        pad[pl.ds(PAD, tt), :] = x_ref[...]
        fold = lambda v: v.reshape(SUB_R // 8, 8, 128).sum(axis=0)
        for g in range(CONV_W // 128):
            ls = slice(g * 128, (g + 1) * 128)
            wv = w_ref[:, ls]
            acc = [jnp.zeros((8, 128), f32) for _ in range(CONV_K + 1)]
            for c in reversed(range(tt // SUB_R)):
                r0 = c * SUB_R
                xs = [pad[pl.ds(PAD - 3 + kk + r0, SUB_R), ls] for kk in range(CONV_K)]
                dy = d_ref[pl.ds(r0, SUB_R), ls]
                if act:
                    y = wv[0:1, :] * xs[0]
                    for kk in range(1, CONV_K):
                        y = y + wv[kk:kk + 1, :] * xs[kk]
                    if has_b:
                        y = y + b_ref[:, ls]
                    dy = dy * _dsilu(y)
                dpad[pl.ds(r0, SUB_R), ls] = dy
                dx = wv[3:4, :] * dy
                for j in range(1, CONV_K):
                    dx = dx + wv[3 - j:4 - j, :] * dpad[pl.ds(r0 + j, SUB_R), ls]
                dx_ref[pl.ds(r0, SUB_R), ls] = dx.astype(bf16)
                for kk in range(CONV_K):
                    acc[kk] = acc[kk] + fold(dy * xs[kk])
                acc[CONV_K] = acc[CONV_K] + fold(dy)
            for kk in range(CONV_K):
                dw_ref[kk:kk + 1, ls] += jnp.sum(acc[kk], axis=0, keepdims=True)
            db_ref[:, ls] += jnp.sum(acc[CONV_K], axis=0, keepdims=True)
        head = dpad[pl.ds(0, PAD), :]
        dpad[pl.ds(tt, PAD), :] = head

    rows8 = tt // PAD
    in_specs = [pl.BlockSpec((tt, CONV_W), lambda g, t: (NT - 1 - t, col_off + g)),
                pl.BlockSpec((PAD, CONV_W), lambda g, t: (jnp.maximum((NT - 1 - t) * rows8 - 1, 0), col_off + g)),
                pl.BlockSpec((CONV_K, CONV_W), lambda g, t: (0, g))]
    args = [x, x, w]
    if has_b:
        in_specs.append(pl.BlockSpec((1, CONV_W), lambda g, t: (0, g)))
        args.append(b)
    in_specs += [pl.BlockSpec((tt, CONV_W), lambda g, t: (NT - 1 - t, g)), pl.BlockSpec(memory_space=pl.ANY)]
    args += [dout, dp]
    return pl.pallas_call(
        body, grid=(G, NT), in_specs=in_specs,
        out_specs=[pl.BlockSpec((tt, CONV_W), lambda g, t: (NT - 1 - t, col_off + g)),
                   pl.BlockSpec((CONV_K, CONV_W), lambda g, t: (0, g)),
                   pl.BlockSpec((1, CONV_W), lambda g, t: (0, g))],
        out_shape=[jax.ShapeDtypeStruct(dp.shape, dp.dtype), jax.ShapeDtypeStruct((CONV_K, C), f32),
                   jax.ShapeDtypeStruct((1, C), f32)],
        input_output_aliases={len(args) - 1: 0},
        scratch_shapes=[pltpu.VMEM((tt + PAD, CONV_W), f32), pltpu.VMEM((tt + PAD, CONV_W), f32)],
        compiler_params=_cparams(("parallel", "arbitrary")), name=name,
    )(*args)


def _lru_gates(xc, wr, wi, br, bi, lam):
    r = _sigmoid(_dot(xc, wr, _NN) + br)
    i = _sigmoid(_dot(xc, wi, _NN) + bi)
    sp = _softplus(-lam)
    a = jnp.exp(-LRU_C * r * sp)
    mult = jnp.sqrt(1.0 - a * a)
    return r, i, sp, a, mult


def _lru_fwd(xc, p, y_off, wr, wi, br, bi, lam, mix, name):
    T = xc.shape[0]
    G = LRU_WIDTH // 128
    tt = min(512, T)
    NT = T // tt

    def body(x_ref, y_ref, wr_ref, wi_ref, br_ref, bi_ref, l_ref, mix_in, o_ref, h_ref, hc):
        t = pl.program_id(1)

        @pl.when(t == 0)
        def _():
            hc[...] = jnp.zeros_like(hc)

        x = x_ref[...]
        r, i, sp, a, mult = _lru_gates(x, wr_ref[...], wi_ref[...], br_ref[...], bi_ref[...], l_ref[...])
        row = lax.broadcasted_iota(jnp.int32, (tt, 128), 0)
        mult = jnp.where((row == 0) & (t == 0), 1.0, mult)
        U = x * i * mult
        A = a
        d = 1
        while d < tt:
            keep = row >= d
            Ush = jnp.where(keep, pltpu.roll(U, d, 0), 0.0)
            Ash = jnp.where(keep, pltpu.roll(A, d, 0), 1.0)
            U = A * Ush + U
            A = A * Ash
            d *= 2
        h = U + A * hc[0:1, :]
        h_ref[...] = h
        hc[...] = jnp.broadcast_to(h[tt - 1:tt, :], hc.shape)
        o_ref[...] = (h * _gelu(y_ref[...])).astype(bf16)

    tile = pl.BlockSpec((tt, 128), lambda g, t: (t, g))
    vec = pl.BlockSpec((1, 128), lambda g, t: (0, g))
    wsp = pl.BlockSpec((128, 128), lambda g, t: (g, g))
    return pl.pallas_call(
        body, grid=(G, NT),
        in_specs=[tile, pl.BlockSpec((tt, 128), lambda g, t: (t, y_off + g)), wsp, wsp, vec, vec, vec,
                  pl.BlockSpec(memory_space=pl.ANY)],
        out_specs=[pl.BlockSpec((tt, 128), lambda g, t: (t, G + g)), tile],
        out_shape=[jax.ShapeDtypeStruct(mix.shape, mix.dtype), jax.ShapeDtypeStruct((T, LRU_WIDTH), f32)],
        input_output_aliases={7: 0},
        scratch_shapes=[pltpu.VMEM((8, 128), f32)],
        compiler_params=_cparams(("parallel", "arbitrary")), name=name,
    )(xc, p, wr, wi, br, bi, lam, mix)


def _lru_bwd(xc, p, y_off, wr, wi, br, bi, lam, hs, dmix, d_off, dp, name):
    T = xc.shape[0]
    G = LRU_WIDTH // 128
    tt = min(512, T)
    NT = T // tt

    def body(x_ref, y_ref, wr_ref, wi_ref, br_ref, bi_ref, l_ref, h_ref, hp_ref, do_ref, dp_in,
             dx_ref, dy_ref, dwr_ref, dwi_ref, dbr_ref, dbi_ref, dl_ref, lc, an):
        t = pl.program_id(1)
        first_tile = t == NT - 1

        @pl.when(t == 0)
        def _():
            lc[...] = jnp.zeros_like(lc)
            an[...] = jnp.zeros_like(an)
            dwr_ref[...] = jnp.zeros_like(dwr_ref)
            dwi_ref[...] = jnp.zeros_like(dwi_ref)
            dbr_ref[...] = jnp.zeros_like(dbr_ref)
            dbi_ref[...] = jnp.zeros_like(dbi_ref)
            dl_ref[...] = jnp.zeros_like(dl_ref)

        x = x_ref[...]
        y = y_ref[...]
        wr, wi, lam_ = wr_ref[...], wi_ref[...], l_ref[...]
        r, i, sp, a, mult_raw = _lru_gates(x, wr, wi, br_ref[...], bi_ref[...], lam_)
        row = lax.broadcasted_iota(jnp.int32, (tt, 128), 0)
        t0 = (row == 0) & first_tile
        mult = jnp.where(t0, 1.0, mult_raw)
        h = h_ref[...]
        do = do_ref[...]
        dh = do * _gelu(y)
        dy_ref[...] = (do * h * _dgelu(y)).astype(bf16)
        B = jnp.where(row == tt - 1, an[0:1, :], pltpu.roll(a, tt - 1, 0))
        L = dh
        d = 1
        while d < tt:
            keep = row < tt - d
            Lsh = jnp.where(keep, pltpu.roll(L, tt - d, 0), 0.0)
            Bsh = jnp.where(keep, pltpu.roll(B, tt - d, 0), 1.0)
            L = L + B * Lsh
            B = B * Bsh
            d *= 2
        L = L + B * lc[0:1, :]
        lc[...] = jnp.broadcast_to(L[0:1, :], lc.shape)
        an[...] = jnp.broadcast_to(a[0:1, :], an.shape)
        hprev = jnp.where(first_tile, 0.0, hp_ref[...])[PAD - 1:PAD, :]
        hm1 = jnp.where(row == 0, hprev, pltpu.roll(h, 1, 0))
        da = L * hm1
        dxc = L * i * mult
        di = L * x * mult
        dmult = jnp.where(t0, 0.0, L * x * i)
        da = da - jnp.where(t0, 0.0, dmult * a / mult_raw)
        dlog_a = da * a
        dr = dlog_a * (-LRU_C) * sp
        dsp = jnp.sum(dlog_a * (-LRU_C) * r, axis=0, keepdims=True)
        dpr = dr * r * (1.0 - r)
        dpi = di * i * (1.0 - i)
        dx_ref[...] = dxc + _dot(dpr, wr, _NT) + _dot(dpi, wi, _NT)
        dwr_ref[0] += _dot(x, dpr, _TN)
        dwi_ref[0] += _dot(x, dpi, _TN)
        dbr_ref[...] += jnp.sum(dpr, axis=0, keepdims=True)
        dbi_ref[...] += jnp.sum(dpi, axis=0, keepdims=True)
        dl_ref[...] += dsp * (-_sigmoid(-lam_))

    rows8 = tt // PAD
    tile = pl.BlockSpec((tt, 128), lambda g, t: (NT - 1 - t, g))
    vec = pl.BlockSpec((1, 128), lambda g, t: (0, g))
    wsp = pl.BlockSpec((128, 128), lambda g, t: (g, g))
    wout = pl.BlockSpec((1, 128, 128), lambda g, t: (g, 0, 0))
    return pl.pallas_call(
        body, grid=(G, NT),
        in_specs=[tile, pl.BlockSpec((tt, 128), lambda g, t: (NT - 1 - t, y_off + g)), wsp, wsp, vec, vec, vec, tile,
                  pl.BlockSpec((PAD, 128), lambda g, t: (jnp.maximum((NT - 1 - t) * rows8 - 1, 0), g)),
                  pl.BlockSpec((tt, 128), lambda g, t: (NT - 1 - t, d_off + g)), pl.BlockSpec(memory_space=pl.ANY)],
        out_specs=[tile, pl.BlockSpec((tt, 128), lambda g, t: (NT - 1 - t, y_off + g)), wout, wout, vec, vec, vec],
        out_shape=[jax.ShapeDtypeStruct((T, LRU_WIDTH), f32), jax.ShapeDtypeStruct(dp.shape, dp.dtype),
                   jax.ShapeDtypeStruct((G, 128, 128), f32), jax.ShapeDtypeStruct((G, 128, 128), f32),
                   jax.ShapeDtypeStruct((1, LRU_WIDTH), f32), jax.ShapeDtypeStruct((1, LRU_WIDTH), f32),
                   jax.ShapeDtypeStruct((1, LRU_WIDTH), f32)],
        input_output_aliases={10: 1},
        scratch_shapes=[pltpu.VMEM((8, 128), f32), pltpu.VMEM((8, 128), f32)],
        compiler_params=_cparams(("parallel", "arbitrary")), name=name,
    )(xc, p, wr, wi, br, bi, lam, hs, hs, dmix, dp)


_NN3 = (((2,), (1,)), ((0,), (0,)))
_NT3 = (((2,), (2,)), ((0,), (0,)))
_TN3 = (((1,), (1,)), ((0,), (0,)))


def _heads(ref):
    return _stack_heads(ref, GDN_HEADS)


def _rowsum(x):
    H, C, L = x.shape
    return _dot(x.reshape(H * C, L), jnp.ones((L, HEAD_DIM), f32), _NN).reshape(H, C, HEAD_DIM)


def _gdn_chunk(qr, kr, v, ba, alog, dtb, S, saved=None):
    C, H = GDN_CHUNK, GDN_HEADS
    lane = lax.broadcasted_iota(jnp.int32, (C, 128), 1)
    lane3 = lax.broadcasted_iota(jnp.int32, (H, C, 128), 2)
    ri = lax.broadcasted_iota(jnp.int32, (C, C), 0)
    ci = lax.broadcasted_iota(jnp.int32, (C, C), 1)
    rowc = lax.broadcasted_iota(jnp.int32, (C, 1), 0)
    col = lambda m, j: jnp.sum(jnp.where(lane == j, m, 0.0), axis=1, keepdims=True)
    cols = lambda m, off: jnp.stack([col(m, off + h) for h in range(H)])
    ea = jnp.exp(alog)
    g_all = -ea * _softplus(ba + dtb)
    tri = (ri >= ci).astype(f32)
    G_all = _dot01(tri, g_all, _NN)
    beta = cols(_sigmoid(ba), 0)
    Gc = cols(G_all, H)
    rq = lax.rsqrt(_rowsum(qr * qr) + EPS)
    rk = lax.rsqrt(_rowsum(kr * kr) + EPS)
    qh, kn = qr * rq, kr * rk
    qn = qh * (HEAD_DIM ** -0.5)
    Grow = _dot01(jnp.ones((H, C, 128), f32), jnp.where(lane3 == 0, Gc, 0.0), _NT3)
    incl = ri >= ci
    Di = jnp.where(incl, jnp.exp(jnp.where(incl, Gc - Grow, 0.0)), 0.0)
    Ds = jnp.where(ri > ci, Di, 0.0)
    Gl = jnp.sum(jnp.where(rowc == C - 1, Gc, 0.0), axis=1, keepdims=True)
    eG = jnp.exp(Gc)
    eGl = jnp.exp(Gl - Gc)
    cd = jnp.exp(Gl)
    kb = kn * beta
    vb = v * beta
    Lm = _dot(kb, kn, _NT3) * Ds
    kbg = kb * eG
    QK = _dot(qn, kn, _NT3) * Di
    qg = qn * eG
    kg = kn * eGl
    if saved is None:
        same = lambda s: (ri // s) == (ci // s)
        Xd = jnp.where(same(8), -Lm, 0.0)
        Tinv = (ri == ci).astype(f32) + Xd
        Pw = Xd
        for _ in range(2):
            Pw = _dot(Pw, Pw, _NN3)
            Tinv = Tinv + _dot(Tinv, Pw, _NN3)
        for s in (8, 16, 32):
            off = jnp.where(same(2 * s) & jnp.logical_not(same(s)), Lm, 0.0)
            Tinv = Tinv - _dot(_dot(Tinv, off, _NN3), Tinv, _NN3)
        w = _dot(Tinv, kbg, _NN3)
        vn = _dot(Tinv, vb, _NN3) - _dot(w, S, _NN3)
        o = _dot(qg, S, _NN3) + _dot(QK, vn, _NN3)
        S1 = S * cd + _dot(kg, vn, _TN3)
    else:
        Tinv, vn, o = saved
        w = _dot(Tinv, kbg, _NN3)
        S1 = None
    return dict(beta=beta, g_all=g_all, rq=rq, rk=rk, qh=qh, kn=kn, qn=qn, Di=Di, Ds=Ds, eG=eG, eGl=eGl, cd=cd,
                kb=kb, vb=vb, Lm=Lm, Tinv=Tinv, kbg=kbg, w=w, QK=QK, qg=qg, kg=kg, vn=vn, o=o, S1=S1,
                lane=lane, ri=ri, ci=ci, rowc=rowc, ea=ea)


def _gdn_specs(T, rev):
    C = GDN_CHUNK
    H = GDN_HEADS
    K = min(GDN_STEP, T // C)
    NS = T // (C * K)
    nn = (lambda n: NS - 1 - n) if rev else (lambda n: n)
    wide = lambda blk: pl.BlockSpec((K * C, H * HEAD_DIM), lambda n: (nn(n), blk))
    one = lambda off: pl.BlockSpec((K * C, HEAD_DIM), lambda n: (nn(n), off))
    vec = pl.BlockSpec((1, 128), lambda n: (0, 0))
    st = lambda rows: pl.BlockSpec((K, H, rows, rows), lambda n: (nn(n), 0, 0, 0))
    return K, NS, wide, one, vec, st


def _gdn_fwd(qkv, p, alog, dtb, nw, name):
    T = qkv.shape[0]
    C, H = GDN_CHUNK, GDN_HEADS
    N = T // C
    K, NS, wide, one, vec, st_spec = _gdn_specs(T, False)

    def body(q_ref, k_ref, v_ref, z_ref, ba_ref, al_ref, dt_ref, nw_ref, y_ref, sp_ref, ti_ref, vn_ref, o_ref, st):
        @pl.when(pl.program_id(0) == 0)
        def _():
            st[...] = jnp.zeros_like(st)

        S = st[...]
        for c in range(K):
            rows = pl.ds(c * C, C)
            at = lambda ref: ref.at[rows, :]
            f = _gdn_chunk(_heads(at(q_ref)), _heads(at(k_ref)), _heads(at(v_ref)), ba_ref[rows, :], al_ref[...],
                           dt_ref[...], S)
            sp_ref[c] = S
            S = f["S1"]
            ti_ref[c] = f["Tinv"]
            o, vn = f["o"], f["vn"]
            r = lax.rsqrt(_rowsum(o * o) * (1.0 / HEAD_DIM) + EPS)
            y = o * r * nw_ref[...] * _silu(_heads(at(z_ref)))
            for h in range(H):
                sl = slice(h * HEAD_DIM, (h + 1) * HEAD_DIM)
                y_ref[rows, sl] = y[h].astype(bf16)
                vn_ref[rows, sl] = vn[h]
                o_ref[rows, sl] = o[h]
        st[...] = S

    wide_f32 = jax.ShapeDtypeStruct((T, H * HEAD_DIM), f32)
    return pl.pallas_call(
        body, grid=(NS,),
        in_specs=[wide(0), wide(1), wide(2), wide(3), one(4 * H), vec, vec, vec],
        out_specs=[wide(0), st_spec(HEAD_DIM), st_spec(C), wide(0), wide(0)],
        out_shape=[jax.ShapeDtypeStruct((T, H * HEAD_DIM), bf16), jax.ShapeDtypeStruct((N, H, HEAD_DIM, HEAD_DIM), f32),
                   jax.ShapeDtypeStruct((N, H, C, C), f32), wide_f32, wide_f32],
        scratch_shapes=[pltpu.VMEM((H, HEAD_DIM, HEAD_DIM), f32)],
        compiler_params=_cparams(("arbitrary",)), name=name,
    )(qkv, qkv, qkv, p, p, alog, dtb, nw)


def _gdn_bwd(qkv, p, alog, dtb, nw, sprev, tinv, vn_all, o_all, dy_all, name):
    T = qkv.shape[0]
    C, H = GDN_CHUNK, GDN_HEADS
    N = T // C
    K, NS, wide, one, vec, st_spec = _gdn_specs(T, True)
    rs = lambda m: jnp.sum(m, axis=2, keepdims=True)

    def put(ref, val, col=0):
        for h in range(H):
            ref[:, col + h * HEAD_DIM:col + (h + 1) * HEAD_DIM] = val[h].astype(ref.dtype)

    def body(q_ref, k_ref, v_ref, z_ref, ba_ref, al_ref, dt_ref, nw_ref, sp_ref, ti_ref, vn_ref, o_ref, dy_ref,
             dqkv_ref, dz_ref, dba_ref, dal_ref, ddt_ref, dnw_ref, dst):
        @pl.when(pl.program_id(0) == 0)
        def _():
            dst[...] = jnp.zeros_like(dst)
            dal_ref[...] = jnp.zeros_like(dal_ref)
            ddt_ref[...] = jnp.zeros_like(ddt_ref)
            dnw_ref[...] = jnp.zeros_like(dnw_ref)

        dS = dst[...]
        for c in reversed(range(K)):
            at = lambda ref, c=c: ref.at[pl.ds(c * C, C), :]
            dS = chunk(at(q_ref), at(k_ref), at(v_ref), at(z_ref), at(ba_ref), al_ref, dt_ref, nw_ref, sp_ref[c], ti_ref[c],
                       at(vn_ref), at(o_ref), at(dy_ref), at(dqkv_ref), at(dz_ref), at(dba_ref), dal_ref, ddt_ref, dnw_ref, dS)
        dst[...] = dS

    def chunk(q_ref, k_ref, v_ref, z_ref, ba_ref, al_ref, dt_ref, nw_ref, S, Tsaved, vn_ref, o_ref, dy_ref,
              dqkv_ref, dz_ref, dba_ref, dal_ref, ddt_ref, dnw_ref, dS1):
        ba, alog, dtb_, nwv = ba_ref[...], al_ref[...], dt_ref[...], nw_ref[...]
        v = _heads(v_ref)
        f = _gdn_chunk(_heads(q_ref), _heads(k_ref), v, ba, alog, dtb_, S, saved=(Tsaved, _heads(vn_ref), _heads(o_ref)))
        beta, kn, qn, kb, vb, Tinv, kbg = f["beta"], f["kn"], f["qn"], f["kb"], f["vb"], f["Tinv"], f["kbg"]
        eG, eGl, cd, Di, Ds, QK, vn, w_, qg, kg = (f["eG"], f["eGl"], f["cd"], f["Di"], f["Ds"], f["QK"], f["vn"],
                                                    f["w"], f["qg"], f["kg"])
        lane, ri, ci, rowc = f["lane"], f["ri"], f["ci"], f["rowc"]
        o = f["o"]
        z = _heads(z_ref)
        dy = _heads(dy_ref)
        r = lax.rsqrt(_rowsum(o * o) * (1.0 / HEAD_DIM) + EPS)
        nrm = o * r
        sz = _silu(z)
        dn = dy * nwv * sz
        put(dz_ref, dy * nrm * nwv * _dsilu(z))
        dnw_ref[...] += jnp.sum(jnp.sum(dy * nrm * sz, axis=0), axis=0, keepdims=True)
        do = r * (dn - nrm * (_rowsum(dn * nrm) * (1.0 / HEAD_DIM)))
        dcd = jnp.sum(jnp.sum(S * dS1, axis=2, keepdims=True), axis=1, keepdims=True)
        dkg = _dot(vn, dS1, _NT3)
        dvn = _dot(kg, dS1, _NN3) + _dot(QK, do, _TN3)
        dqg = _dot(do, S, _NT3)
        dQK = _dot(do, vn, _NT3)
        dw = -_dot(dvn, S, _NT3)
        dS0 = cd * dS1 + _dot(qg, do, _TN3) - _dot(w_, dvn, _TN3)
        dqn = dqg * eG
        dkn = dkg * eGl
        deGl = rs(dkg * kn)
        dQKr = dQK * Di
        E = dQK * QK
        dqn = dqn + _dot(dQKr, kn, _NN3)
        dkn = dkn + _dot(dQKr, qn, _TN3)
        dT = _dot(dvn, vb, _NT3) + _dot(dw, kbg, _NT3)
        dvb = _dot(Tinv, dvn, _TN3)
        dkbg = _dot(Tinv, dw, _TN3)
        dkb = dkbg * eG
        deG = rs(dqg * qn + dkbg * kb)
        dL = -_dot(_dot(Tinv, dT, _TN3), Tinv, _NT3)
        dKK = dL * Ds
        E = E + dL * f["Lm"]
        dkb = dkb + _dot(dKK, kn, _NN3)
        dkn = dkn + _dot(dKK, kb, _TN3) + dkb * beta
        dbeta = rs(dkb * kn + dvb * v)
        put(dqkv_ref, dvb * beta, 2 * H * HEAD_DIM)
        dG = rs(E) - rs(jnp.swapaxes(E, 1, 2)) + deG * eG - deGl * eGl
        dGl = jnp.sum(deGl * eGl, axis=1, keepdims=True) + dcd * cd
        dG = dG + jnp.where(rowc == C - 1, dGl, 0.0)
        qh = f["qh"]
        put(dqkv_ref, (HEAD_DIM ** -0.5) * f["rq"] * (dqn - qh * _rowsum(dqn * qh)))
        put(dqkv_ref, f["rk"] * (dkn - kn * _rowsum(dkn * kn)), H * HEAD_DIM)
        db = dbeta * beta * (1.0 - beta)
        db_all = jnp.where(lane == 0, db[0], 0.0)
        dG_all = jnp.where(lane == H, dG[0], 0.0)
        for h in range(1, H):
            db_all = db_all + jnp.where(lane == h, db[h], 0.0)
            dG_all = dG_all + jnp.where(lane == H + h, dG[h], 0.0)
        triu = (ri <= ci).astype(f32)
        dg_all = _dot01(triu, dG_all, _NN)
        da_all = dg_all * (-f["ea"]) * _sigmoid(ba + dtb_)
        dba_ref[...] = (db_all + da_all).astype(bf16)
        ddt_ref[...] += jnp.sum(da_all, axis=0, keepdims=True)
        dal_ref[...] += jnp.sum(dg_all * f["g_all"], axis=0, keepdims=True)
        return dS0

    small = jax.ShapeDtypeStruct((1, 128), f32)
    return pl.pallas_call(
        body, grid=(NS,),
        in_specs=[wide(0), wide(1), wide(2), wide(3), one(4 * H), vec, vec, vec, st_spec(HEAD_DIM), st_spec(C),
                  wide(0), wide(0), wide(0)],
        out_specs=[pl.BlockSpec((K * C, 3 * H * HEAD_DIM), lambda n: (NS - 1 - n, 0)), wide(3), one(0), vec, vec, vec],
        out_shape=[jax.ShapeDtypeStruct((T, 3 * H * HEAD_DIM), f32), jax.ShapeDtypeStruct((T, ODD_PAD), bf16),
                   jax.ShapeDtypeStruct((T, 128), bf16), small, small, small],
        scratch_shapes=[pltpu.VMEM((H, HEAD_DIM, HEAD_DIM), f32)],
        compiler_params=_cparams(("arbitrary",)), name=name,
    )(qkv, qkv, qkv, p, p, alog, dtb, nw, sprev, tinv, vn_all, o_all, dy_all)


def _lanes_from(x, s):
    return x if s % 128 == 0 else pltpu.roll(x, (128 - s) % 128, 1)


def _odd_assemble(g, name):
    R = g.shape[1]
    tr = min(256, R)
    n_blk = ODD_SHARD_PAD // 128

    def body(g_ref, o_ref):
        lane = lax.broadcasted_iota(jnp.int32, (tr, 128), 1)
        blk = lambda d, m: g_ref[d, :, m * 128:(m + 1) * 128]
        for gb in range(ODD_PAD // 128):
            c0 = 128 * gb
            if c0 >= ODD_IN:
                o_ref[:, c0:c0 + 128] = jnp.zeros((tr, 128), g.dtype)
                continue
            d0 = c0 // ODD_SHARD
            m0, sh = divmod(c0 - ODD_SHARD * d0, 128)
            take = min(128, ODD_SHARD * (d0 + 1) - c0)
            p = _lanes_from(blk(d0, m0), sh)
            if sh and m0 + 1 < n_blk:
                p = jnp.where(lane < 128 - sh, p, _lanes_from(blk(d0, m0 + 1), sh))
            if take < 128:
                nxt = pltpu.roll(blk(d0 + 1, 0), take, 1) if d0 + 1 < N_DEV else jnp.zeros((tr, 128), g.dtype)
                p = jnp.where(lane < take, p, nxt)
            o_ref[:, c0:c0 + 128] = p

    return pl.pallas_call(
        body, grid=(R // tr,),
        in_specs=[pl.BlockSpec((N_DEV, tr, ODD_SHARD_PAD), lambda i: (0, i, 0))],
        out_specs=pl.BlockSpec((tr, ODD_PAD), lambda i: (i, 0)),
        out_shape=jax.ShapeDtypeStruct((R, ODD_PAD), g.dtype),
        compiler_params=_cparams(("parallel",)), name=name,
    )(g)


def _odd_split(w, name):
    R = w.shape[0]
    tr = min(256, R)

    def body(w_ref, o_ref):
        lane = lax.broadcasted_iota(jnp.int32, (tr, 128), 1)
        blk = lambda gb: w_ref[:, gb * 128:(gb + 1) * 128]
        for d in range(N_DEV):
            for m in range(ODD_SHARD_PAD // 128):
                g0, sh = divmod(ODD_SHARD * d + 128 * m, 128)
                p = _lanes_from(blk(g0), sh)
                if sh and g0 + 1 < ODD_PAD // 128:
                    p = jnp.where(lane < 128 - sh, p, _lanes_from(blk(g0 + 1), sh))
                real = ODD_SHARD - 128 * m
                if real < 128:
                    p = jnp.where(lane < real, p, jnp.zeros_like(p))
                o_ref[d, :, m * 128:(m + 1) * 128] = p

    return pl.pallas_call(
        body, grid=(R // tr,),
        in_specs=[pl.BlockSpec((tr, ODD_PAD), lambda i: (i, 0))],
        out_specs=pl.BlockSpec((N_DEV, tr, ODD_SHARD_PAD), lambda i: (0, i, 0)),
        out_shape=jax.ShapeDtypeStruct((N_DEV, R, ODD_SHARD_PAD), w.dtype),
        compiler_params=_cparams(("parallel",)), name=name,
    )(w)


def _adamw(w, gs, m, v, name, layer=None, prev=None):
    R, Cc = w.shape[-2:]
    S = gs.shape[0]
    tr = R
    for cand in (256, 128, 64, 32, 16, 8):
        if R % cand == 0 and R > cand:
            tr = cand
            break
    c1 = 1.0 - ADAM_B1 ** ADAM_STEP
    c2 = 1.0 - ADAM_B2 ** ADAM_STEP

    def body(w_ref, g_ref, m_ref, v_ref, *rest):
        go_ref, d_ref, mo_ref, vo_ref = rest[-4:]
        g = g_ref[0].astype(f32)
        for s in range(1, S):
            g = g + g_ref[s].astype(f32)
        mn = ADAM_B1 * m_ref[...] + (1.0 - ADAM_B1) * g
        vn = ADAM_B2 * v_ref[...] + (1.0 - ADAM_B2) * (g * g)
        go_ref[...] = g
        mo_ref[...] = mn
        vo_ref[...] = vn
        d_ref[...] = -ADAM_LR * ((mn / c1) / (jnp.sqrt(vn / c2) + ADAM_EPS) + ADAM_WD * w_ref[...])

    tc = Cc if gs.shape[-1] == Cc else 128
    if layer is None:
        blk = pl.BlockSpec((tr, tc), lambda i, j: (i, j))
    else:
        blk = pl.BlockSpec((None, tr, tc), lambda i, j: (layer, i, j))
    out = jax.ShapeDtypeStruct(w.shape, f32)
    carried = [] if prev is None else list(prev)
    return pl.pallas_call(
        body, grid=(R // tr, pl.cdiv(Cc, tc)),
        in_specs=[blk, pl.BlockSpec((S, tr, tc), lambda i, j: (0, i, j)), blk, blk]
        + [pl.BlockSpec(memory_space=pl.ANY)] * len(carried),
        out_specs=[blk] * 4, out_shape=[out] * 4,
        input_output_aliases={4 + j: j for j in range(len(carried))},
        compiler_params=_cparams(("parallel", "parallel")), name=name,
    )(w, gs, m, v, *carried)


def _me():
    x, y, c = lax.axis_index("x"), lax.axis_index("y"), lax.axis_index("c")
    return x, y, c, 4 * x + 2 * y + c


def _peer(k):
    x, y, c, _ = _me()
    px = 1 - x if k & 4 else x
    py = 1 - y if k & 2 else y
    pc = 1 - c if k & 1 else c
    return (px, py, pc), 4 * px + 2 * py + pc


_HBM = pl.BlockSpec(memory_space=pltpu.HBM)
_SEM = pl.BlockSpec(memory_space=pltpu.SEMAPHORE)
_EFFECT = pltpu.SideEffectType.DATAFLOW_SIDE_EFFECTING


def _copy(src, land, ssem, rsem, k, blocked, landing_slot_of_peer):
    pid, pidx = _peer(k)
    slot = pidx if landing_slot_of_peer else _me()[3]
    return pltpu.make_async_remote_copy(src_ref=src.at[pidx] if blocked else src, dst_ref=land.at[slot],
                                        send_sem=ssem.at[k - 1], recv_sem=rsem.at[k - 1], device_id=pid, device_id_type=MESH)


def _send_start(srcs, blocked, name):
    n = len(srcs)
    lands = [lax.empty(a.shape if blocked else (N_DEV,) + a.shape, a.dtype) for a in srcs]

    def body(*refs):
        src, land, sems, token = refs[:n], refs[n:2 * n], refs[2 * n:4 * n], refs[-1]
        for i in range(n):
            for k in range(1, N_DEV):
                _copy(src[i], land[i], sems[2 * i], sems[2 * i + 1], k, blocked, False).start()
        token[...] = jnp.zeros_like(token)

    sem = pltpu.SemaphoreType.DMA((N_DEV - 1,))
    hbm = lambda a: pltpu.with_memory_space_constraint(a, pltpu.HBM)
    res = pl.pallas_call(
        body, name=name,
        out_shape=tuple([sem] * (2 * n)) + tuple(pltpu.HBM(a.shape, a.dtype) for a in srcs + lands)
        + (jax.ShapeDtypeStruct((8, 128), f32),),
        in_specs=(_HBM,) * (2 * n),
        out_specs=(_SEM,) * (2 * n) + (_HBM,) * (2 * n) + (pl.BlockSpec(memory_space=pltpu.VMEM),),
        input_output_aliases={j: 2 * n + j for j in range(2 * n)},
        compiler_params=pltpu.CompilerParams(has_side_effects=_EFFECT),
    )(*[hbm(a) for a in srcs], *[hbm(a) for a in lands])
    handles = [(res[2 * i], res[2 * i + 1], res[2 * n + i], res[3 * n + i]) for i in range(n)]
    return handles, res[-1]


def _send_wait(handle, blocked, after, name):
    ssem, rsem, src, land = handle

    def body(src_ref, land_ref, ssem_ref, rsem_ref, after_ref, src_out, land_out):
        for k in range(1, N_DEV):
            cp = _copy(src_ref, land_ref, ssem_ref, rsem_ref, k, blocked, True)
            cp.wait_send()
            cp.wait_recv()

    return pl.pallas_call(
        body, name=name, out_shape=(pltpu.HBM(src.shape, src.dtype), pltpu.HBM(land.shape, land.dtype)),
        in_specs=(_HBM, _HBM, _SEM, _SEM, pl.BlockSpec(memory_space=pl.ANY)), out_specs=(_HBM, _HBM),
        input_output_aliases={0: 0, 1: 1}, compiler_params=pltpu.CompilerParams(has_side_effects=_EFFECT),
    )(src, land, ssem, rsem, after)


def _block_diag(w):
    nb, bs = w.shape[0], w.shape[1]
    eye = jnp.eye(nb, dtype=w.dtype)
    return (eye[:, None, :, None] * w[:, :, None, :]).reshape(nb * bs, nb * bs)


def _diag_blocks(d):
    return jnp.stack([d[g, s * 64:(s + 1) * 64, s * 64:(s + 1) * 64] for g in range(4) for s in range(2)])


_SQUARE_TILES = dict(tm=1024, tn=1024, tk=1024)


def _mlp_fwd(x, hm, wu, wd, tag, epilogue, extras, outs):
    (r,) = _matmul(hm, wu, "nn", outs=[bf16], epilogue=lambda acc: (jnp.maximum(acc, 0.0),), name=f"mlp_up_{tag}")
    res = _matmul(r, wd, "nn", outs=outs, extras=(x,) + tuple(extras), epilogue=epilogue, a_map=jnp.square,
                  name=f"mlp_down_{tag}", **_SQUARE_TILES)
    return res, (hm, r)


def _mlp_bwd(x, nw, wu, wd, saved, dxo, dxo_b, tag):
    hm, r = saved
    (du,) = _matmul(dxo_b, wd, "nt", outs=[bf16], extras=(r,), epilogue=lambda acc, rr: (acc * (2.0 * rr.astype(f32)),),
                    name=f"mlp_dact_{tag}")
    (dwd,) = _matmul(r, dxo_b, "tn", outs=[bf16], a_map=jnp.square, name=f"mlp_dwd_{tag}", **_SQUARE_TILES)
    (dwu,) = _matmul(hm, du, "tn", outs=[bf16], shard_cols=2, name=f"mlp_dwu_{tag}")
    dx, dx_b, dnw = _matmul(du, wu, "nt", outs=_RMS_BWD_OUTS, extras=(x, dxo, nw), epilogue=_rms_bwd_ep,
                            name=f"mlp_dh_{tag}", **_SQUARE_TILES)
    return dx, dx_b, dnw, dwu, dwd.reshape(N_DEV, D_FF // N_DEV, D_MODEL)


def _local_step(x, tgt, P, weight, sink):
    T = x.shape[0]
    cos, sin = _rope_tables(T)
    rtab = _ret_tables()
    row = lambda a: a.reshape(1, -1)
    mix_nw, mlp_nw = P["mixer_norm_w"], P["mlp_norm_w"]
    wr_bd, wi_bd = _block_diag(P["lru_w_r"]), _block_diag(P["lru_w_i"])
    lru_b, lru_br, lru_bi, lru_lam = row(P["lru_conv_b"]), row(P["lru_b_r"]), row(P["lru_b_i"]), row(P["lru_lambda"])
    pad16 = lambda a: jnp.pad(a.reshape(1, GDN_HEADS), ((0, 0), (GDN_HEADS, 128 - 2 * GDN_HEADS)))
    alog, dtb = pad16(P["gdn_a_log"]), pad16(P["gdn_dt_bias"])
    gnw = row(P["gdn_norm_w"])

    x0 = x
    h0 = _rms_fwd(x0, mix_nw[0:1], "rms_mix_0")
    w_ie = weight("w_in_even", h0)
    (pe,) = _matmul(h0, w_ie, "nn", outs=[f32], name="in_even")
    mix0, o_ret, s_ret = _ret_fwd(pe, cos, sin, rtab, "ret_fwd")
    w_lc = weight("lru_conv_w", pe)
    xc = _conv_fwd(pe, 4, w_lc, lru_b, False, "lru_conv_fwd")
    mix0, h_lru = _lru_fwd(xc, pe, 20, wr_bd, wi_bd, lru_br, lru_bi, lru_lam, mix0, "lru_fwd")
    w_oe = weight("w_out_even", mix0)
    x1, hm0 = _matmul(mix0, w_oe, "nn", outs=[f32, bf16], extras=(x0, mlp_nw[0:1]), epilogue=_residual_rms_ep,
                      name="out_even", tm=1024, tn=D_MODEL)
    w_u0, w_d0 = weight("w_up0", x1), weight("w_down0", x1)
    (x2, h1), mlp0 = _mlp_fwd(x1, hm0, w_u0, w_d0, "0", _residual_rms_ep, (mix_nw[1:2],), [f32, bf16])
    w_io = weight("w_in_odd", h1)
    (po,) = _matmul(h1, w_io, "nn", outs=[f32], tm=1024, tn=ODD_PAD // 3, name="in_odd")
    w_gc = weight("gdn_conv_w", po)
    qkv = _conv_fwd(po, 0, w_gc, None, True, "gdn_conv_fwd")
    y_gdn, s_gdn, ti_gdn, vn_gdn, o_gdn = _gdn_fwd(qkv, po, alog, dtb, gnw, "gdn_fwd")
    w_oo = weight("w_out_odd", y_gdn)
    x3, hm1 = _matmul(y_gdn, w_oo, "nn", outs=[f32, bf16], extras=(x2, mlp_nw[1:2]), epilogue=_residual_rms_ep,
                      name="out_odd", tm=1024, tn=D_MODEL)
    w_u1, w_d1 = weight("w_up1", x3), weight("w_down1", x3)
    (loss, dx4, dx4_b, d_final), mlp1 = _mlp_fwd(x3, hm1, w_u1, w_d1, "1", _loss_ep, (row(P["final_norm_w"]), tgt),
                                                 _LOSS_OUTS)
    dx3, dx3_b, d_mlp_nw1, d_wu1, d_wd1 = _mlp_bwd(x3, mlp_nw[1:2], w_u1, w_d1, mlp1, dx4, dx4_b, "1")
    tok = sink(dict(w_up1=d_wu1, w_down1=d_wd1))
    (dy_gdn,) = _matmul(dx3_b, w_oo, "nt", outs=[f32], name="out_odd_dx")
    (d_woo,) = _matmul(y_gdn, dx3_b, "tn", outs=[bf16], name="out_odd_dw")
    dqkv, dpo, dba, d_alog, d_dtb, d_gnw = _gdn_bwd(qkv, po, alog, dtb, gnw + tok[0:1, :], s_gdn, ti_gdn, vn_gdn, o_gdn, dy_gdn,
                                                  "gdn_bwd")
    dpo, d_gconv, _ = _conv_bwd(po, 0, w_gc, None, True, dqkv, dpo, "gdn_conv_bwd")
    dpo = lax.dynamic_update_slice(dpo, dba, (0, 4 * D_MODEL))
    (d_wio,) = _matmul(h1, dpo, "tn", outs=[bf16], tn=ODD_PAD // 3, name="in_odd_dw")
    tok = sink(dict(w_out_odd=d_woo.reshape(N_DEV, D_MODEL // N_DEV, D_MODEL), w_in_odd=_odd_split(d_wio, "w_in_odd_split")))
    dx2, dx2_b, d_mix_nw1 = _matmul(dpo, w_io, "nt", outs=_RMS_BWD_OUTS, extras=(x2, dx3, mix_nw[1:2] + tok[0:1, 0:1]),
                                    epilogue=_rms_bwd_ep, tm=1024, tn=1024, tk=ODD_PAD // 3, name="in_odd_dx")
    dx1, dx1_b, d_mlp_nw0, d_wu0, d_wd0 = _mlp_bwd(x1, mlp_nw[0:1], w_u0, w_d0, mlp0, dx2, dx2_b, "0")
    (d_woe,) = _matmul(mix0, dx1_b, "tn", outs=[bf16], name="out_even_dw")
    tok = sink(dict(w_up0=d_wu0, w_down0=d_wd0, w_out_even=d_woe.reshape(N_DEV, D_MODEL // N_DEV, D_MODEL)))
    (dmix0,) = _matmul(dx1_b, w_oe, "nt", outs=[f32], name="out_even_dx")
    dpe = _ret_bwd(pe, cos, sin, rtab, o_ret, s_ret, dmix0, "ret_bwd")
    dxc, dpe, d_wr, d_wi, d_br, d_bi, d_lam = _lru_bwd(xc, pe, 20, wr_bd, wi_bd, lru_br, lru_bi, lru_lam + tok[0:1, 0:1],
                                                       h_lru, dmix0, 4, dpe, "lru_bwd")
    dpe, d_lconv, d_lconv_b = _conv_bwd(pe, 4, w_lc, lru_b, False, dxc, dpe, "lru_conv_bwd")
    (d_wie,) = _matmul(h0, dpe, "tn", outs=[bf16], shard_cols=2, name="in_even_dw")
    dx0, _, d_mix_nw0 = _matmul(dpe, w_ie, "nt", outs=_RMS_BWD_OUTS, extras=(x0, dx1, mix_nw[0:1]), epilogue=_rms_bwd_ep,
                                name="in_even_dx", **_SQUARE_TILES)

    G = dict(
        mixer_norm_w=jnp.concatenate([d_mix_nw0, d_mix_nw1], axis=0),
        mlp_norm_w=jnp.concatenate([d_mlp_nw0, d_mlp_nw1], axis=0),
        final_norm_w=d_final.reshape(-1),
        w_in_even=d_wie, lru_conv_w=d_lconv, lru_conv_b=d_lconv_b.reshape(-1),
        lru_w_r=_diag_blocks(d_wr), lru_b_r=d_br.reshape(-1), lru_w_i=_diag_blocks(d_wi), lru_b_i=d_bi.reshape(-1),
        lru_lambda=d_lam.reshape(-1), gdn_conv_w=d_gconv,
        gdn_a_log=d_alog[0, GDN_HEADS:2 * GDN_HEADS], gdn_dt_bias=d_dtb[0, GDN_HEADS:2 * GDN_HEADS],
        gdn_norm_w=d_gnw.reshape(-1),
    )
    return loss, dx0, G


_SMALL = ["mixer_norm_w", "mlp_norm_w", "final_norm_w", "lru_conv_b", "lru_w_r", "lru_b_r", "lru_w_i", "lru_b_i",
          "lru_lambda", "gdn_a_log", "gdn_dt_bias", "gdn_norm_w"]
_PACK_ROWS = 688


def _pack(parts):
    flat = jnp.concatenate([p.reshape(-1) for p in parts])
    return jnp.pad(flat, (0, _PACK_ROWS * 128 - flat.shape[0])).reshape(_PACK_ROWS, 128)


def _unpack(packed, shapes):
    flat = packed.reshape(-1)
    out, off = [], 0
    for s in shapes:
        n = int(np.prod(s))
        out.append(flat[off:off + n].reshape(s))
        off += n
    return out


def kernel(x, mixer_norm_w, mlp_norm_w, final_norm_w, w_in_even, lru_conv_w, lru_conv_b, lru_w_r, lru_b_r, lru_w_i, lru_b_i, lru_lambda, w_out_even, w_in_odd, gdn_conv_w, gdn_a_log, gdn_dt_bias, gdn_norm_w, w_out_odd, w_up, w_down, loss_target, m_mixer_norm_w, m_mlp_norm_w, m_final_norm_w, m_w_in_even, m_lru_conv_w, m_lru_conv_b, m_lru_w_r, m_lru_b_r, m_lru_w_i, m_lru_b_i, m_lru_lambda, m_w_out_even, m_w_in_odd, m_gdn_conv_w, m_gdn_a_log, m_gdn_dt_bias, m_gdn_norm_w, m_w_out_odd, m_w_up, m_w_down, v_mixer_norm_w, v_mlp_norm_w, v_final_norm_w, v_w_in_even, v_lru_conv_w, v_lru_conv_b, v_lru_w_r, v_lru_b_r, v_lru_w_i, v_lru_b_i, v_lru_lambda, v_w_out_even, v_w_in_odd, v_gdn_conv_w, v_gdn_a_log, v_gdn_dt_bias, v_gdn_norm_w, v_w_out_odd, v_w_up, v_w_down):
    Pw = dict(mixer_norm_w=mixer_norm_w, mlp_norm_w=mlp_norm_w, final_norm_w=final_norm_w, w_in_even=w_in_even,
              lru_conv_w=lru_conv_w, lru_conv_b=lru_conv_b, lru_w_r=lru_w_r, lru_b_r=lru_b_r, lru_w_i=lru_w_i,
              lru_b_i=lru_b_i, lru_lambda=lru_lambda, w_out_even=w_out_even, w_in_odd=w_in_odd, gdn_conv_w=gdn_conv_w,
              gdn_a_log=gdn_a_log, gdn_dt_bias=gdn_dt_bias, gdn_norm_w=gdn_norm_w, w_out_odd=w_out_odd, w_up=w_up,
              w_down=w_down)
    Pm = dict(mixer_norm_w=m_mixer_norm_w, mlp_norm_w=m_mlp_norm_w, final_norm_w=m_final_norm_w, w_in_even=m_w_in_even,
              lru_conv_w=m_lru_conv_w, lru_conv_b=m_lru_conv_b, lru_w_r=m_lru_w_r, lru_b_r=m_lru_b_r, lru_w_i=m_lru_w_i,
              lru_b_i=m_lru_b_i, lru_lambda=m_lru_lambda, w_out_even=m_w_out_even, w_in_odd=m_w_in_odd,
              gdn_conv_w=m_gdn_conv_w, gdn_a_log=m_gdn_a_log, gdn_dt_bias=m_gdn_dt_bias, gdn_norm_w=m_gdn_norm_w,
              w_out_odd=m_w_out_odd, w_up=m_w_up, w_down=m_w_down)
    Pv = dict(mixer_norm_w=v_mixer_norm_w, mlp_norm_w=v_mlp_norm_w, final_norm_w=v_final_norm_w, w_in_even=v_w_in_even,
              lru_conv_w=v_lru_conv_w, lru_conv_b=v_lru_conv_b, lru_w_r=v_lru_w_r, lru_b_r=v_lru_b_r, lru_w_i=v_lru_w_i,
              lru_b_i=v_lru_b_i, lru_lambda=v_lru_lambda, w_out_even=v_w_out_even, w_in_odd=v_w_in_odd,
              gdn_conv_w=v_gdn_conv_w, gdn_a_log=v_gdn_a_log, gdn_dt_bias=v_gdn_dt_bias, gdn_norm_w=v_gdn_norm_w,
              w_out_odd=v_w_out_odd, w_up=v_w_up, w_down=v_w_down)
    me = _me()[3]
    T = x.shape[1]

    cols = lambda g: jnp.transpose(g, (1, 0, 2)).reshape(g.shape[1], -1)
    rows = lambda g: g.reshape(-1, g.shape[2])
    wide = lambda g: _odd_assemble(g, "w_in_odd_assemble")
    odd_shard = jnp.pad(w_in_odd[0].astype(bf16), ((0, 0), (0, ODD_SHARD_PAD - ODD_SHARD)))
    as_is = lambda g: g
    gather = dict(
        w_in_even=(w_in_even[0].astype(bf16), cols), lru_conv_w=(lru_conv_w[0], cols),
        w_out_even=(w_out_even[0].astype(bf16), rows), w_up0=(w_up[0].astype(bf16), as_is), w_down0=(w_down[0].astype(bf16), rows),
        w_in_odd=(odd_shard, wide), gdn_conv_w=(gdn_conv_w[0], cols),
        w_out_odd=(w_out_odd[0].astype(bf16), rows), w_up1=(w_up[1].astype(bf16), as_is), w_down1=(w_down[1].astype(bf16), rows))
    handles, tok = _send_start([s for s, _ in gather.values()], False, "gather_start")
    handles = dict(zip(gather, handles))
    full = {}

    def weight(name, after):
        if name not in full:
            shard, landed = _send_wait(handles[name], False, after, f"gather_wait_{name}")
            full[name] = gather[name][1](lax.dynamic_update_slice_in_dim(landed, shard[None], me, 0))
        return full[name]

    P = {k: Pw[k] for k in ("mlp_norm_w", "final_norm_w")}
    P["mixer_norm_w"] = mixer_norm_w + tok[0:1, 0:1]
    for k in ("lru_w_r", "lru_w_i", "lru_conv_b", "lru_b_r", "lru_b_i", "lru_lambda", "gdn_a_log", "gdn_dt_bias", "gdn_norm_w"):
        P[k] = Pw[k][0]

    sent = {}

    def sink(grads):
        hs, token = _send_start(list(grads.values()), True, "grads_start_" + "_".join(grads))
        sent.update(zip(grads, hs))
        return token

    loss, dx, G = _local_step(x[0], loss_target[0], P, weight, sink)
    small_g = [G[k].reshape(Pw[k].shape) for k in _SMALL] + [G["lru_conv_w"], G["gdn_conv_w"]]
    packed = _pack(small_g)
    sink(dict(w_in_even=G["w_in_even"], small=jnp.broadcast_to(packed[None], (N_DEV,) + packed.shape)))

    def received(name, after=dx):
        g, landed = _send_wait(sent[name], True, after, f"grads_wait_{name}")
        return lax.dynamic_update_slice_in_dim(landed, lax.dynamic_slice_in_dim(g, me, 1, 0), me, 0)

    out = {}
    nff = D_FF // N_DEV

    def whole(name, gs):
        out[name] = tuple(_adamw(Pw[name], gs, Pm[name], Pv[name], f"adamw_{name}", layer=0))

    def layers(name):
        res = None
        for l in range(2):
            res = _adamw(Pw[name], received(f"{name}{l}"), Pm[name], Pv[name], f"adamw_{name}{l}", layer=l, prev=res)
        out[name] = tuple(res)

    layers("w_up")
    layers("w_down")
    whole("w_out_odd", received("w_out_odd"))
    whole("w_in_odd", received("w_in_odd"))
    whole("w_out_even", received("w_out_even"))
    whole("w_in_even", received("w_in_even", out["w_out_even"][1]))
    small_shapes = [Pw[k].shape for k in _SMALL]
    pw, pm, pv = (_pack([Q[k] for k in _SMALL]) for Q in (Pw, Pm, Pv))
    sg, sd, sm, sv = _adamw(pw, received("small", out["w_in_even"][1]), pm, pv, "adamw_small")
    for arrs_i, packed_out in enumerate((sg, sd, sm, sv)):
        for k, a in zip(_SMALL, _unpack(packed_out, small_shapes)):
            out.setdefault(k, [None] * 4)[arrs_i] = a
    n_small = sum(int(np.prod(s)) for s in small_shapes)
    gflat = sg.reshape(-1)
    g_lconv = gflat[n_small:n_small + CONV_K * LRU_WIDTH].reshape(CONV_K, LRU_WIDTH)
    g_gconv = gflat[n_small + CONV_K * LRU_WIDTH:n_small + CONV_K * (LRU_WIDTH + 3072)].reshape(CONV_K, 3072)
    whole("lru_conv_w", lax.dynamic_slice_in_dim(g_lconv, me * 64, 64, axis=1)[None])
    whole("gdn_conv_w", lax.dynamic_slice_in_dim(g_gconv, me * 384, 384, axis=1)[None])

    names = ["mixer_norm_w", "mlp_norm_w", "final_norm_w", "w_in_even", "lru_conv_w", "lru_conv_b", "lru_w_r", "lru_b_r",
             "lru_w_i", "lru_b_i", "lru_lambda", "w_out_even", "w_in_odd", "gdn_conv_w", "gdn_a_log", "gdn_dt_bias",
             "gdn_norm_w", "w_out_odd", "w_up", "w_down"]
    total = lax.psum(loss[0, 0], ("x", "y", "c"))
    res = [total, dx[None]]
    for j in range(4):
        res += [out[k][j] for k in names]
    return tuple(res)
```

```python
import math

import numpy as np
import jax
import jax.numpy as jnp
from jax import lax
from jax.experimental import pallas as pl
from jax.experimental.pallas import tpu as pltpu

f32 = jnp.float32
bf16 = jnp.bfloat16

N_DEV = 8
D_MODEL = 1024
D_FF = 4096
EPS = 1e-6
RET_HEADS = 4
RET_CHUNK = 128
RET_STEP = 4
ROPE_THETA = 10000.0
LRU_WIDTH = 512
LRU_C = 8.0
GDN_HEADS = 8
GDN_CHUNK = 64
GDN_STEP = 2
HEAD_DIM = 128
ODD_IN = 4112
ODD_PAD = 4224
ODD_SHARD = ODD_IN // N_DEV
ODD_SHARD_PAD = 640
ADAM_LR, ADAM_B1, ADAM_B2, ADAM_EPS, ADAM_WD, ADAM_STEP = 0.001, 0.9, 0.999, 1e-08, 0.01, 10
VMEM_LIMIT = 56 * 1024 * 1024

_NN = (((1,), (0,)), ((), ()))
_NT = (((1,), (1,)), ((), ()))
_TN = (((0,), (0,)), ((), ()))
MESH = pl.DeviceIdType.MESH


def _cparams(sem):
    return pltpu.CompilerParams(dimension_semantics=sem, vmem_limit_bytes=VMEM_LIMIT)


def _dot(a, b, dn):
    return lax.dot_general(a.astype(bf16), b.astype(bf16), dn, preferred_element_type=f32)


def _dot01(a01, b, dn):
    a = a01.astype(bf16)
    b0 = b.astype(bf16)
    r1 = b - b0.astype(f32)
    b1 = r1.astype(bf16)
    b2 = (r1 - b1.astype(f32)).astype(bf16)
    d = lambda q: lax.dot_general(a, q, dn, preferred_element_type=f32)
    return d(b0) + (d(b1) + d(b2))


def _sigmoid(x):
    return jax.nn.sigmoid(x)


def _silu(x):
    return x * _sigmoid(x)


def _dsilu(x):
    s = _sigmoid(x)
    return s * (1.0 + x * (1.0 - s))


def _softplus(x):
    return jnp.maximum(x, 0.0) + jnp.log1p(jnp.exp(-jnp.abs(x)))


_GELU_C = math.sqrt(2.0 / math.pi)


def _gelu(y):
    return 0.5 * y * (1.0 + jnp.tanh(_GELU_C * (y + 0.044715 * y * y * y)))


def _dgelu(y):
    t = jnp.tanh(_GELU_C * (y + 0.044715 * y * y * y))
    return 0.5 * (1.0 + t) + 0.5 * y * (1.0 - t * t) * _GELU_C * (1.0 + 3.0 * 0.044715 * y * y)


def _matmul(a, b, form, *, outs, name, epilogue=None, extras=(), tm=4096, tn=512, tk=1024, shard_cols=0, a_map=None):
    if form == "tn":
        K, M = a.shape
    else:
        M, K = a.shape
    if b.ndim == 3:
        assert form in ("nn", "nt"), name
        N = b.shape[1] if form == "nt" else N_DEV * b.shape[2]
        if form == "nn":
            tn = b.shape[2]
        else:
            tk = b.shape[2]
    else:
        N = b.shape[0] if form == "nt" else b.shape[1]
    ns = N // N_DEV
    if shard_cols:
        tn = ns * shard_cols
    tm, tn, tk = min(tm, M), min(tn, N), min(tk, K)
    assert M % tm == 0 and N % tn == 0 and K % tk == 0, (name, M, N, K, tm, tn, tk)
    nk = K // tk
    dn = {"nn": _NN, "nt": _NT, "tn": _TN}[form]
    if form == "tn":
        a_spec = pl.BlockSpec((tk, tm), lambda i, j, k: (k, i))
    else:
        a_spec = pl.BlockSpec((tm, tk), lambda i, j, k: (i, k))
    if b.ndim == 3:
        b_spec = (pl.BlockSpec((None, tn, tk), lambda i, j, k: (k, j, 0)) if form == "nt"
                  else pl.BlockSpec((None, tk, tn), lambda i, j, k: (j, k, 0)))
    elif form == "nt":
        b_spec = pl.BlockSpec((tn, tk), lambda i, j, k: (j, k))
    else:
        b_spec = pl.BlockSpec((tk, tn), lambda i, j, k: (k, j))
    e_spec = pl.BlockSpec((tm, tn), lambda i, j, k: (i, j))
    v_spec = pl.BlockSpec((1, tn), lambda i, j, k: (0, j))
    if shard_cols:
        o_spec = pl.BlockSpec((shard_cols, tm, ns), lambda i, j, k: (j, i, 0))
        o_shape = (N_DEV, M, ns)
    else:
        o_spec = e_spec
        o_shape = (M, N)
    n_ex = len(extras)
    sums = [isinstance(o, tuple) for o in outs]
    assert not any(sums) or tn == N, name

    def finish(acc, ex, o_refs, row_tile):
        vals = (acc,) if epilogue is None else epilogue(acc, *[e[...] for e in ex])
        for r, v, is_sum in zip(o_refs, vals, sums):
            if is_sum:
                @pl.when(row_tile == 0)
                def _(r=r, v=v):
                    r[...] = v.astype(r.dtype)

                @pl.when(row_tile > 0)
                def _(r=r, v=v):
                    r[...] += v.astype(r.dtype)
            elif shard_cols:
                for s in range(shard_cols):
                    r[s] = v[:, s * ns:(s + 1) * ns].astype(r.dtype)
            else:
                r[...] = v.astype(r.dtype)

    def prod(a_ref, b_ref):
        av = a_ref[...]
        return _dot(av if a_map is None else a_map(av), b_ref[...], dn)

    def body_one(*refs):
        finish(prod(*refs[:2]), refs[2:2 + n_ex], refs[2 + n_ex:], pl.program_id(0))

    def body_acc(*refs):
        a_ref, b_ref = refs[:2]
        acc = refs[-1]
        k = pl.program_id(2)
        row_tile = pl.program_id(0)

        @pl.when(k == 0)
        def _():
            acc[...] = prod(a_ref, b_ref)

        @pl.when((k > 0) & (k < nk - 1))
        def _():
            acc[...] += prod(a_ref, b_ref)

        @pl.when(k == nk - 1)
        def _():
            finish(acc[...] + prod(a_ref, b_ref), refs[2:2 + n_ex], refs[2 + n_ex:-1], row_tile)

    return pl.pallas_call(
        body_one if nk == 1 else body_acc, grid=(M // tm, N // tn, nk),
        in_specs=[a_spec, b_spec] + [v_spec if e.shape[0] == 1 else e_spec for e in extras],
        out_specs=[v_spec if s else o_spec for s in sums],
        out_shape=[jax.ShapeDtypeStruct((1, N), o[1]) if s else jax.ShapeDtypeStruct(o_shape, o) for o, s in zip(outs, sums)],
        scratch_shapes=[] if nk == 1 else [pltpu.VMEM((tm, tn), f32)],
        compiler_params=_cparams(("arbitrary" if any(sums) else "parallel", "parallel", "arbitrary")), name=name,
    )(a, b, *extras)


def _rms_fwd(x, w, name):
    T, D = x.shape
    tt = min(512, T)

    def body(x_ref, w_ref, h_ref):
        xv = x_ref[...]
        r = lax.rsqrt(jnp.mean(xv * xv, axis=1, keepdims=True) + EPS)
        h_ref[...] = (xv * r * w_ref[...]).astype(bf16)

    return pl.pallas_call(
        body, grid=(T // tt,),
        in_specs=[pl.BlockSpec((tt, D), lambda i: (i, 0)), pl.BlockSpec((1, D), lambda i: (0, 0))],
        out_specs=pl.BlockSpec((tt, D), lambda i: (i, 0)),
        out_shape=jax.ShapeDtypeStruct((T, D), bf16),
        compiler_params=_cparams(("parallel",)), name=name,
    )(x, w)


def _residual_rms_ep(acc, res, w):
    x = res + acc
    r = lax.rsqrt(jnp.mean(x * x, axis=1, keepdims=True) + EPS)
    return x, x * r * w


_RMS_BWD_OUTS = [f32, bf16, ("sum", f32)]


def _rms_bwd_ep(dh, x, dres, w):
    r = lax.rsqrt(jnp.mean(x * x, axis=1, keepdims=True) + EPS)
    xn = x * r
    dhw = dh * w
    dx = dres + r * (dhw - xn * jnp.mean(dhw * xn, axis=1, keepdims=True))
    return dx, dx, jnp.sum(dh * xn, axis=0, keepdims=True)


_LOSS_OUTS = [("sum", f32), f32, bf16, ("sum", f32)]


def _loss_ep(acc, res, w, tgt):
    x = res + acc
    D = x.shape[1]
    r = lax.rsqrt(jnp.mean(x * x, axis=1, keepdims=True) + EPS)
    xn = x * r
    e = xn * w - tgt
    loss = 0.5 * jnp.sum(jnp.mean(e * e, axis=1, keepdims=True), axis=0, keepdims=True)
    dy = e * (1.0 / D)
    dyw = dy * w
    dx = r * (dyw - xn * jnp.mean(dyw * xn, axis=1, keepdims=True))
    return jnp.broadcast_to(loss, (1, D)), dx, dx, jnp.sum(dy * xn, axis=0, keepdims=True)


def _ret_tables():
    H, C = RET_HEADS, RET_CHUNK
    lg = np.log1p(-np.exp2(-5.0 - np.arange(H, dtype=np.float32))).astype(np.float32)
    idx = np.arange(C, dtype=np.float32)
    diff = idx[:, None] - idx[None, :]
    causal = diff >= 0
    dm = np.where(causal[None], np.exp(lg[:, None, None] * np.where(causal, diff, 0.0)[None]), 0.0)
    qd = np.exp(lg[:, None] * (idx[None, :] + 1.0))
    kd = np.exp(lg[:, None] * (C - 1.0 - idx[None, :]))
    cg = np.exp(lg * C)
    tab = np.zeros((H, 4, C, HEAD_DIM), np.float32)
    tab[:, 0] = dm
    tab[:, 1] = qd[:, :, None]
    tab[:, 2] = kd[:, :, None]
    tab[:, 3] = cg[:, None, None]
    return jnp.asarray(tab)


def _rope_tables(T):
    half = HEAD_DIM // 2
    inv = ROPE_THETA ** (-jnp.arange(half, dtype=f32) / half)
    ang = jnp.arange(T, dtype=jnp.int32).astype(f32)[:, None] * inv[None, :]
    c, s = jnp.cos(ang), jnp.sin(ang)
    return jnp.concatenate([c, c], axis=1), jnp.concatenate([-s, s], axis=1)


def _rope(x, cos, sin):
    return x * cos + pltpu.roll(x, HEAD_DIM // 2, 1) * sin


def _unrope(y, cos, sin):
    return y * cos + pltpu.roll(y * sin, HEAD_DIM // 2, 1)


def _stack_heads(ref, H, f=None):
    parts = [ref[:, h * HEAD_DIM:(h + 1) * HEAD_DIM] for h in range(H)]
    return jnp.stack(parts if f is None else [f(a) for a in parts])


def _ret_fwd(p, cos, sin, tab, name):
    T = p.shape[0]
    C, H = RET_CHUNK, RET_HEADS
    N = T // C
    K = min(RET_STEP, N)
    NS = N // K
    scale = HEAD_DIM ** -0.5

    def body(q_ref, k_ref, v_ref, g_ref, c_ref, s_ref, t_ref, y_ref, o_ref, sp_ref, st):
        @pl.when(pl.program_id(0) == 0)
        def _():
            st[...] = jnp.zeros_like(st)

        dm, qd, kd, cg = t_ref[:, 0], t_ref[:, 1], t_ref[:, 2], t_ref[:, 3]
        S = st[...]
        for c in range(K):
            rows = pl.ds(c * C, C)
            cos_, sin_ = c_ref[rows, :], s_ref[rows, :]
            rot = lambda a: _rope(a, cos_, sin_)
            q = _stack_heads(q_ref.at[rows, :], H, rot)
            k = _stack_heads(k_ref.at[rows, :], H, rot) * scale
            v = _stack_heads(v_ref.at[rows, :], H)
            P = _dot(q, k, _NT3) * dm
            o = _dot(P, v, _NN3) + _dot(q * qd, S, _NN3)
            sp_ref[c] = S
            S = cg * S + _dot(k * kd, v, _TN3)
            r = lax.rsqrt(jnp.mean(o * o, axis=2, keepdims=True) + EPS)
            y = o * r * _silu(_stack_heads(g_ref.at[rows, :], H))
            for h in range(H):
                o_ref[rows, h * HEAD_DIM:(h + 1) * HEAD_DIM] = o[h]
                y_ref[rows, h * HEAD_DIM:(h + 1) * HEAD_DIM] = y[h].astype(bf16)
        st[...] = S

    wide = lambda blk: pl.BlockSpec((K * C, H * HEAD_DIM), lambda n: (n, blk))
    tbl = pl.BlockSpec((K * C, HEAD_DIM), lambda n: (n, 0))
    return pl.pallas_call(
        body, grid=(NS,),
        in_specs=[wide(0), wide(1), wide(2), wide(3), tbl, tbl,
                  pl.BlockSpec((H, 4, C, HEAD_DIM), lambda n: (0, 0, 0, 0))],
        out_specs=[wide(0), wide(0), pl.BlockSpec((K, H, HEAD_DIM, HEAD_DIM), lambda n: (n, 0, 0, 0))],
        out_shape=[jax.ShapeDtypeStruct((T, D_MODEL), bf16), jax.ShapeDtypeStruct((T, H * HEAD_DIM), f32),
                   jax.ShapeDtypeStruct((N, H, HEAD_DIM, HEAD_DIM), f32)],
        scratch_shapes=[pltpu.VMEM((H, HEAD_DIM, HEAD_DIM), f32)],
        compiler_params=_cparams(("arbitrary",)), name=name,
    )(p, p, p, p, cos, sin, tab)


def _ret_bwd(p, cos, sin, tab, o_raw, sprev, dmix, name):
    T = p.shape[0]
    C, H = RET_CHUNK, RET_HEADS
    N = T // C
    K = min(RET_STEP, N)
    NS = N // K
    scale = HEAD_DIM ** -0.5
    W = H * HEAD_DIM

    def body(q_ref, k_ref, v_ref, g_ref, c_ref, s_ref, t_ref, o_ref, sp_ref, dy_ref, d_ref, dst):
        @pl.when(pl.program_id(0) == 0)
        def _():
            dst[...] = jnp.zeros_like(dst)

        dm, qd, kd, cg = t_ref[:, 0], t_ref[:, 1], t_ref[:, 2], t_ref[:, 3]
        dS1 = dst[...]
        for c in reversed(range(K)):
            rows = pl.ds(c * C, C)
            cos_, sin_ = c_ref[rows, :], s_ref[rows, :]
            rot = lambda a: _rope(a, cos_, sin_)
            q = _stack_heads(q_ref.at[rows, :], H, rot)
            k = _stack_heads(k_ref.at[rows, :], H, rot) * scale
            v = _stack_heads(v_ref.at[rows, :], H)
            g = _stack_heads(g_ref.at[rows, :], H)
            S = sp_ref[c]
            o = _stack_heads(o_ref.at[rows, :], H)
            dy = _stack_heads(dy_ref.at[rows, :], H)
            r = lax.rsqrt(jnp.mean(o * o, axis=2, keepdims=True) + EPS)
            nrm = o * r
            dn = dy * _silu(g)
            dg = dy * nrm * _dsilu(g)
            do = r * (dn - nrm * jnp.mean(dn * nrm, axis=2, keepdims=True))
            P = _dot(q, k, _NT3) * dm
            dP = _dot(do, v, _NT3) * dm
            dq = _dot(dP, k, _NN3) + _dot(do, S, _NT3) * qd
            dk = (_dot(dP, q, _TN3) + _dot(v, dS1, _NT3) * kd) * scale
            dv = _dot(P, do, _TN3) + _dot(k * kd, dS1, _NN3)
            dS1 = cg * dS1 + _dot(q * qd, do, _TN3)
            for h in range(H):
                d_ref[rows, h * HEAD_DIM:(h + 1) * HEAD_DIM] = _unrope(dq[h], cos_, sin_).astype(bf16)
                d_ref[rows, W + h * HEAD_DIM:W + (h + 1) * HEAD_DIM] = _unrope(dk[h], cos_, sin_).astype(bf16)
                d_ref[rows, 2 * W + h * HEAD_DIM:2 * W + (h + 1) * HEAD_DIM] = dv[h].astype(bf16)
                d_ref[rows, 3 * W + h * HEAD_DIM:3 * W + (h + 1) * HEAD_DIM] = dg[h].astype(bf16)
        dst[...] = dS1

    rev = lambda blk: pl.BlockSpec((K * C, W), lambda n: (NS - 1 - n, blk))
    tbl = pl.BlockSpec((K * C, HEAD_DIM), lambda n: (NS - 1 - n, 0))
    return pl.pallas_call(
        body, grid=(NS,),
        in_specs=[rev(0), rev(1), rev(2), rev(3), tbl, tbl,
                  pl.BlockSpec((H, 4, C, HEAD_DIM), lambda n: (0, 0, 0, 0)), rev(0),
                  pl.BlockSpec((K, H, HEAD_DIM, HEAD_DIM), lambda n: (NS - 1 - n, 0, 0, 0)), rev(0)],
        out_specs=pl.BlockSpec((K * C, 4 * W), lambda n: (NS - 1 - n, 0)),
        out_shape=jax.ShapeDtypeStruct((T, 6 * W), bf16),
        scratch_shapes=[pltpu.VMEM((H, HEAD_DIM, HEAD_DIM), f32)],
        compiler_params=_cparams(("arbitrary",)), name=name,
    )(p, p, p, p, cos, sin, tab, o_raw, sprev, dmix)


CONV_K = 4
CONV_W = 512
PAD = 8
SUB_R = 64


def _conv_fwd(x, col_off, w, b, act, name):
    T = x.shape[0]
    C = w.shape[1]
    G = C // CONV_W
    tt = min(512, T)
    NT = T // tt
    has_b = b is not None

    def body(*refs):
        if has_b:
            x_ref, w_ref, b_ref, y_ref, pad = refs
        else:
            x_ref, w_ref, y_ref, pad = refs
        t = pl.program_id(1)

        @pl.when(t == 0)
        def _():
            pad[pl.ds(0, PAD), :] = jnp.zeros((PAD, CONV_W), f32)

        pad[pl.ds(PAD, tt), :] = x_ref[...]
        for g in range(CONV_W // 128):
            ls = slice(g * 128, (g + 1) * 128)
            wv = w_ref[:, ls]
            for c in range(tt // SUB_R):
                r0 = c * SUB_R
                y = wv[0:1, :] * pad[pl.ds(PAD - 3 + r0, SUB_R), ls]
                for kk in range(1, CONV_K):
                    y = y + wv[kk:kk + 1, :] * pad[pl.ds(PAD - 3 + kk + r0, SUB_R), ls]
                if has_b:
                    y = y + b_ref[:, ls]
                y_ref[pl.ds(r0, SUB_R), ls] = _silu(y) if act else y
        tail = pad[pl.ds(tt, PAD), :]
        pad[pl.ds(0, PAD), :] = tail

    in_specs = [pl.BlockSpec((tt, CONV_W), lambda g, t: (t, col_off + g)),
                pl.BlockSpec((CONV_K, CONV_W), lambda g, t: (0, g))]
    args = [x, w]
    if has_b:
        in_specs.append(pl.BlockSpec((1, CONV_W), lambda g, t: (0, g)))
        args.append(b)
    return pl.pallas_call(
        body, grid=(G, NT), in_specs=in_specs,
        out_specs=pl.BlockSpec((tt, CONV_W), lambda g, t: (t, g)),
        out_shape=jax.ShapeDtypeStruct((T, C), f32),
        scratch_shapes=[pltpu.VMEM((tt + PAD, CONV_W), f32)],
        compiler_params=_cparams(("parallel", "arbitrary")), name=name,
    )(*args)


def _conv_bwd(x, col_off, w, b, act, dout, dp, name):
    T = x.shape[0]
    C = w.shape[1]
    G = C // CONV_W
    tt = min(512, T)
    NT = T // tt
    has_b = b is not None

    def body(*refs):
        if has_b:
            x_ref, xp_ref, w_ref, b_ref, d_ref, dp_in, dx_ref, dw_ref, db_ref, pad, dpad = refs
        else:
            x_ref, xp_ref, w_ref, d_ref, dp_in, dx_ref, dw_ref, db_ref, pad, dpad = refs
        t = pl.program_id(1)
        first_tile = t == NT - 1

        @pl.when(t == 0)
        def _():
            dpad[pl.ds(tt, PAD), :] = jnp.zeros((PAD, CONV_W), f32)
            dw_ref[...] = jnp.zeros_like(dw_ref)
            db_ref[...] = jnp.zeros_like(db_ref)

        pad[pl.ds(0, PAD), :] = jnp.where(first_tile, 0.0, xp_ref[...])
        pad[pl.ds(PAD, tt), :] = x_ref[...]
        fold = lambda v: v.reshape(SUB_R // 8, 8, 128).sum(axis=0)
        for g in range(CONV_W // 128):
            ls = slice(g * 128, (g + 1) * 128)
            wv = w_ref[:, ls]
            acc = [jnp.zeros((8, 128), f32) for _ in range(CONV_K + 1)]
            for c in reversed(range(tt // SUB_R)):
                r0 = c * SUB_R
                xs = [pad[pl.ds(PAD - 3 + kk + r0, SUB_R), ls] for kk in range(CONV_K)]
                dy = d_ref[pl.ds(r0, SUB_R), ls]
                if act:
                    y = wv[0:1, :] * xs[0]
                    for kk in range(1, CONV_K):
                        y = y + wv[kk:kk + 1, :] * xs[kk]
                    if has_b:
                        y = y + b_ref[:, ls]
                    dy = dy * _dsilu(y)
                dpad[pl.ds(r0, SUB_R), ls] = dy
                dx = wv[3:4, :] * dy
                for j in range(1, CONV_K):
                    dx = dx + wv[3 - j:4 - j, :] * dpad[pl.ds(r0 + j, SUB_R), ls]
                dx_ref[pl.ds(r0, SUB_R), ls] = dx.astype(bf16)
                for kk in range(CONV_K):
                    acc[kk] = acc[kk] + fold(dy * xs[kk])
                acc[CONV_K] = acc[CONV_K] + fold(dy)
            for kk in range(CONV_K):
                dw_ref[kk:kk + 1, ls] += jnp.sum(acc[kk], axis=0, keepdims=True)
            db_ref[:, ls] += jnp.sum(acc[CONV_K], axis=0, keepdims=True)
        head = dpad[pl.ds(0, PAD), :]
        dpad[pl.ds(tt, PAD), :] = head

    rows8 = tt // PAD
    in_specs = [pl.BlockSpec((tt, CONV_W), lambda g, t: (NT - 1 - t, col_off + g)),
                pl.BlockSpec((PAD, CONV_W), lambda g, t: (jnp.maximum((NT - 1 - t) * rows8 - 1, 0), col_off + g)),
                pl.BlockSpec((CONV_K, CONV_W), lambda g, t: (0, g))]
    args = [x, x, w]
    if has_b:
        in_specs.append(pl.BlockSpec((1, CONV_W), lambda g, t: (0, g)))
        args.append(b)
    in_specs += [pl.BlockSpec((tt, CONV_W), lambda g, t: (NT - 1 - t, g)), pl.BlockSpec(memory_space=pl.ANY)]
    args += [dout, dp]
    return pl.pallas_call(
        body, grid=(G, NT), in_specs=in_specs,
        out_specs=[pl.BlockSpec((tt, CONV_W), lambda g, t: (NT - 1 - t, col_off + g)),
                   pl.BlockSpec((CONV_K, CONV_W), lambda g, t: (0, g)),
                   pl.BlockSpec((1, CONV_W), lambda g, t: (0, g))],
        out_shape=[jax.ShapeDtypeStruct(dp.shape, dp.dtype), jax.ShapeDtypeStruct((CONV_K, C), f32),
                   jax.ShapeDtypeStruct((1, C), f32)],
        input_output_aliases={len(args) - 1: 0},
        scratch_shapes=[pltpu.VMEM((tt + PAD, CONV_W), f32), pltpu.VMEM((tt + PAD, CONV_W), f32)],
        compiler_params=_cparams(("parallel", "arbitrary")), name=name,
    )(*args)


def _lru_gates(xc, wr, wi, br, bi, lam):
    r = _sigmoid(_dot(xc, wr, _NN) + br)
    i = _sigmoid(_dot(xc, wi, _NN) + bi)
    sp = _softplus(-lam)
    a = jnp.exp(-LRU_C * r * sp)
    mult = jnp.sqrt(1.0 - a * a)
    return r, i, sp, a, mult


def _lru_fwd(xc, p, y_off, wr, wi, br, bi, lam, mix, name):
    T = xc.shape[0]
    G = LRU_WIDTH // 128
    tt = min(512, T)
    NT = T // tt

    def body(x_ref, y_ref, wr_ref, wi_ref, br_ref, bi_ref, l_ref, mix_in, o_ref, h_ref, hc):
        t = pl.program_id(1)

        @pl.when(t == 0)
        def _():
            hc[...] = jnp.zeros_like(hc)

        x = x_ref[...]
        r, i, sp, a, mult = _lru_gates(x, wr_ref[...], wi_ref[...], br_ref[...], bi_ref[...], l_ref[...])
        row = lax.broadcasted_iota(jnp.int32, (tt, 128), 0)
        mult = jnp.where((row == 0) & (t == 0), 1.0, mult)
        U = x * i * mult
        A = a
        d = 1
        while d < tt:
            keep = row >= d
            Ush = jnp.where(keep, pltpu.roll(U, d, 0), 0.0)
            Ash = jnp.where(keep, pltpu.roll(A, d, 0), 1.0)
            U = A * Ush + U
            A = A * Ash
            d *= 2
        h = U + A * hc[0:1, :]
        h_ref[...] = h
        hc[...] = jnp.broadcast_to(h[tt - 1:tt, :], hc.shape)
        o_ref[...] = (h * _gelu(y_ref[...])).astype(bf16)

    tile = pl.BlockSpec((tt, 128), lambda g, t: (t, g))
    vec = pl.BlockSpec((1, 128), lambda g, t: (0, g))
    wsp = pl.BlockSpec((128, 128), lambda g, t: (g, g))
    return pl.pallas_call(
        body, grid=(G, NT),
        in_specs=[tile, pl.BlockSpec((tt, 128), lambda g, t: (t, y_off + g)), wsp, wsp, vec, vec, vec,
                  pl.BlockSpec(memory_space=pl.ANY)],
        out_specs=[pl.BlockSpec((tt, 128), lambda g, t: (t, G + g)), tile],
        out_shape=[jax.ShapeDtypeStruct(mix.shape, mix.dtype), jax.ShapeDtypeStruct((T, LRU_WIDTH), f32)],
        input_output_aliases={7: 0},
        scratch_shapes=[pltpu.VMEM((8, 128), f32)],
        compiler_params=_cparams(("parallel", "arbitrary")), name=name,
    )(xc, p, wr, wi, br, bi, lam, mix)


def _lru_bwd(xc, p, y_off, wr, wi, br, bi, lam, hs, dmix, d_off, dp, name):
    T = xc.shape[0]
    G = LRU_WIDTH // 128
    tt = min(512, T)
    NT = T // tt

    def body(x_ref, y_ref, wr_ref, wi_ref, br_ref, bi_ref, l_ref, h_ref, hp_ref, do_ref, dp_in,
             dx_ref, dy_ref, dwr_ref, dwi_ref, dbr_ref, dbi_ref, dl_ref, lc, an):
        t = pl.program_id(1)
        first_tile = t == NT - 1

        @pl.when(t == 0)
        def _():
            lc[...] = jnp.zeros_like(lc)
            an[...] = jnp.zeros_like(an)
            dwr_ref[...] = jnp.zeros_like(dwr_ref)
            dwi_ref[...] = jnp.zeros_like(dwi_ref)
            dbr_ref[...] = jnp.zeros_like(dbr_ref)
            dbi_ref[...] = jnp.zeros_like(dbi_ref)
            dl_ref[...] = jnp.zeros_like(dl_ref)

        x = x_ref[...]
        y = y_ref[...]
        wr, wi, lam_ = wr_ref[...], wi_ref[...], l_ref[...]
        r, i, sp, a, mult_raw = _lru_gates(x, wr, wi, br_ref[...], bi_ref[...], lam_)
        row = lax.broadcasted_iota(jnp.int32, (tt, 128), 0)
        t0 = (row == 0) & first_tile
        mult = jnp.where(t0, 1.0, mult_raw)
        h = h_ref[...]
        do = do_ref[...]
        dh = do * _gelu(y)
        dy_ref[...] = (do * h * _dgelu(y)).astype(bf16)
        B = jnp.where(row == tt - 1, an[0:1, :], pltpu.roll(a, tt - 1, 0))
        L = dh
        d = 1
        while d < tt:
            keep = row < tt - d
            Lsh = jnp.where(keep, pltpu.roll(L, tt - d, 0), 0.0)
            Bsh = jnp.where(keep, pltpu.roll(B, tt - d, 0), 1.0)
            L = L + B * Lsh
            B = B * Bsh
            d *= 2
        L = L + B * lc[0:1, :]
        lc[...] = jnp.broadcast_to(L[0:1, :], lc.shape)
        an[...] = jnp.broadcast_to(a[0:1, :], an.shape)
        hprev = jnp.where(first_tile, 0.0, hp_ref[...])[PAD - 1:PAD, :]
        hm1 = jnp.where(row == 0, hprev, pltpu.roll(h, 1, 0))
        da = L * hm1
        dxc = L * i * mult
        di = L * x * mult
        dmult = jnp.where(t0, 0.0, L * x * i)
        da = da - jnp.where(t0, 0.0, dmult * a / mult_raw)
        dlog_a = da * a
        dr = dlog_a * (-LRU_C) * sp
        dsp = jnp.sum(dlog_a * (-LRU_C) * r, axis=0, keepdims=True)
        dpr = dr * r * (1.0 - r)
        dpi = di * i * (1.0 - i)
        dx_ref[...] = dxc + _dot(dpr, wr, _NT) + _dot(dpi, wi, _NT)
        dwr_ref[0] += _dot(x, dpr, _TN)
        dwi_ref[0] += _dot(x, dpi, _TN)
        dbr_ref[...] += jnp.sum(dpr, axis=0, keepdims=True)
        dbi_ref[...] += jnp.sum(dpi, axis=0, keepdims=True)
        dl_ref[...] += dsp * (-_sigmoid(-lam_))

    rows8 = tt // PAD
    tile = pl.BlockSpec((tt, 128), lambda g, t: (NT - 1 - t, g))
    vec = pl.BlockSpec((1, 128), lambda g, t: (0, g))
    wsp = pl.BlockSpec((128, 128), lambda g, t: (g, g))
    wout = pl.BlockSpec((1, 128, 128), lambda g, t: (g, 0, 0))
    return pl.pallas_call(
        body, grid=(G, NT),
        in_specs=[tile, pl.BlockSpec((tt, 128), lambda g, t: (NT - 1 - t, y_off + g)), wsp, wsp, vec, vec, vec, tile,
                  pl.BlockSpec((PAD, 128), lambda g, t: (jnp.maximum((NT - 1 - t) * rows8 - 1, 0), g)),
                  pl.BlockSpec((tt, 128), lambda g, t: (NT - 1 - t, d_off + g)), pl.BlockSpec(memory_space=pl.ANY)],
        out_specs=[tile, pl.BlockSpec((tt, 128), lambda g, t: (NT - 1 - t, y_off + g)), wout, wout, vec, vec, vec],
        out_shape=[jax.ShapeDtypeStruct((T, LRU_WIDTH), f32), jax.ShapeDtypeStruct(dp.shape, dp.dtype),
                   jax.ShapeDtypeStruct((G, 128, 128), f32), jax.ShapeDtypeStruct((G, 128, 128), f32),
                   jax.ShapeDtypeStruct((1, LRU_WIDTH), f32), jax.ShapeDtypeStruct((1, LRU_WIDTH), f32),
                   jax.ShapeDtypeStruct((1, LRU_WIDTH), f32)],
        input_output_aliases={10: 1},
        scratch_shapes=[pltpu.VMEM((8, 128), f32), pltpu.VMEM((8, 128), f32)],
        compiler_params=_cparams(("parallel", "arbitrary")), name=name,
    )(xc, p, wr, wi, br, bi, lam, hs, hs, dmix, dp)


_NN3 = (((2,), (1,)), ((0,), (0,)))
_NT3 = (((2,), (2,)), ((0,), (0,)))
_TN3 = (((1,), (1,)), ((0,), (0,)))


def _heads(ref):
    return _stack_heads(ref, GDN_HEADS)


def _rowsum(x):
    H, C, L = x.shape
    return _dot(x.reshape(H * C, L), jnp.ones((L, HEAD_DIM), f32), _NN).reshape(H, C, HEAD_DIM)


def _gdn_chunk(qr, kr, v, ba, alog, dtb, S, saved=None):
    C, H = GDN_CHUNK, GDN_HEADS
    lane = lax.broadcasted_iota(jnp.int32, (C, 128), 1)
    lane3 = lax.broadcasted_iota(jnp.int32, (H, C, 128), 2)
    ri = lax.broadcasted_iota(jnp.int32, (C, C), 0)
    ci = lax.broadcasted_iota(jnp.int32, (C, C), 1)
    rowc = lax.broadcasted_iota(jnp.int32, (C, 1), 0)
    col = lambda m, j: jnp.sum(jnp.where(lane == j, m, 0.0), axis=1, keepdims=True)
    cols = lambda m, off: jnp.stack([col(m, off + h) for h in range(H)])
    ea = jnp.exp(alog)
    g_all = -ea * _softplus(ba + dtb)
    tri = (ri >= ci).astype(f32)
    G_all = _dot01(tri, g_all, _NN)
    beta = cols(_sigmoid(ba), 0)
    Gc = cols(G_all, H)
    rq = lax.rsqrt(_rowsum(qr * qr) + EPS)
    rk = lax.rsqrt(_rowsum(kr * kr) + EPS)
    qh, kn = qr * rq, kr * rk
    qn = qh * (HEAD_DIM ** -0.5)
    Grow = _dot01(jnp.ones((H, C, 128), f32), jnp.where(lane3 == 0, Gc, 0.0), _NT3)
    incl = ri >= ci
    Di = jnp.where(incl, jnp.exp(jnp.where(incl, Gc - Grow, 0.0)), 0.0)
    Ds = jnp.where(ri > ci, Di, 0.0)
    Gl = jnp.sum(jnp.where(rowc == C - 1, Gc, 0.0), axis=1, keepdims=True)
    eG = jnp.exp(Gc)
    eGl = jnp.exp(Gl - Gc)
    cd = jnp.exp(Gl)
    kb = kn * beta
    vb = v * beta
    Lm = _dot(kb, kn, _NT3) * Ds
    kbg = kb * eG
    QK = _dot(qn, kn, _NT3) * Di
    qg = qn * eG
    kg = kn * eGl
    if saved is None:
        same = lambda s: (ri // s) == (ci // s)
        Xd = jnp.where(same(8), -Lm, 0.0)
        Tinv = (ri == ci).astype(f32) + Xd
        Pw = Xd
        for _ in range(2):
            Pw = _dot(Pw, Pw, _NN3)
            Tinv = Tinv + _dot(Tinv, Pw, _NN3)
        for s in (8, 16, 32):
            off = jnp.where(same(2 * s) & jnp.logical_not(same(s)), Lm, 0.0)
            Tinv = Tinv - _dot(_dot(Tinv, off, _NN3), Tinv, _NN3)
        w = _dot(Tinv, kbg, _NN3)
        vn = _dot(Tinv, vb, _NN3) - _dot(w, S, _NN3)
        o = _dot(qg, S, _NN3) + _dot(QK, vn, _NN3)
        S1 = S * cd + _dot(kg, vn, _TN3)
    else:
        Tinv, vn, o = saved
        w = _dot(Tinv, kbg, _NN3)
        S1 = None
    return dict(beta=beta, g_all=g_all, rq=rq, rk=rk, qh=qh, kn=kn, qn=qn, Di=Di, Ds=Ds, eG=eG, eGl=eGl, cd=cd,
                kb=kb, vb=vb, Lm=Lm, Tinv=Tinv, kbg=kbg, w=w, QK=QK, qg=qg, kg=kg, vn=vn, o=o, S1=S1,
                lane=lane, ri=ri, ci=ci, rowc=rowc, ea=ea)


def _gdn_specs(T, rev):
    C = GDN_CHUNK
    H = GDN_HEADS
    K = min(GDN_STEP, T // C)
    NS = T // (C * K)
    nn = (lambda n: NS - 1 - n) if rev else (lambda n: n)
    wide = lambda blk: pl.BlockSpec((K * C, H * HEAD_DIM), lambda n: (nn(n), blk))
    one = lambda off: pl.BlockSpec((K * C, HEAD_DIM), lambda n: (nn(n), off))
    vec = pl.BlockSpec((1, 128), lambda n: (0, 0))
    st = lambda rows: pl.BlockSpec((K, H, rows, rows), lambda n: (nn(n), 0, 0, 0))
    return K, NS, wide, one, vec, st


def _gdn_fwd(qkv, p, alog, dtb, nw, name):
    T = qkv.shape[0]
    C, H = GDN_CHUNK, GDN_HEADS
    N = T // C
    K, NS, wide, one, vec, st_spec = _gdn_specs(T, False)

    def body(q_ref, k_ref, v_ref, z_ref, ba_ref, al_ref, dt_ref, nw_ref, y_ref, sp_ref, ti_ref, vn_ref, o_ref, st):
        @pl.when(pl.program_id(0) == 0)
        def _():
            st[...] = jnp.zeros_like(st)

        S = st[...]
        for c in range(K):
            rows = pl.ds(c * C, C)
            at = lambda ref: ref.at[rows, :]
            f = _gdn_chunk(_heads(at(q_ref)), _heads(at(k_ref)), _heads(at(v_ref)), ba_ref[rows, :], al_ref[...],
                           dt_ref[...], S)
            sp_ref[c] = S
            S = f["S1"]
            ti_ref[c] = f["Tinv"]
            o, vn = f["o"], f["vn"]
            r = lax.rsqrt(_rowsum(o * o) * (1.0 / HEAD_DIM) + EPS)
            y = o * r * nw_ref[...] * _silu(_heads(at(z_ref)))
            for h in range(H):
                sl = slice(h * HEAD_DIM, (h + 1) * HEAD_DIM)
                y_ref[rows, sl] = y[h].astype(bf16)
                vn_ref[rows, sl] = vn[h]
                o_ref[rows, sl] = o[h]
        st[...] = S

    wide_f32 = jax.ShapeDtypeStruct((T, H * HEAD_DIM), f32)
    return pl.pallas_call(
        body, grid=(NS,),
        in_specs=[wide(0), wide(1), wide(2), wide(3), one(4 * H), vec, vec, vec],
        out_specs=[wide(0), st_spec(HEAD_DIM), st_spec(C), wide(0), wide(0)],
        out_shape=[jax.ShapeDtypeStruct((T, H * HEAD_DIM), bf16), jax.ShapeDtypeStruct((N, H, HEAD_DIM, HEAD_DIM), f32),
                   jax.ShapeDtypeStruct((N, H, C, C), f32), wide_f32, wide_f32],
        scratch_shapes=[pltpu.VMEM((H, HEAD_DIM, HEAD_DIM), f32)],
        compiler_params=_cparams(("arbitrary",)), name=name,
    )(qkv, qkv, qkv, p, p, alog, dtb, nw)


def _gdn_bwd(qkv, p, alog, dtb, nw, sprev, tinv, vn_all, o_all, dy_all, name):
    T = qkv.shape[0]
    C, H = GDN_CHUNK, GDN_HEADS
    N = T // C
    K, NS, wide, one, vec, st_spec = _gdn_specs(T, True)
    rs = lambda m: jnp.sum(m, axis=2, keepdims=True)

    def put(ref, val, col=0):
        for h in range(H):
            ref[:, col + h * HEAD_DIM:col + (h + 1) * HEAD_DIM] = val[h].astype(ref.dtype)

    def body(q_ref, k_ref, v_ref, z_ref, ba_ref, al_ref, dt_ref, nw_ref, sp_ref, ti_ref, vn_ref, o_ref, dy_ref,
             dqkv_ref, dz_ref, dba_ref, dal_ref, ddt_ref, dnw_ref, dst):
        @pl.when(pl.program_id(0) == 0)
        def _():
            dst[...] = jnp.zeros_like(dst)
            dal_ref[...] = jnp.zeros_like(dal_ref)
            ddt_ref[...] = jnp.zeros_like(ddt_ref)
            dnw_ref[...] = jnp.zeros_like(dnw_ref)

        dS = dst[...]
        for c in reversed(range(K)):
            at = lambda ref, c=c: ref.at[pl.ds(c * C, C), :]
            dS = chunk(at(q_ref), at(k_ref), at(v_ref), at(z_ref), at(ba_ref), al_ref, dt_ref, nw_ref, sp_ref[c], ti_ref[c],
                       at(vn_ref), at(o_ref), at(dy_ref), at(dqkv_ref), at(dz_ref), at(dba_ref), dal_ref, ddt_ref, dnw_ref, dS)
        dst[...] = dS

    def chunk(q_ref, k_ref, v_ref, z_ref, ba_ref, al_ref, dt_ref, nw_ref, S, Tsaved, vn_ref, o_ref, dy_ref,
              dqkv_ref, dz_ref, dba_ref, dal_ref, ddt_ref, dnw_ref, dS1):
        ba, alog, dtb_, nwv = ba_ref[...], al_ref[...], dt_ref[...], nw_ref[...]
        v = _heads(v_ref)
        f = _gdn_chunk(_heads(q_ref), _heads(k_ref), v, ba, alog, dtb_, S, saved=(Tsaved, _heads(vn_ref), _heads(o_ref)))
        beta, kn, qn, kb, vb, Tinv, kbg = f["beta"], f["kn"], f["qn"], f["kb"], f["vb"], f["Tinv"], f["kbg"]
        eG, eGl, cd, Di, Ds, QK, vn, w_, qg, kg = (f["eG"], f["eGl"], f["cd"], f["Di"], f["Ds"], f["QK"], f["vn"],
                                                    f["w"], f["qg"], f["kg"])
        lane, ri, ci, rowc = f["lane"], f["ri"], f["ci"], f["rowc"]
        o = f["o"]
        z = _heads(z_ref)
        dy = _heads(dy_ref)
        r = lax.rsqrt(_rowsum(o * o) * (1.0 / HEAD_DIM) + EPS)
        nrm = o * r
        sz = _silu(z)
        dn = dy * nwv * sz
        put(dz_ref, dy * nrm * nwv * _dsilu(z))
        dnw_ref[...] += jnp.sum(jnp.sum(dy * nrm * sz, axis=0), axis=0, keepdims=True)
        do = r * (dn - nrm * (_rowsum(dn * nrm) * (1.0 / HEAD_DIM)))
        dcd = jnp.sum(jnp.sum(S * dS1, axis=2, keepdims=True), axis=1, keepdims=True)
        dkg = _dot(vn, dS1, _NT3)
        dvn = _dot(kg, dS1, _NN3) + _dot(QK, do, _TN3)
        dqg = _dot(do, S, _NT3)
        dQK = _dot(do, vn, _NT3)
        dw = -_dot(dvn, S, _NT3)
        dS0 = cd * dS1 + _dot(qg, do, _TN3) - _dot(w_, dvn, _TN3)
        dqn = dqg * eG
        dkn = dkg * eGl
        deGl = rs(dkg * kn)
        dQKr = dQK * Di
        E = dQK * QK
        dqn = dqn + _dot(dQKr, kn, _NN3)
        dkn = dkn + _dot(dQKr, qn, _TN3)
        dT = _dot(dvn, vb, _NT3) + _dot(dw, kbg, _NT3)
        dvb = _dot(Tinv, dvn, _TN3)
        dkbg = _dot(Tinv, dw, _TN3)
        dkb = dkbg * eG
        deG = rs(dqg * qn + dkbg * kb)
        dL = -_dot(_dot(Tinv, dT, _TN3), Tinv, _NT3)
        dKK = dL * Ds
        E = E + dL * f["Lm"]
        dkb = dkb + _dot(dKK, kn, _NN3)
        dkn = dkn + _dot(dKK, kb, _TN3) + dkb * beta
        dbeta = rs(dkb * kn + dvb * v)
        put(dqkv_ref, dvb * beta, 2 * H * HEAD_DIM)
        dG = rs(E) - rs(jnp.swapaxes(E, 1, 2)) + deG * eG - deGl * eGl
        dGl = jnp.sum(deGl * eGl, axis=1, keepdims=True) + dcd * cd
        dG = dG + jnp.where(rowc == C - 1, dGl, 0.0)
        qh = f["qh"]
        put(dqkv_ref, (HEAD_DIM ** -0.5) * f["rq"] * (dqn - qh * _rowsum(dqn * qh)))
        put(dqkv_ref, f["rk"] * (dkn - kn * _rowsum(dkn * kn)), H * HEAD_DIM)
        db = dbeta * beta * (1.0 - beta)
        db_all = jnp.where(lane == 0, db[0], 0.0)
        dG_all = jnp.where(lane == H, dG[0], 0.0)
        for h in range(1, H):
            db_all = db_all + jnp.where(lane == h, db[h], 0.0)
            dG_all = dG_all + jnp.where(lane == H + h, dG[h], 0.0)
        triu = (ri <= ci).astype(f32)
        dg_all = _dot01(triu, dG_all, _NN)
        da_all = dg_all * (-f["ea"]) * _sigmoid(ba + dtb_)
        dba_ref[...] = (db_all + da_all).astype(bf16)
        ddt_ref[...] += jnp.sum(da_all, axis=0, keepdims=True)
        dal_ref[...] += jnp.sum(dg_all * f["g_all"], axis=0, keepdims=True)
        return dS0

    small = jax.ShapeDtypeStruct((1, 128), f32)
    return pl.pallas_call(
        body, grid=(NS,),
        in_specs=[wide(0), wide(1), wide(2), wide(3), one(4 * H), vec, vec, vec, st_spec(HEAD_DIM), st_spec(C),
                  wide(0), wide(0), wide(0)],
        out_specs=[pl.BlockSpec((K * C, 3 * H * HEAD_DIM), lambda n: (NS - 1 - n, 0)), wide(3), one(0), vec, vec, vec],
        out_shape=[jax.ShapeDtypeStruct((T, 3 * H * HEAD_DIM), f32), jax.ShapeDtypeStruct((T, ODD_PAD), bf16),
                   jax.ShapeDtypeStruct((T, 128), bf16), small, small, small],
        scratch_shapes=[pltpu.VMEM((H, HEAD_DIM, HEAD_DIM), f32)],
        compiler_params=_cparams(("arbitrary",)), name=name,
    )(qkv, qkv, qkv, p, p, alog, dtb, nw, sprev, tinv, vn_all, o_all, dy_all)


def _lanes_from(x, s):
    return x if s % 128 == 0 else pltpu.roll(x, (128 - s) % 128, 1)


def _odd_assemble(g, name):
    R = g.shape[1]
    tr = min(256, R)
    n_blk = ODD_SHARD_PAD // 128

    def body(g_ref, o_ref):
        lane = lax.broadcasted_iota(jnp.int32, (tr, 128), 1)
        blk = lambda d, m: g_ref[d, :, m * 128:(m + 1) * 128]
        for gb in range(ODD_PAD // 128):
            c0 = 128 * gb
            if c0 >= ODD_IN:
                o_ref[:, c0:c0 + 128] = jnp.zeros((tr, 128), g.dtype)
                continue
            d0 = c0 // ODD_SHARD
            m0, sh = divmod(c0 - ODD_SHARD * d0, 128)
            take = min(128, ODD_SHARD * (d0 + 1) - c0)
            p = _lanes_from(blk(d0, m0), sh)
            if sh and m0 + 1 < n_blk:
                p = jnp.where(lane < 128 - sh, p, _lanes_from(blk(d0, m0 + 1), sh))
            if take < 128:
                nxt = pltpu.roll(blk(d0 + 1, 0), take, 1) if d0 + 1 < N_DEV else jnp.zeros((tr, 128), g.dtype)
                p = jnp.where(lane < take, p, nxt)
            o_ref[:, c0:c0 + 128] = p

    return pl.pallas_call(
        body, grid=(R // tr,),
        in_specs=[pl.BlockSpec((N_DEV, tr, ODD_SHARD_PAD), lambda i: (0, i, 0))],
        out_specs=pl.BlockSpec((tr, ODD_PAD), lambda i: (i, 0)),
        out_shape=jax.ShapeDtypeStruct((R, ODD_PAD), g.dtype),
        compiler_params=_cparams(("parallel",)), name=name,
    )(g)


def _odd_split(w, name):
    R = w.shape[0]
    tr = min(256, R)

    def body(w_ref, o_ref):
        lane = lax.broadcasted_iota(jnp.int32, (tr, 128), 1)
        blk = lambda gb: w_ref[:, gb * 128:(gb + 1) * 128]
        for d in range(N_DEV):
            for m in range(ODD_SHARD_PAD // 128):
                g0, sh = divmod(ODD_SHARD * d + 128 * m, 128)
                p = _lanes_from(blk(g0), sh)
                if sh and g0 + 1 < ODD_PAD // 128:
                    p = jnp.where(lane < 128 - sh, p, _lanes_from(blk(g0 + 1), sh))
                real = ODD_SHARD - 128 * m
                if real < 128:
                    p = jnp.where(lane < real, p, jnp.zeros_like(p))
                o_ref[d, :, m * 128:(m + 1) * 128] = p

    return pl.pallas_call(
        body, grid=(R // tr,),
        in_specs=[pl.BlockSpec((tr, ODD_PAD), lambda i: (i, 0))],
        out_specs=pl.BlockSpec((N_DEV, tr, ODD_SHARD_PAD), lambda i: (0, i, 0)),
        out_shape=jax.ShapeDtypeStruct((N_DEV, R, ODD_SHARD_PAD), w.dtype),
        compiler_params=_cparams(("parallel",)), name=name,
    )(w)


def _adamw(w, gs, m, v, name, layer=None, prev=None):
    R, Cc = w.shape[-2:]
    S = gs.shape[0]
    tr = R
    for cand in (256, 128, 64, 32, 16, 8):
        if R % cand == 0 and R > cand:
            tr = cand
            break
    c1 = 1.0 - ADAM_B1 ** ADAM_STEP
    c2 = 1.0 - ADAM_B2 ** ADAM_STEP

    def body(w_ref, g_ref, m_ref, v_ref, *rest):
        go_ref, d_ref, mo_ref, vo_ref = rest[-4:]
        g = g_ref[0].astype(f32)
        for s in range(1, S):
            g = g + g_ref[s].astype(f32)
        mn = ADAM_B1 * m_ref[...] + (1.0 - ADAM_B1) * g
        vn = ADAM_B2 * v_ref[...] + (1.0 - ADAM_B2) * (g * g)
        go_ref[...] = g
        mo_ref[...] = mn
        vo_ref[...] = vn
        d_ref[...] = -ADAM_LR * ((mn / c1) / (jnp.sqrt(vn / c2) + ADAM_EPS) + ADAM_WD * w_ref[...])

    tc = Cc if gs.shape[-1] == Cc else 128
    if layer is None:
        blk = pl.BlockSpec((tr, tc), lambda i, j: (i, j))
    else:
        blk = pl.BlockSpec((None, tr, tc), lambda i, j: (layer, i, j))
    out = jax.ShapeDtypeStruct(w.shape, f32)
    carried = [] if prev is None else list(prev)
    return pl.pallas_call(
        body, grid=(R // tr, pl.cdiv(Cc, tc)),
        in_specs=[blk, pl.BlockSpec((S, tr, tc), lambda i, j: (0, i, j)), blk, blk]
        + [pl.BlockSpec(memory_space=pl.ANY)] * len(carried),
        out_specs=[blk] * 4, out_shape=[out] * 4,
        input_output_aliases={4 + j: j for j in range(len(carried))},
        compiler_params=_cparams(("parallel", "parallel")), name=name,
    )(w, gs, m, v, *carried)


def _me():
    x, y, c = lax.axis_index("x"), lax.axis_index("y"), lax.axis_index("c")
    return x, y, c, 4 * x + 2 * y + c


def _peer(k):
    x, y, c, _ = _me()
    px = 1 - x if k & 4 else x
    py = 1 - y if k & 2 else y
    pc = 1 - c if k & 1 else c
    return (px, py, pc), 4 * px + 2 * py + pc


_HBM = pl.BlockSpec(memory_space=pltpu.HBM)
_SEM = pl.BlockSpec(memory_space=pltpu.SEMAPHORE)
_EFFECT = pltpu.SideEffectType.DATAFLOW_SIDE_EFFECTING


def _copy(src, land, ssem, rsem, k, blocked, landing_slot_of_peer):
    pid, pidx = _peer(k)
    slot = pidx if landing_slot_of_peer else _me()[3]
    return pltpu.make_async_remote_copy(src_ref=src.at[pidx] if blocked else src, dst_ref=land.at[slot],
                                        send_sem=ssem.at[k - 1], recv_sem=rsem.at[k - 1], device_id=pid, device_id_type=MESH)


def _send_start(srcs, blocked, name):
    n = len(srcs)
    lands = [lax.empty(a.shape if blocked else (N_DEV,) + a.shape, a.dtype) for a in srcs]

    def body(*refs):
        src, land, sems, token = refs[:n], refs[n:2 * n], refs[2 * n:4 * n], refs[-1]
        for i in range(n):
            for k in range(1, N_DEV):
                _copy(src[i], land[i], sems[2 * i], sems[2 * i + 1], k, blocked, False).start()
        token[...] = jnp.zeros_like(token)

    sem = pltpu.SemaphoreType.DMA((N_DEV - 1,))
    hbm = lambda a: pltpu.with_memory_space_constraint(a, pltpu.HBM)
    res = pl.pallas_call(
        body, name=name,
        out_shape=tuple([sem] * (2 * n)) + tuple(pltpu.HBM(a.shape, a.dtype) for a in srcs + lands)
        + (jax.ShapeDtypeStruct((8, 128), f32),),
        in_specs=(_HBM,) * (2 * n),
        out_specs=(_SEM,) * (2 * n) + (_HBM,) * (2 * n) + (pl.BlockSpec(memory_space=pltpu.VMEM),),
        input_output_aliases={j: 2 * n + j for j in range(2 * n)},
        compiler_params=pltpu.CompilerParams(has_side_effects=_EFFECT),
    )(*[hbm(a) for a in srcs], *[hbm(a) for a in lands])
    handles = [(res[2 * i], res[2 * i + 1], res[2 * n + i], res[3 * n + i]) for i in range(n)]
    return handles, res[-1]


def _send_wait(handle, blocked, after, name):
    ssem, rsem, src, land = handle

    def body(src_ref, land_ref, ssem_ref, rsem_ref, after_ref, src_out, land_out):
        for k in range(1, N_DEV):
            cp = _copy(src_ref, land_ref, ssem_ref, rsem_ref, k, blocked, True)
            cp.wait_send()
            cp.wait_recv()

    return pl.pallas_call(
        body, name=name, out_shape=(pltpu.HBM(src.shape, src.dtype), pltpu.HBM(land.shape, land.dtype)),
        in_specs=(_HBM, _HBM, _SEM, _SEM, pl.BlockSpec(memory_space=pl.ANY)), out_specs=(_HBM, _HBM),
        input_output_aliases={0: 0, 1: 1}, compiler_params=pltpu.CompilerParams(has_side_effects=_EFFECT),
    )(src, land, ssem, rsem, after)


def _block_diag(w):
    nb, bs = w.shape[0], w.shape[1]
    eye = jnp.eye(nb, dtype=w.dtype)
    return (eye[:, None, :, None] * w[:, :, None, :]).reshape(nb * bs, nb * bs)


def _diag_blocks(d):
    return jnp.stack([d[g, s * 64:(s + 1) * 64, s * 64:(s + 1) * 64] for g in range(4) for s in range(2)])


_SQUARE_TILES = dict(tm=1024, tn=1024, tk=1024)


def _mlp_fwd(x, hm, wu, wd, tag, epilogue, extras, outs):
    (r,) = _matmul(hm, wu, "nn", outs=[bf16], epilogue=lambda acc: (jnp.maximum(acc, 0.0),), name=f"mlp_up_{tag}")
    res = _matmul(r, wd, "nn", outs=outs, extras=(x,) + tuple(extras), epilogue=epilogue, a_map=jnp.square,
                  name=f"mlp_down_{tag}", **_SQUARE_TILES)
    return res, (hm, r)


def _mlp_bwd(x, nw, wu, wd, saved, dxo, dxo_b, tag):
    hm, r = saved
    (du,) = _matmul(dxo_b, wd, "nt", outs=[bf16], extras=(r,), epilogue=lambda acc, rr: (acc * (2.0 * rr.astype(f32)),),
                    name=f"mlp_dact_{tag}")
    (dwd,) = _matmul(r, dxo_b, "tn", outs=[bf16], a_map=jnp.square, name=f"mlp_dwd_{tag}", **_SQUARE_TILES)
    (dwu,) = _matmul(hm, du, "tn", outs=[bf16], shard_cols=2, name=f"mlp_dwu_{tag}")
    dx, dx_b, dnw = _matmul(du, wu, "nt", outs=_RMS_BWD_OUTS, extras=(x, dxo, nw), epilogue=_rms_bwd_ep,
                            name=f"mlp_dh_{tag}", **_SQUARE_TILES)
    return dx, dx_b, dnw, dwu, dwd.reshape(N_DEV, D_FF // N_DEV, D_MODEL)


def _local_step(x, tgt, P, weight, sink):
    T = x.shape[0]
    cos, sin = _rope_tables(T)
    rtab = _ret_tables()
    row = lambda a: a.reshape(1, -1)
    mix_nw, mlp_nw = P["mixer_norm_w"], P["mlp_norm_w"]
    wr_bd, wi_bd = _block_diag(P["lru_w_r"]), _block_diag(P["lru_w_i"])
    lru_b, lru_br, lru_bi, lru_lam = row(P["lru_conv_b"]), row(P["lru_b_r"]), row(P["lru_b_i"]), row(P["lru_lambda"])
    pad16 = lambda a: jnp.pad(a.reshape(1, GDN_HEADS), ((0, 0), (GDN_HEADS, 128 - 2 * GDN_HEADS)))
    alog, dtb = pad16(P["gdn_a_log"]), pad16(P["gdn_dt_bias"])
    gnw = row(P["gdn_norm_w"])

    x0 = x
    h0 = _rms_fwd(x0, mix_nw[0:1], "rms_mix_0")
    w_ie = weight("w_in_even", h0)
    (pe,) = _matmul(h0, w_ie, "nn", outs=[f32], name="in_even")
    mix0, o_ret, s_ret = _ret_fwd(pe, cos, sin, rtab, "ret_fwd")
    w_lc = weight("lru_conv_w", pe)
    xc = _conv_fwd(pe, 4, w_lc, lru_b, False, "lru_conv_fwd")
    mix0, h_lru = _lru_fwd(xc, pe, 20, wr_bd, wi_bd, lru_br, lru_bi, lru_lam, mix0, "lru_fwd")
    w_oe = weight("w_out_even", mix0)
    x1, hm0 = _matmul(mix0, w_oe, "nn", outs=[f32, bf16], extras=(x0, mlp_nw[0:1]), epilogue=_residual_rms_ep,
                      name="out_even", tm=1024, tn=D_MODEL)
    w_u0, w_d0 = weight("w_up0", x1), weight("w_down0", x1)
    (x2, h1), mlp0 = _mlp_fwd(x1, hm0, w_u0, w_d0, "0", _residual_rms_ep, (mix_nw[1:2],), [f32, bf16])
    w_io = weight("w_in_odd", h1)
    (po,) = _matmul(h1, w_io, "nn", outs=[f32], tm=2048, tn=ODD_PAD // 3, name="in_odd")
    w_gc = weight("gdn_conv_w", po)
    qkv = _conv_fwd(po, 0, w_gc, None, True, "gdn_conv_fwd")
    y_gdn, s_gdn, ti_gdn, vn_gdn, o_gdn = _gdn_fwd(qkv, po, alog, dtb, gnw, "gdn_fwd")
    w_oo = weight("w_out_odd", y_gdn)
    x3, hm1 = _matmul(y_gdn, w_oo, "nn", outs=[f32, bf16], extras=(x2, mlp_nw[1:2]), epilogue=_residual_rms_ep,
                      name="out_odd", tm=1024, tn=D_MODEL)
    w_u1, w_d1 = weight("w_up1", x3), weight("w_down1", x3)
    (loss, dx4, dx4_b, d_final), mlp1 = _mlp_fwd(x3, hm1, w_u1, w_d1, "1", _loss_ep, (row(P["final_norm_w"]), tgt),
                                                 _LOSS_OUTS)
    dx3, dx3_b, d_mlp_nw1, d_wu1, d_wd1 = _mlp_bwd(x3, mlp_nw[1:2], w_u1, w_d1, mlp1, dx4, dx4_b, "1")
    tok = sink(dict(w_up1=d_wu1, w_down1=d_wd1))
    (dy_gdn,) = _matmul(dx3_b, w_oo, "nt", outs=[f32], name="out_odd_dx")
    (d_woo,) = _matmul(y_gdn, dx3_b, "tn", outs=[bf16], name="out_odd_dw")
    dqkv, dpo, dba, d_alog, d_dtb, d_gnw = _gdn_bwd(qkv, po, alog, dtb, gnw + tok[0:1, :], s_gdn, ti_gdn, vn_gdn, o_gdn, dy_gdn,
                                                  "gdn_bwd")
    dpo, d_gconv, _ = _conv_bwd(po, 0, w_gc, None, True, dqkv, dpo, "gdn_conv_bwd")
    dpo = lax.dynamic_update_slice(dpo, dba, (0, 4 * D_MODEL))
    (d_wio,) = _matmul(h1, dpo, "tn", outs=[bf16], tn=ODD_PAD // 3, name="in_odd_dw")
    tok = sink(dict(w_out_odd=d_woo.reshape(N_DEV, D_MODEL // N_DEV, D_MODEL), w_in_odd=_odd_split(d_wio, "w_in_odd_split")))
    dx2, dx2_b, d_mix_nw1 = _matmul(dpo, w_io, "nt", outs=_RMS_BWD_OUTS, extras=(x2, dx3, mix_nw[1:2] + tok[0:1, 0:1]),
                                    epilogue=_rms_bwd_ep, tm=1024, tn=1024, tk=ODD_PAD // 3, name="in_odd_dx")
    dx1, dx1_b, d_mlp_nw0, d_wu0, d_wd0 = _mlp_bwd(x1, mlp_nw[0:1], w_u0, w_d0, mlp0, dx2, dx2_b, "0")
    (d_woe,) = _matmul(mix0, dx1_b, "tn", outs=[bf16], name="out_even_dw")
    tok = sink(dict(w_up0=d_wu0, w_down0=d_wd0, w_out_even=d_woe.reshape(N_DEV, D_MODEL // N_DEV, D_MODEL)))
    (dmix0,) = _matmul(dx1_b, w_oe, "nt", outs=[f32], name="out_even_dx")
    dpe = _ret_bwd(pe, cos, sin, rtab, o_ret, s_ret, dmix0, "ret_bwd")
    dxc, dpe, d_wr, d_wi, d_br, d_bi, d_lam = _lru_bwd(xc, pe, 20, wr_bd, wi_bd, lru_br, lru_bi, lru_lam + tok[0:1, 0:1],
                                                       h_lru, dmix0, 4, dpe, "lru_bwd")
    dpe, d_lconv, d_lconv_b = _conv_bwd(pe, 4, w_lc, lru_b, False, dxc, dpe, "lru_conv_bwd")
    (d_wie,) = _matmul(h0, dpe, "tn", outs=[bf16], shard_cols=2, name="in_even_dw")
    dx0, _, d_mix_nw0 = _matmul(dpe, w_ie, "nt", outs=_RMS_BWD_OUTS, extras=(x0, dx1, mix_nw[0:1]), epilogue=_rms_bwd_ep,
                                name="in_even_dx", **_SQUARE_TILES)

    G = dict(
        mixer_norm_w=jnp.concatenate([d_mix_nw0, d_mix_nw1], axis=0),
        mlp_norm_w=jnp.concatenate([d_mlp_nw0, d_mlp_nw1], axis=0),
        final_norm_w=d_final.reshape(-1),
        w_in_even=d_wie, lru_conv_w=d_lconv, lru_conv_b=d_lconv_b.reshape(-1),
        lru_w_r=_diag_blocks(d_wr), lru_b_r=d_br.reshape(-1), lru_w_i=_diag_blocks(d_wi), lru_b_i=d_bi.reshape(-1),
        lru_lambda=d_lam.reshape(-1), gdn_conv_w=d_gconv,
        gdn_a_log=d_alog[0, GDN_HEADS:2 * GDN_HEADS], gdn_dt_bias=d_dtb[0, GDN_HEADS:2 * GDN_HEADS],
        gdn_norm_w=d_gnw.reshape(-1),
    )
    return loss, dx0, G


_SMALL = ["mixer_norm_w", "mlp_norm_w", "final_norm_w", "lru_conv_b", "lru_w_r", "lru_b_r", "lru_w_i", "lru_b_i",
          "lru_lambda", "gdn_a_log", "gdn_dt_bias", "gdn_norm_w"]
_PACK_ROWS = 688


def _pack(parts):
    flat = jnp.concatenate([p.reshape(-1) for p in parts])
    return jnp.pad(flat, (0, _PACK_ROWS * 128 - flat.shape[0])).reshape(_PACK_ROWS, 128)


def _unpack(packed, shapes):
    flat = packed.reshape(-1)
    out, off = [], 0
    for s in shapes:
        n = int(np.prod(s))
        out.append(flat[off:off + n].reshape(s))
        off += n
    return out


def kernel(x, mixer_norm_w, mlp_norm_w, final_norm_w, w_in_even, lru_conv_w, lru_conv_b, lru_w_r, lru_b_r, lru_w_i, lru_b_i, lru_lambda, w_out_even, w_in_odd, gdn_conv_w, gdn_a_log, gdn_dt_bias, gdn_norm_w, w_out_odd, w_up, w_down, loss_target, m_mixer_norm_w, m_mlp_norm_w, m_final_norm_w, m_w_in_even, m_lru_conv_w, m_lru_conv_b, m_lru_w_r, m_lru_b_r, m_lru_w_i, m_lru_b_i, m_lru_lambda, m_w_out_even, m_w_in_odd, m_gdn_conv_w, m_gdn_a_log, m_gdn_dt_bias, m_gdn_norm_w, m_w_out_odd, m_w_up, m_w_down, v_mixer_norm_w, v_mlp_norm_w, v_final_norm_w, v_w_in_even, v_lru_conv_w, v_lru_conv_b, v_lru_w_r, v_lru_b_r, v_lru_w_i, v_lru_b_i, v_lru_lambda, v_w_out_even, v_w_in_odd, v_gdn_conv_w, v_gdn_a_log, v_gdn_dt_bias, v_gdn_norm_w, v_w_out_odd, v_w_up, v_w_down):
    Pw = dict(mixer_norm_w=mixer_norm_w, mlp_norm_w=mlp_norm_w, final_norm_w=final_norm_w, w_in_even=w_in_even,
              lru_conv_w=lru_conv_w, lru_conv_b=lru_conv_b, lru_w_r=lru_w_r, lru_b_r=lru_b_r, lru_w_i=lru_w_i,
              lru_b_i=lru_b_i, lru_lambda=lru_lambda, w_out_even=w_out_even, w_in_odd=w_in_odd, gdn_conv_w=gdn_conv_w,
              gdn_a_log=gdn_a_log, gdn_dt_bias=gdn_dt_bias, gdn_norm_w=gdn_norm_w, w_out_odd=w_out_odd, w_up=w_up,
              w_down=w_down)
    Pm = dict(mixer_norm_w=m_mixer_norm_w, mlp_norm_w=m_mlp_norm_w, final_norm_w=m_final_norm_w, w_in_even=m_w_in_even,
              lru_conv_w=m_lru_conv_w, lru_conv_b=m_lru_conv_b, lru_w_r=m_lru_w_r, lru_b_r=m_lru_b_r, lru_w_i=m_lru_w_i,
              lru_b_i=m_lru_b_i, lru_lambda=m_lru_lambda, w_out_even=m_w_out_even, w_in_odd=m_w_in_odd,
              gdn_conv_w=m_gdn_conv_w, gdn_a_log=m_gdn_a_log, gdn_dt_bias=m_gdn_dt_bias, gdn_norm_w=m_gdn_norm_w,
              w_out_odd=m_w_out_odd, w_up=m_w_up, w_down=m_w_down)
    Pv = dict(mixer_norm_w=v_mixer_norm_w, mlp_norm_w=v_mlp_norm_w, final_norm_w=v_final_norm_w, w_in_even=v_w_in_even,
              lru_conv_w=v_lru_conv_w, lru_conv_b=v_lru_conv_b, lru_w_r=v_lru_w_r, lru_b_r=v_lru_b_r, lru_w_i=v_lru_w_i,
              lru_b_i=v_lru_b_i, lru_lambda=v_lru_lambda, w_out_even=v_w_out_even, w_in_odd=v_w_in_odd,
              gdn_conv_w=v_gdn_conv_w, gdn_a_log=v_gdn_a_log, gdn_dt_bias=v_gdn_dt_bias, gdn_norm_w=v_gdn_norm_w,
              w_out_odd=v_w_out_odd, w_up=v_w_up, w_down=v_w_down)
    me = _me()[3]
    T = x.shape[1]

    cols = lambda g: jnp.transpose(g, (1, 0, 2)).reshape(g.shape[1], -1)
    rows = lambda g: g.reshape(-1, g.shape[2])
    wide = lambda g: _odd_assemble(g, "w_in_odd_assemble")
    odd_shard = jnp.pad(w_in_odd[0].astype(bf16), ((0, 0), (0, ODD_SHARD_PAD - ODD_SHARD)))
    as_is = lambda g: g
    gather = dict(
        w_in_even=(w_in_even[0].astype(bf16), cols), lru_conv_w=(lru_conv_w[0], cols),
        w_out_even=(w_out_even[0].astype(bf16), rows), w_up0=(w_up[0].astype(bf16), as_is), w_down0=(w_down[0].astype(bf16), rows),
        w_in_odd=(odd_shard, wide), gdn_conv_w=(gdn_conv_w[0], cols),
        w_out_odd=(w_out_odd[0].astype(bf16), rows), w_up1=(w_up[1].astype(bf16), as_is), w_down1=(w_down[1].astype(bf16), rows))
    handles, tok = _send_start([s for s, _ in gather.values()], False, "gather_start")
    handles = dict(zip(gather, handles))
    full = {}

    def weight(name, after):
        if name not in full:
            shard, landed = _send_wait(handles[name], False, after, f"gather_wait_{name}")
            full[name] = gather[name][1](lax.dynamic_update_slice_in_dim(landed, shard[None], me, 0))
        return full[name]

    P = {k: Pw[k] for k in ("mlp_norm_w", "final_norm_w")}
    P["mixer_norm_w"] = mixer_norm_w + tok[0:1, 0:1]
    for k in ("lru_w_r", "lru_w_i", "lru_conv_b", "lru_b_r", "lru_b_i", "lru_lambda", "gdn_a_log", "gdn_dt_bias", "gdn_norm_w"):
        P[k] = Pw[k][0]

    sent = {}

    def sink(grads):
        hs, token = _send_start(list(grads.values()), True, "grads_start_" + "_".join(grads))
        sent.update(zip(grads, hs))
        return token

    loss, dx, G = _local_step(x[0], loss_target[0], P, weight, sink)
    small_g = [G[k].reshape(Pw[k].shape) for k in _SMALL] + [G["lru_conv_w"], G["gdn_conv_w"], loss[0, 0:1]]
    packed = _pack(small_g)
    sink(dict(w_in_even=G["w_in_even"], small=jnp.broadcast_to(packed[None], (N_DEV,) + packed.shape)))

    def received(name, after=dx):
        g, landed = _send_wait(sent[name], True, after, f"grads_wait_{name}")
        return lax.dynamic_update_slice_in_dim(landed, lax.dynamic_slice_in_dim(g, me, 1, 0), me, 0)

    out = {}
    nff = D_FF // N_DEV

    def whole(name, gs):
        out[name] = tuple(_adamw(Pw[name], gs, Pm[name], Pv[name], f"adamw_{name}", layer=0))

    def layers(name):
        res = None
        for l in range(2):
            res = _adamw(Pw[name], received(f"{name}{l}"), Pm[name], Pv[name], f"adamw_{name}{l}", layer=l, prev=res)
        out[name] = tuple(res)

    layers("w_up")
    layers("w_down")
    whole("w_out_odd", received("w_out_odd"))
    whole("w_in_odd", received("w_in_odd"))
    whole("w_out_even", received("w_out_even"))
    whole("w_in_even", received("w_in_even", out["w_out_even"][1]))
    small_shapes = [Pw[k].shape for k in _SMALL]
    pw, pm, pv = (_pack([Q[k] for k in _SMALL]) for Q in (Pw, Pm, Pv))
    sg, sd, sm, sv = _adamw(pw, received("small", out["w_in_even"][1]), pm, pv, "adamw_small")
    for arrs_i, packed_out in enumerate((sg, sd, sm, sv)):
        for k, a in zip(_SMALL, _unpack(packed_out, small_shapes)):
            out.setdefault(k, [None] * 4)[arrs_i] = a
    n_small = sum(int(np.prod(s)) for s in small_shapes)
    gflat = sg.reshape(-1)
    g_lconv = gflat[n_small:n_small + CONV_K * LRU_WIDTH].reshape(CONV_K, LRU_WIDTH)
    g_gconv = gflat[n_small + CONV_K * LRU_WIDTH:n_small + CONV_K * (LRU_WIDTH + 3072)].reshape(CONV_K, 3072)
    whole("lru_conv_w", lax.dynamic_slice_in_dim(g_lconv, me * 64, 64, axis=1)[None])
    whole("gdn_conv_w", lax.dynamic_slice_in_dim(g_gconv, me * 384, 384, axis=1)[None])

    names = ["mixer_norm_w", "mlp_norm_w", "final_norm_w", "w_in_even", "lru_conv_w", "lru_conv_b", "lru_w_r", "lru_b_r",
             "lru_w_i", "lru_b_i", "lru_lambda", "w_out_even", "w_in_odd", "gdn_conv_w", "gdn_a_log", "gdn_dt_bias",
             "gdn_norm_w", "w_out_odd", "w_up", "w_down"]
    total = gflat[n_small + CONV_K * (LRU_WIDTH + 3072)]
    res = [total, dx[None]]
    for j in range(4):
        res += [out[k][j] for k in names]
    return tuple(res)
```

```python
import math

import numpy as np
import jax
import jax.numpy as jnp
from jax import lax
from jax.experimental import pallas as pl
from jax.experimental.pallas import tpu as pltpu

f32 = jnp.float32
bf16 = jnp.bfloat16

N_DEV = 8
D_MODEL = 1024
D_FF = 4096
EPS = 1e-6
RET_HEADS = 4
RET_CHUNK = 128
RET_STEP = 4
ROPE_THETA = 10000.0
LRU_WIDTH = 512
LRU_C = 8.0
GDN_HEADS = 8
GDN_CHUNK = 64
GDN_STEP = 2
HEAD_DIM = 128
ODD_IN = 4112
ODD_PAD = 4224
ODD_SHARD = ODD_IN // N_DEV
ODD_SHARD_PAD = 640
ADAM_LR, ADAM_B1, ADAM_B2, ADAM_EPS, ADAM_WD, ADAM_STEP = 0.001, 0.9, 0.999, 1e-08, 0.01, 10
VMEM_LIMIT = 56 * 1024 * 1024

_NN = (((1,), (0,)), ((), ()))
_NT = (((1,), (1,)), ((), ()))
_TN = (((0,), (0,)), ((), ()))
MESH = pl.DeviceIdType.MESH


def _cparams(sem):
    return pltpu.CompilerParams(dimension_semantics=sem, vmem_limit_bytes=VMEM_LIMIT)


def _dot(a, b, dn):
    return lax.dot_general(a.astype(bf16), b.astype(bf16), dn, preferred_element_type=f32)


def _dot01(a01, b, dn):
    a = a01.astype(bf16)
    b0 = b.astype(bf16)
    r1 = b - b0.astype(f32)
    b1 = r1.astype(bf16)
    b2 = (r1 - b1.astype(f32)).astype(bf16)
    d = lambda q: lax.dot_general(a, q, dn, preferred_element_type=f32)
    return d(b0) + (d(b1) + d(b2))


def _sigmoid(x):
    return jax.nn.sigmoid(x)


def _silu(x):
    return x * _sigmoid(x)


def _dsilu(x):
    s = _sigmoid(x)
    return s * (1.0 + x * (1.0 - s))


def _softplus(x):
    return jnp.maximum(x, 0.0) + jnp.log1p(jnp.exp(-jnp.abs(x)))


_GELU_C = math.sqrt(2.0 / math.pi)


def _gelu(y):
    return 0.5 * y * (1.0 + jnp.tanh(_GELU_C * (y + 0.044715 * y * y * y)))


def _dgelu(y):
    t = jnp.tanh(_GELU_C * (y + 0.044715 * y * y * y))
    return 0.5 * (1.0 + t) + 0.5 * y * (1.0 - t * t) * _GELU_C * (1.0 + 3.0 * 0.044715 * y * y)


def _matmul(a, b, form, *, outs, name, epilogue=None, extras=(), tm=4096, tn=512, tk=1024, shard_cols=0, a_map=None):
    if form == "tn":
        K, M = a.shape
    else:
        M, K = a.shape
    if b.ndim == 3:
        assert form in ("nn", "nt"), name
        N = b.shape[1] if form == "nt" else N_DEV * b.shape[2]
        if form == "nn":
            tn = b.shape[2]
        else:
            tk = b.shape[2]
    else:
        N = b.shape[0] if form == "nt" else b.shape[1]
    ns = N // N_DEV
    if shard_cols:
        tn = ns * shard_cols
    tm, tn, tk = min(tm, M), min(tn, N), min(tk, K)
    assert M % tm == 0 and N % tn == 0 and K % tk == 0, (name, M, N, K, tm, tn, tk)
    nk = K // tk
    dn = {"nn": _NN, "nt": _NT, "tn": _TN}[form]
    if form == "tn":
        a_spec = pl.BlockSpec((tk, tm), lambda i, j, k: (k, i))
    else:
        a_spec = pl.BlockSpec((tm, tk), lambda i, j, k: (i, k))
    if b.ndim == 3:
        b_spec = (pl.BlockSpec((None, tn, tk), lambda i, j, k: (k, j, 0)) if form == "nt"
                  else pl.BlockSpec((None, tk, tn), lambda i, j, k: (j, k, 0)))
    elif form == "nt":
        b_spec = pl.BlockSpec((tn, tk), lambda i, j, k: (j, k))
    else:
        b_spec = pl.BlockSpec((tk, tn), lambda i, j, k: (k, j))
    e_spec = pl.BlockSpec((tm, tn), lambda i, j, k: (i, j))
    v_spec = pl.BlockSpec((1, tn), lambda i, j, k: (0, j))
    if shard_cols:
        o_spec = pl.BlockSpec((shard_cols, tm, ns), lambda i, j, k: (j, i, 0))
        o_shape = (N_DEV, M, ns)
    else:
        o_spec = e_spec
        o_shape = (M, N)
    n_ex = len(extras)
    sums = [isinstance(o, tuple) for o in outs]
    assert not any(sums) or tn == N, name

    def finish(acc, ex, o_refs, row_tile):
        vals = (acc,) if epilogue is None else epilogue(acc, *[e[...] for e in ex])
        for r, v, is_sum in zip(o_refs, vals, sums):
            if is_sum:
                @pl.when(row_tile == 0)
                def _(r=r, v=v):
                    r[...] = v.astype(r.dtype)

                @pl.when(row_tile > 0)
                def _(r=r, v=v):
                    r[...] += v.astype(r.dtype)
            elif shard_cols:
                for s in range(shard_cols):
                    r[s] = v[:, s * ns:(s + 1) * ns].astype(r.dtype)
            else:
                r[...] = v.astype(r.dtype)

    def prod(a_ref, b_ref):
        av = a_ref[...]
        return _dot(av if a_map is None else a_map(av), b_ref[...], dn)

    def body_one(*refs):
        finish(prod(*refs[:2]), refs[2:2 + n_ex], refs[2 + n_ex:], pl.program_id(0))

    def body_acc(*refs):
        a_ref, b_ref = refs[:2]
        acc = refs[-1]
        k = pl.program_id(2)
        row_tile = pl.program_id(0)

        @pl.when(k == 0)
        def _():
            acc[...] = prod(a_ref, b_ref)

        @pl.when((k > 0) & (k < nk - 1))
        def _():
            acc[...] += prod(a_ref, b_ref)

        @pl.when(k == nk - 1)
        def _():
            finish(acc[...] + prod(a_ref, b_ref), refs[2:2 + n_ex], refs[2 + n_ex:-1], row_tile)

    return pl.pallas_call(
        body_one if nk == 1 else body_acc, grid=(M // tm, N // tn, nk),
        in_specs=[a_spec, b_spec] + [v_spec if e.shape[0] == 1 else e_spec for e in extras],
        out_specs=[v_spec if s else o_spec for s in sums],
        out_shape=[jax.ShapeDtypeStruct((1, N), o[1]) if s else jax.ShapeDtypeStruct(o_shape, o) for o, s in zip(outs, sums)],
        scratch_shapes=[] if nk == 1 else [pltpu.VMEM((tm, tn), f32)],
        compiler_params=_cparams(("arbitrary" if any(sums) else "parallel", "parallel", "arbitrary")), name=name,
    )(a, b, *extras)


def _rms_fwd(x, w, name):
    T, D = x.shape
    tt = min(512, T)

    def body(x_ref, w_ref, h_ref):
        xv = x_ref[...]
        r = lax.rsqrt(jnp.mean(xv * xv, axis=1, keepdims=True) + EPS)
        h_ref[...] = (xv * r * w_ref[...]).astype(bf16)

    return pl.pallas_call(
        body, grid=(T // tt,),
        in_specs=[pl.BlockSpec((tt, D), lambda i: (i, 0)), pl.BlockSpec((1, D), lambda i: (0, 0))],
        out_specs=pl.BlockSpec((tt, D), lambda i: (i, 0)),
        out_shape=jax.ShapeDtypeStruct((T, D), bf16),
        compiler_params=_cparams(("parallel",)), name=name,
    )(x, w)


def _residual_rms_ep(acc, res, w):
    x = res + acc
    r = lax.rsqrt(jnp.mean(x * x, axis=1, keepdims=True) + EPS)
    return x, x * r * w


_RMS_BWD_OUTS = [f32, bf16, ("sum", f32)]


def _rms_bwd_ep(dh, x, dres, w):
    r = lax.rsqrt(jnp.mean(x * x, axis=1, keepdims=True) + EPS)
    xn = x * r
    dhw = dh * w
    dx = dres + r * (dhw - xn * jnp.mean(dhw * xn, axis=1, keepdims=True))
    return dx, dx, jnp.sum(dh * xn, axis=0, keepdims=True)


_LOSS_OUTS = [("sum", f32), f32, bf16, ("sum", f32)]


def _loss_ep(acc, res, w, tgt):
    x = res + acc
    D = x.shape[1]
    r = lax.rsqrt(jnp.mean(x * x, axis=1, keepdims=True) + EPS)
    xn = x * r
    e = xn * w - tgt
    loss = 0.5 * jnp.sum(jnp.mean(e * e, axis=1, keepdims=True), axis=0, keepdims=True)
    dy = e * (1.0 / D)
    dyw = dy * w
    dx = r * (dyw - xn * jnp.mean(dyw * xn, axis=1, keepdims=True))
    return jnp.broadcast_to(loss, (1, D)), dx, dx, jnp.sum(dy * xn, axis=0, keepdims=True)


def _ret_tables():
    H, C = RET_HEADS, RET_CHUNK
    lg = np.log1p(-np.exp2(-5.0 - np.arange(H, dtype=np.float32))).astype(np.float32)
    idx = np.arange(C, dtype=np.float32)
    diff = idx[:, None] - idx[None, :]
    causal = diff >= 0
    dm = np.where(causal[None], np.exp(lg[:, None, None] * np.where(causal, diff, 0.0)[None]), 0.0)
    qd = np.exp(lg[:, None] * (idx[None, :] + 1.0))
    kd = np.exp(lg[:, None] * (C - 1.0 - idx[None, :]))
    cg = np.exp(lg * C)
    tab = np.zeros((H, 4, C, HEAD_DIM), np.float32)
    tab[:, 0] = dm
    tab[:, 1] = qd[:, :, None]
    tab[:, 2] = kd[:, :, None]
    tab[:, 3] = cg[:, None, None]
    return jnp.asarray(tab)


def _rope_tables(T):
    half = HEAD_DIM // 2
    inv = ROPE_THETA ** (-jnp.arange(half, dtype=f32) / half)
    ang = jnp.arange(T, dtype=jnp.int32).astype(f32)[:, None] * inv[None, :]
    c, s = jnp.cos(ang), jnp.sin(ang)
    return jnp.concatenate([c, c], axis=1), jnp.concatenate([-s, s], axis=1)


def _rope(x, cos, sin):
    return x * cos + pltpu.roll(x, HEAD_DIM // 2, 1) * sin


def _unrope(y, cos, sin):
    return y * cos + pltpu.roll(y * sin, HEAD_DIM // 2, 1)


def _stack_heads(ref, H, f=None):
    parts = [ref[:, h * HEAD_DIM:(h + 1) * HEAD_DIM] for h in range(H)]
    return jnp.stack(parts if f is None else [f(a) for a in parts])


def _ret_fwd(p, cos, sin, tab, name):
    T = p.shape[0]
    C, H = RET_CHUNK, RET_HEADS
    N = T // C
    K = min(RET_STEP, N)
    NS = N // K
    scale = HEAD_DIM ** -0.5

    def body(q_ref, k_ref, v_ref, g_ref, c_ref, s_ref, t_ref, y_ref, o_ref, sp_ref, st):
        @pl.when(pl.program_id(0) == 0)
        def _():
            st[...] = jnp.zeros_like(st)

        dm, qd, kd, cg = t_ref[:, 0], t_ref[:, 1], t_ref[:, 2], t_ref[:, 3]
        S = st[...]
        for c in range(K):
            rows = pl.ds(c * C, C)
            cos_, sin_ = c_ref[rows, :], s_ref[rows, :]
            rot = lambda a: _rope(a, cos_, sin_)
            q = _stack_heads(q_ref.at[rows, :], H, rot)
            k = _stack_heads(k_ref.at[rows, :], H, rot) * scale
            v = _stack_heads(v_ref.at[rows, :], H)
            P = _dot(q, k, _NT3) * dm
            o = _dot(P, v, _NN3) + _dot(q * qd, S, _NN3)
            sp_ref[c] = S
            S = cg * S + _dot(k * kd, v, _TN3)
            r = lax.rsqrt(jnp.mean(o * o, axis=2, keepdims=True) + EPS)
            y = o * r * _silu(_stack_heads(g_ref.at[rows, :], H))
            for h in range(H):
                o_ref[rows, h * HEAD_DIM:(h + 1) * HEAD_DIM] = o[h]
                y_ref[rows, h * HEAD_DIM:(h + 1) * HEAD_DIM] = y[h].astype(bf16)
        st[...] = S

    wide = lambda blk: pl.BlockSpec((K * C, H * HEAD_DIM), lambda n: (n, blk))
    tbl = pl.BlockSpec((K * C, HEAD_DIM), lambda n: (n, 0))
    return pl.pallas_call(
        body, grid=(NS,),
        in_specs=[wide(0), wide(1), wide(2), wide(3), tbl, tbl,
                  pl.BlockSpec((H, 4, C, HEAD_DIM), lambda n: (0, 0, 0, 0))],
        out_specs=[wide(0), wide(0), pl.BlockSpec((K, H, HEAD_DIM, HEAD_DIM), lambda n: (n, 0, 0, 0))],
        out_shape=[jax.ShapeDtypeStruct((T, D_MODEL), bf16), jax.ShapeDtypeStruct((T, H * HEAD_DIM), f32),
                   jax.ShapeDtypeStruct((N, H, HEAD_DIM, HEAD_DIM), f32)],
        scratch_shapes=[pltpu.VMEM((H, HEAD_DIM, HEAD_DIM), f32)],
        compiler_params=_cparams(("arbitrary",)), name=name,
    )(p, p, p, p, cos, sin, tab)


def _ret_bwd(p, cos, sin, tab, o_raw, sprev, dmix, name):
    T = p.shape[0]
    C, H = RET_CHUNK, RET_HEADS
    N = T // C
    K = min(RET_STEP, N)
    NS = N // K
    scale = HEAD_DIM ** -0.5
    W = H * HEAD_DIM

    def body(q_ref, k_ref, v_ref, g_ref, c_ref, s_ref, t_ref, o_ref, sp_ref, dy_ref, d_ref, dst):
        @pl.when(pl.program_id(0) == 0)
        def _():
            dst[...] = jnp.zeros_like(dst)

        dm, qd, kd, cg = t_ref[:, 0], t_ref[:, 1], t_ref[:, 2], t_ref[:, 3]
        dS1 = dst[...]
        for c in reversed(range(K)):
            rows = pl.ds(c * C, C)
            cos_, sin_ = c_ref[rows, :], s_ref[rows, :]
            rot = lambda a: _rope(a, cos_, sin_)
            q = _stack_heads(q_ref.at[rows, :], H, rot)
            k = _stack_heads(k_ref.at[rows, :], H, rot) * scale
            v = _stack_heads(v_ref.at[rows, :], H)
            g = _stack_heads(g_ref.at[rows, :], H)
            S = sp_ref[c]
            o = _stack_heads(o_ref.at[rows, :], H)
            dy = _stack_heads(dy_ref.at[rows, :], H)
            r = lax.rsqrt(jnp.mean(o * o, axis=2, keepdims=True) + EPS)
            nrm = o * r
            dn = dy * _silu(g)
            dg = dy * nrm * _dsilu(g)
            do = r * (dn - nrm * jnp.mean(dn * nrm, axis=2, keepdims=True))
            P = _dot(q, k, _NT3) * dm
            dP = _dot(do, v, _NT3) * dm
            dq = _dot(dP, k, _NN3) + _dot(do, S, _NT3) * qd
            dk = (_dot(dP, q, _TN3) + _dot(v, dS1, _NT3) * kd) * scale
            dv = _dot(P, do, _TN3) + _dot(k * kd, dS1, _NN3)
            dS1 = cg * dS1 + _dot(q * qd, do, _TN3)
            for h in range(H):
                d_ref[rows, h * HEAD_DIM:(h + 1) * HEAD_DIM] = _unrope(dq[h], cos_, sin_).astype(bf16)
                d_ref[rows, W + h * HEAD_DIM:W + (h + 1) * HEAD_DIM] = _unrope(dk[h], cos_, sin_).astype(bf16)
                d_ref[rows, 2 * W + h * HEAD_DIM:2 * W + (h + 1) * HEAD_DIM] = dv[h].astype(bf16)
                d_ref[rows, 3 * W + h * HEAD_DIM:3 * W + (h + 1) * HEAD_DIM] = dg[h].astype(bf16)
        dst[...] = dS1

    rev = lambda blk: pl.BlockSpec((K * C, W), lambda n: (NS - 1 - n, blk))
    tbl = pl.BlockSpec((K * C, HEAD_DIM), lambda n: (NS - 1 - n, 0))
    return pl.pallas_call(
        body, grid=(NS,),
        in_specs=[rev(0), rev(1), rev(2), rev(3), tbl, tbl,
                  pl.BlockSpec((H, 4, C, HEAD_DIM), lambda n: (0, 0, 0, 0)), rev(0),
                  pl.BlockSpec((K, H, HEAD_DIM, HEAD_DIM), lambda n: (NS - 1 - n, 0, 0, 0)), rev(0)],
        out_specs=pl.BlockSpec((K * C, 4 * W), lambda n: (NS - 1 - n, 0)),
        out_shape=jax.ShapeDtypeStruct((T, 6 * W), bf16),
        scratch_shapes=[pltpu.VMEM((H, HEAD_DIM, HEAD_DIM), f32)],
        compiler_params=_cparams(("arbitrary",)), name=name,
    )(p, p, p, p, cos, sin, tab, o_raw, sprev, dmix)


CONV_K = 4
CONV_W = 512
PAD = 8
SUB_R = 64


def _conv_fwd(x, col_off, w, b, act, name):
    T = x.shape[0]
    C = w.shape[1]
    G = C // CONV_W
    tt = min(512, T)
    NT = T // tt
    has_b = b is not None

    def body(*refs):
        if has_b:
            x_ref, w_ref, b_ref, y_ref, pad = refs
        else:
            x_ref, w_ref, y_ref, pad = refs
        t = pl.program_id(1)

        @pl.when(t == 0)
        def _():
            pad[pl.ds(0, PAD), :] = jnp.zeros((PAD, CONV_W), f32)

        pad[pl.ds(PAD, tt), :] = x_ref[...]
        for g in range(CONV_W // 128):
            ls = slice(g * 128, (g + 1) * 128)
            wv = w_ref[:, ls]
            for c in range(tt // SUB_R):
                r0 = c * SUB_R
                y = wv[0:1, :] * pad[pl.ds(PAD - 3 + r0, SUB_R), ls]
                for kk in range(1, CONV_K):
                    y = y + wv[kk:kk + 1, :] * pad[pl.ds(PAD - 3 + kk + r0, SUB_R), ls]
                if has_b:
                    y = y + b_ref[:, ls]
                y_ref[pl.ds(r0, SUB_R), ls] = _silu(y) if act else y
        tail = pad[pl.ds(tt, PAD), :]
        pad[pl.ds(0, PAD), :] = tail

    in_specs = [pl.BlockSpec((tt, CONV_W), lambda g, t: (t, col_off + g)),
                pl.BlockSpec((CONV_K, CONV_W), lambda g, t: (0, g))]
    args = [x, w]
    if has_b:
        in_specs.append(pl.BlockSpec((1, CONV_W), lambda g, t: (0, g)))
        args.append(b)
    return pl.pallas_call(
        body, grid=(G, NT), in_specs=in_specs,
        out_specs=pl.BlockSpec((tt, CONV_W), lambda g, t: (t, g)),
        out_shape=jax.ShapeDtypeStruct((T, C), f32),
        scratch_shapes=[pltpu.VMEM((tt + PAD, CONV_W), f32)],
        compiler_params=_cparams(("parallel", "arbitrary")), name=name,
    )(*args)


def _conv_bwd(x, col_off, w, b, act, dout, dp, name):
    T = x.shape[0]
    C = w.shape[1]
    G = C // CONV_W
    tt = min(512, T)
    NT = T // tt
    has_b = b is not None

    def body(*refs):
        if has_b:
            x_ref, xp_ref, w_ref, b_ref, d_ref, dp_in, dx_ref, dw_ref, db_ref, pad, dpad = refs
        else:
            x_ref, xp_ref, w_ref, d_ref, dp_in, dx_ref, dw_ref, db_ref, pad, dpad = refs
        t = pl.program_id(1)
        first_tile = t == NT - 1

        @pl.when(t == 0)
        def _():
            dpad[pl.ds(tt, PAD), :] = jnp.zeros((PAD, CONV_W), f32)
            dw_ref[...] = jnp.zeros_like(dw_ref)
            db_ref[...] = jnp.zeros_like(db_ref)

        pad[pl.ds(0, PAD), :] = jnp.where(first_tile, 0.0, xp_ref[...])
        pad[pl.ds(PAD, tt), :] = x_ref[...]
        fold = lambda v: v.reshape(SUB_R // 8, 8, 128).sum(axis=0)
        for g in range(CONV_W // 128):
            ls = slice(g * 128, (g + 1) * 128)
            wv = w_ref[:, ls]
            acc = [jnp.zeros((8, 128), f32) for _ in range(CONV_K + 1)]
            for c in reversed(range(tt // SUB_R)):
                r0 = c * SUB_R
                xs = [pad[pl.ds(PAD - 3 + kk + r0, SUB_R), ls] for kk in range(CONV_K)]
                dy = d_ref[pl.ds(r0, SUB_R), ls]
                if act:
                    y = wv[0:1, :] * xs[0]
                    for kk in range(1, CONV_K):
                        y = y + wv[kk:kk + 1, :] * xs[kk]
                    if has_b:
                        y = y + b_ref[:, ls]
                    dy = dy * _dsilu(y)
                dpad[pl.ds(r0, SUB_R), ls] = dy
                dx = wv[3:4, :] * dy
                for j in range(1, CONV_K):
                    dx = dx + wv[3 - j:4 - j, :] * dpad[pl.ds(r0 + j, SUB_R), ls]
                dx_ref[pl.ds(r0, SUB_R), ls] = dx.astype(bf16)
                for kk in range(CONV_K):
                    acc[kk] = acc[kk] + fold(dy * xs[kk])
                acc[CONV_K] = acc[CONV_K] + fold(dy)
            for kk in range(CONV_K):
                dw_ref[kk:kk + 1, ls] += jnp.sum(acc[kk], axis=0, keepdims=True)
            db_ref[:, ls] += jnp.sum(acc[CONV_K], axis=0, keepdims=True)
        head = dpad[pl.ds(0, PAD), :]
        dpad[pl.ds(tt, PAD), :] = head

    rows8 = tt // PAD
    in_specs = [pl.BlockSpec((tt, CONV_W), lambda g, t: (NT - 1 - t, col_off + g)),
                pl.BlockSpec((PAD, CONV_W), lambda g, t: (jnp.maximum((NT - 1 - t) * rows8 - 1, 0), col_off + g)),
                pl.BlockSpec((CONV_K, CONV_W), lambda g, t: (0, g))]
    args = [x, x, w]
    if has_b:
        in_specs.append(pl.BlockSpec((1, CONV_W), lambda g, t: (0, g)))
        args.append(b)
    in_specs += [pl.BlockSpec((tt, CONV_W), lambda g, t: (NT - 1 - t, g)), pl.BlockSpec(memory_space=pl.ANY)]
    args += [dout, dp]
    return pl.pallas_call(
        body, grid=(G, NT), in_specs=in_specs,
        out_specs=[pl.BlockSpec((tt, CONV_W), lambda g, t: (NT - 1 - t, col_off + g)),
                   pl.BlockSpec((CONV_K, CONV_W), lambda g, t: (0, g)),
                   pl.BlockSpec((1, CONV_W), lambda g, t: (0, g))],
        out_shape=[jax.ShapeDtypeStruct(dp.shape, dp.dtype), jax.ShapeDtypeStruct((CONV_K, C), f32),
                   jax.ShapeDtypeStruct((1, C), f32)],
        input_output_aliases={len(args) - 1: 0},
        scratch_shapes=[pltpu.VMEM((tt + PAD, CONV_W), f32), pltpu.VMEM((tt + PAD, CONV_W), f32)],
        compiler_params=_cparams(("parallel", "arbitrary")), name=name,
    )(*args)


def _lru_gates(xc, wr, wi, br, bi, lam):
    r = _sigmoid(_dot(xc, wr, _NN) + br)
    i = _sigmoid(_dot(xc, wi, _NN) + bi)
    sp = _softplus(-lam)
    a = jnp.exp(-LRU_C * r * sp)
    mult = jnp.sqrt(1.0 - a * a)
    return r, i, sp, a, mult


def _lru_fwd(xc, p, y_off, wr, wi, br, bi, lam, mix, name):
    T = xc.shape[0]
    G = LRU_WIDTH // 128
    tt = min(512, T)
    NT = T // tt

    def body(x_ref, y_ref, wr_ref, wi_ref, br_ref, bi_ref, l_ref, mix_in, o_ref, h_ref, hc):
        t = pl.program_id(1)

        @pl.when(t == 0)
        def _():
            hc[...] = jnp.zeros_like(hc)

        x = x_ref[...]
        r, i, sp, a, mult = _lru_gates(x, wr_ref[...], wi_ref[...], br_ref[...], bi_ref[...], l_ref[...])
        row = lax.broadcasted_iota(jnp.int32, (tt, 128), 0)
        mult = jnp.where((row == 0) & (t == 0), 1.0, mult)
        U = x * i * mult
        A = a
        d = 1
        while d < tt:
            keep = row >= d
            Ush = jnp.where(keep, pltpu.roll(U, d, 0), 0.0)
            Ash = jnp.where(keep, pltpu.roll(A, d, 0), 1.0)
            U = A * Ush + U
            A = A * Ash
            d *= 2
        h = U + A * hc[0:1, :]
        h_ref[...] = h
        hc[...] = jnp.broadcast_to(h[tt - 1:tt, :], hc.shape)
        o_ref[...] = (h * _gelu(y_ref[...])).astype(bf16)

    tile = pl.BlockSpec((tt, 128), lambda g, t: (t, g))
    vec = pl.BlockSpec((1, 128), lambda g, t: (0, g))
    wsp = pl.BlockSpec((128, 128), lambda g, t: (g, g))
    return pl.pallas_call(
        body, grid=(G, NT),
        in_specs=[tile, pl.BlockSpec((tt, 128), lambda g, t: (t, y_off + g)), wsp, wsp, vec, vec, vec,
                  pl.BlockSpec(memory_space=pl.ANY)],
        out_specs=[pl.BlockSpec((tt, 128), lambda g, t: (t, G + g)), tile],
        out_shape=[jax.ShapeDtypeStruct(mix.shape, mix.dtype), jax.ShapeDtypeStruct((T, LRU_WIDTH), f32)],
        input_output_aliases={7: 0},
        scratch_shapes=[pltpu.VMEM((8, 128), f32)],
        compiler_params=_cparams(("parallel", "arbitrary")), name=name,
    )(xc, p, wr, wi, br, bi, lam, mix)


def _lru_bwd(xc, p, y_off, wr, wi, br, bi, lam, hs, dmix, d_off, dp, name):
    T = xc.shape[0]
    G = LRU_WIDTH // 128
    tt = min(512, T)
    NT = T // tt

    def body(x_ref, y_ref, wr_ref, wi_ref, br_ref, bi_ref, l_ref, h_ref, hp_ref, do_ref, dp_in,
             dx_ref, dy_ref, dwr_ref, dwi_ref, dbr_ref, dbi_ref, dl_ref, lc, an):
        t = pl.program_id(1)
        first_tile = t == NT - 1

        @pl.when(t == 0)
        def _():
            lc[...] = jnp.zeros_like(lc)
            an[...] = jnp.zeros_like(an)
            dwr_ref[...] = jnp.zeros_like(dwr_ref)
            dwi_ref[...] = jnp.zeros_like(dwi_ref)
            dbr_ref[...] = jnp.zeros_like(dbr_ref)
            dbi_ref[...] = jnp.zeros_like(dbi_ref)
            dl_ref[...] = jnp.zeros_like(dl_ref)

        x = x_ref[...]
        y = y_ref[...]
        wr, wi, lam_ = wr_ref[...], wi_ref[...], l_ref[...]
        r, i, sp, a, mult_raw = _lru_gates(x, wr, wi, br_ref[...], bi_ref[...], lam_)
        row = lax.broadcasted_iota(jnp.int32, (tt, 128), 0)
        t0 = (row == 0) & first_tile
        mult = jnp.where(t0, 1.0, mult_raw)
        h = h_ref[...]
        do = do_ref[...]
        dh = do * _gelu(y)
        dy_ref[...] = (do * h * _dgelu(y)).astype(bf16)
        B = jnp.where(row == tt - 1, an[0:1, :], pltpu.roll(a, tt - 1, 0))
        L = dh
        d = 1
        while d < tt:
            keep = row < tt - d
            Lsh = jnp.where(keep, pltpu.roll(L, tt - d, 0), 0.0)
            Bsh = jnp.where(keep, pltpu.roll(B, tt - d, 0), 1.0)
            L = L + B * Lsh
            B = B * Bsh
            d *= 2
        L = L + B * lc[0:1, :]
        lc[...] = jnp.broadcast_to(L[0:1, :], lc.shape)
        an[...] = jnp.broadcast_to(a[0:1, :], an.shape)
        hprev = jnp.where(first_tile, 0.0, hp_ref[...])[PAD - 1:PAD, :]
        hm1 = jnp.where(row == 0, hprev, pltpu.roll(h, 1, 0))
        da = L * hm1
        dxc = L * i * mult
        di = L * x * mult
        dmult = jnp.where(t0, 0.0, L * x * i)
        da = da - jnp.where(t0, 0.0, dmult * a / mult_raw)
        dlog_a = da * a
        dr = dlog_a * (-LRU_C) * sp
        dsp = jnp.sum(dlog_a * (-LRU_C) * r, axis=0, keepdims=True)
        dpr = dr * r * (1.0 - r)
        dpi = di * i * (1.0 - i)
        dx_ref[...] = dxc + _dot(dpr, wr, _NT) + _dot(dpi, wi, _NT)
        dwr_ref[0] += _dot(x, dpr, _TN)
        dwi_ref[0] += _dot(x, dpi, _TN)
        dbr_ref[...] += jnp.sum(dpr, axis=0, keepdims=True)
        dbi_ref[...] += jnp.sum(dpi, axis=0, keepdims=True)
        dl_ref[...] += dsp * (-_sigmoid(-lam_))

    rows8 = tt // PAD
    tile = pl.BlockSpec((tt, 128), lambda g, t: (NT - 1 - t, g))
    vec = pl.BlockSpec((1, 128), lambda g, t: (0, g))
    wsp = pl.BlockSpec((128, 128), lambda g, t: (g, g))
    wout = pl.BlockSpec((1, 128, 128), lambda g, t: (g, 0, 0))
    return pl.pallas_call(
        body, grid=(G, NT),
        in_specs=[tile, pl.BlockSpec((tt, 128), lambda g, t: (NT - 1 - t, y_off + g)), wsp, wsp, vec, vec, vec, tile,
                  pl.BlockSpec((PAD, 128), lambda g, t: (jnp.maximum((NT - 1 - t) * rows8 - 1, 0), g)),
                  pl.BlockSpec((tt, 128), lambda g, t: (NT - 1 - t, d_off + g)), pl.BlockSpec(memory_space=pl.ANY)],
        out_specs=[tile, pl.BlockSpec((tt, 128), lambda g, t: (NT - 1 - t, y_off + g)), wout, wout, vec, vec, vec],
        out_shape=[jax.ShapeDtypeStruct((T, LRU_WIDTH), f32), jax.ShapeDtypeStruct(dp.shape, dp.dtype),
                   jax.ShapeDtypeStruct((G, 128, 128), f32), jax.ShapeDtypeStruct((G, 128, 128), f32),
                   jax.ShapeDtypeStruct((1, LRU_WIDTH), f32), jax.ShapeDtypeStruct((1, LRU_WIDTH), f32),
                   jax.ShapeDtypeStruct((1, LRU_WIDTH), f32)],
        input_output_aliases={10: 1},
        scratch_shapes=[pltpu.VMEM((8, 128), f32), pltpu.VMEM((8, 128), f32)],
        compiler_params=_cparams(("parallel", "arbitrary")), name=name,
    )(xc, p, wr, wi, br, bi, lam, hs, hs, dmix, dp)


_NN3 = (((2,), (1,)), ((0,), (0,)))
_NT3 = (((2,), (2,)), ((0,), (0,)))
_TN3 = (((1,), (1,)), ((0,), (0,)))


def _heads(ref):
    return _stack_heads(ref, GDN_HEADS)


def _rowsum(x):
    H, C, L = x.shape
    return _dot(x.reshape(H * C, L), jnp.ones((L, HEAD_DIM), f32), _NN).reshape(H, C, HEAD_DIM)


def _gdn_chunk(qr, kr, v, ba, alog, dtb, S, saved=None):
    C, H = GDN_CHUNK, GDN_HEADS
    lane = lax.broadcasted_iota(jnp.int32, (C, 128), 1)
    lane3 = lax.broadcasted_iota(jnp.int32, (H, C, 128), 2)
    ri = lax.broadcasted_iota(jnp.int32, (C, C), 0)
    ci = lax.broadcasted_iota(jnp.int32, (C, C), 1)
    rowc = lax.broadcasted_iota(jnp.int32, (C, 1), 0)
    col = lambda m, j: jnp.sum(jnp.where(lane == j, m, 0.0), axis=1, keepdims=True)
    cols = lambda m, off: jnp.stack([col(m, off + h) for h in range(H)])
    ea = jnp.exp(alog)
    g_all = -ea * _softplus(ba + dtb)
    tri = (ri >= ci).astype(f32)
    G_all = _dot01(tri, g_all, _NN)
    wide = lambda c: jnp.broadcast_to(c, (H, C, 128))
    beta = wide(cols(_sigmoid(ba), 0))
    Gc = cols(G_all, H)
    rq = lax.rsqrt(_rowsum(qr * qr) + EPS)
    rk = lax.rsqrt(_rowsum(kr * kr) + EPS)
    qh, kn = qr * rq, kr * rk
    qn = qh * (HEAD_DIM ** -0.5)
    Grow = _dot01(jnp.ones((H, C, 128), f32), jnp.where(lane3 == 0, Gc, 0.0), _NT3)
    incl = ri >= ci
    Di = jnp.where(incl, jnp.exp(jnp.where(incl, Gc - Grow, 0.0)), 0.0)
    Ds = jnp.where(ri > ci, Di, 0.0)
    Gl = jnp.sum(jnp.where(rowc == C - 1, Gc, 0.0), axis=1, keepdims=True)
    eG = wide(jnp.exp(Gc))
    eGl = wide(jnp.exp(Gl - Gc))
    cd = jnp.exp(Gl)
    kb = kn * beta
    vb = v * beta
    Lm = _dot(kb, kn, _NT3) * Ds
    kbg = kb * eG
    QK = _dot(qn, kn, _NT3) * Di
    qg = qn * eG
    kg = kn * eGl
    if saved is None:
        same = lambda s: (ri // s) == (ci // s)
        Xd = jnp.where(same(8), -Lm, 0.0)
        Tinv = (ri == ci).astype(f32) + Xd
        Pw = Xd
        for _ in range(2):
            Pw = _dot(Pw, Pw, _NN3)
            Tinv = Tinv + _dot(Tinv, Pw, _NN3)
        for s in (8, 16, 32):
            off = jnp.where(same(2 * s) & jnp.logical_not(same(s)), Lm, 0.0)
            Tinv = Tinv - _dot(_dot(Tinv, off, _NN3), Tinv, _NN3)
        w = _dot(Tinv, kbg, _NN3)
        vn = _dot(Tinv, vb, _NN3) - _dot(w, S, _NN3)
        o = _dot(qg, S, _NN3) + _dot(QK, vn, _NN3)
        S1 = S * cd + _dot(kg, vn, _TN3)
    else:
        Tinv, vn, o = saved
        w = _dot(Tinv, kbg, _NN3)
        S1 = None
    return dict(beta=beta, g_all=g_all, rq=rq, rk=rk, qh=qh, kn=kn, qn=qn, Di=Di, Ds=Ds, eG=eG, eGl=eGl, cd=cd,
                kb=kb, vb=vb, Lm=Lm, Tinv=Tinv, kbg=kbg, w=w, QK=QK, qg=qg, kg=kg, vn=vn, o=o, S1=S1,
                lane=lane, ri=ri, ci=ci, rowc=rowc, ea=ea)


def _gdn_specs(T, rev):
    C = GDN_CHUNK
    H = GDN_HEADS
    K = min(GDN_STEP, T // C)
    NS = T // (C * K)
    nn = (lambda n: NS - 1 - n) if rev else (lambda n: n)
    wide = lambda blk: pl.BlockSpec((K * C, H * HEAD_DIM), lambda n: (nn(n), blk))
    one = lambda off: pl.BlockSpec((K * C, HEAD_DIM), lambda n: (nn(n), off))
    vec = pl.BlockSpec((1, 128), lambda n: (0, 0))
    st = lambda rows: pl.BlockSpec((K, H, rows, rows), lambda n: (nn(n), 0, 0, 0))
    return K, NS, wide, one, vec, st


def _gdn_fwd(qkv, p, alog, dtb, nw, name):
    T = qkv.shape[0]
    C, H = GDN_CHUNK, GDN_HEADS
    N = T // C
    K, NS, wide, one, vec, st_spec = _gdn_specs(T, False)

    def body(q_ref, k_ref, v_ref, z_ref, ba_ref, al_ref, dt_ref, nw_ref, y_ref, sp_ref, ti_ref, vn_ref, o_ref, st):
        @pl.when(pl.program_id(0) == 0)
        def _():
            st[...] = jnp.zeros_like(st)

        S = st[...]
        for c in range(K):
            rows = pl.ds(c * C, C)
            at = lambda ref: ref.at[rows, :]
            f = _gdn_chunk(_heads(at(q_ref)), _heads(at(k_ref)), _heads(at(v_ref)), ba_ref[rows, :], al_ref[...],
                           dt_ref[...], S)
            sp_ref[c] = S
            S = f["S1"]
            ti_ref[c] = f["Tinv"]
            o, vn = f["o"], f["vn"]
            r = lax.rsqrt(_rowsum(o * o) * (1.0 / HEAD_DIM) + EPS)
            y = o * r * nw_ref[...] * _silu(_heads(at(z_ref)))
            for h in range(H):
                sl = slice(h * HEAD_DIM, (h + 1) * HEAD_DIM)
                y_ref[rows, sl] = y[h].astype(bf16)
                vn_ref[rows, sl] = vn[h]
                o_ref[rows, sl] = o[h]
        st[...] = S

    wide_f32 = jax.ShapeDtypeStruct((T, H * HEAD_DIM), f32)
    return pl.pallas_call(
        body, grid=(NS,),
        in_specs=[wide(0), wide(1), wide(2), wide(3), one(4 * H), vec, vec, vec],
        out_specs=[wide(0), st_spec(HEAD_DIM), st_spec(C), wide(0), wide(0)],
        out_shape=[jax.ShapeDtypeStruct((T, H * HEAD_DIM), bf16), jax.ShapeDtypeStruct((N, H, HEAD_DIM, HEAD_DIM), f32),
                   jax.ShapeDtypeStruct((N, H, C, C), f32), wide_f32, wide_f32],
        scratch_shapes=[pltpu.VMEM((H, HEAD_DIM, HEAD_DIM), f32)],
        compiler_params=_cparams(("arbitrary",)), name=name,
    )(qkv, qkv, qkv, p, p, alog, dtb, nw)


def _gdn_bwd(qkv, p, alog, dtb, nw, sprev, tinv, vn_all, o_all, dy_all, name):
    T = qkv.shape[0]
    C, H = GDN_CHUNK, GDN_HEADS
    N = T // C
    K, NS, wide, one, vec, st_spec = _gdn_specs(T, True)
    rs = lambda m: jnp.sum(m, axis=2, keepdims=True)

    def put(ref, val, col=0):
        for h in range(H):
            ref[:, col + h * HEAD_DIM:col + (h + 1) * HEAD_DIM] = val[h].astype(ref.dtype)

    def body(q_ref, k_ref, v_ref, z_ref, ba_ref, al_ref, dt_ref, nw_ref, sp_ref, ti_ref, vn_ref, o_ref, dy_ref,
             dqkv_ref, dz_ref, dba_ref, dal_ref, ddt_ref, dnw_ref, dst):
        @pl.when(pl.program_id(0) == 0)
        def _():
            dst[...] = jnp.zeros_like(dst)
            dal_ref[...] = jnp.zeros_like(dal_ref)
            ddt_ref[...] = jnp.zeros_like(ddt_ref)
            dnw_ref[...] = jnp.zeros_like(dnw_ref)

        dS = dst[...]
        for c in reversed(range(K)):
            at = lambda ref, c=c: ref.at[pl.ds(c * C, C), :]
            dS = chunk(at(q_ref), at(k_ref), at(v_ref), at(z_ref), at(ba_ref), al_ref, dt_ref, nw_ref, sp_ref[c], ti_ref[c],
                       at(vn_ref), at(o_ref), at(dy_ref), at(dqkv_ref), at(dz_ref), at(dba_ref), dal_ref, ddt_ref, dnw_ref, dS)
        dst[...] = dS

    def chunk(q_ref, k_ref, v_ref, z_ref, ba_ref, al_ref, dt_ref, nw_ref, S, Tsaved, vn_ref, o_ref, dy_ref,
              dqkv_ref, dz_ref, dba_ref, dal_ref, ddt_ref, dnw_ref, dS1):
        ba, alog, dtb_, nwv = ba_ref[...], al_ref[...], dt_ref[...], nw_ref[...]
        v = _heads(v_ref)
        f = _gdn_chunk(_heads(q_ref), _heads(k_ref), v, ba, alog, dtb_, S, saved=(Tsaved, _heads(vn_ref), _heads(o_ref)))
        beta, kn, qn, kb, vb, Tinv, kbg = f["beta"], f["kn"], f["qn"], f["kb"], f["vb"], f["Tinv"], f["kbg"]
        eG, eGl, cd, Di, Ds, QK, vn, w_, qg, kg = (f["eG"], f["eGl"], f["cd"], f["Di"], f["Ds"], f["QK"], f["vn"],
                                                    f["w"], f["qg"], f["kg"])
        lane, ri, ci, rowc = f["lane"], f["ri"], f["ci"], f["rowc"]
        o = f["o"]
        z = _heads(z_ref)
        dy = _heads(dy_ref)
        r = lax.rsqrt(_rowsum(o * o) * (1.0 / HEAD_DIM) + EPS)
        nrm = o * r
        sz = _silu(z)
        dn = dy * nwv * sz
        put(dz_ref, dy * nrm * nwv * _dsilu(z))
        dnw_ref[...] += jnp.sum(jnp.sum(dy * nrm * sz, axis=0), axis=0, keepdims=True)
        do = r * (dn - nrm * (_rowsum(dn * nrm) * (1.0 / HEAD_DIM)))
        dcd = jnp.sum(jnp.sum(S * dS1, axis=2, keepdims=True), axis=1, keepdims=True)
        dkg = _dot(vn, dS1, _NT3)
        dvn = _dot(kg, dS1, _NN3) + _dot(QK, do, _TN3)
        dqg = _dot(do, S, _NT3)
        dQK = _dot(do, vn, _NT3)
        dw = -_dot(dvn, S, _NT3)
        dS0 = cd * dS1 + _dot(qg, do, _TN3) - _dot(w_, dvn, _TN3)
        dqn = dqg * eG
        dkn = dkg * eGl
        deGl = rs(dkg * kn)
        dQKr = dQK * Di
        E = dQK * QK
        dqn = dqn + _dot(dQKr, kn, _NN3)
        dkn = dkn + _dot(dQKr, qn, _TN3)
        dT = _dot(dvn, vb, _NT3) + _dot(dw, kbg, _NT3)
        dvb = _dot(Tinv, dvn, _TN3)
        dkbg = _dot(Tinv, dw, _TN3)
        dkb = dkbg * eG
        deG = rs(dqg * qn + dkbg * kb)
        dL = -_dot(_dot(Tinv, dT, _TN3), Tinv, _NT3)
        dKK = dL * Ds
        E = E + dL * f["Lm"]
        dkb = dkb + _dot(dKK, kn, _NN3)
        dkn = dkn + _dot(dKK, kb, _TN3) + dkb * beta
        dbeta = rs(dkb * kn + dvb * v)
        put(dqkv_ref, dvb * beta, 2 * H * HEAD_DIM)
        dG = rs(E) - rs(jnp.swapaxes(E, 1, 2)) + deG * eG - deGl * eGl
        dGl = jnp.sum(deGl * eGl, axis=1, keepdims=True) + dcd * cd
        dG = dG + jnp.where(rowc == C - 1, dGl, 0.0)
        qh = f["qh"]
        put(dqkv_ref, (HEAD_DIM ** -0.5) * f["rq"] * (dqn - qh * _rowsum(dqn * qh)))
        put(dqkv_ref, f["rk"] * (dkn - kn * _rowsum(dkn * kn)), H * HEAD_DIM)
        db = dbeta * beta * (1.0 - beta)
        db_all = jnp.where(lane == 0, db[0], 0.0)
        dG_all = jnp.where(lane == H, dG[0], 0.0)
        for h in range(1, H):
            db_all = db_all + jnp.where(lane == h, db[h], 0.0)
            dG_all = dG_all + jnp.where(lane == H + h, dG[h], 0.0)
        triu = (ri <= ci).astype(f32)
        dg_all = _dot01(triu, dG_all, _NN)
        da_all = dg_all * (-f["ea"]) * _sigmoid(ba + dtb_)
        dba_ref[...] = (db_all + da_all).astype(bf16)
        ddt_ref[...] += jnp.sum(da_all, axis=0, keepdims=True)
        dal_ref[...] += jnp.sum(dg_all * f["g_all"], axis=0, keepdims=True)
        return dS0

    small = jax.ShapeDtypeStruct((1, 128), f32)
    return pl.pallas_call(
        body, grid=(NS,),
        in_specs=[wide(0), wide(1), wide(2), wide(3), one(4 * H), vec, vec, vec, st_spec(HEAD_DIM), st_spec(C),
                  wide(0), wide(0), wide(0)],
        out_specs=[pl.BlockSpec((K * C, 3 * H * HEAD_DIM), lambda n: (NS - 1 - n, 0)), wide(3), one(0), vec, vec, vec],
        out_shape=[jax.ShapeDtypeStruct((T, 3 * H * HEAD_DIM), f32), jax.ShapeDtypeStruct((T, ODD_PAD), bf16),
                   jax.ShapeDtypeStruct((T, 128), bf16), small, small, small],
        scratch_shapes=[pltpu.VMEM((H, HEAD_DIM, HEAD_DIM), f32)],
        compiler_params=_cparams(("arbitrary",)), name=name,
    )(qkv, qkv, qkv, p, p, alog, dtb, nw, sprev, tinv, vn_all, o_all, dy_all)


def _lanes_from(x, s):
    return x if s % 128 == 0 else pltpu.roll(x, (128 - s) % 128, 1)


def _odd_assemble(g, name):
    R = g.shape[1]
    tr = min(256, R)
    n_blk = ODD_SHARD_PAD // 128

    def body(g_ref, o_ref):
        lane = lax.broadcasted_iota(jnp.int32, (tr, 128), 1)
        blk = lambda d, m: g_ref[d, :, m * 128:(m + 1) * 128]
        for gb in range(ODD_PAD // 128):
            c0 = 128 * gb
            if c0 >= ODD_IN:
                o_ref[:, c0:c0 + 128] = jnp.zeros((tr, 128), g.dtype)
                continue
            d0 = c0 // ODD_SHARD
            m0, sh = divmod(c0 - ODD_SHARD * d0, 128)
            take = min(128, ODD_SHARD * (d0 + 1) - c0)
            p = _lanes_from(blk(d0, m0), sh)
            if sh and m0 + 1 < n_blk:
                p = jnp.where(lane < 128 - sh, p, _lanes_from(blk(d0, m0 + 1), sh))
            if take < 128:
                nxt = pltpu.roll(blk(d0 + 1, 0), take, 1) if d0 + 1 < N_DEV else jnp.zeros((tr, 128), g.dtype)
                p = jnp.where(lane < take, p, nxt)
            o_ref[:, c0:c0 + 128] = p

    return pl.pallas_call(
        body, grid=(R // tr,),
        in_specs=[pl.BlockSpec((N_DEV, tr, ODD_SHARD_PAD), lambda i: (0, i, 0))],
        out_specs=pl.BlockSpec((tr, ODD_PAD), lambda i: (i, 0)),
        out_shape=jax.ShapeDtypeStruct((R, ODD_PAD), g.dtype),
        compiler_params=_cparams(("parallel",)), name=name,
    )(g)


def _odd_split(w, name):
    R = w.shape[0]
    tr = min(256, R)

    def body(w_ref, o_ref):
        lane = lax.broadcasted_iota(jnp.int32, (tr, 128), 1)
        blk = lambda gb: w_ref[:, gb * 128:(gb + 1) * 128]
        for d in range(N_DEV):
            for m in range(ODD_SHARD_PAD // 128):
                g0, sh = divmod(ODD_SHARD * d + 128 * m, 128)
                p = _lanes_from(blk(g0), sh)
                if sh and g0 + 1 < ODD_PAD // 128:
                    p = jnp.where(lane < 128 - sh, p, _lanes_from(blk(g0 + 1), sh))
                real = ODD_SHARD - 128 * m
                if real < 128:
                    p = jnp.where(lane < real, p, jnp.zeros_like(p))
                o_ref[d, :, m * 128:(m + 1) * 128] = p

    return pl.pallas_call(
        body, grid=(R // tr,),
        in_specs=[pl.BlockSpec((tr, ODD_PAD), lambda i: (i, 0))],
        out_specs=pl.BlockSpec((N_DEV, tr, ODD_SHARD_PAD), lambda i: (0, i, 0)),
        out_shape=jax.ShapeDtypeStruct((N_DEV, R, ODD_SHARD_PAD), w.dtype),
        compiler_params=_cparams(("parallel",)), name=name,
    )(w)


def _adamw(w, gs, m, v, name, layer=None, prev=None):
    R, Cc = w.shape[-2:]
    S = gs.shape[0]
    tr = R
    if S * R * Cc * 4 > (4 << 20):
        for cand in (256, 128, 64, 32, 16, 8):
            if R % cand == 0 and R > cand:
                tr = cand
                break
    c1 = 1.0 - ADAM_B1 ** ADAM_STEP
    c2 = 1.0 - ADAM_B2 ** ADAM_STEP

    def body(w_ref, g_ref, m_ref, v_ref, *rest):
        go_ref, d_ref, mo_ref, vo_ref = rest[-4:]
        g = g_ref[0].astype(f32)
        for s in range(1, S):
            g = g + g_ref[s].astype(f32)
        mn = ADAM_B1 * m_ref[...] + (1.0 - ADAM_B1) * g
        vn = ADAM_B2 * v_ref[...] + (1.0 - ADAM_B2) * (g * g)
        go_ref[...] = g
        mo_ref[...] = mn
        vo_ref[...] = vn
        d_ref[...] = -ADAM_LR * ((mn / c1) / (jnp.sqrt(vn / c2) + ADAM_EPS) + ADAM_WD * w_ref[...])

    tc = Cc if gs.shape[-1] == Cc else 128
    if layer is None:
        blk = pl.BlockSpec((tr, tc), lambda i, j: (i, j))
    else:
        blk = pl.BlockSpec((None, tr, tc), lambda i, j: (layer, i, j))
    out = jax.ShapeDtypeStruct(w.shape, f32)
    carried = [] if prev is None else list(prev)
    return pl.pallas_call(
        body, grid=(R // tr, pl.cdiv(Cc, tc)),
        in_specs=[blk, pl.BlockSpec((S, tr, tc), lambda i, j: (0, i, j)), blk, blk]
        + [pl.BlockSpec(memory_space=pl.ANY)] * len(carried),
        out_specs=[blk] * 4, out_shape=[out] * 4,
        input_output_aliases={4 + j: j for j in range(len(carried))},
        compiler_params=_cparams(("parallel", "parallel")), name=name,
    )(w, gs, m, v, *carried)


def _me():
    x, y, c = lax.axis_index("x"), lax.axis_index("y"), lax.axis_index("c")
    return x, y, c, 4 * x + 2 * y + c


def _peer(k):
    x, y, c, _ = _me()
    px = 1 - x if k & 4 else x
    py = 1 - y if k & 2 else y
    pc = 1 - c if k & 1 else c
    return (px, py, pc), 4 * px + 2 * py + pc


_HBM = pl.BlockSpec(memory_space=pltpu.HBM)
_SEM = pl.BlockSpec(memory_space=pltpu.SEMAPHORE)
_EFFECT = pltpu.SideEffectType.DATAFLOW_SIDE_EFFECTING


def _copy(src, land, ssem, rsem, k, blocked, landing_slot_of_peer):
    pid, pidx = _peer(k)
    slot = pidx if landing_slot_of_peer else _me()[3]
    return pltpu.make_async_remote_copy(src_ref=src.at[pidx] if blocked else src, dst_ref=land.at[slot],
                                        send_sem=ssem.at[k - 1], recv_sem=rsem.at[k - 1], device_id=pid, device_id_type=MESH)


def _send_start(srcs, blocked, name):
    n = len(srcs)
    lands = [lax.empty(a.shape if blocked else (N_DEV,) + a.shape, a.dtype) for a in srcs]

    def body(*refs):
        src, land, sems, token = refs[:n], refs[n:2 * n], refs[2 * n:4 * n], refs[-1]
        for i in range(n):
            for k in range(1, N_DEV):
                _copy(src[i], land[i], sems[2 * i], sems[2 * i + 1], k, blocked, False).start()
        token[...] = jnp.zeros_like(token)

    sem = pltpu.SemaphoreType.DMA((N_DEV - 1,))
    hbm = lambda a: pltpu.with_memory_space_constraint(a, pltpu.HBM)
    res = pl.pallas_call(
        body, name=name,
        out_shape=tuple([sem] * (2 * n)) + tuple(pltpu.HBM(a.shape, a.dtype) for a in srcs + lands)
        + (jax.ShapeDtypeStruct((8, 128), f32),),
        in_specs=(_HBM,) * (2 * n),
        out_specs=(_SEM,) * (2 * n) + (_HBM,) * (2 * n) + (pl.BlockSpec(memory_space=pltpu.VMEM),),
        input_output_aliases={j: 2 * n + j for j in range(2 * n)},
        compiler_params=pltpu.CompilerParams(has_side_effects=_EFFECT),
    )(*[hbm(a) for a in srcs], *[hbm(a) for a in lands])
    handles = [(res[2 * i], res[2 * i + 1], res[2 * n + i], res[3 * n + i]) for i in range(n)]
    return handles, res[-1]


def _send_wait(handle, blocked, after, name):
    ssem, rsem, src, land = handle

    def body(src_ref, land_ref, ssem_ref, rsem_ref, after_ref, src_out, land_out):
        for k in range(1, N_DEV):
            cp = _copy(src_ref, land_ref, ssem_ref, rsem_ref, k, blocked, True)
            cp.wait_send()
            cp.wait_recv()

    return pl.pallas_call(
        body, name=name, out_shape=(pltpu.HBM(src.shape, src.dtype), pltpu.HBM(land.shape, land.dtype)),
        in_specs=(_HBM, _HBM, _SEM, _SEM, pl.BlockSpec(memory_space=pl.ANY)), out_specs=(_HBM, _HBM),
        input_output_aliases={0: 0, 1: 1}, compiler_params=pltpu.CompilerParams(has_side_effects=_EFFECT),
    )(src, land, ssem, rsem, after)


def _block_diag(w):
    nb, bs = w.shape[0], w.shape[1]
    eye = jnp.eye(nb, dtype=w.dtype)
    return (eye[:, None, :, None] * w[:, :, None, :]).reshape(nb * bs, nb * bs)


def _diag_blocks(d):
    return jnp.stack([d[g, s * 64:(s + 1) * 64, s * 64:(s + 1) * 64] for g in range(4) for s in range(2)])


_SQUARE_TILES = dict(tm=1024, tn=1024, tk=1024)


def _mlp_fwd(x, hm, wu, wd, tag, epilogue, extras, outs):
    (r,) = _matmul(hm, wu, "nn", outs=[bf16], epilogue=lambda acc: (jnp.maximum(acc, 0.0),), name=f"mlp_up_{tag}")
    res = _matmul(r, wd, "nn", outs=outs, extras=(x,) + tuple(extras), epilogue=epilogue, a_map=jnp.square,
                  name=f"mlp_down_{tag}", **_SQUARE_TILES)
    return res, (hm, r)


def _mlp_bwd(x, nw, wu, wd, saved, dxo, dxo_b, tag):
    hm, r = saved
    (du,) = _matmul(dxo_b, wd, "nt", outs=[bf16], extras=(r,), epilogue=lambda acc, rr: (acc * (2.0 * rr.astype(f32)),),
                    name=f"mlp_dact_{tag}")
    (dwd,) = _matmul(r, dxo_b, "tn", outs=[bf16], a_map=jnp.square, name=f"mlp_dwd_{tag}", **_SQUARE_TILES)
    (dwu,) = _matmul(hm, du, "tn", outs=[bf16], shard_cols=2, name=f"mlp_dwu_{tag}")
    dx, dx_b, dnw = _matmul(du, wu, "nt", outs=_RMS_BWD_OUTS, extras=(x, dxo, nw), epilogue=_rms_bwd_ep,
                            name=f"mlp_dh_{tag}", **_SQUARE_TILES)
    return dx, dx_b, dnw, dwu, dwd.reshape(N_DEV, D_FF // N_DEV, D_MODEL)


def _local_step(x, tgt, P, weight, sink):
    T = x.shape[0]
    cos, sin = _rope_tables(T)
    rtab = _ret_tables()
    row = lambda a: a.reshape(1, -1)
    mix_nw, mlp_nw = P["mixer_norm_w"], P["mlp_norm_w"]
    wr_bd, wi_bd = _block_diag(P["lru_w_r"]), _block_diag(P["lru_w_i"])
    lru_b, lru_br, lru_bi, lru_lam = row(P["lru_conv_b"]), row(P["lru_b_r"]), row(P["lru_b_i"]), row(P["lru_lambda"])
    pad16 = lambda a: jnp.pad(a.reshape(1, GDN_HEADS), ((0, 0), (GDN_HEADS, 128 - 2 * GDN_HEADS)))
    alog, dtb = pad16(P["gdn_a_log"]), pad16(P["gdn_dt_bias"])
    gnw = row(P["gdn_norm_w"])

    x0 = x
    h0 = _rms_fwd(x0, mix_nw[0:1], "rms_mix_0")
    w_ie = weight("w_in_even", h0)
    (pe,) = _matmul(h0, w_ie, "nn", outs=[f32], name="in_even")
    mix0, o_ret, s_ret = _ret_fwd(pe, cos, sin, rtab, "ret_fwd")
    w_lc = weight("lru_conv_w", pe)
    xc = _conv_fwd(pe, 4, w_lc, lru_b, False, "lru_conv_fwd")
    mix0, h_lru = _lru_fwd(xc, pe, 20, wr_bd, wi_bd, lru_br, lru_bi, lru_lam, mix0, "lru_fwd")
    w_oe = weight("w_out_even", mix0)
    x1, hm0 = _matmul(mix0, w_oe, "nn", outs=[f32, bf16], extras=(x0, mlp_nw[0:1]), epilogue=_residual_rms_ep,
                      name="out_even", tm=1024, tn=D_MODEL)
    w_u0, w_d0 = weight("w_up0", x1), weight("w_down0", x1)
    (x2, h1), mlp0 = _mlp_fwd(x1, hm0, w_u0, w_d0, "0", _residual_rms_ep, (mix_nw[1:2],), [f32, bf16])
    w_io = weight("w_in_odd", h1)
    (po,) = _matmul(h1, w_io, "nn", outs=[f32], tm=2048, tn=ODD_PAD // 3, name="in_odd")
    w_gc = weight("gdn_conv_w", po)
    qkv = _conv_fwd(po, 0, w_gc, None, True, "gdn_conv_fwd")
    y_gdn, s_gdn, ti_gdn, vn_gdn, o_gdn = _gdn_fwd(qkv, po, alog, dtb, gnw, "gdn_fwd")
    w_oo = weight("w_out_odd", y_gdn)
    x3, hm1 = _matmul(y_gdn, w_oo, "nn", outs=[f32, bf16], extras=(x2, mlp_nw[1:2]), epilogue=_residual_rms_ep,
                      name="out_odd", tm=1024, tn=D_MODEL)
    w_u1, w_d1 = weight("w_up1", x3), weight("w_down1", x3)
    (loss, dx4, dx4_b, d_final), mlp1 = _mlp_fwd(x3, hm1, w_u1, w_d1, "1", _loss_ep, (row(P["final_norm_w"]), tgt),
                                                 _LOSS_OUTS)
    dx3, dx3_b, d_mlp_nw1, d_wu1, d_wd1 = _mlp_bwd(x3, mlp_nw[1:2], w_u1, w_d1, mlp1, dx4, dx4_b, "1")
    tok = sink(dict(w_up1=d_wu1, w_down1=d_wd1))
    (dy_gdn,) = _matmul(dx3_b, w_oo, "nt", outs=[f32], name="out_odd_dx")
    (d_woo,) = _matmul(y_gdn, dx3_b, "tn", outs=[bf16], name="out_odd_dw")
    dqkv, dpo, dba, d_alog, d_dtb, d_gnw = _gdn_bwd(qkv, po, alog, dtb, gnw + tok[0:1, :], s_gdn, ti_gdn, vn_gdn, o_gdn, dy_gdn,
                                                  "gdn_bwd")
    dpo, d_gconv, _ = _conv_bwd(po, 0, w_gc, None, True, dqkv, dpo, "gdn_conv_bwd")
    dpo = lax.dynamic_update_slice(dpo, dba, (0, 4 * D_MODEL))
    (d_wio,) = _matmul(h1, dpo, "tn", outs=[bf16], tn=ODD_PAD // 3, name="in_odd_dw")
    tok = sink(dict(w_out_odd=d_woo.reshape(N_DEV, D_MODEL // N_DEV, D_MODEL), w_in_odd=_odd_split(d_wio, "w_in_odd_split")))
    dx2, dx2_b, d_mix_nw1 = _matmul(dpo, w_io, "nt", outs=_RMS_BWD_OUTS, extras=(x2, dx3, mix_nw[1:2] + tok[0:1, 0:1]),
                                    epilogue=_rms_bwd_ep, tm=1024, tn=1024, tk=ODD_PAD // 3, name="in_odd_dx")
    dx1, dx1_b, d_mlp_nw0, d_wu0, d_wd0 = _mlp_bwd(x1, mlp_nw[0:1], w_u0, w_d0, mlp0, dx2, dx2_b, "0")
    (d_woe,) = _matmul(mix0, dx1_b, "tn", outs=[bf16], name="out_even_dw")
    tok = sink(dict(w_up0=d_wu0, w_down0=d_wd0, w_out_even=d_woe.reshape(N_DEV, D_MODEL // N_DEV, D_MODEL)))
    (dmix0,) = _matmul(dx1_b, w_oe, "nt", outs=[f32], name="out_even_dx")
    dpe = _ret_bwd(pe, cos, sin, rtab, o_ret, s_ret, dmix0, "ret_bwd")
    dxc, dpe, d_wr, d_wi, d_br, d_bi, d_lam = _lru_bwd(xc, pe, 20, wr_bd, wi_bd, lru_br, lru_bi, lru_lam + tok[0:1, 0:1],
                                                       h_lru, dmix0, 4, dpe, "lru_bwd")
    dpe, d_lconv, d_lconv_b = _conv_bwd(pe, 4, w_lc, lru_b, False, dxc, dpe, "lru_conv_bwd")
    (d_wie,) = _matmul(h0, dpe, "tn", outs=[bf16], shard_cols=2, name="in_even_dw")
    tok = sink(dict(w_in_even=d_wie))
    dx0, _, d_mix_nw0 = _matmul(dpe, w_ie, "nt", outs=_RMS_BWD_OUTS, extras=(x0, dx1, mix_nw[0:1] + tok[0:1, 0:1]),
                                epilogue=_rms_bwd_ep, name="in_even_dx", **_SQUARE_TILES)

    G = dict(
        mixer_norm_w=jnp.concatenate([d_mix_nw0, d_mix_nw1], axis=0),
        mlp_norm_w=jnp.concatenate([d_mlp_nw0, d_mlp_nw1], axis=0),
        final_norm_w=d_final.reshape(-1),
        lru_conv_w=d_lconv, lru_conv_b=d_lconv_b.reshape(-1),
        lru_w_r=_diag_blocks(d_wr), lru_b_r=d_br.reshape(-1), lru_w_i=_diag_blocks(d_wi), lru_b_i=d_bi.reshape(-1),
        lru_lambda=d_lam.reshape(-1), gdn_conv_w=d_gconv,
        gdn_a_log=d_alog[0, GDN_HEADS:2 * GDN_HEADS], gdn_dt_bias=d_dtb[0, GDN_HEADS:2 * GDN_HEADS],
        gdn_norm_w=d_gnw.reshape(-1),
    )
    return loss, dx0, G


_SMALL = ["mixer_norm_w", "mlp_norm_w", "final_norm_w", "lru_conv_b", "lru_w_r", "lru_b_r", "lru_w_i", "lru_b_i",
          "lru_lambda", "gdn_a_log", "gdn_dt_bias", "gdn_norm_w"]
_PACK_ROWS = 688


def _pack(parts):
    flat = jnp.concatenate([p.reshape(-1) for p in parts])
    return jnp.pad(flat, (0, _PACK_ROWS * 128 - flat.shape[0])).reshape(_PACK_ROWS, 128)


def _unpack(packed, shapes):
    flat = packed.reshape(-1)
    out, off = [], 0
    for s in shapes:
        n = int(np.prod(s))
        out.append(flat[off:off + n].reshape(s))
        off += n
    return out


def kernel(x, mixer_norm_w, mlp_norm_w, final_norm_w, w_in_even, lru_conv_w, lru_conv_b, lru_w_r, lru_b_r, lru_w_i, lru_b_i, lru_lambda, w_out_even, w_in_odd, gdn_conv_w, gdn_a_log, gdn_dt_bias, gdn_norm_w, w_out_odd, w_up, w_down, loss_target, m_mixer_norm_w, m_mlp_norm_w, m_final_norm_w, m_w_in_even, m_lru_conv_w, m_lru_conv_b, m_lru_w_r, m_lru_b_r, m_lru_w_i, m_lru_b_i, m_lru_lambda, m_w_out_even, m_w_in_odd, m_gdn_conv_w, m_gdn_a_log, m_gdn_dt_bias, m_gdn_norm_w, m_w_out_odd, m_w_up, m_w_down, v_mixer_norm_w, v_mlp_norm_w, v_final_norm_w, v_w_in_even, v_lru_conv_w, v_lru_conv_b, v_lru_w_r, v_lru_b_r, v_lru_w_i, v_lru_b_i, v_lru_lambda, v_w_out_even, v_w_in_odd, v_gdn_conv_w, v_gdn_a_log, v_gdn_dt_bias, v_gdn_norm_w, v_w_out_odd, v_w_up, v_w_down):
    Pw = dict(mixer_norm_w=mixer_norm_w, mlp_norm_w=mlp_norm_w, final_norm_w=final_norm_w, w_in_even=w_in_even,
              lru_conv_w=lru_conv_w, lru_conv_b=lru_conv_b, lru_w_r=lru_w_r, lru_b_r=lru_b_r, lru_w_i=lru_w_i,
              lru_b_i=lru_b_i, lru_lambda=lru_lambda, w_out_even=w_out_even, w_in_odd=w_in_odd, gdn_conv_w=gdn_conv_w,
              gdn_a_log=gdn_a_log, gdn_dt_bias=gdn_dt_bias, gdn_norm_w=gdn_norm_w, w_out_odd=w_out_odd, w_up=w_up,
              w_down=w_down)
    Pm = dict(mixer_norm_w=m_mixer_norm_w, mlp_norm_w=m_mlp_norm_w, final_norm_w=m_final_norm_w, w_in_even=m_w_in_even,
              lru_conv_w=m_lru_conv_w, lru_conv_b=m_lru_conv_b, lru_w_r=m_lru_w_r, lru_b_r=m_lru_b_r, lru_w_i=m_lru_w_i,
              lru_b_i=m_lru_b_i, lru_lambda=m_lru_lambda, w_out_even=m_w_out_even, w_in_odd=m_w_in_odd,
              gdn_conv_w=m_gdn_conv_w, gdn_a_log=m_gdn_a_log, gdn_dt_bias=m_gdn_dt_bias, gdn_norm_w=m_gdn_norm_w,
              w_out_odd=m_w_out_odd, w_up=m_w_up, w_down=m_w_down)
    Pv = dict(mixer_norm_w=v_mixer_norm_w, mlp_norm_w=v_mlp_norm_w, final_norm_w=v_final_norm_w, w_in_even=v_w_in_even,
              lru_conv_w=v_lru_conv_w, lru_conv_b=v_lru_conv_b, lru_w_r=v_lru_w_r, lru_b_r=v_lru_b_r, lru_w_i=v_lru_w_i,
              lru_b_i=v_lru_b_i, lru_lambda=v_lru_lambda, w_out_even=v_w_out_even, w_in_odd=v_w_in_odd,
              gdn_conv_w=v_gdn_conv_w, gdn_a_log=v_gdn_a_log, gdn_dt_bias=v_gdn_dt_bias, gdn_norm_w=v_gdn_norm_w,
              w_out_odd=v_w_out_odd, w_up=v_w_up, w_down=v_w_down)
    me = _me()[3]
    T = x.shape[1]

    cols = lambda g: jnp.transpose(g, (1, 0, 2)).reshape(g.shape[1], -1)
    rows = lambda g: g.reshape(-1, g.shape[2])
    wide = lambda g: _odd_assemble(g, "w_in_odd_assemble")
    odd_shard = jnp.pad(w_in_odd[0].astype(bf16), ((0, 0), (0, ODD_SHARD_PAD - ODD_SHARD)))
    as_is = lambda g: g
    gather = dict(
        w_in_even=(w_in_even[0].astype(bf16), cols), lru_conv_w=(lru_conv_w[0], cols),
        w_out_even=(w_out_even[0].astype(bf16), rows), w_up0=(w_up[0].astype(bf16), as_is), w_down0=(w_down[0].astype(bf16), rows),
        w_in_odd=(odd_shard, wide), gdn_conv_w=(gdn_conv_w[0], cols),
        w_out_odd=(w_out_odd[0].astype(bf16), rows), w_up1=(w_up[1].astype(bf16), as_is), w_down1=(w_down[1].astype(bf16), rows))
    handles, tok = _send_start([s for s, _ in gather.values()], False, "gather_start")
    handles = dict(zip(gather, handles))
    full = {}

    def weight(name, after):
        if name not in full:
            shard, landed = _send_wait(handles[name], False, after, f"gather_wait_{name}")
            full[name] = gather[name][1](lax.dynamic_update_slice_in_dim(landed, shard[None], me, 0))
        return full[name]

    P = {k: Pw[k] for k in ("mlp_norm_w", "final_norm_w")}
    P["mixer_norm_w"] = mixer_norm_w + tok[0:1, 0:1]
    for k in ("lru_w_r", "lru_w_i", "lru_conv_b", "lru_b_r", "lru_b_i", "lru_lambda", "gdn_a_log", "gdn_dt_bias", "gdn_norm_w"):
        P[k] = Pw[k][0]

    sent = {}

    def sink(grads):
        hs, token = _send_start(list(grads.values()), True, "grads_start_" + "_".join(grads))
        sent.update(zip(grads, hs))
        return token

    loss, dx, G = _local_step(x[0], loss_target[0], P, weight, sink)
    small_g = [G[k].reshape(Pw[k].shape) for k in _SMALL] + [G["lru_conv_w"], G["gdn_conv_w"], loss[0, 0:1]]
    packed = _pack(small_g)
    sink(dict(small=jnp.broadcast_to(packed[None], (N_DEV,) + packed.shape)))

    def received(name, after=dx):
        g, landed = _send_wait(sent[name], True, after, f"grads_wait_{name}")
        return lax.dynamic_update_slice_in_dim(landed, lax.dynamic_slice_in_dim(g, me, 1, 0), me, 0)

    out = {}
    nff = D_FF // N_DEV

    def whole(name, gs):
        out[name] = tuple(_adamw(Pw[name], gs, Pm[name], Pv[name], f"adamw_{name}", layer=0))

    def layers(name):
        res = None
        for l in range(2):
            res = _adamw(Pw[name], received(f"{name}{l}"), Pm[name], Pv[name], f"adamw_{name}{l}", layer=l, prev=res)
        out[name] = tuple(res)

    layers("w_up")
    layers("w_down")
    whole("w_out_odd", received("w_out_odd"))
    whole("w_in_odd", received("w_in_odd"))
    whole("w_out_even", received("w_out_even"))
    whole("w_in_even", received("w_in_even", out["w_out_even"][1]))
    small_shapes = [Pw[k].shape for k in _SMALL]
    pw, pm, pv = (_pack([Q[k] for k in _SMALL]) for Q in (Pw, Pm, Pv))
    sg, sd, sm, sv = _adamw(pw, received("small", out["w_in_even"][1]), pm, pv, "adamw_small")
    for arrs_i, packed_out in enumerate((sg, sd, sm, sv)):
        for k, a in zip(_SMALL, _unpack(packed_out, small_shapes)):
            out.setdefault(k, [None] * 4)[arrs_i] = a
    n_small = sum(int(np.prod(s)) for s in small_shapes)
    gflat = sg.reshape(-1)
    g_lconv = gflat[n_small:n_small + CONV_K * LRU_WIDTH].reshape(CONV_K, LRU_WIDTH)
    g_gconv = gflat[n_small + CONV_K * LRU_WIDTH:n_small + CONV_K * (LRU_WIDTH + 3072)].reshape(CONV_K, 3072)
    whole("lru_conv_w", lax.dynamic_slice_in_dim(g_lconv, me * 64, 64, axis=1)[None])
    whole("gdn_conv_w", lax.dynamic_slice_in_dim(g_gconv, me * 384, 384, axis=1)[None])

    names = ["mixer_norm_w", "mlp_norm_w", "final_norm_w", "w_in_even", "lru_conv_w", "lru_conv_b", "lru_w_r", "lru_b_r",
             "lru_w_i", "lru_b_i", "lru_lambda", "w_out_even", "w_in_odd", "gdn_conv_w", "gdn_a_log", "gdn_dt_bias",
             "gdn_norm_w", "w_out_odd", "w_up", "w_down"]
    total = gflat[n_small + CONV_K * (LRU_WIDTH + 3072)]
    res = [total, dx[None]]
    for j in range(4):
        res += [out[k][j] for k in names]
    return tuple(res)
```

```python
import math

import numpy as np
import jax
import jax.numpy as jnp
from jax import lax
from jax.experimental import pallas as pl
from jax.experimental.pallas import tpu as pltpu

f32 = jnp.float32
bf16 = jnp.bfloat16

N_DEV = 8
D_MODEL = 1024
D_FF = 4096
EPS = 1e-6
RET_HEADS = 4
RET_CHUNK = 128
RET_STEP = 4
ROPE_THETA = 10000.0
LRU_WIDTH = 512
LRU_C = 8.0
GDN_HEADS = 8
GDN_CHUNK = 64
GDN_STEP = 4
HEAD_DIM = 128
ODD_IN = 4112
ODD_PAD = 4224
ODD_SHARD = ODD_IN // N_DEV
ODD_SHARD_PAD = 640
ADAM_LR, ADAM_B1, ADAM_B2, ADAM_EPS, ADAM_WD, ADAM_STEP = 0.001, 0.9, 0.999, 1e-08, 0.01, 10
VMEM_LIMIT = 56 * 1024 * 1024

_NN = (((1,), (0,)), ((), ()))
_NT = (((1,), (1,)), ((), ()))
_TN = (((0,), (0,)), ((), ()))
MESH = pl.DeviceIdType.MESH


def _cparams(sem):
    return pltpu.CompilerParams(dimension_semantics=sem, vmem_limit_bytes=VMEM_LIMIT)


def _dot(a, b, dn):
    return lax.dot_general(a.astype(bf16), b.astype(bf16), dn, preferred_element_type=f32)


def _dot01(a01, b, dn):
    a = a01.astype(bf16)
    b0 = b.astype(bf16)
    r1 = b - b0.astype(f32)
    b1 = r1.astype(bf16)
    b2 = (r1 - b1.astype(f32)).astype(bf16)
    d = lambda q: lax.dot_general(a, q, dn, preferred_element_type=f32)
    return d(b0) + (d(b1) + d(b2))


def _sigmoid(x):
    return jax.nn.sigmoid(x)


def _silu(x):
    return x * _sigmoid(x)


def _dsilu(x):
    s = _sigmoid(x)
    return s * (1.0 + x * (1.0 - s))


def _softplus(x):
    return jnp.maximum(x, 0.0) + jnp.log1p(jnp.exp(-jnp.abs(x)))


_GELU_C = math.sqrt(2.0 / math.pi)


def _gelu(y):
    return 0.5 * y * (1.0 + jnp.tanh(_GELU_C * (y + 0.044715 * y * y * y)))


def _dgelu(y):
    t = jnp.tanh(_GELU_C * (y + 0.044715 * y * y * y))
    return 0.5 * (1.0 + t) + 0.5 * y * (1.0 - t * t) * _GELU_C * (1.0 + 3.0 * 0.044715 * y * y)


def _matmul(a, b, form, *, outs, name, epilogue=None, extras=(), tm=4096, tn=512, tk=1024, shard_cols=0, a_map=None):
    if form == "tn":
        K, M = a.shape
    else:
        M, K = a.shape
    if b.ndim == 3:
        assert form in ("nn", "nt"), name
        N = b.shape[1] if form == "nt" else N_DEV * b.shape[2]
        if form == "nn":
            tn = b.shape[2]
        else:
            tk = b.shape[2]
    else:
        N = b.shape[0] if form == "nt" else b.shape[1]
    ns = N // N_DEV
    if shard_cols:
        tn = ns * shard_cols
    tm, tn, tk = min(tm, M), min(tn, N), min(tk, K)
    assert M % tm == 0 and N % tn == 0 and K % tk == 0, (name, M, N, K, tm, tn, tk)
    nk = K // tk
    dn = {"nn": _NN, "nt": _NT, "tn": _TN}[form]
    if form == "tn":
        a_spec = pl.BlockSpec((tk, tm), lambda i, j, k: (k, i))
    else:
        a_spec = pl.BlockSpec((tm, tk), lambda i, j, k: (i, k))
    if b.ndim == 3:
        b_spec = (pl.BlockSpec((None, tn, tk), lambda i, j, k: (k, j, 0)) if form == "nt"
                  else pl.BlockSpec((None, tk, tn), lambda i, j, k: (j, k, 0)))
    elif form == "nt":
        b_spec = pl.BlockSpec((tn, tk), lambda i, j, k: (j, k))
    else:
        b_spec = pl.BlockSpec((tk, tn), lambda i, j, k: (k, j))
    e_spec = pl.BlockSpec((tm, tn), lambda i, j, k: (i, j))
    v_spec = pl.BlockSpec((1, tn), lambda i, j, k: (0, j))
    if shard_cols:
        o_spec = pl.BlockSpec((shard_cols, tm, ns), lambda i, j, k: (j, i, 0))
        o_shape = (N_DEV, M, ns)
    else:
        o_spec = e_spec
        o_shape = (M, N)
    n_ex = len(extras)
    sums = [isinstance(o, tuple) for o in outs]
    assert not any(sums) or tn == N, name

    def finish(acc, ex, o_refs, row_tile):
        vals = (acc,) if epilogue is None else epilogue(acc, *[e[...] for e in ex])
        for r, v, is_sum in zip(o_refs, vals, sums):
            if is_sum:
                @pl.when(row_tile == 0)
                def _(r=r, v=v):
                    r[...] = v.astype(r.dtype)

                @pl.when(row_tile > 0)
                def _(r=r, v=v):
                    r[...] += v.astype(r.dtype)
            elif shard_cols:
                for s in range(shard_cols):
                    r[s] = v[:, s * ns:(s + 1) * ns].astype(r.dtype)
            else:
                r[...] = v.astype(r.dtype)

    def prod(a_ref, b_ref):
        av = a_ref[...]
        return _dot(av if a_map is None else a_map(av), b_ref[...], dn)

    def body_one(*refs):
        finish(prod(*refs[:2]), refs[2:2 + n_ex], refs[2 + n_ex:], pl.program_id(0))

    def body_acc(*refs):
        a_ref, b_ref = refs[:2]
        acc = refs[-1]
        k = pl.program_id(2)
        row_tile = pl.program_id(0)

        @pl.when(k == 0)
        def _():
            acc[...] = prod(a_ref, b_ref)

        @pl.when((k > 0) & (k < nk - 1))
        def _():
            acc[...] += prod(a_ref, b_ref)

        @pl.when(k == nk - 1)
        def _():
            finish(acc[...] + prod(a_ref, b_ref), refs[2:2 + n_ex], refs[2 + n_ex:-1], row_tile)

    return pl.pallas_call(
        body_one if nk == 1 else body_acc, grid=(M // tm, N // tn, nk),
        in_specs=[a_spec, b_spec] + [v_spec if e.shape[0] == 1 else e_spec for e in extras],
        out_specs=[v_spec if s else o_spec for s in sums],
        out_shape=[jax.ShapeDtypeStruct((1, N), o[1]) if s else jax.ShapeDtypeStruct(o_shape, o) for o, s in zip(outs, sums)],
        scratch_shapes=[] if nk == 1 else [pltpu.VMEM((tm, tn), f32)],
        compiler_params=_cparams(("arbitrary" if any(sums) else "parallel", "parallel", "arbitrary")), name=name,
    )(a, b, *extras)


def _rms_fwd(x, w, name):
    T, D = x.shape
    tt = min(512, T)

    def body(x_ref, w_ref, h_ref):
        xv = x_ref[...]
        r = lax.rsqrt(jnp.mean(xv * xv, axis=1, keepdims=True) + EPS)
        h_ref[...] = (xv * r * w_ref[...]).astype(bf16)

    return pl.pallas_call(
        body, grid=(T // tt,),
        in_specs=[pl.BlockSpec((tt, D), lambda i: (i, 0)), pl.BlockSpec((1, D), lambda i: (0, 0))],
        out_specs=pl.BlockSpec((tt, D), lambda i: (i, 0)),
        out_shape=jax.ShapeDtypeStruct((T, D), bf16),
        compiler_params=_cparams(("parallel",)), name=name,
    )(x, w)


def _residual_rms_ep(acc, res, w):
    x = res + acc
    r = lax.rsqrt(jnp.mean(x * x, axis=1, keepdims=True) + EPS)
    return x, x * r * w


_RMS_BWD_OUTS = [f32, bf16, ("sum", f32)]


def _rms_bwd_ep(dh, x, dres, w):
    r = lax.rsqrt(jnp.mean(x * x, axis=1, keepdims=True) + EPS)
    xn = x * r
    dhw = dh * w
    dx = dres + r * (dhw - xn * jnp.mean(dhw * xn, axis=1, keepdims=True))
    return dx, dx, jnp.sum(dh * xn, axis=0, keepdims=True)


_LOSS_OUTS = [("sum", f32), f32, bf16, ("sum", f32)]


def _loss_ep(acc, res, w, tgt):
    x = res + acc
    D = x.shape[1]
    r = lax.rsqrt(jnp.mean(x * x, axis=1, keepdims=True) + EPS)
    xn = x * r
    e = xn * w - tgt
    loss = 0.5 * jnp.sum(jnp.mean(e * e, axis=1, keepdims=True), axis=0, keepdims=True)
    dy = e * (1.0 / D)
    dyw = dy * w
    dx = r * (dyw - xn * jnp.mean(dyw * xn, axis=1, keepdims=True))
    return jnp.broadcast_to(loss, (1, D)), dx, dx, jnp.sum(dy * xn, axis=0, keepdims=True)


def _ret_tables():
    H, C = RET_HEADS, RET_CHUNK
    lg = np.log1p(-np.exp2(-5.0 - np.arange(H, dtype=np.float32))).astype(np.float32)
    idx = np.arange(C, dtype=np.float32)
    diff = idx[:, None] - idx[None, :]
    causal = diff >= 0
    dm = np.where(causal[None], np.exp(lg[:, None, None] * np.where(causal, diff, 0.0)[None]), 0.0)
    qd = np.exp(lg[:, None] * (idx[None, :] + 1.0))
    kd = np.exp(lg[:, None] * (C - 1.0 - idx[None, :]))
    cg = np.exp(lg * C)
    tab = np.zeros((H, 4, C, HEAD_DIM), np.float32)
    tab[:, 0] = dm
    tab[:, 1] = qd[:, :, None]
    tab[:, 2] = kd[:, :, None]
    tab[:, 3] = cg[:, None, None]
    return jnp.asarray(tab)


def _rope_tables(T):
    half = HEAD_DIM // 2
    inv = ROPE_THETA ** (-jnp.arange(half, dtype=f32) / half)
    ang = jnp.arange(T, dtype=jnp.int32).astype(f32)[:, None] * inv[None, :]
    c, s = jnp.cos(ang), jnp.sin(ang)
    return jnp.concatenate([c, c], axis=1), jnp.concatenate([-s, s], axis=1)


def _rope(x, cos, sin):
    return x * cos + pltpu.roll(x, HEAD_DIM // 2, 1) * sin


def _unrope(y, cos, sin):
    return y * cos + pltpu.roll(y * sin, HEAD_DIM // 2, 1)


def _stack_heads(ref, H, f=None):
    parts = [ref[:, h * HEAD_DIM:(h + 1) * HEAD_DIM] for h in range(H)]
    return jnp.stack(parts if f is None else [f(a) for a in parts])


def _ret_fwd(p, cos, sin, tab, name):
    T = p.shape[0]
    C, H = RET_CHUNK, RET_HEADS
    N = T // C
    K = min(RET_STEP, N)
    NS = N // K
    scale = HEAD_DIM ** -0.5

    def body(q_ref, k_ref, v_ref, g_ref, c_ref, s_ref, t_ref, y_ref, o_ref, sp_ref, st):
        @pl.when(pl.program_id(0) == 0)
        def _():
            st[...] = jnp.zeros_like(st)

        dm, qd, kd, cg = t_ref[:, 0], t_ref[:, 1], t_ref[:, 2], t_ref[:, 3]
        S = st[...]
        for c in range(K):
            rows = pl.ds(c * C, C)
            cos_, sin_ = c_ref[rows, :], s_ref[rows, :]
            rot = lambda a: _rope(a, cos_, sin_)
            q = _stack_heads(q_ref.at[rows, :], H, rot)
            k = _stack_heads(k_ref.at[rows, :], H, rot) * scale
            v = _stack_heads(v_ref.at[rows, :], H)
            P = _dot(q, k, _NT3) * dm
            o = _dot(P, v, _NN3) + _dot(q * qd, S, _NN3)
            sp_ref[c] = S
            S = cg * S + _dot(k * kd, v, _TN3)
            r = lax.rsqrt(jnp.mean(o * o, axis=2, keepdims=True) + EPS)
            y = o * r * _silu(_stack_heads(g_ref.at[rows, :], H))
            for h in range(H):
                o_ref[rows, h * HEAD_DIM:(h + 1) * HEAD_DIM] = o[h]
                y_ref[rows, h * HEAD_DIM:(h + 1) * HEAD_DIM] = y[h].astype(bf16)
        st[...] = S

    wide = lambda blk: pl.BlockSpec((K * C, H * HEAD_DIM), lambda n: (n, blk))
    tbl = pl.BlockSpec((K * C, HEAD_DIM), lambda n: (n, 0))
    return pl.pallas_call(
        body, grid=(NS,),
        in_specs=[wide(0), wide(1), wide(2), wide(3), tbl, tbl,
                  pl.BlockSpec((H, 4, C, HEAD_DIM), lambda n: (0, 0, 0, 0))],
        out_specs=[wide(0), wide(0), pl.BlockSpec((K, H, HEAD_DIM, HEAD_DIM), lambda n: (n, 0, 0, 0))],
        out_shape=[jax.ShapeDtypeStruct((T, D_MODEL), bf16), jax.ShapeDtypeStruct((T, H * HEAD_DIM), f32),
                   jax.ShapeDtypeStruct((N, H, HEAD_DIM, HEAD_DIM), f32)],
        scratch_shapes=[pltpu.VMEM((H, HEAD_DIM, HEAD_DIM), f32)],
        compiler_params=_cparams(("arbitrary",)), name=name,
    )(p, p, p, p, cos, sin, tab)


def _ret_bwd(p, cos, sin, tab, o_raw, sprev, dmix, name):
    T = p.shape[0]
    C, H = RET_CHUNK, RET_HEADS
    N = T // C
    K = min(RET_STEP, N)
    NS = N // K
    scale = HEAD_DIM ** -0.5
    W = H * HEAD_DIM

    def body(q_ref, k_ref, v_ref, g_ref, c_ref, s_ref, t_ref, o_ref, sp_ref, dy_ref, d_ref, dst):
        @pl.when(pl.program_id(0) == 0)
        def _():
            dst[...] = jnp.zeros_like(dst)

        dm, qd, kd, cg = t_ref[:, 0], t_ref[:, 1], t_ref[:, 2], t_ref[:, 3]
        dS1 = dst[...]
        for c in reversed(range(K)):
            rows = pl.ds(c * C, C)
            cos_, sin_ = c_ref[rows, :], s_ref[rows, :]
            rot = lambda a: _rope(a, cos_, sin_)
            q = _stack_heads(q_ref.at[rows, :], H, rot)
            k = _stack_heads(k_ref.at[rows, :], H, rot) * scale
            v = _stack_heads(v_ref.at[rows, :], H)
            g = _stack_heads(g_ref.at[rows, :], H)
            S = sp_ref[c]
            o = _stack_heads(o_ref.at[rows, :], H)
            dy = _stack_heads(dy_ref.at[rows, :], H)
            r = lax.rsqrt(jnp.mean(o * o, axis=2, keepdims=True) + EPS)
            nrm = o * r
            dn = dy * _silu(g)
            dg = dy * nrm * _dsilu(g)
            do = r * (dn - nrm * jnp.mean(dn * nrm, axis=2, keepdims=True))
            P = _dot(q, k, _NT3) * dm
            dP = _dot(do, v, _NT3) * dm
            dq = _dot(dP, k, _NN3) + _dot(do, S, _NT3) * qd
            dk = (_dot(dP, q, _TN3) + _dot(v, dS1, _NT3) * kd) * scale
            dv = _dot(P, do, _TN3) + _dot(k * kd, dS1, _NN3)
            dS1 = cg * dS1 + _dot(q * qd, do, _TN3)
            for h in range(H):
                d_ref[rows, h * HEAD_DIM:(h + 1) * HEAD_DIM] = _unrope(dq[h], cos_, sin_).astype(bf16)
                d_ref[rows, W + h * HEAD_DIM:W + (h + 1) * HEAD_DIM] = _unrope(dk[h], cos_, sin_).astype(bf16)
                d_ref[rows, 2 * W + h * HEAD_DIM:2 * W + (h + 1) * HEAD_DIM] = dv[h].astype(bf16)
                d_ref[rows, 3 * W + h * HEAD_DIM:3 * W + (h + 1) * HEAD_DIM] = dg[h].astype(bf16)
        dst[...] = dS1

    rev = lambda blk: pl.BlockSpec((K * C, W), lambda n: (NS - 1 - n, blk))
    tbl = pl.BlockSpec((K * C, HEAD_DIM), lambda n: (NS - 1 - n, 0))
    return pl.pallas_call(
        body, grid=(NS,),
        in_specs=[rev(0), rev(1), rev(2), rev(3), tbl, tbl,
                  pl.BlockSpec((H, 4, C, HEAD_DIM), lambda n: (0, 0, 0, 0)), rev(0),
                  pl.BlockSpec((K, H, HEAD_DIM, HEAD_DIM), lambda n: (NS - 1 - n, 0, 0, 0)), rev(0)],
        out_specs=pl.BlockSpec((K * C, 4 * W), lambda n: (NS - 1 - n, 0)),
        out_shape=jax.ShapeDtypeStruct((T, 6 * W), bf16),
        scratch_shapes=[pltpu.VMEM((H, HEAD_DIM, HEAD_DIM), f32)],
        compiler_params=_cparams(("arbitrary",)), name=name,
    )(p, p, p, p, cos, sin, tab, o_raw, sprev, dmix)


CONV_K = 4
CONV_W = 512
PAD = 8
SUB_R = 64


def _conv_fwd(x, col_off, w, b, act, name):
    T = x.shape[0]
    C = w.shape[1]
    G = C // CONV_W
    tt = min(512, T)
    NT = T // tt
    has_b = b is not None

    def body(*refs):
        if has_b:
            x_ref, w_ref, b_ref, y_ref, pad = refs
        else:
            x_ref, w_ref, y_ref, pad = refs
        t = pl.program_id(1)

        @pl.when(t == 0)
        def _():
            pad[pl.ds(0, PAD), :] = jnp.zeros((PAD, CONV_W), f32)

        pad[pl.ds(PAD, tt), :] = x_ref[...]
        for g in range(CONV_W // 128):
            ls = slice(g * 128, (g + 1) * 128)
            wv = w_ref[:, ls]
            for c in range(tt // SUB_R):
                r0 = c * SUB_R
                y = wv[0:1, :] * pad[pl.ds(PAD - 3 + r0, SUB_R), ls]
                for kk in range(1, CONV_K):
                    y = y + wv[kk:kk + 1, :] * pad[pl.ds(PAD - 3 + kk + r0, SUB_R), ls]
                if has_b:
                    y = y + b_ref[:, ls]
                y_ref[pl.ds(r0, SUB_R), ls] = _silu(y) if act else y
        tail = pad[pl.ds(tt, PAD), :]
        pad[pl.ds(0, PAD), :] = tail

    in_specs = [pl.BlockSpec((tt, CONV_W), lambda g, t: (t, col_off + g)),
                pl.BlockSpec((CONV_K, CONV_W), lambda g, t: (0, g))]
    args = [x, w]
    if has_b:
        in_specs.append(pl.BlockSpec((1, CONV_W), lambda g, t: (0, g)))
        args.append(b)
    return pl.pallas_call(
        body, grid=(G, NT), in_specs=in_specs,
        out_specs=pl.BlockSpec((tt, CONV_W), lambda g, t: (t, g)),
        out_shape=jax.ShapeDtypeStruct((T, C), f32),
        scratch_shapes=[pltpu.VMEM((tt + PAD, CONV_W), f32)],
        compiler_params=_cparams(("parallel", "arbitrary")), name=name,
    )(*args)


def _conv_bwd(x, col_off, w, b, act, dout, dp, name):
    T = x.shape[0]
    C = w.shape[1]
    G = C // CONV_W
    tt = min(512, T)
    NT = T // tt
    has_b = b is not None

    def body(*refs):
        if has_b:
            x_ref, xp_ref, w_ref, b_ref, d_ref, dp_in, dx_ref, dw_ref, db_ref, pad, dpad = refs
        else:
            x_ref, xp_ref, w_ref, d_ref, dp_in, dx_ref, dw_ref, db_ref, pad, dpad = refs
        t = pl.program_id(1)
        first_tile = t == NT - 1

        @pl.when(t == 0)
        def _():
            dpad[pl.ds(tt, PAD), :] = jnp.zeros((PAD, CONV_W), f32)
            dw_ref[...] = jnp.zeros_like(dw_ref)
            db_ref[...] = jnp.zeros_like(db_ref)

        pad[pl.ds(0, PAD), :] = jnp.where(first_tile, 0.0, xp_ref[...])
        pad[pl.ds(PAD, tt), :] = x_ref[...]
        fold = lambda v: v.reshape(SUB_R // 8, 8, 128).sum(axis=0)
        for g in range(CONV_W // 128):
            ls = slice(g * 128, (g + 1) * 128)
            wv = w_ref[:, ls]
            acc = [jnp.zeros((8, 128), f32) for _ in range(CONV_K + 1)]
            for c in reversed(range(tt // SUB_R)):
                r0 = c * SUB_R
                xs = [pad[pl.ds(PAD - 3 + kk + r0, SUB_R), ls] for kk in range(CONV_K)]
                dy = d_ref[pl.ds(r0, SUB_R), ls]
                if act:
                    y = wv[0:1, :] * xs[0]
                    for kk in range(1, CONV_K):
                        y = y + wv[kk:kk + 1, :] * xs[kk]
                    if has_b:
                        y = y + b_ref[:, ls]
                    dy = dy * _dsilu(y)
                dpad[pl.ds(r0, SUB_R), ls] = dy
                dx = wv[3:4, :] * dy
                for j in range(1, CONV_K):
                    dx = dx + wv[3 - j:4 - j, :] * dpad[pl.ds(r0 + j, SUB_R), ls]
                dx_ref[pl.ds(r0, SUB_R), ls] = dx.astype(bf16)
                for kk in range(CONV_K):
                    acc[kk] = acc[kk] + fold(dy * xs[kk])
                acc[CONV_K] = acc[CONV_K] + fold(dy)
            for kk in range(CONV_K):
                dw_ref[kk:kk + 1, ls] += jnp.sum(acc[kk], axis=0, keepdims=True)
            db_ref[:, ls] += jnp.sum(acc[CONV_K], axis=0, keepdims=True)
        head = dpad[pl.ds(0, PAD), :]
        dpad[pl.ds(tt, PAD), :] = head

    rows8 = tt // PAD
    in_specs = [pl.BlockSpec((tt, CONV_W), lambda g, t: (NT - 1 - t, col_off + g)),
                pl.BlockSpec((PAD, CONV_W), lambda g, t: (jnp.maximum((NT - 1 - t) * rows8 - 1, 0), col_off + g)),
                pl.BlockSpec((CONV_K, CONV_W), lambda g, t: (0, g))]
    args = [x, x, w]
    if has_b:
        in_specs.append(pl.BlockSpec((1, CONV_W), lambda g, t: (0, g)))
        args.append(b)
    in_specs += [pl.BlockSpec((tt, CONV_W), lambda g, t: (NT - 1 - t, g)), pl.BlockSpec(memory_space=pl.ANY)]
    args += [dout, dp]
    return pl.pallas_call(
        body, grid=(G, NT), in_specs=in_specs,
        out_specs=[pl.BlockSpec((tt, CONV_W), lambda g, t: (NT - 1 - t, col_off + g)),
                   pl.BlockSpec((CONV_K, CONV_W), lambda g, t: (0, g)),
                   pl.BlockSpec((1, CONV_W), lambda g, t: (0, g))],
        out_shape=[jax.ShapeDtypeStruct(dp.shape, dp.dtype), jax.ShapeDtypeStruct((CONV_K, C), f32),
                   jax.ShapeDtypeStruct((1, C), f32)],
        input_output_aliases={len(args) - 1: 0},
        scratch_shapes=[pltpu.VMEM((tt + PAD, CONV_W), f32), pltpu.VMEM((tt + PAD, CONV_W), f32)],
        compiler_params=_cparams(("parallel", "arbitrary")), name=name,
    )(*args)


def _lru_gates(xc, wr, wi, br, bi, lam):
    r = _sigmoid(_dot(xc, wr, _NN) + br)
    i = _sigmoid(_dot(xc, wi, _NN) + bi)
    sp = _softplus(-lam)
    a = jnp.exp(-LRU_C * r * sp)
    mult = jnp.sqrt(1.0 - a * a)
    return r, i, sp, a, mult


def _lru_fwd(xc, p, y_off, wr, wi, br, bi, lam, mix, name):
    T = xc.shape[0]
    G = LRU_WIDTH // 128
    tt = min(512, T)
    NT = T // tt

    def body(x_ref, y_ref, wr_ref, wi_ref, br_ref, bi_ref, l_ref, mix_in, o_ref, h_ref, hc):
        t = pl.program_id(1)

        @pl.when(t == 0)
        def _():
            hc[...] = jnp.zeros_like(hc)

        x = x_ref[...]
        r, i, sp, a, mult = _lru_gates(x, wr_ref[...], wi_ref[...], br_ref[...], bi_ref[...], l_ref[...])
        row = lax.broadcasted_iota(jnp.int32, (tt, 128), 0)
        mult = jnp.where((row == 0) & (t == 0), 1.0, mult)
        U = x * i * mult
        A = a
        d = 1
        while d < tt:
            keep = row >= d
            Ush = jnp.where(keep, pltpu.roll(U, d, 0), 0.0)
            Ash = jnp.where(keep, pltpu.roll(A, d, 0), 1.0)
            U = A * Ush + U
            A = A * Ash
            d *= 2
        h = U + A * hc[0:1, :]
        h_ref[...] = h
        hc[...] = jnp.broadcast_to(h[tt - 1:tt, :], hc.shape)
        o_ref[...] = (h * _gelu(y_ref[...])).astype(bf16)

    tile = pl.BlockSpec((tt, 128), lambda g, t: (t, g))
    vec = pl.BlockSpec((1, 128), lambda g, t: (0, g))
    wsp = pl.BlockSpec((128, 128), lambda g, t: (g, g))
    return pl.pallas_call(
        body, grid=(G, NT),
        in_specs=[tile, pl.BlockSpec((tt, 128), lambda g, t: (t, y_off + g)), wsp, wsp, vec, vec, vec,
                  pl.BlockSpec(memory_space=pl.ANY)],
        out_specs=[pl.BlockSpec((tt, 128), lambda g, t: (t, G + g)), tile],
        out_shape=[jax.ShapeDtypeStruct(mix.shape, mix.dtype), jax.ShapeDtypeStruct((T, LRU_WIDTH), f32)],
        input_output_aliases={7: 0},
        scratch_shapes=[pltpu.VMEM((8, 128), f32)],
        compiler_params=_cparams(("parallel", "arbitrary")), name=name,
    )(xc, p, wr, wi, br, bi, lam, mix)


def _lru_bwd(xc, p, y_off, wr, wi, br, bi, lam, hs, dmix, d_off, dp, name):
    T = xc.shape[0]
    G = LRU_WIDTH // 128
    tt = min(512, T)
    NT = T // tt

    def body(x_ref, y_ref, wr_ref, wi_ref, br_ref, bi_ref, l_ref, h_ref, hp_ref, do_ref, dp_in,
             dx_ref, dy_ref, dwr_ref, dwi_ref, dbr_ref, dbi_ref, dl_ref, lc, an):
        t = pl.program_id(1)
        first_tile = t == NT - 1

        @pl.when(t == 0)
        def _():
            lc[...] = jnp.zeros_like(lc)
            an[...] = jnp.zeros_like(an)
            dwr_ref[...] = jnp.zeros_like(dwr_ref)
            dwi_ref[...] = jnp.zeros_like(dwi_ref)
            dbr_ref[...] = jnp.zeros_like(dbr_ref)
            dbi_ref[...] = jnp.zeros_like(dbi_ref)
            dl_ref[...] = jnp.zeros_like(dl_ref)

        x = x_ref[...]
        y = y_ref[...]
        wr, wi, lam_ = wr_ref[...], wi_ref[...], l_ref[...]
        r, i, sp, a, mult_raw = _lru_gates(x, wr, wi, br_ref[...], bi_ref[...], lam_)
        row = lax.broadcasted_iota(jnp.int32, (tt, 128), 0)
        t0 = (row == 0) & first_tile
        mult = jnp.where(t0, 1.0, mult_raw)
        h = h_ref[...]
        do = do_ref[...]
        dh = do * _gelu(y)
        dy_ref[...] = (do * h * _dgelu(y)).astype(bf16)
        B = jnp.where(row == tt - 1, an[0:1, :], pltpu.roll(a, tt - 1, 0))
        L = dh
        d = 1
        while d < tt:
            keep = row < tt - d
            Lsh = jnp.where(keep, pltpu.roll(L, tt - d, 0), 0.0)
            Bsh = jnp.where(keep, pltpu.roll(B, tt - d, 0), 1.0)
            L = L + B * Lsh
            B = B * Bsh
            d *= 2
        L = L + B * lc[0:1, :]
        lc[...] = jnp.broadcast_to(L[0:1, :], lc.shape)
        an[...] = jnp.broadcast_to(a[0:1, :], an.shape)
        hprev = jnp.where(first_tile, 0.0, hp_ref[...])[PAD - 1:PAD, :]
        hm1 = jnp.where(row == 0, hprev, pltpu.roll(h, 1, 0))
        da = L * hm1
        dxc = L * i * mult
        di = L * x * mult
        dmult = jnp.where(t0, 0.0, L * x * i)
        da = da - jnp.where(t0, 0.0, dmult * a / mult_raw)
        dlog_a = da * a
        dr = dlog_a * (-LRU_C) * sp
        dsp = jnp.sum(dlog_a * (-LRU_C) * r, axis=0, keepdims=True)
        dpr = dr * r * (1.0 - r)
        dpi = di * i * (1.0 - i)
        dx_ref[...] = dxc + _dot(dpr, wr, _NT) + _dot(dpi, wi, _NT)
        dwr_ref[0] += _dot(x, dpr, _TN)
        dwi_ref[0] += _dot(x, dpi, _TN)
        dbr_ref[...] += jnp.sum(dpr, axis=0, keepdims=True)
        dbi_ref[...] += jnp.sum(dpi, axis=0, keepdims=True)
        dl_ref[...] += dsp * (-_sigmoid(-lam_))

    rows8 = tt // PAD
    tile = pl.BlockSpec((tt, 128), lambda g, t: (NT - 1 - t, g))
    vec = pl.BlockSpec((1, 128), lambda g, t: (0, g))
    wsp = pl.BlockSpec((128, 128), lambda g, t: (g, g))
    wout = pl.BlockSpec((1, 128, 128), lambda g, t: (g, 0, 0))
    return pl.pallas_call(
        body, grid=(G, NT),
        in_specs=[tile, pl.BlockSpec((tt, 128), lambda g, t: (NT - 1 - t, y_off + g)), wsp, wsp, vec, vec, vec, tile,
                  pl.BlockSpec((PAD, 128), lambda g, t: (jnp.maximum((NT - 1 - t) * rows8 - 1, 0), g)),
                  pl.BlockSpec((tt, 128), lambda g, t: (NT - 1 - t, d_off + g)), pl.BlockSpec(memory_space=pl.ANY)],
        out_specs=[tile, pl.BlockSpec((tt, 128), lambda g, t: (NT - 1 - t, y_off + g)), wout, wout, vec, vec, vec],
        out_shape=[jax.ShapeDtypeStruct((T, LRU_WIDTH), f32), jax.ShapeDtypeStruct(dp.shape, dp.dtype),
                   jax.ShapeDtypeStruct((G, 128, 128), f32), jax.ShapeDtypeStruct((G, 128, 128), f32),
                   jax.ShapeDtypeStruct((1, LRU_WIDTH), f32), jax.ShapeDtypeStruct((1, LRU_WIDTH), f32),
                   jax.ShapeDtypeStruct((1, LRU_WIDTH), f32)],
        input_output_aliases={10: 1},
        scratch_shapes=[pltpu.VMEM((8, 128), f32), pltpu.VMEM((8, 128), f32)],
        compiler_params=_cparams(("parallel", "arbitrary")), name=name,
    )(xc, p, wr, wi, br, bi, lam, hs, hs, dmix, dp)


_NN3 = (((2,), (1,)), ((0,), (0,)))
_NT3 = (((2,), (2,)), ((0,), (0,)))
_TN3 = (((1,), (1,)), ((0,), (0,)))


def _pairs(ref, K):
    C = GDN_CHUNK
    return jnp.stack([ref[c * C:(c + 1) * C, h * HEAD_DIM:(h + 1) * HEAD_DIM] for c in range(K) for h in range(GDN_HEADS)])


def _put_pairs(ref, val, K, col=0):
    C, H = GDN_CHUNK, GDN_HEADS
    for c in range(K):
        for h in range(H):
            ref[c * C:(c + 1) * C, col + h * HEAD_DIM:col + (h + 1) * HEAD_DIM] = val[c * H + h].astype(ref.dtype)


def _rowsum(x):
    H, C, L = x.shape
    return _dot(x.reshape(H * C, L), jnp.ones((L, HEAD_DIM), f32), _NN).reshape(H, C, HEAD_DIM)


def _gdn_pre(qr, kr, v, ba, alog, dtb):
    C, H = GDN_CHUNK, GDN_HEADS
    B = qr.shape[0]
    K = B // H
    lane = lax.broadcasted_iota(jnp.int32, (C, 128), 1)
    lane3 = lax.broadcasted_iota(jnp.int32, (B, C, 128), 2)
    ri = lax.broadcasted_iota(jnp.int32, (C, C), 0)
    ci = lax.broadcasted_iota(jnp.int32, (C, C), 1)
    rowc = lax.broadcasted_iota(jnp.int32, (C, 1), 0)
    col = lambda m, j: jnp.sum(jnp.where(lane == j, m, 0.0), axis=1, keepdims=True)
    ea = jnp.exp(alog)
    tri = (ri >= ci).astype(f32)
    g_all, beta_cols, G_cols = [], [], []
    for c in range(K):
        ba_c = ba[c * C:(c + 1) * C]
        g_c = -ea * _softplus(ba_c + dtb)
        G_c = _dot01(tri, g_c, _NN)
        s_c = _sigmoid(ba_c)
        g_all.append(g_c)
        beta_cols += [col(s_c, h) for h in range(H)]
        G_cols += [col(G_c, H + h) for h in range(H)]
    wide = lambda c: jnp.broadcast_to(c, (B, C, 128))
    beta = wide(jnp.stack(beta_cols))
    Gc = jnp.stack(G_cols)
    rq = lax.rsqrt(_rowsum(qr * qr) + EPS)
    rk = lax.rsqrt(_rowsum(kr * kr) + EPS)
    qh, kn = qr * rq, kr * rk
    qn = qh * (HEAD_DIM ** -0.5)
    Grow = _dot01(jnp.ones((B, C, 128), f32), jnp.where(lane3 == 0, Gc, 0.0), _NT3)
    incl = ri >= ci
    Di = jnp.where(incl, jnp.exp(jnp.where(incl, Gc - Grow, 0.0)), 0.0)
    Ds = jnp.where(ri > ci, Di, 0.0)
    Gl = jnp.sum(jnp.where(rowc == C - 1, Gc, 0.0), axis=1, keepdims=True)
    eG = wide(jnp.exp(Gc))
    eGl = wide(jnp.exp(Gl - Gc))
    cd = jnp.exp(Gl)
    kb = kn * beta
    vb = v * beta
    Lm = _dot(kb, kn, _NT3) * Ds
    kbg = kb * eG
    QK = _dot(qn, kn, _NT3) * Di
    qg = qn * eG
    kg = kn * eGl
    return dict(beta=beta, g_all=g_all, rq=rq, rk=rk, qh=qh, kn=kn, qn=qn, Di=Di, Ds=Ds, eG=eG, eGl=eGl, cd=cd,
                kb=kb, vb=vb, Lm=Lm, kbg=kbg, QK=QK, qg=qg, kg=kg, lane=lane, ri=ri, ci=ci, rowc=rowc, ea=ea)


def _unit_lower_inverse(Lm):
    C = Lm.shape[-1]
    ri = lax.broadcasted_iota(jnp.int32, (C, C), 0)
    ci = lax.broadcasted_iota(jnp.int32, (C, C), 1)
    same = lambda s: (ri // s) == (ci // s)
    Xd = jnp.where(same(8), -Lm, 0.0)
    Tinv = (ri == ci).astype(f32) + Xd
    Pw = Xd
    for _ in range(2):
        Pw = _dot(Pw, Pw, _NN3)
        Tinv = Tinv + _dot(Tinv, Pw, _NN3)
    for s in (8, 16, 32):
        off = jnp.where(same(2 * s) & jnp.logical_not(same(s)), Lm, 0.0)
        Tinv = Tinv - _dot(_dot(Tinv, off, _NN3), Tinv, _NN3)
    return Tinv


def _gdn_specs(T, rev):
    C = GDN_CHUNK
    H = GDN_HEADS
    K = min(GDN_STEP, T // C)
    NS = T // (C * K)
    nn = (lambda n: NS - 1 - n) if rev else (lambda n: n)
    wide = lambda blk: pl.BlockSpec((K * C, H * HEAD_DIM), lambda n: (nn(n), blk))
    one = lambda off: pl.BlockSpec((K * C, HEAD_DIM), lambda n: (nn(n), off))
    vec = pl.BlockSpec((1, 128), lambda n: (0, 0))
    st = lambda rows: pl.BlockSpec((K, H, rows, rows), lambda n: (nn(n), 0, 0, 0))
    return K, NS, wide, one, vec, st


def _gdn_fwd(qkv, p, alog, dtb, nw, name):
    T = qkv.shape[0]
    C, H = GDN_CHUNK, GDN_HEADS
    N = T // C
    K, NS, wide, one, vec, st_spec = _gdn_specs(T, False)

    def body(q_ref, k_ref, v_ref, z_ref, ba_ref, al_ref, dt_ref, nw_ref, y_ref, sp_ref, ti_ref, vn_ref, o_ref, st):
        @pl.when(pl.program_id(0) == 0)
        def _():
            st[...] = jnp.zeros_like(st)

        f = _gdn_pre(_pairs(q_ref, K), _pairs(k_ref, K), _pairs(v_ref, K), ba_ref[...], al_ref[...], dt_ref[...])
        Tinv = _unit_lower_inverse(f["Lm"])
        ti_ref[...] = Tinv.reshape(K, H, C, C)
        w = _dot(Tinv, f["kbg"], _NN3)
        u = _dot(Tinv, f["vb"], _NN3)
        S = st[...]
        vns, os_ = [], []
        for c in range(K):
            sl = slice(c * H, (c + 1) * H)
            sp_ref[c] = S
            vn_c = u[sl] - _dot(w[sl], S, _NN3)
            os_.append(_dot(f["qg"][sl], S, _NN3) + _dot(f["QK"][sl], vn_c, _NN3))
            S = S * f["cd"][sl] + _dot(f["kg"][sl], vn_c, _TN3)
            vns.append(vn_c)
        st[...] = S
        vn, o = jnp.concatenate(vns), jnp.concatenate(os_)
        r = lax.rsqrt(_rowsum(o * o) * (1.0 / HEAD_DIM) + EPS)
        _put_pairs(y_ref, o * r * nw_ref[...] * _silu(_pairs(z_ref, K)), K)
        _put_pairs(vn_ref, vn, K)
        _put_pairs(o_ref, o, K)

    wide_f32 = jax.ShapeDtypeStruct((T, H * HEAD_DIM), f32)
    return pl.pallas_call(
        body, grid=(NS,),
        in_specs=[wide(0), wide(1), wide(2), wide(3), one(4 * H), vec, vec, vec],
        out_specs=[wide(0), st_spec(HEAD_DIM), st_spec(C), wide(0), wide(0)],
        out_shape=[jax.ShapeDtypeStruct((T, H * HEAD_DIM), bf16), jax.ShapeDtypeStruct((N, H, HEAD_DIM, HEAD_DIM), f32),
                   jax.ShapeDtypeStruct((N, H, C, C), f32), wide_f32, wide_f32],
        scratch_shapes=[pltpu.VMEM((H, HEAD_DIM, HEAD_DIM), f32)],
        compiler_params=_cparams(("arbitrary",)), name=name,
    )(qkv, qkv, qkv, p, p, alog, dtb, nw)


def _gdn_bwd(qkv, p, alog, dtb, nw, sprev, tinv, vn_all, o_all, dy_all, name):
    T = qkv.shape[0]
    C, H = GDN_CHUNK, GDN_HEADS
    N = T // C
    K, NS, wide, one, vec, st_spec = _gdn_specs(T, True)
    rs = lambda m: jnp.sum(m, axis=2, keepdims=True)

    def body(q_ref, k_ref, v_ref, z_ref, ba_ref, al_ref, dt_ref, nw_ref, sp_ref, ti_ref, vn_ref, o_ref, dy_ref,
             dqkv_ref, dz_ref, dba_ref, dal_ref, ddt_ref, dnw_ref, dst):
        @pl.when(pl.program_id(0) == 0)
        def _():
            dst[...] = jnp.zeros_like(dst)
            dal_ref[...] = jnp.zeros_like(dal_ref)
            ddt_ref[...] = jnp.zeros_like(ddt_ref)
            dnw_ref[...] = jnp.zeros_like(dnw_ref)

        ba, dtb_, nwv = ba_ref[...], dt_ref[...], nw_ref[...]
        v = _pairs(v_ref, K)
        f = _gdn_pre(_pairs(q_ref, K), _pairs(k_ref, K), v, ba, al_ref[...], dtb_)
        beta, kn, qn, kb, vb, kbg = f["beta"], f["kn"], f["qn"], f["kb"], f["vb"], f["kbg"]
        eG, eGl, cd, Di, Ds, QK, qg, kg = f["eG"], f["eGl"], f["cd"], f["Di"], f["Ds"], f["QK"], f["qg"], f["kg"]
        lane, ri, ci, rowc = f["lane"], f["ri"], f["ci"], f["rowc"]
        Tinv = ti_ref[...].reshape(K * H, C, C)
        S = sp_ref[...].reshape(K * H, HEAD_DIM, HEAD_DIM)
        w_ = _dot(Tinv, kbg, _NN3)
        vn, o = _pairs(vn_ref, K), _pairs(o_ref, K)
        z, dy = _pairs(z_ref, K), _pairs(dy_ref, K)
        r = lax.rsqrt(_rowsum(o * o) * (1.0 / HEAD_DIM) + EPS)
        nrm = o * r
        sz = _silu(z)
        dn = dy * nwv * sz
        _put_pairs(dz_ref, dy * nrm * nwv * _dsilu(z), K)
        dnw_ref[...] += jnp.sum(jnp.sum(dy * nrm * sz, axis=0), axis=0, keepdims=True)
        do = r * (dn - nrm * (_rowsum(dn * nrm) * (1.0 / HEAD_DIM)))
        dvn_do = _dot(QK, do, _TN3)
        dS_do = _dot(qg, do, _TN3)
        dqg = _dot(do, S, _NT3)
        dQK = _dot(do, vn, _NT3)
        dS = dst[...]
        dS1s, dvns = [None] * K, [None] * K
        for c in reversed(range(K)):
            sl = slice(c * H, (c + 1) * H)
            dS1s[c] = dS
            dvns[c] = _dot(kg[sl], dS, _NN3) + dvn_do[sl]
            dS = cd[sl] * dS + dS_do[sl] - _dot(w_[sl], dvns[c], _TN3)
        dst[...] = dS
        dS1, dvn = jnp.concatenate(dS1s), jnp.concatenate(dvns)
        dcd = jnp.sum(jnp.sum(S * dS1, axis=2, keepdims=True), axis=1, keepdims=True)
        dkg = _dot(vn, dS1, _NT3)
        dw = -_dot(dvn, S, _NT3)
        dqn = dqg * eG
        dkn = dkg * eGl
        deGl = rs(dkg * kn)
        dQKr = dQK * Di
        E = dQK * QK
        dqn = dqn + _dot(dQKr, kn, _NN3)
        dkn = dkn + _dot(dQKr, qn, _TN3)
        dT = _dot(dvn, vb, _NT3) + _dot(dw, kbg, _NT3)
        dvb = _dot(Tinv, dvn, _TN3)
        dkbg = _dot(Tinv, dw, _TN3)
        dkb = dkbg * eG
        deG = rs(dqg * qn + dkbg * kb)
        dL = -_dot(_dot(Tinv, dT, _TN3), Tinv, _NT3)
        dKK = dL * Ds
        E = E + dL * f["Lm"]
        dkb = dkb + _dot(dKK, kn, _NN3)
        dkn = dkn + _dot(dKK, kb, _TN3) + dkb * beta
        dbeta = rs(dkb * kn + dvb * v)
        _put_pairs(dqkv_ref, dvb * beta, K, 2 * H * HEAD_DIM)
        dG = rs(E) - rs(jnp.swapaxes(E, 1, 2)) + deG * eG - deGl * eGl
        dGl = jnp.sum(deGl * eGl, axis=1, keepdims=True) + dcd * cd
        dG = dG + jnp.where(rowc == C - 1, dGl, 0.0)
        qh = f["qh"]
        _put_pairs(dqkv_ref, (HEAD_DIM ** -0.5) * f["rq"] * (dqn - qh * _rowsum(dqn * qh)), K)
        _put_pairs(dqkv_ref, f["rk"] * (dkn - kn * _rowsum(dkn * kn)), K, H * HEAD_DIM)
        db = dbeta * beta * (1.0 - beta)
        triu = (ri <= ci).astype(f32)
        for c in range(K):
            db_all = jnp.where(lane == 0, db[c * H], 0.0)
            dG_all = jnp.where(lane == H, dG[c * H], 0.0)
            for h in range(1, H):
                db_all = db_all + jnp.where(lane == h, db[c * H + h], 0.0)
                dG_all = dG_all + jnp.where(lane == H + h, dG[c * H + h], 0.0)
            dg_all = _dot01(triu, dG_all, _NN)
            da_all = dg_all * (-f["ea"]) * _sigmoid(ba[c * C:(c + 1) * C] + dtb_)
            dba_ref[c * C:(c + 1) * C, :] = (db_all + da_all).astype(bf16)
            ddt_ref[...] += jnp.sum(da_all, axis=0, keepdims=True)
            dal_ref[...] += jnp.sum(dg_all * f["g_all"][c], axis=0, keepdims=True)

    small = jax.ShapeDtypeStruct((1, 128), f32)
    return pl.pallas_call(
        body, grid=(NS,),
        in_specs=[wide(0), wide(1), wide(2), wide(3), one(4 * H), vec, vec, vec, st_spec(HEAD_DIM), st_spec(C),
                  wide(0), wide(0), wide(0)],
        out_specs=[pl.BlockSpec((K * C, 3 * H * HEAD_DIM), lambda n: (NS - 1 - n, 0)), wide(3), one(0), vec, vec, vec],
        out_shape=[jax.ShapeDtypeStruct((T, 3 * H * HEAD_DIM), f32), jax.ShapeDtypeStruct((T, ODD_PAD), bf16),
                   jax.ShapeDtypeStruct((T, 128), bf16), small, small, small],
        scratch_shapes=[pltpu.VMEM((H, HEAD_DIM, HEAD_DIM), f32)],
        compiler_params=_cparams(("arbitrary",)), name=name,
    )(qkv, qkv, qkv, p, p, alog, dtb, nw, sprev, tinv, vn_all, o_all, dy_all)


def _lanes_from(x, s):
    return x if s % 128 == 0 else pltpu.roll(x, (128 - s) % 128, 1)


def _odd_assemble(g, name):
    R = g.shape[1]
    tr = min(256, R)
    n_blk = ODD_SHARD_PAD // 128

    def body(g_ref, o_ref):
        lane = lax.broadcasted_iota(jnp.int32, (tr, 128), 1)
        blk = lambda d, m: g_ref[d, :, m * 128:(m + 1) * 128]
        for gb in range(ODD_PAD // 128):
            c0 = 128 * gb
            if c0 >= ODD_IN:
                o_ref[:, c0:c0 + 128] = jnp.zeros((tr, 128), g.dtype)
                continue
            d0 = c0 // ODD_SHARD
            m0, sh = divmod(c0 - ODD_SHARD * d0, 128)
            take = min(128, ODD_SHARD * (d0 + 1) - c0)
            p = _lanes_from(blk(d0, m0), sh)
            if sh and m0 + 1 < n_blk:
                p = jnp.where(lane < 128 - sh, p, _lanes_from(blk(d0, m0 + 1), sh))
            if take < 128:
                nxt = pltpu.roll(blk(d0 + 1, 0), take, 1) if d0 + 1 < N_DEV else jnp.zeros((tr, 128), g.dtype)
                p = jnp.where(lane < take, p, nxt)
            o_ref[:, c0:c0 + 128] = p

    return pl.pallas_call(
        body, grid=(R // tr,),
        in_specs=[pl.BlockSpec((N_DEV, tr, ODD_SHARD_PAD), lambda i: (0, i, 0))],
        out_specs=pl.BlockSpec((tr, ODD_PAD), lambda i: (i, 0)),
        out_shape=jax.ShapeDtypeStruct((R, ODD_PAD), g.dtype),
        compiler_params=_cparams(("parallel",)), name=name,
    )(g)


def _odd_split(w, name):
    R = w.shape[0]
    tr = min(256, R)

    def body(w_ref, o_ref):
        lane = lax.broadcasted_iota(jnp.int32, (tr, 128), 1)
        blk = lambda gb: w_ref[:, gb * 128:(gb + 1) * 128]
        for d in range(N_DEV):
            for m in range(ODD_SHARD_PAD // 128):
                g0, sh = divmod(ODD_SHARD * d + 128 * m, 128)
                p = _lanes_from(blk(g0), sh)
                if sh and g0 + 1 < ODD_PAD // 128:
                    p = jnp.where(lane < 128 - sh, p, _lanes_from(blk(g0 + 1), sh))
                real = ODD_SHARD - 128 * m
                if real < 128:
                    p = jnp.where(lane < real, p, jnp.zeros_like(p))
                o_ref[d, :, m * 128:(m + 1) * 128] = p

    return pl.pallas_call(
        body, grid=(R // tr,),
        in_specs=[pl.BlockSpec((tr, ODD_PAD), lambda i: (i, 0))],
        out_specs=pl.BlockSpec((N_DEV, tr, ODD_SHARD_PAD), lambda i: (0, i, 0)),
        out_shape=jax.ShapeDtypeStruct((N_DEV, R, ODD_SHARD_PAD), w.dtype),
        compiler_params=_cparams(("parallel",)), name=name,
    )(w)


def _adamw(w, gs, m, v, name, layer=None, prev=None):
    R, Cc = w.shape[-2:]
    S = gs.shape[0]
    tr = R
    if S * R * Cc * 4 > (4 << 20):
        for cand in (256, 128, 64, 32, 16, 8):
            if R % cand == 0 and R > cand:
                tr = cand
                break
    c1 = 1.0 - ADAM_B1 ** ADAM_STEP
    c2 = 1.0 - ADAM_B2 ** ADAM_STEP

    def body(w_ref, g_ref, m_ref, v_ref, *rest):
        go_ref, d_ref, mo_ref, vo_ref = rest[-4:]
        g = g_ref[0].astype(f32)
        for s in range(1, S):
            g = g + g_ref[s].astype(f32)
        mn = ADAM_B1 * m_ref[...] + (1.0 - ADAM_B1) * g
        vn = ADAM_B2 * v_ref[...] + (1.0 - ADAM_B2) * (g * g)
        go_ref[...] = g
        mo_ref[...] = mn
        vo_ref[...] = vn
        d_ref[...] = -ADAM_LR * ((mn / c1) / (jnp.sqrt(vn / c2) + ADAM_EPS) + ADAM_WD * w_ref[...])

    tc = Cc if gs.shape[-1] == Cc else 128
    if layer is None:
        blk = pl.BlockSpec((tr, tc), lambda i, j: (i, j))
    else:
        blk = pl.BlockSpec((None, tr, tc), lambda i, j: (layer, i, j))
    out = jax.ShapeDtypeStruct(w.shape, f32)
    carried = [] if prev is None else list(prev)
    return pl.pallas_call(
        body, grid=(R // tr, pl.cdiv(Cc, tc)),
        in_specs=[blk, pl.BlockSpec((S, tr, tc), lambda i, j: (0, i, j)), blk, blk]
        + [pl.BlockSpec(memory_space=pl.ANY)] * len(carried),
        out_specs=[blk] * 4, out_shape=[out] * 4,
        input_output_aliases={4 + j: j for j in range(len(carried))},
        compiler_params=_cparams(("parallel", "parallel")), name=name,
    )(w, gs, m, v, *carried)


def _me():
    x, y, c = lax.axis_index("x"), lax.axis_index("y"), lax.axis_index("c")
    return x, y, c, 4 * x + 2 * y + c


def _peer(k):
    x, y, c, _ = _me()
    px = 1 - x if k & 4 else x
    py = 1 - y if k & 2 else y
    pc = 1 - c if k & 1 else c
    return (px, py, pc), 4 * px + 2 * py + pc


_HBM = pl.BlockSpec(memory_space=pltpu.HBM)
_SEM = pl.BlockSpec(memory_space=pltpu.SEMAPHORE)
_EFFECT = pltpu.SideEffectType.DATAFLOW_SIDE_EFFECTING


def _copy(src, land, ssem, rsem, k, blocked, landing_slot_of_peer):
    pid, pidx = _peer(k)
    slot = pidx if landing_slot_of_peer else _me()[3]
    return pltpu.make_async_remote_copy(src_ref=src.at[pidx] if blocked else src, dst_ref=land.at[slot],
                                        send_sem=ssem.at[k - 1], recv_sem=rsem.at[k - 1], device_id=pid, device_id_type=MESH)


def _send_start(srcs, blocked, name):
    n = len(srcs)
    lands = [lax.empty(a.shape if blocked else (N_DEV,) + a.shape, a.dtype) for a in srcs]

    def body(*refs):
        src, land, sems, token = refs[:n], refs[n:2 * n], refs[2 * n:4 * n], refs[-1]
        for i in range(n):
            for k in range(1, N_DEV):
                _copy(src[i], land[i], sems[2 * i], sems[2 * i + 1], k, blocked, False).start()
        token[...] = jnp.zeros_like(token)

    sem = pltpu.SemaphoreType.DMA((N_DEV - 1,))
    hbm = lambda a: pltpu.with_memory_space_constraint(a, pltpu.HBM)
    res = pl.pallas_call(
        body, name=name,
        out_shape=tuple([sem] * (2 * n)) + tuple(pltpu.HBM(a.shape, a.dtype) for a in srcs + lands)
        + (jax.ShapeDtypeStruct((8, 128), f32),),
        in_specs=(_HBM,) * (2 * n),
        out_specs=(_SEM,) * (2 * n) + (_HBM,) * (2 * n) + (pl.BlockSpec(memory_space=pltpu.VMEM),),
        input_output_aliases={j: 2 * n + j for j in range(2 * n)},
        compiler_params=pltpu.CompilerParams(has_side_effects=_EFFECT),
    )(*[hbm(a) for a in srcs], *[hbm(a) for a in lands])
    handles = [(res[2 * i], res[2 * i + 1], res[2 * n + i], res[3 * n + i]) for i in range(n)]
    return handles, res[-1]


def _send_wait(handle, blocked, after, name):
    ssem, rsem, src, land = handle

    def body(src_ref, land_ref, ssem_ref, rsem_ref, after_ref, src_out, land_out):
        for k in range(1, N_DEV):
            cp = _copy(src_ref, land_ref, ssem_ref, rsem_ref, k, blocked, True)
            cp.wait_send()
            cp.wait_recv()

    return pl.pallas_call(
        body, name=name, out_shape=(pltpu.HBM(src.shape, src.dtype), pltpu.HBM(land.shape, land.dtype)),
        in_specs=(_HBM, _HBM, _SEM, _SEM, pl.BlockSpec(memory_space=pl.ANY)), out_specs=(_HBM, _HBM),
        input_output_aliases={0: 0, 1: 1}, compiler_params=pltpu.CompilerParams(has_side_effects=_EFFECT),
    )(src, land, ssem, rsem, after)


def _block_diag(w):
    nb, bs = w.shape[0], w.shape[1]
    eye = jnp.eye(nb, dtype=w.dtype)
    return (eye[:, None, :, None] * w[:, :, None, :]).reshape(nb * bs, nb * bs)


def _diag_blocks(d):
    return jnp.stack([d[g, s * 64:(s + 1) * 64, s * 64:(s + 1) * 64] for g in range(4) for s in range(2)])


_SQUARE_TILES = dict(tm=1024, tn=1024, tk=1024)


def _mlp_fwd(x, hm, wu, wd, tag, epilogue, extras, outs):
    (r,) = _matmul(hm, wu, "nn", outs=[bf16], epilogue=lambda acc: (jnp.maximum(acc, 0.0),), name=f"mlp_up_{tag}")
    res = _matmul(r, wd, "nn", outs=outs, extras=(x,) + tuple(extras), epilogue=epilogue, a_map=jnp.square,
                  name=f"mlp_down_{tag}", **_SQUARE_TILES)
    return res, (hm, r)


def _mlp_bwd(x, nw, wu, wd, saved, dxo, dxo_b, tag):
    hm, r = saved
    (du,) = _matmul(dxo_b, wd, "nt", outs=[bf16], extras=(r,), epilogue=lambda acc, rr: (acc * (2.0 * rr.astype(f32)),),
                    name=f"mlp_dact_{tag}")
    (dwd,) = _matmul(r, dxo_b, "tn", outs=[bf16], a_map=jnp.square, name=f"mlp_dwd_{tag}", **_SQUARE_TILES)
    (dwu,) = _matmul(hm, du, "tn", outs=[bf16], shard_cols=2, name=f"mlp_dwu_{tag}")
    dx, dx_b, dnw = _matmul(du, wu, "nt", outs=_RMS_BWD_OUTS, extras=(x, dxo, nw), epilogue=_rms_bwd_ep,
                            name=f"mlp_dh_{tag}", **_SQUARE_TILES)
    return dx, dx_b, dnw, dwu, dwd.reshape(N_DEV, D_FF // N_DEV, D_MODEL)


def _local_step(x, tgt, P, weight, sink):
    T = x.shape[0]
    cos, sin = _rope_tables(T)
    rtab = _ret_tables()
    row = lambda a: a.reshape(1, -1)
    mix_nw, mlp_nw = P["mixer_norm_w"], P["mlp_norm_w"]
    wr_bd, wi_bd = _block_diag(P["lru_w_r"]), _block_diag(P["lru_w_i"])
    lru_b, lru_br, lru_bi, lru_lam = row(P["lru_conv_b"]), row(P["lru_b_r"]), row(P["lru_b_i"]), row(P["lru_lambda"])
    pad16 = lambda a: jnp.pad(a.reshape(1, GDN_HEADS), ((0, 0), (GDN_HEADS, 128 - 2 * GDN_HEADS)))
    alog, dtb = pad16(P["gdn_a_log"]), pad16(P["gdn_dt_bias"])
    gnw = row(P["gdn_norm_w"])

    x0 = x
    h0 = _rms_fwd(x0, mix_nw[0:1], "rms_mix_0")
    w_ie = weight("w_in_even", h0)
    (pe,) = _matmul(h0, w_ie, "nn", outs=[f32], name="in_even")
    mix0, o_ret, s_ret = _ret_fwd(pe, cos, sin, rtab, "ret_fwd")
    w_lc = weight("lru_conv_w", pe)
    xc = _conv_fwd(pe, 4, w_lc, lru_b, False, "lru_conv_fwd")
    mix0, h_lru = _lru_fwd(xc, pe, 20, wr_bd, wi_bd, lru_br, lru_bi, lru_lam, mix0, "lru_fwd")
    w_oe = weight("w_out_even", mix0)
    x1, hm0 = _matmul(mix0, w_oe, "nn", outs=[f32, bf16], extras=(x0, mlp_nw[0:1]), epilogue=_residual_rms_ep,
                      name="out_even", tm=1024, tn=D_MODEL)
    w_u0, w_d0 = weight("w_up0", x1), weight("w_down0", x1)
    (x2, h1), mlp0 = _mlp_fwd(x1, hm0, w_u0, w_d0, "0", _residual_rms_ep, (mix_nw[1:2],), [f32, bf16])
    w_io = weight("w_in_odd", h1)
    (po,) = _matmul(h1, w_io, "nn", outs=[f32], tm=2048, tn=ODD_PAD // 3, name="in_odd")
    w_gc = weight("gdn_conv_w", po)
    qkv = _conv_fwd(po, 0, w_gc, None, True, "gdn_conv_fwd")
    y_gdn, s_gdn, ti_gdn, vn_gdn, o_gdn = _gdn_fwd(qkv, po, alog, dtb, gnw, "gdn_fwd")
    w_oo = weight("w_out_odd", y_gdn)
    x3, hm1 = _matmul(y_gdn, w_oo, "nn", outs=[f32, bf16], extras=(x2, mlp_nw[1:2]), epilogue=_residual_rms_ep,
                      name="out_odd", tm=1024, tn=D_MODEL)
    w_u1, w_d1 = weight("w_up1", x3), weight("w_down1", x3)
    (loss, dx4, dx4_b, d_final), mlp1 = _mlp_fwd(x3, hm1, w_u1, w_d1, "1", _loss_ep, (row(P["final_norm_w"]), tgt),
                                                 _LOSS_OUTS)
    dx3, dx3_b, d_mlp_nw1, d_wu1, d_wd1 = _mlp_bwd(x3, mlp_nw[1:2], w_u1, w_d1, mlp1, dx4, dx4_b, "1")
    tok = sink(dict(w_up1=d_wu1, w_down1=d_wd1))
    (dy_gdn,) = _matmul(dx3_b, w_oo, "nt", outs=[f32], name="out_odd_dx")
    (d_woo,) = _matmul(y_gdn, dx3_b, "tn", outs=[bf16], name="out_odd_dw")
    dqkv, dpo, dba, d_alog, d_dtb, d_gnw = _gdn_bwd(qkv, po, alog, dtb, gnw + tok[0:1, :], s_gdn, ti_gdn, vn_gdn, o_gdn, dy_gdn,
                                                  "gdn_bwd")
    dpo, d_gconv, _ = _conv_bwd(po, 0, w_gc, None, True, dqkv, dpo, "gdn_conv_bwd")
    dpo = lax.dynamic_update_slice(dpo, dba, (0, 4 * D_MODEL))
    (d_wio,) = _matmul(h1, dpo, "tn", outs=[bf16], tn=ODD_PAD // 3, name="in_odd_dw")
    tok = sink(dict(w_out_odd=d_woo.reshape(N_DEV, D_MODEL // N_DEV, D_MODEL), w_in_odd=_odd_split(d_wio, "w_in_odd_split")))
    dx2, dx2_b, d_mix_nw1 = _matmul(dpo, w_io, "nt", outs=_RMS_BWD_OUTS, extras=(x2, dx3, mix_nw[1:2] + tok[0:1, 0:1]),
                                    epilogue=_rms_bwd_ep, tm=1024, tn=1024, tk=ODD_PAD // 3, name="in_odd_dx")
    dx1, dx1_b, d_mlp_nw0, d_wu0, d_wd0 = _mlp_bwd(x1, mlp_nw[0:1], w_u0, w_d0, mlp0, dx2, dx2_b, "0")
    (d_woe,) = _matmul(mix0, dx1_b, "tn", outs=[bf16], name="out_even_dw")
    tok = sink(dict(w_up0=d_wu0, w_down0=d_wd0, w_out_even=d_woe.reshape(N_DEV, D_MODEL // N_DEV, D_MODEL)))
    (dmix0,) = _matmul(dx1_b, w_oe, "nt", outs=[f32], name="out_even_dx")
    dpe = _ret_bwd(pe, cos, sin, rtab, o_ret, s_ret, dmix0, "ret_bwd")
    dxc, dpe, d_wr, d_wi, d_br, d_bi, d_lam = _lru_bwd(xc, pe, 20, wr_bd, wi_bd, lru_br, lru_bi, lru_lam + tok[0:1, 0:1],
                                                       h_lru, dmix0, 4, dpe, "lru_bwd")
    dpe, d_lconv, d_lconv_b = _conv_bwd(pe, 4, w_lc, lru_b, False, dxc, dpe, "lru_conv_bwd")
    (d_wie,) = _matmul(h0, dpe, "tn", outs=[bf16], shard_cols=2, name="in_even_dw")
    tok = sink(dict(w_in_even=d_wie))
    dx0, _, d_mix_nw0 = _matmul(dpe, w_ie, "nt", outs=_RMS_BWD_OUTS, extras=(x0, dx1, mix_nw[0:1] + tok[0:1, 0:1]),
                                epilogue=_rms_bwd_ep, name="in_even_dx", **_SQUARE_TILES)

    G = dict(
        mixer_norm_w=jnp.concatenate([d_mix_nw0, d_mix_nw1], axis=0),
        mlp_norm_w=jnp.concatenate([d_mlp_nw0, d_mlp_nw1], axis=0),
        final_norm_w=d_final.reshape(-1),
        lru_conv_w=d_lconv, lru_conv_b=d_lconv_b.reshape(-1),
        lru_w_r=_diag_blocks(d_wr), lru_b_r=d_br.reshape(-1), lru_w_i=_diag_blocks(d_wi), lru_b_i=d_bi.reshape(-1),
        lru_lambda=d_lam.reshape(-1), gdn_conv_w=d_gconv,
        gdn_a_log=d_alog[0, GDN_HEADS:2 * GDN_HEADS], gdn_dt_bias=d_dtb[0, GDN_HEADS:2 * GDN_HEADS],
        gdn_norm_w=d_gnw.reshape(-1),
    )
    return loss, dx0, G


_SMALL = ["mixer_norm_w", "mlp_norm_w", "final_norm_w", "lru_conv_b", "lru_w_r", "lru_b_r", "lru_w_i", "lru_b_i",
          "lru_lambda", "gdn_a_log", "gdn_dt_bias", "gdn_norm_w"]
_PACK_ROWS = 688


def _pack(parts):
    flat = jnp.concatenate([p.reshape(-1) for p in parts])
    return jnp.pad(flat, (0, _PACK_ROWS * 128 - flat.shape[0])).reshape(_PACK_ROWS, 128)


def _unpack(packed, shapes):
    flat = packed.reshape(-1)
    out, off = [], 0
    for s in shapes:
        n = int(np.prod(s))
        out.append(flat[off:off + n].reshape(s))
        off += n
    return out


def kernel(x, mixer_norm_w, mlp_norm_w, final_norm_w, w_in_even, lru_conv_w, lru_conv_b, lru_w_r, lru_b_r, lru_w_i, lru_b_i, lru_lambda, w_out_even, w_in_odd, gdn_conv_w, gdn_a_log, gdn_dt_bias, gdn_norm_w, w_out_odd, w_up, w_down, loss_target, m_mixer_norm_w, m_mlp_norm_w, m_final_norm_w, m_w_in_even, m_lru_conv_w, m_lru_conv_b, m_lru_w_r, m_lru_b_r, m_lru_w_i, m_lru_b_i, m_lru_lambda, m_w_out_even, m_w_in_odd, m_gdn_conv_w, m_gdn_a_log, m_gdn_dt_bias, m_gdn_norm_w, m_w_out_odd, m_w_up, m_w_down, v_mixer_norm_w, v_mlp_norm_w, v_final_norm_w, v_w_in_even, v_lru_conv_w, v_lru_conv_b, v_lru_w_r, v_lru_b_r, v_lru_w_i, v_lru_b_i, v_lru_lambda, v_w_out_even, v_w_in_odd, v_gdn_conv_w, v_gdn_a_log, v_gdn_dt_bias, v_gdn_norm_w, v_w_out_odd, v_w_up, v_w_down):
    Pw = dict(mixer_norm_w=mixer_norm_w, mlp_norm_w=mlp_norm_w, final_norm_w=final_norm_w, w_in_even=w_in_even,
              lru_conv_w=lru_conv_w, lru_conv_b=lru_conv_b, lru_w_r=lru_w_r, lru_b_r=lru_b_r, lru_w_i=lru_w_i,
              lru_b_i=lru_b_i, lru_lambda=lru_lambda, w_out_even=w_out_even, w_in_odd=w_in_odd, gdn_conv_w=gdn_conv_w,
              gdn_a_log=gdn_a_log, gdn_dt_bias=gdn_dt_bias, gdn_norm_w=gdn_norm_w, w_out_odd=w_out_odd, w_up=w_up,
              w_down=w_down)
    Pm = dict(mixer_norm_w=m_mixer_norm_w, mlp_norm_w=m_mlp_norm_w, final_norm_w=m_final_norm_w, w_in_even=m_w_in_even,
              lru_conv_w=m_lru_conv_w, lru_conv_b=m_lru_conv_b, lru_w_r=m_lru_w_r, lru_b_r=m_lru_b_r, lru_w_i=m_lru_w_i,
              lru_b_i=m_lru_b_i, lru_lambda=m_lru_lambda, w_out_even=m_w_out_even, w_in_odd=m_w_in_odd,
              gdn_conv_w=m_gdn_conv_w, gdn_a_log=m_gdn_a_log, gdn_dt_bias=m_gdn_dt_bias, gdn_norm_w=m_gdn_norm_w,
              w_out_odd=m_w_out_odd, w_up=m_w_up, w_down=m_w_down)
    Pv = dict(mixer_norm_w=v_mixer_norm_w, mlp_norm_w=v_mlp_norm_w, final_norm_w=v_final_norm_w, w_in_even=v_w_in_even,
              lru_conv_w=v_lru_conv_w, lru_conv_b=v_lru_conv_b, lru_w_r=v_lru_w_r, lru_b_r=v_lru_b_r, lru_w_i=v_lru_w_i,
              lru_b_i=v_lru_b_i, lru_lambda=v_lru_lambda, w_out_even=v_w_out_even, w_in_odd=v_w_in_odd,
              gdn_conv_w=v_gdn_conv_w, gdn_a_log=v_gdn_a_log, gdn_dt_bias=v_gdn_dt_bias, gdn_norm_w=v_gdn_norm_w,
              w_out_odd=v_w_out_odd, w_up=v_w_up, w_down=v_w_down)
    me = _me()[3]
    T = x.shape[1]

    cols = lambda g: jnp.transpose(g, (1, 0, 2)).reshape(g.shape[1], -1)
    rows = lambda g: g.reshape(-1, g.shape[2])
    wide = lambda g: _odd_assemble(g, "w_in_odd_assemble")
    odd_shard = jnp.pad(w_in_odd[0].astype(bf16), ((0, 0), (0, ODD_SHARD_PAD - ODD_SHARD)))
    as_is = lambda g: g
    gather = dict(
        w_in_even=(w_in_even[0].astype(bf16), cols), lru_conv_w=(lru_conv_w[0], cols),
        w_out_even=(w_out_even[0].astype(bf16), rows), w_up0=(w_up[0].astype(bf16), as_is), w_down0=(w_down[0].astype(bf16), rows),
        w_in_odd=(odd_shard, wide), gdn_conv_w=(gdn_conv_w[0], cols),
        w_out_odd=(w_out_odd[0].astype(bf16), rows), w_up1=(w_up[1].astype(bf16), as_is), w_down1=(w_down[1].astype(bf16), rows))
    handles, tok = _send_start([s for s, _ in gather.values()], False, "gather_start")
    handles = dict(zip(gather, handles))
    full = {}

    def weight(name, after):
        if name not in full:
            shard, landed = _send_wait(handles[name], False, after, f"gather_wait_{name}")
            full[name] = gather[name][1](lax.dynamic_update_slice_in_dim(landed, shard[None], me, 0))
        return full[name]

    P = {k: Pw[k] for k in ("mlp_norm_w", "final_norm_w")}
    P["mixer_norm_w"] = mixer_norm_w + tok[0:1, 0:1]
    for k in ("lru_w_r", "lru_w_i", "lru_conv_b", "lru_b_r", "lru_b_i", "lru_lambda", "gdn_a_log", "gdn_dt_bias", "gdn_norm_w"):
        P[k] = Pw[k][0]

    sent = {}

    def sink(grads):
        hs, token = _send_start(list(grads.values()), True, "grads_start_" + "_".join(grads))
        sent.update(zip(grads, hs))
        return token

    loss, dx, G = _local_step(x[0], loss_target[0], P, weight, sink)
    small_g = [G[k].reshape(Pw[k].shape) for k in _SMALL] + [G["lru_conv_w"], G["gdn_conv_w"], loss[0, 0:1]]
    packed = _pack(small_g)
    sink(dict(small=jnp.broadcast_to(packed[None], (N_DEV,) + packed.shape)))

    def received(name, after=dx):
        g, landed = _send_wait(sent[name], True, after, f"grads_wait_{name}")
        return lax.dynamic_update_slice_in_dim(landed, lax.dynamic_slice_in_dim(g, me, 1, 0), me, 0)

    out = {}
    nff = D_FF // N_DEV

    def whole(name, gs):
        out[name] = tuple(_adamw(Pw[name], gs, Pm[name], Pv[name], f"adamw_{name}", layer=0))

    def layers(name):
        res = None
        for l in range(2):
            res = _adamw(Pw[name], received(f"{name}{l}"), Pm[name], Pv[name], f"adamw_{name}{l}", layer=l, prev=res)
        out[name] = tuple(res)

    layers("w_up")
    layers("w_down")
    whole("w_out_odd", received("w_out_odd"))
    whole("w_in_odd", received("w_in_odd"))
    whole("w_out_even", received("w_out_even"))
    whole("w_in_even", received("w_in_even", out["w_out_even"][1]))
    small_shapes = [Pw[k].shape for k in _SMALL]
    pw, pm, pv = (_pack([Q[k] for k in _SMALL]) for Q in (Pw, Pm, Pv))
    sg, sd, sm, sv = _adamw(pw, received("small", out["w_in_even"][1]), pm, pv, "adamw_small")
    for arrs_i, packed_out in enumerate((sg, sd, sm, sv)):
        for k, a in zip(_SMALL, _unpack(packed_out, small_shapes)):
            out.setdefault(k, [None] * 4)[arrs_i] = a
    n_small = sum(int(np.prod(s)) for s in small_shapes)
    gflat = sg.reshape(-1)
    g_lconv = gflat[n_small:n_small + CONV_K * LRU_WIDTH].reshape(CONV_K, LRU_WIDTH)
    g_gconv = gflat[n_small + CONV_K * LRU_WIDTH:n_small + CONV_K * (LRU_WIDTH + 3072)].reshape(CONV_K, 3072)
    whole("lru_conv_w", lax.dynamic_slice_in_dim(g_lconv, me * 64, 64, axis=1)[None])
    whole("gdn_conv_w", lax.dynamic_slice_in_dim(g_gconv, me * 384, 384, axis=1)[None])

    names = ["mixer_norm_w", "mlp_norm_w", "final_norm_w", "w_in_even", "lru_conv_w", "lru_conv_b", "lru_w_r", "lru_b_r",
             "lru_w_i", "lru_b_i", "lru_lambda", "w_out_even", "w_in_odd", "gdn_conv_w", "gdn_a_log", "gdn_dt_bias",
             "gdn_norm_w", "w_out_odd", "w_up", "w_down"]
    total = gflat[n_small + CONV_K * (LRU_WIDTH + 3072)]
    res = [total, dx[None]]
    for j in range(4):
        res += [out[k][j] for k in names]
    return tuple(res)
```

```python
import math

import numpy as np
import jax
import jax.numpy as jnp
from jax import lax
from jax.experimental import pallas as pl
from jax.experimental.pallas import tpu as pltpu

f32 = jnp.float32
bf16 = jnp.bfloat16

N_DEV = 8
D_MODEL = 1024
D_FF = 4096
EPS = 1e-6
RET_HEADS = 4
RET_CHUNK = 128
RET_STEP = 4
ROPE_THETA = 10000.0
LRU_WIDTH = 512
LRU_C = 8.0
GDN_HEADS = 8
GDN_CHUNK = 64
GDN_STEP = 4
HEAD_DIM = 128
ODD_IN = 4112
ODD_PAD = 4224
ODD_SHARD = ODD_IN // N_DEV
ODD_SHARD_PAD = 640
ADAM_LR, ADAM_B1, ADAM_B2, ADAM_EPS, ADAM_WD, ADAM_STEP = 0.001, 0.9, 0.999, 1e-08, 0.01, 10
VMEM_LIMIT = 56 * 1024 * 1024

_NN = (((1,), (0,)), ((), ()))
_NT = (((1,), (1,)), ((), ()))
_TN = (((0,), (0,)), ((), ()))
MESH = pl.DeviceIdType.MESH


def _cparams(sem):
    return pltpu.CompilerParams(dimension_semantics=sem, vmem_limit_bytes=VMEM_LIMIT)


def _dot(a, b, dn):
    return lax.dot_general(a.astype(bf16), b.astype(bf16), dn, preferred_element_type=f32)


def _dot01(a01, b, dn):
    a = a01.astype(bf16)
    b0 = b.astype(bf16)
    r1 = b - b0.astype(f32)
    b1 = r1.astype(bf16)
    b2 = (r1 - b1.astype(f32)).astype(bf16)
    d = lambda q: lax.dot_general(a, q, dn, preferred_element_type=f32)
    return d(b0) + (d(b1) + d(b2))


def _sigmoid(x):
    return jax.nn.sigmoid(x)


def _silu(x):
    return x * _sigmoid(x)


def _dsilu(x):
    s = _sigmoid(x)
    return s * (1.0 + x * (1.0 - s))


def _softplus(x):
    return jnp.maximum(x, 0.0) + jnp.log1p(jnp.exp(-jnp.abs(x)))


_GELU_C = math.sqrt(2.0 / math.pi)


def _gelu(y):
    return 0.5 * y * (1.0 + jnp.tanh(_GELU_C * (y + 0.044715 * y * y * y)))


def _dgelu(y):
    t = jnp.tanh(_GELU_C * (y + 0.044715 * y * y * y))
    return 0.5 * (1.0 + t) + 0.5 * y * (1.0 - t * t) * _GELU_C * (1.0 + 3.0 * 0.044715 * y * y)


def _matmul(a, b, form, *, outs, name, epilogue=None, extras=(), tm=4096, tn=512, tk=1024, shard_cols=0, a_map=None):
    if form == "tn":
        K, M = a.shape
    else:
        M, K = a.shape
    per_step = 1
    if b.ndim == 3:
        assert form in ("nn", "nt"), name
        N = b.shape[1] if form == "nt" else N_DEV * b.shape[2]
        if form == "nn":
            tn = b.shape[2]
        else:
            per_step = max(1, tk // b.shape[2])
            tk = per_step * b.shape[2]
    else:
        N = b.shape[0] if form == "nt" else b.shape[1]
    ns = N // N_DEV
    if shard_cols:
        tn = ns * shard_cols
    tm, tn, tk = min(tm, M), min(tn, N), min(tk, K)
    assert M % tm == 0 and N % tn == 0 and K % tk == 0, (name, M, N, K, tm, tn, tk)
    nk = K // tk
    dn = {"nn": _NN, "nt": _NT, "tn": _TN}[form]
    if form == "tn":
        a_spec = pl.BlockSpec((tk, tm), lambda i, j, k: (k, i))
    else:
        a_spec = pl.BlockSpec((tm, tk), lambda i, j, k: (i, k))
    if b.ndim == 3:
        b_spec = (pl.BlockSpec((per_step, tn, tk // per_step), lambda i, j, k: (k, j, 0)) if form == "nt"
                  else pl.BlockSpec((None, tk, tn), lambda i, j, k: (j, k, 0)))
    elif form == "nt":
        b_spec = pl.BlockSpec((tn, tk), lambda i, j, k: (j, k))
    else:
        b_spec = pl.BlockSpec((tk, tn), lambda i, j, k: (k, j))
    e_spec = pl.BlockSpec((tm, tn), lambda i, j, k: (i, j))
    v_spec = pl.BlockSpec((1, tn), lambda i, j, k: (0, j))
    if shard_cols:
        o_spec = pl.BlockSpec((shard_cols, tm, ns), lambda i, j, k: (j, i, 0))
        o_shape = (N_DEV, M, ns)
    else:
        o_spec = e_spec
        o_shape = (M, N)
    n_ex = len(extras)
    sums = [isinstance(o, tuple) for o in outs]
    assert not any(sums) or tn == N, name

    def finish(acc, ex, o_refs, row_tile):
        vals = (acc,) if epilogue is None else epilogue(acc, *[e[...] for e in ex])
        for r, v, is_sum in zip(o_refs, vals, sums):
            if is_sum:
                @pl.when(row_tile == 0)
                def _(r=r, v=v):
                    r[...] = v.astype(r.dtype)

                @pl.when(row_tile > 0)
                def _(r=r, v=v):
                    r[...] += v.astype(r.dtype)
            elif shard_cols:
                for s in range(shard_cols):
                    r[s] = v[:, s * ns:(s + 1) * ns].astype(r.dtype)
            else:
                r[...] = v.astype(r.dtype)

    def prod(a_ref, b_ref):
        if b.ndim == 3 and form == "nt":
            w = tk // per_step
            return sum(_dot(a_ref[:, s * w:(s + 1) * w], b_ref[s], dn) for s in range(1, per_step)) + _dot(a_ref[:, 0:w], b_ref[0], dn)
        av = a_ref[...]
        return _dot(av if a_map is None else a_map(av), b_ref[...], dn)

    def body_one(*refs):
        finish(prod(*refs[:2]), refs[2:2 + n_ex], refs[2 + n_ex:], pl.program_id(0))

    def body_acc(*refs):
        a_ref, b_ref = refs[:2]
        acc = refs[-1]
        k = pl.program_id(2)
        row_tile = pl.program_id(0)

        @pl.when(k == 0)
        def _():
            acc[...] = prod(a_ref, b_ref)

        @pl.when((k > 0) & (k < nk - 1))
        def _():
            acc[...] += prod(a_ref, b_ref)

        @pl.when(k == nk - 1)
        def _():
            finish(acc[...] + prod(a_ref, b_ref), refs[2:2 + n_ex], refs[2 + n_ex:-1], row_tile)

    return pl.pallas_call(
        body_one if nk == 1 else body_acc, grid=(M // tm, N // tn, nk),
        in_specs=[a_spec, b_spec] + [v_spec if e.shape[0] == 1 else e_spec for e in extras],
        out_specs=[v_spec if s else o_spec for s in sums],
        out_shape=[jax.ShapeDtypeStruct((1, N), o[1]) if s else jax.ShapeDtypeStruct(o_shape, o) for o, s in zip(outs, sums)],
        scratch_shapes=[] if nk == 1 else [pltpu.VMEM((tm, tn), f32)],
        compiler_params=_cparams(("arbitrary" if any(sums) else "parallel", "parallel", "arbitrary")), name=name,
    )(a, b, *extras)


def _rms_fwd(x, w, name):
    T, D = x.shape
    tt = min(512, T)

    def body(x_ref, w_ref, h_ref):
        xv = x_ref[...]
        r = lax.rsqrt(jnp.mean(xv * xv, axis=1, keepdims=True) + EPS)
        h_ref[...] = (xv * r * w_ref[...]).astype(bf16)

    return pl.pallas_call(
        body, grid=(T // tt,),
        in_specs=[pl.BlockSpec((tt, D), lambda i: (i, 0)), pl.BlockSpec((1, D), lambda i: (0, 0))],
        out_specs=pl.BlockSpec((tt, D), lambda i: (i, 0)),
        out_shape=jax.ShapeDtypeStruct((T, D), bf16),
        compiler_params=_cparams(("parallel",)), name=name,
    )(x, w)


def _residual_rms_ep(acc, res, w):
    x = res + acc
    r = lax.rsqrt(jnp.mean(x * x, axis=1, keepdims=True) + EPS)
    return x, x * r * w


_RMS_BWD_OUTS = [f32, bf16, ("sum", f32)]


def _rms_bwd_ep(dh, x, dres, w):
    r = lax.rsqrt(jnp.mean(x * x, axis=1, keepdims=True) + EPS)
    xn = x * r
    dhw = dh * w
    dx = dres + r * (dhw - xn * jnp.mean(dhw * xn, axis=1, keepdims=True))
    return dx, dx, jnp.sum(dh * xn, axis=0, keepdims=True)


_LOSS_OUTS = [("sum", f32), f32, bf16, ("sum", f32)]


def _loss_ep(acc, res, w, tgt):
    x = res + acc
    D = x.shape[1]
    r = lax.rsqrt(jnp.mean(x * x, axis=1, keepdims=True) + EPS)
    xn = x * r
    e = xn * w - tgt
    loss = 0.5 * jnp.sum(jnp.mean(e * e, axis=1, keepdims=True), axis=0, keepdims=True)
    dy = e * (1.0 / D)
    dyw = dy * w
    dx = r * (dyw - xn * jnp.mean(dyw * xn, axis=1, keepdims=True))
    return jnp.broadcast_to(loss, (1, D)), dx, dx, jnp.sum(dy * xn, axis=0, keepdims=True)


def _ret_tables():
    H, C = RET_HEADS, RET_CHUNK
    lg = np.log1p(-np.exp2(-5.0 - np.arange(H, dtype=np.float32))).astype(np.float32)
    idx = np.arange(C, dtype=np.float32)
    diff = idx[:, None] - idx[None, :]
    causal = diff >= 0
    dm = np.where(causal[None], np.exp(lg[:, None, None] * np.where(causal, diff, 0.0)[None]), 0.0)
    qd = np.exp(lg[:, None] * (idx[None, :] + 1.0))
    kd = np.exp(lg[:, None] * (C - 1.0 - idx[None, :]))
    cg = np.exp(lg * C)
    tab = np.zeros((H, 4, C, HEAD_DIM), np.float32)
    tab[:, 0] = dm
    tab[:, 1] = qd[:, :, None]
    tab[:, 2] = kd[:, :, None]
    tab[:, 3] = cg[:, None, None]
    return jnp.asarray(tab)


def _rope_tables(T):
    half = HEAD_DIM // 2
    inv = ROPE_THETA ** (-jnp.arange(half, dtype=f32) / half)
    ang = jnp.arange(T, dtype=jnp.int32).astype(f32)[:, None] * inv[None, :]
    c, s = jnp.cos(ang), jnp.sin(ang)
    return jnp.concatenate([c, c], axis=1), jnp.concatenate([-s, s], axis=1)


def _rope(x, cos, sin):
    return x * cos + pltpu.roll(x, HEAD_DIM // 2, 1) * sin


def _unrope(y, cos, sin):
    return y * cos + pltpu.roll(y * sin, HEAD_DIM // 2, 1)


def _stack_heads(ref, H, f=None):
    parts = [ref[:, h * HEAD_DIM:(h + 1) * HEAD_DIM] for h in range(H)]
    return jnp.stack(parts if f is None else [f(a) for a in parts])


def _ret_fwd(p, cos, sin, tab, name):
    T = p.shape[0]
    C, H = RET_CHUNK, RET_HEADS
    N = T // C
    K = min(RET_STEP, N)
    NS = N // K
    scale = HEAD_DIM ** -0.5

    def body(q_ref, k_ref, v_ref, g_ref, c_ref, s_ref, t_ref, y_ref, o_ref, sp_ref, st):
        @pl.when(pl.program_id(0) == 0)
        def _():
            st[...] = jnp.zeros_like(st)

        dm, qd, kd, cg = t_ref[:, 0], t_ref[:, 1], t_ref[:, 2], t_ref[:, 3]
        S = st[...]
        for c in range(K):
            rows = pl.ds(c * C, C)
            cos_, sin_ = c_ref[rows, :], s_ref[rows, :]
            rot = lambda a: _rope(a, cos_, sin_)
            q = _stack_heads(q_ref.at[rows, :], H, rot)
            k = _stack_heads(k_ref.at[rows, :], H, rot) * scale
            v = _stack_heads(v_ref.at[rows, :], H)
            P = _dot(q, k, _NT3) * dm
            o = _dot(P, v, _NN3) + _dot(q * qd, S, _NN3)
            sp_ref[c] = S
            S = cg * S + _dot(k * kd, v, _TN3)
            r = lax.rsqrt(jnp.mean(o * o, axis=2, keepdims=True) + EPS)
            y = o * r * _silu(_stack_heads(g_ref.at[rows, :], H))
            for h in range(H):
                o_ref[rows, h * HEAD_DIM:(h + 1) * HEAD_DIM] = o[h]
                y_ref[rows, h * HEAD_DIM:(h + 1) * HEAD_DIM] = y[h].astype(bf16)
        st[...] = S

    wide = lambda blk: pl.BlockSpec((K * C, H * HEAD_DIM), lambda n: (n, blk))
    tbl = pl.BlockSpec((K * C, HEAD_DIM), lambda n: (n, 0))
    return pl.pallas_call(
        body, grid=(NS,),
        in_specs=[wide(0), wide(1), wide(2), wide(3), tbl, tbl,
                  pl.BlockSpec((H, 4, C, HEAD_DIM), lambda n: (0, 0, 0, 0))],
        out_specs=[wide(0), wide(0), pl.BlockSpec((K, H, HEAD_DIM, HEAD_DIM), lambda n: (n, 0, 0, 0))],
        out_shape=[jax.ShapeDtypeStruct((T, D_MODEL), bf16), jax.ShapeDtypeStruct((T, H * HEAD_DIM), f32),
                   jax.ShapeDtypeStruct((N, H, HEAD_DIM, HEAD_DIM), f32)],
        scratch_shapes=[pltpu.VMEM((H, HEAD_DIM, HEAD_DIM), f32)],
        compiler_params=_cparams(("arbitrary",)), name=name,
    )(p, p, p, p, cos, sin, tab)


def _ret_bwd(p, cos, sin, tab, o_raw, sprev, dmix, name):
    T = p.shape[0]
    C, H = RET_CHUNK, RET_HEADS
    N = T // C
    K = min(RET_STEP, N)
    NS = N // K
    scale = HEAD_DIM ** -0.5
    W = H * HEAD_DIM

    def body(q_ref, k_ref, v_ref, g_ref, c_ref, s_ref, t_ref, o_ref, sp_ref, dy_ref, d_ref, dst):
        @pl.when(pl.program_id(0) == 0)
        def _():
            dst[...] = jnp.zeros_like(dst)

        dm, qd, kd, cg = t_ref[:, 0], t_ref[:, 1], t_ref[:, 2], t_ref[:, 3]
        dS1 = dst[...]
        for c in reversed(range(K)):
            rows = pl.ds(c * C, C)
            cos_, sin_ = c_ref[rows, :], s_ref[rows, :]
            rot = lambda a: _rope(a, cos_, sin_)
            q = _stack_heads(q_ref.at[rows, :], H, rot)
            k = _stack_heads(k_ref.at[rows, :], H, rot) * scale
            v = _stack_heads(v_ref.at[rows, :], H)
            g = _stack_heads(g_ref.at[rows, :], H)
            S = sp_ref[c]
            o = _stack_heads(o_ref.at[rows, :], H)
            dy = _stack_heads(dy_ref.at[rows, :], H)
            r = lax.rsqrt(jnp.mean(o * o, axis=2, keepdims=True) + EPS)
            nrm = o * r
            dn = dy * _silu(g)
            dg = dy * nrm * _dsilu(g)
            do = r * (dn - nrm * jnp.mean(dn * nrm, axis=2, keepdims=True))
            P = _dot(q, k, _NT3) * dm
            dP = _dot(do, v, _NT3) * dm
            dq = _dot(dP, k, _NN3) + _dot(do, S, _NT3) * qd
            dk = (_dot(dP, q, _TN3) + _dot(v, dS1, _NT3) * kd) * scale
            dv = _dot(P, do, _TN3) + _dot(k * kd, dS1, _NN3)
            dS1 = cg * dS1 + _dot(q * qd, do, _TN3)
            for h in range(H):
                d_ref[rows, h * HEAD_DIM:(h + 1) * HEAD_DIM] = _unrope(dq[h], cos_, sin_).astype(bf16)
                d_ref[rows, W + h * HEAD_DIM:W + (h + 1) * HEAD_DIM] = _unrope(dk[h], cos_, sin_).astype(bf16)
                d_ref[rows, 2 * W + h * HEAD_DIM:2 * W + (h + 1) * HEAD_DIM] = dv[h].astype(bf16)
                d_ref[rows, 3 * W + h * HEAD_DIM:3 * W + (h + 1) * HEAD_DIM] = dg[h].astype(bf16)
        dst[...] = dS1

    rev = lambda blk: pl.BlockSpec((K * C, W), lambda n: (NS - 1 - n, blk))
    tbl = pl.BlockSpec((K * C, HEAD_DIM), lambda n: (NS - 1 - n, 0))
    return pl.pallas_call(
        body, grid=(NS,),
        in_specs=[rev(0), rev(1), rev(2), rev(3), tbl, tbl,
                  pl.BlockSpec((H, 4, C, HEAD_DIM), lambda n: (0, 0, 0, 0)), rev(0),
                  pl.BlockSpec((K, H, HEAD_DIM, HEAD_DIM), lambda n: (NS - 1 - n, 0, 0, 0)), rev(0)],
        out_specs=pl.BlockSpec((K * C, 4 * W), lambda n: (NS - 1 - n, 0)),
        out_shape=jax.ShapeDtypeStruct((T, 6 * W), bf16),
        scratch_shapes=[pltpu.VMEM((H, HEAD_DIM, HEAD_DIM), f32)],
        compiler_params=_cparams(("arbitrary",)), name=name,
    )(p, p, p, p, cos, sin, tab, o_raw, sprev, dmix)


CONV_K = 4
CONV_W = 512
PAD = 8
SUB_R = 64


def _conv_fwd(x, col_off, w, b, act, name):
    T = x.shape[0]
    C = w.shape[1]
    G = C // CONV_W
    tt = min(512, T)
    NT = T // tt
    has_b = b is not None

    def body(*refs):
        if has_b:
            x_ref, w_ref, b_ref, y_ref, pad = refs
        else:
            x_ref, w_ref, y_ref, pad = refs
        t = pl.program_id(1)

        @pl.when(t == 0)
        def _():
            pad[pl.ds(0, PAD), :] = jnp.zeros((PAD, CONV_W), f32)

        pad[pl.ds(PAD, tt), :] = x_ref[...]
        for g in range(CONV_W // 128):
            ls = slice(g * 128, (g + 1) * 128)
            wv = w_ref[:, ls]
            for c in range(tt // SUB_R):
                r0 = c * SUB_R
                y = wv[0:1, :] * pad[pl.ds(PAD - 3 + r0, SUB_R), ls]
                for kk in range(1, CONV_K):
                    y = y + wv[kk:kk + 1, :] * pad[pl.ds(PAD - 3 + kk + r0, SUB_R), ls]
                if has_b:
                    y = y + b_ref[:, ls]
                y_ref[pl.ds(r0, SUB_R), ls] = _silu(y) if act else y
        tail = pad[pl.ds(tt, PAD), :]
        pad[pl.ds(0, PAD), :] = tail

    in_specs = [pl.BlockSpec((tt, CONV_W), lambda g, t: (t, col_off + g)),
                pl.BlockSpec((CONV_K, CONV_W), lambda g, t: (0, g))]
    args = [x, w]
    if has_b:
        in_specs.append(pl.BlockSpec((1, CONV_W), lambda g, t: (0, g)))
        args.append(b)
    return pl.pallas_call(
        body, grid=(G, NT), in_specs=in_specs,
        out_specs=pl.BlockSpec((tt, CONV_W), lambda g, t: (t, g)),
        out_shape=jax.ShapeDtypeStruct((T, C), f32),
        scratch_shapes=[pltpu.VMEM((tt + PAD, CONV_W), f32)],
        compiler_params=_cparams(("parallel", "arbitrary")), name=name,
    )(*args)


def _conv_bwd(x, col_off, w, b, act, dout, dp, name):
    T = x.shape[0]
    C = w.shape[1]
    G = C // CONV_W
    tt = min(512, T)
    NT = T // tt
    has_b = b is not None

    def body(*refs):
        if has_b:
            x_ref, xp_ref, w_ref, b_ref, d_ref, dp_in, dx_ref, dw_ref, db_ref, pad, dpad = refs
        else:
            x_ref, xp_ref, w_ref, d_ref, dp_in, dx_ref, dw_ref, db_ref, pad, dpad = refs
        t = pl.program_id(1)
        first_tile = t == NT - 1

        @pl.when(t == 0)
        def _():
            dpad[pl.ds(tt, PAD), :] = jnp.zeros((PAD, CONV_W), f32)
            dw_ref[...] = jnp.zeros_like(dw_ref)
            db_ref[...] = jnp.zeros_like(db_ref)

        pad[pl.ds(0, PAD), :] = jnp.where(first_tile, 0.0, xp_ref[...])
        pad[pl.ds(PAD, tt), :] = x_ref[...]
        fold = lambda v: v.reshape(SUB_R // 8, 8, 128).sum(axis=0)
        for g in range(CONV_W // 128):
            ls = slice(g * 128, (g + 1) * 128)
            wv = w_ref[:, ls]
            acc = [jnp.zeros((8, 128), f32) for _ in range(CONV_K + 1)]
            for c in reversed(range(tt // SUB_R)):
                r0 = c * SUB_R
                xs = [pad[pl.ds(PAD - 3 + kk + r0, SUB_R), ls] for kk in range(CONV_K)]
                dy = d_ref[pl.ds(r0, SUB_R), ls]
                if act:
                    y = wv[0:1, :] * xs[0]
                    for kk in range(1, CONV_K):
                        y = y + wv[kk:kk + 1, :] * xs[kk]
                    if has_b:
                        y = y + b_ref[:, ls]
                    dy = dy * _dsilu(y)
                dpad[pl.ds(r0, SUB_R), ls] = dy
                dx = wv[3:4, :] * dy
                for j in range(1, CONV_K):
                    dx = dx + wv[3 - j:4 - j, :] * dpad[pl.ds(r0 + j, SUB_R), ls]
                dx_ref[pl.ds(r0, SUB_R), ls] = dx.astype(bf16)
                for kk in range(CONV_K):
                    acc[kk] = acc[kk] + fold(dy * xs[kk])
                acc[CONV_K] = acc[CONV_K] + fold(dy)
            for kk in range(CONV_K):
                dw_ref[kk:kk + 1, ls] += jnp.sum(acc[kk], axis=0, keepdims=True)
            db_ref[:, ls] += jnp.sum(acc[CONV_K], axis=0, keepdims=True)
        head = dpad[pl.ds(0, PAD), :]
        dpad[pl.ds(tt, PAD), :] = head

    rows8 = tt // PAD
    in_specs = [pl.BlockSpec((tt, CONV_W), lambda g, t: (NT - 1 - t, col_off + g)),
                pl.BlockSpec((PAD, CONV_W), lambda g, t: (jnp.maximum((NT - 1 - t) * rows8 - 1, 0), col_off + g)),
                pl.BlockSpec((CONV_K, CONV_W), lambda g, t: (0, g))]
    args = [x, x, w]
    if has_b:
        in_specs.append(pl.BlockSpec((1, CONV_W), lambda g, t: (0, g)))
        args.append(b)
    in_specs += [pl.BlockSpec((tt, CONV_W), lambda g, t: (NT - 1 - t, g)), pl.BlockSpec(memory_space=pl.ANY)]
    args += [dout, dp]
    return pl.pallas_call(
        body, grid=(G, NT), in_specs=in_specs,
        out_specs=[pl.BlockSpec((tt, CONV_W), lambda g, t: (NT - 1 - t, col_off + g)),
                   pl.BlockSpec((CONV_K, CONV_W), lambda g, t: (0, g)),
                   pl.BlockSpec((1, CONV_W), lambda g, t: (0, g))],
        out_shape=[jax.ShapeDtypeStruct(dp.shape, dp.dtype), jax.ShapeDtypeStruct((CONV_K, C), f32),
                   jax.ShapeDtypeStruct((1, C), f32)],
        input_output_aliases={len(args) - 1: 0},
        scratch_shapes=[pltpu.VMEM((tt + PAD, CONV_W), f32), pltpu.VMEM((tt + PAD, CONV_W), f32)],
        compiler_params=_cparams(("parallel", "arbitrary")), name=name,
    )(*args)


def _lru_gates(xc, wr, wi, br, bi, lam):
    r = _sigmoid(_dot(xc, wr, _NN) + br)
    i = _sigmoid(_dot(xc, wi, _NN) + bi)
    sp = _softplus(-lam)
    a = jnp.exp(-LRU_C * r * sp)
    mult = jnp.sqrt(1.0 - a * a)
    return r, i, sp, a, mult


def _lru_fwd(xc, p, y_off, wr, wi, br, bi, lam, mix, name):
    T = xc.shape[0]
    G = LRU_WIDTH // 128
    tt = min(512, T)
    NT = T // tt

    def body(x_ref, y_ref, wr_ref, wi_ref, br_ref, bi_ref, l_ref, mix_in, o_ref, h_ref, hc):
        t = pl.program_id(1)

        @pl.when(t == 0)
        def _():
            hc[...] = jnp.zeros_like(hc)

        x = x_ref[...]
        r, i, sp, a, mult = _lru_gates(x, wr_ref[...], wi_ref[...], br_ref[...], bi_ref[...], l_ref[...])
        row = lax.broadcasted_iota(jnp.int32, (tt, 128), 0)
        mult = jnp.where((row == 0) & (t == 0), 1.0, mult)
        U = x * i * mult
        A = a
        d = 1
        while d < tt:
            keep = row >= d
            Ush = jnp.where(keep, pltpu.roll(U, d, 0), 0.0)
            Ash = jnp.where(keep, pltpu.roll(A, d, 0), 1.0)
            U = A * Ush + U
            A = A * Ash
            d *= 2
        h = U + A * hc[0:1, :]
        h_ref[...] = h
        hc[...] = jnp.broadcast_to(h[tt - 1:tt, :], hc.shape)
        o_ref[...] = (h * _gelu(y_ref[...])).astype(bf16)

    tile = pl.BlockSpec((tt, 128), lambda g, t: (t, g))
    vec = pl.BlockSpec((1, 128), lambda g, t: (0, g))
    wsp = pl.BlockSpec((128, 128), lambda g, t: (g, g))
    return pl.pallas_call(
        body, grid=(G, NT),
        in_specs=[tile, pl.BlockSpec((tt, 128), lambda g, t: (t, y_off + g)), wsp, wsp, vec, vec, vec,
                  pl.BlockSpec(memory_space=pl.ANY)],
        out_specs=[pl.BlockSpec((tt, 128), lambda g, t: (t, G + g)), tile],
        out_shape=[jax.ShapeDtypeStruct(mix.shape, mix.dtype), jax.ShapeDtypeStruct((T, LRU_WIDTH), f32)],
        input_output_aliases={7: 0},
        scratch_shapes=[pltpu.VMEM((8, 128), f32)],
        compiler_params=_cparams(("parallel", "arbitrary")), name=name,
    )(xc, p, wr, wi, br, bi, lam, mix)


def _lru_bwd(xc, p, y_off, wr, wi, br, bi, lam, hs, dmix, d_off, dp, name):
    T = xc.shape[0]
    G = LRU_WIDTH // 128
    tt = min(512, T)
    NT = T // tt

    def body(x_ref, y_ref, wr_ref, wi_ref, br_ref, bi_ref, l_ref, h_ref, hp_ref, do_ref, dp_in,
             dx_ref, dy_ref, dwr_ref, dwi_ref, dbr_ref, dbi_ref, dl_ref, lc, an):
        t = pl.program_id(1)
        first_tile = t == NT - 1

        @pl.when(t == 0)
        def _():
            lc[...] = jnp.zeros_like(lc)
            an[...] = jnp.zeros_like(an)
            dwr_ref[...] = jnp.zeros_like(dwr_ref)
            dwi_ref[...] = jnp.zeros_like(dwi_ref)
            dbr_ref[...] = jnp.zeros_like(dbr_ref)
            dbi_ref[...] = jnp.zeros_like(dbi_ref)
            dl_ref[...] = jnp.zeros_like(dl_ref)

        x = x_ref[...]
        y = y_ref[...]
        wr, wi, lam_ = wr_ref[...], wi_ref[...], l_ref[...]
        r, i, sp, a, mult_raw = _lru_gates(x, wr, wi, br_ref[...], bi_ref[...], lam_)
        row = lax.broadcasted_iota(jnp.int32, (tt, 128), 0)
        t0 = (row == 0) & first_tile
        mult = jnp.where(t0, 1.0, mult_raw)
        h = h_ref[...]
        do = do_ref[...]
        dh = do * _gelu(y)
        dy_ref[...] = (do * h * _dgelu(y)).astype(bf16)
        B = jnp.where(row == tt - 1, an[0:1, :], pltpu.roll(a, tt - 1, 0))
        L = dh
        d = 1
        while d < tt:
            keep = row < tt - d
            Lsh = jnp.where(keep, pltpu.roll(L, tt - d, 0), 0.0)
            Bsh = jnp.where(keep, pltpu.roll(B, tt - d, 0), 1.0)
            L = L + B * Lsh
            B = B * Bsh
            d *= 2
        L = L + B * lc[0:1, :]
        lc[...] = jnp.broadcast_to(L[0:1, :], lc.shape)
        an[...] = jnp.broadcast_to(a[0:1, :], an.shape)
        hprev = jnp.where(first_tile, 0.0, hp_ref[...])[PAD - 1:PAD, :]
        hm1 = jnp.where(row == 0, hprev, pltpu.roll(h, 1, 0))
        da = L * hm1
        dxc = L * i * mult
        di = L * x * mult
        dmult = jnp.where(t0, 0.0, L * x * i)
        da = da - jnp.where(t0, 0.0, dmult * a / mult_raw)
        dlog_a = da * a
        dr = dlog_a * (-LRU_C) * sp
        dsp = jnp.sum(dlog_a * (-LRU_C) * r, axis=0, keepdims=True)
        dpr = dr * r * (1.0 - r)
        dpi = di * i * (1.0 - i)
        dx_ref[...] = dxc + _dot(dpr, wr, _NT) + _dot(dpi, wi, _NT)
        dwr_ref[0] += _dot(x, dpr, _TN)
        dwi_ref[0] += _dot(x, dpi, _TN)
        dbr_ref[...] += jnp.sum(dpr, axis=0, keepdims=True)
        dbi_ref[...] += jnp.sum(dpi, axis=0, keepdims=True)
        dl_ref[...] += dsp * (-_sigmoid(-lam_))

    rows8 = tt // PAD
    tile = pl.BlockSpec((tt, 128), lambda g, t: (NT - 1 - t, g))
    vec = pl.BlockSpec((1, 128), lambda g, t: (0, g))
    wsp = pl.BlockSpec((128, 128), lambda g, t: (g, g))
    wout = pl.BlockSpec((1, 128, 128), lambda g, t: (g, 0, 0))
    return pl.pallas_call(
        body, grid=(G, NT),
        in_specs=[tile, pl.BlockSpec((tt, 128), lambda g, t: (NT - 1 - t, y_off + g)), wsp, wsp, vec, vec, vec, tile,
                  pl.BlockSpec((PAD, 128), lambda g, t: (jnp.maximum((NT - 1 - t) * rows8 - 1, 0), g)),
                  pl.BlockSpec((tt, 128), lambda g, t: (NT - 1 - t, d_off + g)), pl.BlockSpec(memory_space=pl.ANY)],
        out_specs=[tile, pl.BlockSpec((tt, 128), lambda g, t: (NT - 1 - t, y_off + g)), wout, wout, vec, vec, vec],
        out_shape=[jax.ShapeDtypeStruct((T, LRU_WIDTH), f32), jax.ShapeDtypeStruct(dp.shape, dp.dtype),
                   jax.ShapeDtypeStruct((G, 128, 128), f32), jax.ShapeDtypeStruct((G, 128, 128), f32),
                   jax.ShapeDtypeStruct((1, LRU_WIDTH), f32), jax.ShapeDtypeStruct((1, LRU_WIDTH), f32),
                   jax.ShapeDtypeStruct((1, LRU_WIDTH), f32)],
        input_output_aliases={10: 1},
        scratch_shapes=[pltpu.VMEM((8, 128), f32), pltpu.VMEM((8, 128), f32)],
        compiler_params=_cparams(("parallel", "arbitrary")), name=name,
    )(xc, p, wr, wi, br, bi, lam, hs, hs, dmix, dp)


_NN3 = (((2,), (1,)), ((0,), (0,)))
_NT3 = (((2,), (2,)), ((0,), (0,)))
_TN3 = (((1,), (1,)), ((0,), (0,)))


def _pairs(ref, K):
    C = GDN_CHUNK
    return jnp.stack([ref[c * C:(c + 1) * C, h * HEAD_DIM:(h + 1) * HEAD_DIM] for c in range(K) for h in range(GDN_HEADS)])


def _put_pairs(ref, val, K, col=0):
    C, H = GDN_CHUNK, GDN_HEADS
    for c in range(K):
        for h in range(H):
            ref[c * C:(c + 1) * C, col + h * HEAD_DIM:col + (h + 1) * HEAD_DIM] = val[c * H + h].astype(ref.dtype)


def _rowsum(x):
    H, C, L = x.shape
    return _dot(x.reshape(H * C, L), jnp.ones((L, HEAD_DIM), f32), _NN).reshape(H, C, HEAD_DIM)


def _gdn_pre(qr, kr, v, ba, alog, dtb):
    C, H = GDN_CHUNK, GDN_HEADS
    B = qr.shape[0]
    K = B // H
    lane = lax.broadcasted_iota(jnp.int32, (C, 128), 1)
    lane3 = lax.broadcasted_iota(jnp.int32, (B, C, 128), 2)
    ri = lax.broadcasted_iota(jnp.int32, (C, C), 0)
    ci = lax.broadcasted_iota(jnp.int32, (C, C), 1)
    rowc = lax.broadcasted_iota(jnp.int32, (C, 1), 0)
    col = lambda m, j: jnp.sum(jnp.where(lane == j, m, 0.0), axis=1, keepdims=True)
    ea = jnp.exp(alog)
    tri = (ri >= ci).astype(f32)
    g_all, beta_cols, G_cols = [], [], []
    for c in range(K):
        ba_c = ba[c * C:(c + 1) * C]
        g_c = -ea * _softplus(ba_c + dtb)
        G_c = _dot01(tri, g_c, _NN)
        s_c = _sigmoid(ba_c)
        g_all.append(g_c)
        beta_cols += [col(s_c, h) for h in range(H)]
        G_cols += [col(G_c, H + h) for h in range(H)]
    wide = lambda c: jnp.broadcast_to(c, (B, C, 128))
    beta = wide(jnp.stack(beta_cols))
    Gc = jnp.stack(G_cols)
    rq = lax.rsqrt(_rowsum(qr * qr) + EPS)
    rk = lax.rsqrt(_rowsum(kr * kr) + EPS)
    qh, kn = qr * rq, kr * rk
    qn = qh * (HEAD_DIM ** -0.5)
    Grow = _dot01(jnp.ones((B, C, 128), f32), jnp.where(lane3 == 0, Gc, 0.0), _NT3)
    incl = ri >= ci
    Di = jnp.where(incl, jnp.exp(jnp.where(incl, Gc - Grow, 0.0)), 0.0)
    Ds = jnp.where(ri > ci, Di, 0.0)
    Gl = jnp.sum(jnp.where(rowc == C - 1, Gc, 0.0), axis=1, keepdims=True)
    eG = wide(jnp.exp(Gc))
    eGl = wide(jnp.exp(Gl - Gc))
    cd = jnp.exp(Gl)
    kb = kn * beta
    vb = v * beta
    Lm = _dot(kb, kn, _NT3) * Ds
    kbg = kb * eG
    QK = _dot(qn, kn, _NT3) * Di
    qg = qn * eG
    kg = kn * eGl
    return dict(beta=beta, g_all=g_all, rq=rq, rk=rk, qh=qh, kn=kn, qn=qn, Di=Di, Ds=Ds, eG=eG, eGl=eGl, cd=cd,
                kb=kb, vb=vb, Lm=Lm, kbg=kbg, QK=QK, qg=qg, kg=kg, lane=lane, ri=ri, ci=ci, rowc=rowc, ea=ea)


def _unit_lower_inverse(Lm):
    C = Lm.shape[-1]
    ri = lax.broadcasted_iota(jnp.int32, (C, C), 0)
    ci = lax.broadcasted_iota(jnp.int32, (C, C), 1)
    same = lambda s: (ri // s) == (ci // s)
    Xd = jnp.where(same(8), -Lm, 0.0)
    Tinv = (ri == ci).astype(f32) + Xd
    Pw = Xd
    for _ in range(2):
        Pw = _dot(Pw, Pw, _NN3)
        Tinv = Tinv + _dot(Tinv, Pw, _NN3)
    for s in (8, 16, 32):
        off = jnp.where(same(2 * s) & jnp.logical_not(same(s)), Lm, 0.0)
        Tinv = Tinv - _dot(_dot(Tinv, off, _NN3), Tinv, _NN3)
    return Tinv


def _gdn_specs(T, rev):
    C = GDN_CHUNK
    H = GDN_HEADS
    K = min(GDN_STEP, T // C)
    NS = T // (C * K)
    nn = (lambda n: NS - 1 - n) if rev else (lambda n: n)
    wide = lambda blk: pl.BlockSpec((K * C, H * HEAD_DIM), lambda n: (nn(n), blk))
    one = lambda off: pl.BlockSpec((K * C, HEAD_DIM), lambda n: (nn(n), off))
    vec = pl.BlockSpec((1, 128), lambda n: (0, 0))
    st = lambda rows: pl.BlockSpec((K, H, rows, rows), lambda n: (nn(n), 0, 0, 0))
    return K, NS, wide, one, vec, st


def _gdn_fwd(qkv, p, alog, dtb, nw, name):
    T = qkv.shape[0]
    C, H = GDN_CHUNK, GDN_HEADS
    N = T // C
    K, NS, wide, one, vec, st_spec = _gdn_specs(T, False)

    def body(q_ref, k_ref, v_ref, z_ref, ba_ref, al_ref, dt_ref, nw_ref, y_ref, sp_ref, ti_ref, vn_ref, o_ref, st):
        @pl.when(pl.program_id(0) == 0)
        def _():
            st[...] = jnp.zeros_like(st)

        f = _gdn_pre(_pairs(q_ref, K), _pairs(k_ref, K), _pairs(v_ref, K), ba_ref[...], al_ref[...], dt_ref[...])
        Tinv = _unit_lower_inverse(f["Lm"])
        ti_ref[...] = Tinv.reshape(K, H, C, C).astype(bf16)
        w = _dot(Tinv, f["kbg"], _NN3)
        u = _dot(Tinv, f["vb"], _NN3)
        S = st[...]
        vns, os_ = [], []
        for c in range(K):
            sl = slice(c * H, (c + 1) * H)
            sp_ref[c] = S
            vn_c = u[sl] - _dot(w[sl], S, _NN3)
            os_.append(_dot(f["qg"][sl], S, _NN3) + _dot(f["QK"][sl], vn_c, _NN3))
            S = S * f["cd"][sl] + _dot(f["kg"][sl], vn_c, _TN3)
            vns.append(vn_c)
        st[...] = S
        vn, o = jnp.concatenate(vns), jnp.concatenate(os_)
        r = lax.rsqrt(_rowsum(o * o) * (1.0 / HEAD_DIM) + EPS)
        _put_pairs(y_ref, o * r * nw_ref[...] * _silu(_pairs(z_ref, K)), K)
        _put_pairs(vn_ref, vn, K)
        _put_pairs(o_ref, o, K)

    wide_f32 = jax.ShapeDtypeStruct((T, H * HEAD_DIM), f32)
    return pl.pallas_call(
        body, grid=(NS,),
        in_specs=[wide(0), wide(1), wide(2), wide(3), one(4 * H), vec, vec, vec],
        out_specs=[wide(0), st_spec(HEAD_DIM), st_spec(C), wide(0), wide(0)],
        out_shape=[jax.ShapeDtypeStruct((T, H * HEAD_DIM), bf16), jax.ShapeDtypeStruct((N, H, HEAD_DIM, HEAD_DIM), f32),
                   jax.ShapeDtypeStruct((N, H, C, C), bf16), jax.ShapeDtypeStruct((T, H * HEAD_DIM), bf16), wide_f32],
        scratch_shapes=[pltpu.VMEM((H, HEAD_DIM, HEAD_DIM), f32)],
        compiler_params=_cparams(("arbitrary",)), name=name,
    )(qkv, qkv, qkv, p, p, alog, dtb, nw)


def _gdn_bwd(qkv, p, alog, dtb, nw, sprev, tinv, vn_all, o_all, dy_all, name):
    T = qkv.shape[0]
    C, H = GDN_CHUNK, GDN_HEADS
    N = T // C
    K, NS, wide, one, vec, st_spec = _gdn_specs(T, True)
    rs = lambda m: jnp.sum(m, axis=2, keepdims=True)

    def body(q_ref, k_ref, v_ref, z_ref, ba_ref, al_ref, dt_ref, nw_ref, sp_ref, ti_ref, vn_ref, o_ref, dy_ref,
             dqkv_ref, dz_ref, dba_ref, dal_ref, ddt_ref, dnw_ref, dst):
        @pl.when(pl.program_id(0) == 0)
        def _():
            dst[...] = jnp.zeros_like(dst)
            dal_ref[...] = jnp.zeros_like(dal_ref)
            ddt_ref[...] = jnp.zeros_like(ddt_ref)
            dnw_ref[...] = jnp.zeros_like(dnw_ref)

        ba, dtb_, nwv = ba_ref[...], dt_ref[...], nw_ref[...]
        v = _pairs(v_ref, K)
        f = _gdn_pre(_pairs(q_ref, K), _pairs(k_ref, K), v, ba, al_ref[...], dtb_)
        beta, kn, qn, kb, vb, kbg = f["beta"], f["kn"], f["qn"], f["kb"], f["vb"], f["kbg"]
        eG, eGl, cd, Di, Ds, QK, qg, kg = f["eG"], f["eGl"], f["cd"], f["Di"], f["Ds"], f["QK"], f["qg"], f["kg"]
        lane, ri, ci, rowc = f["lane"], f["ri"], f["ci"], f["rowc"]
        Tinv = ti_ref[...].reshape(K * H, C, C)
        S = sp_ref[...].reshape(K * H, HEAD_DIM, HEAD_DIM)
        w_ = _dot(Tinv, kbg, _NN3)
        vn, o = _pairs(vn_ref, K), _pairs(o_ref, K)
        z, dy = _pairs(z_ref, K), _pairs(dy_ref, K)
        r = lax.rsqrt(_rowsum(o * o) * (1.0 / HEAD_DIM) + EPS)
        nrm = o * r
        sz = _silu(z)
        dn = dy * nwv * sz
        _put_pairs(dz_ref, dy * nrm * nwv * _dsilu(z), K)
        dnw_ref[...] += jnp.sum(jnp.sum(dy * nrm * sz, axis=0), axis=0, keepdims=True)
        do = r * (dn - nrm * (_rowsum(dn * nrm) * (1.0 / HEAD_DIM)))
        dvn_do = _dot(QK, do, _TN3)
        dS_do = _dot(qg, do, _TN3)
        dqg = _dot(do, S, _NT3)
        dQK = _dot(do, vn, _NT3)
        dS = dst[...]
        dS1s, dvns = [None] * K, [None] * K
        for c in reversed(range(K)):
            sl = slice(c * H, (c + 1) * H)
            dS1s[c] = dS
            dvns[c] = _dot(kg[sl], dS, _NN3) + dvn_do[sl]
            dS = cd[sl] * dS + dS_do[sl] - _dot(w_[sl], dvns[c], _TN3)
        dst[...] = dS
        dS1, dvn = jnp.concatenate(dS1s), jnp.concatenate(dvns)
        dcd = jnp.sum(jnp.sum(S * dS1, axis=2, keepdims=True), axis=1, keepdims=True)
        dkg = _dot(vn, dS1, _NT3)
        dw = -_dot(dvn, S, _NT3)
        dqn = dqg * eG
        dkn = dkg * eGl
        deGl = rs(dkg * kn)
        dQKr = dQK * Di
        E = dQK * QK
        dqn = dqn + _dot(dQKr, kn, _NN3)
        dkn = dkn + _dot(dQKr, qn, _TN3)
        dT = _dot(dvn, vb, _NT3) + _dot(dw, kbg, _NT3)
        dvb = _dot(Tinv, dvn, _TN3)
        dkbg = _dot(Tinv, dw, _TN3)
        dkb = dkbg * eG
        deG = rs(dqg * qn + dkbg * kb)
        dL = -_dot(_dot(Tinv, dT, _TN3), Tinv, _NT3)
        dKK = dL * Ds
        E = E + dL * f["Lm"]
        dkb = dkb + _dot(dKK, kn, _NN3)
        dkn = dkn + _dot(dKK, kb, _TN3) + dkb * beta
        dbeta = rs(dkb * kn + dvb * v)
        _put_pairs(dqkv_ref, dvb * beta, K, 2 * H * HEAD_DIM)
        dG = rs(E) - rs(jnp.swapaxes(E, 1, 2)) + deG * eG - deGl * eGl
        dGl = jnp.sum(deGl * eGl, axis=1, keepdims=True) + dcd * cd
        dG = dG + jnp.where(rowc == C - 1, dGl, 0.0)
        qh = f["qh"]
        _put_pairs(dqkv_ref, (HEAD_DIM ** -0.5) * f["rq"] * (dqn - qh * _rowsum(dqn * qh)), K)
        _put_pairs(dqkv_ref, f["rk"] * (dkn - kn * _rowsum(dkn * kn)), K, H * HEAD_DIM)
        db = dbeta * beta * (1.0 - beta)
        triu = (ri <= ci).astype(f32)
        for c in range(K):
            db_all = jnp.where(lane == 0, db[c * H], 0.0)
            dG_all = jnp.where(lane == H, dG[c * H], 0.0)
            for h in range(1, H):
                db_all = db_all + jnp.where(lane == h, db[c * H + h], 0.0)
                dG_all = dG_all + jnp.where(lane == H + h, dG[c * H + h], 0.0)
            dg_all = _dot01(triu, dG_all, _NN)
            da_all = dg_all * (-f["ea"]) * _sigmoid(ba[c * C:(c + 1) * C] + dtb_)
            dba_ref[c * C:(c + 1) * C, :] = (db_all + da_all).astype(bf16)
            ddt_ref[...] += jnp.sum(da_all, axis=0, keepdims=True)
            dal_ref[...] += jnp.sum(dg_all * f["g_all"][c], axis=0, keepdims=True)

    small = jax.ShapeDtypeStruct((1, 128), f32)
    return pl.pallas_call(
        body, grid=(NS,),
        in_specs=[wide(0), wide(1), wide(2), wide(3), one(4 * H), vec, vec, vec, st_spec(HEAD_DIM), st_spec(C),
                  wide(0), wide(0), wide(0)],
        out_specs=[pl.BlockSpec((K * C, 3 * H * HEAD_DIM), lambda n: (NS - 1 - n, 0)), wide(3), one(0), vec, vec, vec],
        out_shape=[jax.ShapeDtypeStruct((T, 3 * H * HEAD_DIM), f32), jax.ShapeDtypeStruct((T, ODD_PAD), bf16),
                   jax.ShapeDtypeStruct((T, 128), bf16), small, small, small],
        scratch_shapes=[pltpu.VMEM((H, HEAD_DIM, HEAD_DIM), f32)],
        compiler_params=_cparams(("arbitrary",)), name=name,
    )(qkv, qkv, qkv, p, p, alog, dtb, nw, sprev, tinv, vn_all, o_all, dy_all)


def _lanes_from(x, s):
    return x if s % 128 == 0 else pltpu.roll(x, (128 - s) % 128, 1)


def _odd_assemble(g, name):
    R = g.shape[1]
    tr = min(256, R)
    n_blk = ODD_SHARD_PAD // 128

    def body(g_ref, o_ref):
        lane = lax.broadcasted_iota(jnp.int32, (tr, 128), 1)
        blk = lambda d, m: g_ref[d, :, m * 128:(m + 1) * 128]
        for gb in range(ODD_PAD // 128):
            c0 = 128 * gb
            if c0 >= ODD_IN:
                o_ref[:, c0:c0 + 128] = jnp.zeros((tr, 128), g.dtype)
                continue
            d0 = c0 // ODD_SHARD
            m0, sh = divmod(c0 - ODD_SHARD * d0, 128)
            take = min(128, ODD_SHARD * (d0 + 1) - c0)
            p = _lanes_from(blk(d0, m0), sh)
            if sh and m0 + 1 < n_blk:
                p = jnp.where(lane < 128 - sh, p, _lanes_from(blk(d0, m0 + 1), sh))
            if take < 128:
                nxt = pltpu.roll(blk(d0 + 1, 0), take, 1) if d0 + 1 < N_DEV else jnp.zeros((tr, 128), g.dtype)
                p = jnp.where(lane < take, p, nxt)
            o_ref[:, c0:c0 + 128] = p

    return pl.pallas_call(
        body, grid=(R // tr,),
        in_specs=[pl.BlockSpec((N_DEV, tr, ODD_SHARD_PAD), lambda i: (0, i, 0))],
        out_specs=pl.BlockSpec((tr, ODD_PAD), lambda i: (i, 0)),
        out_shape=jax.ShapeDtypeStruct((R, ODD_PAD), g.dtype),
        compiler_params=_cparams(("parallel",)), name=name,
    )(g)


def _odd_split(w, name):
    R = w.shape[0]
    tr = min(256, R)

    def body(w_ref, o_ref):
        lane = lax.broadcasted_iota(jnp.int32, (tr, 128), 1)
        blk = lambda gb: w_ref[:, gb * 128:(gb + 1) * 128]
        for d in range(N_DEV):
            for m in range(ODD_SHARD_PAD // 128):
                g0, sh = divmod(ODD_SHARD * d + 128 * m, 128)
                p = _lanes_from(blk(g0), sh)
                if sh and g0 + 1 < ODD_PAD // 128:
                    p = jnp.where(lane < 128 - sh, p, _lanes_from(blk(g0 + 1), sh))
                real = ODD_SHARD - 128 * m
                if real < 128:
                    p = jnp.where(lane < real, p, jnp.zeros_like(p))
                o_ref[d, :, m * 128:(m + 1) * 128] = p

    return pl.pallas_call(
        body, grid=(R // tr,),
        in_specs=[pl.BlockSpec((tr, ODD_PAD), lambda i: (i, 0))],
        out_specs=pl.BlockSpec((N_DEV, tr, ODD_SHARD_PAD), lambda i: (0, i, 0)),
        out_shape=jax.ShapeDtypeStruct((N_DEV, R, ODD_SHARD_PAD), w.dtype),
        compiler_params=_cparams(("parallel",)), name=name,
    )(w)


def _adamw(w, gs, m, v, name, layer=None, prev=None):
    R, Cc = w.shape[-2:]
    S = gs.shape[0]
    tr = R
    if S * R * Cc * 4 > (4 << 20):
        for cand in (256, 128, 64, 32, 16, 8):
            if R % cand == 0 and R > cand:
                tr = cand
                break
    c1 = 1.0 - ADAM_B1 ** ADAM_STEP
    c2 = 1.0 - ADAM_B2 ** ADAM_STEP

    def body(w_ref, g_ref, m_ref, v_ref, *rest):
        go_ref, d_ref, mo_ref, vo_ref = rest[-4:]
        g = g_ref[0].astype(f32)
        for s in range(1, S):
            g = g + g_ref[s].astype(f32)
        mn = ADAM_B1 * m_ref[...] + (1.0 - ADAM_B1) * g
        vn = ADAM_B2 * v_ref[...] + (1.0 - ADAM_B2) * (g * g)
        go_ref[...] = g
        mo_ref[...] = mn
        vo_ref[...] = vn
        d_ref[...] = -ADAM_LR * ((mn / c1) / (jnp.sqrt(vn / c2) + ADAM_EPS) + ADAM_WD * w_ref[...])

    tc = Cc if gs.shape[-1] == Cc else 128
    if layer is None:
        blk = pl.BlockSpec((tr, tc), lambda i, j: (i, j))
    else:
        blk = pl.BlockSpec((None, tr, tc), lambda i, j: (layer, i, j))
    out = jax.ShapeDtypeStruct(w.shape, f32)
    carried = [] if prev is None else list(prev)
    return pl.pallas_call(
        body, grid=(R // tr, pl.cdiv(Cc, tc)),
        in_specs=[blk, pl.BlockSpec((S, tr, tc), lambda i, j: (0, i, j)), blk, blk]
        + [pl.BlockSpec(memory_space=pl.ANY)] * len(carried),
        out_specs=[blk] * 4, out_shape=[out] * 4,
        input_output_aliases={4 + j: j for j in range(len(carried))},
        compiler_params=_cparams(("parallel", "parallel")), name=name,
    )(w, gs, m, v, *carried)


def _me():
    x, y, c = lax.axis_index("x"), lax.axis_index("y"), lax.axis_index("c")
    return x, y, c, 4 * x + 2 * y + c


def _peer(k):
    x, y, c, _ = _me()
    px = 1 - x if k & 4 else x
    py = 1 - y if k & 2 else y
    pc = 1 - c if k & 1 else c
    return (px, py, pc), 4 * px + 2 * py + pc


_HBM = pl.BlockSpec(memory_space=pltpu.HBM)
_SEM = pl.BlockSpec(memory_space=pltpu.SEMAPHORE)
_EFFECT = pltpu.SideEffectType.DATAFLOW_SIDE_EFFECTING


def _copy(src, land, ssem, rsem, k, blocked, landing_slot_of_peer):
    pid, pidx = _peer(k)
    slot = pidx if landing_slot_of_peer else _me()[3]
    return pltpu.make_async_remote_copy(src_ref=src.at[pidx] if blocked else src, dst_ref=land.at[slot],
                                        send_sem=ssem.at[k - 1], recv_sem=rsem.at[k - 1], device_id=pid, device_id_type=MESH)


def _send_start(srcs, blocked, name):
    n = len(srcs)
    lands = [lax.empty(a.shape if blocked else (N_DEV,) + a.shape, a.dtype) for a in srcs]

    def body(*refs):
        src, land, sems, token = refs[:n], refs[n:2 * n], refs[2 * n:4 * n], refs[-1]
        for i in range(n):
            for k in range(1, N_DEV):
                _copy(src[i], land[i], sems[2 * i], sems[2 * i + 1], k, blocked, False).start()
        token[...] = jnp.zeros_like(token)

    sem = pltpu.SemaphoreType.DMA((N_DEV - 1,))
    hbm = lambda a: pltpu.with_memory_space_constraint(a, pltpu.HBM)
    res = pl.pallas_call(
        body, name=name,
        out_shape=tuple([sem] * (2 * n)) + tuple(pltpu.HBM(a.shape, a.dtype) for a in srcs + lands)
        + (jax.ShapeDtypeStruct((8, 128), f32),),
        in_specs=(_HBM,) * (2 * n),
        out_specs=(_SEM,) * (2 * n) + (_HBM,) * (2 * n) + (pl.BlockSpec(memory_space=pltpu.VMEM),),
        input_output_aliases={j: 2 * n + j for j in range(2 * n)},
        compiler_params=pltpu.CompilerParams(has_side_effects=_EFFECT),
    )(*[hbm(a) for a in srcs], *[hbm(a) for a in lands])
    handles = [(res[2 * i], res[2 * i + 1], res[2 * n + i], res[3 * n + i]) for i in range(n)]
    return handles, res[-1]


def _send_wait(handle, blocked, after, name):
    ssem, rsem, src, land = handle

    def body(src_ref, land_ref, ssem_ref, rsem_ref, after_ref, src_out, land_out):
        for k in range(1, N_DEV):
            cp = _copy(src_ref, land_ref, ssem_ref, rsem_ref, k, blocked, True)
            cp.wait_send()
            cp.wait_recv()

    return pl.pallas_call(
        body, name=name, out_shape=(pltpu.HBM(src.shape, src.dtype), pltpu.HBM(land.shape, land.dtype)),
        in_specs=(_HBM, _HBM, _SEM, _SEM, pl.BlockSpec(memory_space=pl.ANY)), out_specs=(_HBM, _HBM),
        input_output_aliases={0: 0, 1: 1}, compiler_params=pltpu.CompilerParams(has_side_effects=_EFFECT),
    )(src, land, ssem, rsem, after)


def _block_diag(w):
    nb, bs = w.shape[0], w.shape[1]
    eye = jnp.eye(nb, dtype=w.dtype)
    return (eye[:, None, :, None] * w[:, :, None, :]).reshape(nb * bs, nb * bs)


def _diag_blocks(d):
    return jnp.stack([d[g, s * 64:(s + 1) * 64, s * 64:(s + 1) * 64] for g in range(4) for s in range(2)])


_SQUARE_TILES = dict(tm=1024, tn=1024, tk=1024)


def _mlp_fwd(x, hm, wu, wd, tag, epilogue, extras, outs):
    (r,) = _matmul(hm, wu, "nn", outs=[bf16], epilogue=lambda acc: (jnp.maximum(acc, 0.0),), name=f"mlp_up_{tag}")
    res = _matmul(r, wd, "nn", outs=outs, extras=(x,) + tuple(extras), epilogue=epilogue, a_map=jnp.square,
                  name=f"mlp_down_{tag}", **_SQUARE_TILES)
    return res, (hm, r)


def _mlp_bwd(x, nw, wu, wd, saved, dxo, dxo_b, tag):
    hm, r = saved
    (du,) = _matmul(dxo_b, wd, "nt", outs=[bf16], extras=(r,), epilogue=lambda acc, rr: (acc * (2.0 * rr.astype(f32)),),
                    name=f"mlp_dact_{tag}")
    (dwd,) = _matmul(r, dxo_b, "tn", outs=[bf16], a_map=jnp.square, name=f"mlp_dwd_{tag}", **_SQUARE_TILES)
    (dwu,) = _matmul(hm, du, "tn", outs=[bf16], shard_cols=2, name=f"mlp_dwu_{tag}")
    dx, dx_b, dnw = _matmul(du, wu, "nt", outs=_RMS_BWD_OUTS, extras=(x, dxo, nw), epilogue=_rms_bwd_ep,
                            name=f"mlp_dh_{tag}", **_SQUARE_TILES)
    return dx, dx_b, dnw, dwu, dwd.reshape(N_DEV, D_FF // N_DEV, D_MODEL)


def _local_step(x, tgt, P, weight, sink):
    T = x.shape[0]
    cos, sin = _rope_tables(T)
    rtab = _ret_tables()
    row = lambda a: a.reshape(1, -1)
    mix_nw, mlp_nw = P["mixer_norm_w"], P["mlp_norm_w"]
    wr_bd, wi_bd = _block_diag(P["lru_w_r"]), _block_diag(P["lru_w_i"])
    lru_b, lru_br, lru_bi, lru_lam = row(P["lru_conv_b"]), row(P["lru_b_r"]), row(P["lru_b_i"]), row(P["lru_lambda"])
    pad16 = lambda a: jnp.pad(a.reshape(1, GDN_HEADS), ((0, 0), (GDN_HEADS, 128 - 2 * GDN_HEADS)))
    alog, dtb = pad16(P["gdn_a_log"]), pad16(P["gdn_dt_bias"])
    gnw = row(P["gdn_norm_w"])

    x0 = x
    h0 = _rms_fwd(x0, mix_nw[0:1], "rms_mix_0")
    w_ie = weight("w_in_even", h0)
    (pe,) = _matmul(h0, w_ie, "nn", outs=[f32], name="in_even")
    mix0, o_ret, s_ret = _ret_fwd(pe, cos, sin, rtab, "ret_fwd")
    w_lc = weight("lru_conv_w", pe)
    xc = _conv_fwd(pe, 4, w_lc, lru_b, False, "lru_conv_fwd")
    mix0, h_lru = _lru_fwd(xc, pe, 20, wr_bd, wi_bd, lru_br, lru_bi, lru_lam, mix0, "lru_fwd")
    w_oe = weight("w_out_even", mix0)
    x1, hm0 = _matmul(mix0, w_oe, "nn", outs=[f32, bf16], extras=(x0, mlp_nw[0:1]), epilogue=_residual_rms_ep,
                      name="out_even", tm=1024, tn=D_MODEL)
    w_u0, w_d0 = weight("w_up0", x1), weight("w_down0", x1)
    (x2, h1), mlp0 = _mlp_fwd(x1, hm0, w_u0, w_d0, "0", _residual_rms_ep, (mix_nw[1:2],), [f32, bf16])
    w_io = weight("w_in_odd", h1)
    (po,) = _matmul(h1, w_io, "nn", outs=[f32], tm=2048, tn=ODD_PAD // 3, name="in_odd")
    w_gc = weight("gdn_conv_w", po)
    qkv = _conv_fwd(po, 0, w_gc, None, True, "gdn_conv_fwd")
    y_gdn, s_gdn, ti_gdn, vn_gdn, o_gdn = _gdn_fwd(qkv, po, alog, dtb, gnw, "gdn_fwd")
    w_oo = weight("w_out_odd", y_gdn)
    x3, hm1 = _matmul(y_gdn, w_oo, "nn", outs=[f32, bf16], extras=(x2, mlp_nw[1:2]), epilogue=_residual_rms_ep,
                      name="out_odd", tm=1024, tn=D_MODEL)
    w_u1, w_d1 = weight("w_up1", x3), weight("w_down1", x3)
    (loss, dx4, dx4_b, d_final), mlp1 = _mlp_fwd(x3, hm1, w_u1, w_d1, "1", _loss_ep, (row(P["final_norm_w"]), tgt),
                                                 _LOSS_OUTS)
    dx3, dx3_b, d_mlp_nw1, d_wu1, d_wd1 = _mlp_bwd(x3, mlp_nw[1:2], w_u1, w_d1, mlp1, dx4, dx4_b, "1")
    tok = sink(dict(w_up1=d_wu1, w_down1=d_wd1))
    (dy_gdn,) = _matmul(dx3_b, w_oo, "nt", outs=[f32], name="out_odd_dx")
    (d_woo,) = _matmul(y_gdn, dx3_b, "tn", outs=[bf16], name="out_odd_dw")
    dqkv, dpo, dba, d_alog, d_dtb, d_gnw = _gdn_bwd(qkv, po, alog, dtb, gnw + tok[0:1, :], s_gdn, ti_gdn, vn_gdn, o_gdn, dy_gdn,
                                                  "gdn_bwd")
    dpo, d_gconv, _ = _conv_bwd(po, 0, w_gc, None, True, dqkv, dpo, "gdn_conv_bwd")
    dpo = lax.dynamic_update_slice(dpo, dba, (0, 4 * D_MODEL))
    (d_wio,) = _matmul(h1, dpo, "tn", outs=[bf16], tn=ODD_PAD // 3, name="in_odd_dw")
    tok = sink(dict(w_out_odd=d_woo.reshape(N_DEV, D_MODEL // N_DEV, D_MODEL), w_in_odd=_odd_split(d_wio, "w_in_odd_split")))
    dx2, dx2_b, d_mix_nw1 = _matmul(dpo, w_io, "nt", outs=_RMS_BWD_OUTS, extras=(x2, dx3, mix_nw[1:2] + tok[0:1, 0:1]),
                                    epilogue=_rms_bwd_ep, tm=1024, tn=1024, tk=ODD_PAD // 3, name="in_odd_dx")
    dx1, dx1_b, d_mlp_nw0, d_wu0, d_wd0 = _mlp_bwd(x1, mlp_nw[0:1], w_u0, w_d0, mlp0, dx2, dx2_b, "0")
    (d_woe,) = _matmul(mix0, dx1_b, "tn", outs=[bf16], name="out_even_dw")
    tok = sink(dict(w_up0=d_wu0, w_down0=d_wd0, w_out_even=d_woe.reshape(N_DEV, D_MODEL // N_DEV, D_MODEL)))
    (dmix0,) = _matmul(dx1_b, w_oe, "nt", outs=[f32], name="out_even_dx")
    dpe = _ret_bwd(pe, cos, sin, rtab, o_ret, s_ret, dmix0, "ret_bwd")
    dxc, dpe, d_wr, d_wi, d_br, d_bi, d_lam = _lru_bwd(xc, pe, 20, wr_bd, wi_bd, lru_br, lru_bi, lru_lam + tok[0:1, 0:1],
                                                       h_lru, dmix0, 4, dpe, "lru_bwd")
    dpe, d_lconv, d_lconv_b = _conv_bwd(pe, 4, w_lc, lru_b, False, dxc, dpe, "lru_conv_bwd")
    (d_wie,) = _matmul(h0, dpe, "tn", outs=[bf16], shard_cols=2, name="in_even_dw")
    tok = sink(dict(w_in_even=d_wie))
    dx0, _, d_mix_nw0 = _matmul(dpe, w_ie, "nt", outs=_RMS_BWD_OUTS, extras=(x0, dx1, mix_nw[0:1] + tok[0:1, 0:1]),
                                epilogue=_rms_bwd_ep, name="in_even_dx", **_SQUARE_TILES)

    G = dict(
        mixer_norm_w=jnp.concatenate([d_mix_nw0, d_mix_nw1], axis=0),
        mlp_norm_w=jnp.concatenate([d_mlp_nw0, d_mlp_nw1], axis=0),
        final_norm_w=d_final.reshape(-1),
        lru_conv_w=d_lconv, lru_conv_b=d_lconv_b.reshape(-1),
        lru_w_r=_diag_blocks(d_wr), lru_b_r=d_br.reshape(-1), lru_w_i=_diag_blocks(d_wi), lru_b_i=d_bi.reshape(-1),
        lru_lambda=d_lam.reshape(-1), gdn_conv_w=d_gconv,
        gdn_a_log=d_alog[0, GDN_HEADS:2 * GDN_HEADS], gdn_dt_bias=d_dtb[0, GDN_HEADS:2 * GDN_HEADS],
        gdn_norm_w=d_gnw.reshape(-1),
    )
    return loss, dx0, G


_SMALL = ["mixer_norm_w", "mlp_norm_w", "final_norm_w", "lru_conv_b", "lru_w_r", "lru_b_r", "lru_w_i", "lru_b_i",
          "lru_lambda", "gdn_a_log", "gdn_dt_bias", "gdn_norm_w"]
_PACK_ROWS = 688


def _pack(parts):
    flat = jnp.concatenate([p.reshape(-1) for p in parts])
    return jnp.pad(flat, (0, _PACK_ROWS * 128 - flat.shape[0])).reshape(_PACK_ROWS, 128)


def _unpack(packed, shapes):
    flat = packed.reshape(-1)
    out, off = [], 0
    for s in shapes:
        n = int(np.prod(s))
        out.append(flat[off:off + n].reshape(s))
        off += n
    return out


def kernel(x, mixer_norm_w, mlp_norm_w, final_norm_w, w_in_even, lru_conv_w, lru_conv_b, lru_w_r, lru_b_r, lru_w_i, lru_b_i, lru_lambda, w_out_even, w_in_odd, gdn_conv_w, gdn_a_log, gdn_dt_bias, gdn_norm_w, w_out_odd, w_up, w_down, loss_target, m_mixer_norm_w, m_mlp_norm_w, m_final_norm_w, m_w_in_even, m_lru_conv_w, m_lru_conv_b, m_lru_w_r, m_lru_b_r, m_lru_w_i, m_lru_b_i, m_lru_lambda, m_w_out_even, m_w_in_odd, m_gdn_conv_w, m_gdn_a_log, m_gdn_dt_bias, m_gdn_norm_w, m_w_out_odd, m_w_up, m_w_down, v_mixer_norm_w, v_mlp_norm_w, v_final_norm_w, v_w_in_even, v_lru_conv_w, v_lru_conv_b, v_lru_w_r, v_lru_b_r, v_lru_w_i, v_lru_b_i, v_lru_lambda, v_w_out_even, v_w_in_odd, v_gdn_conv_w, v_gdn_a_log, v_gdn_dt_bias, v_gdn_norm_w, v_w_out_odd, v_w_up, v_w_down):
    Pw = dict(mixer_norm_w=mixer_norm_w, mlp_norm_w=mlp_norm_w, final_norm_w=final_norm_w, w_in_even=w_in_even,
              lru_conv_w=lru_conv_w, lru_conv_b=lru_conv_b, lru_w_r=lru_w_r, lru_b_r=lru_b_r, lru_w_i=lru_w_i,
              lru_b_i=lru_b_i, lru_lambda=lru_lambda, w_out_even=w_out_even, w_in_odd=w_in_odd, gdn_conv_w=gdn_conv_w,
              gdn_a_log=gdn_a_log, gdn_dt_bias=gdn_dt_bias, gdn_norm_w=gdn_norm_w, w_out_odd=w_out_odd, w_up=w_up,
              w_down=w_down)
    Pm = dict(mixer_norm_w=m_mixer_norm_w, mlp_norm_w=m_mlp_norm_w, final_norm_w=m_final_norm_w, w_in_even=m_w_in_even,
              lru_conv_w=m_lru_conv_w, lru_conv_b=m_lru_conv_b, lru_w_r=m_lru_w_r, lru_b_r=m_lru_b_r, lru_w_i=m_lru_w_i,
              lru_b_i=m_lru_b_i, lru_lambda=m_lru_lambda, w_out_even=m_w_out_even, w_in_odd=m_w_in_odd,
              gdn_conv_w=m_gdn_conv_w, gdn_a_log=m_gdn_a_log, gdn_dt_bias=m_gdn_dt_bias, gdn_norm_w=m_gdn_norm_w,
              w_out_odd=m_w_out_odd, w_up=m_w_up, w_down=m_w_down)
    Pv = dict(mixer_norm_w=v_mixer_norm_w, mlp_norm_w=v_mlp_norm_w, final_norm_w=v_final_norm_w, w_in_even=v_w_in_even,
              lru_conv_w=v_lru_conv_w, lru_conv_b=v_lru_conv_b, lru_w_r=v_lru_w_r, lru_b_r=v_lru_b_r, lru_w_i=v_lru_w_i,
              lru_b_i=v_lru_b_i, lru_lambda=v_lru_lambda, w_out_even=v_w_out_even, w_in_odd=v_w_in_odd,
              gdn_conv_w=v_gdn_conv_w, gdn_a_log=v_gdn_a_log, gdn_dt_bias=v_gdn_dt_bias, gdn_norm_w=v_gdn_norm_w,
              w_out_odd=v_w_out_odd, w_up=v_w_up, w_down=v_w_down)
    me = _me()[3]
    T = x.shape[1]

    cols = lambda g: jnp.transpose(g, (1, 0, 2)).reshape(g.shape[1], -1)
    rows = lambda g: g.reshape(-1, g.shape[2])
    wide = lambda g: _odd_assemble(g, "w_in_odd_assemble")
    odd_shard = jnp.pad(w_in_odd[0].astype(bf16), ((0, 0), (0, ODD_SHARD_PAD - ODD_SHARD)))
    as_is = lambda g: g
    gather = dict(
        w_in_even=(w_in_even[0].astype(bf16), cols), lru_conv_w=(lru_conv_w[0], cols),
        w_out_even=(w_out_even[0].astype(bf16), rows), w_up0=(w_up[0].astype(bf16), as_is), w_down0=(w_down[0].astype(bf16), rows),
        w_in_odd=(odd_shard, wide), gdn_conv_w=(gdn_conv_w[0], cols),
        w_out_odd=(w_out_odd[0].astype(bf16), rows), w_up1=(w_up[1].astype(bf16), as_is), w_down1=(w_down[1].astype(bf16), rows))
    handles, tok = _send_start([s for s, _ in gather.values()], False, "gather_start")
    handles = dict(zip(gather, handles))
    full = {}

    def weight(name, after):
        if name not in full:
            shard, landed = _send_wait(handles[name], False, after, f"gather_wait_{name}")
            full[name] = gather[name][1](lax.dynamic_update_slice_in_dim(landed, shard[None], me, 0))
        return full[name]

    P = {k: Pw[k] for k in ("mlp_norm_w", "final_norm_w")}
    P["mixer_norm_w"] = mixer_norm_w + tok[0:1, 0:1]
    for k in ("lru_w_r", "lru_w_i", "lru_conv_b", "lru_b_r", "lru_b_i", "lru_lambda", "gdn_a_log", "gdn_dt_bias", "gdn_norm_w"):
        P[k] = Pw[k][0]

    sent = {}

    def sink(grads):
        hs, token = _send_start(list(grads.values()), True, "grads_start_" + "_".join(grads))
        sent.update(zip(grads, hs))
        return token

    loss, dx, G = _local_step(x[0], loss_target[0], P, weight, sink)
    small_g = [G[k].reshape(Pw[k].shape) for k in _SMALL] + [G["lru_conv_w"], G["gdn_conv_w"], loss[0, 0:1]]
    packed = _pack(small_g)
    sink(dict(small=jnp.broadcast_to(packed[None], (N_DEV,) + packed.shape)))

    def received(name, after=dx):
        g, landed = _send_wait(sent[name], True, after, f"grads_wait_{name}")
        return lax.dynamic_update_slice_in_dim(landed, lax.dynamic_slice_in_dim(g, me, 1, 0), me, 0)

    out = {}
    nff = D_FF // N_DEV

    def whole(name, gs):
        out[name] = tuple(_adamw(Pw[name], gs, Pm[name], Pv[name], f"adamw_{name}", layer=0))

    def layers(name):
        res = None
        for l in range(2):
            res = _adamw(Pw[name], received(f"{name}{l}"), Pm[name], Pv[name], f"adamw_{name}{l}", layer=l, prev=res)
        out[name] = tuple(res)

    layers("w_up")
    layers("w_down")
    whole("w_out_odd", received("w_out_odd"))
    whole("w_in_odd", received("w_in_odd"))
    whole("w_out_even", received("w_out_even"))
    whole("w_in_even", received("w_in_even", out["w_out_even"][1]))
    small_shapes = [Pw[k].shape for k in _SMALL]
    pw, pm, pv = (_pack([Q[k] for k in _SMALL]) for Q in (Pw, Pm, Pv))
    sg, sd, sm, sv = _adamw(pw, received("small", out["w_in_even"][1]), pm, pv, "adamw_small")
    for arrs_i, packed_out in enumerate((sg, sd, sm, sv)):
        for k, a in zip(_SMALL, _unpack(packed_out, small_shapes)):
            out.setdefault(k, [None] * 4)[arrs_i] = a
    n_small = sum(int(np.prod(s)) for s in small_shapes)
    gflat = sg.reshape(-1)
    g_lconv = gflat[n_small:n_small + CONV_K * LRU_WIDTH].reshape(CONV_K, LRU_WIDTH)
    g_gconv = gflat[n_small + CONV_K * LRU_WIDTH:n_small + CONV_K * (LRU_WIDTH + 3072)].reshape(CONV_K, 3072)
    whole("lru_conv_w", lax.dynamic_slice_in_dim(g_lconv, me * 64, 64, axis=1)[None])
    whole("gdn_conv_w", lax.dynamic_slice_in_dim(g_gconv, me * 384, 384, axis=1)[None])

    names = ["mixer_norm_w", "mlp_norm_w", "final_norm_w", "w_in_even", "lru_conv_w", "lru_conv_b", "lru_w_r", "lru_b_r",
             "lru_w_i", "lru_b_i", "lru_lambda", "w_out_even", "w_in_odd", "gdn_conv_w", "gdn_a_log", "gdn_dt_bias",
             "gdn_norm_w", "w_out_odd", "w_up", "w_down"]
    total = gflat[n_small + CONV_K * (LRU_WIDTH + 3072)]
    res = [total, dx[None]]
    for j in range(4):
        res += [out[k][j] for k in names]
    return tuple(res)
```

```python
import math

import numpy as np
import jax
import jax.numpy as jnp
from jax import lax
from jax.experimental import pallas as pl
from jax.experimental.pallas import tpu as pltpu

f32 = jnp.float32
bf16 = jnp.bfloat16

N_DEV = 8
D_MODEL = 1024
D_FF = 4096
EPS = 1e-6
RET_HEADS = 4
RET_CHUNK = 128
RET_STEP = 4
ROPE_THETA = 10000.0
LRU_WIDTH = 512
LRU_C = 8.0
GDN_HEADS = 8
GDN_CHUNK = 64
GDN_STEP = 4
HEAD_DIM = 128
ODD_IN = 4112
ODD_PAD = 4224
ODD_SHARD = ODD_IN // N_DEV
ODD_SHARD_PAD = 640
ADAM_LR, ADAM_B1, ADAM_B2, ADAM_EPS, ADAM_WD, ADAM_STEP = 0.001, 0.9, 0.999, 1e-08, 0.01, 10
VMEM_LIMIT = 56 * 1024 * 1024

_NN = (((1,), (0,)), ((), ()))
_NT = (((1,), (1,)), ((), ()))
_TN = (((0,), (0,)), ((), ()))
MESH = pl.DeviceIdType.MESH


def _cparams(sem):
    return pltpu.CompilerParams(dimension_semantics=sem, vmem_limit_bytes=VMEM_LIMIT)


def _dot(a, b, dn):
    return lax.dot_general(a.astype(bf16), b.astype(bf16), dn, preferred_element_type=f32)


def _dot01(a01, b, dn):
    a = a01.astype(bf16)
    b0 = b.astype(bf16)
    r1 = b - b0.astype(f32)
    b1 = r1.astype(bf16)
    b2 = (r1 - b1.astype(f32)).astype(bf16)
    d = lambda q: lax.dot_general(a, q, dn, preferred_element_type=f32)
    return d(b0) + (d(b1) + d(b2))


def _sigmoid(x):
    return jax.nn.sigmoid(x)


def _silu(x):
    return x * _sigmoid(x)


def _dsilu(x):
    s = _sigmoid(x)
    return s * (1.0 + x * (1.0 - s))


def _softplus(x):
    return jnp.maximum(x, 0.0) + jnp.log1p(jnp.exp(-jnp.abs(x)))


_GELU_C = math.sqrt(2.0 / math.pi)


def _gelu(y):
    return 0.5 * y * (1.0 + jnp.tanh(_GELU_C * (y + 0.044715 * y * y * y)))


def _dgelu(y):
    t = jnp.tanh(_GELU_C * (y + 0.044715 * y * y * y))
    return 0.5 * (1.0 + t) + 0.5 * y * (1.0 - t * t) * _GELU_C * (1.0 + 3.0 * 0.044715 * y * y)


def _matmul(a, b, form, *, outs, name, epilogue=None, extras=(), tm=4096, tn=512, tk=1024, shard_cols=0, a_map=None):
    if form == "tn":
        K, M = a.shape
    else:
        M, K = a.shape
    per_step = 1
    if b.ndim == 3:
        assert form in ("nn", "nt"), name
        N = b.shape[1] if form == "nt" else N_DEV * b.shape[2]
        if form == "nn":
            tn = b.shape[2]
        else:
            per_step = max(1, tk // b.shape[2])
            tk = per_step * b.shape[2]
    else:
        N = b.shape[0] if form == "nt" else b.shape[1]
    ns = N // N_DEV
    if shard_cols:
        tn = ns * shard_cols
    tm, tn, tk = min(tm, M), min(tn, N), min(tk, K)
    assert M % tm == 0 and N % tn == 0 and K % tk == 0, (name, M, N, K, tm, tn, tk)
    nk = K // tk
    dn = {"nn": _NN, "nt": _NT, "tn": _TN}[form]
    if form == "tn":
        a_spec = pl.BlockSpec((tk, tm), lambda i, j, k: (k, i))
    else:
        a_spec = pl.BlockSpec((tm, tk), lambda i, j, k: (i, k))
    if b.ndim == 3:
        b_spec = (pl.BlockSpec((per_step, tn, tk // per_step), lambda i, j, k: (k, j, 0)) if form == "nt"
                  else pl.BlockSpec((None, tk, tn), lambda i, j, k: (j, k, 0)))
    elif form == "nt":
        b_spec = pl.BlockSpec((tn, tk), lambda i, j, k: (j, k))
    else:
        b_spec = pl.BlockSpec((tk, tn), lambda i, j, k: (k, j))
    e_spec = pl.BlockSpec((tm, tn), lambda i, j, k: (i, j))
    v_spec = pl.BlockSpec((1, tn), lambda i, j, k: (0, j))
    if shard_cols:
        o_spec = pl.BlockSpec((shard_cols, tm, ns), lambda i, j, k: (j, i, 0))
        o_shape = (N_DEV, M, ns)
    else:
        o_spec = e_spec
        o_shape = (M, N)
    n_ex = len(extras)
    sums = [isinstance(o, tuple) for o in outs]
    assert not any(sums) or tn == N, name

    def finish(acc, ex, o_refs, row_tile):
        vals = (acc,) if epilogue is None else epilogue(acc, *[e[...] for e in ex])
        for r, v, is_sum in zip(o_refs, vals, sums):
            if is_sum:
                @pl.when(row_tile == 0)
                def _(r=r, v=v):
                    r[...] = v.astype(r.dtype)

                @pl.when(row_tile > 0)
                def _(r=r, v=v):
                    r[...] += v.astype(r.dtype)
            elif shard_cols:
                for s in range(shard_cols):
                    r[s] = v[:, s * ns:(s + 1) * ns].astype(r.dtype)
            else:
                r[...] = v.astype(r.dtype)

    def prod(a_ref, b_ref):
        if b.ndim == 3 and form == "nt":
            w = tk // per_step
            return sum(_dot(a_ref[:, s * w:(s + 1) * w], b_ref[s], dn) for s in range(1, per_step)) + _dot(a_ref[:, 0:w], b_ref[0], dn)
        av = a_ref[...]
        return _dot(av if a_map is None else a_map(av), b_ref[...], dn)

    def body_one(*refs):
        finish(prod(*refs[:2]), refs[2:2 + n_ex], refs[2 + n_ex:], pl.program_id(0))

    def body_acc(*refs):
        a_ref, b_ref = refs[:2]
        acc = refs[-1]
        k = pl.program_id(2)
        row_tile = pl.program_id(0)

        @pl.when(k == 0)
        def _():
            acc[...] = prod(a_ref, b_ref)

        @pl.when((k > 0) & (k < nk - 1))
        def _():
            acc[...] += prod(a_ref, b_ref)

        @pl.when(k == nk - 1)
        def _():
            finish(acc[...] + prod(a_ref, b_ref), refs[2:2 + n_ex], refs[2 + n_ex:-1], row_tile)

    return pl.pallas_call(
        body_one if nk == 1 else body_acc, grid=(M // tm, N // tn, nk),
        in_specs=[a_spec, b_spec] + [v_spec if e.shape[0] == 1 else e_spec for e in extras],
        out_specs=[v_spec if s else o_spec for s in sums],
        out_shape=[jax.ShapeDtypeStruct((1, N), o[1]) if s else jax.ShapeDtypeStruct(o_shape, o) for o, s in zip(outs, sums)],
        scratch_shapes=[] if nk == 1 else [pltpu.VMEM((tm, tn), f32)],
        compiler_params=_cparams(("arbitrary" if any(sums) else "parallel", "parallel", "arbitrary")), name=name,
    )(a, b, *extras)


def _rms_fwd(x, w, name):
    T, D = x.shape
    tt = min(512, T)

    def body(x_ref, w_ref, h_ref):
        xv = x_ref[...]
        r = lax.rsqrt(jnp.mean(xv * xv, axis=1, keepdims=True) + EPS)
        h_ref[...] = (xv * r * w_ref[...]).astype(bf16)

    return pl.pallas_call(
        body, grid=(T // tt,),
        in_specs=[pl.BlockSpec((tt, D), lambda i: (i, 0)), pl.BlockSpec((1, D), lambda i: (0, 0))],
        out_specs=pl.BlockSpec((tt, D), lambda i: (i, 0)),
        out_shape=jax.ShapeDtypeStruct((T, D), bf16),
        compiler_params=_cparams(("parallel",)), name=name,
    )(x, w)


def _residual_rms_ep(acc, res, w):
    x = res + acc
    r = lax.rsqrt(jnp.mean(x * x, axis=1, keepdims=True) + EPS)
    return x, x * r * w


_RMS_BWD_OUTS = [f32, bf16, ("sum", f32)]


def _rms_bwd_ep(dh, x, dres, w):
    r = lax.rsqrt(jnp.mean(x * x, axis=1, keepdims=True) + EPS)
    xn = x * r
    dhw = dh * w
    dx = dres + r * (dhw - xn * jnp.mean(dhw * xn, axis=1, keepdims=True))
    return dx, dx, jnp.sum(dh * xn, axis=0, keepdims=True)


_LOSS_OUTS = [("sum", f32), f32, bf16, ("sum", f32)]


def _loss_ep(acc, res, w, tgt):
    x = res + acc
    D = x.shape[1]
    r = lax.rsqrt(jnp.mean(x * x, axis=1, keepdims=True) + EPS)
    xn = x * r
    e = xn * w - tgt
    loss = 0.5 * jnp.sum(jnp.mean(e * e, axis=1, keepdims=True), axis=0, keepdims=True)
    dy = e * (1.0 / D)
    dyw = dy * w
    dx = r * (dyw - xn * jnp.mean(dyw * xn, axis=1, keepdims=True))
    return jnp.broadcast_to(loss, (1, D)), dx, dx, jnp.sum(dy * xn, axis=0, keepdims=True)


def _ret_tables():
    H, C = RET_HEADS, RET_CHUNK
    lg = np.log1p(-np.exp2(-5.0 - np.arange(H, dtype=np.float32))).astype(np.float32)
    idx = np.arange(C, dtype=np.float32)
    diff = idx[:, None] - idx[None, :]
    causal = diff >= 0
    dm = np.where(causal[None], np.exp(lg[:, None, None] * np.where(causal, diff, 0.0)[None]), 0.0)
    qd = np.exp(lg[:, None] * (idx[None, :] + 1.0))
    kd = np.exp(lg[:, None] * (C - 1.0 - idx[None, :]))
    cg = np.exp(lg * C)
    tab = np.zeros((H, 4, C, HEAD_DIM), np.float32)
    tab[:, 0] = dm
    tab[:, 1] = qd[:, :, None]
    tab[:, 2] = kd[:, :, None]
    tab[:, 3] = cg[:, None, None]
    return jnp.asarray(tab)


def _rope_tables(T):
    half = HEAD_DIM // 2
    inv = ROPE_THETA ** (-jnp.arange(half, dtype=f32) / half)
    ang = jnp.arange(T, dtype=jnp.int32).astype(f32)[:, None] * inv[None, :]
    c, s = jnp.cos(ang), jnp.sin(ang)
    return jnp.concatenate([c, c], axis=1), jnp.concatenate([-s, s], axis=1)


def _rope(x, cos, sin):
    return x * cos + pltpu.roll(x, HEAD_DIM // 2, 1) * sin


def _unrope(y, cos, sin):
    return y * cos + pltpu.roll(y * sin, HEAD_DIM // 2, 1)


def _stack_heads(ref, H, f=None):
    parts = [ref[:, h * HEAD_DIM:(h + 1) * HEAD_DIM] for h in range(H)]
    return jnp.stack(parts if f is None else [f(a) for a in parts])


def _ret_fwd(p, cos, sin, tab, name):
    T = p.shape[0]
    C, H = RET_CHUNK, RET_HEADS
    N = T // C
    K = min(RET_STEP, N)
    NS = N // K
    scale = HEAD_DIM ** -0.5

    def body(q_ref, k_ref, v_ref, g_ref, c_ref, s_ref, t_ref, y_ref, o_ref, sp_ref, st):
        @pl.when(pl.program_id(0) == 0)
        def _():
            st[...] = jnp.zeros_like(st)

        dm, qd, kd, cg = t_ref[:, 0], t_ref[:, 1], t_ref[:, 2], t_ref[:, 3]
        S = st[...]
        for c in range(K):
            rows = pl.ds(c * C, C)
            cos_, sin_ = c_ref[rows, :], s_ref[rows, :]
            rot = lambda a: _rope(a, cos_, sin_)
            q = _stack_heads(q_ref.at[rows, :], H, rot)
            k = _stack_heads(k_ref.at[rows, :], H, rot) * scale
            v = _stack_heads(v_ref.at[rows, :], H)
            P = _dot(q, k, _NT3) * dm
            o = _dot(P, v, _NN3) + _dot(q * qd, S, _NN3)
            sp_ref[c] = S
            S = cg * S + _dot(k * kd, v, _TN3)
            r = lax.rsqrt(jnp.mean(o * o, axis=2, keepdims=True) + EPS)
            y = o * r * _silu(_stack_heads(g_ref.at[rows, :], H))
            for h in range(H):
                o_ref[rows, h * HEAD_DIM:(h + 1) * HEAD_DIM] = o[h]
                y_ref[rows, h * HEAD_DIM:(h + 1) * HEAD_DIM] = y[h].astype(bf16)
        st[...] = S

    wide = lambda blk: pl.BlockSpec((K * C, H * HEAD_DIM), lambda n: (n, blk))
    tbl = pl.BlockSpec((K * C, HEAD_DIM), lambda n: (n, 0))
    return pl.pallas_call(
        body, grid=(NS,),
        in_specs=[wide(0), wide(1), wide(2), wide(3), tbl, tbl,
                  pl.BlockSpec((H, 4, C, HEAD_DIM), lambda n: (0, 0, 0, 0))],
        out_specs=[wide(0), wide(0), pl.BlockSpec((K, H, HEAD_DIM, HEAD_DIM), lambda n: (n, 0, 0, 0))],
        out_shape=[jax.ShapeDtypeStruct((T, D_MODEL), bf16), jax.ShapeDtypeStruct((T, H * HEAD_DIM), f32),
                   jax.ShapeDtypeStruct((N, H, HEAD_DIM, HEAD_DIM), f32)],
        scratch_shapes=[pltpu.VMEM((H, HEAD_DIM, HEAD_DIM), f32)],
        compiler_params=_cparams(("arbitrary",)), name=name,
    )(p, p, p, p, cos, sin, tab)


def _ret_bwd(p, cos, sin, tab, o_raw, sprev, dmix, name):
    T = p.shape[0]
    C, H = RET_CHUNK, RET_HEADS
    N = T // C
    K = min(RET_STEP, N)
    NS = N // K
    scale = HEAD_DIM ** -0.5
    W = H * HEAD_DIM

    def body(q_ref, k_ref, v_ref, g_ref, c_ref, s_ref, t_ref, o_ref, sp_ref, dy_ref, d_ref, dst):
        @pl.when(pl.program_id(0) == 0)
        def _():
            dst[...] = jnp.zeros_like(dst)

        dm, qd, kd, cg = t_ref[:, 0], t_ref[:, 1], t_ref[:, 2], t_ref[:, 3]
        dS1 = dst[...]
        for c in reversed(range(K)):
            rows = pl.ds(c * C, C)
            cos_, sin_ = c_ref[rows, :], s_ref[rows, :]
            rot = lambda a: _rope(a, cos_, sin_)
            q = _stack_heads(q_ref.at[rows, :], H, rot)
            k = _stack_heads(k_ref.at[rows, :], H, rot) * scale
            v = _stack_heads(v_ref.at[rows, :], H)
            g = _stack_heads(g_ref.at[rows, :], H)
            S = sp_ref[c]
            o = _stack_heads(o_ref.at[rows, :], H)
            dy = _stack_heads(dy_ref.at[rows, :], H)
            r = lax.rsqrt(jnp.mean(o * o, axis=2, keepdims=True) + EPS)
            nrm = o * r
            dn = dy * _silu(g)
            dg = dy * nrm * _dsilu(g)
            do = r * (dn - nrm * jnp.mean(dn * nrm, axis=2, keepdims=True))
            P = _dot(q, k, _NT3) * dm
            dP = _dot(do, v, _NT3) * dm
            dq = _dot(dP, k, _NN3) + _dot(do, S, _NT3) * qd
            dk = (_dot(dP, q, _TN3) + _dot(v, dS1, _NT3) * kd) * scale
            dv = _dot(P, do, _TN3) + _dot(k * kd, dS1, _NN3)
            dS1 = cg * dS1 + _dot(q * qd, do, _TN3)
            for h in range(H):
                d_ref[rows, h * HEAD_DIM:(h + 1) * HEAD_DIM] = _unrope(dq[h], cos_, sin_).astype(bf16)
                d_ref[rows, W + h * HEAD_DIM:W + (h + 1) * HEAD_DIM] = _unrope(dk[h], cos_, sin_).astype(bf16)
                d_ref[rows, 2 * W + h * HEAD_DIM:2 * W + (h + 1) * HEAD_DIM] = dv[h].astype(bf16)
                d_ref[rows, 3 * W + h * HEAD_DIM:3 * W + (h + 1) * HEAD_DIM] = dg[h].astype(bf16)
        dst[...] = dS1

    rev = lambda blk: pl.BlockSpec((K * C, W), lambda n: (NS - 1 - n, blk))
    tbl = pl.BlockSpec((K * C, HEAD_DIM), lambda n: (NS - 1 - n, 0))
    return pl.pallas_call(
        body, grid=(NS,),
        in_specs=[rev(0), rev(1), rev(2), rev(3), tbl, tbl,
                  pl.BlockSpec((H, 4, C, HEAD_DIM), lambda n: (0, 0, 0, 0)), rev(0),
                  pl.BlockSpec((K, H, HEAD_DIM, HEAD_DIM), lambda n: (NS - 1 - n, 0, 0, 0)), rev(0)],
        out_specs=pl.BlockSpec((K * C, 4 * W), lambda n: (NS - 1 - n, 0)),
        out_shape=jax.ShapeDtypeStruct((T, 6 * W), bf16),
        scratch_shapes=[pltpu.VMEM((H, HEAD_DIM, HEAD_DIM), f32)],
        compiler_params=_cparams(("arbitrary",)), name=name,
    )(p, p, p, p, cos, sin, tab, o_raw, sprev, dmix)


CONV_K = 4
CONV_W = 512
PAD = 8
SUB_R = 64


def _conv_fwd(x, col_off, w, b, act, name):
    T = x.shape[0]
    C = w.shape[1]
    G = C // CONV_W
    tt = min(512, T)
    NT = T // tt
    has_b = b is not None

    def body(*refs):
        if has_b:
            x_ref, w_ref, b_ref, y_ref, pad = refs
        else:
            x_ref, w_ref, y_ref, pad = refs
        t = pl.program_id(1)

        @pl.when(t == 0)
        def _():
            pad[pl.ds(0, PAD), :] = jnp.zeros((PAD, CONV_W), f32)

        pad[pl.ds(PAD, tt), :] = x_ref[...]
        for g in range(CONV_W // 128):
            ls = slice(g * 128, (g + 1) * 128)
            wv = w_ref[:, ls]
            for c in range(tt // SUB_R):
                r0 = c * SUB_R
                y = wv[0:1, :] * pad[pl.ds(PAD - 3 + r0, SUB_R), ls]
                for kk in range(1, CONV_K):
                    y = y + wv[kk:kk + 1, :] * pad[pl.ds(PAD - 3 + kk + r0, SUB_R), ls]
                if has_b:
                    y = y + b_ref[:, ls]
                y_ref[pl.ds(r0, SUB_R), ls] = _silu(y) if act else y
        tail = pad[pl.ds(tt, PAD), :]
        pad[pl.ds(0, PAD), :] = tail

    in_specs = [pl.BlockSpec((tt, CONV_W), lambda g, t: (t, col_off + g)),
                pl.BlockSpec((CONV_K, CONV_W), lambda g, t: (0, g))]
    args = [x, w]
    if has_b:
        in_specs.append(pl.BlockSpec((1, CONV_W), lambda g, t: (0, g)))
        args.append(b)
    return pl.pallas_call(
        body, grid=(G, NT), in_specs=in_specs,
        out_specs=pl.BlockSpec((tt, CONV_W), lambda g, t: (t, g)),
        out_shape=jax.ShapeDtypeStruct((T, C), f32),
        scratch_shapes=[pltpu.VMEM((tt + PAD, CONV_W), f32)],
        compiler_params=_cparams(("parallel", "arbitrary")), name=name,
    )(*args)


def _conv_bwd(x, col_off, w, b, act, dout, dp, name):
    T = x.shape[0]
    C = w.shape[1]
    G = C // CONV_W
    tt = min(512, T)
    NT = T // tt
    has_b = b is not None

    def body(*refs):
        if has_b:
            x_ref, xp_ref, w_ref, b_ref, d_ref, dp_in, dx_ref, dw_ref, db_ref, pad, dpad = refs
        else:
            x_ref, xp_ref, w_ref, d_ref, dp_in, dx_ref, dw_ref, db_ref, pad, dpad = refs
        t = pl.program_id(1)
        first_tile = t == NT - 1

        @pl.when(t == 0)
        def _():
            dpad[pl.ds(tt, PAD), :] = jnp.zeros((PAD, CONV_W), f32)
            dw_ref[...] = jnp.zeros_like(dw_ref)
            db_ref[...] = jnp.zeros_like(db_ref)

        pad[pl.ds(0, PAD), :] = jnp.where(first_tile, 0.0, xp_ref[...])
        pad[pl.ds(PAD, tt), :] = x_ref[...]
        fold = lambda v: v.reshape(SUB_R // 8, 8, 128).sum(axis=0)
        for g in range(CONV_W // 128):
            ls = slice(g * 128, (g + 1) * 128)
            wv = w_ref[:, ls]
            acc = [jnp.zeros((8, 128), f32) for _ in range(CONV_K + 1)]
            for c in reversed(range(tt // SUB_R)):
                r0 = c * SUB_R
                xs = [pad[pl.ds(PAD - 3 + kk + r0, SUB_R), ls] for kk in range(CONV_K)]
                dy = d_ref[pl.ds(r0, SUB_R), ls]
                if act:
                    y = wv[0:1, :] * xs[0]
                    for kk in range(1, CONV_K):
                        y = y + wv[kk:kk + 1, :] * xs[kk]
                    if has_b:
                        y = y + b_ref[:, ls]
                    dy = dy * _dsilu(y)
                dpad[pl.ds(r0, SUB_R), ls] = dy
                dx = wv[3:4, :] * dy
                for j in range(1, CONV_K):
                    dx = dx + wv[3 - j:4 - j, :] * dpad[pl.ds(r0 + j, SUB_R), ls]
                dx_ref[pl.ds(r0, SUB_R), ls] = dx.astype(bf16)
                for kk in range(CONV_K):
                    acc[kk] = acc[kk] + fold(dy * xs[kk])
                acc[CONV_K] = acc[CONV_K] + fold(dy)
            for kk in range(CONV_K):
                dw_ref[kk:kk + 1, ls] += jnp.sum(acc[kk], axis=0, keepdims=True)
            db_ref[:, ls] += jnp.sum(acc[CONV_K], axis=0, keepdims=True)
        head = dpad[pl.ds(0, PAD), :]
        dpad[pl.ds(tt, PAD), :] = head

    rows8 = tt // PAD
    in_specs = [pl.BlockSpec((tt, CONV_W), lambda g, t: (NT - 1 - t, col_off + g)),
                pl.BlockSpec((PAD, CONV_W), lambda g, t: (jnp.maximum((NT - 1 - t) * rows8 - 1, 0), col_off + g)),
                pl.BlockSpec((CONV_K, CONV_W), lambda g, t: (0, g))]
    args = [x, x, w]
    if has_b:
        in_specs.append(pl.BlockSpec((1, CONV_W), lambda g, t: (0, g)))
        args.append(b)
    in_specs += [pl.BlockSpec((tt, CONV_W), lambda g, t: (NT - 1 - t, g)), pl.BlockSpec(memory_space=pl.ANY)]
    args += [dout, dp]
    return pl.pallas_call(
        body, grid=(G, NT), in_specs=in_specs,
        out_specs=[pl.BlockSpec((tt, CONV_W), lambda g, t: (NT - 1 - t, col_off + g)),
                   pl.BlockSpec((CONV_K, CONV_W), lambda g, t: (0, g)),
                   pl.BlockSpec((1, CONV_W), lambda g, t: (0, g))],
        out_shape=[jax.ShapeDtypeStruct(dp.shape, dp.dtype), jax.ShapeDtypeStruct((CONV_K, C), f32),
                   jax.ShapeDtypeStruct((1, C), f32)],
        input_output_aliases={len(args) - 1: 0},
        scratch_shapes=[pltpu.VMEM((tt + PAD, CONV_W), f32), pltpu.VMEM((tt + PAD, CONV_W), f32)],
        compiler_params=_cparams(("parallel", "arbitrary")), name=name,
    )(*args)


def _lru_gates(xc, wr, wi, br, bi, lam):
    r = _sigmoid(_dot(xc, wr, _NN) + br)
    i = _sigmoid(_dot(xc, wi, _NN) + bi)
    sp = _softplus(-lam)
    a = jnp.exp(-LRU_C * r * sp)
    mult = jnp.sqrt(1.0 - a * a)
    return r, i, sp, a, mult


def _lru_fwd(xc, p, y_off, wr, wi, br, bi, lam, mix, name):
    T = xc.shape[0]
    G = LRU_WIDTH // 128
    tt = min(512, T)
    NT = T // tt

    def body(x_ref, y_ref, wr_ref, wi_ref, br_ref, bi_ref, l_ref, mix_in, o_ref, h_ref, hc):
        t = pl.program_id(1)

        @pl.when(t == 0)
        def _():
            hc[...] = jnp.zeros_like(hc)

        x = x_ref[...]
        r, i, sp, a, mult = _lru_gates(x, wr_ref[...], wi_ref[...], br_ref[...], bi_ref[...], l_ref[...])
        row = lax.broadcasted_iota(jnp.int32, (tt, 128), 0)
        mult = jnp.where((row == 0) & (t == 0), 1.0, mult)
        U = x * i * mult
        A = a
        d = 1
        while d < tt:
            keep = row >= d
            Ush = jnp.where(keep, pltpu.roll(U, d, 0), 0.0)
            Ash = jnp.where(keep, pltpu.roll(A, d, 0), 1.0)
            U = A * Ush + U
            A = A * Ash
            d *= 2
        h = U + A * hc[0:1, :]
        h_ref[...] = h
        hc[...] = jnp.broadcast_to(h[tt - 1:tt, :], hc.shape)
        o_ref[...] = (h * _gelu(y_ref[...])).astype(bf16)

    tile = pl.BlockSpec((tt, 128), lambda g, t: (t, g))
    vec = pl.BlockSpec((1, 128), lambda g, t: (0, g))
    wsp = pl.BlockSpec((128, 128), lambda g, t: (g, g))
    return pl.pallas_call(
        body, grid=(G, NT),
        in_specs=[tile, pl.BlockSpec((tt, 128), lambda g, t: (t, y_off + g)), wsp, wsp, vec, vec, vec,
                  pl.BlockSpec(memory_space=pl.ANY)],
        out_specs=[pl.BlockSpec((tt, 128), lambda g, t: (t, G + g)), tile],
        out_shape=[jax.ShapeDtypeStruct(mix.shape, mix.dtype), jax.ShapeDtypeStruct((T, LRU_WIDTH), f32)],
        input_output_aliases={7: 0},
        scratch_shapes=[pltpu.VMEM((8, 128), f32)],
        compiler_params=_cparams(("parallel", "arbitrary")), name=name,
    )(xc, p, wr, wi, br, bi, lam, mix)


def _lru_bwd(xc, p, y_off, wr, wi, br, bi, lam, hs, dmix, d_off, dp, name):
    T = xc.shape[0]
    G = LRU_WIDTH // 128
    tt = min(512, T)
    NT = T // tt

    def body(x_ref, y_ref, wr_ref, wi_ref, br_ref, bi_ref, l_ref, h_ref, hp_ref, do_ref, dp_in,
             dx_ref, dy_ref, dwr_ref, dwi_ref, dbr_ref, dbi_ref, dl_ref, lc, an):
        t = pl.program_id(1)
        first_tile = t == NT - 1

        @pl.when(t == 0)
        def _():
            lc[...] = jnp.zeros_like(lc)
            an[...] = jnp.zeros_like(an)
            dwr_ref[...] = jnp.zeros_like(dwr_ref)
            dwi_ref[...] = jnp.zeros_like(dwi_ref)
            dbr_ref[...] = jnp.zeros_like(dbr_ref)
            dbi_ref[...] = jnp.zeros_like(dbi_ref)
            dl_ref[...] = jnp.zeros_like(dl_ref)

        x = x_ref[...]
        y = y_ref[...]
        wr, wi, lam_ = wr_ref[...], wi_ref[...], l_ref[...]
        r, i, sp, a, mult_raw = _lru_gates(x, wr, wi, br_ref[...], bi_ref[...], lam_)
        row = lax.broadcasted_iota(jnp.int32, (tt, 128), 0)
        t0 = (row == 0) & first_tile
        mult = jnp.where(t0, 1.0, mult_raw)
        h = h_ref[...]
        do = do_ref[...]
        dh = do * _gelu(y)
        dy_ref[...] = (do * h * _dgelu(y)).astype(bf16)
        B = jnp.where(row == tt - 1, an[0:1, :], pltpu.roll(a, tt - 1, 0))
        L = dh
        d = 1
        while d < tt:
            keep = row < tt - d
            Lsh = jnp.where(keep, pltpu.roll(L, tt - d, 0), 0.0)
            Bsh = jnp.where(keep, pltpu.roll(B, tt - d, 0), 1.0)
            L = L + B * Lsh
            B = B * Bsh
            d *= 2
        L = L + B * lc[0:1, :]
        lc[...] = jnp.broadcast_to(L[0:1, :], lc.shape)
        an[...] = jnp.broadcast_to(a[0:1, :], an.shape)
        hprev = jnp.where(first_tile, 0.0, hp_ref[...])[PAD - 1:PAD, :]
        hm1 = jnp.where(row == 0, hprev, pltpu.roll(h, 1, 0))
        da = L * hm1
        dxc = L * i * mult
        di = L * x * mult
        dmult = jnp.where(t0, 0.0, L * x * i)
        da = da - jnp.where(t0, 0.0, dmult * a / mult_raw)
        dlog_a = da * a
        dr = dlog_a * (-LRU_C) * sp
        dsp = jnp.sum(dlog_a * (-LRU_C) * r, axis=0, keepdims=True)
        dpr = dr * r * (1.0 - r)
        dpi = di * i * (1.0 - i)
        dx_ref[...] = dxc + _dot(dpr, wr, _NT) + _dot(dpi, wi, _NT)
        dwr_ref[0] += _dot(x, dpr, _TN)
        dwi_ref[0] += _dot(x, dpi, _TN)
        dbr_ref[...] += jnp.sum(dpr, axis=0, keepdims=True)
        dbi_ref[...] += jnp.sum(dpi, axis=0, keepdims=True)
        dl_ref[...] += dsp * (-_sigmoid(-lam_))

    rows8 = tt // PAD
    tile = pl.BlockSpec((tt, 128), lambda g, t: (NT - 1 - t, g))
    vec = pl.BlockSpec((1, 128), lambda g, t: (0, g))
    wsp = pl.BlockSpec((128, 128), lambda g, t: (g, g))
    wout = pl.BlockSpec((1, 128, 128), lambda g, t: (g, 0, 0))
    return pl.pallas_call(
        body, grid=(G, NT),
        in_specs=[tile, pl.BlockSpec((tt, 128), lambda g, t: (NT - 1 - t, y_off + g)), wsp, wsp, vec, vec, vec, tile,
                  pl.BlockSpec((PAD, 128), lambda g, t: (jnp.maximum((NT - 1 - t) * rows8 - 1, 0), g)),
                  pl.BlockSpec((tt, 128), lambda g, t: (NT - 1 - t, d_off + g)), pl.BlockSpec(memory_space=pl.ANY)],
        out_specs=[tile, pl.BlockSpec((tt, 128), lambda g, t: (NT - 1 - t, y_off + g)), wout, wout, vec, vec, vec],
        out_shape=[jax.ShapeDtypeStruct((T, LRU_WIDTH), f32), jax.ShapeDtypeStruct(dp.shape, dp.dtype),
                   jax.ShapeDtypeStruct((G, 128, 128), f32), jax.ShapeDtypeStruct((G, 128, 128), f32),
                   jax.ShapeDtypeStruct((1, LRU_WIDTH), f32), jax.ShapeDtypeStruct((1, LRU_WIDTH), f32),
                   jax.ShapeDtypeStruct((1, LRU_WIDTH), f32)],
        input_output_aliases={10: 1},
        scratch_shapes=[pltpu.VMEM((8, 128), f32), pltpu.VMEM((8, 128), f32)],
        compiler_params=_cparams(("parallel", "arbitrary")), name=name,
    )(xc, p, wr, wi, br, bi, lam, hs, hs, dmix, dp)


_NN3 = (((2,), (1,)), ((0,), (0,)))
_NT3 = (((2,), (2,)), ((0,), (0,)))
_TN3 = (((1,), (1,)), ((0,), (0,)))


def _pairs(ref, K):
    C = GDN_CHUNK
    return jnp.stack([ref[c * C:(c + 1) * C, h * HEAD_DIM:(h + 1) * HEAD_DIM] for c in range(K) for h in range(GDN_HEADS)])


def _put_pairs(ref, val, K, col=0):
    C, H = GDN_CHUNK, GDN_HEADS
    for c in range(K):
        for h in range(H):
            ref[c * C:(c + 1) * C, col + h * HEAD_DIM:col + (h + 1) * HEAD_DIM] = val[c * H + h].astype(ref.dtype)


def _rowsum(x):
    H, C, L = x.shape
    return _dot(x.reshape(H * C, L), jnp.ones((L, HEAD_DIM), f32), _NN).reshape(H, C, HEAD_DIM)


def _gdn_pre(qr, kr, v, ba, alog, dtb):
    C, H = GDN_CHUNK, GDN_HEADS
    B = qr.shape[0]
    K = B // H
    lane = lax.broadcasted_iota(jnp.int32, (C, 128), 1)
    lane3 = lax.broadcasted_iota(jnp.int32, (B, C, 128), 2)
    ri = lax.broadcasted_iota(jnp.int32, (C, C), 0)
    ci = lax.broadcasted_iota(jnp.int32, (C, C), 1)
    rowc = lax.broadcasted_iota(jnp.int32, (C, 1), 0)
    col = lambda m, j: jnp.sum(jnp.where(lane == j, m, 0.0), axis=1, keepdims=True)
    ea = jnp.exp(alog)
    tri = (ri >= ci).astype(f32)
    g_all, beta_cols, G_cols = [], [], []
    for c in range(K):
        ba_c = ba[c * C:(c + 1) * C]
        g_c = -ea * _softplus(ba_c + dtb)
        G_c = _dot01(tri, g_c, _NN)
        s_c = _sigmoid(ba_c)
        g_all.append(g_c)
        beta_cols += [col(s_c, h) for h in range(H)]
        G_cols += [col(G_c, H + h) for h in range(H)]
    wide = lambda c: jnp.broadcast_to(c, (B, C, 128))
    beta = wide(jnp.stack(beta_cols))
    Gc = jnp.stack(G_cols)
    rq = lax.rsqrt(_rowsum(qr * qr) + EPS)
    rk = lax.rsqrt(_rowsum(kr * kr) + EPS)
    qh, kn = qr * rq, kr * rk
    qn = qh * (HEAD_DIM ** -0.5)
    Grow = _dot01(jnp.ones((B, C, 128), f32), jnp.where(lane3 == 0, Gc, 0.0), _NT3)
    incl = ri >= ci
    Di = jnp.where(incl, jnp.exp(jnp.where(incl, Gc - Grow, 0.0)), 0.0)
    Ds = jnp.where(ri > ci, Di, 0.0)
    Gl = jnp.sum(jnp.where(rowc == C - 1, Gc, 0.0), axis=1, keepdims=True)
    eG = wide(jnp.exp(Gc))
    eGl = wide(jnp.exp(Gl - Gc))
    cd = jnp.exp(Gl)
    kb = kn * beta
    vb = v * beta
    Lm = _dot(kb, kn, _NT3) * Ds
    kbg = kb * eG
    QK = _dot(qn, kn, _NT3) * Di
    qg = qn * eG
    kg = kn * eGl
    return dict(beta=beta, g_all=g_all, rq=rq, rk=rk, qh=qh, kn=kn, qn=qn, Di=Di, Ds=Ds, eG=eG, eGl=eGl, cd=cd,
                kb=kb, vb=vb, Lm=Lm, kbg=kbg, QK=QK, qg=qg, kg=kg, lane=lane, ri=ri, ci=ci, rowc=rowc, ea=ea)


def _unit_lower_inverse(Lm):
    C = Lm.shape[-1]
    ri = lax.broadcasted_iota(jnp.int32, (C, C), 0)
    ci = lax.broadcasted_iota(jnp.int32, (C, C), 1)
    same = lambda s: (ri // s) == (ci // s)
    Xd = jnp.where(same(8), -Lm, 0.0)
    Tinv = (ri == ci).astype(f32) + Xd
    Pw = Xd
    for _ in range(2):
        Pw = _dot(Pw, Pw, _NN3)
        Tinv = Tinv + _dot(Tinv, Pw, _NN3)
    for s in (8, 16, 32):
        off = jnp.where(same(2 * s) & jnp.logical_not(same(s)), Lm, 0.0)
        Tinv = Tinv - _dot(_dot(Tinv, off, _NN3), Tinv, _NN3)
    return Tinv


def _gdn_specs(T, rev):
    C = GDN_CHUNK
    H = GDN_HEADS
    K = min(GDN_STEP, T // C)
    NS = T // (C * K)
    nn = (lambda n: NS - 1 - n) if rev else (lambda n: n)
    wide = lambda blk: pl.BlockSpec((K * C, H * HEAD_DIM), lambda n: (nn(n), blk))
    one = lambda off: pl.BlockSpec((K * C, HEAD_DIM), lambda n: (nn(n), off))
    vec = pl.BlockSpec((1, 128), lambda n: (0, 0))
    st = lambda rows: pl.BlockSpec((K, H, rows, rows), lambda n: (nn(n), 0, 0, 0))
    return K, NS, wide, one, vec, st


def _gdn_fwd(qkv, p, alog, dtb, nw, name):
    T = qkv.shape[0]
    C, H = GDN_CHUNK, GDN_HEADS
    N = T // C
    K, NS, wide, one, vec, st_spec = _gdn_specs(T, False)

    def body(q_ref, k_ref, v_ref, z_ref, ba_ref, al_ref, dt_ref, nw_ref, y_ref, sp_ref, ti_ref, vn_ref, o_ref, st):
        @pl.when(pl.program_id(0) == 0)
        def _():
            st[...] = jnp.zeros_like(st)

        f = _gdn_pre(_pairs(q_ref, K), _pairs(k_ref, K), _pairs(v_ref, K), ba_ref[...], al_ref[...], dt_ref[...])
        Tinv = _unit_lower_inverse(f["Lm"])
        ti_ref[...] = Tinv.reshape(K, H, C, C).astype(bf16)
        w = _dot(Tinv, f["kbg"], _NN3)
        u = _dot(Tinv, f["vb"], _NN3)
        S = st[...]
        vns, os_ = [], []
        for c in range(K):
            sl = slice(c * H, (c + 1) * H)
            sp_ref[c] = S
            vn_c = u[sl] - _dot(w[sl], S, _NN3)
            os_.append(_dot(f["qg"][sl], S, _NN3) + _dot(f["QK"][sl], vn_c, _NN3))
            S = S * f["cd"][sl] + _dot(f["kg"][sl], vn_c, _TN3)
            vns.append(vn_c)
        st[...] = S
        vn, o = jnp.concatenate(vns), jnp.concatenate(os_)
        r = lax.rsqrt(_rowsum(o * o) * (1.0 / HEAD_DIM) + EPS)
        _put_pairs(y_ref, o * r * nw_ref[...] * _silu(_pairs(z_ref, K)), K)
        _put_pairs(vn_ref, vn, K)
        _put_pairs(o_ref, o, K)

    wide_f32 = jax.ShapeDtypeStruct((T, H * HEAD_DIM), f32)
    return pl.pallas_call(
        body, grid=(NS,),
        in_specs=[wide(0), wide(1), wide(2), wide(3), one(4 * H), vec, vec, vec],
        out_specs=[wide(0), st_spec(HEAD_DIM), st_spec(C), wide(0), wide(0)],
        out_shape=[jax.ShapeDtypeStruct((T, H * HEAD_DIM), bf16), jax.ShapeDtypeStruct((N, H, HEAD_DIM, HEAD_DIM), f32),
                   jax.ShapeDtypeStruct((N, H, C, C), bf16), jax.ShapeDtypeStruct((T, H * HEAD_DIM), bf16), wide_f32],
        scratch_shapes=[pltpu.VMEM((H, HEAD_DIM, HEAD_DIM), f32)],
        compiler_params=_cparams(("arbitrary",)), name=name,
    )(qkv, qkv, qkv, p, p, alog, dtb, nw)


def _gdn_bwd(qkv, p, alog, dtb, nw, sprev, tinv, vn_all, o_all, dy_all, name):
    T = qkv.shape[0]
    C, H = GDN_CHUNK, GDN_HEADS
    N = T // C
    K, NS, wide, one, vec, st_spec = _gdn_specs(T, True)
    rs = lambda m: jnp.sum(m, axis=2, keepdims=True)

    def body(q_ref, k_ref, v_ref, z_ref, ba_ref, al_ref, dt_ref, nw_ref, sp_ref, ti_ref, vn_ref, o_ref, dy_ref,
             dqkv_ref, dz_ref, dba_ref, dal_ref, ddt_ref, dnw_ref, dst):
        @pl.when(pl.program_id(0) == 0)
        def _():
            dst[...] = jnp.zeros_like(dst)
            dal_ref[...] = jnp.zeros_like(dal_ref)
            ddt_ref[...] = jnp.zeros_like(ddt_ref)
            dnw_ref[...] = jnp.zeros_like(dnw_ref)

        ba, dtb_, nwv = ba_ref[...], dt_ref[...], nw_ref[...]
        v = _pairs(v_ref, K)
        f = _gdn_pre(_pairs(q_ref, K), _pairs(k_ref, K), v, ba, al_ref[...], dtb_)
        beta, kn, qn, kb, vb, kbg = f["beta"], f["kn"], f["qn"], f["kb"], f["vb"], f["kbg"]
        eG, eGl, cd, Di, Ds, QK, qg, kg = f["eG"], f["eGl"], f["cd"], f["Di"], f["Ds"], f["QK"], f["qg"], f["kg"]
        lane, ri, ci, rowc = f["lane"], f["ri"], f["ci"], f["rowc"]
        Tinv = ti_ref[...].reshape(K * H, C, C)
        S = sp_ref[...].reshape(K * H, HEAD_DIM, HEAD_DIM)
        w_ = _dot(Tinv, kbg, _NN3)
        vn, o = _pairs(vn_ref, K), _pairs(o_ref, K)
        z, dy = _pairs(z_ref, K), _pairs(dy_ref, K)
        r = lax.rsqrt(_rowsum(o * o) * (1.0 / HEAD_DIM) + EPS)
        nrm = o * r
        sz = _silu(z)
        dn = dy * nwv * sz
        _put_pairs(dz_ref, dy * nrm * nwv * _dsilu(z), K)
        dnw_ref[...] += jnp.sum(jnp.sum(dy * nrm * sz, axis=0), axis=0, keepdims=True)
        do = r * (dn - nrm * (_rowsum(dn * nrm) * (1.0 / HEAD_DIM)))
        dvn_do = _dot(QK, do, _TN3)
        dS_do = _dot(qg, do, _TN3)
        dqg = _dot(do, S, _NT3)
        dQK = _dot(do, vn, _NT3)
        dS = dst[...]
        dS1s, dvns = [None] * K, [None] * K
        for c in reversed(range(K)):
            sl = slice(c * H, (c + 1) * H)
            dS1s[c] = dS
            dvns[c] = _dot(kg[sl], dS, _NN3) + dvn_do[sl]
            dS = cd[sl] * dS + dS_do[sl] - _dot(w_[sl], dvns[c], _TN3)
        dst[...] = dS
        dS1, dvn = jnp.concatenate(dS1s), jnp.concatenate(dvns)
        dcd = jnp.sum(jnp.sum(S * dS1, axis=2, keepdims=True), axis=1, keepdims=True)
        dkg = _dot(vn, dS1, _NT3)
        dw = -_dot(dvn, S, _NT3)
        dqn = dqg * eG
        dkn = dkg * eGl
        deGl = rs(dkg * kn)
        dQKr = dQK * Di
        E = dQK * QK
        dqn = dqn + _dot(dQKr, kn, _NN3)
        dkn = dkn + _dot(dQKr, qn, _TN3)
        dT = _dot(dvn, vb, _NT3) + _dot(dw, kbg, _NT3)
        dvb = _dot(Tinv, dvn, _TN3)
        dkbg = _dot(Tinv, dw, _TN3)
        dkb = dkbg * eG
        deG = rs(dqg * qn + dkbg * kb)
        dL = -_dot(_dot(Tinv, dT, _TN3), Tinv, _NT3)
        dKK = dL * Ds
        E = E + dL * f["Lm"]
        dkb = dkb + _dot(dKK, kn, _NN3)
        dkn = dkn + _dot(dKK, kb, _TN3) + dkb * beta
        dbeta = rs(dkb * kn + dvb * v)
        _put_pairs(dqkv_ref, dvb * beta, K, 2 * H * HEAD_DIM)
        dG = rs(E) - rs(jnp.swapaxes(E, 1, 2)) + deG * eG - deGl * eGl
        dGl = jnp.sum(deGl * eGl, axis=1, keepdims=True) + dcd * cd
        dG = dG + jnp.where(rowc == C - 1, dGl, 0.0)
        qh = f["qh"]
        _put_pairs(dqkv_ref, (HEAD_DIM ** -0.5) * f["rq"] * (dqn - qh * _rowsum(dqn * qh)), K)
        _put_pairs(dqkv_ref, f["rk"] * (dkn - kn * _rowsum(dkn * kn)), K, H * HEAD_DIM)
        db = dbeta * beta * (1.0 - beta)
        triu = (ri <= ci).astype(f32)
        for c in range(K):
            db_all = jnp.where(lane == 0, db[c * H], 0.0)
            dG_all = jnp.where(lane == H, dG[c * H], 0.0)
            for h in range(1, H):
                db_all = db_all + jnp.where(lane == h, db[c * H + h], 0.0)
                dG_all = dG_all + jnp.where(lane == H + h, dG[c * H + h], 0.0)
            dg_all = _dot01(triu, dG_all, _NN)
            da_all = dg_all * (-f["ea"]) * _sigmoid(ba[c * C:(c + 1) * C] + dtb_)
            dba_ref[c * C:(c + 1) * C, :] = (db_all + da_all).astype(bf16)
            ddt_ref[...] += jnp.sum(da_all, axis=0, keepdims=True)
            dal_ref[...] += jnp.sum(dg_all * f["g_all"][c], axis=0, keepdims=True)

    small = jax.ShapeDtypeStruct((1, 128), f32)
    return pl.pallas_call(
        body, grid=(NS,),
        in_specs=[wide(0), wide(1), wide(2), wide(3), one(4 * H), vec, vec, vec, st_spec(HEAD_DIM), st_spec(C),
                  wide(0), wide(0), wide(0)],
        out_specs=[pl.BlockSpec((K * C, 3 * H * HEAD_DIM), lambda n: (NS - 1 - n, 0)), wide(3), one(0), vec, vec, vec],
        out_shape=[jax.ShapeDtypeStruct((T, 3 * H * HEAD_DIM), f32), jax.ShapeDtypeStruct((T, ODD_PAD), bf16),
                   jax.ShapeDtypeStruct((T, 128), bf16), small, small, small],
        scratch_shapes=[pltpu.VMEM((H, HEAD_DIM, HEAD_DIM), f32)],
        compiler_params=_cparams(("arbitrary",)), name=name,
    )(qkv, qkv, qkv, p, p, alog, dtb, nw, sprev, tinv, vn_all, o_all, dy_all)


def _lanes_from(x, s):
    return x if s % 128 == 0 else pltpu.roll(x, (128 - s) % 128, 1)


def _odd_assemble(g, name):
    R = g.shape[1]
    tr = min(256, R)
    n_blk = ODD_SHARD_PAD // 128

    def body(g_ref, o_ref):
        lane = lax.broadcasted_iota(jnp.int32, (tr, 128), 1)
        blk = lambda d, m: g_ref[d, :, m * 128:(m + 1) * 128]
        for gb in range(ODD_PAD // 128):
            c0 = 128 * gb
            if c0 >= ODD_IN:
                o_ref[:, c0:c0 + 128] = jnp.zeros((tr, 128), g.dtype)
                continue
            d0 = c0 // ODD_SHARD
            m0, sh = divmod(c0 - ODD_SHARD * d0, 128)
            take = min(128, ODD_SHARD * (d0 + 1) - c0)
            p = _lanes_from(blk(d0, m0), sh)
            if sh and m0 + 1 < n_blk:
                p = jnp.where(lane < 128 - sh, p, _lanes_from(blk(d0, m0 + 1), sh))
            if take < 128:
                nxt = pltpu.roll(blk(d0 + 1, 0), take, 1) if d0 + 1 < N_DEV else jnp.zeros((tr, 128), g.dtype)
                p = jnp.where(lane < take, p, nxt)
            o_ref[:, c0:c0 + 128] = p

    return pl.pallas_call(
        body, grid=(R // tr,),
        in_specs=[pl.BlockSpec((N_DEV, tr, ODD_SHARD_PAD), lambda i: (0, i, 0))],
        out_specs=pl.BlockSpec((tr, ODD_PAD), lambda i: (i, 0)),
        out_shape=jax.ShapeDtypeStruct((R, ODD_PAD), g.dtype),
        compiler_params=_cparams(("parallel",)), name=name,
    )(g)


def _odd_split(w, name):
    R = w.shape[0]
    tr = min(256, R)

    def body(w_ref, o_ref):
        lane = lax.broadcasted_iota(jnp.int32, (tr, 128), 1)
        blk = lambda gb: w_ref[:, gb * 128:(gb + 1) * 128]
        for d in range(N_DEV):
            for m in range(ODD_SHARD_PAD // 128):
                g0, sh = divmod(ODD_SHARD * d + 128 * m, 128)
                p = _lanes_from(blk(g0), sh)
                if sh and g0 + 1 < ODD_PAD // 128:
                    p = jnp.where(lane < 128 - sh, p, _lanes_from(blk(g0 + 1), sh))
                real = ODD_SHARD - 128 * m
                if real < 128:
                    p = jnp.where(lane < real, p, jnp.zeros_like(p))
                o_ref[d, :, m * 128:(m + 1) * 128] = p

    return pl.pallas_call(
        body, grid=(R // tr,),
        in_specs=[pl.BlockSpec((tr, ODD_PAD), lambda i: (i, 0))],
        out_specs=pl.BlockSpec((N_DEV, tr, ODD_SHARD_PAD), lambda i: (0, i, 0)),
        out_shape=jax.ShapeDtypeStruct((N_DEV, R, ODD_SHARD_PAD), w.dtype),
        compiler_params=_cparams(("parallel",)), name=name,
    )(w)


def _adamw(w, gs, m, v, name, layer=None, prev=None):
    R, Cc = w.shape[-2:]
    S = gs.shape[0]
    tr = R
    if S * R * Cc * 4 > (4 << 20):
        for cand in (256, 128, 64, 32, 16, 8):
            if R % cand == 0 and R > cand:
                tr = cand
                break
    c1 = 1.0 - ADAM_B1 ** ADAM_STEP
    c2 = 1.0 - ADAM_B2 ** ADAM_STEP

    def body(w_ref, g_ref, m_ref, v_ref, *rest):
        go_ref, d_ref, mo_ref, vo_ref = rest[-4:]
        g = g_ref[0].astype(f32)
        for s in range(1, S):
            g = g + g_ref[s].astype(f32)
        mn = ADAM_B1 * m_ref[...] + (1.0 - ADAM_B1) * g
        vn = ADAM_B2 * v_ref[...] + (1.0 - ADAM_B2) * (g * g)
        go_ref[...] = g
        mo_ref[...] = mn
        vo_ref[...] = vn
        d_ref[...] = -ADAM_LR * ((mn / c1) / (jnp.sqrt(vn / c2) + ADAM_EPS) + ADAM_WD * w_ref[...])

    tc = Cc if gs.shape[-1] == Cc else 128
    if layer is None:
        blk = pl.BlockSpec((tr, tc), lambda i, j: (i, j))
    else:
        blk = pl.BlockSpec((None, tr, tc), lambda i, j: (layer, i, j))
    out = jax.ShapeDtypeStruct(w.shape, f32)
    carried = [] if prev is None else list(prev)
    return pl.pallas_call(
        body, grid=(R // tr, pl.cdiv(Cc, tc)),
        in_specs=[blk, pl.BlockSpec((S, tr, tc), lambda i, j: (0, i, j)), blk, blk]
        + [pl.BlockSpec(memory_space=pl.ANY)] * len(carried),
        out_specs=[blk] * 4, out_shape=[out] * 4,
        input_output_aliases={4 + j: j for j in range(len(carried))},
        compiler_params=_cparams(("parallel", "parallel")), name=name,
    )(w, gs, m, v, *carried)


def _me():
    x, y, c = lax.axis_index("x"), lax.axis_index("y"), lax.axis_index("c")
    return x, y, c, 4 * x + 2 * y + c


def _peer(k):
    x, y, c, _ = _me()
    px = 1 - x if k & 4 else x
    py = 1 - y if k & 2 else y
    pc = 1 - c if k & 1 else c
    return (px, py, pc), 4 * px + 2 * py + pc


_HBM = pl.BlockSpec(memory_space=pltpu.HBM)
_SEM = pl.BlockSpec(memory_space=pltpu.SEMAPHORE)
_EFFECT = pltpu.SideEffectType.DATAFLOW_SIDE_EFFECTING


def _copy(src, land, ssem, rsem, k, blocked, landing_slot_of_peer):
    pid, pidx = _peer(k)
    slot = pidx if landing_slot_of_peer else _me()[3]
    return pltpu.make_async_remote_copy(src_ref=src.at[pidx] if blocked else src, dst_ref=land.at[slot],
                                        send_sem=ssem.at[k - 1], recv_sem=rsem.at[k - 1], device_id=pid, device_id_type=MESH)


def _send_start(srcs, blocked, name):
    n = len(srcs)
    lands = [lax.empty(a.shape if blocked else (N_DEV,) + a.shape, a.dtype) for a in srcs]

    def body(*refs):
        src, land, sems, token = refs[:n], refs[n:2 * n], refs[2 * n:4 * n], refs[-1]
        for i in range(n):
            for k in range(1, N_DEV):
                _copy(src[i], land[i], sems[2 * i], sems[2 * i + 1], k, blocked, False).start()
        token[...] = jnp.zeros_like(token)

    sem = pltpu.SemaphoreType.DMA((N_DEV - 1,))
    hbm = lambda a: pltpu.with_memory_space_constraint(a, pltpu.HBM)
    res = pl.pallas_call(
        body, name=name,
        out_shape=tuple([sem] * (2 * n)) + tuple(pltpu.HBM(a.shape, a.dtype) for a in srcs + lands)
        + (jax.ShapeDtypeStruct((8, 128), f32),),
        in_specs=(_HBM,) * (2 * n),
        out_specs=(_SEM,) * (2 * n) + (_HBM,) * (2 * n) + (pl.BlockSpec(memory_space=pltpu.VMEM),),
        input_output_aliases={j: 2 * n + j for j in range(2 * n)},
        compiler_params=pltpu.CompilerParams(has_side_effects=_EFFECT),
    )(*[hbm(a) for a in srcs], *[hbm(a) for a in lands])
    handles = [(res[2 * i], res[2 * i + 1], res[2 * n + i], res[3 * n + i]) for i in range(n)]
    return handles, res[-1]


def _send_wait(handle, blocked, after, name):
    ssem, rsem, src, land = handle

    def body(src_ref, land_ref, ssem_ref, rsem_ref, after_ref, src_out, land_out):
        for k in range(1, N_DEV):
            cp = _copy(src_ref, land_ref, ssem_ref, rsem_ref, k, blocked, True)
            cp.wait_send()
            cp.wait_recv()

    return pl.pallas_call(
        body, name=name, out_shape=(pltpu.HBM(src.shape, src.dtype), pltpu.HBM(land.shape, land.dtype)),
        in_specs=(_HBM, _HBM, _SEM, _SEM, pl.BlockSpec(memory_space=pl.ANY)), out_specs=(_HBM, _HBM),
        input_output_aliases={0: 0, 1: 1}, compiler_params=pltpu.CompilerParams(has_side_effects=_EFFECT),
    )(src, land, ssem, rsem, after)


def _block_diag(w):
    nb, bs = w.shape[0], w.shape[1]
    eye = jnp.eye(nb, dtype=w.dtype)
    return (eye[:, None, :, None] * w[:, :, None, :]).reshape(nb * bs, nb * bs)


def _diag_blocks(d):
    return jnp.stack([d[g, s * 64:(s + 1) * 64, s * 64:(s + 1) * 64] for g in range(4) for s in range(2)])


_SQUARE_TILES = dict(tm=1024, tn=1024, tk=1024)


def _mlp_fwd(x, hm, wu, wd, tag, epilogue, extras, outs):
    (r,) = _matmul(hm, wu, "nn", outs=[bf16], epilogue=lambda acc: (jnp.maximum(acc, 0.0),), name=f"mlp_up_{tag}")
    res = _matmul(r, wd, "nn", outs=outs, extras=(x,) + tuple(extras), epilogue=epilogue, a_map=jnp.square,
                  name=f"mlp_down_{tag}", **_SQUARE_TILES)
    return res, (hm, r)


def _mlp_bwd(x, nw, wu, wd, saved, dxo, dxo_b, tag, sink):
    hm, r = saved
    (du,) = _matmul(dxo_b, wd, "nt", outs=[bf16], extras=(r,), epilogue=lambda acc, rr: (acc * (2.0 * rr.astype(f32)),),
                    name=f"mlp_dact_{tag}")
    (dwd,) = _matmul(r, dxo_b, "tn", outs=[bf16], a_map=jnp.square, name=f"mlp_dwd_{tag}", **_SQUARE_TILES)
    tok = sink({f"w_down{tag}": dwd.reshape(N_DEV, D_FF // N_DEV, D_MODEL)})
    (dwu,) = _matmul(hm, du, "tn", outs=[bf16], shard_cols=2, extras=(jnp.broadcast_to(tok[0:1, 0:1], (1, D_FF)),),
                     epilogue=lambda acc, zero: (acc + zero,), name=f"mlp_dwu_{tag}")
    tok = sink({f"w_up{tag}": dwu})
    return _matmul(du, wu, "nt", outs=_RMS_BWD_OUTS, extras=(x, dxo, nw + tok[0:1, 0:1]), epilogue=_rms_bwd_ep,
                   name=f"mlp_dh_{tag}", **_SQUARE_TILES)


def _local_step(x, tgt, P, weight, sink):
    T = x.shape[0]
    cos, sin = _rope_tables(T)
    rtab = _ret_tables()
    row = lambda a: a.reshape(1, -1)
    mix_nw, mlp_nw = P["mixer_norm_w"], P["mlp_norm_w"]
    wr_bd, wi_bd = _block_diag(P["lru_w_r"]), _block_diag(P["lru_w_i"])
    lru_b, lru_br, lru_bi, lru_lam = row(P["lru_conv_b"]), row(P["lru_b_r"]), row(P["lru_b_i"]), row(P["lru_lambda"])
    pad16 = lambda a: jnp.pad(a.reshape(1, GDN_HEADS), ((0, 0), (GDN_HEADS, 128 - 2 * GDN_HEADS)))
    alog, dtb = pad16(P["gdn_a_log"]), pad16(P["gdn_dt_bias"])
    gnw = row(P["gdn_norm_w"])

    x0 = x
    h0 = _rms_fwd(x0, mix_nw[0:1], "rms_mix_0")
    w_ie = weight("w_in_even", h0)
    (pe,) = _matmul(h0, w_ie, "nn", outs=[f32], name="in_even")
    mix0, o_ret, s_ret = _ret_fwd(pe, cos, sin, rtab, "ret_fwd")
    w_lc = weight("lru_conv_w", pe)
    xc = _conv_fwd(pe, 4, w_lc, lru_b, False, "lru_conv_fwd")
    mix0, h_lru = _lru_fwd(xc, pe, 20, wr_bd, wi_bd, lru_br, lru_bi, lru_lam, mix0, "lru_fwd")
    w_oe = weight("w_out_even", mix0)
    x1, hm0 = _matmul(mix0, w_oe, "nn", outs=[f32, bf16], extras=(x0, mlp_nw[0:1]), epilogue=_residual_rms_ep,
                      name="out_even", tm=1024, tn=D_MODEL)
    w_u0, w_d0 = weight("w_up0", x1), weight("w_down0", x1)
    (x2, h1), mlp0 = _mlp_fwd(x1, hm0, w_u0, w_d0, "0", _residual_rms_ep, (mix_nw[1:2],), [f32, bf16])
    w_io = weight("w_in_odd", h1)
    (po,) = _matmul(h1, w_io, "nn", outs=[f32], tm=2048, tn=ODD_PAD // 3, name="in_odd")
    w_gc = weight("gdn_conv_w", po)
    qkv = _conv_fwd(po, 0, w_gc, None, True, "gdn_conv_fwd")
    y_gdn, s_gdn, ti_gdn, vn_gdn, o_gdn = _gdn_fwd(qkv, po, alog, dtb, gnw, "gdn_fwd")
    w_oo = weight("w_out_odd", y_gdn)
    x3, hm1 = _matmul(y_gdn, w_oo, "nn", outs=[f32, bf16], extras=(x2, mlp_nw[1:2]), epilogue=_residual_rms_ep,
                      name="out_odd", tm=1024, tn=D_MODEL)
    w_u1, w_d1 = weight("w_up1", x3), weight("w_down1", x3)
    (loss, dx4, dx4_b, d_final), mlp1 = _mlp_fwd(x3, hm1, w_u1, w_d1, "1", _loss_ep, (row(P["final_norm_w"]), tgt),
                                                 _LOSS_OUTS)
    dx3, dx3_b, d_mlp_nw1 = _mlp_bwd(x3, mlp_nw[1:2], w_u1, w_d1, mlp1, dx4, dx4_b, "1", sink)
    (dy_gdn,) = _matmul(dx3_b, w_oo, "nt", outs=[f32], name="out_odd_dx")
    (d_woo,) = _matmul(y_gdn, dx3_b, "tn", outs=[bf16], name="out_odd_dw")
    dqkv, dpo, dba, d_alog, d_dtb, d_gnw = _gdn_bwd(qkv, po, alog, dtb, gnw, s_gdn, ti_gdn, vn_gdn, o_gdn, dy_gdn,
                                                  "gdn_bwd")
    dpo, d_gconv, _ = _conv_bwd(po, 0, w_gc, None, True, dqkv, dpo, "gdn_conv_bwd")
    dpo = lax.dynamic_update_slice(dpo, dba, (0, 4 * D_MODEL))
    (d_wio,) = _matmul(h1, dpo, "tn", outs=[bf16], tn=ODD_PAD // 3, name="in_odd_dw")
    tok = sink(dict(w_out_odd=d_woo.reshape(N_DEV, D_MODEL // N_DEV, D_MODEL), w_in_odd=_odd_split(d_wio, "w_in_odd_split")))
    dx2, dx2_b, d_mix_nw1 = _matmul(dpo, w_io, "nt", outs=_RMS_BWD_OUTS, extras=(x2, dx3, mix_nw[1:2] + tok[0:1, 0:1]),
                                    epilogue=_rms_bwd_ep, tm=1024, tn=1024, tk=ODD_PAD // 3, name="in_odd_dx")
    dx1, dx1_b, d_mlp_nw0 = _mlp_bwd(x1, mlp_nw[0:1], w_u0, w_d0, mlp0, dx2, dx2_b, "0", sink)
    (d_woe,) = _matmul(mix0, dx1_b, "tn", outs=[bf16], name="out_even_dw")
    tok = sink(dict(w_out_even=d_woe.reshape(N_DEV, D_MODEL // N_DEV, D_MODEL)))
    (dmix0,) = _matmul(dx1_b, w_oe, "nt", outs=[f32], name="out_even_dx")
    dpe = _ret_bwd(pe, cos, sin, rtab, o_ret, s_ret, dmix0, "ret_bwd")
    dxc, dpe, d_wr, d_wi, d_br, d_bi, d_lam = _lru_bwd(xc, pe, 20, wr_bd, wi_bd, lru_br, lru_bi, lru_lam + tok[0:1, 0:1],
                                                       h_lru, dmix0, 4, dpe, "lru_bwd")
    dpe, d_lconv, d_lconv_b = _conv_bwd(pe, 4, w_lc, lru_b, False, dxc, dpe, "lru_conv_bwd")
    (d_wie,) = _matmul(h0, dpe, "tn", outs=[bf16], shard_cols=2, name="in_even_dw")
    tok = sink(dict(w_in_even=d_wie))
    dx0, _, d_mix_nw0 = _matmul(dpe, w_ie, "nt", outs=_RMS_BWD_OUTS, extras=(x0, dx1, mix_nw[0:1] + tok[0:1, 0:1]),
                                epilogue=_rms_bwd_ep, name="in_even_dx", **_SQUARE_TILES)

    G = dict(
        mixer_norm_w=jnp.concatenate([d_mix_nw0, d_mix_nw1], axis=0),
        mlp_norm_w=jnp.concatenate([d_mlp_nw0, d_mlp_nw1], axis=0),
        final_norm_w=d_final.reshape(-1),
        lru_conv_w=d_lconv, lru_conv_b=d_lconv_b.reshape(-1),
        lru_w_r=_diag_blocks(d_wr), lru_b_r=d_br.reshape(-1), lru_w_i=_diag_blocks(d_wi), lru_b_i=d_bi.reshape(-1),
        lru_lambda=d_lam.reshape(-1), gdn_conv_w=d_gconv,
        gdn_a_log=d_alog[0, GDN_HEADS:2 * GDN_HEADS], gdn_dt_bias=d_dtb[0, GDN_HEADS:2 * GDN_HEADS],
        gdn_norm_w=d_gnw.reshape(-1),
    )
    return loss, dx0, G


_SMALL = ["mixer_norm_w", "mlp_norm_w", "final_norm_w", "lru_conv_b", "lru_w_r", "lru_b_r", "lru_w_i", "lru_b_i",
          "lru_lambda", "gdn_a_log", "gdn_dt_bias", "gdn_norm_w"]
_PACK_ROWS = 688


def _pack(parts):
    flat = jnp.concatenate([p.reshape(-1) for p in parts])
    return jnp.pad(flat, (0, _PACK_ROWS * 128 - flat.shape[0])).reshape(_PACK_ROWS, 128)


def _unpack(packed, shapes):
    flat = packed.reshape(-1)
    out, off = [], 0
    for s in shapes:
        n = int(np.prod(s))
        out.append(flat[off:off + n].reshape(s))
        off += n
    return out


def kernel(x, mixer_norm_w, mlp_norm_w, final_norm_w, w_in_even, lru_conv_w, lru_conv_b, lru_w_r, lru_b_r, lru_w_i, lru_b_i, lru_lambda, w_out_even, w_in_odd, gdn_conv_w, gdn_a_log, gdn_dt_bias, gdn_norm_w, w_out_odd, w_up, w_down, loss_target, m_mixer_norm_w, m_mlp_norm_w, m_final_norm_w, m_w_in_even, m_lru_conv_w, m_lru_conv_b, m_lru_w_r, m_lru_b_r, m_lru_w_i, m_lru_b_i, m_lru_lambda, m_w_out_even, m_w_in_odd, m_gdn_conv_w, m_gdn_a_log, m_gdn_dt_bias, m_gdn_norm_w, m_w_out_odd, m_w_up, m_w_down, v_mixer_norm_w, v_mlp_norm_w, v_final_norm_w, v_w_in_even, v_lru_conv_w, v_lru_conv_b, v_lru_w_r, v_lru_b_r, v_lru_w_i, v_lru_b_i, v_lru_lambda, v_w_out_even, v_w_in_odd, v_gdn_conv_w, v_gdn_a_log, v_gdn_dt_bias, v_gdn_norm_w, v_w_out_odd, v_w_up, v_w_down):
    Pw = dict(mixer_norm_w=mixer_norm_w, mlp_norm_w=mlp_norm_w, final_norm_w=final_norm_w, w_in_even=w_in_even,
              lru_conv_w=lru_conv_w, lru_conv_b=lru_conv_b, lru_w_r=lru_w_r, lru_b_r=lru_b_r, lru_w_i=lru_w_i,
              lru_b_i=lru_b_i, lru_lambda=lru_lambda, w_out_even=w_out_even, w_in_odd=w_in_odd, gdn_conv_w=gdn_conv_w,
              gdn_a_log=gdn_a_log, gdn_dt_bias=gdn_dt_bias, gdn_norm_w=gdn_norm_w, w_out_odd=w_out_odd, w_up=w_up,
              w_down=w_down)
    Pm = dict(mixer_norm_w=m_mixer_norm_w, mlp_norm_w=m_mlp_norm_w, final_norm_w=m_final_norm_w, w_in_even=m_w_in_even,
              lru_conv_w=m_lru_conv_w, lru_conv_b=m_lru_conv_b, lru_w_r=m_lru_w_r, lru_b_r=m_lru_b_r, lru_w_i=m_lru_w_i,
              lru_b_i=m_lru_b_i, lru_lambda=m_lru_lambda, w_out_even=m_w_out_even, w_in_odd=m_w_in_odd,
              gdn_conv_w=m_gdn_conv_w, gdn_a_log=m_gdn_a_log, gdn_dt_bias=m_gdn_dt_bias, gdn_norm_w=m_gdn_norm_w,
              w_out_odd=m_w_out_odd, w_up=m_w_up, w_down=m_w_down)
    Pv = dict(mixer_norm_w=v_mixer_norm_w, mlp_norm_w=v_mlp_norm_w, final_norm_w=v_final_norm_w, w_in_even=v_w_in_even,
              lru_conv_w=v_lru_conv_w, lru_conv_b=v_lru_conv_b, lru_w_r=v_lru_w_r, lru_b_r=v_lru_b_r, lru_w_i=v_lru_w_i,
              lru_b_i=v_lru_b_i, lru_lambda=v_lru_lambda, w_out_even=v_w_out_even, w_in_odd=v_w_in_odd,
              gdn_conv_w=v_gdn_conv_w, gdn_a_log=v_gdn_a_log, gdn_dt_bias=v_gdn_dt_bias, gdn_norm_w=v_gdn_norm_w,
              w_out_odd=v_w_out_odd, w_up=v_w_up, w_down=v_w_down)
    me = _me()[3]
    T = x.shape[1]

    cols = lambda g: jnp.transpose(g, (1, 0, 2)).reshape(g.shape[1], -1)
    rows = lambda g: g.reshape(-1, g.shape[2])
    wide = lambda g: _odd_assemble(g, "w_in_odd_assemble")
    odd_shard = jnp.pad(w_in_odd[0].astype(bf16), ((0, 0), (0, ODD_SHARD_PAD - ODD_SHARD)))
    as_is = lambda g: g
    gather = dict(
        w_in_even=(w_in_even[0].astype(bf16), cols), lru_conv_w=(lru_conv_w[0], cols),
        w_out_even=(w_out_even[0].astype(bf16), rows), w_up0=(w_up[0].astype(bf16), as_is), w_down0=(w_down[0].astype(bf16), rows),
        w_in_odd=(odd_shard, wide), gdn_conv_w=(gdn_conv_w[0], cols),
        w_out_odd=(w_out_odd[0].astype(bf16), rows), w_up1=(w_up[1].astype(bf16), as_is), w_down1=(w_down[1].astype(bf16), rows))
    handles, tok = _send_start([s for s, _ in gather.values()], False, "gather_start")
    handles = dict(zip(gather, handles))
    full = {}

    def weight(name, after):
        if name not in full:
            shard, landed = _send_wait(handles[name], False, after, f"gather_wait_{name}")
            full[name] = gather[name][1](lax.dynamic_update_slice_in_dim(landed, shard[None], me, 0))
        return full[name]

    P = {k: Pw[k] for k in ("mlp_norm_w", "final_norm_w")}
    P["mixer_norm_w"] = mixer_norm_w + tok[0:1, 0:1]
    for k in ("lru_w_r", "lru_w_i", "lru_conv_b", "lru_b_r", "lru_b_i", "lru_lambda", "gdn_a_log", "gdn_dt_bias", "gdn_norm_w"):
        P[k] = Pw[k][0]

    sent = {}

    def sink(grads):
        hs, token = _send_start(list(grads.values()), True, "grads_start_" + "_".join(grads))
        sent.update(zip(grads, hs))
        return token

    loss, dx, G = _local_step(x[0], loss_target[0], P, weight, sink)
    small_g = [G[k].reshape(Pw[k].shape) for k in _SMALL] + [G["lru_conv_w"], G["gdn_conv_w"], loss[0, 0:1]]
    packed = _pack(small_g)
    sink(dict(small=jnp.broadcast_to(packed[None], (N_DEV,) + packed.shape)))

    def received(name, after=dx):
        g, landed = _send_wait(sent[name], True, after, f"grads_wait_{name}")
        return lax.dynamic_update_slice_in_dim(landed, lax.dynamic_slice_in_dim(g, me, 1, 0), me, 0)

    out = {}
    nff = D_FF // N_DEV

    def whole(name, gs):
        out[name] = tuple(_adamw(Pw[name], gs, Pm[name], Pv[name], f"adamw_{name}", layer=0))

    def layers(name):
        res = None
        for l in range(2):
            res = _adamw(Pw[name], received(f"{name}{l}"), Pm[name], Pv[name], f"adamw_{name}{l}", layer=l, prev=res)
        out[name] = tuple(res)

    layers("w_up")
    layers("w_down")
    whole("w_out_odd", received("w_out_odd"))
    whole("w_in_odd", received("w_in_odd"))
    whole("w_out_even", received("w_out_even"))
    whole("w_in_even", received("w_in_even", out["w_out_even"][1]))
    small_shapes = [Pw[k].shape for k in _SMALL]
    pw, pm, pv = (_pack([Q[k] for k in _SMALL]) for Q in (Pw, Pm, Pv))
    sg, sd, sm, sv = _adamw(pw, received("small", out["w_in_even"][1]), pm, pv, "adamw_small")
    for arrs_i, packed_out in enumerate((sg, sd, sm, sv)):
        for k, a in zip(_SMALL, _unpack(packed_out, small_shapes)):
            out.setdefault(k, [None] * 4)[arrs_i] = a
    n_small = sum(int(np.prod(s)) for s in small_shapes)
    gflat = sg.reshape(-1)
    g_lconv = gflat[n_small:n_small + CONV_K * LRU_WIDTH].reshape(CONV_K, LRU_WIDTH)
    g_gconv = gflat[n_small + CONV_K * LRU_WIDTH:n_small + CONV_K * (LRU_WIDTH + 3072)].reshape(CONV_K, 3072)
    whole("lru_conv_w", lax.dynamic_slice_in_dim(g_lconv, me * 64, 64, axis=1)[None])
    whole("gdn_conv_w", lax.dynamic_slice_in_dim(g_gconv, me * 384, 384, axis=1)[None])

    names = ["mixer_norm_w", "mlp_norm_w", "final_norm_w", "w_in_even", "lru_conv_w", "lru_conv_b", "lru_w_r", "lru_b_r",
             "lru_w_i", "lru_b_i", "lru_lambda", "w_out_even", "w_in_odd", "gdn_conv_w", "gdn_a_log", "gdn_dt_bias",
             "gdn_norm_w", "w_out_odd", "w_up", "w_down"]
    total = gflat[n_small + CONV_K * (LRU_WIDTH + 3072)]
    res = [total, dx[None]]
    for j in range(4):
        res += [out[k][j] for k in names]
    return tuple(res)
```

```python
import math

import numpy as np
import jax
import jax.numpy as jnp
from jax import lax
from jax.experimental import pallas as pl
from jax.experimental.pallas import tpu as pltpu

f32 = jnp.float32
bf16 = jnp.bfloat16

N_DEV = 8
D_MODEL = 1024
D_FF = 4096
EPS = 1e-6
RET_HEADS = 4
RET_CHUNK = 128
RET_STEP = 4
ROPE_THETA = 10000.0
LRU_WIDTH = 512
LRU_C = 8.0
GDN_HEADS = 8
GDN_CHUNK = 64
GDN_STEP = 4
HEAD_DIM = 128
ODD_IN = 4112
ODD_PAD = 4224
ODD_SHARD = ODD_IN // N_DEV
ODD_SHARD_PAD = 640
ADAM_LR, ADAM_B1, ADAM_B2, ADAM_EPS, ADAM_WD, ADAM_STEP = 0.001, 0.9, 0.999, 1e-08, 0.01, 10
VMEM_LIMIT = 56 * 1024 * 1024

_NN = (((1,), (0,)), ((), ()))
_NT = (((1,), (1,)), ((), ()))
_TN = (((0,), (0,)), ((), ()))
MESH = pl.DeviceIdType.MESH


def _cparams(sem):
    return pltpu.CompilerParams(dimension_semantics=sem, vmem_limit_bytes=VMEM_LIMIT)


def _dot(a, b, dn):
    return lax.dot_general(a.astype(bf16), b.astype(bf16), dn, preferred_element_type=f32)


def _dot01(a01, b, dn):
    a = a01.astype(bf16)
    b0 = b.astype(bf16)
    r1 = b - b0.astype(f32)
    b1 = r1.astype(bf16)
    b2 = (r1 - b1.astype(f32)).astype(bf16)
    d = lambda q: lax.dot_general(a, q, dn, preferred_element_type=f32)
    return d(b0) + (d(b1) + d(b2))


def _sigmoid(x):
    return jax.nn.sigmoid(x)


def _silu(x):
    return x * _sigmoid(x)


def _dsilu(x):
    s = _sigmoid(x)
    return s * (1.0 + x * (1.0 - s))


def _softplus(x):
    return jnp.maximum(x, 0.0) + jnp.log1p(jnp.exp(-jnp.abs(x)))


_GELU_C = math.sqrt(2.0 / math.pi)


def _gelu(y):
    return 0.5 * y * (1.0 + jnp.tanh(_GELU_C * (y + 0.044715 * y * y * y)))


def _dgelu(y):
    t = jnp.tanh(_GELU_C * (y + 0.044715 * y * y * y))
    return 0.5 * (1.0 + t) + 0.5 * y * (1.0 - t * t) * _GELU_C * (1.0 + 3.0 * 0.044715 * y * y)


def _matmul(a, b, form, *, outs, name, epilogue=None, extras=(), tm=4096, tn=512, tk=1024, shard_cols=0, a_map=None):
    if form == "tn":
        K, M = a.shape
    else:
        M, K = a.shape
    per_step = 1
    if b.ndim == 3:
        assert form in ("nn", "nt"), name
        N = b.shape[1] if form == "nt" else N_DEV * b.shape[2]
        if form == "nn":
            tn = b.shape[2]
        else:
            per_step = max(1, tk // b.shape[2])
            tk = per_step * b.shape[2]
    else:
        N = b.shape[0] if form == "nt" else b.shape[1]
    ns = N // N_DEV
    if shard_cols:
        tn = ns * shard_cols
    tm, tn, tk = min(tm, M), min(tn, N), min(tk, K)
    assert M % tm == 0 and N % tn == 0 and K % tk == 0, (name, M, N, K, tm, tn, tk)
    nk = K // tk
    dn = {"nn": _NN, "nt": _NT, "tn": _TN}[form]
    if form == "tn":
        a_spec = pl.BlockSpec((tk, tm), lambda i, j, k: (k, i))
    else:
        a_spec = pl.BlockSpec((tm, tk), lambda i, j, k: (i, k))
    if b.ndim == 3:
        b_spec = (pl.BlockSpec((per_step, tn, tk // per_step), lambda i, j, k: (k, j, 0)) if form == "nt"
                  else pl.BlockSpec((None, tk, tn), lambda i, j, k: (j, k, 0)))
    elif form == "nt":
        b_spec = pl.BlockSpec((tn, tk), lambda i, j, k: (j, k))
    else:
        b_spec = pl.BlockSpec((tk, tn), lambda i, j, k: (k, j))
    e_spec = pl.BlockSpec((tm, tn), lambda i, j, k: (i, j))
    v_spec = pl.BlockSpec((1, tn), lambda i, j, k: (0, j))
    if shard_cols:
        o_spec = pl.BlockSpec((shard_cols, tm, ns), lambda i, j, k: (j, i, 0))
        o_shape = (N_DEV, M, ns)
    else:
        o_spec = e_spec
        o_shape = (M, N)
    n_ex = len(extras)
    sums = [isinstance(o, tuple) for o in outs]
    assert not any(sums) or tn == N, name

    def finish(acc, ex, o_refs, row_tile):
        vals = (acc,) if epilogue is None else epilogue(acc, *[e[...] for e in ex])
        for r, v, is_sum in zip(o_refs, vals, sums):
            if is_sum:
                @pl.when(row_tile == 0)
                def _(r=r, v=v):
                    r[...] = v.astype(r.dtype)

                @pl.when(row_tile > 0)
                def _(r=r, v=v):
                    r[...] += v.astype(r.dtype)
            elif shard_cols:
                for s in range(shard_cols):
                    r[s] = v[:, s * ns:(s + 1) * ns].astype(r.dtype)
            else:
                r[...] = v.astype(r.dtype)

    def prod(a_ref, b_ref):
        if b.ndim == 3 and form == "nt":
            w = tk // per_step
            return sum(_dot(a_ref[:, s * w:(s + 1) * w], b_ref[s], dn) for s in range(1, per_step)) + _dot(a_ref[:, 0:w], b_ref[0], dn)
        av = a_ref[...]
        return _dot(av if a_map is None else a_map(av), b_ref[...], dn)

    def body_one(*refs):
        finish(prod(*refs[:2]), refs[2:2 + n_ex], refs[2 + n_ex:], pl.program_id(0))

    def body_acc(*refs):
        a_ref, b_ref = refs[:2]
        acc = refs[-1]
        k = pl.program_id(2)
        row_tile = pl.program_id(0)

        @pl.when(k == 0)
        def _():
            acc[...] = prod(a_ref, b_ref)

        @pl.when((k > 0) & (k < nk - 1))
        def _():
            acc[...] += prod(a_ref, b_ref)

        @pl.when(k == nk - 1)
        def _():
            finish(acc[...] + prod(a_ref, b_ref), refs[2:2 + n_ex], refs[2 + n_ex:-1], row_tile)

    return pl.pallas_call(
        body_one if nk == 1 else body_acc, grid=(M // tm, N // tn, nk),
        in_specs=[a_spec, b_spec] + [v_spec if e.shape[0] == 1 else e_spec for e in extras],
        out_specs=[v_spec if s else o_spec for s in sums],
        out_shape=[jax.ShapeDtypeStruct((1, N), o[1]) if s else jax.ShapeDtypeStruct(o_shape, o) for o, s in zip(outs, sums)],
        scratch_shapes=[] if nk == 1 else [pltpu.VMEM((tm, tn), f32)],
        compiler_params=_cparams(("arbitrary" if any(sums) else "parallel", "parallel", "arbitrary")), name=name,
    )(a, b, *extras)


def _rms_fwd(x, w, name):
    T, D = x.shape
    tt = min(512, T)

    def body(x_ref, w_ref, h_ref):
        xv = x_ref[...]
        r = lax.rsqrt(jnp.mean(xv * xv, axis=1, keepdims=True) + EPS)
        h_ref[...] = (xv * r * w_ref[...]).astype(bf16)

    return pl.pallas_call(
        body, grid=(T // tt,),
        in_specs=[pl.BlockSpec((tt, D), lambda i: (i, 0)), pl.BlockSpec((1, D), lambda i: (0, 0))],
        out_specs=pl.BlockSpec((tt, D), lambda i: (i, 0)),
        out_shape=jax.ShapeDtypeStruct((T, D), bf16),
        compiler_params=_cparams(("parallel",)), name=name,
    )(x, w)


def _residual_rms_ep(acc, res, w):
    x = res + acc
    r = lax.rsqrt(jnp.mean(x * x, axis=1, keepdims=True) + EPS)
    return x, x * r * w


_RMS_BWD_OUTS = [f32, bf16, ("sum", f32)]


def _rms_bwd_ep(dh, x, dres, w):
    r = lax.rsqrt(jnp.mean(x * x, axis=1, keepdims=True) + EPS)
    xn = x * r
    dhw = dh * w
    dx = dres + r * (dhw - xn * jnp.mean(dhw * xn, axis=1, keepdims=True))
    return dx, dx, jnp.sum(dh * xn, axis=0, keepdims=True)


_LOSS_OUTS = [("sum", f32), f32, bf16, ("sum", f32)]


def _loss_ep(acc, res, w, tgt):
    x = res + acc
    D = x.shape[1]
    r = lax.rsqrt(jnp.mean(x * x, axis=1, keepdims=True) + EPS)
    xn = x * r
    e = xn * w - tgt
    loss = 0.5 * jnp.sum(jnp.mean(e * e, axis=1, keepdims=True), axis=0, keepdims=True)
    dy = e * (1.0 / D)
    dyw = dy * w
    dx = r * (dyw - xn * jnp.mean(dyw * xn, axis=1, keepdims=True))
    return jnp.broadcast_to(loss, (1, D)), dx, dx, jnp.sum(dy * xn, axis=0, keepdims=True)


def _ret_tables():
    H, C = RET_HEADS, RET_CHUNK
    lg = np.log1p(-np.exp2(-5.0 - np.arange(H, dtype=np.float32))).astype(np.float32)
    idx = np.arange(C, dtype=np.float32)
    diff = idx[:, None] - idx[None, :]
    causal = diff >= 0
    dm = np.where(causal[None], np.exp(lg[:, None, None] * np.where(causal, diff, 0.0)[None]), 0.0)
    qd = np.exp(lg[:, None] * (idx[None, :] + 1.0))
    kd = np.exp(lg[:, None] * (C - 1.0 - idx[None, :]))
    cg = np.exp(lg * C)
    tab = np.zeros((H, 4, C, HEAD_DIM), np.float32)
    tab[:, 0] = dm
    tab[:, 1] = qd[:, :, None]
    tab[:, 2] = kd[:, :, None]
    tab[:, 3] = cg[:, None, None]
    return jnp.asarray(tab)


def _rope_tables(T):
    half = HEAD_DIM // 2
    inv = ROPE_THETA ** (-jnp.arange(half, dtype=f32) / half)
    ang = jnp.arange(T, dtype=jnp.int32).astype(f32)[:, None] * inv[None, :]
    c, s = jnp.cos(ang), jnp.sin(ang)
    return jnp.concatenate([c, c], axis=1), jnp.concatenate([-s, s], axis=1)


def _rope(x, cos, sin):
    return x * cos + pltpu.roll(x, HEAD_DIM // 2, 1) * sin


def _unrope(y, cos, sin):
    return y * cos + pltpu.roll(y * sin, HEAD_DIM // 2, 1)


def _stack_heads(ref, H, f=None):
    parts = [ref[:, h * HEAD_DIM:(h + 1) * HEAD_DIM] for h in range(H)]
    return jnp.stack(parts if f is None else [f(a) for a in parts])


def _ret_fwd(p, cos, sin, tab, name):
    T = p.shape[0]
    C, H = RET_CHUNK, RET_HEADS
    N = T // C
    K = min(RET_STEP, N)
    NS = N // K
    scale = HEAD_DIM ** -0.5

    def body(q_ref, k_ref, v_ref, g_ref, c_ref, s_ref, t_ref, y_ref, o_ref, sp_ref, st):
        @pl.when(pl.program_id(0) == 0)
        def _():
            st[...] = jnp.zeros_like(st)

        dm, qd, kd, cg = t_ref[:, 0], t_ref[:, 1], t_ref[:, 2], t_ref[:, 3]
        S = st[...]
        for c in range(K):
            rows = pl.ds(c * C, C)
            cos_, sin_ = c_ref[rows, :], s_ref[rows, :]
            rot = lambda a: _rope(a, cos_, sin_)
            q = _stack_heads(q_ref.at[rows, :], H, rot)
            k = _stack_heads(k_ref.at[rows, :], H, rot) * scale
            v = _stack_heads(v_ref.at[rows, :], H)
            P = _dot(q, k, _NT3) * dm
            o = _dot(P, v, _NN3) + _dot(q * qd, S, _NN3)
            sp_ref[c] = S
            S = cg * S + _dot(k * kd, v, _TN3)
            r = lax.rsqrt(jnp.mean(o * o, axis=2, keepdims=True) + EPS)
            y = o * r * _silu(_stack_heads(g_ref.at[rows, :], H))
            for h in range(H):
                o_ref[rows, h * HEAD_DIM:(h + 1) * HEAD_DIM] = o[h]
                y_ref[rows, h * HEAD_DIM:(h + 1) * HEAD_DIM] = y[h].astype(bf16)
        st[...] = S

    wide = lambda blk: pl.BlockSpec((K * C, H * HEAD_DIM), lambda n: (n, blk))
    tbl = pl.BlockSpec((K * C, HEAD_DIM), lambda n: (n, 0))
    return pl.pallas_call(
        body, grid=(NS,),
        in_specs=[wide(0), wide(1), wide(2), wide(3), tbl, tbl,
                  pl.BlockSpec((H, 4, C, HEAD_DIM), lambda n: (0, 0, 0, 0))],
        out_specs=[wide(0), wide(0), pl.BlockSpec((K, H, HEAD_DIM, HEAD_DIM), lambda n: (n, 0, 0, 0))],
        out_shape=[jax.ShapeDtypeStruct((T, D_MODEL), bf16), jax.ShapeDtypeStruct((T, H * HEAD_DIM), f32),
                   jax.ShapeDtypeStruct((N, H, HEAD_DIM, HEAD_DIM), f32)],
        scratch_shapes=[pltpu.VMEM((H, HEAD_DIM, HEAD_DIM), f32)],
        compiler_params=_cparams(("arbitrary",)), name=name,
    )(p, p, p, p, cos, sin, tab)


def _ret_bwd(p, cos, sin, tab, o_raw, sprev, dmix, name):
    T = p.shape[0]
    C, H = RET_CHUNK, RET_HEADS
    N = T // C
    K = min(RET_STEP, N)
    NS = N // K
    scale = HEAD_DIM ** -0.5
    W = H * HEAD_DIM

    def body(q_ref, k_ref, v_ref, g_ref, c_ref, s_ref, t_ref, o_ref, sp_ref, dy_ref, d_ref, dst):
        @pl.when(pl.program_id(0) == 0)
        def _():
            dst[...] = jnp.zeros_like(dst)

        dm, qd, kd, cg = t_ref[:, 0], t_ref[:, 1], t_ref[:, 2], t_ref[:, 3]
        dS1 = dst[...]
        for c in reversed(range(K)):
            rows = pl.ds(c * C, C)
            cos_, sin_ = c_ref[rows, :], s_ref[rows, :]
            rot = lambda a: _rope(a, cos_, sin_)
            q = _stack_heads(q_ref.at[rows, :], H, rot)
            k = _stack_heads(k_ref.at[rows, :], H, rot) * scale
            v = _stack_heads(v_ref.at[rows, :], H)
            g = _stack_heads(g_ref.at[rows, :], H)
            S = sp_ref[c]
            o = _stack_heads(o_ref.at[rows, :], H)
            dy = _stack_heads(dy_ref.at[rows, :], H)
            r = lax.rsqrt(jnp.mean(o * o, axis=2, keepdims=True) + EPS)
            nrm = o * r
            dn = dy * _silu(g)
            dg = dy * nrm * _dsilu(g)
            do = r * (dn - nrm * jnp.mean(dn * nrm, axis=2, keepdims=True))
            P = _dot(q, k, _NT3) * dm
            dP = _dot(do, v, _NT3) * dm
            dq = _dot(dP, k, _NN3) + _dot(do, S, _NT3) * qd
            dk = (_dot(dP, q, _TN3) + _dot(v, dS1, _NT3) * kd) * scale
            dv = _dot(P, do, _TN3) + _dot(k * kd, dS1, _NN3)
            dS1 = cg * dS1 + _dot(q * qd, do, _TN3)
            for h in range(H):
                d_ref[rows, h * HEAD_DIM:(h + 1) * HEAD_DIM] = _unrope(dq[h], cos_, sin_).astype(bf16)
                d_ref[rows, W + h * HEAD_DIM:W + (h + 1) * HEAD_DIM] = _unrope(dk[h], cos_, sin_).astype(bf16)
                d_ref[rows, 2 * W + h * HEAD_DIM:2 * W + (h + 1) * HEAD_DIM] = dv[h].astype(bf16)
                d_ref[rows, 3 * W + h * HEAD_DIM:3 * W + (h + 1) * HEAD_DIM] = dg[h].astype(bf16)
        dst[...] = dS1

    rev = lambda blk: pl.BlockSpec((K * C, W), lambda n: (NS - 1 - n, blk))
    tbl = pl.BlockSpec((K * C, HEAD_DIM), lambda n: (NS - 1 - n, 0))
    return pl.pallas_call(
        body, grid=(NS,),
        in_specs=[rev(0), rev(1), rev(2), rev(3), tbl, tbl,
                  pl.BlockSpec((H, 4, C, HEAD_DIM), lambda n: (0, 0, 0, 0)), rev(0),
                  pl.BlockSpec((K, H, HEAD_DIM, HEAD_DIM), lambda n: (NS - 1 - n, 0, 0, 0)), rev(0)],
        out_specs=pl.BlockSpec((K * C, 4 * W), lambda n: (NS - 1 - n, 0)),
        out_shape=jax.ShapeDtypeStruct((T, 6 * W), bf16),
        scratch_shapes=[pltpu.VMEM((H, HEAD_DIM, HEAD_DIM), f32)],
        compiler_params=_cparams(("arbitrary",)), name=name,
    )(p, p, p, p, cos, sin, tab, o_raw, sprev, dmix)


CONV_K = 4
CONV_W = 512
PAD = 8
SUB_R = 64


def _conv_fwd(x, col_off, w, b, act, name):
    T = x.shape[0]
    C = w.shape[1]
    G = C // CONV_W
    tt = min(512, T)
    NT = T // tt
    has_b = b is not None

    def body(*refs):
        if has_b:
            x_ref, w_ref, b_ref, y_ref, pad = refs
        else:
            x_ref, w_ref, y_ref, pad = refs
        t = pl.program_id(1)

        @pl.when(t == 0)
        def _():
            pad[pl.ds(0, PAD), :] = jnp.zeros((PAD, CONV_W), f32)

        pad[pl.ds(PAD, tt), :] = x_ref[...]
        for g in range(CONV_W // 128):
            ls = slice(g * 128, (g + 1) * 128)
            wv = w_ref[:, ls]
            for c in range(tt // SUB_R):
                r0 = c * SUB_R
                y = wv[0:1, :] * pad[pl.ds(PAD - 3 + r0, SUB_R), ls]
                for kk in range(1, CONV_K):
                    y = y + wv[kk:kk + 1, :] * pad[pl.ds(PAD - 3 + kk + r0, SUB_R), ls]
                if has_b:
                    y = y + b_ref[:, ls]
                y_ref[pl.ds(r0, SUB_R), ls] = _silu(y) if act else y
        tail = pad[pl.ds(tt, PAD), :]
        pad[pl.ds(0, PAD), :] = tail

    in_specs = [pl.BlockSpec((tt, CONV_W), lambda g, t: (t, col_off + g)),
                pl.BlockSpec((CONV_K, CONV_W), lambda g, t: (0, g))]
    args = [x, w]
    if has_b:
        in_specs.append(pl.BlockSpec((1, CONV_W), lambda g, t: (0, g)))
        args.append(b)
    return pl.pallas_call(
        body, grid=(G, NT), in_specs=in_specs,
        out_specs=pl.BlockSpec((tt, CONV_W), lambda g, t: (t, g)),
        out_shape=jax.ShapeDtypeStruct((T, C), f32),
        scratch_shapes=[pltpu.VMEM((tt + PAD, CONV_W), f32)],
        compiler_params=_cparams(("parallel", "arbitrary")), name=name,
    )(*args)


def _conv_bwd(x, col_off, w, b, act, dout, dp, name):
    T = x.shape[0]
    C = w.shape[1]
    G = C // CONV_W
    tt = min(512, T)
    NT = T // tt
    has_b = b is not None

    def body(*refs):
        if has_b:
            x_ref, xp_ref, w_ref, b_ref, d_ref, dp_in, dx_ref, dw_ref, db_ref, pad, dpad = refs
        else:
            x_ref, xp_ref, w_ref, d_ref, dp_in, dx_ref, dw_ref, db_ref, pad, dpad = refs
        t = pl.program_id(1)
        first_tile = t == NT - 1

        @pl.when(t == 0)
        def _():
            dpad[pl.ds(tt, PAD), :] = jnp.zeros((PAD, CONV_W), f32)
            dw_ref[...] = jnp.zeros_like(dw_ref)
            db_ref[...] = jnp.zeros_like(db_ref)

        pad[pl.ds(0, PAD), :] = jnp.where(first_tile, 0.0, xp_ref[...])
        pad[pl.ds(PAD, tt), :] = x_ref[...]
        fold = lambda v: v.reshape(SUB_R // 8, 8, 128).sum(axis=0)
        for g in range(CONV_W // 128):
            ls = slice(g * 128, (g + 1) * 128)
            wv = w_ref[:, ls]
            acc = [jnp.zeros((8, 128), f32) for _ in range(CONV_K + 1)]
            for c in reversed(range(tt // SUB_R)):
                r0 = c * SUB_R
                xs = [pad[pl.ds(PAD - 3 + kk + r0, SUB_R), ls] for kk in range(CONV_K)]
                dy = d_ref[pl.ds(r0, SUB_R), ls]
                if act:
                    y = wv[0:1, :] * xs[0]
                    for kk in range(1, CONV_K):
                        y = y + wv[kk:kk + 1, :] * xs[kk]
                    if has_b:
                        y = y + b_ref[:, ls]
                    dy = dy * _dsilu(y)
                dpad[pl.ds(r0, SUB_R), ls] = dy
                dx = wv[3:4, :] * dy
                for j in range(1, CONV_K):
                    dx = dx + wv[3 - j:4 - j, :] * dpad[pl.ds(r0 + j, SUB_R), ls]
                dx_ref[pl.ds(r0, SUB_R), ls] = dx.astype(bf16)
                for kk in range(CONV_K):
                    acc[kk] = acc[kk] + fold(dy * xs[kk])
                acc[CONV_K] = acc[CONV_K] + fold(dy)
            for kk in range(CONV_K):
                dw_ref[kk:kk + 1, ls] += jnp.sum(acc[kk], axis=0, keepdims=True)
            db_ref[:, ls] += jnp.sum(acc[CONV_K], axis=0, keepdims=True)
        head = dpad[pl.ds(0, PAD), :]
        dpad[pl.ds(tt, PAD), :] = head

    rows8 = tt // PAD
    in_specs = [pl.BlockSpec((tt, CONV_W), lambda g, t: (NT - 1 - t, col_off + g)),
                pl.BlockSpec((PAD, CONV_W), lambda g, t: (jnp.maximum((NT - 1 - t) * rows8 - 1, 0), col_off + g)),
                pl.BlockSpec((CONV_K, CONV_W), lambda g, t: (0, g))]
    args = [x, x, w]
    if has_b:
        in_specs.append(pl.BlockSpec((1, CONV_W), lambda g, t: (0, g)))
        args.append(b)
    in_specs += [pl.BlockSpec((tt, CONV_W), lambda g, t: (NT - 1 - t, g)), pl.BlockSpec(memory_space=pl.ANY)]
    args += [dout, dp]
    return pl.pallas_call(
        body, grid=(G, NT), in_specs=in_specs,
        out_specs=[pl.BlockSpec((tt, CONV_W), lambda g, t: (NT - 1 - t, col_off + g)),
                   pl.BlockSpec((CONV_K, CONV_W), lambda g, t: (0, g)),
                   pl.BlockSpec((1, CONV_W), lambda g, t: (0, g))],
        out_shape=[jax.ShapeDtypeStruct(dp.shape, dp.dtype), jax.ShapeDtypeStruct((CONV_K, C), f32),
                   jax.ShapeDtypeStruct((1, C), f32)],
        input_output_aliases={len(args) - 1: 0},
        scratch_shapes=[pltpu.VMEM((tt + PAD, CONV_W), f32), pltpu.VMEM((tt + PAD, CONV_W), f32)],
        compiler_params=_cparams(("parallel", "arbitrary")), name=name,
    )(*args)


def _lru_gates(xc, wr, wi, br, bi, lam):
    r = _sigmoid(_dot(xc, wr, _NN) + br)
    i = _sigmoid(_dot(xc, wi, _NN) + bi)
    sp = _softplus(-lam)
    a = jnp.exp(-LRU_C * r * sp)
    mult = jnp.sqrt(1.0 - a * a)
    return r, i, sp, a, mult


def _lru_fwd(xc, p, y_off, wr, wi, br, bi, lam, mix, name):
    T = xc.shape[0]
    G = LRU_WIDTH // 128
    tt = min(512, T)
    NT = T // tt

    def body(x_ref, y_ref, wr_ref, wi_ref, br_ref, bi_ref, l_ref, mix_in, o_ref, h_ref, hc):
        t = pl.program_id(1)

        @pl.when(t == 0)
        def _():
            hc[...] = jnp.zeros_like(hc)

        x = x_ref[...]
        r, i, sp, a, mult = _lru_gates(x, wr_ref[...], wi_ref[...], br_ref[...], bi_ref[...], l_ref[...])
        row = lax.broadcasted_iota(jnp.int32, (tt, 128), 0)
        mult = jnp.where((row == 0) & (t == 0), 1.0, mult)
        U = x * i * mult
        A = a
        d = 1
        while d < tt:
            keep = row >= d
            Ush = jnp.where(keep, pltpu.roll(U, d, 0), 0.0)
            Ash = jnp.where(keep, pltpu.roll(A, d, 0), 1.0)
            U = A * Ush + U
            A = A * Ash
            d *= 2
        h = U + A * hc[0:1, :]
        h_ref[...] = h
        hc[...] = jnp.broadcast_to(h[tt - 1:tt, :], hc.shape)
        o_ref[...] = (h * _gelu(y_ref[...])).astype(bf16)

    tile = pl.BlockSpec((tt, 128), lambda g, t: (t, g))
    vec = pl.BlockSpec((1, 128), lambda g, t: (0, g))
    wsp = pl.BlockSpec((128, 128), lambda g, t: (g, g))
    return pl.pallas_call(
        body, grid=(G, NT),
        in_specs=[tile, pl.BlockSpec((tt, 128), lambda g, t: (t, y_off + g)), wsp, wsp, vec, vec, vec,
                  pl.BlockSpec(memory_space=pl.ANY)],
        out_specs=[pl.BlockSpec((tt, 128), lambda g, t: (t, G + g)), tile],
        out_shape=[jax.ShapeDtypeStruct(mix.shape, mix.dtype), jax.ShapeDtypeStruct((T, LRU_WIDTH), f32)],
        input_output_aliases={7: 0},
        scratch_shapes=[pltpu.VMEM((8, 128), f32)],
        compiler_params=_cparams(("parallel", "arbitrary")), name=name,
    )(xc, p, wr, wi, br, bi, lam, mix)


def _lru_bwd(xc, p, y_off, wr, wi, br, bi, lam, hs, dmix, d_off, dp, name):
    T = xc.shape[0]
    G = LRU_WIDTH // 128
    tt = min(512, T)
    NT = T // tt

    def body(x_ref, y_ref, wr_ref, wi_ref, br_ref, bi_ref, l_ref, h_ref, hp_ref, do_ref, dp_in,
             dx_ref, dy_ref, dwr_ref, dwi_ref, dbr_ref, dbi_ref, dl_ref, lc, an):
        t = pl.program_id(1)
        first_tile = t == NT - 1

        @pl.when(t == 0)
        def _():
            lc[...] = jnp.zeros_like(lc)
            an[...] = jnp.zeros_like(an)
            dwr_ref[...] = jnp.zeros_like(dwr_ref)
            dwi_ref[...] = jnp.zeros_like(dwi_ref)
            dbr_ref[...] = jnp.zeros_like(dbr_ref)
            dbi_ref[...] = jnp.zeros_like(dbi_ref)
            dl_ref[...] = jnp.zeros_like(dl_ref)

        x = x_ref[...]
        y = y_ref[...]
        wr, wi, lam_ = wr_ref[...], wi_ref[...], l_ref[...]
        r, i, sp, a, mult_raw = _lru_gates(x, wr, wi, br_ref[...], bi_ref[...], lam_)
        row = lax.broadcasted_iota(jnp.int32, (tt, 128), 0)
        t0 = (row == 0) & first_tile
        mult = jnp.where(t0, 1.0, mult_raw)
        h = h_ref[...]
        do = do_ref[...]
        dh = do * _gelu(y)
        dy_ref[...] = (do * h * _dgelu(y)).astype(bf16)
        B = jnp.where(row == tt - 1, an[0:1, :], pltpu.roll(a, tt - 1, 0))
        L = dh
        d = 1
        while d < tt:
            keep = row < tt - d
            Lsh = jnp.where(keep, pltpu.roll(L, tt - d, 0), 0.0)
            Bsh = jnp.where(keep, pltpu.roll(B, tt - d, 0), 1.0)
            L = L + B * Lsh
            B = B * Bsh
            d *= 2
        L = L + B * lc[0:1, :]
        lc[...] = jnp.broadcast_to(L[0:1, :], lc.shape)
        an[...] = jnp.broadcast_to(a[0:1, :], an.shape)
        hprev = jnp.where(first_tile, 0.0, hp_ref[...])[PAD - 1:PAD, :]
        hm1 = jnp.where(row == 0, hprev, pltpu.roll(h, 1, 0))
        da = L * hm1
        dxc = L * i * mult
        di = L * x * mult
        dmult = jnp.where(t0, 0.0, L * x * i)
        da = da - jnp.where(t0, 0.0, dmult * a / mult_raw)
        dlog_a = da * a
        dr = dlog_a * (-LRU_C) * sp
        dsp = jnp.sum(dlog_a * (-LRU_C) * r, axis=0, keepdims=True)
        dpr = dr * r * (1.0 - r)
        dpi = di * i * (1.0 - i)
        dx_ref[...] = dxc + _dot(dpr, wr, _NT) + _dot(dpi, wi, _NT)
        dwr_ref[0] += _dot(x, dpr, _TN)
        dwi_ref[0] += _dot(x, dpi, _TN)
        dbr_ref[...] += jnp.sum(dpr, axis=0, keepdims=True)
        dbi_ref[...] += jnp.sum(dpi, axis=0, keepdims=True)
        dl_ref[...] += dsp * (-_sigmoid(-lam_))

    rows8 = tt // PAD
    tile = pl.BlockSpec((tt, 128), lambda g, t: (NT - 1 - t, g))
    vec = pl.BlockSpec((1, 128), lambda g, t: (0, g))
    wsp = pl.BlockSpec((128, 128), lambda g, t: (g, g))
    wout = pl.BlockSpec((1, 128, 128), lambda g, t: (g, 0, 0))
    return pl.pallas_call(
        body, grid=(G, NT),
        in_specs=[tile, pl.BlockSpec((tt, 128), lambda g, t: (NT - 1 - t, y_off + g)), wsp, wsp, vec, vec, vec, tile,
                  pl.BlockSpec((PAD, 128), lambda g, t: (jnp.maximum((NT - 1 - t) * rows8 - 1, 0), g)),
                  pl.BlockSpec((tt, 128), lambda g, t: (NT - 1 - t, d_off + g)), pl.BlockSpec(memory_space=pl.ANY)],
        out_specs=[tile, pl.BlockSpec((tt, 128), lambda g, t: (NT - 1 - t, y_off + g)), wout, wout, vec, vec, vec],
        out_shape=[jax.ShapeDtypeStruct((T, LRU_WIDTH), f32), jax.ShapeDtypeStruct(dp.shape, dp.dtype),
                   jax.ShapeDtypeStruct((G, 128, 128), f32), jax.ShapeDtypeStruct((G, 128, 128), f32),
                   jax.ShapeDtypeStruct((1, LRU_WIDTH), f32), jax.ShapeDtypeStruct((1, LRU_WIDTH), f32),
                   jax.ShapeDtypeStruct((1, LRU_WIDTH), f32)],
        input_output_aliases={10: 1},
        scratch_shapes=[pltpu.VMEM((8, 128), f32), pltpu.VMEM((8, 128), f32)],
        compiler_params=_cparams(("parallel", "arbitrary")), name=name,
    )(xc, p, wr, wi, br, bi, lam, hs, hs, dmix, dp)


_NN3 = (((2,), (1,)), ((0,), (0,)))
_NT3 = (((2,), (2,)), ((0,), (0,)))
_TN3 = (((1,), (1,)), ((0,), (0,)))


def _pairs(ref, K):
    C = GDN_CHUNK
    return jnp.stack([ref[c * C:(c + 1) * C, h * HEAD_DIM:(h + 1) * HEAD_DIM] for c in range(K) for h in range(GDN_HEADS)])


def _put_pairs(ref, val, K, col=0):
    C, H = GDN_CHUNK, GDN_HEADS
    for c in range(K):
        for h in range(H):
            ref[c * C:(c + 1) * C, col + h * HEAD_DIM:col + (h + 1) * HEAD_DIM] = val[c * H + h].astype(ref.dtype)


def _rowsum(x):
    H, C, L = x.shape
    return _dot(x.reshape(H * C, L), jnp.ones((L, HEAD_DIM), f32), _NN).reshape(H, C, HEAD_DIM)


def _gdn_pre(qr, kr, v, ba, alog, dtb):
    C, H = GDN_CHUNK, GDN_HEADS
    B = qr.shape[0]
    K = B // H
    lane = lax.broadcasted_iota(jnp.int32, (C, 128), 1)
    lane3 = lax.broadcasted_iota(jnp.int32, (B, C, 128), 2)
    ri = lax.broadcasted_iota(jnp.int32, (C, C), 0)
    ci = lax.broadcasted_iota(jnp.int32, (C, C), 1)
    rowc = lax.broadcasted_iota(jnp.int32, (C, 1), 0)
    col = lambda m, j: jnp.sum(jnp.where(lane == j, m, 0.0), axis=1, keepdims=True)
    ea = jnp.exp(alog)
    tri = (ri >= ci).astype(f32)
    g_all, beta_cols, G_cols = [], [], []
    for c in range(K):
        ba_c = ba[c * C:(c + 1) * C]
        g_c = -ea * _softplus(ba_c + dtb)
        G_c = _dot01(tri, g_c, _NN)
        s_c = _sigmoid(ba_c)
        g_all.append(g_c)
        beta_cols += [col(s_c, h) for h in range(H)]
        G_cols += [col(G_c, H + h) for h in range(H)]
    wide = lambda c: jnp.broadcast_to(c, (B, C, 128))
    beta = wide(jnp.stack(beta_cols))
    Gc = jnp.stack(G_cols)
    rq = lax.rsqrt(_rowsum(qr * qr) + EPS)
    rk = lax.rsqrt(_rowsum(kr * kr) + EPS)
    qh, kn = qr * rq, kr * rk
    qn = qh * (HEAD_DIM ** -0.5)
    Grow = _dot01(jnp.ones((B, C, 128), f32), jnp.where(lane3 == 0, Gc, 0.0), _NT3)
    incl = ri >= ci
    Di = jnp.where(incl, jnp.exp(jnp.where(incl, Gc - Grow, 0.0)), 0.0)
    Ds = jnp.where(ri > ci, Di, 0.0)
    Gl = jnp.sum(jnp.where(rowc == C - 1, Gc, 0.0), axis=1, keepdims=True)
    eG = wide(jnp.exp(Gc))
    eGl = wide(jnp.exp(Gl - Gc))
    cd = jnp.exp(Gl)
    kb = kn * beta
    vb = v * beta
    Lm = _dot(kb, kn, _NT3) * Ds
    kbg = kb * eG
    QK = _dot(qn, kn, _NT3) * Di
    qg = qn * eG
    kg = kn * eGl
    return dict(beta=beta, g_all=g_all, rq=rq, rk=rk, qh=qh, kn=kn, qn=qn, Di=Di, Ds=Ds, eG=eG, eGl=eGl, cd=cd,
                kb=kb, vb=vb, Lm=Lm, kbg=kbg, QK=QK, qg=qg, kg=kg, lane=lane, ri=ri, ci=ci, rowc=rowc, ea=ea)


def _unit_lower_inverse(Lm):
    C = Lm.shape[-1]
    ri = lax.broadcasted_iota(jnp.int32, (C, C), 0)
    ci = lax.broadcasted_iota(jnp.int32, (C, C), 1)
    same = lambda s: (ri // s) == (ci // s)
    Xd = jnp.where(same(8), -Lm, 0.0)
    Tinv = (ri == ci).astype(f32) + Xd
    Pw = Xd
    for _ in range(2):
        Pw = _dot(Pw, Pw, _NN3)
        Tinv = Tinv + _dot(Tinv, Pw, _NN3)
    for s in (8, 16, 32):
        off = jnp.where(same(2 * s) & jnp.logical_not(same(s)), Lm, 0.0)
        Tinv = Tinv - _dot(_dot(Tinv, off, _NN3), Tinv, _NN3)
    return Tinv


def _gdn_specs(T, rev):
    C = GDN_CHUNK
    H = GDN_HEADS
    K = min(GDN_STEP, T // C)
    NS = T // (C * K)
    nn = (lambda n: NS - 1 - n) if rev else (lambda n: n)
    wide = lambda blk: pl.BlockSpec((K * C, H * HEAD_DIM), lambda n: (nn(n), blk))
    one = lambda off: pl.BlockSpec((K * C, HEAD_DIM), lambda n: (nn(n), off))
    vec = pl.BlockSpec((1, 128), lambda n: (0, 0))
    st = lambda rows: pl.BlockSpec((K, H, rows, rows), lambda n: (nn(n), 0, 0, 0))
    return K, NS, wide, one, vec, st


def _gdn_fwd(qkv, p, alog, dtb, nw, name):
    T = qkv.shape[0]
    C, H = GDN_CHUNK, GDN_HEADS
    N = T // C
    K, NS, wide, one, vec, st_spec = _gdn_specs(T, False)

    def body(q_ref, k_ref, v_ref, z_ref, ba_ref, al_ref, dt_ref, nw_ref, y_ref, sp_ref, ti_ref, vn_ref, o_ref, st):
        @pl.when(pl.program_id(0) == 0)
        def _():
            st[...] = jnp.zeros_like(st)

        f = _gdn_pre(_pairs(q_ref, K), _pairs(k_ref, K), _pairs(v_ref, K), ba_ref[...], al_ref[...], dt_ref[...])
        Tinv = _unit_lower_inverse(f["Lm"])
        ti_ref[...] = Tinv.reshape(K, H, C, C).astype(bf16)
        w = _dot(Tinv, f["kbg"], _NN3)
        u = _dot(Tinv, f["vb"], _NN3)
        S = st[...]
        vns, os_ = [], []
        for c in range(K):
            sl = slice(c * H, (c + 1) * H)
            sp_ref[c] = S
            vn_c = u[sl] - _dot(w[sl], S, _NN3)
            os_.append(_dot(f["qg"][sl], S, _NN3) + _dot(f["QK"][sl], vn_c, _NN3))
            S = S * f["cd"][sl] + _dot(f["kg"][sl], vn_c, _TN3)
            vns.append(vn_c)
        st[...] = S
        vn, o = jnp.concatenate(vns), jnp.concatenate(os_)
        r = lax.rsqrt(_rowsum(o * o) * (1.0 / HEAD_DIM) + EPS)
        _put_pairs(y_ref, o * r * nw_ref[...] * _silu(_pairs(z_ref, K)), K)
        _put_pairs(vn_ref, vn, K)
        _put_pairs(o_ref, o, K)

    wide_f32 = jax.ShapeDtypeStruct((T, H * HEAD_DIM), f32)
    return pl.pallas_call(
        body, grid=(NS,),
        in_specs=[wide(0), wide(1), wide(2), wide(3), one(4 * H), vec, vec, vec],
        out_specs=[wide(0), st_spec(HEAD_DIM), st_spec(C), wide(0), wide(0)],
        out_shape=[jax.ShapeDtypeStruct((T, H * HEAD_DIM), bf16), jax.ShapeDtypeStruct((N, H, HEAD_DIM, HEAD_DIM), f32),
                   jax.ShapeDtypeStruct((N, H, C, C), bf16), jax.ShapeDtypeStruct((T, H * HEAD_DIM), bf16), wide_f32],
        scratch_shapes=[pltpu.VMEM((H, HEAD_DIM, HEAD_DIM), f32)],
        compiler_params=_cparams(("arbitrary",)), name=name,
    )(qkv, qkv, qkv, p, p, alog, dtb, nw)


def _gdn_bwd(qkv, p, alog, dtb, nw, sprev, tinv, vn_all, o_all, dy_all, name):
    T = qkv.shape[0]
    C, H = GDN_CHUNK, GDN_HEADS
    N = T // C
    K, NS, wide, one, vec, st_spec = _gdn_specs(T, True)
    rs = lambda m: jnp.sum(m, axis=2, keepdims=True)

    def body(q_ref, k_ref, v_ref, z_ref, ba_ref, al_ref, dt_ref, nw_ref, sp_ref, ti_ref, vn_ref, o_ref, dy_ref,
             dqkv_ref, dz_ref, dba_ref, dal_ref, ddt_ref, dnw_ref, dst):
        @pl.when(pl.program_id(0) == 0)
        def _():
            dst[...] = jnp.zeros_like(dst)
            dal_ref[...] = jnp.zeros_like(dal_ref)
            ddt_ref[...] = jnp.zeros_like(ddt_ref)
            dnw_ref[...] = jnp.zeros_like(dnw_ref)

        ba, dtb_, nwv = ba_ref[...], dt_ref[...], nw_ref[...]
        v = _pairs(v_ref, K)
        f = _gdn_pre(_pairs(q_ref, K), _pairs(k_ref, K), v, ba, al_ref[...], dtb_)
        beta, kn, qn, kb, vb, kbg = f["beta"], f["kn"], f["qn"], f["kb"], f["vb"], f["kbg"]
        eG, eGl, cd, Di, Ds, QK, qg, kg = f["eG"], f["eGl"], f["cd"], f["Di"], f["Ds"], f["QK"], f["qg"], f["kg"]
        lane, ri, ci, rowc = f["lane"], f["ri"], f["ci"], f["rowc"]
        Tinv = ti_ref[...].reshape(K * H, C, C)
        S = sp_ref[...].reshape(K * H, HEAD_DIM, HEAD_DIM)
        w_ = _dot(Tinv, kbg, _NN3)
        vn, o = _pairs(vn_ref, K), _pairs(o_ref, K)
        z, dy = _pairs(z_ref, K), _pairs(dy_ref, K)
        r = lax.rsqrt(_rowsum(o * o) * (1.0 / HEAD_DIM) + EPS)
        nrm = o * r
        sz = _silu(z)
        dn = dy * nwv * sz
        _put_pairs(dz_ref, dy * nrm * nwv * _dsilu(z), K)
        dnw_ref[...] += jnp.sum(jnp.sum(dy * nrm * sz, axis=0), axis=0, keepdims=True)
        do = r * (dn - nrm * (_rowsum(dn * nrm) * (1.0 / HEAD_DIM)))
        dvn_do = _dot(QK, do, _TN3)
        dS_do = _dot(qg, do, _TN3)
        dqg = _dot(do, S, _NT3)
        dQK = _dot(do, vn, _NT3)
        dS = dst[...]
        dS1s, dvns = [None] * K, [None] * K
        for c in reversed(range(K)):
            sl = slice(c * H, (c + 1) * H)
            dS1s[c] = dS
            dvns[c] = _dot(kg[sl], dS, _NN3) + dvn_do[sl]
            dS = cd[sl] * dS + dS_do[sl] - _dot(w_[sl], dvns[c], _TN3)
        dst[...] = dS
        dS1, dvn = jnp.concatenate(dS1s), jnp.concatenate(dvns)
        dcd = jnp.sum(jnp.sum(S * dS1, axis=2, keepdims=True), axis=1, keepdims=True)
        dkg = _dot(vn, dS1, _NT3)
        dw = -_dot(dvn, S, _NT3)
        dqn = dqg * eG
        dkn = dkg * eGl
        deGl = rs(dkg * kn)
        dQKr = dQK * Di
        E = dQK * QK
        dqn = dqn + _dot(dQKr, kn, _NN3)
        dkn = dkn + _dot(dQKr, qn, _TN3)
        dT = _dot(dvn, vb, _NT3) + _dot(dw, kbg, _NT3)
        dvb = _dot(Tinv, dvn, _TN3)
        dkbg = _dot(Tinv, dw, _TN3)
        dkb = dkbg * eG
        deG = rs(dqg * qn + dkbg * kb)
        dL = -_dot(_dot(Tinv, dT, _TN3), Tinv, _NT3)
        dKK = dL * Ds
        E = E + dL * f["Lm"]
        dkb = dkb + _dot(dKK, kn, _NN3)
        dkn = dkn + _dot(dKK, kb, _TN3) + dkb * beta
        dbeta = rs(dkb * kn + dvb * v)
        _put_pairs(dqkv_ref, dvb * beta, K, 2 * H * HEAD_DIM)
        dG = rs(E) - rs(jnp.swapaxes(E, 1, 2)) + deG * eG - deGl * eGl
        dGl = jnp.sum(deGl * eGl, axis=1, keepdims=True) + dcd * cd
        dG = dG + jnp.where(rowc == C - 1, dGl, 0.0)
        qh = f["qh"]
        _put_pairs(dqkv_ref, (HEAD_DIM ** -0.5) * f["rq"] * (dqn - qh * _rowsum(dqn * qh)), K)
        _put_pairs(dqkv_ref, f["rk"] * (dkn - kn * _rowsum(dkn * kn)), K, H * HEAD_DIM)
        db = dbeta * beta * (1.0 - beta)
        triu = (ri <= ci).astype(f32)
        for c in range(K):
            db_all = jnp.where(lane == 0, db[c * H], 0.0)
            dG_all = jnp.where(lane == H, dG[c * H], 0.0)
            for h in range(1, H):
                db_all = db_all + jnp.where(lane == h, db[c * H + h], 0.0)
                dG_all = dG_all + jnp.where(lane == H + h, dG[c * H + h], 0.0)
            dg_all = _dot01(triu, dG_all, _NN)
            da_all = dg_all * (-f["ea"]) * _sigmoid(ba[c * C:(c + 1) * C] + dtb_)
            dba_ref[c * C:(c + 1) * C, :] = (db_all + da_all).astype(bf16)
            ddt_ref[...] += jnp.sum(da_all, axis=0, keepdims=True)
            dal_ref[...] += jnp.sum(dg_all * f["g_all"][c], axis=0, keepdims=True)

    small = jax.ShapeDtypeStruct((1, 128), f32)
    return pl.pallas_call(
        body, grid=(NS,),
        in_specs=[wide(0), wide(1), wide(2), wide(3), one(4 * H), vec, vec, vec, st_spec(HEAD_DIM), st_spec(C),
                  wide(0), wide(0), wide(0)],
        out_specs=[pl.BlockSpec((K * C, 3 * H * HEAD_DIM), lambda n: (NS - 1 - n, 0)), wide(3), one(0), vec, vec, vec],
        out_shape=[jax.ShapeDtypeStruct((T, 3 * H * HEAD_DIM), f32), jax.ShapeDtypeStruct((T, ODD_PAD), bf16),
                   jax.ShapeDtypeStruct((T, 128), bf16), small, small, small],
        scratch_shapes=[pltpu.VMEM((H, HEAD_DIM, HEAD_DIM), f32)],
        compiler_params=_cparams(("arbitrary",)), name=name,
    )(qkv, qkv, qkv, p, p, alog, dtb, nw, sprev, tinv, vn_all, o_all, dy_all)


def _lanes_from(x, s):
    return x if s % 128 == 0 else pltpu.roll(x, (128 - s) % 128, 1)


def _odd_assemble(g, name):
    R = g.shape[1]
    tr = min(256, R)
    n_blk = ODD_SHARD_PAD // 128

    def body(g_ref, o_ref):
        lane = lax.broadcasted_iota(jnp.int32, (tr, 128), 1)
        blk = lambda d, m: g_ref[d, :, m * 128:(m + 1) * 128]
        for gb in range(ODD_PAD // 128):
            c0 = 128 * gb
            if c0 >= ODD_IN:
                o_ref[:, c0:c0 + 128] = jnp.zeros((tr, 128), g.dtype)
                continue
            d0 = c0 // ODD_SHARD
            m0, sh = divmod(c0 - ODD_SHARD * d0, 128)
            take = min(128, ODD_SHARD * (d0 + 1) - c0)
            p = _lanes_from(blk(d0, m0), sh)
            if sh and m0 + 1 < n_blk:
                p = jnp.where(lane < 128 - sh, p, _lanes_from(blk(d0, m0 + 1), sh))
            if take < 128:
                nxt = pltpu.roll(blk(d0 + 1, 0), take, 1) if d0 + 1 < N_DEV else jnp.zeros((tr, 128), g.dtype)
                p = jnp.where(lane < take, p, nxt)
            o_ref[:, c0:c0 + 128] = p

    return pl.pallas_call(
        body, grid=(R // tr,),
        in_specs=[pl.BlockSpec((N_DEV, tr, ODD_SHARD_PAD), lambda i: (0, i, 0))],
        out_specs=pl.BlockSpec((tr, ODD_PAD), lambda i: (i, 0)),
        out_shape=jax.ShapeDtypeStruct((R, ODD_PAD), g.dtype),
        compiler_params=_cparams(("parallel",)), name=name,
    )(g)


def _odd_split(w, name):
    R = w.shape[0]
    tr = min(256, R)

    def body(w_ref, o_ref):
        lane = lax.broadcasted_iota(jnp.int32, (tr, 128), 1)
        blk = lambda gb: w_ref[:, gb * 128:(gb + 1) * 128]
        for d in range(N_DEV):
            for m in range(ODD_SHARD_PAD // 128):
                g0, sh = divmod(ODD_SHARD * d + 128 * m, 128)
                p = _lanes_from(blk(g0), sh)
                if sh and g0 + 1 < ODD_PAD // 128:
                    p = jnp.where(lane < 128 - sh, p, _lanes_from(blk(g0 + 1), sh))
                real = ODD_SHARD - 128 * m
                if real < 128:
                    p = jnp.where(lane < real, p, jnp.zeros_like(p))
                o_ref[d, :, m * 128:(m + 1) * 128] = p

    return pl.pallas_call(
        body, grid=(R // tr,),
        in_specs=[pl.BlockSpec((tr, ODD_PAD), lambda i: (i, 0))],
        out_specs=pl.BlockSpec((N_DEV, tr, ODD_SHARD_PAD), lambda i: (0, i, 0)),
        out_shape=jax.ShapeDtypeStruct((N_DEV, R, ODD_SHARD_PAD), w.dtype),
        compiler_params=_cparams(("parallel",)), name=name,
    )(w)


def _adamw(w, gs, m, v, name, layer=None, prev=None):
    R, Cc = w.shape[-2:]
    S = gs.shape[0]
    tr = R
    if S * R * Cc * 4 > (4 << 20):
        for cand in (256, 128, 64, 32, 16, 8):
            if R % cand == 0 and R > cand:
                tr = cand
                break
    c1 = 1.0 - ADAM_B1 ** ADAM_STEP
    c2 = 1.0 - ADAM_B2 ** ADAM_STEP

    def body(w_ref, g_ref, m_ref, v_ref, *rest):
        go_ref, d_ref, mo_ref, vo_ref = rest[-4:]
        g = g_ref[0].astype(f32)
        for s in range(1, S):
            g = g + g_ref[s].astype(f32)
        mn = ADAM_B1 * m_ref[...] + (1.0 - ADAM_B1) * g
        vn = ADAM_B2 * v_ref[...] + (1.0 - ADAM_B2) * (g * g)
        go_ref[...] = g
        mo_ref[...] = mn
        vo_ref[...] = vn
        d_ref[...] = -ADAM_LR * ((mn / c1) / (jnp.sqrt(vn / c2) + ADAM_EPS) + ADAM_WD * w_ref[...])

    tc = Cc if gs.shape[-1] == Cc else 128
    if layer is None:
        blk = pl.BlockSpec((tr, tc), lambda i, j: (i, j))
    else:
        blk = pl.BlockSpec((None, tr, tc), lambda i, j: (layer, i, j))
    out = jax.ShapeDtypeStruct(w.shape, f32)
    carried = [] if prev is None else list(prev)
    return pl.pallas_call(
        body, grid=(R // tr, pl.cdiv(Cc, tc)),
        in_specs=[blk, pl.BlockSpec((S, tr, tc), lambda i, j: (0, i, j)), blk, blk]
        + [pl.BlockSpec(memory_space=pl.ANY)] * len(carried),
        out_specs=[blk] * 4, out_shape=[out] * 4,
        input_output_aliases={4 + j: j for j in range(len(carried))},
        compiler_params=_cparams(("parallel", "parallel")), name=name,
    )(w, gs, m, v, *carried)


def _me():
    x, y, c = lax.axis_index("x"), lax.axis_index("y"), lax.axis_index("c")
    return x, y, c, 4 * x + 2 * y + c


def _peer(k):
    x, y, c, _ = _me()
    px = 1 - x if k & 4 else x
    py = 1 - y if k & 2 else y
    pc = 1 - c if k & 1 else c
    return (px, py, pc), 4 * px + 2 * py + pc


_HBM = pl.BlockSpec(memory_space=pltpu.HBM)
_SEM = pl.BlockSpec(memory_space=pltpu.SEMAPHORE)
_EFFECT = pltpu.SideEffectType.DATAFLOW_SIDE_EFFECTING


def _copy(src, land, ssem, rsem, k, blocked, landing_slot_of_peer):
    pid, pidx = _peer(k)
    slot = pidx if landing_slot_of_peer else _me()[3]
    return pltpu.make_async_remote_copy(src_ref=src.at[pidx] if blocked else src, dst_ref=land.at[slot],
                                        send_sem=ssem.at[k - 1], recv_sem=rsem.at[k - 1], device_id=pid, device_id_type=MESH)


def _send_start(srcs, blocked, name):
    n = len(srcs)
    lands = [lax.empty(a.shape if blocked else (N_DEV,) + a.shape, a.dtype) for a in srcs]

    def body(*refs):
        src, land, sems, token = refs[:n], refs[n:2 * n], refs[2 * n:4 * n], refs[-1]
        for i in range(n):
            for k in range(1, N_DEV):
                _copy(src[i], land[i], sems[2 * i], sems[2 * i + 1], k, blocked, False).start()
        token[...] = jnp.zeros_like(token)

    sem = pltpu.SemaphoreType.DMA((N_DEV - 1,))
    hbm = lambda a: pltpu.with_memory_space_constraint(a, pltpu.HBM)
    res = pl.pallas_call(
        body, name=name,
        out_shape=tuple([sem] * (2 * n)) + tuple(pltpu.HBM(a.shape, a.dtype) for a in srcs + lands)
        + (jax.ShapeDtypeStruct((8, 128), f32),),
        in_specs=(_HBM,) * (2 * n),
        out_specs=(_SEM,) * (2 * n) + (_HBM,) * (2 * n) + (pl.BlockSpec(memory_space=pltpu.VMEM),),
        input_output_aliases={j: 2 * n + j for j in range(2 * n)},
        compiler_params=pltpu.CompilerParams(has_side_effects=_EFFECT),
    )(*[hbm(a) for a in srcs], *[hbm(a) for a in lands])
    handles = [(res[2 * i], res[2 * i + 1], res[2 * n + i], res[3 * n + i]) for i in range(n)]
    return handles, res[-1]


def _send_wait(handle, blocked, after, name):
    ssem, rsem, src, land = handle

    def body(src_ref, land_ref, ssem_ref, rsem_ref, after_ref, src_out, land_out):
        for k in range(1, N_DEV):
            cp = _copy(src_ref, land_ref, ssem_ref, rsem_ref, k, blocked, True)
            cp.wait_send()
            cp.wait_recv()

    return pl.pallas_call(
        body, name=name, out_shape=(pltpu.HBM(src.shape, src.dtype), pltpu.HBM(land.shape, land.dtype)),
        in_specs=(_HBM, _HBM, _SEM, _SEM, pl.BlockSpec(memory_space=pl.ANY)), out_specs=(_HBM, _HBM),
        input_output_aliases={0: 0, 1: 1}, compiler_params=pltpu.CompilerParams(has_side_effects=_EFFECT),
    )(src, land, ssem, rsem, after)


def _block_diag(w):
    nb, bs = w.shape[0], w.shape[1]
    eye = jnp.eye(nb, dtype=w.dtype)
    return (eye[:, None, :, None] * w[:, :, None, :]).reshape(nb * bs, nb * bs)


def _diag_blocks(d):
    return jnp.stack([d[g, s * 64:(s + 1) * 64, s * 64:(s + 1) * 64] for g in range(4) for s in range(2)])


_SQUARE_TILES = dict(tm=1024, tn=1024, tk=1024)


def _mlp_fwd(x, hm, wu, wd, tag, epilogue, extras, outs):
    (r,) = _matmul(hm, wu, "nn", outs=[bf16], epilogue=lambda acc: (jnp.maximum(acc, 0.0),), name=f"mlp_up_{tag}")
    res = _matmul(r, wd, "nn", outs=outs, extras=(x,) + tuple(extras), epilogue=epilogue, a_map=jnp.square,
                  name=f"mlp_down_{tag}", **_SQUARE_TILES)
    return res, (hm, r)


def _mlp_bwd(x, nw, wu, wd, saved, dxo, dxo_b, tag, sink):
    hm, r = saved
    (du,) = _matmul(dxo_b, wd, "nt", outs=[bf16], extras=(r,), epilogue=lambda acc, rr: (acc * (2.0 * rr.astype(f32)),),
                    name=f"mlp_dact_{tag}")
    (dwd,) = _matmul(r, dxo_b, "tn", outs=[bf16], a_map=jnp.square, name=f"mlp_dwd_{tag}", **_SQUARE_TILES)
    tok = sink({f"w_down{tag}": dwd.reshape(N_DEV, D_FF // N_DEV, D_MODEL)})
    (dwu,) = _matmul(hm, du, "tn", outs=[bf16], shard_cols=2, extras=(jnp.broadcast_to(tok[0:1, 0:1], (1, D_FF)),),
                     epilogue=lambda acc, zero: (acc + zero,), name=f"mlp_dwu_{tag}")
    tok = sink({f"w_up{tag}": dwu})
    return _matmul(du, wu, "nt", outs=_RMS_BWD_OUTS, extras=(x, dxo, nw + tok[0:1, 0:1]), epilogue=_rms_bwd_ep,
                   name=f"mlp_dh_{tag}", **_SQUARE_TILES)


def _local_step(x, tgt, P, weight, sink):
    T = x.shape[0]
    cos, sin = _rope_tables(T)
    rtab = _ret_tables()
    row = lambda a: a.reshape(1, -1)
    mix_nw, mlp_nw = P["mixer_norm_w"], P["mlp_norm_w"]
    wr_bd, wi_bd = _block_diag(P["lru_w_r"]), _block_diag(P["lru_w_i"])
    lru_b, lru_br, lru_bi, lru_lam = row(P["lru_conv_b"]), row(P["lru_b_r"]), row(P["lru_b_i"]), row(P["lru_lambda"])
    pad16 = lambda a: jnp.pad(a.reshape(1, GDN_HEADS), ((0, 0), (GDN_HEADS, 128 - 2 * GDN_HEADS)))
    alog, dtb = pad16(P["gdn_a_log"]), pad16(P["gdn_dt_bias"])
    gnw = row(P["gdn_norm_w"])

    x0 = x
    h0 = _rms_fwd(x0, mix_nw[0:1], "rms_mix_0")
    w_ie = weight("w_in_even", h0)
    (pe,) = _matmul(h0, w_ie, "nn", outs=[f32], name="in_even")
    mix0, o_ret, s_ret = _ret_fwd(pe, cos, sin, rtab, "ret_fwd")
    w_lc = weight("lru_conv_w", pe)
    xc = _conv_fwd(pe, 4, w_lc, lru_b, False, "lru_conv_fwd")
    mix0, h_lru = _lru_fwd(xc, pe, 20, wr_bd, wi_bd, lru_br, lru_bi, lru_lam, mix0, "lru_fwd")
    w_oe = weight("w_out_even", mix0)
    x1, hm0 = _matmul(mix0, w_oe, "nn", outs=[f32, bf16], extras=(x0, mlp_nw[0:1]), epilogue=_residual_rms_ep,
                      name="out_even", tm=1024, tn=D_MODEL)
    w_u0, w_d0 = weight("w_up0", x1), weight("w_down0", x1)
    (x2, h1), mlp0 = _mlp_fwd(x1, hm0, w_u0, w_d0, "0", _residual_rms_ep, (mix_nw[1:2],), [f32, bf16])
    w_io = weight("w_in_odd", h1)
    (po,) = _matmul(h1, w_io, "nn", outs=[f32], tm=2048, tn=ODD_PAD // 3, name="in_odd")
    w_gc = weight("gdn_conv_w", po)
    qkv = _conv_fwd(po, 0, w_gc, None, True, "gdn_conv_fwd")
    y_gdn, s_gdn, ti_gdn, vn_gdn, o_gdn = _gdn_fwd(qkv, po, alog, dtb, gnw, "gdn_fwd")
    w_oo = weight("w_out_odd", y_gdn)
    x3, hm1 = _matmul(y_gdn, w_oo, "nn", outs=[f32, bf16], extras=(x2, mlp_nw[1:2]), epilogue=_residual_rms_ep,
                      name="out_odd", tm=1024, tn=D_MODEL)
    w_u1, w_d1 = weight("w_up1", x3), weight("w_down1", x3)
    (loss, dx4, dx4_b, d_final), mlp1 = _mlp_fwd(x3, hm1, w_u1, w_d1, "1", _loss_ep, (row(P["final_norm_w"]), tgt),
                                                 _LOSS_OUTS)
    dx3, dx3_b, d_mlp_nw1 = _mlp_bwd(x3, mlp_nw[1:2], w_u1, w_d1, mlp1, dx4, dx4_b, "1", sink)
    (dy_gdn,) = _matmul(dx3_b, w_oo, "nt", outs=[f32], name="out_odd_dx")
    (d_woo,) = _matmul(y_gdn, dx3_b, "tn", outs=[bf16], name="out_odd_dw")
    dqkv, dpo, dba, d_alog, d_dtb, d_gnw = _gdn_bwd(qkv, po, alog, dtb, gnw, s_gdn, ti_gdn, vn_gdn, o_gdn, dy_gdn,
                                                  "gdn_bwd")
    dpo, d_gconv, _ = _conv_bwd(po, 0, w_gc, None, True, dqkv, dpo, "gdn_conv_bwd")
    dpo = lax.dynamic_update_slice(dpo, dba, (0, 4 * D_MODEL))
    (d_wio,) = _matmul(h1, dpo, "tn", outs=[bf16], tn=ODD_PAD // 3, name="in_odd_dw")
    tok = sink(dict(w_out_odd=d_woo.reshape(N_DEV, D_MODEL // N_DEV, D_MODEL), w_in_odd=_odd_split(d_wio, "w_in_odd_split")))
    dx2, dx2_b, d_mix_nw1 = _matmul(dpo, w_io, "nt", outs=_RMS_BWD_OUTS, extras=(x2, dx3, mix_nw[1:2] + tok[0:1, 0:1]),
                                    epilogue=_rms_bwd_ep, tm=1024, tn=1024, tk=ODD_PAD // 3, name="in_odd_dx")
    dx1, dx1_b, d_mlp_nw0 = _mlp_bwd(x1, mlp_nw[0:1], w_u0, w_d0, mlp0, dx2, dx2_b, "0", sink)
    (d_woe,) = _matmul(mix0, dx1_b, "tn", outs=[bf16], name="out_even_dw")
    tok = sink(dict(w_out_even=d_woe.reshape(N_DEV, D_MODEL // N_DEV, D_MODEL)))
    (dmix0,) = _matmul(dx1_b, w_oe, "nt", outs=[f32], name="out_even_dx")
    dpe = _ret_bwd(pe, cos, sin, rtab, o_ret, s_ret, dmix0, "ret_bwd")
    dxc, dpe, d_wr, d_wi, d_br, d_bi, d_lam = _lru_bwd(xc, pe, 20, wr_bd, wi_bd, lru_br, lru_bi, lru_lam + tok[0:1, 0:1],
                                                       h_lru, dmix0, 4, dpe, "lru_bwd")
    dpe, d_lconv, d_lconv_b = _conv_bwd(pe, 4, w_lc, lru_b, False, dxc, dpe, "lru_conv_bwd")
    G = dict(
        mlp_norm_w=jnp.concatenate([d_mlp_nw0, d_mlp_nw1], axis=0),
        final_norm_w=d_final.reshape(-1),
        lru_conv_w=d_lconv, lru_conv_b=d_lconv_b.reshape(-1),
        lru_w_r=_diag_blocks(d_wr), lru_b_r=d_br.reshape(-1), lru_w_i=_diag_blocks(d_wi), lru_b_i=d_bi.reshape(-1),
        lru_lambda=d_lam.reshape(-1), gdn_conv_w=d_gconv,
        gdn_a_log=d_alog[0, GDN_HEADS:2 * GDN_HEADS], gdn_dt_bias=d_dtb[0, GDN_HEADS:2 * GDN_HEADS],
        gdn_norm_w=d_gnw.reshape(-1),
    )
    packed = _pack([G[k] for k in _SMALL] + [d_lconv, d_gconv, loss[0, 0:1]])
    tok = sink(dict(small=jnp.broadcast_to(packed[None], (N_DEV,) + packed.shape)))
    (d_wie,) = _matmul(h0, dpe, "tn", outs=[bf16], shard_cols=2, extras=(jnp.broadcast_to(tok[0:1, 0:1], (1, 3 * D_MODEL)),),
                       epilogue=lambda acc, zero: (acc + zero,), name="in_even_dw")
    tok = sink(dict(w_in_even=d_wie))
    dx0, _, d_mix_nw0 = _matmul(dpe, w_ie, "nt", outs=_RMS_BWD_OUTS, extras=(x0, dx1, mix_nw[0:1] + tok[0:1, 0:1]),
                                epilogue=_rms_bwd_ep, name="in_even_dx", **_SQUARE_TILES)
    G["mixer_norm_w"] = jnp.concatenate([d_mix_nw0, d_mix_nw1], axis=0)
    return loss, dx0, G


_SMALL = ["mlp_norm_w", "final_norm_w", "lru_conv_b", "lru_w_r", "lru_b_r", "lru_w_i", "lru_b_i",
          "lru_lambda", "gdn_a_log", "gdn_dt_bias", "gdn_norm_w"]
_PACK_ROWS = 688


def _pack(parts):
    flat = jnp.concatenate([p.reshape(-1) for p in parts])
    return jnp.pad(flat, (0, _PACK_ROWS * 128 - flat.shape[0])).reshape(_PACK_ROWS, 128)


def _unpack(packed, shapes):
    flat = packed.reshape(-1)
    out, off = [], 0
    for s in shapes:
        n = int(np.prod(s))
        out.append(flat[off:off + n].reshape(s))
        off += n
    return out


def kernel(x, mixer_norm_w, mlp_norm_w, final_norm_w, w_in_even, lru_conv_w, lru_conv_b, lru_w_r, lru_b_r, lru_w_i, lru_b_i, lru_lambda, w_out_even, w_in_odd, gdn_conv_w, gdn_a_log, gdn_dt_bias, gdn_norm_w, w_out_odd, w_up, w_down, loss_target, m_mixer_norm_w, m_mlp_norm_w, m_final_norm_w, m_w_in_even, m_lru_conv_w, m_lru_conv_b, m_lru_w_r, m_lru_b_r, m_lru_w_i, m_lru_b_i, m_lru_lambda, m_w_out_even, m_w_in_odd, m_gdn_conv_w, m_gdn_a_log, m_gdn_dt_bias, m_gdn_norm_w, m_w_out_odd, m_w_up, m_w_down, v_mixer_norm_w, v_mlp_norm_w, v_final_norm_w, v_w_in_even, v_lru_conv_w, v_lru_conv_b, v_lru_w_r, v_lru_b_r, v_lru_w_i, v_lru_b_i, v_lru_lambda, v_w_out_even, v_w_in_odd, v_gdn_conv_w, v_gdn_a_log, v_gdn_dt_bias, v_gdn_norm_w, v_w_out_odd, v_w_up, v_w_down):
    Pw = dict(mixer_norm_w=mixer_norm_w, mlp_norm_w=mlp_norm_w, final_norm_w=final_norm_w, w_in_even=w_in_even,
              lru_conv_w=lru_conv_w, lru_conv_b=lru_conv_b, lru_w_r=lru_w_r, lru_b_r=lru_b_r, lru_w_i=lru_w_i,
              lru_b_i=lru_b_i, lru_lambda=lru_lambda, w_out_even=w_out_even, w_in_odd=w_in_odd, gdn_conv_w=gdn_conv_w,
              gdn_a_log=gdn_a_log, gdn_dt_bias=gdn_dt_bias, gdn_norm_w=gdn_norm_w, w_out_odd=w_out_odd, w_up=w_up,
              w_down=w_down)
    Pm = dict(mixer_norm_w=m_mixer_norm_w, mlp_norm_w=m_mlp_norm_w, final_norm_w=m_final_norm_w, w_in_even=m_w_in_even,
              lru_conv_w=m_lru_conv_w, lru_conv_b=m_lru_conv_b, lru_w_r=m_lru_w_r, lru_b_r=m_lru_b_r, lru_w_i=m_lru_w_i,
              lru_b_i=m_lru_b_i, lru_lambda=m_lru_lambda, w_out_even=m_w_out_even, w_in_odd=m_w_in_odd,
              gdn_conv_w=m_gdn_conv_w, gdn_a_log=m_gdn_a_log, gdn_dt_bias=m_gdn_dt_bias, gdn_norm_w=m_gdn_norm_w,
              w_out_odd=m_w_out_odd, w_up=m_w_up, w_down=m_w_down)
    Pv = dict(mixer_norm_w=v_mixer_norm_w, mlp_norm_w=v_mlp_norm_w, final_norm_w=v_final_norm_w, w_in_even=v_w_in_even,
              lru_conv_w=v_lru_conv_w, lru_conv_b=v_lru_conv_b, lru_w_r=v_lru_w_r, lru_b_r=v_lru_b_r, lru_w_i=v_lru_w_i,
              lru_b_i=v_lru_b_i, lru_lambda=v_lru_lambda, w_out_even=v_w_out_even, w_in_odd=v_w_in_odd,
              gdn_conv_w=v_gdn_conv_w, gdn_a_log=v_gdn_a_log, gdn_dt_bias=v_gdn_dt_bias, gdn_norm_w=v_gdn_norm_w,
              w_out_odd=v_w_out_odd, w_up=v_w_up, w_down=v_w_down)
    me = _me()[3]
    T = x.shape[1]

    cols = lambda g: jnp.transpose(g, (1, 0, 2)).reshape(g.shape[1], -1)
    rows = lambda g: g.reshape(-1, g.shape[2])
    wide = lambda g: _odd_assemble(g, "w_in_odd_assemble")
    odd_shard = jnp.pad(w_in_odd[0].astype(bf16), ((0, 0), (0, ODD_SHARD_PAD - ODD_SHARD)))
    as_is = lambda g: g
    gather = dict(
        w_in_even=(w_in_even[0].astype(bf16), cols), lru_conv_w=(lru_conv_w[0], cols),
        w_out_even=(w_out_even[0].astype(bf16), rows), w_up0=(w_up[0].astype(bf16), as_is), w_down0=(w_down[0].astype(bf16), rows),
        w_in_odd=(odd_shard, wide), gdn_conv_w=(gdn_conv_w[0], cols),
        w_out_odd=(w_out_odd[0].astype(bf16), rows), w_up1=(w_up[1].astype(bf16), as_is), w_down1=(w_down[1].astype(bf16), rows))
    handles, tok = _send_start([s for s, _ in gather.values()], False, "gather_start")
    handles = dict(zip(gather, handles))
    full = {}

    def weight(name, after):
        if name not in full:
            shard, landed = _send_wait(handles[name], False, after, f"gather_wait_{name}")
            full[name] = gather[name][1](lax.dynamic_update_slice_in_dim(landed, shard[None], me, 0))
        return full[name]

    P = {k: Pw[k] for k in ("mlp_norm_w", "final_norm_w")}
    P["mixer_norm_w"] = mixer_norm_w + tok[0:1, 0:1]
    for k in ("lru_w_r", "lru_w_i", "lru_conv_b", "lru_b_r", "lru_b_i", "lru_lambda", "gdn_a_log", "gdn_dt_bias", "gdn_norm_w"):
        P[k] = Pw[k][0]

    sent = {}

    def sink(grads):
        hs, token = _send_start(list(grads.values()), True, "grads_start_" + "_".join(grads))
        sent.update(zip(grads, hs))
        return token

    loss, dx, G = _local_step(x[0], loss_target[0], P, weight, sink)
    lanes = lambda a: a.reshape(-1, 128)
    sink(dict(mixer_norm_w=jnp.broadcast_to(lanes(G["mixer_norm_w"])[None], (N_DEV, 2 * D_MODEL // 128, 128))))

    def received(name, after=dx):
        g, landed = _send_wait(sent[name], True, after, f"grads_wait_{name}")
        return lax.dynamic_update_slice_in_dim(landed, lax.dynamic_slice_in_dim(g, me, 1, 0), me, 0)

    out = {}
    nff = D_FF // N_DEV

    def whole(name, gs):
        out[name] = tuple(_adamw(Pw[name], gs, Pm[name], Pv[name], f"adamw_{name}", layer=0))

    def layers(name):
        res = None
        for l in range(2):
            res = _adamw(Pw[name], received(f"{name}{l}"), Pm[name], Pv[name], f"adamw_{name}{l}", layer=l, prev=res)
        out[name] = tuple(res)

    layers("w_up")
    layers("w_down")
    whole("w_out_odd", received("w_out_odd"))
    whole("w_in_odd", received("w_in_odd"))
    whole("w_out_even", received("w_out_even"))
    small_shapes = [Pw[k].shape for k in _SMALL]
    pw, pm, pv = (_pack([Q[k] for k in _SMALL]) for Q in (Pw, Pm, Pv))
    sg, sd, sm, sv = _adamw(pw, received("small", out["w_out_even"][1]), pm, pv, "adamw_small")
    for arrs_i, packed_out in enumerate((sg, sd, sm, sv)):
        for k, a in zip(_SMALL, _unpack(packed_out, small_shapes)):
            out.setdefault(k, [None] * 4)[arrs_i] = a
    whole("w_in_even", received("w_in_even", sd))
    out["mixer_norm_w"] = tuple(
        a.reshape(mixer_norm_w.shape) for a in
        _adamw(lanes(mixer_norm_w), received("mixer_norm_w", out["w_in_even"][1]), lanes(m_mixer_norm_w),
               lanes(v_mixer_norm_w), "adamw_mixer_norm_w"))
    n_small = sum(int(np.prod(s)) for s in small_shapes)
    gflat = sg.reshape(-1)
    g_lconv = gflat[n_small:n_small + CONV_K * LRU_WIDTH].reshape(CONV_K, LRU_WIDTH)
    g_gconv = gflat[n_small + CONV_K * LRU_WIDTH:n_small + CONV_K * (LRU_WIDTH + 3072)].reshape(CONV_K, 3072)
    whole("lru_conv_w", lax.dynamic_slice_in_dim(g_lconv, me * 64, 64, axis=1)[None])
    whole("gdn_conv_w", lax.dynamic_slice_in_dim(g_gconv, me * 384, 384, axis=1)[None])

    names = ["mixer_norm_w", "mlp_norm_w", "final_norm_w", "w_in_even", "lru_conv_w", "lru_conv_b", "lru_w_r", "lru_b_r",
             "lru_w_i", "lru_b_i", "lru_lambda", "w_out_even", "w_in_odd", "gdn_conv_w", "gdn_a_log", "gdn_dt_bias",
             "gdn_norm_w", "w_out_odd", "w_up", "w_down"]
    total = gflat[n_small + CONV_K * (LRU_WIDTH + 3072)]
    res = [total, dx[None]]
    for j in range(4):
        res += [out[k][j] for k in names]
    return tuple(res)
```

```python
import math

import numpy as np
import jax
import jax.numpy as jnp
from jax import lax
from jax.experimental import pallas as pl
from jax.experimental.pallas import tpu as pltpu

f32 = jnp.float32
bf16 = jnp.bfloat16

N_DEV = 8
D_MODEL = 1024
D_FF = 4096
EPS = 1e-6
RET_HEADS = 4
RET_CHUNK = 128
RET_STEP = 4
ROPE_THETA = 10000.0
LRU_WIDTH = 512
LRU_C = 8.0
GDN_HEADS = 8
GDN_CHUNK = 64
GDN_STEP = 4
HEAD_DIM = 128
ODD_IN = 4112
ODD_PAD = 4224
ODD_SHARD = ODD_IN // N_DEV
ODD_SHARD_PAD = 640
ADAM_LR, ADAM_B1, ADAM_B2, ADAM_EPS, ADAM_WD, ADAM_STEP = 0.001, 0.9, 0.999, 1e-08, 0.01, 10
VMEM_LIMIT = 56 * 1024 * 1024

_NN = (((1,), (0,)), ((), ()))
_NT = (((1,), (1,)), ((), ()))
_TN = (((0,), (0,)), ((), ()))
MESH = pl.DeviceIdType.MESH


def _cparams(sem):
    return pltpu.CompilerParams(dimension_semantics=sem, vmem_limit_bytes=VMEM_LIMIT)


def _dot(a, b, dn):
    return lax.dot_general(a.astype(bf16), b.astype(bf16), dn, preferred_element_type=f32)


def _dot01(a01, b, dn):
    a = a01.astype(bf16)
    b0 = b.astype(bf16)
    r1 = b - b0.astype(f32)
    b1 = r1.astype(bf16)
    b2 = (r1 - b1.astype(f32)).astype(bf16)
    d = lambda q: lax.dot_general(a, q, dn, preferred_element_type=f32)
    return d(b0) + (d(b1) + d(b2))


def _sigmoid(x):
    return jax.nn.sigmoid(x)


def _silu(x):
    return x * _sigmoid(x)


def _dsilu(x):
    s = _sigmoid(x)
    return s * (1.0 + x * (1.0 - s))


def _softplus(x):
    return jnp.maximum(x, 0.0) + jnp.log1p(jnp.exp(-jnp.abs(x)))


_GELU_C = math.sqrt(2.0 / math.pi)


def _gelu(y):
    return 0.5 * y * (1.0 + jnp.tanh(_GELU_C * (y + 0.044715 * y * y * y)))


def _dgelu(y):
    t = jnp.tanh(_GELU_C * (y + 0.044715 * y * y * y))
    return 0.5 * (1.0 + t) + 0.5 * y * (1.0 - t * t) * _GELU_C * (1.0 + 3.0 * 0.044715 * y * y)


def _matmul(a, b, form, *, outs, name, epilogue=None, extras=(), tm=4096, tn=512, tk=1024, shard_cols=0, a_map=None):
    if form == "tn":
        K, M = a.shape
    else:
        M, K = a.shape
    per_step = 1
    if b.ndim == 3:
        assert form in ("nn", "nt"), name
        N = b.shape[1] if form == "nt" else N_DEV * b.shape[2]
        if form == "nn":
            tn = b.shape[2]
        else:
            per_step = max(1, tk // b.shape[2])
            tk = per_step * b.shape[2]
    else:
        N = b.shape[0] if form == "nt" else b.shape[1]
    ns = N // N_DEV
    if shard_cols:
        tn = ns * shard_cols
    tm, tn, tk = min(tm, M), min(tn, N), min(tk, K)
    assert M % tm == 0 and N % tn == 0 and K % tk == 0, (name, M, N, K, tm, tn, tk)
    nk = K // tk
    dn = {"nn": _NN, "nt": _NT, "tn": _TN}[form]
    if form == "tn":
        a_spec = pl.BlockSpec((tk, tm), lambda i, j, k: (k, i))
    else:
        a_spec = pl.BlockSpec((tm, tk), lambda i, j, k: (i, k))
    if b.ndim == 3:
        b_spec = (pl.BlockSpec((per_step, tn, tk // per_step), lambda i, j, k: (k, j, 0)) if form == "nt"
                  else pl.BlockSpec((None, tk, tn), lambda i, j, k: (j, k, 0)))
    elif form == "nt":
        b_spec = pl.BlockSpec((tn, tk), lambda i, j, k: (j, k))
    else:
        b_spec = pl.BlockSpec((tk, tn), lambda i, j, k: (k, j))
    e_spec = pl.BlockSpec((tm, tn), lambda i, j, k: (i, j))
    v_spec = pl.BlockSpec((1, tn), lambda i, j, k: (0, j))
    if shard_cols:
        o_spec = pl.BlockSpec((shard_cols, tm, ns), lambda i, j, k: (j, i, 0))
        o_shape = (N_DEV, M, ns)
    else:
        o_spec = e_spec
        o_shape = (M, N)
    n_ex = len(extras)
    sums = [isinstance(o, tuple) for o in outs]
    assert not any(sums) or tn == N, name

    def finish(acc, ex, o_refs, row_tile):
        vals = (acc,) if epilogue is None else epilogue(acc, *[e[...] for e in ex])
        for r, v, is_sum in zip(o_refs, vals, sums):
            if is_sum:
                @pl.when(row_tile == 0)
                def _(r=r, v=v):
                    r[...] = v.astype(r.dtype)

                @pl.when(row_tile > 0)
                def _(r=r, v=v):
                    r[...] += v.astype(r.dtype)
            elif shard_cols:
                for s in range(shard_cols):
                    r[s] = v[:, s * ns:(s + 1) * ns].astype(r.dtype)
            else:
                r[...] = v.astype(r.dtype)

    def prod(a_ref, b_ref):
        if b.ndim == 3 and form == "nt":
            w = tk // per_step
            return sum(_dot(a_ref[:, s * w:(s + 1) * w], b_ref[s], dn) for s in range(1, per_step)) + _dot(a_ref[:, 0:w], b_ref[0], dn)
        av = a_ref[...]
        return _dot(av if a_map is None else a_map(av), b_ref[...], dn)

    def body_one(*refs):
        finish(prod(*refs[:2]), refs[2:2 + n_ex], refs[2 + n_ex:], pl.program_id(0))

    def body_acc(*refs):
        a_ref, b_ref = refs[:2]
        acc = refs[-1]
        k = pl.program_id(2)
        row_tile = pl.program_id(0)

        @pl.when(k == 0)
        def _():
            acc[...] = prod(a_ref, b_ref)

        @pl.when((k > 0) & (k < nk - 1))
        def _():
            acc[...] += prod(a_ref, b_ref)

        @pl.when(k == nk - 1)
        def _():
            finish(acc[...] + prod(a_ref, b_ref), refs[2:2 + n_ex], refs[2 + n_ex:-1], row_tile)

    return pl.pallas_call(
        body_one if nk == 1 else body_acc, grid=(M // tm, N // tn, nk),
        in_specs=[a_spec, b_spec] + [v_spec if e.shape[0] == 1 else e_spec for e in extras],
        out_specs=[v_spec if s else o_spec for s in sums],
        out_shape=[jax.ShapeDtypeStruct((1, N), o[1]) if s else jax.ShapeDtypeStruct(o_shape, o) for o, s in zip(outs, sums)],
        scratch_shapes=[] if nk == 1 else [pltpu.VMEM((tm, tn), f32)],
        compiler_params=_cparams(("arbitrary" if any(sums) else "parallel", "parallel", "arbitrary")), name=name,
    )(a, b, *extras)


def _rms_fwd(x, w, name):
    T, D = x.shape
    tt = min(512, T)

    def body(x_ref, w_ref, h_ref):
        xv = x_ref[...]
        r = lax.rsqrt(jnp.mean(xv * xv, axis=1, keepdims=True) + EPS)
        h_ref[...] = (xv * r * w_ref[...]).astype(bf16)

    return pl.pallas_call(
        body, grid=(T // tt,),
        in_specs=[pl.BlockSpec((tt, D), lambda i: (i, 0)), pl.BlockSpec((1, D), lambda i: (0, 0))],
        out_specs=pl.BlockSpec((tt, D), lambda i: (i, 0)),
        out_shape=jax.ShapeDtypeStruct((T, D), bf16),
        compiler_params=_cparams(("parallel",)), name=name,
    )(x, w)


def _residual_rms_ep(acc, res, w):
    x = res + acc
    r = lax.rsqrt(jnp.mean(x * x, axis=1, keepdims=True) + EPS)
    return x, x * r * w


_RMS_BWD_OUTS = [f32, bf16, ("sum", f32)]


def _rms_bwd_ep(dh, x, dres, w):
    r = lax.rsqrt(jnp.mean(x * x, axis=1, keepdims=True) + EPS)
    xn = x * r
    dhw = dh * w
    dx = dres + r * (dhw - xn * jnp.mean(dhw * xn, axis=1, keepdims=True))
    return dx, dx, jnp.sum(dh * xn, axis=0, keepdims=True)


_LOSS_OUTS = [("sum", f32), f32, bf16, ("sum", f32)]


def _loss_ep(acc, res, w, tgt):
    x = res + acc
    D = x.shape[1]
    r = lax.rsqrt(jnp.mean(x * x, axis=1, keepdims=True) + EPS)
    xn = x * r
    e = xn * w - tgt
    loss = 0.5 * jnp.sum(jnp.mean(e * e, axis=1, keepdims=True), axis=0, keepdims=True)
    dy = e * (1.0 / D)
    dyw = dy * w
    dx = r * (dyw - xn * jnp.mean(dyw * xn, axis=1, keepdims=True))
    return jnp.broadcast_to(loss, (1, D)), dx, dx, jnp.sum(dy * xn, axis=0, keepdims=True)


def _ret_tables():
    H, C = RET_HEADS, RET_CHUNK
    lg = np.log1p(-np.exp2(-5.0 - np.arange(H, dtype=np.float32))).astype(np.float32)
    idx = np.arange(C, dtype=np.float32)
    diff = idx[:, None] - idx[None, :]
    causal = diff >= 0
    dm = np.where(causal[None], np.exp(lg[:, None, None] * np.where(causal, diff, 0.0)[None]), 0.0)
    qd = np.exp(lg[:, None] * (idx[None, :] + 1.0))
    kd = np.exp(lg[:, None] * (C - 1.0 - idx[None, :]))
    cg = np.exp(lg * C)
    tab = np.zeros((H, 4, C, HEAD_DIM), np.float32)
    tab[:, 0] = dm
    tab[:, 1] = qd[:, :, None]
    tab[:, 2] = kd[:, :, None]
    tab[:, 3] = cg[:, None, None]
    return jnp.asarray(tab)


def _rope_tables(T):
    half = HEAD_DIM // 2
    inv = ROPE_THETA ** (-jnp.arange(half, dtype=f32) / half)
    ang = jnp.arange(T, dtype=jnp.int32).astype(f32)[:, None] * inv[None, :]
    c, s = jnp.cos(ang), jnp.sin(ang)
    return jnp.concatenate([c, c], axis=1), jnp.concatenate([-s, s], axis=1)


def _rope(x, cos, sin):
    return x * cos + pltpu.roll(x, HEAD_DIM // 2, 1) * sin


def _unrope(y, cos, sin):
    return y * cos + pltpu.roll(y * sin, HEAD_DIM // 2, 1)


def _stack_heads(ref, H, f=None):
    parts = [ref[:, h * HEAD_DIM:(h + 1) * HEAD_DIM] for h in range(H)]
    return jnp.stack(parts if f is None else [f(a) for a in parts])


def _ret_fwd(p, cos, sin, tab, name):
    T = p.shape[0]
    C, H = RET_CHUNK, RET_HEADS
    N = T // C
    K = min(RET_STEP, N)
    NS = N // K
    scale = HEAD_DIM ** -0.5

    def body(q_ref, k_ref, v_ref, g_ref, c_ref, s_ref, t_ref, y_ref, o_ref, sp_ref, st):
        @pl.when(pl.program_id(0) == 0)
        def _():
            st[...] = jnp.zeros_like(st)

        dm, qd, kd, cg = t_ref[:, 0], t_ref[:, 1], t_ref[:, 2], t_ref[:, 3]
        S = st[...]
        for c in range(K):
            rows = pl.ds(c * C, C)
            cos_, sin_ = c_ref[rows, :], s_ref[rows, :]
            rot = lambda a: _rope(a, cos_, sin_)
            q = _stack_heads(q_ref.at[rows, :], H, rot)
            k = _stack_heads(k_ref.at[rows, :], H, rot) * scale
            v = _stack_heads(v_ref.at[rows, :], H)
            P = _dot(q, k, _NT3) * dm
            o = _dot(P, v, _NN3) + _dot(q * qd, S, _NN3)
            sp_ref[c] = S
            S = cg * S + _dot(k * kd, v, _TN3)
            r = lax.rsqrt(jnp.mean(o * o, axis=2, keepdims=True) + EPS)
            y = o * r * _silu(_stack_heads(g_ref.at[rows, :], H))
            for h in range(H):
                o_ref[rows, h * HEAD_DIM:(h + 1) * HEAD_DIM] = o[h]
                y_ref[rows, h * HEAD_DIM:(h + 1) * HEAD_DIM] = y[h].astype(bf16)
        st[...] = S

    wide = lambda blk: pl.BlockSpec((K * C, H * HEAD_DIM), lambda n: (n, blk))
    tbl = pl.BlockSpec((K * C, HEAD_DIM), lambda n: (n, 0))
    return pl.pallas_call(
        body, grid=(NS,),
        in_specs=[wide(0), wide(1), wide(2), wide(3), tbl, tbl,
                  pl.BlockSpec((H, 4, C, HEAD_DIM), lambda n: (0, 0, 0, 0))],
        out_specs=[wide(0), wide(0), pl.BlockSpec((K, H, HEAD_DIM, HEAD_DIM), lambda n: (n, 0, 0, 0))],
        out_shape=[jax.ShapeDtypeStruct((T, D_MODEL), bf16), jax.ShapeDtypeStruct((T, H * HEAD_DIM), f32),
                   jax.ShapeDtypeStruct((N, H, HEAD_DIM, HEAD_DIM), f32)],
        scratch_shapes=[pltpu.VMEM((H, HEAD_DIM, HEAD_DIM), f32)],
        compiler_params=_cparams(("arbitrary",)), name=name,
    )(p, p, p, p, cos, sin, tab)


def _ret_bwd(p, cos, sin, tab, o_raw, sprev, dmix, name):
    T = p.shape[0]
    C, H = RET_CHUNK, RET_HEADS
    N = T // C
    K = min(RET_STEP, N)
    NS = N // K
    scale = HEAD_DIM ** -0.5
    W = H * HEAD_DIM

    def body(q_ref, k_ref, v_ref, g_ref, c_ref, s_ref, t_ref, o_ref, sp_ref, dy_ref, d_ref, dst):
        @pl.when(pl.program_id(0) == 0)
        def _():
            dst[...] = jnp.zeros_like(dst)

        dm, qd, kd, cg = t_ref[:, 0], t_ref[:, 1], t_ref[:, 2], t_ref[:, 3]
        dS1 = dst[...]
        for c in reversed(range(K)):
            rows = pl.ds(c * C, C)
            cos_, sin_ = c_ref[rows, :], s_ref[rows, :]
            rot = lambda a: _rope(a, cos_, sin_)
            q = _stack_heads(q_ref.at[rows, :], H, rot)
            k = _stack_heads(k_ref.at[rows, :], H, rot) * scale
            v = _stack_heads(v_ref.at[rows, :], H)
            g = _stack_heads(g_ref.at[rows, :], H)
            S = sp_ref[c]
            o = _stack_heads(o_ref.at[rows, :], H)
            dy = _stack_heads(dy_ref.at[rows, :], H)
            r = lax.rsqrt(jnp.mean(o * o, axis=2, keepdims=True) + EPS)
            nrm = o * r
            dn = dy * _silu(g)
            dg = dy * nrm * _dsilu(g)
            do = r * (dn - nrm * jnp.mean(dn * nrm, axis=2, keepdims=True))
            P = _dot(q, k, _NT3) * dm
            dP = _dot(do, v, _NT3) * dm
            dq = _dot(dP, k, _NN3) + _dot(do, S, _NT3) * qd
            dk = (_dot(dP, q, _TN3) + _dot(v, dS1, _NT3) * kd) * scale
            dv = _dot(P, do, _TN3) + _dot(k * kd, dS1, _NN3)
            dS1 = cg * dS1 + _dot(q * qd, do, _TN3)
            for h in range(H):
                d_ref[rows, h * HEAD_DIM:(h + 1) * HEAD_DIM] = _unrope(dq[h], cos_, sin_).astype(bf16)
                d_ref[rows, W + h * HEAD_DIM:W + (h + 1) * HEAD_DIM] = _unrope(dk[h], cos_, sin_).astype(bf16)
                d_ref[rows, 2 * W + h * HEAD_DIM:2 * W + (h + 1) * HEAD_DIM] = dv[h].astype(bf16)
                d_ref[rows, 3 * W + h * HEAD_DIM:3 * W + (h + 1) * HEAD_DIM] = dg[h].astype(bf16)
        dst[...] = dS1

    rev = lambda blk: pl.BlockSpec((K * C, W), lambda n: (NS - 1 - n, blk))
    tbl = pl.BlockSpec((K * C, HEAD_DIM), lambda n: (NS - 1 - n, 0))
    return pl.pallas_call(
        body, grid=(NS,),
        in_specs=[rev(0), rev(1), rev(2), rev(3), tbl, tbl,
                  pl.BlockSpec((H, 4, C, HEAD_DIM), lambda n: (0, 0, 0, 0)), rev(0),
                  pl.BlockSpec((K, H, HEAD_DIM, HEAD_DIM), lambda n: (NS - 1 - n, 0, 0, 0)), rev(0)],
        out_specs=pl.BlockSpec((K * C, 4 * W), lambda n: (NS - 1 - n, 0)),
        out_shape=jax.ShapeDtypeStruct((T, 6 * W), bf16),
        scratch_shapes=[pltpu.VMEM((H, HEAD_DIM, HEAD_DIM), f32)],
        compiler_params=_cparams(("arbitrary",)), name=name,
    )(p, p, p, p, cos, sin, tab, o_raw, sprev, dmix)


CONV_K = 4
CONV_W = 512
PAD = 8
SUB_R = 64


def _conv_fwd(x, col_off, w, b, act, name):
    T = x.shape[0]
    C = w.shape[1]
    G = C // CONV_W
    tt = min(512, T)
    NT = T // tt
    has_b = b is not None

    def body(*refs):
        if has_b:
            x_ref, w_ref, b_ref, y_ref, pad = refs
        else:
            x_ref, w_ref, y_ref, pad = refs
        t = pl.program_id(1)

        @pl.when(t == 0)
        def _():
            pad[pl.ds(0, PAD), :] = jnp.zeros((PAD, CONV_W), f32)

        pad[pl.ds(PAD, tt), :] = x_ref[...]
        for g in range(CONV_W // 128):
            ls = slice(g * 128, (g + 1) * 128)
            wv = w_ref[:, ls]
            for c in range(tt // SUB_R):
                r0 = c * SUB_R
                y = wv[0:1, :] * pad[pl.ds(PAD - 3 + r0, SUB_R), ls]
                for kk in range(1, CONV_K):
                    y = y + wv[kk:kk + 1, :] * pad[pl.ds(PAD - 3 + kk + r0, SUB_R), ls]
                if has_b:
                    y = y + b_ref[:, ls]
                y_ref[pl.ds(r0, SUB_R), ls] = _silu(y) if act else y
        tail = pad[pl.ds(tt, PAD), :]
        pad[pl.ds(0, PAD), :] = tail

    in_specs = [pl.BlockSpec((tt, CONV_W), lambda g, t: (t, col_off + g)),
                pl.BlockSpec((CONV_K, CONV_W), lambda g, t: (0, g))]
    args = [x, w]
    if has_b:
        in_specs.append(pl.BlockSpec((1, CONV_W), lambda g, t: (0, g)))
        args.append(b)
    return pl.pallas_call(
        body, grid=(G, NT), in_specs=in_specs,
        out_specs=pl.BlockSpec((tt, CONV_W), lambda g, t: (t, g)),
        out_shape=jax.ShapeDtypeStruct((T, C), f32),
        scratch_shapes=[pltpu.VMEM((tt + PAD, CONV_W), f32)],
        compiler_params=_cparams(("parallel", "arbitrary")), name=name,
    )(*args)


def _conv_bwd(x, col_off, w, b, act, dout, dp, name):
    T = x.shape[0]
    C = w.shape[1]
    G = C // CONV_W
    tt = min(512, T)
    NT = T // tt
    has_b = b is not None

    def body(*refs):
        if has_b:
            x_ref, xp_ref, w_ref, b_ref, d_ref, dp_in, dx_ref, dw_ref, db_ref, pad, dpad = refs
        else:
            x_ref, xp_ref, w_ref, d_ref, dp_in, dx_ref, dw_ref, db_ref, pad, dpad = refs
        t = pl.program_id(1)
        first_tile = t == NT - 1

        @pl.when(t == 0)
        def _():
            dpad[pl.ds(tt, PAD), :] = jnp.zeros((PAD, CONV_W), f32)
            dw_ref[...] = jnp.zeros_like(dw_ref)
            db_ref[...] = jnp.zeros_like(db_ref)

        pad[pl.ds(0, PAD), :] = jnp.where(first_tile, 0.0, xp_ref[...])
        pad[pl.ds(PAD, tt), :] = x_ref[...]
        fold = lambda v: v.reshape(SUB_R // 8, 8, 128).sum(axis=0)
        for g in range(CONV_W // 128):
            ls = slice(g * 128, (g + 1) * 128)
            wv = w_ref[:, ls]
            acc = [jnp.zeros((8, 128), f32) for _ in range(CONV_K + 1)]
            for c in reversed(range(tt // SUB_R)):
                r0 = c * SUB_R
                xs = [pad[pl.ds(PAD - 3 + kk + r0, SUB_R), ls] for kk in range(CONV_K)]
                dy = d_ref[pl.ds(r0, SUB_R), ls]
                if act:
                    y = wv[0:1, :] * xs[0]
                    for kk in range(1, CONV_K):
                        y = y + wv[kk:kk + 1, :] * xs[kk]
                    if has_b:
                        y = y + b_ref[:, ls]
                    dy = dy * _dsilu(y)
                dpad[pl.ds(r0, SUB_R), ls] = dy
                dx = wv[3:4, :] * dy
                for j in range(1, CONV_K):
                    dx = dx + wv[3 - j:4 - j, :] * dpad[pl.ds(r0 + j, SUB_R), ls]
                dx_ref[pl.ds(r0, SUB_R), ls] = dx.astype(bf16)
                for kk in range(CONV_K):
                    acc[kk] = acc[kk] + fold(dy * xs[kk])
                acc[CONV_K] = acc[CONV_K] + fold(dy)
            for kk in range(CONV_K):
                dw_ref[kk:kk + 1, ls] += jnp.sum(acc[kk], axis=0, keepdims=True)
            db_ref[:, ls] += jnp.sum(acc[CONV_K], axis=0, keepdims=True)
        head = dpad[pl.ds(0, PAD), :]
        dpad[pl.ds(tt, PAD), :] = head

    rows8 = tt // PAD
    in_specs = [pl.BlockSpec((tt, CONV_W), lambda g, t: (NT - 1 - t, col_off + g)),
                pl.BlockSpec((PAD, CONV_W), lambda g, t: (jnp.maximum((NT - 1 - t) * rows8 - 1, 0), col_off + g)),
                pl.BlockSpec((CONV_K, CONV_W), lambda g, t: (0, g))]
    args = [x, x, w]
    if has_b:
        in_specs.append(pl.BlockSpec((1, CONV_W), lambda g, t: (0, g)))
        args.append(b)
    in_specs += [pl.BlockSpec((tt, CONV_W), lambda g, t: (NT - 1 - t, g)), pl.BlockSpec(memory_space=pl.ANY)]
    args += [dout, dp]
    return pl.pallas_call(
        body, grid=(G, NT), in_specs=in_specs,
        out_specs=[pl.BlockSpec((tt, CONV_W), lambda g, t: (NT - 1 - t, col_off + g)),
                   pl.BlockSpec((CONV_K, CONV_W), lambda g, t: (0, g)),
                   pl.BlockSpec((1, CONV_W), lambda g, t: (0, g))],
        out_shape=[jax.ShapeDtypeStruct(dp.shape, dp.dtype), jax.ShapeDtypeStruct((CONV_K, C), f32),
                   jax.ShapeDtypeStruct((1, C), f32)],
        input_output_aliases={len(args) - 1: 0},
        scratch_shapes=[pltpu.VMEM((tt + PAD, CONV_W), f32), pltpu.VMEM((tt + PAD, CONV_W), f32)],
        compiler_params=_cparams(("parallel", "arbitrary")), name=name,
    )(*args)


def _lru_gates(xc, wr, wi, br, bi, lam):
    r = _sigmoid(_dot(xc, wr, _NN) + br)
    i = _sigmoid(_dot(xc, wi, _NN) + bi)
    sp = _softplus(-lam)
    a = jnp.exp(-LRU_C * r * sp)
    mult = jnp.sqrt(1.0 - a * a)
    return r, i, sp, a, mult


def _lru_fwd(xc, p, y_off, wr, wi, br, bi, lam, mix, name):
    T = xc.shape[0]
    G = LRU_WIDTH // 128
    tt = min(512, T)
    NT = T // tt

    def body(x_ref, y_ref, wr_ref, wi_ref, br_ref, bi_ref, l_ref, mix_in, o_ref, h_ref, hc):
        t = pl.program_id(1)

        @pl.when(t == 0)
        def _():
            hc[...] = jnp.zeros_like(hc)

        x = x_ref[...]
        r, i, sp, a, mult = _lru_gates(x, wr_ref[...], wi_ref[...], br_ref[...], bi_ref[...], l_ref[...])
        row = lax.broadcasted_iota(jnp.int32, (tt, 128), 0)
        mult = jnp.where((row == 0) & (t == 0), 1.0, mult)
        U = x * i * mult
        A = a
        d = 1
        while d < tt:
            keep = row >= d
            Ush = jnp.where(keep, pltpu.roll(U, d, 0), 0.0)
            Ash = jnp.where(keep, pltpu.roll(A, d, 0), 1.0)
            U = A * Ush + U
            A = A * Ash
            d *= 2
        h = U + A * hc[0:1, :]
        h_ref[...] = h
        hc[...] = jnp.broadcast_to(h[tt - 1:tt, :], hc.shape)
        o_ref[...] = (h * _gelu(y_ref[...])).astype(bf16)

    tile = pl.BlockSpec((tt, 128), lambda g, t: (t, g))
    vec = pl.BlockSpec((1, 128), lambda g, t: (0, g))
    wsp = pl.BlockSpec((128, 128), lambda g, t: (g, g))
    return pl.pallas_call(
        body, grid=(G, NT),
        in_specs=[tile, pl.BlockSpec((tt, 128), lambda g, t: (t, y_off + g)), wsp, wsp, vec, vec, vec,
                  pl.BlockSpec(memory_space=pl.ANY)],
        out_specs=[pl.BlockSpec((tt, 128), lambda g, t: (t, G + g)), tile],
        out_shape=[jax.ShapeDtypeStruct(mix.shape, mix.dtype), jax.ShapeDtypeStruct((T, LRU_WIDTH), f32)],
        input_output_aliases={7: 0},
        scratch_shapes=[pltpu.VMEM((8, 128), f32)],
        compiler_params=_cparams(("parallel", "arbitrary")), name=name,
    )(xc, p, wr, wi, br, bi, lam, mix)


def _lru_bwd(xc, p, y_off, wr, wi, br, bi, lam, hs, dmix, d_off, dp, name):
    T = xc.shape[0]
    G = LRU_WIDTH // 128
    tt = min(512, T)
    NT = T // tt

    def body(x_ref, y_ref, wr_ref, wi_ref, br_ref, bi_ref, l_ref, h_ref, hp_ref, do_ref, dp_in,
             dx_ref, dy_ref, dwr_ref, dwi_ref, dbr_ref, dbi_ref, dl_ref, lc, an):
        t = pl.program_id(1)
        first_tile = t == NT - 1

        @pl.when(t == 0)
        def _():
            lc[...] = jnp.zeros_like(lc)
            an[...] = jnp.zeros_like(an)
            dwr_ref[...] = jnp.zeros_like(dwr_ref)
            dwi_ref[...] = jnp.zeros_like(dwi_ref)
            dbr_ref[...] = jnp.zeros_like(dbr_ref)
            dbi_ref[...] = jnp.zeros_like(dbi_ref)
            dl_ref[...] = jnp.zeros_like(dl_ref)

        x = x_ref[...]
        y = y_ref[...]
        wr, wi, lam_ = wr_ref[...], wi_ref[...], l_ref[...]
        r, i, sp, a, mult_raw = _lru_gates(x, wr, wi, br_ref[...], bi_ref[...], lam_)
        row = lax.broadcasted_iota(jnp.int32, (tt, 128), 0)
        t0 = (row == 0) & first_tile
        mult = jnp.where(t0, 1.0, mult_raw)
        h = h_ref[...]
        do = do_ref[...]
        dh = do * _gelu(y)
        dy_ref[...] = (do * h * _dgelu(y)).astype(bf16)
        B = jnp.where(row == tt - 1, an[0:1, :], pltpu.roll(a, tt - 1, 0))
        L = dh
        d = 1
        while d < tt:
            keep = row < tt - d
            Lsh = jnp.where(keep, pltpu.roll(L, tt - d, 0), 0.0)
            Bsh = jnp.where(keep, pltpu.roll(B, tt - d, 0), 1.0)
            L = L + B * Lsh
            B = B * Bsh
            d *= 2
        L = L + B * lc[0:1, :]
        lc[...] = jnp.broadcast_to(L[0:1, :], lc.shape)
        an[...] = jnp.broadcast_to(a[0:1, :], an.shape)
        hprev = jnp.where(first_tile, 0.0, hp_ref[...])[PAD - 1:PAD, :]
        hm1 = jnp.where(row == 0, hprev, pltpu.roll(h, 1, 0))
        da = L * hm1
        dxc = L * i * mult
        di = L * x * mult
        dmult = jnp.where(t0, 0.0, L * x * i)
        da = da - jnp.where(t0, 0.0, dmult * a / mult_raw)
        dlog_a = da * a
        dr = dlog_a * (-LRU_C) * sp
        dsp = jnp.sum(dlog_a * (-LRU_C) * r, axis=0, keepdims=True)
        dpr = dr * r * (1.0 - r)
        dpi = di * i * (1.0 - i)
        dx_ref[...] = dxc + _dot(dpr, wr, _NT) + _dot(dpi, wi, _NT)
        dwr_ref[0] += _dot(x, dpr, _TN)
        dwi_ref[0] += _dot(x, dpi, _TN)
        dbr_ref[...] += jnp.sum(dpr, axis=0, keepdims=True)
        dbi_ref[...] += jnp.sum(dpi, axis=0, keepdims=True)
        dl_ref[...] += dsp * (-_sigmoid(-lam_))

    rows8 = tt // PAD
    tile = pl.BlockSpec((tt, 128), lambda g, t: (NT - 1 - t, g))
    vec = pl.BlockSpec((1, 128), lambda g, t: (0, g))
    wsp = pl.BlockSpec((128, 128), lambda g, t: (g, g))
    wout = pl.BlockSpec((1, 128, 128), lambda g, t: (g, 0, 0))
    return pl.pallas_call(
        body, grid=(G, NT),
        in_specs=[tile, pl.BlockSpec((tt, 128), lambda g, t: (NT - 1 - t, y_off + g)), wsp, wsp, vec, vec, vec, tile,
                  pl.BlockSpec((PAD, 128), lambda g, t: (jnp.maximum((NT - 1 - t) * rows8 - 1, 0), g)),
                  pl.BlockSpec((tt, 128), lambda g, t: (NT - 1 - t, d_off + g)), pl.BlockSpec(memory_space=pl.ANY)],
        out_specs=[tile, pl.BlockSpec((tt, 128), lambda g, t: (NT - 1 - t, y_off + g)), wout, wout, vec, vec, vec],
        out_shape=[jax.ShapeDtypeStruct((T, LRU_WIDTH), f32), jax.ShapeDtypeStruct(dp.shape, dp.dtype),
                   jax.ShapeDtypeStruct((G, 128, 128), f32), jax.ShapeDtypeStruct((G, 128, 128), f32),
                   jax.ShapeDtypeStruct((1, LRU_WIDTH), f32), jax.ShapeDtypeStruct((1, LRU_WIDTH), f32),
                   jax.ShapeDtypeStruct((1, LRU_WIDTH), f32)],
        input_output_aliases={10: 1},
        scratch_shapes=[pltpu.VMEM((8, 128), f32), pltpu.VMEM((8, 128), f32)],
        compiler_params=_cparams(("parallel", "arbitrary")), name=name,
    )(xc, p, wr, wi, br, bi, lam, hs, hs, dmix, dp)


_NN3 = (((2,), (1,)), ((0,), (0,)))
_NT3 = (((2,), (2,)), ((0,), (0,)))
_TN3 = (((1,), (1,)), ((0,), (0,)))


def _pairs(ref, K):
    C = GDN_CHUNK
    return jnp.stack([ref[c * C:(c + 1) * C, h * HEAD_DIM:(h + 1) * HEAD_DIM] for c in range(K) for h in range(GDN_HEADS)])


def _put_pairs(ref, val, K, col=0):
    C, H = GDN_CHUNK, GDN_HEADS
    for c in range(K):
        for h in range(H):
            ref[c * C:(c + 1) * C, col + h * HEAD_DIM:col + (h + 1) * HEAD_DIM] = val[c * H + h].astype(ref.dtype)


def _rowsum(x):
    H, C, L = x.shape
    return _dot(x.reshape(H * C, L), jnp.ones((L, HEAD_DIM), f32), _NN).reshape(H, C, HEAD_DIM)


def _gdn_pre(qr, kr, v, ba, alog, dtb):
    C, H = GDN_CHUNK, GDN_HEADS
    B = qr.shape[0]
    K = B // H
    lane = lax.broadcasted_iota(jnp.int32, (C, 128), 1)
    lane3 = lax.broadcasted_iota(jnp.int32, (B, C, 128), 2)
    ri = lax.broadcasted_iota(jnp.int32, (C, C), 0)
    ci = lax.broadcasted_iota(jnp.int32, (C, C), 1)
    rowc = lax.broadcasted_iota(jnp.int32, (C, 1), 0)
    col = lambda m, j: jnp.sum(jnp.where(lane == j, m, 0.0), axis=1, keepdims=True)
    ea = jnp.exp(alog)
    tri = (ri >= ci).astype(f32)
    g_all, beta_cols, G_cols = [], [], []
    for c in range(K):
        ba_c = ba[c * C:(c + 1) * C]
        g_c = -ea * _softplus(ba_c + dtb)
        G_c = _dot01(tri, g_c, _NN)
        s_c = _sigmoid(ba_c)
        g_all.append(g_c)
        beta_cols += [col(s_c, h) for h in range(H)]
        G_cols += [col(G_c, H + h) for h in range(H)]
    wide = lambda c: jnp.broadcast_to(c, (B, C, 128))
    beta = wide(jnp.stack(beta_cols))
    Gc = jnp.stack(G_cols)
    rq = lax.rsqrt(_rowsum(qr * qr) + EPS)
    rk = lax.rsqrt(_rowsum(kr * kr) + EPS)
    qh, kn = qr * rq, kr * rk
    qn = qh * (HEAD_DIM ** -0.5)
    Grow = _dot01(jnp.ones((B, C, 128), f32), jnp.where(lane3 == 0, Gc, 0.0), _NT3)
    incl = ri >= ci
    Di = jnp.where(incl, jnp.exp(jnp.where(incl, Gc - Grow, 0.0)), 0.0)
    Ds = jnp.where(ri > ci, Di, 0.0)
    Gl = jnp.sum(jnp.where(rowc == C - 1, Gc, 0.0), axis=1, keepdims=True)
    eG = wide(jnp.exp(Gc))
    eGl = wide(jnp.exp(Gl - Gc))
    cd = jnp.exp(Gl)
    kb = kn * beta
    vb = v * beta
    Lm = _dot(kb, kn, _NT3) * Ds
    kbg = kb * eG
    QK = _dot(qn, kn, _NT3) * Di
    qg = qn * eG
    kg = kn * eGl
    return dict(beta=beta, g_all=g_all, rq=rq, rk=rk, qh=qh, kn=kn, qn=qn, Di=Di, Ds=Ds, eG=eG, eGl=eGl, cd=cd,
                kb=kb, vb=vb, Lm=Lm, kbg=kbg, QK=QK, qg=qg, kg=kg, lane=lane, ri=ri, ci=ci, rowc=rowc, ea=ea)


def _unit_lower_inverse(Lm):
    C = Lm.shape[-1]
    ri = lax.broadcasted_iota(jnp.int32, (C, C), 0)
    ci = lax.broadcasted_iota(jnp.int32, (C, C), 1)
    same = lambda s: (ri // s) == (ci // s)
    Xd = jnp.where(same(8), -Lm, 0.0)
    Tinv = (ri == ci).astype(f32) + Xd
    Pw = Xd
    for _ in range(2):
        Pw = _dot(Pw, Pw, _NN3)
        Tinv = Tinv + _dot(Tinv, Pw, _NN3)
    for s in (8, 16, 32):
        off = jnp.where(same(2 * s) & jnp.logical_not(same(s)), Lm, 0.0)
        Tinv = Tinv - _dot(_dot(Tinv, off, _NN3), Tinv, _NN3)
    return Tinv


def _gdn_specs(T, rev):
    C = GDN_CHUNK
    H = GDN_HEADS
    K = min(GDN_STEP, T // C)
    NS = T // (C * K)
    nn = (lambda n: NS - 1 - n) if rev else (lambda n: n)
    wide = lambda blk: pl.BlockSpec((K * C, H * HEAD_DIM), lambda n: (nn(n), blk))
    one = lambda off: pl.BlockSpec((K * C, HEAD_DIM), lambda n: (nn(n), off))
    vec = pl.BlockSpec((1, 128), lambda n: (0, 0))
    st = lambda rows: pl.BlockSpec((K, H, rows, rows), lambda n: (nn(n), 0, 0, 0))
    return K, NS, wide, one, vec, st


def _gdn_fwd(qkv, p, alog, dtb, nw, name):
    T = qkv.shape[0]
    C, H = GDN_CHUNK, GDN_HEADS
    N = T // C
    K, NS, wide, one, vec, st_spec = _gdn_specs(T, False)

    def body(q_ref, k_ref, v_ref, z_ref, ba_ref, al_ref, dt_ref, nw_ref, y_ref, sp_ref, ti_ref, vn_ref, o_ref, st):
        @pl.when(pl.program_id(0) == 0)
        def _():
            st[...] = jnp.zeros_like(st)

        f = _gdn_pre(_pairs(q_ref, K), _pairs(k_ref, K), _pairs(v_ref, K), ba_ref[...], al_ref[...], dt_ref[...])
        Tinv = _unit_lower_inverse(f["Lm"])
        ti_ref[...] = Tinv.reshape(K, H, C, C).astype(bf16)
        w = _dot(Tinv, f["kbg"], _NN3)
        u = _dot(Tinv, f["vb"], _NN3)
        S = st[...]
        vns, os_ = [], []
        for c in range(K):
            sl = slice(c * H, (c + 1) * H)
            sp_ref[c] = S
            vn_c = u[sl] - _dot(w[sl], S, _NN3)
            os_.append(_dot(f["qg"][sl], S, _NN3) + _dot(f["QK"][sl], vn_c, _NN3))
            S = S * f["cd"][sl] + _dot(f["kg"][sl], vn_c, _TN3)
            vns.append(vn_c)
        st[...] = S
        vn, o = jnp.concatenate(vns), jnp.concatenate(os_)
        r = lax.rsqrt(_rowsum(o * o) * (1.0 / HEAD_DIM) + EPS)
        _put_pairs(y_ref, o * r * nw_ref[...] * _silu(_pairs(z_ref, K)), K)
        _put_pairs(vn_ref, vn, K)
        _put_pairs(o_ref, o, K)

    wide_f32 = jax.ShapeDtypeStruct((T, H * HEAD_DIM), f32)
    return pl.pallas_call(
        body, grid=(NS,),
        in_specs=[wide(0), wide(1), wide(2), wide(3), one(4 * H), vec, vec, vec],
        out_specs=[wide(0), st_spec(HEAD_DIM), st_spec(C), wide(0), wide(0)],
        out_shape=[jax.ShapeDtypeStruct((T, H * HEAD_DIM), bf16), jax.ShapeDtypeStruct((N, H, HEAD_DIM, HEAD_DIM), f32),
                   jax.ShapeDtypeStruct((N, H, C, C), bf16), jax.ShapeDtypeStruct((T, H * HEAD_DIM), bf16), wide_f32],
        scratch_shapes=[pltpu.VMEM((H, HEAD_DIM, HEAD_DIM), f32)],
        compiler_params=_cparams(("arbitrary",)), name=name,
    )(qkv, qkv, qkv, p, p, alog, dtb, nw)


def _gdn_bwd(qkv, p, alog, dtb, nw, sprev, tinv, vn_all, o_all, dy_all, name):
    T = qkv.shape[0]
    C, H = GDN_CHUNK, GDN_HEADS
    N = T // C
    K, NS, wide, one, vec, st_spec = _gdn_specs(T, True)
    rs = lambda m: jnp.sum(m, axis=2, keepdims=True)

    def body(q_ref, k_ref, v_ref, z_ref, ba_ref, al_ref, dt_ref, nw_ref, sp_ref, ti_ref, vn_ref, o_ref, dy_ref,
             dqkv_ref, dz_ref, dba_ref, dal_ref, ddt_ref, dnw_ref, dst):
        @pl.when(pl.program_id(0) == 0)
        def _():
            dst[...] = jnp.zeros_like(dst)
            dal_ref[...] = jnp.zeros_like(dal_ref)
            ddt_ref[...] = jnp.zeros_like(ddt_ref)
            dnw_ref[...] = jnp.zeros_like(dnw_ref)

        ba, dtb_, nwv = ba_ref[...], dt_ref[...], nw_ref[...]
        v = _pairs(v_ref, K)
        f = _gdn_pre(_pairs(q_ref, K), _pairs(k_ref, K), v, ba, al_ref[...], dtb_)
        beta, kn, qn, kb, vb, kbg = f["beta"], f["kn"], f["qn"], f["kb"], f["vb"], f["kbg"]
        eG, eGl, cd, Di, Ds, QK, qg, kg = f["eG"], f["eGl"], f["cd"], f["Di"], f["Ds"], f["QK"], f["qg"], f["kg"]
        lane, ri, ci, rowc = f["lane"], f["ri"], f["ci"], f["rowc"]
        Tinv = ti_ref[...].reshape(K * H, C, C)
        S = sp_ref[...].reshape(K * H, HEAD_DIM, HEAD_DIM)
        w_ = _dot(Tinv, kbg, _NN3)
        vn, o = _pairs(vn_ref, K), _pairs(o_ref, K)
        z, dy = _pairs(z_ref, K), _pairs(dy_ref, K)
        r = lax.rsqrt(_rowsum(o * o) * (1.0 / HEAD_DIM) + EPS)
        nrm = o * r
        sz = _silu(z)
        dn = dy * nwv * sz
        _put_pairs(dz_ref, dy * nrm * nwv * _dsilu(z), K)
        dnw_ref[...] += jnp.sum(jnp.sum(dy * nrm * sz, axis=0), axis=0, keepdims=True)
        do = r * (dn - nrm * (_rowsum(dn * nrm) * (1.0 / HEAD_DIM)))
        dvn_do = _dot(QK, do, _TN3)
        dS_do = _dot(qg, do, _TN3)
        dqg = _dot(do, S, _NT3)
        dQK = _dot(do, vn, _NT3)
        dS = dst[...]
        dS1s, dvns = [None] * K, [None] * K
        for c in reversed(range(K)):
            sl = slice(c * H, (c + 1) * H)
            dS1s[c] = dS
            dvns[c] = _dot(kg[sl], dS, _NN3) + dvn_do[sl]
            dS = cd[sl] * dS + dS_do[sl] - _dot(w_[sl], dvns[c], _TN3)
        dst[...] = dS
        dS1, dvn = jnp.concatenate(dS1s), jnp.concatenate(dvns)
        dcd = jnp.sum(jnp.sum(S * dS1, axis=2, keepdims=True), axis=1, keepdims=True)
        dkg = _dot(vn, dS1, _NT3)
        dw = -_dot(dvn, S, _NT3)
        dqn = dqg * eG
        dkn = dkg * eGl
        deGl = rs(dkg * kn)
        dQKr = dQK * Di
        E = dQK * QK
        dqn = dqn + _dot(dQKr, kn, _NN3)
        dkn = dkn + _dot(dQKr, qn, _TN3)
        dT = _dot(dvn, vb, _NT3) + _dot(dw, kbg, _NT3)
        dvb = _dot(Tinv, dvn, _TN3)
        dkbg = _dot(Tinv, dw, _TN3)
        dkb = dkbg * eG
        deG = rs(dqg * qn + dkbg * kb)
        dL = -_dot(_dot(Tinv, dT, _TN3), Tinv, _NT3)
        dKK = dL * Ds
        E = E + dL * f["Lm"]
        dkb = dkb + _dot(dKK, kn, _NN3)
        dkn = dkn + _dot(dKK, kb, _TN3) + dkb * beta
        dbeta = rs(dkb * kn + dvb * v)
        _put_pairs(dqkv_ref, dvb * beta, K, 2 * H * HEAD_DIM)
        dG = rs(E) - rs(jnp.swapaxes(E, 1, 2)) + deG * eG - deGl * eGl
        dGl = jnp.sum(deGl * eGl, axis=1, keepdims=True) + dcd * cd
        dG = dG + jnp.where(rowc == C - 1, dGl, 0.0)
        qh = f["qh"]
        _put_pairs(dqkv_ref, (HEAD_DIM ** -0.5) * f["rq"] * (dqn - qh * _rowsum(dqn * qh)), K)
        _put_pairs(dqkv_ref, f["rk"] * (dkn - kn * _rowsum(dkn * kn)), K, H * HEAD_DIM)
        db = dbeta * beta * (1.0 - beta)
        triu = (ri <= ci).astype(f32)
        for c in range(K):
            db_all = jnp.where(lane == 0, db[c * H], 0.0)
            dG_all = jnp.where(lane == H, dG[c * H], 0.0)
            for h in range(1, H):
                db_all = db_all + jnp.where(lane == h, db[c * H + h], 0.0)
                dG_all = dG_all + jnp.where(lane == H + h, dG[c * H + h], 0.0)
            dg_all = _dot01(triu, dG_all, _NN)
            da_all = dg_all * (-f["ea"]) * _sigmoid(ba[c * C:(c + 1) * C] + dtb_)
            dba_ref[c * C:(c + 1) * C, :] = (db_all + da_all).astype(bf16)
            ddt_ref[...] += jnp.sum(da_all, axis=0, keepdims=True)
            dal_ref[...] += jnp.sum(dg_all * f["g_all"][c], axis=0, keepdims=True)

    small = jax.ShapeDtypeStruct((1, 128), f32)
    return pl.pallas_call(
        body, grid=(NS,),
        in_specs=[wide(0), wide(1), wide(2), wide(3), one(4 * H), vec, vec, vec, st_spec(HEAD_DIM), st_spec(C),
                  wide(0), wide(0), wide(0)],
        out_specs=[pl.BlockSpec((K * C, 3 * H * HEAD_DIM), lambda n: (NS - 1 - n, 0)), wide(3), one(0), vec, vec, vec],
        out_shape=[jax.ShapeDtypeStruct((T, 3 * H * HEAD_DIM), f32), jax.ShapeDtypeStruct((T, ODD_PAD), bf16),
                   jax.ShapeDtypeStruct((T, 128), bf16), small, small, small],
        scratch_shapes=[pltpu.VMEM((H, HEAD_DIM, HEAD_DIM), f32)],
        compiler_params=_cparams(("arbitrary",)), name=name,
    )(qkv, qkv, qkv, p, p, alog, dtb, nw, sprev, tinv, vn_all, o_all, dy_all)


def _lanes_from(x, s):
    return x if s % 128 == 0 else pltpu.roll(x, (128 - s) % 128, 1)


def _odd_assemble(g, name):
    R = g.shape[1]
    tr = min(256, R)
    n_blk = ODD_SHARD_PAD // 128

    def body(g_ref, o_ref):
        lane = lax.broadcasted_iota(jnp.int32, (tr, 128), 1)
        blk = lambda d, m: g_ref[d, :, m * 128:(m + 1) * 128]
        for gb in range(ODD_PAD // 128):
            c0 = 128 * gb
            if c0 >= ODD_IN:
                o_ref[:, c0:c0 + 128] = jnp.zeros((tr, 128), g.dtype)
                continue
            d0 = c0 // ODD_SHARD
            m0, sh = divmod(c0 - ODD_SHARD * d0, 128)
            take = min(128, ODD_SHARD * (d0 + 1) - c0)
            p = _lanes_from(blk(d0, m0), sh)
            if sh and m0 + 1 < n_blk:
                p = jnp.where(lane < 128 - sh, p, _lanes_from(blk(d0, m0 + 1), sh))
            if take < 128:
                nxt = pltpu.roll(blk(d0 + 1, 0), take, 1) if d0 + 1 < N_DEV else jnp.zeros((tr, 128), g.dtype)
                p = jnp.where(lane < take, p, nxt)
            o_ref[:, c0:c0 + 128] = p

    return pl.pallas_call(
        body, grid=(R // tr,),
        in_specs=[pl.BlockSpec((N_DEV, tr, ODD_SHARD_PAD), lambda i: (0, i, 0))],
        out_specs=pl.BlockSpec((tr, ODD_PAD), lambda i: (i, 0)),
        out_shape=jax.ShapeDtypeStruct((R, ODD_PAD), g.dtype),
        compiler_params=_cparams(("parallel",)), name=name,
    )(g)


def _odd_split(w, name):
    R = w.shape[0]
    tr = min(256, R)

    def body(w_ref, o_ref):
        lane = lax.broadcasted_iota(jnp.int32, (tr, 128), 1)
        blk = lambda gb: w_ref[:, gb * 128:(gb + 1) * 128]
        for d in range(N_DEV):
            for m in range(ODD_SHARD_PAD // 128):
                g0, sh = divmod(ODD_SHARD * d + 128 * m, 128)
                p = _lanes_from(blk(g0), sh)
                if sh and g0 + 1 < ODD_PAD // 128:
                    p = jnp.where(lane < 128 - sh, p, _lanes_from(blk(g0 + 1), sh))
                real = ODD_SHARD - 128 * m
                if real < 128:
                    p = jnp.where(lane < real, p, jnp.zeros_like(p))
                o_ref[d, :, m * 128:(m + 1) * 128] = p

    return pl.pallas_call(
        body, grid=(R // tr,),
        in_specs=[pl.BlockSpec((tr, ODD_PAD), lambda i: (i, 0))],
        out_specs=pl.BlockSpec((N_DEV, tr, ODD_SHARD_PAD), lambda i: (0, i, 0)),
        out_shape=jax.ShapeDtypeStruct((N_DEV, R, ODD_SHARD_PAD), w.dtype),
        compiler_params=_cparams(("parallel",)), name=name,
    )(w)


def _adam_tile(g, w_ref, m_ref, v_ref, go_ref, d_ref, mo_ref, vo_ref):
    c1 = 1.0 - ADAM_B1 ** ADAM_STEP
    c2 = 1.0 - ADAM_B2 ** ADAM_STEP
    mn = ADAM_B1 * m_ref[...] + (1.0 - ADAM_B1) * g
    vn = ADAM_B2 * v_ref[...] + (1.0 - ADAM_B2) * (g * g)
    go_ref[...] = g
    mo_ref[...] = mn
    vo_ref[...] = vn
    d_ref[...] = -ADAM_LR * ((mn / c1) / (jnp.sqrt(vn / c2) + ADAM_EPS) + ADAM_WD * w_ref[...])


def _adamw(w, gs, m, v, name, layer=None, prev=None):
    R, Cc = w.shape[-2:]
    S = gs.shape[0]
    tr = R
    if S * R * Cc * 4 > (4 << 20):
        for cand in (256, 128, 64, 32, 16, 8):
            if R % cand == 0 and R > cand:
                tr = cand
                break

    def body(w_ref, g_ref, m_ref, v_ref, *rest):
        g = g_ref[0].astype(f32)
        for s in range(1, S):
            g = g + g_ref[s].astype(f32)
        _adam_tile(g, w_ref, m_ref, v_ref, *rest[-4:])

    if layer is None:
        blk = pl.BlockSpec((tr, Cc), lambda i: (i, 0))
    else:
        blk = pl.BlockSpec((None, tr, Cc), lambda i: (layer, i, 0))
    out = jax.ShapeDtypeStruct(w.shape, f32)
    carried = [] if prev is None else list(prev)
    return pl.pallas_call(
        body, grid=(R // tr,),
        in_specs=[blk, pl.BlockSpec((S, tr, Cc), lambda i: (0, i, 0)), blk, blk]
        + [pl.BlockSpec(memory_space=pl.ANY)] * len(carried),
        out_specs=[blk] * 4, out_shape=[out] * 4,
        input_output_aliases={4 + j: j for j in range(len(carried))},
        compiler_params=_cparams(("parallel",)), name=name,
    )(w, gs, m, v, *carried)


def _adamw_column_major(w, gs, m, v, name):
    _, R, Cc = w.shape
    S = gs.shape[0]
    q = R // 128
    dense = lambda a: jnp.transpose(a, (2, 0, 1)).reshape(Cc * q, 128)
    back = lambda a: jnp.transpose(a.reshape(Cc, q, 128), (1, 2, 0)).reshape(1, R, Cc)

    def body(w_ref, g_ref, m_ref, v_ref, go_ref, d_ref, mo_ref, vo_ref, gt):
        for i in range(q):
            g = g_ref[0, i * 128:(i + 1) * 128, :].astype(f32)
            for s in range(1, S):
                g = g + g_ref[s, i * 128:(i + 1) * 128, :].astype(f32)
            gt[pl.ds(i, 128, stride=q), :] = g.T
        _adam_tile(gt[...], w_ref, m_ref, v_ref, go_ref, d_ref, mo_ref, vo_ref)

    blk = pl.BlockSpec((128 * q, 128), lambda j: (j, 0))
    outs = pl.pallas_call(
        body, grid=(pl.cdiv(Cc, 128),),
        in_specs=[blk, pl.BlockSpec((S, R, 128), lambda j: (0, 0, j)), blk, blk],
        out_specs=[blk] * 4, out_shape=[jax.ShapeDtypeStruct((Cc * q, 128), f32)] * 4,
        scratch_shapes=[pltpu.VMEM((128 * q, 128), f32)],
        compiler_params=_cparams(("parallel",)), name=name,
    )(dense(w), gs, dense(m), dense(v))
    return [back(o) for o in outs]


def _me():
    x, y, c = lax.axis_index("x"), lax.axis_index("y"), lax.axis_index("c")
    return x, y, c, 4 * x + 2 * y + c


def _peer(k):
    x, y, c, _ = _me()
    px = 1 - x if k & 4 else x
    py = 1 - y if k & 2 else y
    pc = 1 - c if k & 1 else c
    return (px, py, pc), 4 * px + 2 * py + pc


_HBM = pl.BlockSpec(memory_space=pltpu.HBM)
_SEM = pl.BlockSpec(memory_space=pltpu.SEMAPHORE)
_EFFECT = pltpu.SideEffectType.DATAFLOW_SIDE_EFFECTING


def _copy(src, land, ssem, rsem, k, blocked, landing_slot_of_peer):
    pid, pidx = _peer(k)
    slot = pidx if landing_slot_of_peer else _me()[3]
    return pltpu.make_async_remote_copy(src_ref=src.at[pidx] if blocked else src, dst_ref=land.at[slot],
                                        send_sem=ssem.at[k - 1], recv_sem=rsem.at[k - 1], device_id=pid, device_id_type=MESH)


def _send_start(srcs, blocked, name):
    n = len(srcs)
    lands = [lax.empty(a.shape if blocked else (N_DEV,) + a.shape, a.dtype) for a in srcs]

    def body(*refs):
        src, land, sems, token = refs[:n], refs[n:2 * n], refs[2 * n:4 * n], refs[-1]
        for i in range(n):
            for k in range(1, N_DEV):
                _copy(src[i], land[i], sems[2 * i], sems[2 * i + 1], k, blocked, False).start()
        token[...] = jnp.zeros_like(token)

    sem = pltpu.SemaphoreType.DMA((N_DEV - 1,))
    hbm = lambda a: pltpu.with_memory_space_constraint(a, pltpu.HBM)
    res = pl.pallas_call(
        body, name=name,
        out_shape=tuple([sem] * (2 * n)) + tuple(pltpu.HBM(a.shape, a.dtype) for a in srcs + lands)
        + (jax.ShapeDtypeStruct((8, 128), f32),),
        in_specs=(_HBM,) * (2 * n),
        out_specs=(_SEM,) * (2 * n) + (_HBM,) * (2 * n) + (pl.BlockSpec(memory_space=pltpu.VMEM),),
        input_output_aliases={j: 2 * n + j for j in range(2 * n)},
        compiler_params=pltpu.CompilerParams(has_side_effects=_EFFECT),
    )(*[hbm(a) for a in srcs], *[hbm(a) for a in lands])
    handles = [(res[2 * i], res[2 * i + 1], res[2 * n + i], res[3 * n + i]) for i in range(n)]
    return handles, res[-1]


def _send_wait(handle, blocked, after, name):
    ssem, rsem, src, land = handle

    def body(src_ref, land_ref, ssem_ref, rsem_ref, after_ref, src_out, land_out):
        for k in range(1, N_DEV):
            cp = _copy(src_ref, land_ref, ssem_ref, rsem_ref, k, blocked, True)
            cp.wait_send()
            cp.wait_recv()

    return pl.pallas_call(
        body, name=name, out_shape=(pltpu.HBM(src.shape, src.dtype), pltpu.HBM(land.shape, land.dtype)),
        in_specs=(_HBM, _HBM, _SEM, _SEM, pl.BlockSpec(memory_space=pl.ANY)), out_specs=(_HBM, _HBM),
        input_output_aliases={0: 0, 1: 1}, compiler_params=pltpu.CompilerParams(has_side_effects=_EFFECT),
    )(src, land, ssem, rsem, after)


def _block_diag(w):
    nb, bs = w.shape[0], w.shape[1]
    eye = jnp.eye(nb, dtype=w.dtype)
    return (eye[:, None, :, None] * w[:, :, None, :]).reshape(nb * bs, nb * bs)


def _diag_blocks(d):
    return jnp.stack([d[g, s * 64:(s + 1) * 64, s * 64:(s + 1) * 64] for g in range(4) for s in range(2)])


_SQUARE_TILES = dict(tm=1024, tn=1024, tk=1024)


def _mlp_fwd(x, hm, wu, wd, tag, epilogue, extras, outs):
    (r,) = _matmul(hm, wu, "nn", outs=[bf16], epilogue=lambda acc: (jnp.maximum(acc, 0.0),), name=f"mlp_up_{tag}")
    res = _matmul(r, wd, "nn", outs=outs, extras=(x,) + tuple(extras), epilogue=epilogue, a_map=jnp.square,
                  name=f"mlp_down_{tag}", **_SQUARE_TILES)
    return res, (hm, r)


def _mlp_bwd(x, nw, wu, wd, saved, dxo, dxo_b, tag, sink):
    hm, r = saved
    (du,) = _matmul(dxo_b, wd, "nt", outs=[bf16], extras=(r,), epilogue=lambda acc, rr: (acc * (2.0 * rr.astype(f32)),),
                    name=f"mlp_dact_{tag}")
    (dwd,) = _matmul(r, dxo_b, "tn", outs=[bf16], a_map=jnp.square, name=f"mlp_dwd_{tag}", **_SQUARE_TILES)
    tok = sink({f"w_down{tag}": dwd.reshape(N_DEV, D_FF // N_DEV, D_MODEL)})
    (dwu,) = _matmul(hm, du, "tn", outs=[bf16], shard_cols=2, extras=(jnp.broadcast_to(tok[0:1, 0:1], (1, D_FF)),),
                     epilogue=lambda acc, zero: (acc + zero,), name=f"mlp_dwu_{tag}")
    tok = sink({f"w_up{tag}": dwu})
    return _matmul(du, wu, "nt", outs=_RMS_BWD_OUTS, extras=(x, dxo, nw + tok[0:1, 0:1]), epilogue=_rms_bwd_ep,
                   name=f"mlp_dh_{tag}", **_SQUARE_TILES)


def _local_step(x, tgt, P, weight, sink):
    T = x.shape[0]
    cos, sin = _rope_tables(T)
    rtab = _ret_tables()
    row = lambda a: a.reshape(1, -1)
    mix_nw, mlp_nw = P["mixer_norm_w"], P["mlp_norm_w"]
    wr_bd, wi_bd = _block_diag(P["lru_w_r"]), _block_diag(P["lru_w_i"])
    lru_b, lru_br, lru_bi, lru_lam = row(P["lru_conv_b"]), row(P["lru_b_r"]), row(P["lru_b_i"]), row(P["lru_lambda"])
    pad16 = lambda a: jnp.pad(a.reshape(1, GDN_HEADS), ((0, 0), (GDN_HEADS, 128 - 2 * GDN_HEADS)))
    alog, dtb = pad16(P["gdn_a_log"]), pad16(P["gdn_dt_bias"])
    gnw = row(P["gdn_norm_w"])

    x0 = x
    h0 = _rms_fwd(x0, mix_nw[0:1], "rms_mix_0")
    w_ie = weight("w_in_even", h0)
    (pe,) = _matmul(h0, w_ie, "nn", outs=[f32], name="in_even")
    mix0, o_ret, s_ret = _ret_fwd(pe, cos, sin, rtab, "ret_fwd")
    w_lc = weight("lru_conv_w", pe)
    xc = _conv_fwd(pe, 4, w_lc, lru_b, False, "lru_conv_fwd")
    mix0, h_lru = _lru_fwd(xc, pe, 20, wr_bd, wi_bd, lru_br, lru_bi, lru_lam, mix0, "lru_fwd")
    w_oe = weight("w_out_even", mix0)
    x1, hm0 = _matmul(mix0, w_oe, "nn", outs=[f32, bf16], extras=(x0, mlp_nw[0:1]), epilogue=_residual_rms_ep,
                      name="out_even", tm=1024, tn=D_MODEL)
    w_u0, w_d0 = weight("w_up0", x1), weight("w_down0", x1)
    (x2, h1), mlp0 = _mlp_fwd(x1, hm0, w_u0, w_d0, "0", _residual_rms_ep, (mix_nw[1:2],), [f32, bf16])
    w_io = weight("w_in_odd", h1)
    (po,) = _matmul(h1, w_io, "nn", outs=[f32], tm=2048, tn=ODD_PAD // 3, name="in_odd")
    w_gc = weight("gdn_conv_w", po)
    qkv = _conv_fwd(po, 0, w_gc, None, True, "gdn_conv_fwd")
    y_gdn, s_gdn, ti_gdn, vn_gdn, o_gdn = _gdn_fwd(qkv, po, alog, dtb, gnw, "gdn_fwd")
    w_oo = weight("w_out_odd", y_gdn)
    x3, hm1 = _matmul(y_gdn, w_oo, "nn", outs=[f32, bf16], extras=(x2, mlp_nw[1:2]), epilogue=_residual_rms_ep,
                      name="out_odd", tm=1024, tn=D_MODEL)
    w_u1, w_d1 = weight("w_up1", x3), weight("w_down1", x3)
    (loss, dx4, dx4_b, d_final), mlp1 = _mlp_fwd(x3, hm1, w_u1, w_d1, "1", _loss_ep, (row(P["final_norm_w"]), tgt),
                                                 _LOSS_OUTS)
    dx3, dx3_b, d_mlp_nw1 = _mlp_bwd(x3, mlp_nw[1:2], w_u1, w_d1, mlp1, dx4, dx4_b, "1", sink)
    (dy_gdn,) = _matmul(dx3_b, w_oo, "nt", outs=[f32], name="out_odd_dx")
    (d_woo,) = _matmul(y_gdn, dx3_b, "tn", outs=[bf16], name="out_odd_dw")
    dqkv, dpo, dba, d_alog, d_dtb, d_gnw = _gdn_bwd(qkv, po, alog, dtb, gnw, s_gdn, ti_gdn, vn_gdn, o_gdn, dy_gdn,
                                                  "gdn_bwd")
    dpo, d_gconv, _ = _conv_bwd(po, 0, w_gc, None, True, dqkv, dpo, "gdn_conv_bwd")
    dpo = lax.dynamic_update_slice(dpo, dba, (0, 4 * D_MODEL))
    (d_wio,) = _matmul(h1, dpo, "tn", outs=[bf16], tn=ODD_PAD // 3, name="in_odd_dw")
    tok = sink(dict(w_out_odd=d_woo.reshape(N_DEV, D_MODEL // N_DEV, D_MODEL), w_in_odd=_odd_split(d_wio, "w_in_odd_split")))
    dx2, dx2_b, d_mix_nw1 = _matmul(dpo, w_io, "nt", outs=_RMS_BWD_OUTS, extras=(x2, dx3, mix_nw[1:2] + tok[0:1, 0:1]),
                                    epilogue=_rms_bwd_ep, tm=1024, tn=1024, tk=ODD_PAD // 3, name="in_odd_dx")
    dx1, dx1_b, d_mlp_nw0 = _mlp_bwd(x1, mlp_nw[0:1], w_u0, w_d0, mlp0, dx2, dx2_b, "0", sink)
    (d_woe,) = _matmul(mix0, dx1_b, "tn", outs=[bf16], name="out_even_dw")
    tok = sink(dict(w_out_even=d_woe.reshape(N_DEV, D_MODEL // N_DEV, D_MODEL)))
    (dmix0,) = _matmul(dx1_b, w_oe, "nt", outs=[f32], name="out_even_dx")
    dpe = _ret_bwd(pe, cos, sin, rtab, o_ret, s_ret, dmix0, "ret_bwd")
    dxc, dpe, d_wr, d_wi, d_br, d_bi, d_lam = _lru_bwd(xc, pe, 20, wr_bd, wi_bd, lru_br, lru_bi, lru_lam + tok[0:1, 0:1],
                                                       h_lru, dmix0, 4, dpe, "lru_bwd")
    dpe, d_lconv, d_lconv_b = _conv_bwd(pe, 4, w_lc, lru_b, False, dxc, dpe, "lru_conv_bwd")
    G = dict(
        mlp_norm_w=jnp.concatenate([d_mlp_nw0, d_mlp_nw1], axis=0),
        final_norm_w=d_final.reshape(-1),
        lru_conv_w=d_lconv, lru_conv_b=d_lconv_b.reshape(-1),
        lru_w_r=_diag_blocks(d_wr), lru_b_r=d_br.reshape(-1), lru_w_i=_diag_blocks(d_wi), lru_b_i=d_bi.reshape(-1),
        lru_lambda=d_lam.reshape(-1), gdn_conv_w=d_gconv,
        gdn_a_log=d_alog[0, GDN_HEADS:2 * GDN_HEADS], gdn_dt_bias=d_dtb[0, GDN_HEADS:2 * GDN_HEADS],
        gdn_norm_w=d_gnw.reshape(-1),
    )
    packed = _pack([G[k] for k in _SMALL] + [d_lconv, d_gconv, loss[0, 0:1]])
    tok = sink(dict(small=jnp.broadcast_to(packed[None], (N_DEV,) + packed.shape)))
    (d_wie,) = _matmul(h0, dpe, "tn", outs=[bf16], shard_cols=2, extras=(jnp.broadcast_to(tok[0:1, 0:1], (1, 3 * D_MODEL)),),
                       epilogue=lambda acc, zero: (acc + zero,), name="in_even_dw")
    tok = sink(dict(w_in_even=d_wie))
    dx0, _, d_mix_nw0 = _matmul(dpe, w_ie, "nt", outs=_RMS_BWD_OUTS, extras=(x0, dx1, mix_nw[0:1] + tok[0:1, 0:1]),
                                epilogue=_rms_bwd_ep, name="in_even_dx", **_SQUARE_TILES)
    G["mixer_norm_w"] = jnp.concatenate([d_mix_nw0, d_mix_nw1], axis=0)
    return loss, dx0, G


_SMALL = ["mlp_norm_w", "final_norm_w", "lru_conv_b", "lru_w_r", "lru_b_r", "lru_w_i", "lru_b_i",
          "lru_lambda", "gdn_a_log", "gdn_dt_bias", "gdn_norm_w"]
_PACK_ROWS = 688


def _pack(parts):
    flat = jnp.concatenate([p.reshape(-1) for p in parts])
    return jnp.pad(flat, (0, _PACK_ROWS * 128 - flat.shape[0])).reshape(_PACK_ROWS, 128)


def _unpack(packed, shapes):
    flat = packed.reshape(-1)
    out, off = [], 0
    for s in shapes:
        n = int(np.prod(s))
        out.append(flat[off:off + n].reshape(s))
        off += n
    return out


def kernel(x, mixer_norm_w, mlp_norm_w, final_norm_w, w_in_even, lru_conv_w, lru_conv_b, lru_w_r, lru_b_r, lru_w_i, lru_b_i, lru_lambda, w_out_even, w_in_odd, gdn_conv_w, gdn_a_log, gdn_dt_bias, gdn_norm_w, w_out_odd, w_up, w_down, loss_target, m_mixer_norm_w, m_mlp_norm_w, m_final_norm_w, m_w_in_even, m_lru_conv_w, m_lru_conv_b, m_lru_w_r, m_lru_b_r, m_lru_w_i, m_lru_b_i, m_lru_lambda, m_w_out_even, m_w_in_odd, m_gdn_conv_w, m_gdn_a_log, m_gdn_dt_bias, m_gdn_norm_w, m_w_out_odd, m_w_up, m_w_down, v_mixer_norm_w, v_mlp_norm_w, v_final_norm_w, v_w_in_even, v_lru_conv_w, v_lru_conv_b, v_lru_w_r, v_lru_b_r, v_lru_w_i, v_lru_b_i, v_lru_lambda, v_w_out_even, v_w_in_odd, v_gdn_conv_w, v_gdn_a_log, v_gdn_dt_bias, v_gdn_norm_w, v_w_out_odd, v_w_up, v_w_down):
    Pw = dict(mixer_norm_w=mixer_norm_w, mlp_norm_w=mlp_norm_w, final_norm_w=final_norm_w, w_in_even=w_in_even,
              lru_conv_w=lru_conv_w, lru_conv_b=lru_conv_b, lru_w_r=lru_w_r, lru_b_r=lru_b_r, lru_w_i=lru_w_i,
              lru_b_i=lru_b_i, lru_lambda=lru_lambda, w_out_even=w_out_even, w_in_odd=w_in_odd, gdn_conv_w=gdn_conv_w,
              gdn_a_log=gdn_a_log, gdn_dt_bias=gdn_dt_bias, gdn_norm_w=gdn_norm_w, w_out_odd=w_out_odd, w_up=w_up,
              w_down=w_down)
    Pm = dict(mixer_norm_w=m_mixer_norm_w, mlp_norm_w=m_mlp_norm_w, final_norm_w=m_final_norm_w, w_in_even=m_w_in_even,
              lru_conv_w=m_lru_conv_w, lru_conv_b=m_lru_conv_b, lru_w_r=m_lru_w_r, lru_b_r=m_lru_b_r, lru_w_i=m_lru_w_i,
              lru_b_i=m_lru_b_i, lru_lambda=m_lru_lambda, w_out_even=m_w_out_even, w_in_odd=m_w_in_odd,
              gdn_conv_w=m_gdn_conv_w, gdn_a_log=m_gdn_a_log, gdn_dt_bias=m_gdn_dt_bias, gdn_norm_w=m_gdn_norm_w,
              w_out_odd=m_w_out_odd, w_up=m_w_up, w_down=m_w_down)
    Pv = dict(mixer_norm_w=v_mixer_norm_w, mlp_norm_w=v_mlp_norm_w, final_norm_w=v_final_norm_w, w_in_even=v_w_in_even,
              lru_conv_w=v_lru_conv_w, lru_conv_b=v_lru_conv_b, lru_w_r=v_lru_w_r, lru_b_r=v_lru_b_r, lru_w_i=v_lru_w_i,
              lru_b_i=v_lru_b_i, lru_lambda=v_lru_lambda, w_out_even=v_w_out_even, w_in_odd=v_w_in_odd,
              gdn_conv_w=v_gdn_conv_w, gdn_a_log=v_gdn_a_log, gdn_dt_bias=v_gdn_dt_bias, gdn_norm_w=v_gdn_norm_w,
              w_out_odd=v_w_out_odd, w_up=v_w_up, w_down=v_w_down)
    me = _me()[3]
    T = x.shape[1]

    cols = lambda g: jnp.transpose(g, (1, 0, 2)).reshape(g.shape[1], -1)
    rows = lambda g: g.reshape(-1, g.shape[2])
    wide = lambda g: _odd_assemble(g, "w_in_odd_assemble")
    odd_shard = jnp.pad(w_in_odd[0].astype(bf16), ((0, 0), (0, ODD_SHARD_PAD - ODD_SHARD)))
    as_is = lambda g: g
    gather = dict(
        w_in_even=(w_in_even[0].astype(bf16), cols), lru_conv_w=(lru_conv_w[0], cols),
        w_out_even=(w_out_even[0].astype(bf16), rows), w_up0=(w_up[0].astype(bf16), as_is), w_down0=(w_down[0].astype(bf16), rows),
        w_in_odd=(odd_shard, wide), gdn_conv_w=(gdn_conv_w[0], cols),
        w_out_odd=(w_out_odd[0].astype(bf16), rows), w_up1=(w_up[1].astype(bf16), as_is), w_down1=(w_down[1].astype(bf16), rows))
    handles, tok = _send_start([s for s, _ in gather.values()], False, "gather_start")
    handles = dict(zip(gather, handles))
    full = {}

    def weight(name, after):
        if name not in full:
            shard, landed = _send_wait(handles[name], False, after, f"gather_wait_{name}")
            full[name] = gather[name][1](lax.dynamic_update_slice_in_dim(landed, shard[None], me, 0))
        return full[name]

    P = {k: Pw[k] for k in ("mlp_norm_w", "final_norm_w")}
    P["mixer_norm_w"] = mixer_norm_w + tok[0:1, 0:1]
    for k in ("lru_w_r", "lru_w_i", "lru_conv_b", "lru_b_r", "lru_b_i", "lru_lambda", "gdn_a_log", "gdn_dt_bias", "gdn_norm_w"):
        P[k] = Pw[k][0]

    sent = {}

    def sink(grads):
        hs, token = _send_start(list(grads.values()), True, "grads_start_" + "_".join(grads))
        sent.update(zip(grads, hs))
        return token

    loss, dx, G = _local_step(x[0], loss_target[0], P, weight, sink)
    lanes = lambda a: a.reshape(-1, 128)
    sink(dict(mixer_norm_w=jnp.broadcast_to(lanes(G["mixer_norm_w"])[None], (N_DEV, 2 * D_MODEL // 128, 128))))

    def received(name, after=dx):
        g, landed = _send_wait(sent[name], True, after, f"grads_wait_{name}")
        return lax.dynamic_update_slice_in_dim(landed, lax.dynamic_slice_in_dim(g, me, 1, 0), me, 0)

    out = {}
    nff = D_FF // N_DEV

    def whole(name, gs):
        out[name] = tuple(_adamw(Pw[name], gs, Pm[name], Pv[name], f"adamw_{name}", layer=0))

    def layers(name):
        res = None
        for l in range(2):
            res = _adamw(Pw[name], received(f"{name}{l}"), Pm[name], Pv[name], f"adamw_{name}{l}", layer=l, prev=res)
        out[name] = tuple(res)

    layers("w_up")
    layers("w_down")
    whole("w_out_odd", received("w_out_odd"))
    out["w_in_odd"] = tuple(_adamw_column_major(w_in_odd, received("w_in_odd"), m_w_in_odd, v_w_in_odd, "adamw_w_in_odd"))
    whole("w_out_even", received("w_out_even"))
    small_shapes = [Pw[k].shape for k in _SMALL]
    pw, pm, pv = (_pack([Q[k] for k in _SMALL]) for Q in (Pw, Pm, Pv))
    sg, sd, sm, sv = _adamw(pw, received("small", out["w_out_even"][1]), pm, pv, "adamw_small")
    for arrs_i, packed_out in enumerate((sg, sd, sm, sv)):
        for k, a in zip(_SMALL, _unpack(packed_out, small_shapes)):
            out.setdefault(k, [None] * 4)[arrs_i] = a
    whole("w_in_even", received("w_in_even", sd))
    out["mixer_norm_w"] = tuple(
        a.reshape(mixer_norm_w.shape) for a in
        _adamw(lanes(mixer_norm_w), received("mixer_norm_w", out["w_in_even"][1]), lanes(m_mixer_norm_w),
               lanes(v_mixer_norm_w), "adamw_mixer_norm_w"))
    n_small = sum(int(np.prod(s)) for s in small_shapes)
    gflat = sg.reshape(-1)
    g_lconv = gflat[n_small:n_small + CONV_K * LRU_WIDTH].reshape(CONV_K, LRU_WIDTH)
    g_gconv = gflat[n_small + CONV_K * LRU_WIDTH:n_small + CONV_K * (LRU_WIDTH + 3072)].reshape(CONV_K, 3072)
    whole("lru_conv_w", lax.dynamic_slice_in_dim(g_lconv, me * 64, 64, axis=1)[None])
    whole("gdn_conv_w", lax.dynamic_slice_in_dim(g_gconv, me * 384, 384, axis=1)[None])

    names = ["mixer_norm_w", "mlp_norm_w", "final_norm_w", "w_in_even", "lru_conv_w", "lru_conv_b", "lru_w_r", "lru_b_r",
             "lru_w_i", "lru_b_i", "lru_lambda", "w_out_even", "w_in_odd", "gdn_conv_w", "gdn_a_log", "gdn_dt_bias",
             "gdn_norm_w", "w_out_odd", "w_up", "w_down"]
    total = gflat[n_small + CONV_K * (LRU_WIDTH + 3072)]
    res = [total, dx[None]]
    for j in range(4):
        res += [out[k][j] for k in names]
    return tuple(res)
```

```python
import math

import numpy as np
import jax
import jax.numpy as jnp
from jax import lax
from jax.experimental import pallas as pl
from jax.experimental.pallas import tpu as pltpu

f32 = jnp.float32
bf16 = jnp.bfloat16

N_DEV = 8
D_MODEL = 1024
D_FF = 4096
EPS = 1e-6
RET_HEADS = 4
RET_CHUNK = 128
RET_STEP = 4
ROPE_THETA = 10000.0
LRU_WIDTH = 512
LRU_C = 8.0
GDN_HEADS = 8
GDN_CHUNK = 64
GDN_STEP = 4
HEAD_DIM = 128
ODD_IN = 4112
ODD_PAD = 4224
ODD_SHARD = ODD_IN // N_DEV
ODD_SHARD_PAD = 640
ADAM_LR, ADAM_B1, ADAM_B2, ADAM_EPS, ADAM_WD, ADAM_STEP = 0.001, 0.9, 0.999, 1e-08, 0.01, 10
VMEM_LIMIT = 56 * 1024 * 1024

_NN = (((1,), (0,)), ((), ()))
_NT = (((1,), (1,)), ((), ()))
_TN = (((0,), (0,)), ((), ()))
MESH = pl.DeviceIdType.MESH


def _cparams(sem):
    return pltpu.CompilerParams(dimension_semantics=sem, vmem_limit_bytes=VMEM_LIMIT)


def _dot(a, b, dn):
    return lax.dot_general(a.astype(bf16), b.astype(bf16), dn, preferred_element_type=f32)


def _dot01(a01, b, dn):
    a = a01.astype(bf16)
    b0 = b.astype(bf16)
    r1 = b - b0.astype(f32)
    b1 = r1.astype(bf16)
    b2 = (r1 - b1.astype(f32)).astype(bf16)
    d = lambda q: lax.dot_general(a, q, dn, preferred_element_type=f32)
    return d(b0) + (d(b1) + d(b2))


def _sigmoid(x):
    return jax.nn.sigmoid(x)


def _silu(x):
    return x * _sigmoid(x)


def _dsilu(x):
    s = _sigmoid(x)
    return s * (1.0 + x * (1.0 - s))


def _softplus(x):
    return jnp.maximum(x, 0.0) + jnp.log1p(jnp.exp(-jnp.abs(x)))


_GELU_C = math.sqrt(2.0 / math.pi)


def _gelu(y):
    return 0.5 * y * (1.0 + jnp.tanh(_GELU_C * (y + 0.044715 * y * y * y)))


def _dgelu(y):
    t = jnp.tanh(_GELU_C * (y + 0.044715 * y * y * y))
    return 0.5 * (1.0 + t) + 0.5 * y * (1.0 - t * t) * _GELU_C * (1.0 + 3.0 * 0.044715 * y * y)


def _matmul(a, b, form, *, outs, name, epilogue=None, extras=(), tm=4096, tn=512, tk=1024, shard_cols=0, a_map=None):
    if form == "tn":
        K, M = a.shape
    else:
        M, K = a.shape
    per_step = 1
    if b.ndim == 3:
        assert form in ("nn", "nt"), name
        N = b.shape[1] if form == "nt" else N_DEV * b.shape[2]
        if form == "nn":
            tn = b.shape[2]
        else:
            per_step = max(1, tk // b.shape[2])
            tk = per_step * b.shape[2]
    else:
        N = b.shape[0] if form == "nt" else b.shape[1]
    ns = N // N_DEV
    if shard_cols:
        tn = ns * shard_cols
    tm, tn, tk = min(tm, M), min(tn, N), min(tk, K)
    assert M % tm == 0 and N % tn == 0 and K % tk == 0, (name, M, N, K, tm, tn, tk)
    nk = K // tk
    dn = {"nn": _NN, "nt": _NT, "tn": _TN}[form]
    if form == "tn":
        a_spec = pl.BlockSpec((tk, tm), lambda i, j, k: (k, i))
    else:
        a_spec = pl.BlockSpec((tm, tk), lambda i, j, k: (i, k))
    if b.ndim == 3:
        b_spec = (pl.BlockSpec((per_step, tn, tk // per_step), lambda i, j, k: (k, j, 0)) if form == "nt"
                  else pl.BlockSpec((None, tk, tn), lambda i, j, k: (j, k, 0)))
    elif form == "nt":
        b_spec = pl.BlockSpec((tn, tk), lambda i, j, k: (j, k))
    else:
        b_spec = pl.BlockSpec((tk, tn), lambda i, j, k: (k, j))
    e_spec = pl.BlockSpec((tm, tn), lambda i, j, k: (i, j))
    v_spec = pl.BlockSpec((1, tn), lambda i, j, k: (0, j))
    if shard_cols:
        o_spec = pl.BlockSpec((shard_cols, tm, ns), lambda i, j, k: (j, i, 0))
        o_shape = (N_DEV, M, ns)
    else:
        o_spec = e_spec
        o_shape = (M, N)
    n_ex = len(extras)
    sums = [isinstance(o, tuple) for o in outs]
    assert not any(sums) or tn == N, name

    def finish(acc, ex, o_refs, row_tile):
        vals = (acc,) if epilogue is None else epilogue(acc, *[e[...] for e in ex])
        for r, v, is_sum in zip(o_refs, vals, sums):
            if is_sum:
                @pl.when(row_tile == 0)
                def _(r=r, v=v):
                    r[...] = v.astype(r.dtype)

                @pl.when(row_tile > 0)
                def _(r=r, v=v):
                    r[...] += v.astype(r.dtype)
            elif shard_cols:
                for s in range(shard_cols):
                    r[s] = v[:, s * ns:(s + 1) * ns].astype(r.dtype)
            else:
                r[...] = v.astype(r.dtype)

    def prod(a_ref, b_ref):
        if b.ndim == 3 and form == "nt":
            w = tk // per_step
            return sum(_dot(a_ref[:, s * w:(s + 1) * w], b_ref[s], dn) for s in range(1, per_step)) + _dot(a_ref[:, 0:w], b_ref[0], dn)
        av = a_ref[...]
        return _dot(av if a_map is None else a_map(av), b_ref[...], dn)

    def body_one(*refs):
        finish(prod(*refs[:2]), refs[2:2 + n_ex], refs[2 + n_ex:], pl.program_id(0))

    def body_acc(*refs):
        a_ref, b_ref = refs[:2]
        acc = refs[-1]
        k = pl.program_id(2)
        row_tile = pl.program_id(0)

        @pl.when(k == 0)
        def _():
            acc[...] = prod(a_ref, b_ref)

        @pl.when((k > 0) & (k < nk - 1))
        def _():
            acc[...] += prod(a_ref, b_ref)

        @pl.when(k == nk - 1)
        def _():
            finish(acc[...] + prod(a_ref, b_ref), refs[2:2 + n_ex], refs[2 + n_ex:-1], row_tile)

    return pl.pallas_call(
        body_one if nk == 1 else body_acc, grid=(M // tm, N // tn, nk),
        in_specs=[a_spec, b_spec] + [v_spec if e.shape[0] == 1 else e_spec for e in extras],
        out_specs=[v_spec if s else o_spec for s in sums],
        out_shape=[jax.ShapeDtypeStruct((1, N), o[1]) if s else jax.ShapeDtypeStruct(o_shape, o) for o, s in zip(outs, sums)],
        scratch_shapes=[] if nk == 1 else [pltpu.VMEM((tm, tn), f32)],
        compiler_params=_cparams(("arbitrary" if any(sums) else "parallel", "parallel", "arbitrary")), name=name,
    )(a, b, *extras)


def _rms_fwd(x, w, name):
    T, D = x.shape
    tt = min(512, T)

    def body(x_ref, w_ref, h_ref):
        xv = x_ref[...]
        r = lax.rsqrt(jnp.mean(xv * xv, axis=1, keepdims=True) + EPS)
        h_ref[...] = (xv * r * w_ref[...]).astype(bf16)

    return pl.pallas_call(
        body, grid=(T // tt,),
        in_specs=[pl.BlockSpec((tt, D), lambda i: (i, 0)), pl.BlockSpec((1, D), lambda i: (0, 0))],
        out_specs=pl.BlockSpec((tt, D), lambda i: (i, 0)),
        out_shape=jax.ShapeDtypeStruct((T, D), bf16),
        compiler_params=_cparams(("parallel",)), name=name,
    )(x, w)


def _residual_rms_ep(acc, res, w):
    x = res + acc
    r = lax.rsqrt(jnp.mean(x * x, axis=1, keepdims=True) + EPS)
    return x, x * r * w


_RMS_BWD_OUTS = [f32, bf16, ("sum", f32)]


def _rms_bwd_ep(dh, x, dres, w):
    r = lax.rsqrt(jnp.mean(x * x, axis=1, keepdims=True) + EPS)
    xn = x * r
    dhw = dh * w
    dx = dres + r * (dhw - xn * jnp.mean(dhw * xn, axis=1, keepdims=True))
    return dx, dx, jnp.sum(dh * xn, axis=0, keepdims=True)


_LOSS_OUTS = [("sum", f32), f32, bf16, ("sum", f32)]


def _loss_ep(acc, res, w, tgt):
    x = res + acc
    D = x.shape[1]
    r = lax.rsqrt(jnp.mean(x * x, axis=1, keepdims=True) + EPS)
    xn = x * r
    e = xn * w - tgt
    loss = 0.5 * jnp.sum(jnp.mean(e * e, axis=1, keepdims=True), axis=0, keepdims=True)
    dy = e * (1.0 / D)
    dyw = dy * w
    dx = r * (dyw - xn * jnp.mean(dyw * xn, axis=1, keepdims=True))
    return jnp.broadcast_to(loss, (1, D)), dx, dx, jnp.sum(dy * xn, axis=0, keepdims=True)


def _ret_tables():
    H, C = RET_HEADS, RET_CHUNK
    lg = np.log1p(-np.exp2(-5.0 - np.arange(H, dtype=np.float32))).astype(np.float32)
    idx = np.arange(C, dtype=np.float32)
    diff = idx[:, None] - idx[None, :]
    causal = diff >= 0
    dm = np.where(causal[None], np.exp(lg[:, None, None] * np.where(causal, diff, 0.0)[None]), 0.0)
    qd = np.exp(lg[:, None] * (idx[None, :] + 1.0))
    kd = np.exp(lg[:, None] * (C - 1.0 - idx[None, :]))
    cg = np.exp(lg * C)
    tab = np.zeros((H, 4, C, HEAD_DIM), np.float32)
    tab[:, 0] = dm
    tab[:, 1] = qd[:, :, None]
    tab[:, 2] = kd[:, :, None]
    tab[:, 3] = cg[:, None, None]
    return jnp.asarray(tab)


def _rope_tables(T):
    half = HEAD_DIM // 2
    inv = ROPE_THETA ** (-jnp.arange(half, dtype=f32) / half)
    ang = jnp.arange(T, dtype=jnp.int32).astype(f32)[:, None] * inv[None, :]
    c, s = jnp.cos(ang), jnp.sin(ang)
    return jnp.concatenate([c, c], axis=1), jnp.concatenate([-s, s], axis=1)


def _rope(x, cos, sin):
    return x * cos + pltpu.roll(x, HEAD_DIM // 2, 1) * sin


def _unrope(y, cos, sin):
    return y * cos + pltpu.roll(y * sin, HEAD_DIM // 2, 1)


def _stack_heads(ref, H, f=None):
    parts = [ref[:, h * HEAD_DIM:(h + 1) * HEAD_DIM] for h in range(H)]
    return jnp.stack(parts if f is None else [f(a) for a in parts])


def _ret_fwd(p, cos, sin, tab, name):
    T = p.shape[0]
    C, H = RET_CHUNK, RET_HEADS
    N = T // C
    K = min(RET_STEP, N)
    NS = N // K
    scale = HEAD_DIM ** -0.5

    def body(q_ref, k_ref, v_ref, g_ref, c_ref, s_ref, t_ref, y_ref, o_ref, sp_ref, st):
        @pl.when(pl.program_id(0) == 0)
        def _():
            st[...] = jnp.zeros_like(st)

        dm, qd, kd, cg = t_ref[:, 0], t_ref[:, 1], t_ref[:, 2], t_ref[:, 3]
        S = st[...]
        for c in range(K):
            rows = pl.ds(c * C, C)
            cos_, sin_ = c_ref[rows, :], s_ref[rows, :]
            rot = lambda a: _rope(a, cos_, sin_)
            q = _stack_heads(q_ref.at[rows, :], H, rot)
            k = _stack_heads(k_ref.at[rows, :], H, rot) * scale
            v = _stack_heads(v_ref.at[rows, :], H)
            P = _dot(q, k, _NT3) * dm
            o = _dot(P, v, _NN3) + _dot(q * qd, S, _NN3)
            sp_ref[c] = S
            S = cg * S + _dot(k * kd, v, _TN3)
            r = lax.rsqrt(jnp.mean(o * o, axis=2, keepdims=True) + EPS)
            y = o * r * _silu(_stack_heads(g_ref.at[rows, :], H))
            for h in range(H):
                o_ref[rows, h * HEAD_DIM:(h + 1) * HEAD_DIM] = o[h]
                y_ref[rows, h * HEAD_DIM:(h + 1) * HEAD_DIM] = y[h].astype(bf16)
        st[...] = S

    wide = lambda blk: pl.BlockSpec((K * C, H * HEAD_DIM), lambda n: (n, blk))
    tbl = pl.BlockSpec((K * C, HEAD_DIM), lambda n: (n, 0))
    return pl.pallas_call(
        body, grid=(NS,),
        in_specs=[wide(0), wide(1), wide(2), wide(3), tbl, tbl,
                  pl.BlockSpec((H, 4, C, HEAD_DIM), lambda n: (0, 0, 0, 0))],
        out_specs=[wide(0), wide(0), pl.BlockSpec((K, H, HEAD_DIM, HEAD_DIM), lambda n: (n, 0, 0, 0))],
        out_shape=[jax.ShapeDtypeStruct((T, D_MODEL), bf16), jax.ShapeDtypeStruct((T, H * HEAD_DIM), f32),
                   jax.ShapeDtypeStruct((N, H, HEAD_DIM, HEAD_DIM), f32)],
        scratch_shapes=[pltpu.VMEM((H, HEAD_DIM, HEAD_DIM), f32)],
        compiler_params=_cparams(("arbitrary",)), name=name,
    )(p, p, p, p, cos, sin, tab)


def _ret_bwd(p, cos, sin, tab, o_raw, sprev, dmix, name):
    T = p.shape[0]
    C, H = RET_CHUNK, RET_HEADS
    N = T // C
    K = min(RET_STEP, N)
    NS = N // K
    scale = HEAD_DIM ** -0.5
    W = H * HEAD_DIM

    def body(q_ref, k_ref, v_ref, g_ref, c_ref, s_ref, t_ref, o_ref, sp_ref, dy_ref, d_ref, dst):
        @pl.when(pl.program_id(0) == 0)
        def _():
            dst[...] = jnp.zeros_like(dst)

        dm, qd, kd, cg = t_ref[:, 0], t_ref[:, 1], t_ref[:, 2], t_ref[:, 3]
        dS1 = dst[...]
        for c in reversed(range(K)):
            rows = pl.ds(c * C, C)
            cos_, sin_ = c_ref[rows, :], s_ref[rows, :]
            rot = lambda a: _rope(a, cos_, sin_)
            q = _stack_heads(q_ref.at[rows, :], H, rot)
            k = _stack_heads(k_ref.at[rows, :], H, rot) * scale
            v = _stack_heads(v_ref.at[rows, :], H)
            g = _stack_heads(g_ref.at[rows, :], H)
            S = sp_ref[c]
            o = _stack_heads(o_ref.at[rows, :], H)
            dy = _stack_heads(dy_ref.at[rows, :], H)
            r = lax.rsqrt(jnp.mean(o * o, axis=2, keepdims=True) + EPS)
            nrm = o * r
            dn = dy * _silu(g)
            dg = dy * nrm * _dsilu(g)
            do = r * (dn - nrm * jnp.mean(dn * nrm, axis=2, keepdims=True))
            P = _dot(q, k, _NT3) * dm
            dP = _dot(do, v, _NT3) * dm
            dq = _dot(dP, k, _NN3) + _dot(do, S, _NT3) * qd
            dk = (_dot(dP, q, _TN3) + _dot(v, dS1, _NT3) * kd) * scale
            dv = _dot(P, do, _TN3) + _dot(k * kd, dS1, _NN3)
            dS1 = cg * dS1 + _dot(q * qd, do, _TN3)
            for h in range(H):
                d_ref[rows, h * HEAD_DIM:(h + 1) * HEAD_DIM] = _unrope(dq[h], cos_, sin_).astype(bf16)
                d_ref[rows, W + h * HEAD_DIM:W + (h + 1) * HEAD_DIM] = _unrope(dk[h], cos_, sin_).astype(bf16)
                d_ref[rows, 2 * W + h * HEAD_DIM:2 * W + (h + 1) * HEAD_DIM] = dv[h].astype(bf16)
                d_ref[rows, 3 * W + h * HEAD_DIM:3 * W + (h + 1) * HEAD_DIM] = dg[h].astype(bf16)
        dst[...] = dS1

    rev = lambda blk: pl.BlockSpec((K * C, W), lambda n: (NS - 1 - n, blk))
    tbl = pl.BlockSpec((K * C, HEAD_DIM), lambda n: (NS - 1 - n, 0))
    return pl.pallas_call(
        body, grid=(NS,),
        in_specs=[rev(0), rev(1), rev(2), rev(3), tbl, tbl,
                  pl.BlockSpec((H, 4, C, HEAD_DIM), lambda n: (0, 0, 0, 0)), rev(0),
                  pl.BlockSpec((K, H, HEAD_DIM, HEAD_DIM), lambda n: (NS - 1 - n, 0, 0, 0)), rev(0)],
        out_specs=pl.BlockSpec((K * C, 4 * W), lambda n: (NS - 1 - n, 0)),
        out_shape=jax.ShapeDtypeStruct((T, 6 * W), bf16),
        scratch_shapes=[pltpu.VMEM((H, HEAD_DIM, HEAD_DIM), f32)],
        compiler_params=_cparams(("arbitrary",)), name=name,
    )(p, p, p, p, cos, sin, tab, o_raw, sprev, dmix)


CONV_K = 4
CONV_W = 512
PAD = 8
SUB_R = 64


def _conv_fwd(x, col_off, w, b, act, name):
    T = x.shape[0]
    C = w.shape[1]
    G = C // CONV_W
    tt = min(512, T)
    NT = T // tt
    has_b = b is not None

    def body(*refs):
        if has_b:
            x_ref, w_ref, b_ref, y_ref, pad = refs
        else:
            x_ref, w_ref, y_ref, pad = refs
        t = pl.program_id(1)

        @pl.when(t == 0)
        def _():
            pad[pl.ds(0, PAD), :] = jnp.zeros((PAD, CONV_W), f32)

        pad[pl.ds(PAD, tt), :] = x_ref[...]
        for g in range(CONV_W // 128):
            ls = slice(g * 128, (g + 1) * 128)
            wv = w_ref[:, ls]
            for c in range(tt // SUB_R):
                r0 = c * SUB_R
                y = wv[0:1, :] * pad[pl.ds(PAD - 3 + r0, SUB_R), ls]
                for kk in range(1, CONV_K):
                    y = y + wv[kk:kk + 1, :] * pad[pl.ds(PAD - 3 + kk + r0, SUB_R), ls]
                if has_b:
                    y = y + b_ref[:, ls]
                y_ref[pl.ds(r0, SUB_R), ls] = _silu(y) if act else y
        tail = pad[pl.ds(tt, PAD), :]
        pad[pl.ds(0, PAD), :] = tail

    in_specs = [pl.BlockSpec((tt, CONV_W), lambda g, t: (t, col_off + g)),
                pl.BlockSpec((CONV_K, CONV_W), lambda g, t: (0, g))]
    args = [x, w]
    if has_b:
        in_specs.append(pl.BlockSpec((1, CONV_W), lambda g, t: (0, g)))
        args.append(b)
    return pl.pallas_call(
        body, grid=(G, NT), in_specs=in_specs,
        out_specs=pl.BlockSpec((tt, CONV_W), lambda g, t: (t, g)),
        out_shape=jax.ShapeDtypeStruct((T, C), f32),
        scratch_shapes=[pltpu.VMEM((tt + PAD, CONV_W), f32)],
        compiler_params=_cparams(("parallel", "arbitrary")), name=name,
    )(*args)


def _conv_bwd(x, col_off, w, b, act, dout, dp, name):
    T = x.shape[0]
    C = w.shape[1]
    G = C // CONV_W
    tt = min(512, T)
    NT = T // tt
    has_b = b is not None

    def body(*refs):
        if has_b:
            x_ref, xp_ref, w_ref, b_ref, d_ref, dp_in, dx_ref, dw_ref, db_ref, pad, dpad = refs
        else:
            x_ref, xp_ref, w_ref, d_ref, dp_in, dx_ref, dw_ref, db_ref, pad, dpad = refs
        t = pl.program_id(1)
        first_tile = t == NT - 1

        @pl.when(t == 0)
        def _():
            dpad[pl.ds(tt, PAD), :] = jnp.zeros((PAD, CONV_W), f32)
            dw_ref[...] = jnp.zeros_like(dw_ref)
            db_ref[...] = jnp.zeros_like(db_ref)

        pad[pl.ds(0, PAD), :] = jnp.where(first_tile, 0.0, xp_ref[...])
        pad[pl.ds(PAD, tt), :] = x_ref[...]
        fold = lambda v: v.reshape(SUB_R // 8, 8, 128).sum(axis=0)
        for g in range(CONV_W // 128):
            ls = slice(g * 128, (g + 1) * 128)
            wv = w_ref[:, ls]
            acc = [jnp.zeros((8, 128), f32) for _ in range(CONV_K + 1)]
            for c in reversed(range(tt // SUB_R)):
                r0 = c * SUB_R
                xs = [pad[pl.ds(PAD - 3 + kk + r0, SUB_R), ls] for kk in range(CONV_K)]
                dy = d_ref[pl.ds(r0, SUB_R), ls]
                if act:
                    y = wv[0:1, :] * xs[0]
                    for kk in range(1, CONV_K):
                        y = y + wv[kk:kk + 1, :] * xs[kk]
                    if has_b:
                        y = y + b_ref[:, ls]
                    dy = dy * _dsilu(y)
                dpad[pl.ds(r0, SUB_R), ls] = dy
                dx = wv[3:4, :] * dy
                for j in range(1, CONV_K):
                    dx = dx + wv[3 - j:4 - j, :] * dpad[pl.ds(r0 + j, SUB_R), ls]
                dx_ref[pl.ds(r0, SUB_R), ls] = dx.astype(bf16)
                for kk in range(CONV_K):
                    acc[kk] = acc[kk] + fold(dy * xs[kk])
                acc[CONV_K] = acc[CONV_K] + fold(dy)
            for kk in range(CONV_K):
                dw_ref[kk:kk + 1, ls] += jnp.sum(acc[kk], axis=0, keepdims=True)
            db_ref[:, ls] += jnp.sum(acc[CONV_K], axis=0, keepdims=True)
        head = dpad[pl.ds(0, PAD), :]
        dpad[pl.ds(tt, PAD), :] = head

    rows8 = tt // PAD
    in_specs = [pl.BlockSpec((tt, CONV_W), lambda g, t: (NT - 1 - t, col_off + g)),
                pl.BlockSpec((PAD, CONV_W), lambda g, t: (jnp.maximum((NT - 1 - t) * rows8 - 1, 0), col_off + g)),
                pl.BlockSpec((CONV_K, CONV_W), lambda g, t: (0, g))]
    args = [x, x, w]
    if has_b:
        in_specs.append(pl.BlockSpec((1, CONV_W), lambda g, t: (0, g)))
        args.append(b)
    in_specs += [pl.BlockSpec((tt, CONV_W), lambda g, t: (NT - 1 - t, g)), pl.BlockSpec(memory_space=pl.ANY)]
    args += [dout, dp]
    return pl.pallas_call(
        body, grid=(G, NT), in_specs=in_specs,
        out_specs=[pl.BlockSpec((tt, CONV_W), lambda g, t: (NT - 1 - t, col_off + g)),
                   pl.BlockSpec((CONV_K, CONV_W), lambda g, t: (0, g)),
                   pl.BlockSpec((1, CONV_W), lambda g, t: (0, g))],
        out_shape=[jax.ShapeDtypeStruct(dp.shape, dp.dtype), jax.ShapeDtypeStruct((CONV_K, C), f32),
                   jax.ShapeDtypeStruct((1, C), f32)],
        input_output_aliases={len(args) - 1: 0},
        scratch_shapes=[pltpu.VMEM((tt + PAD, CONV_W), f32), pltpu.VMEM((tt + PAD, CONV_W), f32)],
        compiler_params=_cparams(("parallel", "arbitrary")), name=name,
    )(*args)


def _lru_gates(xc, wr, wi, br, bi, lam):
    r = _sigmoid(_dot(xc, wr, _NN) + br)
    i = _sigmoid(_dot(xc, wi, _NN) + bi)
    sp = _softplus(-lam)
    a = jnp.exp(-LRU_C * r * sp)
    mult = jnp.sqrt(1.0 - a * a)
    return r, i, sp, a, mult


def _lru_fwd(xc, p, y_off, wr, wi, br, bi, lam, mix, name):
    T = xc.shape[0]
    G = LRU_WIDTH // 128
    tt = min(512, T)
    NT = T // tt

    def body(x_ref, y_ref, wr_ref, wi_ref, br_ref, bi_ref, l_ref, mix_in, o_ref, h_ref, hc):
        t = pl.program_id(1)

        @pl.when(t == 0)
        def _():
            hc[...] = jnp.zeros_like(hc)

        x = x_ref[...]
        r, i, sp, a, mult = _lru_gates(x, wr_ref[...], wi_ref[...], br_ref[...], bi_ref[...], l_ref[...])
        row = lax.broadcasted_iota(jnp.int32, (tt, 128), 0)
        mult = jnp.where((row == 0) & (t == 0), 1.0, mult)
        U = x * i * mult
        A = a
        d = 1
        while d < tt:
            keep = row >= d
            Ush = jnp.where(keep, pltpu.roll(U, d, 0), 0.0)
            Ash = jnp.where(keep, pltpu.roll(A, d, 0), 1.0)
            U = A * Ush + U
            A = A * Ash
            d *= 2
        h = U + A * hc[0:1, :]
        h_ref[...] = h
        hc[...] = jnp.broadcast_to(h[tt - 1:tt, :], hc.shape)
        o_ref[...] = (h * _gelu(y_ref[...])).astype(bf16)

    tile = pl.BlockSpec((tt, 128), lambda g, t: (t, g))
    vec = pl.BlockSpec((1, 128), lambda g, t: (0, g))
    wsp = pl.BlockSpec((128, 128), lambda g, t: (g, g))
    return pl.pallas_call(
        body, grid=(G, NT),
        in_specs=[tile, pl.BlockSpec((tt, 128), lambda g, t: (t, y_off + g)), wsp, wsp, vec, vec, vec,
                  pl.BlockSpec(memory_space=pl.ANY)],
        out_specs=[pl.BlockSpec((tt, 128), lambda g, t: (t, G + g)), tile],
        out_shape=[jax.ShapeDtypeStruct(mix.shape, mix.dtype), jax.ShapeDtypeStruct((T, LRU_WIDTH), f32)],
        input_output_aliases={7: 0},
        scratch_shapes=[pltpu.VMEM((8, 128), f32)],
        compiler_params=_cparams(("parallel", "arbitrary")), name=name,
    )(xc, p, wr, wi, br, bi, lam, mix)


def _lru_bwd(xc, p, y_off, wr, wi, br, bi, lam, hs, dmix, d_off, dp, name):
    T = xc.shape[0]
    G = LRU_WIDTH // 128
    tt = min(512, T)
    NT = T // tt

    def body(x_ref, y_ref, wr_ref, wi_ref, br_ref, bi_ref, l_ref, h_ref, hp_ref, do_ref, dp_in,
             dx_ref, dy_ref, dwr_ref, dwi_ref, dbr_ref, dbi_ref, dl_ref, lc, an):
        t = pl.program_id(1)
        first_tile = t == NT - 1

        @pl.when(t == 0)
        def _():
            lc[...] = jnp.zeros_like(lc)
            an[...] = jnp.zeros_like(an)
            dwr_ref[...] = jnp.zeros_like(dwr_ref)
            dwi_ref[...] = jnp.zeros_like(dwi_ref)
            dbr_ref[...] = jnp.zeros_like(dbr_ref)
            dbi_ref[...] = jnp.zeros_like(dbi_ref)
            dl_ref[...] = jnp.zeros_like(dl_ref)

        x = x_ref[...]
        y = y_ref[...]
        wr, wi, lam_ = wr_ref[...], wi_ref[...], l_ref[...]
        r, i, sp, a, mult_raw = _lru_gates(x, wr, wi, br_ref[...], bi_ref[...], lam_)
        row = lax.broadcasted_iota(jnp.int32, (tt, 128), 0)
        t0 = (row == 0) & first_tile
        mult = jnp.where(t0, 1.0, mult_raw)
        h = h_ref[...]
        do = do_ref[...]
        dh = do * _gelu(y)
        dy_ref[...] = (do * h * _dgelu(y)).astype(bf16)
        B = jnp.where(row == tt - 1, an[0:1, :], pltpu.roll(a, tt - 1, 0))
        L = dh
        d = 1
        while d < tt:
            keep = row < tt - d
            Lsh = jnp.where(keep, pltpu.roll(L, tt - d, 0), 0.0)
            Bsh = jnp.where(keep, pltpu.roll(B, tt - d, 0), 1.0)
            L = L + B * Lsh
            B = B * Bsh
            d *= 2
        L = L + B * lc[0:1, :]
        lc[...] = jnp.broadcast_to(L[0:1, :], lc.shape)
        an[...] = jnp.broadcast_to(a[0:1, :], an.shape)
        hprev = jnp.where(first_tile, 0.0, hp_ref[...])[PAD - 1:PAD, :]
        hm1 = jnp.where(row == 0, hprev, pltpu.roll(h, 1, 0))
        da = L * hm1
        dxc = L * i * mult
        di = L * x * mult
        dmult = jnp.where(t0, 0.0, L * x * i)
        da = da - jnp.where(t0, 0.0, dmult * a / mult_raw)
        dlog_a = da * a
        dr = dlog_a * (-LRU_C) * sp
        dsp = jnp.sum(dlog_a * (-LRU_C) * r, axis=0, keepdims=True)
        dpr = dr * r * (1.0 - r)
        dpi = di * i * (1.0 - i)
        dx_ref[...] = dxc + _dot(dpr, wr, _NT) + _dot(dpi, wi, _NT)
        for d_ref, dpre in ((dwr_ref, dpr), (dwi_ref, dpi)):
            dw = _dot(x, dpre, _TN)
            for s in range(2):
                d_ref[s] += dw[s * 64:(s + 1) * 64, s * 64:(s + 1) * 64]
        dbr_ref[...] += jnp.sum(dpr, axis=0, keepdims=True)
        dbi_ref[...] += jnp.sum(dpi, axis=0, keepdims=True)
        dl_ref[...] += dsp * (-_sigmoid(-lam_))

    rows8 = tt // PAD
    tile = pl.BlockSpec((tt, 128), lambda g, t: (NT - 1 - t, g))
    vec = pl.BlockSpec((1, 128), lambda g, t: (0, g))
    wsp = pl.BlockSpec((128, 128), lambda g, t: (g, g))
    wout = pl.BlockSpec((2, 64, 64), lambda g, t: (g, 0, 0))
    return pl.pallas_call(
        body, grid=(G, NT),
        in_specs=[tile, pl.BlockSpec((tt, 128), lambda g, t: (NT - 1 - t, y_off + g)), wsp, wsp, vec, vec, vec, tile,
                  pl.BlockSpec((PAD, 128), lambda g, t: (jnp.maximum((NT - 1 - t) * rows8 - 1, 0), g)),
                  pl.BlockSpec((tt, 128), lambda g, t: (NT - 1 - t, d_off + g)), pl.BlockSpec(memory_space=pl.ANY)],
        out_specs=[tile, pl.BlockSpec((tt, 128), lambda g, t: (NT - 1 - t, y_off + g)), wout, wout, vec, vec, vec],
        out_shape=[jax.ShapeDtypeStruct((T, LRU_WIDTH), f32), jax.ShapeDtypeStruct(dp.shape, dp.dtype),
                   jax.ShapeDtypeStruct((2 * G, 64, 64), f32), jax.ShapeDtypeStruct((2 * G, 64, 64), f32),
                   jax.ShapeDtypeStruct((1, LRU_WIDTH), f32), jax.ShapeDtypeStruct((1, LRU_WIDTH), f32),
                   jax.ShapeDtypeStruct((1, LRU_WIDTH), f32)],
        input_output_aliases={10: 1},
        scratch_shapes=[pltpu.VMEM((8, 128), f32), pltpu.VMEM((8, 128), f32)],
        compiler_params=_cparams(("parallel", "arbitrary")), name=name,
    )(xc, p, wr, wi, br, bi, lam, hs, hs, dmix, dp)


_NN3 = (((2,), (1,)), ((0,), (0,)))
_NT3 = (((2,), (2,)), ((0,), (0,)))
_TN3 = (((1,), (1,)), ((0,), (0,)))


def _pairs(ref, K):
    C = GDN_CHUNK
    return jnp.stack([ref[c * C:(c + 1) * C, h * HEAD_DIM:(h + 1) * HEAD_DIM] for c in range(K) for h in range(GDN_HEADS)])


def _put_pairs(ref, val, K, col=0):
    C, H = GDN_CHUNK, GDN_HEADS
    for c in range(K):
        for h in range(H):
            ref[c * C:(c + 1) * C, col + h * HEAD_DIM:col + (h + 1) * HEAD_DIM] = val[c * H + h].astype(ref.dtype)


def _rowsum(x):
    H, C, L = x.shape
    return _dot(x.reshape(H * C, L), jnp.ones((L, HEAD_DIM), f32), _NN).reshape(H, C, HEAD_DIM)


def _gdn_pre(qr, kr, v, ba, alog, dtb):
    C, H = GDN_CHUNK, GDN_HEADS
    B = qr.shape[0]
    K = B // H
    lane = lax.broadcasted_iota(jnp.int32, (C, 128), 1)
    lane3 = lax.broadcasted_iota(jnp.int32, (B, C, 128), 2)
    ri = lax.broadcasted_iota(jnp.int32, (C, C), 0)
    ci = lax.broadcasted_iota(jnp.int32, (C, C), 1)
    rowc = lax.broadcasted_iota(jnp.int32, (C, 1), 0)
    col = lambda m, j: jnp.sum(jnp.where(lane == j, m, 0.0), axis=1, keepdims=True)
    ea = jnp.exp(alog)
    tri = (ri >= ci).astype(f32)
    g_all, beta_cols, G_cols = [], [], []
    for c in range(K):
        ba_c = ba[c * C:(c + 1) * C]
        g_c = -ea * _softplus(ba_c + dtb)
        G_c = _dot01(tri, g_c, _NN)
        s_c = _sigmoid(ba_c)
        g_all.append(g_c)
        beta_cols += [col(s_c, h) for h in range(H)]
        G_cols += [col(G_c, H + h) for h in range(H)]
    wide = lambda c: jnp.broadcast_to(c, (B, C, 128))
    beta = wide(jnp.stack(beta_cols))
    Gc = jnp.stack(G_cols)
    rq = lax.rsqrt(_rowsum(qr * qr) + EPS)
    rk = lax.rsqrt(_rowsum(kr * kr) + EPS)
    qh, kn = qr * rq, kr * rk
    qn = qh * (HEAD_DIM ** -0.5)
    Grow = _dot01(jnp.ones((B, C, 128), f32), jnp.where(lane3 == 0, Gc, 0.0), _NT3)
    incl = ri >= ci
    Di = jnp.where(incl, jnp.exp(jnp.where(incl, Gc - Grow, 0.0)), 0.0)
    Ds = jnp.where(ri > ci, Di, 0.0)
    Gl = jnp.sum(jnp.where(rowc == C - 1, Gc, 0.0), axis=1, keepdims=True)
    eG = wide(jnp.exp(Gc))
    eGl = wide(jnp.exp(Gl - Gc))
    cd = jnp.exp(Gl)
    kb = kn * beta
    vb = v * beta
    Lm = _dot(kb, kn, _NT3) * Ds
    kbg = kb * eG
    QK = _dot(qn, kn, _NT3) * Di
    qg = qn * eG
    kg = kn * eGl
    return dict(beta=beta, g_all=g_all, rq=rq, rk=rk, qh=qh, kn=kn, qn=qn, Di=Di, Ds=Ds, eG=eG, eGl=eGl, cd=cd,
                kb=kb, vb=vb, Lm=Lm, kbg=kbg, QK=QK, qg=qg, kg=kg, lane=lane, ri=ri, ci=ci, rowc=rowc, ea=ea)


def _unit_lower_inverse(Lm):
    C = Lm.shape[-1]
    ri = lax.broadcasted_iota(jnp.int32, (C, C), 0)
    ci = lax.broadcasted_iota(jnp.int32, (C, C), 1)
    same = lambda s: (ri // s) == (ci // s)
    Xd = jnp.where(same(8), -Lm, 0.0)
    Tinv = (ri == ci).astype(f32) + Xd
    Pw = Xd
    for _ in range(2):
        Pw = _dot(Pw, Pw, _NN3)
        Tinv = Tinv + _dot(Tinv, Pw, _NN3)
    for s in (8, 16, 32):
        off = jnp.where(same(2 * s) & jnp.logical_not(same(s)), Lm, 0.0)
        Tinv = Tinv - _dot(_dot(Tinv, off, _NN3), Tinv, _NN3)
    return Tinv


def _gdn_specs(T, rev):
    C = GDN_CHUNK
    H = GDN_HEADS
    K = min(GDN_STEP, T // C)
    NS = T // (C * K)
    nn = (lambda n: NS - 1 - n) if rev else (lambda n: n)
    wide = lambda blk: pl.BlockSpec((K * C, H * HEAD_DIM), lambda n: (nn(n), blk))
    one = lambda off: pl.BlockSpec((K * C, HEAD_DIM), lambda n: (nn(n), off))
    vec = pl.BlockSpec((1, 128), lambda n: (0, 0))
    st = lambda rows: pl.BlockSpec((K, H, rows, rows), lambda n: (nn(n), 0, 0, 0))
    return K, NS, wide, one, vec, st


def _gdn_fwd(qkv, p, alog, dtb, nw, name):
    T = qkv.shape[0]
    C, H = GDN_CHUNK, GDN_HEADS
    N = T // C
    K, NS, wide, one, vec, st_spec = _gdn_specs(T, False)

    def body(q_ref, k_ref, v_ref, z_ref, ba_ref, al_ref, dt_ref, nw_ref, y_ref, sp_ref, ti_ref, vn_ref, o_ref, st):
        @pl.when(pl.program_id(0) == 0)
        def _():
            st[...] = jnp.zeros_like(st)

        f = _gdn_pre(_pairs(q_ref, K), _pairs(k_ref, K), _pairs(v_ref, K), ba_ref[...], al_ref[...], dt_ref[...])
        Tinv = _unit_lower_inverse(f["Lm"])
        ti_ref[...] = Tinv.reshape(K, H, C, C).astype(bf16)
        w = _dot(Tinv, f["kbg"], _NN3)
        u = _dot(Tinv, f["vb"], _NN3)
        S = st[...]
        vns, os_ = [], []
        for c in range(K):
            sl = slice(c * H, (c + 1) * H)
            sp_ref[c] = S
            vn_c = u[sl] - _dot(w[sl], S, _NN3)
            os_.append(_dot(f["qg"][sl], S, _NN3) + _dot(f["QK"][sl], vn_c, _NN3))
            S = S * f["cd"][sl] + _dot(f["kg"][sl], vn_c, _TN3)
            vns.append(vn_c)
        st[...] = S
        vn, o = jnp.concatenate(vns), jnp.concatenate(os_)
        r = lax.rsqrt(_rowsum(o * o) * (1.0 / HEAD_DIM) + EPS)
        _put_pairs(y_ref, o * r * nw_ref[...] * _silu(_pairs(z_ref, K)), K)
        _put_pairs(vn_ref, vn, K)
        _put_pairs(o_ref, o, K)

    wide_f32 = jax.ShapeDtypeStruct((T, H * HEAD_DIM), f32)
    return pl.pallas_call(
        body, grid=(NS,),
        in_specs=[wide(0), wide(1), wide(2), wide(3), one(4 * H), vec, vec, vec],
        out_specs=[wide(0), st_spec(HEAD_DIM), st_spec(C), wide(0), wide(0)],
        out_shape=[jax.ShapeDtypeStruct((T, H * HEAD_DIM), bf16), jax.ShapeDtypeStruct((N, H, HEAD_DIM, HEAD_DIM), f32),
                   jax.ShapeDtypeStruct((N, H, C, C), bf16), jax.ShapeDtypeStruct((T, H * HEAD_DIM), bf16), wide_f32],
        scratch_shapes=[pltpu.VMEM((H, HEAD_DIM, HEAD_DIM), f32)],
        compiler_params=_cparams(("arbitrary",)), name=name,
    )(qkv, qkv, qkv, p, p, alog, dtb, nw)


def _gdn_bwd(qkv, p, alog, dtb, nw, sprev, tinv, vn_all, o_all, dy_all, name):
    T = qkv.shape[0]
    C, H = GDN_CHUNK, GDN_HEADS
    N = T // C
    K, NS, wide, one, vec, st_spec = _gdn_specs(T, True)
    rs = lambda m: jnp.sum(m, axis=2, keepdims=True)

    def body(q_ref, k_ref, v_ref, z_ref, ba_ref, al_ref, dt_ref, nw_ref, sp_ref, ti_ref, vn_ref, o_ref, dy_ref,
             dqkv_ref, dz_ref, dba_ref, dal_ref, ddt_ref, dnw_ref, dst):
        @pl.when(pl.program_id(0) == 0)
        def _():
            dst[...] = jnp.zeros_like(dst)
            dal_ref[...] = jnp.zeros_like(dal_ref)
            ddt_ref[...] = jnp.zeros_like(ddt_ref)
            dnw_ref[...] = jnp.zeros_like(dnw_ref)

        ba, dtb_, nwv = ba_ref[...], dt_ref[...], nw_ref[...]
        v = _pairs(v_ref, K)
        f = _gdn_pre(_pairs(q_ref, K), _pairs(k_ref, K), v, ba, al_ref[...], dtb_)
        beta, kn, qn, kb, vb, kbg = f["beta"], f["kn"], f["qn"], f["kb"], f["vb"], f["kbg"]
        eG, eGl, cd, Di, Ds, QK, qg, kg = f["eG"], f["eGl"], f["cd"], f["Di"], f["Ds"], f["QK"], f["qg"], f["kg"]
        lane, ri, ci, rowc = f["lane"], f["ri"], f["ci"], f["rowc"]
        Tinv = ti_ref[...].reshape(K * H, C, C)
        S = sp_ref[...].reshape(K * H, HEAD_DIM, HEAD_DIM)
        w_ = _dot(Tinv, kbg, _NN3)
        vn, o = _pairs(vn_ref, K), _pairs(o_ref, K)
        z, dy = _pairs(z_ref, K), _pairs(dy_ref, K)
        r = lax.rsqrt(_rowsum(o * o) * (1.0 / HEAD_DIM) + EPS)
        nrm = o * r
        sz = _silu(z)
        dn = dy * nwv * sz
        _put_pairs(dz_ref, dy * nrm * nwv * _dsilu(z), K)
        dnw_ref[...] += jnp.sum(jnp.sum(dy * nrm * sz, axis=0), axis=0, keepdims=True)
        do = r * (dn - nrm * (_rowsum(dn * nrm) * (1.0 / HEAD_DIM)))
        dvn_do = _dot(QK, do, _TN3)
        dS_do = _dot(qg, do, _TN3)
        dqg = _dot(do, S, _NT3)
        dQK = _dot(do, vn, _NT3)
        dS = dst[...]
        dS1s, dvns = [None] * K, [None] * K
        for c in reversed(range(K)):
            sl = slice(c * H, (c + 1) * H)
            dS1s[c] = dS
            dvns[c] = _dot(kg[sl], dS, _NN3) + dvn_do[sl]
            dS = cd[sl] * dS + dS_do[sl] - _dot(w_[sl], dvns[c], _TN3)
        dst[...] = dS
        dS1, dvn = jnp.concatenate(dS1s), jnp.concatenate(dvns)
        dcd = jnp.sum(jnp.sum(S * dS1, axis=2, keepdims=True), axis=1, keepdims=True)
        dkg = _dot(vn, dS1, _NT3)
        dw = -_dot(dvn, S, _NT3)
        dqn = dqg * eG
        dkn = dkg * eGl
        deGl = rs(dkg * kn)
        dQKr = dQK * Di
        E = dQK * QK
        dqn = dqn + _dot(dQKr, kn, _NN3)
        dkn = dkn + _dot(dQKr, qn, _TN3)
        dT = _dot(dvn, vb, _NT3) + _dot(dw, kbg, _NT3)
        dvb = _dot(Tinv, dvn, _TN3)
        dkbg = _dot(Tinv, dw, _TN3)
        dkb = dkbg * eG
        deG = rs(dqg * qn + dkbg * kb)
        dL = -_dot(_dot(Tinv, dT, _TN3), Tinv, _NT3)
        dKK = dL * Ds
        E = E + dL * f["Lm"]
        dkb = dkb + _dot(dKK, kn, _NN3)
        dkn = dkn + _dot(dKK, kb, _TN3) + dkb * beta
        dbeta = rs(dkb * kn + dvb * v)
        _put_pairs(dqkv_ref, dvb * beta, K, 2 * H * HEAD_DIM)
        dG = rs(E) - rs(jnp.swapaxes(E, 1, 2)) + deG * eG - deGl * eGl
        dGl = jnp.sum(deGl * eGl, axis=1, keepdims=True) + dcd * cd
        dG = dG + jnp.where(rowc == C - 1, dGl, 0.0)
        qh = f["qh"]
        _put_pairs(dqkv_ref, (HEAD_DIM ** -0.5) * f["rq"] * (dqn - qh * _rowsum(dqn * qh)), K)
        _put_pairs(dqkv_ref, f["rk"] * (dkn - kn * _rowsum(dkn * kn)), K, H * HEAD_DIM)
        db = dbeta * beta * (1.0 - beta)
        triu = (ri <= ci).astype(f32)
        for c in range(K):
            db_all = jnp.where(lane == 0, db[c * H], 0.0)
            dG_all = jnp.where(lane == H, dG[c * H], 0.0)
            for h in range(1, H):
                db_all = db_all + jnp.where(lane == h, db[c * H + h], 0.0)
                dG_all = dG_all + jnp.where(lane == H + h, dG[c * H + h], 0.0)
            dg_all = _dot01(triu, dG_all, _NN)
            da_all = dg_all * (-f["ea"]) * _sigmoid(ba[c * C:(c + 1) * C] + dtb_)
            dba_ref[c * C:(c + 1) * C, :] = (db_all + da_all).astype(bf16)
            ddt_ref[...] += jnp.sum(da_all, axis=0, keepdims=True)
            dal_ref[...] += jnp.sum(dg_all * f["g_all"][c], axis=0, keepdims=True)

    small = jax.ShapeDtypeStruct((1, 128), f32)
    return pl.pallas_call(
        body, grid=(NS,),
        in_specs=[wide(0), wide(1), wide(2), wide(3), one(4 * H), vec, vec, vec, st_spec(HEAD_DIM), st_spec(C),
                  wide(0), wide(0), wide(0)],
        out_specs=[pl.BlockSpec((K * C, 3 * H * HEAD_DIM), lambda n: (NS - 1 - n, 0)), wide(3), one(0), vec, vec, vec],
        out_shape=[jax.ShapeDtypeStruct((T, 3 * H * HEAD_DIM), f32), jax.ShapeDtypeStruct((T, ODD_PAD), bf16),
                   jax.ShapeDtypeStruct((T, 128), bf16), small, small, small],
        scratch_shapes=[pltpu.VMEM((H, HEAD_DIM, HEAD_DIM), f32)],
        compiler_params=_cparams(("arbitrary",)), name=name,
    )(qkv, qkv, qkv, p, p, alog, dtb, nw, sprev, tinv, vn_all, o_all, dy_all)


def _lanes_from(x, s):
    return x if s % 128 == 0 else pltpu.roll(x, (128 - s) % 128, 1)


def _odd_assemble(g, name):
    R = g.shape[1]
    tr = min(256, R)
    n_blk = ODD_SHARD_PAD // 128

    def body(g_ref, o_ref):
        lane = lax.broadcasted_iota(jnp.int32, (tr, 128), 1)
        blk = lambda d, m: g_ref[d, :, m * 128:(m + 1) * 128]
        for gb in range(ODD_PAD // 128):
            c0 = 128 * gb
            if c0 >= ODD_IN:
                o_ref[:, c0:c0 + 128] = jnp.zeros((tr, 128), g.dtype)
                continue
            d0 = c0 // ODD_SHARD
            m0, sh = divmod(c0 - ODD_SHARD * d0, 128)
            take = min(128, ODD_SHARD * (d0 + 1) - c0)
            p = _lanes_from(blk(d0, m0), sh)
            if sh and m0 + 1 < n_blk:
                p = jnp.where(lane < 128 - sh, p, _lanes_from(blk(d0, m0 + 1), sh))
            if take < 128:
                nxt = pltpu.roll(blk(d0 + 1, 0), take, 1) if d0 + 1 < N_DEV else jnp.zeros((tr, 128), g.dtype)
                p = jnp.where(lane < take, p, nxt)
            o_ref[:, c0:c0 + 128] = p

    return pl.pallas_call(
        body, grid=(R // tr,),
        in_specs=[pl.BlockSpec((N_DEV, tr, ODD_SHARD_PAD), lambda i: (0, i, 0))],
        out_specs=pl.BlockSpec((tr, ODD_PAD), lambda i: (i, 0)),
        out_shape=jax.ShapeDtypeStruct((R, ODD_PAD), g.dtype),
        compiler_params=_cparams(("parallel",)), name=name,
    )(g)


def _odd_split(w, name):
    R = w.shape[0]
    tr = min(256, R)

    def body(w_ref, o_ref):
        lane = lax.broadcasted_iota(jnp.int32, (tr, 128), 1)
        blk = lambda gb: w_ref[:, gb * 128:(gb + 1) * 128]
        for d in range(N_DEV):
            for m in range(ODD_SHARD_PAD // 128):
                g0, sh = divmod(ODD_SHARD * d + 128 * m, 128)
                p = _lanes_from(blk(g0), sh)
                if sh and g0 + 1 < ODD_PAD // 128:
                    p = jnp.where(lane < 128 - sh, p, _lanes_from(blk(g0 + 1), sh))
                real = ODD_SHARD - 128 * m
                if real < 128:
                    p = jnp.where(lane < real, p, jnp.zeros_like(p))
                o_ref[d, :, m * 128:(m + 1) * 128] = p

    return pl.pallas_call(
        body, grid=(R // tr,),
        in_specs=[pl.BlockSpec((tr, ODD_PAD), lambda i: (i, 0))],
        out_specs=pl.BlockSpec((N_DEV, tr, ODD_SHARD_PAD), lambda i: (0, i, 0)),
        out_shape=jax.ShapeDtypeStruct((N_DEV, R, ODD_SHARD_PAD), w.dtype),
        compiler_params=_cparams(("parallel",)), name=name,
    )(w)


def _adam_tile(g, w_ref, m_ref, v_ref, go_ref, d_ref, mo_ref, vo_ref):
    c1 = 1.0 - ADAM_B1 ** ADAM_STEP
    c2 = 1.0 - ADAM_B2 ** ADAM_STEP
    mn = ADAM_B1 * m_ref[...] + (1.0 - ADAM_B1) * g
    vn = ADAM_B2 * v_ref[...] + (1.0 - ADAM_B2) * (g * g)
    go_ref[...] = g
    mo_ref[...] = mn
    vo_ref[...] = vn
    d_ref[...] = -ADAM_LR * ((mn / c1) / (jnp.sqrt(vn / c2) + ADAM_EPS) + ADAM_WD * w_ref[...])


def _adamw(w, gs, m, v, name, layer=None, prev=None):
    R, Cc = w.shape[-2:]
    S = gs.shape[0]
    tr = R
    if S * R * Cc * 4 > (4 << 20):
        for cand in (256, 128, 64, 32, 16, 8):
            if R % cand == 0 and R > cand:
                tr = cand
                break

    def body(w_ref, g_ref, m_ref, v_ref, *rest):
        g = g_ref[0].astype(f32)
        for s in range(1, S):
            g = g + g_ref[s].astype(f32)
        _adam_tile(g, w_ref, m_ref, v_ref, *rest[-4:])

    if layer is None:
        blk = pl.BlockSpec((tr, Cc), lambda i: (i, 0))
    else:
        blk = pl.BlockSpec((None, tr, Cc), lambda i: (layer, i, 0))
    out = jax.ShapeDtypeStruct(w.shape, f32)
    carried = [] if prev is None else list(prev)
    return pl.pallas_call(
        body, grid=(R // tr,),
        in_specs=[blk, pl.BlockSpec((S, tr, Cc), lambda i: (0, i, 0)), blk, blk]
        + [pl.BlockSpec(memory_space=pl.ANY)] * len(carried),
        out_specs=[blk] * 4, out_shape=[out] * 4,
        input_output_aliases={4 + j: j for j in range(len(carried))},
        compiler_params=_cparams(("parallel",)), name=name,
    )(w, gs, m, v, *carried)


def _adamw_column_major(w, gs, m, v, name):
    _, R, Cc = w.shape
    S = gs.shape[0]
    q = R // 128
    dense = lambda a: jnp.transpose(a, (2, 0, 1)).reshape(Cc * q, 128)
    back = lambda a: jnp.transpose(a.reshape(Cc, q, 128), (1, 2, 0)).reshape(1, R, Cc)

    def body(w_ref, g_ref, m_ref, v_ref, go_ref, d_ref, mo_ref, vo_ref, gt):
        for i in range(q):
            g = g_ref[0, i * 128:(i + 1) * 128, :].astype(f32)
            for s in range(1, S):
                g = g + g_ref[s, i * 128:(i + 1) * 128, :].astype(f32)
            gt[pl.ds(i, 128, stride=q), :] = g.T
        _adam_tile(gt[...], w_ref, m_ref, v_ref, go_ref, d_ref, mo_ref, vo_ref)

    blk = pl.BlockSpec((128 * q, 128), lambda j: (j, 0))
    outs = pl.pallas_call(
        body, grid=(pl.cdiv(Cc, 128),),
        in_specs=[blk, pl.BlockSpec((S, R, 128), lambda j: (0, 0, j)), blk, blk],
        out_specs=[blk] * 4, out_shape=[jax.ShapeDtypeStruct((Cc * q, 128), f32)] * 4,
        scratch_shapes=[pltpu.VMEM((128 * q, 128), f32)],
        compiler_params=_cparams(("parallel",)), name=name,
    )(dense(w), gs, dense(m), dense(v))
    return [back(o) for o in outs]


def _me():
    x, y, c = lax.axis_index("x"), lax.axis_index("y"), lax.axis_index("c")
    return x, y, c, 4 * x + 2 * y + c


def _peer(k):
    x, y, c, _ = _me()
    px = 1 - x if k & 4 else x
    py = 1 - y if k & 2 else y
    pc = 1 - c if k & 1 else c
    return (px, py, pc), 4 * px + 2 * py + pc


_HBM = pl.BlockSpec(memory_space=pltpu.HBM)
_SEM = pl.BlockSpec(memory_space=pltpu.SEMAPHORE)
_EFFECT = pltpu.SideEffectType.DATAFLOW_SIDE_EFFECTING


def _copy(src, land, ssem, rsem, k, blocked, landing_slot_of_peer):
    pid, pidx = _peer(k)
    slot = pidx if landing_slot_of_peer else _me()[3]
    return pltpu.make_async_remote_copy(src_ref=src.at[pidx] if blocked else src, dst_ref=land.at[slot],
                                        send_sem=ssem.at[k - 1], recv_sem=rsem.at[k - 1], device_id=pid, device_id_type=MESH)


def _own_copy(src, land, rsem, blocked):
    me = _me()[3]
    return pltpu.make_async_copy(src.at[me] if blocked else src, land.at[me], rsem.at[N_DEV - 1])


def _send_start(srcs, blocked, name):
    n = len(srcs)
    lands = [lax.empty(a.shape if blocked else (N_DEV,) + a.shape, a.dtype) for a in srcs]

    def body(*refs):
        src, land, sems, token = refs[:n], refs[n:2 * n], refs[2 * n:4 * n], refs[-1]
        for i in range(n):
            for k in range(1, N_DEV):
                _copy(src[i], land[i], sems[2 * i], sems[2 * i + 1], k, blocked, False).start()
        for i in range(n):
            _own_copy(src[i], land[i], sems[2 * i + 1], blocked).start()
        token[...] = jnp.zeros_like(token)

    sems = (pltpu.SemaphoreType.DMA((N_DEV - 1,)), pltpu.SemaphoreType.DMA((N_DEV,)))
    hbm = lambda a: pltpu.with_memory_space_constraint(a, pltpu.HBM)
    res = pl.pallas_call(
        body, name=name,
        out_shape=sems * n + tuple(pltpu.HBM(a.shape, a.dtype) for a in srcs + lands)
        + (jax.ShapeDtypeStruct((8, 128), f32),),
        in_specs=(_HBM,) * (2 * n),
        out_specs=(_SEM,) * (2 * n) + (_HBM,) * (2 * n) + (pl.BlockSpec(memory_space=pltpu.VMEM),),
        input_output_aliases={j: 2 * n + j for j in range(2 * n)},
        compiler_params=pltpu.CompilerParams(has_side_effects=_EFFECT),
    )(*[hbm(a) for a in srcs], *[hbm(a) for a in lands])
    handles = [(res[2 * i], res[2 * i + 1], res[2 * n + i], res[3 * n + i]) for i in range(n)]
    return handles, res[-1]


def _send_wait(handle, blocked, after, name):
    ssem, rsem, src, land = handle
    after = tuple(after) if isinstance(after, (tuple, list)) else (after,)

    def body(src_ref, land_ref, ssem_ref, rsem_ref, *rest):
        for k in range(1, N_DEV):
            cp = _copy(src_ref, land_ref, ssem_ref, rsem_ref, k, blocked, True)
            cp.wait_send()
            cp.wait_recv()
        _own_copy(src_ref, land_ref, rsem_ref, blocked).wait()

    return pl.pallas_call(
        body, name=name, out_shape=(pltpu.HBM(src.shape, src.dtype), pltpu.HBM(land.shape, land.dtype)),
        in_specs=(_HBM, _HBM, _SEM, _SEM) + (pl.BlockSpec(memory_space=pl.ANY),) * len(after), out_specs=(_HBM, _HBM),
        input_output_aliases={0: 0, 1: 1}, compiler_params=pltpu.CompilerParams(has_side_effects=_EFFECT),
    )(src, land, ssem, rsem, *after)


def _block_diag(w):
    nb, bs = w.shape[0], w.shape[1]
    eye = jnp.eye(nb, dtype=w.dtype)
    return (eye[:, None, :, None] * w[:, :, None, :]).reshape(nb * bs, nb * bs)


_SQUARE_TILES = dict(tm=1024, tn=1024, tk=1024)


def _mlp_fwd(x, hm, wu, wd, tag, epilogue, extras, outs):
    (r,) = _matmul(hm, wu, "nn", outs=[bf16], epilogue=lambda acc: (jnp.maximum(acc, 0.0),), name=f"mlp_up_{tag}")
    res = _matmul(r, wd, "nn", outs=outs, extras=(x,) + tuple(extras), epilogue=epilogue, a_map=jnp.square,
                  name=f"mlp_down_{tag}", **_SQUARE_TILES)
    return res, (hm, r)


def _mlp_bwd(x, nw, wu, wd, saved, dxo, dxo_b, tag, sink):
    hm, r = saved
    (du,) = _matmul(dxo_b, wd, "nt", outs=[bf16], extras=(r,), epilogue=lambda acc, rr: (acc * (2.0 * rr.astype(f32)),),
                    name=f"mlp_dact_{tag}")
    (dwd,) = _matmul(r, dxo_b, "tn", outs=[bf16], a_map=jnp.square, name=f"mlp_dwd_{tag}", **_SQUARE_TILES)
    tok = sink({f"w_down{tag}": dwd.reshape(N_DEV, D_FF // N_DEV, D_MODEL)})
    (dwu,) = _matmul(hm, du, "tn", outs=[bf16], shard_cols=2, extras=(jnp.broadcast_to(tok[0:1, 0:1], (1, D_FF)),),
                     epilogue=lambda acc, zero: (acc + zero,), name=f"mlp_dwu_{tag}")
    tok = sink({f"w_up{tag}": dwu})
    return _matmul(du, wu, "nt", outs=_RMS_BWD_OUTS, extras=(x, dxo, nw + tok[0:1, 0:1]), epilogue=_rms_bwd_ep,
                   name=f"mlp_dh_{tag}", **_SQUARE_TILES)


def _local_step(x, tgt, P, weight, sink):
    T = x.shape[0]
    cos, sin = _rope_tables(T)
    rtab = _ret_tables()
    row = lambda a: a.reshape(1, -1)
    mix_nw, mlp_nw = P["mixer_norm_w"], P["mlp_norm_w"]
    wr_bd, wi_bd = _block_diag(P["lru_w_r"]), _block_diag(P["lru_w_i"])
    lru_b, lru_br, lru_bi, lru_lam = row(P["lru_conv_b"]), row(P["lru_b_r"]), row(P["lru_b_i"]), row(P["lru_lambda"])
    pad16 = lambda a: jnp.pad(a.reshape(1, GDN_HEADS), ((0, 0), (GDN_HEADS, 128 - 2 * GDN_HEADS)))
    alog, dtb = pad16(P["gdn_a_log"]), pad16(P["gdn_dt_bias"])
    gnw = row(P["gdn_norm_w"])

    x0 = x
    h0 = _rms_fwd(x0, mix_nw[0:1], "rms_mix_0")
    w_ie = weight("w_in_even", (h0, cos, sin, wr_bd, wi_bd))
    (pe,) = _matmul(h0, w_ie, "nn", outs=[f32], name="in_even")
    mix0, o_ret, s_ret = _ret_fwd(pe, cos, sin, rtab, "ret_fwd")
    w_lc = weight("lru_conv_w", pe)
    xc = _conv_fwd(pe, 4, w_lc, lru_b, False, "lru_conv_fwd")
    mix0, h_lru = _lru_fwd(xc, pe, 20, wr_bd, wi_bd, lru_br, lru_bi, lru_lam, mix0, "lru_fwd")
    w_oe = weight("w_out_even", mix0)
    x1, hm0 = _matmul(mix0, w_oe, "nn", outs=[f32, bf16], extras=(x0, mlp_nw[0:1]), epilogue=_residual_rms_ep,
                      name="out_even", tm=1024, tn=D_MODEL)
    w_u0, w_d0 = weight("w_up0", x1), weight("w_down0", x1)
    (x2, h1), mlp0 = _mlp_fwd(x1, hm0, w_u0, w_d0, "0", _residual_rms_ep, (mix_nw[1:2],), [f32, bf16])
    w_io = weight("w_in_odd", h1)
    (po,) = _matmul(h1, w_io, "nn", outs=[f32], tm=2048, tn=ODD_PAD // 3, name="in_odd")
    w_gc = weight("gdn_conv_w", po)
    qkv = _conv_fwd(po, 0, w_gc, None, True, "gdn_conv_fwd")
    y_gdn, s_gdn, ti_gdn, vn_gdn, o_gdn = _gdn_fwd(qkv, po, alog, dtb, gnw, "gdn_fwd")
    w_oo = weight("w_out_odd", y_gdn)
    x3, hm1 = _matmul(y_gdn, w_oo, "nn", outs=[f32, bf16], extras=(x2, mlp_nw[1:2]), epilogue=_residual_rms_ep,
                      name="out_odd", tm=1024, tn=D_MODEL)
    w_u1, w_d1 = weight("w_up1", x3), weight("w_down1", x3)
    (loss, dx4, dx4_b, d_final), mlp1 = _mlp_fwd(x3, hm1, w_u1, w_d1, "1", _loss_ep, (row(P["final_norm_w"]), tgt),
                                                 _LOSS_OUTS)
    dx3, dx3_b, d_mlp_nw1 = _mlp_bwd(x3, mlp_nw[1:2], w_u1, w_d1, mlp1, dx4, dx4_b, "1", sink)
    (dy_gdn,) = _matmul(dx3_b, w_oo, "nt", outs=[f32], name="out_odd_dx")
    (d_woo,) = _matmul(y_gdn, dx3_b, "tn", outs=[bf16], name="out_odd_dw")
    dqkv, dpo, dba, d_alog, d_dtb, d_gnw = _gdn_bwd(qkv, po, alog, dtb, gnw, s_gdn, ti_gdn, vn_gdn, o_gdn, dy_gdn,
                                                  "gdn_bwd")
    dpo, d_gconv, _ = _conv_bwd(po, 0, w_gc, None, True, dqkv, dpo, "gdn_conv_bwd")
    dpo = lax.dynamic_update_slice(dpo, dba, (0, 4 * D_MODEL))
    (d_wio,) = _matmul(h1, dpo, "tn", outs=[bf16], tn=ODD_PAD // 3, name="in_odd_dw")
    tok = sink(dict(w_out_odd=d_woo.reshape(N_DEV, D_MODEL // N_DEV, D_MODEL), w_in_odd=_odd_split(d_wio, "w_in_odd_split")))
    dx2, dx2_b, d_mix_nw1 = _matmul(dpo, w_io, "nt", outs=_RMS_BWD_OUTS, extras=(x2, dx3, mix_nw[1:2] + tok[0:1, 0:1]),
                                    epilogue=_rms_bwd_ep, tm=1024, tn=1024, tk=ODD_PAD // 3, name="in_odd_dx")
    dx1, dx1_b, d_mlp_nw0 = _mlp_bwd(x1, mlp_nw[0:1], w_u0, w_d0, mlp0, dx2, dx2_b, "0", sink)
    (d_woe,) = _matmul(mix0, dx1_b, "tn", outs=[bf16], name="out_even_dw")
    tok = sink(dict(w_out_even=d_woe.reshape(N_DEV, D_MODEL // N_DEV, D_MODEL)))
    (dmix0,) = _matmul(dx1_b, w_oe, "nt", outs=[f32], name="out_even_dx")
    dpe = _ret_bwd(pe, cos, sin, rtab, o_ret, s_ret, dmix0, "ret_bwd")
    dxc, dpe, d_wr, d_wi, d_br, d_bi, d_lam = _lru_bwd(xc, pe, 20, wr_bd, wi_bd, lru_br, lru_bi, lru_lam + tok[0:1, 0:1],
                                                       h_lru, dmix0, 4, dpe, "lru_bwd")
    dpe, d_lconv, d_lconv_b = _conv_bwd(pe, 4, w_lc, lru_b, False, dxc, dpe, "lru_conv_bwd")
    G = dict(
        mlp_norm_w=jnp.concatenate([d_mlp_nw0, d_mlp_nw1], axis=0),
        final_norm_w=d_final.reshape(-1),
        lru_conv_w=d_lconv, lru_conv_b=d_lconv_b.reshape(-1),
        lru_w_r=d_wr, lru_b_r=d_br.reshape(-1), lru_w_i=d_wi, lru_b_i=d_bi.reshape(-1),
        lru_lambda=d_lam.reshape(-1), gdn_conv_w=d_gconv,
        gdn_a_log=d_alog[0, GDN_HEADS:2 * GDN_HEADS], gdn_dt_bias=d_dtb[0, GDN_HEADS:2 * GDN_HEADS],
        gdn_norm_w=d_gnw.reshape(-1),
    )
    packed = _pack([G[k] for k in _SMALL] + [d_lconv, d_gconv, loss[0, 0:1]])
    tok = sink(dict(small=jnp.broadcast_to(packed[None], (N_DEV,) + packed.shape)))
    (d_wie,) = _matmul(h0, dpe, "tn", outs=[bf16], shard_cols=2, extras=(jnp.broadcast_to(tok[0:1, 0:1], (1, 3 * D_MODEL)),),
                       epilogue=lambda acc, zero: (acc + zero,), name="in_even_dw")
    tok = sink(dict(w_in_even=d_wie))
    dx0, _, d_mix_nw0 = _matmul(dpe, w_ie, "nt", outs=_RMS_BWD_OUTS, extras=(x0, dx1, mix_nw[0:1] + tok[0:1, 0:1]),
                                epilogue=_rms_bwd_ep, name="in_even_dx", **_SQUARE_TILES)
    G["mixer_norm_w"] = jnp.concatenate([d_mix_nw0, d_mix_nw1], axis=0)
    return loss, dx0, G


_SMALL = ["mlp_norm_w", "final_norm_w", "lru_conv_b", "lru_w_r", "lru_b_r", "lru_w_i", "lru_b_i",
          "lru_lambda", "gdn_a_log", "gdn_dt_bias", "gdn_norm_w"]
_PACK_ROWS = 688


def _pack(parts):
    flat = jnp.concatenate([p.reshape(-1) for p in parts])
    return jnp.pad(flat, (0, _PACK_ROWS * 128 - flat.shape[0])).reshape(_PACK_ROWS, 128)


def _unpack(packed, shapes):
    flat = packed.reshape(-1)
    out, off = [], 0
    for s in shapes:
        n = int(np.prod(s))
        out.append(flat[off:off + n].reshape(s))
        off += n
    return out


def kernel(x, mixer_norm_w, mlp_norm_w, final_norm_w, w_in_even, lru_conv_w, lru_conv_b, lru_w_r, lru_b_r, lru_w_i, lru_b_i, lru_lambda, w_out_even, w_in_odd, gdn_conv_w, gdn_a_log, gdn_dt_bias, gdn_norm_w, w_out_odd, w_up, w_down, loss_target, m_mixer_norm_w, m_mlp_norm_w, m_final_norm_w, m_w_in_even, m_lru_conv_w, m_lru_conv_b, m_lru_w_r, m_lru_b_r, m_lru_w_i, m_lru_b_i, m_lru_lambda, m_w_out_even, m_w_in_odd, m_gdn_conv_w, m_gdn_a_log, m_gdn_dt_bias, m_gdn_norm_w, m_w_out_odd, m_w_up, m_w_down, v_mixer_norm_w, v_mlp_norm_w, v_final_norm_w, v_w_in_even, v_lru_conv_w, v_lru_conv_b, v_lru_w_r, v_lru_b_r, v_lru_w_i, v_lru_b_i, v_lru_lambda, v_w_out_even, v_w_in_odd, v_gdn_conv_w, v_gdn_a_log, v_gdn_dt_bias, v_gdn_norm_w, v_w_out_odd, v_w_up, v_w_down):
    Pw = dict(mixer_norm_w=mixer_norm_w, mlp_norm_w=mlp_norm_w, final_norm_w=final_norm_w, w_in_even=w_in_even,
              lru_conv_w=lru_conv_w, lru_conv_b=lru_conv_b, lru_w_r=lru_w_r, lru_b_r=lru_b_r, lru_w_i=lru_w_i,
              lru_b_i=lru_b_i, lru_lambda=lru_lambda, w_out_even=w_out_even, w_in_odd=w_in_odd, gdn_conv_w=gdn_conv_w,
              gdn_a_log=gdn_a_log, gdn_dt_bias=gdn_dt_bias, gdn_norm_w=gdn_norm_w, w_out_odd=w_out_odd, w_up=w_up,
              w_down=w_down)
    Pm = dict(mixer_norm_w=m_mixer_norm_w, mlp_norm_w=m_mlp_norm_w, final_norm_w=m_final_norm_w, w_in_even=m_w_in_even,
              lru_conv_w=m_lru_conv_w, lru_conv_b=m_lru_conv_b, lru_w_r=m_lru_w_r, lru_b_r=m_lru_b_r, lru_w_i=m_lru_w_i,
              lru_b_i=m_lru_b_i, lru_lambda=m_lru_lambda, w_out_even=m_w_out_even, w_in_odd=m_w_in_odd,
              gdn_conv_w=m_gdn_conv_w, gdn_a_log=m_gdn_a_log, gdn_dt_bias=m_gdn_dt_bias, gdn_norm_w=m_gdn_norm_w,
              w_out_odd=m_w_out_odd, w_up=m_w_up, w_down=m_w_down)
    Pv = dict(mixer_norm_w=v_mixer_norm_w, mlp_norm_w=v_mlp_norm_w, final_norm_w=v_final_norm_w, w_in_even=v_w_in_even,
              lru_conv_w=v_lru_conv_w, lru_conv_b=v_lru_conv_b, lru_w_r=v_lru_w_r, lru_b_r=v_lru_b_r, lru_w_i=v_lru_w_i,
              lru_b_i=v_lru_b_i, lru_lambda=v_lru_lambda, w_out_even=v_w_out_even, w_in_odd=v_w_in_odd,
              gdn_conv_w=v_gdn_conv_w, gdn_a_log=v_gdn_a_log, gdn_dt_bias=v_gdn_dt_bias, gdn_norm_w=v_gdn_norm_w,
              w_out_odd=v_w_out_odd, w_up=v_w_up, w_down=v_w_down)
    me = _me()[3]
    T = x.shape[1]

    cols = lambda g: jnp.transpose(g, (1, 0, 2)).reshape(g.shape[1], -1)
    rows = lambda g: g.reshape(-1, g.shape[2])
    wide = lambda g: _odd_assemble(g, "w_in_odd_assemble")
    odd_shard = jnp.pad(w_in_odd[0].astype(bf16), ((0, 0), (0, ODD_SHARD_PAD - ODD_SHARD)))
    as_is = lambda g: g
    gather = dict(
        w_in_even=(w_in_even[0].astype(bf16), cols), lru_conv_w=(lru_conv_w[0], cols),
        w_out_even=(w_out_even[0].astype(bf16), rows), w_up0=(w_up[0].astype(bf16), as_is), w_down0=(w_down[0].astype(bf16), rows),
        w_in_odd=(odd_shard, wide), gdn_conv_w=(gdn_conv_w[0], cols),
        w_out_odd=(w_out_odd[0].astype(bf16), rows), w_up1=(w_up[1].astype(bf16), as_is), w_down1=(w_down[1].astype(bf16), rows))
    handles, tok = _send_start([s for s, _ in gather.values()], False, "gather_start")
    handles = dict(zip(gather, handles))
    full = {}

    def weight(name, after):
        if name not in full:
            full[name] = gather[name][1](_send_wait(handles[name], False, after, f"gather_wait_{name}")[1])
        return full[name]

    P = {k: Pw[k] for k in ("mlp_norm_w", "final_norm_w")}
    P["mixer_norm_w"] = mixer_norm_w + tok[0:1, 0:1]
    for k in ("lru_w_r", "lru_w_i", "lru_conv_b", "lru_b_r", "lru_b_i", "lru_lambda", "gdn_a_log", "gdn_dt_bias", "gdn_norm_w"):
        P[k] = Pw[k][0]

    sent = {}

    def sink(grads):
        hs, token = _send_start(list(grads.values()), True, "grads_start_" + "_".join(grads))
        sent.update(zip(grads, hs))
        return token

    loss, dx, G = _local_step(x[0], loss_target[0], P, weight, sink)
    lanes = lambda a: a.reshape(-1, 128)
    sink(dict(mixer_norm_w=jnp.broadcast_to(lanes(G["mixer_norm_w"])[None], (N_DEV, 2 * D_MODEL // 128, 128))))

    def received(name, after=dx):
        return _send_wait(sent[name], True, after, f"grads_wait_{name}")[1]

    out = {}
    nff = D_FF // N_DEV

    def whole(name, gs):
        out[name] = tuple(_adamw(Pw[name], gs, Pm[name], Pv[name], f"adamw_{name}", layer=0))

    def layers(name):
        res = None
        for l in range(2):
            res = _adamw(Pw[name], received(f"{name}{l}"), Pm[name], Pv[name], f"adamw_{name}{l}", layer=l, prev=res)
        out[name] = tuple(res)

    layers("w_up")
    layers("w_down")
    whole("w_out_odd", received("w_out_odd"))
    out["w_in_odd"] = tuple(_adamw_column_major(w_in_odd, received("w_in_odd"), m_w_in_odd, v_w_in_odd, "adamw_w_in_odd"))
    whole("w_out_even", received("w_out_even"))
    small_shapes = [Pw[k].shape for k in _SMALL]
    pw, pm, pv = (_pack([Q[k] for k in _SMALL]) for Q in (Pw, Pm, Pv))
    sg, sd, sm, sv = _adamw(pw, received("small", out["w_out_even"][1]), pm, pv, "adamw_small")
    for arrs_i, packed_out in enumerate((sg, sd, sm, sv)):
        for k, a in zip(_SMALL, _unpack(packed_out, small_shapes)):
            out.setdefault(k, [None] * 4)[arrs_i] = a
    whole("w_in_even", received("w_in_even", sd))
    out["mixer_norm_w"] = tuple(
        a.reshape(mixer_norm_w.shape) for a in
        _adamw(lanes(mixer_norm_w), received("mixer_norm_w", out["w_in_even"][1]), lanes(m_mixer_norm_w),
               lanes(v_mixer_norm_w), "adamw_mixer_norm_w"))
    n_small = sum(int(np.prod(s)) for s in small_shapes)
    gflat = sg.reshape(-1)
    g_lconv = gflat[n_small:n_small + CONV_K * LRU_WIDTH].reshape(CONV_K, LRU_WIDTH)
    g_gconv = gflat[n_small + CONV_K * LRU_WIDTH:n_small + CONV_K * (LRU_WIDTH + 3072)].reshape(CONV_K, 3072)
    whole("lru_conv_w", lax.dynamic_slice_in_dim(g_lconv, me * 64, 64, axis=1)[None])
    whole("gdn_conv_w", lax.dynamic_slice_in_dim(g_gconv, me * 384, 384, axis=1)[None])

    names = ["mixer_norm_w", "mlp_norm_w", "final_norm_w", "w_in_even", "lru_conv_w", "lru_conv_b", "lru_w_r", "lru_b_r",
             "lru_w_i", "lru_b_i", "lru_lambda", "w_out_even", "w_in_odd", "gdn_conv_w", "gdn_a_log", "gdn_dt_bias",
             "gdn_norm_w", "w_out_odd", "w_up", "w_down"]
    total = gflat[n_small + CONV_K * (LRU_WIDTH + 3072)]
    res = [total, dx[None]]
    for j in range(4):
        res += [out[k][j] for k in names]
    return tuple(res)
```

```python
import math

import numpy as np
import jax
import jax.numpy as jnp
from jax import lax
from jax.experimental import pallas as pl
from jax.experimental.pallas import tpu as pltpu

f32 = jnp.float32
bf16 = jnp.bfloat16

N_DEV = 8
D_MODEL = 1024
D_FF = 4096
EPS = 1e-6
RET_HEADS = 4
RET_CHUNK = 128
RET_STEP = 4
ROPE_THETA = 10000.0
LRU_WIDTH = 512
LRU_C = 8.0
GDN_HEADS = 8
GDN_CHUNK = 64
GDN_STEP = 4
HEAD_DIM = 128
ODD_IN = 4112
ODD_PAD = 4224
ODD_SHARD = ODD_IN // N_DEV
ODD_SHARD_PAD = 640
ADAM_LR, ADAM_B1, ADAM_B2, ADAM_EPS, ADAM_WD, ADAM_STEP = 0.001, 0.9, 0.999, 1e-08, 0.01, 10
VMEM_LIMIT = 56 * 1024 * 1024

_NN = (((1,), (0,)), ((), ()))
_NT = (((1,), (1,)), ((), ()))
_TN = (((0,), (0,)), ((), ()))
MESH = pl.DeviceIdType.MESH


def _cparams(sem):
    return pltpu.CompilerParams(dimension_semantics=sem, vmem_limit_bytes=VMEM_LIMIT)


def _dot(a, b, dn):
    return lax.dot_general(a.astype(bf16), b.astype(bf16), dn, preferred_element_type=f32)


def _dot01(a01, b, dn):
    a = a01.astype(bf16)
    b0 = b.astype(bf16)
    r1 = b - b0.astype(f32)
    b1 = r1.astype(bf16)
    b2 = (r1 - b1.astype(f32)).astype(bf16)
    d = lambda q: lax.dot_general(a, q, dn, preferred_element_type=f32)
    return d(b0) + (d(b1) + d(b2))


def _sigmoid(x):
    return jax.nn.sigmoid(x)


def _silu(x):
    return x * _sigmoid(x)


def _dsilu(x):
    s = _sigmoid(x)
    return s * (1.0 + x * (1.0 - s))


def _softplus(x):
    return jnp.maximum(x, 0.0) + jnp.log1p(jnp.exp(-jnp.abs(x)))


_GELU_C = math.sqrt(2.0 / math.pi)


def _gelu(y):
    return 0.5 * y * (1.0 + jnp.tanh(_GELU_C * (y + 0.044715 * y * y * y)))


def _dgelu(y):
    t = jnp.tanh(_GELU_C * (y + 0.044715 * y * y * y))
    return 0.5 * (1.0 + t) + 0.5 * y * (1.0 - t * t) * _GELU_C * (1.0 + 3.0 * 0.044715 * y * y)


def _matmul(a, b, form, *, outs, name, epilogue=None, extras=(), tm=4096, tn=512, tk=1024, shard_cols=0, a_map=None):
    if form == "tn":
        K, M = a.shape
    else:
        M, K = a.shape
    per_step = 1
    if b.ndim == 3:
        assert form in ("nn", "nt"), name
        N = b.shape[1] if form == "nt" else N_DEV * b.shape[2]
        if form == "nn":
            tn = b.shape[2]
        else:
            per_step = max(1, tk // b.shape[2])
            tk = per_step * b.shape[2]
    else:
        N = b.shape[0] if form == "nt" else b.shape[1]
    ns = N // N_DEV
    if shard_cols:
        tn = ns * shard_cols
    tm, tn, tk = min(tm, M), min(tn, N), min(tk, K)
    assert M % tm == 0 and N % tn == 0 and K % tk == 0, (name, M, N, K, tm, tn, tk)
    nk = K // tk
    dn = {"nn": _NN, "nt": _NT, "tn": _TN}[form]
    if form == "tn":
        a_spec = pl.BlockSpec((tk, tm), lambda i, j, k: (k, i))
    else:
        a_spec = pl.BlockSpec((tm, tk), lambda i, j, k: (i, k))
    if b.ndim == 3:
        b_spec = (pl.BlockSpec((per_step, tn, tk // per_step), lambda i, j, k: (k, j, 0)) if form == "nt"
                  else pl.BlockSpec((None, tk, tn), lambda i, j, k: (j, k, 0)))
    elif form == "nt":
        b_spec = pl.BlockSpec((tn, tk), lambda i, j, k: (j, k))
    else:
        b_spec = pl.BlockSpec((tk, tn), lambda i, j, k: (k, j))
    e_spec = pl.BlockSpec((tm, tn), lambda i, j, k: (i, j))
    v_spec = pl.BlockSpec((1, tn), lambda i, j, k: (0, j))
    if shard_cols:
        o_spec = pl.BlockSpec((shard_cols, tm, ns), lambda i, j, k: (j, i, 0))
        o_shape = (N_DEV, M, ns)
    else:
        o_spec = e_spec
        o_shape = (M, N)
    n_ex = len(extras)
    sums = [isinstance(o, tuple) for o in outs]
    assert not any(sums) or tn == N, name

    def finish(acc, ex, o_refs, row_tile):
        vals = (acc,) if epilogue is None else epilogue(acc, *[e[...] for e in ex])
        for r, v, is_sum in zip(o_refs, vals, sums):
            if is_sum:
                @pl.when(row_tile == 0)
                def _(r=r, v=v):
                    r[...] = v.astype(r.dtype)

                @pl.when(row_tile > 0)
                def _(r=r, v=v):
                    r[...] += v.astype(r.dtype)
            elif shard_cols:
                for s in range(shard_cols):
                    r[s] = v[:, s * ns:(s + 1) * ns].astype(r.dtype)
            else:
                r[...] = v.astype(r.dtype)

    def prod(a_ref, b_ref):
        if b.ndim == 3 and form == "nt":
            w = tk // per_step
            return sum(_dot(a_ref[:, s * w:(s + 1) * w], b_ref[s], dn) for s in range(1, per_step)) + _dot(a_ref[:, 0:w], b_ref[0], dn)
        av = a_ref[...]
        return _dot(av if a_map is None else a_map(av), b_ref[...], dn)

    def body_one(*refs):
        finish(prod(*refs[:2]), refs[2:2 + n_ex], refs[2 + n_ex:], pl.program_id(0))

    def body_acc(*refs):
        a_ref, b_ref = refs[:2]
        acc = refs[-1]
        k = pl.program_id(2)
        row_tile = pl.program_id(0)

        @pl.when(k == 0)
        def _():
            acc[...] = prod(a_ref, b_ref)

        @pl.when((k > 0) & (k < nk - 1))
        def _():
            acc[...] += prod(a_ref, b_ref)

        @pl.when(k == nk - 1)
        def _():
            finish(acc[...] + prod(a_ref, b_ref), refs[2:2 + n_ex], refs[2 + n_ex:-1], row_tile)

    return pl.pallas_call(
        body_one if nk == 1 else body_acc, grid=(M // tm, N // tn, nk),
        in_specs=[a_spec, b_spec] + [v_spec if e.shape[0] == 1 else e_spec for e in extras],
        out_specs=[v_spec if s else o_spec for s in sums],
        out_shape=[jax.ShapeDtypeStruct((1, N), o[1]) if s else jax.ShapeDtypeStruct(o_shape, o) for o, s in zip(outs, sums)],
        scratch_shapes=[] if nk == 1 else [pltpu.VMEM((tm, tn), f32)],
        compiler_params=_cparams(("arbitrary" if any(sums) else "parallel", "parallel", "arbitrary")), name=name,
    )(a, b, *extras)


def _rms_fwd(x, w, name):
    T, D = x.shape
    tt = min(512, T)

    def body(x_ref, w_ref, h_ref):
        xv = x_ref[...]
        r = lax.rsqrt(jnp.mean(xv * xv, axis=1, keepdims=True) + EPS)
        h_ref[...] = (xv * r * w_ref[...]).astype(bf16)

    return pl.pallas_call(
        body, grid=(T // tt,),
        in_specs=[pl.BlockSpec((tt, D), lambda i: (i, 0)), pl.BlockSpec((1, D), lambda i: (0, 0))],
        out_specs=pl.BlockSpec((tt, D), lambda i: (i, 0)),
        out_shape=jax.ShapeDtypeStruct((T, D), bf16),
        compiler_params=_cparams(("parallel",)), name=name,
    )(x, w)


def _residual_rms_ep(acc, res, w):
    x = res + acc
    r = lax.rsqrt(jnp.mean(x * x, axis=1, keepdims=True) + EPS)
    return x, x * r * w


_RMS_BWD_OUTS = [f32, bf16, ("sum", f32)]


def _rms_bwd_ep(dh, x, dres, w):
    r = lax.rsqrt(jnp.mean(x * x, axis=1, keepdims=True) + EPS)
    xn = x * r
    dhw = dh * w
    dx = dres + r * (dhw - xn * jnp.mean(dhw * xn, axis=1, keepdims=True))
    return dx, dx, jnp.sum(dh * xn, axis=0, keepdims=True)


_LOSS_OUTS = [("sum", f32), f32, bf16, ("sum", f32)]


def _loss_ep(acc, res, w, tgt):
    x = res + acc
    D = x.shape[1]
    r = lax.rsqrt(jnp.mean(x * x, axis=1, keepdims=True) + EPS)
    xn = x * r
    e = xn * w - tgt
    loss = 0.5 * jnp.sum(jnp.mean(e * e, axis=1, keepdims=True), axis=0, keepdims=True)
    dy = e * (1.0 / D)
    dyw = dy * w
    dx = r * (dyw - xn * jnp.mean(dyw * xn, axis=1, keepdims=True))
    return jnp.broadcast_to(loss, (1, D)), dx, dx, jnp.sum(dy * xn, axis=0, keepdims=True)


def _ret_tables():
    H, C = RET_HEADS, RET_CHUNK
    lg = np.log1p(-np.exp2(-5.0 - np.arange(H, dtype=np.float32))).astype(np.float32)
    idx = np.arange(C, dtype=np.float32)
    diff = idx[:, None] - idx[None, :]
    causal = diff >= 0
    dm = np.where(causal[None], np.exp(lg[:, None, None] * np.where(causal, diff, 0.0)[None]), 0.0)
    qd = np.exp(lg[:, None] * (idx[None, :] + 1.0))
    kd = np.exp(lg[:, None] * (C - 1.0 - idx[None, :]))
    cg = np.exp(lg * C)
    tab = np.zeros((H, 4, C, HEAD_DIM), np.float32)
    tab[:, 0] = dm
    tab[:, 1] = qd[:, :, None]
    tab[:, 2] = kd[:, :, None]
    tab[:, 3] = cg[:, None, None]
    return jnp.asarray(tab)


def _rope_tables(T):
    half = HEAD_DIM // 2
    inv = ROPE_THETA ** (-jnp.arange(half, dtype=f32) / half)
    ang = jnp.arange(T, dtype=jnp.int32).astype(f32)[:, None] * inv[None, :]
    c, s = jnp.cos(ang), jnp.sin(ang)
    return jnp.concatenate([c, c], axis=1), jnp.concatenate([-s, s], axis=1)


def _rope(x, cos, sin):
    return x * cos + pltpu.roll(x, HEAD_DIM // 2, 1) * sin


def _unrope(y, cos, sin):
    return y * cos + pltpu.roll(y * sin, HEAD_DIM // 2, 1)


def _stack_heads(ref, H, f=None):
    parts = [ref[:, h * HEAD_DIM:(h + 1) * HEAD_DIM] for h in range(H)]
    return jnp.stack(parts if f is None else [f(a) for a in parts])


def _ret_fwd(p, cos, sin, tab, name):
    T = p.shape[0]
    C, H = RET_CHUNK, RET_HEADS
    N = T // C
    K = min(RET_STEP, N)
    NS = N // K
    scale = HEAD_DIM ** -0.5

    def body(q_ref, k_ref, v_ref, g_ref, c_ref, s_ref, t_ref, y_ref, o_ref, sp_ref, st):
        @pl.when(pl.program_id(0) == 0)
        def _():
            st[...] = jnp.zeros_like(st)

        dm, qd, kd, cg = t_ref[:, 0], t_ref[:, 1], t_ref[:, 2], t_ref[:, 3]
        S = st[...]
        for c in range(K):
            rows = pl.ds(c * C, C)
            cos_, sin_ = c_ref[rows, :], s_ref[rows, :]
            rot = lambda a: _rope(a, cos_, sin_)
            q = _stack_heads(q_ref.at[rows, :], H, rot)
            k = _stack_heads(k_ref.at[rows, :], H, rot) * scale
            v = _stack_heads(v_ref.at[rows, :], H)
            P = _dot(q, k, _NT3) * dm
            o = _dot(P, v, _NN3) + _dot(q * qd, S, _NN3)
            sp_ref[c] = S
            S = cg * S + _dot(k * kd, v, _TN3)
            r = lax.rsqrt(jnp.mean(o * o, axis=2, keepdims=True) + EPS)
            y = o * r * _silu(_stack_heads(g_ref.at[rows, :], H))
            for h in range(H):
                o_ref[rows, h * HEAD_DIM:(h + 1) * HEAD_DIM] = o[h]
                y_ref[rows, h * HEAD_DIM:(h + 1) * HEAD_DIM] = y[h].astype(bf16)
        st[...] = S

    wide = lambda blk: pl.BlockSpec((K * C, H * HEAD_DIM), lambda n: (n, blk))
    tbl = pl.BlockSpec((K * C, HEAD_DIM), lambda n: (n, 0))
    return pl.pallas_call(
        body, grid=(NS,),
        in_specs=[wide(0), wide(1), wide(2), wide(3), tbl, tbl,
                  pl.BlockSpec((H, 4, C, HEAD_DIM), lambda n: (0, 0, 0, 0))],
        out_specs=[wide(0), wide(0), pl.BlockSpec((K, H, HEAD_DIM, HEAD_DIM), lambda n: (n, 0, 0, 0))],
        out_shape=[jax.ShapeDtypeStruct((T, D_MODEL), bf16), jax.ShapeDtypeStruct((T, H * HEAD_DIM), f32),
                   jax.ShapeDtypeStruct((N, H, HEAD_DIM, HEAD_DIM), f32)],
        scratch_shapes=[pltpu.VMEM((H, HEAD_DIM, HEAD_DIM), f32)],
        compiler_params=_cparams(("arbitrary",)), name=name,
    )(p, p, p, p, cos, sin, tab)


def _ret_bwd(p, cos, sin, tab, o_raw, sprev, dmix, name):
    T = p.shape[0]
    C, H = RET_CHUNK, RET_HEADS
    N = T // C
    K = min(RET_STEP, N)
    NS = N // K
    scale = HEAD_DIM ** -0.5
    W = H * HEAD_DIM

    def body(q_ref, k_ref, v_ref, g_ref, c_ref, s_ref, t_ref, o_ref, sp_ref, dy_ref, d_ref, dst):
        @pl.when(pl.program_id(0) == 0)
        def _():
            dst[...] = jnp.zeros_like(dst)

        dm, qd, kd, cg = t_ref[:, 0], t_ref[:, 1], t_ref[:, 2], t_ref[:, 3]
        dS1 = dst[...]
        for c in reversed(range(K)):
            rows = pl.ds(c * C, C)
            cos_, sin_ = c_ref[rows, :], s_ref[rows, :]
            rot = lambda a: _rope(a, cos_, sin_)
            q = _stack_heads(q_ref.at[rows, :], H, rot)
            k = _stack_heads(k_ref.at[rows, :], H, rot) * scale
            v = _stack_heads(v_ref.at[rows, :], H)
            g = _stack_heads(g_ref.at[rows, :], H)
            S = sp_ref[c]
            o = _stack_heads(o_ref.at[rows, :], H)
            dy = _stack_heads(dy_ref.at[rows, :], H)
            r = lax.rsqrt(jnp.mean(o * o, axis=2, keepdims=True) + EPS)
            nrm = o * r
            dn = dy * _silu(g)
            dg = dy * nrm * _dsilu(g)
            do = r * (dn - nrm * jnp.mean(dn * nrm, axis=2, keepdims=True))
            P = _dot(q, k, _NT3) * dm
            dP = _dot(do, v, _NT3) * dm
            dq = _dot(dP, k, _NN3) + _dot(do, S, _NT3) * qd
            dk = (_dot(dP, q, _TN3) + _dot(v, dS1, _NT3) * kd) * scale
            dv = _dot(P, do, _TN3) + _dot(k * kd, dS1, _NN3)
            dS1 = cg * dS1 + _dot(q * qd, do, _TN3)
            for h in range(H):
                d_ref[rows, h * HEAD_DIM:(h + 1) * HEAD_DIM] = _unrope(dq[h], cos_, sin_).astype(bf16)
                d_ref[rows, W + h * HEAD_DIM:W + (h + 1) * HEAD_DIM] = _unrope(dk[h], cos_, sin_).astype(bf16)
                d_ref[rows, 2 * W + h * HEAD_DIM:2 * W + (h + 1) * HEAD_DIM] = dv[h].astype(bf16)
                d_ref[rows, 3 * W + h * HEAD_DIM:3 * W + (h + 1) * HEAD_DIM] = dg[h].astype(bf16)
        dst[...] = dS1

    rev = lambda blk: pl.BlockSpec((K * C, W), lambda n: (NS - 1 - n, blk))
    tbl = pl.BlockSpec((K * C, HEAD_DIM), lambda n: (NS - 1 - n, 0))
    return pl.pallas_call(
        body, grid=(NS,),
        in_specs=[rev(0), rev(1), rev(2), rev(3), tbl, tbl,
                  pl.BlockSpec((H, 4, C, HEAD_DIM), lambda n: (0, 0, 0, 0)), rev(0),
                  pl.BlockSpec((K, H, HEAD_DIM, HEAD_DIM), lambda n: (NS - 1 - n, 0, 0, 0)), rev(0)],
        out_specs=pl.BlockSpec((K * C, 4 * W), lambda n: (NS - 1 - n, 0)),
        out_shape=jax.ShapeDtypeStruct((T, 6 * W), bf16),
        scratch_shapes=[pltpu.VMEM((H, HEAD_DIM, HEAD_DIM), f32)],
        compiler_params=_cparams(("arbitrary",)), name=name,
    )(p, p, p, p, cos, sin, tab, o_raw, sprev, dmix)


CONV_K = 4
CONV_W = 512
PAD = 8
SUB_R = 64


def _conv_fwd(x, col_off, w, b, act, name):
    T = x.shape[0]
    C = w.shape[1]
    G = C // CONV_W
    tt = min(512, T)
    NT = T // tt
    has_b = b is not None

    def body(*refs):
        if has_b:
            x_ref, w_ref, b_ref, y_ref, pad = refs
        else:
            x_ref, w_ref, y_ref, pad = refs
        t = pl.program_id(1)

        @pl.when(t == 0)
        def _():
            pad[pl.ds(0, PAD), :] = jnp.zeros((PAD, CONV_W), f32)

        pad[pl.ds(PAD, tt), :] = x_ref[...]
        for g in range(CONV_W // 128):
            ls = slice(g * 128, (g + 1) * 128)
            wv = w_ref[:, ls]
            for c in range(tt // SUB_R):
                r0 = c * SUB_R
                y = wv[0:1, :] * pad[pl.ds(PAD - 3 + r0, SUB_R), ls]
                for kk in range(1, CONV_K):
                    y = y + wv[kk:kk + 1, :] * pad[pl.ds(PAD - 3 + kk + r0, SUB_R), ls]
                if has_b:
                    y = y + b_ref[:, ls]
                y_ref[pl.ds(r0, SUB_R), ls] = _silu(y) if act else y
        tail = pad[pl.ds(tt, PAD), :]
        pad[pl.ds(0, PAD), :] = tail

    in_specs = [pl.BlockSpec((tt, CONV_W), lambda g, t: (t, col_off + g)),
                pl.BlockSpec((CONV_K, CONV_W), lambda g, t: (0, g))]
    args = [x, w]
    if has_b:
        in_specs.append(pl.BlockSpec((1, CONV_W), lambda g, t: (0, g)))
        args.append(b)
    return pl.pallas_call(
        body, grid=(G, NT), in_specs=in_specs,
        out_specs=pl.BlockSpec((tt, CONV_W), lambda g, t: (t, g)),
        out_shape=jax.ShapeDtypeStruct((T, C), f32),
        scratch_shapes=[pltpu.VMEM((tt + PAD, CONV_W), f32)],
        compiler_params=_cparams(("parallel", "arbitrary")), name=name,
    )(*args)


def _conv_bwd(x, col_off, w, b, act, dout, dp, name):
    T = x.shape[0]
    C = w.shape[1]
    G = C // CONV_W
    tt = min(512, T)
    NT = T // tt
    has_b = b is not None

    def body(*refs):
        if has_b:
            x_ref, xp_ref, w_ref, b_ref, d_ref, dp_in, dx_ref, dw_ref, db_ref, pad, dpad = refs
        else:
            x_ref, xp_ref, w_ref, d_ref, dp_in, dx_ref, dw_ref, db_ref, pad, dpad = refs
        t = pl.program_id(1)
        first_tile = t == NT - 1

        @pl.when(t == 0)
        def _():
            dpad[pl.ds(tt, PAD), :] = jnp.zeros((PAD, CONV_W), f32)
            dw_ref[...] = jnp.zeros_like(dw_ref)
            db_ref[...] = jnp.zeros_like(db_ref)

        pad[pl.ds(0, PAD), :] = jnp.where(first_tile, 0.0, xp_ref[...])
        pad[pl.ds(PAD, tt), :] = x_ref[...]
        fold = lambda v: v.reshape(SUB_R // 8, 8, 128).sum(axis=0)
        for g in range(CONV_W // 128):
            ls = slice(g * 128, (g + 1) * 128)
            wv = w_ref[:, ls]
            acc = [jnp.zeros((8, 128), f32) for _ in range(CONV_K + 1)]
            for c in reversed(range(tt // SUB_R)):
                r0 = c * SUB_R
                xs = [pad[pl.ds(PAD - 3 + kk + r0, SUB_R), ls] for kk in range(CONV_K)]
                dy = d_ref[pl.ds(r0, SUB_R), ls]
                if act:
                    y = wv[0:1, :] * xs[0]
                    for kk in range(1, CONV_K):
                        y = y + wv[kk:kk + 1, :] * xs[kk]
                    if has_b:
                        y = y + b_ref[:, ls]
                    dy = dy * _dsilu(y)
                dpad[pl.ds(r0, SUB_R), ls] = dy
                dx = wv[3:4, :] * dy
                for j in range(1, CONV_K):
                    dx = dx + wv[3 - j:4 - j, :] * dpad[pl.ds(r0 + j, SUB_R), ls]
                dx_ref[pl.ds(r0, SUB_R), ls] = dx.astype(bf16)
                for kk in range(CONV_K):
                    acc[kk] = acc[kk] + fold(dy * xs[kk])
                acc[CONV_K] = acc[CONV_K] + fold(dy)
            for kk in range(CONV_K):
                dw_ref[kk:kk + 1, ls] += jnp.sum(acc[kk], axis=0, keepdims=True)
            db_ref[:, ls] += jnp.sum(acc[CONV_K], axis=0, keepdims=True)
        head = dpad[pl.ds(0, PAD), :]
        dpad[pl.ds(tt, PAD), :] = head

    rows8 = tt // PAD
    in_specs = [pl.BlockSpec((tt, CONV_W), lambda g, t: (NT - 1 - t, col_off + g)),
                pl.BlockSpec((PAD, CONV_W), lambda g, t: (jnp.maximum((NT - 1 - t) * rows8 - 1, 0), col_off + g)),
                pl.BlockSpec((CONV_K, CONV_W), lambda g, t: (0, g))]
    args = [x, x, w]
    if has_b:
        in_specs.append(pl.BlockSpec((1, CONV_W), lambda g, t: (0, g)))
        args.append(b)
    in_specs += [pl.BlockSpec((tt, CONV_W), lambda g, t: (NT - 1 - t, g)), pl.BlockSpec(memory_space=pl.ANY)]
    args += [dout, dp]
    return pl.pallas_call(
        body, grid=(G, NT), in_specs=in_specs,
        out_specs=[pl.BlockSpec((tt, CONV_W), lambda g, t: (NT - 1 - t, col_off + g)),
                   pl.BlockSpec((CONV_K, CONV_W), lambda g, t: (0, g)),
                   pl.BlockSpec((1, CONV_W), lambda g, t: (0, g))],
        out_shape=[jax.ShapeDtypeStruct(dp.shape, dp.dtype), jax.ShapeDtypeStruct((CONV_K, C), f32),
                   jax.ShapeDtypeStruct((1, C), f32)],
        input_output_aliases={len(args) - 1: 0},
        scratch_shapes=[pltpu.VMEM((tt + PAD, CONV_W), f32), pltpu.VMEM((tt + PAD, CONV_W), f32)],
        compiler_params=_cparams(("parallel", "arbitrary")), name=name,
    )(*args)


def _lru_gates(xc, wr, wi, br, bi, lam):
    r = _sigmoid(_dot(xc, wr, _NN) + br)
    i = _sigmoid(_dot(xc, wi, _NN) + bi)
    sp = _softplus(-lam)
    a = jnp.exp(-LRU_C * r * sp)
    mult = jnp.sqrt(1.0 - a * a)
    return r, i, sp, a, mult


def _lru_fwd(xc, p, y_off, wr, wi, br, bi, lam, mix, name):
    T = xc.shape[0]
    G = LRU_WIDTH // 128
    tt = min(512, T)
    NT = T // tt

    def body(x_ref, y_ref, wr_ref, wi_ref, br_ref, bi_ref, l_ref, mix_in, o_ref, h_ref, hc):
        t = pl.program_id(1)

        @pl.when(t == 0)
        def _():
            hc[...] = jnp.zeros_like(hc)

        x = x_ref[...]
        r, i, sp, a, mult = _lru_gates(x, wr_ref[...], wi_ref[...], br_ref[...], bi_ref[...], l_ref[...])
        row = lax.broadcasted_iota(jnp.int32, (tt, 128), 0)
        mult = jnp.where((row == 0) & (t == 0), 1.0, mult)
        U = x * i * mult
        A = a
        d = 1
        while d < tt:
            keep = row >= d
            Ush = jnp.where(keep, pltpu.roll(U, d, 0), 0.0)
            Ash = jnp.where(keep, pltpu.roll(A, d, 0), 1.0)
            U = A * Ush + U
            A = A * Ash
            d *= 2
        h = U + A * hc[0:1, :]
        h_ref[...] = h
        hc[...] = jnp.broadcast_to(h[tt - 1:tt, :], hc.shape)
        o_ref[...] = (h * _gelu(y_ref[...])).astype(bf16)

    tile = pl.BlockSpec((tt, 128), lambda g, t: (t, g))
    vec = pl.BlockSpec((1, 128), lambda g, t: (0, g))
    wsp = pl.BlockSpec((128, 128), lambda g, t: (g, g))
    return pl.pallas_call(
        body, grid=(G, NT),
        in_specs=[tile, pl.BlockSpec((tt, 128), lambda g, t: (t, y_off + g)), wsp, wsp, vec, vec, vec,
                  pl.BlockSpec(memory_space=pl.ANY)],
        out_specs=[pl.BlockSpec((tt, 128), lambda g, t: (t, G + g)), tile],
        out_shape=[jax.ShapeDtypeStruct(mix.shape, mix.dtype), jax.ShapeDtypeStruct((T, LRU_WIDTH), f32)],
        input_output_aliases={7: 0},
        scratch_shapes=[pltpu.VMEM((8, 128), f32)],
        compiler_params=_cparams(("parallel", "arbitrary")), name=name,
    )(xc, p, wr, wi, br, bi, lam, mix)


def _lru_bwd(xc, p, y_off, wr, wi, br, bi, lam, hs, dmix, d_off, dp, name):
    T = xc.shape[0]
    G = LRU_WIDTH // 128
    tt = min(512, T)
    NT = T // tt

    def body(x_ref, y_ref, wr_ref, wi_ref, br_ref, bi_ref, l_ref, h_ref, hp_ref, do_ref, dp_in,
             dx_ref, dy_ref, dwr_ref, dwi_ref, dbr_ref, dbi_ref, dl_ref, lc, an):
        t = pl.program_id(1)
        first_tile = t == NT - 1

        @pl.when(t == 0)
        def _():
            lc[...] = jnp.zeros_like(lc)
            an[...] = jnp.zeros_like(an)
            dwr_ref[...] = jnp.zeros_like(dwr_ref)
            dwi_ref[...] = jnp.zeros_like(dwi_ref)
            dbr_ref[...] = jnp.zeros_like(dbr_ref)
            dbi_ref[...] = jnp.zeros_like(dbi_ref)
            dl_ref[...] = jnp.zeros_like(dl_ref)

        x = x_ref[...]
        y = y_ref[...]
        wr, wi, lam_ = wr_ref[...], wi_ref[...], l_ref[...]
        r, i, sp, a, mult_raw = _lru_gates(x, wr, wi, br_ref[...], bi_ref[...], lam_)
        row = lax.broadcasted_iota(jnp.int32, (tt, 128), 0)
        t0 = (row == 0) & first_tile
        mult = jnp.where(t0, 1.0, mult_raw)
        h = h_ref[...]
        do = do_ref[...]
        dh = do * _gelu(y)
        dy_ref[...] = (do * h * _dgelu(y)).astype(bf16)
        B = jnp.where(row == tt - 1, an[0:1, :], pltpu.roll(a, tt - 1, 0))
        L = dh
        d = 1
        while d < tt:
            keep = row < tt - d
            Lsh = jnp.where(keep, pltpu.roll(L, tt - d, 0), 0.0)
            Bsh = jnp.where(keep, pltpu.roll(B, tt - d, 0), 1.0)
            L = L + B * Lsh
            B = B * Bsh
            d *= 2
        L = L + B * lc[0:1, :]
        lc[...] = jnp.broadcast_to(L[0:1, :], lc.shape)
        an[...] = jnp.broadcast_to(a[0:1, :], an.shape)
        hprev = jnp.where(first_tile, 0.0, hp_ref[...])[PAD - 1:PAD, :]
        hm1 = jnp.where(row == 0, hprev, pltpu.roll(h, 1, 0))
        da = L * hm1
        dxc = L * i * mult
        di = L * x * mult
        dmult = jnp.where(t0, 0.0, L * x * i)
        da = da - jnp.where(t0, 0.0, dmult * a / mult_raw)
        dlog_a = da * a
        dr = dlog_a * (-LRU_C) * sp
        dsp = jnp.sum(dlog_a * (-LRU_C) * r, axis=0, keepdims=True)
        dpr = dr * r * (1.0 - r)
        dpi = di * i * (1.0 - i)
        dx_ref[...] = dxc + _dot(dpr, wr, _NT) + _dot(dpi, wi, _NT)
        for d_ref, dpre in ((dwr_ref, dpr), (dwi_ref, dpi)):
            dw = _dot(x, dpre, _TN)
            for s in range(2):
                d_ref[s] += dw[s * 64:(s + 1) * 64, s * 64:(s + 1) * 64]
        dbr_ref[...] += jnp.sum(dpr, axis=0, keepdims=True)
        dbi_ref[...] += jnp.sum(dpi, axis=0, keepdims=True)
        dl_ref[...] += dsp * (-_sigmoid(-lam_))

    rows8 = tt // PAD
    tile = pl.BlockSpec((tt, 128), lambda g, t: (NT - 1 - t, g))
    vec = pl.BlockSpec((1, 128), lambda g, t: (0, g))
    wsp = pl.BlockSpec((128, 128), lambda g, t: (g, g))
    wout = pl.BlockSpec((2, 64, 64), lambda g, t: (g, 0, 0))
    return pl.pallas_call(
        body, grid=(G, NT),
        in_specs=[tile, pl.BlockSpec((tt, 128), lambda g, t: (NT - 1 - t, y_off + g)), wsp, wsp, vec, vec, vec, tile,
                  pl.BlockSpec((PAD, 128), lambda g, t: (jnp.maximum((NT - 1 - t) * rows8 - 1, 0), g)),
                  pl.BlockSpec((tt, 128), lambda g, t: (NT - 1 - t, d_off + g)), pl.BlockSpec(memory_space=pl.ANY)],
        out_specs=[tile, pl.BlockSpec((tt, 128), lambda g, t: (NT - 1 - t, y_off + g)), wout, wout, vec, vec, vec],
        out_shape=[jax.ShapeDtypeStruct((T, LRU_WIDTH), f32), jax.ShapeDtypeStruct(dp.shape, dp.dtype),
                   jax.ShapeDtypeStruct((2 * G, 64, 64), f32), jax.ShapeDtypeStruct((2 * G, 64, 64), f32),
                   jax.ShapeDtypeStruct((1, LRU_WIDTH), f32), jax.ShapeDtypeStruct((1, LRU_WIDTH), f32),
                   jax.ShapeDtypeStruct((1, LRU_WIDTH), f32)],
        input_output_aliases={10: 1},
        scratch_shapes=[pltpu.VMEM((8, 128), f32), pltpu.VMEM((8, 128), f32)],
        compiler_params=_cparams(("parallel", "arbitrary")), name=name,
    )(xc, p, wr, wi, br, bi, lam, hs, hs, dmix, dp)


_NN3 = (((2,), (1,)), ((0,), (0,)))
_NT3 = (((2,), (2,)), ((0,), (0,)))
_TN3 = (((1,), (1,)), ((0,), (0,)))


def _pairs(ref, K):
    C = GDN_CHUNK
    return jnp.stack([ref[c * C:(c + 1) * C, h * HEAD_DIM:(h + 1) * HEAD_DIM] for c in range(K) for h in range(GDN_HEADS)])


def _put_pairs(ref, val, K, col=0):
    C, H = GDN_CHUNK, GDN_HEADS
    for c in range(K):
        for h in range(H):
            ref[c * C:(c + 1) * C, col + h * HEAD_DIM:col + (h + 1) * HEAD_DIM] = val[c * H + h].astype(ref.dtype)


def _rowsum(x):
    H, C, L = x.shape
    return _dot(x.reshape(H * C, L), jnp.ones((L, HEAD_DIM), f32), _NN).reshape(H, C, HEAD_DIM)


def _gdn_pre(qr, kr, v, ba, alog, dtb):
    C, H = GDN_CHUNK, GDN_HEADS
    B = qr.shape[0]
    K = B // H
    lane = lax.broadcasted_iota(jnp.int32, (C, 128), 1)
    lane3 = lax.broadcasted_iota(jnp.int32, (B, C, 128), 2)
    ri = lax.broadcasted_iota(jnp.int32, (C, C), 0)
    ci = lax.broadcasted_iota(jnp.int32, (C, C), 1)
    rowc = lax.broadcasted_iota(jnp.int32, (C, 1), 0)
    col = lambda m, j: jnp.sum(jnp.where(lane == j, m, 0.0), axis=1, keepdims=True)
    ea = jnp.exp(alog)
    tri = (ri >= ci).astype(f32)
    g_all, beta_cols, G_cols = [], [], []
    for c in range(K):
        ba_c = ba[c * C:(c + 1) * C]
        g_c = -ea * _softplus(ba_c + dtb)
        G_c = _dot01(tri, g_c, _NN)
        s_c = _sigmoid(ba_c)
        g_all.append(g_c)
        beta_cols += [col(s_c, h) for h in range(H)]
        G_cols += [col(G_c, H + h) for h in range(H)]
    wide = lambda c: jnp.broadcast_to(c, (B, C, 128))
    beta = wide(jnp.stack(beta_cols))
    Gc = jnp.stack(G_cols)
    rq = lax.rsqrt(_rowsum(qr * qr) + EPS)
    rk = lax.rsqrt(_rowsum(kr * kr) + EPS)
    qh, kn = qr * rq, kr * rk
    qn = qh * (HEAD_DIM ** -0.5)
    Grow = _dot01(jnp.ones((B, C, 128), f32), jnp.where(lane3 == 0, Gc, 0.0), _NT3)
    incl = ri >= ci
    Di = jnp.where(incl, jnp.exp(jnp.where(incl, Gc - Grow, 0.0)), 0.0)
    Ds = jnp.where(ri > ci, Di, 0.0)
    Gl = jnp.sum(jnp.where(rowc == C - 1, Gc, 0.0), axis=1, keepdims=True)
    eG = wide(jnp.exp(Gc))
    eGl = wide(jnp.exp(Gl - Gc))
    cd = jnp.exp(Gl)
    kb = kn * beta
    vb = v * beta
    Lm = _dot(kb, kn, _NT3) * Ds
    kbg = kb * eG
    QK = _dot(qn, kn, _NT3) * Di
    qg = qn * eG
    kg = kn * eGl
    return dict(beta=beta, g_all=g_all, rq=rq, rk=rk, qh=qh, kn=kn, qn=qn, Di=Di, Ds=Ds, eG=eG, eGl=eGl, cd=cd,
                kb=kb, vb=vb, Lm=Lm, kbg=kbg, QK=QK, qg=qg, kg=kg, lane=lane, ri=ri, ci=ci, rowc=rowc, ea=ea)


def _unit_lower_inverse(Lm):
    C = Lm.shape[-1]
    ri = lax.broadcasted_iota(jnp.int32, (C, C), 0)
    ci = lax.broadcasted_iota(jnp.int32, (C, C), 1)
    same = lambda s: (ri // s) == (ci // s)
    Xd = jnp.where(same(8), -Lm, 0.0)
    Tinv = (ri == ci).astype(f32) + Xd
    Pw = Xd
    for _ in range(2):
        Pw = _dot(Pw, Pw, _NN3)
        Tinv = Tinv + _dot(Tinv, Pw, _NN3)
    for s in (8, 16, 32):
        off = jnp.where(same(2 * s) & jnp.logical_not(same(s)), Lm, 0.0)
        Tinv = Tinv - _dot(_dot(Tinv, off, _NN3), Tinv, _NN3)
    return Tinv


def _gdn_specs(T, rev):
    C = GDN_CHUNK
    H = GDN_HEADS
    K = min(GDN_STEP, T // C)
    NS = T // (C * K)
    nn = (lambda n: NS - 1 - n) if rev else (lambda n: n)
    wide = lambda blk: pl.BlockSpec((K * C, H * HEAD_DIM), lambda n: (nn(n), blk))
    one = lambda off: pl.BlockSpec((K * C, HEAD_DIM), lambda n: (nn(n), off))
    vec = pl.BlockSpec((1, 128), lambda n: (0, 0))
    st = lambda rows: pl.BlockSpec((K, H, rows, rows), lambda n: (nn(n), 0, 0, 0))
    return K, NS, wide, one, vec, st


def _gdn_fwd(qkv, p, alog, dtb, nw, name):
    T = qkv.shape[0]
    C, H = GDN_CHUNK, GDN_HEADS
    N = T // C
    K, NS, wide, one, vec, st_spec = _gdn_specs(T, False)

    def body(q_ref, k_ref, v_ref, z_ref, ba_ref, al_ref, dt_ref, nw_ref, y_ref, sp_ref, ti_ref, vn_ref, o_ref, st):
        @pl.when(pl.program_id(0) == 0)
        def _():
            st[...] = jnp.zeros_like(st)

        f = _gdn_pre(_pairs(q_ref, K), _pairs(k_ref, K), _pairs(v_ref, K), ba_ref[...], al_ref[...], dt_ref[...])
        Tinv = _unit_lower_inverse(f["Lm"])
        ti_ref[...] = Tinv.reshape(K, H, C, C).astype(bf16)
        w = _dot(Tinv, f["kbg"], _NN3)
        u = _dot(Tinv, f["vb"], _NN3)
        S = st[...]
        vns, os_ = [], []
        for c in range(K):
            sl = slice(c * H, (c + 1) * H)
            sp_ref[c] = S
            vn_c = u[sl] - _dot(w[sl], S, _NN3)
            os_.append(_dot(f["qg"][sl], S, _NN3) + _dot(f["QK"][sl], vn_c, _NN3))
            S = S * f["cd"][sl] + _dot(f["kg"][sl], vn_c, _TN3)
            vns.append(vn_c)
        st[...] = S
        vn, o = jnp.concatenate(vns), jnp.concatenate(os_)
        r = lax.rsqrt(_rowsum(o * o) * (1.0 / HEAD_DIM) + EPS)
        _put_pairs(y_ref, o * r * nw_ref[...] * _silu(_pairs(z_ref, K)), K)
        _put_pairs(vn_ref, vn, K)
        _put_pairs(o_ref, o, K)

    wide_f32 = jax.ShapeDtypeStruct((T, H * HEAD_DIM), f32)
    return pl.pallas_call(
        body, grid=(NS,),
        in_specs=[wide(0), wide(1), wide(2), wide(3), one(4 * H), vec, vec, vec],
        out_specs=[wide(0), st_spec(HEAD_DIM), st_spec(C), wide(0), wide(0)],
        out_shape=[jax.ShapeDtypeStruct((T, H * HEAD_DIM), bf16), jax.ShapeDtypeStruct((N, H, HEAD_DIM, HEAD_DIM), f32),
                   jax.ShapeDtypeStruct((N, H, C, C), bf16), jax.ShapeDtypeStruct((T, H * HEAD_DIM), bf16), wide_f32],
        scratch_shapes=[pltpu.VMEM((H, HEAD_DIM, HEAD_DIM), f32)],
        compiler_params=_cparams(("arbitrary",)), name=name,
    )(qkv, qkv, qkv, p, p, alog, dtb, nw)


def _gdn_bwd(qkv, p, alog, dtb, nw, sprev, tinv, vn_all, o_all, dy_all, name):
    T = qkv.shape[0]
    C, H = GDN_CHUNK, GDN_HEADS
    N = T // C
    K, NS, wide, one, vec, st_spec = _gdn_specs(T, True)
    rs = lambda m: jnp.sum(m, axis=2, keepdims=True)

    def body(q_ref, k_ref, v_ref, z_ref, ba_ref, al_ref, dt_ref, nw_ref, sp_ref, ti_ref, vn_ref, o_ref, dy_ref,
             dqkv_ref, dz_ref, dba_ref, dal_ref, ddt_ref, dnw_ref, dst):
        @pl.when(pl.program_id(0) == 0)
        def _():
            dst[...] = jnp.zeros_like(dst)
            dal_ref[...] = jnp.zeros_like(dal_ref)
            ddt_ref[...] = jnp.zeros_like(ddt_ref)
            dnw_ref[...] = jnp.zeros_like(dnw_ref)

        ba, dtb_, nwv = ba_ref[...], dt_ref[...], nw_ref[...]
        v = _pairs(v_ref, K)
        f = _gdn_pre(_pairs(q_ref, K), _pairs(k_ref, K), v, ba, al_ref[...], dtb_)
        beta, kn, qn, kb, vb, kbg = f["beta"], f["kn"], f["qn"], f["kb"], f["vb"], f["kbg"]
        eG, eGl, cd, Di, Ds, QK, qg, kg = f["eG"], f["eGl"], f["cd"], f["Di"], f["Ds"], f["QK"], f["qg"], f["kg"]
        lane, ri, ci, rowc = f["lane"], f["ri"], f["ci"], f["rowc"]
        Tinv = ti_ref[...].reshape(K * H, C, C)
        S = sp_ref[...].reshape(K * H, HEAD_DIM, HEAD_DIM)
        w_ = _dot(Tinv, kbg, _NN3)
        vn, o = _pairs(vn_ref, K), _pairs(o_ref, K)
        z, dy = _pairs(z_ref, K), _pairs(dy_ref, K)
        r = lax.rsqrt(_rowsum(o * o) * (1.0 / HEAD_DIM) + EPS)
        nrm = o * r
        sz = _silu(z)
        dn = dy * nwv * sz
        _put_pairs(dz_ref, dy * nrm * nwv * _dsilu(z), K)
        dnw_ref[...] += jnp.sum(jnp.sum(dy * nrm * sz, axis=0), axis=0, keepdims=True)
        do = r * (dn - nrm * (_rowsum(dn * nrm) * (1.0 / HEAD_DIM)))
        dvn_do = _dot(QK, do, _TN3)
        dS_do = _dot(qg, do, _TN3)
        dqg = _dot(do, S, _NT3)
        dQK = _dot(do, vn, _NT3)
        dS = dst[...]
        dS1s, dvns = [None] * K, [None] * K
        for c in reversed(range(K)):
            sl = slice(c * H, (c + 1) * H)
            dS1s[c] = dS
            dvns[c] = _dot(kg[sl], dS, _NN3) + dvn_do[sl]
            dS = cd[sl] * dS + dS_do[sl] - _dot(w_[sl], dvns[c], _TN3)
        dst[...] = dS
        dS1, dvn = jnp.concatenate(dS1s), jnp.concatenate(dvns)
        dcd = jnp.sum(jnp.sum(S * dS1, axis=2, keepdims=True), axis=1, keepdims=True)
        dkg = _dot(vn, dS1, _NT3)
        dw = -_dot(dvn, S, _NT3)
        dqn = dqg * eG
        dkn = dkg * eGl
        deGl = rs(dkg * kn)
        dQKr = dQK * Di
        E = dQK * QK
        dqn = dqn + _dot(dQKr, kn, _NN3)
        dkn = dkn + _dot(dQKr, qn, _TN3)
        dT = _dot(dvn, vb, _NT3) + _dot(dw, kbg, _NT3)
        dvb = _dot(Tinv, dvn, _TN3)
        dkbg = _dot(Tinv, dw, _TN3)
        dkb = dkbg * eG
        deG = rs(dqg * qn + dkbg * kb)
        dL = -_dot(_dot(Tinv, dT, _TN3), Tinv, _NT3)
        dKK = dL * Ds
        E = E + dL * f["Lm"]
        dkb = dkb + _dot(dKK, kn, _NN3)
        dkn = dkn + _dot(dKK, kb, _TN3) + dkb * beta
        dbeta = rs(dkb * kn + dvb * v)
        _put_pairs(dqkv_ref, dvb * beta, K, 2 * H * HEAD_DIM)
        dG = rs(E) - rs(jnp.swapaxes(E, 1, 2)) + deG * eG - deGl * eGl
        dGl = jnp.sum(deGl * eGl, axis=1, keepdims=True) + dcd * cd
        dG = dG + jnp.where(rowc == C - 1, dGl, 0.0)
        qh = f["qh"]
        _put_pairs(dqkv_ref, (HEAD_DIM ** -0.5) * f["rq"] * (dqn - qh * _rowsum(dqn * qh)), K)
        _put_pairs(dqkv_ref, f["rk"] * (dkn - kn * _rowsum(dkn * kn)), K, H * HEAD_DIM)
        db = dbeta * beta * (1.0 - beta)
        triu = (ri <= ci).astype(f32)
        for c in range(K):
            db_all = jnp.where(lane == 0, db[c * H], 0.0)
            dG_all = jnp.where(lane == H, dG[c * H], 0.0)
            for h in range(1, H):
                db_all = db_all + jnp.where(lane == h, db[c * H + h], 0.0)
                dG_all = dG_all + jnp.where(lane == H + h, dG[c * H + h], 0.0)
            dg_all = _dot01(triu, dG_all, _NN)
            da_all = dg_all * (-f["ea"]) * _sigmoid(ba[c * C:(c + 1) * C] + dtb_)
            dba_ref[c * C:(c + 1) * C, :] = (db_all + da_all).astype(bf16)
            ddt_ref[...] += jnp.sum(da_all, axis=0, keepdims=True)
            dal_ref[...] += jnp.sum(dg_all * f["g_all"][c], axis=0, keepdims=True)

    small = jax.ShapeDtypeStruct((1, 128), f32)
    return pl.pallas_call(
        body, grid=(NS,),
        in_specs=[wide(0), wide(1), wide(2), wide(3), one(4 * H), vec, vec, vec, st_spec(HEAD_DIM), st_spec(C),
                  wide(0), wide(0), wide(0)],
        out_specs=[pl.BlockSpec((K * C, 3 * H * HEAD_DIM), lambda n: (NS - 1 - n, 0)), wide(3), one(0), vec, vec, vec],
        out_shape=[jax.ShapeDtypeStruct((T, 3 * H * HEAD_DIM), f32), jax.ShapeDtypeStruct((T, ODD_PAD), bf16),
                   jax.ShapeDtypeStruct((T, 128), bf16), small, small, small],
        scratch_shapes=[pltpu.VMEM((H, HEAD_DIM, HEAD_DIM), f32)],
        compiler_params=_cparams(("arbitrary",)), name=name,
    )(qkv, qkv, qkv, p, p, alog, dtb, nw, sprev, tinv, vn_all, o_all, dy_all)


def _lanes_from(x, s):
    return x if s % 128 == 0 else pltpu.roll(x, (128 - s) % 128, 1)


def _odd_assemble(g, name):
    R = g.shape[1]
    tr = min(256, R)
    n_blk = ODD_SHARD_PAD // 128

    def body(g_ref, o_ref):
        lane = lax.broadcasted_iota(jnp.int32, (tr, 128), 1)
        blk = lambda d, m: g_ref[d, :, m * 128:(m + 1) * 128]
        for gb in range(ODD_PAD // 128):
            c0 = 128 * gb
            if c0 >= ODD_IN:
                o_ref[:, c0:c0 + 128] = jnp.zeros((tr, 128), g.dtype)
                continue
            d0 = c0 // ODD_SHARD
            m0, sh = divmod(c0 - ODD_SHARD * d0, 128)
            take = min(128, ODD_SHARD * (d0 + 1) - c0)
            p = _lanes_from(blk(d0, m0), sh)
            if sh and m0 + 1 < n_blk:
                p = jnp.where(lane < 128 - sh, p, _lanes_from(blk(d0, m0 + 1), sh))
            if take < 128:
                nxt = pltpu.roll(blk(d0 + 1, 0), take, 1) if d0 + 1 < N_DEV else jnp.zeros((tr, 128), g.dtype)
                p = jnp.where(lane < take, p, nxt)
            o_ref[:, c0:c0 + 128] = p

    return pl.pallas_call(
        body, grid=(R // tr,),
        in_specs=[pl.BlockSpec((N_DEV, tr, ODD_SHARD_PAD), lambda i: (0, i, 0))],
        out_specs=pl.BlockSpec((tr, ODD_PAD), lambda i: (i, 0)),
        out_shape=jax.ShapeDtypeStruct((R, ODD_PAD), g.dtype),
        compiler_params=_cparams(("parallel",)), name=name,
    )(g)


def _odd_split(w, name):
    R = w.shape[0]
    tr = min(256, R)

    def body(w_ref, o_ref):
        lane = lax.broadcasted_iota(jnp.int32, (tr, 128), 1)
        blk = lambda gb: w_ref[:, gb * 128:(gb + 1) * 128]
        for d in range(N_DEV):
            for m in range(ODD_SHARD_PAD // 128):
                g0, sh = divmod(ODD_SHARD * d + 128 * m, 128)
                p = _lanes_from(blk(g0), sh)
                if sh and g0 + 1 < ODD_PAD // 128:
                    p = jnp.where(lane < 128 - sh, p, _lanes_from(blk(g0 + 1), sh))
                real = ODD_SHARD - 128 * m
                if real < 128:
                    p = jnp.where(lane < real, p, jnp.zeros_like(p))
                o_ref[d, :, m * 128:(m + 1) * 128] = p

    return pl.pallas_call(
        body, grid=(R // tr,),
        in_specs=[pl.BlockSpec((tr, ODD_PAD), lambda i: (i, 0))],
        out_specs=pl.BlockSpec((N_DEV, tr, ODD_SHARD_PAD), lambda i: (0, i, 0)),
        out_shape=jax.ShapeDtypeStruct((N_DEV, R, ODD_SHARD_PAD), w.dtype),
        compiler_params=_cparams(("parallel",)), name=name,
    )(w)


def _adam_tile(g, w_ref, m_ref, v_ref, go_ref, d_ref, mo_ref, vo_ref):
    c1 = 1.0 - ADAM_B1 ** ADAM_STEP
    c2 = 1.0 - ADAM_B2 ** ADAM_STEP
    mn = ADAM_B1 * m_ref[...] + (1.0 - ADAM_B1) * g
    vn = ADAM_B2 * v_ref[...] + (1.0 - ADAM_B2) * (g * g)
    go_ref[...] = g
    mo_ref[...] = mn
    vo_ref[...] = vn
    d_ref[...] = -ADAM_LR * ((mn / c1) / (jnp.sqrt(vn / c2) + ADAM_EPS) + ADAM_WD * w_ref[...])


def _adamw(w, gs, m, v, name, layer=None, prev=None):
    R, Cc = w.shape[-2:]
    S = gs.shape[0]
    tr = R
    if S * R * Cc * 4 > (4 << 20):
        for cand in (256, 128, 64, 32, 16, 8):
            if R % cand == 0 and R > cand:
                tr = cand
                break

    def body(w_ref, g_ref, m_ref, v_ref, *rest):
        g = g_ref[0].astype(f32)
        for s in range(1, S):
            g = g + g_ref[s].astype(f32)
        _adam_tile(g, w_ref, m_ref, v_ref, *rest[-4:])

    if layer is None:
        blk = pl.BlockSpec((tr, Cc), lambda i: (i, 0))
    else:
        blk = pl.BlockSpec((None, tr, Cc), lambda i: (layer, i, 0))
    out = jax.ShapeDtypeStruct(w.shape, f32)
    carried = [] if prev is None else list(prev)
    return pl.pallas_call(
        body, grid=(R // tr,),
        in_specs=[blk, pl.BlockSpec((S, tr, Cc), lambda i: (0, i, 0)), blk, blk]
        + [pl.BlockSpec(memory_space=pl.ANY)] * len(carried),
        out_specs=[blk] * 4, out_shape=[out] * 4,
        input_output_aliases={4 + j: j for j in range(len(carried))},
        compiler_params=_cparams(("parallel",)), name=name,
    )(w, gs, m, v, *carried)


def _adamw_column_major(w, gs, m, v, name):
    _, R, Cc = w.shape
    S = gs.shape[0]
    q = R // 128
    dense = lambda a: jnp.transpose(a, (2, 0, 1)).reshape(Cc * q, 128)
    back = lambda a: jnp.transpose(a.reshape(Cc, q, 128), (1, 2, 0)).reshape(1, R, Cc)

    def body(w_ref, g_ref, m_ref, v_ref, go_ref, d_ref, mo_ref, vo_ref, gt):
        for i in range(q):
            g = g_ref[0, i * 128:(i + 1) * 128, :].astype(f32)
            for s in range(1, S):
                g = g + g_ref[s, i * 128:(i + 1) * 128, :].astype(f32)
            gt[pl.ds(i, 128, stride=q), :] = g.T
        _adam_tile(gt[...], w_ref, m_ref, v_ref, go_ref, d_ref, mo_ref, vo_ref)

    blk = pl.BlockSpec((128 * q, 128), lambda j: (j, 0))
    outs = pl.pallas_call(
        body, grid=(pl.cdiv(Cc, 128),),
        in_specs=[blk, pl.BlockSpec((S, R, 128), lambda j: (0, 0, j)), blk, blk],
        out_specs=[blk] * 4, out_shape=[jax.ShapeDtypeStruct((Cc * q, 128), f32)] * 4,
        scratch_shapes=[pltpu.VMEM((128 * q, 128), f32)],
        compiler_params=_cparams(("parallel",)), name=name,
    )(dense(w), gs, dense(m), dense(v))
    return [back(o) for o in outs]


def _me():
    x, y, c = lax.axis_index("x"), lax.axis_index("y"), lax.axis_index("c")
    return x, y, c, 4 * x + 2 * y + c


def _peer(k):
    x, y, c, _ = _me()
    px = 1 - x if k & 4 else x
    py = 1 - y if k & 2 else y
    pc = 1 - c if k & 1 else c
    return (px, py, pc), 4 * px + 2 * py + pc


_HBM = pl.BlockSpec(memory_space=pltpu.HBM)
_SEM = pl.BlockSpec(memory_space=pltpu.SEMAPHORE)
_EFFECT = pltpu.SideEffectType.DATAFLOW_SIDE_EFFECTING


def _copy(src, land, ssem, rsem, k, blocked, landing_slot_of_peer):
    pid, pidx = _peer(k)
    slot = pidx if landing_slot_of_peer else _me()[3]
    return pltpu.make_async_remote_copy(src_ref=src.at[pidx] if blocked else src, dst_ref=land.at[slot],
                                        send_sem=ssem.at[k - 1], recv_sem=rsem.at[k - 1], device_id=pid, device_id_type=MESH)


def _own_copy(src, land, rsem, blocked):
    me = _me()[3]
    return pltpu.make_async_copy(src.at[me] if blocked else src, land.at[me], rsem.at[N_DEV - 1])


_ALL_PEERS = tuple(range(1, N_DEV))
_SAME_CORE_PEERS = (1, 2, 4, 6)
_OTHER_CORE_PEERS = (3, 5, 7)


def _relay(lands, name):
    n = len(lands)

    def body(*refs):
        land, ssem, rsem = refs[:n], refs[-2], refs[-1]
        sibling, _ = _peer(1)
        copies = []
        for i in range(n):
            for j, k in enumerate(_OTHER_CORE_PEERS):
                sems = dict(send_sem=ssem.at[3 * i + j], recv_sem=rsem.at[3 * i + j], device_id=sibling, device_id_type=MESH)
                _, outgoing = _peer(k - 1)
                _, incoming = _peer(k)
                send = pltpu.make_async_remote_copy(src_ref=land[i].at[outgoing], dst_ref=land[i].at[outgoing], **sems)
                recv = pltpu.make_async_remote_copy(src_ref=land[i].at[incoming], dst_ref=land[i].at[incoming], **sems)
                send.start()
                copies.append((send, recv))
        for send, recv in copies:
            send.wait_send()
            recv.wait_recv()

    res = pl.pallas_call(
        body, name=name, out_shape=tuple(pltpu.HBM(a.shape, a.dtype) for a in lands),
        in_specs=(_HBM,) * n, out_specs=(_HBM,) * n, input_output_aliases={i: i for i in range(n)},
        scratch_shapes=[pltpu.SemaphoreType.DMA((3 * n,)), pltpu.SemaphoreType.DMA((3 * n,))],
        compiler_params=pltpu.CompilerParams(has_side_effects=_EFFECT),
    )(*lands)
    return list(res)


def _send_start(srcs, blocked, name, relayed=()):
    n = len(srcs)
    lands = [lax.empty(a.shape if blocked else (N_DEV,) + a.shape, a.dtype) for a in srcs]

    def body(*refs):
        src, land, sems, token = refs[:n], refs[n:2 * n], refs[2 * n:4 * n], refs[-1]
        for i in range(n):
            for k in (_SAME_CORE_PEERS if i in relayed else _ALL_PEERS):
                _copy(src[i], land[i], sems[2 * i], sems[2 * i + 1], k, blocked, False).start()
        for i in range(n):
            _own_copy(src[i], land[i], sems[2 * i + 1], blocked).start()
        token[...] = jnp.zeros_like(token)

    sems = (pltpu.SemaphoreType.DMA((N_DEV - 1,)), pltpu.SemaphoreType.DMA((N_DEV,)))
    hbm = lambda a: pltpu.with_memory_space_constraint(a, pltpu.HBM)
    res = pl.pallas_call(
        body, name=name,
        out_shape=sems * n + tuple(pltpu.HBM(a.shape, a.dtype) for a in srcs + lands)
        + (jax.ShapeDtypeStruct((8, 128), f32),),
        in_specs=(_HBM,) * (2 * n),
        out_specs=(_SEM,) * (2 * n) + (_HBM,) * (2 * n) + (pl.BlockSpec(memory_space=pltpu.VMEM),),
        input_output_aliases={j: 2 * n + j for j in range(2 * n)},
        compiler_params=pltpu.CompilerParams(has_side_effects=_EFFECT),
    )(*[hbm(a) for a in srcs], *[hbm(a) for a in lands])
    handles = [(res[2 * i], res[2 * i + 1], res[2 * n + i], res[3 * n + i]) for i in range(n)]
    return handles, res[-1]


def _send_wait(handle, blocked, after, name, relayed=False):
    ssem, rsem, src, land = handle
    after = tuple(after) if isinstance(after, (tuple, list)) else (after,)

    def body(src_ref, land_ref, ssem_ref, rsem_ref, *rest):
        for k in (_SAME_CORE_PEERS if relayed else _ALL_PEERS):
            cp = _copy(src_ref, land_ref, ssem_ref, rsem_ref, k, blocked, True)
            cp.wait_send()
            cp.wait_recv()
        _own_copy(src_ref, land_ref, rsem_ref, blocked).wait()

    return pl.pallas_call(
        body, name=name, out_shape=(pltpu.HBM(src.shape, src.dtype), pltpu.HBM(land.shape, land.dtype)),
        in_specs=(_HBM, _HBM, _SEM, _SEM) + (pl.BlockSpec(memory_space=pl.ANY),) * len(after), out_specs=(_HBM, _HBM),
        input_output_aliases={0: 0, 1: 1}, compiler_params=pltpu.CompilerParams(has_side_effects=_EFFECT),
    )(src, land, ssem, rsem, *after)


def _block_diag(w):
    nb, bs = w.shape[0], w.shape[1]
    eye = jnp.eye(nb, dtype=w.dtype)
    return (eye[:, None, :, None] * w[:, :, None, :]).reshape(nb * bs, nb * bs)


_SQUARE_TILES = dict(tm=1024, tn=1024, tk=1024)


def _mlp_fwd(x, hm, wu, wd, tag, epilogue, extras, outs):
    (r,) = _matmul(hm, wu, "nn", outs=[bf16], epilogue=lambda acc: (jnp.maximum(acc, 0.0),), name=f"mlp_up_{tag}")
    res = _matmul(r, wd, "nn", outs=outs, extras=(x,) + tuple(extras), epilogue=epilogue, a_map=jnp.square,
                  name=f"mlp_down_{tag}", **_SQUARE_TILES)
    return res, (hm, r)


def _mlp_bwd(x, nw, wu, wd, saved, dxo, dxo_b, tag, sink):
    hm, r = saved
    (du,) = _matmul(dxo_b, wd, "nt", outs=[bf16], extras=(r,), epilogue=lambda acc, rr: (acc * (2.0 * rr.astype(f32)),),
                    name=f"mlp_dact_{tag}")
    (dwd,) = _matmul(r, dxo_b, "tn", outs=[bf16], a_map=jnp.square, name=f"mlp_dwd_{tag}", **_SQUARE_TILES)
    tok = sink({f"w_down{tag}": dwd.reshape(N_DEV, D_FF // N_DEV, D_MODEL)})
    (dwu,) = _matmul(hm, du, "tn", outs=[bf16], shard_cols=2, extras=(jnp.broadcast_to(tok[0:1, 0:1], (1, D_FF)),),
                     epilogue=lambda acc, zero: (acc + zero,), name=f"mlp_dwu_{tag}")
    tok = sink({f"w_up{tag}": dwu})
    return _matmul(du, wu, "nt", outs=_RMS_BWD_OUTS, extras=(x, dxo, nw + tok[0:1, 0:1]), epilogue=_rms_bwd_ep,
                   name=f"mlp_dh_{tag}", **_SQUARE_TILES)


def _local_step(x, tgt, P, weight, sink):
    T = x.shape[0]
    cos, sin = _rope_tables(T)
    rtab = _ret_tables()
    row = lambda a: a.reshape(1, -1)
    mix_nw, mlp_nw = P["mixer_norm_w"], P["mlp_norm_w"]
    wr_bd, wi_bd = _block_diag(P["lru_w_r"]), _block_diag(P["lru_w_i"])
    lru_b, lru_br, lru_bi, lru_lam = row(P["lru_conv_b"]), row(P["lru_b_r"]), row(P["lru_b_i"]), row(P["lru_lambda"])
    pad16 = lambda a: jnp.pad(a.reshape(1, GDN_HEADS), ((0, 0), (GDN_HEADS, 128 - 2 * GDN_HEADS)))
    alog, dtb = pad16(P["gdn_a_log"]), pad16(P["gdn_dt_bias"])
    gnw = row(P["gdn_norm_w"])

    x0 = x
    h0 = _rms_fwd(x0, mix_nw[0:1], "rms_mix_0")
    w_ie = weight("w_in_even", (h0, cos, sin, wr_bd, wi_bd))
    (pe,) = _matmul(h0, w_ie, "nn", outs=[f32], name="in_even")
    mix0, o_ret, s_ret = _ret_fwd(pe, cos, sin, rtab, "ret_fwd")
    w_lc = weight("lru_conv_w", pe)
    xc = _conv_fwd(pe, 4, w_lc, lru_b, False, "lru_conv_fwd")
    mix0, h_lru = _lru_fwd(xc, pe, 20, wr_bd, wi_bd, lru_br, lru_bi, lru_lam, mix0, "lru_fwd")
    w_oe = weight("w_out_even", mix0)
    x1, hm0 = _matmul(mix0, w_oe, "nn", outs=[f32, bf16], extras=(x0, mlp_nw[0:1]), epilogue=_residual_rms_ep,
                      name="out_even", tm=1024, tn=D_MODEL)
    w_u0, w_d0 = weight("w_up0", x1), weight("w_down0", x1)
    (x2, h1), mlp0 = _mlp_fwd(x1, hm0, w_u0, w_d0, "0", _residual_rms_ep, (mix_nw[1:2],), [f32, bf16])
    w_io = weight("w_in_odd", h1)
    (po,) = _matmul(h1, w_io, "nn", outs=[f32], tm=2048, tn=ODD_PAD // 3, name="in_odd")
    w_gc = weight("gdn_conv_w", po)
    qkv = _conv_fwd(po, 0, w_gc, None, True, "gdn_conv_fwd")
    y_gdn, s_gdn, ti_gdn, vn_gdn, o_gdn = _gdn_fwd(qkv, po, alog, dtb, gnw, "gdn_fwd")
    w_oo = weight("w_out_odd", y_gdn)
    x3, hm1 = _matmul(y_gdn, w_oo, "nn", outs=[f32, bf16], extras=(x2, mlp_nw[1:2]), epilogue=_residual_rms_ep,
                      name="out_odd", tm=1024, tn=D_MODEL)
    w_u1, w_d1 = weight("w_up1", x3), weight("w_down1", x3)
    (loss, dx4, dx4_b, d_final), mlp1 = _mlp_fwd(x3, hm1, w_u1, w_d1, "1", _loss_ep, (row(P["final_norm_w"]), tgt),
                                                 _LOSS_OUTS)
    dx3, dx3_b, d_mlp_nw1 = _mlp_bwd(x3, mlp_nw[1:2], w_u1, w_d1, mlp1, dx4, dx4_b, "1", sink)
    (dy_gdn,) = _matmul(dx3_b, w_oo, "nt", outs=[f32], name="out_odd_dx")
    (d_woo,) = _matmul(y_gdn, dx3_b, "tn", outs=[bf16], name="out_odd_dw")
    dqkv, dpo, dba, d_alog, d_dtb, d_gnw = _gdn_bwd(qkv, po, alog, dtb, gnw, s_gdn, ti_gdn, vn_gdn, o_gdn, dy_gdn,
                                                  "gdn_bwd")
    dpo, d_gconv, _ = _conv_bwd(po, 0, w_gc, None, True, dqkv, dpo, "gdn_conv_bwd")
    dpo = lax.dynamic_update_slice(dpo, dba, (0, 4 * D_MODEL))
    (d_wio,) = _matmul(h1, dpo, "tn", outs=[bf16], tn=ODD_PAD // 3, name="in_odd_dw")
    tok = sink(dict(w_out_odd=d_woo.reshape(N_DEV, D_MODEL // N_DEV, D_MODEL), w_in_odd=_odd_split(d_wio, "w_in_odd_split")))
    dx2, dx2_b, d_mix_nw1 = _matmul(dpo, w_io, "nt", outs=_RMS_BWD_OUTS, extras=(x2, dx3, mix_nw[1:2] + tok[0:1, 0:1]),
                                    epilogue=_rms_bwd_ep, tm=1024, tn=1024, tk=ODD_PAD // 3, name="in_odd_dx")
    dx1, dx1_b, d_mlp_nw0 = _mlp_bwd(x1, mlp_nw[0:1], w_u0, w_d0, mlp0, dx2, dx2_b, "0", sink)
    (d_woe,) = _matmul(mix0, dx1_b, "tn", outs=[bf16], name="out_even_dw")
    tok = sink(dict(w_out_even=d_woe.reshape(N_DEV, D_MODEL // N_DEV, D_MODEL)))
    (dmix0,) = _matmul(dx1_b, w_oe, "nt", outs=[f32], name="out_even_dx")
    dpe = _ret_bwd(pe, cos, sin, rtab, o_ret, s_ret, dmix0, "ret_bwd")
    dxc, dpe, d_wr, d_wi, d_br, d_bi, d_lam = _lru_bwd(xc, pe, 20, wr_bd, wi_bd, lru_br, lru_bi, lru_lam + tok[0:1, 0:1],
                                                       h_lru, dmix0, 4, dpe, "lru_bwd")
    dpe, d_lconv, d_lconv_b = _conv_bwd(pe, 4, w_lc, lru_b, False, dxc, dpe, "lru_conv_bwd")
    G = dict(
        mlp_norm_w=jnp.concatenate([d_mlp_nw0, d_mlp_nw1], axis=0),
        final_norm_w=d_final.reshape(-1),
        lru_conv_w=d_lconv, lru_conv_b=d_lconv_b.reshape(-1),
        lru_w_r=d_wr, lru_b_r=d_br.reshape(-1), lru_w_i=d_wi, lru_b_i=d_bi.reshape(-1),
        lru_lambda=d_lam.reshape(-1), gdn_conv_w=d_gconv,
        gdn_a_log=d_alog[0, GDN_HEADS:2 * GDN_HEADS], gdn_dt_bias=d_dtb[0, GDN_HEADS:2 * GDN_HEADS],
        gdn_norm_w=d_gnw.reshape(-1),
    )
    packed = _pack([G[k] for k in _SMALL] + [d_lconv, d_gconv, loss[0, 0:1]])
    tok = sink(dict(small=jnp.broadcast_to(packed[None], (N_DEV,) + packed.shape)))
    (d_wie,) = _matmul(h0, dpe, "tn", outs=[bf16], shard_cols=2, extras=(jnp.broadcast_to(tok[0:1, 0:1], (1, 3 * D_MODEL)),),
                       epilogue=lambda acc, zero: (acc + zero,), name="in_even_dw")
    tok = sink(dict(w_in_even=d_wie))
    dx0, _, d_mix_nw0 = _matmul(dpe, w_ie, "nt", outs=_RMS_BWD_OUTS, extras=(x0, dx1, mix_nw[0:1] + tok[0:1, 0:1]),
                                epilogue=_rms_bwd_ep, name="in_even_dx", **_SQUARE_TILES)
    G["mixer_norm_w"] = jnp.concatenate([d_mix_nw0, d_mix_nw1], axis=0)
    return loss, dx0, G


_SMALL = ["mlp_norm_w", "final_norm_w", "lru_conv_b", "lru_w_r", "lru_b_r", "lru_w_i", "lru_b_i",
          "lru_lambda", "gdn_a_log", "gdn_dt_bias", "gdn_norm_w"]
_PACK_ROWS = 688


def _pack(parts):
    flat = jnp.concatenate([p.reshape(-1) for p in parts])
    return jnp.pad(flat, (0, _PACK_ROWS * 128 - flat.shape[0])).reshape(_PACK_ROWS, 128)


def _unpack(packed, shapes):
    flat = packed.reshape(-1)
    out, off = [], 0
    for s in shapes:
        n = int(np.prod(s))
        out.append(flat[off:off + n].reshape(s))
        off += n
    return out


def kernel(x, mixer_norm_w, mlp_norm_w, final_norm_w, w_in_even, lru_conv_w, lru_conv_b, lru_w_r, lru_b_r, lru_w_i, lru_b_i, lru_lambda, w_out_even, w_in_odd, gdn_conv_w, gdn_a_log, gdn_dt_bias, gdn_norm_w, w_out_odd, w_up, w_down, loss_target, m_mixer_norm_w, m_mlp_norm_w, m_final_norm_w, m_w_in_even, m_lru_conv_w, m_lru_conv_b, m_lru_w_r, m_lru_b_r, m_lru_w_i, m_lru_b_i, m_lru_lambda, m_w_out_even, m_w_in_odd, m_gdn_conv_w, m_gdn_a_log, m_gdn_dt_bias, m_gdn_norm_w, m_w_out_odd, m_w_up, m_w_down, v_mixer_norm_w, v_mlp_norm_w, v_final_norm_w, v_w_in_even, v_lru_conv_w, v_lru_conv_b, v_lru_w_r, v_lru_b_r, v_lru_w_i, v_lru_b_i, v_lru_lambda, v_w_out_even, v_w_in_odd, v_gdn_conv_w, v_gdn_a_log, v_gdn_dt_bias, v_gdn_norm_w, v_w_out_odd, v_w_up, v_w_down):
    Pw = dict(mixer_norm_w=mixer_norm_w, mlp_norm_w=mlp_norm_w, final_norm_w=final_norm_w, w_in_even=w_in_even,
              lru_conv_w=lru_conv_w, lru_conv_b=lru_conv_b, lru_w_r=lru_w_r, lru_b_r=lru_b_r, lru_w_i=lru_w_i,
              lru_b_i=lru_b_i, lru_lambda=lru_lambda, w_out_even=w_out_even, w_in_odd=w_in_odd, gdn_conv_w=gdn_conv_w,
              gdn_a_log=gdn_a_log, gdn_dt_bias=gdn_dt_bias, gdn_norm_w=gdn_norm_w, w_out_odd=w_out_odd, w_up=w_up,
              w_down=w_down)
    Pm = dict(mixer_norm_w=m_mixer_norm_w, mlp_norm_w=m_mlp_norm_w, final_norm_w=m_final_norm_w, w_in_even=m_w_in_even,
              lru_conv_w=m_lru_conv_w, lru_conv_b=m_lru_conv_b, lru_w_r=m_lru_w_r, lru_b_r=m_lru_b_r, lru_w_i=m_lru_w_i,
              lru_b_i=m_lru_b_i, lru_lambda=m_lru_lambda, w_out_even=m_w_out_even, w_in_odd=m_w_in_odd,
              gdn_conv_w=m_gdn_conv_w, gdn_a_log=m_gdn_a_log, gdn_dt_bias=m_gdn_dt_bias, gdn_norm_w=m_gdn_norm_w,
              w_out_odd=m_w_out_odd, w_up=m_w_up, w_down=m_w_down)
    Pv = dict(mixer_norm_w=v_mixer_norm_w, mlp_norm_w=v_mlp_norm_w, final_norm_w=v_final_norm_w, w_in_even=v_w_in_even,
              lru_conv_w=v_lru_conv_w, lru_conv_b=v_lru_conv_b, lru_w_r=v_lru_w_r, lru_b_r=v_lru_b_r, lru_w_i=v_lru_w_i,
              lru_b_i=v_lru_b_i, lru_lambda=v_lru_lambda, w_out_even=v_w_out_even, w_in_odd=v_w_in_odd,
              gdn_conv_w=v_gdn_conv_w, gdn_a_log=v_gdn_a_log, gdn_dt_bias=v_gdn_dt_bias, gdn_norm_w=v_gdn_norm_w,
              w_out_odd=v_w_out_odd, w_up=v_w_up, w_down=v_w_down)
    me = _me()[3]
    T = x.shape[1]

    cols = lambda g: jnp.transpose(g, (1, 0, 2)).reshape(g.shape[1], -1)
    rows = lambda g: g.reshape(-1, g.shape[2])
    wide = lambda g: _odd_assemble(g, "w_in_odd_assemble")
    odd_shard = jnp.pad(w_in_odd[0].astype(bf16), ((0, 0), (0, ODD_SHARD_PAD - ODD_SHARD)))
    as_is = lambda g: g
    gather = dict(
        w_in_even=(w_in_even[0].astype(bf16), cols), lru_conv_w=(lru_conv_w[0], cols),
        w_out_even=(w_out_even[0].astype(bf16), rows), w_up0=(w_up[0].astype(bf16), as_is), w_down0=(w_down[0].astype(bf16), rows),
        w_in_odd=(odd_shard, wide), gdn_conv_w=(gdn_conv_w[0], cols),
        w_out_odd=(w_out_odd[0].astype(bf16), rows), w_up1=(w_up[1].astype(bf16), as_is), w_down1=(w_down[1].astype(bf16), rows))
    relay_groups = (("w_in_even",), ("w_out_even", "w_up0", "w_down0"), ("w_in_odd",))
    relayed = sum(relay_groups, ())
    handles, tok = _send_start([s for s, _ in gather.values()], False, "gather_start",
                               relayed=[i for i, name in enumerate(gather) if name in relayed])
    handles = dict(zip(gather, handles))
    landed, full = {}, {}

    def weight(name, after):
        if name not in landed:
            group = next((g for g in relay_groups if name in g), (name,))
            lands = [_send_wait(handles[n], False, after, f"gather_wait_{n}", relayed=n in relayed)[1] for n in group]
            landed.update(zip(group, _relay(lands, "relay_" + "_".join(group)) if name in relayed else lands))
        if name not in full:
            full[name] = gather[name][1](landed[name])
        return full[name]

    P = {k: Pw[k] for k in ("mlp_norm_w", "final_norm_w")}
    P["mixer_norm_w"] = mixer_norm_w + tok[0:1, 0:1]
    for k in ("lru_w_r", "lru_w_i", "lru_conv_b", "lru_b_r", "lru_b_i", "lru_lambda", "gdn_a_log", "gdn_dt_bias", "gdn_norm_w"):
        P[k] = Pw[k][0]

    sent = {}

    def sink(grads):
        hs, token = _send_start(list(grads.values()), True, "grads_start_" + "_".join(grads))
        sent.update(zip(grads, hs))
        return token

    loss, dx, G = _local_step(x[0], loss_target[0], P, weight, sink)
    lanes = lambda a: a.reshape(-1, 128)
    sink(dict(mixer_norm_w=jnp.broadcast_to(lanes(G["mixer_norm_w"])[None], (N_DEV, 2 * D_MODEL // 128, 128))))

    def received(name, after=dx):
        return _send_wait(sent[name], True, after, f"grads_wait_{name}")[1]

    out = {}
    nff = D_FF // N_DEV

    def whole(name, gs):
        out[name] = tuple(_adamw(Pw[name], gs, Pm[name], Pv[name], f"adamw_{name}", layer=0))

    def layers(name):
        res = None
        for l in range(2):
            res = _adamw(Pw[name], received(f"{name}{l}"), Pm[name], Pv[name], f"adamw_{name}{l}", layer=l, prev=res)
        out[name] = tuple(res)

    layers("w_up")
    layers("w_down")
    whole("w_out_odd", received("w_out_odd"))
    out["w_in_odd"] = tuple(_adamw_column_major(w_in_odd, received("w_in_odd"), m_w_in_odd, v_w_in_odd, "adamw_w_in_odd"))
    whole("w_out_even", received("w_out_even"))
    small_shapes = [Pw[k].shape for k in _SMALL]
    pw, pm, pv = (_pack([Q[k] for k in _SMALL]) for Q in (Pw, Pm, Pv))
    sg, sd, sm, sv = _adamw(pw, received("small", out["w_out_even"][1]), pm, pv, "adamw_small")
    for arrs_i, packed_out in enumerate((sg, sd, sm, sv)):
        for k, a in zip(_SMALL, _unpack(packed_out, small_shapes)):
            out.setdefault(k, [None] * 4)[arrs_i] = a
    whole("w_in_even", received("w_in_even", sd))
    out["mixer_norm_w"] = tuple(
        a.reshape(mixer_norm_w.shape) for a in
        _adamw(lanes(mixer_norm_w), received("mixer_norm_w", out["w_in_even"][1]), lanes(m_mixer_norm_w),
               lanes(v_mixer_norm_w), "adamw_mixer_norm_w"))
    n_small = sum(int(np.prod(s)) for s in small_shapes)
    gflat = sg.reshape(-1)
    g_lconv = gflat[n_small:n_small + CONV_K * LRU_WIDTH].reshape(CONV_K, LRU_WIDTH)
    g_gconv = gflat[n_small + CONV_K * LRU_WIDTH:n_small + CONV_K * (LRU_WIDTH + 3072)].reshape(CONV_K, 3072)
    whole("lru_conv_w", lax.dynamic_slice_in_dim(g_lconv, me * 64, 64, axis=1)[None])
    whole("gdn_conv_w", lax.dynamic_slice_in_dim(g_gconv, me * 384, 384, axis=1)[None])

    names = ["mixer_norm_w", "mlp_norm_w", "final_norm_w", "w_in_even", "lru_conv_w", "lru_conv_b", "lru_w_r", "lru_b_r",
             "lru_w_i", "lru_b_i", "lru_lambda", "w_out_even", "w_in_odd", "gdn_conv_w", "gdn_a_log", "gdn_dt_bias",
             "gdn_norm_w", "w_out_odd", "w_up", "w_down"]
    total = gflat[n_small + CONV_K * (LRU_WIDTH + 3072)]
    res = [total, dx[None]]
    for j in range(4):
        res += [out[k][j] for k in names]
    return tuple(res)
```

```python
import math

import numpy as np
import jax
import jax.numpy as jnp
from jax import lax
from jax.experimental import pallas as pl
from jax.experimental.pallas import tpu as pltpu

f32 = jnp.float32
bf16 = jnp.bfloat16

N_DEV = 8
D_MODEL = 1024
D_FF = 4096
EPS = 1e-6
RET_HEADS = 4
RET_CHUNK = 128
RET_STEP = 4
ROPE_THETA = 10000.0
LRU_WIDTH = 512
LRU_C = 8.0
GDN_HEADS = 8
GDN_CHUNK = 64
GDN_STEP = 4
HEAD_DIM = 128
ODD_IN = 4112
ODD_PAD = 4224
ODD_SHARD = ODD_IN // N_DEV
ODD_SHARD_PAD = 640
ADAM_LR, ADAM_B1, ADAM_B2, ADAM_EPS, ADAM_WD, ADAM_STEP = 0.001, 0.9, 0.999, 1e-08, 0.01, 10
VMEM_LIMIT = 56 * 1024 * 1024

_NN = (((1,), (0,)), ((), ()))
_NT = (((1,), (1,)), ((), ()))
_TN = (((0,), (0,)), ((), ()))
MESH = pl.DeviceIdType.MESH


def _cparams(sem):
    return pltpu.CompilerParams(dimension_semantics=sem, vmem_limit_bytes=VMEM_LIMIT)


def _dot(a, b, dn):
    return lax.dot_general(a.astype(bf16), b.astype(bf16), dn, preferred_element_type=f32)


def _dot01(a01, b, dn):
    a = a01.astype(bf16)
    b0 = b.astype(bf16)
    r1 = b - b0.astype(f32)
    b1 = r1.astype(bf16)
    b2 = (r1 - b1.astype(f32)).astype(bf16)
    d = lambda q: lax.dot_general(a, q, dn, preferred_element_type=f32)
    return d(b0) + (d(b1) + d(b2))


def _sigmoid(x):
    return jax.nn.sigmoid(x)


def _silu(x):
    return x * _sigmoid(x)


def _dsilu(x):
    s = _sigmoid(x)
    return s * (1.0 + x * (1.0 - s))


def _softplus(x):
    return jnp.maximum(x, 0.0) + jnp.log1p(jnp.exp(-jnp.abs(x)))


_GELU_C = math.sqrt(2.0 / math.pi)


def _gelu(y):
    return 0.5 * y * (1.0 + jnp.tanh(_GELU_C * (y + 0.044715 * y * y * y)))


def _dgelu(y):
    t = jnp.tanh(_GELU_C * (y + 0.044715 * y * y * y))
    return 0.5 * (1.0 + t) + 0.5 * y * (1.0 - t * t) * _GELU_C * (1.0 + 3.0 * 0.044715 * y * y)


def _matmul(a, b, form, *, outs, name, epilogue=None, extras=(), tm=4096, tn=512, tk=1024, shard_cols=0, a_map=None):
    if form == "tn":
        K, M = a.shape
    else:
        M, K = a.shape
    per_step = 1
    if b.ndim == 3:
        assert form in ("nn", "nt"), name
        N = b.shape[1] if form == "nt" else N_DEV * b.shape[2]
        if form == "nn":
            tn = b.shape[2]
        else:
            per_step = max(1, tk // b.shape[2])
            tk = per_step * b.shape[2]
    else:
        N = b.shape[0] if form == "nt" else b.shape[1]
    ns = N // N_DEV
    if shard_cols:
        tn = ns * shard_cols
    tm, tn, tk = min(tm, M), min(tn, N), min(tk, K)
    assert M % tm == 0 and N % tn == 0 and K % tk == 0, (name, M, N, K, tm, tn, tk)
    nk = K // tk
    dn = {"nn": _NN, "nt": _NT, "tn": _TN}[form]
    if form == "tn":
        a_spec = pl.BlockSpec((tk, tm), lambda i, j, k: (k, i))
    else:
        a_spec = pl.BlockSpec((tm, tk), lambda i, j, k: (i, k))
    if b.ndim == 3:
        b_spec = (pl.BlockSpec((per_step, tn, tk // per_step), lambda i, j, k: (k, j, 0)) if form == "nt"
                  else pl.BlockSpec((None, tk, tn), lambda i, j, k: (j, k, 0)))
    elif form == "nt":
        b_spec = pl.BlockSpec((tn, tk), lambda i, j, k: (j, k))
    else:
        b_spec = pl.BlockSpec((tk, tn), lambda i, j, k: (k, j))
    e_spec = pl.BlockSpec((tm, tn), lambda i, j, k: (i, j))
    v_spec = pl.BlockSpec((1, tn), lambda i, j, k: (0, j))
    if shard_cols:
        o_spec = pl.BlockSpec((shard_cols, tm, ns), lambda i, j, k: (j, i, 0))
        o_shape = (N_DEV, M, ns)
    else:
        o_spec = e_spec
        o_shape = (M, N)
    n_ex = len(extras)
    sums = [isinstance(o, tuple) for o in outs]
    assert not any(sums) or tn == N, name

    def finish(acc, ex, o_refs, row_tile):
        vals = (acc,) if epilogue is None else epilogue(acc, *[e[...] for e in ex])
        for r, v, is_sum in zip(o_refs, vals, sums):
            if is_sum:
                @pl.when(row_tile == 0)
                def _(r=r, v=v):
                    r[...] = v.astype(r.dtype)

                @pl.when(row_tile > 0)
                def _(r=r, v=v):
                    r[...] += v.astype(r.dtype)
            elif shard_cols:
                for s in range(shard_cols):
                    r[s] = v[:, s * ns:(s + 1) * ns].astype(r.dtype)
            else:
                r[...] = v.astype(r.dtype)

    def prod(a_ref, b_ref):
        if b.ndim == 3 and form == "nt":
            w = tk // per_step
            return sum(_dot(a_ref[:, s * w:(s + 1) * w], b_ref[s], dn) for s in range(1, per_step)) + _dot(a_ref[:, 0:w], b_ref[0], dn)
        av = a_ref[...]
        return _dot(av if a_map is None else a_map(av), b_ref[...], dn)

    def body_one(*refs):
        finish(prod(*refs[:2]), refs[2:2 + n_ex], refs[2 + n_ex:], pl.program_id(0))

    def body_acc(*refs):
        a_ref, b_ref = refs[:2]
        acc = refs[-1]
        k = pl.program_id(2)
        row_tile = pl.program_id(0)

        @pl.when(k == 0)
        def _():
            acc[...] = prod(a_ref, b_ref)

        @pl.when((k > 0) & (k < nk - 1))
        def _():
            acc[...] += prod(a_ref, b_ref)

        @pl.when(k == nk - 1)
        def _():
            finish(acc[...] + prod(a_ref, b_ref), refs[2:2 + n_ex], refs[2 + n_ex:-1], row_tile)

    return pl.pallas_call(
        body_one if nk == 1 else body_acc, grid=(M // tm, N // tn, nk),
        in_specs=[a_spec, b_spec] + [v_spec if e.shape[0] == 1 else e_spec for e in extras],
        out_specs=[v_spec if s else o_spec for s in sums],
        out_shape=[jax.ShapeDtypeStruct((1, N), o[1]) if s else jax.ShapeDtypeStruct(o_shape, o) for o, s in zip(outs, sums)],
        scratch_shapes=[] if nk == 1 else [pltpu.VMEM((tm, tn), f32)],
        compiler_params=_cparams(("arbitrary" if any(sums) else "parallel", "parallel", "arbitrary")), name=name,
    )(a, b, *extras)


def _rms_fwd(x, w, name):
    T, D = x.shape
    tt = min(512, T)

    def body(x_ref, w_ref, h_ref):
        xv = x_ref[...]
        r = lax.rsqrt(jnp.mean(xv * xv, axis=1, keepdims=True) + EPS)
        h_ref[...] = (xv * r * w_ref[...]).astype(bf16)

    return pl.pallas_call(
        body, grid=(T // tt,),
        in_specs=[pl.BlockSpec((tt, D), lambda i: (i, 0)), pl.BlockSpec((1, D), lambda i: (0, 0))],
        out_specs=pl.BlockSpec((tt, D), lambda i: (i, 0)),
        out_shape=jax.ShapeDtypeStruct((T, D), bf16),
        compiler_params=_cparams(("parallel",)), name=name,
    )(x, w)


def _residual_rms_ep(acc, res, w):
    x = res + acc
    r = lax.rsqrt(jnp.mean(x * x, axis=1, keepdims=True) + EPS)
    return x, x * r * w


_RMS_BWD_OUTS = [f32, bf16, ("sum", f32)]


def _rms_bwd_ep(dh, x, dres, w):
    r = lax.rsqrt(jnp.mean(x * x, axis=1, keepdims=True) + EPS)
    xn = x * r
    dhw = dh * w
    dx = dres + r * (dhw - xn * jnp.mean(dhw * xn, axis=1, keepdims=True))
    return dx, dx, jnp.sum(dh * xn, axis=0, keepdims=True)


_LOSS_OUTS = [("sum", f32), f32, bf16, ("sum", f32)]


def _loss_ep(acc, res, w, tgt):
    x = res + acc
    D = x.shape[1]
    r = lax.rsqrt(jnp.mean(x * x, axis=1, keepdims=True) + EPS)
    xn = x * r
    e = xn * w - tgt
    loss = 0.5 * jnp.sum(jnp.mean(e * e, axis=1, keepdims=True), axis=0, keepdims=True)
    dy = e * (1.0 / D)
    dyw = dy * w
    dx = r * (dyw - xn * jnp.mean(dyw * xn, axis=1, keepdims=True))
    return jnp.broadcast_to(loss, (1, D)), dx, dx, jnp.sum(dy * xn, axis=0, keepdims=True)


def _ret_tables():
    H, C = RET_HEADS, RET_CHUNK
    lg = np.log1p(-np.exp2(-5.0 - np.arange(H, dtype=np.float32))).astype(np.float32)
    idx = np.arange(C, dtype=np.float32)
    diff = idx[:, None] - idx[None, :]
    causal = diff >= 0
    dm = np.where(causal[None], np.exp(lg[:, None, None] * np.where(causal, diff, 0.0)[None]), 0.0)
    qd = np.exp(lg[:, None] * (idx[None, :] + 1.0))
    kd = np.exp(lg[:, None] * (C - 1.0 - idx[None, :]))
    cg = np.exp(lg * C)
    tab = np.zeros((H, 4, C, HEAD_DIM), np.float32)
    tab[:, 0] = dm
    tab[:, 1] = qd[:, :, None]
    tab[:, 2] = kd[:, :, None]
    tab[:, 3] = cg[:, None, None]
    return jnp.asarray(tab)


def _rope_tables(T):
    half = HEAD_DIM // 2
    inv = ROPE_THETA ** (-jnp.arange(half, dtype=f32) / half)
    ang = jnp.arange(T, dtype=jnp.int32).astype(f32)[:, None] * inv[None, :]
    c, s = jnp.cos(ang), jnp.sin(ang)
    return jnp.concatenate([c, c], axis=1), jnp.concatenate([-s, s], axis=1)


def _rope(x, cos, sin):
    return x * cos + pltpu.roll(x, HEAD_DIM // 2, 1) * sin


def _unrope(y, cos, sin):
    return y * cos + pltpu.roll(y * sin, HEAD_DIM // 2, 1)


def _stack_heads(ref, H, f=None):
    parts = [ref[:, h * HEAD_DIM:(h + 1) * HEAD_DIM] for h in range(H)]
    return jnp.stack(parts if f is None else [f(a) for a in parts])


def _ret_fwd(p, cos, sin, tab, name):
    T = p.shape[0]
    C, H = RET_CHUNK, RET_HEADS
    N = T // C
    K = min(RET_STEP, N)
    NS = N // K
    scale = HEAD_DIM ** -0.5

    def body(q_ref, k_ref, v_ref, g_ref, c_ref, s_ref, t_ref, y_ref, o_ref, sp_ref, st):
        @pl.when(pl.program_id(0) == 0)
        def _():
            st[...] = jnp.zeros_like(st)

        dm, qd, kd, cg = t_ref[:, 0], t_ref[:, 1], t_ref[:, 2], t_ref[:, 3]
        S = st[...]
        for c in range(K):
            rows = pl.ds(c * C, C)
            cos_, sin_ = c_ref[rows, :], s_ref[rows, :]
            rot = lambda a: _rope(a, cos_, sin_)
            q = _stack_heads(q_ref.at[rows, :], H, rot)
            k = _stack_heads(k_ref.at[rows, :], H, rot) * scale
            v = _stack_heads(v_ref.at[rows, :], H)
            P = _dot(q, k, _NT3) * dm
            o = _dot(P, v, _NN3) + _dot(q * qd, S, _NN3)
            sp_ref[c] = S
            S = cg * S + _dot(k * kd, v, _TN3)
            r = lax.rsqrt(jnp.mean(o * o, axis=2, keepdims=True) + EPS)
            y = o * r * _silu(_stack_heads(g_ref.at[rows, :], H))
            for h in range(H):
                o_ref[rows, h * HEAD_DIM:(h + 1) * HEAD_DIM] = o[h]
                y_ref[rows, h * HEAD_DIM:(h + 1) * HEAD_DIM] = y[h].astype(bf16)
        st[...] = S

    wide = lambda blk: pl.BlockSpec((K * C, H * HEAD_DIM), lambda n: (n, blk))
    tbl = pl.BlockSpec((K * C, HEAD_DIM), lambda n: (n, 0))
    return pl.pallas_call(
        body, grid=(NS,),
        in_specs=[wide(0), wide(1), wide(2), wide(3), tbl, tbl,
                  pl.BlockSpec((H, 4, C, HEAD_DIM), lambda n: (0, 0, 0, 0))],
        out_specs=[wide(0), wide(0), pl.BlockSpec((K, H, HEAD_DIM, HEAD_DIM), lambda n: (n, 0, 0, 0))],
        out_shape=[jax.ShapeDtypeStruct((T, D_MODEL), bf16), jax.ShapeDtypeStruct((T, H * HEAD_DIM), f32),
                   jax.ShapeDtypeStruct((N, H, HEAD_DIM, HEAD_DIM), f32)],
        scratch_shapes=[pltpu.VMEM((H, HEAD_DIM, HEAD_DIM), f32)],
        compiler_params=_cparams(("arbitrary",)), name=name,
    )(p, p, p, p, cos, sin, tab)


def _ret_bwd(p, cos, sin, tab, o_raw, sprev, dmix, name):
    T = p.shape[0]
    C, H = RET_CHUNK, RET_HEADS
    N = T // C
    K = min(RET_STEP, N)
    NS = N // K
    scale = HEAD_DIM ** -0.5
    W = H * HEAD_DIM

    def body(q_ref, k_ref, v_ref, g_ref, c_ref, s_ref, t_ref, o_ref, sp_ref, dy_ref, d_ref, dst):
        @pl.when(pl.program_id(0) == 0)
        def _():
            dst[...] = jnp.zeros_like(dst)

        dm, qd, kd, cg = t_ref[:, 0], t_ref[:, 1], t_ref[:, 2], t_ref[:, 3]
        dS1 = dst[...]
        for c in reversed(range(K)):
            rows = pl.ds(c * C, C)
            cos_, sin_ = c_ref[rows, :], s_ref[rows, :]
            rot = lambda a: _rope(a, cos_, sin_)
            q = _stack_heads(q_ref.at[rows, :], H, rot)
            k = _stack_heads(k_ref.at[rows, :], H, rot) * scale
            v = _stack_heads(v_ref.at[rows, :], H)
            g = _stack_heads(g_ref.at[rows, :], H)
            S = sp_ref[c]
            o = _stack_heads(o_ref.at[rows, :], H)
            dy = _stack_heads(dy_ref.at[rows, :], H)
            r = lax.rsqrt(jnp.mean(o * o, axis=2, keepdims=True) + EPS)
            nrm = o * r
            dn = dy * _silu(g)
            dg = dy * nrm * _dsilu(g)
            do = r * (dn - nrm * jnp.mean(dn * nrm, axis=2, keepdims=True))
            P = _dot(q, k, _NT3) * dm
            dP = _dot(do, v, _NT3) * dm
            dq = _dot(dP, k, _NN3) + _dot(do, S, _NT3) * qd
            dk = (_dot(dP, q, _TN3) + _dot(v, dS1, _NT3) * kd) * scale
            dv = _dot(P, do, _TN3) + _dot(k * kd, dS1, _NN3)
            dS1 = cg * dS1 + _dot(q * qd, do, _TN3)
            for h in range(H):
                d_ref[rows, h * HEAD_DIM:(h + 1) * HEAD_DIM] = _unrope(dq[h], cos_, sin_).astype(bf16)
                d_ref[rows, W + h * HEAD_DIM:W + (h + 1) * HEAD_DIM] = _unrope(dk[h], cos_, sin_).astype(bf16)
                d_ref[rows, 2 * W + h * HEAD_DIM:2 * W + (h + 1) * HEAD_DIM] = dv[h].astype(bf16)
                d_ref[rows, 3 * W + h * HEAD_DIM:3 * W + (h + 1) * HEAD_DIM] = dg[h].astype(bf16)
        dst[...] = dS1

    rev = lambda blk: pl.BlockSpec((K * C, W), lambda n: (NS - 1 - n, blk))
    tbl = pl.BlockSpec((K * C, HEAD_DIM), lambda n: (NS - 1 - n, 0))
    return pl.pallas_call(
        body, grid=(NS,),
        in_specs=[rev(0), rev(1), rev(2), rev(3), tbl, tbl,
                  pl.BlockSpec((H, 4, C, HEAD_DIM), lambda n: (0, 0, 0, 0)), rev(0),
                  pl.BlockSpec((K, H, HEAD_DIM, HEAD_DIM), lambda n: (NS - 1 - n, 0, 0, 0)), rev(0)],
        out_specs=pl.BlockSpec((K * C, 4 * W), lambda n: (NS - 1 - n, 0)),
        out_shape=jax.ShapeDtypeStruct((T, 6 * W), bf16),
        scratch_shapes=[pltpu.VMEM((H, HEAD_DIM, HEAD_DIM), f32)],
        compiler_params=_cparams(("arbitrary",)), name=name,
    )(p, p, p, p, cos, sin, tab, o_raw, sprev, dmix)


CONV_K = 4
CONV_W = 512
PAD = 8
SUB_R = 64


def _conv_fwd(x, col_off, w, b, act, name):
    T = x.shape[0]
    C = w.shape[1]
    G = C // CONV_W
    tt = min(512, T)
    NT = T // tt
    has_b = b is not None

    def body(*refs):
        if has_b:
            x_ref, w_ref, b_ref, y_ref, pad = refs
        else:
            x_ref, w_ref, y_ref, pad = refs
        t = pl.program_id(1)

        @pl.when(t == 0)
        def _():
            pad[pl.ds(0, PAD), :] = jnp.zeros((PAD, CONV_W), f32)

        pad[pl.ds(PAD, tt), :] = x_ref[...]
        for g in range(CONV_W // 128):
            ls = slice(g * 128, (g + 1) * 128)
            wv = w_ref[:, ls]
            for c in range(tt // SUB_R):
                r0 = c * SUB_R
                y = wv[0:1, :] * pad[pl.ds(PAD - 3 + r0, SUB_R), ls]
                for kk in range(1, CONV_K):
                    y = y + wv[kk:kk + 1, :] * pad[pl.ds(PAD - 3 + kk + r0, SUB_R), ls]
                if has_b:
                    y = y + b_ref[:, ls]
                y_ref[pl.ds(r0, SUB_R), ls] = _silu(y) if act else y
        tail = pad[pl.ds(tt, PAD), :]
        pad[pl.ds(0, PAD), :] = tail

    in_specs = [pl.BlockSpec((tt, CONV_W), lambda g, t: (t, col_off + g)),
                pl.BlockSpec((CONV_K, CONV_W), lambda g, t: (0, g))]
    args = [x, w]
    if has_b:
        in_specs.append(pl.BlockSpec((1, CONV_W), lambda g, t: (0, g)))
        args.append(b)
    return pl.pallas_call(
        body, grid=(G, NT), in_specs=in_specs,
        out_specs=pl.BlockSpec((tt, CONV_W), lambda g, t: (t, g)),
        out_shape=jax.ShapeDtypeStruct((T, C), f32),
        scratch_shapes=[pltpu.VMEM((tt + PAD, CONV_W), f32)],
        compiler_params=_cparams(("parallel", "arbitrary")), name=name,
    )(*args)


def _conv_bwd(x, col_off, w, b, act, dout, dp, name):
    T = x.shape[0]
    C = w.shape[1]
    G = C // CONV_W
    tt = min(512, T)
    NT = T // tt
    has_b = b is not None

    def body(*refs):
        if has_b:
            x_ref, xp_ref, w_ref, b_ref, d_ref, dp_in, dx_ref, dw_ref, db_ref, pad, dpad = refs
        else:
            x_ref, xp_ref, w_ref, d_ref, dp_in, dx_ref, dw_ref, db_ref, pad, dpad = refs
        t = pl.program_id(1)
        first_tile = t == NT - 1

        @pl.when(t == 0)
        def _():
            dpad[pl.ds(tt, PAD), :] = jnp.zeros((PAD, CONV_W), f32)
            dw_ref[...] = jnp.zeros_like(dw_ref)
            db_ref[...] = jnp.zeros_like(db_ref)

        pad[pl.ds(0, PAD), :] = jnp.where(first_tile, 0.0, xp_ref[...])
        pad[pl.ds(PAD, tt), :] = x_ref[...]
        fold = lambda v: v.reshape(SUB_R // 8, 8, 128).sum(axis=0)
        for g in range(CONV_W // 128):
            ls = slice(g * 128, (g + 1) * 128)
            wv = w_ref[:, ls]
            acc = [jnp.zeros((8, 128), f32) for _ in range(CONV_K + 1)]
            for c in reversed(range(tt // SUB_R)):
                r0 = c * SUB_R
                xs = [pad[pl.ds(PAD - 3 + kk + r0, SUB_R), ls] for kk in range(CONV_K)]
                dy = d_ref[pl.ds(r0, SUB_R), ls]
                if act:
                    y = wv[0:1, :] * xs[0]
                    for kk in range(1, CONV_K):
                        y = y + wv[kk:kk + 1, :] * xs[kk]
                    if has_b:
                        y = y + b_ref[:, ls]
                    dy = dy * _dsilu(y)
                dpad[pl.ds(r0, SUB_R), ls] = dy
                dx = wv[3:4, :] * dy
                for j in range(1, CONV_K):
                    dx = dx + wv[3 - j:4 - j, :] * dpad[pl.ds(r0 + j, SUB_R), ls]
                dx_ref[pl.ds(r0, SUB_R), ls] = dx.astype(bf16)
                for kk in range(CONV_K):
                    acc[kk] = acc[kk] + fold(dy * xs[kk])
                acc[CONV_K] = acc[CONV_K] + fold(dy)
            for kk in range(CONV_K):
                dw_ref[kk:kk + 1, ls] += jnp.sum(acc[kk], axis=0, keepdims=True)
            db_ref[:, ls] += jnp.sum(acc[CONV_K], axis=0, keepdims=True)
        head = dpad[pl.ds(0, PAD), :]
        dpad[pl.ds(tt, PAD), :] = head

    rows8 = tt // PAD
    in_specs = [pl.BlockSpec((tt, CONV_W), lambda g, t: (NT - 1 - t, col_off + g)),
                pl.BlockSpec((PAD, CONV_W), lambda g, t: (jnp.maximum((NT - 1 - t) * rows8 - 1, 0), col_off + g)),
                pl.BlockSpec((CONV_K, CONV_W), lambda g, t: (0, g))]
    args = [x, x, w]
    if has_b:
        in_specs.append(pl.BlockSpec((1, CONV_W), lambda g, t: (0, g)))
        args.append(b)
    in_specs += [pl.BlockSpec((tt, CONV_W), lambda g, t: (NT - 1 - t, g)), pl.BlockSpec(memory_space=pl.ANY)]
    args += [dout, dp]
    return pl.pallas_call(
        body, grid=(G, NT), in_specs=in_specs,
        out_specs=[pl.BlockSpec((tt, CONV_W), lambda g, t: (NT - 1 - t, col_off + g)),
                   pl.BlockSpec((CONV_K, CONV_W), lambda g, t: (0, g)),
                   pl.BlockSpec((1, CONV_W), lambda g, t: (0, g))],
        out_shape=[jax.ShapeDtypeStruct(dp.shape, dp.dtype), jax.ShapeDtypeStruct((CONV_K, C), f32),
                   jax.ShapeDtypeStruct((1, C), f32)],
        input_output_aliases={len(args) - 1: 0},
        scratch_shapes=[pltpu.VMEM((tt + PAD, CONV_W), f32), pltpu.VMEM((tt + PAD, CONV_W), f32)],
        compiler_params=_cparams(("parallel", "arbitrary")), name=name,
    )(*args)


def _lru_gates(xc, wr, wi, br, bi, lam):
    r = _sigmoid(_dot(xc, wr, _NN) + br)
    i = _sigmoid(_dot(xc, wi, _NN) + bi)
    sp = _softplus(-lam)
    a = jnp.exp(-LRU_C * r * sp)
    mult = jnp.sqrt(1.0 - a * a)
    return r, i, sp, a, mult


def _lru_fwd(xc, p, y_off, wr, wi, br, bi, lam, mix, name):
    T = xc.shape[0]
    G = LRU_WIDTH // 128
    tt = min(512, T)
    NT = T // tt

    def body(x_ref, y_ref, wr_ref, wi_ref, br_ref, bi_ref, l_ref, mix_in, o_ref, h_ref, hc):
        t = pl.program_id(1)

        @pl.when(t == 0)
        def _():
            hc[...] = jnp.zeros_like(hc)

        x = x_ref[...]
        r, i, sp, a, mult = _lru_gates(x, wr_ref[...], wi_ref[...], br_ref[...], bi_ref[...], l_ref[...])
        row = lax.broadcasted_iota(jnp.int32, (tt, 128), 0)
        mult = jnp.where((row == 0) & (t == 0), 1.0, mult)
        U = x * i * mult
        A = a
        d = 1
        while d < tt:
            keep = row >= d
            Ush = jnp.where(keep, pltpu.roll(U, d, 0), 0.0)
            Ash = jnp.where(keep, pltpu.roll(A, d, 0), 1.0)
            U = A * Ush + U
            A = A * Ash
            d *= 2
        h = U + A * hc[0:1, :]
        h_ref[...] = h
        hc[...] = jnp.broadcast_to(h[tt - 1:tt, :], hc.shape)
        o_ref[...] = (h * _gelu(y_ref[...])).astype(bf16)

    tile = pl.BlockSpec((tt, 128), lambda g, t: (t, g))
    vec = pl.BlockSpec((1, 128), lambda g, t: (0, g))
    wsp = pl.BlockSpec((128, 128), lambda g, t: (g, g))
    return pl.pallas_call(
        body, grid=(G, NT),
        in_specs=[tile, pl.BlockSpec((tt, 128), lambda g, t: (t, y_off + g)), wsp, wsp, vec, vec, vec,
                  pl.BlockSpec(memory_space=pl.ANY)],
        out_specs=[pl.BlockSpec((tt, 128), lambda g, t: (t, G + g)), tile],
        out_shape=[jax.ShapeDtypeStruct(mix.shape, mix.dtype), jax.ShapeDtypeStruct((T, LRU_WIDTH), f32)],
        input_output_aliases={7: 0},
        scratch_shapes=[pltpu.VMEM((8, 128), f32)],
        compiler_params=_cparams(("parallel", "arbitrary")), name=name,
    )(xc, p, wr, wi, br, bi, lam, mix)


def _lru_bwd(xc, p, y_off, wr, wi, br, bi, lam, hs, dmix, d_off, dp, name):
    T = xc.shape[0]
    G = LRU_WIDTH // 128
    tt = min(512, T)
    NT = T // tt

    def body(x_ref, y_ref, wr_ref, wi_ref, br_ref, bi_ref, l_ref, h_ref, hp_ref, do_ref, dp_in,
             dx_ref, dy_ref, dwr_ref, dwi_ref, dbr_ref, dbi_ref, dl_ref, lc, an):
        t = pl.program_id(1)
        first_tile = t == NT - 1

        @pl.when(t == 0)
        def _():
            lc[...] = jnp.zeros_like(lc)
            an[...] = jnp.zeros_like(an)
            dwr_ref[...] = jnp.zeros_like(dwr_ref)
            dwi_ref[...] = jnp.zeros_like(dwi_ref)
            dbr_ref[...] = jnp.zeros_like(dbr_ref)
            dbi_ref[...] = jnp.zeros_like(dbi_ref)
            dl_ref[...] = jnp.zeros_like(dl_ref)

        x = x_ref[...]
        y = y_ref[...]
        wr, wi, lam_ = wr_ref[...], wi_ref[...], l_ref[...]
        r, i, sp, a, mult_raw = _lru_gates(x, wr, wi, br_ref[...], bi_ref[...], lam_)
        row = lax.broadcasted_iota(jnp.int32, (tt, 128), 0)
        t0 = (row == 0) & first_tile
        mult = jnp.where(t0, 1.0, mult_raw)
        h = h_ref[...]
        do = do_ref[...]
        dh = do * _gelu(y)
        dy_ref[...] = (do * h * _dgelu(y)).astype(bf16)
        B = jnp.where(row == tt - 1, an[0:1, :], pltpu.roll(a, tt - 1, 0))
        L = dh
        d = 1
        while d < tt:
            keep = row < tt - d
            Lsh = jnp.where(keep, pltpu.roll(L, tt - d, 0), 0.0)
            Bsh = jnp.where(keep, pltpu.roll(B, tt - d, 0), 1.0)
            L = L + B * Lsh
            B = B * Bsh
            d *= 2
        L = L + B * lc[0:1, :]
        lc[...] = jnp.broadcast_to(L[0:1, :], lc.shape)
        an[...] = jnp.broadcast_to(a[0:1, :], an.shape)
        hprev = jnp.where(first_tile, 0.0, hp_ref[...])[PAD - 1:PAD, :]
        hm1 = jnp.where(row == 0, hprev, pltpu.roll(h, 1, 0))
        da = L * hm1
        dxc = L * i * mult
        di = L * x * mult
        dmult = jnp.where(t0, 0.0, L * x * i)
        da = da - jnp.where(t0, 0.0, dmult * a / mult_raw)
        dlog_a = da * a
        dr = dlog_a * (-LRU_C) * sp
        dsp = jnp.sum(dlog_a * (-LRU_C) * r, axis=0, keepdims=True)
        dpr = dr * r * (1.0 - r)
        dpi = di * i * (1.0 - i)
        dx_ref[...] = dxc + _dot(dpr, wr, _NT) + _dot(dpi, wi, _NT)
        for d_ref, dpre in ((dwr_ref, dpr), (dwi_ref, dpi)):
            dw = _dot(x, dpre, _TN)
            for s in range(2):
                d_ref[s] += dw[s * 64:(s + 1) * 64, s * 64:(s + 1) * 64]
        dbr_ref[...] += jnp.sum(dpr, axis=0, keepdims=True)
        dbi_ref[...] += jnp.sum(dpi, axis=0, keepdims=True)
        dl_ref[...] += dsp * (-_sigmoid(-lam_))

    rows8 = tt // PAD
    tile = pl.BlockSpec((tt, 128), lambda g, t: (NT - 1 - t, g))
    vec = pl.BlockSpec((1, 128), lambda g, t: (0, g))
    wsp = pl.BlockSpec((128, 128), lambda g, t: (g, g))
    wout = pl.BlockSpec((2, 64, 64), lambda g, t: (g, 0, 0))
    return pl.pallas_call(
        body, grid=(G, NT),
        in_specs=[tile, pl.BlockSpec((tt, 128), lambda g, t: (NT - 1 - t, y_off + g)), wsp, wsp, vec, vec, vec, tile,
                  pl.BlockSpec((PAD, 128), lambda g, t: (jnp.maximum((NT - 1 - t) * rows8 - 1, 0), g)),
                  pl.BlockSpec((tt, 128), lambda g, t: (NT - 1 - t, d_off + g)), pl.BlockSpec(memory_space=pl.ANY)],
        out_specs=[tile, pl.BlockSpec((tt, 128), lambda g, t: (NT - 1 - t, y_off + g)), wout, wout, vec, vec, vec],
        out_shape=[jax.ShapeDtypeStruct((T, LRU_WIDTH), f32), jax.ShapeDtypeStruct(dp.shape, dp.dtype),
                   jax.ShapeDtypeStruct((2 * G, 64, 64), f32), jax.ShapeDtypeStruct((2 * G, 64, 64), f32),
                   jax.ShapeDtypeStruct((1, LRU_WIDTH), f32), jax.ShapeDtypeStruct((1, LRU_WIDTH), f32),
                   jax.ShapeDtypeStruct((1, LRU_WIDTH), f32)],
        input_output_aliases={10: 1},
        scratch_shapes=[pltpu.VMEM((8, 128), f32), pltpu.VMEM((8, 128), f32)],
        compiler_params=_cparams(("parallel", "arbitrary")), name=name,
    )(xc, p, wr, wi, br, bi, lam, hs, hs, dmix, dp)


_NN3 = (((2,), (1,)), ((0,), (0,)))
_NT3 = (((2,), (2,)), ((0,), (0,)))
_TN3 = (((1,), (1,)), ((0,), (0,)))


def _pairs(ref, K):
    C = GDN_CHUNK
    return jnp.stack([ref[c * C:(c + 1) * C, h * HEAD_DIM:(h + 1) * HEAD_DIM] for c in range(K) for h in range(GDN_HEADS)])


def _put_pairs(ref, val, K, col=0):
    C, H = GDN_CHUNK, GDN_HEADS
    for c in range(K):
        for h in range(H):
            ref[c * C:(c + 1) * C, col + h * HEAD_DIM:col + (h + 1) * HEAD_DIM] = val[c * H + h].astype(ref.dtype)


def _rowsum(x):
    H, C, L = x.shape
    return _dot(x.reshape(H * C, L), jnp.ones((L, HEAD_DIM), f32), _NN).reshape(H, C, HEAD_DIM)


def _gdn_pre(qr, kr, v, ba, alog, dtb):
    C, H = GDN_CHUNK, GDN_HEADS
    B = qr.shape[0]
    K = B // H
    lane = lax.broadcasted_iota(jnp.int32, (C, 128), 1)
    lane3 = lax.broadcasted_iota(jnp.int32, (B, C, 128), 2)
    ri = lax.broadcasted_iota(jnp.int32, (C, C), 0)
    ci = lax.broadcasted_iota(jnp.int32, (C, C), 1)
    rowc = lax.broadcasted_iota(jnp.int32, (C, 1), 0)
    col = lambda m, j: jnp.sum(jnp.where(lane == j, m, 0.0), axis=1, keepdims=True)
    ea = jnp.exp(alog)
    tri = (ri >= ci).astype(f32)
    g_all, beta_cols, G_cols = [], [], []
    for c in range(K):
        ba_c = ba[c * C:(c + 1) * C]
        g_c = -ea * _softplus(ba_c + dtb)
        G_c = _dot01(tri, g_c, _NN)
        s_c = _sigmoid(ba_c)
        g_all.append(g_c)
        beta_cols += [col(s_c, h) for h in range(H)]
        G_cols += [col(G_c, H + h) for h in range(H)]
    wide = lambda c: jnp.broadcast_to(c, (B, C, 128))
    beta = wide(jnp.stack(beta_cols))
    Gc = jnp.stack(G_cols)
    rq = lax.rsqrt(_rowsum(qr * qr) + EPS)
    rk = lax.rsqrt(_rowsum(kr * kr) + EPS)
    qh, kn = qr * rq, kr * rk
    qn = qh * (HEAD_DIM ** -0.5)
    Grow = _dot01(jnp.ones((B, C, 128), f32), jnp.where(lane3 == 0, Gc, 0.0), _NT3)
    incl = ri >= ci
    Di = jnp.where(incl, jnp.exp(jnp.where(incl, Gc - Grow, 0.0)), 0.0)
    Ds = jnp.where(ri > ci, Di, 0.0)
    Gl = jnp.sum(jnp.where(rowc == C - 1, Gc, 0.0), axis=1, keepdims=True)
    eG = wide(jnp.exp(Gc))
    eGl = wide(jnp.exp(Gl - Gc))
    cd = jnp.exp(Gl)
    kb = kn * beta
    vb = v * beta
    Lm = _dot(kb, kn, _NT3) * Ds
    kbg = kb * eG
    QK = _dot(qn, kn, _NT3) * Di
    qg = qn * eG
    kg = kn * eGl
    return dict(beta=beta, g_all=g_all, rq=rq, rk=rk, qh=qh, kn=kn, qn=qn, Di=Di, Ds=Ds, eG=eG, eGl=eGl, cd=cd,
                kb=kb, vb=vb, Lm=Lm, kbg=kbg, QK=QK, qg=qg, kg=kg, lane=lane, ri=ri, ci=ci, rowc=rowc, ea=ea)


def _unit_lower_inverse(Lm):
    C = Lm.shape[-1]
    ri = lax.broadcasted_iota(jnp.int32, (C, C), 0)
    ci = lax.broadcasted_iota(jnp.int32, (C, C), 1)
    same = lambda s: (ri // s) == (ci // s)
    Xd = jnp.where(same(8), -Lm, 0.0)
    Tinv = (ri == ci).astype(f32) + Xd
    Pw = Xd
    for _ in range(2):
        Pw = _dot(Pw, Pw, _NN3)
        Tinv = Tinv + _dot(Tinv, Pw, _NN3)
    for s in (8, 16, 32):
        off = jnp.where(same(2 * s) & jnp.logical_not(same(s)), Lm, 0.0)
        Tinv = Tinv - _dot(_dot(Tinv, off, _NN3), Tinv, _NN3)
    return Tinv


def _gdn_specs(T, rev):
    C = GDN_CHUNK
    H = GDN_HEADS
    K = min(GDN_STEP, T // C)
    NS = T // (C * K)
    nn = (lambda n: NS - 1 - n) if rev else (lambda n: n)
    wide = lambda blk: pl.BlockSpec((K * C, H * HEAD_DIM), lambda n: (nn(n), blk))
    one = lambda off: pl.BlockSpec((K * C, HEAD_DIM), lambda n: (nn(n), off))
    vec = pl.BlockSpec((1, 128), lambda n: (0, 0))
    st = lambda rows: pl.BlockSpec((K, H, rows, rows), lambda n: (nn(n), 0, 0, 0))
    return K, NS, wide, one, vec, st


def _gdn_fwd(qkv, p, alog, dtb, nw, name):
    T = qkv.shape[0]
    C, H = GDN_CHUNK, GDN_HEADS
    N = T // C
    K, NS, wide, one, vec, st_spec = _gdn_specs(T, False)

    def body(q_ref, k_ref, v_ref, z_ref, ba_ref, al_ref, dt_ref, nw_ref, y_ref, sp_ref, ti_ref, vn_ref, o_ref, st):
        @pl.when(pl.program_id(0) == 0)
        def _():
            st[...] = jnp.zeros_like(st)

        f = _gdn_pre(_pairs(q_ref, K), _pairs(k_ref, K), _pairs(v_ref, K), ba_ref[...], al_ref[...], dt_ref[...])
        Tinv = _unit_lower_inverse(f["Lm"])
        ti_ref[...] = Tinv.reshape(K, H, C, C).astype(bf16)
        w = _dot(Tinv, f["kbg"], _NN3)
        u = _dot(Tinv, f["vb"], _NN3)
        S = st[...]
        vns, os_ = [], []
        for c in range(K):
            sl = slice(c * H, (c + 1) * H)
            sp_ref[c] = S
            vn_c = u[sl] - _dot(w[sl], S, _NN3)
            os_.append(_dot(f["qg"][sl], S, _NN3) + _dot(f["QK"][sl], vn_c, _NN3))
            S = S * f["cd"][sl] + _dot(f["kg"][sl], vn_c, _TN3)
            vns.append(vn_c)
        st[...] = S
        vn, o = jnp.concatenate(vns), jnp.concatenate(os_)
        r = lax.rsqrt(_rowsum(o * o) * (1.0 / HEAD_DIM) + EPS)
        _put_pairs(y_ref, o * r * nw_ref[...] * _silu(_pairs(z_ref, K)), K)
        _put_pairs(vn_ref, vn, K)
        _put_pairs(o_ref, o, K)

    wide_f32 = jax.ShapeDtypeStruct((T, H * HEAD_DIM), f32)
    return pl.pallas_call(
        body, grid=(NS,),
        in_specs=[wide(0), wide(1), wide(2), wide(3), one(4 * H), vec, vec, vec],
        out_specs=[wide(0), st_spec(HEAD_DIM), st_spec(C), wide(0), wide(0)],
        out_shape=[jax.ShapeDtypeStruct((T, H * HEAD_DIM), bf16), jax.ShapeDtypeStruct((N, H, HEAD_DIM, HEAD_DIM), f32),
                   jax.ShapeDtypeStruct((N, H, C, C), bf16), jax.ShapeDtypeStruct((T, H * HEAD_DIM), bf16), wide_f32],
        scratch_shapes=[pltpu.VMEM((H, HEAD_DIM, HEAD_DIM), f32)],
        compiler_params=_cparams(("arbitrary",)), name=name,
    )(qkv, qkv, qkv, p, p, alog, dtb, nw)


def _gdn_bwd(qkv, p, alog, dtb, nw, sprev, tinv, vn_all, o_all, dy_all, name):
    T = qkv.shape[0]
    C, H = GDN_CHUNK, GDN_HEADS
    N = T // C
    K, NS, wide, one, vec, st_spec = _gdn_specs(T, True)
    rs = lambda m: jnp.sum(m, axis=2, keepdims=True)

    def body(q_ref, k_ref, v_ref, z_ref, ba_ref, al_ref, dt_ref, nw_ref, sp_ref, ti_ref, vn_ref, o_ref, dy_ref,
             dqkv_ref, dz_ref, dba_ref, dal_ref, ddt_ref, dnw_ref, dst):
        @pl.when(pl.program_id(0) == 0)
        def _():
            dst[...] = jnp.zeros_like(dst)
            dal_ref[...] = jnp.zeros_like(dal_ref)
            ddt_ref[...] = jnp.zeros_like(ddt_ref)
            dnw_ref[...] = jnp.zeros_like(dnw_ref)

        ba, dtb_, nwv = ba_ref[...], dt_ref[...], nw_ref[...]
        v = _pairs(v_ref, K)
        f = _gdn_pre(_pairs(q_ref, K), _pairs(k_ref, K), v, ba, al_ref[...], dtb_)
        beta, kn, qn, kb, vb, kbg = f["beta"], f["kn"], f["qn"], f["kb"], f["vb"], f["kbg"]
        eG, eGl, cd, Di, Ds, QK, qg, kg = f["eG"], f["eGl"], f["cd"], f["Di"], f["Ds"], f["QK"], f["qg"], f["kg"]
        lane, ri, ci, rowc = f["lane"], f["ri"], f["ci"], f["rowc"]
        Tinv = ti_ref[...].reshape(K * H, C, C)
        S = sp_ref[...].reshape(K * H, HEAD_DIM, HEAD_DIM)
        w_ = _dot(Tinv, kbg, _NN3)
        vn, o = _pairs(vn_ref, K), _pairs(o_ref, K)
        z, dy = _pairs(z_ref, K), _pairs(dy_ref, K)
        r = lax.rsqrt(_rowsum(o * o) * (1.0 / HEAD_DIM) + EPS)
        nrm = o * r
        sz = _silu(z)
        dn = dy * nwv * sz
        _put_pairs(dz_ref, dy * nrm * nwv * _dsilu(z), K)
        dnw_ref[...] += jnp.sum(jnp.sum(dy * nrm * sz, axis=0), axis=0, keepdims=True)
        do = r * (dn - nrm * (_rowsum(dn * nrm) * (1.0 / HEAD_DIM)))
        dvn_do = _dot(QK, do, _TN3)
        dS_do = _dot(qg, do, _TN3)
        dqg = _dot(do, S, _NT3)
        dQK = _dot(do, vn, _NT3)
        dS = dst[...]
        dS1s, dvns = [None] * K, [None] * K
        for c in reversed(range(K)):
            sl = slice(c * H, (c + 1) * H)
            dS1s[c] = dS
            dvns[c] = _dot(kg[sl], dS, _NN3) + dvn_do[sl]
            dS = cd[sl] * dS + dS_do[sl] - _dot(w_[sl], dvns[c], _TN3)
        dst[...] = dS
        dS1, dvn = jnp.concatenate(dS1s), jnp.concatenate(dvns)
        dcd = jnp.sum(jnp.sum(S * dS1, axis=2, keepdims=True), axis=1, keepdims=True)
        dkg = _dot(vn, dS1, _NT3)
        dw = -_dot(dvn, S, _NT3)
        dqn = dqg * eG
        dkn = dkg * eGl
        deGl = rs(dkg * kn)
        dQKr = dQK * Di
        E = dQK * QK
        dqn = dqn + _dot(dQKr, kn, _NN3)
        dkn = dkn + _dot(dQKr, qn, _TN3)
        dT = _dot(dvn, vb, _NT3) + _dot(dw, kbg, _NT3)
        dvb = _dot(Tinv, dvn, _TN3)
        dkbg = _dot(Tinv, dw, _TN3)
        dkb = dkbg * eG
        deG = rs(dqg * qn + dkbg * kb)
        dL = -_dot(_dot(Tinv, dT, _TN3), Tinv, _NT3)
        dKK = dL * Ds
        E = E + dL * f["Lm"]
        dkb = dkb + _dot(dKK, kn, _NN3)
        dkn = dkn + _dot(dKK, kb, _TN3) + dkb * beta
        dbeta = rs(dkb * kn + dvb * v)
        _put_pairs(dqkv_ref, dvb * beta, K, 2 * H * HEAD_DIM)
        dG = rs(E) - rs(jnp.swapaxes(E, 1, 2)) + deG * eG - deGl * eGl
        dGl = jnp.sum(deGl * eGl, axis=1, keepdims=True) + dcd * cd
        dG = dG + jnp.where(rowc == C - 1, dGl, 0.0)
        qh = f["qh"]
        _put_pairs(dqkv_ref, (HEAD_DIM ** -0.5) * f["rq"] * (dqn - qh * _rowsum(dqn * qh)), K)
        _put_pairs(dqkv_ref, f["rk"] * (dkn - kn * _rowsum(dkn * kn)), K, H * HEAD_DIM)
        db = dbeta * beta * (1.0 - beta)
        triu = (ri <= ci).astype(f32)
        for c in range(K):
            db_all = jnp.where(lane == 0, db[c * H], 0.0)
            dG_all = jnp.where(lane == H, dG[c * H], 0.0)
            for h in range(1, H):
                db_all = db_all + jnp.where(lane == h, db[c * H + h], 0.0)
                dG_all = dG_all + jnp.where(lane == H + h, dG[c * H + h], 0.0)
            dg_all = _dot01(triu, dG_all, _NN)
            da_all = dg_all * (-f["ea"]) * _sigmoid(ba[c * C:(c + 1) * C] + dtb_)
            dba_ref[c * C:(c + 1) * C, :] = (db_all + da_all).astype(bf16)
            ddt_ref[...] += jnp.sum(da_all, axis=0, keepdims=True)
            dal_ref[...] += jnp.sum(dg_all * f["g_all"][c], axis=0, keepdims=True)

    small = jax.ShapeDtypeStruct((1, 128), f32)
    return pl.pallas_call(
        body, grid=(NS,),
        in_specs=[wide(0), wide(1), wide(2), wide(3), one(4 * H), vec, vec, vec, st_spec(HEAD_DIM), st_spec(C),
                  wide(0), wide(0), wide(0)],
        out_specs=[pl.BlockSpec((K * C, 3 * H * HEAD_DIM), lambda n: (NS - 1 - n, 0)), wide(3), one(0), vec, vec, vec],
        out_shape=[jax.ShapeDtypeStruct((T, 3 * H * HEAD_DIM), f32), jax.ShapeDtypeStruct((T, ODD_PAD), bf16),
                   jax.ShapeDtypeStruct((T, 128), bf16), small, small, small],
        scratch_shapes=[pltpu.VMEM((H, HEAD_DIM, HEAD_DIM), f32)],
        compiler_params=_cparams(("arbitrary",)), name=name,
    )(qkv, qkv, qkv, p, p, alog, dtb, nw, sprev, tinv, vn_all, o_all, dy_all)


def _lanes_from(x, s):
    return x if s % 128 == 0 else pltpu.roll(x, (128 - s) % 128, 1)


def _odd_assemble(g, name):
    R = g.shape[1]
    tr = min(256, R)
    n_blk = ODD_SHARD_PAD // 128

    def body(g_ref, o_ref):
        lane = lax.broadcasted_iota(jnp.int32, (tr, 128), 1)
        blk = lambda d, m: g_ref[d, :, m * 128:(m + 1) * 128]
        for gb in range(ODD_PAD // 128):
            c0 = 128 * gb
            if c0 >= ODD_IN:
                o_ref[:, c0:c0 + 128] = jnp.zeros((tr, 128), g.dtype)
                continue
            d0 = c0 // ODD_SHARD
            m0, sh = divmod(c0 - ODD_SHARD * d0, 128)
            take = min(128, ODD_SHARD * (d0 + 1) - c0)
            p = _lanes_from(blk(d0, m0), sh)
            if sh and m0 + 1 < n_blk:
                p = jnp.where(lane < 128 - sh, p, _lanes_from(blk(d0, m0 + 1), sh))
            if take < 128:
                nxt = pltpu.roll(blk(d0 + 1, 0), take, 1) if d0 + 1 < N_DEV else jnp.zeros((tr, 128), g.dtype)
                p = jnp.where(lane < take, p, nxt)
            o_ref[:, c0:c0 + 128] = p

    return pl.pallas_call(
        body, grid=(R // tr,),
        in_specs=[pl.BlockSpec((N_DEV, tr, ODD_SHARD_PAD), lambda i: (0, i, 0))],
        out_specs=pl.BlockSpec((tr, ODD_PAD), lambda i: (i, 0)),
        out_shape=jax.ShapeDtypeStruct((R, ODD_PAD), g.dtype),
        compiler_params=_cparams(("parallel",)), name=name,
    )(g)


def _odd_split(w, name):
    R = w.shape[0]
    tr = min(256, R)

    def body(w_ref, o_ref):
        lane = lax.broadcasted_iota(jnp.int32, (tr, 128), 1)
        blk = lambda gb: w_ref[:, gb * 128:(gb + 1) * 128]
        for d in range(N_DEV):
            for m in range(ODD_SHARD_PAD // 128):
                g0, sh = divmod(ODD_SHARD * d + 128 * m, 128)
                p = _lanes_from(blk(g0), sh)
                if sh and g0 + 1 < ODD_PAD // 128:
                    p = jnp.where(lane < 128 - sh, p, _lanes_from(blk(g0 + 1), sh))
                real = ODD_SHARD - 128 * m
                if real < 128:
                    p = jnp.where(lane < real, p, jnp.zeros_like(p))
                o_ref[d, :, m * 128:(m + 1) * 128] = p

    return pl.pallas_call(
        body, grid=(R // tr,),
        in_specs=[pl.BlockSpec((tr, ODD_PAD), lambda i: (i, 0))],
        out_specs=pl.BlockSpec((N_DEV, tr, ODD_SHARD_PAD), lambda i: (0, i, 0)),
        out_shape=jax.ShapeDtypeStruct((N_DEV, R, ODD_SHARD_PAD), w.dtype),
        compiler_params=_cparams(("parallel",)), name=name,
    )(w)


def _adam_tile(g, w_ref, m_ref, v_ref, go_ref, d_ref, mo_ref, vo_ref):
    c1 = 1.0 - ADAM_B1 ** ADAM_STEP
    c2 = 1.0 - ADAM_B2 ** ADAM_STEP
    mn = ADAM_B1 * m_ref[...] + (1.0 - ADAM_B1) * g
    vn = ADAM_B2 * v_ref[...] + (1.0 - ADAM_B2) * (g * g)
    go_ref[...] = g
    mo_ref[...] = mn
    vo_ref[...] = vn
    d_ref[...] = -ADAM_LR * ((mn / c1) / (jnp.sqrt(vn / c2) + ADAM_EPS) + ADAM_WD * w_ref[...])


def _adamw(w, gs, m, v, name, layer=None, prev=None):
    R, Cc = w.shape[-2:]
    S = gs.shape[0]
    tr = R
    if S * R * Cc * 4 > (4 << 20):
        for cand in (256, 128, 64, 32, 16, 8):
            if R % cand == 0 and R > cand:
                tr = cand
                break

    def body(w_ref, g_ref, m_ref, v_ref, *rest):
        g = g_ref[0].astype(f32)
        for s in range(1, S):
            g = g + g_ref[s].astype(f32)
        _adam_tile(g, w_ref, m_ref, v_ref, *rest[-4:])

    if layer is None:
        blk = pl.BlockSpec((tr, Cc), lambda i: (i, 0))
    else:
        blk = pl.BlockSpec((None, tr, Cc), lambda i: (layer, i, 0))
    out = jax.ShapeDtypeStruct(w.shape, f32)
    carried = [] if prev is None else list(prev)
    return pl.pallas_call(
        body, grid=(R // tr,),
        in_specs=[blk, pl.BlockSpec((S, tr, Cc), lambda i: (0, i, 0)), blk, blk]
        + [pl.BlockSpec(memory_space=pl.ANY)] * len(carried),
        out_specs=[blk] * 4, out_shape=[out] * 4,
        input_output_aliases={4 + j: j for j in range(len(carried))},
        compiler_params=_cparams(("parallel",)), name=name,
    )(w, gs, m, v, *carried)


def _adamw_column_major(w, gs, m, v, name):
    _, R, Cc = w.shape
    S = gs.shape[0]
    q = R // 128
    dense = lambda a: jnp.transpose(a, (2, 0, 1)).reshape(Cc * q, 128)
    back = lambda a: jnp.transpose(a.reshape(Cc, q, 128), (1, 2, 0)).reshape(1, R, Cc)

    def body(w_ref, g_ref, m_ref, v_ref, go_ref, d_ref, mo_ref, vo_ref, gt):
        for i in range(q):
            g = g_ref[0, i * 128:(i + 1) * 128, :].astype(f32)
            for s in range(1, S):
                g = g + g_ref[s, i * 128:(i + 1) * 128, :].astype(f32)
            gt[pl.ds(i, 128, stride=q), :] = g.T
        _adam_tile(gt[...], w_ref, m_ref, v_ref, go_ref, d_ref, mo_ref, vo_ref)

    blk = pl.BlockSpec((128 * q, 128), lambda j: (j, 0))
    outs = pl.pallas_call(
        body, grid=(pl.cdiv(Cc, 128),),
        in_specs=[blk, pl.BlockSpec((S, R, 128), lambda j: (0, 0, j)), blk, blk],
        out_specs=[blk] * 4, out_shape=[jax.ShapeDtypeStruct((Cc * q, 128), f32)] * 4,
        scratch_shapes=[pltpu.VMEM((128 * q, 128), f32)],
        compiler_params=_cparams(("parallel",)), name=name,
    )(dense(w), gs, dense(m), dense(v))
    return [back(o) for o in outs]


def _me():
    x, y, c = lax.axis_index("x"), lax.axis_index("y"), lax.axis_index("c")
    return x, y, c, 4 * x + 2 * y + c


def _peer(k):
    x, y, c, _ = _me()
    px = 1 - x if k & 4 else x
    py = 1 - y if k & 2 else y
    pc = 1 - c if k & 1 else c
    return (px, py, pc), 4 * px + 2 * py + pc


_HBM = pl.BlockSpec(memory_space=pltpu.HBM)
_SEM = pl.BlockSpec(memory_space=pltpu.SEMAPHORE)
_EFFECT = pltpu.SideEffectType.DATAFLOW_SIDE_EFFECTING


def _copy(src, land, ssem, rsem, k, blocked, landing_slot_of_peer):
    pid, pidx = _peer(k)
    slot = pidx if landing_slot_of_peer else _me()[3]
    return pltpu.make_async_remote_copy(src_ref=src.at[pidx] if blocked else src, dst_ref=land.at[slot],
                                        send_sem=ssem.at[k - 1], recv_sem=rsem.at[k - 1], device_id=pid, device_id_type=MESH)


def _own_copy(src, land, rsem, blocked):
    me = _me()[3]
    return pltpu.make_async_copy(src.at[me] if blocked else src, land.at[me], rsem.at[N_DEV - 1])


_ALL_PEERS = tuple(range(1, N_DEV))
_SAME_CORE_PEERS = (1, 2, 4, 6)
_OTHER_CORE_PEERS = (3, 5, 7)


def _relay_copies(land, ssem, rsem):
    sibling, _ = _peer(1)
    copies = []
    for i in range(len(land)):
        for j, k in enumerate(_OTHER_CORE_PEERS):
            sems = dict(send_sem=ssem.at[3 * i + j], recv_sem=rsem.at[3 * i + j], device_id=sibling, device_id_type=MESH)
            _, outgoing = _peer(k - 1)
            _, incoming = _peer(k)
            copies.append((pltpu.make_async_remote_copy(src_ref=land[i].at[outgoing], dst_ref=land[i].at[outgoing], **sems),
                           pltpu.make_async_remote_copy(src_ref=land[i].at[incoming], dst_ref=land[i].at[incoming], **sems)))
    return copies


def _relay_start(lands, name):
    n = len(lands)

    def body(*refs):
        for send, _ in _relay_copies(refs[:n], refs[n], refs[n + 1]):
            send.start()
        refs[-1][...] = jnp.zeros_like(refs[-1])

    sem = pltpu.SemaphoreType.DMA((3 * n,))
    res = pl.pallas_call(
        body, name=name,
        out_shape=(sem, sem) + tuple(pltpu.HBM(a.shape, a.dtype) for a in lands) + (jax.ShapeDtypeStruct((8, 128), f32),),
        in_specs=(_HBM,) * n, out_specs=(_SEM, _SEM) + (_HBM,) * n + (pl.BlockSpec(memory_space=pltpu.VMEM),),
        input_output_aliases={i: 2 + i for i in range(n)},
        compiler_params=pltpu.CompilerParams(has_side_effects=_EFFECT),
    )(*lands)
    return res[:-1], res[-1]


def _relay_wait(handle, after, name):
    ssem, rsem, lands = handle[0], handle[1], handle[2:]
    n = len(lands)
    after = tuple(after) if isinstance(after, (tuple, list)) else (after,)

    def body(*refs):
        for send, recv in _relay_copies(refs[:n], refs[n], refs[n + 1]):
            send.wait_send()
            recv.wait_recv()

    return pl.pallas_call(
        body, name=name, out_shape=tuple(pltpu.HBM(a.shape, a.dtype) for a in lands),
        in_specs=(_HBM,) * n + (_SEM, _SEM) + (pl.BlockSpec(memory_space=pl.ANY),) * len(after), out_specs=(_HBM,) * n,
        input_output_aliases={i: i for i in range(n)}, compiler_params=pltpu.CompilerParams(has_side_effects=_EFFECT),
    )(*lands, ssem, rsem, *after)


def _send_start(srcs, blocked, name, relayed=()):
    n = len(srcs)
    lands = [lax.empty(a.shape if blocked else (N_DEV,) + a.shape, a.dtype) for a in srcs]

    def body(*refs):
        src, land, sems, token = refs[:n], refs[n:2 * n], refs[2 * n:4 * n], refs[-1]
        for i in range(n):
            for k in (_SAME_CORE_PEERS if i in relayed else _ALL_PEERS):
                _copy(src[i], land[i], sems[2 * i], sems[2 * i + 1], k, blocked, False).start()
        for i in range(n):
            _own_copy(src[i], land[i], sems[2 * i + 1], blocked).start()
        token[...] = jnp.zeros_like(token)

    sems = (pltpu.SemaphoreType.DMA((N_DEV - 1,)), pltpu.SemaphoreType.DMA((N_DEV,)))
    hbm = lambda a: pltpu.with_memory_space_constraint(a, pltpu.HBM)
    res = pl.pallas_call(
        body, name=name,
        out_shape=sems * n + tuple(pltpu.HBM(a.shape, a.dtype) for a in srcs + lands)
        + (jax.ShapeDtypeStruct((8, 128), f32),),
        in_specs=(_HBM,) * (2 * n),
        out_specs=(_SEM,) * (2 * n) + (_HBM,) * (2 * n) + (pl.BlockSpec(memory_space=pltpu.VMEM),),
        input_output_aliases={j: 2 * n + j for j in range(2 * n)},
        compiler_params=pltpu.CompilerParams(has_side_effects=_EFFECT),
    )(*[hbm(a) for a in srcs], *[hbm(a) for a in lands])
    handles = [(res[2 * i], res[2 * i + 1], res[2 * n + i], res[3 * n + i]) for i in range(n)]
    return handles, res[-1]


def _send_wait(handle, blocked, after, name, relayed=False):
    ssem, rsem, src, land = handle
    after = tuple(after) if isinstance(after, (tuple, list)) else (after,)

    def body(src_ref, land_ref, ssem_ref, rsem_ref, *rest):
        for k in (_SAME_CORE_PEERS if relayed else _ALL_PEERS):
            cp = _copy(src_ref, land_ref, ssem_ref, rsem_ref, k, blocked, True)
            cp.wait_send()
            cp.wait_recv()
        _own_copy(src_ref, land_ref, rsem_ref, blocked).wait()

    return pl.pallas_call(
        body, name=name, out_shape=(pltpu.HBM(src.shape, src.dtype), pltpu.HBM(land.shape, land.dtype)),
        in_specs=(_HBM, _HBM, _SEM, _SEM) + (pl.BlockSpec(memory_space=pl.ANY),) * len(after), out_specs=(_HBM, _HBM),
        input_output_aliases={0: 0, 1: 1}, compiler_params=pltpu.CompilerParams(has_side_effects=_EFFECT),
    )(src, land, ssem, rsem, *after)


def _block_diag(w):
    nb, bs = w.shape[0], w.shape[1]
    eye = jnp.eye(nb, dtype=w.dtype)
    return (eye[:, None, :, None] * w[:, :, None, :]).reshape(nb * bs, nb * bs)


_SQUARE_TILES = dict(tm=1024, tn=1024, tk=1024)


def _mlp_fwd(x, hm, wu, wd, tag, epilogue, extras, outs, between=None):
    (r,) = _matmul(hm, wu, "nn", outs=[bf16], epilogue=lambda acc: (jnp.maximum(acc, 0.0),), name=f"mlp_up_{tag}")
    if between is not None:
        extras = (extras[0] + between(r)[0:1, 0:1],) + tuple(extras[1:])
    res = _matmul(r, wd, "nn", outs=outs, extras=(x,) + tuple(extras), epilogue=epilogue, a_map=jnp.square,
                  name=f"mlp_down_{tag}", **_SQUARE_TILES)
    return res, (hm, r)


def _mlp_bwd(x, nw, wu, wd, saved, dxo, dxo_b, tag, sink):
    hm, r = saved
    (du,) = _matmul(dxo_b, wd, "nt", outs=[bf16], extras=(r,), epilogue=lambda acc, rr: (acc * (2.0 * rr.astype(f32)),),
                    name=f"mlp_dact_{tag}")
    (dwd,) = _matmul(r, dxo_b, "tn", outs=[bf16], a_map=jnp.square, name=f"mlp_dwd_{tag}", **_SQUARE_TILES)
    tok = sink({f"w_down{tag}": dwd.reshape(N_DEV, D_FF // N_DEV, D_MODEL)})
    (dwu,) = _matmul(hm, du, "tn", outs=[bf16], shard_cols=2, extras=(jnp.broadcast_to(tok[0:1, 0:1], (1, D_FF)),),
                     epilogue=lambda acc, zero: (acc + zero,), name=f"mlp_dwu_{tag}")
    tok = sink({f"w_up{tag}": dwu})
    return _matmul(du, wu, "nt", outs=_RMS_BWD_OUTS, extras=(x, dxo, nw + tok[0:1, 0:1]), epilogue=_rms_bwd_ep,
                   name=f"mlp_dh_{tag}", **_SQUARE_TILES)


def _local_step(x, tgt, P, weight, sink, ahead):
    T = x.shape[0]
    cos, sin = _rope_tables(T)
    rtab = _ret_tables()
    row = lambda a: a.reshape(1, -1)
    mix_nw, mlp_nw = P["mixer_norm_w"], P["mlp_norm_w"]
    wr_bd, wi_bd = _block_diag(P["lru_w_r"]), _block_diag(P["lru_w_i"])
    lru_b, lru_br, lru_bi, lru_lam = row(P["lru_conv_b"]), row(P["lru_b_r"]), row(P["lru_b_i"]), row(P["lru_lambda"])
    pad16 = lambda a: jnp.pad(a.reshape(1, GDN_HEADS), ((0, 0), (GDN_HEADS, 128 - 2 * GDN_HEADS)))
    alog, dtb = pad16(P["gdn_a_log"]), pad16(P["gdn_dt_bias"])
    gnw = row(P["gdn_norm_w"])

    x0 = x
    h0 = _rms_fwd(x0, mix_nw[0:1], "rms_mix_0")
    w_ie = weight("w_in_even", (h0, cos, sin, wr_bd, wi_bd))
    (pe,) = _matmul(h0, w_ie, "nn", outs=[f32], name="in_even")
    mix0, o_ret, s_ret = _ret_fwd(pe, cos, sin, rtab, "ret_fwd")
    w_lc = weight("lru_conv_w", pe)
    xc = _conv_fwd(pe, 4, w_lc, lru_b, False, "lru_conv_fwd")
    mix0, h_lru = _lru_fwd(xc, pe, 20, wr_bd, wi_bd, lru_br, lru_bi, lru_lam, mix0, "lru_fwd")
    w_oe = weight("w_out_even", mix0)
    tok = ahead(("w_up0", "w_down0"), mix0)
    x1, hm0 = _matmul(mix0, w_oe, "nn", outs=[f32, bf16], extras=(x0, mlp_nw[0:1] + tok[0:1, 0:1]),
                      epilogue=_residual_rms_ep, name="out_even", tm=1024, tn=D_MODEL)
    w_u0, w_d0 = weight("w_up0", x1), weight("w_down0", x1)
    (x2, h1), mlp0 = _mlp_fwd(x1, hm0, w_u0, w_d0, "0", _residual_rms_ep, (mix_nw[1:2],), [f32, bf16],
                              between=lambda r: ahead(("w_in_odd",), r))
    w_io = weight("w_in_odd", h1)
    (po,) = _matmul(h1, w_io, "nn", outs=[f32], tm=2048, tn=ODD_PAD // 3, name="in_odd")
    w_gc = weight("gdn_conv_w", po)
    qkv = _conv_fwd(po, 0, w_gc, None, True, "gdn_conv_fwd")
    y_gdn, s_gdn, ti_gdn, vn_gdn, o_gdn = _gdn_fwd(qkv, po, alog, dtb, gnw, "gdn_fwd")
    w_oo = weight("w_out_odd", y_gdn)
    x3, hm1 = _matmul(y_gdn, w_oo, "nn", outs=[f32, bf16], extras=(x2, mlp_nw[1:2]), epilogue=_residual_rms_ep,
                      name="out_odd", tm=1024, tn=D_MODEL)
    w_u1, w_d1 = weight("w_up1", x3), weight("w_down1", x3)
    (loss, dx4, dx4_b, d_final), mlp1 = _mlp_fwd(x3, hm1, w_u1, w_d1, "1", _loss_ep, (row(P["final_norm_w"]), tgt),
                                                 _LOSS_OUTS)
    dx3, dx3_b, d_mlp_nw1 = _mlp_bwd(x3, mlp_nw[1:2], w_u1, w_d1, mlp1, dx4, dx4_b, "1", sink)
    (dy_gdn,) = _matmul(dx3_b, w_oo, "nt", outs=[f32], name="out_odd_dx")
    (d_woo,) = _matmul(y_gdn, dx3_b, "tn", outs=[bf16], name="out_odd_dw")
    dqkv, dpo, dba, d_alog, d_dtb, d_gnw = _gdn_bwd(qkv, po, alog, dtb, gnw, s_gdn, ti_gdn, vn_gdn, o_gdn, dy_gdn,
                                                  "gdn_bwd")
    dpo, d_gconv, _ = _conv_bwd(po, 0, w_gc, None, True, dqkv, dpo, "gdn_conv_bwd")
    dpo = lax.dynamic_update_slice(dpo, dba, (0, 4 * D_MODEL))
    (d_wio,) = _matmul(h1, dpo, "tn", outs=[bf16], tn=ODD_PAD // 3, name="in_odd_dw")
    tok = sink(dict(w_out_odd=d_woo.reshape(N_DEV, D_MODEL // N_DEV, D_MODEL), w_in_odd=_odd_split(d_wio, "w_in_odd_split")))
    dx2, dx2_b, d_mix_nw1 = _matmul(dpo, w_io, "nt", outs=_RMS_BWD_OUTS, extras=(x2, dx3, mix_nw[1:2] + tok[0:1, 0:1]),
                                    epilogue=_rms_bwd_ep, tm=1024, tn=1024, tk=ODD_PAD // 3, name="in_odd_dx")
    dx1, dx1_b, d_mlp_nw0 = _mlp_bwd(x1, mlp_nw[0:1], w_u0, w_d0, mlp0, dx2, dx2_b, "0", sink)
    (d_woe,) = _matmul(mix0, dx1_b, "tn", outs=[bf16], name="out_even_dw")
    tok = sink(dict(w_out_even=d_woe.reshape(N_DEV, D_MODEL // N_DEV, D_MODEL)))
    (dmix0,) = _matmul(dx1_b, w_oe, "nt", outs=[f32], name="out_even_dx")
    dpe = _ret_bwd(pe, cos, sin, rtab, o_ret, s_ret, dmix0, "ret_bwd")
    dxc, dpe, d_wr, d_wi, d_br, d_bi, d_lam = _lru_bwd(xc, pe, 20, wr_bd, wi_bd, lru_br, lru_bi, lru_lam + tok[0:1, 0:1],
                                                       h_lru, dmix0, 4, dpe, "lru_bwd")
    dpe, d_lconv, d_lconv_b = _conv_bwd(pe, 4, w_lc, lru_b, False, dxc, dpe, "lru_conv_bwd")
    G = dict(
        mlp_norm_w=jnp.concatenate([d_mlp_nw0, d_mlp_nw1], axis=0),
        final_norm_w=d_final.reshape(-1),
        lru_conv_w=d_lconv, lru_conv_b=d_lconv_b.reshape(-1),
        lru_w_r=d_wr, lru_b_r=d_br.reshape(-1), lru_w_i=d_wi, lru_b_i=d_bi.reshape(-1),
        lru_lambda=d_lam.reshape(-1), gdn_conv_w=d_gconv,
        gdn_a_log=d_alog[0, GDN_HEADS:2 * GDN_HEADS], gdn_dt_bias=d_dtb[0, GDN_HEADS:2 * GDN_HEADS],
        gdn_norm_w=d_gnw.reshape(-1),
    )
    packed = _pack([G[k] for k in _SMALL] + [d_lconv, d_gconv, loss[0, 0:1]])
    tok = sink(dict(small=jnp.broadcast_to(packed[None], (N_DEV,) + packed.shape)))
    (d_wie,) = _matmul(h0, dpe, "tn", outs=[bf16], shard_cols=2, extras=(jnp.broadcast_to(tok[0:1, 0:1], (1, 3 * D_MODEL)),),
                       epilogue=lambda acc, zero: (acc + zero,), name="in_even_dw")
    tok = sink(dict(w_in_even=d_wie))
    dx0, _, d_mix_nw0 = _matmul(dpe, w_ie, "nt", outs=_RMS_BWD_OUTS, extras=(x0, dx1, mix_nw[0:1] + tok[0:1, 0:1]),
                                epilogue=_rms_bwd_ep, name="in_even_dx", **_SQUARE_TILES)
    G["mixer_norm_w"] = jnp.concatenate([d_mix_nw0, d_mix_nw1], axis=0)
    return loss, dx0, G


_SMALL = ["mlp_norm_w", "final_norm_w", "lru_conv_b", "lru_w_r", "lru_b_r", "lru_w_i", "lru_b_i",
          "lru_lambda", "gdn_a_log", "gdn_dt_bias", "gdn_norm_w"]
_PACK_ROWS = 688


def _pack(parts):
    flat = jnp.concatenate([p.reshape(-1) for p in parts])
    return jnp.pad(flat, (0, _PACK_ROWS * 128 - flat.shape[0])).reshape(_PACK_ROWS, 128)


def _unpack(packed, shapes):
    flat = packed.reshape(-1)
    out, off = [], 0
    for s in shapes:
        n = int(np.prod(s))
        out.append(flat[off:off + n].reshape(s))
        off += n
    return out


def kernel(x, mixer_norm_w, mlp_norm_w, final_norm_w, w_in_even, lru_conv_w, lru_conv_b, lru_w_r, lru_b_r, lru_w_i, lru_b_i, lru_lambda, w_out_even, w_in_odd, gdn_conv_w, gdn_a_log, gdn_dt_bias, gdn_norm_w, w_out_odd, w_up, w_down, loss_target, m_mixer_norm_w, m_mlp_norm_w, m_final_norm_w, m_w_in_even, m_lru_conv_w, m_lru_conv_b, m_lru_w_r, m_lru_b_r, m_lru_w_i, m_lru_b_i, m_lru_lambda, m_w_out_even, m_w_in_odd, m_gdn_conv_w, m_gdn_a_log, m_gdn_dt_bias, m_gdn_norm_w, m_w_out_odd, m_w_up, m_w_down, v_mixer_norm_w, v_mlp_norm_w, v_final_norm_w, v_w_in_even, v_lru_conv_w, v_lru_conv_b, v_lru_w_r, v_lru_b_r, v_lru_w_i, v_lru_b_i, v_lru_lambda, v_w_out_even, v_w_in_odd, v_gdn_conv_w, v_gdn_a_log, v_gdn_dt_bias, v_gdn_norm_w, v_w_out_odd, v_w_up, v_w_down):
    Pw = dict(mixer_norm_w=mixer_norm_w, mlp_norm_w=mlp_norm_w, final_norm_w=final_norm_w, w_in_even=w_in_even,
              lru_conv_w=lru_conv_w, lru_conv_b=lru_conv_b, lru_w_r=lru_w_r, lru_b_r=lru_b_r, lru_w_i=lru_w_i,
              lru_b_i=lru_b_i, lru_lambda=lru_lambda, w_out_even=w_out_even, w_in_odd=w_in_odd, gdn_conv_w=gdn_conv_w,
              gdn_a_log=gdn_a_log, gdn_dt_bias=gdn_dt_bias, gdn_norm_w=gdn_norm_w, w_out_odd=w_out_odd, w_up=w_up,
              w_down=w_down)
    Pm = dict(mixer_norm_w=m_mixer_norm_w, mlp_norm_w=m_mlp_norm_w, final_norm_w=m_final_norm_w, w_in_even=m_w_in_even,
              lru_conv_w=m_lru_conv_w, lru_conv_b=m_lru_conv_b, lru_w_r=m_lru_w_r, lru_b_r=m_lru_b_r, lru_w_i=m_lru_w_i,
              lru_b_i=m_lru_b_i, lru_lambda=m_lru_lambda, w_out_even=m_w_out_even, w_in_odd=m_w_in_odd,
              gdn_conv_w=m_gdn_conv_w, gdn_a_log=m_gdn_a_log, gdn_dt_bias=m_gdn_dt_bias, gdn_norm_w=m_gdn_norm_w,
              w_out_odd=m_w_out_odd, w_up=m_w_up, w_down=m_w_down)
    Pv = dict(mixer_norm_w=v_mixer_norm_w, mlp_norm_w=v_mlp_norm_w, final_norm_w=v_final_norm_w, w_in_even=v_w_in_even,
              lru_conv_w=v_lru_conv_w, lru_conv_b=v_lru_conv_b, lru_w_r=v_lru_w_r, lru_b_r=v_lru_b_r, lru_w_i=v_lru_w_i,
              lru_b_i=v_lru_b_i, lru_lambda=v_lru_lambda, w_out_even=v_w_out_even, w_in_odd=v_w_in_odd,
              gdn_conv_w=v_gdn_conv_w, gdn_a_log=v_gdn_a_log, gdn_dt_bias=v_gdn_dt_bias, gdn_norm_w=v_gdn_norm_w,
              w_out_odd=v_w_out_odd, w_up=v_w_up, w_down=v_w_down)
    me = _me()[3]
    T = x.shape[1]

    cols = lambda g: jnp.transpose(g, (1, 0, 2)).reshape(g.shape[1], -1)
    rows = lambda g: g.reshape(-1, g.shape[2])
    wide = lambda g: _odd_assemble(g, "w_in_odd_assemble")
    odd_shard = jnp.pad(w_in_odd[0].astype(bf16), ((0, 0), (0, ODD_SHARD_PAD - ODD_SHARD)))
    as_is = lambda g: g
    gather = dict(
        w_in_even=(w_in_even[0].astype(bf16), cols), lru_conv_w=(lru_conv_w[0], cols),
        w_out_even=(w_out_even[0].astype(bf16), rows), w_up0=(w_up[0].astype(bf16), as_is), w_down0=(w_down[0].astype(bf16), rows),
        w_in_odd=(odd_shard, wide), gdn_conv_w=(gdn_conv_w[0], cols),
        w_out_odd=(w_out_odd[0].astype(bf16), rows), w_up1=(w_up[1].astype(bf16), as_is), w_down1=(w_down[1].astype(bf16), rows))
    relay_groups = (("w_in_even",), ("w_up0", "w_down0"), ("w_in_odd",))
    relayed = sum(relay_groups, ())
    handles, tok = _send_start([s for s, _ in gather.values()], False, "gather_start",
                               relayed=[i for i, name in enumerate(gather) if name in relayed])
    handles = dict(zip(gather, handles))
    relays, landed, full = {}, {}, {}

    def ahead(group, after):
        lands = [_send_wait(handles[n], False, after, f"gather_wait_{n}", relayed=True)[1] for n in group]
        relays[group], token = _relay_start(lands, "relay_start_" + "_".join(group))
        return token

    def weight(name, after):
        if name not in landed and name in relayed:
            group = next(g for g in relay_groups if name in g)
            if group not in relays:
                ahead(group, after)
            landed.update(zip(group, _relay_wait(relays[group], after, "relay_wait_" + "_".join(group))))
        elif name not in landed:
            landed[name] = _send_wait(handles[name], False, after, f"gather_wait_{name}")[1]
        if name not in full:
            full[name] = gather[name][1](landed[name])
        return full[name]

    P = {k: Pw[k] for k in ("mlp_norm_w", "final_norm_w")}
    P["mixer_norm_w"] = mixer_norm_w + tok[0:1, 0:1]
    for k in ("lru_w_r", "lru_w_i", "lru_conv_b", "lru_b_r", "lru_b_i", "lru_lambda", "gdn_a_log", "gdn_dt_bias", "gdn_norm_w"):
        P[k] = Pw[k][0]

    sent = {}

    def sink(grads):
        hs, token = _send_start(list(grads.values()), True, "grads_start_" + "_".join(grads))
        sent.update(zip(grads, hs))
        return token

    loss, dx, G = _local_step(x[0], loss_target[0], P, weight, sink, ahead)
    lanes = lambda a: a.reshape(-1, 128)
    sink(dict(mixer_norm_w=jnp.broadcast_to(lanes(G["mixer_norm_w"])[None], (N_DEV, 2 * D_MODEL // 128, 128))))

    def received(name, after=dx):
        return _send_wait(sent[name], True, after, f"grads_wait_{name}")[1]

    out = {}
    nff = D_FF // N_DEV

    def whole(name, gs):
        out[name] = tuple(_adamw(Pw[name], gs, Pm[name], Pv[name], f"adamw_{name}", layer=0))

    def layers(name):
        res = None
        for l in range(2):
            res = _adamw(Pw[name], received(f"{name}{l}"), Pm[name], Pv[name], f"adamw_{name}{l}", layer=l, prev=res)
        out[name] = tuple(res)

    layers("w_up")
    layers("w_down")
    whole("w_out_odd", received("w_out_odd"))
    out["w_in_odd"] = tuple(_adamw_column_major(w_in_odd, received("w_in_odd"), m_w_in_odd, v_w_in_odd, "adamw_w_in_odd"))
    whole("w_out_even", received("w_out_even"))
    small_shapes = [Pw[k].shape for k in _SMALL]
    pw, pm, pv = (_pack([Q[k] for k in _SMALL]) for Q in (Pw, Pm, Pv))
    sg, sd, sm, sv = _adamw(pw, received("small", out["w_out_even"][1]), pm, pv, "adamw_small")
    for arrs_i, packed_out in enumerate((sg, sd, sm, sv)):
        for k, a in zip(_SMALL, _unpack(packed_out, small_shapes)):
            out.setdefault(k, [None] * 4)[arrs_i] = a
    whole("w_in_even", received("w_in_even", sd))
    out["mixer_norm_w"] = tuple(
        a.reshape(mixer_norm_w.shape) for a in
        _adamw(lanes(mixer_norm_w), received("mixer_norm_w", out["w_in_even"][1]), lanes(m_mixer_norm_w),
               lanes(v_mixer_norm_w), "adamw_mixer_norm_w"))
    n_small = sum(int(np.prod(s)) for s in small_shapes)
    gflat = sg.reshape(-1)
    g_lconv = gflat[n_small:n_small + CONV_K * LRU_WIDTH].reshape(CONV_K, LRU_WIDTH)
    g_gconv = gflat[n_small + CONV_K * LRU_WIDTH:n_small + CONV_K * (LRU_WIDTH + 3072)].reshape(CONV_K, 3072)
    whole("lru_conv_w", lax.dynamic_slice_in_dim(g_lconv, me * 64, 64, axis=1)[None])
    whole("gdn_conv_w", lax.dynamic_slice_in_dim(g_gconv, me * 384, 384, axis=1)[None])

    names = ["mixer_norm_w", "mlp_norm_w", "final_norm_w", "w_in_even", "lru_conv_w", "lru_conv_b", "lru_w_r", "lru_b_r",
             "lru_w_i", "lru_b_i", "lru_lambda", "w_out_even", "w_in_odd", "gdn_conv_w", "gdn_a_log", "gdn_dt_bias",
             "gdn_norm_w", "w_out_odd", "w_up", "w_down"]
    total = gflat[n_small + CONV_K * (LRU_WIDTH + 3072)]
    res = [total, dx[None]]
    for j in range(4):
        res += [out[k][j] for k in names]
    return tuple(res)
```

```python
import math

import numpy as np
import jax
import jax.numpy as jnp
from jax import lax
from jax.experimental import pallas as pl
from jax.experimental.pallas import tpu as pltpu

f32 = jnp.float32
bf16 = jnp.bfloat16

N_DEV = 8
D_MODEL = 1024
D_FF = 4096
EPS = 1e-6
RET_HEADS = 4
RET_CHUNK = 128
RET_STEP = 4
ROPE_THETA = 10000.0
LRU_WIDTH = 512
LRU_C = 8.0
GDN_HEADS = 8
GDN_CHUNK = 64
GDN_STEP = 4
HEAD_DIM = 128
ODD_IN = 4112
ODD_PAD = 4224
ODD_SHARD = ODD_IN // N_DEV
ODD_SHARD_PAD = 640
ADAM_LR, ADAM_B1, ADAM_B2, ADAM_EPS, ADAM_WD, ADAM_STEP = 0.001, 0.9, 0.999, 1e-08, 0.01, 10
VMEM_LIMIT = 56 * 1024 * 1024

_NN = (((1,), (0,)), ((), ()))
_NT = (((1,), (1,)), ((), ()))
_TN = (((0,), (0,)), ((), ()))
MESH = pl.DeviceIdType.MESH


def _cparams(sem):
    return pltpu.CompilerParams(dimension_semantics=sem, vmem_limit_bytes=VMEM_LIMIT)


def _dot(a, b, dn):
    return lax.dot_general(a.astype(bf16), b.astype(bf16), dn, preferred_element_type=f32)


def _dot01(a01, b, dn):
    a = a01.astype(bf16)
    b0 = b.astype(bf16)
    r1 = b - b0.astype(f32)
    b1 = r1.astype(bf16)
    b2 = (r1 - b1.astype(f32)).astype(bf16)
    d = lambda q: lax.dot_general(a, q, dn, preferred_element_type=f32)
    return d(b0) + (d(b1) + d(b2))


def _sigmoid(x):
    return jax.nn.sigmoid(x)


def _silu(x):
    return x * _sigmoid(x)


def _dsilu(x):
    s = _sigmoid(x)
    return s * (1.0 + x * (1.0 - s))


def _softplus(x):
    return jnp.maximum(x, 0.0) + jnp.log1p(jnp.exp(-jnp.abs(x)))


_GELU_C = math.sqrt(2.0 / math.pi)


def _gelu(y):
    return 0.5 * y * (1.0 + jnp.tanh(_GELU_C * (y + 0.044715 * y * y * y)))


def _dgelu(y):
    t = jnp.tanh(_GELU_C * (y + 0.044715 * y * y * y))
    return 0.5 * (1.0 + t) + 0.5 * y * (1.0 - t * t) * _GELU_C * (1.0 + 3.0 * 0.044715 * y * y)


def _matmul(a, b, form, *, outs, name, epilogue=None, extras=(), tm=4096, tn=512, tk=1024, shard_cols=0, a_map=None,
            after=()):
    if form == "tn":
        K, M = a.shape
    else:
        M, K = a.shape
    per_step = 1
    if b.ndim == 3:
        assert form in ("nn", "nt"), name
        N = b.shape[1] if form == "nt" else N_DEV * b.shape[2]
        if form == "nn":
            tn = b.shape[2]
        else:
            per_step = max(1, tk // b.shape[2])
            tk = per_step * b.shape[2]
    else:
        N = b.shape[0] if form == "nt" else b.shape[1]
    ns = N // N_DEV
    if shard_cols:
        tn = ns * shard_cols
    tm, tn, tk = min(tm, M), min(tn, N), min(tk, K)
    assert M % tm == 0 and N % tn == 0 and K % tk == 0, (name, M, N, K, tm, tn, tk)
    nk = K // tk
    dn = {"nn": _NN, "nt": _NT, "tn": _TN}[form]
    if form == "tn":
        a_spec = pl.BlockSpec((tk, tm), lambda i, j, k: (k, i))
    else:
        a_spec = pl.BlockSpec((tm, tk), lambda i, j, k: (i, k))
    if b.ndim == 3:
        b_spec = (pl.BlockSpec((per_step, tn, tk // per_step), lambda i, j, k: (k, j, 0)) if form == "nt"
                  else pl.BlockSpec((None, tk, tn), lambda i, j, k: (j, k, 0)))
    elif form == "nt":
        b_spec = pl.BlockSpec((tn, tk), lambda i, j, k: (j, k))
    else:
        b_spec = pl.BlockSpec((tk, tn), lambda i, j, k: (k, j))
    e_spec = pl.BlockSpec((tm, tn), lambda i, j, k: (i, j))
    v_spec = pl.BlockSpec((1, tn), lambda i, j, k: (0, j))
    if shard_cols:
        o_spec = pl.BlockSpec((shard_cols, tm, ns), lambda i, j, k: (j, i, 0))
        o_shape = (N_DEV, M, ns)
    else:
        o_spec = e_spec
        o_shape = (M, N)
    n_ex = len(extras)
    n_in = 2 + n_ex + len(after)
    sums = [isinstance(o, tuple) for o in outs]
    assert not any(sums) or tn == N, name

    def finish(acc, ex, o_refs, row_tile):
        vals = (acc,) if epilogue is None else epilogue(acc, *[e[...] for e in ex])
        for r, v, is_sum in zip(o_refs, vals, sums):
            if is_sum:
                @pl.when(row_tile == 0)
                def _(r=r, v=v):
                    r[...] = v.astype(r.dtype)

                @pl.when(row_tile > 0)
                def _(r=r, v=v):
                    r[...] += v.astype(r.dtype)
            elif shard_cols:
                for s in range(shard_cols):
                    r[s] = v[:, s * ns:(s + 1) * ns].astype(r.dtype)
            else:
                r[...] = v.astype(r.dtype)

    def prod(a_ref, b_ref):
        if b.ndim == 3 and form == "nt":
            w = tk // per_step
            return sum(_dot(a_ref[:, s * w:(s + 1) * w], b_ref[s], dn) for s in range(1, per_step)) + _dot(a_ref[:, 0:w], b_ref[0], dn)
        av = a_ref[...]
        return _dot(av if a_map is None else a_map(av), b_ref[...], dn)

    def body_one(*refs):
        finish(prod(*refs[:2]), refs[2:2 + n_ex], refs[n_in:], pl.program_id(0))

    def body_acc(*refs):
        a_ref, b_ref = refs[:2]
        acc = refs[-1]
        k = pl.program_id(2)
        row_tile = pl.program_id(0)

        @pl.when(k == 0)
        def _():
            acc[...] = prod(a_ref, b_ref)

        @pl.when((k > 0) & (k < nk - 1))
        def _():
            acc[...] += prod(a_ref, b_ref)

        @pl.when(k == nk - 1)
        def _():
            finish(acc[...] + prod(a_ref, b_ref), refs[2:2 + n_ex], refs[n_in:-1], row_tile)

    return pl.pallas_call(
        body_one if nk == 1 else body_acc, grid=(M // tm, N // tn, nk),
        in_specs=[a_spec, b_spec] + [v_spec if e.shape[0] == 1 else e_spec for e in extras]
        + [pl.BlockSpec(memory_space=pl.ANY)] * len(after),
        out_specs=[v_spec if s else o_spec for s in sums],
        out_shape=[jax.ShapeDtypeStruct((1, N), o[1]) if s else jax.ShapeDtypeStruct(o_shape, o) for o, s in zip(outs, sums)],
        scratch_shapes=[] if nk == 1 else [pltpu.VMEM((tm, tn), f32)],
        compiler_params=_cparams(("arbitrary" if any(sums) else "parallel", "parallel", "arbitrary")), name=name,
    )(a, b, *extras, *after)


def _rms_fwd(x, w, name):
    T, D = x.shape
    tt = min(512, T)

    def body(x_ref, w_ref, h_ref):
        xv = x_ref[...]
        r = lax.rsqrt(jnp.mean(xv * xv, axis=1, keepdims=True) + EPS)
        h_ref[...] = (xv * r * w_ref[...]).astype(bf16)

    return pl.pallas_call(
        body, grid=(T // tt,),
        in_specs=[pl.BlockSpec((tt, D), lambda i: (i, 0)), pl.BlockSpec((1, D), lambda i: (0, 0))],
        out_specs=pl.BlockSpec((tt, D), lambda i: (i, 0)),
        out_shape=jax.ShapeDtypeStruct((T, D), bf16),
        compiler_params=_cparams(("parallel",)), name=name,
    )(x, w)


def _residual_rms_ep(acc, res, w):
    x = res + acc
    r = lax.rsqrt(jnp.mean(x * x, axis=1, keepdims=True) + EPS)
    return x, x * r * w


_RMS_BWD_OUTS = [f32, bf16, ("sum", f32)]


def _rms_bwd_ep(dh, x, dres, w):
    r = lax.rsqrt(jnp.mean(x * x, axis=1, keepdims=True) + EPS)
    xn = x * r
    dhw = dh * w
    dx = dres + r * (dhw - xn * jnp.mean(dhw * xn, axis=1, keepdims=True))
    return dx, dx, jnp.sum(dh * xn, axis=0, keepdims=True)


_LOSS_OUTS = [("sum", f32), f32, bf16, ("sum", f32)]


def _loss_ep(acc, res, w, tgt):
    x = res + acc
    D = x.shape[1]
    r = lax.rsqrt(jnp.mean(x * x, axis=1, keepdims=True) + EPS)
    xn = x * r
    e = xn * w - tgt
    loss = 0.5 * jnp.sum(jnp.mean(e * e, axis=1, keepdims=True), axis=0, keepdims=True)
    dy = e * (1.0 / D)
    dyw = dy * w
    dx = r * (dyw - xn * jnp.mean(dyw * xn, axis=1, keepdims=True))
    return jnp.broadcast_to(loss, (1, D)), dx, dx, jnp.sum(dy * xn, axis=0, keepdims=True)


def _ret_tables():
    H, C = RET_HEADS, RET_CHUNK
    lg = np.log1p(-np.exp2(-5.0 - np.arange(H, dtype=np.float32))).astype(np.float32)
    idx = np.arange(C, dtype=np.float32)
    diff = idx[:, None] - idx[None, :]
    causal = diff >= 0
    dm = np.where(causal[None], np.exp(lg[:, None, None] * np.where(causal, diff, 0.0)[None]), 0.0)
    qd = np.exp(lg[:, None] * (idx[None, :] + 1.0))
    kd = np.exp(lg[:, None] * (C - 1.0 - idx[None, :]))
    cg = np.exp(lg * C)
    tab = np.zeros((H, 4, C, HEAD_DIM), np.float32)
    tab[:, 0] = dm
    tab[:, 1] = qd[:, :, None]
    tab[:, 2] = kd[:, :, None]
    tab[:, 3] = cg[:, None, None]
    return jnp.asarray(tab)


def _rope_tables(T):
    half = HEAD_DIM // 2
    inv = np.float32(ROPE_THETA) ** (-np.arange(half, dtype=np.float32) / np.float32(half))
    ang = np.arange(T, dtype=np.float32)[:, None] * inv[None, :]
    c, s = np.cos(ang), np.sin(ang)
    return jnp.asarray(np.concatenate([c, c], axis=1)), jnp.asarray(np.concatenate([-s, s], axis=1))


def _rope(x, cos, sin):
    return x * cos + pltpu.roll(x, HEAD_DIM // 2, 1) * sin


def _unrope(y, cos, sin):
    return y * cos + pltpu.roll(y * sin, HEAD_DIM // 2, 1)


def _stack_heads(ref, H, f=None):
    parts = [ref[:, h * HEAD_DIM:(h + 1) * HEAD_DIM] for h in range(H)]
    return jnp.stack(parts if f is None else [f(a) for a in parts])


def _ret_fwd(p, cos, sin, tab, name):
    T = p.shape[0]
    C, H = RET_CHUNK, RET_HEADS
    N = T // C
    K = min(RET_STEP, N)
    NS = N // K
    scale = HEAD_DIM ** -0.5

    def body(q_ref, k_ref, v_ref, g_ref, c_ref, s_ref, t_ref, y_ref, o_ref, sp_ref, st):
        @pl.when(pl.program_id(0) == 0)
        def _():
            st[...] = jnp.zeros_like(st)

        dm, qd, kd, cg = t_ref[:, 0], t_ref[:, 1], t_ref[:, 2], t_ref[:, 3]
        S = st[...]
        for c in range(K):
            rows = pl.ds(c * C, C)
            cos_, sin_ = c_ref[rows, :], s_ref[rows, :]
            rot = lambda a: _rope(a, cos_, sin_)
            q = _stack_heads(q_ref.at[rows, :], H, rot)
            k = _stack_heads(k_ref.at[rows, :], H, rot) * scale
            v = _stack_heads(v_ref.at[rows, :], H)
            P = _dot(q, k, _NT3) * dm
            o = _dot(P, v, _NN3) + _dot(q * qd, S, _NN3)
            sp_ref[c] = S
            S = cg * S + _dot(k * kd, v, _TN3)
            r = lax.rsqrt(jnp.mean(o * o, axis=2, keepdims=True) + EPS)
            y = o * r * _silu(_stack_heads(g_ref.at[rows, :], H))
            for h in range(H):
                o_ref[rows, h * HEAD_DIM:(h + 1) * HEAD_DIM] = o[h]
                y_ref[rows, h * HEAD_DIM:(h + 1) * HEAD_DIM] = y[h].astype(bf16)
        st[...] = S

    wide = lambda blk: pl.BlockSpec((K * C, H * HEAD_DIM), lambda n: (n, blk))
    tbl = pl.BlockSpec((K * C, HEAD_DIM), lambda n: (n, 0))
    return pl.pallas_call(
        body, grid=(NS,),
        in_specs=[wide(0), wide(1), wide(2), wide(3), tbl, tbl,
                  pl.BlockSpec((H, 4, C, HEAD_DIM), lambda n: (0, 0, 0, 0))],
        out_specs=[wide(0), wide(0), pl.BlockSpec((K, H, HEAD_DIM, HEAD_DIM), lambda n: (n, 0, 0, 0))],
        out_shape=[jax.ShapeDtypeStruct((T, D_MODEL), bf16), jax.ShapeDtypeStruct((T, H * HEAD_DIM), f32),
                   jax.ShapeDtypeStruct((N, H, HEAD_DIM, HEAD_DIM), f32)],
        scratch_shapes=[pltpu.VMEM((H, HEAD_DIM, HEAD_DIM), f32)],
        compiler_params=_cparams(("arbitrary",)), name=name,
    )(p, p, p, p, cos, sin, tab)


def _ret_bwd(p, cos, sin, tab, o_raw, sprev, dmix, name):
    T = p.shape[0]
    C, H = RET_CHUNK, RET_HEADS
    N = T // C
    K = min(RET_STEP, N)
    NS = N // K
    scale = HEAD_DIM ** -0.5
    W = H * HEAD_DIM

    def body(q_ref, k_ref, v_ref, g_ref, c_ref, s_ref, t_ref, o_ref, sp_ref, dy_ref, d_ref, dst):
        @pl.when(pl.program_id(0) == 0)
        def _():
            dst[...] = jnp.zeros_like(dst)

        dm, qd, kd, cg = t_ref[:, 0], t_ref[:, 1], t_ref[:, 2], t_ref[:, 3]
        dS1 = dst[...]
        for c in reversed(range(K)):
            rows = pl.ds(c * C, C)
            cos_, sin_ = c_ref[rows, :], s_ref[rows, :]
            rot = lambda a: _rope(a, cos_, sin_)
            q = _stack_heads(q_ref.at[rows, :], H, rot)
            k = _stack_heads(k_ref.at[rows, :], H, rot) * scale
            v = _stack_heads(v_ref.at[rows, :], H)
            g = _stack_heads(g_ref.at[rows, :], H)
            S = sp_ref[c]
            o = _stack_heads(o_ref.at[rows, :], H)
            dy = _stack_heads(dy_ref.at[rows, :], H)
            r = lax.rsqrt(jnp.mean(o * o, axis=2, keepdims=True) + EPS)
            nrm = o * r
            dn = dy * _silu(g)
            dg = dy * nrm * _dsilu(g)
            do = r * (dn - nrm * jnp.mean(dn * nrm, axis=2, keepdims=True))
            P = _dot(q, k, _NT3) * dm
            dP = _dot(do, v, _NT3) * dm
            dq = _dot(dP, k, _NN3) + _dot(do, S, _NT3) * qd
            dk = (_dot(dP, q, _TN3) + _dot(v, dS1, _NT3) * kd) * scale
            dv = _dot(P, do, _TN3) + _dot(k * kd, dS1, _NN3)
            dS1 = cg * dS1 + _dot(q * qd, do, _TN3)
            for h in range(H):
                d_ref[rows, h * HEAD_DIM:(h + 1) * HEAD_DIM] = _unrope(dq[h], cos_, sin_).astype(bf16)
                d_ref[rows, W + h * HEAD_DIM:W + (h + 1) * HEAD_DIM] = _unrope(dk[h], cos_, sin_).astype(bf16)
                d_ref[rows, 2 * W + h * HEAD_DIM:2 * W + (h + 1) * HEAD_DIM] = dv[h].astype(bf16)
                d_ref[rows, 3 * W + h * HEAD_DIM:3 * W + (h + 1) * HEAD_DIM] = dg[h].astype(bf16)
        dst[...] = dS1

    rev = lambda blk: pl.BlockSpec((K * C, W), lambda n: (NS - 1 - n, blk))
    tbl = pl.BlockSpec((K * C, HEAD_DIM), lambda n: (NS - 1 - n, 0))
    return pl.pallas_call(
        body, grid=(NS,),
        in_specs=[rev(0), rev(1), rev(2), rev(3), tbl, tbl,
                  pl.BlockSpec((H, 4, C, HEAD_DIM), lambda n: (0, 0, 0, 0)), rev(0),
                  pl.BlockSpec((K, H, HEAD_DIM, HEAD_DIM), lambda n: (NS - 1 - n, 0, 0, 0)), rev(0)],
        out_specs=pl.BlockSpec((K * C, 4 * W), lambda n: (NS - 1 - n, 0)),
        out_shape=jax.ShapeDtypeStruct((T, 6 * W), bf16),
        scratch_shapes=[pltpu.VMEM((H, HEAD_DIM, HEAD_DIM), f32)],
        compiler_params=_cparams(("arbitrary",)), name=name,
    )(p, p, p, p, cos, sin, tab, o_raw, sprev, dmix)


CONV_K = 4
CONV_W = 512
PAD = 8
SUB_R = 64


def _conv_fwd(x, col_off, w, b, act, name):
    T = x.shape[0]
    C = w.shape[1]
    G = C // CONV_W
    tt = min(512, T)
    NT = T // tt
    has_b = b is not None

    def body(*refs):
        if has_b:
            x_ref, w_ref, b_ref, y_ref, pad = refs
        else:
            x_ref, w_ref, y_ref, pad = refs
        t = pl.program_id(1)

        @pl.when(t == 0)
        def _():
            pad[pl.ds(0, PAD), :] = jnp.zeros((PAD, CONV_W), f32)

        pad[pl.ds(PAD, tt), :] = x_ref[...]
        for g in range(CONV_W // 128):
            ls = slice(g * 128, (g + 1) * 128)
            wv = w_ref[:, ls]
            for c in range(tt // SUB_R):
                r0 = c * SUB_R
                y = wv[0:1, :] * pad[pl.ds(PAD - 3 + r0, SUB_R), ls]
                for kk in range(1, CONV_K):
                    y = y + wv[kk:kk + 1, :] * pad[pl.ds(PAD - 3 + kk + r0, SUB_R), ls]
                if has_b:
                    y = y + b_ref[:, ls]
                y_ref[pl.ds(r0, SUB_R), ls] = _silu(y) if act else y
        tail = pad[pl.ds(tt, PAD), :]
        pad[pl.ds(0, PAD), :] = tail

    in_specs = [pl.BlockSpec((tt, CONV_W), lambda g, t: (t, col_off + g)),
                pl.BlockSpec((CONV_K, CONV_W), lambda g, t: (0, g))]
    args = [x, w]
    if has_b:
        in_specs.append(pl.BlockSpec((1, CONV_W), lambda g, t: (0, g)))
        args.append(b)
    return pl.pallas_call(
        body, grid=(G, NT), in_specs=in_specs,
        out_specs=pl.BlockSpec((tt, CONV_W), lambda g, t: (t, g)),
        out_shape=jax.ShapeDtypeStruct((T, C), f32),
        scratch_shapes=[pltpu.VMEM((tt + PAD, CONV_W), f32)],
        compiler_params=_cparams(("parallel", "arbitrary")), name=name,
    )(*args)


def _conv_bwd(x, col_off, w, b, act, dout, dp, name):
    T = x.shape[0]
    C = w.shape[1]
    G = C // CONV_W
    tt = min(512, T)
    NT = T // tt
    has_b = b is not None

    def body(*refs):
        if has_b:
            x_ref, xp_ref, w_ref, b_ref, d_ref, dp_in, dx_ref, dw_ref, db_ref, pad, dpad = refs
        else:
            x_ref, xp_ref, w_ref, d_ref, dp_in, dx_ref, dw_ref, db_ref, pad, dpad = refs
        t = pl.program_id(1)
        first_tile = t == NT - 1

        @pl.when(t == 0)
        def _():
            dpad[pl.ds(tt, PAD), :] = jnp.zeros((PAD, CONV_W), f32)
            dw_ref[...] = jnp.zeros_like(dw_ref)
            db_ref[...] = jnp.zeros_like(db_ref)

        pad[pl.ds(0, PAD), :] = jnp.where(first_tile, 0.0, xp_ref[...])
        pad[pl.ds(PAD, tt), :] = x_ref[...]
        fold = lambda v: v.reshape(SUB_R // 8, 8, 128).sum(axis=0)
        for g in range(CONV_W // 128):
            ls = slice(g * 128, (g + 1) * 128)
            wv = w_ref[:, ls]
            acc = [jnp.zeros((8, 128), f32) for _ in range(CONV_K + 1)]
            for c in reversed(range(tt // SUB_R)):
                r0 = c * SUB_R
                xs = [pad[pl.ds(PAD - 3 + kk + r0, SUB_R), ls] for kk in range(CONV_K)]
                dy = d_ref[pl.ds(r0, SUB_R), ls]
                if act:
                    y = wv[0:1, :] * xs[0]
                    for kk in range(1, CONV_K):
                        y = y + wv[kk:kk + 1, :] * xs[kk]
                    if has_b:
                        y = y + b_ref[:, ls]
                    dy = dy * _dsilu(y)
                dpad[pl.ds(r0, SUB_R), ls] = dy
                dx = wv[3:4, :] * dy
                for j in range(1, CONV_K):
                    dx = dx + wv[3 - j:4 - j, :] * dpad[pl.ds(r0 + j, SUB_R), ls]
                dx_ref[pl.ds(r0, SUB_R), ls] = dx.astype(bf16)
                for kk in range(CONV_K):
                    acc[kk] = acc[kk] + fold(dy * xs[kk])
                acc[CONV_K] = acc[CONV_K] + fold(dy)
            for kk in range(CONV_K):
                dw_ref[kk:kk + 1, ls] += jnp.sum(acc[kk], axis=0, keepdims=True)
            db_ref[:, ls] += jnp.sum(acc[CONV_K], axis=0, keepdims=True)
        head = dpad[pl.ds(0, PAD), :]
        dpad[pl.ds(tt, PAD), :] = head

    rows8 = tt // PAD
    in_specs = [pl.BlockSpec((tt, CONV_W), lambda g, t: (NT - 1 - t, col_off + g)),
                pl.BlockSpec((PAD, CONV_W), lambda g, t: (jnp.maximum((NT - 1 - t) * rows8 - 1, 0), col_off + g)),
                pl.BlockSpec((CONV_K, CONV_W), lambda g, t: (0, g))]
    args = [x, x, w]
    if has_b:
        in_specs.append(pl.BlockSpec((1, CONV_W), lambda g, t: (0, g)))
        args.append(b)
    in_specs += [pl.BlockSpec((tt, CONV_W), lambda g, t: (NT - 1 - t, g)), pl.BlockSpec(memory_space=pl.ANY)]
    args += [dout, dp]
    return pl.pallas_call(
        body, grid=(G, NT), in_specs=in_specs,
        out_specs=[pl.BlockSpec((tt, CONV_W), lambda g, t: (NT - 1 - t, col_off + g)),
                   pl.BlockSpec((CONV_K, CONV_W), lambda g, t: (0, g)),
                   pl.BlockSpec((1, CONV_W), lambda g, t: (0, g))],
        out_shape=[jax.ShapeDtypeStruct(dp.shape, dp.dtype), jax.ShapeDtypeStruct((CONV_K, C), f32),
                   jax.ShapeDtypeStruct((1, C), f32)],
        input_output_aliases={len(args) - 1: 0},
        scratch_shapes=[pltpu.VMEM((tt + PAD, CONV_W), f32), pltpu.VMEM((tt + PAD, CONV_W), f32)],
        compiler_params=_cparams(("parallel", "arbitrary")), name=name,
    )(*args)


def _lru_gates(xc, wr, wi, br, bi, lam):
    r = _sigmoid(_dot(xc, wr, _NN) + br)
    i = _sigmoid(_dot(xc, wi, _NN) + bi)
    sp = _softplus(-lam)
    a = jnp.exp(-LRU_C * r * sp)
    mult = jnp.sqrt(1.0 - a * a)
    return r, i, sp, a, mult


def _lru_fwd(xc, p, y_off, wr, wi, br, bi, lam, mix, name):
    T = xc.shape[0]
    G = LRU_WIDTH // 128
    tt = min(512, T)
    NT = T // tt

    def body(x_ref, y_ref, wr_ref, wi_ref, br_ref, bi_ref, l_ref, mix_in, o_ref, h_ref, hc):
        t = pl.program_id(1)

        @pl.when(t == 0)
        def _():
            hc[...] = jnp.zeros_like(hc)

        x = x_ref[...]
        r, i, sp, a, mult = _lru_gates(x, wr_ref[...], wi_ref[...], br_ref[...], bi_ref[...], l_ref[...])
        row = lax.broadcasted_iota(jnp.int32, (tt, 128), 0)
        mult = jnp.where((row == 0) & (t == 0), 1.0, mult)
        U = x * i * mult
        A = a
        d = 1
        while d < tt:
            keep = row >= d
            Ush = jnp.where(keep, pltpu.roll(U, d, 0), 0.0)
            Ash = jnp.where(keep, pltpu.roll(A, d, 0), 1.0)
            U = A * Ush + U
            A = A * Ash
            d *= 2
        h = U + A * hc[0:1, :]
        h_ref[...] = h
        hc[...] = jnp.broadcast_to(h[tt - 1:tt, :], hc.shape)
        o_ref[...] = (h * _gelu(y_ref[...])).astype(bf16)

    tile = pl.BlockSpec((tt, 128), lambda g, t: (t, g))
    vec = pl.BlockSpec((1, 128), lambda g, t: (0, g))
    wsp = pl.BlockSpec((128, 128), lambda g, t: (g, g))
    return pl.pallas_call(
        body, grid=(G, NT),
        in_specs=[tile, pl.BlockSpec((tt, 128), lambda g, t: (t, y_off + g)), wsp, wsp, vec, vec, vec,
                  pl.BlockSpec(memory_space=pl.ANY)],
        out_specs=[pl.BlockSpec((tt, 128), lambda g, t: (t, G + g)), tile],
        out_shape=[jax.ShapeDtypeStruct(mix.shape, mix.dtype), jax.ShapeDtypeStruct((T, LRU_WIDTH), f32)],
        input_output_aliases={7: 0},
        scratch_shapes=[pltpu.VMEM((8, 128), f32)],
        compiler_params=_cparams(("parallel", "arbitrary")), name=name,
    )(xc, p, wr, wi, br, bi, lam, mix)


def _lru_bwd(xc, p, y_off, wr, wi, br, bi, lam, hs, dmix, d_off, dp, name):
    T = xc.shape[0]
    G = LRU_WIDTH // 128
    tt = min(512, T)
    NT = T // tt

    def body(x_ref, y_ref, wr_ref, wi_ref, br_ref, bi_ref, l_ref, h_ref, hp_ref, do_ref, dp_in,
             dx_ref, dy_ref, dwr_ref, dwi_ref, dbr_ref, dbi_ref, dl_ref, lc, an):
        t = pl.program_id(1)
        first_tile = t == NT - 1

        @pl.when(t == 0)
        def _():
            lc[...] = jnp.zeros_like(lc)
            an[...] = jnp.zeros_like(an)
            dwr_ref[...] = jnp.zeros_like(dwr_ref)
            dwi_ref[...] = jnp.zeros_like(dwi_ref)
            dbr_ref[...] = jnp.zeros_like(dbr_ref)
            dbi_ref[...] = jnp.zeros_like(dbi_ref)
            dl_ref[...] = jnp.zeros_like(dl_ref)

        x = x_ref[...]
        y = y_ref[...]
        wr, wi, lam_ = wr_ref[...], wi_ref[...], l_ref[...]
        r, i, sp, a, mult_raw = _lru_gates(x, wr, wi, br_ref[...], bi_ref[...], lam_)
        row = lax.broadcasted_iota(jnp.int32, (tt, 128), 0)
        t0 = (row == 0) & first_tile
        mult = jnp.where(t0, 1.0, mult_raw)
        h = h_ref[...]
        do = do_ref[...]
        dh = do * _gelu(y)
        dy_ref[...] = (do * h * _dgelu(y)).astype(bf16)
        B = jnp.where(row == tt - 1, an[0:1, :], pltpu.roll(a, tt - 1, 0))
        L = dh
        d = 1
        while d < tt:
            keep = row < tt - d
            Lsh = jnp.where(keep, pltpu.roll(L, tt - d, 0), 0.0)
            Bsh = jnp.where(keep, pltpu.roll(B, tt - d, 0), 1.0)
            L = L + B * Lsh
            B = B * Bsh
            d *= 2
        L = L + B * lc[0:1, :]
        lc[...] = jnp.broadcast_to(L[0:1, :], lc.shape)
        an[...] = jnp.broadcast_to(a[0:1, :], an.shape)
        hprev = jnp.where(first_tile, 0.0, hp_ref[...])[PAD - 1:PAD, :]
        hm1 = jnp.where(row == 0, hprev, pltpu.roll(h, 1, 0))
        da = L * hm1
        dxc = L * i * mult
        di = L * x * mult
        dmult = jnp.where(t0, 0.0, L * x * i)
        da = da - jnp.where(t0, 0.0, dmult * a / mult_raw)
        dlog_a = da * a
        dr = dlog_a * (-LRU_C) * sp
        dsp = jnp.sum(dlog_a * (-LRU_C) * r, axis=0, keepdims=True)
        dpr = dr * r * (1.0 - r)
        dpi = di * i * (1.0 - i)
        dx_ref[...] = dxc + _dot(dpr, wr, _NT) + _dot(dpi, wi, _NT)
        for d_ref, dpre in ((dwr_ref, dpr), (dwi_ref, dpi)):
            dw = _dot(x, dpre, _TN)
            for s in range(2):
                d_ref[s] += dw[s * 64:(s + 1) * 64, s * 64:(s + 1) * 64]
        dbr_ref[...] += jnp.sum(dpr, axis=0, keepdims=True)
        dbi_ref[...] += jnp.sum(dpi, axis=0, keepdims=True)
        dl_ref[...] += dsp * (-_sigmoid(-lam_))

    rows8 = tt // PAD
    tile = pl.BlockSpec((tt, 128), lambda g, t: (NT - 1 - t, g))
    vec = pl.BlockSpec((1, 128), lambda g, t: (0, g))
    wsp = pl.BlockSpec((128, 128), lambda g, t: (g, g))
    wout = pl.BlockSpec((2, 64, 64), lambda g, t: (g, 0, 0))
    return pl.pallas_call(
        body, grid=(G, NT),
        in_specs=[tile, pl.BlockSpec((tt, 128), lambda g, t: (NT - 1 - t, y_off + g)), wsp, wsp, vec, vec, vec, tile,
                  pl.BlockSpec((PAD, 128), lambda g, t: (jnp.maximum((NT - 1 - t) * rows8 - 1, 0), g)),
                  pl.BlockSpec((tt, 128), lambda g, t: (NT - 1 - t, d_off + g)), pl.BlockSpec(memory_space=pl.ANY)],
        out_specs=[tile, pl.BlockSpec((tt, 128), lambda g, t: (NT - 1 - t, y_off + g)), wout, wout, vec, vec, vec],
        out_shape=[jax.ShapeDtypeStruct((T, LRU_WIDTH), f32), jax.ShapeDtypeStruct(dp.shape, dp.dtype),
                   jax.ShapeDtypeStruct((2 * G, 64, 64), f32), jax.ShapeDtypeStruct((2 * G, 64, 64), f32),
                   jax.ShapeDtypeStruct((1, LRU_WIDTH), f32), jax.ShapeDtypeStruct((1, LRU_WIDTH), f32),
                   jax.ShapeDtypeStruct((1, LRU_WIDTH), f32)],
        input_output_aliases={10: 1},
        scratch_shapes=[pltpu.VMEM((8, 128), f32), pltpu.VMEM((8, 128), f32)],
        compiler_params=_cparams(("parallel", "arbitrary")), name=name,
    )(xc, p, wr, wi, br, bi, lam, hs, hs, dmix, dp)


_NN3 = (((2,), (1,)), ((0,), (0,)))
_NT3 = (((2,), (2,)), ((0,), (0,)))
_TN3 = (((1,), (1,)), ((0,), (0,)))


def _pairs(ref, K):
    C = GDN_CHUNK
    return jnp.stack([ref[c * C:(c + 1) * C, h * HEAD_DIM:(h + 1) * HEAD_DIM] for c in range(K) for h in range(GDN_HEADS)])


def _put_pairs(ref, val, K, col=0):
    C, H = GDN_CHUNK, GDN_HEADS
    for c in range(K):
        for h in range(H):
            ref[c * C:(c + 1) * C, col + h * HEAD_DIM:col + (h + 1) * HEAD_DIM] = val[c * H + h].astype(ref.dtype)


def _rowsum(x):
    H, C, L = x.shape
    return _dot(x.reshape(H * C, L), jnp.ones((L, HEAD_DIM), f32), _NN).reshape(H, C, HEAD_DIM)


def _gdn_pre(qr, kr, v, ba, alog, dtb):
    C, H = GDN_CHUNK, GDN_HEADS
    B = qr.shape[0]
    K = B // H
    lane = lax.broadcasted_iota(jnp.int32, (C, 128), 1)
    lane3 = lax.broadcasted_iota(jnp.int32, (B, C, 128), 2)
    ri = lax.broadcasted_iota(jnp.int32, (C, C), 0)
    ci = lax.broadcasted_iota(jnp.int32, (C, C), 1)
    rowc = lax.broadcasted_iota(jnp.int32, (C, 1), 0)
    col = lambda m, j: jnp.sum(jnp.where(lane == j, m, 0.0), axis=1, keepdims=True)
    ea = jnp.exp(alog)
    tri = (ri >= ci).astype(f32)
    g_all, beta_cols, G_cols = [], [], []
    for c in range(K):
        ba_c = ba[c * C:(c + 1) * C]
        g_c = -ea * _softplus(ba_c + dtb)
        G_c = _dot01(tri, g_c, _NN)
        s_c = _sigmoid(ba_c)
        g_all.append(g_c)
        beta_cols += [col(s_c, h) for h in range(H)]
        G_cols += [col(G_c, H + h) for h in range(H)]
    wide = lambda c: jnp.broadcast_to(c, (B, C, 128))
    beta = wide(jnp.stack(beta_cols))
    Gc = jnp.stack(G_cols)
    rq = lax.rsqrt(_rowsum(qr * qr) + EPS)
    rk = lax.rsqrt(_rowsum(kr * kr) + EPS)
    qh, kn = qr * rq, kr * rk
    qn = qh * (HEAD_DIM ** -0.5)
    Grow = _dot01(jnp.ones((B, C, 128), f32), jnp.where(lane3 == 0, Gc, 0.0), _NT3)
    incl = ri >= ci
    Di = jnp.where(incl, jnp.exp(jnp.where(incl, Gc - Grow, 0.0)), 0.0)
    Ds = jnp.where(ri > ci, Di, 0.0)
    Gl = jnp.sum(jnp.where(rowc == C - 1, Gc, 0.0), axis=1, keepdims=True)
    eG = wide(jnp.exp(Gc))
    eGl = wide(jnp.exp(Gl - Gc))
    cd = jnp.exp(Gl)
    kb = kn * beta
    vb = v * beta
    Lm = _dot(kb, kn, _NT3) * Ds
    kbg = kb * eG
    QK = _dot(qn, kn, _NT3) * Di
    qg = qn * eG
    kg = kn * eGl
    return dict(beta=beta, g_all=g_all, rq=rq, rk=rk, qh=qh, kn=kn, qn=qn, Di=Di, Ds=Ds, eG=eG, eGl=eGl, cd=cd,
                kb=kb, vb=vb, Lm=Lm, kbg=kbg, QK=QK, qg=qg, kg=kg, lane=lane, ri=ri, ci=ci, rowc=rowc, ea=ea)


def _unit_lower_inverse(Lm):
    C = Lm.shape[-1]
    ri = lax.broadcasted_iota(jnp.int32, (C, C), 0)
    ci = lax.broadcasted_iota(jnp.int32, (C, C), 1)
    same = lambda s: (ri // s) == (ci // s)
    Xd = jnp.where(same(8), -Lm, 0.0)
    Tinv = (ri == ci).astype(f32) + Xd
    Pw = Xd
    for _ in range(2):
        Pw = _dot(Pw, Pw, _NN3)
        Tinv = Tinv + _dot(Tinv, Pw, _NN3)
    for s in (8, 16, 32):
        off = jnp.where(same(2 * s) & jnp.logical_not(same(s)), Lm, 0.0)
        Tinv = Tinv - _dot(_dot(Tinv, off, _NN3), Tinv, _NN3)
    return Tinv


def _gdn_specs(T, rev):
    C = GDN_CHUNK
    H = GDN_HEADS
    K = min(GDN_STEP, T // C)
    NS = T // (C * K)
    nn = (lambda n: NS - 1 - n) if rev else (lambda n: n)
    wide = lambda blk: pl.BlockSpec((K * C, H * HEAD_DIM), lambda n: (nn(n), blk))
    one = lambda off: pl.BlockSpec((K * C, HEAD_DIM), lambda n: (nn(n), off))
    vec = pl.BlockSpec((1, 128), lambda n: (0, 0))
    st = lambda rows: pl.BlockSpec((K, H, rows, rows), lambda n: (nn(n), 0, 0, 0))
    return K, NS, wide, one, vec, st


def _gdn_fwd(qkv, p, alog, dtb, nw, name):
    T = qkv.shape[0]
    C, H = GDN_CHUNK, GDN_HEADS
    N = T // C
    K, NS, wide, one, vec, st_spec = _gdn_specs(T, False)

    def body(q_ref, k_ref, v_ref, z_ref, ba_ref, al_ref, dt_ref, nw_ref, y_ref, sp_ref, ti_ref, vn_ref, o_ref, st):
        @pl.when(pl.program_id(0) == 0)
        def _():
            st[...] = jnp.zeros_like(st)

        f = _gdn_pre(_pairs(q_ref, K), _pairs(k_ref, K), _pairs(v_ref, K), ba_ref[...], al_ref[...], dt_ref[...])
        Tinv = _unit_lower_inverse(f["Lm"])
        ti_ref[...] = Tinv.reshape(K, H, C, C).astype(bf16)
        w = _dot(Tinv, f["kbg"], _NN3)
        u = _dot(Tinv, f["vb"], _NN3)
        S = st[...]
        vns, os_ = [], []
        for c in range(K):
            sl = slice(c * H, (c + 1) * H)
            sp_ref[c] = S
            vn_c = u[sl] - _dot(w[sl], S, _NN3)
            os_.append(_dot(f["qg"][sl], S, _NN3) + _dot(f["QK"][sl], vn_c, _NN3))
            S = S * f["cd"][sl] + _dot(f["kg"][sl], vn_c, _TN3)
            vns.append(vn_c)
        st[...] = S
        vn, o = jnp.concatenate(vns), jnp.concatenate(os_)
        r = lax.rsqrt(_rowsum(o * o) * (1.0 / HEAD_DIM) + EPS)
        _put_pairs(y_ref, o * r * nw_ref[...] * _silu(_pairs(z_ref, K)), K)
        _put_pairs(vn_ref, vn, K)
        _put_pairs(o_ref, o, K)

    wide_f32 = jax.ShapeDtypeStruct((T, H * HEAD_DIM), f32)
    return pl.pallas_call(
        body, grid=(NS,),
        in_specs=[wide(0), wide(1), wide(2), wide(3), one(4 * H), vec, vec, vec],
        out_specs=[wide(0), st_spec(HEAD_DIM), st_spec(C), wide(0), wide(0)],
        out_shape=[jax.ShapeDtypeStruct((T, H * HEAD_DIM), bf16), jax.ShapeDtypeStruct((N, H, HEAD_DIM, HEAD_DIM), f32),
                   jax.ShapeDtypeStruct((N, H, C, C), bf16), jax.ShapeDtypeStruct((T, H * HEAD_DIM), bf16), wide_f32],
        scratch_shapes=[pltpu.VMEM((H, HEAD_DIM, HEAD_DIM), f32)],
        compiler_params=_cparams(("arbitrary",)), name=name,
    )(qkv, qkv, qkv, p, p, alog, dtb, nw)


def _gdn_bwd(qkv, p, alog, dtb, nw, sprev, tinv, vn_all, o_all, dy_all, name):
    T = qkv.shape[0]
    C, H = GDN_CHUNK, GDN_HEADS
    N = T // C
    K, NS, wide, one, vec, st_spec = _gdn_specs(T, True)
    rs = lambda m: jnp.sum(m, axis=2, keepdims=True)

    def body(q_ref, k_ref, v_ref, z_ref, ba_ref, al_ref, dt_ref, nw_ref, sp_ref, ti_ref, vn_ref, o_ref, dy_ref,
             dqkv_ref, dz_ref, dba_ref, dal_ref, ddt_ref, dnw_ref, dst):
        @pl.when(pl.program_id(0) == 0)
        def _():
            dst[...] = jnp.zeros_like(dst)
            dal_ref[...] = jnp.zeros_like(dal_ref)
            ddt_ref[...] = jnp.zeros_like(ddt_ref)
            dnw_ref[...] = jnp.zeros_like(dnw_ref)

        ba, dtb_, nwv = ba_ref[...], dt_ref[...], nw_ref[...]
        v = _pairs(v_ref, K)
        f = _gdn_pre(_pairs(q_ref, K), _pairs(k_ref, K), v, ba, al_ref[...], dtb_)
        beta, kn, qn, kb, vb, kbg = f["beta"], f["kn"], f["qn"], f["kb"], f["vb"], f["kbg"]
        eG, eGl, cd, Di, Ds, QK, qg, kg = f["eG"], f["eGl"], f["cd"], f["Di"], f["Ds"], f["QK"], f["qg"], f["kg"]
        lane, ri, ci, rowc = f["lane"], f["ri"], f["ci"], f["rowc"]
        Tinv = ti_ref[...].reshape(K * H, C, C)
        S = sp_ref[...].reshape(K * H, HEAD_DIM, HEAD_DIM)
        w_ = _dot(Tinv, kbg, _NN3)
        vn, o = _pairs(vn_ref, K), _pairs(o_ref, K)
        z, dy = _pairs(z_ref, K), _pairs(dy_ref, K)
        r = lax.rsqrt(_rowsum(o * o) * (1.0 / HEAD_DIM) + EPS)
        nrm = o * r
        sz = _silu(z)
        dn = dy * nwv * sz
        _put_pairs(dz_ref, dy * nrm * nwv * _dsilu(z), K)
        dnw_ref[...] += jnp.sum(jnp.sum(dy * nrm * sz, axis=0), axis=0, keepdims=True)
        do = r * (dn - nrm * (_rowsum(dn * nrm) * (1.0 / HEAD_DIM)))
        dvn_do = _dot(QK, do, _TN3)
        dS_do = _dot(qg, do, _TN3)
        dqg = _dot(do, S, _NT3)
        dQK = _dot(do, vn, _NT3)
        dS = dst[...]
        dS1s, dvns = [None] * K, [None] * K
        for c in reversed(range(K)):
            sl = slice(c * H, (c + 1) * H)
            dS1s[c] = dS
            dvns[c] = _dot(kg[sl], dS, _NN3) + dvn_do[sl]
            dS = cd[sl] * dS + dS_do[sl] - _dot(w_[sl], dvns[c], _TN3)
        dst[...] = dS
        dS1, dvn = jnp.concatenate(dS1s), jnp.concatenate(dvns)
        dcd = jnp.sum(jnp.sum(S * dS1, axis=2, keepdims=True), axis=1, keepdims=True)
        dkg = _dot(vn, dS1, _NT3)
        dw = -_dot(dvn, S, _NT3)
        dqn = dqg * eG
        dkn = dkg * eGl
        deGl = rs(dkg * kn)
        dQKr = dQK * Di
        E = dQK * QK
        dqn = dqn + _dot(dQKr, kn, _NN3)
        dkn = dkn + _dot(dQKr, qn, _TN3)
        dT = _dot(dvn, vb, _NT3) + _dot(dw, kbg, _NT3)
        dvb = _dot(Tinv, dvn, _TN3)
        dkbg = _dot(Tinv, dw, _TN3)
        dkb = dkbg * eG
        deG = rs(dqg * qn + dkbg * kb)
        dL = -_dot(_dot(Tinv, dT, _TN3), Tinv, _NT3)
        dKK = dL * Ds
        E = E + dL * f["Lm"]
        dkb = dkb + _dot(dKK, kn, _NN3)
        dkn = dkn + _dot(dKK, kb, _TN3) + dkb * beta
        dbeta = rs(dkb * kn + dvb * v)
        _put_pairs(dqkv_ref, dvb * beta, K, 2 * H * HEAD_DIM)
        dG = rs(E) - rs(jnp.swapaxes(E, 1, 2)) + deG * eG - deGl * eGl
        dGl = jnp.sum(deGl * eGl, axis=1, keepdims=True) + dcd * cd
        dG = dG + jnp.where(rowc == C - 1, dGl, 0.0)
        qh = f["qh"]
        _put_pairs(dqkv_ref, (HEAD_DIM ** -0.5) * f["rq"] * (dqn - qh * _rowsum(dqn * qh)), K)
        _put_pairs(dqkv_ref, f["rk"] * (dkn - kn * _rowsum(dkn * kn)), K, H * HEAD_DIM)
        db = dbeta * beta * (1.0 - beta)
        triu = (ri <= ci).astype(f32)
        for c in range(K):
            db_all = jnp.where(lane == 0, db[c * H], 0.0)
            dG_all = jnp.where(lane == H, dG[c * H], 0.0)
            for h in range(1, H):
                db_all = db_all + jnp.where(lane == h, db[c * H + h], 0.0)
                dG_all = dG_all + jnp.where(lane == H + h, dG[c * H + h], 0.0)
            dg_all = _dot01(triu, dG_all, _NN)
            da_all = dg_all * (-f["ea"]) * _sigmoid(ba[c * C:(c + 1) * C] + dtb_)
            dba_ref[c * C:(c + 1) * C, :] = (db_all + da_all).astype(bf16)
            ddt_ref[...] += jnp.sum(da_all, axis=0, keepdims=True)
            dal_ref[...] += jnp.sum(dg_all * f["g_all"][c], axis=0, keepdims=True)

    small = jax.ShapeDtypeStruct((1, 128), f32)
    return pl.pallas_call(
        body, grid=(NS,),
        in_specs=[wide(0), wide(1), wide(2), wide(3), one(4 * H), vec, vec, vec, st_spec(HEAD_DIM), st_spec(C),
                  wide(0), wide(0), wide(0)],
        out_specs=[pl.BlockSpec((K * C, 3 * H * HEAD_DIM), lambda n: (NS - 1 - n, 0)), wide(3), one(0), vec, vec, vec],
        out_shape=[jax.ShapeDtypeStruct((T, 3 * H * HEAD_DIM), f32), jax.ShapeDtypeStruct((T, ODD_PAD), bf16),
                   jax.ShapeDtypeStruct((T, 128), bf16), small, small, small],
        scratch_shapes=[pltpu.VMEM((H, HEAD_DIM, HEAD_DIM), f32)],
        compiler_params=_cparams(("arbitrary",)), name=name,
    )(qkv, qkv, qkv, p, p, alog, dtb, nw, sprev, tinv, vn_all, o_all, dy_all)


def _lanes_from(x, s):
    return x if s % 128 == 0 else pltpu.roll(x, (128 - s) % 128, 1)


def _odd_assemble(g, name):
    R = g.shape[1]
    tr = min(256, R)
    n_blk = ODD_SHARD_PAD // 128

    def body(g_ref, o_ref):
        lane = lax.broadcasted_iota(jnp.int32, (tr, 128), 1)
        blk = lambda d, m: g_ref[d, :, m * 128:(m + 1) * 128]
        for gb in range(ODD_PAD // 128):
            c0 = 128 * gb
            if c0 >= ODD_IN:
                o_ref[:, c0:c0 + 128] = jnp.zeros((tr, 128), g.dtype)
                continue
            d0 = c0 // ODD_SHARD
            m0, sh = divmod(c0 - ODD_SHARD * d0, 128)
            take = min(128, ODD_SHARD * (d0 + 1) - c0)
            p = _lanes_from(blk(d0, m0), sh)
            if sh and m0 + 1 < n_blk:
                p = jnp.where(lane < 128 - sh, p, _lanes_from(blk(d0, m0 + 1), sh))
            if take < 128:
                nxt = pltpu.roll(blk(d0 + 1, 0), take, 1) if d0 + 1 < N_DEV else jnp.zeros((tr, 128), g.dtype)
                p = jnp.where(lane < take, p, nxt)
            o_ref[:, c0:c0 + 128] = p

    return pl.pallas_call(
        body, grid=(R // tr,),
        in_specs=[pl.BlockSpec((N_DEV, tr, ODD_SHARD_PAD), lambda i: (0, i, 0))],
        out_specs=pl.BlockSpec((tr, ODD_PAD), lambda i: (i, 0)),
        out_shape=jax.ShapeDtypeStruct((R, ODD_PAD), g.dtype),
        compiler_params=_cparams(("parallel",)), name=name,
    )(g)


def _odd_split(w, name):
    R = w.shape[0]
    tr = min(256, R)

    def body(w_ref, o_ref):
        lane = lax.broadcasted_iota(jnp.int32, (tr, 128), 1)
        blk = lambda gb: w_ref[:, gb * 128:(gb + 1) * 128]
        for d in range(N_DEV):
            for m in range(ODD_SHARD_PAD // 128):
                g0, sh = divmod(ODD_SHARD * d + 128 * m, 128)
                p = _lanes_from(blk(g0), sh)
                if sh and g0 + 1 < ODD_PAD // 128:
                    p = jnp.where(lane < 128 - sh, p, _lanes_from(blk(g0 + 1), sh))
                real = ODD_SHARD - 128 * m
                if real < 128:
                    p = jnp.where(lane < real, p, jnp.zeros_like(p))
                o_ref[d, :, m * 128:(m + 1) * 128] = p

    return pl.pallas_call(
        body, grid=(R // tr,),
        in_specs=[pl.BlockSpec((tr, ODD_PAD), lambda i: (i, 0))],
        out_specs=pl.BlockSpec((N_DEV, tr, ODD_SHARD_PAD), lambda i: (0, i, 0)),
        out_shape=jax.ShapeDtypeStruct((N_DEV, R, ODD_SHARD_PAD), w.dtype),
        compiler_params=_cparams(("parallel",)), name=name,
    )(w)


def _adam_tile(g, w_ref, m_ref, v_ref, go_ref, d_ref, mo_ref, vo_ref):
    c1 = 1.0 - ADAM_B1 ** ADAM_STEP
    c2 = 1.0 - ADAM_B2 ** ADAM_STEP
    mn = ADAM_B1 * m_ref[...] + (1.0 - ADAM_B1) * g
    vn = ADAM_B2 * v_ref[...] + (1.0 - ADAM_B2) * (g * g)
    go_ref[...] = g
    mo_ref[...] = mn
    vo_ref[...] = vn
    d_ref[...] = -ADAM_LR * ((mn / c1) / (jnp.sqrt(vn / c2) + ADAM_EPS) + ADAM_WD * w_ref[...])


def _adamw(w, gs, m, v, name, layer=None, prev=None):
    R, Cc = w.shape[-2:]
    S = gs.shape[0]
    tr = R
    if S * R * Cc * 4 > (4 << 20):
        for cand in (256, 128, 64, 32, 16, 8):
            if R % cand == 0 and R > cand:
                tr = cand
                break

    def body(w_ref, g_ref, m_ref, v_ref, *rest):
        g = g_ref[0].astype(f32)
        for s in range(1, S):
            g = g + g_ref[s].astype(f32)
        _adam_tile(g, w_ref, m_ref, v_ref, *rest[-4:])

    if layer is None:
        blk = pl.BlockSpec((tr, Cc), lambda i: (i, 0))
    else:
        blk = pl.BlockSpec((None, tr, Cc), lambda i: (layer, i, 0))
    out = jax.ShapeDtypeStruct(w.shape, f32)
    carried = [] if prev is None else list(prev)
    return pl.pallas_call(
        body, grid=(R // tr,),
        in_specs=[blk, pl.BlockSpec((S, tr, Cc), lambda i: (0, i, 0)), blk, blk]
        + [pl.BlockSpec(memory_space=pl.ANY)] * len(carried),
        out_specs=[blk] * 4, out_shape=[out] * 4,
        input_output_aliases={4 + j: j for j in range(len(carried))},
        compiler_params=_cparams(("parallel",)), name=name,
    )(w, gs, m, v, *carried)


def _adamw_column_major(w, gs, m, v, name):
    _, R, Cc = w.shape
    S = gs.shape[0]
    q = R // 128
    dense = lambda a: jnp.transpose(a, (2, 0, 1)).reshape(Cc * q, 128)
    back = lambda a: jnp.transpose(a.reshape(Cc, q, 128), (1, 2, 0)).reshape(1, R, Cc)

    def body(w_ref, g_ref, m_ref, v_ref, go_ref, d_ref, mo_ref, vo_ref, gt):
        for i in range(q):
            g = g_ref[0, i * 128:(i + 1) * 128, :].astype(f32)
            for s in range(1, S):
                g = g + g_ref[s, i * 128:(i + 1) * 128, :].astype(f32)
            gt[pl.ds(i, 128, stride=q), :] = g.T
        _adam_tile(gt[...], w_ref, m_ref, v_ref, go_ref, d_ref, mo_ref, vo_ref)

    blk = pl.BlockSpec((128 * q, 128), lambda j: (j, 0))
    outs = pl.pallas_call(
        body, grid=(pl.cdiv(Cc, 128),),
        in_specs=[blk, pl.BlockSpec((S, R, 128), lambda j: (0, 0, j)), blk, blk],
        out_specs=[blk] * 4, out_shape=[jax.ShapeDtypeStruct((Cc * q, 128), f32)] * 4,
        scratch_shapes=[pltpu.VMEM((128 * q, 128), f32)],
        compiler_params=_cparams(("parallel",)), name=name,
    )(dense(w), gs, dense(m), dense(v))
    return [back(o) for o in outs]


def _me():
    x, y, c = lax.axis_index("x"), lax.axis_index("y"), lax.axis_index("c")
    return x, y, c, 4 * x + 2 * y + c


def _peer(k):
    x, y, c, _ = _me()
    px = 1 - x if k & 4 else x
    py = 1 - y if k & 2 else y
    pc = 1 - c if k & 1 else c
    return (px, py, pc), 4 * px + 2 * py + pc


_HBM = pl.BlockSpec(memory_space=pltpu.HBM)
_SEM = pl.BlockSpec(memory_space=pltpu.SEMAPHORE)
_EFFECT = pltpu.SideEffectType.DATAFLOW_SIDE_EFFECTING


def _copy(src, land, ssem, rsem, k, blocked, landing_slot_of_peer):
    pid, pidx = _peer(k)
    slot = pidx if landing_slot_of_peer else _me()[3]
    return pltpu.make_async_remote_copy(src_ref=src.at[pidx] if blocked else src, dst_ref=land.at[slot],
                                        send_sem=ssem.at[k - 1], recv_sem=rsem.at[k - 1], device_id=pid, device_id_type=MESH)


def _own_copy(src, land, rsem, blocked):
    me = _me()[3]
    return pltpu.make_async_copy(src.at[me] if blocked else src, land.at[me], rsem.at[N_DEV - 1])


_ALL_PEERS = tuple(range(1, N_DEV))
_SAME_CORE_PEERS = (1, 2, 4, 6)
_OTHER_CORE_PEERS = (3, 5, 7)


def _relay_copies(land, ssem, rsem):
    sibling, _ = _peer(1)
    copies = []
    for i in range(len(land)):
        for j, k in enumerate(_OTHER_CORE_PEERS):
            sems = dict(send_sem=ssem.at[3 * i + j], recv_sem=rsem.at[3 * i + j], device_id=sibling, device_id_type=MESH)
            _, outgoing = _peer(k - 1)
            _, incoming = _peer(k)
            copies.append((pltpu.make_async_remote_copy(src_ref=land[i].at[outgoing], dst_ref=land[i].at[outgoing], **sems),
                           pltpu.make_async_remote_copy(src_ref=land[i].at[incoming], dst_ref=land[i].at[incoming], **sems)))
    return copies


def _relay_start(lands, name):
    n = len(lands)

    def body(*refs):
        for send, _ in _relay_copies(refs[:n], refs[n], refs[n + 1]):
            send.start()
        refs[-1][...] = jnp.zeros_like(refs[-1])

    sem = pltpu.SemaphoreType.DMA((3 * n,))
    res = pl.pallas_call(
        body, name=name,
        out_shape=(sem, sem) + tuple(pltpu.HBM(a.shape, a.dtype) for a in lands) + (jax.ShapeDtypeStruct((8, 128), f32),),
        in_specs=(_HBM,) * n, out_specs=(_SEM, _SEM) + (_HBM,) * n + (pl.BlockSpec(memory_space=pltpu.VMEM),),
        input_output_aliases={i: 2 + i for i in range(n)},
        compiler_params=pltpu.CompilerParams(has_side_effects=_EFFECT),
    )(*lands)
    return res[:-1], res[-1]


def _relay_wait(handle, after, name):
    ssem, rsem, lands = handle[0], handle[1], handle[2:]
    n = len(lands)
    after = tuple(after) if isinstance(after, (tuple, list)) else (after,)

    def body(*refs):
        for send, recv in _relay_copies(refs[:n], refs[n], refs[n + 1]):
            send.wait_send()
            recv.wait_recv()

    return pl.pallas_call(
        body, name=name, out_shape=tuple(pltpu.HBM(a.shape, a.dtype) for a in lands),
        in_specs=(_HBM,) * n + (_SEM, _SEM) + (pl.BlockSpec(memory_space=pl.ANY),) * len(after), out_specs=(_HBM,) * n,
        input_output_aliases={i: i for i in range(n)}, compiler_params=pltpu.CompilerParams(has_side_effects=_EFFECT),
    )(*lands, ssem, rsem, *after)


def _send_start(srcs, blocked, name, relayed=()):
    n = len(srcs)
    lands = [lax.empty(a.shape if blocked else (N_DEV,) + a.shape, a.dtype) for a in srcs]

    def body(*refs):
        src, land, sems, token = refs[:n], refs[n:2 * n], refs[2 * n:4 * n], refs[-1]
        for i in range(n):
            for k in (_SAME_CORE_PEERS if i in relayed else _ALL_PEERS):
                _copy(src[i], land[i], sems[2 * i], sems[2 * i + 1], k, blocked, False).start()
        for i in range(n):
            _own_copy(src[i], land[i], sems[2 * i + 1], blocked).start()
        token[...] = jnp.zeros_like(token)

    sems = (pltpu.SemaphoreType.DMA((N_DEV - 1,)), pltpu.SemaphoreType.DMA((N_DEV,)))
    hbm = lambda a: pltpu.with_memory_space_constraint(a, pltpu.HBM)
    res = pl.pallas_call(
        body, name=name,
        out_shape=sems * n + tuple(pltpu.HBM(a.shape, a.dtype) for a in srcs + lands)
        + (jax.ShapeDtypeStruct((8, 128), f32),),
        in_specs=(_HBM,) * (2 * n),
        out_specs=(_SEM,) * (2 * n) + (_HBM,) * (2 * n) + (pl.BlockSpec(memory_space=pltpu.VMEM),),
        input_output_aliases={j: 2 * n + j for j in range(2 * n)},
        compiler_params=pltpu.CompilerParams(has_side_effects=_EFFECT),
    )(*[hbm(a) for a in srcs], *[hbm(a) for a in lands])
    handles = [(res[2 * i], res[2 * i + 1], res[2 * n + i], res[3 * n + i]) for i in range(n)]
    return handles, res[-1]


def _send_wait(handle, blocked, after, name, relayed=False):
    ssem, rsem, src, land = handle
    after = tuple(after) if isinstance(after, (tuple, list)) else (after,)

    def body(src_ref, land_ref, ssem_ref, rsem_ref, *rest):
        for k in (_SAME_CORE_PEERS if relayed else _ALL_PEERS):
            cp = _copy(src_ref, land_ref, ssem_ref, rsem_ref, k, blocked, True)
            cp.wait_send()
            cp.wait_recv()
        _own_copy(src_ref, land_ref, rsem_ref, blocked).wait()

    return pl.pallas_call(
        body, name=name, out_shape=(pltpu.HBM(src.shape, src.dtype), pltpu.HBM(land.shape, land.dtype)),
        in_specs=(_HBM, _HBM, _SEM, _SEM) + (pl.BlockSpec(memory_space=pl.ANY),) * len(after), out_specs=(_HBM, _HBM),
        input_output_aliases={0: 0, 1: 1}, compiler_params=pltpu.CompilerParams(has_side_effects=_EFFECT),
    )(src, land, ssem, rsem, *after)


def _block_diag(w):
    nb, bs = w.shape[0], w.shape[1]
    eye = jnp.eye(nb, dtype=w.dtype)
    return (eye[:, None, :, None] * w[:, :, None, :]).reshape(nb * bs, nb * bs)


_SQUARE_TILES = dict(tm=1024, tn=1024, tk=1024)


def _mlp_fwd(x, hm, wu, wd, tag, epilogue, extras, outs, between=None):
    (r,) = _matmul(hm, wu, "nn", outs=[bf16], epilogue=lambda acc: (jnp.maximum(acc, 0.0),), name=f"mlp_up_{tag}")
    res = _matmul(r, wd, "nn", outs=outs, extras=(x,) + tuple(extras), epilogue=epilogue, a_map=jnp.square,
                  after=() if between is None else (between(r),), name=f"mlp_down_{tag}", **_SQUARE_TILES)
    return res, (hm, r)


def _mlp_bwd(x, nw, wu, wd, saved, dxo, dxo_b, tag, sink):
    hm, r = saved
    (du,) = _matmul(dxo_b, wd, "nt", outs=[bf16], extras=(r,), epilogue=lambda acc, rr: (acc * (2.0 * rr.astype(f32)),),
                    name=f"mlp_dact_{tag}")
    (dwd,) = _matmul(r, dxo_b, "tn", outs=[bf16], a_map=jnp.square, name=f"mlp_dwd_{tag}", **_SQUARE_TILES)
    tok = sink({f"w_down{tag}": dwd.reshape(N_DEV, D_FF // N_DEV, D_MODEL)})
    (dwu,) = _matmul(hm, du, "tn", outs=[bf16], shard_cols=2, after=(tok,), name=f"mlp_dwu_{tag}")
    tok = sink({f"w_up{tag}": dwu})
    return _matmul(du, wu, "nt", outs=_RMS_BWD_OUTS, extras=(x, dxo, nw), epilogue=_rms_bwd_ep, after=(tok,),
                   name=f"mlp_dh_{tag}", **_SQUARE_TILES)


def _local_step(x, tgt, P, weight, sink, ahead):
    T = x.shape[0]
    cos, sin = _rope_tables(T)
    rtab = _ret_tables()
    row = lambda a: a.reshape(1, -1)
    mix_nw, mlp_nw = P["mixer_norm_w"], P["mlp_norm_w"]
    wr_bd, wi_bd = _block_diag(P["lru_w_r"]), _block_diag(P["lru_w_i"])
    lru_b, lru_br, lru_bi, lru_lam = row(P["lru_conv_b"]), row(P["lru_b_r"]), row(P["lru_b_i"]), row(P["lru_lambda"])
    pad16 = lambda a: jnp.pad(a.reshape(1, GDN_HEADS), ((0, 0), (GDN_HEADS, 128 - 2 * GDN_HEADS)))
    alog, dtb = pad16(P["gdn_a_log"]), pad16(P["gdn_dt_bias"])
    gnw = row(P["gdn_norm_w"])

    x0 = x
    h0 = _rms_fwd(x0, mix_nw[0:1], "rms_mix_0")
    w_ie = weight("w_in_even", (h0, cos, sin, wr_bd, wi_bd))
    (pe,) = _matmul(h0, w_ie, "nn", outs=[f32], name="in_even")
    mix0, o_ret, s_ret = _ret_fwd(pe, cos, sin, rtab, "ret_fwd")
    w_lc = weight("lru_conv_w", pe)
    xc = _conv_fwd(pe, 4, w_lc, lru_b, False, "lru_conv_fwd")
    mix0, h_lru = _lru_fwd(xc, pe, 20, wr_bd, wi_bd, lru_br, lru_bi, lru_lam, mix0, "lru_fwd")
    w_oe = weight("w_out_even", mix0)
    tok = ahead(("w_up0", "w_down0"), mix0)
    x1, hm0 = _matmul(mix0, w_oe, "nn", outs=[f32, bf16], extras=(x0, mlp_nw[0:1]), epilogue=_residual_rms_ep,
                      after=(tok,), name="out_even", tm=1024, tn=D_MODEL)
    w_u0, w_d0 = weight("w_up0", x1), weight("w_down0", x1)
    (x2, h1), mlp0 = _mlp_fwd(x1, hm0, w_u0, w_d0, "0", _residual_rms_ep, (mix_nw[1:2],), [f32, bf16],
                              between=lambda r: ahead(("w_in_odd",), r))
    w_io = weight("w_in_odd", h1)
    (po,) = _matmul(h1, w_io, "nn", outs=[f32], tm=2048, tn=ODD_PAD // 3, name="in_odd")
    w_gc = weight("gdn_conv_w", po)
    qkv = _conv_fwd(po, 0, w_gc, None, True, "gdn_conv_fwd")
    y_gdn, s_gdn, ti_gdn, vn_gdn, o_gdn = _gdn_fwd(qkv, po, alog, dtb, gnw, "gdn_fwd")
    w_oo = weight("w_out_odd", y_gdn)
    x3, hm1 = _matmul(y_gdn, w_oo, "nn", outs=[f32, bf16], extras=(x2, mlp_nw[1:2]), epilogue=_residual_rms_ep,
                      name="out_odd", tm=1024, tn=D_MODEL)
    w_u1, w_d1 = weight("w_up1", x3), weight("w_down1", x3)
    (loss, dx4, dx4_b, d_final), mlp1 = _mlp_fwd(x3, hm1, w_u1, w_d1, "1", _loss_ep, (row(P["final_norm_w"]), tgt),
                                                 _LOSS_OUTS)
    dx3, dx3_b, d_mlp_nw1 = _mlp_bwd(x3, mlp_nw[1:2], w_u1, w_d1, mlp1, dx4, dx4_b, "1", sink)
    (dy_gdn,) = _matmul(dx3_b, w_oo, "nt", outs=[f32], name="out_odd_dx")
    (d_woo,) = _matmul(y_gdn, dx3_b, "tn", outs=[bf16], name="out_odd_dw")
    dqkv, dpo, dba, d_alog, d_dtb, d_gnw = _gdn_bwd(qkv, po, alog, dtb, gnw, s_gdn, ti_gdn, vn_gdn, o_gdn, dy_gdn,
                                                  "gdn_bwd")
    dpo, d_gconv, _ = _conv_bwd(po, 0, w_gc, None, True, dqkv, dpo, "gdn_conv_bwd")
    dpo = lax.dynamic_update_slice(dpo, dba, (0, 4 * D_MODEL))
    (d_wio,) = _matmul(h1, dpo, "tn", outs=[bf16], tn=ODD_PAD // 3, name="in_odd_dw")
    tok = sink(dict(w_out_odd=d_woo.reshape(N_DEV, D_MODEL // N_DEV, D_MODEL), w_in_odd=_odd_split(d_wio, "w_in_odd_split")))
    dx2, dx2_b, d_mix_nw1 = _matmul(dpo, w_io, "nt", outs=_RMS_BWD_OUTS, extras=(x2, dx3, mix_nw[1:2]),
                                    epilogue=_rms_bwd_ep, after=(tok,), tm=1024, tn=1024, tk=ODD_PAD // 3, name="in_odd_dx")
    dx1, dx1_b, d_mlp_nw0 = _mlp_bwd(x1, mlp_nw[0:1], w_u0, w_d0, mlp0, dx2, dx2_b, "0", sink)
    (d_woe,) = _matmul(mix0, dx1_b, "tn", outs=[bf16], name="out_even_dw")
    tok = sink(dict(w_out_even=d_woe.reshape(N_DEV, D_MODEL // N_DEV, D_MODEL)))
    (dmix0,) = _matmul(dx1_b, w_oe, "nt", outs=[f32], name="out_even_dx")
    dpe = _ret_bwd(pe, cos, sin, rtab, o_ret, s_ret, dmix0, "ret_bwd")
    dxc, dpe, d_wr, d_wi, d_br, d_bi, d_lam = _lru_bwd(xc, pe, 20, wr_bd, wi_bd, lru_br, lru_bi, lru_lam + tok[0:1, 0:1],
                                                       h_lru, dmix0, 4, dpe, "lru_bwd")
    dpe, d_lconv, d_lconv_b = _conv_bwd(pe, 4, w_lc, lru_b, False, dxc, dpe, "lru_conv_bwd")
    G = dict(
        mlp_norm_w=jnp.concatenate([d_mlp_nw0, d_mlp_nw1], axis=0),
        final_norm_w=d_final.reshape(-1),
        lru_conv_w=d_lconv, lru_conv_b=d_lconv_b.reshape(-1),
        lru_w_r=d_wr, lru_b_r=d_br.reshape(-1), lru_w_i=d_wi, lru_b_i=d_bi.reshape(-1),
        lru_lambda=d_lam.reshape(-1), gdn_conv_w=d_gconv,
        gdn_a_log=d_alog[0, GDN_HEADS:2 * GDN_HEADS], gdn_dt_bias=d_dtb[0, GDN_HEADS:2 * GDN_HEADS],
        gdn_norm_w=d_gnw.reshape(-1),
    )
    packed = _pack([G[k] for k in _SMALL] + [d_lconv, d_gconv, loss[0, 0:1]])
    tok = sink(dict(small=jnp.broadcast_to(packed[None], (N_DEV,) + packed.shape)))
    (d_wie,) = _matmul(h0, dpe, "tn", outs=[bf16], shard_cols=2, after=(tok,), name="in_even_dw")
    tok = sink(dict(w_in_even=d_wie))
    dx0, _, d_mix_nw0 = _matmul(dpe, w_ie, "nt", outs=_RMS_BWD_OUTS, extras=(x0, dx1, mix_nw[0:1]), epilogue=_rms_bwd_ep,
                                after=(tok,), name="in_even_dx", **_SQUARE_TILES)
    G["mixer_norm_w"] = jnp.concatenate([d_mix_nw0, d_mix_nw1], axis=0)
    return loss, dx0, G


_SMALL = ["mlp_norm_w", "final_norm_w", "lru_conv_b", "lru_w_r", "lru_b_r", "lru_w_i", "lru_b_i",
          "lru_lambda", "gdn_a_log", "gdn_dt_bias", "gdn_norm_w"]
_PACK_ROWS = 688


def _pack(parts):
    flat = jnp.concatenate([p.reshape(-1) for p in parts])
    return jnp.pad(flat, (0, _PACK_ROWS * 128 - flat.shape[0])).reshape(_PACK_ROWS, 128)


def _unpack(packed, shapes):
    flat = packed.reshape(-1)
    out, off = [], 0
    for s in shapes:
        n = int(np.prod(s))
        out.append(flat[off:off + n].reshape(s))
        off += n
    return out


def kernel(x, mixer_norm_w, mlp_norm_w, final_norm_w, w_in_even, lru_conv_w, lru_conv_b, lru_w_r, lru_b_r, lru_w_i, lru_b_i, lru_lambda, w_out_even, w_in_odd, gdn_conv_w, gdn_a_log, gdn_dt_bias, gdn_norm_w, w_out_odd, w_up, w_down, loss_target, m_mixer_norm_w, m_mlp_norm_w, m_final_norm_w, m_w_in_even, m_lru_conv_w, m_lru_conv_b, m_lru_w_r, m_lru_b_r, m_lru_w_i, m_lru_b_i, m_lru_lambda, m_w_out_even, m_w_in_odd, m_gdn_conv_w, m_gdn_a_log, m_gdn_dt_bias, m_gdn_norm_w, m_w_out_odd, m_w_up, m_w_down, v_mixer_norm_w, v_mlp_norm_w, v_final_norm_w, v_w_in_even, v_lru_conv_w, v_lru_conv_b, v_lru_w_r, v_lru_b_r, v_lru_w_i, v_lru_b_i, v_lru_lambda, v_w_out_even, v_w_in_odd, v_gdn_conv_w, v_gdn_a_log, v_gdn_dt_bias, v_gdn_norm_w, v_w_out_odd, v_w_up, v_w_down):
    Pw = dict(mixer_norm_w=mixer_norm_w, mlp_norm_w=mlp_norm_w, final_norm_w=final_norm_w, w_in_even=w_in_even,
              lru_conv_w=lru_conv_w, lru_conv_b=lru_conv_b, lru_w_r=lru_w_r, lru_b_r=lru_b_r, lru_w_i=lru_w_i,
              lru_b_i=lru_b_i, lru_lambda=lru_lambda, w_out_even=w_out_even, w_in_odd=w_in_odd, gdn_conv_w=gdn_conv_w,
              gdn_a_log=gdn_a_log, gdn_dt_bias=gdn_dt_bias, gdn_norm_w=gdn_norm_w, w_out_odd=w_out_odd, w_up=w_up,
              w_down=w_down)
    Pm = dict(mixer_norm_w=m_mixer_norm_w, mlp_norm_w=m_mlp_norm_w, final_norm_w=m_final_norm_w, w_in_even=m_w_in_even,
              lru_conv_w=m_lru_conv_w, lru_conv_b=m_lru_conv_b, lru_w_r=m_lru_w_r, lru_b_r=m_lru_b_r, lru_w_i=m_lru_w_i,
              lru_b_i=m_lru_b_i, lru_lambda=m_lru_lambda, w_out_even=m_w_out_even, w_in_odd=m_w_in_odd,
              gdn_conv_w=m_gdn_conv_w, gdn_a_log=m_gdn_a_log, gdn_dt_bias=m_gdn_dt_bias, gdn_norm_w=m_gdn_norm_w,
              w_out_odd=m_w_out_odd, w_up=m_w_up, w_down=m_w_down)
    Pv = dict(mixer_norm_w=v_mixer_norm_w, mlp_norm_w=v_mlp_norm_w, final_norm_w=v_final_norm_w, w_in_even=v_w_in_even,
              lru_conv_w=v_lru_conv_w, lru_conv_b=v_lru_conv_b, lru_w_r=v_lru_w_r, lru_b_r=v_lru_b_r, lru_w_i=v_lru_w_i,
              lru_b_i=v_lru_b_i, lru_lambda=v_lru_lambda, w_out_even=v_w_out_even, w_in_odd=v_w_in_odd,
              gdn_conv_w=v_gdn_conv_w, gdn_a_log=v_gdn_a_log, gdn_dt_bias=v_gdn_dt_bias, gdn_norm_w=v_gdn_norm_w,
              w_out_odd=v_w_out_odd, w_up=v_w_up, w_down=v_w_down)
    me = _me()[3]
    T = x.shape[1]

    cols = lambda g: jnp.transpose(g, (1, 0, 2)).reshape(g.shape[1], -1)
    rows = lambda g: g.reshape(-1, g.shape[2])
    wide = lambda g: _odd_assemble(g, "w_in_odd_assemble")
    odd_shard = jnp.pad(w_in_odd[0].astype(bf16), ((0, 0), (0, ODD_SHARD_PAD - ODD_SHARD)))
    as_is = lambda g: g
    gather = dict(
        w_in_even=(w_in_even[0].astype(bf16), cols), lru_conv_w=(lru_conv_w[0], cols),
        w_out_even=(w_out_even[0].astype(bf16), rows), w_up0=(w_up[0].astype(bf16), as_is), w_down0=(w_down[0].astype(bf16), rows),
        w_in_odd=(odd_shard, wide), gdn_conv_w=(gdn_conv_w[0], cols),
        w_out_odd=(w_out_odd[0].astype(bf16), rows), w_up1=(w_up[1].astype(bf16), as_is), w_down1=(w_down[1].astype(bf16), rows))
    relay_groups = (("w_in_even",), ("w_up0", "w_down0"), ("w_in_odd",))
    relayed = sum(relay_groups, ())
    handles, tok = _send_start([s for s, _ in gather.values()], False, "gather_start",
                               relayed=[i for i, name in enumerate(gather) if name in relayed])
    handles = dict(zip(gather, handles))
    relays, landed, full = {}, {}, {}

    def ahead(group, after):
        lands = [_send_wait(handles[n], False, after, f"gather_wait_{n}", relayed=True)[1] for n in group]
        relays[group], token = _relay_start(lands, "relay_start_" + "_".join(group))
        return token

    def weight(name, after):
        if name not in landed and name in relayed:
            group = next(g for g in relay_groups if name in g)
            if group not in relays:
                ahead(group, after)
            landed.update(zip(group, _relay_wait(relays[group], after, "relay_wait_" + "_".join(group))))
        elif name not in landed:
            landed[name] = _send_wait(handles[name], False, after, f"gather_wait_{name}")[1]
        if name not in full:
            full[name] = gather[name][1](landed[name])
        return full[name]

    P = {k: Pw[k] for k in ("mlp_norm_w", "final_norm_w")}
    P["mixer_norm_w"] = mixer_norm_w + tok[0:1, 0:1]
    for k in ("lru_w_r", "lru_w_i", "lru_conv_b", "lru_b_r", "lru_b_i", "lru_lambda", "gdn_a_log", "gdn_dt_bias", "gdn_norm_w"):
        P[k] = Pw[k][0]

    sent = {}

    def sink(grads):
        hs, token = _send_start(list(grads.values()), True, "grads_start_" + "_".join(grads))
        sent.update(zip(grads, hs))
        return token

    loss, dx, G = _local_step(x[0], loss_target[0], P, weight, sink, ahead)
    lanes = lambda a: a.reshape(-1, 128)
    sink(dict(mixer_norm_w=jnp.broadcast_to(lanes(G["mixer_norm_w"])[None], (N_DEV, 2 * D_MODEL // 128, 128))))

    def received(name, after=dx):
        return _send_wait(sent[name], True, after, f"grads_wait_{name}")[1]

    out = {}
    nff = D_FF // N_DEV

    def whole(name, gs):
        out[name] = tuple(_adamw(Pw[name], gs, Pm[name], Pv[name], f"adamw_{name}", layer=0))

    def layers(name):
        res = None
        for l in range(2):
            res = _adamw(Pw[name], received(f"{name}{l}"), Pm[name], Pv[name], f"adamw_{name}{l}", layer=l, prev=res)
        out[name] = tuple(res)

    layers("w_up")
    layers("w_down")
    whole("w_out_odd", received("w_out_odd"))
    out["w_in_odd"] = tuple(_adamw_column_major(w_in_odd, received("w_in_odd"), m_w_in_odd, v_w_in_odd, "adamw_w_in_odd"))
    whole("w_out_even", received("w_out_even"))
    small_shapes = [Pw[k].shape for k in _SMALL]
    pw, pm, pv = (_pack([Q[k] for k in _SMALL]) for Q in (Pw, Pm, Pv))
    sg, sd, sm, sv = _adamw(pw, received("small", out["w_out_even"][1]), pm, pv, "adamw_small")
    for arrs_i, packed_out in enumerate((sg, sd, sm, sv)):
        for k, a in zip(_SMALL, _unpack(packed_out, small_shapes)):
            out.setdefault(k, [None] * 4)[arrs_i] = a
    whole("w_in_even", received("w_in_even", sd))
    out["mixer_norm_w"] = tuple(
        a.reshape(mixer_norm_w.shape) for a in
        _adamw(lanes(mixer_norm_w), received("mixer_norm_w", out["w_in_even"][1]), lanes(m_mixer_norm_w),
               lanes(v_mixer_norm_w), "adamw_mixer_norm_w"))
    n_small = sum(int(np.prod(s)) for s in small_shapes)
    gflat = sg.reshape(-1)
    g_lconv = gflat[n_small:n_small + CONV_K * LRU_WIDTH].reshape(CONV_K, LRU_WIDTH)
    g_gconv = gflat[n_small + CONV_K * LRU_WIDTH:n_small + CONV_K * (LRU_WIDTH + 3072)].reshape(CONV_K, 3072)
    whole("lru_conv_w", lax.dynamic_slice_in_dim(g_lconv, me * 64, 64, axis=1)[None])
    whole("gdn_conv_w", lax.dynamic_slice_in_dim(g_gconv, me * 384, 384, axis=1)[None])

    names = ["mixer_norm_w", "mlp_norm_w", "final_norm_w", "w_in_even", "lru_conv_w", "lru_conv_b", "lru_w_r", "lru_b_r",
             "lru_w_i", "lru_b_i", "lru_lambda", "w_out_even", "w_in_odd", "gdn_conv_w", "gdn_a_log", "gdn_dt_bias",
             "gdn_norm_w", "w_out_odd", "w_up", "w_down"]
    total = gflat[n_small + CONV_K * (LRU_WIDTH + 3072)]
    res = [total, dx[None]]
    for j in range(4):
        res += [out[k][j] for k in names]
    return tuple(res)
```

```python
import math

import numpy as np
import jax
import jax.numpy as jnp
from jax import lax
from jax.experimental import pallas as pl
from jax.experimental.pallas import tpu as pltpu

f32 = jnp.float32
bf16 = jnp.bfloat16

N_DEV = 8
D_MODEL = 1024
D_FF = 4096
EPS = 1e-6
RET_HEADS = 4
RET_CHUNK = 128
RET_STEP = 4
ROPE_THETA = 10000.0
LRU_WIDTH = 512
LRU_C = 8.0
GDN_HEADS = 8
GDN_CHUNK = 64
GDN_STEP = 4
HEAD_DIM = 128
ODD_IN = 4112
ODD_PAD = 4224
ODD_SHARD = ODD_IN // N_DEV
ODD_SHARD_PAD = 640
ADAM_LR, ADAM_B1, ADAM_B2, ADAM_EPS, ADAM_WD, ADAM_STEP = 0.001, 0.9, 0.999, 1e-08, 0.01, 10
VMEM_LIMIT = 56 * 1024 * 1024

_NN = (((1,), (0,)), ((), ()))
_NT = (((1,), (1,)), ((), ()))
_TN = (((0,), (0,)), ((), ()))
MESH = pl.DeviceIdType.MESH


def _cparams(sem):
    return pltpu.CompilerParams(dimension_semantics=sem, vmem_limit_bytes=VMEM_LIMIT)


def _dot(a, b, dn):
    return lax.dot_general(a.astype(bf16), b.astype(bf16), dn, preferred_element_type=f32)


def _dot01(a01, b, dn):
    a = a01.astype(bf16)
    b0 = b.astype(bf16)
    r1 = b - b0.astype(f32)
    b1 = r1.astype(bf16)
    b2 = (r1 - b1.astype(f32)).astype(bf16)
    d = lambda q: lax.dot_general(a, q, dn, preferred_element_type=f32)
    return d(b0) + (d(b1) + d(b2))


def _sigmoid(x):
    return jax.nn.sigmoid(x)


def _silu(x):
    return x * _sigmoid(x)


def _dsilu(x):
    s = _sigmoid(x)
    return s * (1.0 + x * (1.0 - s))


def _softplus(x):
    return jnp.maximum(x, 0.0) + jnp.log1p(jnp.exp(-jnp.abs(x)))


_GELU_C = math.sqrt(2.0 / math.pi)


def _gelu(y):
    return 0.5 * y * (1.0 + jnp.tanh(_GELU_C * (y + 0.044715 * y * y * y)))


def _dgelu(y):
    t = jnp.tanh(_GELU_C * (y + 0.044715 * y * y * y))
    return 0.5 * (1.0 + t) + 0.5 * y * (1.0 - t * t) * _GELU_C * (1.0 + 3.0 * 0.044715 * y * y)


def _matmul(a, b, form, *, outs, name, epilogue=None, extras=(), tm=4096, tn=512, tk=1024, shard_cols=0, a_map=None,
            after=()):
    if form == "tn":
        K, M = a.shape
    else:
        M, K = a.shape
    per_step = 1
    if b.ndim == 3:
        assert form in ("nn", "nt"), name
        N = b.shape[1] if form == "nt" else N_DEV * b.shape[2]
        if form == "nn":
            tn = b.shape[2]
        else:
            per_step = max(1, tk // b.shape[2])
            tk = per_step * b.shape[2]
    else:
        N = b.shape[0] if form == "nt" else b.shape[1]
    ns = N // N_DEV
    if shard_cols:
        tn = ns * shard_cols
    tm, tn, tk = min(tm, M), min(tn, N), min(tk, K)
    assert M % tm == 0 and N % tn == 0 and K % tk == 0, (name, M, N, K, tm, tn, tk)
    nk = K // tk
    dn = {"nn": _NN, "nt": _NT, "tn": _TN}[form]
    if form == "tn":
        a_spec = pl.BlockSpec((tk, tm), lambda i, j, k: (k, i))
    else:
        a_spec = pl.BlockSpec((tm, tk), lambda i, j, k: (i, k))
    if b.ndim == 3:
        b_spec = (pl.BlockSpec((per_step, tn, tk // per_step), lambda i, j, k: (k, j, 0)) if form == "nt"
                  else pl.BlockSpec((None, tk, tn), lambda i, j, k: (j, k, 0)))
    elif form == "nt":
        b_spec = pl.BlockSpec((tn, tk), lambda i, j, k: (j, k))
    else:
        b_spec = pl.BlockSpec((tk, tn), lambda i, j, k: (k, j))
    e_spec = pl.BlockSpec((tm, tn), lambda i, j, k: (i, j))
    v_spec = pl.BlockSpec((1, tn), lambda i, j, k: (0, j))
    if shard_cols:
        o_spec = pl.BlockSpec((shard_cols, tm, ns), lambda i, j, k: (j, i, 0))
        o_shape = (N_DEV, M, ns)
    else:
        o_spec = e_spec
        o_shape = (M, N)
    n_ex = len(extras)
    n_in = 2 + n_ex + len(after)
    sums = [isinstance(o, tuple) for o in outs]
    assert not any(sums) or tn == N, name

    def finish(acc, ex, o_refs, row_tile):
        vals = (acc,) if epilogue is None else epilogue(acc, *[e[...] for e in ex])
        for r, v, is_sum in zip(o_refs, vals, sums):
            if is_sum:
                @pl.when(row_tile == 0)
                def _(r=r, v=v):
                    r[...] = v.astype(r.dtype)

                @pl.when(row_tile > 0)
                def _(r=r, v=v):
                    r[...] += v.astype(r.dtype)
            elif shard_cols:
                for s in range(shard_cols):
                    r[s] = v[:, s * ns:(s + 1) * ns].astype(r.dtype)
            else:
                r[...] = v.astype(r.dtype)

    def prod(a_ref, b_ref):
        if b.ndim == 3 and form == "nt":
            w = tk // per_step
            return sum(_dot(a_ref[:, s * w:(s + 1) * w], b_ref[s], dn) for s in range(1, per_step)) + _dot(a_ref[:, 0:w], b_ref[0], dn)
        av = a_ref[...]
        return _dot(av if a_map is None else a_map(av), b_ref[...], dn)

    def body_one(*refs):
        finish(prod(*refs[:2]), refs[2:2 + n_ex], refs[n_in:], pl.program_id(0))

    def body_acc(*refs):
        a_ref, b_ref = refs[:2]
        acc = refs[-1]
        k = pl.program_id(2)
        row_tile = pl.program_id(0)

        @pl.when(k == 0)
        def _():
            acc[...] = prod(a_ref, b_ref)

        @pl.when((k > 0) & (k < nk - 1))
        def _():
            acc[...] += prod(a_ref, b_ref)

        @pl.when(k == nk - 1)
        def _():
            finish(acc[...] + prod(a_ref, b_ref), refs[2:2 + n_ex], refs[n_in:-1], row_tile)

    return pl.pallas_call(
        body_one if nk == 1 else body_acc, grid=(M // tm, N // tn, nk),
        in_specs=[a_spec, b_spec] + [v_spec if e.shape[0] == 1 else e_spec for e in extras]
        + [pl.BlockSpec(memory_space=pl.ANY)] * len(after),
        out_specs=[v_spec if s else o_spec for s in sums],
        out_shape=[jax.ShapeDtypeStruct((1, N), o[1]) if s else jax.ShapeDtypeStruct(o_shape, o) for o, s in zip(outs, sums)],
        scratch_shapes=[] if nk == 1 else [pltpu.VMEM((tm, tn), f32)],
        compiler_params=_cparams(("arbitrary" if any(sums) else "parallel", "parallel", "arbitrary")), name=name,
    )(a, b, *extras, *after)


def _rms_fwd(x, w, name):
    T, D = x.shape
    tt = min(512, T)

    def body(x_ref, w_ref, h_ref):
        xv = x_ref[...]
        r = lax.rsqrt(jnp.mean(xv * xv, axis=1, keepdims=True) + EPS)
        h_ref[...] = (xv * r * w_ref[...]).astype(bf16)

    return pl.pallas_call(
        body, grid=(T // tt,),
        in_specs=[pl.BlockSpec((tt, D), lambda i: (i, 0)), pl.BlockSpec((1, D), lambda i: (0, 0))],
        out_specs=pl.BlockSpec((tt, D), lambda i: (i, 0)),
        out_shape=jax.ShapeDtypeStruct((T, D), bf16),
        compiler_params=_cparams(("parallel",)), name=name,
    )(x, w)


def _residual_rms_ep(acc, res, w):
    x = res + acc
    r = lax.rsqrt(jnp.mean(x * x, axis=1, keepdims=True) + EPS)
    return x, x * r * w


_RMS_BWD_OUTS = [f32, bf16, ("sum", f32)]


def _rms_bwd_ep(dh, x, dres, w):
    r = lax.rsqrt(jnp.mean(x * x, axis=1, keepdims=True) + EPS)
    xn = x * r
    dhw = dh * w
    dx = dres + r * (dhw - xn * jnp.mean(dhw * xn, axis=1, keepdims=True))
    return dx, dx, jnp.sum(dh * xn, axis=0, keepdims=True)


_LOSS_OUTS = [("sum", f32), f32, bf16, ("sum", f32)]


def _loss_ep(acc, res, w, tgt):
    x = res + acc
    D = x.shape[1]
    r = lax.rsqrt(jnp.mean(x * x, axis=1, keepdims=True) + EPS)
    xn = x * r
    e = xn * w - tgt
    loss = 0.5 * jnp.sum(jnp.mean(e * e, axis=1, keepdims=True), axis=0, keepdims=True)
    dy = e * (1.0 / D)
    dyw = dy * w
    dx = r * (dyw - xn * jnp.mean(dyw * xn, axis=1, keepdims=True))
    return jnp.broadcast_to(loss, (1, D)), dx, dx, jnp.sum(dy * xn, axis=0, keepdims=True)


def _ret_tables():
    H, C = RET_HEADS, RET_CHUNK
    lg = np.log1p(-np.exp2(-5.0 - np.arange(H, dtype=np.float32))).astype(np.float32)
    idx = np.arange(C, dtype=np.float32)
    diff = idx[:, None] - idx[None, :]
    causal = diff >= 0
    dm = np.where(causal[None], np.exp(lg[:, None, None] * np.where(causal, diff, 0.0)[None]), 0.0)
    qd = np.exp(lg[:, None] * (idx[None, :] + 1.0))
    kd = np.exp(lg[:, None] * (C - 1.0 - idx[None, :]))
    cg = np.exp(lg * C)
    tab = np.zeros((H, 4, C, HEAD_DIM), np.float32)
    tab[:, 0] = dm
    tab[:, 1] = qd[:, :, None]
    tab[:, 2] = kd[:, :, None]
    tab[:, 3] = cg[:, None, None]
    return jnp.asarray(tab)


def _rope_tables(T):
    half = HEAD_DIM // 2
    inv = np.float32(ROPE_THETA) ** (-np.arange(half, dtype=np.float32) / np.float32(half))
    ang = np.arange(T, dtype=np.float32)[:, None] * inv[None, :]
    c, s = np.cos(ang), np.sin(ang)
    return jnp.asarray(np.concatenate([c, c], axis=1)), jnp.asarray(np.concatenate([-s, s], axis=1))


def _rope(x, cos, sin):
    return x * cos + pltpu.roll(x, HEAD_DIM // 2, 1) * sin


def _unrope(y, cos, sin):
    return y * cos + pltpu.roll(y * sin, HEAD_DIM // 2, 1)


def _stack_heads(ref, H, f=None):
    parts = [ref[:, h * HEAD_DIM:(h + 1) * HEAD_DIM] for h in range(H)]
    return jnp.stack(parts if f is None else [f(a) for a in parts])


def _ret_fwd(p, cos, sin, tab, name):
    T = p.shape[0]
    C, H = RET_CHUNK, RET_HEADS
    N = T // C
    K = min(RET_STEP, N)
    NS = N // K
    scale = HEAD_DIM ** -0.5

    def body(q_ref, k_ref, v_ref, g_ref, c_ref, s_ref, t_ref, y_ref, o_ref, sp_ref, st):
        @pl.when(pl.program_id(0) == 0)
        def _():
            st[...] = jnp.zeros_like(st)

        dm, qd, kd, cg = t_ref[:, 0], t_ref[:, 1], t_ref[:, 2], t_ref[:, 3]
        S = st[...]
        for c in range(K):
            rows = pl.ds(c * C, C)
            cos_, sin_ = c_ref[rows, :], s_ref[rows, :]
            rot = lambda a: _rope(a, cos_, sin_)
            q = _stack_heads(q_ref.at[rows, :], H, rot)
            k = _stack_heads(k_ref.at[rows, :], H, rot) * scale
            v = _stack_heads(v_ref.at[rows, :], H)
            P = _dot(q, k, _NT3) * dm
            o = _dot(P, v, _NN3) + _dot(q * qd, S, _NN3)
            sp_ref[c] = S
            S = cg * S + _dot(k * kd, v, _TN3)
            r = lax.rsqrt(jnp.mean(o * o, axis=2, keepdims=True) + EPS)
            y = o * r * _silu(_stack_heads(g_ref.at[rows, :], H))
            for h in range(H):
                o_ref[rows, h * HEAD_DIM:(h + 1) * HEAD_DIM] = o[h]
                y_ref[rows, h * HEAD_DIM:(h + 1) * HEAD_DIM] = y[h].astype(bf16)
        st[...] = S

    wide = lambda blk: pl.BlockSpec((K * C, H * HEAD_DIM), lambda n: (n, blk))
    tbl = pl.BlockSpec((K * C, HEAD_DIM), lambda n: (n, 0))
    return pl.pallas_call(
        body, grid=(NS,),
        in_specs=[wide(0), wide(1), wide(2), wide(3), tbl, tbl,
                  pl.BlockSpec((H, 4, C, HEAD_DIM), lambda n: (0, 0, 0, 0))],
        out_specs=[wide(0), wide(0), pl.BlockSpec((K, H, HEAD_DIM, HEAD_DIM), lambda n: (n, 0, 0, 0))],
        out_shape=[jax.ShapeDtypeStruct((T, D_MODEL), bf16), jax.ShapeDtypeStruct((T, H * HEAD_DIM), f32),
                   jax.ShapeDtypeStruct((N, H, HEAD_DIM, HEAD_DIM), f32)],
        scratch_shapes=[pltpu.VMEM((H, HEAD_DIM, HEAD_DIM), f32)],
        compiler_params=_cparams(("arbitrary",)), name=name,
    )(p, p, p, p, cos, sin, tab)


def _ret_bwd(p, cos, sin, tab, o_raw, sprev, dmix, name):
    T = p.shape[0]
    C, H = RET_CHUNK, RET_HEADS
    N = T // C
    K = min(RET_STEP, N)
    NS = N // K
    scale = HEAD_DIM ** -0.5
    W = H * HEAD_DIM

    def body(q_ref, k_ref, v_ref, g_ref, c_ref, s_ref, t_ref, o_ref, sp_ref, dy_ref, d_ref, dst):
        @pl.when(pl.program_id(0) == 0)
        def _():
            dst[...] = jnp.zeros_like(dst)

        dm, qd, kd, cg = t_ref[:, 0], t_ref[:, 1], t_ref[:, 2], t_ref[:, 3]
        dS1 = dst[...]
        for c in reversed(range(K)):
            rows = pl.ds(c * C, C)
            cos_, sin_ = c_ref[rows, :], s_ref[rows, :]
            rot = lambda a: _rope(a, cos_, sin_)
            q = _stack_heads(q_ref.at[rows, :], H, rot)
            k = _stack_heads(k_ref.at[rows, :], H, rot) * scale
            v = _stack_heads(v_ref.at[rows, :], H)
            g = _stack_heads(g_ref.at[rows, :], H)
            S = sp_ref[c]
            o = _stack_heads(o_ref.at[rows, :], H)
            dy = _stack_heads(dy_ref.at[rows, :], H)
            r = lax.rsqrt(jnp.mean(o * o, axis=2, keepdims=True) + EPS)
            nrm = o * r
            dn = dy * _silu(g)
            dg = dy * nrm * _dsilu(g)
            do = r * (dn - nrm * jnp.mean(dn * nrm, axis=2, keepdims=True))
            P = _dot(q, k, _NT3) * dm
            dP = _dot(do, v, _NT3) * dm
            dq = _dot(dP, k, _NN3) + _dot(do, S, _NT3) * qd
            dk = (_dot(dP, q, _TN3) + _dot(v, dS1, _NT3) * kd) * scale
            dv = _dot(P, do, _TN3) + _dot(k * kd, dS1, _NN3)
            dS1 = cg * dS1 + _dot(q * qd, do, _TN3)
            for h in range(H):
                d_ref[rows, h * HEAD_DIM:(h + 1) * HEAD_DIM] = _unrope(dq[h], cos_, sin_).astype(bf16)
                d_ref[rows, W + h * HEAD_DIM:W + (h + 1) * HEAD_DIM] = _unrope(dk[h], cos_, sin_).astype(bf16)
                d_ref[rows, 2 * W + h * HEAD_DIM:2 * W + (h + 1) * HEAD_DIM] = dv[h].astype(bf16)
                d_ref[rows, 3 * W + h * HEAD_DIM:3 * W + (h + 1) * HEAD_DIM] = dg[h].astype(bf16)
        dst[...] = dS1

    rev = lambda blk: pl.BlockSpec((K * C, W), lambda n: (NS - 1 - n, blk))
    tbl = pl.BlockSpec((K * C, HEAD_DIM), lambda n: (NS - 1 - n, 0))
    return pl.pallas_call(
        body, grid=(NS,),
        in_specs=[rev(0), rev(1), rev(2), rev(3), tbl, tbl,
                  pl.BlockSpec((H, 4, C, HEAD_DIM), lambda n: (0, 0, 0, 0)), rev(0),
                  pl.BlockSpec((K, H, HEAD_DIM, HEAD_DIM), lambda n: (NS - 1 - n, 0, 0, 0)), rev(0)],
        out_specs=pl.BlockSpec((K * C, 4 * W), lambda n: (NS - 1 - n, 0)),
        out_shape=jax.ShapeDtypeStruct((T, 6 * W), bf16),
        scratch_shapes=[pltpu.VMEM((H, HEAD_DIM, HEAD_DIM), f32)],
        compiler_params=_cparams(("arbitrary",)), name=name,
    )(p, p, p, p, cos, sin, tab, o_raw, sprev, dmix)


CONV_K = 4
CONV_W = 512
PAD = 8
SUB_R = 64


def _conv_fwd(x, col_off, w, b, act, name):
    T = x.shape[0]
    C = w.shape[1]
    G = C // CONV_W
    tt = min(512, T)
    NT = T // tt
    has_b = b is not None

    def body(*refs):
        if has_b:
            x_ref, w_ref, b_ref, y_ref, pad = refs
        else:
            x_ref, w_ref, y_ref, pad = refs
        t = pl.program_id(1)

        @pl.when(t == 0)
        def _():
            pad[pl.ds(0, PAD), :] = jnp.zeros((PAD, CONV_W), f32)

        pad[pl.ds(PAD, tt), :] = x_ref[...]
        for g in range(CONV_W // 128):
            ls = slice(g * 128, (g + 1) * 128)
            wv = w_ref[:, ls]
            for c in range(tt // SUB_R):
                r0 = c * SUB_R
                y = wv[0:1, :] * pad[pl.ds(PAD - 3 + r0, SUB_R), ls]
                for kk in range(1, CONV_K):
                    y = y + wv[kk:kk + 1, :] * pad[pl.ds(PAD - 3 + kk + r0, SUB_R), ls]
                if has_b:
                    y = y + b_ref[:, ls]
                y_ref[pl.ds(r0, SUB_R), ls] = _silu(y) if act else y
        tail = pad[pl.ds(tt, PAD), :]
        pad[pl.ds(0, PAD), :] = tail

    in_specs = [pl.BlockSpec((tt, CONV_W), lambda g, t: (t, col_off + g)),
                pl.BlockSpec((CONV_K, CONV_W), lambda g, t: (0, g))]
    args = [x, w]
    if has_b:
        in_specs.append(pl.BlockSpec((1, CONV_W), lambda g, t: (0, g)))
        args.append(b)
    return pl.pallas_call(
        body, grid=(G, NT), in_specs=in_specs,
        out_specs=pl.BlockSpec((tt, CONV_W), lambda g, t: (t, g)),
        out_shape=jax.ShapeDtypeStruct((T, C), f32),
        scratch_shapes=[pltpu.VMEM((tt + PAD, CONV_W), f32)],
        compiler_params=_cparams(("parallel", "arbitrary")), name=name,
    )(*args)


def _conv_bwd(x, col_off, w, b, act, dout, dp, name):
    T = x.shape[0]
    C = w.shape[1]
    G = C // CONV_W
    tt = min(512, T)
    NT = T // tt
    has_b = b is not None

    def body(*refs):
        if has_b:
            x_ref, xp_ref, w_ref, b_ref, d_ref, dp_in, dx_ref, dw_ref, db_ref, pad, dpad = refs
        else:
            x_ref, xp_ref, w_ref, d_ref, dp_in, dx_ref, dw_ref, db_ref, pad, dpad = refs
        t = pl.program_id(1)
        first_tile = t == NT - 1

        @pl.when(t == 0)
        def _():
            dpad[pl.ds(tt, PAD), :] = jnp.zeros((PAD, CONV_W), f32)
            dw_ref[...] = jnp.zeros_like(dw_ref)
            db_ref[...] = jnp.zeros_like(db_ref)

        pad[pl.ds(0, PAD), :] = jnp.where(first_tile, 0.0, xp_ref[...])
        pad[pl.ds(PAD, tt), :] = x_ref[...]
        fold = lambda v: v.reshape(SUB_R // 8, 8, 128).sum(axis=0)
        for g in range(CONV_W // 128):
            ls = slice(g * 128, (g + 1) * 128)
            wv = w_ref[:, ls]
            acc = [jnp.zeros((8, 128), f32) for _ in range(CONV_K + 1)]
            for c in reversed(range(tt // SUB_R)):
                r0 = c * SUB_R
                xs = [pad[pl.ds(PAD - 3 + kk + r0, SUB_R), ls] for kk in range(CONV_K)]
                dy = d_ref[pl.ds(r0, SUB_R), ls]
                if act:
                    y = wv[0:1, :] * xs[0]
                    for kk in range(1, CONV_K):
                        y = y + wv[kk:kk + 1, :] * xs[kk]
                    if has_b:
                        y = y + b_ref[:, ls]
                    dy = dy * _dsilu(y)
                dpad[pl.ds(r0, SUB_R), ls] = dy
                dx = wv[3:4, :] * dy
                for j in range(1, CONV_K):
                    dx = dx + wv[3 - j:4 - j, :] * dpad[pl.ds(r0 + j, SUB_R), ls]
                dx_ref[pl.ds(r0, SUB_R), ls] = dx.astype(bf16)
                for kk in range(CONV_K):
                    acc[kk] = acc[kk] + fold(dy * xs[kk])
                acc[CONV_K] = acc[CONV_K] + fold(dy)
            for kk in range(CONV_K):
                dw_ref[kk:kk + 1, ls] += jnp.sum(acc[kk], axis=0, keepdims=True)
            db_ref[:, ls] += jnp.sum(acc[CONV_K], axis=0, keepdims=True)
        head = dpad[pl.ds(0, PAD), :]
        dpad[pl.ds(tt, PAD), :] = head

    rows8 = tt // PAD
    in_specs = [pl.BlockSpec((tt, CONV_W), lambda g, t: (NT - 1 - t, col_off + g)),
                pl.BlockSpec((PAD, CONV_W), lambda g, t: (jnp.maximum((NT - 1 - t) * rows8 - 1, 0), col_off + g)),
                pl.BlockSpec((CONV_K, CONV_W), lambda g, t: (0, g))]
    args = [x, x, w]
    if has_b:
        in_specs.append(pl.BlockSpec((1, CONV_W), lambda g, t: (0, g)))
        args.append(b)
    in_specs += [pl.BlockSpec((tt, CONV_W), lambda g, t: (NT - 1 - t, g)), pl.BlockSpec(memory_space=pl.ANY)]
    args += [dout, dp]
    return pl.pallas_call(
        body, grid=(G, NT), in_specs=in_specs,
        out_specs=[pl.BlockSpec((tt, CONV_W), lambda g, t: (NT - 1 - t, col_off + g)),
                   pl.BlockSpec((CONV_K, CONV_W), lambda g, t: (0, g)),
                   pl.BlockSpec((1, CONV_W), lambda g, t: (0, g))],
        out_shape=[jax.ShapeDtypeStruct(dp.shape, dp.dtype), jax.ShapeDtypeStruct((CONV_K, C), f32),
                   jax.ShapeDtypeStruct((1, C), f32)],
        input_output_aliases={len(args) - 1: 0},
        scratch_shapes=[pltpu.VMEM((tt + PAD, CONV_W), f32), pltpu.VMEM((tt + PAD, CONV_W), f32)],
        compiler_params=_cparams(("parallel", "arbitrary")), name=name,
    )(*args)


def _lru_gates(xc, wr, wi, br, bi, lam):
    r = _sigmoid(_dot(xc, wr, _NN) + br)
    i = _sigmoid(_dot(xc, wi, _NN) + bi)
    sp = _softplus(-lam)
    a = jnp.exp(-LRU_C * r * sp)
    mult = jnp.sqrt(1.0 - a * a)
    return r, i, sp, a, mult


def _lru_fwd(xc, p, y_off, wr, wi, br, bi, lam, mix, name):
    T = xc.shape[0]
    G = LRU_WIDTH // 128
    tt = min(512, T)
    NT = T // tt

    def body(x_ref, y_ref, wr_ref, wi_ref, br_ref, bi_ref, l_ref, mix_in, o_ref, h_ref, hc):
        t = pl.program_id(1)

        @pl.when(t == 0)
        def _():
            hc[...] = jnp.zeros_like(hc)

        x = x_ref[...]
        r, i, sp, a, mult = _lru_gates(x, wr_ref[...], wi_ref[...], br_ref[...], bi_ref[...], l_ref[...])
        row = lax.broadcasted_iota(jnp.int32, (tt, 128), 0)
        mult = jnp.where((row == 0) & (t == 0), 1.0, mult)
        U = x * i * mult
        A = a
        d = 1
        while d < tt:
            keep = row >= d
            Ush = jnp.where(keep, pltpu.roll(U, d, 0), 0.0)
            Ash = jnp.where(keep, pltpu.roll(A, d, 0), 1.0)
            U = A * Ush + U
            A = A * Ash
            d *= 2
        h = U + A * hc[0:1, :]
        h_ref[...] = h
        hc[...] = jnp.broadcast_to(h[tt - 1:tt, :], hc.shape)
        o_ref[...] = (h * _gelu(y_ref[...])).astype(bf16)

    tile = pl.BlockSpec((tt, 128), lambda g, t: (t, g))
    vec = pl.BlockSpec((1, 128), lambda g, t: (0, g))
    wsp = pl.BlockSpec((128, 128), lambda g, t: (g, g))
    return pl.pallas_call(
        body, grid=(G, NT),
        in_specs=[tile, pl.BlockSpec((tt, 128), lambda g, t: (t, y_off + g)), wsp, wsp, vec, vec, vec,
                  pl.BlockSpec(memory_space=pl.ANY)],
        out_specs=[pl.BlockSpec((tt, 128), lambda g, t: (t, G + g)), tile],
        out_shape=[jax.ShapeDtypeStruct(mix.shape, mix.dtype), jax.ShapeDtypeStruct((T, LRU_WIDTH), f32)],
        input_output_aliases={7: 0},
        scratch_shapes=[pltpu.VMEM((8, 128), f32)],
        compiler_params=_cparams(("parallel", "arbitrary")), name=name,
    )(xc, p, wr, wi, br, bi, lam, mix)


def _lru_bwd(xc, p, y_off, wr, wi, br, bi, lam, hs, dmix, d_off, dp, name):
    T = xc.shape[0]
    G = LRU_WIDTH // 128
    tt = min(512, T)
    NT = T // tt

    def body(x_ref, y_ref, wr_ref, wi_ref, br_ref, bi_ref, l_ref, h_ref, hp_ref, do_ref, dp_in,
             dx_ref, dy_ref, dwr_ref, dwi_ref, dbr_ref, dbi_ref, dl_ref, lc, an):
        t = pl.program_id(1)
        first_tile = t == NT - 1

        @pl.when(t == 0)
        def _():
            lc[...] = jnp.zeros_like(lc)
            an[...] = jnp.zeros_like(an)
            dwr_ref[...] = jnp.zeros_like(dwr_ref)
            dwi_ref[...] = jnp.zeros_like(dwi_ref)
            dbr_ref[...] = jnp.zeros_like(dbr_ref)
            dbi_ref[...] = jnp.zeros_like(dbi_ref)
            dl_ref[...] = jnp.zeros_like(dl_ref)

        x = x_ref[...]
        y = y_ref[...]
        wr, wi, lam_ = wr_ref[...], wi_ref[...], l_ref[...]
        r, i, sp, a, mult_raw = _lru_gates(x, wr, wi, br_ref[...], bi_ref[...], lam_)
        row = lax.broadcasted_iota(jnp.int32, (tt, 128), 0)
        t0 = (row == 0) & first_tile
        mult = jnp.where(t0, 1.0, mult_raw)
        h = h_ref[...]
        do = do_ref[...]
        dh = do * _gelu(y)
        dy_ref[...] = (do * h * _dgelu(y)).astype(bf16)
        B = jnp.where(row == tt - 1, an[0:1, :], pltpu.roll(a, tt - 1, 0))
        L = dh
        d = 1
        while d < tt:
            keep = row < tt - d
            Lsh = jnp.where(keep, pltpu.roll(L, tt - d, 0), 0.0)
            Bsh = jnp.where(keep, pltpu.roll(B, tt - d, 0), 1.0)
            L = L + B * Lsh
            B = B * Bsh
            d *= 2
        L = L + B * lc[0:1, :]
        lc[...] = jnp.broadcast_to(L[0:1, :], lc.shape)
        an[...] = jnp.broadcast_to(a[0:1, :], an.shape)
        hprev = jnp.where(first_tile, 0.0, hp_ref[...])[PAD - 1:PAD, :]
        hm1 = jnp.where(row == 0, hprev, pltpu.roll(h, 1, 0))
        da = L * hm1
        dxc = L * i * mult
        di = L * x * mult
        dmult = jnp.where(t0, 0.0, L * x * i)
        da = da - jnp.where(t0, 0.0, dmult * a / mult_raw)
        dlog_a = da * a
        dr = dlog_a * (-LRU_C) * sp
        dsp = jnp.sum(dlog_a * (-LRU_C) * r, axis=0, keepdims=True)
        dpr = dr * r * (1.0 - r)
        dpi = di * i * (1.0 - i)
        dx_ref[...] = dxc + _dot(dpr, wr, _NT) + _dot(dpi, wi, _NT)
        for d_ref, dpre in ((dwr_ref, dpr), (dwi_ref, dpi)):
            dw = _dot(x, dpre, _TN)
            for s in range(2):
                d_ref[s] += dw[s * 64:(s + 1) * 64, s * 64:(s + 1) * 64]
        dbr_ref[...] += jnp.sum(dpr, axis=0, keepdims=True)
        dbi_ref[...] += jnp.sum(dpi, axis=0, keepdims=True)
        dl_ref[...] += dsp * (-_sigmoid(-lam_))

    rows8 = tt // PAD
    tile = pl.BlockSpec((tt, 128), lambda g, t: (NT - 1 - t, g))
    vec = pl.BlockSpec((1, 128), lambda g, t: (0, g))
    wsp = pl.BlockSpec((128, 128), lambda g, t: (g, g))
    wout = pl.BlockSpec((2, 64, 64), lambda g, t: (g, 0, 0))
    return pl.pallas_call(
        body, grid=(G, NT),
        in_specs=[tile, pl.BlockSpec((tt, 128), lambda g, t: (NT - 1 - t, y_off + g)), wsp, wsp, vec, vec, vec, tile,
                  pl.BlockSpec((PAD, 128), lambda g, t: (jnp.maximum((NT - 1 - t) * rows8 - 1, 0), g)),
                  pl.BlockSpec((tt, 128), lambda g, t: (NT - 1 - t, d_off + g)), pl.BlockSpec(memory_space=pl.ANY)],
        out_specs=[tile, pl.BlockSpec((tt, 128), lambda g, t: (NT - 1 - t, y_off + g)), wout, wout, vec, vec, vec],
        out_shape=[jax.ShapeDtypeStruct((T, LRU_WIDTH), f32), jax.ShapeDtypeStruct(dp.shape, dp.dtype),
                   jax.ShapeDtypeStruct((2 * G, 64, 64), f32), jax.ShapeDtypeStruct((2 * G, 64, 64), f32),
                   jax.ShapeDtypeStruct((1, LRU_WIDTH), f32), jax.ShapeDtypeStruct((1, LRU_WIDTH), f32),
                   jax.ShapeDtypeStruct((1, LRU_WIDTH), f32)],
        input_output_aliases={10: 1},
        scratch_shapes=[pltpu.VMEM((8, 128), f32), pltpu.VMEM((8, 128), f32)],
        compiler_params=_cparams(("parallel", "arbitrary")), name=name,
    )(xc, p, wr, wi, br, bi, lam, hs, hs, dmix, dp)


_NN3 = (((2,), (1,)), ((0,), (0,)))
_NT3 = (((2,), (2,)), ((0,), (0,)))
_TN3 = (((1,), (1,)), ((0,), (0,)))


def _pairs(ref, K):
    C = GDN_CHUNK
    return jnp.stack([ref[c * C:(c + 1) * C, h * HEAD_DIM:(h + 1) * HEAD_DIM] for c in range(K) for h in range(GDN_HEADS)])


def _put_pairs(ref, val, K, col=0):
    C, H = GDN_CHUNK, GDN_HEADS
    for c in range(K):
        for h in range(H):
            ref[c * C:(c + 1) * C, col + h * HEAD_DIM:col + (h + 1) * HEAD_DIM] = val[c * H + h].astype(ref.dtype)


def _rowsum(x):
    H, C, L = x.shape
    return _dot(x.reshape(H * C, L), jnp.ones((L, HEAD_DIM), f32), _NN).reshape(H, C, HEAD_DIM)


def _gdn_pre(qr, kr, v, ba, alog, dtb):
    C, H = GDN_CHUNK, GDN_HEADS
    B = qr.shape[0]
    K = B // H
    lane = lax.broadcasted_iota(jnp.int32, (C, 128), 1)
    lane3 = lax.broadcasted_iota(jnp.int32, (B, C, 128), 2)
    ri = lax.broadcasted_iota(jnp.int32, (C, C), 0)
    ci = lax.broadcasted_iota(jnp.int32, (C, C), 1)
    rowc = lax.broadcasted_iota(jnp.int32, (C, 1), 0)
    col = lambda m, j: jnp.sum(jnp.where(lane == j, m, 0.0), axis=1, keepdims=True)
    ea = jnp.exp(alog)
    tri = (ri >= ci).astype(f32)
    g_all, beta_cols, G_cols = [], [], []
    for c in range(K):
        ba_c = ba[c * C:(c + 1) * C]
        g_c = -ea * _softplus(ba_c + dtb)
        G_c = _dot01(tri, g_c, _NN)
        s_c = _sigmoid(ba_c)
        g_all.append(g_c)
        beta_cols += [col(s_c, h) for h in range(H)]
        G_cols += [col(G_c, H + h) for h in range(H)]
    wide = lambda c: jnp.broadcast_to(c, (B, C, 128))
    beta = wide(jnp.stack(beta_cols))
    Gc = jnp.stack(G_cols)
    rq = lax.rsqrt(_rowsum(qr * qr) + EPS)
    rk = lax.rsqrt(_rowsum(kr * kr) + EPS)
    qh, kn = qr * rq, kr * rk
    qn = qh * (HEAD_DIM ** -0.5)
    Grow = _dot01(jnp.ones((B, C, 128), f32), jnp.where(lane3 == 0, Gc, 0.0), _NT3)
    incl = ri >= ci
    Di = jnp.where(incl, jnp.exp(jnp.where(incl, Gc - Grow, 0.0)), 0.0)
    Ds = jnp.where(ri > ci, Di, 0.0)
    Gl = jnp.sum(jnp.where(rowc == C - 1, Gc, 0.0), axis=1, keepdims=True)
    eG = wide(jnp.exp(Gc))
    eGl = wide(jnp.exp(Gl - Gc))
    cd = jnp.exp(Gl)
    kb = kn * beta
    vb = v * beta
    Lm = _dot(kb, kn, _NT3) * Ds
    kbg = kb * eG
    QK = _dot(qn, kn, _NT3) * Di
    qg = qn * eG
    kg = kn * eGl
    return dict(beta=beta, g_all=g_all, rq=rq, rk=rk, qh=qh, kn=kn, qn=qn, Di=Di, Ds=Ds, eG=eG, eGl=eGl, cd=cd,
                kb=kb, vb=vb, Lm=Lm, kbg=kbg, QK=QK, qg=qg, kg=kg, lane=lane, ri=ri, ci=ci, rowc=rowc, ea=ea)


def _unit_lower_inverse(Lm):
    C = Lm.shape[-1]
    ri = lax.broadcasted_iota(jnp.int32, (C, C), 0)
    ci = lax.broadcasted_iota(jnp.int32, (C, C), 1)
    same = lambda s: (ri // s) == (ci // s)
    Xd = jnp.where(same(8), -Lm, 0.0)
    Tinv = (ri == ci).astype(f32) + Xd
    Pw = Xd
    for _ in range(2):
        Pw = _dot(Pw, Pw, _NN3)
        Tinv = Tinv + _dot(Tinv, Pw, _NN3)
    for s in (8, 16, 32):
        off = jnp.where(same(2 * s) & jnp.logical_not(same(s)), Lm, 0.0)
        Tinv = Tinv - _dot(_dot(Tinv, off, _NN3), Tinv, _NN3)
    return Tinv


def _gdn_specs(T, rev):
    C = GDN_CHUNK
    H = GDN_HEADS
    K = min(GDN_STEP, T // C)
    NS = T // (C * K)
    nn = (lambda n: NS - 1 - n) if rev else (lambda n: n)
    wide = lambda blk: pl.BlockSpec((K * C, H * HEAD_DIM), lambda n: (nn(n), blk))
    one = lambda off: pl.BlockSpec((K * C, HEAD_DIM), lambda n: (nn(n), off))
    vec = pl.BlockSpec((1, 128), lambda n: (0, 0))
    st = lambda rows: pl.BlockSpec((K, H, rows, rows), lambda n: (nn(n), 0, 0, 0))
    return K, NS, wide, one, vec, st


def _gdn_fwd(qkv, p, alog, dtb, nw, name):
    T = qkv.shape[0]
    C, H = GDN_CHUNK, GDN_HEADS
    N = T // C
    K, NS, wide, one, vec, st_spec = _gdn_specs(T, False)

    def body(q_ref, k_ref, v_ref, z_ref, ba_ref, al_ref, dt_ref, nw_ref, y_ref, sp_ref, ti_ref, vn_ref, o_ref, st):
        @pl.when(pl.program_id(0) == 0)
        def _():
            st[...] = jnp.zeros_like(st)

        f = _gdn_pre(_pairs(q_ref, K), _pairs(k_ref, K), _pairs(v_ref, K), ba_ref[...], al_ref[...], dt_ref[...])
        Tinv = _unit_lower_inverse(f["Lm"])
        ti_ref[...] = Tinv.reshape(K, H, C, C).astype(bf16)
        w = _dot(Tinv, f["kbg"], _NN3)
        u = _dot(Tinv, f["vb"], _NN3)
        S = st[...]
        vns, os_ = [], []
        for c in range(K):
            sl = slice(c * H, (c + 1) * H)
            sp_ref[c] = S
            vn_c = u[sl] - _dot(w[sl], S, _NN3)
            os_.append(_dot(f["qg"][sl], S, _NN3) + _dot(f["QK"][sl], vn_c, _NN3))
            S = S * f["cd"][sl] + _dot(f["kg"][sl], vn_c, _TN3)
            vns.append(vn_c)
        st[...] = S
        vn, o = jnp.concatenate(vns), jnp.concatenate(os_)
        r = lax.rsqrt(_rowsum(o * o) * (1.0 / HEAD_DIM) + EPS)
        _put_pairs(y_ref, o * r * nw_ref[...] * _silu(_pairs(z_ref, K)), K)
        _put_pairs(vn_ref, vn, K)
        _put_pairs(o_ref, o, K)

    wide_f32 = jax.ShapeDtypeStruct((T, H * HEAD_DIM), f32)
    return pl.pallas_call(
        body, grid=(NS,),
        in_specs=[wide(0), wide(1), wide(2), wide(3), one(4 * H), vec, vec, vec],
        out_specs=[wide(0), st_spec(HEAD_DIM), st_spec(C), wide(0), wide(0)],
        out_shape=[jax.ShapeDtypeStruct((T, H * HEAD_DIM), bf16), jax.ShapeDtypeStruct((N, H, HEAD_DIM, HEAD_DIM), f32),
                   jax.ShapeDtypeStruct((N, H, C, C), bf16), jax.ShapeDtypeStruct((T, H * HEAD_DIM), bf16), wide_f32],
        scratch_shapes=[pltpu.VMEM((H, HEAD_DIM, HEAD_DIM), f32)],
        compiler_params=_cparams(("arbitrary",)), name=name,
    )(qkv, qkv, qkv, p, p, alog, dtb, nw)


def _gdn_bwd(qkv, p, alog, dtb, nw, sprev, tinv, vn_all, o_all, dy_all, name):
    T = qkv.shape[0]
    C, H = GDN_CHUNK, GDN_HEADS
    N = T // C
    K, NS, wide, one, vec, st_spec = _gdn_specs(T, True)
    rs = lambda m: jnp.sum(m, axis=2, keepdims=True)

    def body(q_ref, k_ref, v_ref, z_ref, ba_ref, al_ref, dt_ref, nw_ref, sp_ref, ti_ref, vn_ref, o_ref, dy_ref,
             dqkv_ref, dz_ref, dba_ref, dal_ref, ddt_ref, dnw_ref, dst):
        @pl.when(pl.program_id(0) == 0)
        def _():
            dst[...] = jnp.zeros_like(dst)
            dal_ref[...] = jnp.zeros_like(dal_ref)
            ddt_ref[...] = jnp.zeros_like(ddt_ref)
            dnw_ref[...] = jnp.zeros_like(dnw_ref)

        ba, dtb_, nwv = ba_ref[...], dt_ref[...], nw_ref[...]
        v = _pairs(v_ref, K)
        f = _gdn_pre(_pairs(q_ref, K), _pairs(k_ref, K), v, ba, al_ref[...], dtb_)
        beta, kn, qn, kb, vb, kbg = f["beta"], f["kn"], f["qn"], f["kb"], f["vb"], f["kbg"]
        eG, eGl, cd, Di, Ds, QK, qg, kg = f["eG"], f["eGl"], f["cd"], f["Di"], f["Ds"], f["QK"], f["qg"], f["kg"]
        lane, ri, ci, rowc = f["lane"], f["ri"], f["ci"], f["rowc"]
        Tinv = ti_ref[...].reshape(K * H, C, C)
        S = sp_ref[...].reshape(K * H, HEAD_DIM, HEAD_DIM)
        w_ = _dot(Tinv, kbg, _NN3)
        vn, o = _pairs(vn_ref, K), _pairs(o_ref, K)
        z, dy = _pairs(z_ref, K), _pairs(dy_ref, K)
        r = lax.rsqrt(_rowsum(o * o) * (1.0 / HEAD_DIM) + EPS)
        nrm = o * r
        sz = _silu(z)
        dn = dy * nwv * sz
        _put_pairs(dz_ref, dy * nrm * nwv * _dsilu(z), K)
        dnw_ref[...] += jnp.sum(jnp.sum(dy * nrm * sz, axis=0), axis=0, keepdims=True)
        do = r * (dn - nrm * (_rowsum(dn * nrm) * (1.0 / HEAD_DIM)))
        dvn_do = _dot(QK, do, _TN3)
        dS_do = _dot(qg, do, _TN3)
        dqg = _dot(do, S, _NT3)
        dQK = _dot(do, vn, _NT3)
        dS = dst[...]
        dS1s, dvns = [None] * K, [None] * K
        for c in reversed(range(K)):
            sl = slice(c * H, (c + 1) * H)
            dS1s[c] = dS
            dvns[c] = _dot(kg[sl], dS, _NN3) + dvn_do[sl]
            dS = cd[sl] * dS + dS_do[sl] - _dot(w_[sl], dvns[c], _TN3)
        dst[...] = dS
        dS1, dvn = jnp.concatenate(dS1s), jnp.concatenate(dvns)
        dcd = jnp.sum(jnp.sum(S * dS1, axis=2, keepdims=True), axis=1, keepdims=True)
        dkg = _dot(vn, dS1, _NT3)
        dw = -_dot(dvn, S, _NT3)
        dqn = dqg * eG
        dkn = dkg * eGl
        deGl = rs(dkg * kn)
        dQKr = dQK * Di
        E = dQK * QK
        dqn = dqn + _dot(dQKr, kn, _NN3)
        dkn = dkn + _dot(dQKr, qn, _TN3)
        dT = _dot(dvn, vb, _NT3) + _dot(dw, kbg, _NT3)
        dvb = _dot(Tinv, dvn, _TN3)
        dkbg = _dot(Tinv, dw, _TN3)
        dkb = dkbg * eG
        deG = rs(dqg * qn + dkbg * kb)
        dL = -_dot(_dot(Tinv, dT, _TN3), Tinv, _NT3)
        dKK = dL * Ds
        E = E + dL * f["Lm"]
        dkb = dkb + _dot(dKK, kn, _NN3)
        dkn = dkn + _dot(dKK, kb, _TN3) + dkb * beta
        dbeta = rs(dkb * kn + dvb * v)
        _put_pairs(dqkv_ref, dvb * beta, K, 2 * H * HEAD_DIM)
        dG = rs(E) - rs(jnp.swapaxes(E, 1, 2)) + deG * eG - deGl * eGl
        dGl = jnp.sum(deGl * eGl, axis=1, keepdims=True) + dcd * cd
        dG = dG + jnp.where(rowc == C - 1, dGl, 0.0)
        qh = f["qh"]
        _put_pairs(dqkv_ref, (HEAD_DIM ** -0.5) * f["rq"] * (dqn - qh * _rowsum(dqn * qh)), K)
        _put_pairs(dqkv_ref, f["rk"] * (dkn - kn * _rowsum(dkn * kn)), K, H * HEAD_DIM)
        db = dbeta * beta * (1.0 - beta)
        triu = (ri <= ci).astype(f32)
        for c in range(K):
            db_all = jnp.where(lane == 0, db[c * H], 0.0)
            dG_all = jnp.where(lane == H, dG[c * H], 0.0)
            for h in range(1, H):
                db_all = db_all + jnp.where(lane == h, db[c * H + h], 0.0)
                dG_all = dG_all + jnp.where(lane == H + h, dG[c * H + h], 0.0)
            dg_all = _dot01(triu, dG_all, _NN)
            da_all = dg_all * (-f["ea"]) * _sigmoid(ba[c * C:(c + 1) * C] + dtb_)
            dba_ref[c * C:(c + 1) * C, :] = (db_all + da_all).astype(bf16)
            ddt_ref[...] += jnp.sum(da_all, axis=0, keepdims=True)
            dal_ref[...] += jnp.sum(dg_all * f["g_all"][c], axis=0, keepdims=True)

    small = jax.ShapeDtypeStruct((1, 128), f32)
    return pl.pallas_call(
        body, grid=(NS,),
        in_specs=[wide(0), wide(1), wide(2), wide(3), one(4 * H), vec, vec, vec, st_spec(HEAD_DIM), st_spec(C),
                  wide(0), wide(0), wide(0)],
        out_specs=[pl.BlockSpec((K * C, 3 * H * HEAD_DIM), lambda n: (NS - 1 - n, 0)), wide(3), one(0), vec, vec, vec],
        out_shape=[jax.ShapeDtypeStruct((T, 3 * H * HEAD_DIM), f32), jax.ShapeDtypeStruct((T, ODD_PAD), bf16),
                   jax.ShapeDtypeStruct((T, 128), bf16), small, small, small],
        scratch_shapes=[pltpu.VMEM((H, HEAD_DIM, HEAD_DIM), f32)],
        compiler_params=_cparams(("arbitrary",)), name=name,
    )(qkv, qkv, qkv, p, p, alog, dtb, nw, sprev, tinv, vn_all, o_all, dy_all)


def _lanes_from(x, s):
    return x if s % 128 == 0 else pltpu.roll(x, (128 - s) % 128, 1)


def _odd_assemble(g, name):
    R = g.shape[1]
    tr = min(256, R)
    n_blk = ODD_SHARD_PAD // 128

    def body(g_ref, o_ref):
        lane = lax.broadcasted_iota(jnp.int32, (tr, 128), 1)
        blk = lambda d, m: g_ref[d, :, m * 128:(m + 1) * 128]
        for gb in range(ODD_PAD // 128):
            c0 = 128 * gb
            if c0 >= ODD_IN:
                o_ref[:, c0:c0 + 128] = jnp.zeros((tr, 128), g.dtype)
                continue
            d0 = c0 // ODD_SHARD
            m0, sh = divmod(c0 - ODD_SHARD * d0, 128)
            take = min(128, ODD_SHARD * (d0 + 1) - c0)
            p = _lanes_from(blk(d0, m0), sh)
            if sh and m0 + 1 < n_blk:
                p = jnp.where(lane < 128 - sh, p, _lanes_from(blk(d0, m0 + 1), sh))
            if take < 128:
                nxt = pltpu.roll(blk(d0 + 1, 0), take, 1) if d0 + 1 < N_DEV else jnp.zeros((tr, 128), g.dtype)
                p = jnp.where(lane < take, p, nxt)
            o_ref[:, c0:c0 + 128] = p

    return pl.pallas_call(
        body, grid=(R // tr,),
        in_specs=[pl.BlockSpec((N_DEV, tr, ODD_SHARD_PAD), lambda i: (0, i, 0))],
        out_specs=pl.BlockSpec((tr, ODD_PAD), lambda i: (i, 0)),
        out_shape=jax.ShapeDtypeStruct((R, ODD_PAD), g.dtype),
        compiler_params=_cparams(("parallel",)), name=name,
    )(g)


def _odd_split(w, name):
    R = w.shape[0]
    tr = min(256, R)

    def body(w_ref, o_ref):
        lane = lax.broadcasted_iota(jnp.int32, (tr, 128), 1)
        blk = lambda gb: w_ref[:, gb * 128:(gb + 1) * 128]
        for d in range(N_DEV):
            for m in range(ODD_SHARD_PAD // 128):
                g0, sh = divmod(ODD_SHARD * d + 128 * m, 128)
                p = _lanes_from(blk(g0), sh)
                if sh and g0 + 1 < ODD_PAD // 128:
                    p = jnp.where(lane < 128 - sh, p, _lanes_from(blk(g0 + 1), sh))
                real = ODD_SHARD - 128 * m
                if real < 128:
                    p = jnp.where(lane < real, p, jnp.zeros_like(p))
                o_ref[d, :, m * 128:(m + 1) * 128] = p

    return pl.pallas_call(
        body, grid=(R // tr,),
        in_specs=[pl.BlockSpec((tr, ODD_PAD), lambda i: (i, 0))],
        out_specs=pl.BlockSpec((N_DEV, tr, ODD_SHARD_PAD), lambda i: (0, i, 0)),
        out_shape=jax.ShapeDtypeStruct((N_DEV, R, ODD_SHARD_PAD), w.dtype),
        compiler_params=_cparams(("parallel",)), name=name,
    )(w)


def _adam_tile(g, w_ref, m_ref, v_ref, go_ref, d_ref, mo_ref, vo_ref):
    c1 = 1.0 - ADAM_B1 ** ADAM_STEP
    c2 = 1.0 - ADAM_B2 ** ADAM_STEP
    mn = ADAM_B1 * m_ref[...] + (1.0 - ADAM_B1) * g
    vn = ADAM_B2 * v_ref[...] + (1.0 - ADAM_B2) * (g * g)
    go_ref[...] = g
    mo_ref[...] = mn
    vo_ref[...] = vn
    d_ref[...] = -ADAM_LR * ((mn / c1) / (jnp.sqrt(vn / c2) + ADAM_EPS) + ADAM_WD * w_ref[...])


def _adamw(w, gs, m, v, name, layer=None, prev=None):
    R, Cc = w.shape[-2:]
    S = gs.shape[0]
    tr = R
    if S * R * Cc * 4 > (4 << 20):
        for cand in (256, 128, 64, 32, 16, 8):
            if R % cand == 0 and R > cand:
                tr = cand
                break

    def body(w_ref, g_ref, m_ref, v_ref, *rest):
        g = g_ref[0].astype(f32)
        for s in range(1, S):
            g = g + g_ref[s].astype(f32)
        _adam_tile(g, w_ref, m_ref, v_ref, *rest[-4:])

    if layer is None:
        blk = pl.BlockSpec((tr, Cc), lambda i: (i, 0))
    else:
        blk = pl.BlockSpec((None, tr, Cc), lambda i: (layer, i, 0))
    out = jax.ShapeDtypeStruct(w.shape, f32)
    carried = [] if prev is None else list(prev)
    return pl.pallas_call(
        body, grid=(R // tr,),
        in_specs=[blk, pl.BlockSpec((S, tr, Cc), lambda i: (0, i, 0)), blk, blk]
        + [pl.BlockSpec(memory_space=pl.ANY)] * len(carried),
        out_specs=[blk] * 4, out_shape=[out] * 4,
        input_output_aliases={4 + j: j for j in range(len(carried))},
        compiler_params=_cparams(("parallel",)), name=name,
    )(w, gs, m, v, *carried)


def _adamw_column_major(w, gs, m, v, name):
    _, R, Cc = w.shape
    S = gs.shape[0]
    q = R // 128
    dense = lambda a: jnp.transpose(a, (2, 0, 1)).reshape(Cc * q, 128)
    back = lambda a: jnp.transpose(a.reshape(Cc, q, 128), (1, 2, 0)).reshape(1, R, Cc)

    def body(w_ref, g_ref, m_ref, v_ref, go_ref, d_ref, mo_ref, vo_ref, gt):
        for i in range(q):
            g = g_ref[0, i * 128:(i + 1) * 128, :].astype(f32)
            for s in range(1, S):
                g = g + g_ref[s, i * 128:(i + 1) * 128, :].astype(f32)
            gt[pl.ds(i, 128, stride=q), :] = g.T
        _adam_tile(gt[...], w_ref, m_ref, v_ref, go_ref, d_ref, mo_ref, vo_ref)

    blk = pl.BlockSpec((128 * q, 128), lambda j: (j, 0))
    outs = pl.pallas_call(
        body, grid=(pl.cdiv(Cc, 128),),
        in_specs=[blk, pl.BlockSpec((S, R, 128), lambda j: (0, 0, j)), blk, blk],
        out_specs=[blk] * 4, out_shape=[jax.ShapeDtypeStruct((Cc * q, 128), f32)] * 4,
        scratch_shapes=[pltpu.VMEM((128 * q, 128), f32)],
        compiler_params=_cparams(("parallel",)), name=name,
    )(dense(w), gs, dense(m), dense(v))
    return [back(o) for o in outs]


def _me():
    x, y, c = lax.axis_index("x"), lax.axis_index("y"), lax.axis_index("c")
    return x, y, c, 4 * x + 2 * y + c


def _peer(k):
    x, y, c, _ = _me()
    px = 1 - x if k & 4 else x
    py = 1 - y if k & 2 else y
    pc = 1 - c if k & 1 else c
    return (px, py, pc), 4 * px + 2 * py + pc


_HBM = pl.BlockSpec(memory_space=pltpu.HBM)
_SEM = pl.BlockSpec(memory_space=pltpu.SEMAPHORE)
_EFFECT = pltpu.SideEffectType.DATAFLOW_SIDE_EFFECTING


def _copy(src, land, ssem, rsem, k, blocked, landing_slot_of_peer):
    pid, pidx = _peer(k)
    slot = pidx if landing_slot_of_peer else _me()[3]
    return pltpu.make_async_remote_copy(src_ref=src.at[pidx] if blocked else src, dst_ref=land.at[slot],
                                        send_sem=ssem.at[k - 1], recv_sem=rsem.at[k - 1], device_id=pid, device_id_type=MESH)


def _own_copy(src, land, rsem, blocked):
    me = _me()[3]
    return pltpu.make_async_copy(src.at[me] if blocked else src, land.at[me], rsem.at[N_DEV - 1])


_ALL_PEERS = tuple(range(1, N_DEV))
_SAME_CORE_PEERS = (1, 2, 4, 6)
_OTHER_CORE_PEERS = (3, 5, 7)


def _relay_copies(land, ssem, rsem):
    sibling, _ = _peer(1)
    copies = []
    for i in range(len(land)):
        for j, k in enumerate(_OTHER_CORE_PEERS):
            sems = dict(send_sem=ssem.at[3 * i + j], recv_sem=rsem.at[3 * i + j], device_id=sibling, device_id_type=MESH)
            _, outgoing = _peer(k - 1)
            _, incoming = _peer(k)
            copies.append((pltpu.make_async_remote_copy(src_ref=land[i].at[outgoing], dst_ref=land[i].at[outgoing], **sems),
                           pltpu.make_async_remote_copy(src_ref=land[i].at[incoming], dst_ref=land[i].at[incoming], **sems)))
    return copies


def _relay_start(lands, name):
    n = len(lands)

    def body(*refs):
        for send, _ in _relay_copies(refs[:n], refs[n], refs[n + 1]):
            send.start()
        refs[-1][...] = jnp.zeros_like(refs[-1])

    sem = pltpu.SemaphoreType.DMA((3 * n,))
    res = pl.pallas_call(
        body, name=name,
        out_shape=(sem, sem) + tuple(pltpu.HBM(a.shape, a.dtype) for a in lands) + (jax.ShapeDtypeStruct((8, 128), f32),),
        in_specs=(_HBM,) * n, out_specs=(_SEM, _SEM) + (_HBM,) * n + (pl.BlockSpec(memory_space=pltpu.VMEM),),
        input_output_aliases={i: 2 + i for i in range(n)},
        compiler_params=pltpu.CompilerParams(has_side_effects=_EFFECT),
    )(*lands)
    return res[:-1], res[-1]


def _relay_wait(handle, after, name):
    ssem, rsem, lands = handle[0], handle[1], handle[2:]
    n = len(lands)
    after = tuple(after) if isinstance(after, (tuple, list)) else (after,)

    def body(*refs):
        for send, recv in _relay_copies(refs[:n], refs[n], refs[n + 1]):
            send.wait_send()
            recv.wait_recv()

    return pl.pallas_call(
        body, name=name, out_shape=tuple(pltpu.HBM(a.shape, a.dtype) for a in lands),
        in_specs=(_HBM,) * n + (_SEM, _SEM) + (pl.BlockSpec(memory_space=pl.ANY),) * len(after), out_specs=(_HBM,) * n,
        input_output_aliases={i: i for i in range(n)}, compiler_params=pltpu.CompilerParams(has_side_effects=_EFFECT),
    )(*lands, ssem, rsem, *after)


def _send_start(srcs, blocked, name, relayed=()):
    n = len(srcs)
    lands = [lax.empty(a.shape if blocked else (N_DEV,) + a.shape, a.dtype) for a in srcs]

    def body(*refs):
        src, land, sems, token = refs[:n], refs[n:2 * n], refs[2 * n:4 * n], refs[-1]
        for i in range(n):
            for k in (_SAME_CORE_PEERS if i in relayed else _ALL_PEERS):
                _copy(src[i], land[i], sems[2 * i], sems[2 * i + 1], k, blocked, False).start()
        for i in range(n):
            _own_copy(src[i], land[i], sems[2 * i + 1], blocked).start()
        token[...] = jnp.zeros_like(token)

    sems = (pltpu.SemaphoreType.DMA((N_DEV - 1,)), pltpu.SemaphoreType.DMA((N_DEV,)))
    hbm = lambda a: pltpu.with_memory_space_constraint(a, pltpu.HBM)
    res = pl.pallas_call(
        body, name=name,
        out_shape=sems * n + tuple(pltpu.HBM(a.shape, a.dtype) for a in srcs + lands)
        + (jax.ShapeDtypeStruct((8, 128), f32),),
        in_specs=(_HBM,) * (2 * n),
        out_specs=(_SEM,) * (2 * n) + (_HBM,) * (2 * n) + (pl.BlockSpec(memory_space=pltpu.VMEM),),
        input_output_aliases={j: 2 * n + j for j in range(2 * n)},
        compiler_params=pltpu.CompilerParams(has_side_effects=_EFFECT),
    )(*[hbm(a) for a in srcs], *[hbm(a) for a in lands])
    handles = [(res[2 * i], res[2 * i + 1], res[2 * n + i], res[3 * n + i]) for i in range(n)]
    return handles, res[-1]


def _send_wait(handle, blocked, after, name, relayed=False):
    ssem, rsem, src, land = handle
    after = tuple(after) if isinstance(after, (tuple, list)) else (after,)

    def body(src_ref, land_ref, ssem_ref, rsem_ref, *rest):
        for k in (_SAME_CORE_PEERS if relayed else _ALL_PEERS):
            cp = _copy(src_ref, land_ref, ssem_ref, rsem_ref, k, blocked, True)
            cp.wait_send()
            cp.wait_recv()
        _own_copy(src_ref, land_ref, rsem_ref, blocked).wait()

    return pl.pallas_call(
        body, name=name, out_shape=(pltpu.HBM(src.shape, src.dtype), pltpu.HBM(land.shape, land.dtype)),
        in_specs=(_HBM, _HBM, _SEM, _SEM) + (pl.BlockSpec(memory_space=pl.ANY),) * len(after), out_specs=(_HBM, _HBM),
        input_output_aliases={0: 0, 1: 1}, compiler_params=pltpu.CompilerParams(has_side_effects=_EFFECT),
    )(src, land, ssem, rsem, *after)


def _block_diag(w):
    nb, bs = w.shape[0], w.shape[1]
    eye = jnp.eye(nb, dtype=w.dtype)
    return (eye[:, None, :, None] * w[:, :, None, :]).reshape(nb * bs, nb * bs)


_SQUARE_TILES = dict(tm=1024, tn=1024, tk=1024)


def _mlp_fwd(x, hm, wu, wd, tag, epilogue, extras, outs, between=None):
    (r,) = _matmul(hm, wu, "nn", outs=[bf16], epilogue=lambda acc: (jnp.maximum(acc, 0.0),), name=f"mlp_up_{tag}")
    res = _matmul(r, wd, "nn", outs=outs, extras=(x,) + tuple(extras), epilogue=epilogue, a_map=jnp.square,
                  after=() if between is None else (between(r),), name=f"mlp_down_{tag}", **_SQUARE_TILES)
    return res, (hm, r)


def _mlp_bwd(x, nw, wu, wd, saved, dxo, dxo_b, tag, sink, urgent):
    hm, r = saved
    (du,) = _matmul(dxo_b, wd, "nt", outs=[bf16], extras=(r,), epilogue=lambda acc, rr: (acc * (2.0 * rr.astype(f32)),),
                    name=f"mlp_dact_{tag}")
    (dwd,) = _matmul(r, dxo_b, "tn", outs=[bf16], a_map=jnp.square, name=f"mlp_dwd_{tag}", **_SQUARE_TILES)
    down = {f"w_down{tag}": dwd.reshape(N_DEV, D_FF // N_DEV, D_MODEL)}
    if urgent:
        tok = sink(down)
        (dwu,) = _matmul(hm, du, "tn", outs=[bf16], shard_cols=2, after=(tok,), name=f"mlp_dwu_{tag}")
        tok = sink({f"w_up{tag}": dwu})
    else:
        (dwu,) = _matmul(hm, du, "tn", outs=[bf16], shard_cols=2, name=f"mlp_dwu_{tag}")
        tok = sink({f"w_up{tag}": dwu, **down})
    return _matmul(du, wu, "nt", outs=_RMS_BWD_OUTS, extras=(x, dxo, nw), epilogue=_rms_bwd_ep, after=(tok,),
                   name=f"mlp_dh_{tag}", **_SQUARE_TILES)


def _local_step(x, tgt, P, weight, sink, ahead):
    T = x.shape[0]
    cos, sin = _rope_tables(T)
    rtab = _ret_tables()
    row = lambda a: a.reshape(1, -1)
    mix_nw, mlp_nw = P["mixer_norm_w"], P["mlp_norm_w"]
    wr_bd, wi_bd = _block_diag(P["lru_w_r"]), _block_diag(P["lru_w_i"])
    lru_b, lru_br, lru_bi, lru_lam = row(P["lru_conv_b"]), row(P["lru_b_r"]), row(P["lru_b_i"]), row(P["lru_lambda"])
    pad16 = lambda a: jnp.pad(a.reshape(1, GDN_HEADS), ((0, 0), (GDN_HEADS, 128 - 2 * GDN_HEADS)))
    alog, dtb = pad16(P["gdn_a_log"]), pad16(P["gdn_dt_bias"])
    gnw = row(P["gdn_norm_w"])

    x0 = x
    h0 = _rms_fwd(x0, mix_nw[0:1], "rms_mix_0")
    w_ie = weight("w_in_even", (h0, cos, sin, wr_bd, wi_bd))
    (pe,) = _matmul(h0, w_ie, "nn", outs=[f32], name="in_even")
    mix0, o_ret, s_ret = _ret_fwd(pe, cos, sin, rtab, "ret_fwd")
    w_lc = weight("lru_conv_w", pe)
    xc = _conv_fwd(pe, 4, w_lc, lru_b, False, "lru_conv_fwd")
    mix0, h_lru = _lru_fwd(xc, pe, 20, wr_bd, wi_bd, lru_br, lru_bi, lru_lam, mix0, "lru_fwd")
    w_oe = weight("w_out_even", mix0)
    tok = ahead(("w_up0", "w_down0"), mix0)
    x1, hm0 = _matmul(mix0, w_oe, "nn", outs=[f32, bf16], extras=(x0, mlp_nw[0:1]), epilogue=_residual_rms_ep,
                      after=(tok,), name="out_even", tm=1024, tn=D_MODEL)
    w_u0, w_d0 = weight("w_up0", x1), weight("w_down0", x1)
    (x2, h1), mlp0 = _mlp_fwd(x1, hm0, w_u0, w_d0, "0", _residual_rms_ep, (mix_nw[1:2],), [f32, bf16],
                              between=lambda r: ahead(("w_in_odd",), r))
    w_io = weight("w_in_odd", h1)
    (po,) = _matmul(h1, w_io, "nn", outs=[f32], tm=2048, tn=ODD_PAD // 3, name="in_odd")
    w_gc = weight("gdn_conv_w", po)
    qkv = _conv_fwd(po, 0, w_gc, None, True, "gdn_conv_fwd")
    y_gdn, s_gdn, ti_gdn, vn_gdn, o_gdn = _gdn_fwd(qkv, po, alog, dtb, gnw, "gdn_fwd")
    w_oo = weight("w_out_odd", y_gdn)
    x3, hm1 = _matmul(y_gdn, w_oo, "nn", outs=[f32, bf16], extras=(x2, mlp_nw[1:2]), epilogue=_residual_rms_ep,
                      name="out_odd", tm=1024, tn=D_MODEL)
    w_u1, w_d1 = weight("w_up1", x3), weight("w_down1", x3)
    (loss, dx4, dx4_b, d_final), mlp1 = _mlp_fwd(x3, hm1, w_u1, w_d1, "1", _loss_ep, (row(P["final_norm_w"]), tgt),
                                                 _LOSS_OUTS)
    dx3, dx3_b, d_mlp_nw1 = _mlp_bwd(x3, mlp_nw[1:2], w_u1, w_d1, mlp1, dx4, dx4_b, "1", sink, False)
    (dy_gdn,) = _matmul(dx3_b, w_oo, "nt", outs=[f32], name="out_odd_dx")
    (d_woo,) = _matmul(y_gdn, dx3_b, "tn", outs=[bf16], name="out_odd_dw")
    dqkv, dpo, dba, d_alog, d_dtb, d_gnw = _gdn_bwd(qkv, po, alog, dtb, gnw, s_gdn, ti_gdn, vn_gdn, o_gdn, dy_gdn,
                                                  "gdn_bwd")
    dpo, d_gconv, _ = _conv_bwd(po, 0, w_gc, None, True, dqkv, dpo, "gdn_conv_bwd")
    dpo = lax.dynamic_update_slice(dpo, dba, (0, 4 * D_MODEL))
    (d_wio,) = _matmul(h1, dpo, "tn", outs=[bf16], tn=ODD_PAD // 3, name="in_odd_dw")
    tok = sink(dict(w_out_odd=d_woo.reshape(N_DEV, D_MODEL // N_DEV, D_MODEL), w_in_odd=_odd_split(d_wio, "w_in_odd_split")))
    dx2, dx2_b, d_mix_nw1 = _matmul(dpo, w_io, "nt", outs=_RMS_BWD_OUTS, extras=(x2, dx3, mix_nw[1:2]),
                                    epilogue=_rms_bwd_ep, after=(tok,), tm=1024, tn=1024, tk=ODD_PAD // 3, name="in_odd_dx")
    dx1, dx1_b, d_mlp_nw0 = _mlp_bwd(x1, mlp_nw[0:1], w_u0, w_d0, mlp0, dx2, dx2_b, "0", sink, True)
    (d_woe,) = _matmul(mix0, dx1_b, "tn", outs=[bf16], name="out_even_dw")
    tok = sink(dict(w_out_even=d_woe.reshape(N_DEV, D_MODEL // N_DEV, D_MODEL)))
    (dmix0,) = _matmul(dx1_b, w_oe, "nt", outs=[f32], name="out_even_dx")
    dpe = _ret_bwd(pe, cos, sin, rtab, o_ret, s_ret, dmix0, "ret_bwd")
    dxc, dpe, d_wr, d_wi, d_br, d_bi, d_lam = _lru_bwd(xc, pe, 20, wr_bd, wi_bd, lru_br, lru_bi, lru_lam + tok[0:1, 0:1],
                                                       h_lru, dmix0, 4, dpe, "lru_bwd")
    dpe, d_lconv, d_lconv_b = _conv_bwd(pe, 4, w_lc, lru_b, False, dxc, dpe, "lru_conv_bwd")
    G = dict(
        mlp_norm_w=jnp.concatenate([d_mlp_nw0, d_mlp_nw1], axis=0),
        final_norm_w=d_final.reshape(-1),
        lru_conv_w=d_lconv, lru_conv_b=d_lconv_b.reshape(-1),
        lru_w_r=d_wr, lru_b_r=d_br.reshape(-1), lru_w_i=d_wi, lru_b_i=d_bi.reshape(-1),
        lru_lambda=d_lam.reshape(-1), gdn_conv_w=d_gconv,
        gdn_a_log=d_alog[0, GDN_HEADS:2 * GDN_HEADS], gdn_dt_bias=d_dtb[0, GDN_HEADS:2 * GDN_HEADS],
        gdn_norm_w=d_gnw.reshape(-1),
    )
    packed = _pack([G[k] for k in _SMALL] + [d_lconv, d_gconv, loss[0, 0:1]])
    tok = sink(dict(small=jnp.broadcast_to(packed[None], (N_DEV,) + packed.shape)))
    (d_wie,) = _matmul(h0, dpe, "tn", outs=[bf16], shard_cols=2, after=(tok,), name="in_even_dw")
    tok = sink(dict(w_in_even=d_wie))
    dx0, _, d_mix_nw0 = _matmul(dpe, w_ie, "nt", outs=_RMS_BWD_OUTS, extras=(x0, dx1, mix_nw[0:1]), epilogue=_rms_bwd_ep,
                                after=(tok,), name="in_even_dx", **_SQUARE_TILES)
    G["mixer_norm_w"] = jnp.concatenate([d_mix_nw0, d_mix_nw1], axis=0)
    return loss, dx0, G


_SMALL = ["mlp_norm_w", "final_norm_w", "lru_conv_b", "lru_w_r", "lru_b_r", "lru_w_i", "lru_b_i",
          "lru_lambda", "gdn_a_log", "gdn_dt_bias", "gdn_norm_w"]
_PACK_ROWS = 688


def _pack(parts):
    flat = jnp.concatenate([p.reshape(-1) for p in parts])
    return jnp.pad(flat, (0, _PACK_ROWS * 128 - flat.shape[0])).reshape(_PACK_ROWS, 128)


def _unpack(packed, shapes):
    flat = packed.reshape(-1)
    out, off = [], 0
    for s in shapes:
        n = int(np.prod(s))
        out.append(flat[off:off + n].reshape(s))
        off += n
    return out


def kernel(x, mixer_norm_w, mlp_norm_w, final_norm_w, w_in_even, lru_conv_w, lru_conv_b, lru_w_r, lru_b_r, lru_w_i, lru_b_i, lru_lambda, w_out_even, w_in_odd, gdn_conv_w, gdn_a_log, gdn_dt_bias, gdn_norm_w, w_out_odd, w_up, w_down, loss_target, m_mixer_norm_w, m_mlp_norm_w, m_final_norm_w, m_w_in_even, m_lru_conv_w, m_lru_conv_b, m_lru_w_r, m_lru_b_r, m_lru_w_i, m_lru_b_i, m_lru_lambda, m_w_out_even, m_w_in_odd, m_gdn_conv_w, m_gdn_a_log, m_gdn_dt_bias, m_gdn_norm_w, m_w_out_odd, m_w_up, m_w_down, v_mixer_norm_w, v_mlp_norm_w, v_final_norm_w, v_w_in_even, v_lru_conv_w, v_lru_conv_b, v_lru_w_r, v_lru_b_r, v_lru_w_i, v_lru_b_i, v_lru_lambda, v_w_out_even, v_w_in_odd, v_gdn_conv_w, v_gdn_a_log, v_gdn_dt_bias, v_gdn_norm_w, v_w_out_odd, v_w_up, v_w_down):
    Pw = dict(mixer_norm_w=mixer_norm_w, mlp_norm_w=mlp_norm_w, final_norm_w=final_norm_w, w_in_even=w_in_even,
              lru_conv_w=lru_conv_w, lru_conv_b=lru_conv_b, lru_w_r=lru_w_r, lru_b_r=lru_b_r, lru_w_i=lru_w_i,
              lru_b_i=lru_b_i, lru_lambda=lru_lambda, w_out_even=w_out_even, w_in_odd=w_in_odd, gdn_conv_w=gdn_conv_w,
              gdn_a_log=gdn_a_log, gdn_dt_bias=gdn_dt_bias, gdn_norm_w=gdn_norm_w, w_out_odd=w_out_odd, w_up=w_up,
              w_down=w_down)
    Pm = dict(mixer_norm_w=m_mixer_norm_w, mlp_norm_w=m_mlp_norm_w, final_norm_w=m_final_norm_w, w_in_even=m_w_in_even,
              lru_conv_w=m_lru_conv_w, lru_conv_b=m_lru_conv_b, lru_w_r=m_lru_w_r, lru_b_r=m_lru_b_r, lru_w_i=m_lru_w_i,
              lru_b_i=m_lru_b_i, lru_lambda=m_lru_lambda, w_out_even=m_w_out_even, w_in_odd=m_w_in_odd,
              gdn_conv_w=m_gdn_conv_w, gdn_a_log=m_gdn_a_log, gdn_dt_bias=m_gdn_dt_bias, gdn_norm_w=m_gdn_norm_w,
              w_out_odd=m_w_out_odd, w_up=m_w_up, w_down=m_w_down)
    Pv = dict(mixer_norm_w=v_mixer_norm_w, mlp_norm_w=v_mlp_norm_w, final_norm_w=v_final_norm_w, w_in_even=v_w_in_even,
              lru_conv_w=v_lru_conv_w, lru_conv_b=v_lru_conv_b, lru_w_r=v_lru_w_r, lru_b_r=v_lru_b_r, lru_w_i=v_lru_w_i,
              lru_b_i=v_lru_b_i, lru_lambda=v_lru_lambda, w_out_even=v_w_out_even, w_in_odd=v_w_in_odd,
              gdn_conv_w=v_gdn_conv_w, gdn_a_log=v_gdn_a_log, gdn_dt_bias=v_gdn_dt_bias, gdn_norm_w=v_gdn_norm_w,
              w_out_odd=v_w_out_odd, w_up=v_w_up, w_down=v_w_down)
    me = _me()[3]
    T = x.shape[1]

    cols = lambda g: jnp.transpose(g, (1, 0, 2)).reshape(g.shape[1], -1)
    rows = lambda g: g.reshape(-1, g.shape[2])
    wide = lambda g: _odd_assemble(g, "w_in_odd_assemble")
    as_is = lambda g: g
    relay_groups = (("w_in_even",), ("w_up0", "w_down0"), ("w_in_odd",))
    relayed = sum(relay_groups, ())
    (first,), started = _send_start([w_in_even[0].astype(bf16)], False, "gather_start_w_in_even", relayed=[0])
    cast = lambda a: (a + started[0:1, 0:1]).astype(bf16)
    odd_shard = jnp.pad(cast(w_in_odd[0]), ((0, 0), (0, ODD_SHARD_PAD - ODD_SHARD)))
    gather = dict(
        w_in_even=(None, cols), lru_conv_w=(lru_conv_w[0], cols),
        w_out_even=(cast(w_out_even[0]), rows), w_up0=(cast(w_up[0]), as_is), w_down0=(cast(w_down[0]), rows),
        w_in_odd=(odd_shard, wide), gdn_conv_w=(gdn_conv_w[0], cols),
        w_out_odd=(cast(w_out_odd[0]), rows), w_up1=(cast(w_up[1]), as_is), w_down1=(cast(w_down[1]), rows))
    later = [name for name in gather if name != "w_in_even"]
    handles, tok = _send_start([gather[name][0] for name in later], False, "gather_start",
                               relayed=[i for i, name in enumerate(later) if name in relayed])
    handles = dict(zip(later, handles), w_in_even=first)
    relays, landed, full = {}, {}, {}

    def ahead(group, after):
        lands = [_send_wait(handles[n], False, after, f"gather_wait_{n}", relayed=True)[1] for n in group]
        relays[group], token = _relay_start(lands, "relay_start_" + "_".join(group))
        return token

    def weight(name, after):
        if name not in landed and name in relayed:
            group = next(g for g in relay_groups if name in g)
            if group not in relays:
                ahead(group, after)
            landed.update(zip(group, _relay_wait(relays[group], after, "relay_wait_" + "_".join(group))))
        elif name not in landed:
            landed[name] = _send_wait(handles[name], False, after, f"gather_wait_{name}")[1]
        if name not in full:
            full[name] = gather[name][1](landed[name])
        return full[name]

    P = {k: Pw[k] for k in ("mlp_norm_w", "final_norm_w")}
    P["mixer_norm_w"] = mixer_norm_w + tok[0:1, 0:1]
    for k in ("lru_w_r", "lru_w_i", "lru_conv_b", "lru_b_r", "lru_b_i", "lru_lambda", "gdn_a_log", "gdn_dt_bias", "gdn_norm_w"):
        P[k] = Pw[k][0]

    sent = {}

    def sink(grads):
        hs, token = _send_start(list(grads.values()), True, "grads_start_" + "_".join(grads))
        sent.update(zip(grads, hs))
        return token

    loss, dx, G = _local_step(x[0], loss_target[0], P, weight, sink, ahead)
    lanes = lambda a: a.reshape(-1, 128)
    sink(dict(mixer_norm_w=jnp.broadcast_to(lanes(G["mixer_norm_w"])[None], (N_DEV, 2 * D_MODEL // 128, 128))))

    def received(name, after=dx):
        return _send_wait(sent[name], True, after, f"grads_wait_{name}")[1]

    out = {}
    nff = D_FF // N_DEV

    def whole(name, gs):
        out[name] = tuple(_adamw(Pw[name], gs, Pm[name], Pv[name], f"adamw_{name}", layer=0))

    def layers(name):
        res = None
        for l in range(2):
            res = _adamw(Pw[name], received(f"{name}{l}"), Pm[name], Pv[name], f"adamw_{name}{l}", layer=l, prev=res)
        out[name] = tuple(res)

    layers("w_up")
    layers("w_down")
    whole("w_out_odd", received("w_out_odd"))
    out["w_in_odd"] = tuple(_adamw_column_major(w_in_odd, received("w_in_odd"), m_w_in_odd, v_w_in_odd, "adamw_w_in_odd"))
    whole("w_out_even", received("w_out_even"))
    small_shapes = [Pw[k].shape for k in _SMALL]
    pw, pm, pv = (_pack([Q[k] for k in _SMALL]) for Q in (Pw, Pm, Pv))
    sg, sd, sm, sv = _adamw(pw, received("small", out["w_out_even"][1]), pm, pv, "adamw_small")
    for arrs_i, packed_out in enumerate((sg, sd, sm, sv)):
        for k, a in zip(_SMALL, _unpack(packed_out, small_shapes)):
            out.setdefault(k, [None] * 4)[arrs_i] = a
    whole("w_in_even", received("w_in_even", sd))
    out["mixer_norm_w"] = tuple(
        a.reshape(mixer_norm_w.shape) for a in
        _adamw(lanes(mixer_norm_w), received("mixer_norm_w", out["w_in_even"][1]), lanes(m_mixer_norm_w),
               lanes(v_mixer_norm_w), "adamw_mixer_norm_w"))
    n_small = sum(int(np.prod(s)) for s in small_shapes)
    gflat = sg.reshape(-1)
    g_lconv = gflat[n_small:n_small + CONV_K * LRU_WIDTH].reshape(CONV_K, LRU_WIDTH)
    g_gconv = gflat[n_small + CONV_K * LRU_WIDTH:n_small + CONV_K * (LRU_WIDTH + 3072)].reshape(CONV_K, 3072)
    whole("lru_conv_w", lax.dynamic_slice_in_dim(g_lconv, me * 64, 64, axis=1)[None])
    whole("gdn_conv_w", lax.dynamic_slice_in_dim(g_gconv, me * 384, 384, axis=1)[None])

    names = ["mixer_norm_w", "mlp_norm_w", "final_norm_w", "w_in_even", "lru_conv_w", "lru_conv_b", "lru_w_r", "lru_b_r",
             "lru_w_i", "lru_b_i", "lru_lambda", "w_out_even", "w_in_odd", "gdn_conv_w", "gdn_a_log", "gdn_dt_bias",
             "gdn_norm_w", "w_out_odd", "w_up", "w_down"]
    total = gflat[n_small + CONV_K * (LRU_WIDTH + 3072)]
    res = [total, dx[None]]
    for j in range(4):
        res += [out[k][j] for k in names]
    return tuple(res)
```

```python
import math

import numpy as np
import jax
import jax.numpy as jnp
from jax import lax
from jax.experimental import pallas as pl
from jax.experimental.pallas import tpu as pltpu

f32 = jnp.float32
bf16 = jnp.bfloat16

N_DEV = 8
D_MODEL = 1024
D_FF = 4096
EPS = 1e-6
RET_HEADS = 4
RET_CHUNK = 128
RET_STEP = 4
ROPE_THETA = 10000.0
LRU_WIDTH = 512
LRU_C = 8.0
GDN_HEADS = 8
GDN_CHUNK = 64
GDN_STEP = 4
HEAD_DIM = 128
ODD_IN = 4112
ODD_PAD = 4224
ODD_SHARD = ODD_IN // N_DEV
ODD_SHARD_PAD = 640
ADAM_LR, ADAM_B1, ADAM_B2, ADAM_EPS, ADAM_WD, ADAM_STEP = 0.001, 0.9, 0.999, 1e-08, 0.01, 10
VMEM_LIMIT = 56 * 1024 * 1024

_NN = (((1,), (0,)), ((), ()))
_NT = (((1,), (1,)), ((), ()))
_TN = (((0,), (0,)), ((), ()))
MESH = pl.DeviceIdType.MESH


def _cparams(sem):
    return pltpu.CompilerParams(dimension_semantics=sem, vmem_limit_bytes=VMEM_LIMIT)


def _dot(a, b, dn):
    return lax.dot_general(a.astype(bf16), b.astype(bf16), dn, preferred_element_type=f32)


def _dot01(a01, b, dn):
    a = a01.astype(bf16)
    b0 = b.astype(bf16)
    r1 = b - b0.astype(f32)
    b1 = r1.astype(bf16)
    b2 = (r1 - b1.astype(f32)).astype(bf16)
    d = lambda q: lax.dot_general(a, q, dn, preferred_element_type=f32)
    return d(b0) + (d(b1) + d(b2))


def _sigmoid(x):
    return jax.nn.sigmoid(x)


def _silu(x):
    return x * _sigmoid(x)


def _dsilu(x):
    s = _sigmoid(x)
    return s * (1.0 + x * (1.0 - s))


def _softplus(x):
    return jnp.maximum(x, 0.0) + jnp.log1p(jnp.exp(-jnp.abs(x)))


_GELU_C = math.sqrt(2.0 / math.pi)


def _gelu(y):
    return 0.5 * y * (1.0 + jnp.tanh(_GELU_C * (y + 0.044715 * y * y * y)))


def _dgelu(y):
    t = jnp.tanh(_GELU_C * (y + 0.044715 * y * y * y))
    return 0.5 * (1.0 + t) + 0.5 * y * (1.0 - t * t) * _GELU_C * (1.0 + 3.0 * 0.044715 * y * y)


def _matmul(a, b, form, *, outs, name, epilogue=None, extras=(), tm=4096, tn=512, tk=1024, shard_cols=0, a_map=None,
            after=()):
    if form == "tn":
        K, M = a.shape
    else:
        M, K = a.shape
    per_step = 1
    if b.ndim == 3:
        assert form in ("nn", "nt"), name
        N = b.shape[1] if form == "nt" else N_DEV * b.shape[2]
        if form == "nn":
            tn = b.shape[2]
        else:
            per_step = max(1, tk // b.shape[2])
            tk = per_step * b.shape[2]
    else:
        N = b.shape[0] if form == "nt" else b.shape[1]
    ns = N // N_DEV
    if shard_cols:
        tn = ns * shard_cols
    tm, tn, tk = min(tm, M), min(tn, N), min(tk, K)
    assert M % tm == 0 and N % tn == 0 and K % tk == 0, (name, M, N, K, tm, tn, tk)
    nk = K // tk
    dn = {"nn": _NN, "nt": _NT, "tn": _TN}[form]
    if form == "tn":
        a_spec = pl.BlockSpec((tk, tm), lambda i, j, k: (k, i))
    else:
        a_spec = pl.BlockSpec((tm, tk), lambda i, j, k: (i, k))
    if b.ndim == 3:
        b_spec = (pl.BlockSpec((per_step, tn, tk // per_step), lambda i, j, k: (k, j, 0)) if form == "nt"
                  else pl.BlockSpec((None, tk, tn), lambda i, j, k: (j, k, 0)))
    elif form == "nt":
        b_spec = pl.BlockSpec((tn, tk), lambda i, j, k: (j, k))
    else:
        b_spec = pl.BlockSpec((tk, tn), lambda i, j, k: (k, j))
    e_spec = pl.BlockSpec((tm, tn), lambda i, j, k: (i, j))
    v_spec = pl.BlockSpec((1, tn), lambda i, j, k: (0, j))
    if shard_cols:
        o_spec = pl.BlockSpec((shard_cols, tm, ns), lambda i, j, k: (j, i, 0))
        o_shape = (N_DEV, M, ns)
    else:
        o_spec = e_spec
        o_shape = (M, N)
    n_ex = len(extras)
    n_in = 2 + n_ex + len(after)
    sums = [isinstance(o, tuple) for o in outs]
    assert not any(sums) or tn == N, name

    def finish(acc, ex, o_refs, row_tile):
        vals = (acc,) if epilogue is None else epilogue(acc, *[e[...] for e in ex])
        for r, v, is_sum in zip(o_refs, vals, sums):
            if is_sum:
                @pl.when(row_tile == 0)
                def _(r=r, v=v):
                    r[...] = v.astype(r.dtype)

                @pl.when(row_tile > 0)
                def _(r=r, v=v):
                    r[...] += v.astype(r.dtype)
            elif shard_cols:
                for s in range(shard_cols):
                    r[s] = v[:, s * ns:(s + 1) * ns].astype(r.dtype)
            else:
                r[...] = v.astype(r.dtype)

    def prod(a_ref, b_ref):
        if b.ndim == 3 and form == "nt":
            w = tk // per_step
            return sum(_dot(a_ref[:, s * w:(s + 1) * w], b_ref[s], dn) for s in range(1, per_step)) + _dot(a_ref[:, 0:w], b_ref[0], dn)
        av = a_ref[...]
        return _dot(av if a_map is None else a_map(av), b_ref[...], dn)

    def body_one(*refs):
        finish(prod(*refs[:2]), refs[2:2 + n_ex], refs[n_in:], pl.program_id(0))

    def body_acc(*refs):
        a_ref, b_ref = refs[:2]
        acc = refs[-1]
        k = pl.program_id(2)
        row_tile = pl.program_id(0)

        @pl.when(k == 0)
        def _():
            acc[...] = prod(a_ref, b_ref)

        @pl.when((k > 0) & (k < nk - 1))
        def _():
            acc[...] += prod(a_ref, b_ref)

        @pl.when(k == nk - 1)
        def _():
            finish(acc[...] + prod(a_ref, b_ref), refs[2:2 + n_ex], refs[n_in:-1], row_tile)

    return pl.pallas_call(
        body_one if nk == 1 else body_acc, grid=(M // tm, N // tn, nk),
        in_specs=[a_spec, b_spec] + [v_spec if e.shape[0] == 1 else e_spec for e in extras]
        + [pl.BlockSpec(memory_space=pl.ANY)] * len(after),
        out_specs=[v_spec if s else o_spec for s in sums],
        out_shape=[jax.ShapeDtypeStruct((1, N), o[1]) if s else jax.ShapeDtypeStruct(o_shape, o) for o, s in zip(outs, sums)],
        scratch_shapes=[] if nk == 1 else [pltpu.VMEM((tm, tn), f32)],
        compiler_params=_cparams(("arbitrary" if any(sums) else "parallel", "parallel", "arbitrary")), name=name,
    )(a, b, *extras, *after)


def _rms_fwd(x, w, name):
    T, D = x.shape
    tt = min(512, T)

    def body(x_ref, w_ref, h_ref):
        xv = x_ref[...]
        r = lax.rsqrt(jnp.mean(xv * xv, axis=1, keepdims=True) + EPS)
        h_ref[...] = (xv * r * w_ref[...]).astype(bf16)

    return pl.pallas_call(
        body, grid=(T // tt,),
        in_specs=[pl.BlockSpec((tt, D), lambda i: (i, 0)), pl.BlockSpec((1, D), lambda i: (0, 0))],
        out_specs=pl.BlockSpec((tt, D), lambda i: (i, 0)),
        out_shape=jax.ShapeDtypeStruct((T, D), bf16),
        compiler_params=_cparams(("parallel",)), name=name,
    )(x, w)


def _residual_rms_ep(acc, res, w):
    x = res + acc
    r = lax.rsqrt(jnp.mean(x * x, axis=1, keepdims=True) + EPS)
    return x, x * r * w


_RMS_BWD_OUTS = [f32, bf16, ("sum", f32)]


def _rms_bwd_ep(dh, x, dres, w):
    r = lax.rsqrt(jnp.mean(x * x, axis=1, keepdims=True) + EPS)
    xn = x * r
    dhw = dh * w
    dx = dres + r * (dhw - xn * jnp.mean(dhw * xn, axis=1, keepdims=True))
    return dx, dx, jnp.sum(dh * xn, axis=0, keepdims=True)


_LOSS_OUTS = [("sum", f32), f32, bf16, ("sum", f32)]


def _loss_ep(acc, res, w, tgt):
    x = res + acc
    D = x.shape[1]
    r = lax.rsqrt(jnp.mean(x * x, axis=1, keepdims=True) + EPS)
    xn = x * r
    e = xn * w - tgt
    loss = 0.5 * jnp.sum(jnp.mean(e * e, axis=1, keepdims=True), axis=0, keepdims=True)
    dy = e * (1.0 / D)
    dyw = dy * w
    dx = r * (dyw - xn * jnp.mean(dyw * xn, axis=1, keepdims=True))
    return jnp.broadcast_to(loss, (1, D)), dx, dx, jnp.sum(dy * xn, axis=0, keepdims=True)


def _ret_tables():
    H, C = RET_HEADS, RET_CHUNK
    lg = np.log1p(-np.exp2(-5.0 - np.arange(H, dtype=np.float32))).astype(np.float32)
    idx = np.arange(C, dtype=np.float32)
    diff = idx[:, None] - idx[None, :]
    causal = diff >= 0
    dm = np.where(causal[None], np.exp(lg[:, None, None] * np.where(causal, diff, 0.0)[None]), 0.0)
    qd = np.exp(lg[:, None] * (idx[None, :] + 1.0))
    kd = np.exp(lg[:, None] * (C - 1.0 - idx[None, :]))
    cg = np.exp(lg * C)
    tab = np.zeros((H, 4, C, HEAD_DIM), np.float32)
    tab[:, 0] = dm
    tab[:, 1] = qd[:, :, None]
    tab[:, 2] = kd[:, :, None]
    tab[:, 3] = cg[:, None, None]
    return jnp.asarray(tab)


def _rope_tables(T):
    half = HEAD_DIM // 2
    inv = np.float32(ROPE_THETA) ** (-np.arange(half, dtype=np.float32) / np.float32(half))
    ang = np.arange(T, dtype=np.float32)[:, None] * inv[None, :]
    c, s = np.cos(ang), np.sin(ang)
    return jnp.asarray(np.concatenate([c, c], axis=1)), jnp.asarray(np.concatenate([-s, s], axis=1))


def _rope(x, cos, sin):
    return x * cos + pltpu.roll(x, HEAD_DIM // 2, 1) * sin


def _unrope(y, cos, sin):
    return y * cos + pltpu.roll(y * sin, HEAD_DIM // 2, 1)


def _stack_heads(ref, H, f=None):
    parts = [ref[:, h * HEAD_DIM:(h + 1) * HEAD_DIM] for h in range(H)]
    return jnp.stack(parts if f is None else [f(a) for a in parts])


def _ret_fwd(p, cos, sin, tab, name):
    T = p.shape[0]
    C, H = RET_CHUNK, RET_HEADS
    N = T // C
    K = min(RET_STEP, N)
    NS = N // K
    scale = HEAD_DIM ** -0.5

    def body(q_ref, k_ref, v_ref, g_ref, c_ref, s_ref, t_ref, y_ref, o_ref, sp_ref, st):
        @pl.when(pl.program_id(0) == 0)
        def _():
            st[...] = jnp.zeros_like(st)

        dm, qd, kd, cg = t_ref[:, 0], t_ref[:, 1], t_ref[:, 2], t_ref[:, 3]
        S = st[...]
        for c in range(K):
            rows = pl.ds(c * C, C)
            cos_, sin_ = c_ref[rows, :], s_ref[rows, :]
            rot = lambda a: _rope(a, cos_, sin_)
            q = _stack_heads(q_ref.at[rows, :], H, rot)
            k = _stack_heads(k_ref.at[rows, :], H, rot) * scale
            v = _stack_heads(v_ref.at[rows, :], H)
            P = _dot(q, k, _NT3) * dm
            o = _dot(P, v, _NN3) + _dot(q * qd, S, _NN3)
            sp_ref[c] = S
            S = cg * S + _dot(k * kd, v, _TN3)
            r = lax.rsqrt(jnp.mean(o * o, axis=2, keepdims=True) + EPS)
            y = o * r * _silu(_stack_heads(g_ref.at[rows, :], H))
            for h in range(H):
                o_ref[rows, h * HEAD_DIM:(h + 1) * HEAD_DIM] = o[h]
                y_ref[rows, h * HEAD_DIM:(h + 1) * HEAD_DIM] = y[h].astype(bf16)
        st[...] = S

    wide = lambda blk: pl.BlockSpec((K * C, H * HEAD_DIM), lambda n: (n, blk))
    tbl = pl.BlockSpec((K * C, HEAD_DIM), lambda n: (n, 0))
    return pl.pallas_call(
        body, grid=(NS,),
        in_specs=[wide(0), wide(1), wide(2), wide(3), tbl, tbl,
                  pl.BlockSpec((H, 4, C, HEAD_DIM), lambda n: (0, 0, 0, 0))],
        out_specs=[wide(0), wide(0), pl.BlockSpec((K, H, HEAD_DIM, HEAD_DIM), lambda n: (n, 0, 0, 0))],
        out_shape=[jax.ShapeDtypeStruct((T, D_MODEL), bf16), jax.ShapeDtypeStruct((T, H * HEAD_DIM), f32),
                   jax.ShapeDtypeStruct((N, H, HEAD_DIM, HEAD_DIM), f32)],
        scratch_shapes=[pltpu.VMEM((H, HEAD_DIM, HEAD_DIM), f32)],
        compiler_params=_cparams(("arbitrary",)), name=name,
    )(p, p, p, p, cos, sin, tab)


def _ret_bwd(p, cos, sin, tab, o_raw, sprev, dmix, name):
    T = p.shape[0]
    C, H = RET_CHUNK, RET_HEADS
    N = T // C
    K = min(RET_STEP, N)
    NS = N // K
    scale = HEAD_DIM ** -0.5
    W = H * HEAD_DIM

    def body(q_ref, k_ref, v_ref, g_ref, c_ref, s_ref, t_ref, o_ref, sp_ref, dy_ref, d_ref, dst):
        @pl.when(pl.program_id(0) == 0)
        def _():
            dst[...] = jnp.zeros_like(dst)

        dm, qd, kd, cg = t_ref[:, 0], t_ref[:, 1], t_ref[:, 2], t_ref[:, 3]
        dS1 = dst[...]
        for c in reversed(range(K)):
            rows = pl.ds(c * C, C)
            cos_, sin_ = c_ref[rows, :], s_ref[rows, :]
            rot = lambda a: _rope(a, cos_, sin_)
            q = _stack_heads(q_ref.at[rows, :], H, rot)
            k = _stack_heads(k_ref.at[rows, :], H, rot) * scale
            v = _stack_heads(v_ref.at[rows, :], H)
            g = _stack_heads(g_ref.at[rows, :], H)
            S = sp_ref[c]
            o = _stack_heads(o_ref.at[rows, :], H)
            dy = _stack_heads(dy_ref.at[rows, :], H)
            r = lax.rsqrt(jnp.mean(o * o, axis=2, keepdims=True) + EPS)
            nrm = o * r
            dn = dy * _silu(g)
            dg = dy * nrm * _dsilu(g)
            do = r * (dn - nrm * jnp.mean(dn * nrm, axis=2, keepdims=True))
            P = _dot(q, k, _NT3) * dm
            dP = _dot(do, v, _NT3) * dm
            dq = _dot(dP, k, _NN3) + _dot(do, S, _NT3) * qd
            dk = (_dot(dP, q, _TN3) + _dot(v, dS1, _NT3) * kd) * scale
            dv = _dot(P, do, _TN3) + _dot(k * kd, dS1, _NN3)
            dS1 = cg * dS1 + _dot(q * qd, do, _TN3)
            for h in range(H):
                d_ref[rows, h * HEAD_DIM:(h + 1) * HEAD_DIM] = _unrope(dq[h], cos_, sin_).astype(bf16)
                d_ref[rows, W + h * HEAD_DIM:W + (h + 1) * HEAD_DIM] = _unrope(dk[h], cos_, sin_).astype(bf16)
                d_ref[rows, 2 * W + h * HEAD_DIM:2 * W + (h + 1) * HEAD_DIM] = dv[h].astype(bf16)
                d_ref[rows, 3 * W + h * HEAD_DIM:3 * W + (h + 1) * HEAD_DIM] = dg[h].astype(bf16)
        dst[...] = dS1

    rev = lambda blk: pl.BlockSpec((K * C, W), lambda n: (NS - 1 - n, blk))
    tbl = pl.BlockSpec((K * C, HEAD_DIM), lambda n: (NS - 1 - n, 0))
    return pl.pallas_call(
        body, grid=(NS,),
        in_specs=[rev(0), rev(1), rev(2), rev(3), tbl, tbl,
                  pl.BlockSpec((H, 4, C, HEAD_DIM), lambda n: (0, 0, 0, 0)), rev(0),
                  pl.BlockSpec((K, H, HEAD_DIM, HEAD_DIM), lambda n: (NS - 1 - n, 0, 0, 0)), rev(0)],
        out_specs=pl.BlockSpec((K * C, 4 * W), lambda n: (NS - 1 - n, 0)),
        out_shape=jax.ShapeDtypeStruct((T, 6 * W), bf16),
        scratch_shapes=[pltpu.VMEM((H, HEAD_DIM, HEAD_DIM), f32)],
        compiler_params=_cparams(("arbitrary",)), name=name,
    )(p, p, p, p, cos, sin, tab, o_raw, sprev, dmix)


CONV_K = 4
CONV_W = 512


def _conv_block(C, col_off):
    cw = CONV_W if (C % (2 * CONV_W) or col_off % 2) else 2 * CONV_W
    return cw, col_off * CONV_W // cw
PAD = 8
SUB_R = 64


def _conv_fwd(x, col_off, w, b, act, name):
    T = x.shape[0]
    C = w.shape[1]
    cw, cb = _conv_block(C, col_off)
    G = C // cw
    tt = min(1024, T)
    NT = T // tt
    has_b = b is not None

    def body(*refs):
        if has_b:
            x_ref, w_ref, b_ref, y_ref, pad = refs
        else:
            x_ref, w_ref, y_ref, pad = refs
        t = pl.program_id(1)

        @pl.when(t == 0)
        def _():
            pad[pl.ds(0, PAD), :] = jnp.zeros((PAD, cw), f32)

        pad[pl.ds(PAD, tt), :] = x_ref[...]
        for g in range(cw // 128):
            ls = slice(g * 128, (g + 1) * 128)
            wv = w_ref[:, ls]
            for c in range(tt // SUB_R):
                r0 = c * SUB_R
                y = wv[0:1, :] * pad[pl.ds(PAD - 3 + r0, SUB_R), ls]
                for kk in range(1, CONV_K):
                    y = y + wv[kk:kk + 1, :] * pad[pl.ds(PAD - 3 + kk + r0, SUB_R), ls]
                if has_b:
                    y = y + b_ref[:, ls]
                y_ref[pl.ds(r0, SUB_R), ls] = _silu(y) if act else y
        tail = pad[pl.ds(tt, PAD), :]
        pad[pl.ds(0, PAD), :] = tail

    in_specs = [pl.BlockSpec((tt, cw), lambda g, t: (t, cb + g)),
                pl.BlockSpec((CONV_K, cw), lambda g, t: (0, g))]
    args = [x, w]
    if has_b:
        in_specs.append(pl.BlockSpec((1, cw), lambda g, t: (0, g)))
        args.append(b)
    return pl.pallas_call(
        body, grid=(G, NT), in_specs=in_specs,
        out_specs=pl.BlockSpec((tt, cw), lambda g, t: (t, g)),
        out_shape=jax.ShapeDtypeStruct((T, C), f32),
        scratch_shapes=[pltpu.VMEM((tt + PAD, cw), f32)],
        compiler_params=_cparams(("parallel", "arbitrary")), name=name,
    )(*args)


def _conv_bwd(x, col_off, w, b, act, dout, dp, name):
    T = x.shape[0]
    C = w.shape[1]
    cw, cb = _conv_block(C, col_off)
    G = C // cw
    tt = min(1024, T)
    NT = T // tt
    has_b = b is not None

    def body(*refs):
        if has_b:
            x_ref, xp_ref, w_ref, b_ref, d_ref, dp_in, dx_ref, dw_ref, db_ref, pad, dpad = refs
        else:
            x_ref, xp_ref, w_ref, d_ref, dp_in, dx_ref, dw_ref, db_ref, pad, dpad = refs
        t = pl.program_id(1)
        first_tile = t == NT - 1

        @pl.when(t == 0)
        def _():
            dpad[pl.ds(tt, PAD), :] = jnp.zeros((PAD, cw), f32)
            dw_ref[...] = jnp.zeros_like(dw_ref)
            db_ref[...] = jnp.zeros_like(db_ref)

        pad[pl.ds(0, PAD), :] = jnp.where(first_tile, 0.0, xp_ref[...])
        pad[pl.ds(PAD, tt), :] = x_ref[...]
        fold = lambda v: v.reshape(SUB_R // 8, 8, 128).sum(axis=0)
        for g in range(cw // 128):
            ls = slice(g * 128, (g + 1) * 128)
            wv = w_ref[:, ls]
            acc =[jnp.zeros((8, 128), f32) for _ in range(CONV_K + 1)]
            for c in reversed(range(tt // SUB_R)):
                r0 = c * SUB_R
                xs = [pad[pl.ds(PAD - 3 + kk + r0, SUB_R), ls] for kk in range(CONV_K)]
                dy = d_ref[pl.ds(r0, SUB_R), ls]
                if act:
                    y = wv[0:1, :] * xs[0]
                    for kk in range(1, CONV_K):
                        y = y + wv[kk:kk + 1, :] * xs[kk]
                    if has_b:
                        y = y + b_ref[:, ls]
                    dy = dy * _dsilu(y)
                dpad[pl.ds(r0, SUB_R), ls] = dy
                dx = wv[3:4, :] * dy
                for j in range(1, CONV_K):
                    dx = dx + wv[3 - j:4 - j, :] * dpad[pl.ds(r0 + j, SUB_R), ls]
                dx_ref[pl.ds(r0, SUB_R), ls] = dx.astype(bf16)
                for kk in range(CONV_K):
                    acc[kk] = acc[kk] + fold(dy * xs[kk])
                acc[CONV_K] = acc[CONV_K] + fold(dy)
            for kk in range(CONV_K):
                dw_ref[kk:kk + 1, ls] += jnp.sum(acc[kk], axis=0, keepdims=True)
            db_ref[:, ls] += jnp.sum(acc[CONV_K], axis=0, keepdims=True)
        head = dpad[pl.ds(0, PAD), :]
        dpad[pl.ds(tt, PAD), :] = head

    rows8 = tt // PAD
    in_specs = [pl.BlockSpec((tt, cw), lambda g, t: (NT - 1 - t, cb + g)),
                pl.BlockSpec((PAD, cw), lambda g, t: (jnp.maximum((NT - 1 - t) * rows8 - 1, 0), cb + g)),
                pl.BlockSpec((CONV_K, cw), lambda g, t: (0, g))]
    args = [x, x, w]
    if has_b:
        in_specs.append(pl.BlockSpec((1, cw), lambda g, t: (0, g)))
        args.append(b)
    in_specs += [pl.BlockSpec((tt, cw), lambda g, t: (NT - 1 - t, g)), pl.BlockSpec(memory_space=pl.ANY)]
    args += [dout, dp]
    return pl.pallas_call(
        body, grid=(G, NT), in_specs=in_specs,
        out_specs=[pl.BlockSpec((tt, cw), lambda g, t: (NT - 1 - t, cb + g)),
                   pl.BlockSpec((CONV_K, cw), lambda g, t: (0, g)),
                   pl.BlockSpec((1, cw), lambda g, t: (0, g))],
        out_shape=[jax.ShapeDtypeStruct(dp.shape, dp.dtype), jax.ShapeDtypeStruct((CONV_K, C), f32),
                   jax.ShapeDtypeStruct((1, C), f32)],
        input_output_aliases={len(args) - 1: 0},
        scratch_shapes=[pltpu.VMEM((tt + PAD, cw), f32), pltpu.VMEM((tt + PAD, cw), f32)],
        compiler_params=_cparams(("parallel", "arbitrary")), name=name,
    )(*args)


def _lru_gates(xc, wr, wi, br, bi, lam):
    r = _sigmoid(_dot(xc, wr, _NN) + br)
    i = _sigmoid(_dot(xc, wi, _NN) + bi)
    sp = _softplus(-lam)
    a = jnp.exp(-LRU_C * r * sp)
    mult = jnp.sqrt(1.0 - a * a)
    return r, i, sp, a, mult


def _lru_fwd(xc, p, y_off, wr, wi, br, bi, lam, mix, name):
    T = xc.shape[0]
    G = LRU_WIDTH // 128
    tt = min(512, T)
    NT = T // tt

    def body(x_ref, y_ref, wr_ref, wi_ref, br_ref, bi_ref, l_ref, mix_in, o_ref, h_ref, hc):
        t = pl.program_id(1)

        @pl.when(t == 0)
        def _():
            hc[...] = jnp.zeros_like(hc)

        x = x_ref[...]
        r, i, sp, a, mult = _lru_gates(x, wr_ref[...], wi_ref[...], br_ref[...], bi_ref[...], l_ref[...])
        row = lax.broadcasted_iota(jnp.int32, (tt, 128), 0)
        mult = jnp.where((row == 0) & (t == 0), 1.0, mult)
        U = x * i * mult
        A = a
        d = 1
        while d < tt:
            keep = row >= d
            Ush = jnp.where(keep, pltpu.roll(U, d, 0), 0.0)
            Ash = jnp.where(keep, pltpu.roll(A, d, 0), 1.0)
            U = A * Ush + U
            A = A * Ash
            d *= 2
        h = U + A * hc[0:1, :]
        h_ref[...] = h
        hc[...] = jnp.broadcast_to(h[tt - 1:tt, :], hc.shape)
        o_ref[...] = (h * _gelu(y_ref[...])).astype(bf16)

    tile = pl.BlockSpec((tt, 128), lambda g, t: (t, g))
    vec = pl.BlockSpec((1, 128), lambda g, t: (0, g))
    wsp = pl.BlockSpec((128, 128), lambda g, t: (g, g))
    return pl.pallas_call(
        body, grid=(G, NT),
        in_specs=[tile, pl.BlockSpec((tt, 128), lambda g, t: (t, y_off + g)), wsp, wsp, vec, vec, vec,
                  pl.BlockSpec(memory_space=pl.ANY)],
        out_specs=[pl.BlockSpec((tt, 128), lambda g, t: (t, G + g)), tile],
        out_shape=[jax.ShapeDtypeStruct(mix.shape, mix.dtype), jax.ShapeDtypeStruct((T, LRU_WIDTH), f32)],
        input_output_aliases={7: 0},
        scratch_shapes=[pltpu.VMEM((8, 128), f32)],
        compiler_params=_cparams(("parallel", "arbitrary")), name=name,
    )(xc, p, wr, wi, br, bi, lam, mix)


def _lru_bwd(xc, p, y_off, wr, wi, br, bi, lam, hs, dmix, d_off, dp, name):
    T = xc.shape[0]
    G = LRU_WIDTH // 128
    tt = min(512, T)
    NT = T // tt

    def body(x_ref, y_ref, wr_ref, wi_ref, br_ref, bi_ref, l_ref, h_ref, hp_ref, do_ref, dp_in,
             dx_ref, dy_ref, dwr_ref, dwi_ref, dbr_ref, dbi_ref, dl_ref, lc, an):
        t = pl.program_id(1)
        first_tile = t == NT - 1

        @pl.when(t == 0)
        def _():
            lc[...] = jnp.zeros_like(lc)
            an[...] = jnp.zeros_like(an)
            dwr_ref[...] = jnp.zeros_like(dwr_ref)
            dwi_ref[...] = jnp.zeros_like(dwi_ref)
            dbr_ref[...] = jnp.zeros_like(dbr_ref)
            dbi_ref[...] = jnp.zeros_like(dbi_ref)
            dl_ref[...] = jnp.zeros_like(dl_ref)

        x = x_ref[...]
        y = y_ref[...]
        wr, wi, lam_ = wr_ref[...], wi_ref[...], l_ref[...]
        r, i, sp, a, mult_raw = _lru_gates(x, wr, wi, br_ref[...], bi_ref[...], lam_)
        row = lax.broadcasted_iota(jnp.int32, (tt, 128), 0)
        t0 = (row == 0) & first_tile
        mult = jnp.where(t0, 1.0, mult_raw)
        h = h_ref[...]
        do = do_ref[...]
        dh = do * _gelu(y)
        dy_ref[...] = (do * h * _dgelu(y)).astype(bf16)
        B = jnp.where(row == tt - 1, an[0:1, :], pltpu.roll(a, tt - 1, 0))
        L = dh
        d = 1
        while d < tt:
            keep = row < tt - d
            Lsh = jnp.where(keep, pltpu.roll(L, tt - d, 0), 0.0)
            Bsh = jnp.where(keep, pltpu.roll(B, tt - d, 0), 1.0)
            L = L + B * Lsh
            B = B * Bsh
            d *= 2
        L = L + B * lc[0:1, :]
        lc[...] = jnp.broadcast_to(L[0:1, :], lc.shape)
        an[...] = jnp.broadcast_to(a[0:1, :], an.shape)
        hprev = jnp.where(first_tile, 0.0, hp_ref[...])[PAD - 1:PAD, :]
        hm1 = jnp.where(row == 0, hprev, pltpu.roll(h, 1, 0))
        da = L * hm1
        dxc = L * i * mult
        di = L * x * mult
        dmult = jnp.where(t0, 0.0, L * x * i)
        da = da - jnp.where(t0, 0.0, dmult * a / mult_raw)
        dlog_a = da * a
        dr = dlog_a * (-LRU_C) * sp
        dsp = jnp.sum(dlog_a * (-LRU_C) * r, axis=0, keepdims=True)
        dpr = dr * r * (1.0 - r)
        dpi = di * i * (1.0 - i)
        dx_ref[...] = dxc + _dot(dpr, wr, _NT) + _dot(dpi, wi, _NT)
        for d_ref, dpre in ((dwr_ref, dpr), (dwi_ref, dpi)):
            dw = _dot(x, dpre, _TN)
            for s in range(2):
                d_ref[s] += dw[s * 64:(s + 1) * 64, s * 64:(s + 1) * 64]
        dbr_ref[...] += jnp.sum(dpr, axis=0, keepdims=True)
        dbi_ref[...] += jnp.sum(dpi, axis=0, keepdims=True)
        dl_ref[...] += dsp * (-_sigmoid(-lam_))

    rows8 = tt // PAD
    tile = pl.BlockSpec((tt, 128), lambda g, t: (NT - 1 - t, g))
    vec = pl.BlockSpec((1, 128), lambda g, t: (0, g))
    wsp = pl.BlockSpec((128, 128), lambda g, t: (g, g))
    wout = pl.BlockSpec((2, 64, 64), lambda g, t: (g, 0, 0))
    return pl.pallas_call(
        body, grid=(G, NT),
        in_specs=[tile, pl.BlockSpec((tt, 128), lambda g, t: (NT - 1 - t, y_off + g)), wsp, wsp, vec, vec, vec, tile,
                  pl.BlockSpec((PAD, 128), lambda g, t: (jnp.maximum((NT - 1 - t) * rows8 - 1, 0), g)),
                  pl.BlockSpec((tt, 128), lambda g, t: (NT - 1 - t, d_off + g)), pl.BlockSpec(memory_space=pl.ANY)],
        out_specs=[tile, pl.BlockSpec((tt, 128), lambda g, t: (NT - 1 - t, y_off + g)), wout, wout, vec, vec, vec],
        out_shape=[jax.ShapeDtypeStruct((T, LRU_WIDTH), f32), jax.ShapeDtypeStruct(dp.shape, dp.dtype),
                   jax.ShapeDtypeStruct((2 * G, 64, 64), f32), jax.ShapeDtypeStruct((2 * G, 64, 64), f32),
                   jax.ShapeDtypeStruct((1, LRU_WIDTH), f32), jax.ShapeDtypeStruct((1, LRU_WIDTH), f32),
                   jax.ShapeDtypeStruct((1, LRU_WIDTH), f32)],
        input_output_aliases={10: 1},
        scratch_shapes=[pltpu.VMEM((8, 128), f32), pltpu.VMEM((8, 128), f32)],
        compiler_params=_cparams(("parallel", "arbitrary")), name=name,
    )(xc, p, wr, wi, br, bi, lam, hs, hs, dmix, dp)


_NN3 = (((2,), (1,)), ((0,), (0,)))
_NT3 = (((2,), (2,)), ((0,), (0,)))
_TN3 = (((1,), (1,)), ((0,), (0,)))


def _pairs(ref, K):
    C = GDN_CHUNK
    return jnp.stack([ref[c * C:(c + 1) * C, h * HEAD_DIM:(h + 1) * HEAD_DIM] for c in range(K) for h in range(GDN_HEADS)])


def _put_pairs(ref, val, K, col=0):
    C, H = GDN_CHUNK, GDN_HEADS
    for c in range(K):
        for h in range(H):
            ref[c * C:(c + 1) * C, col + h * HEAD_DIM:col + (h + 1) * HEAD_DIM] = val[c * H + h].astype(ref.dtype)


def _rowsum(x):
    H, C, L = x.shape
    return _dot(x.reshape(H * C, L), jnp.ones((L, HEAD_DIM), f32), _NN).reshape(H, C, HEAD_DIM)


def _gdn_pre(qr, kr, v, ba, alog, dtb):
    C, H = GDN_CHUNK, GDN_HEADS
    B = qr.shape[0]
    K = B // H
    lane = lax.broadcasted_iota(jnp.int32, (C, 128), 1)
    lane3 = lax.broadcasted_iota(jnp.int32, (B, C, 128), 2)
    ri = lax.broadcasted_iota(jnp.int32, (C, C), 0)
    ci = lax.broadcasted_iota(jnp.int32, (C, C), 1)
    rowc = lax.broadcasted_iota(jnp.int32, (C, 1), 0)
    col = lambda m, j: jnp.sum(jnp.where(lane == j, m, 0.0), axis=1, keepdims=True)
    ea = jnp.exp(alog)
    tri = (ri >= ci).astype(f32)
    g_all, beta_cols, G_cols = [], [], []
    for c in range(K):
        ba_c = ba[c * C:(c + 1) * C]
        g_c = -ea * _softplus(ba_c + dtb)
        G_c = _dot01(tri, g_c, _NN)
        s_c = _sigmoid(ba_c)
        g_all.append(g_c)
        beta_cols += [col(s_c, h) for h in range(H)]
        G_cols += [col(G_c, H + h) for h in range(H)]
    wide = lambda c: jnp.broadcast_to(c, (B, C, 128))
    beta = wide(jnp.stack(beta_cols))
    Gc = jnp.stack(G_cols)
    rq = lax.rsqrt(_rowsum(qr * qr) + EPS)
    rk = lax.rsqrt(_rowsum(kr * kr) + EPS)
    qh, kn = qr * rq, kr * rk
    qn = qh * (HEAD_DIM ** -0.5)
    Grow = _dot01(jnp.ones((B, C, 128), f32), jnp.where(lane3 == 0, Gc, 0.0), _NT3)
    incl = ri >= ci
    Di = jnp.where(incl, jnp.exp(jnp.where(incl, Gc - Grow, 0.0)), 0.0)
    Ds = jnp.where(ri > ci, Di, 0.0)
    Gl = jnp.sum(jnp.where(rowc == C - 1, Gc, 0.0), axis=1, keepdims=True)
    eG = wide(jnp.exp(Gc))
    eGl = wide(jnp.exp(Gl - Gc))
    cd = jnp.exp(Gl)
    kb = kn * beta
    vb = v * beta
    Lm = _dot(kb, kn, _NT3) * Ds
    kbg = kb * eG
    QK = _dot(qn, kn, _NT3) * Di
    qg = qn * eG
    kg = kn * eGl
    return dict(beta=beta, g_all=g_all, rq=rq, rk=rk, qh=qh, kn=kn, qn=qn, Di=Di, Ds=Ds, eG=eG, eGl=eGl, cd=cd,
                kb=kb, vb=vb, Lm=Lm, kbg=kbg, QK=QK, qg=qg, kg=kg, lane=lane, ri=ri, ci=ci, rowc=rowc, ea=ea)


def _unit_lower_inverse(Lm):
    C = Lm.shape[-1]
    ri = lax.broadcasted_iota(jnp.int32, (C, C), 0)
    ci = lax.broadcasted_iota(jnp.int32, (C, C), 1)
    same = lambda s: (ri // s) == (ci // s)
    Xd = jnp.where(same(8), -Lm, 0.0)
    Tinv = (ri == ci).astype(f32) + Xd
    Pw = Xd
    for _ in range(2):
        Pw = _dot(Pw, Pw, _NN3)
        Tinv = Tinv + _dot(Tinv, Pw, _NN3)
    for s in (8, 16, 32):
        off = jnp.where(same(2 * s) & jnp.logical_not(same(s)), Lm, 0.0)
        Tinv = Tinv - _dot(_dot(Tinv, off, _NN3), Tinv, _NN3)
    return Tinv


def _gdn_specs(T, rev):
    C = GDN_CHUNK
    H = GDN_HEADS
    K = min(GDN_STEP, T // C)
    NS = T // (C * K)
    nn = (lambda n: NS - 1 - n) if rev else (lambda n: n)
    wide = lambda blk: pl.BlockSpec((K * C, H * HEAD_DIM), lambda n: (nn(n), blk))
    one = lambda off: pl.BlockSpec((K * C, HEAD_DIM), lambda n: (nn(n), off))
    vec = pl.BlockSpec((1, 128), lambda n: (0, 0))
    st = lambda rows: pl.BlockSpec((K, H, rows, rows), lambda n: (nn(n), 0, 0, 0))
    return K, NS, wide, one, vec, st


def _gdn_fwd(qkv, p, alog, dtb, nw, name):
    T = qkv.shape[0]
    C, H = GDN_CHUNK, GDN_HEADS
    N = T // C
    K, NS, wide, one, vec, st_spec = _gdn_specs(T, False)

    def body(q_ref, k_ref, v_ref, z_ref, ba_ref, al_ref, dt_ref, nw_ref, y_ref, sp_ref, ti_ref, vn_ref, o_ref, st):
        @pl.when(pl.program_id(0) == 0)
        def _():
            st[...] = jnp.zeros_like(st)

        f = _gdn_pre(_pairs(q_ref, K), _pairs(k_ref, K), _pairs(v_ref, K), ba_ref[...], al_ref[...], dt_ref[...])
        Tinv = _unit_lower_inverse(f["Lm"])
        ti_ref[...] = Tinv.reshape(K, H, C, C).astype(bf16)
        w = _dot(Tinv, f["kbg"], _NN3)
        u = _dot(Tinv, f["vb"], _NN3)
        S = st[...]
        vns, os_ = [], []
        for c in range(K):
            sl = slice(c * H, (c + 1) * H)
            sp_ref[c] = S
            vn_c = u[sl] - _dot(w[sl], S, _NN3)
            os_.append(_dot(f["qg"][sl], S, _NN3) + _dot(f["QK"][sl], vn_c, _NN3))
            S = S * f["cd"][sl] + _dot(f["kg"][sl], vn_c, _TN3)
            vns.append(vn_c)
        st[...] = S
        vn, o = jnp.concatenate(vns), jnp.concatenate(os_)
        r = lax.rsqrt(_rowsum(o * o) * (1.0 / HEAD_DIM) + EPS)
        _put_pairs(y_ref, o * r * nw_ref[...] * _silu(_pairs(z_ref, K)), K)
        _put_pairs(vn_ref, vn, K)
        _put_pairs(o_ref, o, K)

    wide_f32 = jax.ShapeDtypeStruct((T, H * HEAD_DIM), f32)
    return pl.pallas_call(
        body, grid=(NS,),
        in_specs=[wide(0), wide(1), wide(2), wide(3), one(4 * H), vec, vec, vec],
        out_specs=[wide(0), st_spec(HEAD_DIM), st_spec(C), wide(0), wide(0)],
        out_shape=[jax.ShapeDtypeStruct((T, H * HEAD_DIM), bf16), jax.ShapeDtypeStruct((N, H, HEAD_DIM, HEAD_DIM), f32),
                   jax.ShapeDtypeStruct((N, H, C, C), bf16), jax.ShapeDtypeStruct((T, H * HEAD_DIM), bf16), wide_f32],
        scratch_shapes=[pltpu.VMEM((H, HEAD_DIM, HEAD_DIM), f32)],
        compiler_params=_cparams(("arbitrary",)), name=name,
    )(qkv, qkv, qkv, p, p, alog, dtb, nw)


def _gdn_bwd(qkv, p, alog, dtb, nw, sprev, tinv, vn_all, o_all, dy_all, name):
    T = qkv.shape[0]
    C, H = GDN_CHUNK, GDN_HEADS
    N = T // C
    K, NS, wide, one, vec, st_spec = _gdn_specs(T, True)
    rs = lambda m: jnp.sum(m, axis=2, keepdims=True)

    def body(q_ref, k_ref, v_ref, z_ref, ba_ref, al_ref, dt_ref, nw_ref, sp_ref, ti_ref, vn_ref, o_ref, dy_ref,
             dqkv_ref, dz_ref, dba_ref, dal_ref, ddt_ref, dnw_ref, dst):
        @pl.when(pl.program_id(0) == 0)
        def _():
            dst[...] = jnp.zeros_like(dst)
            dal_ref[...] = jnp.zeros_like(dal_ref)
            ddt_ref[...] = jnp.zeros_like(ddt_ref)
            dnw_ref[...] = jnp.zeros_like(dnw_ref)

        ba, dtb_, nwv = ba_ref[...], dt_ref[...], nw_ref[...]
        v = _pairs(v_ref, K)
        f = _gdn_pre(_pairs(q_ref, K), _pairs(k_ref, K), v, ba, al_ref[...], dtb_)
        beta, kn, qn, kb, vb, kbg = f["beta"], f["kn"], f["qn"], f["kb"], f["vb"], f["kbg"]
        eG, eGl, cd, Di, Ds, QK, qg, kg = f["eG"], f["eGl"], f["cd"], f["Di"], f["Ds"], f["QK"], f["qg"], f["kg"]
        lane, ri, ci, rowc = f["lane"], f["ri"], f["ci"], f["rowc"]
        Tinv = ti_ref[...].reshape(K * H, C, C)
        S = sp_ref[...].reshape(K * H, HEAD_DIM, HEAD_DIM)
        w_ = _dot(Tinv, kbg, _NN3)
        vn, o = _pairs(vn_ref, K), _pairs(o_ref, K)
        z, dy = _pairs(z_ref, K), _pairs(dy_ref, K)
        r = lax.rsqrt(_rowsum(o * o) * (1.0 / HEAD_DIM) + EPS)
        nrm = o * r
        sz = _silu(z)
        dn = dy * nwv * sz
        _put_pairs(dz_ref, dy * nrm * nwv * _dsilu(z), K)
        dnw_ref[...] += jnp.sum(jnp.sum(dy * nrm * sz, axis=0), axis=0, keepdims=True)
        do = r * (dn - nrm * (_rowsum(dn * nrm) * (1.0 / HEAD_DIM)))
        dvn_do = _dot(QK, do, _TN3)
        dS_do = _dot(qg, do, _TN3)
        dqg = _dot(do, S, _NT3)
        dQK = _dot(do, vn, _NT3)
        dS = dst[...]
        dS1s, dvns = [None] * K, [None] * K
        for c in reversed(range(K)):
            sl = slice(c * H, (c + 1) * H)
            dS1s[c] = dS
            dvns[c] = _dot(kg[sl], dS, _NN3) + dvn_do[sl]
            dS = cd[sl] * dS + dS_do[sl] - _dot(w_[sl], dvns[c], _TN3)
        dst[...] = dS
        dS1, dvn = jnp.concatenate(dS1s), jnp.concatenate(dvns)
        dcd = jnp.sum(jnp.sum(S * dS1, axis=2, keepdims=True), axis=1, keepdims=True)
        dkg = _dot(vn, dS1, _NT3)
        dw = -_dot(dvn, S, _NT3)
        dqn = dqg * eG
        dkn = dkg * eGl
        deGl = rs(dkg * kn)
        dQKr = dQK * Di
        E = dQK * QK
        dqn = dqn + _dot(dQKr, kn, _NN3)
        dkn = dkn + _dot(dQKr, qn, _TN3)
        dT = _dot(dvn, vb, _NT3) + _dot(dw, kbg, _NT3)
        dvb = _dot(Tinv, dvn, _TN3)
        dkbg = _dot(Tinv, dw, _TN3)
        dkb = dkbg * eG
        deG = rs(dqg * qn + dkbg * kb)
        dL = -_dot(_dot(Tinv, dT, _TN3), Tinv, _NT3)
        dKK = dL * Ds
        E = E + dL * f["Lm"]
        dkb = dkb + _dot(dKK, kn, _NN3)
        dkn = dkn + _dot(dKK, kb, _TN3) + dkb * beta
        dbeta = rs(dkb * kn + dvb * v)
        _put_pairs(dqkv_ref, dvb * beta, K, 2 * H * HEAD_DIM)
        dG = rs(E) - rs(jnp.swapaxes(E, 1, 2)) + deG * eG - deGl * eGl
        dGl = jnp.sum(deGl * eGl, axis=1, keepdims=True) + dcd * cd
        dG = dG + jnp.where(rowc == C - 1, dGl, 0.0)
        qh = f["qh"]
        _put_pairs(dqkv_ref, (HEAD_DIM ** -0.5) * f["rq"] * (dqn - qh * _rowsum(dqn * qh)), K)
        _put_pairs(dqkv_ref, f["rk"] * (dkn - kn * _rowsum(dkn * kn)), K, H * HEAD_DIM)
        db = dbeta * beta * (1.0 - beta)
        triu = (ri <= ci).astype(f32)
        for c in range(K):
            db_all = jnp.where(lane == 0, db[c * H], 0.0)
            dG_all = jnp.where(lane == H, dG[c * H], 0.0)
            for h in range(1, H):
                db_all = db_all + jnp.where(lane == h, db[c * H + h], 0.0)
                dG_all = dG_all + jnp.where(lane == H + h, dG[c * H + h], 0.0)
            dg_all = _dot01(triu, dG_all, _NN)
            da_all = dg_all * (-f["ea"]) * _sigmoid(ba[c * C:(c + 1) * C] + dtb_)
            dba_ref[c * C:(c + 1) * C, :] = (db_all + da_all).astype(bf16)
            ddt_ref[...] += jnp.sum(da_all, axis=0, keepdims=True)
            dal_ref[...] += jnp.sum(dg_all * f["g_all"][c], axis=0, keepdims=True)

    small = jax.ShapeDtypeStruct((1, 128), f32)
    return pl.pallas_call(
        body, grid=(NS,),
        in_specs=[wide(0), wide(1), wide(2), wide(3), one(4 * H), vec, vec, vec, st_spec(HEAD_DIM), st_spec(C),
                  wide(0), wide(0), wide(0)],
        out_specs=[pl.BlockSpec((K * C, 3 * H * HEAD_DIM), lambda n: (NS - 1 - n, 0)), wide(3), one(0), vec, vec, vec],
        out_shape=[jax.ShapeDtypeStruct((T, 3 * H * HEAD_DIM), f32), jax.ShapeDtypeStruct((T, ODD_PAD), bf16),
                   jax.ShapeDtypeStruct((T, 128), bf16), small, small, small],
        scratch_shapes=[pltpu.VMEM((H, HEAD_DIM, HEAD_DIM), f32)],
        compiler_params=_cparams(("arbitrary",)), name=name,
    )(qkv, qkv, qkv, p, p, alog, dtb, nw, sprev, tinv, vn_all, o_all, dy_all)


def _lanes_from(x, s):
    return x if s % 128 == 0 else pltpu.roll(x, (128 - s) % 128, 1)


def _odd_assemble(g, name):
    R = g.shape[1]
    tr = min(256, R)
    n_blk = ODD_SHARD_PAD // 128

    def body(g_ref, o_ref):
        lane = lax.broadcasted_iota(jnp.int32, (tr, 128), 1)
        blk = lambda d, m: g_ref[d, :, m * 128:(m + 1) * 128]
        for gb in range(ODD_PAD // 128):
            c0 = 128 * gb
            if c0 >= ODD_IN:
                o_ref[:, c0:c0 + 128] = jnp.zeros((tr, 128), g.dtype)
                continue
            d0 = c0 // ODD_SHARD
            m0, sh = divmod(c0 - ODD_SHARD * d0, 128)
            take = min(128, ODD_SHARD * (d0 + 1) - c0)
            p = _lanes_from(blk(d0, m0), sh)
            if sh and m0 + 1 < n_blk:
                p = jnp.where(lane < 128 - sh, p, _lanes_from(blk(d0, m0 + 1), sh))
            if take < 128:
                nxt = pltpu.roll(blk(d0 + 1, 0), take, 1) if d0 + 1 < N_DEV else jnp.zeros((tr, 128), g.dtype)
                p = jnp.where(lane < take, p, nxt)
            o_ref[:, c0:c0 + 128] = p

    return pl.pallas_call(
        body, grid=(R // tr,),
        in_specs=[pl.BlockSpec((N_DEV, tr, ODD_SHARD_PAD), lambda i: (0, i, 0))],
        out_specs=pl.BlockSpec((tr, ODD_PAD), lambda i: (i, 0)),
        out_shape=jax.ShapeDtypeStruct((R, ODD_PAD), g.dtype),
        compiler_params=_cparams(("parallel",)), name=name,
    )(g)


def _odd_split(w, name):
    R = w.shape[0]
    tr = min(256, R)

    def body(w_ref, o_ref):
        lane = lax.broadcasted_iota(jnp.int32, (tr, 128), 1)
        blk = lambda gb: w_ref[:, gb * 128:(gb + 1) * 128]
        for d in range(N_DEV):
            for m in range(ODD_SHARD_PAD // 128):
                g0, sh = divmod(ODD_SHARD * d + 128 * m, 128)
                p = _lanes_from(blk(g0), sh)
                if sh and g0 + 1 < ODD_PAD // 128:
                    p = jnp.where(lane < 128 - sh, p, _lanes_from(blk(g0 + 1), sh))
                real = ODD_SHARD - 128 * m
                if real < 128:
                    p = jnp.where(lane < real, p, jnp.zeros_like(p))
                o_ref[d, :, m * 128:(m + 1) * 128] = p

    return pl.pallas_call(
        body, grid=(R // tr,),
        in_specs=[pl.BlockSpec((tr, ODD_PAD), lambda i: (i, 0))],
        out_specs=pl.BlockSpec((N_DEV, tr, ODD_SHARD_PAD), lambda i: (0, i, 0)),
        out_shape=jax.ShapeDtypeStruct((N_DEV, R, ODD_SHARD_PAD), w.dtype),
        compiler_params=_cparams(("parallel",)), name=name,
    )(w)


def _adam_tile(g, w_ref, m_ref, v_ref, go_ref, d_ref, mo_ref, vo_ref):
    c1 = 1.0 - ADAM_B1 ** ADAM_STEP
    c2 = 1.0 - ADAM_B2 ** ADAM_STEP
    mn = ADAM_B1 * m_ref[...] + (1.0 - ADAM_B1) * g
    vn = ADAM_B2 * v_ref[...] + (1.0 - ADAM_B2) * (g * g)
    go_ref[...] = g
    mo_ref[...] = mn
    vo_ref[...] = vn
    d_ref[...] = -ADAM_LR * ((mn / c1) / (jnp.sqrt(vn / c2) + ADAM_EPS) + ADAM_WD * w_ref[...])


def _adamw(w, gs, m, v, name, layer=None, prev=None):
    R, Cc = w.shape[-2:]
    S = gs.shape[0]
    tr = R
    if S * R * Cc * 4 > (4 << 20):
        for cand in (256, 128, 64, 32, 16, 8):
            if R % cand == 0 and R > cand:
                tr = cand
                break

    def body(w_ref, g_ref, m_ref, v_ref, *rest):
        g = g_ref[0].astype(f32)
        for s in range(1, S):
            g = g + g_ref[s].astype(f32)
        _adam_tile(g, w_ref, m_ref, v_ref, *rest[-4:])

    if layer is None:
        blk = pl.BlockSpec((tr, Cc), lambda i: (i, 0))
    else:
        blk = pl.BlockSpec((None, tr, Cc), lambda i: (layer, i, 0))
    out = jax.ShapeDtypeStruct(w.shape, f32)
    carried = [] if prev is None else list(prev)
    return pl.pallas_call(
        body, grid=(R // tr,),
        in_specs=[blk, pl.BlockSpec((S, tr, Cc), lambda i: (0, i, 0)), blk, blk]
        + [pl.BlockSpec(memory_space=pl.ANY)] * len(carried),
        out_specs=[blk] * 4, out_shape=[out] * 4,
        input_output_aliases={4 + j: j for j in range(len(carried))},
        compiler_params=_cparams(("parallel",)), name=name,
    )(w, gs, m, v, *carried)


def _adamw_column_major(w, gs, m, v, name):
    _, R, Cc = w.shape
    S = gs.shape[0]
    q = R // 128
    dense = lambda a: jnp.transpose(a, (2, 0, 1)).reshape(Cc * q, 128)
    back = lambda a: jnp.transpose(a.reshape(Cc, q, 128), (1, 2, 0)).reshape(1, R, Cc)

    def body(w_ref, g_ref, m_ref, v_ref, go_ref, d_ref, mo_ref, vo_ref, gt):
        for i in range(q):
            g = g_ref[0, i * 128:(i + 1) * 128, :].astype(f32)
            for s in range(1, S):
                g = g + g_ref[s, i * 128:(i + 1) * 128, :].astype(f32)
            gt[pl.ds(i, 128, stride=q), :] = g.T
        _adam_tile(gt[...], w_ref, m_ref, v_ref, go_ref, d_ref, mo_ref, vo_ref)

    blk = pl.BlockSpec((128 * q, 128), lambda j: (j, 0))
    outs = pl.pallas_call(
        body, grid=(pl.cdiv(Cc, 128),),
        in_specs=[blk, pl.BlockSpec((S, R, 128), lambda j: (0, 0, j)), blk, blk],
        out_specs=[blk] * 4, out_shape=[jax.ShapeDtypeStruct((Cc * q, 128), f32)] * 4,
        scratch_shapes=[pltpu.VMEM((128 * q, 128), f32)],
        compiler_params=_cparams(("parallel",)), name=name,
    )(dense(w), gs, dense(m), dense(v))
    return [back(o) for o in outs]


def _me():
    x, y, c = lax.axis_index("x"), lax.axis_index("y"), lax.axis_index("c")
    return x, y, c, 4 * x + 2 * y + c


def _peer(k):
    x, y, c, _ = _me()
    px = 1 - x if k & 4 else x
    py = 1 - y if k & 2 else y
    pc = 1 - c if k & 1 else c
    return (px, py, pc), 4 * px + 2 * py + pc


_HBM = pl.BlockSpec(memory_space=pltpu.HBM)
_SEM = pl.BlockSpec(memory_space=pltpu.SEMAPHORE)
_EFFECT = pltpu.SideEffectType.DATAFLOW_SIDE_EFFECTING


def _copy(src, land, ssem, rsem, k, blocked, landing_slot_of_peer):
    pid, pidx = _peer(k)
    slot = pidx if landing_slot_of_peer else _me()[3]
    return pltpu.make_async_remote_copy(src_ref=src.at[pidx] if blocked else src, dst_ref=land.at[slot],
                                        send_sem=ssem.at[k - 1], recv_sem=rsem.at[k - 1], device_id=pid, device_id_type=MESH)


def _own_copy(src, land, rsem, blocked):
    me = _me()[3]
    return pltpu.make_async_copy(src.at[me] if blocked else src, land.at[me], rsem.at[N_DEV - 1])


_ALL_PEERS = tuple(range(1, N_DEV))
_SAME_CORE_PEERS = (1, 2, 4, 6)
_OTHER_CORE_PEERS = (3, 5, 7)


def _relay_copies(land, ssem, rsem):
    sibling, _ = _peer(1)
    copies = []
    for i in range(len(land)):
        for j, k in enumerate(_OTHER_CORE_PEERS):
            sems = dict(send_sem=ssem.at[3 * i + j], recv_sem=rsem.at[3 * i + j], device_id=sibling, device_id_type=MESH)
            _, outgoing = _peer(k - 1)
            _, incoming = _peer(k)
            copies.append((pltpu.make_async_remote_copy(src_ref=land[i].at[outgoing], dst_ref=land[i].at[outgoing], **sems),
                           pltpu.make_async_remote_copy(src_ref=land[i].at[incoming], dst_ref=land[i].at[incoming], **sems)))
    return copies


def _relay_start(lands, name):
    n = len(lands)

    def body(*refs):
        for send, _ in _relay_copies(refs[:n], refs[n], refs[n + 1]):
            send.start()
        refs[-1][...] = jnp.zeros_like(refs[-1])

    sem = pltpu.SemaphoreType.DMA((3 * n,))
    res = pl.pallas_call(
        body, name=name,
        out_shape=(sem, sem) + tuple(pltpu.HBM(a.shape, a.dtype) for a in lands) + (jax.ShapeDtypeStruct((8, 128), f32),),
        in_specs=(_HBM,) * n, out_specs=(_SEM, _SEM) + (_HBM,) * n + (pl.BlockSpec(memory_space=pltpu.VMEM),),
        input_output_aliases={i: 2 + i for i in range(n)},
        compiler_params=pltpu.CompilerParams(has_side_effects=_EFFECT),
    )(*lands)
    return res[:-1], res[-1]


def _relay_wait(handle, after, name):
    ssem, rsem, lands = handle[0], handle[1], handle[2:]
    n = len(lands)
    after = tuple(after) if isinstance(after, (tuple, list)) else (after,)

    def body(*refs):
        for send, recv in _relay_copies(refs[:n], refs[n], refs[n + 1]):
            send.wait_send()
            recv.wait_recv()

    return pl.pallas_call(
        body, name=name, out_shape=tuple(pltpu.HBM(a.shape, a.dtype) for a in lands),
        in_specs=(_HBM,) * n + (_SEM, _SEM) + (pl.BlockSpec(memory_space=pl.ANY),) * len(after), out_specs=(_HBM,) * n,
        input_output_aliases={i: i for i in range(n)}, compiler_params=pltpu.CompilerParams(has_side_effects=_EFFECT),
    )(*lands, ssem, rsem, *after)


def _send_start(srcs, blocked, name, relayed=()):
    n = len(srcs)
    lands = [lax.empty(a.shape if blocked else (N_DEV,) + a.shape, a.dtype) for a in srcs]

    def body(*refs):
        src, land, sems, token = refs[:n], refs[n:2 * n], refs[2 * n:4 * n], refs[-1]
        for i in range(n):
            for k in (_SAME_CORE_PEERS if i in relayed else _ALL_PEERS):
                _copy(src[i], land[i], sems[2 * i], sems[2 * i + 1], k, blocked, False).start()
        for i in range(n):
            _own_copy(src[i], land[i], sems[2 * i + 1], blocked).start()
        token[...] = jnp.zeros_like(token)

    sems = (pltpu.SemaphoreType.DMA((N_DEV - 1,)), pltpu.SemaphoreType.DMA((N_DEV,)))
    hbm = lambda a: pltpu.with_memory_space_constraint(a, pltpu.HBM)
    res = pl.pallas_call(
        body, name=name,
        out_shape=sems * n + tuple(pltpu.HBM(a.shape, a.dtype) for a in srcs + lands)
        + (jax.ShapeDtypeStruct((8, 128), f32),),
        in_specs=(_HBM,) * (2 * n),
        out_specs=(_SEM,) * (2 * n) + (_HBM,) * (2 * n) + (pl.BlockSpec(memory_space=pltpu.VMEM),),
        input_output_aliases={j: 2 * n + j for j in range(2 * n)},
        compiler_params=pltpu.CompilerParams(has_side_effects=_EFFECT),
    )(*[hbm(a) for a in srcs], *[hbm(a) for a in lands])
    handles = [(res[2 * i], res[2 * i + 1], res[2 * n + i], res[3 * n + i]) for i in range(n)]
    return handles, res[-1]


def _send_wait(handle, blocked, after, name, relayed=False):
    ssem, rsem, src, land = handle
    after = tuple(after) if isinstance(after, (tuple, list)) else (after,)

    def body(src_ref, land_ref, ssem_ref, rsem_ref, *rest):
        for k in (_SAME_CORE_PEERS if relayed else _ALL_PEERS):
            cp = _copy(src_ref, land_ref, ssem_ref, rsem_ref, k, blocked, True)
            cp.wait_send()
            cp.wait_recv()
        _own_copy(src_ref, land_ref, rsem_ref, blocked).wait()

    return pl.pallas_call(
        body, name=name, out_shape=(pltpu.HBM(src.shape, src.dtype), pltpu.HBM(land.shape, land.dtype)),
        in_specs=(_HBM, _HBM, _SEM, _SEM) + (pl.BlockSpec(memory_space=pl.ANY),) * len(after), out_specs=(_HBM, _HBM),
        input_output_aliases={0: 0, 1: 1}, compiler_params=pltpu.CompilerParams(has_side_effects=_EFFECT),
    )(src, land, ssem, rsem, *after)


def _block_diag(w):
    nb, bs = w.shape[0], w.shape[1]
    eye = jnp.eye(nb, dtype=w.dtype)
    return (eye[:, None, :, None] * w[:, :, None, :]).reshape(nb * bs, nb * bs)


_SQUARE_TILES = dict(tm=1024, tn=1024, tk=1024)


def _mlp_fwd(x, hm, wu, wd, tag, epilogue, extras, outs, between=None):
    (r,) = _matmul(hm, wu, "nn", outs=[bf16], epilogue=lambda acc: (jnp.maximum(acc, 0.0),), name=f"mlp_up_{tag}")
    res = _matmul(r, wd, "nn", outs=outs, extras=(x,) + tuple(extras), epilogue=epilogue, a_map=jnp.square,
                  after=() if between is None else (between(r),), name=f"mlp_down_{tag}", **_SQUARE_TILES)
    return res, (hm, r)


def _mlp_bwd(x, nw, wu, wd, saved, dxo, dxo_b, tag, sink, urgent):
    hm, r = saved
    (du,) = _matmul(dxo_b, wd, "nt", outs=[bf16], extras=(r,), epilogue=lambda acc, rr: (acc * (2.0 * rr.astype(f32)),),
                    name=f"mlp_dact_{tag}")
    (dwd,) = _matmul(r, dxo_b, "tn", outs=[bf16], a_map=jnp.square, name=f"mlp_dwd_{tag}", **_SQUARE_TILES)
    down = {f"w_down{tag}": dwd.reshape(N_DEV, D_FF // N_DEV, D_MODEL)}
    if urgent:
        tok = sink(down)
        (dwu,) = _matmul(hm, du, "tn", outs=[bf16], shard_cols=2, after=(tok,), name=f"mlp_dwu_{tag}")
        tok = sink({f"w_up{tag}": dwu})
    else:
        (dwu,) = _matmul(hm, du, "tn", outs=[bf16], shard_cols=2, name=f"mlp_dwu_{tag}")
        tok = sink({f"w_up{tag}": dwu, **down})
    return _matmul(du, wu, "nt", outs=_RMS_BWD_OUTS, extras=(x, dxo, nw), epilogue=_rms_bwd_ep, after=(tok,),
                   name=f"mlp_dh_{tag}", **_SQUARE_TILES)


def _local_step(x, tgt, P, weight, sink, ahead):
    T = x.shape[0]
    cos, sin = _rope_tables(T)
    rtab = _ret_tables()
    row = lambda a: a.reshape(1, -1)
    mix_nw, mlp_nw = P["mixer_norm_w"], P["mlp_norm_w"]
    wr_bd, wi_bd = _block_diag(P["lru_w_r"]), _block_diag(P["lru_w_i"])
    lru_b, lru_br, lru_bi, lru_lam = row(P["lru_conv_b"]), row(P["lru_b_r"]), row(P["lru_b_i"]), row(P["lru_lambda"])
    pad16 = lambda a: jnp.pad(a.reshape(1, GDN_HEADS), ((0, 0), (GDN_HEADS, 128 - 2 * GDN_HEADS)))
    alog, dtb = pad16(P["gdn_a_log"]), pad16(P["gdn_dt_bias"])
    gnw = row(P["gdn_norm_w"])

    x0 = x
    h0 = _rms_fwd(x0, mix_nw[0:1], "rms_mix_0")
    w_ie = weight("w_in_even", (h0, cos, sin, wr_bd, wi_bd))
    (pe,) = _matmul(h0, w_ie, "nn", outs=[f32], name="in_even")
    mix0, o_ret, s_ret = _ret_fwd(pe, cos, sin, rtab, "ret_fwd")
    w_lc = weight("lru_conv_w", pe)
    xc = _conv_fwd(pe, 4, w_lc, lru_b, False, "lru_conv_fwd")
    mix0, h_lru = _lru_fwd(xc, pe, 20, wr_bd, wi_bd, lru_br, lru_bi, lru_lam, mix0, "lru_fwd")
    w_oe = weight("w_out_even", mix0)
    tok = ahead(("w_up0", "w_down0"), mix0)
    x1, hm0 = _matmul(mix0, w_oe, "nn", outs=[f32, bf16], extras=(x0, mlp_nw[0:1]), epilogue=_residual_rms_ep,
                      after=(tok,), name="out_even", tm=1024, tn=D_MODEL)
    w_u0, w_d0 = weight("w_up0", x1), weight("w_down0", x1)
    (x2, h1), mlp0 = _mlp_fwd(x1, hm0, w_u0, w_d0, "0", _residual_rms_ep, (mix_nw[1:2],), [f32, bf16],
                              between=lambda r: ahead(("w_in_odd",), r))
    w_io = weight("w_in_odd", h1)
    (po,) = _matmul(h1, w_io, "nn", outs=[f32], tm=2048, tn=ODD_PAD // 3, name="in_odd")
    w_gc = weight("gdn_conv_w", po)
    qkv = _conv_fwd(po, 0, w_gc, None, True, "gdn_conv_fwd")
    y_gdn, s_gdn, ti_gdn, vn_gdn, o_gdn = _gdn_fwd(qkv, po, alog, dtb, gnw, "gdn_fwd")
    w_oo = weight("w_out_odd", y_gdn)
    x3, hm1 = _matmul(y_gdn, w_oo, "nn", outs=[f32, bf16], extras=(x2, mlp_nw[1:2]), epilogue=_residual_rms_ep,
                      name="out_odd", tm=1024, tn=D_MODEL)
    w_u1, w_d1 = weight("w_up1", x3), weight("w_down1", x3)
    (loss, dx4, dx4_b, d_final), mlp1 = _mlp_fwd(x3, hm1, w_u1, w_d1, "1", _loss_ep, (row(P["final_norm_w"]), tgt),
                                                 _LOSS_OUTS)
    dx3, dx3_b, d_mlp_nw1 = _mlp_bwd(x3, mlp_nw[1:2], w_u1, w_d1, mlp1, dx4, dx4_b, "1", sink, False)
    (dy_gdn,) = _matmul(dx3_b, w_oo, "nt", outs=[f32], name="out_odd_dx")
    (d_woo,) = _matmul(y_gdn, dx3_b, "tn", outs=[bf16], name="out_odd_dw")
    dqkv, dpo, dba, d_alog, d_dtb, d_gnw = _gdn_bwd(qkv, po, alog, dtb, gnw, s_gdn, ti_gdn, vn_gdn, o_gdn, dy_gdn,
                                                  "gdn_bwd")
    dpo, d_gconv, _ = _conv_bwd(po, 0, w_gc, None, True, dqkv, dpo, "gdn_conv_bwd")
    dpo = lax.dynamic_update_slice(dpo, dba, (0, 4 * D_MODEL))
    (d_wio,) = _matmul(h1, dpo, "tn", outs=[bf16], tn=ODD_PAD // 3, name="in_odd_dw")
    tok = sink(dict(w_out_odd=d_woo.reshape(N_DEV, D_MODEL // N_DEV, D_MODEL), w_in_odd=_odd_split(d_wio, "w_in_odd_split")))
    dx2, dx2_b, d_mix_nw1 = _matmul(dpo, w_io, "nt", outs=_RMS_BWD_OUTS, extras=(x2, dx3, mix_nw[1:2]),
                                    epilogue=_rms_bwd_ep, after=(tok,), tm=1024, tn=1024, tk=ODD_PAD // 3, name="in_odd_dx")
    dx1, dx1_b, d_mlp_nw0 = _mlp_bwd(x1, mlp_nw[0:1], w_u0, w_d0, mlp0, dx2, dx2_b, "0", sink, True)
    (d_woe,) = _matmul(mix0, dx1_b, "tn", outs=[bf16], name="out_even_dw")
    tok = sink(dict(w_out_even=d_woe.reshape(N_DEV, D_MODEL // N_DEV, D_MODEL)))
    (dmix0,) = _matmul(dx1_b, w_oe, "nt", outs=[f32], name="out_even_dx")
    dpe = _ret_bwd(pe, cos, sin, rtab, o_ret, s_ret, dmix0, "ret_bwd")
    dxc, dpe, d_wr, d_wi, d_br, d_bi, d_lam = _lru_bwd(xc, pe, 20, wr_bd, wi_bd, lru_br, lru_bi, lru_lam + tok[0:1, 0:1],
                                                       h_lru, dmix0, 4, dpe, "lru_bwd")
    dpe, d_lconv, d_lconv_b = _conv_bwd(pe, 4, w_lc, lru_b, False, dxc, dpe, "lru_conv_bwd")
    G = dict(
        mlp_norm_w=jnp.concatenate([d_mlp_nw0, d_mlp_nw1], axis=0),
        final_norm_w=d_final.reshape(-1),
        lru_conv_w=d_lconv, lru_conv_b=d_lconv_b.reshape(-1),
        lru_w_r=d_wr, lru_b_r=d_br.reshape(-1), lru_w_i=d_wi, lru_b_i=d_bi.reshape(-1),
        lru_lambda=d_lam.reshape(-1), gdn_conv_w=d_gconv,
        gdn_a_log=d_alog[0, GDN_HEADS:2 * GDN_HEADS], gdn_dt_bias=d_dtb[0, GDN_HEADS:2 * GDN_HEADS],
        gdn_norm_w=d_gnw.reshape(-1),
    )
    packed = _pack([G[k] for k in _SMALL] + [d_lconv, d_gconv, loss[0, 0:1]])
    tok = sink(dict(small=jnp.broadcast_to(packed[None], (N_DEV,) + packed.shape)))
    (d_wie,) = _matmul(h0, dpe, "tn", outs=[bf16], shard_cols=2, after=(tok,), name="in_even_dw")
    tok = sink(dict(w_in_even=d_wie))
    dx0, _, d_mix_nw0 = _matmul(dpe, w_ie, "nt", outs=_RMS_BWD_OUTS, extras=(x0, dx1, mix_nw[0:1]), epilogue=_rms_bwd_ep,
                                after=(tok,), name="in_even_dx", **_SQUARE_TILES)
    G["mixer_norm_w"] = jnp.concatenate([d_mix_nw0, d_mix_nw1], axis=0)
    return loss, dx0, G


_SMALL = ["mlp_norm_w", "final_norm_w", "lru_conv_b", "lru_w_r", "lru_b_r", "lru_w_i", "lru_b_i",
          "lru_lambda", "gdn_a_log", "gdn_dt_bias", "gdn_norm_w"]
_PACK_ROWS = 688


def _pack(parts):
    flat = jnp.concatenate([p.reshape(-1) for p in parts])
    return jnp.pad(flat, (0, _PACK_ROWS * 128 - flat.shape[0])).reshape(_PACK_ROWS, 128)


def _unpack(packed, shapes):
    flat = packed.reshape(-1)
    out, off = [], 0
    for s in shapes:
        n = int(np.prod(s))
        out.append(flat[off:off + n].reshape(s))
        off += n
    return out


def kernel(x, mixer_norm_w, mlp_norm_w, final_norm_w, w_in_even, lru_conv_w, lru_conv_b, lru_w_r, lru_b_r, lru_w_i, lru_b_i, lru_lambda, w_out_even, w_in_odd, gdn_conv_w, gdn_a_log, gdn_dt_bias, gdn_norm_w, w_out_odd, w_up, w_down, loss_target, m_mixer_norm_w, m_mlp_norm_w, m_final_norm_w, m_w_in_even, m_lru_conv_w, m_lru_conv_b, m_lru_w_r, m_lru_b_r, m_lru_w_i, m_lru_b_i, m_lru_lambda, m_w_out_even, m_w_in_odd, m_gdn_conv_w, m_gdn_a_log, m_gdn_dt_bias, m_gdn_norm_w, m_w_out_odd, m_w_up, m_w_down, v_mixer_norm_w, v_mlp_norm_w, v_final_norm_w, v_w_in_even, v_lru_conv_w, v_lru_conv_b, v_lru_w_r, v_lru_b_r, v_lru_w_i, v_lru_b_i, v_lru_lambda, v_w_out_even, v_w_in_odd, v_gdn_conv_w, v_gdn_a_log, v_gdn_dt_bias, v_gdn_norm_w, v_w_out_odd, v_w_up, v_w_down):
    Pw = dict(mixer_norm_w=mixer_norm_w, mlp_norm_w=mlp_norm_w, final_norm_w=final_norm_w, w_in_even=w_in_even,
              lru_conv_w=lru_conv_w, lru_conv_b=lru_conv_b, lru_w_r=lru_w_r, lru_b_r=lru_b_r, lru_w_i=lru_w_i,
              lru_b_i=lru_b_i, lru_lambda=lru_lambda, w_out_even=w_out_even, w_in_odd=w_in_odd, gdn_conv_w=gdn_conv_w,
              gdn_a_log=gdn_a_log, gdn_dt_bias=gdn_dt_bias, gdn_norm_w=gdn_norm_w, w_out_odd=w_out_odd, w_up=w_up,
              w_down=w_down)
    Pm = dict(mixer_norm_w=m_mixer_norm_w, mlp_norm_w=m_mlp_norm_w, final_norm_w=m_final_norm_w, w_in_even=m_w_in_even,
              lru_conv_w=m_lru_conv_w, lru_conv_b=m_lru_conv_b, lru_w_r=m_lru_w_r, lru_b_r=m_lru_b_r, lru_w_i=m_lru_w_i,
              lru_b_i=m_lru_b_i, lru_lambda=m_lru_lambda, w_out_even=m_w_out_even, w_in_odd=m_w_in_odd,
              gdn_conv_w=m_gdn_conv_w, gdn_a_log=m_gdn_a_log, gdn_dt_bias=m_gdn_dt_bias, gdn_norm_w=m_gdn_norm_w,
              w_out_odd=m_w_out_odd, w_up=m_w_up, w_down=m_w_down)
    Pv = dict(mixer_norm_w=v_mixer_norm_w, mlp_norm_w=v_mlp_norm_w, final_norm_w=v_final_norm_w, w_in_even=v_w_in_even,
              lru_conv_w=v_lru_conv_w, lru_conv_b=v_lru_conv_b, lru_w_r=v_lru_w_r, lru_b_r=v_lru_b_r, lru_w_i=v_lru_w_i,
              lru_b_i=v_lru_b_i, lru_lambda=v_lru_lambda, w_out_even=v_w_out_even, w_in_odd=v_w_in_odd,
              gdn_conv_w=v_gdn_conv_w, gdn_a_log=v_gdn_a_log, gdn_dt_bias=v_gdn_dt_bias, gdn_norm_w=v_gdn_norm_w,
              w_out_odd=v_w_out_odd, w_up=v_w_up, w_down=v_w_down)
    me = _me()[3]
    T = x.shape[1]

    cols = lambda g: jnp.transpose(g, (1, 0, 2)).reshape(g.shape[1], -1)
    rows = lambda g: g.reshape(-1, g.shape[2])
    wide = lambda g: _odd_assemble(g, "w_in_odd_assemble")
    as_is = lambda g: g
    relay_groups = (("w_in_even",), ("w_up0", "w_down0"), ("w_in_odd",))
    relayed = sum(relay_groups, ())
    (first,), started = _send_start([w_in_even[0].astype(bf16)], False, "gather_start_w_in_even", relayed=[0])
    cast = lambda a: (a + started[0:1, 0:1]).astype(bf16)
    odd_shard = jnp.pad(cast(w_in_odd[0]), ((0, 0), (0, ODD_SHARD_PAD - ODD_SHARD)))
    gather = dict(
        w_in_even=(None, cols), lru_conv_w=(lru_conv_w[0], cols),
        w_out_even=(cast(w_out_even[0]), rows), w_up0=(cast(w_up[0]), as_is), w_down0=(cast(w_down[0]), rows),
        w_in_odd=(odd_shard, wide), gdn_conv_w=(gdn_conv_w[0], cols),
        w_out_odd=(cast(w_out_odd[0]), rows), w_up1=(cast(w_up[1]), as_is), w_down1=(cast(w_down[1]), rows))
    later = [name for name in gather if name != "w_in_even"]
    handles, tok = _send_start([gather[name][0] for name in later], False, "gather_start",
                               relayed=[i for i, name in enumerate(later) if name in relayed])
    handles = dict(zip(later, handles), w_in_even=first)
    relays, landed, full = {}, {}, {}

    def ahead(group, after):
        lands = [_send_wait(handles[n], False, after, f"gather_wait_{n}", relayed=True)[1] for n in group]
        relays[group], token = _relay_start(lands, "relay_start_" + "_".join(group))
        return token

    def weight(name, after):
        if name not in landed and name in relayed:
            group = next(g for g in relay_groups if name in g)
            if group not in relays:
                ahead(group, after)
            landed.update(zip(group, _relay_wait(relays[group], after, "relay_wait_" + "_".join(group))))
        elif name not in landed:
            landed[name] = _send_wait(handles[name], False, after, f"gather_wait_{name}")[1]
        if name not in full:
            full[name] = gather[name][1](landed[name])
        return full[name]

    P = {k: Pw[k] for k in ("mlp_norm_w", "final_norm_w")}
    P["mixer_norm_w"] = mixer_norm_w + tok[0:1, 0:1]
    for k in ("lru_w_r", "lru_w_i", "lru_conv_b", "lru_b_r", "lru_b_i", "lru_lambda", "gdn_a_log", "gdn_dt_bias", "gdn_norm_w"):
        P[k] = Pw[k][0]

    sent = {}

    def sink(grads):
        hs, token = _send_start(list(grads.values()), True, "grads_start_" + "_".join(grads))
        sent.update(zip(grads, hs))
        return token

    loss, dx, G = _local_step(x[0], loss_target[0], P, weight, sink, ahead)
    lanes = lambda a: a.reshape(-1, 128)
    sink(dict(mixer_norm_w=jnp.broadcast_to(lanes(G["mixer_norm_w"])[None], (N_DEV, 2 * D_MODEL // 128, 128))))

    def received(name, after=dx):
        return _send_wait(sent[name], True, after, f"grads_wait_{name}")[1]

    out = {}
    nff = D_FF // N_DEV

    def whole(name, gs):
        out[name] = tuple(_adamw(Pw[name], gs, Pm[name], Pv[name], f"adamw_{name}", layer=0))

    def layers(name):
        res = None
        for l in range(2):
            res = _adamw(Pw[name], received(f"{name}{l}"), Pm[name], Pv[name], f"adamw_{name}{l}", layer=l, prev=res)
        out[name] = tuple(res)

    layers("w_up")
    layers("w_down")
    whole("w_out_odd", received("w_out_odd"))
    out["w_in_odd"] = tuple(_adamw_column_major(w_in_odd, received("w_in_odd"), m_w_in_odd, v_w_in_odd, "adamw_w_in_odd"))
    whole("w_out_even", received("w_out_even"))
    small_shapes = [Pw[k].shape for k in _SMALL]
    pw, pm, pv = (_pack([Q[k] for k in _SMALL]) for Q in (Pw, Pm, Pv))
    sg, sd, sm, sv = _adamw(pw, received("small", out["w_out_even"][1]), pm, pv, "adamw_small")
    for arrs_i, packed_out in enumerate((sg, sd, sm, sv)):
        for k, a in zip(_SMALL, _unpack(packed_out, small_shapes)):
            out.setdefault(k, [None] * 4)[arrs_i] = a
    whole("w_in_even", received("w_in_even", sd))
    out["mixer_norm_w"] = tuple(
        a.reshape(mixer_norm_w.shape) for a in
        _adamw(lanes(mixer_norm_w), received("mixer_norm_w", out["w_in_even"][1]), lanes(m_mixer_norm_w),
               lanes(v_mixer_norm_w), "adamw_mixer_norm_w"))
    n_small = sum(int(np.prod(s)) for s in small_shapes)
    gflat = sg.reshape(-1)
    g_lconv = gflat[n_small:n_small + CONV_K * LRU_WIDTH].reshape(CONV_K, LRU_WIDTH)
    g_gconv = gflat[n_small + CONV_K * LRU_WIDTH:n_small + CONV_K * (LRU_WIDTH + 3072)].reshape(CONV_K, 3072)
    whole("lru_conv_w", lax.dynamic_slice_in_dim(g_lconv, me * 64, 64, axis=1)[None])
    whole("gdn_conv_w", lax.dynamic_slice_in_dim(g_gconv, me * 384, 384, axis=1)[None])

    names = ["mixer_norm_w", "mlp_norm_w", "final_norm_w", "w_in_even", "lru_conv_w", "lru_conv_b", "lru_w_r", "lru_b_r",
             "lru_w_i", "lru_b_i", "lru_lambda", "w_out_even", "w_in_odd", "gdn_conv_w", "gdn_a_log", "gdn_dt_bias",
             "gdn_norm_w", "w_out_odd", "w_up", "w_down"]
    total = gflat[n_small + CONV_K * (LRU_WIDTH + 3072)]
    res = [total, dx[None]]
    for j in range(4):
        res += [out[k][j] for k in names]
    return tuple(res)
```

```python
import math

import numpy as np
import jax
import jax.numpy as jnp
from jax import lax
from jax.experimental import pallas as pl
from jax.experimental.pallas import tpu as pltpu

f32 = jnp.float32
bf16 = jnp.bfloat16

N_DEV = 8
D_MODEL = 1024
D_FF = 4096
EPS = 1e-6
RET_HEADS = 4
RET_CHUNK = 128
RET_STEP = 4
ROPE_THETA = 10000.0
LRU_WIDTH = 512
LRU_C = 8.0
GDN_HEADS = 8
GDN_CHUNK = 64
GDN_STEP = 4
HEAD_DIM = 128
ODD_IN = 4112
ODD_PAD = 4224
ODD_SHARD = ODD_IN // N_DEV
ODD_SHARD_PAD = 640
ADAM_LR, ADAM_B1, ADAM_B2, ADAM_EPS, ADAM_WD, ADAM_STEP = 0.001, 0.9, 0.999, 1e-08, 0.01, 10
VMEM_LIMIT = 56 * 1024 * 1024

_NN = (((1,), (0,)), ((), ()))
_NT = (((1,), (1,)), ((), ()))
_TN = (((0,), (0,)), ((), ()))
MESH = pl.DeviceIdType.MESH


def _cparams(sem):
    return pltpu.CompilerParams(dimension_semantics=sem, vmem_limit_bytes=VMEM_LIMIT)


def _dot(a, b, dn):
    return lax.dot_general(a.astype(bf16), b.astype(bf16), dn, preferred_element_type=f32)


def _dot01(a01, b, dn):
    a = a01.astype(bf16)
    b0 = b.astype(bf16)
    r1 = b - b0.astype(f32)
    b1 = r1.astype(bf16)
    b2 = (r1 - b1.astype(f32)).astype(bf16)
    d = lambda q: lax.dot_general(a, q, dn, preferred_element_type=f32)
    return d(b0) + (d(b1) + d(b2))


def _sigmoid(x):
    return jax.nn.sigmoid(x)


def _silu(x):
    return x * _sigmoid(x)


def _dsilu(x):
    s = _sigmoid(x)
    return s * (1.0 + x * (1.0 - s))


def _softplus(x):
    return jnp.maximum(x, 0.0) + jnp.log1p(jnp.exp(-jnp.abs(x)))


_GELU_C = math.sqrt(2.0 / math.pi)


def _gelu(y):
    return 0.5 * y * (1.0 + jnp.tanh(_GELU_C * (y + 0.044715 * y * y * y)))


def _dgelu(y):
    t = jnp.tanh(_GELU_C * (y + 0.044715 * y * y * y))
    return 0.5 * (1.0 + t) + 0.5 * y * (1.0 - t * t) * _GELU_C * (1.0 + 3.0 * 0.044715 * y * y)


def _matmul(a, b, form, *, outs, name, epilogue=None, extras=(), tm=4096, tn=512, tk=1024, shard_cols=0, a_map=None,
            after=()):
    if form == "tn":
        K, M = a.shape
    else:
        M, K = a.shape
    per_step = 1
    if b.ndim == 3:
        assert form in ("nn", "nt"), name
        N = b.shape[1] if form == "nt" else N_DEV * b.shape[2]
        if form == "nn":
            tn = b.shape[2]
        else:
            per_step = max(1, tk // b.shape[2])
            tk = per_step * b.shape[2]
    else:
        N = b.shape[0] if form == "nt" else b.shape[1]
    ns = N // N_DEV
    if shard_cols:
        tn = ns * shard_cols
    tm, tn, tk = min(tm, M), min(tn, N), min(tk, K)
    assert M % tm == 0 and N % tn == 0 and K % tk == 0, (name, M, N, K, tm, tn, tk)
    nk = K // tk
    dn = {"nn": _NN, "nt": _NT, "tn": _TN}[form]
    if form == "tn":
        a_spec = pl.BlockSpec((tk, tm), lambda i, j, k: (k, i))
    else:
        a_spec = pl.BlockSpec((tm, tk), lambda i, j, k: (i, k))
    if b.ndim == 3:
        b_spec = (pl.BlockSpec((per_step, tn, tk // per_step), lambda i, j, k: (k, j, 0)) if form == "nt"
                  else pl.BlockSpec((None, tk, tn), lambda i, j, k: (j, k, 0)))
    elif form == "nt":
        b_spec = pl.BlockSpec((tn, tk), lambda i, j, k: (j, k))
    else:
        b_spec = pl.BlockSpec((tk, tn), lambda i, j, k: (k, j))
    e_spec = pl.BlockSpec((tm, tn), lambda i, j, k: (i, j))
    v_spec = pl.BlockSpec((1, tn), lambda i, j, k: (0, j))
    if shard_cols:
        o_spec = pl.BlockSpec((shard_cols, tm, ns), lambda i, j, k: (j, i, 0))
        o_shape = (N_DEV, M, ns)
    else:
        o_spec = e_spec
        o_shape = (M, N)
    n_ex = len(extras)
    n_in = 2 + n_ex + len(after)
    sums = [isinstance(o, tuple) for o in outs]
    assert not any(sums) or tn == N, name

    def finish(acc, ex, o_refs, row_tile):
        vals = (acc,) if epilogue is None else epilogue(acc, *[e[...] for e in ex])
        for r, v, is_sum in zip(o_refs, vals, sums):
            if is_sum:
                @pl.when(row_tile == 0)
                def _(r=r, v=v):
                    r[...] = v.astype(r.dtype)

                @pl.when(row_tile > 0)
                def _(r=r, v=v):
                    r[...] += v.astype(r.dtype)
            elif shard_cols:
                for s in range(shard_cols):
                    r[s] = v[:, s * ns:(s + 1) * ns].astype(r.dtype)
            else:
                r[...] = v.astype(r.dtype)

    def prod(a_ref, b_ref):
        if b.ndim == 3 and form == "nt":
            w = tk // per_step
            return sum(_dot(a_ref[:, s * w:(s + 1) * w], b_ref[s], dn) for s in range(1, per_step)) + _dot(a_ref[:, 0:w], b_ref[0], dn)
        av = a_ref[...]
        return _dot(av if a_map is None else a_map(av), b_ref[...], dn)

    def body_one(*refs):
        finish(prod(*refs[:2]), refs[2:2 + n_ex], refs[n_in:], pl.program_id(0))

    def body_acc(*refs):
        a_ref, b_ref = refs[:2]
        acc = refs[-1]
        k = pl.program_id(2)
        row_tile = pl.program_id(0)

        @pl.when(k == 0)
        def _():
            acc[...] = prod(a_ref, b_ref)

        @pl.when((k > 0) & (k < nk - 1))
        def _():
            acc[...] += prod(a_ref, b_ref)

        @pl.when(k == nk - 1)
        def _():
            finish(acc[...] + prod(a_ref, b_ref), refs[2:2 + n_ex], refs[n_in:-1], row_tile)

    return pl.pallas_call(
        body_one if nk == 1 else body_acc, grid=(M // tm, N // tn, nk),
        in_specs=[a_spec, b_spec] + [v_spec if e.shape[0] == 1 else e_spec for e in extras]
        + [pl.BlockSpec(memory_space=pl.ANY)] * len(after),
        out_specs=[v_spec if s else o_spec for s in sums],
        out_shape=[jax.ShapeDtypeStruct((1, N), o[1]) if s else jax.ShapeDtypeStruct(o_shape, o) for o, s in zip(outs, sums)],
        scratch_shapes=[] if nk == 1 else [pltpu.VMEM((tm, tn), f32)],
        compiler_params=_cparams(("arbitrary" if any(sums) else "parallel", "parallel", "arbitrary")), name=name,
    )(a, b, *extras, *after)


def _rms_fwd(x, w, name):
    T, D = x.shape
    tt = min(1024, T)

    def body(x_ref, w_ref, h_ref):
        xv = x_ref[...]
        r = lax.rsqrt(jnp.mean(xv * xv, axis=1, keepdims=True) + EPS)
        h_ref[...] = (xv * r * w_ref[...]).astype(bf16)

    return pl.pallas_call(
        body, grid=(T // tt,),
        in_specs=[pl.BlockSpec((tt, D), lambda i: (i, 0)), pl.BlockSpec((1, D), lambda i: (0, 0))],
        out_specs=pl.BlockSpec((tt, D), lambda i: (i, 0)),
        out_shape=jax.ShapeDtypeStruct((T, D), bf16),
        compiler_params=_cparams(("parallel",)), name=name,
    )(x, w)


def _residual_rms_ep(acc, res, w):
    x = res + acc
    r = lax.rsqrt(jnp.mean(x * x, axis=1, keepdims=True) + EPS)
    return x, x * r * w


_RMS_BWD_OUTS = [f32, bf16, ("sum", f32)]


def _rms_bwd_ep(dh, x, dres, w):
    r = lax.rsqrt(jnp.mean(x * x, axis=1, keepdims=True) + EPS)
    xn = x * r
    dhw = dh * w
    dx = dres + r * (dhw - xn * jnp.mean(dhw * xn, axis=1, keepdims=True))
    return dx, dx, jnp.sum(dh * xn, axis=0, keepdims=True)


_LOSS_OUTS = [("sum", f32), f32, bf16, ("sum", f32)]


def _loss_ep(acc, res, w, tgt):
    x = res + acc
    D = x.shape[1]
    r = lax.rsqrt(jnp.mean(x * x, axis=1, keepdims=True) + EPS)
    xn = x * r
    e = xn * w - tgt
    loss = 0.5 * jnp.sum(jnp.mean(e * e, axis=1, keepdims=True), axis=0, keepdims=True)
    dy = e * (1.0 / D)
    dyw = dy * w
    dx = r * (dyw - xn * jnp.mean(dyw * xn, axis=1, keepdims=True))
    return jnp.broadcast_to(loss, (1, D)), dx, dx, jnp.sum(dy * xn, axis=0, keepdims=True)


def _ret_tables():
    H, C = RET_HEADS, RET_CHUNK
    lg = np.log1p(-np.exp2(-5.0 - np.arange(H, dtype=np.float32))).astype(np.float32)
    idx = np.arange(C, dtype=np.float32)
    diff = idx[:, None] - idx[None, :]
    causal = diff >= 0
    dm = np.where(causal[None], np.exp(lg[:, None, None] * np.where(causal, diff, 0.0)[None]), 0.0)
    qd = np.exp(lg[:, None] * (idx[None, :] + 1.0))
    kd = np.exp(lg[:, None] * (C - 1.0 - idx[None, :]))
    cg = np.exp(lg * C)
    tab = np.zeros((H, 4, C, HEAD_DIM), np.float32)
    tab[:, 0] = dm
    tab[:, 1] = qd[:, :, None]
    tab[:, 2] = kd[:, :, None]
    tab[:, 3] = cg[:, None, None]
    return jnp.asarray(tab)


def _rope_tables(T):
    half = HEAD_DIM // 2
    inv = np.float32(ROPE_THETA) ** (-np.arange(half, dtype=np.float32) / np.float32(half))
    ang = np.arange(T, dtype=np.float32)[:, None] * inv[None, :]
    c, s = np.cos(ang), np.sin(ang)
    return jnp.asarray(np.concatenate([c, c], axis=1)), jnp.asarray(np.concatenate([-s, s], axis=1))


def _rope(x, cos, sin):
    return x * cos + pltpu.roll(x, HEAD_DIM // 2, 1) * sin


def _unrope(y, cos, sin):
    return y * cos + pltpu.roll(y * sin, HEAD_DIM // 2, 1)


def _stack_heads(ref, H, f=None):
    parts = [ref[:, h * HEAD_DIM:(h + 1) * HEAD_DIM] for h in range(H)]
    return jnp.stack(parts if f is None else [f(a) for a in parts])


def _ret_fwd(p, cos, sin, tab, name):
    T = p.shape[0]
    C, H = RET_CHUNK, RET_HEADS
    N = T // C
    K = min(RET_STEP, N)
    NS = N // K
    scale = HEAD_DIM ** -0.5

    def body(q_ref, k_ref, v_ref, g_ref, c_ref, s_ref, t_ref, y_ref, o_ref, sp_ref, st):
        @pl.when(pl.program_id(0) == 0)
        def _():
            st[...] = jnp.zeros_like(st)

        dm, qd, kd, cg = t_ref[:, 0], t_ref[:, 1], t_ref[:, 2], t_ref[:, 3]
        S = st[...]
        for c in range(K):
            rows = pl.ds(c * C, C)
            cos_, sin_ = c_ref[rows, :], s_ref[rows, :]
            rot = lambda a: _rope(a, cos_, sin_)
            q = _stack_heads(q_ref.at[rows, :], H, rot)
            k = _stack_heads(k_ref.at[rows, :], H, rot) * scale
            v = _stack_heads(v_ref.at[rows, :], H)
            P = _dot(q, k, _NT3) * dm
            o = _dot(P, v, _NN3) + _dot(q * qd, S, _NN3)
            sp_ref[c] = S
            S = cg * S + _dot(k * kd, v, _TN3)
            r = lax.rsqrt(jnp.mean(o * o, axis=2, keepdims=True) + EPS)
            y = o * r * _silu(_stack_heads(g_ref.at[rows, :], H))
            for h in range(H):
                o_ref[rows, h * HEAD_DIM:(h + 1) * HEAD_DIM] = o[h]
                y_ref[rows, h * HEAD_DIM:(h + 1) * HEAD_DIM] = y[h].astype(bf16)
        st[...] = S

    wide = lambda blk: pl.BlockSpec((K * C, H * HEAD_DIM), lambda n: (n, blk))
    tbl = pl.BlockSpec((K * C, HEAD_DIM), lambda n: (n, 0))
    return pl.pallas_call(
        body, grid=(NS,),
        in_specs=[wide(0), wide(1), wide(2), wide(3), tbl, tbl,
                  pl.BlockSpec((H, 4, C, HEAD_DIM), lambda n: (0, 0, 0, 0))],
        out_specs=[wide(0), wide(0), pl.BlockSpec((K, H, HEAD_DIM, HEAD_DIM), lambda n: (n, 0, 0, 0))],
        out_shape=[jax.ShapeDtypeStruct((T, D_MODEL), bf16), jax.ShapeDtypeStruct((T, H * HEAD_DIM), f32),
                   jax.ShapeDtypeStruct((N, H, HEAD_DIM, HEAD_DIM), f32)],
        scratch_shapes=[pltpu.VMEM((H, HEAD_DIM, HEAD_DIM), f32)],
        compiler_params=_cparams(("arbitrary",)), name=name,
    )(p, p, p, p, cos, sin, tab)


def _ret_bwd(p, cos, sin, tab, o_raw, sprev, dmix, name):
    T = p.shape[0]
    C, H = RET_CHUNK, RET_HEADS
    N = T // C
    K = min(RET_STEP, N)
    NS = N // K
    scale = HEAD_DIM ** -0.5
    W = H * HEAD_DIM

    def body(q_ref, k_ref, v_ref, g_ref, c_ref, s_ref, t_ref, o_ref, sp_ref, dy_ref, d_ref, dst):
        @pl.when(pl.program_id(0) == 0)
        def _():
            dst[...] = jnp.zeros_like(dst)

        dm, qd, kd, cg = t_ref[:, 0], t_ref[:, 1], t_ref[:, 2], t_ref[:, 3]
        dS1 = dst[...]
        for c in reversed(range(K)):
            rows = pl.ds(c * C, C)
            cos_, sin_ = c_ref[rows, :], s_ref[rows, :]
            rot = lambda a: _rope(a, cos_, sin_)
            q = _stack_heads(q_ref.at[rows, :], H, rot)
            k = _stack_heads(k_ref.at[rows, :], H, rot) * scale
            v = _stack_heads(v_ref.at[rows, :], H)
            g = _stack_heads(g_ref.at[rows, :], H)
            S = sp_ref[c]
            o = _stack_heads(o_ref.at[rows, :], H)
            dy = _stack_heads(dy_ref.at[rows, :], H)
            r = lax.rsqrt(jnp.mean(o * o, axis=2, keepdims=True) + EPS)
            nrm = o * r
            dn = dy * _silu(g)
            dg = dy * nrm * _dsilu(g)
            do = r * (dn - nrm * jnp.mean(dn * nrm, axis=2, keepdims=True))
            P = _dot(q, k, _NT3) * dm
            dP = _dot(do, v, _NT3) * dm
            dq = _dot(dP, k, _NN3) + _dot(do, S, _NT3) * qd
            dk = (_dot(dP, q, _TN3) + _dot(v, dS1, _NT3) * kd) * scale
            dv = _dot(P, do, _TN3) + _dot(k * kd, dS1, _NN3)
            dS1 = cg * dS1 + _dot(q * qd, do, _TN3)
            for h in range(H):
                d_ref[rows, h * HEAD_DIM:(h + 1) * HEAD_DIM] = _unrope(dq[h], cos_, sin_).astype(bf16)
                d_ref[rows, W + h * HEAD_DIM:W + (h + 1) * HEAD_DIM] = _unrope(dk[h], cos_, sin_).astype(bf16)
                d_ref[rows, 2 * W + h * HEAD_DIM:2 * W + (h + 1) * HEAD_DIM] = dv[h].astype(bf16)
                d_ref[rows, 3 * W + h * HEAD_DIM:3 * W + (h + 1) * HEAD_DIM] = dg[h].astype(bf16)
        dst[...] = dS1

    rev = lambda blk: pl.BlockSpec((K * C, W), lambda n: (NS - 1 - n, blk))
    tbl = pl.BlockSpec((K * C, HEAD_DIM), lambda n: (NS - 1 - n, 0))
    return pl.pallas_call(
        body, grid=(NS,),
        in_specs=[rev(0), rev(1), rev(2), rev(3), tbl, tbl,
                  pl.BlockSpec((H, 4, C, HEAD_DIM), lambda n: (0, 0, 0, 0)), rev(0),
                  pl.BlockSpec((K, H, HEAD_DIM, HEAD_DIM), lambda n: (NS - 1 - n, 0, 0, 0)), rev(0)],
        out_specs=pl.BlockSpec((K * C, 4 * W), lambda n: (NS - 1 - n, 0)),
        out_shape=jax.ShapeDtypeStruct((T, 6 * W), bf16),
        scratch_shapes=[pltpu.VMEM((H, HEAD_DIM, HEAD_DIM), f32)],
        compiler_params=_cparams(("arbitrary",)), name=name,
    )(p, p, p, p, cos, sin, tab, o_raw, sprev, dmix)


CONV_K = 4
CONV_W = 512


def _conv_block(C, col_off):
    cw = CONV_W if (C % (2 * CONV_W) or col_off % 2) else 2 * CONV_W
    return cw, col_off * CONV_W // cw
PAD = 8
SUB_R = 64


def _conv_fwd(x, col_off, w, b, act, name):
    T = x.shape[0]
    C = w.shape[1]
    cw, cb = _conv_block(C, col_off)
    G = C // cw
    tt = min(1024, T)
    NT = T // tt
    has_b = b is not None

    def body(*refs):
        if has_b:
            x_ref, w_ref, b_ref, y_ref, pad = refs
        else:
            x_ref, w_ref, y_ref, pad = refs
        t = pl.program_id(1)

        @pl.when(t == 0)
        def _():
            pad[pl.ds(0, PAD), :] = jnp.zeros((PAD, cw), f32)

        pad[pl.ds(PAD, tt), :] = x_ref[...]
        for g in range(cw // 128):
            ls = slice(g * 128, (g + 1) * 128)
            wv = w_ref[:, ls]
            for c in range(tt // SUB_R):
                r0 = c * SUB_R
                y = wv[0:1, :] * pad[pl.ds(PAD - 3 + r0, SUB_R), ls]
                for kk in range(1, CONV_K):
                    y = y + wv[kk:kk + 1, :] * pad[pl.ds(PAD - 3 + kk + r0, SUB_R), ls]
                if has_b:
                    y = y + b_ref[:, ls]
                y_ref[pl.ds(r0, SUB_R), ls] = _silu(y) if act else y
        tail = pad[pl.ds(tt, PAD), :]
        pad[pl.ds(0, PAD), :] = tail

    in_specs = [pl.BlockSpec((tt, cw), lambda g, t: (t, cb + g)),
                pl.BlockSpec((CONV_K, cw), lambda g, t: (0, g))]
    args = [x, w]
    if has_b:
        in_specs.append(pl.BlockSpec((1, cw), lambda g, t: (0, g)))
        args.append(b)
    return pl.pallas_call(
        body, grid=(G, NT), in_specs=in_specs,
        out_specs=pl.BlockSpec((tt, cw), lambda g, t: (t, g)),
        out_shape=jax.ShapeDtypeStruct((T, C), f32),
        scratch_shapes=[pltpu.VMEM((tt + PAD, cw), f32)],
        compiler_params=_cparams(("parallel", "arbitrary")), name=name,
    )(*args)


def _conv_bwd(x, col_off, w, b, act, dout, dp, name):
    T = x.shape[0]
    C = w.shape[1]
    cw, cb = _conv_block(C, col_off)
    G = C // cw
    tt = min(1024, T)
    NT = T // tt
    has_b = b is not None

    def body(*refs):
        if has_b:
            x_ref, xp_ref, w_ref, b_ref, d_ref, dp_in, dx_ref, dw_ref, db_ref, pad, dpad = refs
        else:
            x_ref, xp_ref, w_ref, d_ref, dp_in, dx_ref, dw_ref, db_ref, pad, dpad = refs
        t = pl.program_id(1)
        first_tile = t == NT - 1

        @pl.when(t == 0)
        def _():
            dpad[pl.ds(tt, PAD), :] = jnp.zeros((PAD, cw), f32)
            dw_ref[...] = jnp.zeros_like(dw_ref)
            db_ref[...] = jnp.zeros_like(db_ref)

        pad[pl.ds(0, PAD), :] = jnp.where(first_tile, 0.0, xp_ref[...])
        pad[pl.ds(PAD, tt), :] = x_ref[...]
        fold = lambda v: v.reshape(SUB_R // 8, 8, 128).sum(axis=0)
        for g in range(cw // 128):
            ls = slice(g * 128, (g + 1) * 128)
            wv = w_ref[:, ls]
            acc =[jnp.zeros((8, 128), f32) for _ in range(CONV_K + 1)]
            for c in reversed(range(tt // SUB_R)):
                r0 = c * SUB_R
                xs = [pad[pl.ds(PAD - 3 + kk + r0, SUB_R), ls] for kk in range(CONV_K)]
                dy = d_ref[pl.ds(r0, SUB_R), ls]
                if act:
                    y = wv[0:1, :] * xs[0]
                    for kk in range(1, CONV_K):
                        y = y + wv[kk:kk + 1, :] * xs[kk]
                    if has_b:
                        y = y + b_ref[:, ls]
                    dy = dy * _dsilu(y)
                dpad[pl.ds(r0, SUB_R), ls] = dy
                dx = wv[3:4, :] * dy
                for j in range(1, CONV_K):
                    dx = dx + wv[3 - j:4 - j, :] * dpad[pl.ds(r0 + j, SUB_R), ls]
                dx_ref[pl.ds(r0, SUB_R), ls] = dx.astype(bf16)
                for kk in range(CONV_K):
                    acc[kk] = acc[kk] + fold(dy * xs[kk])
                acc[CONV_K] = acc[CONV_K] + fold(dy)
            for kk in range(CONV_K):
                dw_ref[kk:kk + 1, ls] += jnp.sum(acc[kk], axis=0, keepdims=True)
            db_ref[:, ls] += jnp.sum(acc[CONV_K], axis=0, keepdims=True)
        head = dpad[pl.ds(0, PAD), :]
        dpad[pl.ds(tt, PAD), :] = head

    rows8 = tt // PAD
    in_specs = [pl.BlockSpec((tt, cw), lambda g, t: (NT - 1 - t, cb + g)),
                pl.BlockSpec((PAD, cw), lambda g, t: (jnp.maximum((NT - 1 - t) * rows8 - 1, 0), cb + g)),
                pl.BlockSpec((CONV_K, cw), lambda g, t: (0, g))]
    args = [x, x, w]
    if has_b:
        in_specs.append(pl.BlockSpec((1, cw), lambda g, t: (0, g)))
        args.append(b)
    in_specs += [pl.BlockSpec((tt, cw), lambda g, t: (NT - 1 - t, g)), pl.BlockSpec(memory_space=pl.ANY)]
    args += [dout, dp]
    return pl.pallas_call(
        body, grid=(G, NT), in_specs=in_specs,
        out_specs=[pl.BlockSpec((tt, cw), lambda g, t: (NT - 1 - t, cb + g)),
                   pl.BlockSpec((CONV_K, cw), lambda g, t: (0, g)),
                   pl.BlockSpec((1, cw), lambda g, t: (0, g))],
        out_shape=[jax.ShapeDtypeStruct(dp.shape, dp.dtype), jax.ShapeDtypeStruct((CONV_K, C), f32),
                   jax.ShapeDtypeStruct((1, C), f32)],
        input_output_aliases={len(args) - 1: 0},
        scratch_shapes=[pltpu.VMEM((tt + PAD, cw), f32), pltpu.VMEM((tt + PAD, cw), f32)],
        compiler_params=_cparams(("parallel", "arbitrary")), name=name,
    )(*args)


def _lru_gates(xc, wr, wi, br, bi, lam):
    r = _sigmoid(_dot(xc, wr, _NN) + br)
    i = _sigmoid(_dot(xc, wi, _NN) + bi)
    sp = _softplus(-lam)
    a = jnp.exp(-LRU_C * r * sp)
    mult = jnp.sqrt(1.0 - a * a)
    return r, i, sp, a, mult


def _lru_fwd(xc, p, y_off, wr, wi, br, bi, lam, mix, name):
    T = xc.shape[0]
    G = LRU_WIDTH // 128
    tt = min(512, T)
    NT = T // tt

    def body(x_ref, y_ref, wr_ref, wi_ref, br_ref, bi_ref, l_ref, mix_in, o_ref, h_ref, hc):
        t = pl.program_id(1)

        @pl.when(t == 0)
        def _():
            hc[...] = jnp.zeros_like(hc)

        x = x_ref[...]
        r, i, sp, a, mult = _lru_gates(x, wr_ref[...], wi_ref[...], br_ref[...], bi_ref[...], l_ref[...])
        row = lax.broadcasted_iota(jnp.int32, (tt, 128), 0)
        mult = jnp.where((row == 0) & (t == 0), 1.0, mult)
        U = x * i * mult
        A = a
        d = 1
        while d < tt:
            keep = row >= d
            Ush = jnp.where(keep, pltpu.roll(U, d, 0), 0.0)
            Ash = jnp.where(keep, pltpu.roll(A, d, 0), 1.0)
            U = A * Ush + U
            A = A * Ash
            d *= 2
        h = U + A * hc[0:1, :]
        h_ref[...] = h
        hc[...] = jnp.broadcast_to(h[tt - 1:tt, :], hc.shape)
        o_ref[...] = (h * _gelu(y_ref[...])).astype(bf16)

    tile = pl.BlockSpec((tt, 128), lambda g, t: (t, g))
    vec = pl.BlockSpec((1, 128), lambda g, t: (0, g))
    wsp = pl.BlockSpec((128, 128), lambda g, t: (g, g))
    return pl.pallas_call(
        body, grid=(G, NT),
        in_specs=[tile, pl.BlockSpec((tt, 128), lambda g, t: (t, y_off + g)), wsp, wsp, vec, vec, vec,
                  pl.BlockSpec(memory_space=pl.ANY)],
        out_specs=[pl.BlockSpec((tt, 128), lambda g, t: (t, G + g)), tile],
        out_shape=[jax.ShapeDtypeStruct(mix.shape, mix.dtype), jax.ShapeDtypeStruct((T, LRU_WIDTH), f32)],
        input_output_aliases={7: 0},
        scratch_shapes=[pltpu.VMEM((8, 128), f32)],
        compiler_params=_cparams(("parallel", "arbitrary")), name=name,
    )(xc, p, wr, wi, br, bi, lam, mix)


def _lru_bwd(xc, p, y_off, wr, wi, br, bi, lam, hs, dmix, d_off, dp, name):
    T = xc.shape[0]
    G = LRU_WIDTH // 128
    tt = min(512, T)
    NT = T // tt

    def body(x_ref, y_ref, wr_ref, wi_ref, br_ref, bi_ref, l_ref, h_ref, hp_ref, do_ref, dp_in,
             dx_ref, dy_ref, dwr_ref, dwi_ref, dbr_ref, dbi_ref, dl_ref, lc, an):
        t = pl.program_id(1)
        first_tile = t == NT - 1

        @pl.when(t == 0)
        def _():
            lc[...] = jnp.zeros_like(lc)
            an[...] = jnp.zeros_like(an)
            dwr_ref[...] = jnp.zeros_like(dwr_ref)
            dwi_ref[...] = jnp.zeros_like(dwi_ref)
            dbr_ref[...] = jnp.zeros_like(dbr_ref)
            dbi_ref[...] = jnp.zeros_like(dbi_ref)
            dl_ref[...] = jnp.zeros_like(dl_ref)

        x = x_ref[...]
        y = y_ref[...]
        wr, wi, lam_ = wr_ref[...], wi_ref[...], l_ref[...]
        r, i, sp, a, mult_raw = _lru_gates(x, wr, wi, br_ref[...], bi_ref[...], lam_)
        row = lax.broadcasted_iota(jnp.int32, (tt, 128), 0)
        t0 = (row == 0) & first_tile
        mult = jnp.where(t0, 1.0, mult_raw)
        h = h_ref[...]
        do = do_ref[...]
        dh = do * _gelu(y)
        dy_ref[...] = (do * h * _dgelu(y)).astype(bf16)
        B = jnp.where(row == tt - 1, an[0:1, :], pltpu.roll(a, tt - 1, 0))
        L = dh
        d = 1
        while d < tt:
            keep = row < tt - d
            Lsh = jnp.where(keep, pltpu.roll(L, tt - d, 0), 0.0)
            Bsh = jnp.where(keep, pltpu.roll(B, tt - d, 0), 1.0)
            L = L + B * Lsh
            B = B * Bsh
            d *= 2
        L = L + B * lc[0:1, :]
        lc[...] = jnp.broadcast_to(L[0:1, :], lc.shape)
        an[...] = jnp.broadcast_to(a[0:1, :], an.shape)
        hprev = jnp.where(first_tile, 0.0, hp_ref[...])[PAD - 1:PAD, :]
        hm1 = jnp.where(row == 0, hprev, pltpu.roll(h, 1, 0))
        da = L * hm1
        dxc = L * i * mult
        di = L * x * mult
        dmult = jnp.where(t0, 0.0, L * x * i)
        da = da - jnp.where(t0, 0.0, dmult * a / mult_raw)
        dlog_a = da * a
        dr = dlog_a * (-LRU_C) * sp
        dsp = jnp.sum(dlog_a * (-LRU_C) * r, axis=0, keepdims=True)
        dpr = dr * r * (1.0 - r)
        dpi = di * i * (1.0 - i)
        dx_ref[...] = dxc + _dot(dpr, wr, _NT) + _dot(dpi, wi, _NT)
        for d_ref, dpre in ((dwr_ref, dpr), (dwi_ref, dpi)):
            dw = _dot(x, dpre, _TN)
            for s in range(2):
                d_ref[s] += dw[s * 64:(s + 1) * 64, s * 64:(s + 1) * 64]
        dbr_ref[...] += jnp.sum(dpr, axis=0, keepdims=True)
        dbi_ref[...] += jnp.sum(dpi, axis=0, keepdims=True)
        dl_ref[...] += dsp * (-_sigmoid(-lam_))

    rows8 = tt // PAD
    tile = pl.BlockSpec((tt, 128), lambda g, t: (NT - 1 - t, g))
    vec = pl.BlockSpec((1, 128), lambda g, t: (0, g))
    wsp = pl.BlockSpec((128, 128), lambda g, t: (g, g))
    wout = pl.BlockSpec((2, 64, 64), lambda g, t: (g, 0, 0))
    return pl.pallas_call(
        body, grid=(G, NT),
        in_specs=[tile, pl.BlockSpec((tt, 128), lambda g, t: (NT - 1 - t, y_off + g)), wsp, wsp, vec, vec, vec, tile,
                  pl.BlockSpec((PAD, 128), lambda g, t: (jnp.maximum((NT - 1 - t) * rows8 - 1, 0), g)),
                  pl.BlockSpec((tt, 128), lambda g, t: (NT - 1 - t, d_off + g)), pl.BlockSpec(memory_space=pl.ANY)],
        out_specs=[tile, pl.BlockSpec((tt, 128), lambda g, t: (NT - 1 - t, y_off + g)), wout, wout, vec, vec, vec],
        out_shape=[jax.ShapeDtypeStruct((T, LRU_WIDTH), f32), jax.ShapeDtypeStruct(dp.shape, dp.dtype),
                   jax.ShapeDtypeStruct((2 * G, 64, 64), f32), jax.ShapeDtypeStruct((2 * G, 64, 64), f32),
                   jax.ShapeDtypeStruct((1, LRU_WIDTH), f32), jax.ShapeDtypeStruct((1, LRU_WIDTH), f32),
                   jax.ShapeDtypeStruct((1, LRU_WIDTH), f32)],
        input_output_aliases={10: 1},
        scratch_shapes=[pltpu.VMEM((8, 128), f32), pltpu.VMEM((8, 128), f32)],
        compiler_params=_cparams(("parallel", "arbitrary")), name=name,
    )(xc, p, wr, wi, br, bi, lam, hs, hs, dmix, dp)


_NN3 = (((2,), (1,)), ((0,), (0,)))
_NT3 = (((2,), (2,)), ((0,), (0,)))
_TN3 = (((1,), (1,)), ((0,), (0,)))


def _pairs(ref, K):
    C = GDN_CHUNK
    return jnp.stack([ref[c * C:(c + 1) * C, h * HEAD_DIM:(h + 1) * HEAD_DIM] for c in range(K) for h in range(GDN_HEADS)])


def _put_pairs(ref, val, K, col=0):
    C, H = GDN_CHUNK, GDN_HEADS
    for c in range(K):
        for h in range(H):
            ref[c * C:(c + 1) * C, col + h * HEAD_DIM:col + (h + 1) * HEAD_DIM] = val[c * H + h].astype(ref.dtype)


def _rowsum(x):
    H, C, L = x.shape
    return _dot(x.reshape(H * C, L), jnp.ones((L, HEAD_DIM), f32), _NN).reshape(H, C, HEAD_DIM)


def _gdn_pre(qr, kr, v, ba, alog, dtb):
    C, H = GDN_CHUNK, GDN_HEADS
    B = qr.shape[0]
    K = B // H
    lane = lax.broadcasted_iota(jnp.int32, (C, 128), 1)
    lane3 = lax.broadcasted_iota(jnp.int32, (B, C, 128), 2)
    ri = lax.broadcasted_iota(jnp.int32, (C, C), 0)
    ci = lax.broadcasted_iota(jnp.int32, (C, C), 1)
    rowc = lax.broadcasted_iota(jnp.int32, (C, 1), 0)
    col = lambda m, j: jnp.sum(jnp.where(lane == j, m, 0.0), axis=1, keepdims=True)
    ea = jnp.exp(alog)
    tri = (ri >= ci).astype(f32)
    g_all, beta_cols, G_cols = [], [], []
    for c in range(K):
        ba_c = ba[c * C:(c + 1) * C]
        g_c = -ea * _softplus(ba_c + dtb)
        G_c = _dot01(tri, g_c, _NN)
        s_c = _sigmoid(ba_c)
        g_all.append(g_c)
        beta_cols += [col(s_c, h) for h in range(H)]
        G_cols += [col(G_c, H + h) for h in range(H)]
    wide = lambda c: jnp.broadcast_to(c, (B, C, 128))
    beta = wide(jnp.stack(beta_cols))
    Gc = jnp.stack(G_cols)
    rq = lax.rsqrt(_rowsum(qr * qr) + EPS)
    rk = lax.rsqrt(_rowsum(kr * kr) + EPS)
    qh, kn = qr * rq, kr * rk
    qn = qh * (HEAD_DIM ** -0.5)
    Grow = _dot01(jnp.ones((B, C, 128), f32), jnp.where(lane3 == 0, Gc, 0.0), _NT3)
    incl = ri >= ci
    Di = jnp.where(incl, jnp.exp(jnp.where(incl, Gc - Grow, 0.0)), 0.0)
    Ds = jnp.where(ri > ci, Di, 0.0)
    Gl = jnp.sum(jnp.where(rowc == C - 1, Gc, 0.0), axis=1, keepdims=True)
    eG = wide(jnp.exp(Gc))
    eGl = wide(jnp.exp(Gl - Gc))
    cd = jnp.exp(Gl)
    kb = kn * beta
    vb = v * beta
    Lm = _dot(kb, kn, _NT3) * Ds
    kbg = kb * eG
    QK = _dot(qn, kn, _NT3) * Di
    qg = qn * eG
    kg = kn * eGl
    return dict(beta=beta, g_all=g_all, rq=rq, rk=rk, qh=qh, kn=kn, qn=qn, Di=Di, Ds=Ds, eG=eG, eGl=eGl, cd=cd,
                kb=kb, vb=vb, Lm=Lm, kbg=kbg, QK=QK, qg=qg, kg=kg, lane=lane, ri=ri, ci=ci, rowc=rowc, ea=ea)


def _unit_lower_inverse(Lm):
    C = Lm.shape[-1]
    ri = lax.broadcasted_iota(jnp.int32, (C, C), 0)
    ci = lax.broadcasted_iota(jnp.int32, (C, C), 1)
    same = lambda s: (ri // s) == (ci // s)
    Xd = jnp.where(same(8), -Lm, 0.0)
    Tinv = (ri == ci).astype(f32) + Xd
    Pw = Xd
    for _ in range(2):
        Pw = _dot(Pw, Pw, _NN3)
        Tinv = Tinv + _dot(Tinv, Pw, _NN3)
    for s in (8, 16, 32):
        off = jnp.where(same(2 * s) & jnp.logical_not(same(s)), Lm, 0.0)
        Tinv = Tinv - _dot(_dot(Tinv, off, _NN3), Tinv, _NN3)
    return Tinv


def _gdn_specs(T, rev):
    C = GDN_CHUNK
    H = GDN_HEADS
    K = min(GDN_STEP, T // C)
    NS = T // (C * K)
    nn = (lambda n: NS - 1 - n) if rev else (lambda n: n)
    wide = lambda blk: pl.BlockSpec((K * C, H * HEAD_DIM), lambda n: (nn(n), blk))
    one = lambda off: pl.BlockSpec((K * C, HEAD_DIM), lambda n: (nn(n), off))
    vec = pl.BlockSpec((1, 128), lambda n: (0, 0))
    st = lambda rows: pl.BlockSpec((K, H, rows, rows), lambda n: (nn(n), 0, 0, 0))
    return K, NS, wide, one, vec, st


def _gdn_fwd(qkv, p, alog, dtb, nw, name):
    T = qkv.shape[0]
    C, H = GDN_CHUNK, GDN_HEADS
    N = T // C
    K, NS, wide, one, vec, st_spec = _gdn_specs(T, False)

    def body(q_ref, k_ref, v_ref, z_ref, ba_ref, al_ref, dt_ref, nw_ref, y_ref, sp_ref, ti_ref, vn_ref, o_ref, st):
        @pl.when(pl.program_id(0) == 0)
        def _():
            st[...] = jnp.zeros_like(st)

        f = _gdn_pre(_pairs(q_ref, K), _pairs(k_ref, K), _pairs(v_ref, K), ba_ref[...], al_ref[...], dt_ref[...])
        Tinv = _unit_lower_inverse(f["Lm"])
        ti_ref[...] = Tinv.reshape(K, H, C, C).astype(bf16)
        w = _dot(Tinv, f["kbg"], _NN3)
        u = _dot(Tinv, f["vb"], _NN3)
        S = st[...]
        vns, os_ = [], []
        for c in range(K):
            sl = slice(c * H, (c + 1) * H)
            sp_ref[c] = S
            vn_c = u[sl] - _dot(w[sl], S, _NN3)
            os_.append(_dot(f["qg"][sl], S, _NN3) + _dot(f["QK"][sl], vn_c, _NN3))
            S = S * f["cd"][sl] + _dot(f["kg"][sl], vn_c, _TN3)
            vns.append(vn_c)
        st[...] = S
        vn, o = jnp.concatenate(vns), jnp.concatenate(os_)
        r = lax.rsqrt(_rowsum(o * o) * (1.0 / HEAD_DIM) + EPS)
        _put_pairs(y_ref, o * r * nw_ref[...] * _silu(_pairs(z_ref, K)), K)
        _put_pairs(vn_ref, vn, K)
        _put_pairs(o_ref, o, K)

    wide_f32 = jax.ShapeDtypeStruct((T, H * HEAD_DIM), f32)
    return pl.pallas_call(
        body, grid=(NS,),
        in_specs=[wide(0), wide(1), wide(2), wide(3), one(4 * H), vec, vec, vec],
        out_specs=[wide(0), st_spec(HEAD_DIM), st_spec(C), wide(0), wide(0)],
        out_shape=[jax.ShapeDtypeStruct((T, H * HEAD_DIM), bf16), jax.ShapeDtypeStruct((N, H, HEAD_DIM, HEAD_DIM), f32),
                   jax.ShapeDtypeStruct((N, H, C, C), bf16), jax.ShapeDtypeStruct((T, H * HEAD_DIM), bf16), wide_f32],
        scratch_shapes=[pltpu.VMEM((H, HEAD_DIM, HEAD_DIM), f32)],
        compiler_params=_cparams(("arbitrary",)), name=name,
    )(qkv, qkv, qkv, p, p, alog, dtb, nw)


def _gdn_bwd(qkv, p, alog, dtb, nw, sprev, tinv, vn_all, o_all, dy_all, name):
    T = qkv.shape[0]
    C, H = GDN_CHUNK, GDN_HEADS
    N = T // C
    K, NS, wide, one, vec, st_spec = _gdn_specs(T, True)
    rs = lambda m: jnp.sum(m, axis=2, keepdims=True)

    def body(q_ref, k_ref, v_ref, z_ref, ba_ref, al_ref, dt_ref, nw_ref, sp_ref, ti_ref, vn_ref, o_ref, dy_ref,
             dqkv_ref, dz_ref, dba_ref, dal_ref, ddt_ref, dnw_ref, dst):
        @pl.when(pl.program_id(0) == 0)
        def _():
            dst[...] = jnp.zeros_like(dst)
            dal_ref[...] = jnp.zeros_like(dal_ref)
            ddt_ref[...] = jnp.zeros_like(ddt_ref)
            dnw_ref[...] = jnp.zeros_like(dnw_ref)

        ba, dtb_, nwv = ba_ref[...], dt_ref[...], nw_ref[...]
        v = _pairs(v_ref, K)
        f = _gdn_pre(_pairs(q_ref, K), _pairs(k_ref, K), v, ba, al_ref[...], dtb_)
        beta, kn, qn, kb, vb, kbg = f["beta"], f["kn"], f["qn"], f["kb"], f["vb"], f["kbg"]
        eG, eGl, cd, Di, Ds, QK, qg, kg = f["eG"], f["eGl"], f["cd"], f["Di"], f["Ds"], f["QK"], f["qg"], f["kg"]
        lane, ri, ci, rowc = f["lane"], f["ri"], f["ci"], f["rowc"]
        Tinv = ti_ref[...].reshape(K * H, C, C)
        S = sp_ref[...].reshape(K * H, HEAD_DIM, HEAD_DIM)
        w_ = _dot(Tinv, kbg, _NN3)
        vn, o = _pairs(vn_ref, K), _pairs(o_ref, K)
        z, dy = _pairs(z_ref, K), _pairs(dy_ref, K)
        r = lax.rsqrt(_rowsum(o * o) * (1.0 / HEAD_DIM) + EPS)
        nrm = o * r
        sz = _silu(z)
        dn = dy * nwv * sz
        _put_pairs(dz_ref, dy * nrm * nwv * _dsilu(z), K)
        dnw_ref[...] += jnp.sum(jnp.sum(dy * nrm * sz, axis=0), axis=0, keepdims=True)
        do = r * (dn - nrm * (_rowsum(dn * nrm) * (1.0 / HEAD_DIM)))
        dvn_do = _dot(QK, do, _TN3)
        dS_do = _dot(qg, do, _TN3)
        dqg = _dot(do, S, _NT3)
        dQK = _dot(do, vn, _NT3)
        dS = dst[...]
        dS1s, dvns = [None] * K, [None] * K
        for c in reversed(range(K)):
            sl = slice(c * H, (c + 1) * H)
            dS1s[c] = dS
            dvns[c] = _dot(kg[sl], dS, _NN3) + dvn_do[sl]
            dS = cd[sl] * dS + dS_do[sl] - _dot(w_[sl], dvns[c], _TN3)
        dst[...] = dS
        dS1, dvn = jnp.concatenate(dS1s), jnp.concatenate(dvns)
        dcd = jnp.sum(jnp.sum(S * dS1, axis=2, keepdims=True), axis=1, keepdims=True)
        dkg = _dot(vn, dS1, _NT3)
        dw = -_dot(dvn, S, _NT3)
        dqn = dqg * eG
        dkn = dkg * eGl
        deGl = rs(dkg * kn)
        dQKr = dQK * Di
        E = dQK * QK
        dqn = dqn + _dot(dQKr, kn, _NN3)
        dkn = dkn + _dot(dQKr, qn, _TN3)
        dT = _dot(dvn, vb, _NT3) + _dot(dw, kbg, _NT3)
        dvb = _dot(Tinv, dvn, _TN3)
        dkbg = _dot(Tinv, dw, _TN3)
        dkb = dkbg * eG
        deG = rs(dqg * qn + dkbg * kb)
        dL = -_dot(_dot(Tinv, dT, _TN3), Tinv, _NT3)
        dKK = dL * Ds
        E = E + dL * f["Lm"]
        dkb = dkb + _dot(dKK, kn, _NN3)
        dkn = dkn + _dot(dKK, kb, _TN3) + dkb * beta
        dbeta = rs(dkb * kn + dvb * v)
        _put_pairs(dqkv_ref, dvb * beta, K, 2 * H * HEAD_DIM)
        dG = rs(E) - rs(jnp.swapaxes(E, 1, 2)) + deG * eG - deGl * eGl
        dGl = jnp.sum(deGl * eGl, axis=1, keepdims=True) + dcd * cd
        dG = dG + jnp.where(rowc == C - 1, dGl, 0.0)
        qh = f["qh"]
        _put_pairs(dqkv_ref, (HEAD_DIM ** -0.5) * f["rq"] * (dqn - qh * _rowsum(dqn * qh)), K)
        _put_pairs(dqkv_ref, f["rk"] * (dkn - kn * _rowsum(dkn * kn)), K, H * HEAD_DIM)
        db = dbeta * beta * (1.0 - beta)
        triu = (ri <= ci).astype(f32)
        for c in range(K):
            db_all = jnp.where(lane == 0, db[c * H], 0.0)
            dG_all = jnp.where(lane == H, dG[c * H], 0.0)
            for h in range(1, H):
                db_all = db_all + jnp.where(lane == h, db[c * H + h], 0.0)
                dG_all = dG_all + jnp.where(lane == H + h, dG[c * H + h], 0.0)
            dg_all = _dot01(triu, dG_all, _NN)
            da_all = dg_all * (-f["ea"]) * _sigmoid(ba[c * C:(c + 1) * C] + dtb_)
            dba_ref[c * C:(c + 1) * C, :] = (db_all + da_all).astype(bf16)
            ddt_ref[...] += jnp.sum(da_all, axis=0, keepdims=True)
            dal_ref[...] += jnp.sum(dg_all * f["g_all"][c], axis=0, keepdims=True)

    small = jax.ShapeDtypeStruct((1, 128), f32)
    return pl.pallas_call(
        body, grid=(NS,),
        in_specs=[wide(0), wide(1), wide(2), wide(3), one(4 * H), vec, vec, vec, st_spec(HEAD_DIM), st_spec(C),
                  wide(0), wide(0), wide(0)],
        out_specs=[pl.BlockSpec((K * C, 3 * H * HEAD_DIM), lambda n: (NS - 1 - n, 0)), wide(3), one(0), vec, vec, vec],
        out_shape=[jax.ShapeDtypeStruct((T, 3 * H * HEAD_DIM), f32), jax.ShapeDtypeStruct((T, ODD_PAD), bf16),
                   jax.ShapeDtypeStruct((T, 128), bf16), small, small, small],
        scratch_shapes=[pltpu.VMEM((H, HEAD_DIM, HEAD_DIM), f32)],
        compiler_params=_cparams(("arbitrary",)), name=name,
    )(qkv, qkv, qkv, p, p, alog, dtb, nw, sprev, tinv, vn_all, o_all, dy_all)


def _lanes_from(x, s):
    return x if s % 128 == 0 else pltpu.roll(x, (128 - s) % 128, 1)


def _odd_assemble(g, name):
    R = g.shape[1]
    tr = min(256, R)
    n_blk = ODD_SHARD_PAD // 128

    def body(g_ref, o_ref):
        lane = lax.broadcasted_iota(jnp.int32, (tr, 128), 1)
        blk = lambda d, m: g_ref[d, :, m * 128:(m + 1) * 128]
        for gb in range(ODD_PAD // 128):
            c0 = 128 * gb
            if c0 >= ODD_IN:
                o_ref[:, c0:c0 + 128] = jnp.zeros((tr, 128), g.dtype)
                continue
            d0 = c0 // ODD_SHARD
            m0, sh = divmod(c0 - ODD_SHARD * d0, 128)
            take = min(128, ODD_SHARD * (d0 + 1) - c0)
            p = _lanes_from(blk(d0, m0), sh)
            if sh and m0 + 1 < n_blk:
                p = jnp.where(lane < 128 - sh, p, _lanes_from(blk(d0, m0 + 1), sh))
            if take < 128:
                nxt = pltpu.roll(blk(d0 + 1, 0), take, 1) if d0 + 1 < N_DEV else jnp.zeros((tr, 128), g.dtype)
                p = jnp.where(lane < take, p, nxt)
            o_ref[:, c0:c0 + 128] = p

    return pl.pallas_call(
        body, grid=(R // tr,),
        in_specs=[pl.BlockSpec((N_DEV, tr, ODD_SHARD_PAD), lambda i: (0, i, 0))],
        out_specs=pl.BlockSpec((tr, ODD_PAD), lambda i: (i, 0)),
        out_shape=jax.ShapeDtypeStruct((R, ODD_PAD), g.dtype),
        compiler_params=_cparams(("parallel",)), name=name,
    )(g)


def _odd_split(w, name):
    R = w.shape[0]
    tr = min(256, R)

    def body(w_ref, o_ref):
        lane = lax.broadcasted_iota(jnp.int32, (tr, 128), 1)
        blk = lambda gb: w_ref[:, gb * 128:(gb + 1) * 128]
        for d in range(N_DEV):
            for m in range(ODD_SHARD_PAD // 128):
                g0, sh = divmod(ODD_SHARD * d + 128 * m, 128)
                p = _lanes_from(blk(g0), sh)
                if sh and g0 + 1 < ODD_PAD // 128:
                    p = jnp.where(lane < 128 - sh, p, _lanes_from(blk(g0 + 1), sh))
                real = ODD_SHARD - 128 * m
                if real < 128:
                    p = jnp.where(lane < real, p, jnp.zeros_like(p))
                o_ref[d, :, m * 128:(m + 1) * 128] = p

    return pl.pallas_call(
        body, grid=(R // tr,),
        in_specs=[pl.BlockSpec((tr, ODD_PAD), lambda i: (i, 0))],
        out_specs=pl.BlockSpec((N_DEV, tr, ODD_SHARD_PAD), lambda i: (0, i, 0)),
        out_shape=jax.ShapeDtypeStruct((N_DEV, R, ODD_SHARD_PAD), w.dtype),
        compiler_params=_cparams(("parallel",)), name=name,
    )(w)


def _adam_tile(g, w_ref, m_ref, v_ref, go_ref, d_ref, mo_ref, vo_ref):
    c1 = 1.0 - ADAM_B1 ** ADAM_STEP
    c2 = 1.0 - ADAM_B2 ** ADAM_STEP
    mn = ADAM_B1 * m_ref[...] + (1.0 - ADAM_B1) * g
    vn = ADAM_B2 * v_ref[...] + (1.0 - ADAM_B2) * (g * g)
    go_ref[...] = g
    mo_ref[...] = mn
    vo_ref[...] = vn
    d_ref[...] = -ADAM_LR * ((mn / c1) / (jnp.sqrt(vn / c2) + ADAM_EPS) + ADAM_WD * w_ref[...])


def _adamw(w, gs, m, v, name, layer=None, prev=None):
    R, Cc = w.shape[-2:]
    S = gs.shape[0]
    tr = R
    if S * R * Cc * 4 > (4 << 20):
        for cand in (256, 128, 64, 32, 16, 8):
            if R % cand == 0 and R > cand:
                tr = cand
                break

    def body(w_ref, g_ref, m_ref, v_ref, *rest):
        g = g_ref[0].astype(f32)
        for s in range(1, S):
            g = g + g_ref[s].astype(f32)
        _adam_tile(g, w_ref, m_ref, v_ref, *rest[-4:])

    if layer is None:
        blk = pl.BlockSpec((tr, Cc), lambda i: (i, 0))
    else:
        blk = pl.BlockSpec((None, tr, Cc), lambda i: (layer, i, 0))
    out = jax.ShapeDtypeStruct(w.shape, f32)
    carried = [] if prev is None else list(prev)
    return pl.pallas_call(
        body, grid=(R // tr,),
        in_specs=[blk, pl.BlockSpec((S, tr, Cc), lambda i: (0, i, 0)), blk, blk]
        + [pl.BlockSpec(memory_space=pl.ANY)] * len(carried),
        out_specs=[blk] * 4, out_shape=[out] * 4,
        input_output_aliases={4 + j: j for j in range(len(carried))},
        compiler_params=_cparams(("parallel",)), name=name,
    )(w, gs, m, v, *carried)


def _adamw_column_major(w, gs, m, v, name):
    _, R, Cc = w.shape
    S = gs.shape[0]
    q = R // 128
    dense = lambda a: jnp.transpose(a, (2, 0, 1)).reshape(Cc * q, 128)
    back = lambda a: jnp.transpose(a.reshape(Cc, q, 128), (1, 2, 0)).reshape(1, R, Cc)

    def body(w_ref, g_ref, m_ref, v_ref, go_ref, d_ref, mo_ref, vo_ref, gt):
        for i in range(q):
            g = g_ref[0, i * 128:(i + 1) * 128, :].astype(f32)
            for s in range(1, S):
                g = g + g_ref[s, i * 128:(i + 1) * 128, :].astype(f32)
            gt[pl.ds(i, 128, stride=q), :] = g.T
        _adam_tile(gt[...], w_ref, m_ref, v_ref, go_ref, d_ref, mo_ref, vo_ref)

    blk = pl.BlockSpec((128 * q, 128), lambda j: (j, 0))
    outs = pl.pallas_call(
        body, grid=(pl.cdiv(Cc, 128),),
        in_specs=[blk, pl.BlockSpec((S, R, 128), lambda j: (0, 0, j)), blk, blk],
        out_specs=[blk] * 4, out_shape=[jax.ShapeDtypeStruct((Cc * q, 128), f32)] * 4,
        scratch_shapes=[pltpu.VMEM((128 * q, 128), f32)],
        compiler_params=_cparams(("parallel",)), name=name,
    )(dense(w), gs, dense(m), dense(v))
    return [back(o) for o in outs]


def _me():
    x, y, c = lax.axis_index("x"), lax.axis_index("y"), lax.axis_index("c")
    return x, y, c, 4 * x + 2 * y + c


def _peer(k):
    x, y, c, _ = _me()
    px = 1 - x if k & 4 else x
    py = 1 - y if k & 2 else y
    pc = 1 - c if k & 1 else c
    return (px, py, pc), 4 * px + 2 * py + pc


_HBM = pl.BlockSpec(memory_space=pltpu.HBM)
_SEM = pl.BlockSpec(memory_space=pltpu.SEMAPHORE)
_EFFECT = pltpu.SideEffectType.DATAFLOW_SIDE_EFFECTING


def _copy(src, land, ssem, rsem, k, blocked, landing_slot_of_peer):
    pid, pidx = _peer(k)
    slot = pidx if landing_slot_of_peer else _me()[3]
    return pltpu.make_async_remote_copy(src_ref=src.at[pidx] if blocked else src, dst_ref=land.at[slot],
                                        send_sem=ssem.at[k - 1], recv_sem=rsem.at[k - 1], device_id=pid, device_id_type=MESH)


def _own_copy(src, land, rsem, blocked):
    me = _me()[3]
    return pltpu.make_async_copy(src.at[me] if blocked else src, land.at[me], rsem.at[N_DEV - 1])


_ALL_PEERS = tuple(range(1, N_DEV))
_SAME_CORE_PEERS = (1, 2, 4, 6)
_OTHER_CORE_PEERS = (3, 5, 7)


def _relay_copies(land, ssem, rsem):
    sibling, _ = _peer(1)
    copies = []
    for i in range(len(land)):
        for j, k in enumerate(_OTHER_CORE_PEERS):
            sems = dict(send_sem=ssem.at[3 * i + j], recv_sem=rsem.at[3 * i + j], device_id=sibling, device_id_type=MESH)
            _, outgoing = _peer(k - 1)
            _, incoming = _peer(k)
            copies.append((pltpu.make_async_remote_copy(src_ref=land[i].at[outgoing], dst_ref=land[i].at[outgoing], **sems),
                           pltpu.make_async_remote_copy(src_ref=land[i].at[incoming], dst_ref=land[i].at[incoming], **sems)))
    return copies


def _relay_start(lands, name):
    n = len(lands)

    def body(*refs):
        for send, _ in _relay_copies(refs[:n], refs[n], refs[n + 1]):
            send.start()
        refs[-1][...] = jnp.zeros_like(refs[-1])

    sem = pltpu.SemaphoreType.DMA((3 * n,))
    res = pl.pallas_call(
        body, name=name,
        out_shape=(sem, sem) + tuple(pltpu.HBM(a.shape, a.dtype) for a in lands) + (jax.ShapeDtypeStruct((8, 128), f32),),
        in_specs=(_HBM,) * n, out_specs=(_SEM, _SEM) + (_HBM,) * n + (pl.BlockSpec(memory_space=pltpu.VMEM),),
        input_output_aliases={i: 2 + i for i in range(n)},
        compiler_params=pltpu.CompilerParams(has_side_effects=_EFFECT),
    )(*lands)
    return res[:-1], res[-1]


def _relay_wait(handle, after, name):
    ssem, rsem, lands = handle[0], handle[1], handle[2:]
    n = len(lands)
    after = tuple(after) if isinstance(after, (tuple, list)) else (after,)

    def body(*refs):
        for send, recv in _relay_copies(refs[:n], refs[n], refs[n + 1]):
            send.wait_send()
            recv.wait_recv()

    return pl.pallas_call(
        body, name=name, out_shape=tuple(pltpu.HBM(a.shape, a.dtype) for a in lands),
        in_specs=(_HBM,) * n + (_SEM, _SEM) + (pl.BlockSpec(memory_space=pl.ANY),) * len(after), out_specs=(_HBM,) * n,
        input_output_aliases={i: i for i in range(n)}, compiler_params=pltpu.CompilerParams(has_side_effects=_EFFECT),
    )(*lands, ssem, rsem, *after)


def _send_start(srcs, blocked, name, relayed=()):
    n = len(srcs)
    lands = [lax.empty(a.shape if blocked else (N_DEV,) + a.shape, a.dtype) for a in srcs]

    def body(*refs):
        src, land, sems, token = refs[:n], refs[n:2 * n], refs[2 * n:4 * n], refs[-1]
        for i in range(n):
            for k in (_SAME_CORE_PEERS if i in relayed else _ALL_PEERS):
                _copy(src[i], land[i], sems[2 * i], sems[2 * i + 1], k, blocked, False).start()
        for i in range(n):
            _own_copy(src[i], land[i], sems[2 * i + 1], blocked).start()
        token[...] = jnp.zeros_like(token)

    sems = (pltpu.SemaphoreType.DMA((N_DEV - 1,)), pltpu.SemaphoreType.DMA((N_DEV,)))
    hbm = lambda a: pltpu.with_memory_space_constraint(a, pltpu.HBM)
    res = pl.pallas_call(
        body, name=name,
        out_shape=sems * n + tuple(pltpu.HBM(a.shape, a.dtype) for a in srcs + lands)
        + (jax.ShapeDtypeStruct((8, 128), f32),),
        in_specs=(_HBM,) * (2 * n),
        out_specs=(_SEM,) * (2 * n) + (_HBM,) * (2 * n) + (pl.BlockSpec(memory_space=pltpu.VMEM),),
        input_output_aliases={j: 2 * n + j for j in range(2 * n)},
        compiler_params=pltpu.CompilerParams(has_side_effects=_EFFECT),
    )(*[hbm(a) for a in srcs], *[hbm(a) for a in lands])
    handles = [(res[2 * i], res[2 * i + 1], res[2 * n + i], res[3 * n + i]) for i in range(n)]
    return handles, res[-1]


def _send_wait(handle, blocked, after, name, relayed=False):
    ssem, rsem, src, land = handle
    after = tuple(after) if isinstance(after, (tuple, list)) else (after,)

    def body(src_ref, land_ref, ssem_ref, rsem_ref, *rest):
        for k in (_SAME_CORE_PEERS if relayed else _ALL_PEERS):
            cp = _copy(src_ref, land_ref, ssem_ref, rsem_ref, k, blocked, True)
            cp.wait_send()
            cp.wait_recv()
        _own_copy(src_ref, land_ref, rsem_ref, blocked).wait()

    return pl.pallas_call(
        body, name=name, out_shape=(pltpu.HBM(src.shape, src.dtype), pltpu.HBM(land.shape, land.dtype)),
        in_specs=(_HBM, _HBM, _SEM, _SEM) + (pl.BlockSpec(memory_space=pl.ANY),) * len(after), out_specs=(_HBM, _HBM),
        input_output_aliases={0: 0, 1: 1}, compiler_params=pltpu.CompilerParams(has_side_effects=_EFFECT),
    )(src, land, ssem, rsem, *after)


def _block_diag(w):
    nb, bs = w.shape[0], w.shape[1]
    eye = jnp.eye(nb, dtype=w.dtype)
    return (eye[:, None, :, None] * w[:, :, None, :]).reshape(nb * bs, nb * bs)


_SQUARE_TILES = dict(tm=1024, tn=1024, tk=1024)


def _mlp_fwd(x, hm, wu, wd, tag, epilogue, extras, outs, between=None):
    (r,) = _matmul(hm, wu, "nn", outs=[bf16], epilogue=lambda acc: (jnp.maximum(acc, 0.0),), name=f"mlp_up_{tag}")
    res = _matmul(r, wd, "nn", outs=outs, extras=(x,) + tuple(extras), epilogue=epilogue, a_map=jnp.square,
                  after=() if between is None else (between(r),), name=f"mlp_down_{tag}", **_SQUARE_TILES)
    return res, (hm, r)


def _mlp_bwd(x, nw, wu, wd, saved, dxo, dxo_b, tag, sink, urgent):
    hm, r = saved
    (du,) = _matmul(dxo_b, wd, "nt", outs=[bf16], extras=(r,), epilogue=lambda acc, rr: (acc * (2.0 * rr.astype(f32)),),
                    name=f"mlp_dact_{tag}")
    (dwd,) = _matmul(r, dxo_b, "tn", outs=[bf16], a_map=jnp.square, name=f"mlp_dwd_{tag}", **_SQUARE_TILES)
    down = {f"w_down{tag}": dwd.reshape(N_DEV, D_FF // N_DEV, D_MODEL)}
    if urgent:
        tok = sink(down)
        (dwu,) = _matmul(hm, du, "tn", outs=[bf16], shard_cols=2, after=(tok,), name=f"mlp_dwu_{tag}")
        tok = sink({f"w_up{tag}": dwu})
    else:
        (dwu,) = _matmul(hm, du, "tn", outs=[bf16], shard_cols=2, name=f"mlp_dwu_{tag}")
        tok = sink({f"w_up{tag}": dwu, **down})
    return _matmul(du, wu, "nt", outs=_RMS_BWD_OUTS, extras=(x, dxo, nw), epilogue=_rms_bwd_ep, after=(tok,),
                   name=f"mlp_dh_{tag}", **_SQUARE_TILES)


def _local_step(x, tgt, P, weight, sink, ahead):
    T = x.shape[0]
    cos, sin = _rope_tables(T)
    rtab = _ret_tables()
    row = lambda a: a.reshape(1, -1)
    mix_nw, mlp_nw = P["mixer_norm_w"], P["mlp_norm_w"]
    wr_bd, wi_bd = _block_diag(P["lru_w_r"]), _block_diag(P["lru_w_i"])
    lru_b, lru_br, lru_bi, lru_lam = row(P["lru_conv_b"]), row(P["lru_b_r"]), row(P["lru_b_i"]), row(P["lru_lambda"])
    pad16 = lambda a: jnp.pad(a.reshape(1, GDN_HEADS), ((0, 0), (GDN_HEADS, 128 - 2 * GDN_HEADS)))
    alog, dtb = pad16(P["gdn_a_log"]), pad16(P["gdn_dt_bias"])
    gnw = row(P["gdn_norm_w"])

    x0 = x
    h0 = _rms_fwd(x0, mix_nw[0:1], "rms_mix_0")
    w_ie = weight("w_in_even", (h0, wr_bd, wi_bd))
    (pe,) = _matmul(h0, w_ie, "nn", outs=[f32], name="in_even")
    mix0, o_ret, s_ret = _ret_fwd(pe, cos, sin, rtab, "ret_fwd")
    w_lc = weight("lru_conv_w", pe)
    xc = _conv_fwd(pe, 4, w_lc, lru_b, False, "lru_conv_fwd")
    mix0, h_lru = _lru_fwd(xc, pe, 20, wr_bd, wi_bd, lru_br, lru_bi, lru_lam, mix0, "lru_fwd")
    w_oe = weight("w_out_even", mix0)
    tok = ahead(("w_up0", "w_down0"), mix0)
    x1, hm0 = _matmul(mix0, w_oe, "nn", outs=[f32, bf16], extras=(x0, mlp_nw[0:1]), epilogue=_residual_rms_ep,
                      after=(tok,), name="out_even", tm=1024, tn=D_MODEL)
    w_u0, w_d0 = weight("w_up0", x1), weight("w_down0", x1)
    (x2, h1), mlp0 = _mlp_fwd(x1, hm0, w_u0, w_d0, "0", _residual_rms_ep, (mix_nw[1:2],), [f32, bf16],
                              between=lambda r: ahead(("w_in_odd",), r))
    w_io = weight("w_in_odd", h1)
    (po,) = _matmul(h1, w_io, "nn", outs=[f32], tm=2048, tn=ODD_PAD // 3, name="in_odd")
    w_gc = weight("gdn_conv_w", po)
    qkv = _conv_fwd(po, 0, w_gc, None, True, "gdn_conv_fwd")
    y_gdn, s_gdn, ti_gdn, vn_gdn, o_gdn = _gdn_fwd(qkv, po, alog, dtb, gnw, "gdn_fwd")
    w_oo = weight("w_out_odd", y_gdn)
    x3, hm1 = _matmul(y_gdn, w_oo, "nn", outs=[f32, bf16], extras=(x2, mlp_nw[1:2]), epilogue=_residual_rms_ep,
                      name="out_odd", tm=1024, tn=D_MODEL)
    w_u1, w_d1 = weight("w_up1", x3), weight("w_down1", x3)
    (loss, dx4, dx4_b, d_final), mlp1 = _mlp_fwd(x3, hm1, w_u1, w_d1, "1", _loss_ep, (row(P["final_norm_w"]), tgt),
                                                 _LOSS_OUTS)
    dx3, dx3_b, d_mlp_nw1 = _mlp_bwd(x3, mlp_nw[1:2], w_u1, w_d1, mlp1, dx4, dx4_b, "1", sink, False)
    (dy_gdn,) = _matmul(dx3_b, w_oo, "nt", outs=[f32], name="out_odd_dx")
    (d_woo,) = _matmul(y_gdn, dx3_b, "tn", outs=[bf16], name="out_odd_dw")
    dqkv, dpo, dba, d_alog, d_dtb, d_gnw = _gdn_bwd(qkv, po, alog, dtb, gnw, s_gdn, ti_gdn, vn_gdn, o_gdn, dy_gdn,
                                                  "gdn_bwd")
    dpo, d_gconv, _ = _conv_bwd(po, 0, w_gc, None, True, dqkv, dpo, "gdn_conv_bwd")
    dpo = lax.dynamic_update_slice(dpo, dba, (0, 4 * D_MODEL))
    (d_wio,) = _matmul(h1, dpo, "tn", outs=[bf16], tn=ODD_PAD // 3, name="in_odd_dw")
    tok = sink(dict(w_out_odd=d_woo.reshape(N_DEV, D_MODEL // N_DEV, D_MODEL), w_in_odd=_odd_split(d_wio, "w_in_odd_split")))
    dx2, dx2_b, d_mix_nw1 = _matmul(dpo, w_io, "nt", outs=_RMS_BWD_OUTS, extras=(x2, dx3, mix_nw[1:2]),
                                    epilogue=_rms_bwd_ep, after=(tok,), tm=1024, tn=1024, tk=ODD_PAD // 3, name="in_odd_dx")
    dx1, dx1_b, d_mlp_nw0 = _mlp_bwd(x1, mlp_nw[0:1], w_u0, w_d0, mlp0, dx2, dx2_b, "0", sink, True)
    (d_woe,) = _matmul(mix0, dx1_b, "tn", outs=[bf16], name="out_even_dw")
    tok = sink(dict(w_out_even=d_woe.reshape(N_DEV, D_MODEL // N_DEV, D_MODEL)))
    (dmix0,) = _matmul(dx1_b, w_oe, "nt", outs=[f32], name="out_even_dx")
    dpe = _ret_bwd(pe, cos, sin, rtab, o_ret, s_ret, dmix0, "ret_bwd")
    dxc, dpe, d_wr, d_wi, d_br, d_bi, d_lam = _lru_bwd(xc, pe, 20, wr_bd, wi_bd, lru_br, lru_bi, lru_lam + tok[0:1, 0:1],
                                                       h_lru, dmix0, 4, dpe, "lru_bwd")
    dpe, d_lconv, d_lconv_b = _conv_bwd(pe, 4, w_lc, lru_b, False, dxc, dpe, "lru_conv_bwd")
    G = dict(
        mlp_norm_w=jnp.concatenate([d_mlp_nw0, d_mlp_nw1], axis=0),
        final_norm_w=d_final.reshape(-1),
        lru_conv_w=d_lconv, lru_conv_b=d_lconv_b.reshape(-1),
        lru_w_r=d_wr, lru_b_r=d_br.reshape(-1), lru_w_i=d_wi, lru_b_i=d_bi.reshape(-1),
        lru_lambda=d_lam.reshape(-1), gdn_conv_w=d_gconv,
        gdn_a_log=d_alog[0, GDN_HEADS:2 * GDN_HEADS], gdn_dt_bias=d_dtb[0, GDN_HEADS:2 * GDN_HEADS],
        gdn_norm_w=d_gnw.reshape(-1),
    )
    packed = _pack([G[k] for k in _SMALL] + [d_lconv, d_gconv, loss[0, 0:1]])
    tok = sink(dict(small=jnp.broadcast_to(packed[None], (N_DEV,) + packed.shape)))
    (d_wie,) = _matmul(h0, dpe, "tn", outs=[bf16], shard_cols=2, after=(tok,), name="in_even_dw")
    tok = sink(dict(w_in_even=d_wie))
    dx0, _, d_mix_nw0 = _matmul(dpe, w_ie, "nt", outs=_RMS_BWD_OUTS, extras=(x0, dx1, mix_nw[0:1]), epilogue=_rms_bwd_ep,
                                after=(tok,), name="in_even_dx", **_SQUARE_TILES)
    G["mixer_norm_w"] = jnp.concatenate([d_mix_nw0, d_mix_nw1], axis=0)
    return loss, dx0, G


_SMALL = ["mlp_norm_w", "final_norm_w", "lru_conv_b", "lru_w_r", "lru_b_r", "lru_w_i", "lru_b_i",
          "lru_lambda", "gdn_a_log", "gdn_dt_bias", "gdn_norm_w"]
_PACK_ROWS = 688


def _pack(parts):
    flat = jnp.concatenate([p.reshape(-1) for p in parts])
    return jnp.pad(flat, (0, _PACK_ROWS * 128 - flat.shape[0])).reshape(_PACK_ROWS, 128)


def _unpack(packed, shapes):
    flat = packed.reshape(-1)
    out, off = [], 0
    for s in shapes:
        n = int(np.prod(s))
        out.append(flat[off:off + n].reshape(s))
        off += n
    return out


def kernel(x, mixer_norm_w, mlp_norm_w, final_norm_w, w_in_even, lru_conv_w, lru_conv_b, lru_w_r, lru_b_r, lru_w_i, lru_b_i, lru_lambda, w_out_even, w_in_odd, gdn_conv_w, gdn_a_log, gdn_dt_bias, gdn_norm_w, w_out_odd, w_up, w_down, loss_target, m_mixer_norm_w, m_mlp_norm_w, m_final_norm_w, m_w_in_even, m_lru_conv_w, m_lru_conv_b, m_lru_w_r, m_lru_b_r, m_lru_w_i, m_lru_b_i, m_lru_lambda, m_w_out_even, m_w_in_odd, m_gdn_conv_w, m_gdn_a_log, m_gdn_dt_bias, m_gdn_norm_w, m_w_out_odd, m_w_up, m_w_down, v_mixer_norm_w, v_mlp_norm_w, v_final_norm_w, v_w_in_even, v_lru_conv_w, v_lru_conv_b, v_lru_w_r, v_lru_b_r, v_lru_w_i, v_lru_b_i, v_lru_lambda, v_w_out_even, v_w_in_odd, v_gdn_conv_w, v_gdn_a_log, v_gdn_dt_bias, v_gdn_norm_w, v_w_out_odd, v_w_up, v_w_down):
    Pw = dict(mixer_norm_w=mixer_norm_w, mlp_norm_w=mlp_norm_w, final_norm_w=final_norm_w, w_in_even=w_in_even,
              lru_conv_w=lru_conv_w, lru_conv_b=lru_conv_b, lru_w_r=lru_w_r, lru_b_r=lru_b_r, lru_w_i=lru_w_i,
              lru_b_i=lru_b_i, lru_lambda=lru_lambda, w_out_even=w_out_even, w_in_odd=w_in_odd, gdn_conv_w=gdn_conv_w,
              gdn_a_log=gdn_a_log, gdn_dt_bias=gdn_dt_bias, gdn_norm_w=gdn_norm_w, w_out_odd=w_out_odd, w_up=w_up,
              w_down=w_down)
    Pm = dict(mixer_norm_w=m_mixer_norm_w, mlp_norm_w=m_mlp_norm_w, final_norm_w=m_final_norm_w, w_in_even=m_w_in_even,
              lru_conv_w=m_lru_conv_w, lru_conv_b=m_lru_conv_b, lru_w_r=m_lru_w_r, lru_b_r=m_lru_b_r, lru_w_i=m_lru_w_i,
              lru_b_i=m_lru_b_i, lru_lambda=m_lru_lambda, w_out_even=m_w_out_even, w_in_odd=m_w_in_odd,
              gdn_conv_w=m_gdn_conv_w, gdn_a_log=m_gdn_a_log, gdn_dt_bias=m_gdn_dt_bias, gdn_norm_w=m_gdn_norm_w,
              w_out_odd=m_w_out_odd, w_up=m_w_up, w_down=m_w_down)
    Pv = dict(mixer_norm_w=v_mixer_norm_w, mlp_norm_w=v_mlp_norm_w, final_norm_w=v_final_norm_w, w_in_even=v_w_in_even,
              lru_conv_w=v_lru_conv_w, lru_conv_b=v_lru_conv_b, lru_w_r=v_lru_w_r, lru_b_r=v_lru_b_r, lru_w_i=v_lru_w_i,
              lru_b_i=v_lru_b_i, lru_lambda=v_lru_lambda, w_out_even=v_w_out_even, w_in_odd=v_w_in_odd,
              gdn_conv_w=v_gdn_conv_w, gdn_a_log=v_gdn_a_log, gdn_dt_bias=v_gdn_dt_bias, gdn_norm_w=v_gdn_norm_w,
              w_out_odd=v_w_out_odd, w_up=v_w_up, w_down=v_w_down)
    me = _me()[3]
    T = x.shape[1]

    cols = lambda g: jnp.transpose(g, (1, 0, 2)).reshape(g.shape[1], -1)
    rows = lambda g: g.reshape(-1, g.shape[2])
    wide = lambda g: _odd_assemble(g, "w_in_odd_assemble")
    as_is = lambda g: g
    relay_groups = (("w_in_even",), ("w_up0", "w_down0"), ("w_in_odd",))
    relayed = sum(relay_groups, ())
    (first,), started = _send_start([w_in_even[0].astype(bf16)], False, "gather_start_w_in_even", relayed=[0])
    cast = lambda a: (a + started[0:1, 0:1]).astype(bf16)
    odd_shard = jnp.pad(cast(w_in_odd[0]), ((0, 0), (0, ODD_SHARD_PAD - ODD_SHARD)))
    gather = dict(
        w_in_even=(None, cols), lru_conv_w=(lru_conv_w[0], cols),
        w_out_even=(cast(w_out_even[0]), rows), w_up0=(cast(w_up[0]), as_is), w_down0=(cast(w_down[0]), rows),
        w_in_odd=(odd_shard, wide), gdn_conv_w=(gdn_conv_w[0], cols),
        w_out_odd=(cast(w_out_odd[0]), rows), w_up1=(cast(w_up[1]), as_is), w_down1=(cast(w_down[1]), rows))
    later = [name for name in gather if name != "w_in_even"]
    handles, tok = _send_start([gather[name][0] for name in later], False, "gather_start",
                               relayed=[i for i, name in enumerate(later) if name in relayed])
    handles = dict(zip(later, handles), w_in_even=first)
    relays, landed, full = {}, {}, {}

    def ahead(group, after):
        lands = [_send_wait(handles[n], False, after, f"gather_wait_{n}", relayed=True)[1] for n in group]
        relays[group], token = _relay_start(lands, "relay_start_" + "_".join(group))
        return token

    def weight(name, after):
        if name not in landed and name in relayed:
            group = next(g for g in relay_groups if name in g)
            if group not in relays:
                ahead(group, after)
            landed.update(zip(group, _relay_wait(relays[group], after, "relay_wait_" + "_".join(group))))
        elif name not in landed:
            landed[name] = _send_wait(handles[name], False, after, f"gather_wait_{name}")[1]
        if name not in full:
            full[name] = gather[name][1](landed[name])
        return full[name]

    P = {k: Pw[k] for k in ("mlp_norm_w", "final_norm_w")}
    P["mixer_norm_w"] = mixer_norm_w + tok[0:1, 0:1]
    for k in ("lru_w_r", "lru_w_i", "lru_conv_b", "lru_b_r", "lru_b_i", "lru_lambda", "gdn_a_log", "gdn_dt_bias", "gdn_norm_w"):
        P[k] = Pw[k][0]

    sent = {}

    def sink(grads):
        hs, token = _send_start(list(grads.values()), True, "grads_start_" + "_".join(grads))
        sent.update(zip(grads, hs))
        return token

    loss, dx, G = _local_step(x[0], loss_target[0], P, weight, sink, ahead)
    lanes = lambda a: a.reshape(-1, 128)
    sink(dict(mixer_norm_w=jnp.broadcast_to(lanes(G["mixer_norm_w"])[None], (N_DEV, 2 * D_MODEL // 128, 128))))

    def received(name, after=dx):
        return _send_wait(sent[name], True, after, f"grads_wait_{name}")[1]

    out = {}
    nff = D_FF // N_DEV

    def whole(name, gs):
        out[name] = tuple(_adamw(Pw[name], gs, Pm[name], Pv[name], f"adamw_{name}", layer=0))

    def layers(name):
        res = None
        for l in range(2):
            res = _adamw(Pw[name], received(f"{name}{l}"), Pm[name], Pv[name], f"adamw_{name}{l}", layer=l, prev=res)
        out[name] = tuple(res)

    layers("w_up")
    layers("w_down")
    whole("w_out_odd", received("w_out_odd"))
    out["w_in_odd"] = tuple(_adamw_column_major(w_in_odd, received("w_in_odd"), m_w_in_odd, v_w_in_odd, "adamw_w_in_odd"))
    whole("w_out_even", received("w_out_even"))
    small_shapes = [Pw[k].shape for k in _SMALL]
    pw, pm, pv = (_pack([Q[k] for k in _SMALL]) for Q in (Pw, Pm, Pv))
    sg, sd, sm, sv = _adamw(pw, received("small", out["w_out_even"][1]), pm, pv, "adamw_small")
    for arrs_i, packed_out in enumerate((sg, sd, sm, sv)):
        for k, a in zip(_SMALL, _unpack(packed_out, small_shapes)):
            out.setdefault(k, [None] * 4)[arrs_i] = a
    whole("w_in_even", received("w_in_even", sd))
    out["mixer_norm_w"] = tuple(
        a.reshape(mixer_norm_w.shape) for a in
        _adamw(lanes(mixer_norm_w), received("mixer_norm_w", out["w_in_even"][1]), lanes(m_mixer_norm_w),
               lanes(v_mixer_norm_w), "adamw_mixer_norm_w"))
    n_small = sum(int(np.prod(s)) for s in small_shapes)
    gflat = sg.reshape(-1)
    g_lconv = gflat[n_small:n_small + CONV_K * LRU_WIDTH].reshape(CONV_K, LRU_WIDTH)
    g_gconv = gflat[n_small + CONV_K * LRU_WIDTH:n_small + CONV_K * (LRU_WIDTH + 3072)].reshape(CONV_K, 3072)
    whole("lru_conv_w", lax.dynamic_slice_in_dim(g_lconv, me * 64, 64, axis=1)[None])
    whole("gdn_conv_w", lax.dynamic_slice_in_dim(g_gconv, me * 384, 384, axis=1)[None])

    names = ["mixer_norm_w", "mlp_norm_w", "final_norm_w", "w_in_even", "lru_conv_w", "lru_conv_b", "lru_w_r", "lru_b_r",
             "lru_w_i", "lru_b_i", "lru_lambda", "w_out_even", "w_in_odd", "gdn_conv_w", "gdn_a_log", "gdn_dt_bias",
             "gdn_norm_w", "w_out_odd", "w_up", "w_down"]
    total = gflat[n_small + CONV_K * (LRU_WIDTH + 3072)]
    res = [total, dx[None]]
    for j in range(4):
        res += [out[k][j] for k in names]
    return tuple(res)
```

```python
import math

import numpy as np
import jax
import jax.numpy as jnp
from jax import lax
from jax.experimental import pallas as pl
from jax.experimental.pallas import tpu as pltpu

f32 = jnp.float32
bf16 = jnp.bfloat16

N_DEV = 8
D_MODEL = 1024
D_FF = 4096
EPS = 1e-6
RET_HEADS = 4
RET_CHUNK = 128
RET_STEP = 8
ROPE_THETA = 10000.0
LRU_WIDTH = 512
LRU_C = 8.0
GDN_HEADS = 8
GDN_CHUNK = 64
GDN_STEP = 4
HEAD_DIM = 128
ODD_IN = 4112
ODD_PAD = 4224
ODD_SHARD = ODD_IN // N_DEV
ODD_SHARD_PAD = 640
ADAM_LR, ADAM_B1, ADAM_B2, ADAM_EPS, ADAM_WD, ADAM_STEP = 0.001, 0.9, 0.999, 1e-08, 0.01, 10
VMEM_LIMIT = 56 * 1024 * 1024

_NN = (((1,), (0,)), ((), ()))
_NT = (((1,), (1,)), ((), ()))
_TN = (((0,), (0,)), ((), ()))
MESH = pl.DeviceIdType.MESH


def _cparams(sem):
    return pltpu.CompilerParams(dimension_semantics=sem, vmem_limit_bytes=VMEM_LIMIT)


def _dot(a, b, dn):
    return lax.dot_general(a.astype(bf16), b.astype(bf16), dn, preferred_element_type=f32)


def _dot01(a01, b, dn):
    a = a01.astype(bf16)
    b0 = b.astype(bf16)
    r1 = b - b0.astype(f32)
    b1 = r1.astype(bf16)
    b2 = (r1 - b1.astype(f32)).astype(bf16)
    d = lambda q: lax.dot_general(a, q, dn, preferred_element_type=f32)
    return d(b0) + (d(b1) + d(b2))


def _sigmoid(x):
    return jax.nn.sigmoid(x)


def _silu(x):
    return x * _sigmoid(x)


def _dsilu(x):
    s = _sigmoid(x)
    return s * (1.0 + x * (1.0 - s))


def _softplus(x):
    return jnp.maximum(x, 0.0) + jnp.log1p(jnp.exp(-jnp.abs(x)))


_GELU_C = math.sqrt(2.0 / math.pi)


def _gelu(y):
    return 0.5 * y * (1.0 + jnp.tanh(_GELU_C * (y + 0.044715 * y * y * y)))


def _dgelu(y):
    t = jnp.tanh(_GELU_C * (y + 0.044715 * y * y * y))
    return 0.5 * (1.0 + t) + 0.5 * y * (1.0 - t * t) * _GELU_C * (1.0 + 3.0 * 0.044715 * y * y)


def _matmul(a, b, form, *, outs, name, epilogue=None, extras=(), tm=4096, tn=512, tk=1024, shard_cols=0, a_map=None,
            after=()):
    if form == "tn":
        K, M = a.shape
    else:
        M, K = a.shape
    per_step = 1
    if b.ndim == 3:
        assert form in ("nn", "nt"), name
        N = b.shape[1] if form == "nt" else N_DEV * b.shape[2]
        if form == "nn":
            tn = b.shape[2]
        else:
            per_step = max(1, tk // b.shape[2])
            tk = per_step * b.shape[2]
    else:
        N = b.shape[0] if form == "nt" else b.shape[1]
    ns = N // N_DEV
    if shard_cols:
        tn = ns * shard_cols
    tm, tn, tk = min(tm, M), min(tn, N), min(tk, K)
    assert M % tm == 0 and N % tn == 0 and K % tk == 0, (name, M, N, K, tm, tn, tk)
    nk = K // tk
    dn = {"nn": _NN, "nt": _NT, "tn": _TN}[form]
    if form == "tn":
        a_spec = pl.BlockSpec((tk, tm), lambda i, j, k: (k, i))
    else:
        a_spec = pl.BlockSpec((tm, tk), lambda i, j, k: (i, k))
    if b.ndim == 3:
        b_spec = (pl.BlockSpec((per_step, tn, tk // per_step), lambda i, j, k: (k, j, 0)) if form == "nt"
                  else pl.BlockSpec((None, tk, tn), lambda i, j, k: (j, k, 0)))
    elif form == "nt":
        b_spec = pl.BlockSpec((tn, tk), lambda i, j, k: (j, k))
    else:
        b_spec = pl.BlockSpec((tk, tn), lambda i, j, k: (k, j))
    e_spec = pl.BlockSpec((tm, tn), lambda i, j, k: (i, j))
    v_spec = pl.BlockSpec((1, tn), lambda i, j, k: (0, j))
    if shard_cols:
        o_spec = pl.BlockSpec((shard_cols, tm, ns), lambda i, j, k: (j, i, 0))
        o_shape = (N_DEV, M, ns)
    else:
        o_spec = e_spec
        o_shape = (M, N)
    n_ex = len(extras)
    n_in = 2 + n_ex + len(after)
    sums = [isinstance(o, tuple) for o in outs]
    assert not any(sums) or tn == N, name

    def finish(acc, ex, o_refs, row_tile):
        vals = (acc,) if epilogue is None else epilogue(acc, *[e[...] for e in ex])
        for r, v, is_sum in zip(o_refs, vals, sums):
            if is_sum:
                @pl.when(row_tile == 0)
                def _(r=r, v=v):
                    r[...] = v.astype(r.dtype)

                @pl.when(row_tile > 0)
                def _(r=r, v=v):
                    r[...] += v.astype(r.dtype)
            elif shard_cols:
                for s in range(shard_cols):
                    r[s] = v[:, s * ns:(s + 1) * ns].astype(r.dtype)
            else:
                r[...] = v.astype(r.dtype)

    def prod(a_ref, b_ref):
        if b.ndim == 3 and form == "nt":
            w = tk // per_step
            return sum(_dot(a_ref[:, s * w:(s + 1) * w], b_ref[s], dn) for s in range(1, per_step)) + _dot(a_ref[:, 0:w], b_ref[0], dn)
        av = a_ref[...]
        return _dot(av if a_map is None else a_map(av), b_ref[...], dn)

    def body_one(*refs):
        finish(prod(*refs[:2]), refs[2:2 + n_ex], refs[n_in:], pl.program_id(0))

    def body_acc(*refs):
        a_ref, b_ref = refs[:2]
        acc = refs[-1]
        k = pl.program_id(2)
        row_tile = pl.program_id(0)

        @pl.when(k == 0)
        def _():
            acc[...] = prod(a_ref, b_ref)

        @pl.when((k > 0) & (k < nk - 1))
        def _():
            acc[...] += prod(a_ref, b_ref)

        @pl.when(k == nk - 1)
        def _():
            finish(acc[...] + prod(a_ref, b_ref), refs[2:2 + n_ex], refs[n_in:-1], row_tile)

    return pl.pallas_call(
        body_one if nk == 1 else body_acc, grid=(M // tm, N // tn, nk),
        in_specs=[a_spec, b_spec] + [v_spec if e.shape[0] == 1 else e_spec for e in extras]
        + [pl.BlockSpec(memory_space=pl.ANY)] * len(after),
        out_specs=[v_spec if s else o_spec for s in sums],
        out_shape=[jax.ShapeDtypeStruct((1, N), o[1]) if s else jax.ShapeDtypeStruct(o_shape, o) for o, s in zip(outs, sums)],
        scratch_shapes=[] if nk == 1 else [pltpu.VMEM((tm, tn), f32)],
        compiler_params=_cparams(("arbitrary" if any(sums) else "parallel", "parallel", "arbitrary")), name=name,
    )(a, b, *extras, *after)


def _rms_fwd(x, w, name):
    T, D = x.shape
    tt = min(1024, T)

    def body(x_ref, w_ref, h_ref):
        xv = x_ref[...]
        r = lax.rsqrt(jnp.mean(xv * xv, axis=1, keepdims=True) + EPS)
        h_ref[...] = (xv * r * w_ref[...]).astype(bf16)

    return pl.pallas_call(
        body, grid=(T // tt,),
        in_specs=[pl.BlockSpec((tt, D), lambda i: (i, 0)), pl.BlockSpec((1, D), lambda i: (0, 0))],
        out_specs=pl.BlockSpec((tt, D), lambda i: (i, 0)),
        out_shape=jax.ShapeDtypeStruct((T, D), bf16),
        compiler_params=_cparams(("parallel",)), name=name,
    )(x, w)


def _residual_rms_ep(acc, res, w):
    x = res + acc
    r = lax.rsqrt(jnp.mean(x * x, axis=1, keepdims=True) + EPS)
    return x, x * r * w


_RMS_BWD_OUTS = [f32, bf16, ("sum", f32)]


def _rms_bwd_ep(dh, x, dres, w):
    r = lax.rsqrt(jnp.mean(x * x, axis=1, keepdims=True) + EPS)
    xn = x * r
    dhw = dh * w
    dx = dres + r * (dhw - xn * jnp.mean(dhw * xn, axis=1, keepdims=True))
    return dx, dx, jnp.sum(dh * xn, axis=0, keepdims=True)


_LOSS_OUTS = [("sum", f32), f32, bf16, ("sum", f32)]


def _loss_ep(acc, res, w, tgt):
    x = res + acc
    D = x.shape[1]
    r = lax.rsqrt(jnp.mean(x * x, axis=1, keepdims=True) + EPS)
    xn = x * r
    e = xn * w - tgt
    loss = 0.5 * jnp.sum(jnp.mean(e * e, axis=1, keepdims=True), axis=0, keepdims=True)
    dy = e * (1.0 / D)
    dyw = dy * w
    dx = r * (dyw - xn * jnp.mean(dyw * xn, axis=1, keepdims=True))
    return jnp.broadcast_to(loss, (1, D)), dx, dx, jnp.sum(dy * xn, axis=0, keepdims=True)


def _ret_tables():
    H, C = RET_HEADS, RET_CHUNK
    lg = np.log1p(-np.exp2(-5.0 - np.arange(H, dtype=np.float32))).astype(np.float32)
    idx = np.arange(C, dtype=np.float32)
    diff = idx[:, None] - idx[None, :]
    causal = diff >= 0
    dm = np.where(causal[None], np.exp(lg[:, None, None] * np.where(causal, diff, 0.0)[None]), 0.0)
    qd = np.exp(lg[:, None] * (idx[None, :] + 1.0))
    kd = np.exp(lg[:, None] * (C - 1.0 - idx[None, :]))
    cg = np.exp(lg * C)
    tab = np.zeros((H, 4, C, HEAD_DIM), np.float32)
    tab[:, 0] = dm
    tab[:, 1] = qd[:, :, None]
    tab[:, 2] = kd[:, :, None]
    tab[:, 3] = cg[:, None, None]
    return jnp.asarray(tab)


def _rope_tables(T):
    half = HEAD_DIM // 2
    inv = np.float32(ROPE_THETA) ** (-np.arange(half, dtype=np.float32) / np.float32(half))
    ang = np.arange(T, dtype=np.float32)[:, None] * inv[None, :]
    c, s = np.cos(ang), np.sin(ang)
    return jnp.asarray(np.concatenate([c, c], axis=1)), jnp.asarray(np.concatenate([-s, s], axis=1))


def _rope(x, cos, sin):
    return x * cos + pltpu.roll(x, HEAD_DIM // 2, 1) * sin


def _unrope(y, cos, sin):
    return y * cos + pltpu.roll(y * sin, HEAD_DIM // 2, 1)


def _stack_heads(ref, H, f=None):
    parts = [ref[:, h * HEAD_DIM:(h + 1) * HEAD_DIM] for h in range(H)]
    return jnp.stack(parts if f is None else [f(a) for a in parts])


def _ret_fwd(p, cos, sin, tab, name):
    T = p.shape[0]
    C, H = RET_CHUNK, RET_HEADS
    N = T // C
    K = min(RET_STEP, N)
    NS = N // K
    scale = HEAD_DIM ** -0.5

    def body(q_ref, k_ref, v_ref, g_ref, c_ref, s_ref, t_ref, y_ref, o_ref, sp_ref, st):
        @pl.when(pl.program_id(0) == 0)
        def _():
            st[...] = jnp.zeros_like(st)

        dm, qd, kd, cg = t_ref[:, 0], t_ref[:, 1], t_ref[:, 2], t_ref[:, 3]
        S = st[...]
        for c in range(K):
            rows = pl.ds(c * C, C)
            cos_, sin_ = c_ref[rows, :], s_ref[rows, :]
            rot = lambda a: _rope(a, cos_, sin_)
            q = _stack_heads(q_ref.at[rows, :], H, rot)
            k = _stack_heads(k_ref.at[rows, :], H, rot) * scale
            v = _stack_heads(v_ref.at[rows, :], H)
            P = _dot(q, k, _NT3) * dm
            o = _dot(P, v, _NN3) + _dot(q * qd, S, _NN3)
            sp_ref[c] = S
            S = cg * S + _dot(k * kd, v, _TN3)
            r = lax.rsqrt(jnp.mean(o * o, axis=2, keepdims=True) + EPS)
            y = o * r * _silu(_stack_heads(g_ref.at[rows, :], H))
            for h in range(H):
                o_ref[rows, h * HEAD_DIM:(h + 1) * HEAD_DIM] = o[h]
                y_ref[rows, h * HEAD_DIM:(h + 1) * HEAD_DIM] = y[h].astype(bf16)
        st[...] = S

    wide = lambda blk: pl.BlockSpec((K * C, H * HEAD_DIM), lambda n: (n, blk))
    tbl = pl.BlockSpec((K * C, HEAD_DIM), lambda n: (n, 0))
    return pl.pallas_call(
        body, grid=(NS,),
        in_specs=[wide(0), wide(1), wide(2), wide(3), tbl, tbl,
                  pl.BlockSpec((H, 4, C, HEAD_DIM), lambda n: (0, 0, 0, 0))],
        out_specs=[wide(0), wide(0), pl.BlockSpec((K, H, HEAD_DIM, HEAD_DIM), lambda n: (n, 0, 0, 0))],
        out_shape=[jax.ShapeDtypeStruct((T, D_MODEL), bf16), jax.ShapeDtypeStruct((T, H * HEAD_DIM), f32),
                   jax.ShapeDtypeStruct((N, H, HEAD_DIM, HEAD_DIM), f32)],
        scratch_shapes=[pltpu.VMEM((H, HEAD_DIM, HEAD_DIM), f32)],
        compiler_params=_cparams(("arbitrary",)), name=name,
    )(p, p, p, p, cos, sin, tab)


def _ret_bwd(p, cos, sin, tab, o_raw, sprev, dmix, name):
    T = p.shape[0]
    C, H = RET_CHUNK, RET_HEADS
    N = T // C
    K = min(RET_STEP, N)
    NS = N // K
    scale = HEAD_DIM ** -0.5
    W = H * HEAD_DIM

    def body(q_ref, k_ref, v_ref, g_ref, c_ref, s_ref, t_ref, o_ref, sp_ref, dy_ref, d_ref, dst):
        @pl.when(pl.program_id(0) == 0)
        def _():
            dst[...] = jnp.zeros_like(dst)

        dm, qd, kd, cg = t_ref[:, 0], t_ref[:, 1], t_ref[:, 2], t_ref[:, 3]
        dS1 = dst[...]
        for c in reversed(range(K)):
            rows = pl.ds(c * C, C)
            cos_, sin_ = c_ref[rows, :], s_ref[rows, :]
            rot = lambda a: _rope(a, cos_, sin_)
            q = _stack_heads(q_ref.at[rows, :], H, rot)
            k = _stack_heads(k_ref.at[rows, :], H, rot) * scale
            v = _stack_heads(v_ref.at[rows, :], H)
            g = _stack_heads(g_ref.at[rows, :], H)
            S = sp_ref[c]
            o = _stack_heads(o_ref.at[rows, :], H)
            dy = _stack_heads(dy_ref.at[rows, :], H)
            r = lax.rsqrt(jnp.mean(o * o, axis=2, keepdims=True) + EPS)
            nrm = o * r
            dn = dy * _silu(g)
            dg = dy * nrm * _dsilu(g)
            do = r * (dn - nrm * jnp.mean(dn * nrm, axis=2, keepdims=True))
            P = _dot(q, k, _NT3) * dm
            dP = _dot(do, v, _NT3) * dm
            dq = _dot(dP, k, _NN3) + _dot(do, S, _NT3) * qd
            dk = (_dot(dP, q, _TN3) + _dot(v, dS1, _NT3) * kd) * scale
            dv = _dot(P, do, _TN3) + _dot(k * kd, dS1, _NN3)
            dS1 = cg * dS1 + _dot(q * qd, do, _TN3)
            for h in range(H):
                d_ref[rows, h * HEAD_DIM:(h + 1) * HEAD_DIM] = _unrope(dq[h], cos_, sin_).astype(bf16)
                d_ref[rows, W + h * HEAD_DIM:W + (h + 1) * HEAD_DIM] = _unrope(dk[h], cos_, sin_).astype(bf16)
                d_ref[rows, 2 * W + h * HEAD_DIM:2 * W + (h + 1) * HEAD_DIM] = dv[h].astype(bf16)
                d_ref[rows, 3 * W + h * HEAD_DIM:3 * W + (h + 1) * HEAD_DIM] = dg[h].astype(bf16)
        dst[...] = dS1

    rev = lambda blk: pl.BlockSpec((K * C, W), lambda n: (NS - 1 - n, blk))
    tbl = pl.BlockSpec((K * C, HEAD_DIM), lambda n: (NS - 1 - n, 0))
    return pl.pallas_call(
        body, grid=(NS,),
        in_specs=[rev(0), rev(1), rev(2), rev(3), tbl, tbl,
                  pl.BlockSpec((H, 4, C, HEAD_DIM), lambda n: (0, 0, 0, 0)), rev(0),
                  pl.BlockSpec((K, H, HEAD_DIM, HEAD_DIM), lambda n: (NS - 1 - n, 0, 0, 0)), rev(0)],
        out_specs=pl.BlockSpec((K * C, 4 * W), lambda n: (NS - 1 - n, 0)),
        out_shape=jax.ShapeDtypeStruct((T, 6 * W), bf16),
        scratch_shapes=[pltpu.VMEM((H, HEAD_DIM, HEAD_DIM), f32)],
        compiler_params=_cparams(("arbitrary",)), name=name,
    )(p, p, p, p, cos, sin, tab, o_raw, sprev, dmix)


CONV_K = 4
CONV_W = 512


def _conv_block(C, col_off):
    cw = CONV_W if (C % (2 * CONV_W) or col_off % 2) else 2 * CONV_W
    return cw, col_off * CONV_W // cw
PAD = 8
SUB_R = 64


def _conv_fwd(x, col_off, w, b, act, name):
    T = x.shape[0]
    C = w.shape[1]
    cw, cb = _conv_block(C, col_off)
    G = C // cw
    tt = min(1024, T)
    NT = T // tt
    has_b = b is not None

    def body(*refs):
        if has_b:
            x_ref, w_ref, b_ref, y_ref, pad = refs
        else:
            x_ref, w_ref, y_ref, pad = refs
        t = pl.program_id(1)

        @pl.when(t == 0)
        def _():
            pad[pl.ds(0, PAD), :] = jnp.zeros((PAD, cw), f32)

        pad[pl.ds(PAD, tt), :] = x_ref[...]
        for g in range(cw // 128):
            ls = slice(g * 128, (g + 1) * 128)
            wv = w_ref[:, ls]
            for c in range(tt // SUB_R):
                r0 = c * SUB_R
                y = wv[0:1, :] * pad[pl.ds(PAD - 3 + r0, SUB_R), ls]
                for kk in range(1, CONV_K):
                    y = y + wv[kk:kk + 1, :] * pad[pl.ds(PAD - 3 + kk + r0, SUB_R), ls]
                if has_b:
                    y = y + b_ref[:, ls]
                y_ref[pl.ds(r0, SUB_R), ls] = _silu(y) if act else y
        tail = pad[pl.ds(tt, PAD), :]
        pad[pl.ds(0, PAD), :] = tail

    in_specs = [pl.BlockSpec((tt, cw), lambda g, t: (t, cb + g)),
                pl.BlockSpec((CONV_K, cw), lambda g, t: (0, g))]
    args = [x, w]
    if has_b:
        in_specs.append(pl.BlockSpec((1, cw), lambda g, t: (0, g)))
        args.append(b)
    return pl.pallas_call(
        body, grid=(G, NT), in_specs=in_specs,
        out_specs=pl.BlockSpec((tt, cw), lambda g, t: (t, g)),
        out_shape=jax.ShapeDtypeStruct((T, C), f32),
        scratch_shapes=[pltpu.VMEM((tt + PAD, cw), f32)],
        compiler_params=_cparams(("parallel", "arbitrary")), name=name,
    )(*args)


def _conv_bwd(x, col_off, w, b, act, dout, dp, name):
    T = x.shape[0]
    C = w.shape[1]
    cw, cb = _conv_block(C, col_off)
    G = C // cw
    tt = min(1024, T)
    NT = T // tt
    has_b = b is not None

    def body(*refs):
        if has_b:
            x_ref, xp_ref, w_ref, b_ref, d_ref, dp_in, dx_ref, dw_ref, db_ref, pad, dpad = refs
        else:
            x_ref, xp_ref, w_ref, d_ref, dp_in, dx_ref, dw_ref, db_ref, pad, dpad = refs
        t = pl.program_id(1)
        first_tile = t == NT - 1

        @pl.when(t == 0)
        def _():
            dpad[pl.ds(tt, PAD), :] = jnp.zeros((PAD, cw), f32)
            dw_ref[...] = jnp.zeros_like(dw_ref)
            db_ref[...] = jnp.zeros_like(db_ref)

        pad[pl.ds(0, PAD), :] = jnp.where(first_tile, 0.0, xp_ref[...])
        pad[pl.ds(PAD, tt), :] = x_ref[...]
        fold = lambda v: v.reshape(SUB_R // 8, 8, 128).sum(axis=0)
        for g in range(cw // 128):
            ls = slice(g * 128, (g + 1) * 128)
            wv = w_ref[:, ls]
            acc =[jnp.zeros((8, 128), f32) for _ in range(CONV_K + 1)]
            for c in reversed(range(tt // SUB_R)):
                r0 = c * SUB_R
                xs = [pad[pl.ds(PAD - 3 + kk + r0, SUB_R), ls] for kk in range(CONV_K)]
                dy = d_ref[pl.ds(r0, SUB_R), ls]
                if act:
                    y = wv[0:1, :] * xs[0]
                    for kk in range(1, CONV_K):
                        y = y + wv[kk:kk + 1, :] * xs[kk]
                    if has_b:
                        y = y + b_ref[:, ls]
                    dy = dy * _dsilu(y)
                dpad[pl.ds(r0, SUB_R), ls] = dy
                dx = wv[3:4, :] * dy
                for j in range(1, CONV_K):
                    dx = dx + wv[3 - j:4 - j, :] * dpad[pl.ds(r0 + j, SUB_R), ls]
                dx_ref[pl.ds(r0, SUB_R), ls] = dx.astype(bf16)
                for kk in range(CONV_K):
                    acc[kk] = acc[kk] + fold(dy * xs[kk])
                acc[CONV_K] = acc[CONV_K] + fold(dy)
            for kk in range(CONV_K):
                dw_ref[kk:kk + 1, ls] += jnp.sum(acc[kk], axis=0, keepdims=True)
            db_ref[:, ls] += jnp.sum(acc[CONV_K], axis=0, keepdims=True)
        head = dpad[pl.ds(0, PAD), :]
        dpad[pl.ds(tt, PAD), :] = head

    rows8 = tt // PAD
    in_specs = [pl.BlockSpec((tt, cw), lambda g, t: (NT - 1 - t, cb + g)),
                pl.BlockSpec((PAD, cw), lambda g, t: (jnp.maximum((NT - 1 - t) * rows8 - 1, 0), cb + g)),
                pl.BlockSpec((CONV_K, cw), lambda g, t: (0, g))]
    args = [x, x, w]
    if has_b:
        in_specs.append(pl.BlockSpec((1, cw), lambda g, t: (0, g)))
        args.append(b)
    in_specs += [pl.BlockSpec((tt, cw), lambda g, t: (NT - 1 - t, g)), pl.BlockSpec(memory_space=pl.ANY)]
    args += [dout, dp]
    return pl.pallas_call(
        body, grid=(G, NT), in_specs=in_specs,
        out_specs=[pl.BlockSpec((tt, cw), lambda g, t: (NT - 1 - t, cb + g)),
                   pl.BlockSpec((CONV_K, cw), lambda g, t: (0, g)),
                   pl.BlockSpec((1, cw), lambda g, t: (0, g))],
        out_shape=[jax.ShapeDtypeStruct(dp.shape, dp.dtype), jax.ShapeDtypeStruct((CONV_K, C), f32),
                   jax.ShapeDtypeStruct((1, C), f32)],
        input_output_aliases={len(args) - 1: 0},
        scratch_shapes=[pltpu.VMEM((tt + PAD, cw), f32), pltpu.VMEM((tt + PAD, cw), f32)],
        compiler_params=_cparams(("parallel", "arbitrary")), name=name,
    )(*args)


def _lru_gates(xc, wr, wi, br, bi, lam):
    r = _sigmoid(_dot(xc, wr, _NN) + br)
    i = _sigmoid(_dot(xc, wi, _NN) + bi)
    sp = _softplus(-lam)
    a = jnp.exp(-LRU_C * r * sp)
    mult = jnp.sqrt(1.0 - a * a)
    return r, i, sp, a, mult


def _lru_fwd(xc, p, y_off, wr, wi, br, bi, lam, mix, name):
    T = xc.shape[0]
    G = LRU_WIDTH // 128
    tt = min(512, T)
    NT = T // tt

    def body(x_ref, y_ref, wr_ref, wi_ref, br_ref, bi_ref, l_ref, mix_in, o_ref, h_ref, hc):
        t = pl.program_id(1)

        @pl.when(t == 0)
        def _():
            hc[...] = jnp.zeros_like(hc)

        x = x_ref[...]
        r, i, sp, a, mult = _lru_gates(x, wr_ref[...], wi_ref[...], br_ref[...], bi_ref[...], l_ref[...])
        row = lax.broadcasted_iota(jnp.int32, (tt, 128), 0)
        mult = jnp.where((row == 0) & (t == 0), 1.0, mult)
        U = x * i * mult
        A = a
        d = 1
        while d < tt:
            keep = row >= d
            Ush = jnp.where(keep, pltpu.roll(U, d, 0), 0.0)
            Ash = jnp.where(keep, pltpu.roll(A, d, 0), 1.0)
            U = A * Ush + U
            A = A * Ash
            d *= 2
        h = U + A * hc[0:1, :]
        h_ref[...] = h
        hc[...] = jnp.broadcast_to(h[tt - 1:tt, :], hc.shape)
        o_ref[...] = (h * _gelu(y_ref[...])).astype(bf16)

    tile = pl.BlockSpec((tt, 128), lambda g, t: (t, g))
    vec = pl.BlockSpec((1, 128), lambda g, t: (0, g))
    wsp = pl.BlockSpec((128, 128), lambda g, t: (g, g))
    return pl.pallas_call(
        body, grid=(G, NT),
        in_specs=[tile, pl.BlockSpec((tt, 128), lambda g, t: (t, y_off + g)), wsp, wsp, vec, vec, vec,
                  pl.BlockSpec(memory_space=pl.ANY)],
        out_specs=[pl.BlockSpec((tt, 128), lambda g, t: (t, G + g)), tile],
        out_shape=[jax.ShapeDtypeStruct(mix.shape, mix.dtype), jax.ShapeDtypeStruct((T, LRU_WIDTH), f32)],
        input_output_aliases={7: 0},
        scratch_shapes=[pltpu.VMEM((8, 128), f32)],
        compiler_params=_cparams(("parallel", "arbitrary")), name=name,
    )(xc, p, wr, wi, br, bi, lam, mix)


def _lru_bwd(xc, p, y_off, wr, wi, br, bi, lam, hs, dmix, d_off, dp, name):
    T = xc.shape[0]
    G = LRU_WIDTH // 128
    tt = min(512, T)
    NT = T // tt

    def body(x_ref, y_ref, wr_ref, wi_ref, br_ref, bi_ref, l_ref, h_ref, hp_ref, do_ref, dp_in,
             dx_ref, dy_ref, dwr_ref, dwi_ref, dbr_ref, dbi_ref, dl_ref, lc, an):
        t = pl.program_id(1)
        first_tile = t == NT - 1

        @pl.when(t == 0)
        def _():
            lc[...] = jnp.zeros_like(lc)
            an[...] = jnp.zeros_like(an)
            dwr_ref[...] = jnp.zeros_like(dwr_ref)
            dwi_ref[...] = jnp.zeros_like(dwi_ref)
            dbr_ref[...] = jnp.zeros_like(dbr_ref)
            dbi_ref[...] = jnp.zeros_like(dbi_ref)
            dl_ref[...] = jnp.zeros_like(dl_ref)

        x = x_ref[...]
        y = y_ref[...]
        wr, wi, lam_ = wr_ref[...], wi_ref[...], l_ref[...]
        r, i, sp, a, mult_raw = _lru_gates(x, wr, wi, br_ref[...], bi_ref[...], lam_)
        row = lax.broadcasted_iota(jnp.int32, (tt, 128), 0)
        t0 = (row == 0) & first_tile
        mult = jnp.where(t0, 1.0, mult_raw)
        h = h_ref[...]
        do = do_ref[...]
        dh = do * _gelu(y)
        dy_ref[...] = (do * h * _dgelu(y)).astype(bf16)
        B = jnp.where(row == tt - 1, an[0:1, :], pltpu.roll(a, tt - 1, 0))
        L = dh
        d = 1
        while d < tt:
            keep = row < tt - d
            Lsh = jnp.where(keep, pltpu.roll(L, tt - d, 0), 0.0)
            Bsh = jnp.where(keep, pltpu.roll(B, tt - d, 0), 1.0)
            L = L + B * Lsh
            B = B * Bsh
            d *= 2
        L = L + B * lc[0:1, :]
        lc[...] = jnp.broadcast_to(L[0:1, :], lc.shape)
        an[...] = jnp.broadcast_to(a[0:1, :], an.shape)
        hprev = jnp.where(first_tile, 0.0, hp_ref[...])[PAD - 1:PAD, :]
        hm1 = jnp.where(row == 0, hprev, pltpu.roll(h, 1, 0))
        da = L * hm1
        dxc = L * i * mult
        di = L * x * mult
        dmult = jnp.where(t0, 0.0, L * x * i)
        da = da - jnp.where(t0, 0.0, dmult * a / mult_raw)
        dlog_a = da * a
        dr = dlog_a * (-LRU_C) * sp
        dsp = jnp.sum(dlog_a * (-LRU_C) * r, axis=0, keepdims=True)
        dpr = dr * r * (1.0 - r)
        dpi = di * i * (1.0 - i)
        dx_ref[...] = dxc + _dot(dpr, wr, _NT) + _dot(dpi, wi, _NT)
        for d_ref, dpre in ((dwr_ref, dpr), (dwi_ref, dpi)):
            dw = _dot(x, dpre, _TN)
            for s in range(2):
                d_ref[s] += dw[s * 64:(s + 1) * 64, s * 64:(s + 1) * 64]
        dbr_ref[...] += jnp.sum(dpr, axis=0, keepdims=True)
        dbi_ref[...] += jnp.sum(dpi, axis=0, keepdims=True)
        dl_ref[...] += dsp * (-_sigmoid(-lam_))

    rows8 = tt // PAD
    tile = pl.BlockSpec((tt, 128), lambda g, t: (NT - 1 - t, g))
    vec = pl.BlockSpec((1, 128), lambda g, t: (0, g))
    wsp = pl.BlockSpec((128, 128), lambda g, t: (g, g))
    wout = pl.BlockSpec((2, 64, 64), lambda g, t: (g, 0, 0))
    return pl.pallas_call(
        body, grid=(G, NT),
        in_specs=[tile, pl.BlockSpec((tt, 128), lambda g, t: (NT - 1 - t, y_off + g)), wsp, wsp, vec, vec, vec, tile,
                  pl.BlockSpec((PAD, 128), lambda g, t: (jnp.maximum((NT - 1 - t) * rows8 - 1, 0), g)),
                  pl.BlockSpec((tt, 128), lambda g, t: (NT - 1 - t, d_off + g)), pl.BlockSpec(memory_space=pl.ANY)],
        out_specs=[tile, pl.BlockSpec((tt, 128), lambda g, t: (NT - 1 - t, y_off + g)), wout, wout, vec, vec, vec],
        out_shape=[jax.ShapeDtypeStruct((T, LRU_WIDTH), f32), jax.ShapeDtypeStruct(dp.shape, dp.dtype),
                   jax.ShapeDtypeStruct((2 * G, 64, 64), f32), jax.ShapeDtypeStruct((2 * G, 64, 64), f32),
                   jax.ShapeDtypeStruct((1, LRU_WIDTH), f32), jax.ShapeDtypeStruct((1, LRU_WIDTH), f32),
                   jax.ShapeDtypeStruct((1, LRU_WIDTH), f32)],
        input_output_aliases={10: 1},
        scratch_shapes=[pltpu.VMEM((8, 128), f32), pltpu.VMEM((8, 128), f32)],
        compiler_params=_cparams(("parallel", "arbitrary")), name=name,
    )(xc, p, wr, wi, br, bi, lam, hs, hs, dmix, dp)


_NN3 = (((2,), (1,)), ((0,), (0,)))
_NT3 = (((2,), (2,)), ((0,), (0,)))
_TN3 = (((1,), (1,)), ((0,), (0,)))


def _pairs(ref, K):
    C = GDN_CHUNK
    return jnp.stack([ref[c * C:(c + 1) * C, h * HEAD_DIM:(h + 1) * HEAD_DIM] for c in range(K) for h in range(GDN_HEADS)])


def _put_pairs(ref, val, K, col=0):
    C, H = GDN_CHUNK, GDN_HEADS
    for c in range(K):
        for h in range(H):
            ref[c * C:(c + 1) * C, col + h * HEAD_DIM:col + (h + 1) * HEAD_DIM] = val[c * H + h].astype(ref.dtype)


def _rowsum(x):
    H, C, L = x.shape
    return _dot(x.reshape(H * C, L), jnp.ones((L, HEAD_DIM), f32), _NN).reshape(H, C, HEAD_DIM)


def _gdn_pre(qr, kr, v, ba, alog, dtb):
    C, H = GDN_CHUNK, GDN_HEADS
    B = qr.shape[0]
    K = B // H
    lane = lax.broadcasted_iota(jnp.int32, (C, 128), 1)
    lane3 = lax.broadcasted_iota(jnp.int32, (B, C, 128), 2)
    ri = lax.broadcasted_iota(jnp.int32, (C, C), 0)
    ci = lax.broadcasted_iota(jnp.int32, (C, C), 1)
    rowc = lax.broadcasted_iota(jnp.int32, (C, 1), 0)
    col = lambda m, j: jnp.sum(jnp.where(lane == j, m, 0.0), axis=1, keepdims=True)
    ea = jnp.exp(alog)
    tri = (ri >= ci).astype(f32)
    g_all, beta_cols, G_cols = [], [], []
    for c in range(K):
        ba_c = ba[c * C:(c + 1) * C]
        g_c = -ea * _softplus(ba_c + dtb)
        G_c = _dot01(tri, g_c, _NN)
        s_c = _sigmoid(ba_c)
        g_all.append(g_c)
        beta_cols += [col(s_c, h) for h in range(H)]
        G_cols += [col(G_c, H + h) for h in range(H)]
    wide = lambda c: jnp.broadcast_to(c, (B, C, 128))
    beta = wide(jnp.stack(beta_cols))
    Gc = jnp.stack(G_cols)
    rq = lax.rsqrt(_rowsum(qr * qr) + EPS)
    rk = lax.rsqrt(_rowsum(kr * kr) + EPS)
    qh, kn = qr * rq, kr * rk
    qn = qh * (HEAD_DIM ** -0.5)
    Grow = _dot01(jnp.ones((B, C, 128), f32), jnp.where(lane3 == 0, Gc, 0.0), _NT3)
    incl = ri >= ci
    Di = jnp.where(incl, jnp.exp(jnp.where(incl, Gc - Grow, 0.0)), 0.0)
    Ds = jnp.where(ri > ci, Di, 0.0)
    Gl = jnp.sum(jnp.where(rowc == C - 1, Gc, 0.0), axis=1, keepdims=True)
    eG = wide(jnp.exp(Gc))
    eGl = wide(jnp.exp(Gl - Gc))
    cd = jnp.exp(Gl)
    kb = kn * beta
    vb = v * beta
    Lm = _dot(kb, kn, _NT3) * Ds
    kbg = kb * eG
    QK = _dot(qn, kn, _NT3) * Di
    qg = qn * eG
    kg = kn * eGl
    return dict(beta=beta, g_all=g_all, rq=rq, rk=rk, qh=qh, kn=kn, qn=qn, Di=Di, Ds=Ds, eG=eG, eGl=eGl, cd=cd,
                kb=kb, vb=vb, Lm=Lm, kbg=kbg, QK=QK, qg=qg, kg=kg, lane=lane, ri=ri, ci=ci, rowc=rowc, ea=ea)


def _unit_lower_inverse(Lm):
    C = Lm.shape[-1]
    ri = lax.broadcasted_iota(jnp.int32, (C, C), 0)
    ci = lax.broadcasted_iota(jnp.int32, (C, C), 1)
    same = lambda s: (ri // s) == (ci // s)
    Xd = jnp.where(same(8), -Lm, 0.0)
    Tinv = (ri == ci).astype(f32) + Xd
    Pw = Xd
    for _ in range(2):
        Pw = _dot(Pw, Pw, _NN3)
        Tinv = Tinv + _dot(Tinv, Pw, _NN3)
    for s in (8, 16, 32):
        off = jnp.where(same(2 * s) & jnp.logical_not(same(s)), Lm, 0.0)
        Tinv = Tinv - _dot(_dot(Tinv, off, _NN3), Tinv, _NN3)
    return Tinv


def _gdn_specs(T, rev):
    C = GDN_CHUNK
    H = GDN_HEADS
    K = min(GDN_STEP, T // C)
    NS = T // (C * K)
    nn = (lambda n: NS - 1 - n) if rev else (lambda n: n)
    wide = lambda blk: pl.BlockSpec((K * C, H * HEAD_DIM), lambda n: (nn(n), blk))
    one = lambda off: pl.BlockSpec((K * C, HEAD_DIM), lambda n: (nn(n), off))
    vec = pl.BlockSpec((1, 128), lambda n: (0, 0))
    st = lambda rows: pl.BlockSpec((K, H, rows, rows), lambda n: (nn(n), 0, 0, 0))
    return K, NS, wide, one, vec, st


def _gdn_fwd(qkv, p, alog, dtb, nw, name):
    T = qkv.shape[0]
    C, H = GDN_CHUNK, GDN_HEADS
    N = T // C
    K, NS, wide, one, vec, st_spec = _gdn_specs(T, False)

    def body(q_ref, k_ref, v_ref, z_ref, ba_ref, al_ref, dt_ref, nw_ref, y_ref, sp_ref, ti_ref, vn_ref, o_ref, st):
        @pl.when(pl.program_id(0) == 0)
        def _():
            st[...] = jnp.zeros_like(st)

        f = _gdn_pre(_pairs(q_ref, K), _pairs(k_ref, K), _pairs(v_ref, K), ba_ref[...], al_ref[...], dt_ref[...])
        Tinv = _unit_lower_inverse(f["Lm"])
        ti_ref[...] = Tinv.reshape(K, H, C, C).astype(bf16)
        w = _dot(Tinv, f["kbg"], _NN3)
        u = _dot(Tinv, f["vb"], _NN3)
        S = st[...]
        vns, os_ = [], []
        for c in range(K):
            sl = slice(c * H, (c + 1) * H)
            sp_ref[c] = S
            vn_c = u[sl] - _dot(w[sl], S, _NN3)
            os_.append(_dot(f["qg"][sl], S, _NN3) + _dot(f["QK"][sl], vn_c, _NN3))
            S = S * f["cd"][sl] + _dot(f["kg"][sl], vn_c, _TN3)
            vns.append(vn_c)
        st[...] = S
        vn, o = jnp.concatenate(vns), jnp.concatenate(os_)
        r = lax.rsqrt(_rowsum(o * o) * (1.0 / HEAD_DIM) + EPS)
        _put_pairs(y_ref, o * r * nw_ref[...] * _silu(_pairs(z_ref, K)), K)
        _put_pairs(vn_ref, vn, K)
        _put_pairs(o_ref, o, K)

    wide_f32 = jax.ShapeDtypeStruct((T, H * HEAD_DIM), f32)
    return pl.pallas_call(
        body, grid=(NS,),
        in_specs=[wide(0), wide(1), wide(2), wide(3), one(4 * H), vec, vec, vec],
        out_specs=[wide(0), st_spec(HEAD_DIM), st_spec(C), wide(0), wide(0)],
        out_shape=[jax.ShapeDtypeStruct((T, H * HEAD_DIM), bf16), jax.ShapeDtypeStruct((N, H, HEAD_DIM, HEAD_DIM), f32),
                   jax.ShapeDtypeStruct((N, H, C, C), bf16), jax.ShapeDtypeStruct((T, H * HEAD_DIM), bf16), wide_f32],
        scratch_shapes=[pltpu.VMEM((H, HEAD_DIM, HEAD_DIM), f32)],
        compiler_params=_cparams(("arbitrary",)), name=name,
    )(qkv, qkv, qkv, p, p, alog, dtb, nw)


def _gdn_bwd(qkv, p, alog, dtb, nw, sprev, tinv, vn_all, o_all, dy_all, name):
    T = qkv.shape[0]
    C, H = GDN_CHUNK, GDN_HEADS
    N = T // C
    K, NS, wide, one, vec, st_spec = _gdn_specs(T, True)
    rs = lambda m: jnp.sum(m, axis=2, keepdims=True)

    def body(q_ref, k_ref, v_ref, z_ref, ba_ref, al_ref, dt_ref, nw_ref, sp_ref, ti_ref, vn_ref, o_ref, dy_ref,
             dqkv_ref, dz_ref, dba_ref, dal_ref, ddt_ref, dnw_ref, dst):
        @pl.when(pl.program_id(0) == 0)
        def _():
            dst[...] = jnp.zeros_like(dst)
            dal_ref[...] = jnp.zeros_like(dal_ref)
            ddt_ref[...] = jnp.zeros_like(ddt_ref)
            dnw_ref[...] = jnp.zeros_like(dnw_ref)

        ba, dtb_, nwv = ba_ref[...], dt_ref[...], nw_ref[...]
        v = _pairs(v_ref, K)
        f = _gdn_pre(_pairs(q_ref, K), _pairs(k_ref, K), v, ba, al_ref[...], dtb_)
        beta, kn, qn, kb, vb, kbg = f["beta"], f["kn"], f["qn"], f["kb"], f["vb"], f["kbg"]
        eG, eGl, cd, Di, Ds, QK, qg, kg = f["eG"], f["eGl"], f["cd"], f["Di"], f["Ds"], f["QK"], f["qg"], f["kg"]
        lane, ri, ci, rowc = f["lane"], f["ri"], f["ci"], f["rowc"]
        Tinv = ti_ref[...].reshape(K * H, C, C)
        S = sp_ref[...].reshape(K * H, HEAD_DIM, HEAD_DIM)
        w_ = _dot(Tinv, kbg, _NN3)
        vn, o = _pairs(vn_ref, K), _pairs(o_ref, K)
        z, dy = _pairs(z_ref, K), _pairs(dy_ref, K)
        r = lax.rsqrt(_rowsum(o * o) * (1.0 / HEAD_DIM) + EPS)
        nrm = o * r
        sz = _silu(z)
        dn = dy * nwv * sz
        _put_pairs(dz_ref, dy * nrm * nwv * _dsilu(z), K)
        dnw_ref[...] += jnp.sum(jnp.sum(dy * nrm * sz, axis=0), axis=0, keepdims=True)
        do = r * (dn - nrm * (_rowsum(dn * nrm) * (1.0 / HEAD_DIM)))
        dvn_do = _dot(QK, do, _TN3)
        dS_do = _dot(qg, do, _TN3)
        dqg = _dot(do, S, _NT3)
        dQK = _dot(do, vn, _NT3)
        dS = dst[...]
        dS1s, dvns = [None] * K, [None] * K
        for c in reversed(range(K)):
            sl = slice(c * H, (c + 1) * H)
            dS1s[c] = dS
            dvns[c] = _dot(kg[sl], dS, _NN3) + dvn_do[sl]
            dS = cd[sl] * dS + dS_do[sl] - _dot(w_[sl], dvns[c], _TN3)
        dst[...] = dS
        dS1, dvn = jnp.concatenate(dS1s), jnp.concatenate(dvns)
        dcd = jnp.sum(jnp.sum(S * dS1, axis=2, keepdims=True), axis=1, keepdims=True)
        dkg = _dot(vn, dS1, _NT3)
        dw = -_dot(dvn, S, _NT3)
        dqn = dqg * eG
        dkn = dkg * eGl
        deGl = rs(dkg * kn)
        dQKr = dQK * Di
        E = dQK * QK
        dqn = dqn + _dot(dQKr, kn, _NN3)
        dkn = dkn + _dot(dQKr, qn, _TN3)
        dT = _dot(dvn, vb, _NT3) + _dot(dw, kbg, _NT3)
        dvb = _dot(Tinv, dvn, _TN3)
        dkbg = _dot(Tinv, dw, _TN3)
        dkb = dkbg * eG
        deG = rs(dqg * qn + dkbg * kb)
        dL = -_dot(_dot(Tinv, dT, _TN3), Tinv, _NT3)
        dKK = dL * Ds
        E = E + dL * f["Lm"]
        dkb = dkb + _dot(dKK, kn, _NN3)
        dkn = dkn + _dot(dKK, kb, _TN3) + dkb * beta
        dbeta = rs(dkb * kn + dvb * v)
        _put_pairs(dqkv_ref, dvb * beta, K, 2 * H * HEAD_DIM)
        dG = rs(E) - rs(jnp.swapaxes(E, 1, 2)) + deG * eG - deGl * eGl
        dGl = jnp.sum(deGl * eGl, axis=1, keepdims=True) + dcd * cd
        dG = dG + jnp.where(rowc == C - 1, dGl, 0.0)
        qh = f["qh"]
        _put_pairs(dqkv_ref, (HEAD_DIM ** -0.5) * f["rq"] * (dqn - qh * _rowsum(dqn * qh)), K)
        _put_pairs(dqkv_ref, f["rk"] * (dkn - kn * _rowsum(dkn * kn)), K, H * HEAD_DIM)
        db = dbeta * beta * (1.0 - beta)
        triu = (ri <= ci).astype(f32)
        for c in range(K):
            db_all = jnp.where(lane == 0, db[c * H], 0.0)
            dG_all = jnp.where(lane == H, dG[c * H], 0.0)
            for h in range(1, H):
                db_all = db_all + jnp.where(lane == h, db[c * H + h], 0.0)
                dG_all = dG_all + jnp.where(lane == H + h, dG[c * H + h], 0.0)
            dg_all = _dot01(triu, dG_all, _NN)
            da_all = dg_all * (-f["ea"]) * _sigmoid(ba[c * C:(c + 1) * C] + dtb_)
            dba_ref[c * C:(c + 1) * C, :] = (db_all + da_all).astype(bf16)
            ddt_ref[...] += jnp.sum(da_all, axis=0, keepdims=True)
            dal_ref[...] += jnp.sum(dg_all * f["g_all"][c], axis=0, keepdims=True)

    small = jax.ShapeDtypeStruct((1, 128), f32)
    return pl.pallas_call(
        body, grid=(NS,),
        in_specs=[wide(0), wide(1), wide(2), wide(3), one(4 * H), vec, vec, vec, st_spec(HEAD_DIM), st_spec(C),
                  wide(0), wide(0), wide(0)],
        out_specs=[pl.BlockSpec((K * C, 3 * H * HEAD_DIM), lambda n: (NS - 1 - n, 0)), wide(3), one(0), vec, vec, vec],
        out_shape=[jax.ShapeDtypeStruct((T, 3 * H * HEAD_DIM), f32), jax.ShapeDtypeStruct((T, ODD_PAD), bf16),
                   jax.ShapeDtypeStruct((T, 128), bf16), small, small, small],
        scratch_shapes=[pltpu.VMEM((H, HEAD_DIM, HEAD_DIM), f32)],
        compiler_params=_cparams(("arbitrary",)), name=name,
    )(qkv, qkv, qkv, p, p, alog, dtb, nw, sprev, tinv, vn_all, o_all, dy_all)


def _lanes_from(x, s):
    return x if s % 128 == 0 else pltpu.roll(x, (128 - s) % 128, 1)


def _odd_assemble(g, name):
    R = g.shape[1]
    tr = min(256, R)
    n_blk = ODD_SHARD_PAD // 128

    def body(g_ref, o_ref):
        lane = lax.broadcasted_iota(jnp.int32, (tr, 128), 1)
        blk = lambda d, m: g_ref[d, :, m * 128:(m + 1) * 128]
        for gb in range(ODD_PAD // 128):
            c0 = 128 * gb
            if c0 >= ODD_IN:
                o_ref[:, c0:c0 + 128] = jnp.zeros((tr, 128), g.dtype)
                continue
            d0 = c0 // ODD_SHARD
            m0, sh = divmod(c0 - ODD_SHARD * d0, 128)
            take = min(128, ODD_SHARD * (d0 + 1) - c0)
            p = _lanes_from(blk(d0, m0), sh)
            if sh and m0 + 1 < n_blk:
                p = jnp.where(lane < 128 - sh, p, _lanes_from(blk(d0, m0 + 1), sh))
            if take < 128:
                nxt = pltpu.roll(blk(d0 + 1, 0), take, 1) if d0 + 1 < N_DEV else jnp.zeros((tr, 128), g.dtype)
                p = jnp.where(lane < take, p, nxt)
            o_ref[:, c0:c0 + 128] = p

    return pl.pallas_call(
        body, grid=(R // tr,),
        in_specs=[pl.BlockSpec((N_DEV, tr, ODD_SHARD_PAD), lambda i: (0, i, 0))],
        out_specs=pl.BlockSpec((tr, ODD_PAD), lambda i: (i, 0)),
        out_shape=jax.ShapeDtypeStruct((R, ODD_PAD), g.dtype),
        compiler_params=_cparams(("parallel",)), name=name,
    )(g)


def _odd_split(w, name):
    R = w.shape[0]
    tr = min(256, R)

    def body(w_ref, o_ref):
        lane = lax.broadcasted_iota(jnp.int32, (tr, 128), 1)
        blk = lambda gb: w_ref[:, gb * 128:(gb + 1) * 128]
        for d in range(N_DEV):
            for m in range(ODD_SHARD_PAD // 128):
                g0, sh = divmod(ODD_SHARD * d + 128 * m, 128)
                p = _lanes_from(blk(g0), sh)
                if sh and g0 + 1 < ODD_PAD // 128:
                    p = jnp.where(lane < 128 - sh, p, _lanes_from(blk(g0 + 1), sh))
                real = ODD_SHARD - 128 * m
                if real < 128:
                    p = jnp.where(lane < real, p, jnp.zeros_like(p))
                o_ref[d, :, m * 128:(m + 1) * 128] = p

    return pl.pallas_call(
        body, grid=(R // tr,),
        in_specs=[pl.BlockSpec((tr, ODD_PAD), lambda i: (i, 0))],
        out_specs=pl.BlockSpec((N_DEV, tr, ODD_SHARD_PAD), lambda i: (0, i, 0)),
        out_shape=jax.ShapeDtypeStruct((N_DEV, R, ODD_SHARD_PAD), w.dtype),
        compiler_params=_cparams(("parallel",)), name=name,
    )(w)


def _adam_tile(g, w_ref, m_ref, v_ref, go_ref, d_ref, mo_ref, vo_ref):
    c1 = 1.0 - ADAM_B1 ** ADAM_STEP
    c2 = 1.0 - ADAM_B2 ** ADAM_STEP
    mn = ADAM_B1 * m_ref[...] + (1.0 - ADAM_B1) * g
    vn = ADAM_B2 * v_ref[...] + (1.0 - ADAM_B2) * (g * g)
    go_ref[...] = g
    mo_ref[...] = mn
    vo_ref[...] = vn
    d_ref[...] = -ADAM_LR * ((mn / c1) / (jnp.sqrt(vn / c2) + ADAM_EPS) + ADAM_WD * w_ref[...])


def _adamw(w, gs, m, v, name, layer=None, prev=None):
    R, Cc = w.shape[-2:]
    S = gs.shape[0]
    tr = R
    if S * R * Cc * 4 > (4 << 20):
        for cand in (256, 128, 64, 32, 16, 8):
            if R % cand == 0 and R > cand:
                tr = cand
                break

    def body(w_ref, g_ref, m_ref, v_ref, *rest):
        g = g_ref[0].astype(f32)
        for s in range(1, S):
            g = g + g_ref[s].astype(f32)
        _adam_tile(g, w_ref, m_ref, v_ref, *rest[-4:])

    if layer is None:
        blk = pl.BlockSpec((tr, Cc), lambda i: (i, 0))
    else:
        blk = pl.BlockSpec((None, tr, Cc), lambda i: (layer, i, 0))
    out = jax.ShapeDtypeStruct(w.shape, f32)
    carried = [] if prev is None else list(prev)
    return pl.pallas_call(
        body, grid=(R // tr,),
        in_specs=[blk, pl.BlockSpec((S, tr, Cc), lambda i: (0, i, 0)), blk, blk]
        + [pl.BlockSpec(memory_space=pl.ANY)] * len(carried),
        out_specs=[blk] * 4, out_shape=[out] * 4,
        input_output_aliases={4 + j: j for j in range(len(carried))},
        compiler_params=_cparams(("parallel",)), name=name,
    )(w, gs, m, v, *carried)


def _adamw_column_major(w, gs, m, v, name):
    _, R, Cc = w.shape
    S = gs.shape[0]
    q = R // 128
    dense = lambda a: jnp.transpose(a, (2, 0, 1)).reshape(Cc * q, 128)
    back = lambda a: jnp.transpose(a.reshape(Cc, q, 128), (1, 2, 0)).reshape(1, R, Cc)

    def body(w_ref, g_ref, m_ref, v_ref, go_ref, d_ref, mo_ref, vo_ref, gt):
        for i in range(q):
            g = g_ref[0, i * 128:(i + 1) * 128, :].astype(f32)
            for s in range(1, S):
                g = g + g_ref[s, i * 128:(i + 1) * 128, :].astype(f32)
            gt[pl.ds(i, 128, stride=q), :] = g.T
        _adam_tile(gt[...], w_ref, m_ref, v_ref, go_ref, d_ref, mo_ref, vo_ref)

    blk = pl.BlockSpec((128 * q, 128), lambda j: (j, 0))
    outs = pl.pallas_call(
        body, grid=(pl.cdiv(Cc, 128),),
        in_specs=[blk, pl.BlockSpec((S, R, 128), lambda j: (0, 0, j)), blk, blk],
        out_specs=[blk] * 4, out_shape=[jax.ShapeDtypeStruct((Cc * q, 128), f32)] * 4,
        scratch_shapes=[pltpu.VMEM((128 * q, 128), f32)],
        compiler_params=_cparams(("parallel",)), name=name,
    )(dense(w), gs, dense(m), dense(v))
    return [back(o) for o in outs]


def _me():
    x, y, c = lax.axis_index("x"), lax.axis_index("y"), lax.axis_index("c")
    return x, y, c, 4 * x + 2 * y + c


def _peer(k):
    x, y, c, _ = _me()
    px = 1 - x if k & 4 else x
    py = 1 - y if k & 2 else y
    pc = 1 - c if k & 1 else c
    return (px, py, pc), 4 * px + 2 * py + pc


_HBM = pl.BlockSpec(memory_space=pltpu.HBM)
_SEM = pl.BlockSpec(memory_space=pltpu.SEMAPHORE)
_EFFECT = pltpu.SideEffectType.DATAFLOW_SIDE_EFFECTING


def _copy(src, land, ssem, rsem, k, blocked, landing_slot_of_peer):
    pid, pidx = _peer(k)
    slot = pidx if landing_slot_of_peer else _me()[3]
    return pltpu.make_async_remote_copy(src_ref=src.at[pidx] if blocked else src, dst_ref=land.at[slot],
                                        send_sem=ssem.at[k - 1], recv_sem=rsem.at[k - 1], device_id=pid, device_id_type=MESH)


def _own_copy(src, land, rsem, blocked):
    me = _me()[3]
    return pltpu.make_async_copy(src.at[me] if blocked else src, land.at[me], rsem.at[N_DEV - 1])


_ALL_PEERS = tuple(range(1, N_DEV))
_SAME_CORE_PEERS = (1, 2, 4, 6)
_OTHER_CORE_PEERS = (3, 5, 7)


def _relay_copies(land, ssem, rsem):
    sibling, _ = _peer(1)
    copies = []
    for i in range(len(land)):
        for j, k in enumerate(_OTHER_CORE_PEERS):
            sems = dict(send_sem=ssem.at[3 * i + j], recv_sem=rsem.at[3 * i + j], device_id=sibling, device_id_type=MESH)
            _, outgoing = _peer(k - 1)
            _, incoming = _peer(k)
            copies.append((pltpu.make_async_remote_copy(src_ref=land[i].at[outgoing], dst_ref=land[i].at[outgoing], **sems),
                           pltpu.make_async_remote_copy(src_ref=land[i].at[incoming], dst_ref=land[i].at[incoming], **sems)))
    return copies


def _relay_start(lands, name):
    n = len(lands)

    def body(*refs):
        for send, _ in _relay_copies(refs[:n], refs[n], refs[n + 1]):
            send.start()
        refs[-1][...] = jnp.zeros_like(refs[-1])

    sem = pltpu.SemaphoreType.DMA((3 * n,))
    res = pl.pallas_call(
        body, name=name,
        out_shape=(sem, sem) + tuple(pltpu.HBM(a.shape, a.dtype) for a in lands) + (jax.ShapeDtypeStruct((8, 128), f32),),
        in_specs=(_HBM,) * n, out_specs=(_SEM, _SEM) + (_HBM,) * n + (pl.BlockSpec(memory_space=pltpu.VMEM),),
        input_output_aliases={i: 2 + i for i in range(n)},
        compiler_params=pltpu.CompilerParams(has_side_effects=_EFFECT),
    )(*lands)
    return res[:-1], res[-1]


def _relay_wait(handle, after, name):
    ssem, rsem, lands = handle[0], handle[1], handle[2:]
    n = len(lands)
    after = tuple(after) if isinstance(after, (tuple, list)) else (after,)

    def body(*refs):
        for send, recv in _relay_copies(refs[:n], refs[n], refs[n + 1]):
            send.wait_send()
            recv.wait_recv()

    return pl.pallas_call(
        body, name=name, out_shape=tuple(pltpu.HBM(a.shape, a.dtype) for a in lands),
        in_specs=(_HBM,) * n + (_SEM, _SEM) + (pl.BlockSpec(memory_space=pl.ANY),) * len(after), out_specs=(_HBM,) * n,
        input_output_aliases={i: i for i in range(n)}, compiler_params=pltpu.CompilerParams(has_side_effects=_EFFECT),
    )(*lands, ssem, rsem, *after)


def _send_start(srcs, blocked, name, relayed=()):
    n = len(srcs)
    lands = [lax.empty(a.shape if blocked else (N_DEV,) + a.shape, a.dtype) for a in srcs]

    def body(*refs):
        src, land, sems, token = refs[:n], refs[n:2 * n], refs[2 * n:4 * n], refs[-1]
        for i in range(n):
            for k in (_SAME_CORE_PEERS if i in relayed else _ALL_PEERS):
                _copy(src[i], land[i], sems[2 * i], sems[2 * i + 1], k, blocked, False).start()
        for i in range(n):
            _own_copy(src[i], land[i], sems[2 * i + 1], blocked).start()
        token[...] = jnp.zeros_like(token)

    sems = (pltpu.SemaphoreType.DMA((N_DEV - 1,)), pltpu.SemaphoreType.DMA((N_DEV,)))
    hbm = lambda a: pltpu.with_memory_space_constraint(a, pltpu.HBM)
    res = pl.pallas_call(
        body, name=name,
        out_shape=sems * n + tuple(pltpu.HBM(a.shape, a.dtype) for a in srcs + lands)
        + (jax.ShapeDtypeStruct((8, 128), f32),),
        in_specs=(_HBM,) * (2 * n),
        out_specs=(_SEM,) * (2 * n) + (_HBM,) * (2 * n) + (pl.BlockSpec(memory_space=pltpu.VMEM),),
        input_output_aliases={j: 2 * n + j for j in range(2 * n)},
        compiler_params=pltpu.CompilerParams(has_side_effects=_EFFECT),
    )(*[hbm(a) for a in srcs], *[hbm(a) for a in lands])
    handles = [(res[2 * i], res[2 * i + 1], res[2 * n + i], res[3 * n + i]) for i in range(n)]
    return handles, res[-1]


def _send_wait(handle, blocked, after, name, relayed=False):
    ssem, rsem, src, land = handle
    after = tuple(after) if isinstance(after, (tuple, list)) else (after,)

    def body(src_ref, land_ref, ssem_ref, rsem_ref, *rest):
        for k in (_SAME_CORE_PEERS if relayed else _ALL_PEERS):
            cp = _copy(src_ref, land_ref, ssem_ref, rsem_ref, k, blocked, True)
            cp.wait_send()
            cp.wait_recv()
        _own_copy(src_ref, land_ref, rsem_ref, blocked).wait()

    return pl.pallas_call(
        body, name=name, out_shape=(pltpu.HBM(src.shape, src.dtype), pltpu.HBM(land.shape, land.dtype)),
        in_specs=(_HBM, _HBM, _SEM, _SEM) + (pl.BlockSpec(memory_space=pl.ANY),) * len(after), out_specs=(_HBM, _HBM),
        input_output_aliases={0: 0, 1: 1}, compiler_params=pltpu.CompilerParams(has_side_effects=_EFFECT),
    )(src, land, ssem, rsem, *after)


def _block_diag(w):
    nb, bs = w.shape[0], w.shape[1]
    eye = jnp.eye(nb, dtype=w.dtype)
    return (eye[:, None, :, None] * w[:, :, None, :]).reshape(nb * bs, nb * bs)


_SQUARE_TILES = dict(tm=1024, tn=1024, tk=1024)


def _mlp_fwd(x, hm, wu, wd, tag, epilogue, extras, outs, between=None):
    (r,) = _matmul(hm, wu, "nn", outs=[bf16], epilogue=lambda acc: (jnp.maximum(acc, 0.0),), name=f"mlp_up_{tag}")
    res = _matmul(r, wd, "nn", outs=outs, extras=(x,) + tuple(extras), epilogue=epilogue, a_map=jnp.square,
                  after=() if between is None else (between(r),), name=f"mlp_down_{tag}", **_SQUARE_TILES)
    return res, (hm, r)


def _mlp_bwd(x, nw, wu, wd, saved, dxo, dxo_b, tag, sink, urgent):
    hm, r = saved
    (du,) = _matmul(dxo_b, wd, "nt", outs=[bf16], extras=(r,), epilogue=lambda acc, rr: (acc * (2.0 * rr.astype(f32)),),
                    name=f"mlp_dact_{tag}")
    (dwd,) = _matmul(r, dxo_b, "tn", outs=[bf16], a_map=jnp.square, name=f"mlp_dwd_{tag}", **_SQUARE_TILES)
    down = {f"w_down{tag}": dwd.reshape(N_DEV, D_FF // N_DEV, D_MODEL)}
    if urgent:
        tok = sink(down)
        (dwu,) = _matmul(hm, du, "tn", outs=[bf16], shard_cols=2, after=(tok,), name=f"mlp_dwu_{tag}")
        tok = sink({f"w_up{tag}": dwu})
    else:
        (dwu,) = _matmul(hm, du, "tn", outs=[bf16], shard_cols=2, name=f"mlp_dwu_{tag}")
        tok = sink({f"w_up{tag}": dwu, **down})
    return _matmul(du, wu, "nt", outs=_RMS_BWD_OUTS, extras=(x, dxo, nw), epilogue=_rms_bwd_ep, after=(tok,),
                   name=f"mlp_dh_{tag}", **_SQUARE_TILES)


def _local_step(x, tgt, P, weight, sink, ahead):
    T = x.shape[0]
    cos, sin = _rope_tables(T)
    rtab = _ret_tables()
    row = lambda a: a.reshape(1, -1)
    mix_nw, mlp_nw = P["mixer_norm_w"], P["mlp_norm_w"]
    wr_bd, wi_bd = _block_diag(P["lru_w_r"]), _block_diag(P["lru_w_i"])
    lru_b, lru_br, lru_bi, lru_lam = row(P["lru_conv_b"]), row(P["lru_b_r"]), row(P["lru_b_i"]), row(P["lru_lambda"])
    pad16 = lambda a: jnp.pad(a.reshape(1, GDN_HEADS), ((0, 0), (GDN_HEADS, 128 - 2 * GDN_HEADS)))
    alog, dtb = pad16(P["gdn_a_log"]), pad16(P["gdn_dt_bias"])
    gnw = row(P["gdn_norm_w"])

    x0 = x
    h0 = _rms_fwd(x0, mix_nw[0:1], "rms_mix_0")
    w_ie = weight("w_in_even", (h0, wr_bd, wi_bd))
    (pe,) = _matmul(h0, w_ie, "nn", outs=[f32], name="in_even")
    mix0, o_ret, s_ret = _ret_fwd(pe, cos, sin, rtab, "ret_fwd")
    w_lc = weight("lru_conv_w", pe)
    xc = _conv_fwd(pe, 4, w_lc, lru_b, False, "lru_conv_fwd")
    mix0, h_lru = _lru_fwd(xc, pe, 20, wr_bd, wi_bd, lru_br, lru_bi, lru_lam, mix0, "lru_fwd")
    w_oe = weight("w_out_even", mix0)
    tok = ahead(("w_up0", "w_down0"), mix0)
    x1, hm0 = _matmul(mix0, w_oe, "nn", outs=[f32, bf16], extras=(x0, mlp_nw[0:1]), epilogue=_residual_rms_ep,
                      after=(tok,), name="out_even", tm=1024, tn=D_MODEL)
    w_u0, w_d0 = weight("w_up0", x1), weight("w_down0", x1)
    (x2, h1), mlp0 = _mlp_fwd(x1, hm0, w_u0, w_d0, "0", _residual_rms_ep, (mix_nw[1:2],), [f32, bf16],
                              between=lambda r: ahead(("w_in_odd",), r))
    w_io = weight("w_in_odd", h1)
    (po,) = _matmul(h1, w_io, "nn", outs=[f32], tm=2048, tn=ODD_PAD // 3, name="in_odd")
    w_gc = weight("gdn_conv_w", po)
    qkv = _conv_fwd(po, 0, w_gc, None, True, "gdn_conv_fwd")
    y_gdn, s_gdn, ti_gdn, vn_gdn, o_gdn = _gdn_fwd(qkv, po, alog, dtb, gnw, "gdn_fwd")
    w_oo = weight("w_out_odd", y_gdn)
    x3, hm1 = _matmul(y_gdn, w_oo, "nn", outs=[f32, bf16], extras=(x2, mlp_nw[1:2]), epilogue=_residual_rms_ep,
                      name="out_odd", tm=1024, tn=D_MODEL)
    w_u1, w_d1 = weight("w_up1", x3), weight("w_down1", x3)
    (loss, dx4, dx4_b, d_final), mlp1 = _mlp_fwd(x3, hm1, w_u1, w_d1, "1", _loss_ep, (row(P["final_norm_w"]), tgt),
                                                 _LOSS_OUTS)
    dx3, dx3_b, d_mlp_nw1 = _mlp_bwd(x3, mlp_nw[1:2], w_u1, w_d1, mlp1, dx4, dx4_b, "1", sink, False)
    (dy_gdn,) = _matmul(dx3_b, w_oo, "nt", outs=[f32], name="out_odd_dx")
    (d_woo,) = _matmul(y_gdn, dx3_b, "tn", outs=[bf16], name="out_odd_dw")
    dqkv, dpo, dba, d_alog, d_dtb, d_gnw = _gdn_bwd(qkv, po, alog, dtb, gnw, s_gdn, ti_gdn, vn_gdn, o_gdn, dy_gdn,
                                                  "gdn_bwd")
    dpo, d_gconv, _ = _conv_bwd(po, 0, w_gc, None, True, dqkv, dpo, "gdn_conv_bwd")
    dpo = lax.dynamic_update_slice(dpo, dba, (0, 4 * D_MODEL))
    (d_wio,) = _matmul(h1, dpo, "tn", outs=[bf16], tn=ODD_PAD // 3, name="in_odd_dw")
    tok = sink(dict(w_out_odd=d_woo.reshape(N_DEV, D_MODEL // N_DEV, D_MODEL), w_in_odd=_odd_split(d_wio, "w_in_odd_split")))
    dx2, dx2_b, d_mix_nw1 = _matmul(dpo, w_io, "nt", outs=_RMS_BWD_OUTS, extras=(x2, dx3, mix_nw[1:2]),
                                    epilogue=_rms_bwd_ep, after=(tok,), tm=1024, tn=1024, tk=ODD_PAD // 3, name="in_odd_dx")
    dx1, dx1_b, d_mlp_nw0 = _mlp_bwd(x1, mlp_nw[0:1], w_u0, w_d0, mlp0, dx2, dx2_b, "0", sink, True)
    (d_woe,) = _matmul(mix0, dx1_b, "tn", outs=[bf16], name="out_even_dw")
    tok = sink(dict(w_out_even=d_woe.reshape(N_DEV, D_MODEL // N_DEV, D_MODEL)))
    (dmix0,) = _matmul(dx1_b, w_oe, "nt", outs=[f32], name="out_even_dx")
    dpe = _ret_bwd(pe, cos, sin, rtab, o_ret, s_ret, dmix0, "ret_bwd")
    dxc, dpe, d_wr, d_wi, d_br, d_bi, d_lam = _lru_bwd(xc, pe, 20, wr_bd, wi_bd, lru_br, lru_bi, lru_lam + tok[0:1, 0:1],
                                                       h_lru, dmix0, 4, dpe, "lru_bwd")
    dpe, d_lconv, d_lconv_b = _conv_bwd(pe, 4, w_lc, lru_b, False, dxc, dpe, "lru_conv_bwd")
    G = dict(
        mlp_norm_w=jnp.concatenate([d_mlp_nw0, d_mlp_nw1], axis=0),
        final_norm_w=d_final.reshape(-1),
        lru_conv_w=d_lconv, lru_conv_b=d_lconv_b.reshape(-1),
        lru_w_r=d_wr, lru_b_r=d_br.reshape(-1), lru_w_i=d_wi, lru_b_i=d_bi.reshape(-1),
        lru_lambda=d_lam.reshape(-1), gdn_conv_w=d_gconv,
        gdn_a_log=d_alog[0, GDN_HEADS:2 * GDN_HEADS], gdn_dt_bias=d_dtb[0, GDN_HEADS:2 * GDN_HEADS],
        gdn_norm_w=d_gnw.reshape(-1),
    )
    packed = _pack([G[k] for k in _SMALL] + [d_lconv, d_gconv, loss[0, 0:1]])
    tok = sink(dict(small=jnp.broadcast_to(packed[None], (N_DEV,) + packed.shape)))
    (d_wie,) = _matmul(h0, dpe, "tn", outs=[bf16], shard_cols=2, after=(tok,), name="in_even_dw")
    tok = sink(dict(w_in_even=d_wie))
    dx0, _, d_mix_nw0 = _matmul(dpe, w_ie, "nt", outs=_RMS_BWD_OUTS, extras=(x0, dx1, mix_nw[0:1]), epilogue=_rms_bwd_ep,
                                after=(tok,), name="in_even_dx", **_SQUARE_TILES)
    G["mixer_norm_w"] = jnp.concatenate([d_mix_nw0, d_mix_nw1], axis=0)
    return loss, dx0, G


_SMALL = ["mlp_norm_w", "final_norm_w", "lru_conv_b", "lru_w_r", "lru_b_r", "lru_w_i", "lru_b_i",
          "lru_lambda", "gdn_a_log", "gdn_dt_bias", "gdn_norm_w"]
_PACK_ROWS = 688


def _pack(parts):
    flat = jnp.concatenate([p.reshape(-1) for p in parts])
    return jnp.pad(flat, (0, _PACK_ROWS * 128 - flat.shape[0])).reshape(_PACK_ROWS, 128)


def _unpack(packed, shapes):
    flat = packed.reshape(-1)
    out, off = [], 0
    for s in shapes:
        n = int(np.prod(s))
        out.append(flat[off:off + n].reshape(s))
        off += n
    return out


def kernel(x, mixer_norm_w, mlp_norm_w, final_norm_w, w_in_even, lru_conv_w, lru_conv_b, lru_w_r, lru_b_r, lru_w_i, lru_b_i, lru_lambda, w_out_even, w_in_odd, gdn_conv_w, gdn_a_log, gdn_dt_bias, gdn_norm_w, w_out_odd, w_up, w_down, loss_target, m_mixer_norm_w, m_mlp_norm_w, m_final_norm_w, m_w_in_even, m_lru_conv_w, m_lru_conv_b, m_lru_w_r, m_lru_b_r, m_lru_w_i, m_lru_b_i, m_lru_lambda, m_w_out_even, m_w_in_odd, m_gdn_conv_w, m_gdn_a_log, m_gdn_dt_bias, m_gdn_norm_w, m_w_out_odd, m_w_up, m_w_down, v_mixer_norm_w, v_mlp_norm_w, v_final_norm_w, v_w_in_even, v_lru_conv_w, v_lru_conv_b, v_lru_w_r, v_lru_b_r, v_lru_w_i, v_lru_b_i, v_lru_lambda, v_w_out_even, v_w_in_odd, v_gdn_conv_w, v_gdn_a_log, v_gdn_dt_bias, v_gdn_norm_w, v_w_out_odd, v_w_up, v_w_down):
    Pw = dict(mixer_norm_w=mixer_norm_w, mlp_norm_w=mlp_norm_w, final_norm_w=final_norm_w, w_in_even=w_in_even,
              lru_conv_w=lru_conv_w, lru_conv_b=lru_conv_b, lru_w_r=lru_w_r, lru_b_r=lru_b_r, lru_w_i=lru_w_i,
              lru_b_i=lru_b_i, lru_lambda=lru_lambda, w_out_even=w_out_even, w_in_odd=w_in_odd, gdn_conv_w=gdn_conv_w,
              gdn_a_log=gdn_a_log, gdn_dt_bias=gdn_dt_bias, gdn_norm_w=gdn_norm_w, w_out_odd=w_out_odd, w_up=w_up,
              w_down=w_down)
    Pm = dict(mixer_norm_w=m_mixer_norm_w, mlp_norm_w=m_mlp_norm_w, final_norm_w=m_final_norm_w, w_in_even=m_w_in_even,
              lru_conv_w=m_lru_conv_w, lru_conv_b=m_lru_conv_b, lru_w_r=m_lru_w_r, lru_b_r=m_lru_b_r, lru_w_i=m_lru_w_i,
              lru_b_i=m_lru_b_i, lru_lambda=m_lru_lambda, w_out_even=m_w_out_even, w_in_odd=m_w_in_odd,
              gdn_conv_w=m_gdn_conv_w, gdn_a_log=m_gdn_a_log, gdn_dt_bias=m_gdn_dt_bias, gdn_norm_w=m_gdn_norm_w,
              w_out_odd=m_w_out_odd, w_up=m_w_up, w_down=m_w_down)
    Pv = dict(mixer_norm_w=v_mixer_norm_w, mlp_norm_w=v_mlp_norm_w, final_norm_w=v_final_norm_w, w_in_even=v_w_in_even,
              lru_conv_w=v_lru_conv_w, lru_conv_b=v_lru_conv_b, lru_w_r=v_lru_w_r, lru_b_r=v_lru_b_r, lru_w_i=v_lru_w_i,
              lru_b_i=v_lru_b_i, lru_lambda=v_lru_lambda, w_out_even=v_w_out_even, w_in_odd=v_w_in_odd,
              gdn_conv_w=v_gdn_conv_w, gdn_a_log=v_gdn_a_log, gdn_dt_bias=v_gdn_dt_bias, gdn_norm_w=v_gdn_norm_w,
              w_out_odd=v_w_out_odd, w_up=v_w_up, w_down=v_w_down)
    me = _me()[3]
    T = x.shape[1]

    cols = lambda g: jnp.transpose(g, (1, 0, 2)).reshape(g.shape[1], -1)
    rows = lambda g: g.reshape(-1, g.shape[2])
    wide = lambda g: _odd_assemble(g, "w_in_odd_assemble")
    as_is = lambda g: g
    relay_groups = (("w_in_even",), ("w_up0", "w_down0"), ("w_in_odd",))
    relayed = sum(relay_groups, ())
    (first,), started = _send_start([w_in_even[0].astype(bf16)], False, "gather_start_w_in_even", relayed=[0])
    cast = lambda a: (a + started[0:1, 0:1]).astype(bf16)
    odd_shard = jnp.pad(cast(w_in_odd[0]), ((0, 0), (0, ODD_SHARD_PAD - ODD_SHARD)))
    gather = dict(
        w_in_even=(None, cols), lru_conv_w=(lru_conv_w[0], cols),
        w_out_even=(cast(w_out_even[0]), rows), w_up0=(cast(w_up[0]), as_is), w_down0=(cast(w_down[0]), rows),
        w_in_odd=(odd_shard, wide), gdn_conv_w=(gdn_conv_w[0], cols),
        w_out_odd=(cast(w_out_odd[0]), rows), w_up1=(cast(w_up[1]), as_is), w_down1=(cast(w_down[1]), rows))
    later = [name for name in gather if name != "w_in_even"]
    handles, tok = _send_start([gather[name][0] for name in later], False, "gather_start",
                               relayed=[i for i, name in enumerate(later) if name in relayed])
    handles = dict(zip(later, handles), w_in_even=first)
    relays, landed, full = {}, {}, {}

    def ahead(group, after):
        lands = [_send_wait(handles[n], False, after, f"gather_wait_{n}", relayed=True)[1] for n in group]
        relays[group], token = _relay_start(lands, "relay_start_" + "_".join(group))
        return token

    def weight(name, after):
        if name not in landed and name in relayed:
            group = next(g for g in relay_groups if name in g)
            if group not in relays:
                ahead(group, after)
            landed.update(zip(group, _relay_wait(relays[group], after, "relay_wait_" + "_".join(group))))
        elif name not in landed:
            landed[name] = _send_wait(handles[name], False, after, f"gather_wait_{name}")[1]
        if name not in full:
            full[name] = gather[name][1](landed[name])
        return full[name]

    P = {k: Pw[k] for k in ("mlp_norm_w", "final_norm_w")}
    P["mixer_norm_w"] = mixer_norm_w + tok[0:1, 0:1]
    for k in ("lru_w_r", "lru_w_i", "lru_conv_b", "lru_b_r", "lru_b_i", "lru_lambda", "gdn_a_log", "gdn_dt_bias", "gdn_norm_w"):
        P[k] = Pw[k][0]

    sent = {}

    def sink(grads):
        hs, token = _send_start(list(grads.values()), True, "grads_start_" + "_".join(grads))
        sent.update(zip(grads, hs))
        return token

    loss, dx, G = _local_step(x[0], loss_target[0], P, weight, sink, ahead)
    lanes = lambda a: a.reshape(-1, 128)
    sink(dict(mixer_norm_w=jnp.broadcast_to(lanes(G["mixer_norm_w"])[None], (N_DEV, 2 * D_MODEL // 128, 128))))

    def received(name, after=dx):
        return _send_wait(sent[name], True, after, f"grads_wait_{name}")[1]

    out = {}
    nff = D_FF // N_DEV

    def whole(name, gs):
        out[name] = tuple(_adamw(Pw[name], gs, Pm[name], Pv[name], f"adamw_{name}", layer=0))

    def layers(name):
        res = None
        for l in range(2):
            res = _adamw(Pw[name], received(f"{name}{l}"), Pm[name], Pv[name], f"adamw_{name}{l}", layer=l, prev=res)
        out[name] = tuple(res)

    layers("w_up")
    layers("w_down")
    whole("w_out_odd", received("w_out_odd"))
    out["w_in_odd"] = tuple(_adamw_column_major(w_in_odd, received("w_in_odd"), m_w_in_odd, v_w_in_odd, "adamw_w_in_odd"))
    whole("w_out_even", received("w_out_even"))
    small_shapes = [Pw[k].shape for k in _SMALL]
    pw, pm, pv = (_pack([Q[k] for k in _SMALL]) for Q in (Pw, Pm, Pv))
    sg, sd, sm, sv = _adamw(pw, received("small", out["w_out_even"][1]), pm, pv, "adamw_small")
    for arrs_i, packed_out in enumerate((sg, sd, sm, sv)):
        for k, a in zip(_SMALL, _unpack(packed_out, small_shapes)):
            out.setdefault(k, [None] * 4)[arrs_i] = a
    whole("w_in_even", received("w_in_even", sd))
    out["mixer_norm_w"] = tuple(
        a.reshape(mixer_norm_w.shape) for a in
        _adamw(lanes(mixer_norm_w), received("mixer_norm_w", out["w_in_even"][1]), lanes(m_mixer_norm_w),
               lanes(v_mixer_norm_w), "adamw_mixer_norm_w"))
    n_small = sum(int(np.prod(s)) for s in small_shapes)
    gflat = sg.reshape(-1)
    g_lconv = gflat[n_small:n_small + CONV_K * LRU_WIDTH].reshape(CONV_K, LRU_WIDTH)
    g_gconv = gflat[n_small + CONV_K * LRU_WIDTH:n_small + CONV_K * (LRU_WIDTH + 3072)].reshape(CONV_K, 3072)
    whole("lru_conv_w", lax.dynamic_slice_in_dim(g_lconv, me * 64, 64, axis=1)[None])
    whole("gdn_conv_w", lax.dynamic_slice_in_dim(g_gconv, me * 384, 384, axis=1)[None])

    names = ["mixer_norm_w", "mlp_norm_w", "final_norm_w", "w_in_even", "lru_conv_w", "lru_conv_b", "lru_w_r", "lru_b_r",
             "lru_w_i", "lru_b_i", "lru_lambda", "w_out_even", "w_in_odd", "gdn_conv_w", "gdn_a_log", "gdn_dt_bias",
             "gdn_norm_w", "w_out_odd", "w_up", "w_down"]
    total = gflat[n_small + CONV_K * (LRU_WIDTH + 3072)]
    res = [total, dx[None]]
    for j in range(4):
        res += [out[k][j] for k in names]
    return tuple(res)
```

```python
import math

import numpy as np
import jax
import jax.numpy as jnp
from jax import lax
from jax.experimental import pallas as pl
from jax.experimental.pallas import tpu as pltpu

f32 = jnp.float32
bf16 = jnp.bfloat16

N_DEV = 8
D_MODEL = 1024
D_FF = 4096
EPS = 1e-6
RET_HEADS = 4
RET_CHUNK = 128
RET_STEP = 4
ROPE_THETA = 10000.0
LRU_WIDTH = 512
LRU_C = 8.0
GDN_HEADS = 8
GDN_CHUNK = 64
GDN_STEP = 4
HEAD_DIM = 128
ODD_IN = 4112
ODD_PAD = 4224
ODD_SHARD = ODD_IN // N_DEV
ODD_SHARD_PAD = 640
ADAM_LR, ADAM_B1, ADAM_B2, ADAM_EPS, ADAM_WD, ADAM_STEP = 0.001, 0.9, 0.999, 1e-08, 0.01, 10
VMEM_LIMIT = 56 * 1024 * 1024

_NN = (((1,), (0,)), ((), ()))
_NT = (((1,), (1,)), ((), ()))
_TN = (((0,), (0,)), ((), ()))
MESH = pl.DeviceIdType.MESH


def _cparams(sem):
    return pltpu.CompilerParams(dimension_semantics=sem, vmem_limit_bytes=VMEM_LIMIT)


def _dot(a, b, dn):
    return lax.dot_general(a.astype(bf16), b.astype(bf16), dn, preferred_element_type=f32)


def _dot01(a01, b, dn):
    a = a01.astype(bf16)
    b0 = b.astype(bf16)
    r1 = b - b0.astype(f32)
    b1 = r1.astype(bf16)
    b2 = (r1 - b1.astype(f32)).astype(bf16)
    d = lambda q: lax.dot_general(a, q, dn, preferred_element_type=f32)
    return d(b0) + (d(b1) + d(b2))


def _sigmoid(x):
    return jax.nn.sigmoid(x)


def _silu(x):
    return x * _sigmoid(x)


def _dsilu(x):
    s = _sigmoid(x)
    return s * (1.0 + x * (1.0 - s))


def _softplus(x):
    return jnp.maximum(x, 0.0) + jnp.log1p(jnp.exp(-jnp.abs(x)))


_GELU_C = math.sqrt(2.0 / math.pi)


def _gelu(y):
    return 0.5 * y * (1.0 + jnp.tanh(_GELU_C * (y + 0.044715 * y * y * y)))


def _dgelu(y):
    t = jnp.tanh(_GELU_C * (y + 0.044715 * y * y * y))
    return 0.5 * (1.0 + t) + 0.5 * y * (1.0 - t * t) * _GELU_C * (1.0 + 3.0 * 0.044715 * y * y)


def _matmul(a, b, form, *, outs, name, epilogue=None, extras=(), tm=4096, tn=512, tk=1024, shard_cols=0, a_map=None,
            after=()):
    if form == "tn":
        K, M = a.shape
    else:
        M, K = a.shape
    per_step = 1
    if b.ndim == 3:
        assert form in ("nn", "nt"), name
        N = b.shape[1] if form == "nt" else N_DEV * b.shape[2]
        if form == "nn":
            tn = b.shape[2]
        else:
            per_step = max(1, tk // b.shape[2])
            tk = per_step * b.shape[2]
    else:
        N = b.shape[0] if form == "nt" else b.shape[1]
    ns = N // N_DEV
    if shard_cols:
        tn = ns * shard_cols
    tm, tn, tk = min(tm, M), min(tn, N), min(tk, K)
    assert M % tm == 0 and N % tn == 0 and K % tk == 0, (name, M, N, K, tm, tn, tk)
    nk = K // tk
    dn = {"nn": _NN, "nt": _NT, "tn": _TN}[form]
    if form == "tn":
        a_spec = pl.BlockSpec((tk, tm), lambda i, j, k: (k, i))
    else:
        a_spec = pl.BlockSpec((tm, tk), lambda i, j, k: (i, k))
    if b.ndim == 3:
        b_spec = (pl.BlockSpec((per_step, tn, tk // per_step), lambda i, j, k: (k, j, 0)) if form == "nt"
                  else pl.BlockSpec((None, tk, tn), lambda i, j, k: (j, k, 0)))
    elif form == "nt":
        b_spec = pl.BlockSpec((tn, tk), lambda i, j, k: (j, k))
    else:
        b_spec = pl.BlockSpec((tk, tn), lambda i, j, k: (k, j))
    e_spec = pl.BlockSpec((tm, tn), lambda i, j, k: (i, j))
    v_spec = pl.BlockSpec((1, tn), lambda i, j, k: (0, j))
    if shard_cols:
        o_spec = pl.BlockSpec((shard_cols, tm, ns), lambda i, j, k: (j, i, 0))
        o_shape = (N_DEV, M, ns)
    else:
        o_spec = e_spec
        o_shape = (M, N)
    n_ex = len(extras)
    n_in = 2 + n_ex + len(after)
    sums = [isinstance(o, tuple) for o in outs]
    assert not any(sums) or tn == N, name

    def finish(acc, ex, o_refs, row_tile):
        vals = (acc,) if epilogue is None else epilogue(acc, *[e[...] for e in ex])
        for r, v, is_sum in zip(o_refs, vals, sums):
            if is_sum:
                @pl.when(row_tile == 0)
                def _(r=r, v=v):
                    r[...] = v.astype(r.dtype)

                @pl.when(row_tile > 0)
                def _(r=r, v=v):
                    r[...] += v.astype(r.dtype)
            elif shard_cols:
                for s in range(shard_cols):
                    r[s] = v[:, s * ns:(s + 1) * ns].astype(r.dtype)
            else:
                r[...] = v.astype(r.dtype)

    def prod(a_ref, b_ref):
        if b.ndim == 3 and form == "nt":
            w = tk // per_step
            return sum(_dot(a_ref[:, s * w:(s + 1) * w], b_ref[s], dn) for s in range(1, per_step)) + _dot(a_ref[:, 0:w], b_ref[0], dn)
        av = a_ref[...]
        return _dot(av if a_map is None else a_map(av), b_ref[...], dn)

    def body_one(*refs):
        finish(prod(*refs[:2]), refs[2:2 + n_ex], refs[n_in:], pl.program_id(0))

    def body_acc(*refs):
        a_ref, b_ref = refs[:2]
        acc = refs[-1]
        k = pl.program_id(2)
        row_tile = pl.program_id(0)

        @pl.when(k == 0)
        def _():
            acc[...] = prod(a_ref, b_ref)

        @pl.when((k > 0) & (k < nk - 1))
        def _():
            acc[...] += prod(a_ref, b_ref)

        @pl.when(k == nk - 1)
        def _():
            finish(acc[...] + prod(a_ref, b_ref), refs[2:2 + n_ex], refs[n_in:-1], row_tile)

    return pl.pallas_call(
        body_one if nk == 1 else body_acc, grid=(M // tm, N // tn, nk),
        in_specs=[a_spec, b_spec] + [v_spec if e.shape[0] == 1 else e_spec for e in extras]
        + [pl.BlockSpec(memory_space=pl.ANY)] * len(after),
        out_specs=[v_spec if s else o_spec for s in sums],
        out_shape=[jax.ShapeDtypeStruct((1, N), o[1]) if s else jax.ShapeDtypeStruct(o_shape, o) for o, s in zip(outs, sums)],
        scratch_shapes=[] if nk == 1 else [pltpu.VMEM((tm, tn), f32)],
        compiler_params=_cparams(("arbitrary" if any(sums) else "parallel", "parallel", "arbitrary")), name=name,
    )(a, b, *extras, *after)


def _rms_fwd(x, w, name):
    T, D = x.shape
    tt = min(1024, T)

    def body(x_ref, w_ref, h_ref):
        xv = x_ref[...]
        r = lax.rsqrt(jnp.mean(xv * xv, axis=1, keepdims=True) + EPS)
        h_ref[...] = (xv * r * w_ref[...]).astype(bf16)

    return pl.pallas_call(
        body, grid=(T // tt,),
        in_specs=[pl.BlockSpec((tt, D), lambda i: (i, 0)), pl.BlockSpec((1, D), lambda i: (0, 0))],
        out_specs=pl.BlockSpec((tt, D), lambda i: (i, 0)),
        out_shape=jax.ShapeDtypeStruct((T, D), bf16),
        compiler_params=_cparams(("parallel",)), name=name,
    )(x, w)


def _residual_rms_ep(acc, res, w):
    x = res + acc
    r = lax.rsqrt(jnp.mean(x * x, axis=1, keepdims=True) + EPS)
    return x, x * r * w


_RMS_BWD_OUTS = [f32, bf16, ("sum", f32)]


def _rms_bwd_ep(dh, x, dres, w):
    r = lax.rsqrt(jnp.mean(x * x, axis=1, keepdims=True) + EPS)
    xn = x * r
    dhw = dh * w
    dx = dres + r * (dhw - xn * jnp.mean(dhw * xn, axis=1, keepdims=True))
    return dx, dx, jnp.sum(dh * xn, axis=0, keepdims=True)


_LOSS_OUTS = [("sum", f32), f32, bf16, ("sum", f32)]


def _loss_ep(acc, res, w, tgt):
    x = res + acc
    D = x.shape[1]
    r = lax.rsqrt(jnp.mean(x * x, axis=1, keepdims=True) + EPS)
    xn = x * r
    e = xn * w - tgt
    loss = 0.5 * jnp.sum(jnp.mean(e * e, axis=1, keepdims=True), axis=0, keepdims=True)
    dy = e * (1.0 / D)
    dyw = dy * w
    dx = r * (dyw - xn * jnp.mean(dyw * xn, axis=1, keepdims=True))
    return jnp.broadcast_to(loss, (1, D)), dx, dx, jnp.sum(dy * xn, axis=0, keepdims=True)


def _ret_tables():
    H, C = RET_HEADS, RET_CHUNK
    lg = np.log1p(-np.exp2(-5.0 - np.arange(H, dtype=np.float32))).astype(np.float32)
    idx = np.arange(C, dtype=np.float32)
    diff = idx[:, None] - idx[None, :]
    causal = diff >= 0
    dm = np.where(causal[None], np.exp(lg[:, None, None] * np.where(causal, diff, 0.0)[None]), 0.0)
    qd = np.exp(lg[:, None] * (idx[None, :] + 1.0))
    kd = np.exp(lg[:, None] * (C - 1.0 - idx[None, :]))
    cg = np.exp(lg * C)
    tab = np.zeros((H, 4, C, HEAD_DIM), np.float32)
    tab[:, 0] = dm
    tab[:, 1] = qd[:, :, None]
    tab[:, 2] = kd[:, :, None]
    tab[:, 3] = cg[:, None, None]
    return jnp.asarray(tab)


def _rope_tables(T):
    half = HEAD_DIM // 2
    inv = np.float32(ROPE_THETA) ** (-np.arange(half, dtype=np.float32) / np.float32(half))
    ang = np.arange(T, dtype=np.float32)[:, None] * inv[None, :]
    c, s = np.cos(ang), np.sin(ang)
    return jnp.asarray(np.concatenate([c, c], axis=1)), jnp.asarray(np.concatenate([-s, s], axis=1))


def _rope(x, cos, sin):
    return x * cos + pltpu.roll(x, HEAD_DIM // 2, 1) * sin


def _unrope(y, cos, sin):
    return y * cos + pltpu.roll(y * sin, HEAD_DIM // 2, 1)


def _stack_heads(ref, H, f=None):
    parts = [ref[:, h * HEAD_DIM:(h + 1) * HEAD_DIM] for h in range(H)]
    return jnp.stack(parts if f is None else [f(a) for a in parts])


def _ret_fwd(p, cos, sin, tab, name):
    T = p.shape[0]
    C, H = RET_CHUNK, RET_HEADS
    N = T // C
    K = min(RET_STEP, N)
    NS = N // K
    scale = HEAD_DIM ** -0.5

    def body(q_ref, k_ref, v_ref, g_ref, c_ref, s_ref, t_ref, y_ref, o_ref, sp_ref, st):
        @pl.when(pl.program_id(0) == 0)
        def _():
            st[...] = jnp.zeros_like(st)

        dm, qd, kd, cg = t_ref[:, 0], t_ref[:, 1], t_ref[:, 2], t_ref[:, 3]
        S = st[...]
        for c in range(K):
            rows = pl.ds(c * C, C)
            cos_, sin_ = c_ref[rows, :], s_ref[rows, :]
            rot = lambda a: _rope(a, cos_, sin_)
            q = _stack_heads(q_ref.at[rows, :], H, rot)
            k = _stack_heads(k_ref.at[rows, :], H, rot) * scale
            v = _stack_heads(v_ref.at[rows, :], H)
            P = _dot(q, k, _NT3) * dm
            o = _dot(P, v, _NN3) + _dot(q * qd, S, _NN3)
            sp_ref[c] = S
            S = cg * S + _dot(k * kd, v, _TN3)
            r = lax.rsqrt(jnp.mean(o * o, axis=2, keepdims=True) + EPS)
            y = o * r * _silu(_stack_heads(g_ref.at[rows, :], H))
            for h in range(H):
                o_ref[rows, h * HEAD_DIM:(h + 1) * HEAD_DIM] = o[h]
                y_ref[rows, h * HEAD_DIM:(h + 1) * HEAD_DIM] = y[h].astype(bf16)
        st[...] = S

    wide = lambda blk: pl.BlockSpec((K * C, H * HEAD_DIM), lambda n: (n, blk))
    tbl = pl.BlockSpec((K * C, HEAD_DIM), lambda n: (n, 0))
    return pl.pallas_call(
        body, grid=(NS,),
        in_specs=[wide(0), wide(1), wide(2), wide(3), tbl, tbl,
                  pl.BlockSpec((H, 4, C, HEAD_DIM), lambda n: (0, 0, 0, 0))],
        out_specs=[wide(0), wide(0), pl.BlockSpec((K, H, HEAD_DIM, HEAD_DIM), lambda n: (n, 0, 0, 0))],
        out_shape=[jax.ShapeDtypeStruct((T, D_MODEL), bf16), jax.ShapeDtypeStruct((T, H * HEAD_DIM), f32),
                   jax.ShapeDtypeStruct((N, H, HEAD_DIM, HEAD_DIM), f32)],
        scratch_shapes=[pltpu.VMEM((H, HEAD_DIM, HEAD_DIM), f32)],
        compiler_params=_cparams(("arbitrary",)), name=name,
    )(p, p, p, p, cos, sin, tab)


def _ret_bwd(p, cos, sin, tab, o_raw, sprev, dmix, name):
    T = p.shape[0]
    C, H = RET_CHUNK, RET_HEADS
    N = T // C
    K = min(RET_STEP, N)
    NS = N // K
    scale = HEAD_DIM ** -0.5
    W = H * HEAD_DIM

    def body(q_ref, k_ref, v_ref, g_ref, c_ref, s_ref, t_ref, o_ref, sp_ref, dy_ref, d_ref, dst):
        @pl.when(pl.program_id(0) == 0)
        def _():
            dst[...] = jnp.zeros_like(dst)

        dm, qd, kd, cg = t_ref[:, 0], t_ref[:, 1], t_ref[:, 2], t_ref[:, 3]
        dS1 = dst[...]
        for c in reversed(range(K)):
            rows = pl.ds(c * C, C)
            cos_, sin_ = c_ref[rows, :], s_ref[rows, :]
            rot = lambda a: _rope(a, cos_, sin_)
            q = _stack_heads(q_ref.at[rows, :], H, rot)
            k = _stack_heads(k_ref.at[rows, :], H, rot) * scale
            v = _stack_heads(v_ref.at[rows, :], H)
            g = _stack_heads(g_ref.at[rows, :], H)
            S = sp_ref[c]
            o = _stack_heads(o_ref.at[rows, :], H)
            dy = _stack_heads(dy_ref.at[rows, :], H)
            r = lax.rsqrt(jnp.mean(o * o, axis=2, keepdims=True) + EPS)
            nrm = o * r
            dn = dy * _silu(g)
            dg = dy * nrm * _dsilu(g)
            do = r * (dn - nrm * jnp.mean(dn * nrm, axis=2, keepdims=True))
            P = _dot(q, k, _NT3) * dm
            dP = _dot(do, v, _NT3) * dm
            dq = _dot(dP, k, _NN3) + _dot(do, S, _NT3) * qd
            dk = (_dot(dP, q, _TN3) + _dot(v, dS1, _NT3) * kd) * scale
            dv = _dot(P, do, _TN3) + _dot(k * kd, dS1, _NN3)
            dS1 = cg * dS1 + _dot(q * qd, do, _TN3)
            for h in range(H):
                d_ref[rows, h * HEAD_DIM:(h + 1) * HEAD_DIM] = _unrope(dq[h], cos_, sin_).astype(bf16)
                d_ref[rows, W + h * HEAD_DIM:W + (h + 1) * HEAD_DIM] = _unrope(dk[h], cos_, sin_).astype(bf16)
                d_ref[rows, 2 * W + h * HEAD_DIM:2 * W + (h + 1) * HEAD_DIM] = dv[h].astype(bf16)
                d_ref[rows, 3 * W + h * HEAD_DIM:3 * W + (h + 1) * HEAD_DIM] = dg[h].astype(bf16)
        dst[...] = dS1

    rev = lambda blk: pl.BlockSpec((K * C, W), lambda n: (NS - 1 - n, blk))
    tbl = pl.BlockSpec((K * C, HEAD_DIM), lambda n: (NS - 1 - n, 0))
    return pl.pallas_call(
        body, grid=(NS,),
        in_specs=[rev(0), rev(1), rev(2), rev(3), tbl, tbl,
                  pl.BlockSpec((H, 4, C, HEAD_DIM), lambda n: (0, 0, 0, 0)), rev(0),
                  pl.BlockSpec((K, H, HEAD_DIM, HEAD_DIM), lambda n: (NS - 1 - n, 0, 0, 0)), rev(0)],
        out_specs=pl.BlockSpec((K * C, 4 * W), lambda n: (NS - 1 - n, 0)),
        out_shape=jax.ShapeDtypeStruct((T, 6 * W), bf16),
        scratch_shapes=[pltpu.VMEM((H, HEAD_DIM, HEAD_DIM), f32)],
        compiler_params=_cparams(("arbitrary",)), name=name,
    )(p, p, p, p, cos, sin, tab, o_raw, sprev, dmix)


CONV_K = 4
CONV_W = 512


def _conv_block(C, col_off):
    cw = CONV_W if (C % (2 * CONV_W) or col_off % 2) else 2 * CONV_W
    return cw, col_off * CONV_W // cw
PAD = 8
SUB_R = 64


def _conv_fwd(x, col_off, w, b, act, name):
    T = x.shape[0]
    C = w.shape[1]
    cw, cb = _conv_block(C, col_off)
    G = C // cw
    tt = min(1024, T)
    NT = T // tt
    has_b = b is not None

    def body(*refs):
        if has_b:
            x_ref, w_ref, b_ref, y_ref, pad = refs
        else:
            x_ref, w_ref, y_ref, pad = refs
        t = pl.program_id(1)

        @pl.when(t == 0)
        def _():
            pad[pl.ds(0, PAD), :] = jnp.zeros((PAD, cw), f32)

        pad[pl.ds(PAD, tt), :] = x_ref[...]
        for g in range(cw // 128):
            ls = slice(g * 128, (g + 1) * 128)
            wv = w_ref[:, ls]
            for c in range(tt // SUB_R):
                r0 = c * SUB_R
                y = wv[0:1, :] * pad[pl.ds(PAD - 3 + r0, SUB_R), ls]
                for kk in range(1, CONV_K):
                    y = y + wv[kk:kk + 1, :] * pad[pl.ds(PAD - 3 + kk + r0, SUB_R), ls]
                if has_b:
                    y = y + b_ref[:, ls]
                y_ref[pl.ds(r0, SUB_R), ls] = _silu(y) if act else y
        tail = pad[pl.ds(tt, PAD), :]
        pad[pl.ds(0, PAD), :] = tail

    in_specs = [pl.BlockSpec((tt, cw), lambda g, t: (t, cb + g)),
                pl.BlockSpec((CONV_K, cw), lambda g, t: (0, g))]
    args = [x, w]
    if has_b:
        in_specs.append(pl.BlockSpec((1, cw), lambda g, t: (0, g)))
        args.append(b)
    return pl.pallas_call(
        body, grid=(G, NT), in_specs=in_specs,
        out_specs=pl.BlockSpec((tt, cw), lambda g, t: (t, g)),
        out_shape=jax.ShapeDtypeStruct((T, C), f32),
        scratch_shapes=[pltpu.VMEM((tt + PAD, cw), f32)],
        compiler_params=_cparams(("parallel", "arbitrary")), name=name,
    )(*args)


def _conv_bwd(x, col_off, w, b, act, dout, dp, name):
    T = x.shape[0]
    C = w.shape[1]
    cw, cb = _conv_block(C, col_off)
    G = C // cw
    tt = min(1024, T)
    NT = T // tt
    has_b = b is not None

    def body(*refs):
        if has_b:
            x_ref, xp_ref, w_ref, b_ref, d_ref, dp_in, dx_ref, dw_ref, db_ref, pad, dpad = refs
        else:
            x_ref, xp_ref, w_ref, d_ref, dp_in, dx_ref, dw_ref, db_ref, pad, dpad = refs
        t = pl.program_id(1)
        first_tile = t == NT - 1

        @pl.when(t == 0)
        def _():
            dpad[pl.ds(tt, PAD), :] = jnp.zeros((PAD, cw), f32)
            dw_ref[...] = jnp.zeros_like(dw_ref)
            db_ref[...] = jnp.zeros_like(db_ref)

        pad[pl.ds(0, PAD), :] = jnp.where(first_tile, 0.0, xp_ref[...])
        pad[pl.ds(PAD, tt), :] = x_ref[...]
        fold = lambda v: v.reshape(SUB_R // 8, 8, 128).sum(axis=0)
        for g in range(cw // 128):
            ls = slice(g * 128, (g + 1) * 128)
            wv = w_ref[:, ls]
            acc =[jnp.zeros((8, 128), f32) for _ in range(CONV_K + 1)]
            for c in reversed(range(tt // SUB_R)):
                r0 = c * SUB_R
                xs = [pad[pl.ds(PAD - 3 + kk + r0, SUB_R), ls] for kk in range(CONV_K)]
                dy = d_ref[pl.ds(r0, SUB_R), ls]
                if act:
                    y = wv[0:1, :] * xs[0]
                    for kk in range(1, CONV_K):
                        y = y + wv[kk:kk + 1, :] * xs[kk]
                    if has_b:
                        y = y + b_ref[:, ls]
                    dy = dy * _dsilu(y)
                dpad[pl.ds(r0, SUB_R), ls] = dy
                dx = wv[3:4, :] * dy
                for j in range(1, CONV_K):
                    dx = dx + wv[3 - j:4 - j, :] * dpad[pl.ds(r0 + j, SUB_R), ls]
                dx_ref[pl.ds(r0, SUB_R), ls] = dx.astype(bf16)
                for kk in range(CONV_K):
                    acc[kk] = acc[kk] + fold(dy * xs[kk])
                acc[CONV_K] = acc[CONV_K] + fold(dy)
            for kk in range(CONV_K):
                dw_ref[kk:kk + 1, ls] += jnp.sum(acc[kk], axis=0, keepdims=True)
            db_ref[:, ls] += jnp.sum(acc[CONV_K], axis=0, keepdims=True)
        head = dpad[pl.ds(0, PAD), :]
        dpad[pl.ds(tt, PAD), :] = head

    rows8 = tt // PAD
    in_specs = [pl.BlockSpec((tt, cw), lambda g, t: (NT - 1 - t, cb + g)),
                pl.BlockSpec((PAD, cw), lambda g, t: (jnp.maximum((NT - 1 - t) * rows8 - 1, 0), cb + g)),
                pl.BlockSpec((CONV_K, cw), lambda g, t: (0, g))]
    args = [x, x, w]
    if has_b:
        in_specs.append(pl.BlockSpec((1, cw), lambda g, t: (0, g)))
        args.append(b)
    in_specs += [pl.BlockSpec((tt, cw), lambda g, t: (NT - 1 - t, g)), pl.BlockSpec(memory_space=pl.ANY)]
    args += [dout, dp]
    return pl.pallas_call(
        body, grid=(G, NT), in_specs=in_specs,
        out_specs=[pl.BlockSpec((tt, cw), lambda g, t: (NT - 1 - t, cb + g)),
                   pl.BlockSpec((CONV_K, cw), lambda g, t: (0, g)),
                   pl.BlockSpec((1, cw), lambda g, t: (0, g))],
        out_shape=[jax.ShapeDtypeStruct(dp.shape, dp.dtype), jax.ShapeDtypeStruct((CONV_K, C), f32),
                   jax.ShapeDtypeStruct((1, C), f32)],
        input_output_aliases={len(args) - 1: 0},
        scratch_shapes=[pltpu.VMEM((tt + PAD, cw), f32), pltpu.VMEM((tt + PAD, cw), f32)],
        compiler_params=_cparams(("parallel", "arbitrary")), name=name,
    )(*args)


def _lru_gates(xc, wr, wi, br, bi, lam):
    r = _sigmoid(_dot(xc, wr, _NN) + br)
    i = _sigmoid(_dot(xc, wi, _NN) + bi)
    sp = _softplus(-lam)
    a = jnp.exp(-LRU_C * r * sp)
    mult = jnp.sqrt(1.0 - a * a)
    return r, i, sp, a, mult


def _lru_fwd(xc, p, y_off, wr, wi, br, bi, lam, mix, name):
    T = xc.shape[0]
    G = LRU_WIDTH // 128
    tt = min(512, T)
    NT = T // tt

    def body(x_ref, y_ref, wr_ref, wi_ref, br_ref, bi_ref, l_ref, mix_in, o_ref, h_ref, hc):
        t = pl.program_id(1)

        @pl.when(t == 0)
        def _():
            hc[...] = jnp.zeros_like(hc)

        x = x_ref[...]
        r, i, sp, a, mult = _lru_gates(x, wr_ref[...], wi_ref[...], br_ref[...], bi_ref[...], l_ref[...])
        row = lax.broadcasted_iota(jnp.int32, (tt, 128), 0)
        mult = jnp.where((row == 0) & (t == 0), 1.0, mult)
        U = x * i * mult
        A = a
        d = 1
        while d < tt:
            keep = row >= d
            Ush = jnp.where(keep, pltpu.roll(U, d, 0), 0.0)
            Ash = jnp.where(keep, pltpu.roll(A, d, 0), 1.0)
            U = A * Ush + U
            A = A * Ash
            d *= 2
        h = U + A * hc[0:1, :]
        h_ref[...] = h
        hc[...] = jnp.broadcast_to(h[tt - 1:tt, :], hc.shape)
        o_ref[...] = (h * _gelu(y_ref[...])).astype(bf16)

    tile = pl.BlockSpec((tt, 128), lambda g, t: (t, g))
    vec = pl.BlockSpec((1, 128), lambda g, t: (0, g))
    wsp = pl.BlockSpec((128, 128), lambda g, t: (g, g))
    return pl.pallas_call(
        body, grid=(G, NT),
        in_specs=[tile, pl.BlockSpec((tt, 128), lambda g, t: (t, y_off + g)), wsp, wsp, vec, vec, vec,
                  pl.BlockSpec(memory_space=pl.ANY)],
        out_specs=[pl.BlockSpec((tt, 128), lambda g, t: (t, G + g)), tile],
        out_shape=[jax.ShapeDtypeStruct(mix.shape, mix.dtype), jax.ShapeDtypeStruct((T, LRU_WIDTH), f32)],
        input_output_aliases={7: 0},
        scratch_shapes=[pltpu.VMEM((8, 128), f32)],
        compiler_params=_cparams(("parallel", "arbitrary")), name=name,
    )(xc, p, wr, wi, br, bi, lam, mix)


def _lru_bwd(xc, p, y_off, wr, wi, br, bi, lam, hs, dmix, d_off, dp, name):
    T = xc.shape[0]
    G = LRU_WIDTH // 128
    tt = min(512, T)
    NT = T // tt

    def body(x_ref, y_ref, wr_ref, wi_ref, br_ref, bi_ref, l_ref, h_ref, hp_ref, do_ref, dp_in,
             dx_ref, dy_ref, dwr_ref, dwi_ref, dbr_ref, dbi_ref, dl_ref, lc, an):
        t = pl.program_id(1)
        first_tile = t == NT - 1

        @pl.when(t == 0)
        def _():
            lc[...] = jnp.zeros_like(lc)
            an[...] = jnp.zeros_like(an)
            dwr_ref[...] = jnp.zeros_like(dwr_ref)
            dwi_ref[...] = jnp.zeros_like(dwi_ref)
            dbr_ref[...] = jnp.zeros_like(dbr_ref)
            dbi_ref[...] = jnp.zeros_like(dbi_ref)
            dl_ref[...] = jnp.zeros_like(dl_ref)

        x = x_ref[...]
        y = y_ref[...]
        wr, wi, lam_ = wr_ref[...], wi_ref[...], l_ref[...]
        r, i, sp, a, mult_raw = _lru_gates(x, wr, wi, br_ref[...], bi_ref[...], lam_)
        row = lax.broadcasted_iota(jnp.int32, (tt, 128), 0)
        t0 = (row == 0) & first_tile
        mult = jnp.where(t0, 1.0, mult_raw)
        h = h_ref[...]
        do = do_ref[...]
        dh = do * _gelu(y)
        dy_ref[...] = (do * h * _dgelu(y)).astype(bf16)
        B = jnp.where(row == tt - 1, an[0:1, :], pltpu.roll(a, tt - 1, 0))
        L = dh
        d = 1
        while d < tt:
            keep = row < tt - d
            Lsh = jnp.where(keep, pltpu.roll(L, tt - d, 0), 0.0)
            Bsh = jnp.where(keep, pltpu.roll(B, tt - d, 0), 1.0)
            L = L + B * Lsh
            B = B * Bsh
            d *= 2
        L = L + B * lc[0:1, :]
        lc[...] = jnp.broadcast_to(L[0:1, :], lc.shape)
        an[...] = jnp.broadcast_to(a[0:1, :], an.shape)
        hprev = jnp.where(first_tile, 0.0, hp_ref[...])[PAD - 1:PAD, :]
        hm1 = jnp.where(row == 0, hprev, pltpu.roll(h, 1, 0))
        da = L * hm1
        dxc = L * i * mult
        di = L * x * mult
        dmult = jnp.where(t0, 0.0, L * x * i)
        da = da - jnp.where(t0, 0.0, dmult * a / mult_raw)
        dlog_a = da * a
        dr = dlog_a * (-LRU_C) * sp
        dsp = jnp.sum(dlog_a * (-LRU_C) * r, axis=0, keepdims=True)
        dpr = dr * r * (1.0 - r)
        dpi = di * i * (1.0 - i)
        dx_ref[...] = dxc + _dot(dpr, wr, _NT) + _dot(dpi, wi, _NT)
        for d_ref, dpre in ((dwr_ref, dpr), (dwi_ref, dpi)):
            dw = _dot(x, dpre, _TN)
            for s in range(2):
                d_ref[s] += dw[s * 64:(s + 1) * 64, s * 64:(s + 1) * 64]
        dbr_ref[...] += jnp.sum(dpr, axis=0, keepdims=True)
        dbi_ref[...] += jnp.sum(dpi, axis=0, keepdims=True)
        dl_ref[...] += dsp * (-_sigmoid(-lam_))

    rows8 = tt // PAD
    tile = pl.BlockSpec((tt, 128), lambda g, t: (NT - 1 - t, g))
    vec = pl.BlockSpec((1, 128), lambda g, t: (0, g))
    wsp = pl.BlockSpec((128, 128), lambda g, t: (g, g))
    wout = pl.BlockSpec((2, 64, 64), lambda g, t: (g, 0, 0))
    return pl.pallas_call(
        body, grid=(G, NT),
        in_specs=[tile, pl.BlockSpec((tt, 128), lambda g, t: (NT - 1 - t, y_off + g)), wsp, wsp, vec, vec, vec, tile,
                  pl.BlockSpec((PAD, 128), lambda g, t: (jnp.maximum((NT - 1 - t) * rows8 - 1, 0), g)),
                  pl.BlockSpec((tt, 128), lambda g, t: (NT - 1 - t, d_off + g)), pl.BlockSpec(memory_space=pl.ANY)],
        out_specs=[tile, pl.BlockSpec((tt, 128), lambda g, t: (NT - 1 - t, y_off + g)), wout, wout, vec, vec, vec],
        out_shape=[jax.ShapeDtypeStruct((T, LRU_WIDTH), f32), jax.ShapeDtypeStruct(dp.shape, dp.dtype),
                   jax.ShapeDtypeStruct((2 * G, 64, 64), f32), jax.ShapeDtypeStruct((2 * G, 64, 64), f32),
                   jax.ShapeDtypeStruct((1, LRU_WIDTH), f32), jax.ShapeDtypeStruct((1, LRU_WIDTH), f32),
                   jax.ShapeDtypeStruct((1, LRU_WIDTH), f32)],
        input_output_aliases={10: 1},
        scratch_shapes=[pltpu.VMEM((8, 128), f32), pltpu.VMEM((8, 128), f32)],
        compiler_params=_cparams(("parallel", "arbitrary")), name=name,
    )(xc, p, wr, wi, br, bi, lam, hs, hs, dmix, dp)


_NN3 = (((2,), (1,)), ((0,), (0,)))
_NT3 = (((2,), (2,)), ((0,), (0,)))
_TN3 = (((1,), (1,)), ((0,), (0,)))


def _pairs(ref, K):
    C = GDN_CHUNK
    return jnp.stack([ref[c * C:(c + 1) * C, h * HEAD_DIM:(h + 1) * HEAD_DIM] for c in range(K) for h in range(GDN_HEADS)])


def _put_pairs(ref, val, K, col=0):
    C, H = GDN_CHUNK, GDN_HEADS
    for c in range(K):
        for h in range(H):
            ref[c * C:(c + 1) * C, col + h * HEAD_DIM:col + (h + 1) * HEAD_DIM] = val[c * H + h].astype(ref.dtype)


def _rowsum(x):
    H, C, L = x.shape
    return _dot(x.reshape(H * C, L), jnp.ones((L, HEAD_DIM), f32), _NN).reshape(H, C, HEAD_DIM)


def _gdn_pre(qr, kr, v, ba, alog, dtb):
    C, H = GDN_CHUNK, GDN_HEADS
    B = qr.shape[0]
    K = B // H
    lane = lax.broadcasted_iota(jnp.int32, (C, 128), 1)
    lane3 = lax.broadcasted_iota(jnp.int32, (B, C, 128), 2)
    ri = lax.broadcasted_iota(jnp.int32, (C, C), 0)
    ci = lax.broadcasted_iota(jnp.int32, (C, C), 1)
    rowc = lax.broadcasted_iota(jnp.int32, (C, 1), 0)
    col = lambda m, j: jnp.sum(jnp.where(lane == j, m, 0.0), axis=1, keepdims=True)
    ea = jnp.exp(alog)
    tri = (ri >= ci).astype(f32)
    g_all, beta_cols, G_cols = [], [], []
    for c in range(K):
        ba_c = ba[c * C:(c + 1) * C]
        g_c = -ea * _softplus(ba_c + dtb)
        G_c = _dot01(tri, g_c, _NN)
        s_c = _sigmoid(ba_c)
        g_all.append(g_c)
        beta_cols += [col(s_c, h) for h in range(H)]
        G_cols += [col(G_c, H + h) for h in range(H)]
    wide = lambda c: jnp.broadcast_to(c, (B, C, 128))
    beta = wide(jnp.stack(beta_cols))
    Gc = jnp.stack(G_cols)
    rq = lax.rsqrt(_rowsum(qr * qr) + EPS)
    rk = lax.rsqrt(_rowsum(kr * kr) + EPS)
    qh, kn = qr * rq, kr * rk
    qn = qh * (HEAD_DIM ** -0.5)
    Grow = _dot01(jnp.ones((B, C, 128), f32), jnp.where(lane3 == 0, Gc, 0.0), _NT3)
    incl = ri >= ci
    Di = jnp.where(incl, jnp.exp(jnp.where(incl, Gc - Grow, 0.0)), 0.0)
    Ds = jnp.where(ri > ci, Di, 0.0)
    Gl = jnp.sum(jnp.where(rowc == C - 1, Gc, 0.0), axis=1, keepdims=True)
    eG = wide(jnp.exp(Gc))
    eGl = wide(jnp.exp(Gl - Gc))
    cd = jnp.exp(Gl)
    kb = kn * beta
    vb = v * beta
    Lm = _dot(kb, kn, _NT3) * Ds
    kbg = kb * eG
    QK = _dot(qn, kn, _NT3) * Di
    qg = qn * eG
    kg = kn * eGl
    return dict(beta=beta, g_all=g_all, rq=rq, rk=rk, qh=qh, kn=kn, qn=qn, Di=Di, Ds=Ds, eG=eG, eGl=eGl, cd=cd,
                kb=kb, vb=vb, Lm=Lm, kbg=kbg, QK=QK, qg=qg, kg=kg, lane=lane, ri=ri, ci=ci, rowc=rowc, ea=ea)


def _unit_lower_inverse(Lm):
    C = Lm.shape[-1]
    ri = lax.broadcasted_iota(jnp.int32, (C, C), 0)
    ci = lax.broadcasted_iota(jnp.int32, (C, C), 1)
    same = lambda s: (ri // s) == (ci // s)
    Xd = jnp.where(same(8), -Lm, 0.0)
    Tinv = (ri == ci).astype(f32) + Xd
    Pw = Xd
    for _ in range(2):
        Pw = _dot(Pw, Pw, _NN3)
        Tinv = Tinv + _dot(Tinv, Pw, _NN3)
    for s in (8, 16, 32):
        off = jnp.where(same(2 * s) & jnp.logical_not(same(s)), Lm, 0.0)
        Tinv = Tinv - _dot(_dot(Tinv, off, _NN3), Tinv, _NN3)
    return Tinv


def _gdn_specs(T, rev):
    C = GDN_CHUNK
    H = GDN_HEADS
    K = min(GDN_STEP, T // C)
    NS = T // (C * K)
    nn = (lambda n: NS - 1 - n) if rev else (lambda n: n)
    wide = lambda blk: pl.BlockSpec((K * C, H * HEAD_DIM), lambda n: (nn(n), blk))
    one = lambda off: pl.BlockSpec((K * C, HEAD_DIM), lambda n: (nn(n), off))
    vec = pl.BlockSpec((1, 128), lambda n: (0, 0))
    st = lambda rows: pl.BlockSpec((K, H, rows, rows), lambda n: (nn(n), 0, 0, 0))
    return K, NS, wide, one, vec, st


def _gdn_fwd(qkv, p, alog, dtb, nw, name):
    T = qkv.shape[0]
    C, H = GDN_CHUNK, GDN_HEADS
    N = T // C
    K, NS, wide, one, vec, st_spec = _gdn_specs(T, False)

    def body(q_ref, k_ref, v_ref, z_ref, ba_ref, al_ref, dt_ref, nw_ref, y_ref, sp_ref, ti_ref, vn_ref, o_ref, st):
        @pl.when(pl.program_id(0) == 0)
        def _():
            st[...] = jnp.zeros_like(st)

        f = _gdn_pre(_pairs(q_ref, K), _pairs(k_ref, K), _pairs(v_ref, K), ba_ref[...], al_ref[...], dt_ref[...])
        Tinv = _unit_lower_inverse(f["Lm"])
        ti_ref[...] = Tinv.reshape(K, H, C, C).astype(bf16)
        w = _dot(Tinv, f["kbg"], _NN3)
        u = _dot(Tinv, f["vb"], _NN3)
        S = st[...]
        vns, os_ = [], []
        for c in range(K):
            sl = slice(c * H, (c + 1) * H)
            sp_ref[c] = S
            vn_c = u[sl] - _dot(w[sl], S, _NN3)
            os_.append(_dot(f["qg"][sl], S, _NN3) + _dot(f["QK"][sl], vn_c, _NN3))
            S = S * f["cd"][sl] + _dot(f["kg"][sl], vn_c, _TN3)
            vns.append(vn_c)
        st[...] = S
        vn, o = jnp.concatenate(vns), jnp.concatenate(os_)
        r = lax.rsqrt(_rowsum(o * o) * (1.0 / HEAD_DIM) + EPS)
        _put_pairs(y_ref, o * r * nw_ref[...] * _silu(_pairs(z_ref, K)), K)
        _put_pairs(vn_ref, vn, K)
        _put_pairs(o_ref, o, K)

    wide_f32 = jax.ShapeDtypeStruct((T, H * HEAD_DIM), f32)
    return pl.pallas_call(
        body, grid=(NS,),
        in_specs=[wide(0), wide(1), wide(2), wide(3), one(4 * H), vec, vec, vec],
        out_specs=[wide(0), st_spec(HEAD_DIM), st_spec(C), wide(0), wide(0)],
        out_shape=[jax.ShapeDtypeStruct((T, H * HEAD_DIM), bf16), jax.ShapeDtypeStruct((N, H, HEAD_DIM, HEAD_DIM), f32),
                   jax.ShapeDtypeStruct((N, H, C, C), bf16), jax.ShapeDtypeStruct((T, H * HEAD_DIM), bf16), wide_f32],
        scratch_shapes=[pltpu.VMEM((H, HEAD_DIM, HEAD_DIM), f32)],
        compiler_params=_cparams(("arbitrary",)), name=name,
    )(qkv, qkv, qkv, p, p, alog, dtb, nw)


def _gdn_bwd(qkv, p, alog, dtb, nw, sprev, tinv, vn_all, o_all, dy_all, name):
    T = qkv.shape[0]
    C, H = GDN_CHUNK, GDN_HEADS
    N = T // C
    K, NS, wide, one, vec, st_spec = _gdn_specs(T, True)
    rs = lambda m: jnp.sum(m, axis=2, keepdims=True)

    def body(q_ref, k_ref, v_ref, z_ref, ba_ref, al_ref, dt_ref, nw_ref, sp_ref, ti_ref, vn_ref, o_ref, dy_ref,
             dqkv_ref, dz_ref, dba_ref, dal_ref, ddt_ref, dnw_ref, dst):
        @pl.when(pl.program_id(0) == 0)
        def _():
            dst[...] = jnp.zeros_like(dst)
            dal_ref[...] = jnp.zeros_like(dal_ref)
            ddt_ref[...] = jnp.zeros_like(ddt_ref)
            dnw_ref[...] = jnp.zeros_like(dnw_ref)

        ba, dtb_, nwv = ba_ref[...], dt_ref[...], nw_ref[...]
        v = _pairs(v_ref, K)
        f = _gdn_pre(_pairs(q_ref, K), _pairs(k_ref, K), v, ba, al_ref[...], dtb_)
        beta, kn, qn, kb, vb, kbg = f["beta"], f["kn"], f["qn"], f["kb"], f["vb"], f["kbg"]
        eG, eGl, cd, Di, Ds, QK, qg, kg = f["eG"], f["eGl"], f["cd"], f["Di"], f["Ds"], f["QK"], f["qg"], f["kg"]
        lane, ri, ci, rowc = f["lane"], f["ri"], f["ci"], f["rowc"]
        Tinv = ti_ref[...].reshape(K * H, C, C)
        S = sp_ref[...].reshape(K * H, HEAD_DIM, HEAD_DIM)
        w_ = _dot(Tinv, kbg, _NN3)
        vn, o = _pairs(vn_ref, K), _pairs(o_ref, K)
        z, dy = _pairs(z_ref, K), _pairs(dy_ref, K)
        r = lax.rsqrt(_rowsum(o * o) * (1.0 / HEAD_DIM) + EPS)
        nrm = o * r
        sz = _silu(z)
        dn = dy * nwv * sz
        _put_pairs(dz_ref, dy * nrm * nwv * _dsilu(z), K)
        dnw_ref[...] += jnp.sum(jnp.sum(dy * nrm * sz, axis=0), axis=0, keepdims=True)
        do = r * (dn - nrm * (_rowsum(dn * nrm) * (1.0 / HEAD_DIM)))
        dvn_do = _dot(QK, do, _TN3)
        dS_do = _dot(qg, do, _TN3)
        dqg = _dot(do, S, _NT3)
        dQK = _dot(do, vn, _NT3)
        dS = dst[...]
        dS1s, dvns = [None] * K, [None] * K
        for c in reversed(range(K)):
            sl = slice(c * H, (c + 1) * H)
            dS1s[c] = dS
            dvns[c] = _dot(kg[sl], dS, _NN3) + dvn_do[sl]
            dS = cd[sl] * dS + dS_do[sl] - _dot(w_[sl], dvns[c], _TN3)
        dst[...] = dS
        dS1, dvn = jnp.concatenate(dS1s), jnp.concatenate(dvns)
        dcd = jnp.sum(jnp.sum(S * dS1, axis=2, keepdims=True), axis=1, keepdims=True)
        dkg = _dot(vn, dS1, _NT3)
        dw = -_dot(dvn, S, _NT3)
        dqn = dqg * eG
        dkn = dkg * eGl
        deGl = rs(dkg * kn)
        dQKr = dQK * Di
        E = dQK * QK
        dqn = dqn + _dot(dQKr, kn, _NN3)
        dkn = dkn + _dot(dQKr, qn, _TN3)
        dT = _dot(dvn, vb, _NT3) + _dot(dw, kbg, _NT3)
        dvb = _dot(Tinv, dvn, _TN3)
        dkbg = _dot(Tinv, dw, _TN3)
        dkb = dkbg * eG
        deG = rs(dqg * qn + dkbg * kb)
        dL = -_dot(_dot(Tinv, dT, _TN3), Tinv, _NT3)
        dKK = dL * Ds
        E = E + dL * f["Lm"]
        dkb = dkb + _dot(dKK, kn, _NN3)
        dkn = dkn + _dot(dKK, kb, _TN3) + dkb * beta
        dbeta = rs(dkb * kn + dvb * v)
        _put_pairs(dqkv_ref, dvb * beta, K, 2 * H * HEAD_DIM)
        dG = rs(E) - rs(jnp.swapaxes(E, 1, 2)) + deG * eG - deGl * eGl
        dGl = jnp.sum(deGl * eGl, axis=1, keepdims=True) + dcd * cd
        dG = dG + jnp.where(rowc == C - 1, dGl, 0.0)
        qh = f["qh"]
        _put_pairs(dqkv_ref, (HEAD_DIM ** -0.5) * f["rq"] * (dqn - qh * _rowsum(dqn * qh)), K)
        _put_pairs(dqkv_ref, f["rk"] * (dkn - kn * _rowsum(dkn * kn)), K, H * HEAD_DIM)
        db = dbeta * beta * (1.0 - beta)
        triu = (ri <= ci).astype(f32)
        for c in range(K):
            db_all = jnp.where(lane == 0, db[c * H], 0.0)
            dG_all = jnp.where(lane == H, dG[c * H], 0.0)
            for h in range(1, H):
                db_all = db_all + jnp.where(lane == h, db[c * H + h], 0.0)
                dG_all = dG_all + jnp.where(lane == H + h, dG[c * H + h], 0.0)
            dg_all = _dot01(triu, dG_all, _NN)
            da_all = dg_all * (-f["ea"]) * _sigmoid(ba[c * C:(c + 1) * C] + dtb_)
            dba_ref[c * C:(c + 1) * C, :] = (db_all + da_all).astype(bf16)
            ddt_ref[...] += jnp.sum(da_all, axis=0, keepdims=True)
            dal_ref[...] += jnp.sum(dg_all * f["g_all"][c], axis=0, keepdims=True)

    small = jax.ShapeDtypeStruct((1, 128), f32)
    return pl.pallas_call(
        body, grid=(NS,),
        in_specs=[wide(0), wide(1), wide(2), wide(3), one(4 * H), vec, vec, vec, st_spec(HEAD_DIM), st_spec(C),
                  wide(0), wide(0), wide(0)],
        out_specs=[pl.BlockSpec((K * C, 3 * H * HEAD_DIM), lambda n: (NS - 1 - n, 0)), wide(3), one(0), vec, vec, vec],
        out_shape=[jax.ShapeDtypeStruct((T, 3 * H * HEAD_DIM), f32), jax.ShapeDtypeStruct((T, ODD_PAD), bf16),
                   jax.ShapeDtypeStruct((T, 128), bf16), small, small, small],
        scratch_shapes=[pltpu.VMEM((H, HEAD_DIM, HEAD_DIM), f32)],
        compiler_params=_cparams(("arbitrary",)), name=name,
    )(qkv, qkv, qkv, p, p, alog, dtb, nw, sprev, tinv, vn_all, o_all, dy_all)


def _lanes_from(x, s):
    return x if s % 128 == 0 else pltpu.roll(x, (128 - s) % 128, 1)


def _odd_assemble(g, name):
    R = g.shape[1]
    tr = min(256, R)
    n_blk = ODD_SHARD_PAD // 128

    def body(g_ref, o_ref):
        lane = lax.broadcasted_iota(jnp.int32, (tr, 128), 1)
        blk = lambda d, m: g_ref[d, :, m * 128:(m + 1) * 128]
        for gb in range(ODD_PAD // 128):
            c0 = 128 * gb
            if c0 >= ODD_IN:
                o_ref[:, c0:c0 + 128] = jnp.zeros((tr, 128), g.dtype)
                continue
            d0 = c0 // ODD_SHARD
            m0, sh = divmod(c0 - ODD_SHARD * d0, 128)
            take = min(128, ODD_SHARD * (d0 + 1) - c0)
            p = _lanes_from(blk(d0, m0), sh)
            if sh and m0 + 1 < n_blk:
                p = jnp.where(lane < 128 - sh, p, _lanes_from(blk(d0, m0 + 1), sh))
            if take < 128:
                nxt = pltpu.roll(blk(d0 + 1, 0), take, 1) if d0 + 1 < N_DEV else jnp.zeros((tr, 128), g.dtype)
                p = jnp.where(lane < take, p, nxt)
            o_ref[:, c0:c0 + 128] = p

    return pl.pallas_call(
        body, grid=(R // tr,),
        in_specs=[pl.BlockSpec((N_DEV, tr, ODD_SHARD_PAD), lambda i: (0, i, 0))],
        out_specs=pl.BlockSpec((tr, ODD_PAD), lambda i: (i, 0)),
        out_shape=jax.ShapeDtypeStruct((R, ODD_PAD), g.dtype),
        compiler_params=_cparams(("parallel",)), name=name,
    )(g)


def _odd_split(w, name):
    R = w.shape[0]
    tr = min(256, R)

    def body(w_ref, o_ref):
        lane = lax.broadcasted_iota(jnp.int32, (tr, 128), 1)
        blk = lambda gb: w_ref[:, gb * 128:(gb + 1) * 128]
        for d in range(N_DEV):
            for m in range(ODD_SHARD_PAD // 128):
                g0, sh = divmod(ODD_SHARD * d + 128 * m, 128)
                p = _lanes_from(blk(g0), sh)
                if sh and g0 + 1 < ODD_PAD // 128:
                    p = jnp.where(lane < 128 - sh, p, _lanes_from(blk(g0 + 1), sh))
                real = ODD_SHARD - 128 * m
                if real < 128:
                    p = jnp.where(lane < real, p, jnp.zeros_like(p))
                o_ref[d, :, m * 128:(m + 1) * 128] = p

    return pl.pallas_call(
        body, grid=(R // tr,),
        in_specs=[pl.BlockSpec((tr, ODD_PAD), lambda i: (i, 0))],
        out_specs=pl.BlockSpec((N_DEV, tr, ODD_SHARD_PAD), lambda i: (0, i, 0)),
        out_shape=jax.ShapeDtypeStruct((N_DEV, R, ODD_SHARD_PAD), w.dtype),
        compiler_params=_cparams(("parallel",)), name=name,
    )(w)


def _adam_tile(g, w_ref, m_ref, v_ref, go_ref, d_ref, mo_ref, vo_ref):
    c1 = 1.0 - ADAM_B1 ** ADAM_STEP
    c2 = 1.0 - ADAM_B2 ** ADAM_STEP
    mn = ADAM_B1 * m_ref[...] + (1.0 - ADAM_B1) * g
    vn = ADAM_B2 * v_ref[...] + (1.0 - ADAM_B2) * (g * g)
    go_ref[...] = g
    mo_ref[...] = mn
    vo_ref[...] = vn
    d_ref[...] = -ADAM_LR * ((mn / c1) / (jnp.sqrt(vn / c2) + ADAM_EPS) + ADAM_WD * w_ref[...])


def _adamw(w, gs, m, v, name, layer=None, prev=None):
    R, Cc = w.shape[-2:]
    S = gs.shape[0]
    tr = R
    if S * R * Cc * 4 > (4 << 20):
        for cand in (512, 256, 128, 64, 32, 16, 8):
            if R % cand == 0 and R > cand:
                tr = cand
                break

    def body(w_ref, g_ref, m_ref, v_ref, *rest):
        g = g_ref[0].astype(f32)
        for s in range(1, S):
            g = g + g_ref[s].astype(f32)
        _adam_tile(g, w_ref, m_ref, v_ref, *rest[-4:])

    if layer is None:
        blk = pl.BlockSpec((tr, Cc), lambda i: (i, 0))
    else:
        blk = pl.BlockSpec((None, tr, Cc), lambda i: (layer, i, 0))
    out = jax.ShapeDtypeStruct(w.shape, f32)
    carried = [] if prev is None else list(prev)
    return pl.pallas_call(
        body, grid=(R // tr,),
        in_specs=[blk, pl.BlockSpec((S, tr, Cc), lambda i: (0, i, 0)), blk, blk]
        + [pl.BlockSpec(memory_space=pl.ANY)] * len(carried),
        out_specs=[blk] * 4, out_shape=[out] * 4,
        input_output_aliases={4 + j: j for j in range(len(carried))},
        compiler_params=_cparams(("parallel",)), name=name,
    )(w, gs, m, v, *carried)


def _adamw_column_major(w, gs, m, v, name):
    _, R, Cc = w.shape
    S = gs.shape[0]
    q = R // 128
    dense = lambda a: jnp.transpose(a, (2, 0, 1)).reshape(Cc * q, 128)
    back = lambda a: jnp.transpose(a.reshape(Cc, q, 128), (1, 2, 0)).reshape(1, R, Cc)

    def body(w_ref, g_ref, m_ref, v_ref, go_ref, d_ref, mo_ref, vo_ref, gt):
        for i in range(q):
            g = g_ref[0, i * 128:(i + 1) * 128, :].astype(f32)
            for s in range(1, S):
                g = g + g_ref[s, i * 128:(i + 1) * 128, :].astype(f32)
            gt[pl.ds(i, 128, stride=q), :] = g.T
        _adam_tile(gt[...], w_ref, m_ref, v_ref, go_ref, d_ref, mo_ref, vo_ref)

    blk = pl.BlockSpec((128 * q, 128), lambda j: (j, 0))
    outs = pl.pallas_call(
        body, grid=(pl.cdiv(Cc, 128),),
        in_specs=[blk, pl.BlockSpec((S, R, 128), lambda j: (0, 0, j)), blk, blk],
        out_specs=[blk] * 4, out_shape=[jax.ShapeDtypeStruct((Cc * q, 128), f32)] * 4,
        scratch_shapes=[pltpu.VMEM((128 * q, 128), f32)],
        compiler_params=_cparams(("parallel",)), name=name,
    )(dense(w), gs, dense(m), dense(v))
    return [back(o) for o in outs]


def _me():
    x, y, c = lax.axis_index("x"), lax.axis_index("y"), lax.axis_index("c")
    return x, y, c, 4 * x + 2 * y + c


def _peer(k):
    x, y, c, _ = _me()
    px = 1 - x if k & 4 else x
    py = 1 - y if k & 2 else y
    pc = 1 - c if k & 1 else c
    return (px, py, pc), 4 * px + 2 * py + pc


_HBM = pl.BlockSpec(memory_space=pltpu.HBM)
_SEM = pl.BlockSpec(memory_space=pltpu.SEMAPHORE)
_EFFECT = pltpu.SideEffectType.DATAFLOW_SIDE_EFFECTING


def _copy(src, land, ssem, rsem, k, blocked, landing_slot_of_peer):
    pid, pidx = _peer(k)
    slot = pidx if landing_slot_of_peer else _me()[3]
    return pltpu.make_async_remote_copy(src_ref=src.at[pidx] if blocked else src, dst_ref=land.at[slot],
                                        send_sem=ssem.at[k - 1], recv_sem=rsem.at[k - 1], device_id=pid, device_id_type=MESH)


def _own_copy(src, land, rsem, blocked):
    me = _me()[3]
    return pltpu.make_async_copy(src.at[me] if blocked else src, land.at[me], rsem.at[N_DEV - 1])


_ALL_PEERS = tuple(range(1, N_DEV))
_SAME_CORE_PEERS = (1, 2, 4, 6)
_OTHER_CORE_PEERS = (3, 5, 7)


def _relay_copies(land, ssem, rsem):
    sibling, _ = _peer(1)
    copies = []
    for i in range(len(land)):
        for j, k in enumerate(_OTHER_CORE_PEERS):
            sems = dict(send_sem=ssem.at[3 * i + j], recv_sem=rsem.at[3 * i + j], device_id=sibling, device_id_type=MESH)
            _, outgoing = _peer(k - 1)
            _, incoming = _peer(k)
            copies.append((pltpu.make_async_remote_copy(src_ref=land[i].at[outgoing], dst_ref=land[i].at[outgoing], **sems),
                           pltpu.make_async_remote_copy(src_ref=land[i].at[incoming], dst_ref=land[i].at[incoming], **sems)))
    return copies


def _relay_start(lands, name):
    n = len(lands)

    def body(*refs):
        for send, _ in _relay_copies(refs[:n], refs[n], refs[n + 1]):
            send.start()
        refs[-1][...] = jnp.zeros_like(refs[-1])

    sem = pltpu.SemaphoreType.DMA((3 * n,))
    res = pl.pallas_call(
        body, name=name,
        out_shape=(sem, sem) + tuple(pltpu.HBM(a.shape, a.dtype) for a in lands) + (jax.ShapeDtypeStruct((8, 128), f32),),
        in_specs=(_HBM,) * n, out_specs=(_SEM, _SEM) + (_HBM,) * n + (pl.BlockSpec(memory_space=pltpu.VMEM),),
        input_output_aliases={i: 2 + i for i in range(n)},
        compiler_params=pltpu.CompilerParams(has_side_effects=_EFFECT),
    )(*lands)
    return res[:-1], res[-1]


def _relay_wait(handle, after, name):
    ssem, rsem, lands = handle[0], handle[1], handle[2:]
    n = len(lands)
    after = tuple(after) if isinstance(after, (tuple, list)) else (after,)

    def body(*refs):
        for send, recv in _relay_copies(refs[:n], refs[n], refs[n + 1]):
            send.wait_send()
            recv.wait_recv()

    return pl.pallas_call(
        body, name=name, out_shape=tuple(pltpu.HBM(a.shape, a.dtype) for a in lands),
        in_specs=(_HBM,) * n + (_SEM, _SEM) + (pl.BlockSpec(memory_space=pl.ANY),) * len(after), out_specs=(_HBM,) * n,
        input_output_aliases={i: i for i in range(n)}, compiler_params=pltpu.CompilerParams(has_side_effects=_EFFECT),
    )(*lands, ssem, rsem, *after)


def _send_start(srcs, blocked, name, relayed=()):
    n = len(srcs)
    lands = [lax.empty(a.shape if blocked else (N_DEV,) + a.shape, a.dtype) for a in srcs]

    def body(*refs):
        src, land, sems, token = refs[:n], refs[n:2 * n], refs[2 * n:4 * n], refs[-1]
        for i in range(n):
            for k in (_SAME_CORE_PEERS if i in relayed else _ALL_PEERS):
                _copy(src[i], land[i], sems[2 * i], sems[2 * i + 1], k, blocked, False).start()
        for i in range(n):
            _own_copy(src[i], land[i], sems[2 * i + 1], blocked).start()
        token[...] = jnp.zeros_like(token)

    sems = (pltpu.SemaphoreType.DMA((N_DEV - 1,)), pltpu.SemaphoreType.DMA((N_DEV,)))
    hbm = lambda a: pltpu.with_memory_space_constraint(a, pltpu.HBM)
    res = pl.pallas_call(
        body, name=name,
        out_shape=sems * n + tuple(pltpu.HBM(a.shape, a.dtype) for a in srcs + lands)
        + (jax.ShapeDtypeStruct((8, 128), f32),),
        in_specs=(_HBM,) * (2 * n),
        out_specs=(_SEM,) * (2 * n) + (_HBM,) * (2 * n) + (pl.BlockSpec(memory_space=pltpu.VMEM),),
        input_output_aliases={j: 2 * n + j for j in range(2 * n)},
        compiler_params=pltpu.CompilerParams(has_side_effects=_EFFECT),
    )(*[hbm(a) for a in srcs], *[hbm(a) for a in lands])
    handles = [(res[2 * i], res[2 * i + 1], res[2 * n + i], res[3 * n + i]) for i in range(n)]
    return handles, res[-1]


def _send_wait(handle, blocked, after, name, relayed=False):
    ssem, rsem, src, land = handle
    after = tuple(after) if isinstance(after, (tuple, list)) else (after,)

    def body(src_ref, land_ref, ssem_ref, rsem_ref, *rest):
        for k in (_SAME_CORE_PEERS if relayed else _ALL_PEERS):
            cp = _copy(src_ref, land_ref, ssem_ref, rsem_ref, k, blocked, True)
            cp.wait_send()
            cp.wait_recv()
        _own_copy(src_ref, land_ref, rsem_ref, blocked).wait()

    return pl.pallas_call(
        body, name=name, out_shape=(pltpu.HBM(src.shape, src.dtype), pltpu.HBM(land.shape, land.dtype)),
        in_specs=(_HBM, _HBM, _SEM, _SEM) + (pl.BlockSpec(memory_space=pl.ANY),) * len(after), out_specs=(_HBM, _HBM),
        input_output_aliases={0: 0, 1: 1}, compiler_params=pltpu.CompilerParams(has_side_effects=_EFFECT),
    )(src, land, ssem, rsem, *after)


def _block_diag(w):
    nb, bs = w.shape[0], w.shape[1]
    eye = jnp.eye(nb, dtype=w.dtype)
    return (eye[:, None, :, None] * w[:, :, None, :]).reshape(nb * bs, nb * bs)


_SQUARE_TILES = dict(tm=1024, tn=1024, tk=1024)


def _mlp_fwd(x, hm, wu, wd, tag, epilogue, extras, outs, between=None):
    (r,) = _matmul(hm, wu, "nn", outs=[bf16], epilogue=lambda acc: (jnp.maximum(acc, 0.0),), name=f"mlp_up_{tag}")
    res = _matmul(r, wd, "nn", outs=outs, extras=(x,) + tuple(extras), epilogue=epilogue, a_map=jnp.square,
                  after=() if between is None else (between(r),), name=f"mlp_down_{tag}", **_SQUARE_TILES)
    return res, (hm, r)


def _mlp_bwd(x, nw, wu, wd, saved, dxo, dxo_b, tag, sink, urgent):
    hm, r = saved
    (du,) = _matmul(dxo_b, wd, "nt", outs=[bf16], extras=(r,), epilogue=lambda acc, rr: (acc * (2.0 * rr.astype(f32)),),
                    name=f"mlp_dact_{tag}")
    (dwd,) = _matmul(r, dxo_b, "tn", outs=[bf16], a_map=jnp.square, name=f"mlp_dwd_{tag}", **_SQUARE_TILES)
    down = {f"w_down{tag}": dwd.reshape(N_DEV, D_FF // N_DEV, D_MODEL)}
    if urgent:
        tok = sink(down)
        (dwu,) = _matmul(hm, du, "tn", outs=[bf16], shard_cols=2, after=(tok,), name=f"mlp_dwu_{tag}")
        tok = sink({f"w_up{tag}": dwu})
    else:
        (dwu,) = _matmul(hm, du, "tn", outs=[bf16], shard_cols=2, name=f"mlp_dwu_{tag}")
        tok = sink({f"w_up{tag}": dwu, **down})
    return _matmul(du, wu, "nt", outs=_RMS_BWD_OUTS, extras=(x, dxo, nw), epilogue=_rms_bwd_ep, after=(tok,),
                   name=f"mlp_dh_{tag}", **_SQUARE_TILES)


def _local_step(x, tgt, P, weight, sink, ahead):
    T = x.shape[0]
    cos, sin = _rope_tables(T)
    rtab = _ret_tables()
    row = lambda a: a.reshape(1, -1)
    mix_nw, mlp_nw = P["mixer_norm_w"], P["mlp_norm_w"]
    wr_bd, wi_bd = _block_diag(P["lru_w_r"]), _block_diag(P["lru_w_i"])
    lru_b, lru_br, lru_bi, lru_lam = row(P["lru_conv_b"]), row(P["lru_b_r"]), row(P["lru_b_i"]), row(P["lru_lambda"])
    pad16 = lambda a: jnp.pad(a.reshape(1, GDN_HEADS), ((0, 0), (GDN_HEADS, 128 - 2 * GDN_HEADS)))
    alog, dtb = pad16(P["gdn_a_log"]), pad16(P["gdn_dt_bias"])
    gnw = row(P["gdn_norm_w"])

    x0 = x
    h0 = _rms_fwd(x0, mix_nw[0:1], "rms_mix_0")
    w_ie = weight("w_in_even", (h0, wr_bd, wi_bd))
    (pe,) = _matmul(h0, w_ie, "nn", outs=[f32], name="in_even")
    mix0, o_ret, s_ret = _ret_fwd(pe, cos, sin, rtab, "ret_fwd")
    w_lc = weight("lru_conv_w", pe)
    xc = _conv_fwd(pe, 4, w_lc, lru_b, False, "lru_conv_fwd")
    mix0, h_lru = _lru_fwd(xc, pe, 20, wr_bd, wi_bd, lru_br, lru_bi, lru_lam, mix0, "lru_fwd")
    w_oe = weight("w_out_even", mix0)
    tok = ahead(("w_up0", "w_down0"), mix0)
    x1, hm0 = _matmul(mix0, w_oe, "nn", outs=[f32, bf16], extras=(x0, mlp_nw[0:1]), epilogue=_residual_rms_ep,
                      after=(tok,), name="out_even", tm=1024, tn=D_MODEL)
    w_u0, w_d0 = weight("w_up0", x1), weight("w_down0", x1)
    (x2, h1), mlp0 = _mlp_fwd(x1, hm0, w_u0, w_d0, "0", _residual_rms_ep, (mix_nw[1:2],), [f32, bf16],
                              between=lambda r: ahead(("w_in_odd",), r))
    w_io = weight("w_in_odd", h1)
    (po,) = _matmul(h1, w_io, "nn", outs=[f32], tm=2048, tn=ODD_PAD // 3, name="in_odd")
    w_gc = weight("gdn_conv_w", po)
    qkv = _conv_fwd(po, 0, w_gc, None, True, "gdn_conv_fwd")
    y_gdn, s_gdn, ti_gdn, vn_gdn, o_gdn = _gdn_fwd(qkv, po, alog, dtb, gnw, "gdn_fwd")
    w_oo = weight("w_out_odd", y_gdn)
    x3, hm1 = _matmul(y_gdn, w_oo, "nn", outs=[f32, bf16], extras=(x2, mlp_nw[1:2]), epilogue=_residual_rms_ep,
                      name="out_odd", tm=1024, tn=D_MODEL)
    w_u1, w_d1 = weight("w_up1", x3), weight("w_down1", x3)
    (loss, dx4, dx4_b, d_final), mlp1 = _mlp_fwd(x3, hm1, w_u1, w_d1, "1", _loss_ep, (row(P["final_norm_w"]), tgt),
                                                 _LOSS_OUTS)
    dx3, dx3_b, d_mlp_nw1 = _mlp_bwd(x3, mlp_nw[1:2], w_u1, w_d1, mlp1, dx4, dx4_b, "1", sink, False)
    (dy_gdn,) = _matmul(dx3_b, w_oo, "nt", outs=[f32], name="out_odd_dx")
    (d_woo,) = _matmul(y_gdn, dx3_b, "tn", outs=[bf16], name="out_odd_dw")
    dqkv, dpo, dba, d_alog, d_dtb, d_gnw = _gdn_bwd(qkv, po, alog, dtb, gnw, s_gdn, ti_gdn, vn_gdn, o_gdn, dy_gdn,
                                                  "gdn_bwd")
    dpo, d_gconv, _ = _conv_bwd(po, 0, w_gc, None, True, dqkv, dpo, "gdn_conv_bwd")
    dpo = lax.dynamic_update_slice(dpo, dba, (0, 4 * D_MODEL))
    (d_wio,) = _matmul(h1, dpo, "tn", outs=[bf16], tn=ODD_PAD // 3, name="in_odd_dw")
    tok = sink(dict(w_out_odd=d_woo.reshape(N_DEV, D_MODEL // N_DEV, D_MODEL), w_in_odd=_odd_split(d_wio, "w_in_odd_split")))
    dx2, dx2_b, d_mix_nw1 = _matmul(dpo, w_io, "nt", outs=_RMS_BWD_OUTS, extras=(x2, dx3, mix_nw[1:2]),
                                    epilogue=_rms_bwd_ep, after=(tok,), tm=1024, tn=1024, tk=ODD_PAD // 3, name="in_odd_dx")
    dx1, dx1_b, d_mlp_nw0 = _mlp_bwd(x1, mlp_nw[0:1], w_u0, w_d0, mlp0, dx2, dx2_b, "0", sink, True)
    (d_woe,) = _matmul(mix0, dx1_b, "tn", outs=[bf16], name="out_even_dw")
    tok = sink(dict(w_out_even=d_woe.reshape(N_DEV, D_MODEL // N_DEV, D_MODEL)))
    (dmix0,) = _matmul(dx1_b, w_oe, "nt", outs=[f32], name="out_even_dx")
    dpe = _ret_bwd(pe, cos, sin, rtab, o_ret, s_ret, dmix0, "ret_bwd")
    dxc, dpe, d_wr, d_wi, d_br, d_bi, d_lam = _lru_bwd(xc, pe, 20, wr_bd, wi_bd, lru_br, lru_bi, lru_lam + tok[0:1, 0:1],
                                                       h_lru, dmix0, 4, dpe, "lru_bwd")
    dpe, d_lconv, d_lconv_b = _conv_bwd(pe, 4, w_lc, lru_b, False, dxc, dpe, "lru_conv_bwd")
    G = dict(
        mlp_norm_w=jnp.concatenate([d_mlp_nw0, d_mlp_nw1], axis=0),
        final_norm_w=d_final.reshape(-1),
        lru_conv_w=d_lconv, lru_conv_b=d_lconv_b.reshape(-1),
        lru_w_r=d_wr, lru_b_r=d_br.reshape(-1), lru_w_i=d_wi, lru_b_i=d_bi.reshape(-1),
        lru_lambda=d_lam.reshape(-1), gdn_conv_w=d_gconv,
        gdn_a_log=d_alog[0, GDN_HEADS:2 * GDN_HEADS], gdn_dt_bias=d_dtb[0, GDN_HEADS:2 * GDN_HEADS],
        gdn_norm_w=d_gnw.reshape(-1),
    )
    packed = _pack([G[k] for k in _SMALL] + [d_lconv, d_gconv, loss[0, 0:1]])
    tok = sink(dict(small=jnp.broadcast_to(packed[None], (N_DEV,) + packed.shape)))
    (d_wie,) = _matmul(h0, dpe, "tn", outs=[bf16], shard_cols=2, after=(tok,), name="in_even_dw")
    tok = sink(dict(w_in_even=d_wie))
    dx0, _, d_mix_nw0 = _matmul(dpe, w_ie, "nt", outs=_RMS_BWD_OUTS, extras=(x0, dx1, mix_nw[0:1]), epilogue=_rms_bwd_ep,
                                after=(tok,), name="in_even_dx", **_SQUARE_TILES)
    G["mixer_norm_w"] = jnp.concatenate([d_mix_nw0, d_mix_nw1], axis=0)
    return loss, dx0, G


_SMALL = ["mlp_norm_w", "final_norm_w", "lru_conv_b", "lru_w_r", "lru_b_r", "lru_w_i", "lru_b_i",
          "lru_lambda", "gdn_a_log", "gdn_dt_bias", "gdn_norm_w"]
_PACK_ROWS = 688


def _pack(parts):
    flat = jnp.concatenate([p.reshape(-1) for p in parts])
    return jnp.pad(flat, (0, _PACK_ROWS * 128 - flat.shape[0])).reshape(_PACK_ROWS, 128)


def _unpack(packed, shapes):
    flat = packed.reshape(-1)
    out, off = [], 0
    for s in shapes:
        n = int(np.prod(s))
        out.append(flat[off:off + n].reshape(s))
        off += n
    return out


def kernel(x, mixer_norm_w, mlp_norm_w, final_norm_w, w_in_even, lru_conv_w, lru_conv_b, lru_w_r, lru_b_r, lru_w_i, lru_b_i, lru_lambda, w_out_even, w_in_odd, gdn_conv_w, gdn_a_log, gdn_dt_bias, gdn_norm_w, w_out_odd, w_up, w_down, loss_target, m_mixer_norm_w, m_mlp_norm_w, m_final_norm_w, m_w_in_even, m_lru_conv_w, m_lru_conv_b, m_lru_w_r, m_lru_b_r, m_lru_w_i, m_lru_b_i, m_lru_lambda, m_w_out_even, m_w_in_odd, m_gdn_conv_w, m_gdn_a_log, m_gdn_dt_bias, m_gdn_norm_w, m_w_out_odd, m_w_up, m_w_down, v_mixer_norm_w, v_mlp_norm_w, v_final_norm_w, v_w_in_even, v_lru_conv_w, v_lru_conv_b, v_lru_w_r, v_lru_b_r, v_lru_w_i, v_lru_b_i, v_lru_lambda, v_w_out_even, v_w_in_odd, v_gdn_conv_w, v_gdn_a_log, v_gdn_dt_bias, v_gdn_norm_w, v_w_out_odd, v_w_up, v_w_down):
    Pw = dict(mixer_norm_w=mixer_norm_w, mlp_norm_w=mlp_norm_w, final_norm_w=final_norm_w, w_in_even=w_in_even,
              lru_conv_w=lru_conv_w, lru_conv_b=lru_conv_b, lru_w_r=lru_w_r, lru_b_r=lru_b_r, lru_w_i=lru_w_i,
              lru_b_i=lru_b_i, lru_lambda=lru_lambda, w_out_even=w_out_even, w_in_odd=w_in_odd, gdn_conv_w=gdn_conv_w,
              gdn_a_log=gdn_a_log, gdn_dt_bias=gdn_dt_bias, gdn_norm_w=gdn_norm_w, w_out_odd=w_out_odd, w_up=w_up,
              w_down=w_down)
    Pm = dict(mixer_norm_w=m_mixer_norm_w, mlp_norm_w=m_mlp_norm_w, final_norm_w=m_final_norm_w, w_in_even=m_w_in_even,
              lru_conv_w=m_lru_conv_w, lru_conv_b=m_lru_conv_b, lru_w_r=m_lru_w_r, lru_b_r=m_lru_b_r, lru_w_i=m_lru_w_i,
              lru_b_i=m_lru_b_i, lru_lambda=m_lru_lambda, w_out_even=m_w_out_even, w_in_odd=m_w_in_odd,
              gdn_conv_w=m_gdn_conv_w, gdn_a_log=m_gdn_a_log, gdn_dt_bias=m_gdn_dt_bias, gdn_norm_w=m_gdn_norm_w,
              w_out_odd=m_w_out_odd, w_up=m_w_up, w_down=m_w_down)
    Pv = dict(mixer_norm_w=v_mixer_norm_w, mlp_norm_w=v_mlp_norm_w, final_norm_w=v_final_norm_w, w_in_even=v_w_in_even,
              lru_conv_w=v_lru_conv_w, lru_conv_b=v_lru_conv_b, lru_w_r=v_lru_w_r, lru_b_r=v_lru_b_r, lru_w_i=v_lru_w_i,
              lru_b_i=v_lru_b_i, lru_lambda=v_lru_lambda, w_out_even=v_w_out_even, w_in_odd=v_w_in_odd,
              gdn_conv_w=v_gdn_conv_w, gdn_a_log=v_gdn_a_log, gdn_dt_bias=v_gdn_dt_bias, gdn_norm_w=v_gdn_norm_w,
              w_out_odd=v_w_out_odd, w_up=v_w_up, w_down=v_w_down)
    me = _me()[3]
    T = x.shape[1]

    cols = lambda g: jnp.transpose(g, (1, 0, 2)).reshape(g.shape[1], -1)
    rows = lambda g: g.reshape(-1, g.shape[2])
    wide = lambda g: _odd_assemble(g, "w_in_odd_assemble")
    as_is = lambda g: g
    relay_groups = (("w_in_even",), ("w_up0", "w_down0"), ("w_in_odd",))
    relayed = sum(relay_groups, ())
    (first,), started = _send_start([w_in_even[0].astype(bf16)], False, "gather_start_w_in_even", relayed=[0])
    cast = lambda a: (a + started[0:1, 0:1]).astype(bf16)
    odd_shard = jnp.pad(cast(w_in_odd[0]), ((0, 0), (0, ODD_SHARD_PAD - ODD_SHARD)))
    gather = dict(
        w_in_even=(None, cols), lru_conv_w=(lru_conv_w[0], cols),
        w_out_even=(cast(w_out_even[0]), rows), w_up0=(cast(w_up[0]), as_is), w_down0=(cast(w_down[0]), rows),
        w_in_odd=(odd_shard, wide), gdn_conv_w=(gdn_conv_w[0], cols),
        w_out_odd=(cast(w_out_odd[0]), rows), w_up1=(cast(w_up[1]), as_is), w_down1=(cast(w_down[1]), rows))
    later = [name for name in gather if name != "w_in_even"]
    handles, tok = _send_start([gather[name][0] for name in later], False, "gather_start",
                               relayed=[i for i, name in enumerate(later) if name in relayed])
    handles = dict(zip(later, handles), w_in_even=first)
    relays, landed, full = {}, {}, {}

    def ahead(group, after):
        lands = [_send_wait(handles[n], False, after, f"gather_wait_{n}", relayed=True)[1] for n in group]
        relays[group], token = _relay_start(lands, "relay_start_" + "_".join(group))
        return token

    def weight(name, after):
        if name not in landed and name in relayed:
            group = next(g for g in relay_groups if name in g)
            if group not in relays:
                ahead(group, after)
            landed.update(zip(group, _relay_wait(relays[group], after, "relay_wait_" + "_".join(group))))
        elif name not in landed:
            landed[name] = _send_wait(handles[name], False, after, f"gather_wait_{name}")[1]
        if name not in full:
            full[name] = gather[name][1](landed[name])
        return full[name]

    P = {k: Pw[k] for k in ("mlp_norm_w", "final_norm_w")}
    P["mixer_norm_w"] = mixer_norm_w + tok[0:1, 0:1]
    for k in ("lru_w_r", "lru_w_i", "lru_conv_b", "lru_b_r", "lru_b_i", "lru_lambda", "gdn_a_log", "gdn_dt_bias", "gdn_norm_w"):
        P[k] = Pw[k][0]

    sent = {}

    def sink(grads):
        hs, token = _send_start(list(grads.values()), True, "grads_start_" + "_".join(grads))
        sent.update(zip(grads, hs))
        return token

    loss, dx, G = _local_step(x[0], loss_target[0], P, weight, sink, ahead)
    lanes = lambda a: a.reshape(-1, 128)
    sink(dict(mixer_norm_w=jnp.broadcast_to(lanes(G["mixer_norm_w"])[None], (N_DEV, 2 * D_MODEL // 128, 128))))

    def received(name, after=dx):
        return _send_wait(sent[name], True, after, f"grads_wait_{name}")[1]

    out = {}
    nff = D_FF // N_DEV

    def whole(name, gs):
        out[name] = tuple(_adamw(Pw[name], gs, Pm[name], Pv[name], f"adamw_{name}", layer=0))

    def layers(name):
        res = None
        for l in range(2):
            res = _adamw(Pw[name], received(f"{name}{l}"), Pm[name], Pv[name], f"adamw_{name}{l}", layer=l, prev=res)
        out[name] = tuple(res)

    layers("w_up")
    layers("w_down")
    whole("w_out_odd", received("w_out_odd"))
    out["w_in_odd"] = tuple(_adamw_column_major(w_in_odd, received("w_in_odd"), m_w_in_odd, v_w_in_odd, "adamw_w_in_odd"))
    whole("w_out_even", received("w_out_even"))
    small_shapes = [Pw[k].shape for k in _SMALL]
    pw, pm, pv = (_pack([Q[k] for k in _SMALL]) for Q in (Pw, Pm, Pv))
    sg, sd, sm, sv = _adamw(pw, received("small", out["w_out_even"][1]), pm, pv, "adamw_small")
    for arrs_i, packed_out in enumerate((sg, sd, sm, sv)):
        for k, a in zip(_SMALL, _unpack(packed_out, small_shapes)):
            out.setdefault(k, [None] * 4)[arrs_i] = a
    whole("w_in_even", received("w_in_even", sd))
    out["mixer_norm_w"] = tuple(
        a.reshape(mixer_norm_w.shape) for a in
        _adamw(lanes(mixer_norm_w), received("mixer_norm_w", out["w_in_even"][1]), lanes(m_mixer_norm_w),
               lanes(v_mixer_norm_w), "adamw_mixer_norm_w"))
    n_small = sum(int(np.prod(s)) for s in small_shapes)
    gflat = sg.reshape(-1)
    g_lconv = gflat[n_small:n_small + CONV_K * LRU_WIDTH].reshape(CONV_K, LRU_WIDTH)
    g_gconv = gflat[n_small + CONV_K * LRU_WIDTH:n_small + CONV_K * (LRU_WIDTH + 3072)].reshape(CONV_K, 3072)
    whole("lru_conv_w", lax.dynamic_slice_in_dim(g_lconv, me * 64, 64, axis=1)[None])
    whole("gdn_conv_w", lax.dynamic_slice_in_dim(g_gconv, me * 384, 384, axis=1)[None])

    names = ["mixer_norm_w", "mlp_norm_w", "final_norm_w", "w_in_even", "lru_conv_w", "lru_conv_b", "lru_w_r", "lru_b_r",
             "lru_w_i", "lru_b_i", "lru_lambda", "w_out_even", "w_in_odd", "gdn_conv_w", "gdn_a_log", "gdn_dt_bias",
             "gdn_norm_w", "w_out_odd", "w_up", "w_down"]
    total = gflat[n_small + CONV_K * (LRU_WIDTH + 3072)]
    res = [total, dx[None]]
    for j in range(4):
        res += [out[k][j] for k in names]
    return tuple(res)
```

```python
import math

import numpy as np
import jax
import jax.numpy as jnp
from jax import lax
from jax.experimental import pallas as pl
from jax.experimental.pallas import tpu as pltpu

f32 = jnp.float32
bf16 = jnp.bfloat16

N_DEV = 8
D_MODEL = 1024
D_FF = 4096
EPS = 1e-6
RET_HEADS = 4
RET_CHUNK = 128
RET_STEP = 4
ROPE_THETA = 10000.0
LRU_WIDTH = 512
LRU_C = 8.0
GDN_HEADS = 8
GDN_CHUNK = 64
GDN_STEP = 4
HEAD_DIM = 128
ODD_IN = 4112
ODD_PAD = 4224
ODD_SHARD = ODD_IN // N_DEV
ODD_SHARD_PAD = 640
ADAM_LR, ADAM_B1, ADAM_B2, ADAM_EPS, ADAM_WD, ADAM_STEP = 0.001, 0.9, 0.999, 1e-08, 0.01, 10
VMEM_LIMIT = 56 * 1024 * 1024

_NN = (((1,), (0,)), ((), ()))
_NT = (((1,), (1,)), ((), ()))
_TN = (((0,), (0,)), ((), ()))
MESH = pl.DeviceIdType.MESH


def _cparams(sem):
    return pltpu.CompilerParams(dimension_semantics=sem, vmem_limit_bytes=VMEM_LIMIT)


def _dot(a, b, dn):
    return lax.dot_general(a.astype(bf16), b.astype(bf16), dn, preferred_element_type=f32)


def _dot01(a01, b, dn):
    a = a01.astype(bf16)
    b0 = b.astype(bf16)
    r1 = b - b0.astype(f32)
    b1 = r1.astype(bf16)
    b2 = (r1 - b1.astype(f32)).astype(bf16)
    d = lambda q: lax.dot_general(a, q, dn, preferred_element_type=f32)
    return d(b0) + (d(b1) + d(b2))


def _sigmoid(x):
    return jax.nn.sigmoid(x)


def _silu(x):
    return x * _sigmoid(x)


def _dsilu(x):
    s = _sigmoid(x)
    return s * (1.0 + x * (1.0 - s))


def _softplus(x):
    return jnp.maximum(x, 0.0) + jnp.log1p(jnp.exp(-jnp.abs(x)))


_GELU_C = math.sqrt(2.0 / math.pi)


def _gelu(y):
    return 0.5 * y * (1.0 + jnp.tanh(_GELU_C * (y + 0.044715 * y * y * y)))


def _dgelu(y):
    t = jnp.tanh(_GELU_C * (y + 0.044715 * y * y * y))
    return 0.5 * (1.0 + t) + 0.5 * y * (1.0 - t * t) * _GELU_C * (1.0 + 3.0 * 0.044715 * y * y)


def _matmul(a, b, form, *, outs, name, epilogue=None, extras=(), tm=4096, tn=512, tk=1024, shard_cols=0, a_map=None,
            after=()):
    if form == "tn":
        K, M = a.shape
    else:
        M, K = a.shape
    per_step = 1
    if b.ndim == 3:
        assert form in ("nn", "nt"), name
        N = b.shape[1] if form == "nt" else N_DEV * b.shape[2]
        if form == "nn":
            tn = b.shape[2]
        else:
            per_step = max(1, tk // b.shape[2])
            tk = per_step * b.shape[2]
    else:
        N = b.shape[0] if form == "nt" else b.shape[1]
    ns = N // N_DEV
    if shard_cols:
        tn = ns * shard_cols
    tm, tn, tk = min(tm, M), min(tn, N), min(tk, K)
    assert M % tm == 0 and N % tn == 0 and K % tk == 0, (name, M, N, K, tm, tn, tk)
    nk = K // tk
    dn = {"nn": _NN, "nt": _NT, "tn": _TN}[form]
    if form == "tn":
        a_spec = pl.BlockSpec((tk, tm), lambda i, j, k: (k, i))
    else:
        a_spec = pl.BlockSpec((tm, tk), lambda i, j, k: (i, k))
    if b.ndim == 3:
        b_spec = (pl.BlockSpec((per_step, tn, tk // per_step), lambda i, j, k: (k, j, 0)) if form == "nt"
                  else pl.BlockSpec((None, tk, tn), lambda i, j, k: (j, k, 0)))
    elif form == "nt":
        b_spec = pl.BlockSpec((tn, tk), lambda i, j, k: (j, k))
    else:
        b_spec = pl.BlockSpec((tk, tn), lambda i, j, k: (k, j))
    e_spec = pl.BlockSpec((tm, tn), lambda i, j, k: (i, j))
    v_spec = pl.BlockSpec((1, tn), lambda i, j, k: (0, j))
    if shard_cols:
        o_spec = pl.BlockSpec((shard_cols, tm, ns), lambda i, j, k: (j, i, 0))
        o_shape = (N_DEV, M, ns)
    else:
        o_spec = e_spec
        o_shape = (M, N)
    n_ex = len(extras)
    n_in = 2 + n_ex + len(after)
    sums = [isinstance(o, tuple) for o in outs]
    assert not any(sums) or tn == N, name

    def finish(acc, ex, o_refs, row_tile):
        vals = (acc,) if epilogue is None else epilogue(acc, *[e[...] for e in ex])
        for r, v, is_sum in zip(o_refs, vals, sums):
            if is_sum:
                @pl.when(row_tile == 0)
                def _(r=r, v=v):
                    r[...] = v.astype(r.dtype)

                @pl.when(row_tile > 0)
                def _(r=r, v=v):
                    r[...] += v.astype(r.dtype)
            elif shard_cols:
                for s in range(shard_cols):
                    r[s] = v[:, s * ns:(s + 1) * ns].astype(r.dtype)
            else:
                r[...] = v.astype(r.dtype)

    def prod(a_ref, b_ref):
        if b.ndim == 3 and form == "nt":
            w = tk // per_step
            return sum(_dot(a_ref[:, s * w:(s + 1) * w], b_ref[s], dn) for s in range(1, per_step)) + _dot(a_ref[:, 0:w], b_ref[0], dn)
        av = a_ref[...]
        return _dot(av if a_map is None else a_map(av), b_ref[...], dn)

    def body_one(*refs):
        finish(prod(*refs[:2]), refs[2:2 + n_ex], refs[n_in:], pl.program_id(0))

    def body_acc(*refs):
        a_ref, b_ref = refs[:2]
        acc = refs[-1]
        k = pl.program_id(2)
        row_tile = pl.program_id(0)

        @pl.when(k == 0)
        def _():
            acc[...] = prod(a_ref, b_ref)

        @pl.when((k > 0) & (k < nk - 1))
        def _():
            acc[...] += prod(a_ref, b_ref)

        @pl.when(k == nk - 1)
        def _():
            finish(acc[...] + prod(a_ref, b_ref), refs[2:2 + n_ex], refs[n_in:-1], row_tile)

    return pl.pallas_call(
        body_one if nk == 1 else body_acc, grid=(M // tm, N // tn, nk),
        in_specs=[a_spec, b_spec] + [v_spec if e.shape[0] == 1 else e_spec for e in extras]
        + [pl.BlockSpec(memory_space=pl.ANY)] * len(after),
        out_specs=[v_spec if s else o_spec for s in sums],
        out_shape=[jax.ShapeDtypeStruct((1, N), o[1]) if s else jax.ShapeDtypeStruct(o_shape, o) for o, s in zip(outs, sums)],
        scratch_shapes=[] if nk == 1 else [pltpu.VMEM((tm, tn), f32)],
        compiler_params=_cparams(("arbitrary" if any(sums) else "parallel", "parallel", "arbitrary")), name=name,
    )(a, b, *extras, *after)


def _rms_fwd(x, w, name):
    T, D = x.shape
    tt = min(512, T)
    n = T // tt
    nb = min(3, n)

    def body(x_hbm, w_ref, h_hbm, xb, hb, rsem, wsem):
        load = lambda i: pltpu.make_async_copy(x_hbm.at[pl.ds(i * tt, tt)], xb.at[i % nb], rsem.at[i % nb])
        store = lambda i: pltpu.make_async_copy(hb.at[i % 2], h_hbm.at[pl.ds(i * tt, tt)], wsem.at[i % 2])
        for i in range(nb):
            load(i).start()
        for i in range(n):
            load(i).wait()
            if i >= 2:
                store(i - 2).wait()
            xv = xb[i % nb]
            r = lax.rsqrt(jnp.mean(xv * xv, axis=1, keepdims=True) + EPS)
            hb[i % 2] = (xv * r * w_ref[...]).astype(bf16)
            store(i).start()
            if i + nb < n:
                load(i + nb).start()
        for i in range(max(n - 2, 0), n):
            store(i).wait()

    return pl.pallas_call(
        body, in_specs=[pl.BlockSpec(memory_space=pl.ANY), pl.BlockSpec(memory_space=pltpu.VMEM)],
        out_specs=pl.BlockSpec(memory_space=pl.ANY), out_shape=jax.ShapeDtypeStruct((T, D), bf16),
        scratch_shapes=[pltpu.VMEM((nb, tt, D), f32), pltpu.VMEM((2, tt, D), bf16),
                        pltpu.SemaphoreType.DMA((nb,)), pltpu.SemaphoreType.DMA((2,))],
        compiler_params=_cparams(()), name=name,
    )(x, w)


def _residual_rms_ep(acc, res, w):
    x = res + acc
    r = lax.rsqrt(jnp.mean(x * x, axis=1, keepdims=True) + EPS)
    return x, x * r * w


_RMS_BWD_OUTS = [f32, bf16, ("sum", f32)]


def _rms_bwd_ep(dh, x, dres, w):
    r = lax.rsqrt(jnp.mean(x * x, axis=1, keepdims=True) + EPS)
    xn = x * r
    dhw = dh * w
    dx = dres + r * (dhw - xn * jnp.mean(dhw * xn, axis=1, keepdims=True))
    return dx, dx, jnp.sum(dh * xn, axis=0, keepdims=True)


_LOSS_OUTS = [("sum", f32), f32, bf16, ("sum", f32)]


def _loss_ep(acc, res, w, tgt):
    x = res + acc
    D = x.shape[1]
    r = lax.rsqrt(jnp.mean(x * x, axis=1, keepdims=True) + EPS)
    xn = x * r
    e = xn * w - tgt
    loss = 0.5 * jnp.sum(jnp.mean(e * e, axis=1, keepdims=True), axis=0, keepdims=True)
    dy = e * (1.0 / D)
    dyw = dy * w
    dx = r * (dyw - xn * jnp.mean(dyw * xn, axis=1, keepdims=True))
    return jnp.broadcast_to(loss, (1, D)), dx, dx, jnp.sum(dy * xn, axis=0, keepdims=True)


def _ret_tables():
    H, C = RET_HEADS, RET_CHUNK
    lg = np.log1p(-np.exp2(-5.0 - np.arange(H, dtype=np.float32))).astype(np.float32)
    idx = np.arange(C, dtype=np.float32)
    diff = idx[:, None] - idx[None, :]
    causal = diff >= 0
    dm = np.where(causal[None], np.exp(lg[:, None, None] * np.where(causal, diff, 0.0)[None]), 0.0)
    qd = np.exp(lg[:, None] * (idx[None, :] + 1.0))
    kd = np.exp(lg[:, None] * (C - 1.0 - idx[None, :]))
    cg = np.exp(lg * C)
    tab = np.zeros((H, 4, C, HEAD_DIM), np.float32)
    tab[:, 0] = dm
    tab[:, 1] = qd[:, :, None]
    tab[:, 2] = kd[:, :, None]
    tab[:, 3] = cg[:, None, None]
    return jnp.asarray(tab)


def _rope_tables(T):
    half = HEAD_DIM // 2
    inv = np.float32(ROPE_THETA) ** (-np.arange(half, dtype=np.float32) / np.float32(half))
    ang = np.arange(T, dtype=np.float32)[:, None] * inv[None, :]
    c, s = np.cos(ang), np.sin(ang)
    return jnp.asarray(np.concatenate([c, c], axis=1)), jnp.asarray(np.concatenate([-s, s], axis=1))


def _rope(x, cos, sin):
    return x * cos + pltpu.roll(x, HEAD_DIM // 2, 1) * sin


def _unrope(y, cos, sin):
    return y * cos + pltpu.roll(y * sin, HEAD_DIM // 2, 1)


def _stack_heads(ref, H, f=None):
    parts = [ref[:, h * HEAD_DIM:(h + 1) * HEAD_DIM] for h in range(H)]
    return jnp.stack(parts if f is None else [f(a) for a in parts])


def _ret_fwd(p, cos, sin, tab, name):
    T = p.shape[0]
    C, H = RET_CHUNK, RET_HEADS
    N = T // C
    K = min(RET_STEP, N)
    NS = N // K
    scale = HEAD_DIM ** -0.5

    def body(q_ref, k_ref, v_ref, g_ref, c_ref, s_ref, t_ref, y_ref, o_ref, sp_ref, st):
        @pl.when(pl.program_id(0) == 0)
        def _():
            st[...] = jnp.zeros_like(st)

        dm, qd, kd, cg = t_ref[:, 0], t_ref[:, 1], t_ref[:, 2], t_ref[:, 3]
        S = st[...]
        for c in range(K):
            rows = pl.ds(c * C, C)
            cos_, sin_ = c_ref[rows, :], s_ref[rows, :]
            rot = lambda a: _rope(a, cos_, sin_)
            q = _stack_heads(q_ref.at[rows, :], H, rot)
            k = _stack_heads(k_ref.at[rows, :], H, rot) * scale
            v = _stack_heads(v_ref.at[rows, :], H)
            P = _dot(q, k, _NT3) * dm
            o = _dot(P, v, _NN3) + _dot(q * qd, S, _NN3)
            sp_ref[c] = S
            S = cg * S + _dot(k * kd, v, _TN3)
            r = lax.rsqrt(jnp.mean(o * o, axis=2, keepdims=True) + EPS)
            y = o * r * _silu(_stack_heads(g_ref.at[rows, :], H))
            for h in range(H):
                o_ref[rows, h * HEAD_DIM:(h + 1) * HEAD_DIM] = o[h]
                y_ref[rows, h * HEAD_DIM:(h + 1) * HEAD_DIM] = y[h].astype(bf16)
        st[...] = S

    wide = lambda blk: pl.BlockSpec((K * C, H * HEAD_DIM), lambda n: (n, blk))
    tbl = pl.BlockSpec((K * C, HEAD_DIM), lambda n: (n, 0))
    return pl.pallas_call(
        body, grid=(NS,),
        in_specs=[wide(0), wide(1), wide(2), wide(3), tbl, tbl,
                  pl.BlockSpec((H, 4, C, HEAD_DIM), lambda n: (0, 0, 0, 0))],
        out_specs=[wide(0), wide(0), pl.BlockSpec((K, H, HEAD_DIM, HEAD_DIM), lambda n: (n, 0, 0, 0))],
        out_shape=[jax.ShapeDtypeStruct((T, D_MODEL), bf16), jax.ShapeDtypeStruct((T, H * HEAD_DIM), f32),
                   jax.ShapeDtypeStruct((N, H, HEAD_DIM, HEAD_DIM), f32)],
        scratch_shapes=[pltpu.VMEM((H, HEAD_DIM, HEAD_DIM), f32)],
        compiler_params=_cparams(("arbitrary",)), name=name,
    )(p, p, p, p, cos, sin, tab)


def _ret_bwd(p, cos, sin, tab, o_raw, sprev, dmix, name):
    T = p.shape[0]
    C, H = RET_CHUNK, RET_HEADS
    N = T // C
    K = min(RET_STEP, N)
    NS = N // K
    scale = HEAD_DIM ** -0.5
    W = H * HEAD_DIM

    def body(q_ref, k_ref, v_ref, g_ref, c_ref, s_ref, t_ref, o_ref, sp_ref, dy_ref, d_ref, dst):
        @pl.when(pl.program_id(0) == 0)
        def _():
            dst[...] = jnp.zeros_like(dst)

        dm, qd, kd, cg = t_ref[:, 0], t_ref[:, 1], t_ref[:, 2], t_ref[:, 3]
        dS1 = dst[...]
        for c in reversed(range(K)):
            rows = pl.ds(c * C, C)
            cos_, sin_ = c_ref[rows, :], s_ref[rows, :]
            rot = lambda a: _rope(a, cos_, sin_)
            q = _stack_heads(q_ref.at[rows, :], H, rot)
            k = _stack_heads(k_ref.at[rows, :], H, rot) * scale
            v = _stack_heads(v_ref.at[rows, :], H)
            g = _stack_heads(g_ref.at[rows, :], H)
            S = sp_ref[c]
            o = _stack_heads(o_ref.at[rows, :], H)
            dy = _stack_heads(dy_ref.at[rows, :], H)
            r = lax.rsqrt(jnp.mean(o * o, axis=2, keepdims=True) + EPS)
            nrm = o * r
            dn = dy * _silu(g)
            dg = dy * nrm * _dsilu(g)
            do = r * (dn - nrm * jnp.mean(dn * nrm, axis=2, keepdims=True))
            P = _dot(q, k, _NT3) * dm
            dP = _dot(do, v, _NT3) * dm
            dq = _dot(dP, k, _NN3) + _dot(do, S, _NT3) * qd
            dk = (_dot(dP, q, _TN3) + _dot(v, dS1, _NT3) * kd) * scale
            dv = _dot(P, do, _TN3) + _dot(k * kd, dS1, _NN3)
            dS1 = cg * dS1 + _dot(q * qd, do, _TN3)
            for h in range(H):
                d_ref[rows, h * HEAD_DIM:(h + 1) * HEAD_DIM] = _unrope(dq[h], cos_, sin_).astype(bf16)
                d_ref[rows, W + h * HEAD_DIM:W + (h + 1) * HEAD_DIM] = _unrope(dk[h], cos_, sin_).astype(bf16)
                d_ref[rows, 2 * W + h * HEAD_DIM:2 * W + (h + 1) * HEAD_DIM] = dv[h].astype(bf16)
                d_ref[rows, 3 * W + h * HEAD_DIM:3 * W + (h + 1) * HEAD_DIM] = dg[h].astype(bf16)
        dst[...] = dS1

    rev = lambda blk: pl.BlockSpec((K * C, W), lambda n: (NS - 1 - n, blk))
    tbl = pl.BlockSpec((K * C, HEAD_DIM), lambda n: (NS - 1 - n, 0))
    return pl.pallas_call(
        body, grid=(NS,),
        in_specs=[rev(0), rev(1), rev(2), rev(3), tbl, tbl,
                  pl.BlockSpec((H, 4, C, HEAD_DIM), lambda n: (0, 0, 0, 0)), rev(0),
                  pl.BlockSpec((K, H, HEAD_DIM, HEAD_DIM), lambda n: (NS - 1 - n, 0, 0, 0)), rev(0)],
        out_specs=pl.BlockSpec((K * C, 4 * W), lambda n: (NS - 1 - n, 0)),
        out_shape=jax.ShapeDtypeStruct((T, 6 * W), bf16),
        scratch_shapes=[pltpu.VMEM((H, HEAD_DIM, HEAD_DIM), f32)],
        compiler_params=_cparams(("arbitrary",)), name=name,
    )(p, p, p, p, cos, sin, tab, o_raw, sprev, dmix)


CONV_K = 4
CONV_W = 512


def _conv_block(C, col_off):
    cw = CONV_W if (C % (2 * CONV_W) or col_off % 2) else 2 * CONV_W
    return cw, col_off * CONV_W // cw
PAD = 8
SUB_R = 64


def _conv_fwd(x, col_off, w, b, act, name):
    T = x.shape[0]
    C = w.shape[1]
    cw, cb = _conv_block(C, col_off)
    G = C // cw
    tt = min(1024, T)
    NT = T // tt
    has_b = b is not None

    def body(*refs):
        if has_b:
            x_ref, w_ref, b_ref, y_ref, pad = refs
        else:
            x_ref, w_ref, y_ref, pad = refs
        t = pl.program_id(1)

        @pl.when(t == 0)
        def _():
            pad[pl.ds(0, PAD), :] = jnp.zeros((PAD, cw), f32)

        pad[pl.ds(PAD, tt), :] = x_ref[...]
        for g in range(cw // 128):
            ls = slice(g * 128, (g + 1) * 128)
            wv = w_ref[:, ls]
            for c in range(tt // SUB_R):
                r0 = c * SUB_R
                y = wv[0:1, :] * pad[pl.ds(PAD - 3 + r0, SUB_R), ls]
                for kk in range(1, CONV_K):
                    y = y + wv[kk:kk + 1, :] * pad[pl.ds(PAD - 3 + kk + r0, SUB_R), ls]
                if has_b:
                    y = y + b_ref[:, ls]
                y_ref[pl.ds(r0, SUB_R), ls] = _silu(y) if act else y
        tail = pad[pl.ds(tt, PAD), :]
        pad[pl.ds(0, PAD), :] = tail

    in_specs = [pl.BlockSpec((tt, cw), lambda g, t: (t, cb + g)),
                pl.BlockSpec((CONV_K, cw), lambda g, t: (0, g))]
    args = [x, w]
    if has_b:
        in_specs.append(pl.BlockSpec((1, cw), lambda g, t: (0, g)))
        args.append(b)
    return pl.pallas_call(
        body, grid=(G, NT), in_specs=in_specs,
        out_specs=pl.BlockSpec((tt, cw), lambda g, t: (t, g)),
        out_shape=jax.ShapeDtypeStruct((T, C), f32),
        scratch_shapes=[pltpu.VMEM((tt + PAD, cw), f32)],
        compiler_params=_cparams(("parallel", "arbitrary")), name=name,
    )(*args)


def _conv_bwd(x, col_off, w, b, act, dout, dp, name):
    T = x.shape[0]
    C = w.shape[1]
    cw, cb = _conv_block(C, col_off)
    G = C // cw
    tt = min(1024, T)
    NT = T // tt
    has_b = b is not None

    def body(*refs):
        if has_b:
            x_ref, xp_ref, w_ref, b_ref, d_ref, dp_in, dx_ref, dw_ref, db_ref, pad, dpad = refs
        else:
            x_ref, xp_ref, w_ref, d_ref, dp_in, dx_ref, dw_ref, db_ref, pad, dpad = refs
        t = pl.program_id(1)
        first_tile = t == NT - 1

        @pl.when(t == 0)
        def _():
            dpad[pl.ds(tt, PAD), :] = jnp.zeros((PAD, cw), f32)
            dw_ref[...] = jnp.zeros_like(dw_ref)
            db_ref[...] = jnp.zeros_like(db_ref)

        pad[pl.ds(0, PAD), :] = jnp.where(first_tile, 0.0, xp_ref[...])
        pad[pl.ds(PAD, tt), :] = x_ref[...]
        fold = lambda v: v.reshape(SUB_R // 8, 8, 128).sum(axis=0)
        for g in range(cw // 128):
            ls = slice(g * 128, (g + 1) * 128)
            wv = w_ref[:, ls]
            acc =[jnp.zeros((8, 128), f32) for _ in range(CONV_K + 1)]
            for c in reversed(range(tt // SUB_R)):
                r0 = c * SUB_R
                xs = [pad[pl.ds(PAD - 3 + kk + r0, SUB_R), ls] for kk in range(CONV_K)]
                dy = d_ref[pl.ds(r0, SUB_R), ls]
                if act:
                    y = wv[0:1, :] * xs[0]
                    for kk in range(1, CONV_K):
                        y = y + wv[kk:kk + 1, :] * xs[kk]
                    if has_b:
                        y = y + b_ref[:, ls]
                    dy = dy * _dsilu(y)
                dpad[pl.ds(r0, SUB_R), ls] = dy
                dx = wv[3:4, :] * dy
                for j in range(1, CONV_K):
                    dx = dx + wv[3 - j:4 - j, :] * dpad[pl.ds(r0 + j, SUB_R), ls]
                dx_ref[pl.ds(r0, SUB_R), ls] = dx.astype(bf16)
                for kk in range(CONV_K):
                    acc[kk] = acc[kk] + fold(dy * xs[kk])
                acc[CONV_K] = acc[CONV_K] + fold(dy)
            for kk in range(CONV_K):
                dw_ref[kk:kk + 1, ls] += jnp.sum(acc[kk], axis=0, keepdims=True)
            db_ref[:, ls] += jnp.sum(acc[CONV_K], axis=0, keepdims=True)
        head = dpad[pl.ds(0, PAD), :]
        dpad[pl.ds(tt, PAD), :] = head

    rows8 = tt // PAD
    in_specs = [pl.BlockSpec((tt, cw), lambda g, t: (NT - 1 - t, cb + g)),
                pl.BlockSpec((PAD, cw), lambda g, t: (jnp.maximum((NT - 1 - t) * rows8 - 1, 0), cb + g)),
                pl.BlockSpec((CONV_K, cw), lambda g, t: (0, g))]
    args = [x, x, w]
    if has_b:
        in_specs.append(pl.BlockSpec((1, cw), lambda g, t: (0, g)))
        args.append(b)
    in_specs += [pl.BlockSpec((tt, cw), lambda g, t: (NT - 1 - t, g)), pl.BlockSpec(memory_space=pl.ANY)]
    args += [dout, dp]
    return pl.pallas_call(
        body, grid=(G, NT), in_specs=in_specs,
        out_specs=[pl.BlockSpec((tt, cw), lambda g, t: (NT - 1 - t, cb + g)),
                   pl.BlockSpec((CONV_K, cw), lambda g, t: (0, g)),
                   pl.BlockSpec((1, cw), lambda g, t: (0, g))],
        out_shape=[jax.ShapeDtypeStruct(dp.shape, dp.dtype), jax.ShapeDtypeStruct((CONV_K, C), f32),
                   jax.ShapeDtypeStruct((1, C), f32)],
        input_output_aliases={len(args) - 1: 0},
        scratch_shapes=[pltpu.VMEM((tt + PAD, cw), f32), pltpu.VMEM((tt + PAD, cw), f32)],
        compiler_params=_cparams(("parallel", "arbitrary")), name=name,
    )(*args)


def _lru_gates(xc, wr, wi, br, bi, lam):
    r = _sigmoid(_dot(xc, wr, _NN) + br)
    i = _sigmoid(_dot(xc, wi, _NN) + bi)
    sp = _softplus(-lam)
    a = jnp.exp(-LRU_C * r * sp)
    mult = jnp.sqrt(1.0 - a * a)
    return r, i, sp, a, mult


def _lru_fwd(xc, p, y_off, wr, wi, br, bi, lam, mix, name):
    T = xc.shape[0]
    G = LRU_WIDTH // 128
    tt = min(512, T)
    NT = T // tt

    def body(x_ref, y_ref, wr_ref, wi_ref, br_ref, bi_ref, l_ref, mix_in, o_ref, h_ref, hc):
        t = pl.program_id(1)

        @pl.when(t == 0)
        def _():
            hc[...] = jnp.zeros_like(hc)

        x = x_ref[...]
        r, i, sp, a, mult = _lru_gates(x, wr_ref[...], wi_ref[...], br_ref[...], bi_ref[...], l_ref[...])
        row = lax.broadcasted_iota(jnp.int32, (tt, 128), 0)
        mult = jnp.where((row == 0) & (t == 0), 1.0, mult)
        U = x * i * mult
        A = a
        d = 1
        while d < tt:
            keep = row >= d
            Ush = jnp.where(keep, pltpu.roll(U, d, 0), 0.0)
            Ash = jnp.where(keep, pltpu.roll(A, d, 0), 1.0)
            U = A * Ush + U
            A = A * Ash
            d *= 2
        h = U + A * hc[0:1, :]
        h_ref[...] = h
        hc[...] = jnp.broadcast_to(h[tt - 1:tt, :], hc.shape)
        o_ref[...] = (h * _gelu(y_ref[...])).astype(bf16)

    tile = pl.BlockSpec((tt, 128), lambda g, t: (t, g))
    vec = pl.BlockSpec((1, 128), lambda g, t: (0, g))
    wsp = pl.BlockSpec((128, 128), lambda g, t: (g, g))
    return pl.pallas_call(
        body, grid=(G, NT),
        in_specs=[tile, pl.BlockSpec((tt, 128), lambda g, t: (t, y_off + g)), wsp, wsp, vec, vec, vec,
                  pl.BlockSpec(memory_space=pl.ANY)],
        out_specs=[pl.BlockSpec((tt, 128), lambda g, t: (t, G + g)), tile],
        out_shape=[jax.ShapeDtypeStruct(mix.shape, mix.dtype), jax.ShapeDtypeStruct((T, LRU_WIDTH), f32)],
        input_output_aliases={7: 0},
        scratch_shapes=[pltpu.VMEM((8, 128), f32)],
        compiler_params=_cparams(("parallel", "arbitrary")), name=name,
    )(xc, p, wr, wi, br, bi, lam, mix)


def _lru_bwd(xc, p, y_off, wr, wi, br, bi, lam, hs, dmix, d_off, dp, name):
    T = xc.shape[0]
    G = LRU_WIDTH // 128
    tt = min(512, T)
    NT = T // tt

    def body(x_ref, y_ref, wr_ref, wi_ref, br_ref, bi_ref, l_ref, h_ref, hp_ref, do_ref, dp_in,
             dx_ref, dy_ref, dwr_ref, dwi_ref, dbr_ref, dbi_ref, dl_ref, lc, an):
        t = pl.program_id(1)
        first_tile = t == NT - 1

        @pl.when(t == 0)
        def _():
            lc[...] = jnp.zeros_like(lc)
            an[...] = jnp.zeros_like(an)
            dwr_ref[...] = jnp.zeros_like(dwr_ref)
            dwi_ref[...] = jnp.zeros_like(dwi_ref)
            dbr_ref[...] = jnp.zeros_like(dbr_ref)
            dbi_ref[...] = jnp.zeros_like(dbi_ref)
            dl_ref[...] = jnp.zeros_like(dl_ref)

        x = x_ref[...]
        y = y_ref[...]
        wr, wi, lam_ = wr_ref[...], wi_ref[...], l_ref[...]
        r, i, sp, a, mult_raw = _lru_gates(x, wr, wi, br_ref[...], bi_ref[...], lam_)
        row = lax.broadcasted_iota(jnp.int32, (tt, 128), 0)
        t0 = (row == 0) & first_tile
        mult = jnp.where(t0, 1.0, mult_raw)
        h = h_ref[...]
        do = do_ref[...]
        dh = do * _gelu(y)
        dy_ref[...] = (do * h * _dgelu(y)).astype(bf16)
        B = jnp.where(row == tt - 1, an[0:1, :], pltpu.roll(a, tt - 1, 0))
        L = dh
        d = 1
        while d < tt:
            keep = row < tt - d
            Lsh = jnp.where(keep, pltpu.roll(L, tt - d, 0), 0.0)
            Bsh = jnp.where(keep, pltpu.roll(B, tt - d, 0), 1.0)
            L = L + B * Lsh
            B = B * Bsh
            d *= 2
        L = L + B * lc[0:1, :]
        lc[...] = jnp.broadcast_to(L[0:1, :], lc.shape)
        an[...] = jnp.broadcast_to(a[0:1, :], an.shape)
        hprev = jnp.where(first_tile, 0.0, hp_ref[...])[PAD - 1:PAD, :]
        hm1 = jnp.where(row == 0, hprev, pltpu.roll(h, 1, 0))
        da = L * hm1
        dxc = L * i * mult
        di = L * x * mult
        dmult = jnp.where(t0, 0.0, L * x * i)
        da = da - jnp.where(t0, 0.0, dmult * a / mult_raw)
        dlog_a = da * a
        dr = dlog_a * (-LRU_C) * sp
        dsp = jnp.sum(dlog_a * (-LRU_C) * r, axis=0, keepdims=True)
        dpr = dr * r * (1.0 - r)
        dpi = di * i * (1.0 - i)
        dx_ref[...] = dxc + _dot(dpr, wr, _NT) + _dot(dpi, wi, _NT)
        for d_ref, dpre in ((dwr_ref, dpr), (dwi_ref, dpi)):
            dw = _dot(x, dpre, _TN)
            for s in range(2):
                d_ref[s] += dw[s * 64:(s + 1) * 64, s * 64:(s + 1) * 64]
        dbr_ref[...] += jnp.sum(dpr, axis=0, keepdims=True)
        dbi_ref[...] += jnp.sum(dpi, axis=0, keepdims=True)
        dl_ref[...] += dsp * (-_sigmoid(-lam_))

    rows8 = tt // PAD
    tile = pl.BlockSpec((tt, 128), lambda g, t: (NT - 1 - t, g))
    vec = pl.BlockSpec((1, 128), lambda g, t: (0, g))
    wsp = pl.BlockSpec((128, 128), lambda g, t: (g, g))
    wout = pl.BlockSpec((2, 64, 64), lambda g, t: (g, 0, 0))
    return pl.pallas_call(
        body, grid=(G, NT),
        in_specs=[tile, pl.BlockSpec((tt, 128), lambda g, t: (NT - 1 - t, y_off + g)), wsp, wsp, vec, vec, vec, tile,
                  pl.BlockSpec((PAD, 128), lambda g, t: (jnp.maximum((NT - 1 - t) * rows8 - 1, 0), g)),
                  pl.BlockSpec((tt, 128), lambda g, t: (NT - 1 - t, d_off + g)), pl.BlockSpec(memory_space=pl.ANY)],
        out_specs=[tile, pl.BlockSpec((tt, 128), lambda g, t: (NT - 1 - t, y_off + g)), wout, wout, vec, vec, vec],
        out_shape=[jax.ShapeDtypeStruct((T, LRU_WIDTH), f32), jax.ShapeDtypeStruct(dp.shape, dp.dtype),
                   jax.ShapeDtypeStruct((2 * G, 64, 64), f32), jax.ShapeDtypeStruct((2 * G, 64, 64), f32),
                   jax.ShapeDtypeStruct((1, LRU_WIDTH), f32), jax.ShapeDtypeStruct((1, LRU_WIDTH), f32),
                   jax.ShapeDtypeStruct((1, LRU_WIDTH), f32)],
        input_output_aliases={10: 1},
        scratch_shapes=[pltpu.VMEM((8, 128), f32), pltpu.VMEM((8, 128), f32)],
        compiler_params=_cparams(("parallel", "arbitrary")), name=name,
    )(xc, p, wr, wi, br, bi, lam, hs, hs, dmix, dp)


_NN3 = (((2,), (1,)), ((0,), (0,)))
_NT3 = (((2,), (2,)), ((0,), (0,)))
_TN3 = (((1,), (1,)), ((0,), (0,)))


def _pairs(ref, K):
    C = GDN_CHUNK
    return jnp.stack([ref[c * C:(c + 1) * C, h * HEAD_DIM:(h + 1) * HEAD_DIM] for c in range(K) for h in range(GDN_HEADS)])


def _put_pairs(ref, val, K, col=0):
    C, H = GDN_CHUNK, GDN_HEADS
    for c in range(K):
        for h in range(H):
            ref[c * C:(c + 1) * C, col + h * HEAD_DIM:col + (h + 1) * HEAD_DIM] = val[c * H + h].astype(ref.dtype)


def _rowsum(x):
    H, C, L = x.shape
    return _dot(x.reshape(H * C, L), jnp.ones((L, HEAD_DIM), f32), _NN).reshape(H, C, HEAD_DIM)


def _gdn_pre(qr, kr, v, ba, alog, dtb):
    C, H = GDN_CHUNK, GDN_HEADS
    B = qr.shape[0]
    K = B // H
    lane = lax.broadcasted_iota(jnp.int32, (C, 128), 1)
    lane3 = lax.broadcasted_iota(jnp.int32, (B, C, 128), 2)
    ri = lax.broadcasted_iota(jnp.int32, (C, C), 0)
    ci = lax.broadcasted_iota(jnp.int32, (C, C), 1)
    rowc = lax.broadcasted_iota(jnp.int32, (C, 1), 0)
    col = lambda m, j: jnp.sum(jnp.where(lane == j, m, 0.0), axis=1, keepdims=True)
    ea = jnp.exp(alog)
    tri = (ri >= ci).astype(f32)
    g_all, beta_cols, G_cols = [], [], []
    for c in range(K):
        ba_c = ba[c * C:(c + 1) * C]
        g_c = -ea * _softplus(ba_c + dtb)
        G_c = _dot01(tri, g_c, _NN)
        s_c = _sigmoid(ba_c)
        g_all.append(g_c)
        beta_cols += [col(s_c, h) for h in range(H)]
        G_cols += [col(G_c, H + h) for h in range(H)]
    wide = lambda c: jnp.broadcast_to(c, (B, C, 128))
    beta = wide(jnp.stack(beta_cols))
    Gc = jnp.stack(G_cols)
    rq = lax.rsqrt(_rowsum(qr * qr) + EPS)
    rk = lax.rsqrt(_rowsum(kr * kr) + EPS)
    qh, kn = qr * rq, kr * rk
    qn = qh * (HEAD_DIM ** -0.5)
    Grow = _dot01(jnp.ones((B, C, 128), f32), jnp.where(lane3 == 0, Gc, 0.0), _NT3)
    incl = ri >= ci
    Di = jnp.where(incl, jnp.exp(jnp.where(incl, Gc - Grow, 0.0)), 0.0)
    Ds = jnp.where(ri > ci, Di, 0.0)
    Gl = jnp.sum(jnp.where(rowc == C - 1, Gc, 0.0), axis=1, keepdims=True)
    eG = wide(jnp.exp(Gc))
    eGl = wide(jnp.exp(Gl - Gc))
    cd = jnp.exp(Gl)
    kb = kn * beta
    vb = v * beta
    Lm = _dot(kb, kn, _NT3) * Ds
    kbg = kb * eG
    QK = _dot(qn, kn, _NT3) * Di
    qg = qn * eG
    kg = kn * eGl
    return dict(beta=beta, g_all=g_all, rq=rq, rk=rk, qh=qh, kn=kn, qn=qn, Di=Di, Ds=Ds, eG=eG, eGl=eGl, cd=cd,
                kb=kb, vb=vb, Lm=Lm, kbg=kbg, QK=QK, qg=qg, kg=kg, lane=lane, ri=ri, ci=ci, rowc=rowc, ea=ea)


def _unit_lower_inverse(Lm):
    C = Lm.shape[-1]
    ri = lax.broadcasted_iota(jnp.int32, (C, C), 0)
    ci = lax.broadcasted_iota(jnp.int32, (C, C), 1)
    same = lambda s: (ri // s) == (ci // s)
    Xd = jnp.where(same(8), -Lm, 0.0)
    Tinv = (ri == ci).astype(f32) + Xd
    Pw = Xd
    for _ in range(2):
        Pw = _dot(Pw, Pw, _NN3)
        Tinv = Tinv + _dot(Tinv, Pw, _NN3)
    for s in (8, 16, 32):
        off = jnp.where(same(2 * s) & jnp.logical_not(same(s)), Lm, 0.0)
        Tinv = Tinv - _dot(_dot(Tinv, off, _NN3), Tinv, _NN3)
    return Tinv


def _gdn_specs(T, rev):
    C = GDN_CHUNK
    H = GDN_HEADS
    K = min(GDN_STEP, T // C)
    NS = T // (C * K)
    nn = (lambda n: NS - 1 - n) if rev else (lambda n: n)
    wide = lambda blk: pl.BlockSpec((K * C, H * HEAD_DIM), lambda n: (nn(n), blk))
    one = lambda off: pl.BlockSpec((K * C, HEAD_DIM), lambda n: (nn(n), off))
    vec = pl.BlockSpec((1, 128), lambda n: (0, 0))
    st = lambda rows: pl.BlockSpec((K, H, rows, rows), lambda n: (nn(n), 0, 0, 0))
    return K, NS, wide, one, vec, st


def _gdn_fwd(qkv, p, alog, dtb, nw, name):
    T = qkv.shape[0]
    C, H = GDN_CHUNK, GDN_HEADS
    N = T // C
    K, NS, wide, one, vec, st_spec = _gdn_specs(T, False)

    def body(q_ref, k_ref, v_ref, z_ref, ba_ref, al_ref, dt_ref, nw_ref, y_ref, sp_ref, ti_ref, vn_ref, o_ref, st):
        @pl.when(pl.program_id(0) == 0)
        def _():
            st[...] = jnp.zeros_like(st)

        f = _gdn_pre(_pairs(q_ref, K), _pairs(k_ref, K), _pairs(v_ref, K), ba_ref[...], al_ref[...], dt_ref[...])
        Tinv = _unit_lower_inverse(f["Lm"])
        ti_ref[...] = Tinv.reshape(K, H, C, C).astype(bf16)
        w = _dot(Tinv, f["kbg"], _NN3)
        u = _dot(Tinv, f["vb"], _NN3)
        S = st[...]
        vns, os_ = [], []
        for c in range(K):
            sl = slice(c * H, (c + 1) * H)
            sp_ref[c] = S
            vn_c = u[sl] - _dot(w[sl], S, _NN3)
            os_.append(_dot(f["qg"][sl], S, _NN3) + _dot(f["QK"][sl], vn_c, _NN3))
            S = S * f["cd"][sl] + _dot(f["kg"][sl], vn_c, _TN3)
            vns.append(vn_c)
        st[...] = S
        vn, o = jnp.concatenate(vns), jnp.concatenate(os_)
        r = lax.rsqrt(_rowsum(o * o) * (1.0 / HEAD_DIM) + EPS)
        _put_pairs(y_ref, o * r * nw_ref[...] * _silu(_pairs(z_ref, K)), K)
        _put_pairs(vn_ref, vn, K)
        _put_pairs(o_ref, o, K)

    wide_f32 = jax.ShapeDtypeStruct((T, H * HEAD_DIM), f32)
    return pl.pallas_call(
        body, grid=(NS,),
        in_specs=[wide(0), wide(1), wide(2), wide(3), one(4 * H), vec, vec, vec],
        out_specs=[wide(0), st_spec(HEAD_DIM), st_spec(C), wide(0), wide(0)],
        out_shape=[jax.ShapeDtypeStruct((T, H * HEAD_DIM), bf16), jax.ShapeDtypeStruct((N, H, HEAD_DIM, HEAD_DIM), f32),
                   jax.ShapeDtypeStruct((N, H, C, C), bf16), jax.ShapeDtypeStruct((T, H * HEAD_DIM), bf16), wide_f32],
        scratch_shapes=[pltpu.VMEM((H, HEAD_DIM, HEAD_DIM), f32)],
        compiler_params=_cparams(("arbitrary",)), name=name,
    )(qkv, qkv, qkv, p, p, alog, dtb, nw)


def _gdn_bwd(qkv, p, alog, dtb, nw, sprev, tinv, vn_all, o_all, dy_all, name):
    T = qkv.shape[0]
    C, H = GDN_CHUNK, GDN_HEADS
    N = T // C
    K, NS, wide, one, vec, st_spec = _gdn_specs(T, True)
    rs = lambda m: jnp.sum(m, axis=2, keepdims=True)

    def body(q_ref, k_ref, v_ref, z_ref, ba_ref, al_ref, dt_ref, nw_ref, sp_ref, ti_ref, vn_ref, o_ref, dy_ref,
             dqkv_ref, dz_ref, dba_ref, dal_ref, ddt_ref, dnw_ref, dst):
        @pl.when(pl.program_id(0) == 0)
        def _():
            dst[...] = jnp.zeros_like(dst)
            dal_ref[...] = jnp.zeros_like(dal_ref)
            ddt_ref[...] = jnp.zeros_like(ddt_ref)
            dnw_ref[...] = jnp.zeros_like(dnw_ref)

        ba, dtb_, nwv = ba_ref[...], dt_ref[...], nw_ref[...]
        v = _pairs(v_ref, K)
        f = _gdn_pre(_pairs(q_ref, K), _pairs(k_ref, K), v, ba, al_ref[...], dtb_)
        beta, kn, qn, kb, vb, kbg = f["beta"], f["kn"], f["qn"], f["kb"], f["vb"], f["kbg"]
        eG, eGl, cd, Di, Ds, QK, qg, kg = f["eG"], f["eGl"], f["cd"], f["Di"], f["Ds"], f["QK"], f["qg"], f["kg"]
        lane, ri, ci, rowc = f["lane"], f["ri"], f["ci"], f["rowc"]
        Tinv = ti_ref[...].reshape(K * H, C, C)
        S = sp_ref[...].reshape(K * H, HEAD_DIM, HEAD_DIM)
        w_ = _dot(Tinv, kbg, _NN3)
        vn, o = _pairs(vn_ref, K), _pairs(o_ref, K)
        z, dy = _pairs(z_ref, K), _pairs(dy_ref, K)
        r = lax.rsqrt(_rowsum(o * o) * (1.0 / HEAD_DIM) + EPS)
        nrm = o * r
        sz = _silu(z)
        dn = dy * nwv * sz
        _put_pairs(dz_ref, dy * nrm * nwv * _dsilu(z), K)
        dnw_ref[...] += jnp.sum(jnp.sum(dy * nrm * sz, axis=0), axis=0, keepdims=True)
        do = r * (dn - nrm * (_rowsum(dn * nrm) * (1.0 / HEAD_DIM)))
        dvn_do = _dot(QK, do, _TN3)
        dS_do = _dot(qg, do, _TN3)
        dqg = _dot(do, S, _NT3)
        dQK = _dot(do, vn, _NT3)
        dS = dst[...]
        dS1s, dvns = [None] * K, [None] * K
        for c in reversed(range(K)):
            sl = slice(c * H, (c + 1) * H)
            dS1s[c] = dS
            dvns[c] = _dot(kg[sl], dS, _NN3) + dvn_do[sl]
            dS = cd[sl] * dS + dS_do[sl] - _dot(w_[sl], dvns[c], _TN3)
        dst[...] = dS
        dS1, dvn = jnp.concatenate(dS1s), jnp.concatenate(dvns)
        dcd = jnp.sum(jnp.sum(S * dS1, axis=2, keepdims=True), axis=1, keepdims=True)
        dkg = _dot(vn, dS1, _NT3)
        dw = -_dot(dvn, S, _NT3)
        dqn = dqg * eG
        dkn = dkg * eGl
        deGl = rs(dkg * kn)
        dQKr = dQK * Di
        E = dQK * QK
        dqn = dqn + _dot(dQKr, kn, _NN3)
        dkn = dkn + _dot(dQKr, qn, _TN3)
        dT = _dot(dvn, vb, _NT3) + _dot(dw, kbg, _NT3)
        dvb = _dot(Tinv, dvn, _TN3)
        dkbg = _dot(Tinv, dw, _TN3)
        dkb = dkbg * eG
        deG = rs(dqg * qn + dkbg * kb)
        dL = -_dot(_dot(Tinv, dT, _TN3), Tinv, _NT3)
        dKK = dL * Ds
        E = E + dL * f["Lm"]
        dkb = dkb + _dot(dKK, kn, _NN3)
        dkn = dkn + _dot(dKK, kb, _TN3) + dkb * beta
        dbeta = rs(dkb * kn + dvb * v)
        _put_pairs(dqkv_ref, dvb * beta, K, 2 * H * HEAD_DIM)
        dG = rs(E) - rs(jnp.swapaxes(E, 1, 2)) + deG * eG - deGl * eGl
        dGl = jnp.sum(deGl * eGl, axis=1, keepdims=True) + dcd * cd
        dG = dG + jnp.where(rowc == C - 1, dGl, 0.0)
        qh = f["qh"]
        _put_pairs(dqkv_ref, (HEAD_DIM ** -0.5) * f["rq"] * (dqn - qh * _rowsum(dqn * qh)), K)
        _put_pairs(dqkv_ref, f["rk"] * (dkn - kn * _rowsum(dkn * kn)), K, H * HEAD_DIM)
        db = dbeta * beta * (1.0 - beta)
        triu = (ri <= ci).astype(f32)
        for c in range(K):
            db_all = jnp.where(lane == 0, db[c * H], 0.0)
            dG_all = jnp.where(lane == H, dG[c * H], 0.0)
            for h in range(1, H):
                db_all = db_all + jnp.where(lane == h, db[c * H + h], 0.0)
                dG_all = dG_all + jnp.where(lane == H + h, dG[c * H + h], 0.0)
            dg_all = _dot01(triu, dG_all, _NN)
            da_all = dg_all * (-f["ea"]) * _sigmoid(ba[c * C:(c + 1) * C] + dtb_)
            dba_ref[c * C:(c + 1) * C, :] = (db_all + da_all).astype(bf16)
            ddt_ref[...] += jnp.sum(da_all, axis=0, keepdims=True)
            dal_ref[...] += jnp.sum(dg_all * f["g_all"][c], axis=0, keepdims=True)

    small = jax.ShapeDtypeStruct((1, 128), f32)
    return pl.pallas_call(
        body, grid=(NS,),
        in_specs=[wide(0), wide(1), wide(2), wide(3), one(4 * H), vec, vec, vec, st_spec(HEAD_DIM), st_spec(C),
                  wide(0), wide(0), wide(0)],
        out_specs=[pl.BlockSpec((K * C, 3 * H * HEAD_DIM), lambda n: (NS - 1 - n, 0)), wide(3), one(0), vec, vec, vec],
        out_shape=[jax.ShapeDtypeStruct((T, 3 * H * HEAD_DIM), f32), jax.ShapeDtypeStruct((T, ODD_PAD), bf16),
                   jax.ShapeDtypeStruct((T, 128), bf16), small, small, small],
        scratch_shapes=[pltpu.VMEM((H, HEAD_DIM, HEAD_DIM), f32)],
        compiler_params=_cparams(("arbitrary",)), name=name,
    )(qkv, qkv, qkv, p, p, alog, dtb, nw, sprev, tinv, vn_all, o_all, dy_all)


def _lanes_from(x, s):
    return x if s % 128 == 0 else pltpu.roll(x, (128 - s) % 128, 1)


def _odd_assemble(g, name):
    R = g.shape[1]
    tr = min(256, R)
    n_blk = ODD_SHARD_PAD // 128

    def body(g_ref, o_ref):
        lane = lax.broadcasted_iota(jnp.int32, (tr, 128), 1)
        blk = lambda d, m: g_ref[d, :, m * 128:(m + 1) * 128]
        for gb in range(ODD_PAD // 128):
            c0 = 128 * gb
            if c0 >= ODD_IN:
                o_ref[:, c0:c0 + 128] = jnp.zeros((tr, 128), g.dtype)
                continue
            d0 = c0 // ODD_SHARD
            m0, sh = divmod(c0 - ODD_SHARD * d0, 128)
            take = min(128, ODD_SHARD * (d0 + 1) - c0)
            p = _lanes_from(blk(d0, m0), sh)
            if sh and m0 + 1 < n_blk:
                p = jnp.where(lane < 128 - sh, p, _lanes_from(blk(d0, m0 + 1), sh))
            if take < 128:
                nxt = pltpu.roll(blk(d0 + 1, 0), take, 1) if d0 + 1 < N_DEV else jnp.zeros((tr, 128), g.dtype)
                p = jnp.where(lane < take, p, nxt)
            o_ref[:, c0:c0 + 128] = p

    return pl.pallas_call(
        body, grid=(R // tr,),
        in_specs=[pl.BlockSpec((N_DEV, tr, ODD_SHARD_PAD), lambda i: (0, i, 0))],
        out_specs=pl.BlockSpec((tr, ODD_PAD), lambda i: (i, 0)),
        out_shape=jax.ShapeDtypeStruct((R, ODD_PAD), g.dtype),
        compiler_params=_cparams(("parallel",)), name=name,
    )(g)


def _odd_split(w, name):
    R = w.shape[0]
    tr = min(256, R)

    def body(w_ref, o_ref):
        lane = lax.broadcasted_iota(jnp.int32, (tr, 128), 1)
        blk = lambda gb: w_ref[:, gb * 128:(gb + 1) * 128]
        for d in range(N_DEV):
            for m in range(ODD_SHARD_PAD // 128):
                g0, sh = divmod(ODD_SHARD * d + 128 * m, 128)
                p = _lanes_from(blk(g0), sh)
                if sh and g0 + 1 < ODD_PAD // 128:
                    p = jnp.where(lane < 128 - sh, p, _lanes_from(blk(g0 + 1), sh))
                real = ODD_SHARD - 128 * m
                if real < 128:
                    p = jnp.where(lane < real, p, jnp.zeros_like(p))
                o_ref[d, :, m * 128:(m + 1) * 128] = p

    return pl.pallas_call(
        body, grid=(R // tr,),
        in_specs=[pl.BlockSpec((tr, ODD_PAD), lambda i: (i, 0))],
        out_specs=pl.BlockSpec((N_DEV, tr, ODD_SHARD_PAD), lambda i: (0, i, 0)),
        out_shape=jax.ShapeDtypeStruct((N_DEV, R, ODD_SHARD_PAD), w.dtype),
        compiler_params=_cparams(("parallel",)), name=name,
    )(w)


def _adam_tile(g, w_ref, m_ref, v_ref, go_ref, d_ref, mo_ref, vo_ref):
    c1 = 1.0 - ADAM_B1 ** ADAM_STEP
    c2 = 1.0 - ADAM_B2 ** ADAM_STEP
    mn = ADAM_B1 * m_ref[...] + (1.0 - ADAM_B1) * g
    vn = ADAM_B2 * v_ref[...] + (1.0 - ADAM_B2) * (g * g)
    go_ref[...] = g
    mo_ref[...] = mn
    vo_ref[...] = vn
    d_ref[...] = -ADAM_LR * ((mn / c1) / (jnp.sqrt(vn / c2) + ADAM_EPS) + ADAM_WD * w_ref[...])


def _adamw(w, gs, m, v, name, layer=None, prev=None):
    R, Cc = w.shape[-2:]
    S = gs.shape[0]
    tr = R
    if S * R * Cc * 4 > (4 << 20):
        for cand in (512, 256, 128, 64, 32, 16, 8):
            if R % cand == 0 and R > cand:
                tr = cand
                break

    def body(w_ref, g_ref, m_ref, v_ref, *rest):
        g = g_ref[0].astype(f32)
        for s in range(1, S):
            g = g + g_ref[s].astype(f32)
        _adam_tile(g, w_ref, m_ref, v_ref, *rest[-4:])

    if layer is None:
        blk = pl.BlockSpec((tr, Cc), lambda i: (i, 0))
    else:
        blk = pl.BlockSpec((None, tr, Cc), lambda i: (layer, i, 0))
    out = jax.ShapeDtypeStruct(w.shape, f32)
    carried = [] if prev is None else list(prev)
    return pl.pallas_call(
        body, grid=(R // tr,),
        in_specs=[blk, pl.BlockSpec((S, tr, Cc), lambda i: (0, i, 0)), blk, blk]
        + [pl.BlockSpec(memory_space=pl.ANY)] * len(carried),
        out_specs=[blk] * 4, out_shape=[out] * 4,
        input_output_aliases={4 + j: j for j in range(len(carried))},
        compiler_params=_cparams(("parallel",)), name=name,
    )(w, gs, m, v, *carried)


def _adamw_column_major(w, gs, m, v, name):
    _, R, Cc = w.shape
    S = gs.shape[0]
    q = R // 128
    dense = lambda a: jnp.transpose(a, (2, 0, 1)).reshape(Cc * q, 128)
    back = lambda a: jnp.transpose(a.reshape(Cc, q, 128), (1, 2, 0)).reshape(1, R, Cc)

    def body(w_ref, g_ref, m_ref, v_ref, go_ref, d_ref, mo_ref, vo_ref, gt):
        for i in range(q):
            g = g_ref[0, i * 128:(i + 1) * 128, :].astype(f32)
            for s in range(1, S):
                g = g + g_ref[s, i * 128:(i + 1) * 128, :].astype(f32)
            gt[pl.ds(i, 128, stride=q), :] = g.T
        _adam_tile(gt[...], w_ref, m_ref, v_ref, go_ref, d_ref, mo_ref, vo_ref)

    blk = pl.BlockSpec((128 * q, 128), lambda j: (j, 0))
    outs = pl.pallas_call(
        body, grid=(pl.cdiv(Cc, 128),),
        in_specs=[blk, pl.BlockSpec((S, R, 128), lambda j: (0, 0, j)), blk, blk],
        out_specs=[blk] * 4, out_shape=[jax.ShapeDtypeStruct((Cc * q, 128), f32)] * 4,
        scratch_shapes=[pltpu.VMEM((128 * q, 128), f32)],
        compiler_params=_cparams(("parallel",)), name=name,
    )(dense(w), gs, dense(m), dense(v))
    return [back(o) for o in outs]


def _me():
    x, y, c = lax.axis_index("x"), lax.axis_index("y"), lax.axis_index("c")
    return x, y, c, 4 * x + 2 * y + c


def _peer(k):
    x, y, c, _ = _me()
    px = 1 - x if k & 4 else x
    py = 1 - y if k & 2 else y
    pc = 1 - c if k & 1 else c
    return (px, py, pc), 4 * px + 2 * py + pc


_HBM = pl.BlockSpec(memory_space=pltpu.HBM)
_SEM = pl.BlockSpec(memory_space=pltpu.SEMAPHORE)
_EFFECT = pltpu.SideEffectType.DATAFLOW_SIDE_EFFECTING


def _copy(src, land, ssem, rsem, k, blocked, landing_slot_of_peer):
    pid, pidx = _peer(k)
    slot = pidx if landing_slot_of_peer else _me()[3]
    return pltpu.make_async_remote_copy(src_ref=src.at[pidx] if blocked else src, dst_ref=land.at[slot],
                                        send_sem=ssem.at[k - 1], recv_sem=rsem.at[k - 1], device_id=pid, device_id_type=MESH)


def _own_copy(src, land, rsem, blocked):
    me = _me()[3]
    return pltpu.make_async_copy(src.at[me] if blocked else src, land.at[me], rsem.at[N_DEV - 1])


_ALL_PEERS = tuple(range(1, N_DEV))
_SAME_CORE_PEERS = (1, 2, 4, 6)
_OTHER_CORE_PEERS = (3, 5, 7)


def _relay_copies(land, ssem, rsem):
    sibling, _ = _peer(1)
    copies = []
    for i in range(len(land)):
        for j, k in enumerate(_OTHER_CORE_PEERS):
            sems = dict(send_sem=ssem.at[3 * i + j], recv_sem=rsem.at[3 * i + j], device_id=sibling, device_id_type=MESH)
            _, outgoing = _peer(k - 1)
            _, incoming = _peer(k)
            copies.append((pltpu.make_async_remote_copy(src_ref=land[i].at[outgoing], dst_ref=land[i].at[outgoing], **sems),
                           pltpu.make_async_remote_copy(src_ref=land[i].at[incoming], dst_ref=land[i].at[incoming], **sems)))
    return copies


def _relay_start(lands, name):
    n = len(lands)

    def body(*refs):
        for send, _ in _relay_copies(refs[:n], refs[n], refs[n + 1]):
            send.start()
        refs[-1][...] = jnp.zeros_like(refs[-1])

    sem = pltpu.SemaphoreType.DMA((3 * n,))
    res = pl.pallas_call(
        body, name=name,
        out_shape=(sem, sem) + tuple(pltpu.HBM(a.shape, a.dtype) for a in lands) + (jax.ShapeDtypeStruct((8, 128), f32),),
        in_specs=(_HBM,) * n, out_specs=(_SEM, _SEM) + (_HBM,) * n + (pl.BlockSpec(memory_space=pltpu.VMEM),),
        input_output_aliases={i: 2 + i for i in range(n)},
        compiler_params=pltpu.CompilerParams(has_side_effects=_EFFECT),
    )(*lands)
    return res[:-1], res[-1]


def _relay_wait(handle, after, name):
    ssem, rsem, lands = handle[0], handle[1], handle[2:]
    n = len(lands)
    after = tuple(after) if isinstance(after, (tuple, list)) else (after,)

    def body(*refs):
        for send, recv in _relay_copies(refs[:n], refs[n], refs[n + 1]):
            send.wait_send()
            recv.wait_recv()

    return pl.pallas_call(
        body, name=name, out_shape=tuple(pltpu.HBM(a.shape, a.dtype) for a in lands),
        in_specs=(_HBM,) * n + (_SEM, _SEM) + (pl.BlockSpec(memory_space=pl.ANY),) * len(after), out_specs=(_HBM,) * n,
        input_output_aliases={i: i for i in range(n)}, compiler_params=pltpu.CompilerParams(has_side_effects=_EFFECT),
    )(*lands, ssem, rsem, *after)


def _send_start(srcs, blocked, name, relayed=()):
    n = len(srcs)
    lands = [lax.empty(a.shape if blocked else (N_DEV,) + a.shape, a.dtype) for a in srcs]

    def body(*refs):
        src, land, sems, token = refs[:n], refs[n:2 * n], refs[2 * n:4 * n], refs[-1]
        for i in range(n):
            for k in (_SAME_CORE_PEERS if i in relayed else _ALL_PEERS):
                _copy(src[i], land[i], sems[2 * i], sems[2 * i + 1], k, blocked, False).start()
        for i in range(n):
            _own_copy(src[i], land[i], sems[2 * i + 1], blocked).start()
        token[...] = jnp.zeros_like(token)

    sems = (pltpu.SemaphoreType.DMA((N_DEV - 1,)), pltpu.SemaphoreType.DMA((N_DEV,)))
    hbm = lambda a: pltpu.with_memory_space_constraint(a, pltpu.HBM)
    res = pl.pallas_call(
        body, name=name,
        out_shape=sems * n + tuple(pltpu.HBM(a.shape, a.dtype) for a in srcs + lands)
        + (jax.ShapeDtypeStruct((8, 128), f32),),
        in_specs=(_HBM,) * (2 * n),
        out_specs=(_SEM,) * (2 * n) + (_HBM,) * (2 * n) + (pl.BlockSpec(memory_space=pltpu.VMEM),),
        input_output_aliases={j: 2 * n + j for j in range(2 * n)},
        compiler_params=pltpu.CompilerParams(has_side_effects=_EFFECT),
    )(*[hbm(a) for a in srcs], *[hbm(a) for a in lands])
    handles = [(res[2 * i], res[2 * i + 1], res[2 * n + i], res[3 * n + i]) for i in range(n)]
    return handles, res[-1]


def _send_wait(handle, blocked, after, name, relayed=False):
    ssem, rsem, src, land = handle
    after = tuple(after) if isinstance(after, (tuple, list)) else (after,)

    def body(src_ref, land_ref, ssem_ref, rsem_ref, *rest):
        for k in (_SAME_CORE_PEERS if relayed else _ALL_PEERS):
            cp = _copy(src_ref, land_ref, ssem_ref, rsem_ref, k, blocked, True)
            cp.wait_send()
            cp.wait_recv()
        _own_copy(src_ref, land_ref, rsem_ref, blocked).wait()

    return pl.pallas_call(
        body, name=name, out_shape=(pltpu.HBM(src.shape, src.dtype), pltpu.HBM(land.shape, land.dtype)),
        in_specs=(_HBM, _HBM, _SEM, _SEM) + (pl.BlockSpec(memory_space=pl.ANY),) * len(after), out_specs=(_HBM, _HBM),
        input_output_aliases={0: 0, 1: 1}, compiler_params=pltpu.CompilerParams(has_side_effects=_EFFECT),
    )(src, land, ssem, rsem, *after)


def _block_diag(w):
    nb, bs = w.shape[0], w.shape[1]
    eye = jnp.eye(nb, dtype=w.dtype)
    return (eye[:, None, :, None] * w[:, :, None, :]).reshape(nb * bs, nb * bs)


_SQUARE_TILES = dict(tm=1024, tn=1024, tk=1024)


def _mlp_fwd(x, hm, wu, wd, tag, epilogue, extras, outs, between=None):
    (r,) = _matmul(hm, wu, "nn", outs=[bf16], epilogue=lambda acc: (jnp.maximum(acc, 0.0),), name=f"mlp_up_{tag}")
    res = _matmul(r, wd, "nn", outs=outs, extras=(x,) + tuple(extras), epilogue=epilogue, a_map=jnp.square,
                  after=() if between is None else (between(r),), name=f"mlp_down_{tag}", **_SQUARE_TILES)
    return res, (hm, r)


def _mlp_bwd(x, nw, wu, wd, saved, dxo, dxo_b, tag, sink, urgent):
    hm, r = saved
    (du,) = _matmul(dxo_b, wd, "nt", outs=[bf16], extras=(r,), epilogue=lambda acc, rr: (acc * (2.0 * rr.astype(f32)),),
                    name=f"mlp_dact_{tag}")
    (dwd,) = _matmul(r, dxo_b, "tn", outs=[bf16], a_map=jnp.square, name=f"mlp_dwd_{tag}", **_SQUARE_TILES)
    down = {f"w_down{tag}": dwd.reshape(N_DEV, D_FF // N_DEV, D_MODEL)}
    if urgent:
        tok = sink(down)
        (dwu,) = _matmul(hm, du, "tn", outs=[bf16], shard_cols=2, after=(tok,), name=f"mlp_dwu_{tag}")
        tok = sink({f"w_up{tag}": dwu})
    else:
        (dwu,) = _matmul(hm, du, "tn", outs=[bf16], shard_cols=2, name=f"mlp_dwu_{tag}")
        tok = sink({f"w_up{tag}": dwu, **down})
    return _matmul(du, wu, "nt", outs=_RMS_BWD_OUTS, extras=(x, dxo, nw), epilogue=_rms_bwd_ep, after=(tok,),
                   name=f"mlp_dh_{tag}", **_SQUARE_TILES)


def _local_step(x, tgt, P, weight, sink, ahead):
    T = x.shape[0]
    cos, sin = _rope_tables(T)
    rtab = _ret_tables()
    row = lambda a: a.reshape(1, -1)
    mix_nw, mlp_nw = P["mixer_norm_w"], P["mlp_norm_w"]
    wr_bd, wi_bd = _block_diag(P["lru_w_r"]), _block_diag(P["lru_w_i"])
    lru_b, lru_br, lru_bi, lru_lam = row(P["lru_conv_b"]), row(P["lru_b_r"]), row(P["lru_b_i"]), row(P["lru_lambda"])
    pad16 = lambda a: jnp.pad(a.reshape(1, GDN_HEADS), ((0, 0), (GDN_HEADS, 128 - 2 * GDN_HEADS)))
    alog, dtb = pad16(P["gdn_a_log"]), pad16(P["gdn_dt_bias"])
    gnw = row(P["gdn_norm_w"])

    x0 = x
    h0 = _rms_fwd(x0, mix_nw[0:1], "rms_mix_0")
    w_ie = weight("w_in_even", (h0, wr_bd, wi_bd))
    (pe,) = _matmul(h0, w_ie, "nn", outs=[f32], name="in_even")
    mix0, o_ret, s_ret = _ret_fwd(pe, cos, sin, rtab, "ret_fwd")
    w_lc = weight("lru_conv_w", pe)
    xc = _conv_fwd(pe, 4, w_lc, lru_b, False, "lru_conv_fwd")
    mix0, h_lru = _lru_fwd(xc, pe, 20, wr_bd, wi_bd, lru_br, lru_bi, lru_lam, mix0, "lru_fwd")
    w_oe = weight("w_out_even", mix0)
    tok = ahead(("w_up0", "w_down0"), mix0)
    x1, hm0 = _matmul(mix0, w_oe, "nn", outs=[f32, bf16], extras=(x0, mlp_nw[0:1]), epilogue=_residual_rms_ep,
                      after=(tok,), name="out_even", tm=1024, tn=D_MODEL)
    w_u0, w_d0 = weight("w_up0", x1), weight("w_down0", x1)
    (x2, h1), mlp0 = _mlp_fwd(x1, hm0, w_u0, w_d0, "0", _residual_rms_ep, (mix_nw[1:2],), [f32, bf16],
                              between=lambda r: ahead(("w_in_odd",), r))
    w_io = weight("w_in_odd", h1)
    (po,) = _matmul(h1, w_io, "nn", outs=[f32], tm=2048, tn=ODD_PAD // 3, name="in_odd")
    w_gc = weight("gdn_conv_w", po)
    qkv = _conv_fwd(po, 0, w_gc, None, True, "gdn_conv_fwd")
    y_gdn, s_gdn, ti_gdn, vn_gdn, o_gdn = _gdn_fwd(qkv, po, alog, dtb, gnw, "gdn_fwd")
    w_oo = weight("w_out_odd", y_gdn)
    x3, hm1 = _matmul(y_gdn, w_oo, "nn", outs=[f32, bf16], extras=(x2, mlp_nw[1:2]), epilogue=_residual_rms_ep,
                      name="out_odd", tm=1024, tn=D_MODEL)
    w_u1, w_d1 = weight("w_up1", x3), weight("w_down1", x3)
    (loss, dx4, dx4_b, d_final), mlp1 = _mlp_fwd(x3, hm1, w_u1, w_d1, "1", _loss_ep, (row(P["final_norm_w"]), tgt),
                                                 _LOSS_OUTS)
    dx3, dx3_b, d_mlp_nw1 = _mlp_bwd(x3, mlp_nw[1:2], w_u1, w_d1, mlp1, dx4, dx4_b, "1", sink, False)
    (dy_gdn,) = _matmul(dx3_b, w_oo, "nt", outs=[f32], name="out_odd_dx")
    (d_woo,) = _matmul(y_gdn, dx3_b, "tn", outs=[bf16], name="out_odd_dw")
    dqkv, dpo, dba, d_alog, d_dtb, d_gnw = _gdn_bwd(qkv, po, alog, dtb, gnw, s_gdn, ti_gdn, vn_gdn, o_gdn, dy_gdn,
                                                  "gdn_bwd")
    dpo, d_gconv, _ = _conv_bwd(po, 0, w_gc, None, True, dqkv, dpo, "gdn_conv_bwd")
    dpo = lax.dynamic_update_slice(dpo, dba, (0, 4 * D_MODEL))
    (d_wio,) = _matmul(h1, dpo, "tn", outs=[bf16], tn=ODD_PAD // 3, name="in_odd_dw")
    tok = sink(dict(w_out_odd=d_woo.reshape(N_DEV, D_MODEL // N_DEV, D_MODEL), w_in_odd=_odd_split(d_wio, "w_in_odd_split")))
    dx2, dx2_b, d_mix_nw1 = _matmul(dpo, w_io, "nt", outs=_RMS_BWD_OUTS, extras=(x2, dx3, mix_nw[1:2]),
                                    epilogue=_rms_bwd_ep, after=(tok,), tm=1024, tn=1024, tk=ODD_PAD // 3, name="in_odd_dx")
    dx1, dx1_b, d_mlp_nw0 = _mlp_bwd(x1, mlp_nw[0:1], w_u0, w_d0, mlp0, dx2, dx2_b, "0", sink, True)
    (d_woe,) = _matmul(mix0, dx1_b, "tn", outs=[bf16], name="out_even_dw")
    tok = sink(dict(w_out_even=d_woe.reshape(N_DEV, D_MODEL // N_DEV, D_MODEL)))
    (dmix0,) = _matmul(dx1_b, w_oe, "nt", outs=[f32], name="out_even_dx")
    dpe = _ret_bwd(pe, cos, sin, rtab, o_ret, s_ret, dmix0, "ret_bwd")
    dxc, dpe, d_wr, d_wi, d_br, d_bi, d_lam = _lru_bwd(xc, pe, 20, wr_bd, wi_bd, lru_br, lru_bi, lru_lam + tok[0:1, 0:1],
                                                       h_lru, dmix0, 4, dpe, "lru_bwd")
    dpe, d_lconv, d_lconv_b = _conv_bwd(pe, 4, w_lc, lru_b, False, dxc, dpe, "lru_conv_bwd")
    G = dict(
        mlp_norm_w=jnp.concatenate([d_mlp_nw0, d_mlp_nw1], axis=0),
        final_norm_w=d_final.reshape(-1),
        lru_conv_w=d_lconv, lru_conv_b=d_lconv_b.reshape(-1),
        lru_w_r=d_wr, lru_b_r=d_br.reshape(-1), lru_w_i=d_wi, lru_b_i=d_bi.reshape(-1),
        lru_lambda=d_lam.reshape(-1), gdn_conv_w=d_gconv,
        gdn_a_log=d_alog[0, GDN_HEADS:2 * GDN_HEADS], gdn_dt_bias=d_dtb[0, GDN_HEADS:2 * GDN_HEADS],
        gdn_norm_w=d_gnw.reshape(-1),
    )
    packed = _pack([G[k] for k in _SMALL] + [d_lconv, d_gconv, loss[0, 0:1]])
    tok = sink(dict(small=jnp.broadcast_to(packed[None], (N_DEV,) + packed.shape)))
    (d_wie,) = _matmul(h0, dpe, "tn", outs=[bf16], shard_cols=2, after=(tok,), name="in_even_dw")
    tok = sink(dict(w_in_even=d_wie))
    dx0, _, d_mix_nw0 = _matmul(dpe, w_ie, "nt", outs=_RMS_BWD_OUTS, extras=(x0, dx1, mix_nw[0:1]), epilogue=_rms_bwd_ep,
                                after=(tok,), name="in_even_dx", **_SQUARE_TILES)
    G["mixer_norm_w"] = jnp.concatenate([d_mix_nw0, d_mix_nw1], axis=0)
    return loss, dx0, G


_SMALL = ["mlp_norm_w", "final_norm_w", "lru_conv_b", "lru_w_r", "lru_b_r", "lru_w_i", "lru_b_i",
          "lru_lambda", "gdn_a_log", "gdn_dt_bias", "gdn_norm_w"]
_PACK_ROWS = 688


def _pack(parts):
    flat = jnp.concatenate([p.reshape(-1) for p in parts])
    return jnp.pad(flat, (0, _PACK_ROWS * 128 - flat.shape[0])).reshape(_PACK_ROWS, 128)


def _unpack(packed, shapes):
    flat = packed.reshape(-1)
    out, off = [], 0
    for s in shapes:
        n = int(np.prod(s))
        out.append(flat[off:off + n].reshape(s))
        off += n
    return out


def kernel(x, mixer_norm_w, mlp_norm_w, final_norm_w, w_in_even, lru_conv_w, lru_conv_b, lru_w_r, lru_b_r, lru_w_i, lru_b_i, lru_lambda, w_out_even, w_in_odd, gdn_conv_w, gdn_a_log, gdn_dt_bias, gdn_norm_w, w_out_odd, w_up, w_down, loss_target, m_mixer_norm_w, m_mlp_norm_w, m_final_norm_w, m_w_in_even, m_lru_conv_w, m_lru_conv_b, m_lru_w_r, m_lru_b_r, m_lru_w_i, m_lru_b_i, m_lru_lambda, m_w_out_even, m_w_in_odd, m_gdn_conv_w, m_gdn_a_log, m_gdn_dt_bias, m_gdn_norm_w, m_w_out_odd, m_w_up, m_w_down, v_mixer_norm_w, v_mlp_norm_w, v_final_norm_w, v_w_in_even, v_lru_conv_w, v_lru_conv_b, v_lru_w_r, v_lru_b_r, v_lru_w_i, v_lru_b_i, v_lru_lambda, v_w_out_even, v_w_in_odd, v_gdn_conv_w, v_gdn_a_log, v_gdn_dt_bias, v_gdn_norm_w, v_w_out_odd, v_w_up, v_w_down):
    Pw = dict(mixer_norm_w=mixer_norm_w, mlp_norm_w=mlp_norm_w, final_norm_w=final_norm_w, w_in_even=w_in_even,
              lru_conv_w=lru_conv_w, lru_conv_b=lru_conv_b, lru_w_r=lru_w_r, lru_b_r=lru_b_r, lru_w_i=lru_w_i,
              lru_b_i=lru_b_i, lru_lambda=lru_lambda, w_out_even=w_out_even, w_in_odd=w_in_odd, gdn_conv_w=gdn_conv_w,
              gdn_a_log=gdn_a_log, gdn_dt_bias=gdn_dt_bias, gdn_norm_w=gdn_norm_w, w_out_odd=w_out_odd, w_up=w_up,
              w_down=w_down)
    Pm = dict(mixer_norm_w=m_mixer_norm_w, mlp_norm_w=m_mlp_norm_w, final_norm_w=m_final_norm_w, w_in_even=m_w_in_even,
              lru_conv_w=m_lru_conv_w, lru_conv_b=m_lru_conv_b, lru_w_r=m_lru_w_r, lru_b_r=m_lru_b_r, lru_w_i=m_lru_w_i,
              lru_b_i=m_lru_b_i, lru_lambda=m_lru_lambda, w_out_even=m_w_out_even, w_in_odd=m_w_in_odd,
              gdn_conv_w=m_gdn_conv_w, gdn_a_log=m_gdn_a_log, gdn_dt_bias=m_gdn_dt_bias, gdn_norm_w=m_gdn_norm_w,
              w_out_odd=m_w_out_odd, w_up=m_w_up, w_down=m_w_down)
    Pv = dict(mixer_norm_w=v_mixer_norm_w, mlp_norm_w=v_mlp_norm_w, final_norm_w=v_final_norm_w, w_in_even=v_w_in_even,
              lru_conv_w=v_lru_conv_w, lru_conv_b=v_lru_conv_b, lru_w_r=v_lru_w_r, lru_b_r=v_lru_b_r, lru_w_i=v_lru_w_i,
              lru_b_i=v_lru_b_i, lru_lambda=v_lru_lambda, w_out_even=v_w_out_even, w_in_odd=v_w_in_odd,
              gdn_conv_w=v_gdn_conv_w, gdn_a_log=v_gdn_a_log, gdn_dt_bias=v_gdn_dt_bias, gdn_norm_w=v_gdn_norm_w,
              w_out_odd=v_w_out_odd, w_up=v_w_up, w_down=v_w_down)
    me = _me()[3]
    T = x.shape[1]

    cols = lambda g: jnp.transpose(g, (1, 0, 2)).reshape(g.shape[1], -1)
    rows = lambda g: g.reshape(-1, g.shape[2])
    wide = lambda g: _odd_assemble(g, "w_in_odd_assemble")
    as_is = lambda g: g
    relay_groups = (("w_in_even",), ("w_up0", "w_down0"), ("w_in_odd",))
    relayed = sum(relay_groups, ())
    (first,), started = _send_start([w_in_even[0].astype(bf16)], False, "gather_start_w_in_even", relayed=[0])
    cast = lambda a: (a + started[0:1, 0:1]).astype(bf16)
    odd_shard = jnp.pad(cast(w_in_odd[0]), ((0, 0), (0, ODD_SHARD_PAD - ODD_SHARD)))
    gather = dict(
        w_in_even=(None, cols), lru_conv_w=(lru_conv_w[0], cols),
        w_out_even=(cast(w_out_even[0]), rows), w_up0=(cast(w_up[0]), as_is), w_down0=(cast(w_down[0]), rows),
        w_in_odd=(odd_shard, wide), gdn_conv_w=(gdn_conv_w[0], cols),
        w_out_odd=(cast(w_out_odd[0]), rows), w_up1=(cast(w_up[1]), as_is), w_down1=(cast(w_down[1]), rows))
    later = [name for name in gather if name != "w_in_even"]
    handles, tok = _send_start([gather[name][0] for name in later], False, "gather_start",
                               relayed=[i for i, name in enumerate(later) if name in relayed])
    handles = dict(zip(later, handles), w_in_even=first)
    relays, landed, full = {}, {}, {}

    def ahead(group, after):
        lands = [_send_wait(handles[n], False, after, f"gather_wait_{n}", relayed=True)[1] for n in group]
        relays[group], token = _relay_start(lands, "relay_start_" + "_".join(group))
        return token

    def weight(name, after):
        if name not in landed and name in relayed:
            group = next(g for g in relay_groups if name in g)
            if group not in relays:
                ahead(group, after)
            landed.update(zip(group, _relay_wait(relays[group], after, "relay_wait_" + "_".join(group))))
        elif name not in landed:
            landed[name] = _send_wait(handles[name], False, after, f"gather_wait_{name}")[1]
        if name not in full:
            full[name] = gather[name][1](landed[name])
        return full[name]

    P = {k: Pw[k] for k in ("mlp_norm_w", "final_norm_w")}
    P["mixer_norm_w"] = mixer_norm_w + tok[0:1, 0:1]
    for k in ("lru_w_r", "lru_w_i", "lru_conv_b", "lru_b_r", "lru_b_i", "lru_lambda", "gdn_a_log", "gdn_dt_bias", "gdn_norm_w"):
        P[k] = Pw[k][0]

    sent = {}

    def sink(grads):
        hs, token = _send_start(list(grads.values()), True, "grads_start_" + "_".join(grads))
        sent.update(zip(grads, hs))
        return token

    loss, dx, G = _local_step(x[0], loss_target[0], P, weight, sink, ahead)
    lanes = lambda a: a.reshape(-1, 128)
    sink(dict(mixer_norm_w=jnp.broadcast_to(lanes(G["mixer_norm_w"])[None], (N_DEV, 2 * D_MODEL // 128, 128))))

    def received(name, after=dx):
        return _send_wait(sent[name], True, after, f"grads_wait_{name}")[1]

    out = {}
    nff = D_FF // N_DEV

    def whole(name, gs):
        out[name] = tuple(_adamw(Pw[name], gs, Pm[name], Pv[name], f"adamw_{name}", layer=0))

    def layers(name):
        res = None
        for l in range(2):
            res = _adamw(Pw[name], received(f"{name}{l}"), Pm[name], Pv[name], f"adamw_{name}{l}", layer=l, prev=res)
        out[name] = tuple(res)

    layers("w_up")
    layers("w_down")
    whole("w_out_odd", received("w_out_odd"))
    out["w_in_odd"] = tuple(_adamw_column_major(w_in_odd, received("w_in_odd"), m_w_in_odd, v_w_in_odd, "adamw_w_in_odd"))
    whole("w_out_even", received("w_out_even"))
    small_shapes = [Pw[k].shape for k in _SMALL]
    pw, pm, pv = (_pack([Q[k] for k in _SMALL]) for Q in (Pw, Pm, Pv))
    sg, sd, sm, sv = _adamw(pw, received("small", out["w_out_even"][1]), pm, pv, "adamw_small")
    for arrs_i, packed_out in enumerate((sg, sd, sm, sv)):
        for k, a in zip(_SMALL, _unpack(packed_out, small_shapes)):
            out.setdefault(k, [None] * 4)[arrs_i] = a
    whole("w_in_even", received("w_in_even", sd))
    out["mixer_norm_w"] = tuple(
        a.reshape(mixer_norm_w.shape) for a in
        _adamw(lanes(mixer_norm_w), received("mixer_norm_w", out["w_in_even"][1]), lanes(m_mixer_norm_w),
               lanes(v_mixer_norm_w), "adamw_mixer_norm_w"))
    n_small = sum(int(np.prod(s)) for s in small_shapes)
    gflat = sg.reshape(-1)
    g_lconv = gflat[n_small:n_small + CONV_K * LRU_WIDTH].reshape(CONV_K, LRU_WIDTH)
    g_gconv = gflat[n_small + CONV_K * LRU_WIDTH:n_small + CONV_K * (LRU_WIDTH + 3072)].reshape(CONV_K, 3072)
    whole("lru_conv_w", lax.dynamic_slice_in_dim(g_lconv, me * 64, 64, axis=1)[None])
    whole("gdn_conv_w", lax.dynamic_slice_in_dim(g_gconv, me * 384, 384, axis=1)[None])

    names = ["mixer_norm_w", "mlp_norm_w", "final_norm_w", "w_in_even", "lru_conv_w", "lru_conv_b", "lru_w_r", "lru_b_r",
             "lru_w_i", "lru_b_i", "lru_lambda", "w_out_even", "w_in_odd", "gdn_conv_w", "gdn_a_log", "gdn_dt_bias",
             "gdn_norm_w", "w_out_odd", "w_up", "w_down"]
    total = gflat[n_small + CONV_K * (LRU_WIDTH + 3072)]
    res = [total, dx[None]]
    for j in range(4):
        res += [out[k][j] for k in names]
    return tuple(res)
```
